```python
import jax, jax.numpy as jnp
from jax import lax
import numpy as np

D_MODEL = 2048
BATCH = 8
SEQ = 4096
DEPTH = 1

HEAD_DIM = 128
N_DELTA_HEADS = D_MODEL // (2 * HEAD_DIM)
N_ATTN_HEADS = D_MODEL // (2 * HEAD_DIM)
DELTA_WIDTH = N_DELTA_HEADS * HEAD_DIM
ATTN_WIDTH = N_ATTN_HEADS * HEAD_DIM
MIX_WIDTH = DELTA_WIDTH + ATTN_WIDTH
CONV_WIDTH = 4
CHUNK = 64
DILATED_PATTERNS = ((128, 1), (512, 4), (2048, 16))
ROPE_THETA = 10000.0
D_FF = ((8 * D_MODEL // 3 + 255) // 256) * 256
EPS = 1e-6

PROJ_SIZES = (DELTA_WIDTH, DELTA_WIDTH, DELTA_WIDTH, DELTA_WIDTH, N_DELTA_HEADS, N_DELTA_HEADS,
              ATTN_WIDTH, ATTN_WIDTH, ATTN_WIDTH)
PROJ_WIDTH = int(sum(PROJ_SIZES))
SPLIT_POINTS = tuple(int(s) for s in np.cumsum(PROJ_SIZES)[:-1])

kernel_name = "hybrid_deltanet_dilated_swa_layer"


def rms_norm(x, w):
    xf = x.astype(jnp.float32)
    y = xf * lax.rsqrt(jnp.mean(xf * xf, axis=-1, keepdims=True) + EPS)
    return (y * w.astype(jnp.float32)).astype(x.dtype)


def l2_norm(x):
    xf = x.astype(jnp.float32)
    return xf * lax.rsqrt(jnp.sum(xf * xf, axis=-1, keepdims=True) + EPS)


def rope(x, positions):
    half = x.shape[-1] // 2
    inv_freq = ROPE_THETA ** (-jnp.arange(half, dtype=jnp.float32) / half)
    ang = positions.astype(jnp.float32)[..., None] * inv_freq
    cos = jnp.cos(ang)[:, :, None, :]
    sin = jnp.sin(ang)[:, :, None, :]
    xf = x.astype(jnp.float32)
    x1, x2 = xf[..., :half], xf[..., half:]
    return jnp.concatenate([x1 * cos - x2 * sin, x2 * cos + x1 * sin], axis=-1).astype(x.dtype)


def causal_depthwise_conv_silu(x, w):
    k_len, chans = w.shape
    y = lax.conv_general_dilated(x, w[:, None, :].astype(x.dtype), window_strides=(1,),
                                 padding=[(k_len - 1, 0)],
                                 dimension_numbers=('NWC', 'WIO', 'NWC'),
                                 feature_group_count=chans)
    return jax.nn.silu(y)


def gated_delta_rule_chunked(q, k, v, g, beta):
    bsz, seq, heads, dk = q.shape
    dv = v.shape[-1]
    n = seq // CHUNK
    def to_chunks(t):
        return t.reshape(bsz, n, CHUNK, heads, t.shape[-1]).transpose(0, 3, 1, 2, 4)
    qc, kc, vc = to_chunks(q), to_chunks(k), to_chunks(v)
    gc = jnp.cumsum(g.reshape(bsz, n, CHUNK, heads).transpose(0, 3, 1, 2), axis=-1)
    bc = beta.reshape(bsz, n, CHUNK, heads).transpose(0, 3, 1, 2)
    idx = jnp.arange(CHUNK)
    causal = idx[:, None] >= idx[None, :]
    strict = idx[:, None] > idx[None, :]
    diff = gc[..., :, None] - gc[..., None, :]
    decay = jnp.where(causal, jnp.exp(jnp.where(causal, diff, 0.0)), 0.0)
    kk = jnp.einsum('bhncd,bhnmd->bhncm', kc, kc)
    lower = jnp.where(strict, bc[..., :, None] * kk * decay, 0.0)
    a_mat = lower + jnp.eye(CHUNK, dtype=jnp.float32)
    rhs = jnp.concatenate([vc * bc[..., None], kc * (bc * jnp.exp(gc))[..., None]], axis=-1)
    sol = lax.linalg.triangular_solve(a_mat, rhs, left_side=True, lower=True, unit_diagonal=True)
    u, w = sol[..., :dv], sol[..., dv:]
    qk = jnp.einsum('bhncd,bhnmd->bhncm', qc, kc) * decay
    q_dec = qc * jnp.exp(gc)[..., None]
    k_dec = kc * jnp.exp(gc[..., -1:] - gc)[..., None]
    chunk_decay = jnp.exp(gc[..., -1])

    def step(state, xs):
        u_c, w_c, qk_c, qd_c, kd_c, dec_c = xs
        v_new = u_c - jnp.einsum('bhcd,bhde->bhce', w_c, state)
        o_c = jnp.einsum('bhcd,bhde->bhce', qd_c, state) + jnp.einsum('bhcm,bhme->bhce', qk_c, v_new)
        state = state * dec_c[..., None, None] + jnp.einsum('bhcd,bhce->bhde', kd_c, v_new)
        return state, o_c

    xs = tuple(jnp.moveaxis(t, 2, 0) for t in (u, w, qk, q_dec, k_dec, chunk_decay))
    state0 = jnp.zeros((bsz, heads, dk, dv), jnp.float32)
    _, o = lax.scan(step, state0, xs)
    return o.transpose(1, 0, 3, 2, 4).reshape(bsz, seq, heads, dv)


def dilated_window_branch(q, k, v, window, dilation):
    bsz, seq, heads, hd = q.shape
    span = window // dilation
    blk = span
    unit = dilation * blk
    s_pad = -(-seq // unit) * unit
    nb = s_pad // unit
    def to_blocks(t):
        t = jnp.pad(t, [(0, 0), (0, s_pad - seq), (0, 0), (0, 0)])
        return t.reshape(bsz, nb, blk, dilation, heads, hd)
    def with_prev(t):
        prev = jnp.pad(t, [(0, 0), (1, 0), (0, 0), (0, 0), (0, 0), (0, 0)])[:, :-1]
        return jnp.concatenate([prev, t], axis=2)
    qb = to_blocks(q)
    kw = with_prev(to_blocks(k))
    vw = with_prev(to_blocks(v))
    s = jnp.einsum('bnqrhd,bnkrhd->bnrhqk', qb, kw,
                   preferred_element_type=jnp.float32) * (hd ** -0.5)
    qi = jnp.arange(blk)[:, None]
    ki = jnp.arange(2 * blk)[None, :]
    dist = qi + blk - ki
    band = (dist >= 0) & (dist <= span)
    mask = band[None] & ((jnp.arange(nb)[:, None, None] > 0) | (ki[None] >= blk))
    s = jnp.where(mask[None, :, None, None], s, -jnp.inf)
    m = jnp.max(s, axis=-1)
    p = jnp.exp(s - m[..., None])
    den = jnp.sum(p, axis=-1)
    o = jnp.einsum('bnrhqk,bnkrhd->bnqrhd', p, vw.astype(jnp.float32))
    m = m.transpose(0, 1, 4, 2, 3)
    den = den.transpose(0, 1, 4, 2, 3)
    o = o / den[..., None]
    o = o.reshape(bsz, s_pad, heads, hd)[:, :seq]
    m = m.reshape(bsz, s_pad, heads)[:, :seq]
    den = den.reshape(bsz, s_pad, heads)[:, :seq]
    return o, m, den


def dilated_mixture_attention(q, k, v):
    branches = [dilated_window_branch(q, k, v, w, d) for (w, d) in DILATED_PATTERNS]
    m_all = branches[0][1]
    for _, m_i, _ in branches[1:]:
        m_all = jnp.maximum(m_all, m_i)
    num = 0.0
    tot = 0.0
    for o_i, m_i, den_i in branches:
        wt = den_i * jnp.exp(m_i - m_all)
        num = num + wt[..., None] * o_i
        tot = tot + wt
    return num / tot[..., None]


def _fwd_setup_inputs(seed: int = 0) -> dict:
    key = jax.random.key(seed)
    ks = jax.random.split(key, 20)
    f32 = jnp.float32
    x = jax.random.normal(ks[0], (BATCH, SEQ, D_MODEL), f32)
    offset = jax.random.randint(ks[1], (BATCH, 1), 0, 1024, dtype=jnp.int32)
    positions = (offset + jnp.arange(SEQ, dtype=jnp.int32)[None, :]).astype(jnp.int32)
    def gain(k, n):
        return 1.0 + 0.02 * jax.random.normal(k, (DEPTH, n), f32)
    attn_norm_w = gain(ks[2], D_MODEL)
    w_in = jax.random.normal(ks[3], (DEPTH, D_MODEL, PROJ_WIDTH), f32) * D_MODEL ** -0.5
    conv_w = jax.random.normal(ks[4], (DEPTH, CONV_WIDTH, 3 * DELTA_WIDTH), f32) * CONV_WIDTH ** -0.5
    a_log = jnp.log(jax.random.uniform(ks[5], (DEPTH, N_DELTA_HEADS), f32, 1.0, 16.0))
    dt = jnp.exp(jax.random.uniform(ks[6], (DEPTH, N_DELTA_HEADS), f32, np.log(1e-3), np.log(1e-1)))
    dt_bias = dt + jnp.log(-jnp.expm1(-dt))
    delta_out_norm_w = gain(ks[7], HEAD_DIM)
    q_norm_w = gain(ks[8], HEAD_DIM)
    k_norm_w = gain(ks[9], HEAD_DIM)
    attn_out_norm_w = gain(ks[10], HEAD_DIM)
    w_out = jax.random.normal(ks[11], (DEPTH, MIX_WIDTH, D_MODEL), f32) * MIX_WIDTH ** -0.5
    ffn_norm_w = gain(ks[12], D_MODEL)
    w_gate_up = jax.random.normal(ks[13], (DEPTH, D_MODEL, 2 * D_FF), f32) * D_MODEL ** -0.5
    w_down = jax.random.normal(ks[14], (DEPTH, D_FF, D_MODEL), f32) * D_FF ** -0.5
    return {'x': x, 'positions': positions, 'attn_norm_w': attn_norm_w, 'w_in': w_in,
            'conv_w': conv_w, 'a_log': a_log, 'dt_bias': dt_bias,
            'delta_out_norm_w': delta_out_norm_w, 'q_norm_w': q_norm_w, 'k_norm_w': k_norm_w,
            'attn_out_norm_w': attn_out_norm_w, 'w_out': w_out, 'ffn_norm_w': ffn_norm_w,
            'w_gate_up': w_gate_up, 'w_down': w_down}


def _fwd_reference(x, positions, attn_norm_w, w_in, conv_w, a_log, dt_bias, delta_out_norm_w,
              q_norm_w, k_norm_w, attn_out_norm_w, w_out, ffn_norm_w, w_gate_up, w_down):
    bsz, seq, _ = x.shape
    for l in range(DEPTH):
        h = rms_norm(x, attn_norm_w[l])
        proj = h @ w_in[l].astype(h.dtype)
        qkv_raw, dz, db, da, aq, ak, av = jnp.split(
            proj, [3 * DELTA_WIDTH] + list(SPLIT_POINTS[3:]), axis=-1)
        qkv = causal_depthwise_conv_silu(qkv_raw, conv_w[l])
        dq, dk, dv = jnp.split(qkv, 3, axis=-1)
        dq = l2_norm(dq.reshape(bsz, seq, N_DELTA_HEADS, HEAD_DIM)) * (HEAD_DIM ** -0.5)
        dk = l2_norm(dk.reshape(bsz, seq, N_DELTA_HEADS, HEAD_DIM))
        dv = dv.reshape(bsz, seq, N_DELTA_HEADS, HEAD_DIM).astype(jnp.float32)
        beta = jax.nn.sigmoid(db.astype(jnp.float32))
        g = -jnp.exp(a_log[l].astype(jnp.float32)) * jax.nn.softplus(
            da.astype(jnp.float32) + dt_bias[l].astype(jnp.float32))
        o_a = gated_delta_rule_chunked(dq, dk, dv, g, beta)
        z = dz.reshape(bsz, seq, N_DELTA_HEADS, HEAD_DIM).astype(jnp.float32)
        o_a = (rms_norm(o_a, delta_out_norm_w[l]) * jax.nn.silu(z)).astype(x.dtype)
        aq = rope(rms_norm(aq.reshape(bsz, seq, N_ATTN_HEADS, HEAD_DIM), q_norm_w[l]), positions)
        ak = rope(rms_norm(ak.reshape(bsz, seq, N_ATTN_HEADS, HEAD_DIM), k_norm_w[l]), positions)
        av = av.reshape(bsz, seq, N_ATTN_HEADS, HEAD_DIM)
        o_b = dilated_mixture_attention(aq, ak, av)
        o_b = rms_norm(o_b, attn_out_norm_w[l]).astype(x.dtype)
        mixed = jnp.concatenate([o_a.reshape(bsz, seq, DELTA_WIDTH),
                                 o_b.reshape(bsz, seq, ATTN_WIDTH)], axis=-1)
        x = x + mixed @ w_out[l].astype(x.dtype)
        h = rms_norm(x, ffn_norm_w[l])
        gate, up = jnp.split(h @ w_gate_up[l].astype(h.dtype), 2, axis=-1)
        x = x + (jax.nn.silu(gate) * up) @ w_down[l].astype(x.dtype)
    return x


import jax as _jax
import jax.numpy as _jnp

TWIN_FORMAT = 'train_step'
FWD_PARAMS = ['x', 'positions', 'attn_norm_w', 'w_in', 'conv_w', 'a_log', 'dt_bias', 'delta_out_norm_w', 'q_norm_w', 'k_norm_w', 'attn_out_norm_w', 'w_out', 'ffn_norm_w', 'w_gate_up', 'w_down']
TWIN_WEIGHTS = ['attn_norm_w', 'w_in', 'conv_w', 'a_log', 'dt_bias', 'delta_out_norm_w', 'q_norm_w', 'k_norm_w', 'attn_out_norm_w', 'w_out', 'ffn_norm_w', 'w_gate_up', 'w_down']
TWIN_DIFF_INPUT = 'x'
TWIN_INPUTS = ['x', 'positions', 'attn_norm_w', 'w_in', 'conv_w', 'a_log', 'dt_bias', 'delta_out_norm_w', 'q_norm_w', 'k_norm_w', 'attn_out_norm_w', 'w_out', 'ffn_norm_w', 'w_gate_up', 'w_down', 'loss_target', 'm_attn_norm_w', 'm_w_in', 'm_conv_w', 'm_a_log', 'm_dt_bias', 'm_delta_out_norm_w', 'm_q_norm_w', 'm_k_norm_w', 'm_attn_out_norm_w', 'm_w_out', 'm_ffn_norm_w', 'm_w_gate_up', 'm_w_down', 'v_attn_norm_w', 'v_w_in', 'v_conv_w', 'v_a_log', 'v_dt_bias', 'v_delta_out_norm_w', 'v_q_norm_w', 'v_k_norm_w', 'v_attn_out_norm_w', 'v_w_out', 'v_ffn_norm_w', 'v_w_gate_up', 'v_w_down']
TWIN_OUTPUTS = ['loss', 'grad_x', 'grad_attn_norm_w', 'grad_w_in', 'grad_conv_w', 'grad_a_log', 'grad_dt_bias', 'grad_delta_out_norm_w', 'grad_q_norm_w', 'grad_k_norm_w', 'grad_attn_out_norm_w', 'grad_w_out', 'grad_ffn_norm_w', 'grad_w_gate_up', 'grad_w_down', 'delta_attn_norm_w', 'delta_w_in', 'delta_conv_w', 'delta_a_log', 'delta_dt_bias', 'delta_delta_out_norm_w', 'delta_q_norm_w', 'delta_k_norm_w', 'delta_attn_out_norm_w', 'delta_w_out', 'delta_ffn_norm_w', 'delta_w_gate_up', 'delta_w_down', 'new_m_attn_norm_w', 'new_m_w_in', 'new_m_conv_w', 'new_m_a_log', 'new_m_dt_bias', 'new_m_delta_out_norm_w', 'new_m_q_norm_w', 'new_m_k_norm_w', 'new_m_attn_out_norm_w', 'new_m_w_out', 'new_m_ffn_norm_w', 'new_m_w_gate_up', 'new_m_w_down', 'new_v_attn_norm_w', 'new_v_w_in', 'new_v_conv_w', 'new_v_a_log', 'new_v_dt_bias', 'new_v_delta_out_norm_w', 'new_v_q_norm_w', 'new_v_k_norm_w', 'new_v_attn_out_norm_w', 'new_v_w_out', 'new_v_ffn_norm_w', 'new_v_w_gate_up', 'new_v_w_down']
TWIN_LEAF_KINDS = {'loss': 'loss', 'grad_x': 'grad_x', 'grad_attn_norm_w': 'grad_w', 'grad_w_in': 'grad_w', 'grad_conv_w': 'grad_w', 'grad_a_log': 'grad_w', 'grad_dt_bias': 'grad_w', 'grad_delta_out_norm_w': 'grad_w', 'grad_q_norm_w': 'grad_w', 'grad_k_norm_w': 'grad_w', 'grad_attn_out_norm_w': 'grad_w', 'grad_w_out': 'grad_w', 'grad_ffn_norm_w': 'grad_w', 'grad_w_gate_up': 'grad_w', 'grad_w_down': 'grad_w', 'delta_attn_norm_w': 'delta_w', 'delta_w_in': 'delta_w', 'delta_conv_w': 'delta_w', 'delta_a_log': 'delta_w', 'delta_dt_bias': 'delta_w', 'delta_delta_out_norm_w': 'delta_w', 'delta_q_norm_w': 'delta_w', 'delta_k_norm_w': 'delta_w', 'delta_attn_out_norm_w': 'delta_w', 'delta_w_out': 'delta_w', 'delta_ffn_norm_w': 'delta_w', 'delta_w_gate_up': 'delta_w', 'delta_w_down': 'delta_w', 'new_m_attn_norm_w': 'new_m', 'new_m_w_in': 'new_m', 'new_m_conv_w': 'new_m', 'new_m_a_log': 'new_m', 'new_m_dt_bias': 'new_m', 'new_m_delta_out_norm_w': 'new_m', 'new_m_q_norm_w': 'new_m', 'new_m_k_norm_w': 'new_m', 'new_m_attn_out_norm_w': 'new_m', 'new_m_w_out': 'new_m', 'new_m_ffn_norm_w': 'new_m', 'new_m_w_gate_up': 'new_m', 'new_m_w_down': 'new_m', 'new_v_attn_norm_w': 'new_v', 'new_v_w_in': 'new_v', 'new_v_conv_w': 'new_v', 'new_v_a_log': 'new_v', 'new_v_dt_bias': 'new_v', 'new_v_delta_out_norm_w': 'new_v', 'new_v_q_norm_w': 'new_v', 'new_v_k_norm_w': 'new_v', 'new_v_attn_out_norm_w': 'new_v', 'new_v_w_out': 'new_v', 'new_v_ffn_norm_w': 'new_v', 'new_v_w_gate_up': 'new_v', 'new_v_w_down': 'new_v'}


def _forward(args):
    return _fwd_reference(*[args[k] for k in FWD_PARAMS])


def _output_shape():
    def fwd():
        inp = _fwd_setup_inputs(0)
        return _fwd_reference(*[inp[k] for k in FWD_PARAMS])
    out = _jax.eval_shape(fwd)
    return out.shape, out.dtype

N_MICROBATCH = 1
ADAM_LR = 0.001
ADAM_B1 = 0.9
ADAM_B2 = 0.999
ADAM_EPS = 1e-08
ADAM_WD = 0.01
ADAM_STEP = 10
PER_EXAMPLE_BATCH_AXIS = {'x': 0, 'positions': 0, 'loss_target': 0}
SHARED_INPUTS = []
_WEIGHT_DTYPES = {'attn_norm_w': _jnp.float32, 'w_in': _jnp.float32, 'conv_w': _jnp.float32, 'a_log': _jnp.float32, 'dt_bias': _jnp.float32, 'delta_out_norm_w': _jnp.float32, 'q_norm_w': _jnp.float32, 'k_norm_w': _jnp.float32, 'attn_out_norm_w': _jnp.float32, 'w_out': _jnp.float32, 'ffn_norm_w': _jnp.float32, 'w_gate_up': _jnp.float32, 'w_down': _jnp.float32}
MOMENT_SCALE = {'attn_norm_w': 3.230063e+00, 'w_in': 2.058765e-01, 'conv_w': 2.473152e-01, 'a_log': 2.476165e+01, 'dt_bias': 2.383860e+01, 'delta_out_norm_w': 4.369661e+01, 'q_norm_w': 4.873424e-01, 'k_norm_w': 5.392782e-01, 'attn_out_norm_w': 1.273486e+02, 'w_out': 3.776934e-01, 'ffn_norm_w': 1.241676e+01, 'w_gate_up': 1.115184e-01, 'w_down': 1.384109e-01}


def _to_microbatches(a, axis):
    t = _jnp.moveaxis(a, axis, 0)
    t = t.reshape((N_MICROBATCH, t.shape[0] // N_MICROBATCH) + t.shape[1:])
    return _jnp.moveaxis(t, 1, axis + 1)


def setup_inputs(seed: int = 0) -> dict:
    inp = _fwd_setup_inputs(seed)
    key = _jax.random.fold_in(_jax.random.key(seed), 7919)
    shape, _ = _output_shape()
    out = dict(inp)
    out["loss_target"] = _jax.random.normal(_jax.random.fold_in(key, 0), shape, _jnp.float32)
    for i, name in enumerate(TWIN_WEIGHTS):
        w = inp[name].astype(_jnp.float32)
        if MOMENT_SCALE is None:
            s = _jnp.sqrt(_jnp.mean(_jnp.square(w)) + 1e-30)
        else:
            s = MOMENT_SCALE[name]
        km, kv = _jax.random.split(_jax.random.fold_in(key, i + 1))
        out[name] = w
        out["m_" + name] = s * _jax.random.normal(km, w.shape, _jnp.float32)
        out["v_" + name] = (s * s) * _jax.random.uniform(kv, w.shape, _jnp.float32, 0.5, 1.5)
    if N_MICROBATCH > 1:
        for name, axis in PER_EXAMPLE_BATCH_AXIS.items():
            out[name] = _to_microbatches(out[name], axis)
    return {'x': out['x'], 'positions': out['positions'], 'attn_norm_w': out['attn_norm_w'], 'w_in': out['w_in'], 'conv_w': out['conv_w'], 'a_log': out['a_log'], 'dt_bias': out['dt_bias'], 'delta_out_norm_w': out['delta_out_norm_w'], 'q_norm_w': out['q_norm_w'], 'k_norm_w': out['k_norm_w'], 'attn_out_norm_w': out['attn_out_norm_w'], 'w_out': out['w_out'], 'ffn_norm_w': out['ffn_norm_w'], 'w_gate_up': out['w_gate_up'], 'w_down': out['w_down'], 'loss_target': out['loss_target'], 'm_attn_norm_w': out['m_attn_norm_w'], 'm_w_in': out['m_w_in'], 'm_conv_w': out['m_conv_w'], 'm_a_log': out['m_a_log'], 'm_dt_bias': out['m_dt_bias'], 'm_delta_out_norm_w': out['m_delta_out_norm_w'], 'm_q_norm_w': out['m_q_norm_w'], 'm_k_norm_w': out['m_k_norm_w'], 'm_attn_out_norm_w': out['m_attn_out_norm_w'], 'm_w_out': out['m_w_out'], 'm_ffn_norm_w': out['m_ffn_norm_w'], 'm_w_gate_up': out['m_w_gate_up'], 'm_w_down': out['m_w_down'], 'v_attn_norm_w': out['v_attn_norm_w'], 'v_w_in': out['v_w_in'], 'v_conv_w': out['v_conv_w'], 'v_a_log': out['v_a_log'], 'v_dt_bias': out['v_dt_bias'], 'v_delta_out_norm_w': out['v_delta_out_norm_w'], 'v_q_norm_w': out['v_q_norm_w'], 'v_k_norm_w': out['v_k_norm_w'], 'v_attn_out_norm_w': out['v_attn_out_norm_w'], 'v_w_out': out['v_w_out'], 'v_ffn_norm_w': out['v_ffn_norm_w'], 'v_w_gate_up': out['v_w_gate_up'], 'v_w_down': out['v_w_down']}


def _loss(weights, diff, rest, loss_target):
    with _jax.named_scope("forward"):
        args = {**rest, TWIN_DIFF_INPUT: diff, **{k: w.astype(_WEIGHT_DTYPES[k]) for k, w in weights.items()}}
        y = _forward(args)
    with _jax.named_scope("loss_head"):
        err = _jnp.square(y.astype(_jnp.float32) - loss_target)
        return 0.5 * _jnp.sum(_jnp.mean(err, axis=-1)) if err.ndim else 0.5 * err


def _adamw(w, g, m, v):
    m = ADAM_B1 * m + (1.0 - ADAM_B1) * g
    v = ADAM_B2 * v + (1.0 - ADAM_B2) * _jnp.square(g)
    m_hat = m / (1.0 - ADAM_B1 ** ADAM_STEP)
    v_hat = v / (1.0 - ADAM_B2 ** ADAM_STEP)
    delta = -ADAM_LR * (m_hat / (_jnp.sqrt(v_hat) + ADAM_EPS) + ADAM_WD * w)
    return delta, m, v


def reference(x, positions, attn_norm_w, w_in, conv_w, a_log, dt_bias, delta_out_norm_w, q_norm_w, k_norm_w, attn_out_norm_w, w_out, ffn_norm_w, w_gate_up, w_down, loss_target, m_attn_norm_w, m_w_in, m_conv_w, m_a_log, m_dt_bias, m_delta_out_norm_w, m_q_norm_w, m_k_norm_w, m_attn_out_norm_w, m_w_out, m_ffn_norm_w, m_w_gate_up, m_w_down, v_attn_norm_w, v_w_in, v_conv_w, v_a_log, v_dt_bias, v_delta_out_norm_w, v_q_norm_w, v_k_norm_w, v_attn_out_norm_w, v_w_out, v_ffn_norm_w, v_w_gate_up, v_w_down):
    given = dict(x=x, positions=positions, attn_norm_w=attn_norm_w, w_in=w_in, conv_w=conv_w, a_log=a_log, dt_bias=dt_bias, delta_out_norm_w=delta_out_norm_w, q_norm_w=q_norm_w, k_norm_w=k_norm_w, attn_out_norm_w=attn_out_norm_w, w_out=w_out, ffn_norm_w=ffn_norm_w, w_gate_up=w_gate_up, w_down=w_down, loss_target=loss_target, m_attn_norm_w=m_attn_norm_w, m_w_in=m_w_in, m_conv_w=m_conv_w, m_a_log=m_a_log, m_dt_bias=m_dt_bias, m_delta_out_norm_w=m_delta_out_norm_w, m_q_norm_w=m_q_norm_w, m_k_norm_w=m_k_norm_w, m_attn_out_norm_w=m_attn_out_norm_w, m_w_out=m_w_out, m_ffn_norm_w=m_ffn_norm_w, m_w_gate_up=m_w_gate_up, m_w_down=m_w_down, v_attn_norm_w=v_attn_norm_w, v_w_in=v_w_in, v_conv_w=v_conv_w, v_a_log=v_a_log, v_dt_bias=v_dt_bias, v_delta_out_norm_w=v_delta_out_norm_w, v_q_norm_w=v_q_norm_w, v_k_norm_w=v_k_norm_w, v_attn_out_norm_w=v_attn_out_norm_w, v_w_out=v_w_out, v_ffn_norm_w=v_ffn_norm_w, v_w_gate_up=v_w_gate_up, v_w_down=v_w_down)
    weights = {n: given[n] for n in TWIN_WEIGHTS}
    shared = {n: given[n] for n in SHARED_INPUTS}
    per_example = {n: given[n] for n in ['x', 'positions']}
    grad_fn = _jax.value_and_grad(_loss, argnums=(0, 1))

    def one_microbatch(ex, loss_target):
        ex = dict(ex)
        diff = ex.pop(TWIN_DIFF_INPUT)
        return grad_fn(weights, diff, {**shared, **ex}, loss_target)

    if N_MICROBATCH == 1:
        loss, (grad_w, grad_x) = one_microbatch(per_example, given["loss_target"])
    else:
        def body(carry, xs):
            loss_sum, grad_sum = carry
            l_k, (gw_k, gx_k) = one_microbatch(xs[0], xs[1])
            with _jax.named_scope("update"):
                return (loss_sum + l_k, _jax.tree.map(_jnp.add, grad_sum, gw_k)), gx_k

        init = (_jnp.zeros((), _jnp.float32), _jax.tree.map(_jnp.zeros_like, weights))
        (loss, grad_w), grad_x = _jax.lax.scan(body, init, (per_example, given["loss_target"]))
    with _jax.named_scope("update"):
        delta_w, new_m, new_v = {}, {}, {}
        for n in TWIN_WEIGHTS:
            delta_w[n], new_m[n], new_v[n] = _adamw(weights[n], grad_w[n], given["m_" + n], given["v_" + n])
    return (loss, grad_x, *[grad_w[n] for n in TWIN_WEIGHTS], *[delta_w[n] for n in TWIN_WEIGHTS],
            *[new_m[n] for n in TWIN_WEIGHTS], *[new_v[n] for n in TWIN_WEIGHTS])
```

```python
import functools

import jax
import jax.numpy as jnp
from jax import lax
from jax.experimental import pallas as pl
from jax.experimental.pallas import tpu as pltpu

F32 = jnp.float32
BF16 = jnp.bfloat16
HEAD_DIM = 128
CHUNK = 64
SPAN = 128
DILATIONS = (1, 4, 16)
ROPE_THETA = 10000.0
EPS = 1e-6
NEG = -1e30
ADAM_LR, ADAM_B1, ADAM_B2, ADAM_EPS, ADAM_WD, ADAM_STEP = 0.001, 0.9, 0.999, 1e-08, 0.01, 10
VMEM_LIMIT = 48 * 1024 * 1024
MESH = pl.DeviceIdType.MESH

_DN = {"nn": (((1,), (0,)), ((), ())), "nt": (((1,), (1,)), ((), ())), "tn": (((0,), (0,)), ((), ()))}


def _dot(a, b, mode="nn", exact=False):
    if exact:
        return lax.dot_general(a, b, _DN[mode], preferred_element_type=F32, precision=lax.Precision.HIGHEST)
    return lax.dot_general(a.astype(BF16), b.astype(BF16), _DN[mode], preferred_element_type=F32)


def _tile(dim, pref, unit=128):
    t = (min(pref, dim) // unit) * unit
    while t >= unit:
        if dim % t == 0:
            return t
        t -= unit
    return dim


def _params(sem):
    return pltpu.CompilerParams(dimension_semantics=sem, vmem_limit_bytes=VMEM_LIMIT)


def _sigmoid(x):
    return 1.0 / (1.0 + jnp.exp(-x))


def _matmul(a, b, mode, name, add=None, out_dtype=F32, a_cols=None, b_cols=None):
    if mode == "tn":
        out_dtype = BF16
    a_off, a_w = a_cols if a_cols else (0, a.shape[1])
    b_off, b_w = b_cols if b_cols else (0, b.shape[1])
    if mode == "nn":
        m, k, n = a.shape[0], a_w, b_w
        assert b.shape[0] == k
    elif mode == "nt":
        m, k, n = a.shape[0], a_w, b.shape[0]
        assert b_w == k
    else:
        k, m, n = a.shape[0], a_w, b_w
        assert b.shape[0] == k
    tm, tn = _tile(m, 1024, 128), _tile(n, 1024, 128)
    tk = _tile(k, 512, 128)
    if mode == "nn":
        assert a_off % tk == 0 and b_off % tn == 0
        a_spec = pl.BlockSpec((tm, tk), lambda i, j, kk: (i, kk + a_off // tk))
        b_spec = pl.BlockSpec((tk, tn), lambda i, j, kk: (kk, j + b_off // tn))
    elif mode == "nt":
        assert a_off % tk == 0 and b_off % tk == 0
        a_spec = pl.BlockSpec((tm, tk), lambda i, j, kk: (i, kk + a_off // tk))
        b_spec = pl.BlockSpec((tn, tk), lambda i, j, kk: (j, kk + b_off // tk))
    else:
        assert a_off % tm == 0 and b_off % tn == 0
        a_spec = pl.BlockSpec((tk, tm), lambda i, j, kk: (kk, i + a_off // tm))
        b_spec = pl.BlockSpec((tk, tn), lambda i, j, kk: (kk, j + b_off // tn))
    nk = k // tk
    has_add = add is not None

    def body(*refs):
        if has_add:
            a_ref, b_ref, add_ref, o_ref, acc_ref = refs
        else:
            a_ref, b_ref, o_ref, acc_ref = refs
        kk = pl.program_id(2)

        @pl.when(kk == 0)
        def _():
            acc_ref[...] = jnp.zeros_like(acc_ref)

        acc_ref[...] += _dot(a_ref[...], b_ref[...], mode)

        @pl.when(kk == nk - 1)
        def _():
            r = acc_ref[...]
            if has_add:
                r = r + add_ref[...].astype(F32)
            o_ref[...] = r.astype(out_dtype)

    in_specs = [a_spec, b_spec]
    args = [a, b]
    if has_add:
        in_specs.append(pl.BlockSpec((tm, tn), lambda i, j, kk: (i, j)))
        args.append(add)
    return pl.pallas_call(
        body, name=name, grid=(m // tm, n // tn, nk),
        in_specs=in_specs, out_specs=pl.BlockSpec((tm, tn), lambda i, j, kk: (i, j)),
        out_shape=jax.ShapeDtypeStruct((m, n), out_dtype),
        scratch_shapes=[pltpu.VMEM((tm, tn), F32)],
        compiler_params=_params(("parallel", "parallel", "arbitrary")),
    )(*args)


def _rmsnorm_fwd(x, w, name):
    s, d = x.shape
    tr = _tile(s, 512, 8)

    def body(x_ref, w_ref, h_ref):
        xv = x_ref[...]
        rstd = lax.rsqrt(jnp.mean(xv * xv, axis=-1, keepdims=True) + EPS)
        h_ref[...] = (xv * rstd * w_ref[...]).astype(BF16)

    return pl.pallas_call(
        body, name=name, grid=(s // tr,),
        in_specs=[pl.BlockSpec((tr, d), lambda i: (i, 0)), pl.BlockSpec((1, d), lambda i: (0, 0))],
        out_specs=pl.BlockSpec((tr, d), lambda i: (i, 0)),
        out_shape=jax.ShapeDtypeStruct((s, d), BF16),
        compiler_params=_params(("parallel",)),
    )(x, w)


def _rmsnorm_bwd(dh, x, w, res, name):
    s, d = x.shape
    tr = _tile(s, 256, 8)

    def body(dh_ref, x_ref, w_ref, res_ref, dx_ref, dw_ref):
        xv = x_ref[...]
        rstd = lax.rsqrt(jnp.mean(xv * xv, axis=-1, keepdims=True) + EPS)
        xhat = xv * rstd
        dhv = dh_ref[...]
        gw = dhv * w_ref[...]
        dx_ref[...] = res_ref[...] + rstd * (gw - xhat * jnp.mean(gw * xhat, axis=-1, keepdims=True))

        @pl.when(pl.program_id(0) == 0)
        def _():
            dw_ref[...] = jnp.zeros_like(dw_ref)

        dw_ref[...] += jnp.sum(dhv * xhat, axis=0, keepdims=True)

    row = pl.BlockSpec((tr, d), lambda i: (i, 0))
    vec = pl.BlockSpec((1, d), lambda i: (0, 0))
    return pl.pallas_call(
        body, name=name, grid=(s // tr,),
        in_specs=[row, row, vec, row], out_specs=[row, vec],
        out_shape=[jax.ShapeDtypeStruct((s, d), F32), jax.ShapeDtypeStruct((1, d), F32)],
        compiler_params=_params(("arbitrary",)),
    )(dh, x, w, res)


def _conv_taps(x, w, rows):
    shifted = [x]
    for sft in (1, 2, 3):
        shifted.append(jnp.where(rows >= sft, pltpu.roll(x, sft, 0), 0.0))
    y = w[3:4, :] * shifted[0] + w[2:3, :] * shifted[1] + w[1:2, :] * shifted[2] + w[0:1, :] * shifted[3]
    return y, shifted


def _delta_pre_fwd(qkvz, conv_w, heads):
    s = qkvz.shape[0]
    nblk = 3 * heads

    def body(x_ref, w_ref, o_ref):
        part = pl.program_id(0) // heads
        rows = lax.broadcasted_iota(jnp.int32, (s, HEAD_DIM), 0)
        y, _ = _conv_taps(x_ref[...], w_ref[...], rows)
        a = y * _sigmoid(y)
        rs = lax.rsqrt(jnp.sum(a * a, axis=-1, keepdims=True) + EPS)
        fac = jnp.where(part == 0, rs * (HEAD_DIM ** -0.5), jnp.where(part == 1, rs, 1.0))
        o_ref[...] = a * fac

    return pl.pallas_call(
        body, name="delta_pre_fwd", grid=(nblk,),
        in_specs=[pl.BlockSpec((s, HEAD_DIM), lambda i: (0, i)), pl.BlockSpec((4, HEAD_DIM), lambda i: (0, i))],
        out_specs=pl.BlockSpec((s, HEAD_DIM), lambda i: (0, i)),
        out_shape=jax.ShapeDtypeStruct((s, 3 * heads * HEAD_DIM), F32),
        compiler_params=_params(("parallel",)),
    )(qkvz, conv_w)


def _delta_pre_bwd(dqkv, qkvz, conv_w, heads):
    s = qkvz.shape[0]
    nblk = 3 * heads

    def body(d_ref, x_ref, w_ref, dx_ref, dw_ref):
        part = pl.program_id(0) // heads
        rows = lax.broadcasted_iota(jnp.int32, (s, HEAD_DIM), 0)
        w = w_ref[...]
        y, shifted = _conv_taps(x_ref[...], w, rows)
        sg = _sigmoid(y)
        a = y * sg
        rs = lax.rsqrt(jnp.sum(a * a, axis=-1, keepdims=True) + EPS)
        unit = a * rs
        dn = d_ref[...]
        scale = jnp.where(part == 0, HEAD_DIM ** -0.5, 1.0)
        da_norm = scale * rs * (dn - unit * jnp.sum(dn * unit, axis=-1, keepdims=True))
        da = jnp.where(part < 2, da_norm, dn)
        dy = da * sg * (1.0 + y * (1.0 - sg))
        dx = w[3:4, :] * dy
        for sft in (1, 2, 3):
            dx = dx + w[3 - sft:4 - sft, :] * jnp.where(rows < s - sft, pltpu.roll(dy, s - sft, 0), 0.0)
        dx_ref[...] = dx
        for sft in range(4):
            dw_ref[3 - sft:4 - sft, :] = jnp.sum(dy * shifted[sft], axis=0, keepdims=True)

    col = pl.BlockSpec((s, HEAD_DIM), lambda i: (0, i))
    wsp = pl.BlockSpec((4, HEAD_DIM), lambda i: (0, i))
    return pl.pallas_call(
        body, name="delta_pre_bwd", grid=(nblk,),
        in_specs=[col, col, wsp], out_specs=[col, wsp],
        out_shape=[jax.ShapeDtypeStruct((s, 3 * heads * HEAD_DIM), F32),
                   jax.ShapeDtypeStruct((4, 3 * heads * HEAD_DIM), F32)],
        compiler_params=_params(("parallel",)),
    )(dqkv, qkvz, conv_w)


def _chunk_common(q, k, v, bd, a_log_h, dt_bias_h, h, heads):
    c = CHUNK
    lane = lax.broadcasted_iota(jnp.int32, bd.shape, 1)
    braw = jnp.sum(jnp.where(lane == h, bd, 0.0), axis=1, keepdims=True)
    draw = jnp.sum(jnp.where(lane == h + heads, bd, 0.0), axis=1, keepdims=True)
    beta = _sigmoid(braw)
    xd = draw + dt_bias_h
    sp = jnp.maximum(xd, 0.0) + jnp.log1p(jnp.exp(-jnp.abs(xd)))
    g = -jnp.exp(a_log_h) * sp
    row = lax.broadcasted_iota(jnp.int32, (c, c), 0)
    col = lax.broadcasted_iota(jnp.int32, (c, c), 1)
    g_b = jnp.broadcast_to(g, (c, c))
    g_row = jnp.sum(jnp.where(row == col, g_b, 0.0), axis=0, keepdims=True)
    gam_col = jnp.sum(jnp.where(col <= row, jnp.broadcast_to(g_row, (c, c)), 0.0), axis=1, keepdims=True)
    gam_row = jnp.sum(jnp.where(row <= col, g_b, 0.0), axis=0, keepdims=True)
    causal = row >= col
    dm = jnp.where(causal, jnp.exp(jnp.where(causal, gam_col - gam_row, 0.0)), 0.0)
    kk = _dot(k, k, "nt")
    low = jnp.where(row > col, beta * kk * dm, 0.0)
    eye = jnp.where(row == col, 1.0, 0.0)
    t = eye - low
    pw = low
    for _ in range(5):
        pw = _dot(pw, pw, exact=True)
        t = t + _dot(t, pw, exact=True)
    e = jnp.exp(gam_col)
    u = _dot(t, beta * v)
    w = _dot(t, (beta * e) * k)
    qk_raw = _dot(q, k, "nt")
    gl = jnp.sum(g, axis=0, keepdims=True)
    el = jnp.exp(gl - gam_col)
    return dict(beta=beta, xd=xd, g=g, row=row, col=col, dm=dm, kk=kk, t=t, e=e, u=u, w=w,
                qk_raw=qk_raw, qk=qk_raw * dm, gl=gl, el=el, qd=e * q, kd=el * k, cd=jnp.exp(gl))


def _delta_chunk_fwd(qkv, bd, a_log, dt_bias, heads):
    s = qkv.shape[0]
    n = s // CHUNK
    blk = lambda part: pl.BlockSpec((CHUNK, HEAD_DIM), lambda i, h: (i, part * heads + h))

    def body(q_ref, k_ref, v_ref, bd_ref, al_ref, dt_ref, o_ref, st_ref, state):
        i, h = pl.program_id(0), pl.program_id(1)

        @pl.when(i == 0)
        def _():
            state[h] = jnp.zeros((HEAD_DIM, HEAD_DIM), F32)

        cm = _chunk_common(q_ref[...], k_ref[...], v_ref[...], bd_ref[...], al_ref[h], dt_ref[h], h, heads)
        st = state[h]
        st_ref[0, 0] = st
        vn = cm["u"] - _dot(cm["w"], st)
        o_ref[...] = _dot(cm["qd"], st) + _dot(cm["qk"], vn)
        state[h] = cm["cd"] * st + _dot(cm["kd"], vn, "tn")

    smem = pl.BlockSpec(memory_space=pltpu.SMEM)
    return pl.pallas_call(
        body, name="delta_chunk_fwd", grid=(n, heads),
        in_specs=[blk(0), blk(1), blk(2), pl.BlockSpec((CHUNK, HEAD_DIM), lambda i, h: (i, 0)), smem, smem],
        out_specs=[pl.BlockSpec((CHUNK, HEAD_DIM), lambda i, h: (i, h)),
                   pl.BlockSpec((1, 1, HEAD_DIM, HEAD_DIM), lambda i, h: (i, h, 0, 0))],
        out_shape=[jax.ShapeDtypeStruct((s, heads * HEAD_DIM), F32),
                   jax.ShapeDtypeStruct((n, heads, HEAD_DIM, HEAD_DIM), F32)],
        scratch_shapes=[pltpu.VMEM((heads, HEAD_DIM, HEAD_DIM), F32)],
        compiler_params=_params(("arbitrary", "arbitrary")),
    )(qkv, qkv, qkv, bd, a_log, dt_bias)


def _delta_chunk_bwd(do, qkv, bd, states, a_log, dt_bias, heads):
    s = qkv.shape[0]
    n = s // CHUNK
    blk = lambda part: pl.BlockSpec((CHUNK, HEAD_DIM), lambda i, h: (n - 1 - i, part * heads + h))

    def body(do_ref, q_ref, k_ref, v_ref, bd_ref, st_ref, al_ref, dt_ref, dq_ref, dk_ref, dv_ref, dbd_ref, dstate):
        i, h = pl.program_id(0), pl.program_id(1)
        c = CHUNK

        @pl.when(i == 0)
        def _():
            dstate[h] = jnp.zeros((HEAD_DIM, HEAD_DIM), F32)

        q, k, v = q_ref[...], k_ref[...], v_ref[...]
        cm = _chunk_common(q, k, v, bd_ref[...], al_ref[h], dt_ref[h], h, heads)
        beta, e, dm, row, col = cm["beta"], cm["e"], cm["dm"], cm["row"], cm["col"]
        st = st_ref[0, 0]
        dsn = dstate[h]
        dov = do_ref[...]
        vn = cm["u"] - _dot(cm["w"], st)
        dvn = _dot(cm["kd"], dsn)
        dkd = _dot(vn, dsn, "nt")
        dcd = jnp.sum(jnp.sum(st * dsn, axis=1, keepdims=True), axis=0, keepdims=True)
        ds = cm["cd"] * dsn
        dqd = _dot(dov, st, "nt")
        ds = ds + _dot(cm["qd"], dov, "tn")
        dqk = _dot(dov, vn, "nt")
        dvn = dvn + _dot(cm["qk"], dov, "tn")
        dw = -_dot(dvn, st, "nt")
        ds = ds - _dot(cm["w"], dvn, "tn")
        dstate[h] = ds
        drhs_u = _dot(cm["t"], dvn, "tn")
        drhs_w = _dot(cm["t"], dw, "tn")
        da = -(_dot(drhs_u, cm["u"], "nt") + _dot(drhs_w, cm["w"], "nt"))
        dl = jnp.where(row > col, da, 0.0)
        dbeta = jnp.sum(dl * cm["kk"] * dm, axis=1, keepdims=True)
        dkk = dl * beta * dm
        dd = dl * beta * cm["kk"]
        dv_ref[...] = beta * drhs_u
        ek = e * k
        dbeta = dbeta + jnp.sum(drhs_u * v, axis=1, keepdims=True) + jnp.sum(drhs_w * ek, axis=1, keepdims=True)
        dk = (beta * e) * drhs_w
        dgam = jnp.sum(drhs_w * (beta * ek), axis=1, keepdims=True)
        dqkm = dqk * dm
        dq = _dot(dqkm, k)
        dk = dk + _dot(dqkm, q, "tn")
        dd = dd + dqk * cm["qk_raw"]
        dk = dk + _dot(dkk, k) + _dot(dkk, k, "tn")
        dq_ref[...] = dq + e * dqd
        dgam = dgam + jnp.sum(dqd * cm["qd"], axis=1, keepdims=True)
        dk_ref[...] = dk + cm["el"] * dkd
        r = jnp.sum(dkd * cm["kd"], axis=1, keepdims=True)
        dgam = dgam - r
        dgl = jnp.sum(r, axis=0, keepdims=True) + dcd * cm["cd"]
        mm = dd * dm
        colsum = jnp.sum(mm, axis=0, keepdims=True)
        colsum_c = jnp.sum(jnp.where(row == col, jnp.broadcast_to(colsum, (c, c)), 0.0), axis=1, keepdims=True)
        dgam = dgam + jnp.sum(mm, axis=1, keepdims=True) - colsum_c
        ridx = lax.broadcasted_iota(jnp.int32, (c, 1), 0)
        dgam = dgam + jnp.where(ridx == c - 1, dgl, 0.0)
        dgam_row = jnp.sum(jnp.where(row == col, jnp.broadcast_to(dgam, (c, c)), 0.0), axis=0, keepdims=True)
        dg = jnp.sum(jnp.where(col >= row, jnp.broadcast_to(dgam_row, (c, c)), 0.0), axis=1, keepdims=True)
        d_xd = dg * (-jnp.exp(al_ref[h])) * _sigmoid(cm["xd"])
        d_braw = dbeta * beta * (1.0 - beta)
        lane = lax.broadcasted_iota(jnp.int32, (c, HEAD_DIM), 1)
        contrib = (jnp.where(lane == h, d_braw, 0.0) + jnp.where(lane == h + heads, d_xd, 0.0)
                   + jnp.where(lane == h + 2 * heads, dg * cm["g"], 0.0))

        @pl.when(h == 0)
        def _():
            dbd_ref[...] = contrib

        @pl.when(h > 0)
        def _():
            dbd_ref[...] += contrib

    smem = pl.BlockSpec(memory_space=pltpu.SMEM)
    shared = pl.BlockSpec((CHUNK, HEAD_DIM), lambda i, h: (n - 1 - i, 0))
    head = pl.BlockSpec((CHUNK, HEAD_DIM), lambda i, h: (n - 1 - i, h))
    dw = heads * HEAD_DIM
    dq, dk, dv, dbd = pl.pallas_call(
        body, name="delta_chunk_bwd", grid=(n, heads),
        in_specs=[head, blk(0), blk(1), blk(2), shared,
                  pl.BlockSpec((1, 1, HEAD_DIM, HEAD_DIM), lambda i, h: (n - 1 - i, h, 0, 0)), smem, smem],
        out_specs=[head, head, head, shared],
        out_shape=[jax.ShapeDtypeStruct((s, dw), F32)] * 3 + [jax.ShapeDtypeStruct((s, HEAD_DIM), F32)],
        scratch_shapes=[pltpu.VMEM((heads, HEAD_DIM, HEAD_DIM), F32)],
        compiler_params=_params(("arbitrary", "arbitrary")),
    )(do, qkv, qkv, qkv, bd, states, a_log, dt_bias)
    return dq, dk, dv, dbd


def _delta_post_fwd(o, qkvz, w, heads):
    s = o.shape[0]
    tr = _tile(s, 1024, 8)

    def body(o_ref, z_ref, w_ref, out_ref):
        ov, z = o_ref[...], z_ref[...]
        rstd = lax.rsqrt(jnp.mean(ov * ov, axis=-1, keepdims=True) + EPS)
        out_ref[...] = (ov * rstd * w_ref[...] * (z * _sigmoid(z))).astype(BF16)

    return pl.pallas_call(
        body, name="delta_post_fwd", grid=(s // tr, heads),
        in_specs=[pl.BlockSpec((tr, HEAD_DIM), lambda i, h: (i, h)),
                  pl.BlockSpec((tr, HEAD_DIM), lambda i, h: (i, 3 * heads + h)),
                  pl.BlockSpec((1, HEAD_DIM), lambda i, h: (0, 0))],
        out_specs=pl.BlockSpec((tr, HEAD_DIM), lambda i, h: (i, h)),
        out_shape=jax.ShapeDtypeStruct((s, heads * HEAD_DIM), BF16),
        compiler_params=_params(("parallel", "parallel")),
    )(o, qkvz, w)


def _delta_post_bwd(dmix, o, qkvz, w, heads):
    s = o.shape[0]
    tr = _tile(s, 1024, 8)

    def body(d_ref, o_ref, z_ref, w_ref, do_ref, dz_ref, dw_ref):
        d, ov, z, wv = d_ref[...], o_ref[...], z_ref[...], w_ref[...]
        sg = _sigmoid(z)
        rstd = lax.rsqrt(jnp.mean(ov * ov, axis=-1, keepdims=True) + EPS)
        ohat = ov * rstd
        dz_ref[...] = d * (ohat * wv) * sg * (1.0 + z * (1.0 - sg))
        dn = d * (z * sg)
        gw = dn * wv
        do_ref[...] = rstd * (gw - ohat * jnp.mean(gw * ohat, axis=-1, keepdims=True))

        @pl.when((pl.program_id(0) == 0) & (pl.program_id(1) == 0))
        def _():
            dw_ref[...] = jnp.zeros_like(dw_ref)

        dw_ref[...] += jnp.sum(dn * ohat, axis=0, keepdims=True)

    head = pl.BlockSpec((tr, HEAD_DIM), lambda i, h: (i, h))
    vec = pl.BlockSpec((1, HEAD_DIM), lambda i, h: (0, 0))
    dw = heads * HEAD_DIM
    return pl.pallas_call(
        body, name="delta_post_bwd", grid=(s // tr, heads),
        in_specs=[head, head, pl.BlockSpec((tr, HEAD_DIM), lambda i, h: (i, 3 * heads + h)), vec],
        out_specs=[head, head, vec],
        out_shape=[jax.ShapeDtypeStruct((s, dw), F32), jax.ShapeDtypeStruct((s, dw), F32),
                   jax.ShapeDtypeStruct((1, HEAD_DIM), F32)],
        compiler_params=_params(("arbitrary", "arbitrary")),
    )(dmix, o, qkvz, w)


def _rope_tables(positions, s):
    half = HEAD_DIM // 2
    inv_freq = ROPE_THETA ** (-jnp.arange(half, dtype=F32) / half)
    ang = positions.reshape(s, 1).astype(F32) * inv_freq
    cos, sin = jnp.cos(ang), jnp.sin(ang)
    return jnp.concatenate([cos, cos], axis=-1), jnp.concatenate([-sin, sin], axis=-1)


def _attn_pre_fwd(aqkv, wq, wk, cosf, sinf, heads):
    s = aqkv.shape[0]
    tr = _tile(s, 1024, 8)

    def body(x_ref, wq_ref, wk_ref, c_ref, s_ref, o_ref):
        xv = x_ref[...]
        wv = jnp.where(pl.program_id(1) < heads, wq_ref[...], wk_ref[...])
        y = xv * lax.rsqrt(jnp.mean(xv * xv, axis=-1, keepdims=True) + EPS) * wv
        o_ref[...] = y * c_ref[...] + pltpu.roll(y, HEAD_DIM // 2, 1) * s_ref[...]

    blk = pl.BlockSpec((tr, HEAD_DIM), lambda i, j: (i, j))
    vec = pl.BlockSpec((1, HEAD_DIM), lambda i, j: (0, 0))
    tab = pl.BlockSpec((tr, HEAD_DIM), lambda i, j: (i, 0))
    return pl.pallas_call(
        body, name="attn_pre_fwd", grid=(s // tr, 2 * heads),
        in_specs=[blk, vec, vec, tab, tab], out_specs=blk,
        out_shape=jax.ShapeDtypeStruct((s, 2 * heads * HEAD_DIM), F32),
        compiler_params=_params(("parallel", "parallel")),
    )(aqkv, wq, wk, cosf, sinf)


def _attn_mask(j):
    qi = lax.broadcasted_iota(jnp.int32, (SPAN, 2 * SPAN), 0)
    ki = lax.broadcasted_iota(jnp.int32, (SPAN, 2 * SPAN), 1)
    dist = qi + SPAN - ki
    return (dist >= 0) & (dist <= SPAN) & ((j > 0) | (ki >= SPAN))


def _attn_fwd(qk, v_src, v_col0, heads, d):
    s = qk.shape[0]
    u = s // d
    nb = u // SPAN
    aw = heads * HEAD_DIM
    qk_v = qk.reshape(u, d * qk.shape[1])
    v_v = v_src.reshape(u, d * v_src.shape[1])
    qw, vw = qk.shape[1] // HEAD_DIM, v_src.shape[1] // HEAD_DIM

    def body(q_ref, kp_ref, kc_ref, vp_ref, vc_ref, o_ref, l_ref):
        j = pl.program_id(2)
        kcat = jnp.concatenate([kp_ref[...], kc_ref[...]], axis=0)
        vcat = jnp.concatenate([vp_ref[...], vc_ref[...]], axis=0)
        sc = _dot(q_ref[...], kcat, "nt") * (HEAD_DIM ** -0.5)
        sc = jnp.where(_attn_mask(j), sc, NEG)
        m = jnp.max(sc, axis=-1, keepdims=True)
        p = jnp.exp(sc - m)
        den = jnp.sum(p, axis=-1, keepdims=True)
        o_ref[...] = _dot(p, vcat) / den
        l_ref[...] = jnp.broadcast_to(m + jnp.log(den), (SPAN, HEAD_DIM))

    cur = lambda off, w: pl.BlockSpec((SPAN, HEAD_DIM), lambda h, r, j: (j, r * w + off + h))
    prev = lambda off, w: pl.BlockSpec((SPAN, HEAD_DIM), lambda h, r, j: (jnp.maximum(j - 1, 0), r * w + off + h))
    out = pl.BlockSpec((SPAN, HEAD_DIM), lambda h, r, j: (j, r * heads + h))
    o, lse = pl.pallas_call(
        body, name=f"attn_fwd_d{d}", grid=(heads, d, nb),
        in_specs=[cur(0, qw), prev(heads, qw), cur(heads, qw), prev(v_col0, vw), cur(v_col0, vw)],
        out_specs=[out, out],
        out_shape=[jax.ShapeDtypeStruct((u, d * aw), F32)] * 2,
        compiler_params=_params(("parallel", "parallel", "parallel")),
    )(qk_v, qk_v, qk_v, v_v, v_v)
    return o.reshape(s, aw), lse.reshape(s, aw)


def _attn_merge_fwd(os_, ls_, w, heads):
    s = os_[0].shape[0]
    tr = _tile(s, 512, 8)

    def body(o1, o2, o3, l1, l2, l3, w_ref, mix_ref, ob_ref, lse_ref):
        la, lb, lc = l1[...], l2[...], l3[...]
        m = jnp.maximum(jnp.maximum(la, lb), lc)
        ea, eb, ec = jnp.exp(la - m), jnp.exp(lb - m), jnp.exp(lc - m)
        tot = ea + eb + ec
        ob = (ea * o1[...] + eb * o2[...] + ec * o3[...]) / tot
        ob_ref[...] = ob
        lse_ref[...] = m + jnp.log(tot)
        rstd = lax.rsqrt(jnp.mean(ob * ob, axis=-1, keepdims=True) + EPS)
        mix_ref[...] = (ob * rstd * w_ref[...]).astype(BF16)

    blk = pl.BlockSpec((tr, HEAD_DIM), lambda i, h: (i, h))
    vec = pl.BlockSpec((1, HEAD_DIM), lambda i, h: (0, 0))
    aw = heads * HEAD_DIM
    return pl.pallas_call(
        body, name="attn_merge_fwd", grid=(s // tr, heads),
        in_specs=[blk] * 6 + [vec], out_specs=[blk, blk, blk],
        out_shape=[jax.ShapeDtypeStruct((s, aw), BF16), jax.ShapeDtypeStruct((s, aw), F32),
                   jax.ShapeDtypeStruct((s, aw), F32)],
        compiler_params=_params(("parallel", "parallel")),
    )(*os_, *ls_, w)


def _attn_merge_bwd(dmix, ob, w, heads):
    s = ob.shape[0]
    tr = _tile(s, 1024, 8)

    def body(d_ref, ob_ref, w_ref, do_ref, dsum_ref, dw_ref):
        d, ov = d_ref[...], ob_ref[...]
        rstd = lax.rsqrt(jnp.mean(ov * ov, axis=-1, keepdims=True) + EPS)
        ohat = ov * rstd
        gw = d * w_ref[...]
        dov = rstd * (gw - ohat * jnp.mean(gw * ohat, axis=-1, keepdims=True))
        do_ref[...] = dov
        dsum_ref[...] = jnp.broadcast_to(jnp.sum(dov * ov, axis=-1, keepdims=True), dov.shape)

        @pl.when((pl.program_id(0) == 0) & (pl.program_id(1) == 0))
        def _():
            dw_ref[...] = jnp.zeros_like(dw_ref)

        dw_ref[...] += jnp.sum(d * ohat, axis=0, keepdims=True)

    blk = pl.BlockSpec((tr, HEAD_DIM), lambda i, h: (i, h))
    vec = pl.BlockSpec((1, HEAD_DIM), lambda i, h: (0, 0))
    aw = heads * HEAD_DIM
    return pl.pallas_call(
        body, name="attn_merge_bwd", grid=(s // tr, heads),
        in_specs=[pl.BlockSpec((tr, HEAD_DIM), lambda i, h: (i, heads + h)), blk, vec],
        out_specs=[blk, blk, vec],
        out_shape=[jax.ShapeDtypeStruct((s, aw), F32), jax.ShapeDtypeStruct((s, aw), F32),
                   jax.ShapeDtypeStruct((1, HEAD_DIM), F32)],
        compiler_params=_params(("arbitrary", "arbitrary")),
    )(dmix, ob, w)


def _attn_bwd(qk, v_src, v_col0, do, lse, dsum, heads, d):
    s = qk.shape[0]
    u = s // d
    nb = u // SPAN
    aw = heads * HEAD_DIM
    view = lambda a: a.reshape(u, d * a.shape[1])
    qw, vw = qk.shape[1] // HEAD_DIM, v_src.shape[1] // HEAD_DIM

    def body(q_ref, kp_ref, kc_ref, vp_ref, vc_ref, do_ref, l_ref, ds_ref, dq_ref, dka_ref, dkb_ref, dva_ref, dvb_ref):
        j = pl.program_id(2)
        q = q_ref[...]
        kcat = jnp.concatenate([kp_ref[...], kc_ref[...]], axis=0)
        vcat = jnp.concatenate([vp_ref[...], vc_ref[...]], axis=0)
        dov = do_ref[...]
        sc = _dot(q, kcat, "nt") * (HEAD_DIM ** -0.5)
        p = jnp.where(_attn_mask(j), jnp.exp(sc - l_ref[:, 0:1]), 0.0)
        dp = _dot(dov, vcat, "nt")
        dsc = p * (dp - ds_ref[:, 0:1]) * (HEAD_DIM ** -0.5)
        dq_ref[...] = _dot(dsc, kcat)
        dk = _dot(dsc, q, "tn")
        dv = _dot(p, dov, "tn")
        dka_ref[...] = dk[:SPAN]
        dkb_ref[...] = dk[SPAN:]
        dva_ref[...] = dv[:SPAN]
        dvb_ref[...] = dv[SPAN:]

    cur = lambda off, w: pl.BlockSpec((SPAN, HEAD_DIM), lambda h, r, j: (j, r * w + off + h))
    prev = lambda off, w: pl.BlockSpec((SPAN, HEAD_DIM), lambda h, r, j: (jnp.maximum(j - 1, 0), r * w + off + h))
    out = pl.BlockSpec((SPAN, HEAD_DIM), lambda h, r, j: (j, r * heads + h))
    res = pl.pallas_call(
        body, name=f"attn_bwd_d{d}", grid=(heads, d, nb),
        in_specs=[cur(0, qw), prev(heads, qw), cur(heads, qw), prev(v_col0, vw), cur(v_col0, vw), out, out, out],
        out_specs=[out] * 5,
        out_shape=[jax.ShapeDtypeStruct((u, d * aw), F32)] * 5,
        compiler_params=_params(("parallel", "parallel", "parallel")),
    )(view(qk), view(qk), view(qk), view(v_src), view(v_src), view(do), view(lse), view(dsum))
    return [r.reshape(s, aw) for r in res]


def _attn_pre_bwd(grads, aqkv, wq, wk, cosf, sinf, heads):
    s = aqkv.shape[0]
    tr = SPAN
    nrow = s // tr
    flat = []
    for g in grads:
        flat.extend(g)

    def body(*refs):
        g_refs, (x_ref, wq_ref, wk_ref, c_ref, s_ref, dx_ref, dwq_ref, dwk_ref) = refs[:15], refs[15:]
        i, j = pl.program_id(0), pl.program_id(1)
        kind = j // heads
        tot_q = jnp.zeros((tr, HEAD_DIM), F32)
        tot_k = jnp.zeros((tr, HEAD_DIM), F32)
        tot_v = jnp.zeros((tr, HEAD_DIM), F32)
        for p_i, d in enumerate(DILATIONS):
            dq, dka, dkb, dva, dvb = g_refs[5 * p_i:5 * p_i + 5]
            live = i + d < nrow
            tot_q = tot_q + dq[...]
            tot_k = tot_k + dkb[...] + jnp.where(live, dka[...], 0.0)
            tot_v = tot_v + dvb[...] + jnp.where(live, dva[...], 0.0)
        dout = jnp.where(kind == 0, tot_q, tot_k)
        dy = dout * c_ref[...] + pltpu.roll(dout * s_ref[...], HEAD_DIM // 2, 1)
        xv = x_ref[...]
        wv = jnp.where(kind == 0, wq_ref[...], wk_ref[...])
        rstd = lax.rsqrt(jnp.mean(xv * xv, axis=-1, keepdims=True) + EPS)
        xhat = xv * rstd
        gw = dy * wv
        dxn = rstd * (gw - xhat * jnp.mean(gw * xhat, axis=-1, keepdims=True))
        dx_ref[...] = jnp.where(kind == 2, tot_v, dxn)
        dwc = jnp.sum(dy * xhat, axis=0, keepdims=True)

        @pl.when((i == 0) & (j == 0))
        def _():
            dwq_ref[...] = jnp.zeros_like(dwq_ref)
            dwk_ref[...] = jnp.zeros_like(dwk_ref)

        @pl.when(kind == 0)
        def _():
            dwq_ref[...] += dwc

        @pl.when(kind == 1)
        def _():
            dwk_ref[...] += dwc

    in_specs = []
    for d in DILATIONS:
        same = pl.BlockSpec((tr, HEAD_DIM), lambda i, j: (i, j % heads))
        later = pl.BlockSpec((tr, HEAD_DIM), lambda i, j, d=d: (jnp.minimum(i + d, nrow - 1), j % heads))
        in_specs += [same, later, same, later, same]
    blk = pl.BlockSpec((tr, HEAD_DIM), lambda i, j: (i, j))
    vec = pl.BlockSpec((1, HEAD_DIM), lambda i, j: (0, 0))
    tab = pl.BlockSpec((tr, HEAD_DIM), lambda i, j: (i, 0))
    in_specs += [blk, vec, vec, tab, tab]
    return pl.pallas_call(
        body, name="attn_pre_bwd", grid=(nrow, 3 * heads),
        in_specs=in_specs, out_specs=[blk, vec, vec],
        out_shape=[jax.ShapeDtypeStruct((s, 3 * heads * HEAD_DIM), F32),
                   jax.ShapeDtypeStruct((1, HEAD_DIM), F32), jax.ShapeDtypeStruct((1, HEAD_DIM), F32)],
        compiler_params=_params(("arbitrary", "arbitrary")),
    )(*flat, aqkv, wq, wk, cosf, sinf)


def _swiglu_fwd(gu, ff):
    s = gu.shape[0]
    tr, tc = _tile(s, 512, 8), _tile(ff, 1024, 128)
    nc = ff // tc

    def body(g_ref, u_ref, o_ref):
        g = g_ref[...]
        o_ref[...] = (g * _sigmoid(g) * u_ref[...]).astype(BF16)

    return pl.pallas_call(
        body, name="swiglu_fwd", grid=(s // tr, nc),
        in_specs=[pl.BlockSpec((tr, tc), lambda i, j: (i, j)), pl.BlockSpec((tr, tc), lambda i, j: (i, j + nc))],
        out_specs=pl.BlockSpec((tr, tc), lambda i, j: (i, j)),
        out_shape=jax.ShapeDtypeStruct((s, ff), BF16),
        compiler_params=_params(("parallel", "parallel")),
    )(gu, gu)


def _swiglu_bwd(dact, gu, ff):
    s = gu.shape[0]
    tr, tc = _tile(s, 512, 8), _tile(ff, 1024, 128)
    nc = ff // tc

    def body(d_ref, g_ref, u_ref, dg_ref, du_ref):
        d, g = d_ref[...], g_ref[...]
        sg = _sigmoid(g)
        dg_ref[...] = (d * u_ref[...] * sg * (1.0 + g * (1.0 - sg))).astype(BF16)
        du_ref[...] = (d * g * sg).astype(BF16)

    lo = pl.BlockSpec((tr, tc), lambda i, j: (i, j))
    hi = pl.BlockSpec((tr, tc), lambda i, j: (i, j + nc))
    dg, du = pl.pallas_call(
        body, name="swiglu_bwd", grid=(s // tr, nc),
        in_specs=[lo, lo, hi], out_specs=[lo, lo],
        out_shape=[jax.ShapeDtypeStruct((s, ff), BF16)] * 2,
        compiler_params=_params(("parallel", "parallel")),
    )(dact, gu, gu)
    return dg, du


def _loss_head(y, target):
    s, d = y.shape
    tr = _tile(s, 512, 8)

    def body(y_ref, t_ref, dy_ref, l_ref):
        err = y_ref[...] - t_ref[...]
        dy_ref[...] = err * (1.0 / d)
        part = 0.5 * jnp.sum(jnp.sum(err * err, axis=-1, keepdims=True) * (1.0 / d), axis=0, keepdims=True)

        @pl.when(pl.program_id(0) == 0)
        def _():
            l_ref[...] = jnp.zeros_like(l_ref)

        lane = lax.broadcasted_iota(jnp.int32, (1, HEAD_DIM), 1)
        l_ref[...] += jnp.where(lane == 0, part, 0.0)

    row = pl.BlockSpec((tr, d), lambda i: (i, 0))
    return pl.pallas_call(
        body, name="loss_head", grid=(s // tr,),
        in_specs=[row, row], out_specs=[row, pl.BlockSpec((1, HEAD_DIM), lambda i: (0, 0))],
        out_shape=[jax.ShapeDtypeStruct((s, d), F32), jax.ShapeDtypeStruct((1, HEAD_DIM), F32)],
        compiler_params=_params(("arbitrary",)),
    )(y, target)


def _colsum(a, name):
    s, d = a.shape
    tr = _tile(s, 1024, 8)

    def body(a_ref, o_ref):
        @pl.when(pl.program_id(0) == 0)
        def _():
            o_ref[...] = jnp.zeros_like(o_ref)

        o_ref[...] += jnp.sum(a_ref[...], axis=0, keepdims=True)

    return pl.pallas_call(
        body, name=name, grid=(s // tr,),
        in_specs=[pl.BlockSpec((tr, d), lambda i: (i, 0))], out_specs=pl.BlockSpec((1, d), lambda i: (0, 0)),
        out_shape=jax.ShapeDtypeStruct((1, d), F32),
        compiler_params=_params(("arbitrary",)),
    )(a)


def _local_step(x, positions, target, small, w_qkvz, w_bd, w_attn, w_out_a, w_out_b, w_gu, w_down, conv_w):
    s, dmod = x.shape
    heads = dmod // (2 * HEAD_DIM)
    dw = heads * HEAD_DIM
    ff = w_down.shape[0]
    a_log, dt_bias = small["a_log"].reshape(-1), small["dt_bias"].reshape(-1)
    cosf, sinf = _rope_tables(positions, s)

    h1 = _rmsnorm_fwd(x, small["attn_norm_w"], "norm1_fwd")
    qkvz = _matmul(h1, w_qkvz, "nn", "proj_qkvz")
    bd = _matmul(h1, w_bd, "nn", "proj_bd")
    aqkv = _matmul(h1, w_attn, "nn", "proj_attn")
    dqkv = _delta_pre_fwd(qkvz, conv_w, heads)
    o_d, states = _delta_chunk_fwd(dqkv, bd, a_log, dt_bias, heads)
    mix_a = _delta_post_fwd(o_d, qkvz, small["delta_out_norm_w"], heads)
    qk_rot = _attn_pre_fwd(aqkv, small["q_norm_w"], small["k_norm_w"], cosf, sinf, heads)
    outs = [_attn_fwd(qk_rot, aqkv, 2 * heads, heads, d) for d in DILATIONS]
    mix_b, ob, lse = _attn_merge_fwd([o for o, _ in outs], [l for _, l in outs], small["attn_out_norm_w"], heads)
    x1 = _matmul(mix_a, w_out_a, "nn", "out_proj_a", add=x)
    x1 = _matmul(mix_b, w_out_b, "nn", "out_proj_b", add=x1)
    h2 = _rmsnorm_fwd(x1, small["ffn_norm_w"], "norm2_fwd")
    gu = _matmul(h2, w_gu, "nn", "ffn_gate_up")
    act = _swiglu_fwd(gu, ff)
    y = _matmul(act, w_down, "nn", "ffn_down", add=x1)
    dy, loss_row = _loss_head(y, target)

    dact = _matmul(dy, w_down, "nt", "ffn_down_dx")
    g_w_down = _matmul(act, dy, "tn", "ffn_down_dw")
    dgate, dup = _swiglu_bwd(dact, gu, ff)
    dh2 = _matmul(dgate, w_gu, "nt", "ffn_gate_dx", b_cols=(0, ff))
    dh2 = _matmul(dup, w_gu, "nt", "ffn_up_dx", b_cols=(ff, ff), add=dh2)
    g_w_gate = _matmul(h2, dgate, "tn", "ffn_gate_dw")
    g_w_up = _matmul(h2, dup, "tn", "ffn_up_dw")
    dx1, g_ffn_norm = _rmsnorm_bwd(dh2, x1, small["ffn_norm_w"], dy, "norm2_bwd")
    dmix = _matmul(dx1, w_out_a, "nt", "out_proj_dx_a")
    dmix_b = _matmul(dx1, w_out_b, "nt", "out_proj_dx_b")
    dmix = jnp.concatenate([dmix, dmix_b], axis=1)
    g_w_out_a = _matmul(mix_a, dx1, "tn", "out_proj_dw_a")
    g_w_out_b = _matmul(mix_b, dx1, "tn", "out_proj_dw_b")
    dob, dsum, g_attn_out_norm = _attn_merge_bwd(dmix, ob, small["attn_out_norm_w"], heads)
    grads = [_attn_bwd(qk_rot, aqkv, 2 * heads, dob, lse, dsum, heads, d) for d in DILATIONS]
    d_aqkv, g_q_norm, g_k_norm = _attn_pre_bwd(grads, aqkv, small["q_norm_w"], small["k_norm_w"], cosf, sinf, heads)
    do_d, dz, g_delta_out_norm = _delta_post_bwd(dmix, o_d, qkvz, small["delta_out_norm_w"], heads)
    ddq, ddk, ddv, dbd = _delta_chunk_bwd(do_d, dqkv, bd, states, a_log, dt_bias, heads)
    d_qkv_raw, g_conv = _delta_pre_bwd(jnp.concatenate([ddq, ddk, ddv], axis=1), qkvz, conv_w, heads)
    d_qkvz = jnp.concatenate([d_qkv_raw, dz], axis=1)
    bd_sums = _colsum(dbd, "bd_colsum")
    dh1 = _matmul(d_qkvz, w_qkvz, "nt", "proj_qkvz_dx")
    dh1 = _matmul(d_aqkv, w_attn, "nt", "proj_attn_dx", add=dh1)
    dh1 = _matmul(dbd, w_bd, "nt", "proj_bd_dx", add=dh1)
    g_w_qkvz = _matmul(h1, d_qkvz, "tn", "proj_qkvz_dw")
    g_w_bd = _matmul(h1, dbd, "tn", "proj_bd_dw")
    g_w_attn = _matmul(h1, d_aqkv, "tn", "proj_attn_dw")
    grad_x, g_attn_norm = _rmsnorm_bwd(dh1, x, small["attn_norm_w"], dx1, "norm1_bwd")
    small_grads = dict(
        attn_norm_w=g_attn_norm, a_log=bd_sums[:, 2 * heads:3 * heads], dt_bias=bd_sums[:, heads:2 * heads],
        delta_out_norm_w=g_delta_out_norm, q_norm_w=g_q_norm, k_norm_w=g_k_norm,
        attn_out_norm_w=g_attn_out_norm, ffn_norm_w=g_ffn_norm, conv_w=g_conv)
    big_grads = dict(w_qkvz=g_w_qkvz, w_bd=g_w_bd, w_attn=g_w_attn, w_out_a=g_w_out_a, w_out_b=g_w_out_b,
                     w_gate=g_w_gate, w_up=g_w_up, w_down=g_w_down)
    return loss_row, grad_x, small_grads, big_grads


def _adamw(w, g, m, v, name):
    r, c = w.shape
    tr = _tile(r, 256, 8)

    def body(w_ref, g_ref, m_ref, v_ref, d_ref, nm_ref, nv_ref):
        gv = g_ref[...]
        nm = ADAM_B1 * m_ref[...] + (1.0 - ADAM_B1) * gv
        nv = ADAM_B2 * v_ref[...] + (1.0 - ADAM_B2) * (gv * gv)
        m_hat = nm / (1.0 - ADAM_B1 ** ADAM_STEP)
        v_hat = nv / (1.0 - ADAM_B2 ** ADAM_STEP)
        d_ref[...] = -ADAM_LR * (m_hat / (jnp.sqrt(v_hat) + ADAM_EPS) + ADAM_WD * w_ref[...])
        nm_ref[...] = nm
        nv_ref[...] = nv

    blk = pl.BlockSpec((tr, c), lambda i: (i, 0))
    return pl.pallas_call(
        body, name=name, grid=(r // tr,),
        in_specs=[blk] * 4, out_specs=[blk] * 3,
        out_shape=[jax.ShapeDtypeStruct((r, c), F32)] * 3,
        compiler_params=_params(("parallel",)),
    )(w, g, m, v)


def _add_bf16(a, b, name):
    n, r, c = a.shape
    tr = _tile(r, 512, 16)

    def body(a_ref, b_ref, o_ref):
        o_ref[...] = (a_ref[...].astype(F32) + b_ref[...].astype(F32)).astype(BF16)

    blk = pl.BlockSpec((1, tr, c), lambda i, j: (i, j, 0))
    return pl.pallas_call(
        body, name=name, grid=(n, r // tr), in_specs=[blk, blk], out_specs=blk,
        out_shape=jax.ShapeDtypeStruct((n, r, c), BF16),
        compiler_params=_params(("parallel", "parallel")),
    )(a, b)


def _sum4_f32(own, others, name):
    r, c = own.shape
    tr = _tile(r, 512, 16)

    def body(a_ref, b_ref, o_ref):
        acc = a_ref[...].astype(F32)
        for j in range(3):
            acc = acc + b_ref[j].astype(F32)
        o_ref[...] = acc

    return pl.pallas_call(
        body, name=name, grid=(r // tr,),
        in_specs=[pl.BlockSpec((tr, c), lambda i: (i, 0)), pl.BlockSpec((3, tr, c), lambda i: (0, i, 0))],
        out_specs=pl.BlockSpec((tr, c), lambda i: (i, 0)),
        out_shape=jax.ShapeDtypeStruct((r, c), F32),
        compiler_params=_params(("parallel",)),
    )(own, others)


def _place():
    x, y, c = lax.axis_index("x"), lax.axis_index("y"), lax.axis_index("c")
    other_chips = [(1 - x, y), (x, 1 - y), (1 - x, 1 - y)]
    return x, y, c, (x, y, 1 - c), other_chips


ANY = pl.BlockSpec(memory_space=pl.ANY)


def _all_gather_weights(flat):
    r, lanes = flat.shape
    half = r // 2

    def body(src_ref, out_ref, send_sems, recv_sems, local_sem):
        x, y, c, sibling, chips = _place()

        def rows(px, py, hc):
            return out_ref.at[2 * px + py, pl.ds(hc * half, half), :]

        def copy(k, src, dst, to):
            return pltpu.make_async_remote_copy(src_ref=src, dst_ref=dst, send_sem=send_sems.at[k],
                                                recv_sem=recv_sems.at[k], device_id=to, device_id_type=MESH)

        mine = pltpu.make_async_copy(src_ref, out_ref.at[2 * x + y], local_sem)
        mine.start()
        first = [copy(j, src_ref.at[pl.ds(c * half, half), :], rows(x, y, c), (px, py, c))
                 for j, (px, py) in enumerate(chips)]
        for cp in first:
            cp.start()
        passed = []
        for j, (px, py) in enumerate(chips):
            copy(j, rows(px, py, c), rows(px, py, c), (px, py, c)).wait_recv()
            fw = copy(3 + j, rows(px, py, c), rows(px, py, c), sibling)
            fw.start()
            passed.append(fw)
        for j, (px, py) in enumerate(chips):
            copy(3 + j, rows(px, py, 1 - c), rows(px, py, 1 - c), sibling).wait_recv()
        for cp in first + passed:
            cp.wait_send()
        mine.wait()

    return pl.pallas_call(
        body, name="all_gather_weights", in_specs=[ANY], out_specs=ANY,
        out_shape=jax.ShapeDtypeStruct((4, r, lanes), flat.dtype),
        scratch_shapes=[pltpu.SemaphoreType.DMA((6,)), pltpu.SemaphoreType.DMA((6,)), pltpu.SemaphoreType.DMA],
    )(flat)


def _swap_with_sibling(a, name):
    def body(a_ref, o_ref, send_sem, recv_sem):
        _, _, _, sibling, _ = _place()
        cp = pltpu.make_async_remote_copy(src_ref=a_ref, dst_ref=o_ref, send_sem=send_sem, recv_sem=recv_sem,
                                          device_id=sibling, device_id_type=MESH)
        cp.start()
        cp.wait()

    return pl.pallas_call(
        body, name=name, in_specs=[ANY], out_specs=ANY, out_shape=jax.ShapeDtypeStruct(a.shape, a.dtype),
        scratch_shapes=[pltpu.SemaphoreType.DMA, pltpu.SemaphoreType.DMA],
    )(a)


def _scatter_to_chips(p):
    _, r, lanes = p.shape

    def body(p_ref, b_ref, send_sems, recv_sems):
        _, _, c, _, chips = _place()
        cps = [pltpu.make_async_remote_copy(src_ref=p_ref.at[2 * px + py], dst_ref=b_ref.at[j],
                                            send_sem=send_sems.at[j], recv_sem=recv_sems.at[j],
                                            device_id=(px, py, c), device_id_type=MESH)
               for j, (px, py) in enumerate(chips)]
        for cp in cps:
            cp.start()
        for cp in cps:
            cp.wait()

    return pl.pallas_call(
        body, name="scatter_to_chips", in_specs=[ANY], out_specs=ANY,
        out_shape=jax.ShapeDtypeStruct((3, r, lanes), p.dtype),
        scratch_shapes=[pltpu.SemaphoreType.DMA((3,)), pltpu.SemaphoreType.DMA((3,))],
    )(p)


def _join_halves(f):
    half, lanes = f.shape

    def body(f_ref, out_ref, send_sem, recv_sem, local_sem):
        _, _, c, sibling, _ = _place()
        mine_rows = out_ref.at[pl.ds(c * half, half), :]
        their_rows = out_ref.at[pl.ds((1 - c) * half, half), :]
        mine = pltpu.make_async_copy(f_ref, mine_rows, local_sem)
        mine.start()
        cp = pltpu.make_async_remote_copy(src_ref=f_ref, dst_ref=mine_rows, send_sem=send_sem, recv_sem=recv_sem,
                                          device_id=sibling, device_id_type=MESH)
        cp.start()
        pltpu.make_async_remote_copy(src_ref=f_ref, dst_ref=their_rows, send_sem=send_sem, recv_sem=recv_sem,
                                     device_id=sibling, device_id_type=MESH).wait_recv()
        cp.wait_send()
        mine.wait()

    return pl.pallas_call(
        body, name="join_halves", in_specs=[ANY], out_specs=ANY,
        out_shape=jax.ShapeDtypeStruct((2 * half, lanes), f.dtype),
        scratch_shapes=[pltpu.SemaphoreType.DMA, pltpu.SemaphoreType.DMA, pltpu.SemaphoreType.DMA],
    )(f)


def _all_reduce_small(v):
    r, lanes = v.shape

    def body(v_ref, out_ref, buf, send_sems, recv_sems):
        x, y, c, sibling, chips = _place()

        def slot(px, py, pc):
            return buf.at[4 * px + 2 * py + pc]

        def copy(k, block, to, src=None):
            return pltpu.make_async_remote_copy(src_ref=slot(*block) if src is None else src, dst_ref=slot(*block),
                                                send_sem=send_sems.at[k], recv_sem=recv_sems.at[k],
                                                device_id=to, device_id_type=MESH)

        me = (x, y, c)
        buf[4 * x + 2 * y + c] = v_ref[...]
        first = [copy(0, me, sibling, src=v_ref)]
        first += [copy(1 + j, me, (*chip, c), src=v_ref) for j, chip in enumerate(chips)]
        for cp in first:
            cp.start()
        passed = [copy(4 + j, (*chip, c), sibling) for j, chip in enumerate(chips)]
        for j, chip in enumerate(chips):
            copy(1 + j, (*chip, c), me).wait_recv()
            passed[j].start()
        copy(0, (x, y, 1 - c), me).wait_recv()
        for j, chip in enumerate(chips):
            copy(4 + j, (*chip, 1 - c), me).wait_recv()
        for cp in first + passed:
            cp.wait_send()
        acc = buf[0]
        for k in range(1, 8):
            acc = acc + buf[k]
        out_ref[...] = acc

    vmem = pl.BlockSpec(memory_space=pltpu.VMEM)
    return pl.pallas_call(
        body, name="all_reduce_small", in_specs=[vmem], out_specs=vmem,
        out_shape=jax.ShapeDtypeStruct((r, lanes), F32),
        scratch_shapes=[pltpu.VMEM((8, r, lanes), F32), pltpu.SemaphoreType.DMA((7,)), pltpu.SemaphoreType.DMA((7,))],
    )(v)


FLAT_LANES = 1024


def _pack_rows(parts, dtype):
    flat = jnp.concatenate([p.reshape(-1).astype(dtype) for p in parts])
    rows = -(-flat.shape[0] // FLAT_LANES)
    rows = -(-rows // 32) * 32
    flat = jnp.pad(flat, (0, rows * FLAT_LANES - flat.shape[0]))
    return flat.reshape(rows, FLAT_LANES)


def _size(shape):
    n = 1
    for d in shape:
        n *= d
    return n


def _unpack_rows(flat, shapes):
    flat = flat.reshape(-1)
    out, off = [], 0
    for shp in shapes:
        out.append(flat[off:off + _size(shp)].reshape(shp))
        off += _size(shp)
    return out


def _pack_small(parts):
    rows = []
    for p in parts:
        f = p.reshape(-1).astype(F32)
        n = -(-f.shape[0] // HEAD_DIM) * HEAD_DIM
        rows.append(jnp.pad(f, (0, n - f.shape[0])).reshape(-1, HEAD_DIM))
    a = jnp.concatenate(rows, axis=0)
    return jnp.pad(a, ((0, -a.shape[0] % 8), (0, 0)))


def _unpack_small(a, shapes):
    out, row = [], 0
    for shp in shapes:
        nrows = -(-_size(shp) // HEAD_DIM)
        out.append(a[row:row + nrows].reshape(-1)[:_size(shp)].reshape(shp))
        row += nrows
    return out


SMALL = ["attn_norm_w", "a_log", "dt_bias", "delta_out_norm_w", "q_norm_w", "k_norm_w", "attn_out_norm_w", "ffn_norm_w"]
BIG = ["w_in", "w_out", "w_gate_up", "w_down"]
ORDER = ["attn_norm_w", "w_in", "conv_w", "a_log", "dt_bias", "delta_out_norm_w", "q_norm_w", "k_norm_w",
         "attn_out_norm_w", "w_out", "ffn_norm_w", "w_gate_up", "w_down"]


def kernel(x, positions, attn_norm_w, w_in, conv_w, a_log, dt_bias, delta_out_norm_w, q_norm_w, k_norm_w, attn_out_norm_w, w_out, ffn_norm_w, w_gate_up, w_down, loss_target, m_attn_norm_w, m_w_in, m_conv_w, m_a_log, m_dt_bias, m_delta_out_norm_w, m_q_norm_w, m_k_norm_w, m_attn_out_norm_w, m_w_out, m_ffn_norm_w, m_w_gate_up, m_w_down, v_attn_norm_w, v_w_in, v_conv_w, v_a_log, v_dt_bias, v_delta_out_norm_w, v_q_norm_w, v_k_norm_w, v_attn_out_norm_w, v_w_out, v_ffn_norm_w, v_w_gate_up, v_w_down):
    wts = dict(attn_norm_w=attn_norm_w, w_in=w_in, conv_w=conv_w, a_log=a_log, dt_bias=dt_bias,
               delta_out_norm_w=delta_out_norm_w, q_norm_w=q_norm_w, k_norm_w=k_norm_w,
               attn_out_norm_w=attn_out_norm_w, w_out=w_out, ffn_norm_w=ffn_norm_w, w_gate_up=w_gate_up, w_down=w_down)
    mom = dict(attn_norm_w=m_attn_norm_w, w_in=m_w_in, conv_w=m_conv_w, a_log=m_a_log, dt_bias=m_dt_bias,
               delta_out_norm_w=m_delta_out_norm_w, q_norm_w=m_q_norm_w, k_norm_w=m_k_norm_w,
               attn_out_norm_w=m_attn_out_norm_w, w_out=m_w_out, ffn_norm_w=m_ffn_norm_w, w_gate_up=m_w_gate_up,
               w_down=m_w_down)
    var = dict(attn_norm_w=v_attn_norm_w, w_in=v_w_in, conv_w=v_conv_w, a_log=v_a_log, dt_bias=v_dt_bias,
               delta_out_norm_w=v_delta_out_norm_w, q_norm_w=v_q_norm_w, k_norm_w=v_k_norm_w,
               attn_out_norm_w=v_attn_out_norm_w, w_out=v_w_out, ffn_norm_w=v_ffn_norm_w, w_gate_up=v_w_gate_up,
               w_down=v_w_down)
    dmod = x.shape[2]
    heads = dmod // (2 * HEAD_DIM)
    dw = heads * HEAD_DIM
    chip = 2 * lax.axis_index("x") + lax.axis_index("y")
    core = lax.axis_index("c")
    n_in, n_out, n_gu, n_down, n_conv = (w_in.shape[2], w_out.shape[1], w_gate_up.shape[2], w_down.shape[1],
                                         conv_w.shape[2])
    big_shapes = [(dmod, n_in), (n_out, dmod), (dmod, n_gu), (n_down, dmod)]

    conv_bits = lax.bitcast_convert_type(conv_w[0], BF16)
    flat = _pack_rows([w_in[0], w_out[0], w_gate_up[0], w_down[0], conv_bits], BF16)
    gathered = _all_gather_weights(flat)
    per_chip = [_unpack_rows(gathered[i], big_shapes + [(4, n_conv, 2)]) for i in range(4)]
    w_in_f = jnp.concatenate([p[0] for p in per_chip], axis=1)
    w_out_f = jnp.concatenate([p[1] for p in per_chip], axis=0)
    w_gu_f = jnp.concatenate([p[2] for p in per_chip], axis=1)
    w_down_f = jnp.concatenate([p[3] for p in per_chip], axis=0)
    conv_f = jnp.concatenate([lax.bitcast_convert_type(p[4], F32) for p in per_chip], axis=1)
    w_bd = jnp.pad(w_in_f[:, 4 * dw:4 * dw + 2 * heads], ((0, 0), (0, HEAD_DIM - 2 * heads)))
    small = {n: wts[n] for n in SMALL}

    loss_row, grad_x, sg, bg = _local_step(
        x[0], positions[0], loss_target[0], small, w_in_f[:, :4 * dw], w_bd, w_in_f[:, 4 * dw + 2 * heads:],
        w_out_f[:dw], w_out_f[dw:], w_gu_f, w_down_f, conv_f)

    g_in = jnp.concatenate([bg["w_qkvz"], bg["w_bd"][:, :2 * heads], bg["w_attn"]], axis=1)
    g_out = jnp.concatenate([bg["w_out_a"], bg["w_out_b"]], axis=0)
    g_gu = jnp.concatenate([bg["w_gate"], bg["w_up"]], axis=1)
    g_flat = jnp.stack([
        _pack_rows([g_in[:, i * n_in:(i + 1) * n_in], g_out[i * n_out:(i + 1) * n_out],
                    g_gu[:, i * n_gu:(i + 1) * n_gu], bg["w_down"][i * n_down:(i + 1) * n_down]], BF16)
        for i in range(4)])
    half = g_flat.shape[1] // 2
    g_mine = lax.dynamic_slice_in_dim(g_flat, core * half, half, axis=1)
    g_theirs = lax.dynamic_slice_in_dim(g_flat, (1 - core) * half, half, axis=1)
    from_sibling = _swap_with_sibling(g_theirs, "swap_grad_halves")
    chip_sum = _add_bf16(g_mine, from_sibling, "chip_partial_sum")
    from_chips = _scatter_to_chips(chip_sum)
    own = lax.dynamic_index_in_dim(chip_sum, chip, axis=0, keepdims=False)
    total = _join_halves(_sum4_f32(own, from_chips, "grad_total"))
    g_big = dict(zip(BIG, _unpack_rows(total, big_shapes)))

    reduced = _all_reduce_small(_pack_small([sg[n] for n in SMALL] + [sg["conv_w"], loss_row]))
    red = _unpack_small(reduced, [wts[n].shape for n in SMALL] + [(4, 4 * n_conv), (1, HEAD_DIM)])
    g_small = dict(zip(SMALL, red[:len(SMALL)]))
    g_conv_full, loss_out = red[len(SMALL)], red[len(SMALL) + 1]
    g_small["conv_w"] = lax.dynamic_slice_in_dim(g_conv_full, chip * n_conv, n_conv, axis=1).reshape(conv_w.shape)

    grads, deltas, new_m, new_v = {}, {}, {}, {}
    for n in BIG:
        shp = wts[n].shape
        d, nm, nv = _adamw(wts[n][0], g_big[n], mom[n][0], var[n][0], "adamw_" + n)
        grads[n], deltas[n], new_m[n], new_v[n] = g_big[n].reshape(shp), d.reshape(shp), nm.reshape(shp), nv.reshape(shp)
    names = SMALL + ["conv_w"]
    shapes = [wts[n].shape for n in names]
    d, nm, nv = _adamw(_pack_small([wts[n] for n in names]), _pack_small([g_small[n] for n in names]),
                       _pack_small([mom[n] for n in names]), _pack_small([var[n] for n in names]), "adamw_small")
    for n, dd, mm, vv in zip(names, _unpack_small(d, shapes), _unpack_small(nm, shapes), _unpack_small(nv, shapes)):
        grads[n], deltas[n], new_m[n], new_v[n] = g_small[n], dd, mm, vv
    return (loss_out[0, 0], grad_x[None], *[grads[n] for n in ORDER], *[deltas[n] for n in ORDER],
            *[new_m[n] for n in ORDER], *[new_v[n] for n in ORDER])
```

```python
import functools

import jax
import jax.numpy as jnp
from jax import lax
from jax.experimental import pallas as pl
from jax.experimental.pallas import tpu as pltpu

F32 = jnp.float32
BF16 = jnp.bfloat16
HEAD_DIM = 128
CHUNK = 64
SPAN = 128
DILATIONS = (1, 4, 16)
ROPE_THETA = 10000.0
EPS = 1e-6
NEG = -1e30
ADAM_LR, ADAM_B1, ADAM_B2, ADAM_EPS, ADAM_WD, ADAM_STEP = 0.001, 0.9, 0.999, 1e-08, 0.01, 10
VMEM_LIMIT = 48 * 1024 * 1024
MESH = pl.DeviceIdType.MESH

_DN = {"nn": (((1,), (0,)), ((), ())), "nt": (((1,), (1,)), ((), ())), "tn": (((0,), (0,)), ((), ()))}


def _dot(a, b, mode="nn", exact=False):
    if exact:
        return lax.dot_general(a, b, _DN[mode], preferred_element_type=F32, precision=lax.Precision.HIGHEST)
    return lax.dot_general(a.astype(BF16), b.astype(BF16), _DN[mode], preferred_element_type=F32)


def _tile(dim, pref, unit=128):
    t = (min(pref, dim) // unit) * unit
    while t >= unit:
        if dim % t == 0:
            return t
        t -= unit
    return dim


def _params(sem):
    return pltpu.CompilerParams(dimension_semantics=sem, vmem_limit_bytes=VMEM_LIMIT)


def _sigmoid(x):
    return 1.0 / (1.0 + jnp.exp(-x))


def _matmul(a, b, mode, name, add=None, out_dtype=F32, a_cols=None, b_cols=None):
    if mode == "tn":
        out_dtype = BF16
    a_off, a_w = a_cols if a_cols else (0, a.shape[1])
    b_off, b_w = b_cols if b_cols else (0, b.shape[1])
    if mode == "nn":
        m, k, n = a.shape[0], a_w, b_w
        assert b.shape[0] == k
    elif mode == "nt":
        m, k, n = a.shape[0], a_w, b.shape[0]
        assert b_w == k
    else:
        k, m, n = a.shape[0], a_w, b_w
        assert b.shape[0] == k
    tm, tn = _tile(m, 1024, 128), _tile(n, 1024, 128)
    tk = _tile(k, 2048 if a.dtype == BF16 and b.dtype == BF16 else 1024, 128)
    if mode == "nn":
        assert a_off % tk == 0 and b_off % tn == 0
        a_spec = pl.BlockSpec((tm, tk), lambda i, j, kk: (i, kk + a_off // tk))
        b_spec = pl.BlockSpec((tk, tn), lambda i, j, kk: (kk, j + b_off // tn))
    elif mode == "nt":
        assert a_off % tk == 0 and b_off % tk == 0
        a_spec = pl.BlockSpec((tm, tk), lambda i, j, kk: (i, kk + a_off // tk))
        b_spec = pl.BlockSpec((tn, tk), lambda i, j, kk: (j, kk + b_off // tk))
    else:
        assert a_off % tm == 0 and b_off % tn == 0
        a_spec = pl.BlockSpec((tk, tm), lambda i, j, kk: (kk, i + a_off // tm))
        b_spec = pl.BlockSpec((tk, tn), lambda i, j, kk: (kk, j + b_off // tn))
    nk = k // tk
    has_add = add is not None

    def body(*refs):
        if has_add:
            a_ref, b_ref, add_ref, o_ref, acc_ref = refs
        else:
            a_ref, b_ref, o_ref, acc_ref = refs
        kk = pl.program_id(2)

        @pl.when(kk == 0)
        def _():
            acc_ref[...] = jnp.zeros_like(acc_ref)

        acc_ref[...] += _dot(a_ref[...], b_ref[...], mode)

        @pl.when(kk == nk - 1)
        def _():
            r = acc_ref[...]
            if has_add:
                r = r + add_ref[...].astype(F32)
            o_ref[...] = r.astype(out_dtype)

    in_specs = [a_spec, b_spec]
    args = [a, b]
    if has_add:
        in_specs.append(pl.BlockSpec((tm, tn), lambda i, j, kk: (i, j)))
        args.append(add)
    return pl.pallas_call(
        body, name=name, grid=(m // tm, n // tn, nk),
        in_specs=in_specs, out_specs=pl.BlockSpec((tm, tn), lambda i, j, kk: (i, j)),
        out_shape=jax.ShapeDtypeStruct((m, n), out_dtype),
        scratch_shapes=[pltpu.VMEM((tm, tn), F32)],
        compiler_params=_params(("parallel", "parallel", "arbitrary")),
    )(*args)


def _rmsnorm_fwd(x, w, name):
    s, d = x.shape
    tr = _tile(s, 512, 8)

    def body(x_ref, w_ref, h_ref):
        xv = x_ref[...]
        rstd = lax.rsqrt(jnp.mean(xv * xv, axis=-1, keepdims=True) + EPS)
        h_ref[...] = (xv * rstd * w_ref[...]).astype(BF16)

    return pl.pallas_call(
        body, name=name, grid=(s // tr,),
        in_specs=[pl.BlockSpec((tr, d), lambda i: (i, 0)), pl.BlockSpec((1, d), lambda i: (0, 0))],
        out_specs=pl.BlockSpec((tr, d), lambda i: (i, 0)),
        out_shape=jax.ShapeDtypeStruct((s, d), BF16),
        compiler_params=_params(("parallel",)),
    )(x, w)


def _rmsnorm_bwd(dh, x, w, res, name):
    s, d = x.shape
    tr = _tile(s, 256, 8)

    def body(dh_ref, x_ref, w_ref, res_ref, dx_ref, dw_ref):
        xv = x_ref[...]
        rstd = lax.rsqrt(jnp.mean(xv * xv, axis=-1, keepdims=True) + EPS)
        xhat = xv * rstd
        dhv = dh_ref[...]
        gw = dhv * w_ref[...]
        dx_ref[...] = res_ref[...] + rstd * (gw - xhat * jnp.mean(gw * xhat, axis=-1, keepdims=True))

        @pl.when(pl.program_id(0) == 0)
        def _():
            dw_ref[...] = jnp.zeros_like(dw_ref)

        dw_ref[...] += jnp.sum(dhv * xhat, axis=0, keepdims=True)

    row = pl.BlockSpec((tr, d), lambda i: (i, 0))
    vec = pl.BlockSpec((1, d), lambda i: (0, 0))
    return pl.pallas_call(
        body, name=name, grid=(s // tr,),
        in_specs=[row, row, vec, row], out_specs=[row, vec],
        out_shape=[jax.ShapeDtypeStruct((s, d), F32), jax.ShapeDtypeStruct((1, d), F32)],
        compiler_params=_params(("arbitrary",)),
    )(dh, x, w, res)


def _conv_taps(x, w, rows):
    shifted = [x]
    for sft in (1, 2, 3):
        shifted.append(jnp.where(rows >= sft, pltpu.roll(x, sft, 0), 0.0))
    y = w[3:4, :] * shifted[0] + w[2:3, :] * shifted[1] + w[1:2, :] * shifted[2] + w[0:1, :] * shifted[3]
    return y, shifted


def _delta_pre_fwd(qkvz, conv_w, heads):
    s = qkvz.shape[0]
    nblk = 3 * heads

    def body(x_ref, w_ref, o_ref):
        part = pl.program_id(0) // heads
        rows = lax.broadcasted_iota(jnp.int32, (s, HEAD_DIM), 0)
        y, _ = _conv_taps(x_ref[...], w_ref[...], rows)
        a = y * _sigmoid(y)
        rs = lax.rsqrt(jnp.sum(a * a, axis=-1, keepdims=True) + EPS)
        fac = jnp.where(part == 0, rs * (HEAD_DIM ** -0.5), jnp.where(part == 1, rs, 1.0))
        o_ref[...] = a * fac

    return pl.pallas_call(
        body, name="delta_pre_fwd", grid=(nblk,),
        in_specs=[pl.BlockSpec((s, HEAD_DIM), lambda i: (0, i)), pl.BlockSpec((4, HEAD_DIM), lambda i: (0, i))],
        out_specs=pl.BlockSpec((s, HEAD_DIM), lambda i: (0, i)),
        out_shape=jax.ShapeDtypeStruct((s, 3 * heads * HEAD_DIM), F32),
        compiler_params=_params(("parallel",)),
    )(qkvz, conv_w)


def _delta_pre_bwd(dqkv, qkvz, conv_w, heads):
    s = qkvz.shape[0]
    nblk = 3 * heads

    def body(d_ref, x_ref, w_ref, dx_ref, dw_ref):
        part = pl.program_id(0) // heads
        rows = lax.broadcasted_iota(jnp.int32, (s, HEAD_DIM), 0)
        w = w_ref[...]
        y, shifted = _conv_taps(x_ref[...], w, rows)
        sg = _sigmoid(y)
        a = y * sg
        rs = lax.rsqrt(jnp.sum(a * a, axis=-1, keepdims=True) + EPS)
        unit = a * rs
        dn = d_ref[...]
        scale = jnp.where(part == 0, HEAD_DIM ** -0.5, 1.0)
        da_norm = scale * rs * (dn - unit * jnp.sum(dn * unit, axis=-1, keepdims=True))
        da = jnp.where(part < 2, da_norm, dn)
        dy = da * sg * (1.0 + y * (1.0 - sg))
        dx = w[3:4, :] * dy
        for sft in (1, 2, 3):
            dx = dx + w[3 - sft:4 - sft, :] * jnp.where(rows < s - sft, pltpu.roll(dy, s - sft, 0), 0.0)
        dx_ref[...] = dx
        for sft in range(4):
            dw_ref[3 - sft:4 - sft, :] = jnp.sum(dy * shifted[sft], axis=0, keepdims=True)

    col = pl.BlockSpec((s, HEAD_DIM), lambda i: (0, i))
    wsp = pl.BlockSpec((4, HEAD_DIM), lambda i: (0, i))
    return pl.pallas_call(
        body, name="delta_pre_bwd", grid=(nblk,),
        in_specs=[col, col, wsp], out_specs=[col, wsp],
        out_shape=[jax.ShapeDtypeStruct((s, 3 * heads * HEAD_DIM), F32),
                   jax.ShapeDtypeStruct((4, 3 * heads * HEAD_DIM), F32)],
        compiler_params=_params(("parallel",)),
    )(dqkv, qkvz, conv_w)


def _chunk_common(q, k, v, bd, a_log_h, dt_bias_h, h, heads):
    c = CHUNK
    lane = lax.broadcasted_iota(jnp.int32, bd.shape, 1)
    braw = jnp.sum(jnp.where(lane == h, bd, 0.0), axis=1, keepdims=True)
    draw = jnp.sum(jnp.where(lane == h + heads, bd, 0.0), axis=1, keepdims=True)
    beta = _sigmoid(braw)
    xd = draw + dt_bias_h
    sp = jnp.maximum(xd, 0.0) + jnp.log1p(jnp.exp(-jnp.abs(xd)))
    g = -jnp.exp(a_log_h) * sp
    row = lax.broadcasted_iota(jnp.int32, (c, c), 0)
    col = lax.broadcasted_iota(jnp.int32, (c, c), 1)
    g_b = jnp.broadcast_to(g, (c, c))
    g_row = jnp.sum(jnp.where(row == col, g_b, 0.0), axis=0, keepdims=True)
    gam_col = jnp.sum(jnp.where(col <= row, jnp.broadcast_to(g_row, (c, c)), 0.0), axis=1, keepdims=True)
    gam_row = jnp.sum(jnp.where(row <= col, g_b, 0.0), axis=0, keepdims=True)
    causal = row >= col
    dm = jnp.where(causal, jnp.exp(jnp.where(causal, gam_col - gam_row, 0.0)), 0.0)
    kk = _dot(k, k, "nt")
    low = jnp.where(row > col, beta * kk * dm, 0.0)
    eye = jnp.where(row == col, 1.0, 0.0)
    t = eye - low
    pw = low
    for _ in range(5):
        pw = _dot(pw, pw, exact=True)
        t = t + _dot(t, pw, exact=True)
    e = jnp.exp(gam_col)
    u = _dot(t, beta * v)
    w = _dot(t, (beta * e) * k)
    qk_raw = _dot(q, k, "nt")
    gl = jnp.sum(g, axis=0, keepdims=True)
    el = jnp.exp(gl - gam_col)
    return dict(beta=beta, xd=xd, g=g, row=row, col=col, dm=dm, kk=kk, t=t, e=e, u=u, w=w,
                qk_raw=qk_raw, qk=qk_raw * dm, gl=gl, el=el, qd=e * q, kd=el * k, cd=jnp.exp(gl))


def _delta_chunk_fwd(qkv, bd, a_log, dt_bias, heads):
    s = qkv.shape[0]
    n = s // CHUNK
    dw = heads * HEAD_DIM
    blk = lambda part: pl.BlockSpec((CHUNK, dw), lambda i: (i, part))

    def body(q_ref, k_ref, v_ref, bd_ref, al_ref, dt_ref, o_ref, st_ref, state):
        @pl.when(pl.program_id(0) == 0)
        def _():
            state[...] = jnp.zeros_like(state)

        bd = bd_ref[...]
        for h in range(heads):
            hs = slice(h * HEAD_DIM, (h + 1) * HEAD_DIM)
            cm = _chunk_common(q_ref[:, hs], k_ref[:, hs], v_ref[:, hs], bd, al_ref[h], dt_ref[h], h, heads)
            st = state[h]
            st_ref[0, h] = st
            vn = cm["u"] - _dot(cm["w"], st)
            o_ref[:, hs] = _dot(cm["qd"], st) + _dot(cm["qk"], vn)
            state[h] = cm["cd"] * st + _dot(cm["kd"], vn, "tn")

    smem = pl.BlockSpec(memory_space=pltpu.SMEM)
    return pl.pallas_call(
        body, name="delta_chunk_fwd", grid=(n,),
        in_specs=[blk(0), blk(1), blk(2), pl.BlockSpec((CHUNK, HEAD_DIM), lambda i: (i, 0)), smem, smem],
        out_specs=[pl.BlockSpec((CHUNK, dw), lambda i: (i, 0)),
                   pl.BlockSpec((1, heads, HEAD_DIM, HEAD_DIM), lambda i: (i, 0, 0, 0))],
        out_shape=[jax.ShapeDtypeStruct((s, dw), F32),
                   jax.ShapeDtypeStruct((n, heads, HEAD_DIM, HEAD_DIM), F32)],
        scratch_shapes=[pltpu.VMEM((heads, HEAD_DIM, HEAD_DIM), F32)],
        compiler_params=_params(("arbitrary",)),
    )(qkv, qkv, qkv, bd, a_log, dt_bias)


def _delta_chunk_bwd(do, qkv, bd, states, a_log, dt_bias, heads):
    s = qkv.shape[0]
    n = s // CHUNK
    dw = heads * HEAD_DIM
    c = CHUNK
    blk = lambda part: pl.BlockSpec((CHUNK, dw), lambda i: (n - 1 - i, part))

    def one_head(h, q, k, v, dov, st, dsn, bd, a_log_h, dt_bias_h):
        cm = _chunk_common(q, k, v, bd, a_log_h, dt_bias_h, h, heads)
        beta, e, dm, row, col = cm["beta"], cm["e"], cm["dm"], cm["row"], cm["col"]
        vn = cm["u"] - _dot(cm["w"], st)
        dvn = _dot(cm["kd"], dsn)
        dkd = _dot(vn, dsn, "nt")
        dcd = jnp.sum(jnp.sum(st * dsn, axis=1, keepdims=True), axis=0, keepdims=True)
        ds = cm["cd"] * dsn
        dqd = _dot(dov, st, "nt")
        ds = ds + _dot(cm["qd"], dov, "tn")
        dqk = _dot(dov, vn, "nt")
        dvn = dvn + _dot(cm["qk"], dov, "tn")
        dw_ = -_dot(dvn, st, "nt")
        ds = ds - _dot(cm["w"], dvn, "tn")
        drhs_u = _dot(cm["t"], dvn, "tn")
        drhs_w = _dot(cm["t"], dw_, "tn")
        da = -(_dot(drhs_u, cm["u"], "nt") + _dot(drhs_w, cm["w"], "nt"))
        dl = jnp.where(row > col, da, 0.0)
        dbeta = jnp.sum(dl * cm["kk"] * dm, axis=1, keepdims=True)
        dkk = dl * beta * dm
        dd = dl * beta * cm["kk"]
        dv = beta * drhs_u
        ek = e * k
        dbeta = dbeta + jnp.sum(drhs_u * v, axis=1, keepdims=True) + jnp.sum(drhs_w * ek, axis=1, keepdims=True)
        dk = (beta * e) * drhs_w
        dgam = jnp.sum(drhs_w * (beta * ek), axis=1, keepdims=True)
        dqkm = dqk * dm
        dq = _dot(dqkm, k)
        dk = dk + _dot(dqkm, q, "tn")
        dd = dd + dqk * cm["qk_raw"]
        dk = dk + _dot(dkk, k) + _dot(dkk, k, "tn")
        dq = dq + e * dqd
        dgam = dgam + jnp.sum(dqd * cm["qd"], axis=1, keepdims=True)
        dk = dk + cm["el"] * dkd
        r = jnp.sum(dkd * cm["kd"], axis=1, keepdims=True)
        dgam = dgam - r
        dgl = jnp.sum(r, axis=0, keepdims=True) + dcd * cm["cd"]
        mm = dd * dm
        colsum = jnp.sum(mm, axis=0, keepdims=True)
        colsum_c = jnp.sum(jnp.where(row == col, jnp.broadcast_to(colsum, (c, c)), 0.0), axis=1, keepdims=True)
        dgam = dgam + jnp.sum(mm, axis=1, keepdims=True) - colsum_c
        ridx = lax.broadcasted_iota(jnp.int32, (c, 1), 0)
        dgam = dgam + jnp.where(ridx == c - 1, dgl, 0.0)
        dgam_row = jnp.sum(jnp.where(row == col, jnp.broadcast_to(dgam, (c, c)), 0.0), axis=0, keepdims=True)
        dg = jnp.sum(jnp.where(col >= row, jnp.broadcast_to(dgam_row, (c, c)), 0.0), axis=1, keepdims=True)
        d_xd = dg * (-jnp.exp(a_log_h)) * _sigmoid(cm["xd"])
        d_braw = dbeta * beta * (1.0 - beta)
        lane = lax.broadcasted_iota(jnp.int32, (c, HEAD_DIM), 1)
        contrib = (jnp.where(lane == h, d_braw, 0.0) + jnp.where(lane == h + heads, d_xd, 0.0)
                   + jnp.where(lane == h + 2 * heads, dg * cm["g"], 0.0))
        return dq, dk, dv, ds, contrib

    def body(do_ref, q_ref, k_ref, v_ref, bd_ref, st_ref, al_ref, dt_ref, dq_ref, dk_ref, dv_ref, dbd_ref, dstate):
        @pl.when(pl.program_id(0) == 0)
        def _():
            dstate[...] = jnp.zeros_like(dstate)

        bd = bd_ref[...]
        total = jnp.zeros((c, HEAD_DIM), F32)
        for h in range(heads):
            hs = slice(h * HEAD_DIM, (h + 1) * HEAD_DIM)
            dq, dk, dv, ds, contrib = one_head(h, q_ref[:, hs], k_ref[:, hs], v_ref[:, hs], do_ref[:, hs],
                                               st_ref[0, h], dstate[h], bd, al_ref[h], dt_ref[h])
            dq_ref[:, hs] = dq
            dk_ref[:, hs] = dk
            dv_ref[:, hs] = dv
            dstate[h] = ds
            total = total + contrib
        dbd_ref[...] = total

    smem = pl.BlockSpec(memory_space=pltpu.SMEM)
    shared = pl.BlockSpec((CHUNK, HEAD_DIM), lambda i: (n - 1 - i, 0))
    wide = pl.BlockSpec((CHUNK, dw), lambda i: (n - 1 - i, 0))
    dq, dk, dv, dbd = pl.pallas_call(
        body, name="delta_chunk_bwd", grid=(n,),
        in_specs=[wide, blk(0), blk(1), blk(2), shared,
                  pl.BlockSpec((1, heads, HEAD_DIM, HEAD_DIM), lambda i: (n - 1 - i, 0, 0, 0)), smem, smem],
        out_specs=[wide, wide, wide, shared],
        out_shape=[jax.ShapeDtypeStruct((s, dw), F32)] * 3 + [jax.ShapeDtypeStruct((s, HEAD_DIM), F32)],
        scratch_shapes=[pltpu.VMEM((heads, HEAD_DIM, HEAD_DIM), F32)],
        compiler_params=_params(("arbitrary",)),
    )(do, qkv, qkv, qkv, bd, states, a_log, dt_bias)
    return dq, dk, dv, dbd


def _delta_post_fwd(o, qkvz, w, heads):
    s = o.shape[0]
    tr = _tile(s, 1024, 8)

    def body(o_ref, z_ref, w_ref, out_ref):
        ov, z = o_ref[...], z_ref[...]
        rstd = lax.rsqrt(jnp.mean(ov * ov, axis=-1, keepdims=True) + EPS)
        out_ref[...] = (ov * rstd * w_ref[...] * (z * _sigmoid(z))).astype(BF16)

    return pl.pallas_call(
        body, name="delta_post_fwd", grid=(s // tr, heads),
        in_specs=[pl.BlockSpec((tr, HEAD_DIM), lambda i, h: (i, h)),
                  pl.BlockSpec((tr, HEAD_DIM), lambda i, h: (i, 3 * heads + h)),
                  pl.BlockSpec((1, HEAD_DIM), lambda i, h: (0, 0))],
        out_specs=pl.BlockSpec((tr, HEAD_DIM), lambda i, h: (i, h)),
        out_shape=jax.ShapeDtypeStruct((s, heads * HEAD_DIM), BF16),
        compiler_params=_params(("parallel", "parallel")),
    )(o, qkvz, w)


def _delta_post_bwd(dmix, o, qkvz, w, heads):
    s = o.shape[0]
    tr = _tile(s, 1024, 8)

    def body(d_ref, o_ref, z_ref, w_ref, do_ref, dz_ref, dw_ref):
        d, ov, z, wv = d_ref[...], o_ref[...], z_ref[...], w_ref[...]
        sg = _sigmoid(z)
        rstd = lax.rsqrt(jnp.mean(ov * ov, axis=-1, keepdims=True) + EPS)
        ohat = ov * rstd
        dz_ref[...] = d * (ohat * wv) * sg * (1.0 + z * (1.0 - sg))
        dn = d * (z * sg)
        gw = dn * wv
        do_ref[...] = rstd * (gw - ohat * jnp.mean(gw * ohat, axis=-1, keepdims=True))

        @pl.when((pl.program_id(0) == 0) & (pl.program_id(1) == 0))
        def _():
            dw_ref[...] = jnp.zeros_like(dw_ref)

        dw_ref[...] += jnp.sum(dn * ohat, axis=0, keepdims=True)

    head = pl.BlockSpec((tr, HEAD_DIM), lambda i, h: (i, h))
    vec = pl.BlockSpec((1, HEAD_DIM), lambda i, h: (0, 0))
    dw = heads * HEAD_DIM
    return pl.pallas_call(
        body, name="delta_post_bwd", grid=(s // tr, heads),
        in_specs=[head, head, pl.BlockSpec((tr, HEAD_DIM), lambda i, h: (i, 3 * heads + h)), vec],
        out_specs=[head, head, vec],
        out_shape=[jax.ShapeDtypeStruct((s, dw), F32), jax.ShapeDtypeStruct((s, dw), F32),
                   jax.ShapeDtypeStruct((1, HEAD_DIM), F32)],
        compiler_params=_params(("arbitrary", "arbitrary")),
    )(dmix, o, qkvz, w)


def _rope_tables(positions, s):
    half = HEAD_DIM // 2
    inv_freq = ROPE_THETA ** (-jnp.arange(half, dtype=F32) / half)
    ang = positions.reshape(s, 1).astype(F32) * inv_freq
    cos, sin = jnp.cos(ang), jnp.sin(ang)
    return jnp.concatenate([cos, cos], axis=-1), jnp.concatenate([-sin, sin], axis=-1)


def _attn_pre_fwd(aqkv, wq, wk, cosf, sinf, heads):
    s = aqkv.shape[0]
    tr = _tile(s, 1024, 8)

    def body(x_ref, wq_ref, wk_ref, c_ref, s_ref, o_ref):
        xv = x_ref[...]
        wv = jnp.where(pl.program_id(1) < heads, wq_ref[...], wk_ref[...])
        y = xv * lax.rsqrt(jnp.mean(xv * xv, axis=-1, keepdims=True) + EPS) * wv
        o_ref[...] = y * c_ref[...] + pltpu.roll(y, HEAD_DIM // 2, 1) * s_ref[...]

    blk = pl.BlockSpec((tr, HEAD_DIM), lambda i, j: (i, j))
    vec = pl.BlockSpec((1, HEAD_DIM), lambda i, j: (0, 0))
    tab = pl.BlockSpec((tr, HEAD_DIM), lambda i, j: (i, 0))
    return pl.pallas_call(
        body, name="attn_pre_fwd", grid=(s // tr, 2 * heads),
        in_specs=[blk, vec, vec, tab, tab], out_specs=blk,
        out_shape=jax.ShapeDtypeStruct((s, 2 * heads * HEAD_DIM), F32),
        compiler_params=_params(("parallel", "parallel")),
    )(aqkv, wq, wk, cosf, sinf)


def _attn_mask(j):
    qi = lax.broadcasted_iota(jnp.int32, (SPAN, 2 * SPAN), 0)
    ki = lax.broadcasted_iota(jnp.int32, (SPAN, 2 * SPAN), 1)
    dist = qi + SPAN - ki
    return (dist >= 0) & (dist <= SPAN) & ((j > 0) | (ki >= SPAN))


def _attn_fwd(qk, v_src, v_col0, heads, d):
    s = qk.shape[0]
    u = s // d
    nb = u // SPAN
    aw = heads * HEAD_DIM
    qk_v = qk.reshape(u, d * qk.shape[1])
    v_v = v_src.reshape(u, d * v_src.shape[1])
    qw, vw = qk.shape[1] // HEAD_DIM, v_src.shape[1] // HEAD_DIM

    def body(q_ref, kp_ref, kc_ref, vp_ref, vc_ref, o_ref, l_ref):
        j = pl.program_id(2)
        kcat = jnp.concatenate([kp_ref[...], kc_ref[...]], axis=0)
        vcat = jnp.concatenate([vp_ref[...], vc_ref[...]], axis=0)
        sc = _dot(q_ref[...], kcat, "nt") * (HEAD_DIM ** -0.5)
        sc = jnp.where(_attn_mask(j), sc, NEG)
        m = jnp.max(sc, axis=-1, keepdims=True)
        p = jnp.exp(sc - m)
        den = jnp.sum(p, axis=-1, keepdims=True)
        o_ref[...] = _dot(p, vcat) / den
        l_ref[...] = jnp.broadcast_to(m + jnp.log(den), (SPAN, HEAD_DIM))

    cur = lambda off, w: pl.BlockSpec((SPAN, HEAD_DIM), lambda h, r, j: (j, r * w + off + h))
    prev = lambda off, w: pl.BlockSpec((SPAN, HEAD_DIM), lambda h, r, j: (jnp.maximum(j - 1, 0), r * w + off + h))
    out = pl.BlockSpec((SPAN, HEAD_DIM), lambda h, r, j: (j, r * heads + h))
    o, lse = pl.pallas_call(
        body, name=f"attn_fwd_d{d}", grid=(heads, d, nb),
        in_specs=[cur(0, qw), prev(heads, qw), cur(heads, qw), prev(v_col0, vw), cur(v_col0, vw)],
        out_specs=[out, out],
        out_shape=[jax.ShapeDtypeStruct((u, d * aw), F32)] * 2,
        compiler_params=_params(("parallel", "parallel", "parallel")),
    )(qk_v, qk_v, qk_v, v_v, v_v)
    return o.reshape(s, aw), lse.reshape(s, aw)


def _attn_merge_fwd(os_, ls_, w, heads):
    s = os_[0].shape[0]
    tr = _tile(s, 512, 8)

    def body(o1, o2, o3, l1, l2, l3, w_ref, mix_ref, ob_ref, lse_ref):
        la, lb, lc = l1[...], l2[...], l3[...]
        m = jnp.maximum(jnp.maximum(la, lb), lc)
        ea, eb, ec = jnp.exp(la - m), jnp.exp(lb - m), jnp.exp(lc - m)
        tot = ea + eb + ec
        ob = (ea * o1[...] + eb * o2[...] + ec * o3[...]) / tot
        ob_ref[...] = ob
        lse_ref[...] = m + jnp.log(tot)
        rstd = lax.rsqrt(jnp.mean(ob * ob, axis=-1, keepdims=True) + EPS)
        mix_ref[...] = (ob * rstd * w_ref[...]).astype(BF16)

    blk = pl.BlockSpec((tr, HEAD_DIM), lambda i, h: (i, h))
    vec = pl.BlockSpec((1, HEAD_DIM), lambda i, h: (0, 0))
    aw = heads * HEAD_DIM
    return pl.pallas_call(
        body, name="attn_merge_fwd", grid=(s // tr, heads),
        in_specs=[blk] * 6 + [vec], out_specs=[blk, blk, blk],
        out_shape=[jax.ShapeDtypeStruct((s, aw), BF16), jax.ShapeDtypeStruct((s, aw), F32),
                   jax.ShapeDtypeStruct((s, aw), F32)],
        compiler_params=_params(("parallel", "parallel")),
    )(*os_, *ls_, w)


def _attn_merge_bwd(dmix, ob, w, heads):
    s = ob.shape[0]
    tr = _tile(s, 1024, 8)

    def body(d_ref, ob_ref, w_ref, do_ref, dsum_ref, dw_ref):
        d, ov = d_ref[...], ob_ref[...]
        rstd = lax.rsqrt(jnp.mean(ov * ov, axis=-1, keepdims=True) + EPS)
        ohat = ov * rstd
        gw = d * w_ref[...]
        dov = rstd * (gw - ohat * jnp.mean(gw * ohat, axis=-1, keepdims=True))
        do_ref[...] = dov
        dsum_ref[...] = jnp.broadcast_to(jnp.sum(dov * ov, axis=-1, keepdims=True), dov.shape)

        @pl.when((pl.program_id(0) == 0) & (pl.program_id(1) == 0))
        def _():
            dw_ref[...] = jnp.zeros_like(dw_ref)

        dw_ref[...] += jnp.sum(d * ohat, axis=0, keepdims=True)

    blk = pl.BlockSpec((tr, HEAD_DIM), lambda i, h: (i, h))
    vec = pl.BlockSpec((1, HEAD_DIM), lambda i, h: (0, 0))
    aw = heads * HEAD_DIM
    return pl.pallas_call(
        body, name="attn_merge_bwd", grid=(s // tr, heads),
        in_specs=[pl.BlockSpec((tr, HEAD_DIM), lambda i, h: (i, heads + h)), blk, vec],
        out_specs=[blk, blk, vec],
        out_shape=[jax.ShapeDtypeStruct((s, aw), F32), jax.ShapeDtypeStruct((s, aw), F32),
                   jax.ShapeDtypeStruct((1, HEAD_DIM), F32)],
        compiler_params=_params(("arbitrary", "arbitrary")),
    )(dmix, ob, w)


def _attn_bwd(qk, v_src, v_col0, do, lse, dsum, heads, d):
    s = qk.shape[0]
    u = s // d
    nb = u // SPAN
    aw = heads * HEAD_DIM
    view = lambda a: a.reshape(u, d * a.shape[1])
    qw, vw = qk.shape[1] // HEAD_DIM, v_src.shape[1] // HEAD_DIM

    def body(q_ref, kp_ref, kc_ref, vp_ref, vc_ref, do_ref, l_ref, ds_ref, dq_ref, dka_ref, dkb_ref, dva_ref, dvb_ref):
        j = pl.program_id(2)
        q = q_ref[...]
        kcat = jnp.concatenate([kp_ref[...], kc_ref[...]], axis=0)
        vcat = jnp.concatenate([vp_ref[...], vc_ref[...]], axis=0)
        dov = do_ref[...]
        sc = _dot(q, kcat, "nt") * (HEAD_DIM ** -0.5)
        p = jnp.where(_attn_mask(j), jnp.exp(sc - l_ref[:, 0:1]), 0.0)
        dp = _dot(dov, vcat, "nt")
        dsc = p * (dp - ds_ref[:, 0:1]) * (HEAD_DIM ** -0.5)
        dq_ref[...] = _dot(dsc, kcat)
        dk = _dot(dsc, q, "tn")
        dv = _dot(p, dov, "tn")
        dka_ref[...] = dk[:SPAN]
        dkb_ref[...] = dk[SPAN:]
        dva_ref[...] = dv[:SPAN]
        dvb_ref[...] = dv[SPAN:]

    cur = lambda off, w: pl.BlockSpec((SPAN, HEAD_DIM), lambda h, r, j: (j, r * w + off + h))
    prev = lambda off, w: pl.BlockSpec((SPAN, HEAD_DIM), lambda h, r, j: (jnp.maximum(j - 1, 0), r * w + off + h))
    out = pl.BlockSpec((SPAN, HEAD_DIM), lambda h, r, j: (j, r * heads + h))
    res = pl.pallas_call(
        body, name=f"attn_bwd_d{d}", grid=(heads, d, nb),
        in_specs=[cur(0, qw), prev(heads, qw), cur(heads, qw), prev(v_col0, vw), cur(v_col0, vw), out, out, out],
        out_specs=[out] * 5,
        out_shape=[jax.ShapeDtypeStruct((u, d * aw), F32)] * 5,
        compiler_params=_params(("parallel", "parallel", "parallel")),
    )(view(qk), view(qk), view(qk), view(v_src), view(v_src), view(do), view(lse), view(dsum))
    return [r.reshape(s, aw) for r in res]


def _attn_pre_bwd(grads, aqkv, wq, wk, cosf, sinf, heads):
    s = aqkv.shape[0]
    tr = SPAN
    nrow = s // tr
    flat = []
    for g in grads:
        flat.extend(g)

    def body(*refs):
        g_refs, (x_ref, wq_ref, wk_ref, c_ref, s_ref, dx_ref, dwq_ref, dwk_ref) = refs[:15], refs[15:]
        i, j = pl.program_id(0), pl.program_id(1)
        kind = j // heads
        tot_q = jnp.zeros((tr, HEAD_DIM), F32)
        tot_k = jnp.zeros((tr, HEAD_DIM), F32)
        tot_v = jnp.zeros((tr, HEAD_DIM), F32)
        for p_i, d in enumerate(DILATIONS):
            dq, dka, dkb, dva, dvb = g_refs[5 * p_i:5 * p_i + 5]
            live = i + d < nrow
            tot_q = tot_q + dq[...]
            tot_k = tot_k + dkb[...] + jnp.where(live, dka[...], 0.0)
            tot_v = tot_v + dvb[...] + jnp.where(live, dva[...], 0.0)
        dout = jnp.where(kind == 0, tot_q, tot_k)
        dy = dout * c_ref[...] + pltpu.roll(dout * s_ref[...], HEAD_DIM // 2, 1)
        xv = x_ref[...]
        wv = jnp.where(kind == 0, wq_ref[...], wk_ref[...])
        rstd = lax.rsqrt(jnp.mean(xv * xv, axis=-1, keepdims=True) + EPS)
        xhat = xv * rstd
        gw = dy * wv
        dxn = rstd * (gw - xhat * jnp.mean(gw * xhat, axis=-1, keepdims=True))
        dx_ref[...] = jnp.where(kind == 2, tot_v, dxn)
        dwc = jnp.sum(dy * xhat, axis=0, keepdims=True)

        @pl.when((i == 0) & (j == 0))
        def _():
            dwq_ref[...] = jnp.zeros_like(dwq_ref)
            dwk_ref[...] = jnp.zeros_like(dwk_ref)

        @pl.when(kind == 0)
        def _():
            dwq_ref[...] += dwc

        @pl.when(kind == 1)
        def _():
            dwk_ref[...] += dwc

    in_specs = []
    for d in DILATIONS:
        same = pl.BlockSpec((tr, HEAD_DIM), lambda i, j: (i, j % heads))
        later = pl.BlockSpec((tr, HEAD_DIM), lambda i, j, d=d: (jnp.minimum(i + d, nrow - 1), j % heads))
        in_specs += [same, later, same, later, same]
    blk = pl.BlockSpec((tr, HEAD_DIM), lambda i, j: (i, j))
    vec = pl.BlockSpec((1, HEAD_DIM), lambda i, j: (0, 0))
    tab = pl.BlockSpec((tr, HEAD_DIM), lambda i, j: (i, 0))
    in_specs += [blk, vec, vec, tab, tab]
    return pl.pallas_call(
        body, name="attn_pre_bwd", grid=(nrow, 3 * heads),
        in_specs=in_specs, out_specs=[blk, vec, vec],
        out_shape=[jax.ShapeDtypeStruct((s, 3 * heads * HEAD_DIM), F32),
                   jax.ShapeDtypeStruct((1, HEAD_DIM), F32), jax.ShapeDtypeStruct((1, HEAD_DIM), F32)],
        compiler_params=_params(("arbitrary", "arbitrary")),
    )(*flat, aqkv, wq, wk, cosf, sinf)


def _swiglu_fwd(gu, ff):
    s = gu.shape[0]
    tr, tc = _tile(s, 512, 8), _tile(ff, 1024, 128)
    nc = ff // tc

    def body(g_ref, u_ref, o_ref):
        g = g_ref[...]
        o_ref[...] = (g * _sigmoid(g) * u_ref[...]).astype(BF16)

    return pl.pallas_call(
        body, name="swiglu_fwd", grid=(s // tr, nc),
        in_specs=[pl.BlockSpec((tr, tc), lambda i, j: (i, j)), pl.BlockSpec((tr, tc), lambda i, j: (i, j + nc))],
        out_specs=pl.BlockSpec((tr, tc), lambda i, j: (i, j)),
        out_shape=jax.ShapeDtypeStruct((s, ff), BF16),
        compiler_params=_params(("parallel", "parallel")),
    )(gu, gu)


def _swiglu_bwd(dact, gu, ff):
    s = gu.shape[0]
    tr, tc = _tile(s, 512, 8), _tile(ff, 1024, 128)
    nc = ff // tc

    def body(d_ref, g_ref, u_ref, dg_ref, du_ref):
        d, g = d_ref[...], g_ref[...]
        sg = _sigmoid(g)
        dg_ref[...] = (d * u_ref[...] * sg * (1.0 + g * (1.0 - sg))).astype(BF16)
        du_ref[...] = (d * g * sg).astype(BF16)

    lo = pl.BlockSpec((tr, tc), lambda i, j: (i, j))
    hi = pl.BlockSpec((tr, tc), lambda i, j: (i, j + nc))
    dg, du = pl.pallas_call(
        body, name="swiglu_bwd", grid=(s // tr, nc),
        in_specs=[lo, lo, hi], out_specs=[lo, lo],
        out_shape=[jax.ShapeDtypeStruct((s, ff), BF16)] * 2,
        compiler_params=_params(("parallel", "parallel")),
    )(dact, gu, gu)
    return dg, du


def _loss_head(y, target):
    s, d = y.shape
    tr = _tile(s, 512, 8)

    def body(y_ref, t_ref, dy_ref, l_ref):
        err = y_ref[...] - t_ref[...]
        dy_ref[...] = err * (1.0 / d)
        part = 0.5 * jnp.sum(jnp.sum(err * err, axis=-1, keepdims=True) * (1.0 / d), axis=0, keepdims=True)

        @pl.when(pl.program_id(0) == 0)
        def _():
            l_ref[...] = jnp.zeros_like(l_ref)

        lane = lax.broadcasted_iota(jnp.int32, (1, HEAD_DIM), 1)
        l_ref[...] += jnp.where(lane == 0, part, 0.0)

    row = pl.BlockSpec((tr, d), lambda i: (i, 0))
    return pl.pallas_call(
        body, name="loss_head", grid=(s // tr,),
        in_specs=[row, row], out_specs=[row, pl.BlockSpec((1, HEAD_DIM), lambda i: (0, 0))],
        out_shape=[jax.ShapeDtypeStruct((s, d), F32), jax.ShapeDtypeStruct((1, HEAD_DIM), F32)],
        compiler_params=_params(("arbitrary",)),
    )(y, target)


def _colsum(a, name):
    s, d = a.shape
    tr = _tile(s, 1024, 8)

    def body(a_ref, o_ref):
        @pl.when(pl.program_id(0) == 0)
        def _():
            o_ref[...] = jnp.zeros_like(o_ref)

        o_ref[...] += jnp.sum(a_ref[...], axis=0, keepdims=True)

    return pl.pallas_call(
        body, name=name, grid=(s // tr,),
        in_specs=[pl.BlockSpec((tr, d), lambda i: (i, 0))], out_specs=pl.BlockSpec((1, d), lambda i: (0, 0)),
        out_shape=jax.ShapeDtypeStruct((1, d), F32),
        compiler_params=_params(("arbitrary",)),
    )(a)


def _local_step(x, positions, target, small, w_qkvz, w_bd, w_attn, w_out_a, w_out_b, w_gu, w_down, conv_w):
    s, dmod = x.shape
    heads = dmod // (2 * HEAD_DIM)
    dw = heads * HEAD_DIM
    ff = w_down.shape[0]
    a_log, dt_bias = small["a_log"].reshape(-1), small["dt_bias"].reshape(-1)
    cosf, sinf = _rope_tables(positions, s)

    h1 = _rmsnorm_fwd(x, small["attn_norm_w"], "norm1_fwd")
    qkvz = _matmul(h1, w_qkvz, "nn", "proj_qkvz")
    bd = _matmul(h1, w_bd, "nn", "proj_bd")
    aqkv = _matmul(h1, w_attn, "nn", "proj_attn")
    dqkv = _delta_pre_fwd(qkvz, conv_w, heads)
    o_d, states = _delta_chunk_fwd(dqkv, bd, a_log, dt_bias, heads)
    mix_a = _delta_post_fwd(o_d, qkvz, small["delta_out_norm_w"], heads)
    qk_rot = _attn_pre_fwd(aqkv, small["q_norm_w"], small["k_norm_w"], cosf, sinf, heads)
    outs = [_attn_fwd(qk_rot, aqkv, 2 * heads, heads, d) for d in DILATIONS]
    mix_b, ob, lse = _attn_merge_fwd([o for o, _ in outs], [l for _, l in outs], small["attn_out_norm_w"], heads)
    x1 = _matmul(mix_a, w_out_a, "nn", "out_proj_a", add=x)
    x1 = _matmul(mix_b, w_out_b, "nn", "out_proj_b", add=x1)
    h2 = _rmsnorm_fwd(x1, small["ffn_norm_w"], "norm2_fwd")
    gu = _matmul(h2, w_gu, "nn", "ffn_gate_up")
    act = _swiglu_fwd(gu, ff)
    y = _matmul(act, w_down, "nn", "ffn_down", add=x1)
    dy, loss_row = _loss_head(y, target)

    dact = _matmul(dy, w_down, "nt", "ffn_down_dx")
    g_w_down = _matmul(act, dy, "tn", "ffn_down_dw")
    dgate, dup = _swiglu_bwd(dact, gu, ff)
    dh2 = _matmul(dgate, w_gu, "nt", "ffn_gate_dx", b_cols=(0, ff))
    dh2 = _matmul(dup, w_gu, "nt", "ffn_up_dx", b_cols=(ff, ff), add=dh2)
    g_w_gate = _matmul(h2, dgate, "tn", "ffn_gate_dw")
    g_w_up = _matmul(h2, dup, "tn", "ffn_up_dw")
    dx1, g_ffn_norm = _rmsnorm_bwd(dh2, x1, small["ffn_norm_w"], dy, "norm2_bwd")
    dmix = _matmul(dx1, w_out_a, "nt", "out_proj_dx_a")
    dmix_b = _matmul(dx1, w_out_b, "nt", "out_proj_dx_b")
    dmix = jnp.concatenate([dmix, dmix_b], axis=1)
    g_w_out_a = _matmul(mix_a, dx1, "tn", "out_proj_dw_a")
    g_w_out_b = _matmul(mix_b, dx1, "tn", "out_proj_dw_b")
    dob, dsum, g_attn_out_norm = _attn_merge_bwd(dmix, ob, small["attn_out_norm_w"], heads)
    grads = [_attn_bwd(qk_rot, aqkv, 2 * heads, dob, lse, dsum, heads, d) for d in DILATIONS]
    d_aqkv, g_q_norm, g_k_norm = _attn_pre_bwd(grads, aqkv, small["q_norm_w"], small["k_norm_w"], cosf, sinf, heads)
    do_d, dz, g_delta_out_norm = _delta_post_bwd(dmix, o_d, qkvz, small["delta_out_norm_w"], heads)
    ddq, ddk, ddv, dbd = _delta_chunk_bwd(do_d, dqkv, bd, states, a_log, dt_bias, heads)
    d_qkv_raw, g_conv = _delta_pre_bwd(jnp.concatenate([ddq, ddk, ddv], axis=1), qkvz, conv_w, heads)
    d_qkvz = jnp.concatenate([d_qkv_raw, dz], axis=1)
    bd_sums = _colsum(dbd, "bd_colsum")
    dh1 = _matmul(d_qkvz, w_qkvz, "nt", "proj_qkvz_dx")
    dh1 = _matmul(d_aqkv, w_attn, "nt", "proj_attn_dx", add=dh1)
    dh1 = _matmul(dbd, w_bd, "nt", "proj_bd_dx", add=dh1)
    g_w_qkvz = _matmul(h1, d_qkvz, "tn", "proj_qkvz_dw")
    g_w_bd = _matmul(h1, dbd, "tn", "proj_bd_dw")
    g_w_attn = _matmul(h1, d_aqkv, "tn", "proj_attn_dw")
    grad_x, g_attn_norm = _rmsnorm_bwd(dh1, x, small["attn_norm_w"], dx1, "norm1_bwd")
    small_grads = dict(
        attn_norm_w=g_attn_norm, a_log=bd_sums[:, 2 * heads:3 * heads], dt_bias=bd_sums[:, heads:2 * heads],
        delta_out_norm_w=g_delta_out_norm, q_norm_w=g_q_norm, k_norm_w=g_k_norm,
        attn_out_norm_w=g_attn_out_norm, ffn_norm_w=g_ffn_norm, conv_w=g_conv)
    big_grads = dict(w_qkvz=g_w_qkvz, w_bd=g_w_bd, w_attn=g_w_attn, w_out_a=g_w_out_a, w_out_b=g_w_out_b,
                     w_gate=g_w_gate, w_up=g_w_up, w_down=g_w_down)
    return loss_row, grad_x, small_grads, big_grads


def _adamw(w, g, m, v, name):
    r, c = w.shape
    tr = _tile(r, 256, 8)

    def body(w_ref, g_ref, m_ref, v_ref, d_ref, nm_ref, nv_ref):
        gv = g_ref[...]
        nm = ADAM_B1 * m_ref[...] + (1.0 - ADAM_B1) * gv
        nv = ADAM_B2 * v_ref[...] + (1.0 - ADAM_B2) * (gv * gv)
        m_hat = nm / (1.0 - ADAM_B1 ** ADAM_STEP)
        v_hat = nv / (1.0 - ADAM_B2 ** ADAM_STEP)
        d_ref[...] = -ADAM_LR * (m_hat / (jnp.sqrt(v_hat) + ADAM_EPS) + ADAM_WD * w_ref[...])
        nm_ref[...] = nm
        nv_ref[...] = nv

    blk = pl.BlockSpec((tr, c), lambda i: (i, 0))
    return pl.pallas_call(
        body, name=name, grid=(r // tr,),
        in_specs=[blk] * 4, out_specs=[blk] * 3,
        out_shape=[jax.ShapeDtypeStruct((r, c), F32)] * 3,
        compiler_params=_params(("parallel",)),
    )(w, g, m, v)


def _add_half_bf16(g, b, place, name):
    n, half, c = b.shape
    tr = _tile(half, 512, 16)
    nb = half // tr

    def body(place_ref, g_ref, b_ref, o_ref):
        o_ref[...] = (g_ref[...].astype(F32) + b_ref[...].astype(F32)).astype(BF16)

    blk = pl.BlockSpec((1, tr, c), lambda i, j, p: (i, j, 0))
    return pl.pallas_call(
        body, name=name,
        grid_spec=pltpu.PrefetchScalarGridSpec(
            num_scalar_prefetch=1, grid=(n, nb),
            in_specs=[pl.BlockSpec((1, tr, c), lambda i, j, p: (i, p[0] * nb + j, 0)), blk], out_specs=blk),
        out_shape=jax.ShapeDtypeStruct((n, half, c), BF16),
        compiler_params=_params(("parallel", "parallel")),
    )(place, g, b)


def _sum4_f32(mine, others, place, name):
    _, half, c = mine.shape
    tr = _tile(half, 512, 16)
    nb = half // tr

    def body(place_ref, a_ref, b_ref, o_ref):
        acc = a_ref[0].astype(F32)
        for j in range(3):
            acc = acc + b_ref[j].astype(F32)
        o_ref[...] = acc

    return pl.pallas_call(
        body, name=name,
        grid_spec=pltpu.PrefetchScalarGridSpec(
            num_scalar_prefetch=1, grid=(nb,),
            in_specs=[pl.BlockSpec((1, tr, c), lambda i, p: (p[1], i, 0)),
                      pl.BlockSpec((3, tr, c), lambda i, p: (0, i, 0))],
            out_specs=pl.BlockSpec((tr, c), lambda i, p: (p[0] * nb + i, 0))),
        out_shape=jax.ShapeDtypeStruct((2 * half, c), F32),
        compiler_params=_params(("parallel",)),
    )(place, mine, others)


def _place():
    x, y, c = lax.axis_index("x"), lax.axis_index("y"), lax.axis_index("c")
    other_chips = [(1 - x, y), (x, 1 - y), (1 - x, 1 - y)]
    return x, y, c, (x, y, 1 - c), other_chips


ANY = pl.BlockSpec(memory_space=pl.ANY)


def _all_gather_weights(slots):
    _, r, lanes = slots.shape
    half = r // 2

    def body(in_ref, out_ref, send_sems, recv_sems):
        x, y, c, sibling, chips = _place()

        def rows(ref, px, py, hc):
            return ref.at[2 * px + py, pl.ds(hc * half, half), :]

        def copy(k, src, dst, to):
            return pltpu.make_async_remote_copy(src_ref=src, dst_ref=dst, send_sem=send_sems.at[k],
                                                recv_sem=recv_sems.at[k], device_id=to, device_id_type=MESH)

        first = [copy(j, rows(in_ref, x, y, c), rows(out_ref, x, y, c), (px, py, c))
                 for j, (px, py) in enumerate(chips)]
        for cp in first:
            cp.start()
        passed = []
        for j, (px, py) in enumerate(chips):
            landed = rows(out_ref, px, py, c)
            copy(j, landed, landed, (px, py, c)).wait_recv()
            fw = copy(3 + j, landed, landed, sibling)
            fw.start()
            passed.append(fw)
        for j, (px, py) in enumerate(chips):
            other = rows(out_ref, px, py, 1 - c)
            copy(3 + j, other, other, sibling).wait_recv()
        for cp in first + passed:
            cp.wait_send()

    return pl.pallas_call(
        body, name="all_gather_weights", in_specs=[ANY], out_specs=ANY, input_output_aliases={0: 0},
        out_shape=jax.ShapeDtypeStruct(slots.shape, slots.dtype),
        scratch_shapes=[pltpu.SemaphoreType.DMA((6,)), pltpu.SemaphoreType.DMA((6,))],
    )(slots)


def _swap_halves_with_sibling(g, name):
    n, r, lanes = g.shape
    half = r // 2

    def body(g_ref, o_ref, send_sem, recv_sem):
        _, _, c, sibling, _ = _place()
        cp = pltpu.make_async_remote_copy(src_ref=g_ref.at[:, pl.ds((1 - c) * half, half), :], dst_ref=o_ref,
                                          send_sem=send_sem, recv_sem=recv_sem, device_id=sibling, device_id_type=MESH)
        cp.start()
        cp.wait()

    return pl.pallas_call(
        body, name=name, in_specs=[ANY], out_specs=ANY, out_shape=jax.ShapeDtypeStruct((n, half, lanes), g.dtype),
        scratch_shapes=[pltpu.SemaphoreType.DMA, pltpu.SemaphoreType.DMA],
    )(g)


def _scatter_to_chips(p):
    _, r, lanes = p.shape

    def body(p_ref, b_ref, send_sems, recv_sems):
        _, _, c, _, chips = _place()
        cps = [pltpu.make_async_remote_copy(src_ref=p_ref.at[2 * px + py], dst_ref=b_ref.at[j],
                                            send_sem=send_sems.at[j], recv_sem=recv_sems.at[j],
                                            device_id=(px, py, c), device_id_type=MESH)
               for j, (px, py) in enumerate(chips)]
        for cp in cps:
            cp.start()
        for cp in cps:
            cp.wait()

    return pl.pallas_call(
        body, name="scatter_to_chips", in_specs=[ANY], out_specs=ANY,
        out_shape=jax.ShapeDtypeStruct((3, r, lanes), p.dtype),
        scratch_shapes=[pltpu.SemaphoreType.DMA((3,)), pltpu.SemaphoreType.DMA((3,))],
    )(p)


def _join_halves(f):
    r, lanes = f.shape
    half = r // 2

    def body(in_ref, out_ref, send_sem, recv_sem):
        _, _, c, sibling, _ = _place()
        cp = pltpu.make_async_remote_copy(src_ref=in_ref.at[pl.ds(c * half, half), :],
                                          dst_ref=out_ref.at[pl.ds(c * half, half), :],
                                          send_sem=send_sem, recv_sem=recv_sem, device_id=sibling, device_id_type=MESH)
        cp.start()
        theirs = out_ref.at[pl.ds((1 - c) * half, half), :]
        pltpu.make_async_remote_copy(src_ref=theirs, dst_ref=theirs, send_sem=send_sem, recv_sem=recv_sem,
                                     device_id=sibling, device_id_type=MESH).wait_recv()
        cp.wait_send()

    return pl.pallas_call(
        body, name="join_halves", in_specs=[ANY], out_specs=ANY, input_output_aliases={0: 0},
        out_shape=jax.ShapeDtypeStruct(f.shape, f.dtype),
        scratch_shapes=[pltpu.SemaphoreType.DMA, pltpu.SemaphoreType.DMA],
    )(f)


def _all_reduce_small(v):
    r, lanes = v.shape

    def body(v_ref, out_ref, buf, send_sems, recv_sems):
        x, y, c, sibling, chips = _place()

        def slot(px, py, pc):
            return buf.at[4 * px + 2 * py + pc]

        def copy(k, block, to, src=None):
            return pltpu.make_async_remote_copy(src_ref=slot(*block) if src is None else src, dst_ref=slot(*block),
                                                send_sem=send_sems.at[k], recv_sem=recv_sems.at[k],
                                                device_id=to, device_id_type=MESH)

        me = (x, y, c)
        buf[4 * x + 2 * y + c] = v_ref[...]
        first = [copy(0, me, sibling, src=v_ref)]
        first += [copy(1 + j, me, (*chip, c), src=v_ref) for j, chip in enumerate(chips)]
        for cp in first:
            cp.start()
        passed = [copy(4 + j, (*chip, c), sibling) for j, chip in enumerate(chips)]
        for j, chip in enumerate(chips):
            copy(1 + j, (*chip, c), me).wait_recv()
            passed[j].start()
        copy(0, (x, y, 1 - c), me).wait_recv()
        for j, chip in enumerate(chips):
            copy(4 + j, (*chip, 1 - c), me).wait_recv()
        for cp in first + passed:
            cp.wait_send()
        acc = buf[0]
        for k in range(1, 8):
            acc = acc + buf[k]
        out_ref[...] = acc

    vmem = pl.BlockSpec(memory_space=pltpu.VMEM)
    return pl.pallas_call(
        body, name="all_reduce_small", in_specs=[vmem], out_specs=vmem,
        out_shape=jax.ShapeDtypeStruct((r, lanes), F32),
        scratch_shapes=[pltpu.VMEM((8, r, lanes), F32), pltpu.SemaphoreType.DMA((7,)), pltpu.SemaphoreType.DMA((7,))],
    )(v)


FLAT_LANES = 1024


def _pack_rows(parts, dtype):
    flat = jnp.concatenate([p.reshape(-1).astype(dtype) for p in parts])
    rows = -(-flat.shape[0] // FLAT_LANES)
    rows = -(-rows // 1024) * 1024
    flat = jnp.pad(flat, (0, rows * FLAT_LANES - flat.shape[0]))
    return flat.reshape(rows, FLAT_LANES)


def _size(shape):
    n = 1
    for d in shape:
        n *= d
    return n


def _unpack_rows(flat, shapes):
    flat = flat.reshape(-1)
    out, off = [], 0
    for shp in shapes:
        out.append(flat[off:off + _size(shp)].reshape(shp))
        off += _size(shp)
    return out


def _pack_small(parts):
    rows = []
    for p in parts:
        f = p.reshape(-1).astype(F32)
        n = -(-f.shape[0] // HEAD_DIM) * HEAD_DIM
        rows.append(jnp.pad(f, (0, n - f.shape[0])).reshape(-1, HEAD_DIM))
    a = jnp.concatenate(rows, axis=0)
    return jnp.pad(a, ((0, -a.shape[0] % 8), (0, 0)))


def _unpack_small(a, shapes):
    out, row = [], 0
    for shp in shapes:
        nrows = -(-_size(shp) // HEAD_DIM)
        out.append(a[row:row + nrows].reshape(-1)[:_size(shp)].reshape(shp))
        row += nrows
    return out


SMALL = ["attn_norm_w", "a_log", "dt_bias", "delta_out_norm_w", "q_norm_w", "k_norm_w", "attn_out_norm_w", "ffn_norm_w"]
BIG = ["w_in", "w_out", "w_gate_up", "w_down"]
ORDER = ["attn_norm_w", "w_in", "conv_w", "a_log", "dt_bias", "delta_out_norm_w", "q_norm_w", "k_norm_w",
         "attn_out_norm_w", "w_out", "ffn_norm_w", "w_gate_up", "w_down"]


def kernel(x, positions, attn_norm_w, w_in, conv_w, a_log, dt_bias, delta_out_norm_w, q_norm_w, k_norm_w, attn_out_norm_w, w_out, ffn_norm_w, w_gate_up, w_down, loss_target, m_attn_norm_w, m_w_in, m_conv_w, m_a_log, m_dt_bias, m_delta_out_norm_w, m_q_norm_w, m_k_norm_w, m_attn_out_norm_w, m_w_out, m_ffn_norm_w, m_w_gate_up, m_w_down, v_attn_norm_w, v_w_in, v_conv_w, v_a_log, v_dt_bias, v_delta_out_norm_w, v_q_norm_w, v_k_norm_w, v_attn_out_norm_w, v_w_out, v_ffn_norm_w, v_w_gate_up, v_w_down):
    wts = dict(attn_norm_w=attn_norm_w, w_in=w_in, conv_w=conv_w, a_log=a_log, dt_bias=dt_bias,
               delta_out_norm_w=delta_out_norm_w, q_norm_w=q_norm_w, k_norm_w=k_norm_w,
               attn_out_norm_w=attn_out_norm_w, w_out=w_out, ffn_norm_w=ffn_norm_w, w_gate_up=w_gate_up, w_down=w_down)
    mom = dict(attn_norm_w=m_attn_norm_w, w_in=m_w_in, conv_w=m_conv_w, a_log=m_a_log, dt_bias=m_dt_bias,
               delta_out_norm_w=m_delta_out_norm_w, q_norm_w=m_q_norm_w, k_norm_w=m_k_norm_w,
               attn_out_norm_w=m_attn_out_norm_w, w_out=m_w_out, ffn_norm_w=m_ffn_norm_w, w_gate_up=m_w_gate_up,
               w_down=m_w_down)
    var = dict(attn_norm_w=v_attn_norm_w, w_in=v_w_in, conv_w=v_conv_w, a_log=v_a_log, dt_bias=v_dt_bias,
               delta_out_norm_w=v_delta_out_norm_w, q_norm_w=v_q_norm_w, k_norm_w=v_k_norm_w,
               attn_out_norm_w=v_attn_out_norm_w, w_out=v_w_out, ffn_norm_w=v_ffn_norm_w, w_gate_up=v_w_gate_up,
               w_down=v_w_down)
    dmod = x.shape[2]
    heads = dmod // (2 * HEAD_DIM)
    dw = heads * HEAD_DIM
    chip = 2 * lax.axis_index("x") + lax.axis_index("y")
    core = lax.axis_index("c")
    n_in, n_out, n_gu, n_down, n_conv = (w_in.shape[2], w_out.shape[1], w_gate_up.shape[2], w_down.shape[1],
                                         conv_w.shape[2])
    big_shapes = [(dmod, n_in), (n_out, dmod), (dmod, n_gu), (n_down, dmod)]

    conv_bits = lax.bitcast_convert_type(conv_w[0], BF16)
    flat = _pack_rows([w_in[0], w_out[0], w_gate_up[0], w_down[0], conv_bits], BF16)
    slots = lax.dynamic_update_index_in_dim(lax.empty((4,) + flat.shape, BF16), flat, chip, axis=0)
    gathered = _all_gather_weights(slots)
    per_chip = [_unpack_rows(gathered[i], big_shapes + [(4, n_conv, 2)]) for i in range(4)]
    w_in_f = jnp.concatenate([p[0] for p in per_chip], axis=1)
    w_out_f = jnp.concatenate([p[1] for p in per_chip], axis=0)
    w_gu_f = jnp.concatenate([p[2] for p in per_chip], axis=1)
    w_down_f = jnp.concatenate([p[3] for p in per_chip], axis=0)
    conv_f = jnp.concatenate([lax.bitcast_convert_type(p[4], F32) for p in per_chip], axis=1)
    w_bd = jnp.pad(w_in_f[:, 4 * dw:4 * dw + 2 * heads], ((0, 0), (0, HEAD_DIM - 2 * heads)))
    small = {n: wts[n] for n in SMALL}

    loss_row, grad_x, sg, bg = _local_step(
        x[0], positions[0], loss_target[0], small, w_in_f[:, :4 * dw], w_bd, w_in_f[:, 4 * dw + 2 * heads:],
        w_out_f[:dw], w_out_f[dw:], w_gu_f, w_down_f, conv_f)

    g_in = jnp.concatenate([bg["w_qkvz"], bg["w_bd"][:, :2 * heads], bg["w_attn"]], axis=1)
    g_out = jnp.concatenate([bg["w_out_a"], bg["w_out_b"]], axis=0)
    g_gu = jnp.concatenate([bg["w_gate"], bg["w_up"]], axis=1)
    g_flat = jnp.stack([
        _pack_rows([g_in[:, i * n_in:(i + 1) * n_in], g_out[i * n_out:(i + 1) * n_out],
                    g_gu[:, i * n_gu:(i + 1) * n_gu], bg["w_down"][i * n_down:(i + 1) * n_down]], BF16)
        for i in range(4)])
    place = jnp.stack([core, chip]).astype(jnp.int32)
    from_sibling = _swap_halves_with_sibling(g_flat, "swap_grad_halves")
    chip_sum = _add_half_bf16(g_flat, from_sibling, place, "chip_partial_sum")
    from_chips = _scatter_to_chips(chip_sum)
    total = _join_halves(_sum4_f32(chip_sum, from_chips, place, "grad_total"))
    g_big = dict(zip(BIG, _unpack_rows(total, big_shapes)))

    reduced = _all_reduce_small(_pack_small([sg[n] for n in SMALL] + [sg["conv_w"], loss_row]))
    red = _unpack_small(reduced, [wts[n].shape for n in SMALL] + [(4, 4 * n_conv), (1, HEAD_DIM)])
    g_small = dict(zip(SMALL, red[:len(SMALL)]))
    g_conv_full, loss_out = red[len(SMALL)], red[len(SMALL) + 1]
    g_small["conv_w"] = lax.dynamic_slice_in_dim(g_conv_full, chip * n_conv, n_conv, axis=1).reshape(conv_w.shape)

    grads, deltas, new_m, new_v = {}, {}, {}, {}
    for n in BIG:
        shp = wts[n].shape
        d, nm, nv = _adamw(wts[n][0], g_big[n], mom[n][0], var[n][0], "adamw_" + n)
        grads[n], deltas[n], new_m[n], new_v[n] = g_big[n].reshape(shp), d.reshape(shp), nm.reshape(shp), nv.reshape(shp)
    names = SMALL + ["conv_w"]
    shapes = [wts[n].shape for n in names]
    d, nm, nv = _adamw(_pack_small([wts[n] for n in names]), _pack_small([g_small[n] for n in names]),
                       _pack_small([mom[n] for n in names]), _pack_small([var[n] for n in names]), "adamw_small")
    for n, dd, mm, vv in zip(names, _unpack_small(d, shapes), _unpack_small(nm, shapes), _unpack_small(nv, shapes)):
        grads[n], deltas[n], new_m[n], new_v[n] = g_small[n], dd, mm, vv
    return (loss_out[0, 0], grad_x[None], *[grads[n] for n in ORDER], *[deltas[n] for n in ORDER],
            *[new_m[n] for n in ORDER], *[new_v[n] for n in ORDER])
```

```python
import functools

import jax
import jax.numpy as jnp
from jax import lax
from jax.experimental import pallas as pl
from jax.experimental.pallas import tpu as pltpu

F32 = jnp.float32
BF16 = jnp.bfloat16
HEAD_DIM = 128
CHUNK = 64
SPAN = 128
DILATIONS = (1, 4, 16)
ROPE_THETA = 10000.0
EPS = 1e-6
NEG = -1e30
ADAM_LR, ADAM_B1, ADAM_B2, ADAM_EPS, ADAM_WD, ADAM_STEP = 0.001, 0.9, 0.999, 1e-08, 0.01, 10
VMEM_LIMIT = 48 * 1024 * 1024
MESH = pl.DeviceIdType.MESH

_DN = {"nn": (((1,), (0,)), ((), ())), "nt": (((1,), (1,)), ((), ())), "tn": (((0,), (0,)), ((), ()))}


def _dot(a, b, mode="nn"):
    (ca, cb), _ = _DN[mode]
    if a.ndim == 3:
        dn = (((ca[0] + 1,), (cb[0] + 1,)), ((0,), (0,)))
    else:
        dn = _DN[mode]
    return lax.dot_general(a.astype(BF16), b.astype(BF16), dn, preferred_element_type=F32)


def _rsum(x):
    return jnp.sum(x, axis=-1, keepdims=True)


def _csum(x):
    return jnp.sum(x, axis=-2, keepdims=True)


def _tile(dim, pref, unit=128):
    t = (min(pref, dim) // unit) * unit
    while t >= unit:
        if dim % t == 0:
            return t
        t -= unit
    return dim


def _params(sem):
    return pltpu.CompilerParams(dimension_semantics=sem, vmem_limit_bytes=VMEM_LIMIT)


def _sigmoid(x):
    return 1.0 / (1.0 + jnp.exp(-x))


def _matmul(a, b, mode, name, add=None, out_dtype=F32, a_cols=None, b_cols=None):
    if mode == "tn":
        out_dtype = BF16
    a_off, a_w = a_cols if a_cols else (0, a.shape[1])
    b_off, b_w = b_cols if b_cols else (0, b.shape[1])
    if mode == "nn":
        m, k, n = a.shape[0], a_w, b_w
        assert b.shape[0] == k
    elif mode == "nt":
        m, k, n = a.shape[0], a_w, b.shape[0]
        assert b_w == k
    else:
        k, m, n = a.shape[0], a_w, b_w
        assert b.shape[0] == k
    tm, tn = _tile(m, 1024, 128), _tile(n, 1024, 128)
    tk = _tile(k, 2048 if a.dtype == BF16 and b.dtype == BF16 else 1024, 128)
    if mode == "nn":
        assert a_off % tk == 0 and b_off % tn == 0
        a_spec = pl.BlockSpec((tm, tk), lambda i, j, kk: (i, kk + a_off // tk))
        b_spec = pl.BlockSpec((tk, tn), lambda i, j, kk: (kk, j + b_off // tn))
    elif mode == "nt":
        assert a_off % tk == 0 and b_off % tk == 0
        a_spec = pl.BlockSpec((tm, tk), lambda i, j, kk: (i, kk + a_off // tk))
        b_spec = pl.BlockSpec((tn, tk), lambda i, j, kk: (j, kk + b_off // tk))
    else:
        assert a_off % tm == 0 and b_off % tn == 0
        a_spec = pl.BlockSpec((tk, tm), lambda i, j, kk: (kk, i + a_off // tm))
        b_spec = pl.BlockSpec((tk, tn), lambda i, j, kk: (kk, j + b_off // tn))
    nk = k // tk
    has_add = add is not None

    def body(*refs):
        if has_add:
            a_ref, b_ref, add_ref, o_ref, acc_ref = refs
        else:
            a_ref, b_ref, o_ref, acc_ref = refs
        kk = pl.program_id(2)

        @pl.when(kk == 0)
        def _():
            acc_ref[...] = jnp.zeros_like(acc_ref)

        acc_ref[...] += _dot(a_ref[...], b_ref[...], mode)

        @pl.when(kk == nk - 1)
        def _():
            r = acc_ref[...]
            if has_add:
                r = r + add_ref[...].astype(F32)
            o_ref[...] = r.astype(out_dtype)

    in_specs = [a_spec, b_spec]
    args = [a, b]
    if has_add:
        in_specs.append(pl.BlockSpec((tm, tn), lambda i, j, kk: (i, j)))
        args.append(add)
    return pl.pallas_call(
        body, name=name, grid=(m // tm, n // tn, nk),
        in_specs=in_specs, out_specs=pl.BlockSpec((tm, tn), lambda i, j, kk: (i, j)),
        out_shape=jax.ShapeDtypeStruct((m, n), out_dtype),
        scratch_shapes=[pltpu.VMEM((tm, tn), F32)],
        compiler_params=_params(("parallel", "parallel", "arbitrary")),
    )(*args)


def _rmsnorm_fwd(x, w, name):
    s, d = x.shape
    tr = _tile(s, 512, 8)

    def body(x_ref, w_ref, h_ref):
        xv = x_ref[...]
        rstd = lax.rsqrt(jnp.mean(xv * xv, axis=-1, keepdims=True) + EPS)
        h_ref[...] = (xv * rstd * w_ref[...]).astype(BF16)

    return pl.pallas_call(
        body, name=name, grid=(s // tr,),
        in_specs=[pl.BlockSpec((tr, d), lambda i: (i, 0)), pl.BlockSpec((1, d), lambda i: (0, 0))],
        out_specs=pl.BlockSpec((tr, d), lambda i: (i, 0)),
        out_shape=jax.ShapeDtypeStruct((s, d), BF16),
        compiler_params=_params(("parallel",)),
    )(x, w)


def _rmsnorm_bwd(dh, x, w, res, name):
    s, d = x.shape
    tr = _tile(s, 256, 8)

    def body(dh_ref, x_ref, w_ref, res_ref, dx_ref, dw_ref):
        xv = x_ref[...]
        rstd = lax.rsqrt(jnp.mean(xv * xv, axis=-1, keepdims=True) + EPS)
        xhat = xv * rstd
        dhv = dh_ref[...]
        gw = dhv * w_ref[...]
        dx_ref[...] = res_ref[...] + rstd * (gw - xhat * jnp.mean(gw * xhat, axis=-1, keepdims=True))

        @pl.when(pl.program_id(0) == 0)
        def _():
            dw_ref[...] = jnp.zeros_like(dw_ref)

        dw_ref[...] += jnp.sum(dhv * xhat, axis=0, keepdims=True)

    row = pl.BlockSpec((tr, d), lambda i: (i, 0))
    vec = pl.BlockSpec((1, d), lambda i: (0, 0))
    return pl.pallas_call(
        body, name=name, grid=(s // tr,),
        in_specs=[row, row, vec, row], out_specs=[row, vec],
        out_shape=[jax.ShapeDtypeStruct((s, d), F32), jax.ShapeDtypeStruct((1, d), F32)],
        compiler_params=_params(("arbitrary",)),
    )(dh, x, w, res)


def _conv_taps(x, w, rows):
    shifted = [x]
    for sft in (1, 2, 3):
        shifted.append(jnp.where(rows >= sft, pltpu.roll(x, sft, 0), 0.0))
    y = w[3:4, :] * shifted[0] + w[2:3, :] * shifted[1] + w[1:2, :] * shifted[2] + w[0:1, :] * shifted[3]
    return y, shifted


def _delta_pre_fwd(qkvz, conv_w, heads):
    s = qkvz.shape[0]
    nblk = 3 * heads

    def body(x_ref, w_ref, o_ref):
        part = pl.program_id(0) // heads
        rows = lax.broadcasted_iota(jnp.int32, (s, HEAD_DIM), 0)
        y, _ = _conv_taps(x_ref[...], w_ref[...], rows)
        a = y * _sigmoid(y)
        rs = lax.rsqrt(jnp.sum(a * a, axis=-1, keepdims=True) + EPS)
        fac = jnp.where(part == 0, rs * (HEAD_DIM ** -0.5), jnp.where(part == 1, rs, 1.0))
        o_ref[...] = a * fac

    return pl.pallas_call(
        body, name="delta_pre_fwd", grid=(nblk,),
        in_specs=[pl.BlockSpec((s, HEAD_DIM), lambda i: (0, i)), pl.BlockSpec((4, HEAD_DIM), lambda i: (0, i))],
        out_specs=pl.BlockSpec((s, HEAD_DIM), lambda i: (0, i)),
        out_shape=jax.ShapeDtypeStruct((s, 3 * heads * HEAD_DIM), F32),
        compiler_params=_params(("parallel",)),
    )(qkvz, conv_w)


def _delta_pre_bwd(dqkv, qkvz, conv_w, heads):
    s = qkvz.shape[0]
    nblk = 3 * heads

    def body(d_ref, x_ref, w_ref, dx_ref, dw_ref):
        part = pl.program_id(0) // heads
        rows = lax.broadcasted_iota(jnp.int32, (s, HEAD_DIM), 0)
        w = w_ref[...]
        y, shifted = _conv_taps(x_ref[...], w, rows)
        sg = _sigmoid(y)
        a = y * sg
        rs = lax.rsqrt(jnp.sum(a * a, axis=-1, keepdims=True) + EPS)
        unit = a * rs
        dn = d_ref[...]
        scale = jnp.where(part == 0, HEAD_DIM ** -0.5, 1.0)
        da_norm = scale * rs * (dn - unit * jnp.sum(dn * unit, axis=-1, keepdims=True))
        da = jnp.where(part < 2, da_norm, dn)
        dy = da * sg * (1.0 + y * (1.0 - sg))
        dx = w[3:4, :] * dy
        for sft in (1, 2, 3):
            dx = dx + w[3 - sft:4 - sft, :] * jnp.where(rows < s - sft, pltpu.roll(dy, s - sft, 0), 0.0)
        dx_ref[...] = dx
        for sft in range(4):
            dw_ref[3 - sft:4 - sft, :] = jnp.sum(dy * shifted[sft], axis=0, keepdims=True)

    col = pl.BlockSpec((s, HEAD_DIM), lambda i: (0, i))
    wsp = pl.BlockSpec((4, HEAD_DIM), lambda i: (0, i))
    return pl.pallas_call(
        body, name="delta_pre_bwd", grid=(nblk,),
        in_specs=[col, col, wsp], out_specs=[col, wsp],
        out_shape=[jax.ShapeDtypeStruct((s, 3 * heads * HEAD_DIM), F32),
                   jax.ShapeDtypeStruct((4, 3 * heads * HEAD_DIM), F32)],
        compiler_params=_params(("parallel",)),
    )(dqkv, qkvz, conv_w)


def _heads_of(ref, heads):
    return jnp.stack([ref[:, h * HEAD_DIM:(h + 1) * HEAD_DIM] for h in range(heads)])


def _chunk_common(q, k, v, bd, a_log, dt_bias, heads):
    c = CHUNK
    braw = jnp.stack([bd[:, h:h + 1] for h in range(heads)])
    draw = jnp.stack([bd[:, heads + h:heads + h + 1] for h in range(heads)])
    beta = _sigmoid(braw)
    xd = draw + dt_bias
    sp = jnp.maximum(xd, 0.0) + jnp.log1p(jnp.exp(-jnp.abs(xd)))
    g = -jnp.exp(a_log) * sp
    row = lax.broadcasted_iota(jnp.int32, (c, c), 0)
    col = lax.broadcasted_iota(jnp.int32, (c, c), 1)
    sq = (heads, c, c)
    g_b = jnp.broadcast_to(g, sq)
    g_row = _csum(jnp.where(row == col, g_b, 0.0))
    gam_col = _rsum(jnp.where(col <= row, jnp.broadcast_to(g_row, sq), 0.0))
    gam_row = _csum(jnp.where(row <= col, g_b, 0.0))
    causal = row >= col
    dm = jnp.where(causal, jnp.exp(jnp.where(causal, gam_col - gam_row, 0.0)), 0.0)
    kk = _dot(k, k, "nt")
    low = jnp.where(row > col, beta * kk * dm, 0.0)
    t = jnp.where(row == col, 1.0, 0.0) - low
    pw = low
    for _ in range(5):
        pw = _dot(pw, pw)
        t = t + _dot(t, pw)
    e = jnp.exp(gam_col)
    u = _dot(t, beta * v)
    w = _dot(t, (beta * e) * k)
    qk_raw = _dot(q, k, "nt")
    gl = _csum(g)
    el = jnp.exp(gl - gam_col)
    return dict(beta=beta, xd=xd, g=g, row=row, col=col, dm=dm, kk=kk, t=t, e=e, u=u, w=w,
                qk_raw=qk_raw, qk=qk_raw * dm, gl=gl, el=el, qd=e * q, kd=el * k, cd=jnp.exp(gl))


def _delta_chunk_fwd(qkv, bd, a_log, dt_bias, heads):
    s = qkv.shape[0]
    n = s // CHUNK
    dw = heads * HEAD_DIM
    blk = lambda part: pl.BlockSpec((CHUNK, dw), lambda i: (i, part))

    def body(q_ref, k_ref, v_ref, bd_ref, al_ref, dt_ref, o_ref, st_ref, state):
        @pl.when(pl.program_id(0) == 0)
        def _():
            state[...] = jnp.zeros_like(state)

        cm = _chunk_common(_heads_of(q_ref, heads), _heads_of(k_ref, heads), _heads_of(v_ref, heads), bd_ref[...],
                           al_ref[...], dt_ref[...], heads)
        st = state[...]
        st_ref[0] = st
        vn = cm["u"] - _dot(cm["w"], st)
        o = _dot(cm["qd"], st) + _dot(cm["qk"], vn)
        for h in range(heads):
            o_ref[:, h * HEAD_DIM:(h + 1) * HEAD_DIM] = o[h]
        state[...] = cm["cd"] * st + _dot(cm["kd"], vn, "tn")

    smem = pl.BlockSpec((heads, 1, 1), lambda i: (0, 0, 0))
    return pl.pallas_call(
        body, name="delta_chunk_fwd", grid=(n,),
        in_specs=[blk(0), blk(1), blk(2), pl.BlockSpec((CHUNK, HEAD_DIM), lambda i: (i, 0)), smem, smem],
        out_specs=[pl.BlockSpec((CHUNK, dw), lambda i: (i, 0)),
                   pl.BlockSpec((1, heads, HEAD_DIM, HEAD_DIM), lambda i: (i, 0, 0, 0))],
        out_shape=[jax.ShapeDtypeStruct((s, dw), F32),
                   jax.ShapeDtypeStruct((n, heads, HEAD_DIM, HEAD_DIM), F32)],
        scratch_shapes=[pltpu.VMEM((heads, HEAD_DIM, HEAD_DIM), F32)],
        compiler_params=_params(("arbitrary",)),
    )(qkv, qkv, qkv, bd, a_log, dt_bias)


def _delta_chunk_bwd(do, qkv, bd, states, a_log, dt_bias, heads):
    s = qkv.shape[0]
    n = s // CHUNK
    dw = heads * HEAD_DIM
    c = CHUNK
    blk = lambda part: pl.BlockSpec((CHUNK, dw), lambda i: (n - 1 - i, part))

    def all_heads(q, k, v, dov, st, dsn, bd, a_log, dt_bias):
        cm = _chunk_common(q, k, v, bd, a_log, dt_bias, heads)
        beta, e, dm, row, col = cm["beta"], cm["e"], cm["dm"], cm["row"], cm["col"]
        sq = (heads, c, c)
        vn = cm["u"] - _dot(cm["w"], st)
        dvn = _dot(cm["kd"], dsn)
        dkd = _dot(vn, dsn, "nt")
        dcd = _csum(_rsum(st * dsn))
        ds = cm["cd"] * dsn
        dqd = _dot(dov, st, "nt")
        ds = ds + _dot(cm["qd"], dov, "tn")
        dqk = _dot(dov, vn, "nt")
        dvn = dvn + _dot(cm["qk"], dov, "tn")
        dw_ = -_dot(dvn, st, "nt")
        ds = ds - _dot(cm["w"], dvn, "tn")
        drhs_u = _dot(cm["t"], dvn, "tn")
        drhs_w = _dot(cm["t"], dw_, "tn")
        da = -(_dot(drhs_u, cm["u"], "nt") + _dot(drhs_w, cm["w"], "nt"))
        dl = jnp.where(row > col, da, 0.0)
        dbeta = _rsum(dl * cm["kk"] * dm)
        dkk = dl * beta * dm
        dd = dl * beta * cm["kk"]
        dv = beta * drhs_u
        ek = e * k
        dbeta = dbeta + _rsum(drhs_u * v) + _rsum(drhs_w * ek)
        dk = (beta * e) * drhs_w
        dgam = _rsum(drhs_w * (beta * ek))
        dqkm = dqk * dm
        dq = _dot(dqkm, k)
        dk = dk + _dot(dqkm, q, "tn")
        dd = dd + dqk * cm["qk_raw"]
        dk = dk + _dot(dkk, k) + _dot(dkk, k, "tn")
        dq = dq + e * dqd
        dgam = dgam + _rsum(dqd * cm["qd"])
        dk = dk + cm["el"] * dkd
        r = _rsum(dkd * cm["kd"])
        dgam = dgam - r
        dgl = _csum(r) + dcd * cm["cd"]
        mm = dd * dm
        colsum_c = _rsum(jnp.where(row == col, jnp.broadcast_to(_csum(mm), sq), 0.0))
        dgam = dgam + _rsum(mm) - colsum_c
        ridx = lax.broadcasted_iota(jnp.int32, (c, 1), 0)
        dgam = dgam + jnp.where(ridx == c - 1, dgl, 0.0)
        dgam_row = _csum(jnp.where(row == col, jnp.broadcast_to(dgam, sq), 0.0))
        dg = _rsum(jnp.where(col >= row, jnp.broadcast_to(dgam_row, sq), 0.0))
        d_xd = dg * (-jnp.exp(a_log)) * _sigmoid(cm["xd"])
        d_braw = dbeta * beta * (1.0 - beta)
        d_alog = dg * cm["g"]
        lane = lax.broadcasted_iota(jnp.int32, (c, HEAD_DIM), 1)
        dbd = jnp.zeros((c, HEAD_DIM), F32)
        for h in range(heads):
            dbd = (dbd + jnp.where(lane == h, d_braw[h], 0.0) + jnp.where(lane == h + heads, d_xd[h], 0.0)
                   + jnp.where(lane == h + 2 * heads, d_alog[h], 0.0))
        return dq, dk, dv, ds, dbd

    def body(do_ref, q_ref, k_ref, v_ref, bd_ref, st_ref, al_ref, dt_ref, dq_ref, dk_ref, dv_ref, dbd_ref, dstate):
        @pl.when(pl.program_id(0) == 0)
        def _():
            dstate[...] = jnp.zeros_like(dstate)

        dq, dk, dv, ds, dbd = all_heads(_heads_of(q_ref, heads), _heads_of(k_ref, heads), _heads_of(v_ref, heads),
                                        _heads_of(do_ref, heads), st_ref[0], dstate[...], bd_ref[...],
                                        al_ref[...], dt_ref[...])
        for h in range(heads):
            hs = slice(h * HEAD_DIM, (h + 1) * HEAD_DIM)
            dq_ref[:, hs] = dq[h]
            dk_ref[:, hs] = dk[h]
            dv_ref[:, hs] = dv[h]
        dstate[...] = ds
        dbd_ref[...] = dbd

    smem = pl.BlockSpec((heads, 1, 1), lambda i: (0, 0, 0))
    shared = pl.BlockSpec((CHUNK, HEAD_DIM), lambda i: (n - 1 - i, 0))
    wide = pl.BlockSpec((CHUNK, dw), lambda i: (n - 1 - i, 0))
    dq, dk, dv, dbd = pl.pallas_call(
        body, name="delta_chunk_bwd", grid=(n,),
        in_specs=[wide, blk(0), blk(1), blk(2), shared,
                  pl.BlockSpec((1, heads, HEAD_DIM, HEAD_DIM), lambda i: (n - 1 - i, 0, 0, 0)), smem, smem],
        out_specs=[wide, wide, wide, shared],
        out_shape=[jax.ShapeDtypeStruct((s, dw), F32)] * 3 + [jax.ShapeDtypeStruct((s, HEAD_DIM), F32)],
        scratch_shapes=[pltpu.VMEM((heads, HEAD_DIM, HEAD_DIM), F32)],
        compiler_params=_params(("arbitrary",)),
    )(do, qkv, qkv, qkv, bd, states, a_log, dt_bias)
    return dq, dk, dv, dbd


def _delta_post_fwd(o, qkvz, w, heads):
    s = o.shape[0]
    tr = _tile(s, 1024, 8)

    def body(o_ref, z_ref, w_ref, out_ref):
        ov, z = o_ref[...], z_ref[...]
        rstd = lax.rsqrt(jnp.mean(ov * ov, axis=-1, keepdims=True) + EPS)
        out_ref[...] = (ov * rstd * w_ref[...] * (z * _sigmoid(z))).astype(BF16)

    return pl.pallas_call(
        body, name="delta_post_fwd", grid=(s // tr, heads),
        in_specs=[pl.BlockSpec((tr, HEAD_DIM), lambda i, h: (i, h)),
                  pl.BlockSpec((tr, HEAD_DIM), lambda i, h: (i, 3 * heads + h)),
                  pl.BlockSpec((1, HEAD_DIM), lambda i, h: (0, 0))],
        out_specs=pl.BlockSpec((tr, HEAD_DIM), lambda i, h: (i, h)),
        out_shape=jax.ShapeDtypeStruct((s, heads * HEAD_DIM), BF16),
        compiler_params=_params(("parallel", "parallel")),
    )(o, qkvz, w)


def _delta_post_bwd(dmix, o, qkvz, w, heads):
    s = o.shape[0]
    tr = _tile(s, 1024, 8)

    def body(d_ref, o_ref, z_ref, w_ref, do_ref, dz_ref, dw_ref):
        d, ov, z, wv = d_ref[...], o_ref[...], z_ref[...], w_ref[...]
        sg = _sigmoid(z)
        rstd = lax.rsqrt(jnp.mean(ov * ov, axis=-1, keepdims=True) + EPS)
        ohat = ov * rstd
        dz_ref[...] = d * (ohat * wv) * sg * (1.0 + z * (1.0 - sg))
        dn = d * (z * sg)
        gw = dn * wv
        do_ref[...] = rstd * (gw - ohat * jnp.mean(gw * ohat, axis=-1, keepdims=True))

        @pl.when((pl.program_id(0) == 0) & (pl.program_id(1) == 0))
        def _():
            dw_ref[...] = jnp.zeros_like(dw_ref)

        dw_ref[...] += jnp.sum(dn * ohat, axis=0, keepdims=True)

    head = pl.BlockSpec((tr, HEAD_DIM), lambda i, h: (i, h))
    vec = pl.BlockSpec((1, HEAD_DIM), lambda i, h: (0, 0))
    dw = heads * HEAD_DIM
    return pl.pallas_call(
        body, name="delta_post_bwd", grid=(s // tr, heads),
        in_specs=[head, head, pl.BlockSpec((tr, HEAD_DIM), lambda i, h: (i, 3 * heads + h)), vec],
        out_specs=[head, head, vec],
        out_shape=[jax.ShapeDtypeStruct((s, dw), F32), jax.ShapeDtypeStruct((s, dw), F32),
                   jax.ShapeDtypeStruct((1, HEAD_DIM), F32)],
        compiler_params=_params(("arbitrary", "arbitrary")),
    )(dmix, o, qkvz, w)


def _rope_tables(positions, s):
    half = HEAD_DIM // 2
    inv_freq = ROPE_THETA ** (-jnp.arange(half, dtype=F32) / half)
    ang = positions.reshape(s, 1).astype(F32) * inv_freq
    cos, sin = jnp.cos(ang), jnp.sin(ang)
    return jnp.concatenate([cos, cos], axis=-1), jnp.concatenate([-sin, sin], axis=-1)


def _attn_pre_fwd(aqkv, wq, wk, cosf, sinf, heads):
    s = aqkv.shape[0]
    tr = _tile(s, 1024, 8)

    def body(x_ref, wq_ref, wk_ref, c_ref, s_ref, o_ref):
        xv = x_ref[...]
        wv = jnp.where(pl.program_id(1) < heads, wq_ref[...], wk_ref[...])
        y = xv * lax.rsqrt(jnp.mean(xv * xv, axis=-1, keepdims=True) + EPS) * wv
        o_ref[...] = y * c_ref[...] + pltpu.roll(y, HEAD_DIM // 2, 1) * s_ref[...]

    blk = pl.BlockSpec((tr, HEAD_DIM), lambda i, j: (i, j))
    vec = pl.BlockSpec((1, HEAD_DIM), lambda i, j: (0, 0))
    tab = pl.BlockSpec((tr, HEAD_DIM), lambda i, j: (i, 0))
    return pl.pallas_call(
        body, name="attn_pre_fwd", grid=(s // tr, 2 * heads),
        in_specs=[blk, vec, vec, tab, tab], out_specs=blk,
        out_shape=jax.ShapeDtypeStruct((s, 2 * heads * HEAD_DIM), F32),
        compiler_params=_params(("parallel", "parallel")),
    )(aqkv, wq, wk, cosf, sinf)


ATT_ROWS = 2048


def _band():
    qi = lax.broadcasted_iota(jnp.int32, (SPAN, 2 * SPAN), 0)
    ki = lax.broadcasted_iota(jnp.int32, (SPAN, 2 * SPAN), 1)
    dist = qi + SPAN - ki
    return (dist >= 0) & (dist <= SPAN), ki >= SPAN


def _sub(g, r, d):
    if d == 1:
        return pl.ds(g * SPAN, SPAN)
    return pl.ds(g * SPAN * d + r, SPAN, stride=d)


def _attn_geometry(s, d):
    rb = min(ATT_ROWS, s)
    pr = SPAN * d
    return rb, pr, rb // pr, s // rb


def _attn_fwd(qk, aqkv, heads, d):
    s = qk.shape[0]
    aw = heads * HEAD_DIM
    rb, pr, g_cnt, nstep = _attn_geometry(s, d)

    def body(q_ref, k_ref, v_ref, kp_ref, vp_ref, o_ref, l_ref):
        first = pl.program_id(1) == 0
        band, own = _band()
        edge = band & (own | jnp.logical_not(first))
        for r in range(d):
            for g in range(g_cnt):
                rows = _sub(g, r, d)
                if g == 0:
                    kp, vp, mask = kp_ref[_sub(0, r, d), :], vp_ref[_sub(0, r, d), :], edge
                else:
                    kp, vp, mask = k_ref[_sub(g - 1, r, d), :], v_ref[_sub(g - 1, r, d), :], band
                kcat = jnp.concatenate([kp, k_ref[rows, :]], axis=0)
                vcat = jnp.concatenate([vp, v_ref[rows, :]], axis=0)
                sc = _dot(q_ref[rows, :], kcat, "nt") * (HEAD_DIM ** -0.5)
                sc = jnp.where(mask, sc, NEG)
                m = jnp.max(sc, axis=-1, keepdims=True)
                p = jnp.exp(sc - m)
                den = jnp.sum(p, axis=-1, keepdims=True)
                o_ref[rows, :] = _dot(p, vcat) / den
                l_ref[rows, :] = jnp.broadcast_to(m + jnp.log(den), (SPAN, HEAD_DIM))

    cur = lambda off: pl.BlockSpec((rb, HEAD_DIM), lambda h, j: (j, off + h))
    prev = lambda off: pl.BlockSpec((pr, HEAD_DIM), lambda h, j: (jnp.maximum(j * (rb // pr) - 1, 0), off + h))
    out = pl.BlockSpec((rb, HEAD_DIM), lambda h, j: (j, h))
    return pl.pallas_call(
        body, name=f"attn_fwd_d{d}", grid=(heads, nstep),
        in_specs=[cur(0), cur(heads), cur(2 * heads), prev(heads), prev(2 * heads)],
        out_specs=[out, out],
        out_shape=[jax.ShapeDtypeStruct((s, aw), F32)] * 2,
        compiler_params=_params(("parallel", "parallel")),
    )(qk, qk, aqkv, qk, aqkv)


def _attn_merge_fwd(os_, ls_, w, heads):
    s = os_[0].shape[0]
    tr = _tile(s, 512, 8)

    def body(o1, o2, o3, l1, l2, l3, w_ref, mix_ref, ob_ref, lse_ref):
        la, lb, lc = l1[...], l2[...], l3[...]
        m = jnp.maximum(jnp.maximum(la, lb), lc)
        ea, eb, ec = jnp.exp(la - m), jnp.exp(lb - m), jnp.exp(lc - m)
        tot = ea + eb + ec
        ob = (ea * o1[...] + eb * o2[...] + ec * o3[...]) / tot
        ob_ref[...] = ob
        lse_ref[...] = m + jnp.log(tot)
        rstd = lax.rsqrt(jnp.mean(ob * ob, axis=-1, keepdims=True) + EPS)
        mix_ref[...] = (ob * rstd * w_ref[...]).astype(BF16)

    blk = pl.BlockSpec((tr, HEAD_DIM), lambda i, h: (i, h))
    vec = pl.BlockSpec((1, HEAD_DIM), lambda i, h: (0, 0))
    aw = heads * HEAD_DIM
    return pl.pallas_call(
        body, name="attn_merge_fwd", grid=(s // tr, heads),
        in_specs=[blk] * 6 + [vec], out_specs=[blk, blk, blk],
        out_shape=[jax.ShapeDtypeStruct((s, aw), BF16), jax.ShapeDtypeStruct((s, aw), F32),
                   jax.ShapeDtypeStruct((s, aw), F32)],
        compiler_params=_params(("parallel", "parallel")),
    )(*os_, *ls_, w)


def _attn_merge_bwd(dmix, ob, w, heads):
    s = ob.shape[0]
    tr = _tile(s, 1024, 8)

    def body(d_ref, ob_ref, w_ref, do_ref, dsum_ref, dw_ref):
        d, ov = d_ref[...], ob_ref[...]
        rstd = lax.rsqrt(jnp.mean(ov * ov, axis=-1, keepdims=True) + EPS)
        ohat = ov * rstd
        gw = d * w_ref[...]
        dov = rstd * (gw - ohat * jnp.mean(gw * ohat, axis=-1, keepdims=True))
        do_ref[...] = dov
        dsum_ref[...] = jnp.broadcast_to(jnp.sum(dov * ov, axis=-1, keepdims=True), dov.shape)

        @pl.when((pl.program_id(0) == 0) & (pl.program_id(1) == 0))
        def _():
            dw_ref[...] = jnp.zeros_like(dw_ref)

        dw_ref[...] += jnp.sum(d * ohat, axis=0, keepdims=True)

    blk = pl.BlockSpec((tr, HEAD_DIM), lambda i, h: (i, h))
    vec = pl.BlockSpec((1, HEAD_DIM), lambda i, h: (0, 0))
    aw = heads * HEAD_DIM
    return pl.pallas_call(
        body, name="attn_merge_bwd", grid=(s // tr, heads),
        in_specs=[pl.BlockSpec((tr, HEAD_DIM), lambda i, h: (i, heads + h)), blk, vec],
        out_specs=[blk, blk, vec],
        out_shape=[jax.ShapeDtypeStruct((s, aw), F32), jax.ShapeDtypeStruct((s, aw), F32),
                   jax.ShapeDtypeStruct((1, HEAD_DIM), F32)],
        compiler_params=_params(("arbitrary", "arbitrary")),
    )(dmix, ob, w)


def _attn_bwd(qk, aqkv, do, lse, dsum, heads, d):
    s = qk.shape[0]
    aw = heads * HEAD_DIM
    rb, pr, g_cnt, nstep = _attn_geometry(s, d)
    scale = HEAD_DIM ** -0.5

    def body(q_ref, k_ref, v_ref, do_ref, l_ref, ds_ref, kp_ref, vp_ref, qn_ref, don_ref, ln_ref, dsn_ref, out_ref):
        first = pl.program_id(1) == 0
        last = pl.program_id(1) == nstep - 1
        band, own = _band()
        edge = band & (own | jnp.logical_not(first))
        dq_ref, dk_ref, dv_ref = out_ref.at[0], out_ref.at[1], out_ref.at[2]
        dk_ref[...] = jnp.zeros((rb, HEAD_DIM), F32)
        dv_ref[...] = jnp.zeros((rb, HEAD_DIM), F32)
        for r in range(d):
            for g in range(g_cnt):
                rows = _sub(g, r, d)
                if g == 0:
                    kp, vp, mask = kp_ref[_sub(0, r, d), :], vp_ref[_sub(0, r, d), :], edge
                else:
                    prows = _sub(g - 1, r, d)
                    kp, vp, mask = k_ref[prows, :], v_ref[prows, :], band
                q, dov = q_ref[rows, :], do_ref[rows, :]
                kcat = jnp.concatenate([kp, k_ref[rows, :]], axis=0)
                vcat = jnp.concatenate([vp, v_ref[rows, :]], axis=0)
                sc = _dot(q, kcat, "nt") * scale
                p = jnp.where(mask, jnp.exp(sc - l_ref[rows, 0:1]), 0.0)
                dsc = p * (_dot(dov, vcat, "nt") - ds_ref[rows, 0:1]) * scale
                dq_ref[rows, :] = _dot(dsc, kcat)
                dk = _dot(dsc, q, "tn")
                dv = _dot(p, dov, "tn")
                dk_ref[rows, :] += dk[SPAN:]
                dv_ref[rows, :] += dv[SPAN:]
                if g > 0:
                    dk_ref[prows, :] += dk[:SPAN]
                    dv_ref[prows, :] += dv[:SPAN]
            rows, nrows = _sub(g_cnt - 1, r, d), _sub(0, r, d)
            q, dov = qn_ref[nrows, :], don_ref[nrows, :]
            sc = _dot(q, k_ref[rows, :], "nt") * scale
            p = jnp.where(band[:, :SPAN] & jnp.logical_not(last), jnp.exp(sc - ln_ref[nrows, 0:1]), 0.0)
            dsc = p * (_dot(dov, v_ref[rows, :], "nt") - dsn_ref[nrows, 0:1]) * scale
            dk_ref[rows, :] += _dot(dsc, q, "tn")
            dv_ref[rows, :] += _dot(p, dov, "tn")

    per = rb // pr
    cur = lambda off: pl.BlockSpec((rb, HEAD_DIM), lambda h, j: (j, off + h))
    prev = lambda off: pl.BlockSpec((pr, HEAD_DIM), lambda h, j: (jnp.maximum(j * per - 1, 0), off + h))
    nxt = lambda off: pl.BlockSpec((pr, HEAD_DIM), lambda h, j: (jnp.minimum((j + 1) * per, s // pr - 1), off + h))
    return pl.pallas_call(
        body, name=f"attn_bwd_d{d}", grid=(heads, nstep),
        in_specs=[cur(0), cur(heads), cur(2 * heads), cur(0), cur(0), cur(0), prev(heads), prev(2 * heads),
                  nxt(0), nxt(0), nxt(0), nxt(0)],
        out_specs=pl.BlockSpec((3, rb, HEAD_DIM), lambda h, j: (0, j, h)),
        out_shape=jax.ShapeDtypeStruct((3, s, aw), F32),
        compiler_params=_params(("parallel", "parallel")),
    )(qk, qk, aqkv, do, lse, dsum, qk, aqkv, qk, do, lse, dsum)


def _attn_pre_bwd(grads, aqkv, wq, wk, cosf, sinf, heads):
    s = aqkv.shape[0]
    tr = _tile(s, 512, 8)
    nrow = s // tr

    def body(g1_ref, g2_ref, g3_ref, x_ref, wq_ref, wk_ref, c_ref, s_ref, dx_ref, dwq_ref, dwk_ref):
        i, j = pl.program_id(0), pl.program_id(1)
        kind = j // heads
        dout = g1_ref[0] + g2_ref[0] + g3_ref[0]
        tot_v = dout
        dy = dout * c_ref[...] + pltpu.roll(dout * s_ref[...], HEAD_DIM // 2, 1)
        xv = x_ref[...]
        wv = jnp.where(kind == 0, wq_ref[...], wk_ref[...])
        rstd = lax.rsqrt(jnp.mean(xv * xv, axis=-1, keepdims=True) + EPS)
        xhat = xv * rstd
        gw = dy * wv
        dxn = rstd * (gw - xhat * jnp.mean(gw * xhat, axis=-1, keepdims=True))
        dx_ref[...] = jnp.where(kind == 2, tot_v, dxn)
        dwc = jnp.sum(dy * xhat, axis=0, keepdims=True)

        @pl.when((i == 0) & (j == 0))
        def _():
            dwq_ref[...] = jnp.zeros_like(dwq_ref)
            dwk_ref[...] = jnp.zeros_like(dwk_ref)

        @pl.when(kind == 0)
        def _():
            dwq_ref[...] += dwc

        @pl.when(kind == 1)
        def _():
            dwk_ref[...] += dwc

    grad = pl.BlockSpec((1, tr, HEAD_DIM), lambda i, j: (j // heads, i, j % heads))
    blk = pl.BlockSpec((tr, HEAD_DIM), lambda i, j: (i, j))
    vec = pl.BlockSpec((1, HEAD_DIM), lambda i, j: (0, 0))
    tab = pl.BlockSpec((tr, HEAD_DIM), lambda i, j: (i, 0))
    return pl.pallas_call(
        body, name="attn_pre_bwd", grid=(nrow, 3 * heads),
        in_specs=[grad, grad, grad, blk, vec, vec, tab, tab], out_specs=[blk, vec, vec],
        out_shape=[jax.ShapeDtypeStruct((s, 3 * heads * HEAD_DIM), F32),
                   jax.ShapeDtypeStruct((1, HEAD_DIM), F32), jax.ShapeDtypeStruct((1, HEAD_DIM), F32)],
        compiler_params=_params(("arbitrary", "arbitrary")),
    )(*grads, aqkv, wq, wk, cosf, sinf)


def _swiglu_fwd(gu, ff):
    s = gu.shape[0]
    tr, tc = _tile(s, 512, 8), _tile(ff, 1024, 128)
    nc = ff // tc

    def body(g_ref, u_ref, o_ref):
        g = g_ref[...]
        o_ref[...] = (g * _sigmoid(g) * u_ref[...]).astype(BF16)

    return pl.pallas_call(
        body, name="swiglu_fwd", grid=(s // tr, nc),
        in_specs=[pl.BlockSpec((tr, tc), lambda i, j: (i, j)), pl.BlockSpec((tr, tc), lambda i, j: (i, j + nc))],
        out_specs=pl.BlockSpec((tr, tc), lambda i, j: (i, j)),
        out_shape=jax.ShapeDtypeStruct((s, ff), BF16),
        compiler_params=_params(("parallel", "parallel")),
    )(gu, gu)


def _swiglu_bwd(dact, gu, ff):
    s = gu.shape[0]
    tr, tc = _tile(s, 512, 8), _tile(ff, 1024, 128)
    nc = ff // tc

    def body(d_ref, g_ref, u_ref, dg_ref, du_ref):
        d, g = d_ref[...], g_ref[...]
        sg = _sigmoid(g)
        dg_ref[...] = (d * u_ref[...] * sg * (1.0 + g * (1.0 - sg))).astype(BF16)
        du_ref[...] = (d * g * sg).astype(BF16)

    lo = pl.BlockSpec((tr, tc), lambda i, j: (i, j))
    hi = pl.BlockSpec((tr, tc), lambda i, j: (i, j + nc))
    dg, du = pl.pallas_call(
        body, name="swiglu_bwd", grid=(s // tr, nc),
        in_specs=[lo, lo, hi], out_specs=[lo, lo],
        out_shape=[jax.ShapeDtypeStruct((s, ff), BF16)] * 2,
        compiler_params=_params(("parallel", "parallel")),
    )(dact, gu, gu)
    return dg, du


def _loss_head(y, target):
    s, d = y.shape
    tr = _tile(s, 512, 8)

    def body(y_ref, t_ref, dy_ref, l_ref):
        err = y_ref[...] - t_ref[...]
        dy_ref[...] = err * (1.0 / d)
        part = 0.5 * jnp.sum(jnp.sum(err * err, axis=-1, keepdims=True) * (1.0 / d), axis=0, keepdims=True)

        @pl.when(pl.program_id(0) == 0)
        def _():
            l_ref[...] = jnp.zeros_like(l_ref)

        lane = lax.broadcasted_iota(jnp.int32, (1, HEAD_DIM), 1)
        l_ref[...] += jnp.where(lane == 0, part, 0.0)

    row = pl.BlockSpec((tr, d), lambda i: (i, 0))
    return pl.pallas_call(
        body, name="loss_head", grid=(s // tr,),
        in_specs=[row, row], out_specs=[row, pl.BlockSpec((1, HEAD_DIM), lambda i: (0, 0))],
        out_shape=[jax.ShapeDtypeStruct((s, d), F32), jax.ShapeDtypeStruct((1, HEAD_DIM), F32)],
        compiler_params=_params(("arbitrary",)),
    )(y, target)


def _colsum(a, name):
    s, d = a.shape
    tr = _tile(s, 1024, 8)

    def body(a_ref, o_ref):
        @pl.when(pl.program_id(0) == 0)
        def _():
            o_ref[...] = jnp.zeros_like(o_ref)

        o_ref[...] += jnp.sum(a_ref[...], axis=0, keepdims=True)

    return pl.pallas_call(
        body, name=name, grid=(s // tr,),
        in_specs=[pl.BlockSpec((tr, d), lambda i: (i, 0))], out_specs=pl.BlockSpec((1, d), lambda i: (0, 0)),
        out_shape=jax.ShapeDtypeStruct((1, d), F32),
        compiler_params=_params(("arbitrary",)),
    )(a)


def _local_step(x, positions, target, small, w_qkvz, w_bd, w_attn, w_out_a, w_out_b, w_gu, w_down, conv_w):
    s, dmod = x.shape
    heads = dmod // (2 * HEAD_DIM)
    dw = heads * HEAD_DIM
    ff = w_down.shape[0]
    a_log, dt_bias = small["a_log"].reshape(heads, 1, 1), small["dt_bias"].reshape(heads, 1, 1)
    cosf, sinf = _rope_tables(positions, s)

    h1 = _rmsnorm_fwd(x, small["attn_norm_w"], "norm1_fwd")
    qkvz = _matmul(h1, w_qkvz, "nn", "proj_qkvz")
    bd = _matmul(h1, w_bd, "nn", "proj_bd")
    aqkv = _matmul(h1, w_attn, "nn", "proj_attn")
    dqkv = _delta_pre_fwd(qkvz, conv_w, heads)
    o_d, states = _delta_chunk_fwd(dqkv, bd, a_log, dt_bias, heads)
    mix_a = _delta_post_fwd(o_d, qkvz, small["delta_out_norm_w"], heads)
    qk_rot = _attn_pre_fwd(aqkv, small["q_norm_w"], small["k_norm_w"], cosf, sinf, heads)
    outs = [_attn_fwd(qk_rot, aqkv, heads, d) for d in DILATIONS]
    mix_b, ob, lse = _attn_merge_fwd([o for o, _ in outs], [l for _, l in outs], small["attn_out_norm_w"], heads)
    x1 = _matmul(mix_a, w_out_a, "nn", "out_proj_a", add=x)
    x1 = _matmul(mix_b, w_out_b, "nn", "out_proj_b", add=x1)
    h2 = _rmsnorm_fwd(x1, small["ffn_norm_w"], "norm2_fwd")
    gu = _matmul(h2, w_gu, "nn", "ffn_gate_up")
    act = _swiglu_fwd(gu, ff)
    y = _matmul(act, w_down, "nn", "ffn_down", add=x1)
    dy, loss_row = _loss_head(y, target)

    dact = _matmul(dy, w_down, "nt", "ffn_down_dx")
    g_w_down = _matmul(act, dy, "tn", "ffn_down_dw")
    dgate, dup = _swiglu_bwd(dact, gu, ff)
    dh2 = _matmul(dgate, w_gu, "nt", "ffn_gate_dx", b_cols=(0, ff))
    dh2 = _matmul(dup, w_gu, "nt", "ffn_up_dx", b_cols=(ff, ff), add=dh2)
    g_w_gate = _matmul(h2, dgate, "tn", "ffn_gate_dw")
    g_w_up = _matmul(h2, dup, "tn", "ffn_up_dw")
    dx1, g_ffn_norm = _rmsnorm_bwd(dh2, x1, small["ffn_norm_w"], dy, "norm2_bwd")
    dmix = _matmul(dx1, w_out_a, "nt", "out_proj_dx_a")
    dmix_b = _matmul(dx1, w_out_b, "nt", "out_proj_dx_b")
    dmix = jnp.concatenate([dmix, dmix_b], axis=1)
    g_w_out_a = _matmul(mix_a, dx1, "tn", "out_proj_dw_a")
    g_w_out_b = _matmul(mix_b, dx1, "tn", "out_proj_dw_b")
    dob, dsum, g_attn_out_norm = _attn_merge_bwd(dmix, ob, small["attn_out_norm_w"], heads)
    grads = [_attn_bwd(qk_rot, aqkv, dob, lse, dsum, heads, d) for d in DILATIONS]
    d_aqkv, g_q_norm, g_k_norm = _attn_pre_bwd(grads, aqkv, small["q_norm_w"], small["k_norm_w"], cosf, sinf, heads)
    do_d, dz, g_delta_out_norm = _delta_post_bwd(dmix, o_d, qkvz, small["delta_out_norm_w"], heads)
    ddq, ddk, ddv, dbd = _delta_chunk_bwd(do_d, dqkv, bd, states, a_log, dt_bias, heads)
    d_qkv_raw, g_conv = _delta_pre_bwd(jnp.concatenate([ddq, ddk, ddv], axis=1), qkvz, conv_w, heads)
    d_qkvz = jnp.concatenate([d_qkv_raw, dz], axis=1)
    bd_sums = _colsum(dbd, "bd_colsum")
    dh1 = _matmul(d_qkvz, w_qkvz, "nt", "proj_qkvz_dx")
    dh1 = _matmul(d_aqkv, w_attn, "nt", "proj_attn_dx", add=dh1)
    dh1 = _matmul(dbd, w_bd, "nt", "proj_bd_dx", add=dh1)
    g_w_qkvz = _matmul(h1, d_qkvz, "tn", "proj_qkvz_dw")
    g_w_bd = _matmul(h1, dbd, "tn", "proj_bd_dw")
    g_w_attn = _matmul(h1, d_aqkv, "tn", "proj_attn_dw")
    grad_x, g_attn_norm = _rmsnorm_bwd(dh1, x, small["attn_norm_w"], dx1, "norm1_bwd")
    small_grads = dict(
        attn_norm_w=g_attn_norm, a_log=bd_sums[:, 2 * heads:3 * heads], dt_bias=bd_sums[:, heads:2 * heads],
        delta_out_norm_w=g_delta_out_norm, q_norm_w=g_q_norm, k_norm_w=g_k_norm,
        attn_out_norm_w=g_attn_out_norm, ffn_norm_w=g_ffn_norm, conv_w=g_conv)
    big_grads = dict(w_qkvz=g_w_qkvz, w_bd=g_w_bd, w_attn=g_w_attn, w_out_a=g_w_out_a, w_out_b=g_w_out_b,
                     w_gate=g_w_gate, w_up=g_w_up, w_down=g_w_down)
    return loss_row, grad_x, small_grads, big_grads


def _adamw(w, g, m, v, name):
    r, c = w.shape
    tr = _tile(r, 256, 8)

    def body(w_ref, g_ref, m_ref, v_ref, d_ref, nm_ref, nv_ref):
        gv = g_ref[...]
        nm = ADAM_B1 * m_ref[...] + (1.0 - ADAM_B1) * gv
        nv = ADAM_B2 * v_ref[...] + (1.0 - ADAM_B2) * (gv * gv)
        m_hat = nm / (1.0 - ADAM_B1 ** ADAM_STEP)
        v_hat = nv / (1.0 - ADAM_B2 ** ADAM_STEP)
        d_ref[...] = -ADAM_LR * (m_hat / (jnp.sqrt(v_hat) + ADAM_EPS) + ADAM_WD * w_ref[...])
        nm_ref[...] = nm
        nv_ref[...] = nv

    blk = pl.BlockSpec((tr, c), lambda i: (i, 0))
    return pl.pallas_call(
        body, name=name, grid=(r // tr,),
        in_specs=[blk] * 4, out_specs=[blk] * 3,
        out_shape=[jax.ShapeDtypeStruct((r, c), F32)] * 3,
        compiler_params=_params(("parallel",)),
    )(w, g, m, v)


def _add_half_bf16(g, b, place, name):
    n, half, c = b.shape
    tr = _tile(half, 512, 16)
    nb = half // tr

    def body(place_ref, g_ref, b_ref, o_ref):
        o_ref[...] = (g_ref[...].astype(F32) + b_ref[...].astype(F32)).astype(BF16)

    blk = pl.BlockSpec((1, tr, c), lambda i, j, p: (i, j, 0))
    return pl.pallas_call(
        body, name=name,
        grid_spec=pltpu.PrefetchScalarGridSpec(
            num_scalar_prefetch=1, grid=(n, nb),
            in_specs=[pl.BlockSpec((1, tr, c), lambda i, j, p: (i, p[0] * nb + j, 0)), blk], out_specs=blk),
        out_shape=jax.ShapeDtypeStruct((n, half, c), BF16),
        compiler_params=_params(("parallel", "parallel")),
    )(place, g, b)


def _sum4_f32(mine, others, place, name):
    _, half, c = mine.shape
    tr = _tile(half, 512, 16)
    nb = half // tr

    def body(place_ref, a_ref, b_ref, o_ref):
        acc = a_ref[0].astype(F32)
        for j in range(3):
            acc = acc + b_ref[j].astype(F32)
        o_ref[...] = acc

    return pl.pallas_call(
        body, name=name,
        grid_spec=pltpu.PrefetchScalarGridSpec(
            num_scalar_prefetch=1, grid=(nb,),
            in_specs=[pl.BlockSpec((1, tr, c), lambda i, p: (p[1], i, 0)),
                      pl.BlockSpec((3, tr, c), lambda i, p: (0, i, 0))],
            out_specs=pl.BlockSpec((tr, c), lambda i, p: (p[0] * nb + i, 0))),
        out_shape=jax.ShapeDtypeStruct((2 * half, c), F32),
        compiler_params=_params(("parallel",)),
    )(place, mine, others)


def _place():
    x, y, c = lax.axis_index("x"), lax.axis_index("y"), lax.axis_index("c")
    other_chips = [(1 - x, y), (x, 1 - y), (1 - x, 1 - y)]
    return x, y, c, (x, y, 1 - c), other_chips


ANY = pl.BlockSpec(memory_space=pl.ANY)


def _all_gather_weights(slots):
    _, r, lanes = slots.shape
    half = r // 2

    def body(in_ref, out_ref, send_sems, recv_sems):
        x, y, c, sibling, chips = _place()

        def rows(ref, px, py, hc):
            return ref.at[2 * px + py, pl.ds(hc * half, half), :]

        def copy(k, src, dst, to):
            return pltpu.make_async_remote_copy(src_ref=src, dst_ref=dst, send_sem=send_sems.at[k],
                                                recv_sem=recv_sems.at[k], device_id=to, device_id_type=MESH)

        first = [copy(j, rows(in_ref, x, y, c), rows(out_ref, x, y, c), (px, py, c))
                 for j, (px, py) in enumerate(chips)]
        for cp in first:
            cp.start()
        passed = []
        for j, (px, py) in enumerate(chips):
            landed = rows(out_ref, px, py, c)
            copy(j, landed, landed, (px, py, c)).wait_recv()
            fw = copy(3 + j, landed, landed, sibling)
            fw.start()
            passed.append(fw)
        for j, (px, py) in enumerate(chips):
            other = rows(out_ref, px, py, 1 - c)
            copy(3 + j, other, other, sibling).wait_recv()
        for cp in first + passed:
            cp.wait_send()

    return pl.pallas_call(
        body, name="all_gather_weights", in_specs=[ANY], out_specs=ANY, input_output_aliases={0: 0},
        out_shape=jax.ShapeDtypeStruct(slots.shape, slots.dtype),
        scratch_shapes=[pltpu.SemaphoreType.DMA((6,)), pltpu.SemaphoreType.DMA((6,))],
    )(slots)


def _swap_halves_with_sibling(g, name):
    n, r, lanes = g.shape
    half = r // 2

    def body(g_ref, o_ref, send_sem, recv_sem):
        _, _, c, sibling, _ = _place()
        cp = pltpu.make_async_remote_copy(src_ref=g_ref.at[:, pl.ds((1 - c) * half, half), :], dst_ref=o_ref,
                                          send_sem=send_sem, recv_sem=recv_sem, device_id=sibling, device_id_type=MESH)
        cp.start()
        cp.wait()

    return pl.pallas_call(
        body, name=name, in_specs=[ANY], out_specs=ANY, out_shape=jax.ShapeDtypeStruct((n, half, lanes), g.dtype),
        scratch_shapes=[pltpu.SemaphoreType.DMA, pltpu.SemaphoreType.DMA],
    )(g)


def _scatter_to_chips(p):
    _, r, lanes = p.shape

    def body(p_ref, b_ref, send_sems, recv_sems):
        _, _, c, _, chips = _place()
        cps = [pltpu.make_async_remote_copy(src_ref=p_ref.at[2 * px + py], dst_ref=b_ref.at[j],
                                            send_sem=send_sems.at[j], recv_sem=recv_sems.at[j],
                                            device_id=(px, py, c), device_id_type=MESH)
               for j, (px, py) in enumerate(chips)]
        for cp in cps:
            cp.start()
        for cp in cps:
            cp.wait()

    return pl.pallas_call(
        body, name="scatter_to_chips", in_specs=[ANY], out_specs=ANY,
        out_shape=jax.ShapeDtypeStruct((3, r, lanes), p.dtype),
        scratch_shapes=[pltpu.SemaphoreType.DMA((3,)), pltpu.SemaphoreType.DMA((3,))],
    )(p)


def _join_halves(f):
    r, lanes = f.shape
    half = r // 2

    def body(in_ref, out_ref, send_sem, recv_sem):
        _, _, c, sibling, _ = _place()
        cp = pltpu.make_async_remote_copy(src_ref=in_ref.at[pl.ds(c * half, half), :],
                                          dst_ref=out_ref.at[pl.ds(c * half, half), :],
                                          send_sem=send_sem, recv_sem=recv_sem, device_id=sibling, device_id_type=MESH)
        cp.start()
        theirs = out_ref.at[pl.ds((1 - c) * half, half), :]
        pltpu.make_async_remote_copy(src_ref=theirs, dst_ref=theirs, send_sem=send_sem, recv_sem=recv_sem,
                                     device_id=sibling, device_id_type=MESH).wait_recv()
        cp.wait_send()

    return pl.pallas_call(
        body, name="join_halves", in_specs=[ANY], out_specs=ANY, input_output_aliases={0: 0},
        out_shape=jax.ShapeDtypeStruct(f.shape, f.dtype),
        scratch_shapes=[pltpu.SemaphoreType.DMA, pltpu.SemaphoreType.DMA],
    )(f)


def _all_reduce_small(v):
    r, lanes = v.shape

    def body(v_ref, out_ref, buf, send_sems, recv_sems):
        x, y, c, sibling, chips = _place()

        def slot(px, py, pc):
            return buf.at[4 * px + 2 * py + pc]

        def copy(k, block, to, src=None):
            return pltpu.make_async_remote_copy(src_ref=slot(*block) if src is None else src, dst_ref=slot(*block),
                                                send_sem=send_sems.at[k], recv_sem=recv_sems.at[k],
                                                device_id=to, device_id_type=MESH)

        me = (x, y, c)
        buf[4 * x + 2 * y + c] = v_ref[...]
        first = [copy(0, me, sibling, src=v_ref)]
        first += [copy(1 + j, me, (*chip, c), src=v_ref) for j, chip in enumerate(chips)]
        for cp in first:
            cp.start()
        passed = [copy(4 + j, (*chip, c), sibling) for j, chip in enumerate(chips)]
        for j, chip in enumerate(chips):
            copy(1 + j, (*chip, c), me).wait_recv()
            passed[j].start()
        copy(0, (x, y, 1 - c), me).wait_recv()
        for j, chip in enumerate(chips):
            copy(4 + j, (*chip, 1 - c), me).wait_recv()
        for cp in first + passed:
            cp.wait_send()
        acc = buf[0]
        for k in range(1, 8):
            acc = acc + buf[k]
        out_ref[...] = acc

    vmem = pl.BlockSpec(memory_space=pltpu.VMEM)
    return pl.pallas_call(
        body, name="all_reduce_small", in_specs=[vmem], out_specs=vmem,
        out_shape=jax.ShapeDtypeStruct((r, lanes), F32),
        scratch_shapes=[pltpu.VMEM((8, r, lanes), F32), pltpu.SemaphoreType.DMA((7,)), pltpu.SemaphoreType.DMA((7,))],
    )(v)


FLAT_LANES = 1024


def _pack_rows(parts, dtype):
    flat = jnp.concatenate([p.reshape(-1).astype(dtype) for p in parts])
    rows = -(-flat.shape[0] // FLAT_LANES)
    rows = -(-rows // 1024) * 1024
    flat = jnp.pad(flat, (0, rows * FLAT_LANES - flat.shape[0]))
    return flat.reshape(rows, FLAT_LANES)


def _size(shape):
    n = 1
    for d in shape:
        n *= d
    return n


def _unpack_rows(flat, shapes):
    flat = flat.reshape(-1)
    out, off = [], 0
    for shp in shapes:
        out.append(flat[off:off + _size(shp)].reshape(shp))
        off += _size(shp)
    return out


def _pack_small(parts):
    rows = []
    for p in parts:
        f = p.reshape(-1).astype(F32)
        n = -(-f.shape[0] // HEAD_DIM) * HEAD_DIM
        rows.append(jnp.pad(f, (0, n - f.shape[0])).reshape(-1, HEAD_DIM))
    a = jnp.concatenate(rows, axis=0)
    return jnp.pad(a, ((0, -a.shape[0] % 8), (0, 0)))


def _unpack_small(a, shapes):
    out, row = [], 0
    for shp in shapes:
        nrows = -(-_size(shp) // HEAD_DIM)
        out.append(a[row:row + nrows].reshape(-1)[:_size(shp)].reshape(shp))
        row += nrows
    return out


SMALL = ["attn_norm_w", "a_log", "dt_bias", "delta_out_norm_w", "q_norm_w", "k_norm_w", "attn_out_norm_w", "ffn_norm_w"]
BIG = ["w_in", "w_out", "w_gate_up", "w_down"]
ORDER = ["attn_norm_w", "w_in", "conv_w", "a_log", "dt_bias", "delta_out_norm_w", "q_norm_w", "k_norm_w",
         "attn_out_norm_w", "w_out", "ffn_norm_w", "w_gate_up", "w_down"]


def kernel(x, positions, attn_norm_w, w_in, conv_w, a_log, dt_bias, delta_out_norm_w, q_norm_w, k_norm_w, attn_out_norm_w, w_out, ffn_norm_w, w_gate_up, w_down, loss_target, m_attn_norm_w, m_w_in, m_conv_w, m_a_log, m_dt_bias, m_delta_out_norm_w, m_q_norm_w, m_k_norm_w, m_attn_out_norm_w, m_w_out, m_ffn_norm_w, m_w_gate_up, m_w_down, v_attn_norm_w, v_w_in, v_conv_w, v_a_log, v_dt_bias, v_delta_out_norm_w, v_q_norm_w, v_k_norm_w, v_attn_out_norm_w, v_w_out, v_ffn_norm_w, v_w_gate_up, v_w_down):
    wts = dict(attn_norm_w=attn_norm_w, w_in=w_in, conv_w=conv_w, a_log=a_log, dt_bias=dt_bias,
               delta_out_norm_w=delta_out_norm_w, q_norm_w=q_norm_w, k_norm_w=k_norm_w,
               attn_out_norm_w=attn_out_norm_w, w_out=w_out, ffn_norm_w=ffn_norm_w, w_gate_up=w_gate_up, w_down=w_down)
    mom = dict(attn_norm_w=m_attn_norm_w, w_in=m_w_in, conv_w=m_conv_w, a_log=m_a_log, dt_bias=m_dt_bias,
               delta_out_norm_w=m_delta_out_norm_w, q_norm_w=m_q_norm_w, k_norm_w=m_k_norm_w,
               attn_out_norm_w=m_attn_out_norm_w, w_out=m_w_out, ffn_norm_w=m_ffn_norm_w, w_gate_up=m_w_gate_up,
               w_down=m_w_down)
    var = dict(attn_norm_w=v_attn_norm_w, w_in=v_w_in, conv_w=v_conv_w, a_log=v_a_log, dt_bias=v_dt_bias,
               delta_out_norm_w=v_delta_out_norm_w, q_norm_w=v_q_norm_w, k_norm_w=v_k_norm_w,
               attn_out_norm_w=v_attn_out_norm_w, w_out=v_w_out, ffn_norm_w=v_ffn_norm_w, w_gate_up=v_w_gate_up,
               w_down=v_w_down)
    dmod = x.shape[2]
    heads = dmod // (2 * HEAD_DIM)
    dw = heads * HEAD_DIM
    chip = 2 * lax.axis_index("x") + lax.axis_index("y")
    core = lax.axis_index("c")
    n_in, n_out, n_gu, n_down, n_conv = (w_in.shape[2], w_out.shape[1], w_gate_up.shape[2], w_down.shape[1],
                                         conv_w.shape[2])
    big_shapes = [(dmod, n_in), (n_out, dmod), (dmod, n_gu), (n_down, dmod)]

    conv_bits = lax.bitcast_convert_type(conv_w[0], BF16)
    flat = _pack_rows([w_in[0], w_out[0], w_gate_up[0], w_down[0], conv_bits], BF16)
    slots = lax.dynamic_update_index_in_dim(lax.empty((4,) + flat.shape, BF16), flat, chip, axis=0)
    gathered = _all_gather_weights(slots)
    per_chip = [_unpack_rows(gathered[i], big_shapes + [(4, n_conv, 2)]) for i in range(4)]
    w_in_f = jnp.concatenate([p[0] for p in per_chip], axis=1)
    w_out_f = jnp.concatenate([p[1] for p in per_chip], axis=0)
    w_gu_f = jnp.concatenate([p[2] for p in per_chip], axis=1)
    w_down_f = jnp.concatenate([p[3] for p in per_chip], axis=0)
    conv_f = jnp.concatenate([lax.bitcast_convert_type(p[4], F32) for p in per_chip], axis=1)
    w_bd = jnp.pad(w_in_f[:, 4 * dw:4 * dw + 2 * heads], ((0, 0), (0, HEAD_DIM - 2 * heads)))
    small = {n: wts[n] for n in SMALL}

    loss_row, grad_x, sg, bg = _local_step(
        x[0], positions[0], loss_target[0], small, w_in_f[:, :4 * dw], w_bd, w_in_f[:, 4 * dw + 2 * heads:],
        w_out_f[:dw], w_out_f[dw:], w_gu_f, w_down_f, conv_f)

    g_in = jnp.concatenate([bg["w_qkvz"], bg["w_bd"][:, :2 * heads], bg["w_attn"]], axis=1)
    g_out = jnp.concatenate([bg["w_out_a"], bg["w_out_b"]], axis=0)
    g_gu = jnp.concatenate([bg["w_gate"], bg["w_up"]], axis=1)
    g_flat = jnp.stack([
        _pack_rows([g_in[:, i * n_in:(i + 1) * n_in], g_out[i * n_out:(i + 1) * n_out],
                    g_gu[:, i * n_gu:(i + 1) * n_gu], bg["w_down"][i * n_down:(i + 1) * n_down]], BF16)
        for i in range(4)])
    place = jnp.stack([core, chip]).astype(jnp.int32)
    from_sibling = _swap_halves_with_sibling(g_flat, "swap_grad_halves")
    chip_sum = _add_half_bf16(g_flat, from_sibling, place, "chip_partial_sum")
    from_chips = _scatter_to_chips(chip_sum)
    total = _join_halves(_sum4_f32(chip_sum, from_chips, place, "grad_total"))
    g_big = dict(zip(BIG, _unpack_rows(total, big_shapes)))

    reduced = _all_reduce_small(_pack_small([sg[n] for n in SMALL] + [sg["conv_w"], loss_row]))
    red = _unpack_small(reduced, [wts[n].shape for n in SMALL] + [(4, 4 * n_conv), (1, HEAD_DIM)])
    g_small = dict(zip(SMALL, red[:len(SMALL)]))
    g_conv_full, loss_out = red[len(SMALL)], red[len(SMALL) + 1]
    g_small["conv_w"] = lax.dynamic_slice_in_dim(g_conv_full, chip * n_conv, n_conv, axis=1).reshape(conv_w.shape)

    grads, deltas, new_m, new_v = {}, {}, {}, {}
    for n in BIG:
        shp = wts[n].shape
        d, nm, nv = _adamw(wts[n][0], g_big[n], mom[n][0], var[n][0], "adamw_" + n)
        grads[n], deltas[n], new_m[n], new_v[n] = g_big[n].reshape(shp), d.reshape(shp), nm.reshape(shp), nv.reshape(shp)
    names = SMALL + ["conv_w"]
    shapes = [wts[n].shape for n in names]
    d, nm, nv = _adamw(_pack_small([wts[n] for n in names]), _pack_small([g_small[n] for n in names]),
                       _pack_small([mom[n] for n in names]), _pack_small([var[n] for n in names]), "adamw_small")
    for n, dd, mm, vv in zip(names, _unpack_small(d, shapes), _unpack_small(nm, shapes), _unpack_small(nv, shapes)):
        grads[n], deltas[n], new_m[n], new_v[n] = g_small[n], dd, mm, vv
    return (loss_out[0, 0], grad_x[None], *[grads[n] for n in ORDER], *[deltas[n] for n in ORDER],
            *[new_m[n] for n in ORDER], *[new_v[n] for n in ORDER])
```

```python
import functools

import jax
import jax.numpy as jnp
from jax import lax
from jax.experimental import pallas as pl
from jax.experimental.pallas import tpu as pltpu

F32 = jnp.float32
BF16 = jnp.bfloat16
HEAD_DIM = 128
CHUNK = 64
SPAN = 128
DILATIONS = (1, 4, 16)
ROPE_THETA = 10000.0
EPS = 1e-6
NEG = -1e30
ADAM_LR, ADAM_B1, ADAM_B2, ADAM_EPS, ADAM_WD, ADAM_STEP = 0.001, 0.9, 0.999, 1e-08, 0.01, 10
VMEM_LIMIT = 48 * 1024 * 1024
MESH = pl.DeviceIdType.MESH

_DN = {"nn": (((1,), (0,)), ((), ())), "nt": (((1,), (1,)), ((), ())), "tn": (((0,), (0,)), ((), ()))}


def _dot(a, b, mode="nn"):
    (ca, cb), _ = _DN[mode]
    if a.ndim == 3:
        dn = (((ca[0] + 1,), (cb[0] + 1,)), ((0,), (0,)))
    else:
        dn = _DN[mode]
    return lax.dot_general(a.astype(BF16), b.astype(BF16), dn, preferred_element_type=F32)


def _rsum(x):
    return jnp.sum(x, axis=-1, keepdims=True)


def _csum(x):
    return jnp.sum(x, axis=-2, keepdims=True)


def _tile(dim, pref, unit=128):
    t = (min(pref, dim) // unit) * unit
    while t >= unit:
        if dim % t == 0:
            return t
        t -= unit
    return dim


def _params(sem):
    return pltpu.CompilerParams(dimension_semantics=sem, vmem_limit_bytes=VMEM_LIMIT)


def _sigmoid(x):
    return 1.0 / (1.0 + jnp.exp(-x))


def _matmul(a, b, mode, name, add=None, out_dtype=F32, a_cols=None, b_cols=None):
    if mode == "tn":
        out_dtype = BF16
    a_off, a_w = a_cols if a_cols else (0, a.shape[1])
    b_off, b_w = b_cols if b_cols else (0, b.shape[1])
    if mode == "nn":
        m, k, n = a.shape[0], a_w, b_w
        assert b.shape[0] == k
    elif mode == "nt":
        m, k, n = a.shape[0], a_w, b.shape[0]
        assert b_w == k
    else:
        k, m, n = a.shape[0], a_w, b_w
        assert b.shape[0] == k
    tm, tn = _tile(m, 1024, 128), _tile(n, 1024, 128)
    tk = _tile(k, 2048 if a.dtype == BF16 and b.dtype == BF16 else 1024, 128)
    if mode == "nn":
        assert a_off % tk == 0 and b_off % tn == 0
        a_spec = pl.BlockSpec((tm, tk), lambda i, j, kk: (i, kk + a_off // tk))
        b_spec = pl.BlockSpec((tk, tn), lambda i, j, kk: (kk, j + b_off // tn))
    elif mode == "nt":
        assert a_off % tk == 0 and b_off % tk == 0
        a_spec = pl.BlockSpec((tm, tk), lambda i, j, kk: (i, kk + a_off // tk))
        b_spec = pl.BlockSpec((tn, tk), lambda i, j, kk: (j, kk + b_off // tk))
    else:
        assert a_off % tm == 0 and b_off % tn == 0
        a_spec = pl.BlockSpec((tk, tm), lambda i, j, kk: (kk, i + a_off // tm))
        b_spec = pl.BlockSpec((tk, tn), lambda i, j, kk: (kk, j + b_off // tn))
    nk = k // tk
    has_add = add is not None

    def body(*refs):
        if has_add:
            a_ref, b_ref, add_ref, o_ref, acc_ref = refs
        else:
            a_ref, b_ref, o_ref, acc_ref = refs
        kk = pl.program_id(2)

        @pl.when(kk == 0)
        def _():
            acc_ref[...] = jnp.zeros_like(acc_ref)

        acc_ref[...] += _dot(a_ref[...], b_ref[...], mode)

        @pl.when(kk == nk - 1)
        def _():
            r = acc_ref[...]
            if has_add:
                r = r + add_ref[...].astype(F32)
            o_ref[...] = r.astype(out_dtype)

    in_specs = [a_spec, b_spec]
    args = [a, b]
    if has_add:
        in_specs.append(pl.BlockSpec((tm, tn), lambda i, j, kk: (i, j)))
        args.append(add)
    return pl.pallas_call(
        body, name=name, grid=(m // tm, n // tn, nk),
        in_specs=in_specs, out_specs=pl.BlockSpec((tm, tn), lambda i, j, kk: (i, j)),
        out_shape=jax.ShapeDtypeStruct((m, n), out_dtype),
        scratch_shapes=[pltpu.VMEM((tm, tn), F32)],
        compiler_params=_params(("parallel", "parallel", "arbitrary")),
    )(*args)


def _rmsnorm_fwd(x, w, name):
    s, d = x.shape
    tr = _tile(s, 512, 8)

    def body(x_ref, w_ref, h_ref):
        xv = x_ref[...]
        rstd = lax.rsqrt(jnp.mean(xv * xv, axis=-1, keepdims=True) + EPS)
        h_ref[...] = (xv * rstd * w_ref[...]).astype(BF16)

    return pl.pallas_call(
        body, name=name, grid=(s // tr,),
        in_specs=[pl.BlockSpec((tr, d), lambda i: (i, 0)), pl.BlockSpec((1, d), lambda i: (0, 0))],
        out_specs=pl.BlockSpec((tr, d), lambda i: (i, 0)),
        out_shape=jax.ShapeDtypeStruct((s, d), BF16),
        compiler_params=_params(("parallel",)),
    )(x, w)


def _rmsnorm_bwd(dh, x, w, res, name):
    s, d = x.shape
    tr = _tile(s, 256, 8)

    def body(dh_ref, x_ref, w_ref, res_ref, dx_ref, dw_ref):
        xv = x_ref[...]
        rstd = lax.rsqrt(jnp.mean(xv * xv, axis=-1, keepdims=True) + EPS)
        xhat = xv * rstd
        dhv = dh_ref[...]
        gw = dhv * w_ref[...]
        dx_ref[...] = res_ref[...] + rstd * (gw - xhat * jnp.mean(gw * xhat, axis=-1, keepdims=True))

        @pl.when(pl.program_id(0) == 0)
        def _():
            dw_ref[...] = jnp.zeros_like(dw_ref)

        dw_ref[...] += jnp.sum(dhv * xhat, axis=0, keepdims=True)

    row = pl.BlockSpec((tr, d), lambda i: (i, 0))
    vec = pl.BlockSpec((1, d), lambda i: (0, 0))
    return pl.pallas_call(
        body, name=name, grid=(s // tr,),
        in_specs=[row, row, vec, row], out_specs=[row, vec],
        out_shape=[jax.ShapeDtypeStruct((s, d), F32), jax.ShapeDtypeStruct((1, d), F32)],
        compiler_params=_params(("arbitrary",)),
    )(dh, x, w, res)


def _conv_taps(x, w, rows):
    shifted = [x]
    for sft in (1, 2, 3):
        shifted.append(jnp.where(rows >= sft, pltpu.roll(x, sft, 0), 0.0))
    y = w[3:4, :] * shifted[0] + w[2:3, :] * shifted[1] + w[1:2, :] * shifted[2] + w[0:1, :] * shifted[3]
    return y, shifted


def _delta_pre_fwd(qkvz, conv_w, heads):
    s = qkvz.shape[0]
    nblk = 3 * heads

    def body(x_ref, w_ref, o_ref):
        part = pl.program_id(0) // heads
        rows = lax.broadcasted_iota(jnp.int32, (s, HEAD_DIM), 0)
        y, _ = _conv_taps(x_ref[...], w_ref[...], rows)
        a = y * _sigmoid(y)
        rs = lax.rsqrt(jnp.sum(a * a, axis=-1, keepdims=True) + EPS)
        fac = jnp.where(part == 0, rs * (HEAD_DIM ** -0.5), jnp.where(part == 1, rs, 1.0))
        o_ref[...] = a * fac

    return pl.pallas_call(
        body, name="delta_pre_fwd", grid=(nblk,),
        in_specs=[pl.BlockSpec((s, HEAD_DIM), lambda i: (0, i)), pl.BlockSpec((4, HEAD_DIM), lambda i: (0, i))],
        out_specs=pl.BlockSpec((s, HEAD_DIM), lambda i: (0, i)),
        out_shape=jax.ShapeDtypeStruct((s, 3 * heads * HEAD_DIM), F32),
        compiler_params=_params(("parallel",)),
    )(qkvz, conv_w)


def _delta_pre_bwd(dqkv, qkvz, conv_w, heads):
    s = qkvz.shape[0]
    nblk = 3 * heads

    def body(d_ref, x_ref, w_ref, dx_ref, dw_ref):
        part = pl.program_id(0) // heads
        rows = lax.broadcasted_iota(jnp.int32, (s, HEAD_DIM), 0)
        w = w_ref[...]
        y, shifted = _conv_taps(x_ref[...], w, rows)
        sg = _sigmoid(y)
        a = y * sg
        rs = lax.rsqrt(jnp.sum(a * a, axis=-1, keepdims=True) + EPS)
        unit = a * rs
        dn = d_ref[...]
        scale = jnp.where(part == 0, HEAD_DIM ** -0.5, 1.0)
        da_norm = scale * rs * (dn - unit * jnp.sum(dn * unit, axis=-1, keepdims=True))
        da = jnp.where(part < 2, da_norm, dn)
        dy = da * sg * (1.0 + y * (1.0 - sg))
        dx = w[3:4, :] * dy
        for sft in (1, 2, 3):
            dx = dx + w[3 - sft:4 - sft, :] * jnp.where(rows < s - sft, pltpu.roll(dy, s - sft, 0), 0.0)
        dx_ref[...] = dx
        for sft in range(4):
            dw_ref[3 - sft:4 - sft, :] = jnp.sum(dy * shifted[sft], axis=0, keepdims=True)

    col = pl.BlockSpec((s, HEAD_DIM), lambda i: (0, i))
    wsp = pl.BlockSpec((4, HEAD_DIM), lambda i: (0, i))
    return pl.pallas_call(
        body, name="delta_pre_bwd", grid=(nblk,),
        in_specs=[col, col, wsp], out_specs=[col, wsp],
        out_shape=[jax.ShapeDtypeStruct((s, 3 * heads * HEAD_DIM), F32),
                   jax.ShapeDtypeStruct((4, 3 * heads * HEAD_DIM), F32)],
        compiler_params=_params(("parallel",)),
    )(dqkv, qkvz, conv_w)


def _heads_of(ref, heads):
    return jnp.stack([ref[:, h * HEAD_DIM:(h + 1) * HEAD_DIM] for h in range(heads)])


def _chunk_common(q, k, v, bd, a_log, dt_bias, heads):
    c = CHUNK
    braw = jnp.stack([bd[:, h:h + 1] for h in range(heads)])
    draw = jnp.stack([bd[:, heads + h:heads + h + 1] for h in range(heads)])
    beta = _sigmoid(braw)
    xd = draw + dt_bias
    sp = jnp.maximum(xd, 0.0) + jnp.log1p(jnp.exp(-jnp.abs(xd)))
    g = -jnp.exp(a_log) * sp
    row = lax.broadcasted_iota(jnp.int32, (c, c), 0)
    col = lax.broadcasted_iota(jnp.int32, (c, c), 1)
    sq = (heads, c, c)
    g_b = jnp.broadcast_to(g, sq)
    g_row = _csum(jnp.where(row == col, g_b, 0.0))
    gam_col = _rsum(jnp.where(col <= row, jnp.broadcast_to(g_row, sq), 0.0))
    gam_row = _csum(jnp.where(row <= col, g_b, 0.0))
    causal = row >= col
    dm = jnp.where(causal, jnp.exp(jnp.where(causal, gam_col - gam_row, 0.0)), 0.0)
    kk = _dot(k, k, "nt")
    low = jnp.where(row > col, beta * kk * dm, 0.0)
    t = jnp.where(row == col, 1.0, 0.0) - low
    pw = low
    for _ in range(5):
        pw = _dot(pw, pw)
        t = t + _dot(t, pw)
    e = jnp.exp(gam_col)
    u = _dot(t, beta * v)
    w = _dot(t, (beta * e) * k)
    qk_raw = _dot(q, k, "nt")
    gl = _csum(g)
    el = jnp.exp(gl - gam_col)
    return dict(beta=beta, xd=xd, g=g, row=row, col=col, dm=dm, kk=kk, t=t, e=e, u=u, w=w,
                qk_raw=qk_raw, qk=qk_raw * dm, gl=gl, el=el, qd=e * q, kd=el * k, cd=jnp.exp(gl))


def _delta_chunk_fwd(qkv, bd, a_log, dt_bias, heads):
    s = qkv.shape[0]
    n = s // CHUNK
    dw = heads * HEAD_DIM
    blk = lambda part: pl.BlockSpec((CHUNK, dw), lambda i: (i, part))

    def body(q_ref, k_ref, v_ref, bd_ref, al_ref, dt_ref, o_ref, st_ref, state):
        @pl.when(pl.program_id(0) == 0)
        def _():
            state[...] = jnp.zeros_like(state)

        cm = _chunk_common(_heads_of(q_ref, heads), _heads_of(k_ref, heads), _heads_of(v_ref, heads), bd_ref[...],
                           al_ref[...], dt_ref[...], heads)
        st = state[...]
        st_ref[0] = st
        vn = cm["u"] - _dot(cm["w"], st)
        o = _dot(cm["qd"], st) + _dot(cm["qk"], vn)
        for h in range(heads):
            o_ref[:, h * HEAD_DIM:(h + 1) * HEAD_DIM] = o[h]
        state[...] = cm["cd"] * st + _dot(cm["kd"], vn, "tn")

    smem = pl.BlockSpec((heads, 1, 1), lambda i: (0, 0, 0))
    return pl.pallas_call(
        body, name="delta_chunk_fwd", grid=(n,),
        in_specs=[blk(0), blk(1), blk(2), pl.BlockSpec((CHUNK, HEAD_DIM), lambda i: (i, 0)), smem, smem],
        out_specs=[pl.BlockSpec((CHUNK, dw), lambda i: (i, 0)),
                   pl.BlockSpec((1, heads, HEAD_DIM, HEAD_DIM), lambda i: (i, 0, 0, 0))],
        out_shape=[jax.ShapeDtypeStruct((s, dw), F32),
                   jax.ShapeDtypeStruct((n, heads, HEAD_DIM, HEAD_DIM), F32)],
        scratch_shapes=[pltpu.VMEM((heads, HEAD_DIM, HEAD_DIM), F32)],
        compiler_params=_params(("arbitrary",)),
    )(qkv, qkv, qkv, bd, a_log, dt_bias)


def _delta_chunk_bwd(do, qkv, bd, states, a_log, dt_bias, heads):
    s = qkv.shape[0]
    n = s // CHUNK
    dw = heads * HEAD_DIM
    c = CHUNK
    blk = lambda part: pl.BlockSpec((CHUNK, dw), lambda i: (n - 1 - i, part))

    def all_heads(q, k, v, dov, st, dsn, bd, a_log, dt_bias):
        cm = _chunk_common(q, k, v, bd, a_log, dt_bias, heads)
        beta, e, dm, row, col = cm["beta"], cm["e"], cm["dm"], cm["row"], cm["col"]
        sq = (heads, c, c)
        vn = cm["u"] - _dot(cm["w"], st)
        dvn = _dot(cm["kd"], dsn)
        dkd = _dot(vn, dsn, "nt")
        dcd = _csum(_rsum(st * dsn))
        ds = cm["cd"] * dsn
        dqd = _dot(dov, st, "nt")
        ds = ds + _dot(cm["qd"], dov, "tn")
        dqk = _dot(dov, vn, "nt")
        dvn = dvn + _dot(cm["qk"], dov, "tn")
        dw_ = -_dot(dvn, st, "nt")
        ds = ds - _dot(cm["w"], dvn, "tn")
        drhs_u = _dot(cm["t"], dvn, "tn")
        drhs_w = _dot(cm["t"], dw_, "tn")
        da = -(_dot(drhs_u, cm["u"], "nt") + _dot(drhs_w, cm["w"], "nt"))
        dl = jnp.where(row > col, da, 0.0)
        dbeta = _rsum(dl * cm["kk"] * dm)
        dkk = dl * beta * dm
        dd = dl * beta * cm["kk"]
        dv = beta * drhs_u
        ek = e * k
        dbeta = dbeta + _rsum(drhs_u * v) + _rsum(drhs_w * ek)
        dk = (beta * e) * drhs_w
        dgam = _rsum(drhs_w * (beta * ek))
        dqkm = dqk * dm
        dq = _dot(dqkm, k)
        dk = dk + _dot(dqkm, q, "tn")
        dd = dd + dqk * cm["qk_raw"]
        dk = dk + _dot(dkk, k) + _dot(dkk, k, "tn")
        dq = dq + e * dqd
        dgam = dgam + _rsum(dqd * cm["qd"])
        dk = dk + cm["el"] * dkd
        r = _rsum(dkd * cm["kd"])
        dgam = dgam - r
        dgl = _csum(r) + dcd * cm["cd"]
        mm = dd * dm
        colsum_c = _rsum(jnp.where(row == col, jnp.broadcast_to(_csum(mm), sq), 0.0))
        dgam = dgam + _rsum(mm) - colsum_c
        ridx = lax.broadcasted_iota(jnp.int32, (c, 1), 0)
        dgam = dgam + jnp.where(ridx == c - 1, dgl, 0.0)
        dgam_row = _csum(jnp.where(row == col, jnp.broadcast_to(dgam, sq), 0.0))
        dg = _rsum(jnp.where(col >= row, jnp.broadcast_to(dgam_row, sq), 0.0))
        d_xd = dg * (-jnp.exp(a_log)) * _sigmoid(cm["xd"])
        d_braw = dbeta * beta * (1.0 - beta)
        d_alog = dg * cm["g"]
        lane = lax.broadcasted_iota(jnp.int32, (c, HEAD_DIM), 1)
        dbd = jnp.zeros((c, HEAD_DIM), F32)
        for h in range(heads):
            dbd = (dbd + jnp.where(lane == h, d_braw[h], 0.0) + jnp.where(lane == h + heads, d_xd[h], 0.0)
                   + jnp.where(lane == h + 2 * heads, d_alog[h], 0.0))
        return dq, dk, dv, ds, dbd

    def body(do_ref, q_ref, k_ref, v_ref, bd_ref, st_ref, al_ref, dt_ref, dq_ref, dk_ref, dv_ref, dbd_ref, dstate):
        @pl.when(pl.program_id(0) == 0)
        def _():
            dstate[...] = jnp.zeros_like(dstate)

        dq, dk, dv, ds, dbd = all_heads(_heads_of(q_ref, heads), _heads_of(k_ref, heads), _heads_of(v_ref, heads),
                                        _heads_of(do_ref, heads), st_ref[0], dstate[...], bd_ref[...],
                                        al_ref[...], dt_ref[...])
        for h in range(heads):
            hs = slice(h * HEAD_DIM, (h + 1) * HEAD_DIM)
            dq_ref[:, hs] = dq[h]
            dk_ref[:, hs] = dk[h]
            dv_ref[:, hs] = dv[h]
        dstate[...] = ds
        dbd_ref[...] = dbd

    smem = pl.BlockSpec((heads, 1, 1), lambda i: (0, 0, 0))
    shared = pl.BlockSpec((CHUNK, HEAD_DIM), lambda i: (n - 1 - i, 0))
    wide = pl.BlockSpec((CHUNK, dw), lambda i: (n - 1 - i, 0))
    dq, dk, dv, dbd = pl.pallas_call(
        body, name="delta_chunk_bwd", grid=(n,),
        in_specs=[wide, blk(0), blk(1), blk(2), shared,
                  pl.BlockSpec((1, heads, HEAD_DIM, HEAD_DIM), lambda i: (n - 1 - i, 0, 0, 0)), smem, smem],
        out_specs=[wide, wide, wide, shared],
        out_shape=[jax.ShapeDtypeStruct((s, dw), F32)] * 3 + [jax.ShapeDtypeStruct((s, HEAD_DIM), F32)],
        scratch_shapes=[pltpu.VMEM((heads, HEAD_DIM, HEAD_DIM), F32)],
        compiler_params=_params(("arbitrary",)),
    )(do, qkv, qkv, qkv, bd, states, a_log, dt_bias)
    return dq, dk, dv, dbd


def _delta_post_fwd(o, qkvz, w, heads):
    s = o.shape[0]
    tr = _tile(s, 1024, 8)

    def body(o_ref, z_ref, w_ref, out_ref):
        ov, z = o_ref[...], z_ref[...]
        rstd = lax.rsqrt(jnp.mean(ov * ov, axis=-1, keepdims=True) + EPS)
        out_ref[...] = (ov * rstd * w_ref[...] * (z * _sigmoid(z))).astype(BF16)

    return pl.pallas_call(
        body, name="delta_post_fwd", grid=(s // tr, heads),
        in_specs=[pl.BlockSpec((tr, HEAD_DIM), lambda i, h: (i, h)),
                  pl.BlockSpec((tr, HEAD_DIM), lambda i, h: (i, 3 * heads + h)),
                  pl.BlockSpec((1, HEAD_DIM), lambda i, h: (0, 0))],
        out_specs=pl.BlockSpec((tr, HEAD_DIM), lambda i, h: (i, h)),
        out_shape=jax.ShapeDtypeStruct((s, heads * HEAD_DIM), BF16),
        compiler_params=_params(("parallel", "parallel")),
    )(o, qkvz, w)


def _delta_post_bwd(dmix, o, qkvz, w, heads):
    s = o.shape[0]
    tr = _tile(s, 1024, 8)

    def body(d_ref, o_ref, z_ref, w_ref, do_ref, dz_ref, dw_ref):
        d, ov, z, wv = d_ref[...], o_ref[...], z_ref[...], w_ref[...]
        sg = _sigmoid(z)
        rstd = lax.rsqrt(jnp.mean(ov * ov, axis=-1, keepdims=True) + EPS)
        ohat = ov * rstd
        dz_ref[...] = d * (ohat * wv) * sg * (1.0 + z * (1.0 - sg))
        dn = d * (z * sg)
        gw = dn * wv
        do_ref[...] = rstd * (gw - ohat * jnp.mean(gw * ohat, axis=-1, keepdims=True))

        @pl.when((pl.program_id(0) == 0) & (pl.program_id(1) == 0))
        def _():
            dw_ref[...] = jnp.zeros_like(dw_ref)

        dw_ref[...] += jnp.sum(dn * ohat, axis=0, keepdims=True)

    head = pl.BlockSpec((tr, HEAD_DIM), lambda i, h: (i, h))
    vec = pl.BlockSpec((1, HEAD_DIM), lambda i, h: (0, 0))
    dw = heads * HEAD_DIM
    return pl.pallas_call(
        body, name="delta_post_bwd", grid=(s // tr, heads),
        in_specs=[head, head, pl.BlockSpec((tr, HEAD_DIM), lambda i, h: (i, 3 * heads + h)), vec],
        out_specs=[head, head, vec],
        out_shape=[jax.ShapeDtypeStruct((s, dw), F32), jax.ShapeDtypeStruct((s, dw), F32),
                   jax.ShapeDtypeStruct((1, HEAD_DIM), F32)],
        compiler_params=_params(("arbitrary", "arbitrary")),
    )(dmix, o, qkvz, w)


def _rope_tables(positions, s):
    half = HEAD_DIM // 2
    inv_freq = ROPE_THETA ** (-jnp.arange(half, dtype=F32) / half)
    ang = positions.reshape(s, 1).astype(F32) * inv_freq
    cos, sin = jnp.cos(ang), jnp.sin(ang)
    return jnp.concatenate([cos, cos], axis=-1), jnp.concatenate([-sin, sin], axis=-1)


def _attn_pre_fwd(aqkv, wq, wk, cosf, sinf, heads):
    s = aqkv.shape[0]
    tr = _tile(s, 1024, 8)

    def body(x_ref, wq_ref, wk_ref, c_ref, s_ref, o_ref):
        xv = x_ref[...]
        wv = jnp.where(pl.program_id(1) < heads, wq_ref[...], wk_ref[...])
        y = xv * lax.rsqrt(jnp.mean(xv * xv, axis=-1, keepdims=True) + EPS) * wv
        o_ref[...] = y * c_ref[...] + pltpu.roll(y, HEAD_DIM // 2, 1) * s_ref[...]

    blk = pl.BlockSpec((tr, HEAD_DIM), lambda i, j: (i, j))
    vec = pl.BlockSpec((1, HEAD_DIM), lambda i, j: (0, 0))
    tab = pl.BlockSpec((tr, HEAD_DIM), lambda i, j: (i, 0))
    return pl.pallas_call(
        body, name="attn_pre_fwd", grid=(s // tr, 2 * heads),
        in_specs=[blk, vec, vec, tab, tab], out_specs=blk,
        out_shape=jax.ShapeDtypeStruct((s, 2 * heads * HEAD_DIM), F32),
        compiler_params=_params(("parallel", "parallel")),
    )(aqkv, wq, wk, cosf, sinf)


ATT_ROWS = 2048


def _band():
    qi = lax.broadcasted_iota(jnp.int32, (SPAN, 2 * SPAN), 0)
    ki = lax.broadcasted_iota(jnp.int32, (SPAN, 2 * SPAN), 1)
    dist = qi + SPAN - ki
    return (dist >= 0) & (dist <= SPAN), ki >= SPAN


def _sub(g, r, d):
    if d == 1:
        return pl.ds(g * SPAN, SPAN)
    return pl.ds(g * SPAN * d + r, SPAN, stride=d)


def _attn_geometry(s, d):
    rb = min(ATT_ROWS, s)
    pr = SPAN * d
    return rb, pr, rb // pr, s // rb


def _attn_fwd(qk, aqkv, heads, d):
    s = qk.shape[0]
    aw = heads * HEAD_DIM
    rb, pr, g_cnt, nstep = _attn_geometry(s, d)

    def body(q_ref, k_ref, v_ref, kp_ref, vp_ref, o_ref, l_ref):
        first = pl.program_id(1) == 0
        band, own = _band()
        edge = band & (own | jnp.logical_not(first))
        for r in range(d):
            for g in range(g_cnt):
                rows = _sub(g, r, d)
                if g == 0:
                    kp, vp, mask = kp_ref[_sub(0, r, d), :], vp_ref[_sub(0, r, d), :], edge
                else:
                    kp, vp, mask = k_ref[_sub(g - 1, r, d), :], v_ref[_sub(g - 1, r, d), :], band
                kcat = jnp.concatenate([kp, k_ref[rows, :]], axis=0)
                vcat = jnp.concatenate([vp, v_ref[rows, :]], axis=0)
                sc = _dot(q_ref[rows, :], kcat, "nt") * (HEAD_DIM ** -0.5)
                sc = jnp.where(mask, sc, NEG)
                m = jnp.max(sc, axis=-1, keepdims=True)
                p = jnp.exp(sc - m)
                den = jnp.sum(p, axis=-1, keepdims=True)
                o_ref[rows, :] = _dot(p, vcat) / den
                l_ref[rows, :] = jnp.broadcast_to(m + jnp.log(den), (SPAN, HEAD_DIM))

    cur = lambda off: pl.BlockSpec((rb, HEAD_DIM), lambda h, j: (j, off + h))
    prev = lambda off: pl.BlockSpec((pr, HEAD_DIM), lambda h, j: (jnp.maximum(j * (rb // pr) - 1, 0), off + h))
    out = pl.BlockSpec((rb, HEAD_DIM), lambda h, j: (j, h))
    return pl.pallas_call(
        body, name=f"attn_fwd_d{d}", grid=(heads, nstep),
        in_specs=[cur(0), cur(heads), cur(2 * heads), prev(heads), prev(2 * heads)],
        out_specs=[out, out],
        out_shape=[jax.ShapeDtypeStruct((s, aw), F32)] * 2,
        compiler_params=_params(("parallel", "parallel")),
    )(qk, qk, aqkv, qk, aqkv)


def _attn_merge_fwd(os_, ls_, w, heads):
    s = os_[0].shape[0]
    tr = _tile(s, 512, 8)

    def body(o1, o2, o3, l1, l2, l3, w_ref, mix_ref, ob_ref, lse_ref):
        la, lb, lc = l1[...], l2[...], l3[...]
        m = jnp.maximum(jnp.maximum(la, lb), lc)
        ea, eb, ec = jnp.exp(la - m), jnp.exp(lb - m), jnp.exp(lc - m)
        tot = ea + eb + ec
        ob = (ea * o1[...] + eb * o2[...] + ec * o3[...]) / tot
        ob_ref[...] = ob
        lse_ref[...] = m + jnp.log(tot)
        rstd = lax.rsqrt(jnp.mean(ob * ob, axis=-1, keepdims=True) + EPS)
        mix_ref[...] = (ob * rstd * w_ref[...]).astype(BF16)

    blk = pl.BlockSpec((tr, HEAD_DIM), lambda i, h: (i, h))
    vec = pl.BlockSpec((1, HEAD_DIM), lambda i, h: (0, 0))
    aw = heads * HEAD_DIM
    return pl.pallas_call(
        body, name="attn_merge_fwd", grid=(s // tr, heads),
        in_specs=[blk] * 6 + [vec], out_specs=[blk, blk, blk],
        out_shape=[jax.ShapeDtypeStruct((s, aw), BF16), jax.ShapeDtypeStruct((s, aw), F32),
                   jax.ShapeDtypeStruct((s, aw), F32)],
        compiler_params=_params(("parallel", "parallel")),
    )(*os_, *ls_, w)


def _attn_merge_bwd(dmix, ob, w, heads):
    s = ob.shape[0]
    tr = _tile(s, 1024, 8)

    def body(d_ref, ob_ref, w_ref, do_ref, dsum_ref, dw_ref):
        d, ov = d_ref[...], ob_ref[...]
        rstd = lax.rsqrt(jnp.mean(ov * ov, axis=-1, keepdims=True) + EPS)
        ohat = ov * rstd
        gw = d * w_ref[...]
        dov = rstd * (gw - ohat * jnp.mean(gw * ohat, axis=-1, keepdims=True))
        do_ref[...] = dov
        dsum_ref[...] = jnp.broadcast_to(jnp.sum(dov * ov, axis=-1, keepdims=True), dov.shape)

        @pl.when((pl.program_id(0) == 0) & (pl.program_id(1) == 0))
        def _():
            dw_ref[...] = jnp.zeros_like(dw_ref)

        dw_ref[...] += jnp.sum(d * ohat, axis=0, keepdims=True)

    blk = pl.BlockSpec((tr, HEAD_DIM), lambda i, h: (i, h))
    vec = pl.BlockSpec((1, HEAD_DIM), lambda i, h: (0, 0))
    aw = heads * HEAD_DIM
    return pl.pallas_call(
        body, name="attn_merge_bwd", grid=(s // tr, heads),
        in_specs=[pl.BlockSpec((tr, HEAD_DIM), lambda i, h: (i, heads + h)), blk, vec],
        out_specs=[blk, blk, vec],
        out_shape=[jax.ShapeDtypeStruct((s, aw), F32), jax.ShapeDtypeStruct((s, aw), F32),
                   jax.ShapeDtypeStruct((1, HEAD_DIM), F32)],
        compiler_params=_params(("arbitrary", "arbitrary")),
    )(dmix, ob, w)


def _attn_bwd(qk, aqkv, do, lse, dsum, heads, d):
    s = qk.shape[0]
    aw = heads * HEAD_DIM
    rb, pr, g_cnt, nstep = _attn_geometry(s, d)
    scale = HEAD_DIM ** -0.5

    def body(q_ref, k_ref, v_ref, do_ref, l_ref, ds_ref, kp_ref, vp_ref, qn_ref, don_ref, ln_ref, dsn_ref, out_ref):
        first = pl.program_id(1) == 0
        last = pl.program_id(1) == nstep - 1
        band, own = _band()
        edge = band & (own | jnp.logical_not(first))
        dq_ref, dk_ref, dv_ref = out_ref.at[0], out_ref.at[1], out_ref.at[2]
        dk_ref[...] = jnp.zeros((rb, HEAD_DIM), F32)
        dv_ref[...] = jnp.zeros((rb, HEAD_DIM), F32)
        for r in range(d):
            for g in range(g_cnt):
                rows = _sub(g, r, d)
                if g == 0:
                    kp, vp, mask = kp_ref[_sub(0, r, d), :], vp_ref[_sub(0, r, d), :], edge
                else:
                    prows = _sub(g - 1, r, d)
                    kp, vp, mask = k_ref[prows, :], v_ref[prows, :], band
                q, dov = q_ref[rows, :], do_ref[rows, :]
                kcat = jnp.concatenate([kp, k_ref[rows, :]], axis=0)
                vcat = jnp.concatenate([vp, v_ref[rows, :]], axis=0)
                sc = _dot(q, kcat, "nt") * scale
                p = jnp.where(mask, jnp.exp(sc - l_ref[rows, 0:1]), 0.0)
                dsc = p * (_dot(dov, vcat, "nt") - ds_ref[rows, 0:1]) * scale
                dq_ref[rows, :] = _dot(dsc, kcat)
                dk = _dot(dsc, q, "tn")
                dv = _dot(p, dov, "tn")
                dk_ref[rows, :] += dk[SPAN:]
                dv_ref[rows, :] += dv[SPAN:]
                if g > 0:
                    dk_ref[prows, :] += dk[:SPAN]
                    dv_ref[prows, :] += dv[:SPAN]
            rows, nrows = _sub(g_cnt - 1, r, d), _sub(0, r, d)
            q, dov = qn_ref[nrows, :], don_ref[nrows, :]
            sc = _dot(q, k_ref[rows, :], "nt") * scale
            p = jnp.where(band[:, :SPAN] & jnp.logical_not(last), jnp.exp(sc - ln_ref[nrows, 0:1]), 0.0)
            dsc = p * (_dot(dov, v_ref[rows, :], "nt") - dsn_ref[nrows, 0:1]) * scale
            dk_ref[rows, :] += _dot(dsc, q, "tn")
            dv_ref[rows, :] += _dot(p, dov, "tn")

    per = rb // pr
    cur = lambda off: pl.BlockSpec((rb, HEAD_DIM), lambda h, j: (j, off + h))
    prev = lambda off: pl.BlockSpec((pr, HEAD_DIM), lambda h, j: (jnp.maximum(j * per - 1, 0), off + h))
    nxt = lambda off: pl.BlockSpec((pr, HEAD_DIM), lambda h, j: (jnp.minimum((j + 1) * per, s // pr - 1), off + h))
    return pl.pallas_call(
        body, name=f"attn_bwd_d{d}", grid=(heads, nstep),
        in_specs=[cur(0), cur(heads), cur(2 * heads), cur(0), cur(0), cur(0), prev(heads), prev(2 * heads),
                  nxt(0), nxt(0), nxt(0), nxt(0)],
        out_specs=pl.BlockSpec((3, rb, HEAD_DIM), lambda h, j: (0, j, h)),
        out_shape=jax.ShapeDtypeStruct((3, s, aw), F32),
        compiler_params=_params(("parallel", "parallel")),
    )(qk, qk, aqkv, do, lse, dsum, qk, aqkv, qk, do, lse, dsum)


def _attn_pre_bwd(grads, aqkv, wq, wk, cosf, sinf, heads):
    s = aqkv.shape[0]
    tr = _tile(s, 512, 8)
    nrow = s // tr

    def body(g1_ref, g2_ref, g3_ref, x_ref, wq_ref, wk_ref, c_ref, s_ref, dx_ref, dwq_ref, dwk_ref):
        i, j = pl.program_id(0), pl.program_id(1)
        kind = j // heads
        dout = g1_ref[0] + g2_ref[0] + g3_ref[0]
        tot_v = dout
        dy = dout * c_ref[...] + pltpu.roll(dout * s_ref[...], HEAD_DIM // 2, 1)
        xv = x_ref[...]
        wv = jnp.where(kind == 0, wq_ref[...], wk_ref[...])
        rstd = lax.rsqrt(jnp.mean(xv * xv, axis=-1, keepdims=True) + EPS)
        xhat = xv * rstd
        gw = dy * wv
        dxn = rstd * (gw - xhat * jnp.mean(gw * xhat, axis=-1, keepdims=True))
        dx_ref[...] = jnp.where(kind == 2, tot_v, dxn)
        dwc = jnp.sum(dy * xhat, axis=0, keepdims=True)

        @pl.when((i == 0) & (j == 0))
        def _():
            dwq_ref[...] = jnp.zeros_like(dwq_ref)
            dwk_ref[...] = jnp.zeros_like(dwk_ref)

        @pl.when(kind == 0)
        def _():
            dwq_ref[...] += dwc

        @pl.when(kind == 1)
        def _():
            dwk_ref[...] += dwc

    grad = pl.BlockSpec((1, tr, HEAD_DIM), lambda i, j: (j // heads, i, j % heads))
    blk = pl.BlockSpec((tr, HEAD_DIM), lambda i, j: (i, j))
    vec = pl.BlockSpec((1, HEAD_DIM), lambda i, j: (0, 0))
    tab = pl.BlockSpec((tr, HEAD_DIM), lambda i, j: (i, 0))
    return pl.pallas_call(
        body, name="attn_pre_bwd", grid=(nrow, 3 * heads),
        in_specs=[grad, grad, grad, blk, vec, vec, tab, tab], out_specs=[blk, vec, vec],
        out_shape=[jax.ShapeDtypeStruct((s, 3 * heads * HEAD_DIM), F32),
                   jax.ShapeDtypeStruct((1, HEAD_DIM), F32), jax.ShapeDtypeStruct((1, HEAD_DIM), F32)],
        compiler_params=_params(("arbitrary", "arbitrary")),
    )(*grads, aqkv, wq, wk, cosf, sinf)


def _swiglu_fwd(gu, ff):
    s = gu.shape[0]
    tr, tc = _tile(s, 512, 8), _tile(ff, 1024, 128)
    nc = ff // tc

    def body(g_ref, u_ref, o_ref):
        g = g_ref[...]
        o_ref[...] = (g * _sigmoid(g) * u_ref[...]).astype(BF16)

    return pl.pallas_call(
        body, name="swiglu_fwd", grid=(s // tr, nc),
        in_specs=[pl.BlockSpec((tr, tc), lambda i, j: (i, j)), pl.BlockSpec((tr, tc), lambda i, j: (i, j + nc))],
        out_specs=pl.BlockSpec((tr, tc), lambda i, j: (i, j)),
        out_shape=jax.ShapeDtypeStruct((s, ff), BF16),
        compiler_params=_params(("parallel", "parallel")),
    )(gu, gu)


def _swiglu_bwd(dact, gu, ff):
    s = gu.shape[0]
    tr, tc = _tile(s, 512, 8), _tile(ff, 1024, 128)
    nc = ff // tc

    def body(d_ref, g_ref, u_ref, dg_ref, du_ref):
        d, g = d_ref[...], g_ref[...]
        sg = _sigmoid(g)
        dg_ref[...] = (d * u_ref[...] * sg * (1.0 + g * (1.0 - sg))).astype(BF16)
        du_ref[...] = (d * g * sg).astype(BF16)

    lo = pl.BlockSpec((tr, tc), lambda i, j: (i, j))
    hi = pl.BlockSpec((tr, tc), lambda i, j: (i, j + nc))
    dg, du = pl.pallas_call(
        body, name="swiglu_bwd", grid=(s // tr, nc),
        in_specs=[lo, lo, hi], out_specs=[lo, lo],
        out_shape=[jax.ShapeDtypeStruct((s, ff), BF16)] * 2,
        compiler_params=_params(("parallel", "parallel")),
    )(dact, gu, gu)
    return dg, du


def _loss_head(y, target):
    s, d = y.shape
    tr = _tile(s, 512, 8)

    def body(y_ref, t_ref, dy_ref, l_ref):
        err = y_ref[...] - t_ref[...]
        dy_ref[...] = err * (1.0 / d)
        part = 0.5 * jnp.sum(jnp.sum(err * err, axis=-1, keepdims=True) * (1.0 / d), axis=0, keepdims=True)

        @pl.when(pl.program_id(0) == 0)
        def _():
            l_ref[...] = jnp.zeros_like(l_ref)

        lane = lax.broadcasted_iota(jnp.int32, (1, HEAD_DIM), 1)
        l_ref[...] += jnp.where(lane == 0, part, 0.0)

    row = pl.BlockSpec((tr, d), lambda i: (i, 0))
    return pl.pallas_call(
        body, name="loss_head", grid=(s // tr,),
        in_specs=[row, row], out_specs=[row, pl.BlockSpec((1, HEAD_DIM), lambda i: (0, 0))],
        out_shape=[jax.ShapeDtypeStruct((s, d), F32), jax.ShapeDtypeStruct((1, HEAD_DIM), F32)],
        compiler_params=_params(("arbitrary",)),
    )(y, target)


def _colsum(a, name):
    s, d = a.shape
    tr = _tile(s, 1024, 8)

    def body(a_ref, o_ref):
        @pl.when(pl.program_id(0) == 0)
        def _():
            o_ref[...] = jnp.zeros_like(o_ref)

        o_ref[...] += jnp.sum(a_ref[...], axis=0, keepdims=True)

    return pl.pallas_call(
        body, name=name, grid=(s // tr,),
        in_specs=[pl.BlockSpec((tr, d), lambda i: (i, 0))], out_specs=pl.BlockSpec((1, d), lambda i: (0, 0)),
        out_shape=jax.ShapeDtypeStruct((1, d), F32),
        compiler_params=_params(("arbitrary",)),
    )(a)


def _local_step(x, positions, target, small, w_qkvz, w_bd, w_attn, w_out_a, w_out_b, w_gu, w_down, conv_w):
    s, dmod = x.shape
    heads = dmod // (2 * HEAD_DIM)
    dw = heads * HEAD_DIM
    ff = w_down.shape[0]
    a_log, dt_bias = small["a_log"].reshape(heads, 1, 1), small["dt_bias"].reshape(heads, 1, 1)
    cosf, sinf = _rope_tables(positions, s)

    h1 = _rmsnorm_fwd(x, small["attn_norm_w"], "norm1_fwd")
    qkvz = _matmul(h1, w_qkvz, "nn", "proj_qkvz")
    bd = _matmul(h1, w_bd, "nn", "proj_bd")
    aqkv = _matmul(h1, w_attn, "nn", "proj_attn")
    dqkv = _delta_pre_fwd(qkvz, conv_w, heads)
    o_d, states = _delta_chunk_fwd(dqkv, bd, a_log, dt_bias, heads)
    mix_a = _delta_post_fwd(o_d, qkvz, small["delta_out_norm_w"], heads)
    qk_rot = _attn_pre_fwd(aqkv, small["q_norm_w"], small["k_norm_w"], cosf, sinf, heads)
    outs = [_attn_fwd(qk_rot, aqkv, heads, d) for d in DILATIONS]
    mix_b, ob, lse = _attn_merge_fwd([o for o, _ in outs], [l for _, l in outs], small["attn_out_norm_w"], heads)
    x1 = _matmul(mix_a, w_out_a, "nn", "out_proj_a", add=x)
    x1 = _matmul(mix_b, w_out_b, "nn", "out_proj_b", add=x1)
    h2 = _rmsnorm_fwd(x1, small["ffn_norm_w"], "norm2_fwd")
    gu = _matmul(h2, w_gu, "nn", "ffn_gate_up")
    act = _swiglu_fwd(gu, ff)
    y = _matmul(act, w_down, "nn", "ffn_down", add=x1)
    dy, loss_row = _loss_head(y, target)

    dact = _matmul(dy, w_down, "nt", "ffn_down_dx")
    g_w_down = _matmul(act, dy, "tn", "ffn_down_dw")
    dgate, dup = _swiglu_bwd(dact, gu, ff)
    dh2 = _matmul(dgate, w_gu, "nt", "ffn_gate_dx", b_cols=(0, ff))
    dh2 = _matmul(dup, w_gu, "nt", "ffn_up_dx", b_cols=(ff, ff), add=dh2)
    g_w_gate = _matmul(h2, dgate, "tn", "ffn_gate_dw")
    g_w_up = _matmul(h2, dup, "tn", "ffn_up_dw")
    dx1, g_ffn_norm = _rmsnorm_bwd(dh2, x1, small["ffn_norm_w"], dy, "norm2_bwd")
    dmix = _matmul(dx1, w_out_a, "nt", "out_proj_dx_a")
    dmix_b = _matmul(dx1, w_out_b, "nt", "out_proj_dx_b")
    dmix = jnp.concatenate([dmix, dmix_b], axis=1)
    g_w_out_a = _matmul(mix_a, dx1, "tn", "out_proj_dw_a")
    g_w_out_b = _matmul(mix_b, dx1, "tn", "out_proj_dw_b")
    dob, dsum, g_attn_out_norm = _attn_merge_bwd(dmix, ob, small["attn_out_norm_w"], heads)
    grads = [_attn_bwd(qk_rot, aqkv, dob, lse, dsum, heads, d) for d in DILATIONS]
    d_aqkv, g_q_norm, g_k_norm = _attn_pre_bwd(grads, aqkv, small["q_norm_w"], small["k_norm_w"], cosf, sinf, heads)
    do_d, dz, g_delta_out_norm = _delta_post_bwd(dmix, o_d, qkvz, small["delta_out_norm_w"], heads)
    ddq, ddk, ddv, dbd = _delta_chunk_bwd(do_d, dqkv, bd, states, a_log, dt_bias, heads)
    d_qkv_raw, g_conv = _delta_pre_bwd(jnp.concatenate([ddq, ddk, ddv], axis=1), qkvz, conv_w, heads)
    d_qkvz = jnp.concatenate([d_qkv_raw, dz], axis=1)
    bd_sums = _colsum(dbd, "bd_colsum")
    dh1 = _matmul(d_qkvz, w_qkvz, "nt", "proj_qkvz_dx")
    dh1 = _matmul(d_aqkv, w_attn, "nt", "proj_attn_dx", add=dh1)
    dh1 = _matmul(dbd, w_bd, "nt", "proj_bd_dx", add=dh1)
    g_w_qkvz = _matmul(h1, d_qkvz, "tn", "proj_qkvz_dw")
    g_w_bd = _matmul(h1, dbd, "tn", "proj_bd_dw")
    g_w_attn = _matmul(h1, d_aqkv, "tn", "proj_attn_dw")
    grad_x, g_attn_norm = _rmsnorm_bwd(dh1, x, small["attn_norm_w"], dx1, "norm1_bwd")
    small_grads = dict(
        attn_norm_w=g_attn_norm, a_log=bd_sums[:, 2 * heads:3 * heads], dt_bias=bd_sums[:, heads:2 * heads],
        delta_out_norm_w=g_delta_out_norm, q_norm_w=g_q_norm, k_norm_w=g_k_norm,
        attn_out_norm_w=g_attn_out_norm, ffn_norm_w=g_ffn_norm, conv_w=g_conv)
    big_grads = dict(w_qkvz=g_w_qkvz, w_bd=g_w_bd, w_attn=g_w_attn, w_out_a=g_w_out_a, w_out_b=g_w_out_b,
                     w_gate=g_w_gate, w_up=g_w_up, w_down=g_w_down)
    return loss_row, grad_x, small_grads, big_grads


def _adamw(w, g, m, v, name):
    r, c = w.shape
    tr = _tile(r, 256, 8)

    def body(w_ref, g_ref, m_ref, v_ref, d_ref, nm_ref, nv_ref):
        gv = g_ref[...]
        nm = ADAM_B1 * m_ref[...] + (1.0 - ADAM_B1) * gv
        nv = ADAM_B2 * v_ref[...] + (1.0 - ADAM_B2) * (gv * gv)
        m_hat = nm / (1.0 - ADAM_B1 ** ADAM_STEP)
        v_hat = nv / (1.0 - ADAM_B2 ** ADAM_STEP)
        d_ref[...] = -ADAM_LR * (m_hat / (jnp.sqrt(v_hat) + ADAM_EPS) + ADAM_WD * w_ref[...])
        nm_ref[...] = nm
        nv_ref[...] = nv

    blk = pl.BlockSpec((tr, c), lambda i: (i, 0))
    return pl.pallas_call(
        body, name=name, grid=(r // tr,),
        in_specs=[blk] * 4, out_specs=[blk] * 3,
        out_shape=[jax.ShapeDtypeStruct((r, c), F32)] * 3,
        compiler_params=_params(("parallel",)),
    )(w, g, m, v)


def _add_half_bf16(g, b, place, name):
    n, half, c = b.shape
    tr = _tile(half, 512, 16)
    nb = half // tr

    def body(place_ref, g_ref, b_ref, o_ref):
        o_ref[...] = (g_ref[...].astype(F32) + b_ref[...].astype(F32)).astype(BF16)

    blk = pl.BlockSpec((1, tr, c), lambda i, j, p: (i, j, 0))
    return pl.pallas_call(
        body, name=name,
        grid_spec=pltpu.PrefetchScalarGridSpec(
            num_scalar_prefetch=1, grid=(n, nb),
            in_specs=[pl.BlockSpec((1, tr, c), lambda i, j, p: (i, p[0] * nb + j, 0)), blk], out_specs=blk),
        out_shape=jax.ShapeDtypeStruct((n, half, c), BF16),
        compiler_params=_params(("parallel", "parallel")),
    )(place, g, b)


def _sum4_f32(mine, others, place, name):
    _, half, c = mine.shape
    tr = _tile(half, 512, 16)
    nb = half // tr

    def body(place_ref, a_ref, b_ref, o_ref):
        acc = a_ref[0].astype(F32)
        for j in range(3):
            acc = acc + b_ref[j].astype(F32)
        o_ref[...] = acc

    return pl.pallas_call(
        body, name=name,
        grid_spec=pltpu.PrefetchScalarGridSpec(
            num_scalar_prefetch=1, grid=(nb,),
            in_specs=[pl.BlockSpec((1, tr, c), lambda i, p: (p[1], i, 0)),
                      pl.BlockSpec((3, tr, c), lambda i, p: (0, i, 0))],
            out_specs=pl.BlockSpec((tr, c), lambda i, p: (p[0] * nb + i, 0))),
        out_shape=jax.ShapeDtypeStruct((2 * half, c), F32),
        compiler_params=_params(("parallel",)),
    )(place, mine, others)


def _place():
    x, y, c = lax.axis_index("x"), lax.axis_index("y"), lax.axis_index("c")
    other_chips = [(1 - x, y), (x, 1 - y), (1 - x, 1 - y)]
    return x, y, c, (x, y, 1 - c), other_chips


ANY = pl.BlockSpec(memory_space=pl.ANY)


def _remote(k, src, dst, to, send_sems, recv_sems):
    return pltpu.make_async_remote_copy(src_ref=src, dst_ref=dst, send_sem=send_sems.at[k], recv_sem=recv_sems.at[k],
                                        device_id=to, device_id_type=MESH)


def _half(ref, lead, hc):
    if lead is None:
        half = ref.shape[0] // 2
        return ref.at[pl.ds(hc * half, half), :]
    half = ref.shape[1] // 2
    return ref.at[lead, pl.ds(hc * half, half), :]


def _all_gather_weights(slots, whole, name):
    nt, nw = len(slots), len(whole)
    n_ici = 3 * (nt + nw)

    def body(*refs):
        ins, outs = refs[:nt + nw], refs[nt + nw:2 * (nt + nw)]
        send_sems, recv_sems = refs[2 * (nt + nw):]
        x, y, c, sibling, chips = _place()
        me = 2 * x + y
        first = []
        for j, (px, py) in enumerate(chips):
            for t in range(nt):
                first.append(_remote(j * (nt + nw) + t, _half(ins[t], me, c), _half(outs[t], me, c), (px, py, c),
                                     send_sems, recv_sems))
            for t in range(nt, nt + nw):
                first.append(_remote(j * (nt + nw) + t, ins[t].at[me], outs[t].at[me], (px, py, c),
                                     send_sems, recv_sems))
        for cp in first:
            cp.start()
        passed = []
        for j, (px, py) in enumerate(chips):
            for t in range(nt):
                landed = _half(outs[t], 2 * px + py, c)
                _remote(j * (nt + nw) + t, landed, landed, (px, py, c), send_sems, recv_sems).wait_recv()
                fw = _remote(n_ici + j * nt + t, landed, landed, sibling, send_sems, recv_sems)
                fw.start()
                passed.append(fw)
            for t in range(nt, nt + nw):
                landed = outs[t].at[2 * px + py]
                _remote(j * (nt + nw) + t, landed, landed, (px, py, c), send_sems, recv_sems).wait_recv()
        for j, (px, py) in enumerate(chips):
            for t in range(nt):
                other = _half(outs[t], 2 * px + py, 1 - c)
                _remote(n_ici + j * nt + t, other, other, sibling, send_sems, recv_sems).wait_recv()
        for cp in first + passed:
            cp.wait_send()

    arrays = list(slots) + list(whole)
    n_sem = n_ici + 3 * nt
    return pl.pallas_call(
        body, name=name, in_specs=[ANY] * len(arrays), out_specs=[ANY] * len(arrays),
        input_output_aliases={i: i for i in range(len(arrays))},
        out_shape=[jax.ShapeDtypeStruct(a.shape, a.dtype) for a in arrays],
        scratch_shapes=[pltpu.SemaphoreType.DMA((n_sem,)), pltpu.SemaphoreType.DMA((n_sem,))],
    )(*arrays)


def _swap_halves_with_sibling(gs, name):
    nt = len(gs)

    def body(*refs):
        g_refs, o_refs, (send_sems, recv_sems) = refs[:nt], refs[nt:2 * nt], refs[2 * nt:]
        _, _, c, sibling, _ = _place()
        cps = []
        for t in range(nt):
            half = g_refs[t].shape[1] // 2
            cps.append(_remote(t, g_refs[t].at[:, pl.ds((1 - c) * half, half), :], o_refs[t], sibling,
                               send_sems, recv_sems))
        for cp in cps:
            cp.start()
        for cp in cps:
            cp.wait()

    return pl.pallas_call(
        body, name=name, in_specs=[ANY] * nt, out_specs=[ANY] * nt,
        out_shape=[jax.ShapeDtypeStruct((g.shape[0], g.shape[1] // 2, g.shape[2]), g.dtype) for g in gs],
        scratch_shapes=[pltpu.SemaphoreType.DMA((nt,)), pltpu.SemaphoreType.DMA((nt,))],
    )(*gs)


def _scatter_to_chips(ps, name):
    nt = len(ps)

    def body(*refs):
        p_refs, b_refs, (send_sems, recv_sems) = refs[:nt], refs[nt:2 * nt], refs[2 * nt:]
        _, _, c, _, chips = _place()
        cps = [_remote(j * nt + t, p_refs[t].at[2 * px + py], b_refs[t].at[j], (px, py, c), send_sems, recv_sems)
               for j, (px, py) in enumerate(chips) for t in range(nt)]
        for cp in cps:
            cp.start()
        for cp in cps:
            cp.wait()

    return pl.pallas_call(
        body, name=name, in_specs=[ANY] * nt, out_specs=[ANY] * nt,
        out_shape=[jax.ShapeDtypeStruct((3,) + p.shape[1:], p.dtype) for p in ps],
        scratch_shapes=[pltpu.SemaphoreType.DMA((3 * nt,)), pltpu.SemaphoreType.DMA((3 * nt,))],
    )(*ps)


def _join_halves(fs, name):
    nt = len(fs)

    def body(*refs):
        in_refs, out_refs, (send_sems, recv_sems) = refs[:nt], refs[nt:2 * nt], refs[2 * nt:]
        _, _, c, sibling, _ = _place()
        cps = [_remote(t, _half(in_refs[t], None, c), _half(out_refs[t], None, c), sibling, send_sems, recv_sems)
               for t in range(nt)]
        for cp in cps:
            cp.start()
        for t in range(nt):
            theirs = _half(out_refs[t], None, 1 - c)
            _remote(t, theirs, theirs, sibling, send_sems, recv_sems).wait_recv()
        for cp in cps:
            cp.wait_send()

    return pl.pallas_call(
        body, name=name, in_specs=[ANY] * nt, out_specs=[ANY] * nt,
        input_output_aliases={i: i for i in range(nt)},
        out_shape=[jax.ShapeDtypeStruct(f.shape, f.dtype) for f in fs],
        scratch_shapes=[pltpu.SemaphoreType.DMA((nt,)), pltpu.SemaphoreType.DMA((nt,))],
    )(*fs)


def _all_reduce_small(v):
    r, lanes = v.shape

    def body(v_ref, out_ref, buf, send_sems, recv_sems):
        x, y, c, sibling, chips = _place()

        def slot(px, py, pc):
            return buf.at[4 * px + 2 * py + pc]

        def copy(k, block, to, src=None):
            return pltpu.make_async_remote_copy(src_ref=slot(*block) if src is None else src, dst_ref=slot(*block),
                                                send_sem=send_sems.at[k], recv_sem=recv_sems.at[k],
                                                device_id=to, device_id_type=MESH)

        me = (x, y, c)
        buf[4 * x + 2 * y + c] = v_ref[...]
        first = [copy(0, me, sibling, src=v_ref)]
        first += [copy(1 + j, me, (*chip, c), src=v_ref) for j, chip in enumerate(chips)]
        for cp in first:
            cp.start()
        passed = [copy(4 + j, (*chip, c), sibling) for j, chip in enumerate(chips)]
        for j, chip in enumerate(chips):
            copy(1 + j, (*chip, c), me).wait_recv()
            passed[j].start()
        copy(0, (x, y, 1 - c), me).wait_recv()
        for j, chip in enumerate(chips):
            copy(4 + j, (*chip, 1 - c), me).wait_recv()
        for cp in first + passed:
            cp.wait_send()
        acc = buf[0]
        for k in range(1, 8):
            acc = acc + buf[k]
        out_ref[...] = acc

    vmem = pl.BlockSpec(memory_space=pltpu.VMEM)
    return pl.pallas_call(
        body, name="all_reduce_small", in_specs=[vmem], out_specs=vmem,
        out_shape=jax.ShapeDtypeStruct((r, lanes), F32),
        scratch_shapes=[pltpu.VMEM((8, r, lanes), F32), pltpu.SemaphoreType.DMA((7,)), pltpu.SemaphoreType.DMA((7,))],
    )(v)


def _size(shape):
    n = 1
    for d in shape:
        n *= d
    return n


def _pack_small(parts):
    rows = []
    for p in parts:
        f = p.reshape(-1).astype(F32)
        n = -(-f.shape[0] // HEAD_DIM) * HEAD_DIM
        rows.append(jnp.pad(f, (0, n - f.shape[0])).reshape(-1, HEAD_DIM))
    a = jnp.concatenate(rows, axis=0)
    return jnp.pad(a, ((0, -a.shape[0] % 8), (0, 0)))


def _unpack_small(a, shapes):
    out, row = [], 0
    for shp in shapes:
        nrows = -(-_size(shp) // HEAD_DIM)
        out.append(a[row:row + nrows].reshape(-1)[:_size(shp)].reshape(shp))
        row += nrows
    return out


SMALL = ["attn_norm_w", "a_log", "dt_bias", "delta_out_norm_w", "q_norm_w", "k_norm_w", "attn_out_norm_w", "ffn_norm_w"]
BIG = ["w_in", "w_out", "w_gate_up", "w_down"]
ORDER = ["attn_norm_w", "w_in", "conv_w", "a_log", "dt_bias", "delta_out_norm_w", "q_norm_w", "k_norm_w",
         "attn_out_norm_w", "w_out", "ffn_norm_w", "w_gate_up", "w_down"]


def kernel(x, positions, attn_norm_w, w_in, conv_w, a_log, dt_bias, delta_out_norm_w, q_norm_w, k_norm_w, attn_out_norm_w, w_out, ffn_norm_w, w_gate_up, w_down, loss_target, m_attn_norm_w, m_w_in, m_conv_w, m_a_log, m_dt_bias, m_delta_out_norm_w, m_q_norm_w, m_k_norm_w, m_attn_out_norm_w, m_w_out, m_ffn_norm_w, m_w_gate_up, m_w_down, v_attn_norm_w, v_w_in, v_conv_w, v_a_log, v_dt_bias, v_delta_out_norm_w, v_q_norm_w, v_k_norm_w, v_attn_out_norm_w, v_w_out, v_ffn_norm_w, v_w_gate_up, v_w_down):
    wts = dict(attn_norm_w=attn_norm_w, w_in=w_in, conv_w=conv_w, a_log=a_log, dt_bias=dt_bias,
               delta_out_norm_w=delta_out_norm_w, q_norm_w=q_norm_w, k_norm_w=k_norm_w,
               attn_out_norm_w=attn_out_norm_w, w_out=w_out, ffn_norm_w=ffn_norm_w, w_gate_up=w_gate_up, w_down=w_down)
    mom = dict(attn_norm_w=m_attn_norm_w, w_in=m_w_in, conv_w=m_conv_w, a_log=m_a_log, dt_bias=m_dt_bias,
               delta_out_norm_w=m_delta_out_norm_w, q_norm_w=m_q_norm_w, k_norm_w=m_k_norm_w,
               attn_out_norm_w=m_attn_out_norm_w, w_out=m_w_out, ffn_norm_w=m_ffn_norm_w, w_gate_up=m_w_gate_up,
               w_down=m_w_down)
    var = dict(attn_norm_w=v_attn_norm_w, w_in=v_w_in, conv_w=v_conv_w, a_log=v_a_log, dt_bias=v_dt_bias,
               delta_out_norm_w=v_delta_out_norm_w, q_norm_w=v_q_norm_w, k_norm_w=v_k_norm_w,
               attn_out_norm_w=v_attn_out_norm_w, w_out=v_w_out, ffn_norm_w=v_ffn_norm_w, w_gate_up=v_w_gate_up,
               w_down=v_w_down)
    dmod = x.shape[2]
    heads = dmod // (2 * HEAD_DIM)
    dw = heads * HEAD_DIM
    chip = 2 * lax.axis_index("x") + lax.axis_index("y")
    core = lax.axis_index("c")
    n_in, n_out, n_gu, n_down, n_conv = (w_in.shape[2], w_out.shape[1], w_gate_up.shape[2], w_down.shape[1],
                                         conv_w.shape[2])

    def slots_of(w, dtype):
        shard = w[0].astype(dtype)
        return lax.dynamic_update_index_in_dim(lax.empty((4,) + shard.shape, dtype), shard, chip, axis=0)

    s_in, s_out, s_gu, s_down, s_conv = _all_gather_weights(
        [slots_of(w_in, BF16), slots_of(w_out, BF16), slots_of(w_gate_up, BF16), slots_of(w_down, BF16)],
        [slots_of(conv_w, F32)], "all_gather_weights")
    by_cols = lambda a: a.transpose(1, 0, 2).reshape(a.shape[1], 4 * a.shape[2])
    w_in_f, w_gu_f, conv_f = by_cols(s_in), by_cols(s_gu), by_cols(s_conv)
    w_out_f, w_down_f = s_out.reshape(4 * n_out, dmod), s_down.reshape(4 * n_down, dmod)
    w_bd = jnp.pad(w_in_f[:, 4 * dw:4 * dw + 2 * heads], ((0, 0), (0, HEAD_DIM - 2 * heads)))
    small = {n: wts[n] for n in SMALL}

    loss_row, grad_x, sg, bg = _local_step(
        x[0], positions[0], loss_target[0], small, w_in_f[:, :4 * dw], w_bd, w_in_f[:, 4 * dw + 2 * heads:],
        w_out_f[:dw], w_out_f[dw:], w_gu_f, w_down_f, conv_f)

    to_slots = lambda a: a.reshape(a.shape[0], 4, a.shape[1] // 4).transpose(1, 0, 2)
    g_in = to_slots(jnp.concatenate([bg["w_qkvz"], bg["w_bd"][:, :2 * heads], bg["w_attn"]], axis=1))
    g_gu = to_slots(jnp.concatenate([bg["w_gate"], bg["w_up"]], axis=1))
    g_out = jnp.concatenate([bg["w_out_a"], bg["w_out_b"]], axis=0).reshape(4, n_out, dmod)
    g_down = bg["w_down"].reshape(4, n_down, dmod)
    gs = [g_in, g_out, g_gu, g_down]
    place = jnp.stack([core, chip]).astype(jnp.int32)
    from_sibling = _swap_halves_with_sibling(gs, "swap_grad_halves")
    chip_sums = [_add_half_bf16(g, b, place, "chip_partial_sum_" + n) for g, b, n in zip(gs, from_sibling, BIG)]
    from_chips = _scatter_to_chips(chip_sums, "scatter_to_chips")
    totals = _join_halves([_sum4_f32(p, b, place, "grad_total_" + n) for p, b, n in zip(chip_sums, from_chips, BIG)],
                          "join_halves")
    g_big = dict(zip(BIG, totals))

    reduced = _all_reduce_small(_pack_small([sg[n] for n in SMALL] + [sg["conv_w"], loss_row]))
    red = _unpack_small(reduced, [wts[n].shape for n in SMALL] + [(4, 4 * n_conv), (1, HEAD_DIM)])
    g_small = dict(zip(SMALL, red[:len(SMALL)]))
    g_conv_full, loss_out = red[len(SMALL)], red[len(SMALL) + 1]
    g_small["conv_w"] = lax.dynamic_slice_in_dim(g_conv_full, chip * n_conv, n_conv, axis=1).reshape(conv_w.shape)

    grads, deltas, new_m, new_v = {}, {}, {}, {}
    for n in BIG:
        shp = wts[n].shape
        d, nm, nv = _adamw(wts[n][0], g_big[n], mom[n][0], var[n][0], "adamw_" + n)
        grads[n], deltas[n], new_m[n], new_v[n] = g_big[n].reshape(shp), d.reshape(shp), nm.reshape(shp), nv.reshape(shp)
    names = SMALL + ["conv_w"]
    shapes = [wts[n].shape for n in names]
    d, nm, nv = _adamw(_pack_small([wts[n] for n in names]), _pack_small([g_small[n] for n in names]),
                       _pack_small([mom[n] for n in names]), _pack_small([var[n] for n in names]), "adamw_small")
    for n, dd, mm, vv in zip(names, _unpack_small(d, shapes), _unpack_small(nm, shapes), _unpack_small(nv, shapes)):
        grads[n], deltas[n], new_m[n], new_v[n] = g_small[n], dd, mm, vv
    return (loss_out[0, 0], grad_x[None], *[grads[n] for n in ORDER], *[deltas[n] for n in ORDER],
            *[new_m[n] for n in ORDER], *[new_v[n] for n in ORDER])
```

```python
import functools

import jax
import jax.numpy as jnp
from jax import lax
from jax.experimental import pallas as pl
from jax.experimental.pallas import tpu as pltpu

F32 = jnp.float32
BF16 = jnp.bfloat16
HEAD_DIM = 128
CHUNK = 64
SPAN = 128
DILATIONS = (1, 4, 16)
ROPE_THETA = 10000.0
EPS = 1e-6
NEG = -1e30
ADAM_LR, ADAM_B1, ADAM_B2, ADAM_EPS, ADAM_WD, ADAM_STEP = 0.001, 0.9, 0.999, 1e-08, 0.01, 10
VMEM_LIMIT = 48 * 1024 * 1024
MESH = pl.DeviceIdType.MESH

_DN = {"nn": (((1,), (0,)), ((), ())), "nt": (((1,), (1,)), ((), ())), "tn": (((0,), (0,)), ((), ()))}


def _dot(a, b, mode="nn"):
    (ca, cb), _ = _DN[mode]
    if a.ndim == 3:
        dn = (((ca[0] + 1,), (cb[0] + 1,)), ((0,), (0,)))
    else:
        dn = _DN[mode]
    return lax.dot_general(a.astype(BF16), b.astype(BF16), dn, preferred_element_type=F32)


def _rsum(x):
    return jnp.sum(x, axis=-1, keepdims=True)


def _csum(x):
    return jnp.sum(x, axis=-2, keepdims=True)


def _tile(dim, pref, unit=128):
    t = (min(pref, dim) // unit) * unit
    while t >= unit:
        if dim % t == 0:
            return t
        t -= unit
    return dim


def _params(sem):
    return pltpu.CompilerParams(dimension_semantics=sem, vmem_limit_bytes=VMEM_LIMIT)


def _sigmoid(x):
    return 1.0 / (1.0 + jnp.exp(-x))


def _matmul(a, b, mode, name, add=None, out_dtype=F32, a_cols=None, b_cols=None, b_rows=None):
    if mode == "tn":
        out_dtype = BF16
    a_off, a_w = a_cols if a_cols else (0, a.shape[1])
    b_off, b_w = b_cols if b_cols else (0, b.shape[1])
    br_off, br_n = b_rows if b_rows else (0, b.shape[0])
    if mode == "nn":
        m, k, n = a.shape[0], a_w, b_w
        assert br_n == k
    elif mode == "nt":
        m, k, n = a.shape[0], a_w, b.shape[0]
        assert b_w == k
    else:
        k, m, n = a.shape[0], a_w, b_w
        assert b.shape[0] == k
    tm, tn = _tile(m, 1024, 128), _tile(n, 1024, 128)
    tk = _tile(k, 2048 if a.dtype == BF16 and b.dtype == BF16 else 1024, 128)
    if mode == "nn":
        assert a_off % tk == 0 and b_off % tn == 0 and br_off % tk == 0
        a_spec = pl.BlockSpec((tm, tk), lambda i, j, kk: (i, kk + a_off // tk))
        b_spec = pl.BlockSpec((tk, tn), lambda i, j, kk: (kk + br_off // tk, j + b_off // tn))
    elif mode == "nt":
        assert a_off % tk == 0 and b_off % tk == 0
        a_spec = pl.BlockSpec((tm, tk), lambda i, j, kk: (i, kk + a_off // tk))
        b_spec = pl.BlockSpec((tn, tk), lambda i, j, kk: (j, kk + b_off // tk))
    else:
        assert a_off % tm == 0 and b_off % tn == 0
        a_spec = pl.BlockSpec((tk, tm), lambda i, j, kk: (kk, i + a_off // tm))
        b_spec = pl.BlockSpec((tk, tn), lambda i, j, kk: (kk, j + b_off // tn))
    nk = k // tk
    has_add = add is not None

    def body(*refs):
        if has_add:
            a_ref, b_ref, add_ref, o_ref, acc_ref = refs
        else:
            a_ref, b_ref, o_ref, acc_ref = refs
        kk = pl.program_id(2)

        @pl.when(kk == 0)
        def _():
            acc_ref[...] = jnp.zeros_like(acc_ref)

        acc_ref[...] += _dot(a_ref[...], b_ref[...], mode)

        @pl.when(kk == nk - 1)
        def _():
            r = acc_ref[...]
            if has_add:
                r = r + add_ref[...].astype(F32)
            o_ref[...] = r.astype(out_dtype)

    in_specs = [a_spec, b_spec]
    args = [a, b]
    if has_add:
        in_specs.append(pl.BlockSpec((tm, tn), lambda i, j, kk: (i, j)))
        args.append(add)
    return pl.pallas_call(
        body, name=name, grid=(m // tm, n // tn, nk),
        in_specs=in_specs, out_specs=pl.BlockSpec((tm, tn), lambda i, j, kk: (i, j)),
        out_shape=jax.ShapeDtypeStruct((m, n), out_dtype),
        scratch_shapes=[pltpu.VMEM((tm, tn), F32)],
        compiler_params=_params(("parallel", "parallel", "arbitrary")),
    )(*args)


def _rmsnorm_fwd(x, w, name):
    s, d = x.shape
    tr = _tile(s, 512, 8)

    def body(x_ref, w_ref, h_ref):
        xv = x_ref[...]
        rstd = lax.rsqrt(jnp.mean(xv * xv, axis=-1, keepdims=True) + EPS)
        h_ref[...] = (xv * rstd * w_ref[...]).astype(BF16)

    return pl.pallas_call(
        body, name=name, grid=(s // tr,),
        in_specs=[pl.BlockSpec((tr, d), lambda i: (i, 0)), pl.BlockSpec((1, d), lambda i: (0, 0))],
        out_specs=pl.BlockSpec((tr, d), lambda i: (i, 0)),
        out_shape=jax.ShapeDtypeStruct((s, d), BF16),
        compiler_params=_params(("parallel",)),
    )(x, w)


def _rmsnorm_bwd(dh, x, w, res, name):
    s, d = x.shape
    tr = _tile(s, 256, 8)

    def body(dh_ref, x_ref, w_ref, res_ref, dx_ref, dx16_ref, dw_ref):
        xv = x_ref[...]
        rstd = lax.rsqrt(jnp.mean(xv * xv, axis=-1, keepdims=True) + EPS)
        xhat = xv * rstd
        dhv = dh_ref[...]
        gw = dhv * w_ref[...]
        dx = res_ref[...] + rstd * (gw - xhat * jnp.mean(gw * xhat, axis=-1, keepdims=True))
        dx_ref[...] = dx
        dx16_ref[...] = dx.astype(BF16)

        @pl.when(pl.program_id(0) == 0)
        def _():
            dw_ref[...] = jnp.zeros_like(dw_ref)

        dw_ref[...] += jnp.sum(dhv * xhat, axis=0, keepdims=True)

    row = pl.BlockSpec((tr, d), lambda i: (i, 0))
    vec = pl.BlockSpec((1, d), lambda i: (0, 0))
    return pl.pallas_call(
        body, name=name, grid=(s // tr,),
        in_specs=[row, row, vec, row], out_specs=[row, row, vec],
        out_shape=[jax.ShapeDtypeStruct((s, d), F32), jax.ShapeDtypeStruct((s, d), BF16),
                   jax.ShapeDtypeStruct((1, d), F32)],
        compiler_params=_params(("arbitrary",)),
    )(dh, x, w, res)


def _conv_taps(x, w, rows):
    shifted = [x]
    for sft in (1, 2, 3):
        shifted.append(jnp.where(rows >= sft, pltpu.roll(x, sft, 0), 0.0))
    y = w[3:4, :] * shifted[0] + w[2:3, :] * shifted[1] + w[1:2, :] * shifted[2] + w[0:1, :] * shifted[3]
    return y, shifted


def _delta_pre_fwd(qkvz, conv_w, heads):
    s = qkvz.shape[0]
    nblk = 3 * heads

    def body(x_ref, w_ref, o_ref):
        part = pl.program_id(0) // heads
        rows = lax.broadcasted_iota(jnp.int32, (s, HEAD_DIM), 0)
        y, _ = _conv_taps(x_ref[...], w_ref[...], rows)
        a = y * _sigmoid(y)
        rs = lax.rsqrt(jnp.sum(a * a, axis=-1, keepdims=True) + EPS)
        fac = jnp.where(part == 0, rs * (HEAD_DIM ** -0.5), jnp.where(part == 1, rs, 1.0))
        o_ref[...] = a * fac

    return pl.pallas_call(
        body, name="delta_pre_fwd", grid=(nblk,),
        in_specs=[pl.BlockSpec((s, HEAD_DIM), lambda i: (0, i)), pl.BlockSpec((4, HEAD_DIM), lambda i: (0, i))],
        out_specs=pl.BlockSpec((s, HEAD_DIM), lambda i: (0, i)),
        out_shape=jax.ShapeDtypeStruct((s, 3 * heads * HEAD_DIM), F32),
        compiler_params=_params(("parallel",)),
    )(qkvz, conv_w)


def _delta_pre_bwd(dqkv, qkvz, conv_w, heads):
    s = qkvz.shape[0]
    nblk = 3 * heads

    def body(d_ref, x_ref, w_ref, dx_ref, dw_ref):
        part = pl.program_id(0) // heads
        rows = lax.broadcasted_iota(jnp.int32, (s, HEAD_DIM), 0)
        w = w_ref[...]
        y, shifted = _conv_taps(x_ref[...], w, rows)
        sg = _sigmoid(y)
        a = y * sg
        rs = lax.rsqrt(jnp.sum(a * a, axis=-1, keepdims=True) + EPS)
        unit = a * rs
        dn = d_ref[...]
        scale = jnp.where(part == 0, HEAD_DIM ** -0.5, 1.0)
        da_norm = scale * rs * (dn - unit * jnp.sum(dn * unit, axis=-1, keepdims=True))
        da = jnp.where(part < 2, da_norm, dn)
        dy = da * sg * (1.0 + y * (1.0 - sg))
        dx = w[3:4, :] * dy
        for sft in (1, 2, 3):
            dx = dx + w[3 - sft:4 - sft, :] * jnp.where(rows < s - sft, pltpu.roll(dy, s - sft, 0), 0.0)
        dx_ref[...] = dx.astype(BF16)
        for sft in range(4):
            dw_ref[3 - sft:4 - sft, :] = jnp.sum(dy * shifted[sft], axis=0, keepdims=True)

    col = pl.BlockSpec((s, HEAD_DIM), lambda i: (0, i))
    wsp = pl.BlockSpec((4, HEAD_DIM), lambda i: (0, i))
    return pl.pallas_call(
        body, name="delta_pre_bwd", grid=(nblk,),
        in_specs=[col, col, wsp], out_specs=[col, wsp],
        out_shape=[jax.ShapeDtypeStruct((s, 3 * heads * HEAD_DIM), BF16),
                   jax.ShapeDtypeStruct((4, 3 * heads * HEAD_DIM), F32)],
        compiler_params=_params(("parallel",)),
    )(dqkv, qkvz, conv_w)


def _heads_of(ref, heads):
    return jnp.stack([ref[:, h * HEAD_DIM:(h + 1) * HEAD_DIM] for h in range(heads)])


def _chunk_common(q, k, v, bd, a_log, dt_bias, heads):
    c = CHUNK
    braw = jnp.stack([bd[:, h:h + 1] for h in range(heads)])
    draw = jnp.stack([bd[:, heads + h:heads + h + 1] for h in range(heads)])
    beta = _sigmoid(braw)
    xd = draw + dt_bias
    sp = jnp.maximum(xd, 0.0) + jnp.log1p(jnp.exp(-jnp.abs(xd)))
    g = -jnp.exp(a_log) * sp
    row = lax.broadcasted_iota(jnp.int32, (c, c), 0)
    col = lax.broadcasted_iota(jnp.int32, (c, c), 1)
    sq = (heads, c, c)
    g_b = jnp.broadcast_to(g, sq)
    g_row = _csum(jnp.where(row == col, g_b, 0.0))
    gam_col = _rsum(jnp.where(col <= row, jnp.broadcast_to(g_row, sq), 0.0))
    gam_row = _csum(jnp.where(row <= col, g_b, 0.0))
    causal = row >= col
    dm = jnp.where(causal, jnp.exp(jnp.where(causal, gam_col - gam_row, 0.0)), 0.0)
    kk = _dot(k, k, "nt")
    low = jnp.where(row > col, beta * kk * dm, 0.0)
    t = jnp.where(row == col, 1.0, 0.0) - low
    pw = low
    for _ in range(5):
        pw = _dot(pw, pw)
        t = t + _dot(t, pw)
    e = jnp.exp(gam_col)
    u = _dot(t, beta * v)
    w = _dot(t, (beta * e) * k)
    qk_raw = _dot(q, k, "nt")
    gl = _csum(g)
    el = jnp.exp(gl - gam_col)
    return dict(beta=beta, xd=xd, g=g, row=row, col=col, dm=dm, kk=kk, t=t, e=e, u=u, w=w,
                qk_raw=qk_raw, qk=qk_raw * dm, gl=gl, el=el, qd=e * q, kd=el * k, cd=jnp.exp(gl))


def _delta_chunk_fwd(qkv, bd, a_log, dt_bias, heads):
    s = qkv.shape[0]
    n = s // CHUNK
    dw = heads * HEAD_DIM
    blk = lambda part: pl.BlockSpec((CHUNK, dw), lambda i: (i, part))

    def body(q_ref, k_ref, v_ref, bd_ref, al_ref, dt_ref, o_ref, st_ref, state):
        @pl.when(pl.program_id(0) == 0)
        def _():
            state[...] = jnp.zeros_like(state)

        cm = _chunk_common(_heads_of(q_ref, heads), _heads_of(k_ref, heads), _heads_of(v_ref, heads), bd_ref[...],
                           al_ref[...], dt_ref[...], heads)
        st = state[...]
        st_ref[0] = st
        vn = cm["u"] - _dot(cm["w"], st)
        o = _dot(cm["qd"], st) + _dot(cm["qk"], vn)
        for h in range(heads):
            o_ref[:, h * HEAD_DIM:(h + 1) * HEAD_DIM] = o[h]
        state[...] = cm["cd"] * st + _dot(cm["kd"], vn, "tn")

    smem = pl.BlockSpec((heads, 1, 1), lambda i: (0, 0, 0))
    return pl.pallas_call(
        body, name="delta_chunk_fwd", grid=(n,),
        in_specs=[blk(0), blk(1), blk(2), pl.BlockSpec((CHUNK, HEAD_DIM), lambda i: (i, 0)), smem, smem],
        out_specs=[pl.BlockSpec((CHUNK, dw), lambda i: (i, 0)),
                   pl.BlockSpec((1, heads, HEAD_DIM, HEAD_DIM), lambda i: (i, 0, 0, 0))],
        out_shape=[jax.ShapeDtypeStruct((s, dw), F32),
                   jax.ShapeDtypeStruct((n, heads, HEAD_DIM, HEAD_DIM), F32)],
        scratch_shapes=[pltpu.VMEM((heads, HEAD_DIM, HEAD_DIM), F32)],
        compiler_params=_params(("arbitrary",)),
    )(qkv, qkv, qkv, bd, a_log, dt_bias)


def _delta_chunk_bwd(do, qkv, bd, states, a_log, dt_bias, heads):
    s = qkv.shape[0]
    n = s // CHUNK
    dw = heads * HEAD_DIM
    c = CHUNK
    blk = lambda part: pl.BlockSpec((CHUNK, dw), lambda i: (n - 1 - i, part))

    def all_heads(q, k, v, dov, st, dsn, bd, a_log, dt_bias):
        cm = _chunk_common(q, k, v, bd, a_log, dt_bias, heads)
        beta, e, dm, row, col = cm["beta"], cm["e"], cm["dm"], cm["row"], cm["col"]
        sq = (heads, c, c)
        vn = cm["u"] - _dot(cm["w"], st)
        dvn = _dot(cm["kd"], dsn)
        dkd = _dot(vn, dsn, "nt")
        dcd = _csum(_rsum(st * dsn))
        ds = cm["cd"] * dsn
        dqd = _dot(dov, st, "nt")
        ds = ds + _dot(cm["qd"], dov, "tn")
        dqk = _dot(dov, vn, "nt")
        dvn = dvn + _dot(cm["qk"], dov, "tn")
        dw_ = -_dot(dvn, st, "nt")
        ds = ds - _dot(cm["w"], dvn, "tn")
        drhs_u = _dot(cm["t"], dvn, "tn")
        drhs_w = _dot(cm["t"], dw_, "tn")
        da = -(_dot(drhs_u, cm["u"], "nt") + _dot(drhs_w, cm["w"], "nt"))
        dl = jnp.where(row > col, da, 0.0)
        dbeta = _rsum(dl * cm["kk"] * dm)
        dkk = dl * beta * dm
        dd = dl * beta * cm["kk"]
        dv = beta * drhs_u
        ek = e * k
        dbeta = dbeta + _rsum(drhs_u * v) + _rsum(drhs_w * ek)
        dk = (beta * e) * drhs_w
        dgam = _rsum(drhs_w * (beta * ek))
        dqkm = dqk * dm
        dq = _dot(dqkm, k)
        dk = dk + _dot(dqkm, q, "tn")
        dd = dd + dqk * cm["qk_raw"]
        dk = dk + _dot(dkk, k) + _dot(dkk, k, "tn")
        dq = dq + e * dqd
        dgam = dgam + _rsum(dqd * cm["qd"])
        dk = dk + cm["el"] * dkd
        r = _rsum(dkd * cm["kd"])
        dgam = dgam - r
        dgl = _csum(r) + dcd * cm["cd"]
        mm = dd * dm
        colsum_c = _rsum(jnp.where(row == col, jnp.broadcast_to(_csum(mm), sq), 0.0))
        dgam = dgam + _rsum(mm) - colsum_c
        ridx = lax.broadcasted_iota(jnp.int32, (c, 1), 0)
        dgam = dgam + jnp.where(ridx == c - 1, dgl, 0.0)
        dgam_row = _csum(jnp.where(row == col, jnp.broadcast_to(dgam, sq), 0.0))
        dg = _rsum(jnp.where(col >= row, jnp.broadcast_to(dgam_row, sq), 0.0))
        d_xd = dg * (-jnp.exp(a_log)) * _sigmoid(cm["xd"])
        d_braw = dbeta * beta * (1.0 - beta)
        d_alog = dg * cm["g"]
        lane = lax.broadcasted_iota(jnp.int32, (c, HEAD_DIM), 1)
        dbd = jnp.zeros((c, HEAD_DIM), F32)
        for h in range(heads):
            dbd = (dbd + jnp.where(lane == h, d_braw[h], 0.0) + jnp.where(lane == h + heads, d_xd[h], 0.0)
                   + jnp.where(lane == h + 2 * heads, d_alog[h], 0.0))
        return dq, dk, dv, ds, dbd

    def body(do_ref, q_ref, k_ref, v_ref, bd_ref, st_ref, al_ref, dt_ref, dqkv_ref, dbd_ref, dstate):
        @pl.when(pl.program_id(0) == 0)
        def _():
            dstate[...] = jnp.zeros_like(dstate)

        dq, dk, dv, ds, dbd = all_heads(_heads_of(q_ref, heads), _heads_of(k_ref, heads), _heads_of(v_ref, heads),
                                        _heads_of(do_ref, heads), st_ref[0], dstate[...], bd_ref[...],
                                        al_ref[...], dt_ref[...])
        for part, val in enumerate((dq, dk, dv)):
            for h in range(heads):
                lo = part * dw + h * HEAD_DIM
                dqkv_ref[:, lo:lo + HEAD_DIM] = val[h]
        dstate[...] = ds
        dbd_ref[...] = dbd

    smem = pl.BlockSpec((heads, 1, 1), lambda i: (0, 0, 0))
    shared = pl.BlockSpec((CHUNK, HEAD_DIM), lambda i: (n - 1 - i, 0))
    wide = pl.BlockSpec((CHUNK, dw), lambda i: (n - 1 - i, 0))
    return pl.pallas_call(
        body, name="delta_chunk_bwd", grid=(n,),
        in_specs=[wide, blk(0), blk(1), blk(2), shared,
                  pl.BlockSpec((1, heads, HEAD_DIM, HEAD_DIM), lambda i: (n - 1 - i, 0, 0, 0)), smem, smem],
        out_specs=[pl.BlockSpec((CHUNK, 3 * dw), lambda i: (n - 1 - i, 0)), shared],
        out_shape=[jax.ShapeDtypeStruct((s, 3 * dw), F32), jax.ShapeDtypeStruct((s, HEAD_DIM), F32)],
        scratch_shapes=[pltpu.VMEM((heads, HEAD_DIM, HEAD_DIM), F32)],
        compiler_params=_params(("arbitrary",)),
    )(do, qkv, qkv, qkv, bd, states, a_log, dt_bias)


def _delta_post_fwd(o, qkvz, w, heads):
    s = o.shape[0]
    tr = _tile(s, 1024, 8)

    def body(o_ref, z_ref, w_ref, out_ref):
        ov, z = o_ref[...], z_ref[...]
        rstd = lax.rsqrt(jnp.mean(ov * ov, axis=-1, keepdims=True) + EPS)
        out_ref[...] = (ov * rstd * w_ref[...] * (z * _sigmoid(z))).astype(BF16)

    return pl.pallas_call(
        body, name="delta_post_fwd", grid=(s // tr, heads),
        in_specs=[pl.BlockSpec((tr, HEAD_DIM), lambda i, h: (i, h)),
                  pl.BlockSpec((tr, HEAD_DIM), lambda i, h: (i, 3 * heads + h)),
                  pl.BlockSpec((1, HEAD_DIM), lambda i, h: (0, 0))],
        out_specs=pl.BlockSpec((tr, HEAD_DIM), lambda i, h: (i, h)),
        out_shape=jax.ShapeDtypeStruct((s, heads * HEAD_DIM), BF16),
        compiler_params=_params(("parallel", "parallel")),
    )(o, qkvz, w)


def _delta_post_bwd(dmix, o, qkvz, w, heads):
    s = o.shape[0]
    tr = _tile(s, 1024, 8)

    def body(d_ref, o_ref, z_ref, w_ref, do_ref, dz_ref, dw_ref):
        d, ov, z, wv = d_ref[...], o_ref[...], z_ref[...], w_ref[...]
        sg = _sigmoid(z)
        rstd = lax.rsqrt(jnp.mean(ov * ov, axis=-1, keepdims=True) + EPS)
        ohat = ov * rstd
        dz_ref[...] = (d * (ohat * wv) * sg * (1.0 + z * (1.0 - sg))).astype(BF16)
        dn = d * (z * sg)
        gw = dn * wv
        do_ref[...] = rstd * (gw - ohat * jnp.mean(gw * ohat, axis=-1, keepdims=True))

        @pl.when((pl.program_id(0) == 0) & (pl.program_id(1) == 0))
        def _():
            dw_ref[...] = jnp.zeros_like(dw_ref)

        dw_ref[...] += jnp.sum(dn * ohat, axis=0, keepdims=True)

    head = pl.BlockSpec((tr, HEAD_DIM), lambda i, h: (i, h))
    vec = pl.BlockSpec((1, HEAD_DIM), lambda i, h: (0, 0))
    dw = heads * HEAD_DIM
    return pl.pallas_call(
        body, name="delta_post_bwd", grid=(s // tr, heads),
        in_specs=[head, head, pl.BlockSpec((tr, HEAD_DIM), lambda i, h: (i, 3 * heads + h)), vec],
        out_specs=[head, head, vec],
        out_shape=[jax.ShapeDtypeStruct((s, dw), F32), jax.ShapeDtypeStruct((s, dw), BF16),
                   jax.ShapeDtypeStruct((1, HEAD_DIM), F32)],
        compiler_params=_params(("arbitrary", "arbitrary")),
    )(dmix, o, qkvz, w)


def _rope_tables(positions, s):
    half = HEAD_DIM // 2
    inv_freq = ROPE_THETA ** (-jnp.arange(half, dtype=F32) / half)
    ang = positions.reshape(s, 1).astype(F32) * inv_freq
    cos, sin = jnp.cos(ang), jnp.sin(ang)
    return jnp.concatenate([cos, cos], axis=-1), jnp.concatenate([-sin, sin], axis=-1)


def _attn_pre_fwd(aqkv, wq, wk, cosf, sinf, heads):
    s = aqkv.shape[0]
    tr = _tile(s, 1024, 8)

    def body(x_ref, wq_ref, wk_ref, c_ref, s_ref, o_ref):
        xv = x_ref[...]
        wv = jnp.where(pl.program_id(1) < heads, wq_ref[...], wk_ref[...])
        y = xv * lax.rsqrt(jnp.mean(xv * xv, axis=-1, keepdims=True) + EPS) * wv
        o_ref[...] = y * c_ref[...] + pltpu.roll(y, HEAD_DIM // 2, 1) * s_ref[...]

    blk = pl.BlockSpec((tr, HEAD_DIM), lambda i, j: (i, j))
    vec = pl.BlockSpec((1, HEAD_DIM), lambda i, j: (0, 0))
    tab = pl.BlockSpec((tr, HEAD_DIM), lambda i, j: (i, 0))
    return pl.pallas_call(
        body, name="attn_pre_fwd", grid=(s // tr, 2 * heads),
        in_specs=[blk, vec, vec, tab, tab], out_specs=blk,
        out_shape=jax.ShapeDtypeStruct((s, 2 * heads * HEAD_DIM), F32),
        compiler_params=_params(("parallel", "parallel")),
    )(aqkv, wq, wk, cosf, sinf)


ATT_ROWS = 2048


def _band():
    qi = lax.broadcasted_iota(jnp.int32, (SPAN, 2 * SPAN), 0)
    ki = lax.broadcasted_iota(jnp.int32, (SPAN, 2 * SPAN), 1)
    dist = qi + SPAN - ki
    return (dist >= 0) & (dist <= SPAN), ki >= SPAN


def _sub(g, r, d):
    if d == 1:
        return pl.ds(g * SPAN, SPAN)
    return pl.ds(g * SPAN * d + r, SPAN, stride=d)


def _attn_geometry(s, d):
    rb = min(ATT_ROWS, s)
    pr = SPAN * d
    return rb, pr, rb // pr, s // rb


def _attn_fwd(qk, aqkv, heads, d):
    s = qk.shape[0]
    aw = heads * HEAD_DIM
    rb, pr, g_cnt, nstep = _attn_geometry(s, d)

    def body(q_ref, k_ref, v_ref, kp_ref, vp_ref, o_ref, l_ref):
        first = pl.program_id(1) == 0
        band, own = _band()
        edge = band & (own | jnp.logical_not(first))
        for r in range(d):
            for g in range(g_cnt):
                rows = _sub(g, r, d)
                if g == 0:
                    kp, vp, mask = kp_ref[_sub(0, r, d), :], vp_ref[_sub(0, r, d), :], edge
                else:
                    kp, vp, mask = k_ref[_sub(g - 1, r, d), :], v_ref[_sub(g - 1, r, d), :], band
                kcat = jnp.concatenate([kp, k_ref[rows, :]], axis=0)
                vcat = jnp.concatenate([vp, v_ref[rows, :]], axis=0)
                sc = _dot(q_ref[rows, :], kcat, "nt") * (HEAD_DIM ** -0.5)
                sc = jnp.where(mask, sc, NEG)
                m = jnp.max(sc, axis=-1, keepdims=True)
                p = jnp.exp(sc - m)
                den = jnp.sum(p, axis=-1, keepdims=True)
                o_ref[rows, :] = _dot(p, vcat) / den
                l_ref[rows, :] = jnp.broadcast_to(m + jnp.log(den), (SPAN, HEAD_DIM))

    cur = lambda off: pl.BlockSpec((rb, HEAD_DIM), lambda h, j: (j, off + h))
    prev = lambda off: pl.BlockSpec((pr, HEAD_DIM), lambda h, j: (jnp.maximum(j * (rb // pr) - 1, 0), off + h))
    out = pl.BlockSpec((rb, HEAD_DIM), lambda h, j: (j, h))
    return pl.pallas_call(
        body, name=f"attn_fwd_d{d}", grid=(heads, nstep),
        in_specs=[cur(0), cur(heads), cur(2 * heads), prev(heads), prev(2 * heads)],
        out_specs=[out, out],
        out_shape=[jax.ShapeDtypeStruct((s, aw), F32)] * 2,
        compiler_params=_params(("parallel", "parallel")),
    )(qk, qk, aqkv, qk, aqkv)


def _attn_merge_fwd(os_, ls_, w, heads):
    s = os_[0].shape[0]
    tr = _tile(s, 512, 8)

    def body(o1, o2, o3, l1, l2, l3, w_ref, mix_ref, ob_ref, lse_ref):
        la, lb, lc = l1[...], l2[...], l3[...]
        m = jnp.maximum(jnp.maximum(la, lb), lc)
        ea, eb, ec = jnp.exp(la - m), jnp.exp(lb - m), jnp.exp(lc - m)
        tot = ea + eb + ec
        ob = (ea * o1[...] + eb * o2[...] + ec * o3[...]) / tot
        ob_ref[...] = ob
        lse_ref[...] = m + jnp.log(tot)
        rstd = lax.rsqrt(jnp.mean(ob * ob, axis=-1, keepdims=True) + EPS)
        mix_ref[...] = (ob * rstd * w_ref[...]).astype(BF16)

    blk = pl.BlockSpec((tr, HEAD_DIM), lambda i, h: (i, h))
    vec = pl.BlockSpec((1, HEAD_DIM), lambda i, h: (0, 0))
    aw = heads * HEAD_DIM
    return pl.pallas_call(
        body, name="attn_merge_fwd", grid=(s // tr, heads),
        in_specs=[blk] * 6 + [vec], out_specs=[blk, blk, blk],
        out_shape=[jax.ShapeDtypeStruct((s, aw), BF16), jax.ShapeDtypeStruct((s, aw), F32),
                   jax.ShapeDtypeStruct((s, aw), F32)],
        compiler_params=_params(("parallel", "parallel")),
    )(*os_, *ls_, w)


def _attn_merge_bwd(dmix, ob, w, heads):
    s = ob.shape[0]
    tr = _tile(s, 1024, 8)

    def body(d_ref, ob_ref, w_ref, do_ref, dsum_ref, dw_ref):
        d, ov = d_ref[...], ob_ref[...]
        rstd = lax.rsqrt(jnp.mean(ov * ov, axis=-1, keepdims=True) + EPS)
        ohat = ov * rstd
        gw = d * w_ref[...]
        dov = rstd * (gw - ohat * jnp.mean(gw * ohat, axis=-1, keepdims=True))
        do_ref[...] = dov
        dsum_ref[...] = jnp.broadcast_to(jnp.sum(dov * ov, axis=-1, keepdims=True), dov.shape)

        @pl.when((pl.program_id(0) == 0) & (pl.program_id(1) == 0))
        def _():
            dw_ref[...] = jnp.zeros_like(dw_ref)

        dw_ref[...] += jnp.sum(d * ohat, axis=0, keepdims=True)

    blk = pl.BlockSpec((tr, HEAD_DIM), lambda i, h: (i, h))
    vec = pl.BlockSpec((1, HEAD_DIM), lambda i, h: (0, 0))
    aw = heads * HEAD_DIM
    return pl.pallas_call(
        body, name="attn_merge_bwd", grid=(s // tr, heads),
        in_specs=[pl.BlockSpec((tr, HEAD_DIM), lambda i, h: (i, heads + h)), blk, vec],
        out_specs=[blk, blk, vec],
        out_shape=[jax.ShapeDtypeStruct((s, aw), F32), jax.ShapeDtypeStruct((s, aw), F32),
                   jax.ShapeDtypeStruct((1, HEAD_DIM), F32)],
        compiler_params=_params(("arbitrary", "arbitrary")),
    )(dmix, ob, w)


def _attn_bwd(qk, aqkv, do, lse, dsum, heads, d):
    s = qk.shape[0]
    aw = heads * HEAD_DIM
    rb, pr, g_cnt, nstep = _attn_geometry(s, d)
    scale = HEAD_DIM ** -0.5

    def body(q_ref, k_ref, v_ref, do_ref, l_ref, ds_ref, kp_ref, vp_ref, qn_ref, don_ref, ln_ref, dsn_ref, out_ref):
        first = pl.program_id(1) == 0
        last = pl.program_id(1) == nstep - 1
        band, own = _band()
        edge = band & (own | jnp.logical_not(first))
        dq_ref, dk_ref, dv_ref = out_ref.at[0], out_ref.at[1], out_ref.at[2]
        dk_ref[...] = jnp.zeros((rb, HEAD_DIM), F32)
        dv_ref[...] = jnp.zeros((rb, HEAD_DIM), F32)
        for r in range(d):
            for g in range(g_cnt):
                rows = _sub(g, r, d)
                if g == 0:
                    kp, vp, mask = kp_ref[_sub(0, r, d), :], vp_ref[_sub(0, r, d), :], edge
                else:
                    prows = _sub(g - 1, r, d)
                    kp, vp, mask = k_ref[prows, :], v_ref[prows, :], band
                q, dov = q_ref[rows, :], do_ref[rows, :]
                kcat = jnp.concatenate([kp, k_ref[rows, :]], axis=0)
                vcat = jnp.concatenate([vp, v_ref[rows, :]], axis=0)
                sc = _dot(q, kcat, "nt") * scale
                p = jnp.where(mask, jnp.exp(sc - l_ref[rows, 0:1]), 0.0)
                dsc = p * (_dot(dov, vcat, "nt") - ds_ref[rows, 0:1]) * scale
                dq_ref[rows, :] = _dot(dsc, kcat)
                dk = _dot(dsc, q, "tn")
                dv = _dot(p, dov, "tn")
                dk_ref[rows, :] += dk[SPAN:]
                dv_ref[rows, :] += dv[SPAN:]
                if g > 0:
                    dk_ref[prows, :] += dk[:SPAN]
                    dv_ref[prows, :] += dv[:SPAN]
            rows, nrows = _sub(g_cnt - 1, r, d), _sub(0, r, d)
            q, dov = qn_ref[nrows, :], don_ref[nrows, :]
            sc = _dot(q, k_ref[rows, :], "nt") * scale
            p = jnp.where(band[:, :SPAN] & jnp.logical_not(last), jnp.exp(sc - ln_ref[nrows, 0:1]), 0.0)
            dsc = p * (_dot(dov, v_ref[rows, :], "nt") - dsn_ref[nrows, 0:1]) * scale
            dk_ref[rows, :] += _dot(dsc, q, "tn")
            dv_ref[rows, :] += _dot(p, dov, "tn")

    per = rb // pr
    cur = lambda off: pl.BlockSpec((rb, HEAD_DIM), lambda h, j: (j, off + h))
    prev = lambda off: pl.BlockSpec((pr, HEAD_DIM), lambda h, j: (jnp.maximum(j * per - 1, 0), off + h))
    nxt = lambda off: pl.BlockSpec((pr, HEAD_DIM), lambda h, j: (jnp.minimum((j + 1) * per, s // pr - 1), off + h))
    return pl.pallas_call(
        body, name=f"attn_bwd_d{d}", grid=(heads, nstep),
        in_specs=[cur(0), cur(heads), cur(2 * heads), cur(0), cur(0), cur(0), prev(heads), prev(2 * heads),
                  nxt(0), nxt(0), nxt(0), nxt(0)],
        out_specs=pl.BlockSpec((3, rb, HEAD_DIM), lambda h, j: (0, j, h)),
        out_shape=jax.ShapeDtypeStruct((3, s, aw), F32),
        compiler_params=_params(("parallel", "parallel")),
    )(qk, qk, aqkv, do, lse, dsum, qk, aqkv, qk, do, lse, dsum)


def _attn_pre_bwd(grads, aqkv, wq, wk, cosf, sinf, heads):
    s = aqkv.shape[0]
    tr = _tile(s, 512, 8)
    nrow = s // tr

    def body(g1_ref, g2_ref, g3_ref, x_ref, wq_ref, wk_ref, c_ref, s_ref, dx_ref, dwq_ref, dwk_ref):
        i, j = pl.program_id(0), pl.program_id(1)
        kind = j // heads
        dout = g1_ref[0] + g2_ref[0] + g3_ref[0]
        tot_v = dout
        dy = dout * c_ref[...] + pltpu.roll(dout * s_ref[...], HEAD_DIM // 2, 1)
        xv = x_ref[...]
        wv = jnp.where(kind == 0, wq_ref[...], wk_ref[...])
        rstd = lax.rsqrt(jnp.mean(xv * xv, axis=-1, keepdims=True) + EPS)
        xhat = xv * rstd
        gw = dy * wv
        dxn = rstd * (gw - xhat * jnp.mean(gw * xhat, axis=-1, keepdims=True))
        dx_ref[...] = jnp.where(kind == 2, tot_v, dxn).astype(BF16)
        dwc = jnp.sum(dy * xhat, axis=0, keepdims=True)

        @pl.when((i == 0) & (j == 0))
        def _():
            dwq_ref[...] = jnp.zeros_like(dwq_ref)
            dwk_ref[...] = jnp.zeros_like(dwk_ref)

        @pl.when(kind == 0)
        def _():
            dwq_ref[...] += dwc

        @pl.when(kind == 1)
        def _():
            dwk_ref[...] += dwc

    grad = pl.BlockSpec((1, tr, HEAD_DIM), lambda i, j: (j // heads, i, j % heads))
    blk = pl.BlockSpec((tr, HEAD_DIM), lambda i, j: (i, j))
    vec = pl.BlockSpec((1, HEAD_DIM), lambda i, j: (0, 0))
    tab = pl.BlockSpec((tr, HEAD_DIM), lambda i, j: (i, 0))
    return pl.pallas_call(
        body, name="attn_pre_bwd", grid=(nrow, 3 * heads),
        in_specs=[grad, grad, grad, blk, vec, vec, tab, tab], out_specs=[blk, vec, vec],
        out_shape=[jax.ShapeDtypeStruct((s, 3 * heads * HEAD_DIM), BF16),
                   jax.ShapeDtypeStruct((1, HEAD_DIM), F32), jax.ShapeDtypeStruct((1, HEAD_DIM), F32)],
        compiler_params=_params(("arbitrary", "arbitrary")),
    )(*grads, aqkv, wq, wk, cosf, sinf)


def _swiglu_fwd(gu, ff):
    s = gu.shape[0]
    tr, tc = _tile(s, 512, 8), _tile(ff, 1024, 128)
    nc = ff // tc

    def body(g_ref, u_ref, o_ref):
        g = g_ref[...].astype(F32)
        o_ref[...] = (g * _sigmoid(g) * u_ref[...].astype(F32)).astype(BF16)

    return pl.pallas_call(
        body, name="swiglu_fwd", grid=(s // tr, nc),
        in_specs=[pl.BlockSpec((tr, tc), lambda i, j: (i, j)), pl.BlockSpec((tr, tc), lambda i, j: (i, j + nc))],
        out_specs=pl.BlockSpec((tr, tc), lambda i, j: (i, j)),
        out_shape=jax.ShapeDtypeStruct((s, ff), BF16),
        compiler_params=_params(("parallel", "parallel")),
    )(gu, gu)


def _swiglu_bwd(dact, gu, ff):
    s = gu.shape[0]
    tr, tc = _tile(s, 512, 8), _tile(ff, 1024, 128)
    nc = ff // tc

    def body(d_ref, g_ref, u_ref, dg_ref, du_ref):
        d, g = d_ref[...].astype(F32), g_ref[...].astype(F32)
        sg = _sigmoid(g)
        dg_ref[...] = (d * u_ref[...].astype(F32) * sg * (1.0 + g * (1.0 - sg))).astype(BF16)
        du_ref[...] = (d * g * sg).astype(BF16)

    lo = pl.BlockSpec((tr, tc), lambda i, j: (i, j))
    hi = pl.BlockSpec((tr, tc), lambda i, j: (i, j + nc))
    dg, du = pl.pallas_call(
        body, name="swiglu_bwd", grid=(s // tr, nc),
        in_specs=[lo, lo, hi], out_specs=[lo, lo],
        out_shape=[jax.ShapeDtypeStruct((s, ff), BF16)] * 2,
        compiler_params=_params(("parallel", "parallel")),
    )(dact, gu, gu)
    return dg, du


def _loss_head(y, target):
    s, d = y.shape
    tr = _tile(s, 512, 8)

    def body(y_ref, t_ref, dy_ref, dy16_ref, l_ref):
        err = y_ref[...] - t_ref[...]
        dy_ref[...] = err * (1.0 / d)
        dy16_ref[...] = (err * (1.0 / d)).astype(BF16)
        part = 0.5 * jnp.sum(jnp.sum(err * err, axis=-1, keepdims=True) * (1.0 / d), axis=0, keepdims=True)

        @pl.when(pl.program_id(0) == 0)
        def _():
            l_ref[...] = jnp.zeros_like(l_ref)

        lane = lax.broadcasted_iota(jnp.int32, (1, HEAD_DIM), 1)
        l_ref[...] += jnp.where(lane == 0, part, 0.0)

    row = pl.BlockSpec((tr, d), lambda i: (i, 0))
    return pl.pallas_call(
        body, name="loss_head", grid=(s // tr,),
        in_specs=[row, row], out_specs=[row, row, pl.BlockSpec((1, HEAD_DIM), lambda i: (0, 0))],
        out_shape=[jax.ShapeDtypeStruct((s, d), F32), jax.ShapeDtypeStruct((s, d), BF16),
                   jax.ShapeDtypeStruct((1, HEAD_DIM), F32)],
        compiler_params=_params(("arbitrary",)),
    )(y, target)


def _colsum(a, name):
    s, d = a.shape
    tr = _tile(s, 1024, 8)

    def body(a_ref, o_ref):
        @pl.when(pl.program_id(0) == 0)
        def _():
            o_ref[...] = jnp.zeros_like(o_ref)

        o_ref[...] += jnp.sum(a_ref[...], axis=0, keepdims=True)

    return pl.pallas_call(
        body, name=name, grid=(s // tr,),
        in_specs=[pl.BlockSpec((tr, d), lambda i: (i, 0))], out_specs=pl.BlockSpec((1, d), lambda i: (0, 0)),
        out_shape=jax.ShapeDtypeStruct((1, d), F32),
        compiler_params=_params(("arbitrary",)),
    )(a)


def _local_step(x, positions, target, small, w_qkvz, w_bd, w_attn, w_out, w_gu, w_down, conv_w):
    s, dmod = x.shape
    heads = dmod // (2 * HEAD_DIM)
    dw = heads * HEAD_DIM
    ff = w_down.shape[0]
    a_log, dt_bias = small["a_log"].reshape(heads, 1, 1), small["dt_bias"].reshape(heads, 1, 1)
    cosf, sinf = _rope_tables(positions, s)

    h1 = _rmsnorm_fwd(x, small["attn_norm_w"], "norm1_fwd")
    qkvz = _matmul(h1, w_qkvz, "nn", "proj_qkvz")
    bd = _matmul(h1, w_bd, "nn", "proj_bd")
    aqkv = _matmul(h1, w_attn, "nn", "proj_attn")
    dqkv = _delta_pre_fwd(qkvz, conv_w, heads)
    o_d, states = _delta_chunk_fwd(dqkv, bd, a_log, dt_bias, heads)
    mix_a = _delta_post_fwd(o_d, qkvz, small["delta_out_norm_w"], heads)
    qk_rot = _attn_pre_fwd(aqkv, small["q_norm_w"], small["k_norm_w"], cosf, sinf, heads)
    outs = [_attn_fwd(qk_rot, aqkv, heads, d) for d in DILATIONS]
    mix_b, ob, lse = _attn_merge_fwd([o for o, _ in outs], [l for _, l in outs], small["attn_out_norm_w"], heads)
    x1 = _matmul(mix_a, w_out, "nn", "out_proj_a", add=x, b_rows=(0, dw))
    x1 = _matmul(mix_b, w_out, "nn", "out_proj_b", add=x1, b_rows=(dw, dw))
    h2 = _rmsnorm_fwd(x1, small["ffn_norm_w"], "norm2_fwd")
    gu = _matmul(h2, w_gu, "nn", "ffn_gate_up", out_dtype=BF16)
    act = _swiglu_fwd(gu, ff)
    y = _matmul(act, w_down, "nn", "ffn_down", add=x1)
    dy, dy16, loss_row = _loss_head(y, target)

    dact = _matmul(dy16, w_down, "nt", "ffn_down_dx", out_dtype=BF16)
    g_w_down = _matmul(act, dy16, "tn", "ffn_down_dw")
    dgate, dup = _swiglu_bwd(dact, gu, ff)
    dh2 = _matmul(dgate, w_gu, "nt", "ffn_gate_dx", b_cols=(0, ff))
    dh2 = _matmul(dup, w_gu, "nt", "ffn_up_dx", b_cols=(ff, ff), add=dh2)
    g_w_gate = _matmul(h2, dgate, "tn", "ffn_gate_dw")
    g_w_up = _matmul(h2, dup, "tn", "ffn_up_dw")
    dx1, dx1_16, g_ffn_norm = _rmsnorm_bwd(dh2, x1, small["ffn_norm_w"], dy, "norm2_bwd")
    dmix = _matmul(dx1_16, w_out, "nt", "out_proj_dx")
    g_w_out_a = _matmul(mix_a, dx1_16, "tn", "out_proj_dw_a")
    g_w_out_b = _matmul(mix_b, dx1_16, "tn", "out_proj_dw_b")
    dob, dsum, g_attn_out_norm = _attn_merge_bwd(dmix, ob, small["attn_out_norm_w"], heads)
    grads = [_attn_bwd(qk_rot, aqkv, dob, lse, dsum, heads, d) for d in DILATIONS]
    d_aqkv, g_q_norm, g_k_norm = _attn_pre_bwd(grads, aqkv, small["q_norm_w"], small["k_norm_w"], cosf, sinf, heads)
    do_d, dz, g_delta_out_norm = _delta_post_bwd(dmix, o_d, qkvz, small["delta_out_norm_w"], heads)
    ddqkv, dbd = _delta_chunk_bwd(do_d, dqkv, bd, states, a_log, dt_bias, heads)
    d_qkv_raw, g_conv = _delta_pre_bwd(ddqkv, qkvz, conv_w, heads)
    bd_sums = _colsum(dbd, "bd_colsum")
    dh1 = _matmul(d_qkv_raw, w_qkvz, "nt", "proj_qkv_dx", b_cols=(0, 3 * dw))
    dh1 = _matmul(dz, w_qkvz, "nt", "proj_z_dx", b_cols=(3 * dw, dw), add=dh1)
    dh1 = _matmul(d_aqkv, w_attn, "nt", "proj_attn_dx", add=dh1)
    dh1 = _matmul(dbd, w_bd, "nt", "proj_bd_dx", add=dh1)
    g_w_qkv = _matmul(h1, d_qkv_raw, "tn", "proj_qkv_dw")
    g_w_z = _matmul(h1, dz, "tn", "proj_z_dw")
    g_w_bd = _matmul(h1, dbd, "tn", "proj_bd_dw")
    g_w_attn = _matmul(h1, d_aqkv, "tn", "proj_attn_dw")
    grad_x, _, g_attn_norm = _rmsnorm_bwd(dh1, x, small["attn_norm_w"], dx1, "norm1_bwd")
    small_grads = dict(
        attn_norm_w=g_attn_norm, a_log=bd_sums[:, 2 * heads:3 * heads], dt_bias=bd_sums[:, heads:2 * heads],
        delta_out_norm_w=g_delta_out_norm, q_norm_w=g_q_norm, k_norm_w=g_k_norm,
        attn_out_norm_w=g_attn_out_norm, ffn_norm_w=g_ffn_norm, conv_w=g_conv)
    big_grads = dict(w_qkv=g_w_qkv, w_z=g_w_z, w_bd=g_w_bd, w_attn=g_w_attn, w_out_a=g_w_out_a, w_out_b=g_w_out_b,
                     w_gate=g_w_gate, w_up=g_w_up, w_down=g_w_down)
    return loss_row, grad_x, small_grads, big_grads


def _adamw(w, g, m, v, name):
    r, c = w.shape
    tr = _tile(r, 256, 8)

    def body(w_ref, g_ref, m_ref, v_ref, d_ref, nm_ref, nv_ref):
        gv = g_ref[...]
        nm = ADAM_B1 * m_ref[...] + (1.0 - ADAM_B1) * gv
        nv = ADAM_B2 * v_ref[...] + (1.0 - ADAM_B2) * (gv * gv)
        m_hat = nm / (1.0 - ADAM_B1 ** ADAM_STEP)
        v_hat = nv / (1.0 - ADAM_B2 ** ADAM_STEP)
        d_ref[...] = -ADAM_LR * (m_hat / (jnp.sqrt(v_hat) + ADAM_EPS) + ADAM_WD * w_ref[...])
        nm_ref[...] = nm
        nv_ref[...] = nv

    blk = pl.BlockSpec((tr, c), lambda i: (i, 0))
    return pl.pallas_call(
        body, name=name, grid=(r // tr,),
        in_specs=[blk] * 4, out_specs=[blk] * 3,
        out_shape=[jax.ShapeDtypeStruct((r, c), F32)] * 3,
        compiler_params=_params(("parallel",)),
    )(w, g, m, v)


def _add_half_bf16(g, b, place, name):
    n, half, c = b.shape
    tr = _tile(half, 512, 16)
    nb = half // tr

    def body(place_ref, g_ref, b_ref, o_ref):
        o_ref[...] = (g_ref[...].astype(F32) + b_ref[...].astype(F32)).astype(BF16)

    blk = pl.BlockSpec((1, tr, c), lambda i, j, p: (i, j, 0))
    return pl.pallas_call(
        body, name=name,
        grid_spec=pltpu.PrefetchScalarGridSpec(
            num_scalar_prefetch=1, grid=(n, nb),
            in_specs=[pl.BlockSpec((1, tr, c), lambda i, j, p: (i, p[0] * nb + j, 0)), blk], out_specs=blk),
        out_shape=jax.ShapeDtypeStruct((n, half, c), BF16),
        compiler_params=_params(("parallel", "parallel")),
    )(place, g, b)


def _sum4_f32(mine, others, place, name):
    _, half, c = mine.shape
    tr = _tile(half, 512, 16)
    nb = half // tr

    def body(place_ref, a_ref, b_ref, o_ref):
        acc = a_ref[0].astype(F32)
        for j in range(3):
            acc = acc + b_ref[j].astype(F32)
        o_ref[...] = acc

    return pl.pallas_call(
        body, name=name,
        grid_spec=pltpu.PrefetchScalarGridSpec(
            num_scalar_prefetch=1, grid=(nb,),
            in_specs=[pl.BlockSpec((1, tr, c), lambda i, p: (p[1], i, 0)),
                      pl.BlockSpec((3, tr, c), lambda i, p: (0, i, 0))],
            out_specs=pl.BlockSpec((tr, c), lambda i, p: (p[0] * nb + i, 0))),
        out_shape=jax.ShapeDtypeStruct((2 * half, c), F32),
        compiler_params=_params(("parallel",)),
    )(place, mine, others)


def _place():
    x, y, c = lax.axis_index("x"), lax.axis_index("y"), lax.axis_index("c")
    other_chips = [(1 - x, y), (x, 1 - y), (1 - x, 1 - y)]
    return x, y, c, (x, y, 1 - c), other_chips


ANY = pl.BlockSpec(memory_space=pl.ANY)


def _remote(k, src, dst, to, send_sems, recv_sems):
    return pltpu.make_async_remote_copy(src_ref=src, dst_ref=dst, send_sem=send_sems.at[k], recv_sem=recv_sems.at[k],
                                        device_id=to, device_id_type=MESH)


def _half(ref, lead, hc):
    if lead is None:
        half = ref.shape[0] // 2
        return ref.at[pl.ds(hc * half, half), :]
    half = ref.shape[1] // 2
    return ref.at[lead, pl.ds(hc * half, half), :]


def _all_gather_weights(slots, whole, name):
    nt, nw = len(slots), len(whole)
    n_ici = 3 * (nt + nw)

    def body(*refs):
        ins, outs = refs[:nt + nw], refs[nt + nw:2 * (nt + nw)]
        send_sems, recv_sems = refs[2 * (nt + nw):]
        x, y, c, sibling, chips = _place()
        me = 2 * x + y
        first = []
        for j, (px, py) in enumerate(chips):
            for t in range(nt):
                first.append(_remote(j * (nt + nw) + t, _half(ins[t], me, c), _half(outs[t], me, c), (px, py, c),
                                     send_sems, recv_sems))
            for t in range(nt, nt + nw):
                first.append(_remote(j * (nt + nw) + t, ins[t].at[me], outs[t].at[me], (px, py, c),
                                     send_sems, recv_sems))
        for cp in first:
            cp.start()
        passed = []
        for j, (px, py) in enumerate(chips):
            for t in range(nt):
                landed = _half(outs[t], 2 * px + py, c)
                _remote(j * (nt + nw) + t, landed, landed, (px, py, c), send_sems, recv_sems).wait_recv()
                fw = _remote(n_ici + j * nt + t, landed, landed, sibling, send_sems, recv_sems)
                fw.start()
                passed.append(fw)
            for t in range(nt, nt + nw):
                landed = outs[t].at[2 * px + py]
                _remote(j * (nt + nw) + t, landed, landed, (px, py, c), send_sems, recv_sems).wait_recv()
        for j, (px, py) in enumerate(chips):
            for t in range(nt):
                other = _half(outs[t], 2 * px + py, 1 - c)
                _remote(n_ici + j * nt + t, other, other, sibling, send_sems, recv_sems).wait_recv()
        for cp in first + passed:
            cp.wait_send()

    arrays = list(slots) + list(whole)
    n_sem = n_ici + 3 * nt
    return pl.pallas_call(
        body, name=name, in_specs=[ANY] * len(arrays), out_specs=[ANY] * len(arrays),
        input_output_aliases={i: i for i in range(len(arrays))},
        out_shape=[jax.ShapeDtypeStruct(a.shape, a.dtype) for a in arrays],
        scratch_shapes=[pltpu.SemaphoreType.DMA((n_sem,)), pltpu.SemaphoreType.DMA((n_sem,))],
    )(*arrays)


def _swap_halves_with_sibling(gs, name):
    nt = len(gs)

    def body(*refs):
        g_refs, o_refs, (send_sems, recv_sems) = refs[:nt], refs[nt:2 * nt], refs[2 * nt:]
        _, _, c, sibling, _ = _place()
        cps = []
        for t in range(nt):
            half = g_refs[t].shape[1] // 2
            cps.append(_remote(t, g_refs[t].at[:, pl.ds((1 - c) * half, half), :], o_refs[t], sibling,
                               send_sems, recv_sems))
        for cp in cps:
            cp.start()
        for cp in cps:
            cp.wait()

    return pl.pallas_call(
        body, name=name, in_specs=[ANY] * nt, out_specs=[ANY] * nt,
        out_shape=[jax.ShapeDtypeStruct((g.shape[0], g.shape[1] // 2, g.shape[2]), g.dtype) for g in gs],
        scratch_shapes=[pltpu.SemaphoreType.DMA((nt,)), pltpu.SemaphoreType.DMA((nt,))],
    )(*gs)


def _scatter_to_chips(ps, name):
    nt = len(ps)

    def body(*refs):
        p_refs, b_refs, (send_sems, recv_sems) = refs[:nt], refs[nt:2 * nt], refs[2 * nt:]
        _, _, c, _, chips = _place()
        cps = [_remote(j * nt + t, p_refs[t].at[2 * px + py], b_refs[t].at[j], (px, py, c), send_sems, recv_sems)
               for j, (px, py) in enumerate(chips) for t in range(nt)]
        for cp in cps:
            cp.start()
        for cp in cps:
            cp.wait()

    return pl.pallas_call(
        body, name=name, in_specs=[ANY] * nt, out_specs=[ANY] * nt,
        out_shape=[jax.ShapeDtypeStruct((3,) + p.shape[1:], p.dtype) for p in ps],
        scratch_shapes=[pltpu.SemaphoreType.DMA((3 * nt,)), pltpu.SemaphoreType.DMA((3 * nt,))],
    )(*ps)


def _join_halves(fs, name):
    nt = len(fs)

    def body(*refs):
        in_refs, out_refs, (send_sems, recv_sems) = refs[:nt], refs[nt:2 * nt], refs[2 * nt:]
        _, _, c, sibling, _ = _place()
        cps = [_remote(t, _half(in_refs[t], None, c), _half(out_refs[t], None, c), sibling, send_sems, recv_sems)
               for t in range(nt)]
        for cp in cps:
            cp.start()
        for t in range(nt):
            theirs = _half(out_refs[t], None, 1 - c)
            _remote(t, theirs, theirs, sibling, send_sems, recv_sems).wait_recv()
        for cp in cps:
            cp.wait_send()

    return pl.pallas_call(
        body, name=name, in_specs=[ANY] * nt, out_specs=[ANY] * nt,
        input_output_aliases={i: i for i in range(nt)},
        out_shape=[jax.ShapeDtypeStruct(f.shape, f.dtype) for f in fs],
        scratch_shapes=[pltpu.SemaphoreType.DMA((nt,)), pltpu.SemaphoreType.DMA((nt,))],
    )(*fs)


def _all_reduce_small(v):
    r, lanes = v.shape

    def body(v_ref, out_ref, buf, send_sems, recv_sems):
        x, y, c, sibling, chips = _place()

        def slot(px, py, pc):
            return buf.at[4 * px + 2 * py + pc]

        def copy(k, block, to, src=None):
            return pltpu.make_async_remote_copy(src_ref=slot(*block) if src is None else src, dst_ref=slot(*block),
                                                send_sem=send_sems.at[k], recv_sem=recv_sems.at[k],
                                                device_id=to, device_id_type=MESH)

        me = (x, y, c)
        buf[4 * x + 2 * y + c] = v_ref[...]
        first = [copy(0, me, sibling, src=v_ref)]
        first += [copy(1 + j, me, (*chip, c), src=v_ref) for j, chip in enumerate(chips)]
        for cp in first:
            cp.start()
        passed = [copy(4 + j, (*chip, c), sibling) for j, chip in enumerate(chips)]
        for j, chip in enumerate(chips):
            copy(1 + j, (*chip, c), me).wait_recv()
            passed[j].start()
        copy(0, (x, y, 1 - c), me).wait_recv()
        for j, chip in enumerate(chips):
            copy(4 + j, (*chip, 1 - c), me).wait_recv()
        for cp in first + passed:
            cp.wait_send()
        acc = buf[0]
        for k in range(1, 8):
            acc = acc + buf[k]
        out_ref[...] = acc

    vmem = pl.BlockSpec(memory_space=pltpu.VMEM)
    return pl.pallas_call(
        body, name="all_reduce_small", in_specs=[vmem], out_specs=vmem,
        out_shape=jax.ShapeDtypeStruct((r, lanes), F32),
        scratch_shapes=[pltpu.VMEM((8, r, lanes), F32), pltpu.SemaphoreType.DMA((7,)), pltpu.SemaphoreType.DMA((7,))],
    )(v)


def _size(shape):
    n = 1
    for d in shape:
        n *= d
    return n


def _pack_small(parts):
    rows = []
    for p in parts:
        f = p.reshape(-1).astype(F32)
        n = -(-f.shape[0] // HEAD_DIM) * HEAD_DIM
        rows.append(jnp.pad(f, (0, n - f.shape[0])).reshape(-1, HEAD_DIM))
    a = jnp.concatenate(rows, axis=0)
    return jnp.pad(a, ((0, -a.shape[0] % 8), (0, 0)))


def _unpack_small(a, shapes):
    out, row = [], 0
    for shp in shapes:
        nrows = -(-_size(shp) // HEAD_DIM)
        out.append(a[row:row + nrows].reshape(-1)[:_size(shp)].reshape(shp))
        row += nrows
    return out


SMALL = ["attn_norm_w", "a_log", "dt_bias", "delta_out_norm_w", "q_norm_w", "k_norm_w", "attn_out_norm_w", "ffn_norm_w"]
BIG = ["w_in", "w_out", "w_gate_up", "w_down"]
ORDER = ["attn_norm_w", "w_in", "conv_w", "a_log", "dt_bias", "delta_out_norm_w", "q_norm_w", "k_norm_w",
         "attn_out_norm_w", "w_out", "ffn_norm_w", "w_gate_up", "w_down"]


def kernel(x, positions, attn_norm_w, w_in, conv_w, a_log, dt_bias, delta_out_norm_w, q_norm_w, k_norm_w, attn_out_norm_w, w_out, ffn_norm_w, w_gate_up, w_down, loss_target, m_attn_norm_w, m_w_in, m_conv_w, m_a_log, m_dt_bias, m_delta_out_norm_w, m_q_norm_w, m_k_norm_w, m_attn_out_norm_w, m_w_out, m_ffn_norm_w, m_w_gate_up, m_w_down, v_attn_norm_w, v_w_in, v_conv_w, v_a_log, v_dt_bias, v_delta_out_norm_w, v_q_norm_w, v_k_norm_w, v_attn_out_norm_w, v_w_out, v_ffn_norm_w, v_w_gate_up, v_w_down):
    wts = dict(attn_norm_w=attn_norm_w, w_in=w_in, conv_w=conv_w, a_log=a_log, dt_bias=dt_bias,
               delta_out_norm_w=delta_out_norm_w, q_norm_w=q_norm_w, k_norm_w=k_norm_w,
               attn_out_norm_w=attn_out_norm_w, w_out=w_out, ffn_norm_w=ffn_norm_w, w_gate_up=w_gate_up, w_down=w_down)
    mom = dict(attn_norm_w=m_attn_norm_w, w_in=m_w_in, conv_w=m_conv_w, a_log=m_a_log, dt_bias=m_dt_bias,
               delta_out_norm_w=m_delta_out_norm_w, q_norm_w=m_q_norm_w, k_norm_w=m_k_norm_w,
               attn_out_norm_w=m_attn_out_norm_w, w_out=m_w_out, ffn_norm_w=m_ffn_norm_w, w_gate_up=m_w_gate_up,
               w_down=m_w_down)
    var = dict(attn_norm_w=v_attn_norm_w, w_in=v_w_in, conv_w=v_conv_w, a_log=v_a_log, dt_bias=v_dt_bias,
               delta_out_norm_w=v_delta_out_norm_w, q_norm_w=v_q_norm_w, k_norm_w=v_k_norm_w,
               attn_out_norm_w=v_attn_out_norm_w, w_out=v_w_out, ffn_norm_w=v_ffn_norm_w, w_gate_up=v_w_gate_up,
               w_down=v_w_down)
    dmod = x.shape[2]
    heads = dmod // (2 * HEAD_DIM)
    dw = heads * HEAD_DIM
    chip = 2 * lax.axis_index("x") + lax.axis_index("y")
    core = lax.axis_index("c")
    n_in, n_out, n_gu, n_down, n_conv = (w_in.shape[2], w_out.shape[1], w_gate_up.shape[2], w_down.shape[1],
                                         conv_w.shape[2])

    def slots_of(w, dtype):
        shard = w[0].astype(dtype)
        return lax.dynamic_update_index_in_dim(lax.empty((4,) + shard.shape, dtype), shard, chip, axis=0)

    s_in, s_out, s_gu, s_down, s_conv = _all_gather_weights(
        [slots_of(w_in, BF16), slots_of(w_out, BF16), slots_of(w_gate_up, BF16), slots_of(w_down, BF16)],
        [slots_of(conv_w, F32)], "all_gather_weights")
    by_cols = lambda a: a.transpose(1, 0, 2).reshape(a.shape[1], 4 * a.shape[2])
    w_in_f, w_gu_f, conv_f = by_cols(s_in), by_cols(s_gu), by_cols(s_conv)
    w_out_f, w_down_f = s_out.reshape(4 * n_out, dmod), s_down.reshape(4 * n_down, dmod)
    w_bd = jnp.pad(w_in_f[:, 4 * dw:4 * dw + 2 * heads], ((0, 0), (0, HEAD_DIM - 2 * heads)))
    small = {n: wts[n] for n in SMALL}

    loss_row, grad_x, sg, bg = _local_step(
        x[0], positions[0], loss_target[0], small, w_in_f[:, :4 * dw], w_bd, w_in_f[:, 4 * dw + 2 * heads:],
        w_out_f, w_gu_f, w_down_f, conv_f)

    to_slots = lambda a: a.reshape(a.shape[0], 4, a.shape[1] // 4).transpose(1, 0, 2)
    g_in = to_slots(jnp.concatenate([bg["w_qkv"], bg["w_z"], bg["w_bd"][:, :2 * heads], bg["w_attn"]], axis=1))
    g_gu = to_slots(jnp.concatenate([bg["w_gate"], bg["w_up"]], axis=1))
    g_out = jnp.concatenate([bg["w_out_a"], bg["w_out_b"]], axis=0).reshape(4, n_out, dmod)
    g_down = bg["w_down"].reshape(4, n_down, dmod)
    gs = [g_in, g_out, g_gu, g_down]
    place = jnp.stack([core, chip]).astype(jnp.int32)
    from_sibling = _swap_halves_with_sibling(gs, "swap_grad_halves")
    chip_sums = [_add_half_bf16(g, b, place, "chip_partial_sum_" + n) for g, b, n in zip(gs, from_sibling, BIG)]
    from_chips = _scatter_to_chips(chip_sums, "scatter_to_chips")
    totals = _join_halves([_sum4_f32(p, b, place, "grad_total_" + n) for p, b, n in zip(chip_sums, from_chips, BIG)],
                          "join_halves")
    g_big = dict(zip(BIG, totals))

    reduced = _all_reduce_small(_pack_small([sg[n] for n in SMALL] + [sg["conv_w"], loss_row]))
    red = _unpack_small(reduced, [wts[n].shape for n in SMALL] + [(4, 4 * n_conv), (1, HEAD_DIM)])
    g_small = dict(zip(SMALL, red[:len(SMALL)]))
    g_conv_full, loss_out = red[len(SMALL)], red[len(SMALL) + 1]
    g_small["conv_w"] = lax.dynamic_slice_in_dim(g_conv_full, chip * n_conv, n_conv, axis=1).reshape(conv_w.shape)

    grads, deltas, new_m, new_v = {}, {}, {}, {}
    for n in BIG:
        shp = wts[n].shape
        d, nm, nv = _adamw(wts[n][0], g_big[n], mom[n][0], var[n][0], "adamw_" + n)
        grads[n], deltas[n], new_m[n], new_v[n] = g_big[n].reshape(shp), d.reshape(shp), nm.reshape(shp), nv.reshape(shp)
    names = SMALL + ["conv_w"]
    shapes = [wts[n].shape for n in names]
    d, nm, nv = _adamw(_pack_small([wts[n] for n in names]), _pack_small([g_small[n] for n in names]),
                       _pack_small([mom[n] for n in names]), _pack_small([var[n] for n in names]), "adamw_small")
    for n, dd, mm, vv in zip(names, _unpack_small(d, shapes), _unpack_small(nm, shapes), _unpack_small(nv, shapes)):
        grads[n], deltas[n], new_m[n], new_v[n] = g_small[n], dd, mm, vv
    return (loss_out[0, 0], grad_x[None], *[grads[n] for n in ORDER], *[deltas[n] for n in ORDER],
            *[new_m[n] for n in ORDER], *[new_v[n] for n in ORDER])
```

```python
import functools

import jax
import jax.numpy as jnp
from jax import lax
from jax.experimental import pallas as pl
from jax.experimental.pallas import tpu as pltpu

F32 = jnp.float32
BF16 = jnp.bfloat16
HEAD_DIM = 128
CHUNK = 64
SPAN = 128
DILATIONS = (1, 4, 16)
ROPE_THETA = 10000.0
EPS = 1e-6
NEG = -1e30
ADAM_LR, ADAM_B1, ADAM_B2, ADAM_EPS, ADAM_WD, ADAM_STEP = 0.001, 0.9, 0.999, 1e-08, 0.01, 10
VMEM_LIMIT = 48 * 1024 * 1024
MESH = pl.DeviceIdType.MESH

_DN = {"nn": (((1,), (0,)), ((), ())), "nt": (((1,), (1,)), ((), ())), "tn": (((0,), (0,)), ((), ()))}


def _dot(a, b, mode="nn"):
    (ca, cb), _ = _DN[mode]
    if a.ndim == 3:
        dn = (((ca[0] + 1,), (cb[0] + 1,)), ((0,), (0,)))
    else:
        dn = _DN[mode]
    return lax.dot_general(a.astype(BF16), b.astype(BF16), dn, preferred_element_type=F32)


def _rsum(x):
    return jnp.sum(x, axis=-1, keepdims=True)


def _csum(x):
    return jnp.sum(x, axis=-2, keepdims=True)


def _tile(dim, pref, unit=128):
    t = (min(pref, dim) // unit) * unit
    while t >= unit:
        if dim % t == 0:
            return t
        t -= unit
    return dim


def _params(sem):
    return pltpu.CompilerParams(dimension_semantics=sem, vmem_limit_bytes=VMEM_LIMIT)


def _sigmoid(x):
    return 1.0 / (1.0 + jnp.exp(-x))


def _matmul(a, b, mode, name, add=None, out_dtype=F32, a_cols=None, b_cols=None, b_rows=None):
    if mode == "tn":
        out_dtype = BF16
    a_off, a_w = a_cols if a_cols else (0, a.shape[1])
    b_off, b_w = b_cols if b_cols else (0, b.shape[1])
    br_off, br_n = b_rows if b_rows else (0, b.shape[0])
    if mode == "nn":
        m, k, n = a.shape[0], a_w, b_w
        assert br_n == k
    elif mode == "nt":
        m, k, n = a.shape[0], a_w, b.shape[0]
        assert b_w == k
    else:
        k, m, n = a.shape[0], a_w, b_w
        assert b.shape[0] == k
    tm, tn = _tile(m, 1024, 128), _tile(n, 1024, 128)
    tk = _tile(k, 2048 if a.dtype == BF16 and b.dtype == BF16 else 1024, 128)
    if mode == "nn":
        assert a_off % tk == 0 and b_off % tn == 0 and br_off % tk == 0
        a_spec = pl.BlockSpec((tm, tk), lambda i, j, kk: (i, kk + a_off // tk))
        b_spec = pl.BlockSpec((tk, tn), lambda i, j, kk: (kk + br_off // tk, j + b_off // tn))
    elif mode == "nt":
        assert a_off % tk == 0 and b_off % tk == 0
        a_spec = pl.BlockSpec((tm, tk), lambda i, j, kk: (i, kk + a_off // tk))
        b_spec = pl.BlockSpec((tn, tk), lambda i, j, kk: (j, kk + b_off // tk))
    else:
        assert a_off % tm == 0 and b_off % tn == 0
        a_spec = pl.BlockSpec((tk, tm), lambda i, j, kk: (kk, i + a_off // tm))
        b_spec = pl.BlockSpec((tk, tn), lambda i, j, kk: (kk, j + b_off // tn))
    nk = k // tk
    has_add = add is not None

    def body(*refs):
        if has_add:
            a_ref, b_ref, add_ref, o_ref, acc_ref = refs
        else:
            a_ref, b_ref, o_ref, acc_ref = refs
        kk = pl.program_id(2)

        @pl.when(kk == 0)
        def _():
            acc_ref[...] = jnp.zeros_like(acc_ref)

        acc_ref[...] += _dot(a_ref[...], b_ref[...], mode)

        @pl.when(kk == nk - 1)
        def _():
            r = acc_ref[...]
            if has_add:
                r = r + add_ref[...].astype(F32)
            o_ref[...] = r.astype(out_dtype)

    in_specs = [a_spec, b_spec]
    args = [a, b]
    if has_add:
        in_specs.append(pl.BlockSpec((tm, tn), lambda i, j, kk: (i, j)))
        args.append(add)
    return pl.pallas_call(
        body, name=name, grid=(m // tm, n // tn, nk),
        in_specs=in_specs, out_specs=pl.BlockSpec((tm, tn), lambda i, j, kk: (i, j)),
        out_shape=jax.ShapeDtypeStruct((m, n), out_dtype),
        scratch_shapes=[pltpu.VMEM((tm, tn), F32)],
        compiler_params=_params(("parallel", "parallel", "arbitrary")),
    )(*args)


def _rmsnorm_fwd(x, w, name, after=None):
    s, d = x.shape
    tr = _tile(s, 512, 8)

    def body(x_ref, w_ref, *rest):
        h_ref = rest[-1]
        xv = x_ref[...]
        rstd = lax.rsqrt(jnp.mean(xv * xv, axis=-1, keepdims=True) + EPS)
        h_ref[...] = (xv * rstd * w_ref[...]).astype(BF16)

    extra = [] if after is None else [after]
    return pl.pallas_call(
        body, name=name, grid=(s // tr,),
        in_specs=[pl.BlockSpec((tr, d), lambda i: (i, 0)), pl.BlockSpec((1, d), lambda i: (0, 0))] + [ANY] * len(extra),
        out_specs=pl.BlockSpec((tr, d), lambda i: (i, 0)),
        out_shape=jax.ShapeDtypeStruct((s, d), BF16),
        compiler_params=_params(("parallel",)),
    )(x, w, *extra)


def _rmsnorm_bwd(dh, x, w, res, name, after=None):
    s, d = x.shape
    tr = _tile(s, 256, 8)

    def body(dh_ref, x_ref, w_ref, res_ref, *rest):
        dx_ref, dx16_ref, dw_ref = rest[-3:]
        xv = x_ref[...]
        rstd = lax.rsqrt(jnp.mean(xv * xv, axis=-1, keepdims=True) + EPS)
        xhat = xv * rstd
        dhv = dh_ref[...]
        gw = dhv * w_ref[...]
        dx = res_ref[...] + rstd * (gw - xhat * jnp.mean(gw * xhat, axis=-1, keepdims=True))
        dx_ref[...] = dx
        dx16_ref[...] = dx.astype(BF16)

        @pl.when(pl.program_id(0) == 0)
        def _():
            dw_ref[...] = jnp.zeros_like(dw_ref)

        dw_ref[...] += jnp.sum(dhv * xhat, axis=0, keepdims=True)

    row = pl.BlockSpec((tr, d), lambda i: (i, 0))
    vec = pl.BlockSpec((1, d), lambda i: (0, 0))
    extra = [] if after is None else [after]
    return pl.pallas_call(
        body, name=name, grid=(s // tr,),
        in_specs=[row, row, vec, row] + [ANY] * len(extra), out_specs=[row, row, vec],
        out_shape=[jax.ShapeDtypeStruct((s, d), F32), jax.ShapeDtypeStruct((s, d), BF16),
                   jax.ShapeDtypeStruct((1, d), F32)],
        compiler_params=_params(("arbitrary",)),
    )(dh, x, w, res, *extra)


def _conv_taps(x, w, rows):
    shifted = [x]
    for sft in (1, 2, 3):
        shifted.append(jnp.where(rows >= sft, pltpu.roll(x, sft, 0), 0.0))
    y = w[3:4, :] * shifted[0] + w[2:3, :] * shifted[1] + w[1:2, :] * shifted[2] + w[0:1, :] * shifted[3]
    return y, shifted


def _delta_pre_fwd(qkvz, conv_w, heads):
    s = qkvz.shape[0]
    nblk = 3 * heads

    def body(x_ref, w_ref, o_ref):
        part = pl.program_id(0) // heads
        rows = lax.broadcasted_iota(jnp.int32, (s, HEAD_DIM), 0)
        y, _ = _conv_taps(x_ref[...], w_ref[...], rows)
        a = y * _sigmoid(y)
        rs = lax.rsqrt(jnp.sum(a * a, axis=-1, keepdims=True) + EPS)
        fac = jnp.where(part == 0, rs * (HEAD_DIM ** -0.5), jnp.where(part == 1, rs, 1.0))
        o_ref[...] = a * fac

    return pl.pallas_call(
        body, name="delta_pre_fwd", grid=(nblk,),
        in_specs=[pl.BlockSpec((s, HEAD_DIM), lambda i: (0, i)), pl.BlockSpec((4, HEAD_DIM), lambda i: (0, i))],
        out_specs=pl.BlockSpec((s, HEAD_DIM), lambda i: (0, i)),
        out_shape=jax.ShapeDtypeStruct((s, 3 * heads * HEAD_DIM), F32),
        compiler_params=_params(("parallel",)),
    )(qkvz, conv_w)


def _delta_pre_bwd(dqkv, qkvz, conv_w, heads):
    s = qkvz.shape[0]
    nblk = 3 * heads

    def body(d_ref, x_ref, w_ref, dx_ref, dw_ref):
        part = pl.program_id(0) // heads
        rows = lax.broadcasted_iota(jnp.int32, (s, HEAD_DIM), 0)
        w = w_ref[...]
        y, shifted = _conv_taps(x_ref[...], w, rows)
        sg = _sigmoid(y)
        a = y * sg
        rs = lax.rsqrt(jnp.sum(a * a, axis=-1, keepdims=True) + EPS)
        unit = a * rs
        dn = d_ref[...]
        scale = jnp.where(part == 0, HEAD_DIM ** -0.5, 1.0)
        da_norm = scale * rs * (dn - unit * jnp.sum(dn * unit, axis=-1, keepdims=True))
        da = jnp.where(part < 2, da_norm, dn)
        dy = da * sg * (1.0 + y * (1.0 - sg))
        dx = w[3:4, :] * dy
        for sft in (1, 2, 3):
            dx = dx + w[3 - sft:4 - sft, :] * jnp.where(rows < s - sft, pltpu.roll(dy, s - sft, 0), 0.0)
        dx_ref[...] = dx.astype(BF16)
        for sft in range(4):
            dw_ref[3 - sft:4 - sft, :] = jnp.sum(dy * shifted[sft], axis=0, keepdims=True)

    col = pl.BlockSpec((s, HEAD_DIM), lambda i: (0, i))
    wsp = pl.BlockSpec((4, HEAD_DIM), lambda i: (0, i))
    return pl.pallas_call(
        body, name="delta_pre_bwd", grid=(nblk,),
        in_specs=[col, col, wsp], out_specs=[col, wsp],
        out_shape=[jax.ShapeDtypeStruct((s, 3 * heads * HEAD_DIM), BF16),
                   jax.ShapeDtypeStruct((4, 3 * heads * HEAD_DIM), F32)],
        compiler_params=_params(("parallel",)),
    )(dqkv, qkvz, conv_w)


def _heads_of(ref, heads):
    return jnp.stack([ref[:, h * HEAD_DIM:(h + 1) * HEAD_DIM] for h in range(heads)])


def _chunk_common(q, k, v, bd, a_log, dt_bias, heads):
    c = CHUNK
    braw = jnp.stack([bd[:, h:h + 1] for h in range(heads)])
    draw = jnp.stack([bd[:, heads + h:heads + h + 1] for h in range(heads)])
    beta = _sigmoid(braw)
    xd = draw + dt_bias
    sp = jnp.maximum(xd, 0.0) + jnp.log1p(jnp.exp(-jnp.abs(xd)))
    g = -jnp.exp(a_log) * sp
    row = lax.broadcasted_iota(jnp.int32, (c, c), 0)
    col = lax.broadcasted_iota(jnp.int32, (c, c), 1)
    sq = (heads, c, c)
    g_b = jnp.broadcast_to(g, sq)
    g_row = _csum(jnp.where(row == col, g_b, 0.0))
    gam_col = _rsum(jnp.where(col <= row, jnp.broadcast_to(g_row, sq), 0.0))
    gam_row = _csum(jnp.where(row <= col, g_b, 0.0))
    causal = row >= col
    dm = jnp.where(causal, jnp.exp(jnp.where(causal, gam_col - gam_row, 0.0)), 0.0)
    kk = _dot(k, k, "nt")
    low = jnp.where(row > col, beta * kk * dm, 0.0)
    t = jnp.where(row == col, 1.0, 0.0) - low
    pw = low
    for _ in range(5):
        pw = _dot(pw, pw)
        t = t + _dot(t, pw)
    e = jnp.exp(gam_col)
    u = _dot(t, beta * v)
    w = _dot(t, (beta * e) * k)
    qk_raw = _dot(q, k, "nt")
    gl = _csum(g)
    el = jnp.exp(gl - gam_col)
    return dict(beta=beta, xd=xd, g=g, row=row, col=col, dm=dm, kk=kk, t=t, e=e, u=u, w=w,
                qk_raw=qk_raw, qk=qk_raw * dm, gl=gl, el=el, qd=e * q, kd=el * k, cd=jnp.exp(gl))


def _delta_chunk_fwd(qkv, bd, a_log, dt_bias, heads):
    s = qkv.shape[0]
    n = s // CHUNK
    dw = heads * HEAD_DIM
    blk = lambda part: pl.BlockSpec((CHUNK, dw), lambda i: (i, part))

    def body(q_ref, k_ref, v_ref, bd_ref, al_ref, dt_ref, o_ref, st_ref, state):
        @pl.when(pl.program_id(0) == 0)
        def _():
            state[...] = jnp.zeros_like(state)

        cm = _chunk_common(_heads_of(q_ref, heads), _heads_of(k_ref, heads), _heads_of(v_ref, heads), bd_ref[...],
                           al_ref[...], dt_ref[...], heads)
        st = state[...]
        st_ref[0] = st
        vn = cm["u"] - _dot(cm["w"], st)
        o = _dot(cm["qd"], st) + _dot(cm["qk"], vn)
        for h in range(heads):
            o_ref[:, h * HEAD_DIM:(h + 1) * HEAD_DIM] = o[h]
        state[...] = cm["cd"] * st + _dot(cm["kd"], vn, "tn")

    smem = pl.BlockSpec((heads, 1, 1), lambda i: (0, 0, 0))
    return pl.pallas_call(
        body, name="delta_chunk_fwd", grid=(n,),
        in_specs=[blk(0), blk(1), blk(2), pl.BlockSpec((CHUNK, HEAD_DIM), lambda i: (i, 0)), smem, smem],
        out_specs=[pl.BlockSpec((CHUNK, dw), lambda i: (i, 0)),
                   pl.BlockSpec((1, heads, HEAD_DIM, HEAD_DIM), lambda i: (i, 0, 0, 0))],
        out_shape=[jax.ShapeDtypeStruct((s, dw), F32),
                   jax.ShapeDtypeStruct((n, heads, HEAD_DIM, HEAD_DIM), F32)],
        scratch_shapes=[pltpu.VMEM((heads, HEAD_DIM, HEAD_DIM), F32)],
        compiler_params=_params(("arbitrary",)),
    )(qkv, qkv, qkv, bd, a_log, dt_bias)


def _delta_chunk_bwd(do, qkv, bd, states, a_log, dt_bias, heads):
    s = qkv.shape[0]
    n = s // CHUNK
    dw = heads * HEAD_DIM
    c = CHUNK
    blk = lambda part: pl.BlockSpec((CHUNK, dw), lambda i: (n - 1 - i, part))

    def all_heads(q, k, v, dov, st, dsn, bd, a_log, dt_bias):
        cm = _chunk_common(q, k, v, bd, a_log, dt_bias, heads)
        beta, e, dm, row, col = cm["beta"], cm["e"], cm["dm"], cm["row"], cm["col"]
        sq = (heads, c, c)
        vn = cm["u"] - _dot(cm["w"], st)
        dvn = _dot(cm["kd"], dsn)
        dkd = _dot(vn, dsn, "nt")
        dcd = _csum(_rsum(st * dsn))
        ds = cm["cd"] * dsn
        dqd = _dot(dov, st, "nt")
        ds = ds + _dot(cm["qd"], dov, "tn")
        dqk = _dot(dov, vn, "nt")
        dvn = dvn + _dot(cm["qk"], dov, "tn")
        dw_ = -_dot(dvn, st, "nt")
        ds = ds - _dot(cm["w"], dvn, "tn")
        drhs_u = _dot(cm["t"], dvn, "tn")
        drhs_w = _dot(cm["t"], dw_, "tn")
        da = -(_dot(drhs_u, cm["u"], "nt") + _dot(drhs_w, cm["w"], "nt"))
        dl = jnp.where(row > col, da, 0.0)
        dbeta = _rsum(dl * cm["kk"] * dm)
        dkk = dl * beta * dm
        dd = dl * beta * cm["kk"]
        dv = beta * drhs_u
        ek = e * k
        dbeta = dbeta + _rsum(drhs_u * v) + _rsum(drhs_w * ek)
        dk = (beta * e) * drhs_w
        dgam = _rsum(drhs_w * (beta * ek))
        dqkm = dqk * dm
        dq = _dot(dqkm, k)
        dk = dk + _dot(dqkm, q, "tn")
        dd = dd + dqk * cm["qk_raw"]
        dk = dk + _dot(dkk, k) + _dot(dkk, k, "tn")
        dq = dq + e * dqd
        dgam = dgam + _rsum(dqd * cm["qd"])
        dk = dk + cm["el"] * dkd
        r = _rsum(dkd * cm["kd"])
        dgam = dgam - r
        dgl = _csum(r) + dcd * cm["cd"]
        mm = dd * dm
        colsum_c = _rsum(jnp.where(row == col, jnp.broadcast_to(_csum(mm), sq), 0.0))
        dgam = dgam + _rsum(mm) - colsum_c
        ridx = lax.broadcasted_iota(jnp.int32, (c, 1), 0)
        dgam = dgam + jnp.where(ridx == c - 1, dgl, 0.0)
        dgam_row = _csum(jnp.where(row == col, jnp.broadcast_to(dgam, sq), 0.0))
        dg = _rsum(jnp.where(col >= row, jnp.broadcast_to(dgam_row, sq), 0.0))
        d_xd = dg * (-jnp.exp(a_log)) * _sigmoid(cm["xd"])
        d_braw = dbeta * beta * (1.0 - beta)
        d_alog = dg * cm["g"]
        lane = lax.broadcasted_iota(jnp.int32, (c, HEAD_DIM), 1)
        dbd = jnp.zeros((c, HEAD_DIM), F32)
        for h in range(heads):
            dbd = (dbd + jnp.where(lane == h, d_braw[h], 0.0) + jnp.where(lane == h + heads, d_xd[h], 0.0)
                   + jnp.where(lane == h + 2 * heads, d_alog[h], 0.0))
        return dq, dk, dv, ds, dbd

    def body(do_ref, q_ref, k_ref, v_ref, bd_ref, st_ref, al_ref, dt_ref, dqkv_ref, dbd_ref, dstate):
        @pl.when(pl.program_id(0) == 0)
        def _():
            dstate[...] = jnp.zeros_like(dstate)

        dq, dk, dv, ds, dbd = all_heads(_heads_of(q_ref, heads), _heads_of(k_ref, heads), _heads_of(v_ref, heads),
                                        _heads_of(do_ref, heads), st_ref[0], dstate[...], bd_ref[...],
                                        al_ref[...], dt_ref[...])
        for part, val in enumerate((dq, dk, dv)):
            for h in range(heads):
                lo = part * dw + h * HEAD_DIM
                dqkv_ref[:, lo:lo + HEAD_DIM] = val[h]
        dstate[...] = ds
        dbd_ref[...] = dbd

    smem = pl.BlockSpec((heads, 1, 1), lambda i: (0, 0, 0))
    shared = pl.BlockSpec((CHUNK, HEAD_DIM), lambda i: (n - 1 - i, 0))
    wide = pl.BlockSpec((CHUNK, dw), lambda i: (n - 1 - i, 0))
    return pl.pallas_call(
        body, name="delta_chunk_bwd", grid=(n,),
        in_specs=[wide, blk(0), blk(1), blk(2), shared,
                  pl.BlockSpec((1, heads, HEAD_DIM, HEAD_DIM), lambda i: (n - 1 - i, 0, 0, 0)), smem, smem],
        out_specs=[pl.BlockSpec((CHUNK, 3 * dw), lambda i: (n - 1 - i, 0)), shared],
        out_shape=[jax.ShapeDtypeStruct((s, 3 * dw), F32), jax.ShapeDtypeStruct((s, HEAD_DIM), F32)],
        scratch_shapes=[pltpu.VMEM((heads, HEAD_DIM, HEAD_DIM), F32)],
        compiler_params=_params(("arbitrary",)),
    )(do, qkv, qkv, qkv, bd, states, a_log, dt_bias)


def _delta_post_fwd(o, qkvz, w, heads):
    s = o.shape[0]
    tr = _tile(s, 1024, 8)

    def body(o_ref, z_ref, w_ref, out_ref):
        ov, z = o_ref[...], z_ref[...]
        rstd = lax.rsqrt(jnp.mean(ov * ov, axis=-1, keepdims=True) + EPS)
        out_ref[...] = (ov * rstd * w_ref[...] * (z * _sigmoid(z))).astype(BF16)

    return pl.pallas_call(
        body, name="delta_post_fwd", grid=(s // tr, heads),
        in_specs=[pl.BlockSpec((tr, HEAD_DIM), lambda i, h: (i, h)),
                  pl.BlockSpec((tr, HEAD_DIM), lambda i, h: (i, 3 * heads + h)),
                  pl.BlockSpec((1, HEAD_DIM), lambda i, h: (0, 0))],
        out_specs=pl.BlockSpec((tr, HEAD_DIM), lambda i, h: (i, h)),
        out_shape=jax.ShapeDtypeStruct((s, heads * HEAD_DIM), BF16),
        compiler_params=_params(("parallel", "parallel")),
    )(o, qkvz, w)


def _delta_post_bwd(dmix, o, qkvz, w, heads):
    s = o.shape[0]
    tr = _tile(s, 1024, 8)

    def body(d_ref, o_ref, z_ref, w_ref, do_ref, dz_ref, dw_ref):
        d, ov, z, wv = d_ref[...], o_ref[...], z_ref[...], w_ref[...]
        sg = _sigmoid(z)
        rstd = lax.rsqrt(jnp.mean(ov * ov, axis=-1, keepdims=True) + EPS)
        ohat = ov * rstd
        dz_ref[...] = (d * (ohat * wv) * sg * (1.0 + z * (1.0 - sg))).astype(BF16)
        dn = d * (z * sg)
        gw = dn * wv
        do_ref[...] = rstd * (gw - ohat * jnp.mean(gw * ohat, axis=-1, keepdims=True))

        @pl.when((pl.program_id(0) == 0) & (pl.program_id(1) == 0))
        def _():
            dw_ref[...] = jnp.zeros_like(dw_ref)

        dw_ref[...] += jnp.sum(dn * ohat, axis=0, keepdims=True)

    head = pl.BlockSpec((tr, HEAD_DIM), lambda i, h: (i, h))
    vec = pl.BlockSpec((1, HEAD_DIM), lambda i, h: (0, 0))
    dw = heads * HEAD_DIM
    return pl.pallas_call(
        body, name="delta_post_bwd", grid=(s // tr, heads),
        in_specs=[head, head, pl.BlockSpec((tr, HEAD_DIM), lambda i, h: (i, 3 * heads + h)), vec],
        out_specs=[head, head, vec],
        out_shape=[jax.ShapeDtypeStruct((s, dw), F32), jax.ShapeDtypeStruct((s, dw), BF16),
                   jax.ShapeDtypeStruct((1, HEAD_DIM), F32)],
        compiler_params=_params(("arbitrary", "arbitrary")),
    )(dmix, o, qkvz, w)


def _rope_tables(positions, s):
    half = HEAD_DIM // 2
    inv_freq = ROPE_THETA ** (-jnp.arange(half, dtype=F32) / half)
    ang = positions.reshape(s, 1).astype(F32) * inv_freq
    cos, sin = jnp.cos(ang), jnp.sin(ang)
    return jnp.concatenate([cos, cos], axis=-1), jnp.concatenate([-sin, sin], axis=-1)


def _attn_pre_fwd(aqkv, wq, wk, cosf, sinf, heads):
    s = aqkv.shape[0]
    tr = _tile(s, 1024, 8)

    def body(x_ref, wq_ref, wk_ref, c_ref, s_ref, o_ref):
        xv = x_ref[...]
        wv = jnp.where(pl.program_id(1) < heads, wq_ref[...], wk_ref[...])
        y = xv * lax.rsqrt(jnp.mean(xv * xv, axis=-1, keepdims=True) + EPS) * wv
        o_ref[...] = y * c_ref[...] + pltpu.roll(y, HEAD_DIM // 2, 1) * s_ref[...]

    blk = pl.BlockSpec((tr, HEAD_DIM), lambda i, j: (i, j))
    vec = pl.BlockSpec((1, HEAD_DIM), lambda i, j: (0, 0))
    tab = pl.BlockSpec((tr, HEAD_DIM), lambda i, j: (i, 0))
    return pl.pallas_call(
        body, name="attn_pre_fwd", grid=(s // tr, 2 * heads),
        in_specs=[blk, vec, vec, tab, tab], out_specs=blk,
        out_shape=jax.ShapeDtypeStruct((s, 2 * heads * HEAD_DIM), F32),
        compiler_params=_params(("parallel", "parallel")),
    )(aqkv, wq, wk, cosf, sinf)


ATT_ROWS = 2048


def _band():
    qi = lax.broadcasted_iota(jnp.int32, (SPAN, 2 * SPAN), 0)
    ki = lax.broadcasted_iota(jnp.int32, (SPAN, 2 * SPAN), 1)
    dist = qi + SPAN - ki
    return (dist >= 0) & (dist <= SPAN), ki >= SPAN


def _sub(g, r, d):
    if d == 1:
        return pl.ds(g * SPAN, SPAN)
    return pl.ds(g * SPAN * d + r, SPAN, stride=d)


def _attn_geometry(s, d):
    rb = min(ATT_ROWS, s)
    pr = SPAN * d
    return rb, pr, rb // pr, s // rb


def _attn_fwd(qk, aqkv, heads, d):
    s = qk.shape[0]
    aw = heads * HEAD_DIM
    rb, pr, g_cnt, nstep = _attn_geometry(s, d)

    def body(q_ref, k_ref, v_ref, kp_ref, vp_ref, o_ref, l_ref):
        first = pl.program_id(1) == 0
        band, own = _band()
        edge = band & (own | jnp.logical_not(first))
        for r in range(d):
            for g in range(g_cnt):
                rows = _sub(g, r, d)
                if g == 0:
                    kp, vp, mask = kp_ref[_sub(0, r, d), :], vp_ref[_sub(0, r, d), :], edge
                else:
                    kp, vp, mask = k_ref[_sub(g - 1, r, d), :], v_ref[_sub(g - 1, r, d), :], band
                kcat = jnp.concatenate([kp, k_ref[rows, :]], axis=0)
                vcat = jnp.concatenate([vp, v_ref[rows, :]], axis=0)
                sc = _dot(q_ref[rows, :], kcat, "nt") * (HEAD_DIM ** -0.5)
                sc = jnp.where(mask, sc, NEG)
                m = jnp.max(sc, axis=-1, keepdims=True)
                p = jnp.exp(sc - m)
                den = jnp.sum(p, axis=-1, keepdims=True)
                o_ref[rows, :] = _dot(p, vcat) / den
                l_ref[rows, :] = jnp.broadcast_to(m + jnp.log(den), (SPAN, HEAD_DIM))

    cur = lambda off: pl.BlockSpec((rb, HEAD_DIM), lambda h, j: (j, off + h))
    prev = lambda off: pl.BlockSpec((pr, HEAD_DIM), lambda h, j: (jnp.maximum(j * (rb // pr) - 1, 0), off + h))
    out = pl.BlockSpec((rb, HEAD_DIM), lambda h, j: (j, h))
    return pl.pallas_call(
        body, name=f"attn_fwd_d{d}", grid=(heads, nstep),
        in_specs=[cur(0), cur(heads), cur(2 * heads), prev(heads), prev(2 * heads)],
        out_specs=[out, out],
        out_shape=[jax.ShapeDtypeStruct((s, aw), F32)] * 2,
        compiler_params=_params(("parallel", "parallel")),
    )(qk, qk, aqkv, qk, aqkv)


def _attn_merge_fwd(os_, ls_, w, heads):
    s = os_[0].shape[0]
    tr = _tile(s, 512, 8)

    def body(o1, o2, o3, l1, l2, l3, w_ref, mix_ref, ob_ref, lse_ref):
        la, lb, lc = l1[...], l2[...], l3[...]
        m = jnp.maximum(jnp.maximum(la, lb), lc)
        ea, eb, ec = jnp.exp(la - m), jnp.exp(lb - m), jnp.exp(lc - m)
        tot = ea + eb + ec
        ob = (ea * o1[...] + eb * o2[...] + ec * o3[...]) / tot
        ob_ref[...] = ob
        lse_ref[...] = m + jnp.log(tot)
        rstd = lax.rsqrt(jnp.mean(ob * ob, axis=-1, keepdims=True) + EPS)
        mix_ref[...] = (ob * rstd * w_ref[...]).astype(BF16)

    blk = pl.BlockSpec((tr, HEAD_DIM), lambda i, h: (i, h))
    vec = pl.BlockSpec((1, HEAD_DIM), lambda i, h: (0, 0))
    aw = heads * HEAD_DIM
    return pl.pallas_call(
        body, name="attn_merge_fwd", grid=(s // tr, heads),
        in_specs=[blk] * 6 + [vec], out_specs=[blk, blk, blk],
        out_shape=[jax.ShapeDtypeStruct((s, aw), BF16), jax.ShapeDtypeStruct((s, aw), F32),
                   jax.ShapeDtypeStruct((s, aw), F32)],
        compiler_params=_params(("parallel", "parallel")),
    )(*os_, *ls_, w)


def _attn_merge_bwd(dmix, ob, w, heads):
    s = ob.shape[0]
    tr = _tile(s, 1024, 8)

    def body(d_ref, ob_ref, w_ref, do_ref, dsum_ref, dw_ref):
        d, ov = d_ref[...], ob_ref[...]
        rstd = lax.rsqrt(jnp.mean(ov * ov, axis=-1, keepdims=True) + EPS)
        ohat = ov * rstd
        gw = d * w_ref[...]
        dov = rstd * (gw - ohat * jnp.mean(gw * ohat, axis=-1, keepdims=True))
        do_ref[...] = dov
        dsum_ref[...] = jnp.broadcast_to(jnp.sum(dov * ov, axis=-1, keepdims=True), dov.shape)

        @pl.when((pl.program_id(0) == 0) & (pl.program_id(1) == 0))
        def _():
            dw_ref[...] = jnp.zeros_like(dw_ref)

        dw_ref[...] += jnp.sum(d * ohat, axis=0, keepdims=True)

    blk = pl.BlockSpec((tr, HEAD_DIM), lambda i, h: (i, h))
    vec = pl.BlockSpec((1, HEAD_DIM), lambda i, h: (0, 0))
    aw = heads * HEAD_DIM
    return pl.pallas_call(
        body, name="attn_merge_bwd", grid=(s // tr, heads),
        in_specs=[pl.BlockSpec((tr, HEAD_DIM), lambda i, h: (i, heads + h)), blk, vec],
        out_specs=[blk, blk, vec],
        out_shape=[jax.ShapeDtypeStruct((s, aw), F32), jax.ShapeDtypeStruct((s, aw), F32),
                   jax.ShapeDtypeStruct((1, HEAD_DIM), F32)],
        compiler_params=_params(("arbitrary", "arbitrary")),
    )(dmix, ob, w)


def _attn_bwd(qk, aqkv, do, lse, dsum, heads, d):
    s = qk.shape[0]
    aw = heads * HEAD_DIM
    rb, pr, g_cnt, nstep = _attn_geometry(s, d)
    scale = HEAD_DIM ** -0.5

    def body(q_ref, k_ref, v_ref, do_ref, l_ref, ds_ref, kp_ref, vp_ref, qn_ref, don_ref, ln_ref, dsn_ref, out_ref):
        first = pl.program_id(1) == 0
        last = pl.program_id(1) == nstep - 1
        band, own = _band()
        edge = band & (own | jnp.logical_not(first))
        dq_ref, dk_ref, dv_ref = out_ref.at[0], out_ref.at[1], out_ref.at[2]
        dk_ref[...] = jnp.zeros((rb, HEAD_DIM), F32)
        dv_ref[...] = jnp.zeros((rb, HEAD_DIM), F32)
        for r in range(d):
            for g in range(g_cnt):
                rows = _sub(g, r, d)
                if g == 0:
                    kp, vp, mask = kp_ref[_sub(0, r, d), :], vp_ref[_sub(0, r, d), :], edge
                else:
                    prows = _sub(g - 1, r, d)
                    kp, vp, mask = k_ref[prows, :], v_ref[prows, :], band
                q, dov = q_ref[rows, :], do_ref[rows, :]
                kcat = jnp.concatenate([kp, k_ref[rows, :]], axis=0)
                vcat = jnp.concatenate([vp, v_ref[rows, :]], axis=0)
                sc = _dot(q, kcat, "nt") * scale
                p = jnp.where(mask, jnp.exp(sc - l_ref[rows, 0:1]), 0.0)
                dsc = p * (_dot(dov, vcat, "nt") - ds_ref[rows, 0:1]) * scale
                dq_ref[rows, :] = _dot(dsc, kcat)
                dk = _dot(dsc, q, "tn")
                dv = _dot(p, dov, "tn")
                dk_ref[rows, :] += dk[SPAN:]
                dv_ref[rows, :] += dv[SPAN:]
                if g > 0:
                    dk_ref[prows, :] += dk[:SPAN]
                    dv_ref[prows, :] += dv[:SPAN]
            rows, nrows = _sub(g_cnt - 1, r, d), _sub(0, r, d)
            q, dov = qn_ref[nrows, :], don_ref[nrows, :]
            sc = _dot(q, k_ref[rows, :], "nt") * scale
            p = jnp.where(band[:, :SPAN] & jnp.logical_not(last), jnp.exp(sc - ln_ref[nrows, 0:1]), 0.0)
            dsc = p * (_dot(dov, v_ref[rows, :], "nt") - dsn_ref[nrows, 0:1]) * scale
            dk_ref[rows, :] += _dot(dsc, q, "tn")
            dv_ref[rows, :] += _dot(p, dov, "tn")

    per = rb // pr
    cur = lambda off: pl.BlockSpec((rb, HEAD_DIM), lambda h, j: (j, off + h))
    prev = lambda off: pl.BlockSpec((pr, HEAD_DIM), lambda h, j: (jnp.maximum(j * per - 1, 0), off + h))
    nxt = lambda off: pl.BlockSpec((pr, HEAD_DIM), lambda h, j: (jnp.minimum((j + 1) * per, s // pr - 1), off + h))
    return pl.pallas_call(
        body, name=f"attn_bwd_d{d}", grid=(heads, nstep),
        in_specs=[cur(0), cur(heads), cur(2 * heads), cur(0), cur(0), cur(0), prev(heads), prev(2 * heads),
                  nxt(0), nxt(0), nxt(0), nxt(0)],
        out_specs=pl.BlockSpec((3, rb, HEAD_DIM), lambda h, j: (0, j, h)),
        out_shape=jax.ShapeDtypeStruct((3, s, aw), F32),
        compiler_params=_params(("parallel", "parallel")),
    )(qk, qk, aqkv, do, lse, dsum, qk, aqkv, qk, do, lse, dsum)


def _attn_pre_bwd(grads, aqkv, wq, wk, cosf, sinf, heads):
    s = aqkv.shape[0]
    tr = _tile(s, 512, 8)
    nrow = s // tr

    def body(g1_ref, g2_ref, g3_ref, x_ref, wq_ref, wk_ref, c_ref, s_ref, dx_ref, dwq_ref, dwk_ref):
        i, j = pl.program_id(0), pl.program_id(1)
        kind = j // heads
        dout = g1_ref[0] + g2_ref[0] + g3_ref[0]
        tot_v = dout
        dy = dout * c_ref[...] + pltpu.roll(dout * s_ref[...], HEAD_DIM // 2, 1)
        xv = x_ref[...]
        wv = jnp.where(kind == 0, wq_ref[...], wk_ref[...])
        rstd = lax.rsqrt(jnp.mean(xv * xv, axis=-1, keepdims=True) + EPS)
        xhat = xv * rstd
        gw = dy * wv
        dxn = rstd * (gw - xhat * jnp.mean(gw * xhat, axis=-1, keepdims=True))
        dx_ref[...] = jnp.where(kind == 2, tot_v, dxn).astype(BF16)
        dwc = jnp.sum(dy * xhat, axis=0, keepdims=True)

        @pl.when((i == 0) & (j == 0))
        def _():
            dwq_ref[...] = jnp.zeros_like(dwq_ref)
            dwk_ref[...] = jnp.zeros_like(dwk_ref)

        @pl.when(kind == 0)
        def _():
            dwq_ref[...] += dwc

        @pl.when(kind == 1)
        def _():
            dwk_ref[...] += dwc

    grad = pl.BlockSpec((1, tr, HEAD_DIM), lambda i, j: (j // heads, i, j % heads))
    blk = pl.BlockSpec((tr, HEAD_DIM), lambda i, j: (i, j))
    vec = pl.BlockSpec((1, HEAD_DIM), lambda i, j: (0, 0))
    tab = pl.BlockSpec((tr, HEAD_DIM), lambda i, j: (i, 0))
    return pl.pallas_call(
        body, name="attn_pre_bwd", grid=(nrow, 3 * heads),
        in_specs=[grad, grad, grad, blk, vec, vec, tab, tab], out_specs=[blk, vec, vec],
        out_shape=[jax.ShapeDtypeStruct((s, 3 * heads * HEAD_DIM), BF16),
                   jax.ShapeDtypeStruct((1, HEAD_DIM), F32), jax.ShapeDtypeStruct((1, HEAD_DIM), F32)],
        compiler_params=_params(("arbitrary", "arbitrary")),
    )(*grads, aqkv, wq, wk, cosf, sinf)


def _swiglu_fwd(gu, ff):
    s = gu.shape[0]
    tr, tc = _tile(s, 512, 8), _tile(ff, 1024, 128)
    nc = ff // tc

    def body(g_ref, u_ref, o_ref):
        g = g_ref[...].astype(F32)
        o_ref[...] = (g * _sigmoid(g) * u_ref[...].astype(F32)).astype(BF16)

    return pl.pallas_call(
        body, name="swiglu_fwd", grid=(s // tr, nc),
        in_specs=[pl.BlockSpec((tr, tc), lambda i, j: (i, j)), pl.BlockSpec((tr, tc), lambda i, j: (i, j + nc))],
        out_specs=pl.BlockSpec((tr, tc), lambda i, j: (i, j)),
        out_shape=jax.ShapeDtypeStruct((s, ff), BF16),
        compiler_params=_params(("parallel", "parallel")),
    )(gu, gu)


def _swiglu_bwd(dact, gu, ff):
    s = gu.shape[0]
    tr, tc = _tile(s, 512, 8), _tile(ff, 1024, 128)
    nc = ff // tc

    def body(d_ref, g_ref, u_ref, dg_ref, du_ref):
        d, g = d_ref[...].astype(F32), g_ref[...].astype(F32)
        sg = _sigmoid(g)
        dg_ref[...] = (d * u_ref[...].astype(F32) * sg * (1.0 + g * (1.0 - sg))).astype(BF16)
        du_ref[...] = (d * g * sg).astype(BF16)

    lo = pl.BlockSpec((tr, tc), lambda i, j: (i, j))
    hi = pl.BlockSpec((tr, tc), lambda i, j: (i, j + nc))
    dg, du = pl.pallas_call(
        body, name="swiglu_bwd", grid=(s // tr, nc),
        in_specs=[lo, lo, hi], out_specs=[lo, lo],
        out_shape=[jax.ShapeDtypeStruct((s, ff), BF16)] * 2,
        compiler_params=_params(("parallel", "parallel")),
    )(dact, gu, gu)
    return dg, du


def _loss_head(y, target):
    s, d = y.shape
    tr = _tile(s, 512, 8)

    def body(y_ref, t_ref, dy_ref, dy16_ref, l_ref):
        err = y_ref[...] - t_ref[...]
        dy_ref[...] = err * (1.0 / d)
        dy16_ref[...] = (err * (1.0 / d)).astype(BF16)
        part = 0.5 * jnp.sum(jnp.sum(err * err, axis=-1, keepdims=True) * (1.0 / d), axis=0, keepdims=True)

        @pl.when(pl.program_id(0) == 0)
        def _():
            l_ref[...] = jnp.zeros_like(l_ref)

        lane = lax.broadcasted_iota(jnp.int32, (1, HEAD_DIM), 1)
        l_ref[...] += jnp.where(lane == 0, part, 0.0)

    row = pl.BlockSpec((tr, d), lambda i: (i, 0))
    return pl.pallas_call(
        body, name="loss_head", grid=(s // tr,),
        in_specs=[row, row], out_specs=[row, row, pl.BlockSpec((1, HEAD_DIM), lambda i: (0, 0))],
        out_shape=[jax.ShapeDtypeStruct((s, d), F32), jax.ShapeDtypeStruct((s, d), BF16),
                   jax.ShapeDtypeStruct((1, HEAD_DIM), F32)],
        compiler_params=_params(("arbitrary",)),
    )(y, target)


def _colsum(a, name):
    s, d = a.shape
    tr = _tile(s, 1024, 8)

    def body(a_ref, o_ref):
        @pl.when(pl.program_id(0) == 0)
        def _():
            o_ref[...] = jnp.zeros_like(o_ref)

        o_ref[...] += jnp.sum(a_ref[...], axis=0, keepdims=True)

    return pl.pallas_call(
        body, name=name, grid=(s // tr,),
        in_specs=[pl.BlockSpec((tr, d), lambda i: (i, 0))], out_specs=pl.BlockSpec((1, d), lambda i: (0, 0)),
        out_shape=jax.ShapeDtypeStruct((1, d), F32),
        compiler_params=_params(("arbitrary",)),
    )(a)


def _local_step(x, positions, target, small, w_qkvz, w_bd, w_attn, conv_w, later_weights, ffn_grads_ready,
                after=None):
    s, dmod = x.shape
    heads = dmod // (2 * HEAD_DIM)
    dw = heads * HEAD_DIM
    a_log, dt_bias = small["a_log"].reshape(heads, 1, 1), small["dt_bias"].reshape(heads, 1, 1)
    cosf, sinf = _rope_tables(positions, s)

    h1 = _rmsnorm_fwd(x, small["attn_norm_w"], "norm1_fwd", after=after)
    qkvz = _matmul(h1, w_qkvz, "nn", "proj_qkvz")
    bd = _matmul(h1, w_bd, "nn", "proj_bd")
    aqkv = _matmul(h1, w_attn, "nn", "proj_attn")
    dqkv = _delta_pre_fwd(qkvz, conv_w, heads)
    o_d, states = _delta_chunk_fwd(dqkv, bd, a_log, dt_bias, heads)
    mix_a = _delta_post_fwd(o_d, qkvz, small["delta_out_norm_w"], heads)
    qk_rot = _attn_pre_fwd(aqkv, small["q_norm_w"], small["k_norm_w"], cosf, sinf, heads)
    outs = [_attn_fwd(qk_rot, aqkv, heads, d) for d in DILATIONS]
    mix_b, ob, lse = _attn_merge_fwd([o for o, _ in outs], [l for _, l in outs], small["attn_out_norm_w"], heads)
    w_out, w_gu, w_down = later_weights((mix_a, mix_b))
    ff = w_down.shape[0]
    x1 = _matmul(mix_a, w_out, "nn", "out_proj_a", add=x, b_rows=(0, dw))
    x1 = _matmul(mix_b, w_out, "nn", "out_proj_b", add=x1, b_rows=(dw, dw))
    h2 = _rmsnorm_fwd(x1, small["ffn_norm_w"], "norm2_fwd")
    gu = _matmul(h2, w_gu, "nn", "ffn_gate_up", out_dtype=BF16)
    act = _swiglu_fwd(gu, ff)
    y = _matmul(act, w_down, "nn", "ffn_down", add=x1)
    dy, dy16, loss_row = _loss_head(y, target)

    dact = _matmul(dy16, w_down, "nt", "ffn_down_dx", out_dtype=BF16)
    g_w_down = _matmul(act, dy16, "tn", "ffn_down_dw")
    dgate, dup = _swiglu_bwd(dact, gu, ff)
    dh2 = _matmul(dgate, w_gu, "nt", "ffn_gate_dx", b_cols=(0, ff))
    dh2 = _matmul(dup, w_gu, "nt", "ffn_up_dx", b_cols=(ff, ff), add=dh2)
    g_w_gate = _matmul(h2, dgate, "tn", "ffn_gate_dw")
    g_w_up = _matmul(h2, dup, "tn", "ffn_up_dw")
    behind = ffn_grads_ready(g_w_gate, g_w_up, g_w_down)
    dx1, dx1_16, g_ffn_norm = _rmsnorm_bwd(dh2, x1, small["ffn_norm_w"], dy, "norm2_bwd", after=behind)
    dmix = _matmul(dx1_16, w_out, "nt", "out_proj_dx")
    g_w_out_a = _matmul(mix_a, dx1_16, "tn", "out_proj_dw_a")
    g_w_out_b = _matmul(mix_b, dx1_16, "tn", "out_proj_dw_b")
    dob, dsum, g_attn_out_norm = _attn_merge_bwd(dmix, ob, small["attn_out_norm_w"], heads)
    grads = [_attn_bwd(qk_rot, aqkv, dob, lse, dsum, heads, d) for d in DILATIONS]
    d_aqkv, g_q_norm, g_k_norm = _attn_pre_bwd(grads, aqkv, small["q_norm_w"], small["k_norm_w"], cosf, sinf, heads)
    do_d, dz, g_delta_out_norm = _delta_post_bwd(dmix, o_d, qkvz, small["delta_out_norm_w"], heads)
    ddqkv, dbd = _delta_chunk_bwd(do_d, dqkv, bd, states, a_log, dt_bias, heads)
    d_qkv_raw, g_conv = _delta_pre_bwd(ddqkv, qkvz, conv_w, heads)
    bd_sums = _colsum(dbd, "bd_colsum")
    dh1 = _matmul(d_qkv_raw, w_qkvz, "nt", "proj_qkv_dx", b_cols=(0, 3 * dw))
    dh1 = _matmul(dz, w_qkvz, "nt", "proj_z_dx", b_cols=(3 * dw, dw), add=dh1)
    dh1 = _matmul(d_aqkv, w_attn, "nt", "proj_attn_dx", add=dh1)
    dh1 = _matmul(dbd, w_bd, "nt", "proj_bd_dx", add=dh1)
    g_w_qkv = _matmul(h1, d_qkv_raw, "tn", "proj_qkv_dw")
    g_w_z = _matmul(h1, dz, "tn", "proj_z_dw")
    g_w_bd = _matmul(h1, dbd, "tn", "proj_bd_dw")
    g_w_attn = _matmul(h1, d_aqkv, "tn", "proj_attn_dw")
    grad_x, _, g_attn_norm = _rmsnorm_bwd(dh1, x, small["attn_norm_w"], dx1, "norm1_bwd")
    small_grads = dict(
        attn_norm_w=g_attn_norm, a_log=bd_sums[:, 2 * heads:3 * heads], dt_bias=bd_sums[:, heads:2 * heads],
        delta_out_norm_w=g_delta_out_norm, q_norm_w=g_q_norm, k_norm_w=g_k_norm,
        attn_out_norm_w=g_attn_out_norm, ffn_norm_w=g_ffn_norm, conv_w=g_conv)
    big_grads = dict(w_qkv=g_w_qkv, w_z=g_w_z, w_bd=g_w_bd, w_attn=g_w_attn, w_out_a=g_w_out_a, w_out_b=g_w_out_b)
    return loss_row, grad_x, small_grads, big_grads


def _adamw(w, g, m, v, name):
    r, c = w.shape
    tr = _tile(r, 256, 8)

    def body(w_ref, g_ref, m_ref, v_ref, d_ref, nm_ref, nv_ref):
        gv = g_ref[...]
        nm = ADAM_B1 * m_ref[...] + (1.0 - ADAM_B1) * gv
        nv = ADAM_B2 * v_ref[...] + (1.0 - ADAM_B2) * (gv * gv)
        m_hat = nm / (1.0 - ADAM_B1 ** ADAM_STEP)
        v_hat = nv / (1.0 - ADAM_B2 ** ADAM_STEP)
        d_ref[...] = -ADAM_LR * (m_hat / (jnp.sqrt(v_hat) + ADAM_EPS) + ADAM_WD * w_ref[...])
        nm_ref[...] = nm
        nv_ref[...] = nv

    blk = pl.BlockSpec((tr, c), lambda i: (i, 0))
    return pl.pallas_call(
        body, name=name, grid=(r // tr,),
        in_specs=[blk] * 4, out_specs=[blk] * 3,
        out_shape=[jax.ShapeDtypeStruct((r, c), F32)] * 3,
        compiler_params=_params(("parallel",)),
    )(w, g, m, v)


def _add_half_bf16(g, b, place, name):
    n, half, c = b.shape
    tr = _tile(half, 512, 16)
    nb = half // tr

    def body(place_ref, g_ref, b_ref, o_ref):
        o_ref[...] = (g_ref[...].astype(F32) + b_ref[...].astype(F32)).astype(BF16)

    blk = pl.BlockSpec((1, tr, c), lambda i, j, p: (i, j, 0))
    return pl.pallas_call(
        body, name=name,
        grid_spec=pltpu.PrefetchScalarGridSpec(
            num_scalar_prefetch=1, grid=(n, nb),
            in_specs=[pl.BlockSpec((1, tr, c), lambda i, j, p: (i, p[0] * nb + j, 0)), blk], out_specs=blk),
        out_shape=jax.ShapeDtypeStruct((n, half, c), BF16),
        compiler_params=_params(("parallel", "parallel")),
    )(place, g, b)


def _sum4_f32(mine, others, place, name):
    _, half, c = mine.shape
    tr = _tile(half, 512, 16)
    nb = half // tr

    def body(place_ref, a_ref, b_ref, o_ref):
        acc = a_ref[0].astype(F32)
        for j in range(3):
            acc = acc + b_ref[j].astype(F32)
        o_ref[...] = acc

    return pl.pallas_call(
        body, name=name,
        grid_spec=pltpu.PrefetchScalarGridSpec(
            num_scalar_prefetch=1, grid=(nb,),
            in_specs=[pl.BlockSpec((1, tr, c), lambda i, p: (p[1], i, 0)),
                      pl.BlockSpec((3, tr, c), lambda i, p: (0, i, 0))],
            out_specs=pl.BlockSpec((tr, c), lambda i, p: (p[0] * nb + i, 0))),
        out_shape=jax.ShapeDtypeStruct((2 * half, c), F32),
        compiler_params=_params(("parallel",)),
    )(place, mine, others)


def _place():
    x, y, c = lax.axis_index("x"), lax.axis_index("y"), lax.axis_index("c")
    other_chips = [(1 - x, y), (x, 1 - y), (1 - x, 1 - y)]
    return x, y, c, (x, y, 1 - c), other_chips


ANY = pl.BlockSpec(memory_space=pl.ANY)


def _remote(k, src, dst, to, send_sems, recv_sems):
    return pltpu.make_async_remote_copy(src_ref=src, dst_ref=dst, send_sem=send_sems.at[k], recv_sem=recv_sems.at[k],
                                        device_id=to, device_id_type=MESH)


def _half(ref, lead, hc):
    if lead is None:
        half = ref.shape[0] // 2
        return ref.at[pl.ds(hc * half, half), :]
    half = ref.shape[1] // 2
    return ref.at[lead, pl.ds(hc * half, half), :]


def _all_gather_weights(slots, whole, name):
    nt, nw = len(slots), len(whole)
    n_ici = 3 * (nt + nw)

    def body(*refs):
        ins, outs = refs[:nt + nw], refs[nt + nw:2 * (nt + nw)]
        send_sems, recv_sems = refs[2 * (nt + nw):]
        x, y, c, sibling, chips = _place()
        me = 2 * x + y
        first = []
        for j, (px, py) in enumerate(chips):
            for t in range(nt):
                first.append(_remote(j * (nt + nw) + t, _half(ins[t], me, c), _half(outs[t], me, c), (px, py, c),
                                     send_sems, recv_sems))
            for t in range(nt, nt + nw):
                first.append(_remote(j * (nt + nw) + t, ins[t].at[me], outs[t].at[me], (px, py, c),
                                     send_sems, recv_sems))
        for cp in first:
            cp.start()
        passed = []
        for j, (px, py) in enumerate(chips):
            for t in range(nt):
                landed = _half(outs[t], 2 * px + py, c)
                _remote(j * (nt + nw) + t, landed, landed, (px, py, c), send_sems, recv_sems).wait_recv()
                fw = _remote(n_ici + j * nt + t, landed, landed, sibling, send_sems, recv_sems)
                fw.start()
                passed.append(fw)
            for t in range(nt, nt + nw):
                landed = outs[t].at[2 * px + py]
                _remote(j * (nt + nw) + t, landed, landed, (px, py, c), send_sems, recv_sems).wait_recv()
        for j, (px, py) in enumerate(chips):
            for t in range(nt):
                other = _half(outs[t], 2 * px + py, 1 - c)
                _remote(n_ici + j * nt + t, other, other, sibling, send_sems, recv_sems).wait_recv()
        for cp in first + passed:
            cp.wait_send()

    arrays = list(slots) + list(whole)
    n_sem = n_ici + 3 * nt
    return pl.pallas_call(
        body, name=name, in_specs=[ANY] * len(arrays), out_specs=[ANY] * len(arrays),
        input_output_aliases={i: i for i in range(len(arrays))},
        out_shape=[jax.ShapeDtypeStruct(a.shape, a.dtype) for a in arrays],
        scratch_shapes=[pltpu.SemaphoreType.DMA((n_sem,)), pltpu.SemaphoreType.DMA((n_sem,))],
    )(*arrays)


def _swap_halves_with_sibling(gs, name):
    nt = len(gs)

    def body(*refs):
        g_refs, o_refs, (send_sems, recv_sems) = refs[:nt], refs[nt:2 * nt], refs[2 * nt:]
        _, _, c, sibling, _ = _place()
        cps = []
        for t in range(nt):
            half = g_refs[t].shape[1] // 2
            cps.append(_remote(t, g_refs[t].at[:, pl.ds((1 - c) * half, half), :], o_refs[t], sibling,
                               send_sems, recv_sems))
        for cp in cps:
            cp.start()
        for cp in cps:
            cp.wait()

    return pl.pallas_call(
        body, name=name, in_specs=[ANY] * nt, out_specs=[ANY] * nt,
        out_shape=[jax.ShapeDtypeStruct((g.shape[0], g.shape[1] // 2, g.shape[2]), g.dtype) for g in gs],
        scratch_shapes=[pltpu.SemaphoreType.DMA((nt,)), pltpu.SemaphoreType.DMA((nt,))],
    )(*gs)


def _scatter_to_chips(ps, name):
    nt = len(ps)

    def body(*refs):
        p_refs, b_refs, (send_sems, recv_sems) = refs[:nt], refs[nt:2 * nt], refs[2 * nt:]
        _, _, c, _, chips = _place()
        cps = [_remote(j * nt + t, p_refs[t].at[2 * px + py], b_refs[t].at[j], (px, py, c), send_sems, recv_sems)
               for j, (px, py) in enumerate(chips) for t in range(nt)]
        for cp in cps:
            cp.start()
        for cp in cps:
            cp.wait()

    return pl.pallas_call(
        body, name=name, in_specs=[ANY] * nt, out_specs=[ANY] * nt,
        out_shape=[jax.ShapeDtypeStruct((3,) + p.shape[1:], p.dtype) for p in ps],
        scratch_shapes=[pltpu.SemaphoreType.DMA((3 * nt,)), pltpu.SemaphoreType.DMA((3 * nt,))],
    )(*ps)


def _join_halves(fs, name):
    nt = len(fs)

    def body(*refs):
        in_refs, out_refs, (send_sems, recv_sems) = refs[:nt], refs[nt:2 * nt], refs[2 * nt:]
        _, _, c, sibling, _ = _place()
        cps = [_remote(t, _half(in_refs[t], None, c), _half(out_refs[t], None, c), sibling, send_sems, recv_sems)
               for t in range(nt)]
        for cp in cps:
            cp.start()
        for t in range(nt):
            theirs = _half(out_refs[t], None, 1 - c)
            _remote(t, theirs, theirs, sibling, send_sems, recv_sems).wait_recv()
        for cp in cps:
            cp.wait_send()

    return pl.pallas_call(
        body, name=name, in_specs=[ANY] * nt, out_specs=[ANY] * nt,
        input_output_aliases={i: i for i in range(nt)},
        out_shape=[jax.ShapeDtypeStruct(f.shape, f.dtype) for f in fs],
        scratch_shapes=[pltpu.SemaphoreType.DMA((nt,)), pltpu.SemaphoreType.DMA((nt,))],
    )(*fs)


HBM = pl.BlockSpec(memory_space=pltpu.HBM)
SEM = pl.BlockSpec(memory_space=pltpu.SEMAPHORE)
EFFECT = pltpu.SideEffectType.DATAFLOW_SIDE_EFFECTING


def _split_start(arrays, after, plan, n_copies, name):
    na = len(arrays)

    def body(*refs):
        ins, send_sems, recv_sems = refs[:na], refs[na + 1], refs[na + 2]
        outs, token = refs[na + 3:2 * na + 3], refs[2 * na + 3]
        for k, (src, dst, to) in enumerate(plan(ins, outs)):
            _remote(k, src, dst, to, send_sems, recv_sems).start()
        token[...] = jnp.zeros_like(token)

    res = pl.pallas_call(
        body, name=name,
        out_shape=(pltpu.SemaphoreType.DMA((n_copies,)), pltpu.SemaphoreType.DMA((n_copies,)),
                   *[pltpu.HBM(a.shape, a.dtype) for a in arrays], jax.ShapeDtypeStruct((8, HEAD_DIM), F32)),
        in_specs=[HBM] * na + [ANY],
        out_specs=(SEM, SEM, *[HBM] * na, pl.BlockSpec(memory_space=pltpu.VMEM)),
        input_output_aliases={i: 2 + i for i in range(na)},
        compiler_params=pltpu.CompilerParams(has_side_effects=EFFECT),
    )(*[pltpu.with_memory_space_constraint(a, pltpu.HBM) for a in arrays], after)
    return res[0], res[1], list(res[2:2 + na]), res[2 + na]


def _split_wait(send_sems, recv_sems, arrays, after, plan, name):
    na = len(arrays)
    after = list(after) if isinstance(after, (list, tuple)) else [after]

    def body(*refs):
        ins, send, recv, outs = refs[:na], refs[na], refs[na + 1], refs[na + 2 + len(after):]
        for k, (src, dst, to) in enumerate(plan(ins, outs)):
            cp = _remote(k, src, dst, to, send, recv)
            cp.wait_send()
            cp.wait_recv()

    res = pl.pallas_call(
        body, name=name, out_shape=tuple(pltpu.HBM(a.shape, a.dtype) for a in arrays),
        in_specs=[HBM] * na + [SEM, SEM] + [ANY] * len(after), out_specs=tuple([HBM] * na),
        input_output_aliases={i: i for i in range(na)},
        compiler_params=pltpu.CompilerParams(has_side_effects=EFFECT),
    )(*arrays, send_sems, recv_sems, *after)
    return list(res)


def _gather_plan(nt):
    def plan(ins, outs):
        x, y, c, _, chips = _place()
        me = 2 * x + y
        return [(_half(ins[t], me, c), _half(outs[t], me, c), (px, py, c)) for px, py in chips for t in range(nt)]
    return plan


def _gather_landed_plan(nt):
    def plan(ins, outs):
        _, _, c, _, chips = _place()
        return [(_half(outs[t], 2 * px + py, c), _half(outs[t], 2 * px + py, c), (px, py, c))
                for px, py in chips for t in range(nt)]
    return plan


def _scatter_plan(nt):
    def plan(ins, outs):
        _, _, c, _, chips = _place()
        return [(ins[t].at[2 * px + py], outs[nt + t].at[j], (px, py, c))
                for j, (px, py) in enumerate(chips) for t in range(nt)]
    return plan


def _pass_halves_to_sibling(slots, name):
    nt = len(slots)

    def body(*refs):
        ins, outs, (send_sems, recv_sems) = refs[:nt], refs[nt:2 * nt], refs[2 * nt:]
        _, _, c, sibling, chips = _place()
        cps = [_remote(j * nt + t, _half(ins[t], 2 * px + py, c), _half(outs[t], 2 * px + py, c), sibling,
                       send_sems, recv_sems)
               for j, (px, py) in enumerate(chips) for t in range(nt)]
        for cp in cps:
            cp.start()
        for j, (px, py) in enumerate(chips):
            for t in range(nt):
                other = _half(outs[t], 2 * px + py, 1 - c)
                _remote(j * nt + t, other, other, sibling, send_sems, recv_sems).wait_recv()
        for cp in cps:
            cp.wait_send()

    return pl.pallas_call(
        body, name=name, in_specs=[ANY] * nt, out_specs=[ANY] * nt,
        input_output_aliases={i: i for i in range(nt)},
        out_shape=[jax.ShapeDtypeStruct(a.shape, a.dtype) for a in slots],
        scratch_shapes=[pltpu.SemaphoreType.DMA((3 * nt,)), pltpu.SemaphoreType.DMA((3 * nt,))],
    )(*slots)


def _all_reduce_small(v):
    r, lanes = v.shape

    def body(v_ref, out_ref, buf, send_sems, recv_sems):
        x, y, c, sibling, chips = _place()

        def slot(px, py, pc):
            return buf.at[4 * px + 2 * py + pc]

        def copy(k, block, to, src=None):
            return pltpu.make_async_remote_copy(src_ref=slot(*block) if src is None else src, dst_ref=slot(*block),
                                                send_sem=send_sems.at[k], recv_sem=recv_sems.at[k],
                                                device_id=to, device_id_type=MESH)

        me = (x, y, c)
        buf[4 * x + 2 * y + c] = v_ref[...]
        first = [copy(0, me, sibling, src=v_ref)]
        first += [copy(1 + j, me, (*chip, c), src=v_ref) for j, chip in enumerate(chips)]
        for cp in first:
            cp.start()
        passed = [copy(4 + j, (*chip, c), sibling) for j, chip in enumerate(chips)]
        for j, chip in enumerate(chips):
            copy(1 + j, (*chip, c), me).wait_recv()
            passed[j].start()
        copy(0, (x, y, 1 - c), me).wait_recv()
        for j, chip in enumerate(chips):
            copy(4 + j, (*chip, 1 - c), me).wait_recv()
        for cp in first + passed:
            cp.wait_send()
        acc = buf[0]
        for k in range(1, 8):
            acc = acc + buf[k]
        out_ref[...] = acc

    vmem = pl.BlockSpec(memory_space=pltpu.VMEM)
    return pl.pallas_call(
        body, name="all_reduce_small", in_specs=[vmem], out_specs=vmem,
        out_shape=jax.ShapeDtypeStruct((r, lanes), F32),
        scratch_shapes=[pltpu.VMEM((8, r, lanes), F32), pltpu.SemaphoreType.DMA((7,)), pltpu.SemaphoreType.DMA((7,))],
    )(v)


def _size(shape):
    n = 1
    for d in shape:
        n *= d
    return n


def _pack_small(parts):
    rows = []
    for p in parts:
        f = p.reshape(-1).astype(F32)
        n = -(-f.shape[0] // HEAD_DIM) * HEAD_DIM
        rows.append(jnp.pad(f, (0, n - f.shape[0])).reshape(-1, HEAD_DIM))
    a = jnp.concatenate(rows, axis=0)
    return jnp.pad(a, ((0, -a.shape[0] % 8), (0, 0)))


def _unpack_small(a, shapes):
    out, row = [], 0
    for shp in shapes:
        nrows = -(-_size(shp) // HEAD_DIM)
        out.append(a[row:row + nrows].reshape(-1)[:_size(shp)].reshape(shp))
        row += nrows
    return out


SMALL = ["attn_norm_w", "a_log", "dt_bias", "delta_out_norm_w", "q_norm_w", "k_norm_w", "attn_out_norm_w", "ffn_norm_w"]
BIG = ["w_in", "w_out", "w_gate_up", "w_down"]
ORDER = ["attn_norm_w", "w_in", "conv_w", "a_log", "dt_bias", "delta_out_norm_w", "q_norm_w", "k_norm_w",
         "attn_out_norm_w", "w_out", "ffn_norm_w", "w_gate_up", "w_down"]


def kernel(x, positions, attn_norm_w, w_in, conv_w, a_log, dt_bias, delta_out_norm_w, q_norm_w, k_norm_w, attn_out_norm_w, w_out, ffn_norm_w, w_gate_up, w_down, loss_target, m_attn_norm_w, m_w_in, m_conv_w, m_a_log, m_dt_bias, m_delta_out_norm_w, m_q_norm_w, m_k_norm_w, m_attn_out_norm_w, m_w_out, m_ffn_norm_w, m_w_gate_up, m_w_down, v_attn_norm_w, v_w_in, v_conv_w, v_a_log, v_dt_bias, v_delta_out_norm_w, v_q_norm_w, v_k_norm_w, v_attn_out_norm_w, v_w_out, v_ffn_norm_w, v_w_gate_up, v_w_down):
    wts = dict(attn_norm_w=attn_norm_w, w_in=w_in, conv_w=conv_w, a_log=a_log, dt_bias=dt_bias,
               delta_out_norm_w=delta_out_norm_w, q_norm_w=q_norm_w, k_norm_w=k_norm_w,
               attn_out_norm_w=attn_out_norm_w, w_out=w_out, ffn_norm_w=ffn_norm_w, w_gate_up=w_gate_up, w_down=w_down)
    mom = dict(attn_norm_w=m_attn_norm_w, w_in=m_w_in, conv_w=m_conv_w, a_log=m_a_log, dt_bias=m_dt_bias,
               delta_out_norm_w=m_delta_out_norm_w, q_norm_w=m_q_norm_w, k_norm_w=m_k_norm_w,
               attn_out_norm_w=m_attn_out_norm_w, w_out=m_w_out, ffn_norm_w=m_ffn_norm_w, w_gate_up=m_w_gate_up,
               w_down=m_w_down)
    var = dict(attn_norm_w=v_attn_norm_w, w_in=v_w_in, conv_w=v_conv_w, a_log=v_a_log, dt_bias=v_dt_bias,
               delta_out_norm_w=v_delta_out_norm_w, q_norm_w=v_q_norm_w, k_norm_w=v_k_norm_w,
               attn_out_norm_w=v_attn_out_norm_w, w_out=v_w_out, ffn_norm_w=v_ffn_norm_w, w_gate_up=v_w_gate_up,
               w_down=v_w_down)
    dmod = x.shape[2]
    heads = dmod // (2 * HEAD_DIM)
    dw = heads * HEAD_DIM
    chip = 2 * lax.axis_index("x") + lax.axis_index("y")
    core = lax.axis_index("c")
    n_in, n_out, n_gu, n_down, n_conv = (w_in.shape[2], w_out.shape[1], w_gate_up.shape[2], w_down.shape[1],
                                         conv_w.shape[2])

    def slots_of(w, dtype):
        shard = w[0].astype(dtype)
        return lax.dynamic_update_index_in_dim(lax.empty((4,) + shard.shape, dtype), shard, chip, axis=0)

    s_in, s_conv = _all_gather_weights([slots_of(w_in, BF16)], [slots_of(conv_w, F32)], "all_gather_w_in")
    later = [slots_of(w_out, BF16), slots_of(w_gate_up, BF16), slots_of(w_down, BF16)]
    w_send, w_recv, later, started = _split_start(later, s_conv, _gather_plan(3), 9, "gather_rest_start")
    by_cols = lambda a: a.transpose(1, 0, 2).reshape(a.shape[1], 4 * a.shape[2])
    w_in_f, conv_f = by_cols(s_in), by_cols(s_conv)
    w_bd = jnp.pad(w_in_f[:, 4 * dw:4 * dw + 2 * heads], ((0, 0), (0, HEAD_DIM - 2 * heads)))
    small = {n: wts[n] for n in SMALL}
    place = jnp.stack([core, chip]).astype(jnp.int32)
    to_slots = lambda a: a.reshape(a.shape[0], 4, a.shape[1] // 4).transpose(1, 0, 2)

    def later_weights(after):
        landed = _split_wait(w_send, w_recv, later, after, _gather_landed_plan(3), "gather_rest_wait")
        s_out, s_gu, s_down = _pass_halves_to_sibling(landed, "gather_rest_pass")
        return s_out.reshape(4 * n_out, dmod), by_cols(s_gu), s_down.reshape(4 * n_down, dmod)

    ffn = {}

    def ffn_grads_ready(g_gate, g_up, g_down):
        gs = [to_slots(jnp.concatenate([g_gate, g_up], axis=1)), g_down.reshape(4, n_down, dmod)]
        from_sibling = _swap_halves_with_sibling(gs, "swap_ffn_grad_halves")
        sums = [_add_half_bf16(g, b, place, "chip_partial_sum_" + n)
                for g, b, n in zip(gs, from_sibling, ["w_gate_up", "w_down"])]
        zones = [lax.empty((3,) + p.shape[1:], BF16) for p in sums]
        ffn["send"], ffn["recv"], ffn["bufs"], token = _split_start(sums + zones, from_sibling[0], _scatter_plan(2), 6,
                                                                    "scatter_ffn_start")
        return token

    loss_row, grad_x, sg, bg = _local_step(
        x[0], positions[0], loss_target[0], small, w_in_f[:, :4 * dw], w_bd, w_in_f[:, 4 * dw + 2 * heads:], conv_f,
        later_weights, ffn_grads_ready, after=started)

    sum_gu, sum_down, got_gu, got_down = _split_wait(ffn["send"], ffn["recv"], ffn["bufs"], grad_x, _scatter_plan(2),
                                                     "scatter_ffn_wait")
    gs = [to_slots(jnp.concatenate([bg["w_qkv"], bg["w_z"], bg["w_bd"][:, :2 * heads], bg["w_attn"]], axis=1)),
          jnp.concatenate([bg["w_out_a"], bg["w_out_b"]], axis=0).reshape(4, n_out, dmod)]
    from_sibling = _swap_halves_with_sibling(gs, "swap_grad_halves")
    sum_in, sum_out = [_add_half_bf16(g, b, place, "chip_partial_sum_" + n)
                       for g, b, n in zip(gs, from_sibling, ["w_in", "w_out"])]
    got_in, got_out = _scatter_to_chips([sum_in, sum_out], "scatter_to_chips")
    totals = _join_halves([_sum4_f32(p, b, place, "grad_total_" + n)
                           for p, b, n in zip([sum_in, sum_out, sum_gu, sum_down],
                                              [got_in, got_out, got_gu, got_down], BIG)], "join_halves")
    g_big = dict(zip(BIG, totals))

    reduced = _all_reduce_small(_pack_small([sg[n] for n in SMALL] + [sg["conv_w"], loss_row]))
    red = _unpack_small(reduced, [wts[n].shape for n in SMALL] + [(4, 4 * n_conv), (1, HEAD_DIM)])
    g_small = dict(zip(SMALL, red[:len(SMALL)]))
    g_conv_full, loss_out = red[len(SMALL)], red[len(SMALL) + 1]
    g_small["conv_w"] = lax.dynamic_slice_in_dim(g_conv_full, chip * n_conv, n_conv, axis=1).reshape(conv_w.shape)

    grads, deltas, new_m, new_v = {}, {}, {}, {}
    for n in BIG:
        shp = wts[n].shape
        d, nm, nv = _adamw(wts[n][0], g_big[n], mom[n][0], var[n][0], "adamw_" + n)
        grads[n], deltas[n], new_m[n], new_v[n] = g_big[n].reshape(shp), d.reshape(shp), nm.reshape(shp), nv.reshape(shp)
    names = SMALL + ["conv_w"]
    shapes = [wts[n].shape for n in names]
    d, nm, nv = _adamw(_pack_small([wts[n] for n in names]), _pack_small([g_small[n] for n in names]),
                       _pack_small([mom[n] for n in names]), _pack_small([var[n] for n in names]), "adamw_small")
    for n, dd, mm, vv in zip(names, _unpack_small(d, shapes), _unpack_small(nm, shapes), _unpack_small(nv, shapes)):
        grads[n], deltas[n], new_m[n], new_v[n] = g_small[n], dd, mm, vv
    return (loss_out[0, 0], grad_x[None], *[grads[n] for n in ORDER], *[deltas[n] for n in ORDER],
            *[new_m[n] for n in ORDER], *[new_v[n] for n in ORDER])
```

```python
import functools

import jax
import jax.numpy as jnp
from jax import lax
from jax.experimental import pallas as pl
from jax.experimental.pallas import tpu as pltpu

F32 = jnp.float32
BF16 = jnp.bfloat16
HEAD_DIM = 128
CHUNK = 64
SPAN = 128
DILATIONS = (1, 4, 16)
ROPE_THETA = 10000.0
EPS = 1e-6
NEG = -1e30
ADAM_LR, ADAM_B1, ADAM_B2, ADAM_EPS, ADAM_WD, ADAM_STEP = 0.001, 0.9, 0.999, 1e-08, 0.01, 10
VMEM_LIMIT = 48 * 1024 * 1024
MESH = pl.DeviceIdType.MESH

_DN = {"nn": (((1,), (0,)), ((), ())), "nt": (((1,), (1,)), ((), ())), "tn": (((0,), (0,)), ((), ()))}


def _dot(a, b, mode="nn"):
    (ca, cb), _ = _DN[mode]
    if a.ndim == 3:
        dn = (((ca[0] + 1,), (cb[0] + 1,)), ((0,), (0,)))
    else:
        dn = _DN[mode]
    return lax.dot_general(a.astype(BF16), b.astype(BF16), dn, preferred_element_type=F32)


def _rsum(x):
    return jnp.sum(x, axis=-1, keepdims=True)


def _csum(x):
    return jnp.sum(x, axis=-2, keepdims=True)


def _tile(dim, pref, unit=128):
    t = (min(pref, dim) // unit) * unit
    while t >= unit:
        if dim % t == 0:
            return t
        t -= unit
    return dim


def _params(sem):
    return pltpu.CompilerParams(dimension_semantics=sem, vmem_limit_bytes=VMEM_LIMIT)


def _sigmoid(x):
    return 1.0 / (1.0 + jnp.exp(-x))


def _matmul(a, b, mode, name, add=None, out_dtype=F32, a_cols=None, b_cols=None, b_rows=None):
    if mode == "tn":
        out_dtype = BF16
    a_off, a_w = a_cols if a_cols else (0, a.shape[1])
    b_off, b_w = b_cols if b_cols else (0, b.shape[1])
    br_off, br_n = b_rows if b_rows else (0, b.shape[0])
    if mode == "nn":
        m, k, n = a.shape[0], a_w, b_w
        assert br_n == k
    elif mode == "nt":
        m, k, n = a.shape[0], a_w, b.shape[0]
        assert b_w == k
    else:
        k, m, n = a.shape[0], a_w, b_w
        assert b.shape[0] == k
    tm, tn = _tile(m, 1024, 128), _tile(n, 1024, 128)
    tk = _tile(k, 2048 if a.dtype == BF16 and b.dtype == BF16 else 1024, 128)
    if mode == "nn":
        assert a_off % tk == 0 and b_off % tn == 0 and br_off % tk == 0
        a_spec = pl.BlockSpec((tm, tk), lambda i, j, kk: (i, kk + a_off // tk))
        b_spec = pl.BlockSpec((tk, tn), lambda i, j, kk: (kk + br_off // tk, j + b_off // tn))
    elif mode == "nt":
        assert a_off % tk == 0 and b_off % tk == 0
        a_spec = pl.BlockSpec((tm, tk), lambda i, j, kk: (i, kk + a_off // tk))
        b_spec = pl.BlockSpec((tn, tk), lambda i, j, kk: (j, kk + b_off // tk))
    else:
        assert a_off % tm == 0 and b_off % tn == 0
        a_spec = pl.BlockSpec((tk, tm), lambda i, j, kk: (kk, i + a_off // tm))
        b_spec = pl.BlockSpec((tk, tn), lambda i, j, kk: (kk, j + b_off // tn))
    nk = k // tk
    has_add = add is not None

    def body(*refs):
        if has_add:
            a_ref, b_ref, add_ref, o_ref, acc_ref = refs
        else:
            a_ref, b_ref, o_ref, acc_ref = refs
        kk = pl.program_id(2)

        @pl.when(kk == 0)
        def _():
            acc_ref[...] = jnp.zeros_like(acc_ref)

        acc_ref[...] += _dot(a_ref[...], b_ref[...], mode)

        @pl.when(kk == nk - 1)
        def _():
            r = acc_ref[...]
            if has_add:
                r = r + add_ref[...].astype(F32)
            o_ref[...] = r.astype(out_dtype)

    in_specs = [a_spec, b_spec]
    args = [a, b]
    if has_add:
        in_specs.append(pl.BlockSpec((tm, tn), lambda i, j, kk: (i, j)))
        args.append(add)
    return pl.pallas_call(
        body, name=name, grid=(m // tm, n // tn, nk),
        in_specs=in_specs, out_specs=pl.BlockSpec((tm, tn), lambda i, j, kk: (i, j)),
        out_shape=jax.ShapeDtypeStruct((m, n), out_dtype),
        scratch_shapes=[pltpu.VMEM((tm, tn), F32)],
        compiler_params=_params(("parallel", "parallel", "arbitrary")),
    )(*args)


def _rmsnorm_fwd(x, w, name, after=None):
    s, d = x.shape
    tr = _tile(s, 512, 8)

    def body(x_ref, w_ref, *rest):
        h_ref = rest[-1]
        xv = x_ref[...]
        rstd = lax.rsqrt(jnp.mean(xv * xv, axis=-1, keepdims=True) + EPS)
        h_ref[...] = (xv * rstd * w_ref[...]).astype(BF16)

    extra = [] if after is None else [after]
    return pl.pallas_call(
        body, name=name, grid=(s // tr,),
        in_specs=[pl.BlockSpec((tr, d), lambda i: (i, 0)), pl.BlockSpec((1, d), lambda i: (0, 0))] + [ANY] * len(extra),
        out_specs=pl.BlockSpec((tr, d), lambda i: (i, 0)),
        out_shape=jax.ShapeDtypeStruct((s, d), BF16),
        compiler_params=_params(("parallel",)),
    )(x, w, *extra)


def _rmsnorm_bwd(dh, x, w, res, name, after=None):
    s, d = x.shape
    tr = _tile(s, 256, 8)

    def body(dh_ref, x_ref, w_ref, res_ref, *rest):
        dx_ref, dx16_ref, dw_ref = rest[-3:]
        xv = x_ref[...]
        rstd = lax.rsqrt(jnp.mean(xv * xv, axis=-1, keepdims=True) + EPS)
        xhat = xv * rstd
        dhv = dh_ref[...]
        gw = dhv * w_ref[...]
        dx = res_ref[...] + rstd * (gw - xhat * jnp.mean(gw * xhat, axis=-1, keepdims=True))
        dx_ref[...] = dx
        dx16_ref[...] = dx.astype(BF16)

        @pl.when(pl.program_id(0) == 0)
        def _():
            dw_ref[...] = jnp.zeros_like(dw_ref)

        dw_ref[...] += jnp.sum(dhv * xhat, axis=0, keepdims=True)

    row = pl.BlockSpec((tr, d), lambda i: (i, 0))
    vec = pl.BlockSpec((1, d), lambda i: (0, 0))
    extra = [] if after is None else [after]
    return pl.pallas_call(
        body, name=name, grid=(s // tr,),
        in_specs=[row, row, vec, row] + [ANY] * len(extra), out_specs=[row, row, vec],
        out_shape=[jax.ShapeDtypeStruct((s, d), F32), jax.ShapeDtypeStruct((s, d), BF16),
                   jax.ShapeDtypeStruct((1, d), F32)],
        compiler_params=_params(("arbitrary",)),
    )(dh, x, w, res, *extra)


def _conv_taps(x, w, rows):
    shifted = [x]
    for sft in (1, 2, 3):
        shifted.append(jnp.where(rows >= sft, pltpu.roll(x, sft, 0), 0.0))
    y = w[3:4, :] * shifted[0] + w[2:3, :] * shifted[1] + w[1:2, :] * shifted[2] + w[0:1, :] * shifted[3]
    return y, shifted


def _delta_pre_fwd(qkvz, conv_w, heads):
    s = qkvz.shape[0]
    nblk = 3 * heads

    def body(x_ref, w_ref, o_ref):
        part = pl.program_id(0) // heads
        rows = lax.broadcasted_iota(jnp.int32, (s, HEAD_DIM), 0)
        y, _ = _conv_taps(x_ref[...], w_ref[...], rows)
        a = y * _sigmoid(y)
        rs = lax.rsqrt(jnp.sum(a * a, axis=-1, keepdims=True) + EPS)
        fac = jnp.where(part == 0, rs * (HEAD_DIM ** -0.5), jnp.where(part == 1, rs, 1.0))
        o_ref[...] = a * fac

    return pl.pallas_call(
        body, name="delta_pre_fwd", grid=(nblk,),
        in_specs=[pl.BlockSpec((s, HEAD_DIM), lambda i: (0, i)), pl.BlockSpec((4, HEAD_DIM), lambda i: (0, i))],
        out_specs=pl.BlockSpec((s, HEAD_DIM), lambda i: (0, i)),
        out_shape=jax.ShapeDtypeStruct((s, 3 * heads * HEAD_DIM), F32),
        compiler_params=_params(("parallel",)),
    )(qkvz, conv_w)


def _delta_pre_bwd(dqkv, qkvz, conv_w, heads):
    s = qkvz.shape[0]
    nblk = 3 * heads

    def body(d_ref, x_ref, w_ref, dx_ref, dw_ref):
        part = pl.program_id(0) // heads
        rows = lax.broadcasted_iota(jnp.int32, (s, HEAD_DIM), 0)
        w = w_ref[...]
        y, shifted = _conv_taps(x_ref[...], w, rows)
        sg = _sigmoid(y)
        a = y * sg
        rs = lax.rsqrt(jnp.sum(a * a, axis=-1, keepdims=True) + EPS)
        unit = a * rs
        dn = d_ref[...]
        scale = jnp.where(part == 0, HEAD_DIM ** -0.5, 1.0)
        da_norm = scale * rs * (dn - unit * jnp.sum(dn * unit, axis=-1, keepdims=True))
        da = jnp.where(part < 2, da_norm, dn)
        dy = da * sg * (1.0 + y * (1.0 - sg))
        dx = w[3:4, :] * dy
        for sft in (1, 2, 3):
            dx = dx + w[3 - sft:4 - sft, :] * jnp.where(rows < s - sft, pltpu.roll(dy, s - sft, 0), 0.0)
        dx_ref[...] = dx.astype(BF16)
        for sft in range(4):
            dw_ref[3 - sft:4 - sft, :] = jnp.sum(dy * shifted[sft], axis=0, keepdims=True)

    col = pl.BlockSpec((s, HEAD_DIM), lambda i: (0, i))
    wsp = pl.BlockSpec((4, HEAD_DIM), lambda i: (0, i))
    return pl.pallas_call(
        body, name="delta_pre_bwd", grid=(nblk,),
        in_specs=[col, col, wsp], out_specs=[col, wsp],
        out_shape=[jax.ShapeDtypeStruct((s, 3 * heads * HEAD_DIM), BF16),
                   jax.ShapeDtypeStruct((4, 3 * heads * HEAD_DIM), F32)],
        compiler_params=_params(("parallel",)),
    )(dqkv, qkvz, conv_w)


def _heads_of(ref, heads):
    return jnp.stack([ref[:, h * HEAD_DIM:(h + 1) * HEAD_DIM] for h in range(heads)])


def _chunk_common(q, k, v, bd, a_log, dt_bias, heads):
    c = CHUNK
    braw = jnp.stack([bd[:, h:h + 1] for h in range(heads)])
    draw = jnp.stack([bd[:, heads + h:heads + h + 1] for h in range(heads)])
    beta = _sigmoid(braw)
    xd = draw + dt_bias
    sp = jnp.maximum(xd, 0.0) + jnp.log1p(jnp.exp(-jnp.abs(xd)))
    g = -jnp.exp(a_log) * sp
    row = lax.broadcasted_iota(jnp.int32, (c, c), 0)
    col = lax.broadcasted_iota(jnp.int32, (c, c), 1)
    sq = (heads, c, c)
    g_b = jnp.broadcast_to(g, sq)
    g_row = _csum(jnp.where(row == col, g_b, 0.0))
    gam_col = _rsum(jnp.where(col <= row, jnp.broadcast_to(g_row, sq), 0.0))
    gam_row = _csum(jnp.where(row <= col, g_b, 0.0))
    causal = row >= col
    dm = jnp.where(causal, jnp.exp(jnp.where(causal, gam_col - gam_row, 0.0)), 0.0)
    kk = _dot(k, k, "nt")
    low = jnp.where(row > col, beta * kk * dm, 0.0)
    t = jnp.where(row == col, 1.0, 0.0) - low
    pw = low
    for _ in range(5):
        pw = _dot(pw, pw)
        t = t + _dot(t, pw)
    e = jnp.exp(gam_col)
    u = _dot(t, beta * v)
    w = _dot(t, (beta * e) * k)
    qk_raw = _dot(q, k, "nt")
    gl = _csum(g)
    el = jnp.exp(gl - gam_col)
    return dict(beta=beta, xd=xd, g=g, row=row, col=col, dm=dm, kk=kk, t=t, e=e, u=u, w=w,
                qk_raw=qk_raw, qk=qk_raw * dm, gl=gl, el=el, qd=e * q, kd=el * k, cd=jnp.exp(gl))


def _delta_chunk_fwd(qkv, bd, a_log, dt_bias, heads):
    s = qkv.shape[0]
    n = s // CHUNK
    dw = heads * HEAD_DIM
    blk = lambda part: pl.BlockSpec((CHUNK, dw), lambda i: (i, part))

    def body(q_ref, k_ref, v_ref, bd_ref, al_ref, dt_ref, o_ref, st_ref, state):
        @pl.when(pl.program_id(0) == 0)
        def _():
            state[...] = jnp.zeros_like(state)

        cm = _chunk_common(_heads_of(q_ref, heads), _heads_of(k_ref, heads), _heads_of(v_ref, heads), bd_ref[...],
                           al_ref[...], dt_ref[...], heads)
        st = state[...]
        st_ref[0] = st
        vn = cm["u"] - _dot(cm["w"], st)
        o = _dot(cm["qd"], st) + _dot(cm["qk"], vn)
        for h in range(heads):
            o_ref[:, h * HEAD_DIM:(h + 1) * HEAD_DIM] = o[h]
        state[...] = cm["cd"] * st + _dot(cm["kd"], vn, "tn")

    smem = pl.BlockSpec((heads, 1, 1), lambda i: (0, 0, 0))
    return pl.pallas_call(
        body, name="delta_chunk_fwd", grid=(n,),
        in_specs=[blk(0), blk(1), blk(2), pl.BlockSpec((CHUNK, HEAD_DIM), lambda i: (i, 0)), smem, smem],
        out_specs=[pl.BlockSpec((CHUNK, dw), lambda i: (i, 0)),
                   pl.BlockSpec((1, heads, HEAD_DIM, HEAD_DIM), lambda i: (i, 0, 0, 0))],
        out_shape=[jax.ShapeDtypeStruct((s, dw), F32),
                   jax.ShapeDtypeStruct((n, heads, HEAD_DIM, HEAD_DIM), F32)],
        scratch_shapes=[pltpu.VMEM((heads, HEAD_DIM, HEAD_DIM), F32)],
        compiler_params=_params(("arbitrary",)),
    )(qkv, qkv, qkv, bd, a_log, dt_bias)


def _delta_chunk_bwd(do, qkv, bd, states, a_log, dt_bias, heads):
    s = qkv.shape[0]
    n = s // CHUNK
    dw = heads * HEAD_DIM
    c = CHUNK
    blk = lambda part: pl.BlockSpec((CHUNK, dw), lambda i: (n - 1 - i, part))

    def all_heads(q, k, v, dov, st, dsn, bd, a_log, dt_bias):
        cm = _chunk_common(q, k, v, bd, a_log, dt_bias, heads)
        beta, e, dm, row, col = cm["beta"], cm["e"], cm["dm"], cm["row"], cm["col"]
        sq = (heads, c, c)
        vn = cm["u"] - _dot(cm["w"], st)
        dvn = _dot(cm["kd"], dsn)
        dkd = _dot(vn, dsn, "nt")
        dcd = _csum(_rsum(st * dsn))
        ds = cm["cd"] * dsn
        dqd = _dot(dov, st, "nt")
        ds = ds + _dot(cm["qd"], dov, "tn")
        dqk = _dot(dov, vn, "nt")
        dvn = dvn + _dot(cm["qk"], dov, "tn")
        dw_ = -_dot(dvn, st, "nt")
        ds = ds - _dot(cm["w"], dvn, "tn")
        drhs_u = _dot(cm["t"], dvn, "tn")
        drhs_w = _dot(cm["t"], dw_, "tn")
        da = -(_dot(drhs_u, cm["u"], "nt") + _dot(drhs_w, cm["w"], "nt"))
        dl = jnp.where(row > col, da, 0.0)
        dbeta = _rsum(dl * cm["kk"] * dm)
        dkk = dl * beta * dm
        dd = dl * beta * cm["kk"]
        dv = beta * drhs_u
        ek = e * k
        dbeta = dbeta + _rsum(drhs_u * v) + _rsum(drhs_w * ek)
        dk = (beta * e) * drhs_w
        dgam = _rsum(drhs_w * (beta * ek))
        dqkm = dqk * dm
        dq = _dot(dqkm, k)
        dk = dk + _dot(dqkm, q, "tn")
        dd = dd + dqk * cm["qk_raw"]
        dk = dk + _dot(dkk, k) + _dot(dkk, k, "tn")
        dq = dq + e * dqd
        dgam = dgam + _rsum(dqd * cm["qd"])
        dk = dk + cm["el"] * dkd
        r = _rsum(dkd * cm["kd"])
        dgam = dgam - r
        dgl = _csum(r) + dcd * cm["cd"]
        mm = dd * dm
        colsum_c = _rsum(jnp.where(row == col, jnp.broadcast_to(_csum(mm), sq), 0.0))
        dgam = dgam + _rsum(mm) - colsum_c
        ridx = lax.broadcasted_iota(jnp.int32, (c, 1), 0)
        dgam = dgam + jnp.where(ridx == c - 1, dgl, 0.0)
        dgam_row = _csum(jnp.where(row == col, jnp.broadcast_to(dgam, sq), 0.0))
        dg = _rsum(jnp.where(col >= row, jnp.broadcast_to(dgam_row, sq), 0.0))
        d_xd = dg * (-jnp.exp(a_log)) * _sigmoid(cm["xd"])
        d_braw = dbeta * beta * (1.0 - beta)
        d_alog = dg * cm["g"]
        lane = lax.broadcasted_iota(jnp.int32, (c, HEAD_DIM), 1)
        dbd = jnp.zeros((c, HEAD_DIM), F32)
        for h in range(heads):
            dbd = (dbd + jnp.where(lane == h, d_braw[h], 0.0) + jnp.where(lane == h + heads, d_xd[h], 0.0)
                   + jnp.where(lane == h + 2 * heads, d_alog[h], 0.0))
        return dq, dk, dv, ds, dbd

    def body(do_ref, q_ref, k_ref, v_ref, bd_ref, st_ref, al_ref, dt_ref, dqkv_ref, dbd_ref, dstate):
        @pl.when(pl.program_id(0) == 0)
        def _():
            dstate[...] = jnp.zeros_like(dstate)

        dq, dk, dv, ds, dbd = all_heads(_heads_of(q_ref, heads), _heads_of(k_ref, heads), _heads_of(v_ref, heads),
                                        _heads_of(do_ref, heads), st_ref[0], dstate[...], bd_ref[...],
                                        al_ref[...], dt_ref[...])
        for part, val in enumerate((dq, dk, dv)):
            for h in range(heads):
                lo = part * dw + h * HEAD_DIM
                dqkv_ref[:, lo:lo + HEAD_DIM] = val[h]
        dstate[...] = ds
        dbd_ref[...] = dbd

    smem = pl.BlockSpec((heads, 1, 1), lambda i: (0, 0, 0))
    shared = pl.BlockSpec((CHUNK, HEAD_DIM), lambda i: (n - 1 - i, 0))
    wide = pl.BlockSpec((CHUNK, dw), lambda i: (n - 1 - i, 0))
    return pl.pallas_call(
        body, name="delta_chunk_bwd", grid=(n,),
        in_specs=[wide, blk(0), blk(1), blk(2), shared,
                  pl.BlockSpec((1, heads, HEAD_DIM, HEAD_DIM), lambda i: (n - 1 - i, 0, 0, 0)), smem, smem],
        out_specs=[pl.BlockSpec((CHUNK, 3 * dw), lambda i: (n - 1 - i, 0)), shared],
        out_shape=[jax.ShapeDtypeStruct((s, 3 * dw), F32), jax.ShapeDtypeStruct((s, HEAD_DIM), F32)],
        scratch_shapes=[pltpu.VMEM((heads, HEAD_DIM, HEAD_DIM), F32)],
        compiler_params=_params(("arbitrary",)),
    )(do, qkv, qkv, qkv, bd, states, a_log, dt_bias)


def _delta_post_fwd(o, qkvz, w, heads):
    s = o.shape[0]
    tr = _tile(s, 1024, 8)

    def body(o_ref, z_ref, w_ref, out_ref):
        ov, z = o_ref[...], z_ref[...]
        rstd = lax.rsqrt(jnp.mean(ov * ov, axis=-1, keepdims=True) + EPS)
        out_ref[...] = (ov * rstd * w_ref[...] * (z * _sigmoid(z))).astype(BF16)

    return pl.pallas_call(
        body, name="delta_post_fwd", grid=(s // tr, heads),
        in_specs=[pl.BlockSpec((tr, HEAD_DIM), lambda i, h: (i, h)),
                  pl.BlockSpec((tr, HEAD_DIM), lambda i, h: (i, 3 * heads + h)),
                  pl.BlockSpec((1, HEAD_DIM), lambda i, h: (0, 0))],
        out_specs=pl.BlockSpec((tr, HEAD_DIM), lambda i, h: (i, h)),
        out_shape=jax.ShapeDtypeStruct((s, heads * HEAD_DIM), BF16),
        compiler_params=_params(("parallel", "parallel")),
    )(o, qkvz, w)


def _delta_post_bwd(dmix, o, qkvz, w, heads):
    s = o.shape[0]
    tr = _tile(s, 1024, 8)

    def body(d_ref, o_ref, z_ref, w_ref, do_ref, dz_ref, dw_ref):
        d, ov, z, wv = d_ref[...], o_ref[...], z_ref[...], w_ref[...]
        sg = _sigmoid(z)
        rstd = lax.rsqrt(jnp.mean(ov * ov, axis=-1, keepdims=True) + EPS)
        ohat = ov * rstd
        dz_ref[...] = (d * (ohat * wv) * sg * (1.0 + z * (1.0 - sg))).astype(BF16)
        dn = d * (z * sg)
        gw = dn * wv
        do_ref[...] = rstd * (gw - ohat * jnp.mean(gw * ohat, axis=-1, keepdims=True))

        @pl.when((pl.program_id(0) == 0) & (pl.program_id(1) == 0))
        def _():
            dw_ref[...] = jnp.zeros_like(dw_ref)

        dw_ref[...] += jnp.sum(dn * ohat, axis=0, keepdims=True)

    head = pl.BlockSpec((tr, HEAD_DIM), lambda i, h: (i, h))
    vec = pl.BlockSpec((1, HEAD_DIM), lambda i, h: (0, 0))
    dw = heads * HEAD_DIM
    return pl.pallas_call(
        body, name="delta_post_bwd", grid=(s // tr, heads),
        in_specs=[head, head, pl.BlockSpec((tr, HEAD_DIM), lambda i, h: (i, 3 * heads + h)), vec],
        out_specs=[head, head, vec],
        out_shape=[jax.ShapeDtypeStruct((s, dw), F32), jax.ShapeDtypeStruct((s, dw), BF16),
                   jax.ShapeDtypeStruct((1, HEAD_DIM), F32)],
        compiler_params=_params(("arbitrary", "arbitrary")),
    )(dmix, o, qkvz, w)


def _rope_tables(positions, s):
    half = HEAD_DIM // 2
    inv_freq = ROPE_THETA ** (-jnp.arange(half, dtype=F32) / half)
    ang = positions.reshape(s, 1).astype(F32) * inv_freq
    cos, sin = jnp.cos(ang), jnp.sin(ang)
    return jnp.concatenate([cos, cos], axis=-1), jnp.concatenate([-sin, sin], axis=-1)


def _attn_pre_fwd(aqkv, wq, wk, cosf, sinf, heads):
    s = aqkv.shape[0]
    tr = _tile(s, 1024, 8)

    def body(x_ref, wq_ref, wk_ref, c_ref, s_ref, o_ref):
        xv = x_ref[...]
        wv = jnp.where(pl.program_id(1) < heads, wq_ref[...], wk_ref[...])
        y = xv * lax.rsqrt(jnp.mean(xv * xv, axis=-1, keepdims=True) + EPS) * wv
        o_ref[...] = y * c_ref[...] + pltpu.roll(y, HEAD_DIM // 2, 1) * s_ref[...]

    blk = pl.BlockSpec((tr, HEAD_DIM), lambda i, j: (i, j))
    vec = pl.BlockSpec((1, HEAD_DIM), lambda i, j: (0, 0))
    tab = pl.BlockSpec((tr, HEAD_DIM), lambda i, j: (i, 0))
    return pl.pallas_call(
        body, name="attn_pre_fwd", grid=(s // tr, 2 * heads),
        in_specs=[blk, vec, vec, tab, tab], out_specs=blk,
        out_shape=jax.ShapeDtypeStruct((s, 2 * heads * HEAD_DIM), F32),
        compiler_params=_params(("parallel", "parallel")),
    )(aqkv, wq, wk, cosf, sinf)


ATT_ROWS = 2048


def _band():
    qi = lax.broadcasted_iota(jnp.int32, (SPAN, 2 * SPAN), 0)
    ki = lax.broadcasted_iota(jnp.int32, (SPAN, 2 * SPAN), 1)
    dist = qi + SPAN - ki
    return (dist >= 0) & (dist <= SPAN), ki >= SPAN


def _sub(g, r, d):
    if d == 1:
        return pl.ds(g * SPAN, SPAN)
    return pl.ds(g * SPAN * d + r, SPAN, stride=d)


def _attn_geometry(s, d):
    rb = min(ATT_ROWS, s)
    pr = SPAN * d
    return rb, pr, rb // pr, s // rb


def _attn_fwd(qk, aqkv, heads, d):
    s = qk.shape[0]
    aw = heads * HEAD_DIM
    rb, pr, g_cnt, nstep = _attn_geometry(s, d)

    def body(q_ref, k_ref, v_ref, kp_ref, vp_ref, o_ref, l_ref):
        first = pl.program_id(1) == 0
        band, own = _band()
        edge = band & (own | jnp.logical_not(first))
        for r in range(d):
            for g in range(g_cnt):
                rows = _sub(g, r, d)
                if g == 0:
                    kp, vp, mask = kp_ref[_sub(0, r, d), :], vp_ref[_sub(0, r, d), :], edge
                else:
                    kp, vp, mask = k_ref[_sub(g - 1, r, d), :], v_ref[_sub(g - 1, r, d), :], band
                kcat = jnp.concatenate([kp, k_ref[rows, :]], axis=0)
                vcat = jnp.concatenate([vp, v_ref[rows, :]], axis=0)
                sc = _dot(q_ref[rows, :], kcat, "nt") * (HEAD_DIM ** -0.5)
                sc = jnp.where(mask, sc, NEG)
                m = jnp.max(sc, axis=-1, keepdims=True)
                p = jnp.exp(sc - m)
                den = jnp.sum(p, axis=-1, keepdims=True)
                o_ref[rows, :] = _dot(p, vcat) / den
                l_ref[rows, :] = jnp.broadcast_to(m + jnp.log(den), (SPAN, HEAD_DIM))

    cur = lambda off: pl.BlockSpec((rb, HEAD_DIM), lambda h, j: (j, off + h))
    prev = lambda off: pl.BlockSpec((pr, HEAD_DIM), lambda h, j: (jnp.maximum(j * (rb // pr) - 1, 0), off + h))
    out = pl.BlockSpec((rb, HEAD_DIM), lambda h, j: (j, h))
    return pl.pallas_call(
        body, name=f"attn_fwd_d{d}", grid=(heads, nstep),
        in_specs=[cur(0), cur(heads), cur(2 * heads), prev(heads), prev(2 * heads)],
        out_specs=[out, out],
        out_shape=[jax.ShapeDtypeStruct((s, aw), F32)] * 2,
        compiler_params=_params(("parallel", "parallel")),
    )(qk, qk, aqkv, qk, aqkv)


def _attn_merge_fwd(os_, ls_, w, heads):
    s = os_[0].shape[0]
    tr = _tile(s, 512, 8)

    def body(o1, o2, o3, l1, l2, l3, w_ref, mix_ref, ob_ref, lse_ref):
        la, lb, lc = l1[...], l2[...], l3[...]
        m = jnp.maximum(jnp.maximum(la, lb), lc)
        ea, eb, ec = jnp.exp(la - m), jnp.exp(lb - m), jnp.exp(lc - m)
        tot = ea + eb + ec
        ob = (ea * o1[...] + eb * o2[...] + ec * o3[...]) / tot
        ob_ref[...] = ob
        lse_ref[...] = m + jnp.log(tot)
        rstd = lax.rsqrt(jnp.mean(ob * ob, axis=-1, keepdims=True) + EPS)
        mix_ref[...] = (ob * rstd * w_ref[...]).astype(BF16)

    blk = pl.BlockSpec((tr, HEAD_DIM), lambda i, h: (i, h))
    vec = pl.BlockSpec((1, HEAD_DIM), lambda i, h: (0, 0))
    aw = heads * HEAD_DIM
    return pl.pallas_call(
        body, name="attn_merge_fwd", grid=(s // tr, heads),
        in_specs=[blk] * 6 + [vec], out_specs=[blk, blk, blk],
        out_shape=[jax.ShapeDtypeStruct((s, aw), BF16), jax.ShapeDtypeStruct((s, aw), F32),
                   jax.ShapeDtypeStruct((s, aw), F32)],
        compiler_params=_params(("parallel", "parallel")),
    )(*os_, *ls_, w)


def _attn_merge_bwd(dmix, ob, w, heads):
    s = ob.shape[0]
    tr = _tile(s, 1024, 8)

    def body(d_ref, ob_ref, w_ref, do_ref, dsum_ref, dw_ref):
        d, ov = d_ref[...], ob_ref[...]
        rstd = lax.rsqrt(jnp.mean(ov * ov, axis=-1, keepdims=True) + EPS)
        ohat = ov * rstd
        gw = d * w_ref[...]
        dov = rstd * (gw - ohat * jnp.mean(gw * ohat, axis=-1, keepdims=True))
        do_ref[...] = dov
        dsum_ref[...] = jnp.broadcast_to(jnp.sum(dov * ov, axis=-1, keepdims=True), dov.shape)

        @pl.when((pl.program_id(0) == 0) & (pl.program_id(1) == 0))
        def _():
            dw_ref[...] = jnp.zeros_like(dw_ref)

        dw_ref[...] += jnp.sum(d * ohat, axis=0, keepdims=True)

    blk = pl.BlockSpec((tr, HEAD_DIM), lambda i, h: (i, h))
    vec = pl.BlockSpec((1, HEAD_DIM), lambda i, h: (0, 0))
    aw = heads * HEAD_DIM
    return pl.pallas_call(
        body, name="attn_merge_bwd", grid=(s // tr, heads),
        in_specs=[pl.BlockSpec((tr, HEAD_DIM), lambda i, h: (i, heads + h)), blk, vec],
        out_specs=[blk, blk, vec],
        out_shape=[jax.ShapeDtypeStruct((s, aw), F32), jax.ShapeDtypeStruct((s, aw), F32),
                   jax.ShapeDtypeStruct((1, HEAD_DIM), F32)],
        compiler_params=_params(("arbitrary", "arbitrary")),
    )(dmix, ob, w)


def _attn_bwd(qk, aqkv, do, lse, dsum, heads, d):
    s = qk.shape[0]
    aw = heads * HEAD_DIM
    rb, pr, g_cnt, nstep = _attn_geometry(s, d)
    scale = HEAD_DIM ** -0.5

    def body(q_ref, k_ref, v_ref, do_ref, l_ref, ds_ref, kp_ref, vp_ref, qn_ref, don_ref, ln_ref, dsn_ref, out_ref):
        first = pl.program_id(1) == 0
        last = pl.program_id(1) == nstep - 1
        band, own = _band()
        edge = band & (own | jnp.logical_not(first))
        dq_ref, dk_ref, dv_ref = out_ref.at[0], out_ref.at[1], out_ref.at[2]
        dk_ref[...] = jnp.zeros((rb, HEAD_DIM), F32)
        dv_ref[...] = jnp.zeros((rb, HEAD_DIM), F32)
        for r in range(d):
            for g in range(g_cnt):
                rows = _sub(g, r, d)
                if g == 0:
                    kp, vp, mask = kp_ref[_sub(0, r, d), :], vp_ref[_sub(0, r, d), :], edge
                else:
                    prows = _sub(g - 1, r, d)
                    kp, vp, mask = k_ref[prows, :], v_ref[prows, :], band
                q, dov = q_ref[rows, :], do_ref[rows, :]
                kcat = jnp.concatenate([kp, k_ref[rows, :]], axis=0)
                vcat = jnp.concatenate([vp, v_ref[rows, :]], axis=0)
                sc = _dot(q, kcat, "nt") * scale
                p = jnp.where(mask, jnp.exp(sc - l_ref[rows, 0:1]), 0.0)
                dsc = p * (_dot(dov, vcat, "nt") - ds_ref[rows, 0:1]) * scale
                dq_ref[rows, :] = _dot(dsc, kcat)
                dk = _dot(dsc, q, "tn")
                dv = _dot(p, dov, "tn")
                dk_ref[rows, :] += dk[SPAN:]
                dv_ref[rows, :] += dv[SPAN:]
                if g > 0:
                    dk_ref[prows, :] += dk[:SPAN]
                    dv_ref[prows, :] += dv[:SPAN]
            rows, nrows = _sub(g_cnt - 1, r, d), _sub(0, r, d)
            q, dov = qn_ref[nrows, :], don_ref[nrows, :]
            sc = _dot(q, k_ref[rows, :], "nt") * scale
            p = jnp.where(band[:, :SPAN] & jnp.logical_not(last), jnp.exp(sc - ln_ref[nrows, 0:1]), 0.0)
            dsc = p * (_dot(dov, v_ref[rows, :], "nt") - dsn_ref[nrows, 0:1]) * scale
            dk_ref[rows, :] += _dot(dsc, q, "tn")
            dv_ref[rows, :] += _dot(p, dov, "tn")

    per = rb // pr
    cur = lambda off: pl.BlockSpec((rb, HEAD_DIM), lambda h, j: (j, off + h))
    prev = lambda off: pl.BlockSpec((pr, HEAD_DIM), lambda h, j: (jnp.maximum(j * per - 1, 0), off + h))
    nxt = lambda off: pl.BlockSpec((pr, HEAD_DIM), lambda h, j: (jnp.minimum((j + 1) * per, s // pr - 1), off + h))
    return pl.pallas_call(
        body, name=f"attn_bwd_d{d}", grid=(heads, nstep),
        in_specs=[cur(0), cur(heads), cur(2 * heads), cur(0), cur(0), cur(0), prev(heads), prev(2 * heads),
                  nxt(0), nxt(0), nxt(0), nxt(0)],
        out_specs=pl.BlockSpec((3, rb, HEAD_DIM), lambda h, j: (0, j, h)),
        out_shape=jax.ShapeDtypeStruct((3, s, aw), F32),
        compiler_params=_params(("parallel", "parallel")),
    )(qk, qk, aqkv, do, lse, dsum, qk, aqkv, qk, do, lse, dsum)


def _attn_pre_bwd(grads, aqkv, wq, wk, cosf, sinf, heads):
    s = aqkv.shape[0]
    tr = _tile(s, 512, 8)
    nrow = s // tr

    def body(g1_ref, g2_ref, g3_ref, x_ref, wq_ref, wk_ref, c_ref, s_ref, dx_ref, dwq_ref, dwk_ref):
        i, j = pl.program_id(0), pl.program_id(1)
        kind = j // heads
        dout = g1_ref[0] + g2_ref[0] + g3_ref[0]
        tot_v = dout
        dy = dout * c_ref[...] + pltpu.roll(dout * s_ref[...], HEAD_DIM // 2, 1)
        xv = x_ref[...]
        wv = jnp.where(kind == 0, wq_ref[...], wk_ref[...])
        rstd = lax.rsqrt(jnp.mean(xv * xv, axis=-1, keepdims=True) + EPS)
        xhat = xv * rstd
        gw = dy * wv
        dxn = rstd * (gw - xhat * jnp.mean(gw * xhat, axis=-1, keepdims=True))
        dx_ref[...] = jnp.where(kind == 2, tot_v, dxn).astype(BF16)
        dwc = jnp.sum(dy * xhat, axis=0, keepdims=True)

        @pl.when((i == 0) & (j == 0))
        def _():
            dwq_ref[...] = jnp.zeros_like(dwq_ref)
            dwk_ref[...] = jnp.zeros_like(dwk_ref)

        @pl.when(kind == 0)
        def _():
            dwq_ref[...] += dwc

        @pl.when(kind == 1)
        def _():
            dwk_ref[...] += dwc

    grad = pl.BlockSpec((1, tr, HEAD_DIM), lambda i, j: (j // heads, i, j % heads))
    blk = pl.BlockSpec((tr, HEAD_DIM), lambda i, j: (i, j))
    vec = pl.BlockSpec((1, HEAD_DIM), lambda i, j: (0, 0))
    tab = pl.BlockSpec((tr, HEAD_DIM), lambda i, j: (i, 0))
    return pl.pallas_call(
        body, name="attn_pre_bwd", grid=(nrow, 3 * heads),
        in_specs=[grad, grad, grad, blk, vec, vec, tab, tab], out_specs=[blk, vec, vec],
        out_shape=[jax.ShapeDtypeStruct((s, 3 * heads * HEAD_DIM), BF16),
                   jax.ShapeDtypeStruct((1, HEAD_DIM), F32), jax.ShapeDtypeStruct((1, HEAD_DIM), F32)],
        compiler_params=_params(("arbitrary", "arbitrary")),
    )(*grads, aqkv, wq, wk, cosf, sinf)


def _swiglu_fwd(gu, ff):
    s = gu.shape[0]
    tr, tc = _tile(s, 512, 8), _tile(ff, 1024, 128)
    nc = ff // tc

    def body(g_ref, u_ref, o_ref):
        g = g_ref[...].astype(F32)
        o_ref[...] = (g * _sigmoid(g) * u_ref[...].astype(F32)).astype(BF16)

    return pl.pallas_call(
        body, name="swiglu_fwd", grid=(s // tr, nc),
        in_specs=[pl.BlockSpec((tr, tc), lambda i, j: (i, j)), pl.BlockSpec((tr, tc), lambda i, j: (i, j + nc))],
        out_specs=pl.BlockSpec((tr, tc), lambda i, j: (i, j)),
        out_shape=jax.ShapeDtypeStruct((s, ff), BF16),
        compiler_params=_params(("parallel", "parallel")),
    )(gu, gu)


def _swiglu_bwd(dact, gu, ff):
    s = gu.shape[0]
    tr, tc = _tile(s, 512, 8), _tile(ff, 1024, 128)
    nc = ff // tc

    def body(d_ref, g_ref, u_ref, dg_ref, du_ref):
        d, g = d_ref[...].astype(F32), g_ref[...].astype(F32)
        sg = _sigmoid(g)
        dg_ref[...] = (d * u_ref[...].astype(F32) * sg * (1.0 + g * (1.0 - sg))).astype(BF16)
        du_ref[...] = (d * g * sg).astype(BF16)

    lo = pl.BlockSpec((tr, tc), lambda i, j: (i, j))
    hi = pl.BlockSpec((tr, tc), lambda i, j: (i, j + nc))
    dg, du = pl.pallas_call(
        body, name="swiglu_bwd", grid=(s // tr, nc),
        in_specs=[lo, lo, hi], out_specs=[lo, lo],
        out_shape=[jax.ShapeDtypeStruct((s, ff), BF16)] * 2,
        compiler_params=_params(("parallel", "parallel")),
    )(dact, gu, gu)
    return dg, du


def _loss_head(y, target):
    s, d = y.shape
    tr = _tile(s, 512, 8)

    def body(y_ref, t_ref, dy_ref, dy16_ref, l_ref):
        err = y_ref[...] - t_ref[...]
        dy_ref[...] = err * (1.0 / d)
        dy16_ref[...] = (err * (1.0 / d)).astype(BF16)
        part = 0.5 * jnp.sum(jnp.sum(err * err, axis=-1, keepdims=True) * (1.0 / d), axis=0, keepdims=True)

        @pl.when(pl.program_id(0) == 0)
        def _():
            l_ref[...] = jnp.zeros_like(l_ref)

        lane = lax.broadcasted_iota(jnp.int32, (1, HEAD_DIM), 1)
        l_ref[...] += jnp.where(lane == 0, part, 0.0)

    row = pl.BlockSpec((tr, d), lambda i: (i, 0))
    return pl.pallas_call(
        body, name="loss_head", grid=(s // tr,),
        in_specs=[row, row], out_specs=[row, row, pl.BlockSpec((1, HEAD_DIM), lambda i: (0, 0))],
        out_shape=[jax.ShapeDtypeStruct((s, d), F32), jax.ShapeDtypeStruct((s, d), BF16),
                   jax.ShapeDtypeStruct((1, HEAD_DIM), F32)],
        compiler_params=_params(("arbitrary",)),
    )(y, target)


def _colsum(a, name):
    s, d = a.shape
    tr = _tile(s, 1024, 8)

    def body(a_ref, o_ref):
        @pl.when(pl.program_id(0) == 0)
        def _():
            o_ref[...] = jnp.zeros_like(o_ref)

        o_ref[...] += jnp.sum(a_ref[...], axis=0, keepdims=True)

    return pl.pallas_call(
        body, name=name, grid=(s // tr,),
        in_specs=[pl.BlockSpec((tr, d), lambda i: (i, 0))], out_specs=pl.BlockSpec((1, d), lambda i: (0, 0)),
        out_shape=jax.ShapeDtypeStruct((1, d), F32),
        compiler_params=_params(("arbitrary",)),
    )(a)


def _local_step(x, positions, target, small, w_qkvz, w_bd, w_attn, conv_w, later_weights, ffn_grads_ready,
                after=None):
    s, dmod = x.shape
    heads = dmod // (2 * HEAD_DIM)
    dw = heads * HEAD_DIM
    a_log, dt_bias = small["a_log"].reshape(heads, 1, 1), small["dt_bias"].reshape(heads, 1, 1)
    cosf, sinf = _rope_tables(positions, s)

    h1 = _rmsnorm_fwd(x, small["attn_norm_w"], "norm1_fwd", after=after)
    qkvz = _matmul(h1, w_qkvz, "nn", "proj_qkvz")
    bd = _matmul(h1, w_bd, "nn", "proj_bd")
    aqkv = _matmul(h1, w_attn, "nn", "proj_attn")
    dqkv = _delta_pre_fwd(qkvz, conv_w, heads)
    o_d, states = _delta_chunk_fwd(dqkv, bd, a_log, dt_bias, heads)
    mix_a = _delta_post_fwd(o_d, qkvz, small["delta_out_norm_w"], heads)
    qk_rot = _attn_pre_fwd(aqkv, small["q_norm_w"], small["k_norm_w"], cosf, sinf, heads)
    outs = [_attn_fwd(qk_rot, aqkv, heads, d) for d in DILATIONS]
    mix_b, ob, lse = _attn_merge_fwd([o for o, _ in outs], [l for _, l in outs], small["attn_out_norm_w"], heads)
    w_out, w_gu, w_down = later_weights((mix_a, mix_b))
    ff = w_down.shape[0]
    x1 = _matmul(mix_a, w_out, "nn", "out_proj_a", add=x, b_rows=(0, dw))
    x1 = _matmul(mix_b, w_out, "nn", "out_proj_b", add=x1, b_rows=(dw, dw))
    h2 = _rmsnorm_fwd(x1, small["ffn_norm_w"], "norm2_fwd")
    gu = _matmul(h2, w_gu, "nn", "ffn_gate_up", out_dtype=BF16)
    act = _swiglu_fwd(gu, ff)
    y = _matmul(act, w_down, "nn", "ffn_down", add=x1)
    dy, dy16, loss_row = _loss_head(y, target)

    dact = _matmul(dy16, w_down, "nt", "ffn_down_dx", out_dtype=BF16)
    g_w_down = _matmul(act, dy16, "tn", "ffn_down_dw")
    dgate, dup = _swiglu_bwd(dact, gu, ff)
    dh2 = _matmul(dgate, w_gu, "nt", "ffn_gate_dx", b_cols=(0, ff))
    dh2 = _matmul(dup, w_gu, "nt", "ffn_up_dx", b_cols=(ff, ff), add=dh2)
    g_w_gate = _matmul(h2, dgate, "tn", "ffn_gate_dw")
    g_w_up = _matmul(h2, dup, "tn", "ffn_up_dw")
    behind = ffn_grads_ready(g_w_gate, g_w_up, g_w_down)
    dx1, dx1_16, g_ffn_norm = _rmsnorm_bwd(dh2, x1, small["ffn_norm_w"], dy, "norm2_bwd", after=behind)
    dmix = _matmul(dx1_16, w_out, "nt", "out_proj_dx")
    g_w_out_a = _matmul(mix_a, dx1_16, "tn", "out_proj_dw_a")
    g_w_out_b = _matmul(mix_b, dx1_16, "tn", "out_proj_dw_b")
    dob, dsum, g_attn_out_norm = _attn_merge_bwd(dmix, ob, small["attn_out_norm_w"], heads)
    grads = [_attn_bwd(qk_rot, aqkv, dob, lse, dsum, heads, d) for d in DILATIONS]
    d_aqkv, g_q_norm, g_k_norm = _attn_pre_bwd(grads, aqkv, small["q_norm_w"], small["k_norm_w"], cosf, sinf, heads)
    do_d, dz, g_delta_out_norm = _delta_post_bwd(dmix, o_d, qkvz, small["delta_out_norm_w"], heads)
    ddqkv, dbd = _delta_chunk_bwd(do_d, dqkv, bd, states, a_log, dt_bias, heads)
    d_qkv_raw, g_conv = _delta_pre_bwd(ddqkv, qkvz, conv_w, heads)
    bd_sums = _colsum(dbd, "bd_colsum")
    dh1 = _matmul(d_qkv_raw, w_qkvz, "nt", "proj_qkv_dx", b_cols=(0, 3 * dw))
    dh1 = _matmul(dz, w_qkvz, "nt", "proj_z_dx", b_cols=(3 * dw, dw), add=dh1)
    dh1 = _matmul(d_aqkv, w_attn, "nt", "proj_attn_dx", add=dh1)
    dh1 = _matmul(dbd, w_bd, "nt", "proj_bd_dx", add=dh1)
    g_w_qkv = _matmul(h1, d_qkv_raw, "tn", "proj_qkv_dw")
    g_w_z = _matmul(h1, dz, "tn", "proj_z_dw")
    g_w_bd = _matmul(h1, dbd, "tn", "proj_bd_dw")
    g_w_attn = _matmul(h1, d_aqkv, "tn", "proj_attn_dw")
    grad_x, _, g_attn_norm = _rmsnorm_bwd(dh1, x, small["attn_norm_w"], dx1, "norm1_bwd")
    small_grads = dict(
        attn_norm_w=g_attn_norm, a_log=bd_sums[:, 2 * heads:3 * heads], dt_bias=bd_sums[:, heads:2 * heads],
        delta_out_norm_w=g_delta_out_norm, q_norm_w=g_q_norm, k_norm_w=g_k_norm,
        attn_out_norm_w=g_attn_out_norm, ffn_norm_w=g_ffn_norm, conv_w=g_conv)
    big_grads = dict(w_qkv=g_w_qkv, w_z=g_w_z, w_bd=g_w_bd, w_attn=g_w_attn, w_out_a=g_w_out_a, w_out_b=g_w_out_b)
    return loss_row, grad_x, small_grads, big_grads


def _adamw(w, g, m, v, name):
    r, c = w.shape
    tr = _tile(r, 256, 8)

    def body(w_ref, g_ref, m_ref, v_ref, d_ref, nm_ref, nv_ref):
        gv = g_ref[...]
        nm = ADAM_B1 * m_ref[...] + (1.0 - ADAM_B1) * gv
        nv = ADAM_B2 * v_ref[...] + (1.0 - ADAM_B2) * (gv * gv)
        m_hat = nm / (1.0 - ADAM_B1 ** ADAM_STEP)
        v_hat = nv / (1.0 - ADAM_B2 ** ADAM_STEP)
        d_ref[...] = -ADAM_LR * (m_hat / (jnp.sqrt(v_hat) + ADAM_EPS) + ADAM_WD * w_ref[...])
        nm_ref[...] = nm
        nv_ref[...] = nv

    blk = pl.BlockSpec((tr, c), lambda i: (i, 0))
    return pl.pallas_call(
        body, name=name, grid=(r // tr,),
        in_specs=[blk] * 4, out_specs=[blk] * 3,
        out_shape=[jax.ShapeDtypeStruct((r, c), F32)] * 3,
        compiler_params=_params(("parallel",)),
    )(w, g, m, v)


def _add_half_bf16(g, b, place, name):
    n, half, c = b.shape
    tr = _tile(half, 512, 16)
    nb = half // tr

    def body(place_ref, g_ref, b_ref, o_ref):
        o_ref[...] = (g_ref[...].astype(F32) + b_ref[...].astype(F32)).astype(BF16)

    blk = pl.BlockSpec((1, tr, c), lambda i, j, p: (i, j, 0))
    return pl.pallas_call(
        body, name=name,
        grid_spec=pltpu.PrefetchScalarGridSpec(
            num_scalar_prefetch=1, grid=(n, nb),
            in_specs=[pl.BlockSpec((1, tr, c), lambda i, j, p: (i, p[0] * nb + j, 0)), blk], out_specs=blk),
        out_shape=jax.ShapeDtypeStruct((n, half, c), BF16),
        compiler_params=_params(("parallel", "parallel")),
    )(place, g, b)


def _sum4_f32(mine, others, place, name):
    _, half, c = mine.shape
    tr = _tile(half, 512, 16)
    nb = half // tr

    def body(place_ref, a_ref, b_ref, o_ref):
        acc = a_ref[0].astype(F32)
        for j in range(3):
            acc = acc + b_ref[j].astype(F32)
        o_ref[...] = acc

    return pl.pallas_call(
        body, name=name,
        grid_spec=pltpu.PrefetchScalarGridSpec(
            num_scalar_prefetch=1, grid=(nb,),
            in_specs=[pl.BlockSpec((1, tr, c), lambda i, p: (p[1], i, 0)),
                      pl.BlockSpec((3, tr, c), lambda i, p: (0, i, 0))],
            out_specs=pl.BlockSpec((tr, c), lambda i, p: (p[0] * nb + i, 0))),
        out_shape=jax.ShapeDtypeStruct((2 * half, c), F32),
        compiler_params=_params(("parallel",)),
    )(place, mine, others)


def _place():
    x, y, c = lax.axis_index("x"), lax.axis_index("y"), lax.axis_index("c")
    other_chips = [(1 - x, y), (x, 1 - y), (1 - x, 1 - y)]
    return x, y, c, (x, y, 1 - c), other_chips


ANY = pl.BlockSpec(memory_space=pl.ANY)


def _remote(k, src, dst, to, send_sems, recv_sems):
    return pltpu.make_async_remote_copy(src_ref=src, dst_ref=dst, send_sem=send_sems.at[k], recv_sem=recv_sems.at[k],
                                        device_id=to, device_id_type=MESH)


def _half(ref, lead, hc):
    if lead is None:
        half = ref.shape[0] // 2
        return ref.at[pl.ds(hc * half, half), :]
    half = ref.shape[1] // 2
    return ref.at[lead, pl.ds(hc * half, half), :]


def _all_gather_weights(slots, whole, name):
    nt, nw = len(slots), len(whole)
    n_ici = 3 * (nt + nw)

    def body(*refs):
        ins, outs = refs[:nt + nw], refs[nt + nw:2 * (nt + nw)]
        send_sems, recv_sems = refs[2 * (nt + nw):]
        x, y, c, sibling, chips = _place()
        me = 2 * x + y
        first = []
        for j, (px, py) in enumerate(chips):
            for t in range(nt):
                first.append(_remote(j * (nt + nw) + t, _half(ins[t], me, c), _half(outs[t], me, c), (px, py, c),
                                     send_sems, recv_sems))
            for t in range(nt, nt + nw):
                first.append(_remote(j * (nt + nw) + t, ins[t].at[me], outs[t].at[me], (px, py, c),
                                     send_sems, recv_sems))
        for cp in first:
            cp.start()
        passed = []
        for j, (px, py) in enumerate(chips):
            for t in range(nt):
                landed = _half(outs[t], 2 * px + py, c)
                _remote(j * (nt + nw) + t, landed, landed, (px, py, c), send_sems, recv_sems).wait_recv()
                fw = _remote(n_ici + j * nt + t, landed, landed, sibling, send_sems, recv_sems)
                fw.start()
                passed.append(fw)
            for t in range(nt, nt + nw):
                landed = outs[t].at[2 * px + py]
                _remote(j * (nt + nw) + t, landed, landed, (px, py, c), send_sems, recv_sems).wait_recv()
        for j, (px, py) in enumerate(chips):
            for t in range(nt):
                other = _half(outs[t], 2 * px + py, 1 - c)
                _remote(n_ici + j * nt + t, other, other, sibling, send_sems, recv_sems).wait_recv()
        for cp in first + passed:
            cp.wait_send()

    arrays = list(slots) + list(whole)
    n_sem = n_ici + 3 * nt
    return pl.pallas_call(
        body, name=name, in_specs=[ANY] * len(arrays), out_specs=[ANY] * len(arrays),
        input_output_aliases={i: i for i in range(len(arrays))},
        out_shape=[jax.ShapeDtypeStruct(a.shape, a.dtype) for a in arrays],
        scratch_shapes=[pltpu.SemaphoreType.DMA((n_sem,)), pltpu.SemaphoreType.DMA((n_sem,))],
    )(*arrays)


def _swap_halves_with_sibling(gs, name):
    nt = len(gs)

    def body(*refs):
        g_refs, o_refs, (send_sems, recv_sems) = refs[:nt], refs[nt:2 * nt], refs[2 * nt:]
        _, _, c, sibling, _ = _place()
        cps = []
        for t in range(nt):
            half = g_refs[t].shape[1] // 2
            cps.append(_remote(t, g_refs[t].at[:, pl.ds((1 - c) * half, half), :], o_refs[t], sibling,
                               send_sems, recv_sems))
        for cp in cps:
            cp.start()
        for cp in cps:
            cp.wait()

    return pl.pallas_call(
        body, name=name, in_specs=[ANY] * nt, out_specs=[ANY] * nt,
        out_shape=[jax.ShapeDtypeStruct((g.shape[0], g.shape[1] // 2, g.shape[2]), g.dtype) for g in gs],
        scratch_shapes=[pltpu.SemaphoreType.DMA((nt,)), pltpu.SemaphoreType.DMA((nt,))],
    )(*gs)


def _scatter_to_chips(ps, name):
    nt = len(ps)

    def body(*refs):
        p_refs, b_refs, (send_sems, recv_sems) = refs[:nt], refs[nt:2 * nt], refs[2 * nt:]
        _, _, c, _, chips = _place()
        cps = [_remote(j * nt + t, p_refs[t].at[2 * px + py], b_refs[t].at[j], (px, py, c), send_sems, recv_sems)
               for j, (px, py) in enumerate(chips) for t in range(nt)]
        for cp in cps:
            cp.start()
        for cp in cps:
            cp.wait()

    return pl.pallas_call(
        body, name=name, in_specs=[ANY] * nt, out_specs=[ANY] * nt,
        out_shape=[jax.ShapeDtypeStruct((3,) + p.shape[1:], p.dtype) for p in ps],
        scratch_shapes=[pltpu.SemaphoreType.DMA((3 * nt,)), pltpu.SemaphoreType.DMA((3 * nt,))],
    )(*ps)


def _join_halves(fs, name):
    nt = len(fs)

    def body(*refs):
        in_refs, out_refs, (send_sems, recv_sems) = refs[:nt], refs[nt:2 * nt], refs[2 * nt:]
        _, _, c, sibling, _ = _place()
        cps = [_remote(t, _half(in_refs[t], None, c), _half(out_refs[t], None, c), sibling, send_sems, recv_sems)
               for t in range(nt)]
        for cp in cps:
            cp.start()
        for t in range(nt):
            theirs = _half(out_refs[t], None, 1 - c)
            _remote(t, theirs, theirs, sibling, send_sems, recv_sems).wait_recv()
        for cp in cps:
            cp.wait_send()

    return pl.pallas_call(
        body, name=name, in_specs=[ANY] * nt, out_specs=[ANY] * nt,
        input_output_aliases={i: i for i in range(nt)},
        out_shape=[jax.ShapeDtypeStruct(f.shape, f.dtype) for f in fs],
        scratch_shapes=[pltpu.SemaphoreType.DMA((nt,)), pltpu.SemaphoreType.DMA((nt,))],
    )(*fs)


HBM = pl.BlockSpec(memory_space=pltpu.HBM)
SEM = pl.BlockSpec(memory_space=pltpu.SEMAPHORE)
EFFECT = pltpu.SideEffectType.DATAFLOW_SIDE_EFFECTING


def _split_start(arrays, after, plan, n_copies, name):
    na = len(arrays)

    def body(*refs):
        ins, send_sems, recv_sems = refs[:na], refs[na + 1], refs[na + 2]
        outs, token = refs[na + 3:2 * na + 3], refs[2 * na + 3]
        for k, (src, dst, to) in enumerate(plan(ins, outs)):
            _remote(k, src, dst, to, send_sems, recv_sems).start()
        token[...] = jnp.zeros_like(token)

    res = pl.pallas_call(
        body, name=name,
        out_shape=(pltpu.SemaphoreType.DMA((n_copies,)), pltpu.SemaphoreType.DMA((n_copies,)),
                   *[pltpu.HBM(a.shape, a.dtype) for a in arrays], jax.ShapeDtypeStruct((8, HEAD_DIM), F32)),
        in_specs=[HBM] * na + [ANY],
        out_specs=(SEM, SEM, *[HBM] * na, pl.BlockSpec(memory_space=pltpu.VMEM)),
        input_output_aliases={i: 2 + i for i in range(na)},
        compiler_params=pltpu.CompilerParams(has_side_effects=EFFECT),
    )(*[pltpu.with_memory_space_constraint(a, pltpu.HBM) for a in arrays], after)
    return res[0], res[1], list(res[2:2 + na]), res[2 + na]


def _split_wait(send_sems, recv_sems, arrays, after, plan, name):
    na = len(arrays)
    after = list(after) if isinstance(after, (list, tuple)) else [after]

    def body(*refs):
        ins, send, recv, outs = refs[:na], refs[na], refs[na + 1], refs[na + 2 + len(after):]
        for k, (src, dst, to) in enumerate(plan(ins, outs)):
            cp = _remote(k, src, dst, to, send, recv)
            cp.wait_send()
            cp.wait_recv()

    res = pl.pallas_call(
        body, name=name, out_shape=tuple(pltpu.HBM(a.shape, a.dtype) for a in arrays),
        in_specs=[HBM] * na + [SEM, SEM] + [ANY] * len(after), out_specs=tuple([HBM] * na),
        input_output_aliases={i: i for i in range(na)},
        compiler_params=pltpu.CompilerParams(has_side_effects=EFFECT),
    )(*arrays, send_sems, recv_sems, *after)
    return list(res)


def _gather_plan(nt):
    def plan(ins, outs):
        x, y, c, _, chips = _place()
        me = 2 * x + y
        return [(_half(ins[t], me, c), _half(outs[t], me, c), (px, py, c)) for px, py in chips for t in range(nt)]
    return plan


def _gather_landed_plan(nt):
    def plan(ins, outs):
        _, _, c, _, chips = _place()
        return [(_half(outs[t], 2 * px + py, c), _half(outs[t], 2 * px + py, c), (px, py, c))
                for px, py in chips for t in range(nt)]
    return plan


def _scatter_plan(nt):
    def plan(ins, outs):
        _, _, c, _, chips = _place()
        return [(ins[t].at[2 * px + py], outs[nt + t].at[j], (px, py, c))
                for j, (px, py) in enumerate(chips) for t in range(nt)]
    return plan


def _pass_halves_to_sibling(slots, name):
    nt = len(slots)

    def body(*refs):
        ins, outs, (send_sems, recv_sems) = refs[:nt], refs[nt:2 * nt], refs[2 * nt:]
        _, _, c, sibling, chips = _place()
        cps = [_remote(j * nt + t, _half(ins[t], 2 * px + py, c), _half(outs[t], 2 * px + py, c), sibling,
                       send_sems, recv_sems)
               for j, (px, py) in enumerate(chips) for t in range(nt)]
        for cp in cps:
            cp.start()
        for j, (px, py) in enumerate(chips):
            for t in range(nt):
                other = _half(outs[t], 2 * px + py, 1 - c)
                _remote(j * nt + t, other, other, sibling, send_sems, recv_sems).wait_recv()
        for cp in cps:
            cp.wait_send()

    return pl.pallas_call(
        body, name=name, in_specs=[ANY] * nt, out_specs=[ANY] * nt,
        input_output_aliases={i: i for i in range(nt)},
        out_shape=[jax.ShapeDtypeStruct(a.shape, a.dtype) for a in slots],
        scratch_shapes=[pltpu.SemaphoreType.DMA((3 * nt,)), pltpu.SemaphoreType.DMA((3 * nt,))],
    )(*slots)


def _all_reduce_small(v):
    r, lanes = v.shape

    def body(v_ref, out_ref, buf, send_sems, recv_sems):
        x, y, c, sibling, chips = _place()

        def slot(px, py, pc):
            return buf.at[4 * px + 2 * py + pc]

        def copy(k, block, to, src=None):
            return pltpu.make_async_remote_copy(src_ref=slot(*block) if src is None else src, dst_ref=slot(*block),
                                                send_sem=send_sems.at[k], recv_sem=recv_sems.at[k],
                                                device_id=to, device_id_type=MESH)

        me = (x, y, c)
        buf[4 * x + 2 * y + c] = v_ref[...]
        first = [copy(0, me, sibling, src=v_ref)]
        first += [copy(1 + j, me, (*chip, c), src=v_ref) for j, chip in enumerate(chips)]
        for cp in first:
            cp.start()
        passed = [copy(4 + j, (*chip, c), sibling) for j, chip in enumerate(chips)]
        for j, chip in enumerate(chips):
            copy(1 + j, (*chip, c), me).wait_recv()
            passed[j].start()
        copy(0, (x, y, 1 - c), me).wait_recv()
        for j, chip in enumerate(chips):
            copy(4 + j, (*chip, 1 - c), me).wait_recv()
        for cp in first + passed:
            cp.wait_send()
        acc = buf[0]
        for k in range(1, 8):
            acc = acc + buf[k]
        out_ref[...] = acc

    vmem = pl.BlockSpec(memory_space=pltpu.VMEM)
    return pl.pallas_call(
        body, name="all_reduce_small", in_specs=[vmem], out_specs=vmem,
        out_shape=jax.ShapeDtypeStruct((r, lanes), F32),
        scratch_shapes=[pltpu.VMEM((8, r, lanes), F32), pltpu.SemaphoreType.DMA((7,)), pltpu.SemaphoreType.DMA((7,))],
    )(v)


def _size(shape):
    n = 1
    for d in shape:
        n *= d
    return n


def _pack_small(parts):
    rows = []
    for p in parts:
        f = p.reshape(-1).astype(F32)
        n = -(-f.shape[0] // HEAD_DIM) * HEAD_DIM
        rows.append(jnp.pad(f, (0, n - f.shape[0])).reshape(-1, HEAD_DIM))
    a = jnp.concatenate(rows, axis=0)
    return jnp.pad(a, ((0, -a.shape[0] % 8), (0, 0)))


def _unpack_small(a, shapes):
    out, row = [], 0
    for shp in shapes:
        nrows = -(-_size(shp) // HEAD_DIM)
        out.append(a[row:row + nrows].reshape(-1)[:_size(shp)].reshape(shp))
        row += nrows
    return out


SMALL = ["attn_norm_w", "a_log", "dt_bias", "delta_out_norm_w", "q_norm_w", "k_norm_w", "attn_out_norm_w", "ffn_norm_w"]
BIG = ["w_in", "w_out", "w_gate_up", "w_down"]
ORDER = ["attn_norm_w", "w_in", "conv_w", "a_log", "dt_bias", "delta_out_norm_w", "q_norm_w", "k_norm_w",
         "attn_out_norm_w", "w_out", "ffn_norm_w", "w_gate_up", "w_down"]


def kernel(x, positions, attn_norm_w, w_in, conv_w, a_log, dt_bias, delta_out_norm_w, q_norm_w, k_norm_w, attn_out_norm_w, w_out, ffn_norm_w, w_gate_up, w_down, loss_target, m_attn_norm_w, m_w_in, m_conv_w, m_a_log, m_dt_bias, m_delta_out_norm_w, m_q_norm_w, m_k_norm_w, m_attn_out_norm_w, m_w_out, m_ffn_norm_w, m_w_gate_up, m_w_down, v_attn_norm_w, v_w_in, v_conv_w, v_a_log, v_dt_bias, v_delta_out_norm_w, v_q_norm_w, v_k_norm_w, v_attn_out_norm_w, v_w_out, v_ffn_norm_w, v_w_gate_up, v_w_down):
    wts = dict(attn_norm_w=attn_norm_w, w_in=w_in, conv_w=conv_w, a_log=a_log, dt_bias=dt_bias,
               delta_out_norm_w=delta_out_norm_w, q_norm_w=q_norm_w, k_norm_w=k_norm_w,
               attn_out_norm_w=attn_out_norm_w, w_out=w_out, ffn_norm_w=ffn_norm_w, w_gate_up=w_gate_up, w_down=w_down)
    mom = dict(attn_norm_w=m_attn_norm_w, w_in=m_w_in, conv_w=m_conv_w, a_log=m_a_log, dt_bias=m_dt_bias,
               delta_out_norm_w=m_delta_out_norm_w, q_norm_w=m_q_norm_w, k_norm_w=m_k_norm_w,
               attn_out_norm_w=m_attn_out_norm_w, w_out=m_w_out, ffn_norm_w=m_ffn_norm_w, w_gate_up=m_w_gate_up,
               w_down=m_w_down)
    var = dict(attn_norm_w=v_attn_norm_w, w_in=v_w_in, conv_w=v_conv_w, a_log=v_a_log, dt_bias=v_dt_bias,
               delta_out_norm_w=v_delta_out_norm_w, q_norm_w=v_q_norm_w, k_norm_w=v_k_norm_w,
               attn_out_norm_w=v_attn_out_norm_w, w_out=v_w_out, ffn_norm_w=v_ffn_norm_w, w_gate_up=v_w_gate_up,
               w_down=v_w_down)
    dmod = x.shape[2]
    heads = dmod // (2 * HEAD_DIM)
    dw = heads * HEAD_DIM
    chip = 2 * lax.axis_index("x") + lax.axis_index("y")
    core = lax.axis_index("c")
    n_in, n_out, n_gu, n_down, n_conv = (w_in.shape[2], w_out.shape[1], w_gate_up.shape[2], w_down.shape[1],
                                         conv_w.shape[2])

    def slots_of(w, dtype):
        shard = w[0].astype(dtype)
        return lax.dynamic_update_index_in_dim(lax.empty((4,) + shard.shape, dtype), shard, chip, axis=0)

    s_in, s_conv = _all_gather_weights([slots_of(w_in, BF16)], [slots_of(conv_w, F32)], "all_gather_w_in")
    later = [slots_of(w_out, BF16), slots_of(w_gate_up, BF16), slots_of(w_down, BF16)]
    w_send, w_recv, later, started = _split_start(later, s_conv, _gather_plan(3), 9, "gather_rest_start")
    by_cols = lambda a: a.transpose(1, 0, 2).reshape(a.shape[1], 4 * a.shape[2])
    w_in_f, conv_f = by_cols(s_in), by_cols(s_conv)
    w_bd = jnp.pad(w_in_f[:, 4 * dw:4 * dw + 2 * heads], ((0, 0), (0, HEAD_DIM - 2 * heads)))
    small = {n: wts[n] for n in SMALL}
    place = jnp.stack([core, chip]).astype(jnp.int32)
    to_slots = lambda a: a.reshape(a.shape[0], 4, a.shape[1] // 4).transpose(1, 0, 2)

    def later_weights(after):
        landed = _split_wait(w_send, w_recv, later, after, _gather_landed_plan(3), "gather_rest_wait")
        s_out, s_gu, s_down = _pass_halves_to_sibling(landed, "gather_rest_pass")
        return s_out.reshape(4 * n_out, dmod), by_cols(s_gu), s_down.reshape(4 * n_down, dmod)

    ffn = {}

    def ffn_grads_ready(g_gate, g_up, g_down):
        gs = [to_slots(jnp.concatenate([g_gate, g_up], axis=1)), g_down.reshape(4, n_down, dmod)]
        from_sibling = _swap_halves_with_sibling(gs, "swap_ffn_grad_halves")
        sums = [_add_half_bf16(g, b, place, "chip_partial_sum_" + n)
                for g, b, n in zip(gs, from_sibling, ["w_gate_up", "w_down"])]
        zones = [lax.empty((3,) + p.shape[1:], BF16) for p in sums]
        ffn["send"], ffn["recv"], ffn["bufs"], token = _split_start(sums + zones, from_sibling[0], _scatter_plan(2), 6,
                                                                    "scatter_ffn_start")
        return token

    loss_row, grad_x, sg, bg = _local_step(
        x[0], positions[0], loss_target[0], small, w_in_f[:, :4 * dw], w_bd, w_in_f[:, 4 * dw + 2 * heads:], conv_f,
        later_weights, ffn_grads_ready, after=started)

    gs = [to_slots(jnp.concatenate([bg["w_qkv"], bg["w_z"], bg["w_bd"][:, :2 * heads], bg["w_attn"]], axis=1)),
          jnp.concatenate([bg["w_out_a"], bg["w_out_b"]], axis=0).reshape(4, n_out, dmod)]
    from_sibling = _swap_halves_with_sibling(gs, "swap_grad_halves")
    sums = [_add_half_bf16(g, b, place, "chip_partial_sum_" + n) for g, b, n in zip(gs, from_sibling, ["w_in", "w_out"])]
    zones = [lax.empty((3,) + p.shape[1:], BF16) for p in sums]
    r_send, r_recv, r_bufs, r_started = _split_start(sums + zones, from_sibling[0], _scatter_plan(2), 6,
                                                     "scatter_rest_start")
    sum_gu, sum_down, got_gu, got_down = _split_wait(ffn["send"], ffn["recv"], ffn["bufs"], (grad_x, r_started),
                                                     _scatter_plan(2), "scatter_ffn_wait")
    g_big = dict(zip(["w_gate_up", "w_down"], _join_halves(
        [_sum4_f32(sum_gu, got_gu, place, "grad_total_w_gate_up"),
         _sum4_f32(sum_down, got_down, place, "grad_total_w_down")], "join_ffn_halves")))
    grads, deltas, new_m, new_v = {}, {}, {}, {}

    def adamw_big(n):
        shp = wts[n].shape
        d, nm, nv = _adamw(wts[n][0], g_big[n], mom[n][0], var[n][0], "adamw_" + n)
        grads[n], deltas[n], new_m[n], new_v[n] = g_big[n].reshape(shp), d.reshape(shp), nm.reshape(shp), nv.reshape(shp)
        return d

    done = [adamw_big("w_gate_up"), adamw_big("w_down")]
    sum_in, sum_out, got_in, got_out = _split_wait(r_send, r_recv, r_bufs, done, _scatter_plan(2), "scatter_rest_wait")
    g_big.update(zip(["w_in", "w_out"], _join_halves(
        [_sum4_f32(sum_in, got_in, place, "grad_total_w_in"),
         _sum4_f32(sum_out, got_out, place, "grad_total_w_out")], "join_rest_halves")))
    adamw_big("w_in")
    adamw_big("w_out")

    reduced = _all_reduce_small(_pack_small([sg[n] for n in SMALL] + [sg["conv_w"], loss_row]))
    red = _unpack_small(reduced, [wts[n].shape for n in SMALL] + [(4, 4 * n_conv), (1, HEAD_DIM)])
    g_small = dict(zip(SMALL, red[:len(SMALL)]))
    g_conv_full, loss_out = red[len(SMALL)], red[len(SMALL) + 1]
    g_small["conv_w"] = lax.dynamic_slice_in_dim(g_conv_full, chip * n_conv, n_conv, axis=1).reshape(conv_w.shape)

    names = SMALL + ["conv_w"]
    shapes = [wts[n].shape for n in names]
    d, nm, nv = _adamw(_pack_small([wts[n] for n in names]), _pack_small([g_small[n] for n in names]),
                       _pack_small([mom[n] for n in names]), _pack_small([var[n] for n in names]), "adamw_small")
    for n, dd, mm, vv in zip(names, _unpack_small(d, shapes), _unpack_small(nm, shapes), _unpack_small(nv, shapes)):
        grads[n], deltas[n], new_m[n], new_v[n] = g_small[n], dd, mm, vv
    return (loss_out[0, 0], grad_x[None], *[grads[n] for n in ORDER], *[deltas[n] for n in ORDER],
            *[new_m[n] for n in ORDER], *[new_v[n] for n in ORDER])
```

```python
import functools

import jax
import jax.numpy as jnp
from jax import lax
from jax.experimental import pallas as pl
from jax.experimental.pallas import tpu as pltpu

F32 = jnp.float32
BF16 = jnp.bfloat16
HEAD_DIM = 128
CHUNK = 128
INV_BLOCK = 64
SPAN = 128
DILATIONS = (1, 4, 16)
ROPE_THETA = 10000.0
EPS = 1e-6
NEG = -1e30
ADAM_LR, ADAM_B1, ADAM_B2, ADAM_EPS, ADAM_WD, ADAM_STEP = 0.001, 0.9, 0.999, 1e-08, 0.01, 10
VMEM_LIMIT = 48 * 1024 * 1024
MESH = pl.DeviceIdType.MESH

_DN = {"nn": (((1,), (0,)), ((), ())), "nt": (((1,), (1,)), ((), ())), "tn": (((0,), (0,)), ((), ()))}


def _dot(a, b, mode="nn"):
    (ca, cb), _ = _DN[mode]
    if a.ndim == 3:
        dn = (((ca[0] + 1,), (cb[0] + 1,)), ((0,), (0,)))
    else:
        dn = _DN[mode]
    return lax.dot_general(a.astype(BF16), b.astype(BF16), dn, preferred_element_type=F32)


def _rsum(x):
    return jnp.sum(x, axis=-1, keepdims=True)


def _csum(x):
    return jnp.sum(x, axis=-2, keepdims=True)


def _tile(dim, pref, unit=128):
    t = (min(pref, dim) // unit) * unit
    while t >= unit:
        if dim % t == 0:
            return t
        t -= unit
    return dim


def _params(sem):
    return pltpu.CompilerParams(dimension_semantics=sem, vmem_limit_bytes=VMEM_LIMIT)


def _sigmoid(x):
    return 1.0 / (1.0 + jnp.exp(-x))


def _matmul(a, b, mode, name, add=None, out_dtype=F32, a_cols=None, b_cols=None, b_rows=None):
    if mode == "tn":
        out_dtype = BF16
    a_off, a_w = a_cols if a_cols else (0, a.shape[1])
    b_off, b_w = b_cols if b_cols else (0, b.shape[1])
    br_off, br_n = b_rows if b_rows else (0, b.shape[0])
    if mode == "nn":
        m, k, n = a.shape[0], a_w, b_w
        assert br_n == k
    elif mode == "nt":
        m, k, n = a.shape[0], a_w, b.shape[0]
        assert b_w == k
    else:
        k, m, n = a.shape[0], a_w, b_w
        assert b.shape[0] == k
    tm, tn = _tile(m, 1024, 128), _tile(n, 1024, 128)
    tk = _tile(k, 2048 if a.dtype == BF16 and b.dtype == BF16 else 1024, 128)
    if mode == "nn":
        assert a_off % tk == 0 and b_off % tn == 0 and br_off % tk == 0
        a_spec = pl.BlockSpec((tm, tk), lambda i, j, kk: (i, kk + a_off // tk))
        b_spec = pl.BlockSpec((tk, tn), lambda i, j, kk: (kk + br_off // tk, j + b_off // tn))
    elif mode == "nt":
        assert a_off % tk == 0 and b_off % tk == 0
        a_spec = pl.BlockSpec((tm, tk), lambda i, j, kk: (i, kk + a_off // tk))
        b_spec = pl.BlockSpec((tn, tk), lambda i, j, kk: (j, kk + b_off // tk))
    else:
        assert a_off % tm == 0 and b_off % tn == 0
        a_spec = pl.BlockSpec((tk, tm), lambda i, j, kk: (kk, i + a_off // tm))
        b_spec = pl.BlockSpec((tk, tn), lambda i, j, kk: (kk, j + b_off // tn))
    nk = k // tk
    has_add = add is not None

    def body(*refs):
        if has_add:
            a_ref, b_ref, add_ref, o_ref, acc_ref = refs
        else:
            a_ref, b_ref, o_ref, acc_ref = refs
        kk = pl.program_id(2)

        @pl.when(kk == 0)
        def _():
            acc_ref[...] = jnp.zeros_like(acc_ref)

        acc_ref[...] += _dot(a_ref[...], b_ref[...], mode)

        @pl.when(kk == nk - 1)
        def _():
            r = acc_ref[...]
            if has_add:
                r = r + add_ref[...].astype(F32)
            o_ref[...] = r.astype(out_dtype)

    in_specs = [a_spec, b_spec]
    args = [a, b]
    if has_add:
        in_specs.append(pl.BlockSpec((tm, tn), lambda i, j, kk: (i, j)))
        args.append(add)
    return pl.pallas_call(
        body, name=name, grid=(m // tm, n // tn, nk),
        in_specs=in_specs, out_specs=pl.BlockSpec((tm, tn), lambda i, j, kk: (i, j)),
        out_shape=jax.ShapeDtypeStruct((m, n), out_dtype),
        scratch_shapes=[pltpu.VMEM((tm, tn), F32)],
        compiler_params=_params(("parallel", "parallel", "arbitrary")),
    )(*args)


def _rmsnorm_fwd(x, w, name, after=None):
    s, d = x.shape
    tr = _tile(s, 512, 8)

    def body(x_ref, w_ref, *rest):
        h_ref = rest[-1]
        xv = x_ref[...]
        rstd = lax.rsqrt(jnp.mean(xv * xv, axis=-1, keepdims=True) + EPS)
        h_ref[...] = (xv * rstd * w_ref[...]).astype(BF16)

    extra = [] if after is None else [after]
    return pl.pallas_call(
        body, name=name, grid=(s // tr,),
        in_specs=[pl.BlockSpec((tr, d), lambda i: (i, 0)), pl.BlockSpec((1, d), lambda i: (0, 0))] + [ANY] * len(extra),
        out_specs=pl.BlockSpec((tr, d), lambda i: (i, 0)),
        out_shape=jax.ShapeDtypeStruct((s, d), BF16),
        compiler_params=_params(("parallel",)),
    )(x, w, *extra)


def _rmsnorm_bwd(dh, x, w, res, name, after=None):
    s, d = x.shape
    tr = _tile(s, 256, 8)

    def body(dh_ref, x_ref, w_ref, res_ref, *rest):
        dx_ref, dx16_ref, dw_ref = rest[-3:]
        xv = x_ref[...]
        rstd = lax.rsqrt(jnp.mean(xv * xv, axis=-1, keepdims=True) + EPS)
        xhat = xv * rstd
        dhv = dh_ref[...]
        gw = dhv * w_ref[...]
        dx = res_ref[...] + rstd * (gw - xhat * jnp.mean(gw * xhat, axis=-1, keepdims=True))
        dx_ref[...] = dx
        dx16_ref[...] = dx.astype(BF16)

        @pl.when(pl.program_id(0) == 0)
        def _():
            dw_ref[...] = jnp.zeros_like(dw_ref)

        dw_ref[...] += jnp.sum(dhv * xhat, axis=0, keepdims=True)

    row = pl.BlockSpec((tr, d), lambda i: (i, 0))
    vec = pl.BlockSpec((1, d), lambda i: (0, 0))
    extra = [] if after is None else [after]
    return pl.pallas_call(
        body, name=name, grid=(s // tr,),
        in_specs=[row, row, vec, row] + [ANY] * len(extra), out_specs=[row, row, vec],
        out_shape=[jax.ShapeDtypeStruct((s, d), F32), jax.ShapeDtypeStruct((s, d), BF16),
                   jax.ShapeDtypeStruct((1, d), F32)],
        compiler_params=_params(("arbitrary",)),
    )(dh, x, w, res, *extra)


def _conv_taps(x, w, rows):
    shifted = [x]
    for sft in (1, 2, 3):
        shifted.append(jnp.where(rows >= sft, pltpu.roll(x, sft, 0), 0.0))
    y = w[3:4, :] * shifted[0] + w[2:3, :] * shifted[1] + w[1:2, :] * shifted[2] + w[0:1, :] * shifted[3]
    return y, shifted


def _delta_pre_fwd(qkvz, conv_w, heads):
    s = qkvz.shape[0]
    nblk = 3 * heads

    def body(x_ref, w_ref, o_ref):
        part = pl.program_id(0) // heads
        rows = lax.broadcasted_iota(jnp.int32, (s, HEAD_DIM), 0)
        y, _ = _conv_taps(x_ref[...], w_ref[...], rows)
        a = y * _sigmoid(y)
        rs = lax.rsqrt(jnp.sum(a * a, axis=-1, keepdims=True) + EPS)
        fac = jnp.where(part == 0, rs * (HEAD_DIM ** -0.5), jnp.where(part == 1, rs, 1.0))
        o_ref[...] = a * fac

    return pl.pallas_call(
        body, name="delta_pre_fwd", grid=(nblk,),
        in_specs=[pl.BlockSpec((s, HEAD_DIM), lambda i: (0, i)), pl.BlockSpec((4, HEAD_DIM), lambda i: (0, i))],
        out_specs=pl.BlockSpec((s, HEAD_DIM), lambda i: (0, i)),
        out_shape=jax.ShapeDtypeStruct((s, 3 * heads * HEAD_DIM), F32),
        compiler_params=_params(("parallel",)),
    )(qkvz, conv_w)


def _delta_pre_bwd(dqkv, qkvz, conv_w, heads):
    s = qkvz.shape[0]
    nblk = 3 * heads

    def body(d_ref, x_ref, w_ref, dx_ref, dw_ref):
        part = pl.program_id(0) // heads
        rows = lax.broadcasted_iota(jnp.int32, (s, HEAD_DIM), 0)
        w = w_ref[...]
        y, shifted = _conv_taps(x_ref[...], w, rows)
        sg = _sigmoid(y)
        a = y * sg
        rs = lax.rsqrt(jnp.sum(a * a, axis=-1, keepdims=True) + EPS)
        unit = a * rs
        dn = d_ref[...]
        scale = jnp.where(part == 0, HEAD_DIM ** -0.5, 1.0)
        da_norm = scale * rs * (dn - unit * jnp.sum(dn * unit, axis=-1, keepdims=True))
        da = jnp.where(part < 2, da_norm, dn)
        dy = da * sg * (1.0 + y * (1.0 - sg))
        dx = w[3:4, :] * dy
        for sft in (1, 2, 3):
            dx = dx + w[3 - sft:4 - sft, :] * jnp.where(rows < s - sft, pltpu.roll(dy, s - sft, 0), 0.0)
        dx_ref[...] = dx.astype(BF16)
        for sft in range(4):
            dw_ref[3 - sft:4 - sft, :] = jnp.sum(dy * shifted[sft], axis=0, keepdims=True)

    col = pl.BlockSpec((s, HEAD_DIM), lambda i: (0, i))
    wsp = pl.BlockSpec((4, HEAD_DIM), lambda i: (0, i))
    return pl.pallas_call(
        body, name="delta_pre_bwd", grid=(nblk,),
        in_specs=[col, col, wsp], out_specs=[col, wsp],
        out_shape=[jax.ShapeDtypeStruct((s, 3 * heads * HEAD_DIM), BF16),
                   jax.ShapeDtypeStruct((4, 3 * heads * HEAD_DIM), F32)],
        compiler_params=_params(("parallel",)),
    )(dqkv, qkvz, conv_w)


def _heads_of(ref, heads):
    return jnp.stack([ref[:, h * HEAD_DIM:(h + 1) * HEAD_DIM] for h in range(heads)])


def _chunk_common(q, k, v, bd, a_log, dt_bias, heads):
    c = CHUNK
    braw = jnp.stack([bd[:, h:h + 1] for h in range(heads)])
    draw = jnp.stack([bd[:, heads + h:heads + h + 1] for h in range(heads)])
    beta = _sigmoid(braw)
    xd = draw + dt_bias
    sp = jnp.maximum(xd, 0.0) + jnp.log1p(jnp.exp(-jnp.abs(xd)))
    g = -jnp.exp(a_log) * sp
    row = lax.broadcasted_iota(jnp.int32, (c, c), 0)
    col = lax.broadcasted_iota(jnp.int32, (c, c), 1)
    sq = (heads, c, c)
    g_b = jnp.broadcast_to(g, sq)
    g_row = _csum(jnp.where(row == col, g_b, 0.0))
    gam_col = _rsum(jnp.where(col <= row, jnp.broadcast_to(g_row, sq), 0.0))
    gam_row = _csum(jnp.where(row <= col, g_b, 0.0))
    causal = row >= col
    dm = jnp.where(causal, jnp.exp(jnp.where(causal, gam_col - gam_row, 0.0)), 0.0)
    kk = _dot(k, k, "nt")
    low = jnp.where(row > col, beta * kk * dm, 0.0)
    assert c in (INV_BLOCK, 2 * INV_BLOCK)
    same = (row // INV_BLOCK) == (col // INV_BLOCK)
    diag = jnp.where(same, low, 0.0)
    t = jnp.where(row == col, 1.0, 0.0) - diag
    pw = diag
    for _ in range((INV_BLOCK - 1).bit_length() - 1):
        pw = _dot(pw, pw)
        t = t + _dot(t, pw)
    if c > INV_BLOCK:
        t = t - _dot(_dot(t, low - diag), t)
    e = jnp.exp(gam_col)
    u = _dot(t, beta * v)
    w = _dot(t, (beta * e) * k)
    qk_raw = _dot(q, k, "nt")
    gl = _csum(g)
    el = jnp.exp(gl - gam_col)
    return dict(beta=beta, xd=xd, g=g, row=row, col=col, dm=dm, kk=kk, t=t, e=e, u=u, w=w,
                qk_raw=qk_raw, qk=qk_raw * dm, gl=gl, el=el, qd=e * q, kd=el * k, cd=jnp.exp(gl))


def _delta_chunk_fwd(qkv, bd, a_log, dt_bias, heads):
    s = qkv.shape[0]
    n = s // CHUNK
    dw = heads * HEAD_DIM
    blk = lambda part: pl.BlockSpec((CHUNK, dw), lambda i: (i, part))

    def body(q_ref, k_ref, v_ref, bd_ref, al_ref, dt_ref, o_ref, st_ref, state):
        @pl.when(pl.program_id(0) == 0)
        def _():
            state[...] = jnp.zeros_like(state)

        cm = _chunk_common(_heads_of(q_ref, heads), _heads_of(k_ref, heads), _heads_of(v_ref, heads), bd_ref[...],
                           al_ref[...], dt_ref[...], heads)
        st = state[...]
        st_ref[0] = st
        vn = cm["u"] - _dot(cm["w"], st)
        o = _dot(cm["qd"], st) + _dot(cm["qk"], vn)
        for h in range(heads):
            o_ref[:, h * HEAD_DIM:(h + 1) * HEAD_DIM] = o[h]
        state[...] = cm["cd"] * st + _dot(cm["kd"], vn, "tn")

    smem = pl.BlockSpec((heads, 1, 1), lambda i: (0, 0, 0))
    return pl.pallas_call(
        body, name="delta_chunk_fwd", grid=(n,),
        in_specs=[blk(0), blk(1), blk(2), pl.BlockSpec((CHUNK, HEAD_DIM), lambda i: (i, 0)), smem, smem],
        out_specs=[pl.BlockSpec((CHUNK, dw), lambda i: (i, 0)),
                   pl.BlockSpec((1, heads, HEAD_DIM, HEAD_DIM), lambda i: (i, 0, 0, 0))],
        out_shape=[jax.ShapeDtypeStruct((s, dw), F32),
                   jax.ShapeDtypeStruct((n, heads, HEAD_DIM, HEAD_DIM), F32)],
        scratch_shapes=[pltpu.VMEM((heads, HEAD_DIM, HEAD_DIM), F32)],
        compiler_params=_params(("arbitrary",)),
    )(qkv, qkv, qkv, bd, a_log, dt_bias)


def _delta_chunk_bwd(do, qkv, bd, states, a_log, dt_bias, heads):
    s = qkv.shape[0]
    n = s // CHUNK
    dw = heads * HEAD_DIM
    c = CHUNK
    blk = lambda part: pl.BlockSpec((CHUNK, dw), lambda i: (n - 1 - i, part))

    def all_heads(q, k, v, dov, st, dsn, bd, a_log, dt_bias):
        cm = _chunk_common(q, k, v, bd, a_log, dt_bias, heads)
        beta, e, dm, row, col = cm["beta"], cm["e"], cm["dm"], cm["row"], cm["col"]
        sq = (heads, c, c)
        vn = cm["u"] - _dot(cm["w"], st)
        dvn = _dot(cm["kd"], dsn)
        dkd = _dot(vn, dsn, "nt")
        dcd = _csum(_rsum(st * dsn))
        ds = cm["cd"] * dsn
        dqd = _dot(dov, st, "nt")
        ds = ds + _dot(cm["qd"], dov, "tn")
        dqk = _dot(dov, vn, "nt")
        dvn = dvn + _dot(cm["qk"], dov, "tn")
        dw_ = -_dot(dvn, st, "nt")
        ds = ds - _dot(cm["w"], dvn, "tn")
        drhs_u = _dot(cm["t"], dvn, "tn")
        drhs_w = _dot(cm["t"], dw_, "tn")
        da = -(_dot(drhs_u, cm["u"], "nt") + _dot(drhs_w, cm["w"], "nt"))
        dl = jnp.where(row > col, da, 0.0)
        dbeta = _rsum(dl * cm["kk"] * dm)
        dkk = dl * beta * dm
        dd = dl * beta * cm["kk"]
        dv = beta * drhs_u
        ek = e * k
        dbeta = dbeta + _rsum(drhs_u * v) + _rsum(drhs_w * ek)
        dk = (beta * e) * drhs_w
        dgam = _rsum(drhs_w * (beta * ek))
        dqkm = dqk * dm
        dq = _dot(dqkm, k)
        dk = dk + _dot(dqkm, q, "tn")
        dd = dd + dqk * cm["qk_raw"]
        dk = dk + _dot(dkk, k) + _dot(dkk, k, "tn")
        dq = dq + e * dqd
        dgam = dgam + _rsum(dqd * cm["qd"])
        dk = dk + cm["el"] * dkd
        r = _rsum(dkd * cm["kd"])
        dgam = dgam - r
        dgl = _csum(r) + dcd * cm["cd"]
        mm = dd * dm
        colsum_c = _rsum(jnp.where(row == col, jnp.broadcast_to(_csum(mm), sq), 0.0))
        dgam = dgam + _rsum(mm) - colsum_c
        ridx = lax.broadcasted_iota(jnp.int32, (c, 1), 0)
        dgam = dgam + jnp.where(ridx == c - 1, dgl, 0.0)
        dgam_row = _csum(jnp.where(row == col, jnp.broadcast_to(dgam, sq), 0.0))
        dg = _rsum(jnp.where(col >= row, jnp.broadcast_to(dgam_row, sq), 0.0))
        d_xd = dg * (-jnp.exp(a_log)) * _sigmoid(cm["xd"])
        d_braw = dbeta * beta * (1.0 - beta)
        d_alog = dg * cm["g"]
        lane = lax.broadcasted_iota(jnp.int32, (c, HEAD_DIM), 1)
        dbd = jnp.zeros((c, HEAD_DIM), F32)
        for h in range(heads):
            dbd = (dbd + jnp.where(lane == h, d_braw[h], 0.0) + jnp.where(lane == h + heads, d_xd[h], 0.0)
                   + jnp.where(lane == h + 2 * heads, d_alog[h], 0.0))
        return dq, dk, dv, ds, dbd

    def body(do_ref, q_ref, k_ref, v_ref, bd_ref, st_ref, al_ref, dt_ref, dqkv_ref, dbd_ref, dstate):
        @pl.when(pl.program_id(0) == 0)
        def _():
            dstate[...] = jnp.zeros_like(dstate)

        dq, dk, dv, ds, dbd = all_heads(_heads_of(q_ref, heads), _heads_of(k_ref, heads), _heads_of(v_ref, heads),
                                        _heads_of(do_ref, heads), st_ref[0], dstate[...], bd_ref[...],
                                        al_ref[...], dt_ref[...])
        for part, val in enumerate((dq, dk, dv)):
            for h in range(heads):
                lo = part * dw + h * HEAD_DIM
                dqkv_ref[:, lo:lo + HEAD_DIM] = val[h]
        dstate[...] = ds
        dbd_ref[...] = dbd

    smem = pl.BlockSpec((heads, 1, 1), lambda i: (0, 0, 0))
    shared = pl.BlockSpec((CHUNK, HEAD_DIM), lambda i: (n - 1 - i, 0))
    wide = pl.BlockSpec((CHUNK, dw), lambda i: (n - 1 - i, 0))
    return pl.pallas_call(
        body, name="delta_chunk_bwd", grid=(n,),
        in_specs=[wide, blk(0), blk(1), blk(2), shared,
                  pl.BlockSpec((1, heads, HEAD_DIM, HEAD_DIM), lambda i: (n - 1 - i, 0, 0, 0)), smem, smem],
        out_specs=[pl.BlockSpec((CHUNK, 3 * dw), lambda i: (n - 1 - i, 0)), shared],
        out_shape=[jax.ShapeDtypeStruct((s, 3 * dw), F32), jax.ShapeDtypeStruct((s, HEAD_DIM), F32)],
        scratch_shapes=[pltpu.VMEM((heads, HEAD_DIM, HEAD_DIM), F32)],
        compiler_params=_params(("arbitrary",)),
    )(do, qkv, qkv, qkv, bd, states, a_log, dt_bias)


def _delta_post_fwd(o, qkvz, w, heads):
    s = o.shape[0]
    tr = _tile(s, 1024, 8)

    def body(o_ref, z_ref, w_ref, out_ref):
        ov, z = o_ref[...], z_ref[...]
        rstd = lax.rsqrt(jnp.mean(ov * ov, axis=-1, keepdims=True) + EPS)
        out_ref[...] = (ov * rstd * w_ref[...] * (z * _sigmoid(z))).astype(BF16)

    return pl.pallas_call(
        body, name="delta_post_fwd", grid=(s // tr, heads),
        in_specs=[pl.BlockSpec((tr, HEAD_DIM), lambda i, h: (i, h)),
                  pl.BlockSpec((tr, HEAD_DIM), lambda i, h: (i, 3 * heads + h)),
                  pl.BlockSpec((1, HEAD_DIM), lambda i, h: (0, 0))],
        out_specs=pl.BlockSpec((tr, HEAD_DIM), lambda i, h: (i, h)),
        out_shape=jax.ShapeDtypeStruct((s, heads * HEAD_DIM), BF16),
        compiler_params=_params(("parallel", "parallel")),
    )(o, qkvz, w)


def _delta_post_bwd(dmix, o, qkvz, w, heads):
    s = o.shape[0]
    tr = _tile(s, 1024, 8)

    def body(d_ref, o_ref, z_ref, w_ref, do_ref, dz_ref, dw_ref):
        d, ov, z, wv = d_ref[...], o_ref[...], z_ref[...], w_ref[...]
        sg = _sigmoid(z)
        rstd = lax.rsqrt(jnp.mean(ov * ov, axis=-1, keepdims=True) + EPS)
        ohat = ov * rstd
        dz_ref[...] = (d * (ohat * wv) * sg * (1.0 + z * (1.0 - sg))).astype(BF16)
        dn = d * (z * sg)
        gw = dn * wv
        do_ref[...] = rstd * (gw - ohat * jnp.mean(gw * ohat, axis=-1, keepdims=True))

        @pl.when((pl.program_id(0) == 0) & (pl.program_id(1) == 0))
        def _():
            dw_ref[...] = jnp.zeros_like(dw_ref)

        dw_ref[...] += jnp.sum(dn * ohat, axis=0, keepdims=True)

    head = pl.BlockSpec((tr, HEAD_DIM), lambda i, h: (i, h))
    vec = pl.BlockSpec((1, HEAD_DIM), lambda i, h: (0, 0))
    dw = heads * HEAD_DIM
    return pl.pallas_call(
        body, name="delta_post_bwd", grid=(s // tr, heads),
        in_specs=[head, head, pl.BlockSpec((tr, HEAD_DIM), lambda i, h: (i, 3 * heads + h)), vec],
        out_specs=[head, head, vec],
        out_shape=[jax.ShapeDtypeStruct((s, dw), F32), jax.ShapeDtypeStruct((s, dw), BF16),
                   jax.ShapeDtypeStruct((1, HEAD_DIM), F32)],
        compiler_params=_params(("arbitrary", "arbitrary")),
    )(dmix, o, qkvz, w)


def _rope_tables(positions, s):
    half = HEAD_DIM // 2
    inv_freq = ROPE_THETA ** (-jnp.arange(half, dtype=F32) / half)
    ang = positions.reshape(s, 1).astype(F32) * inv_freq
    cos, sin = jnp.cos(ang), jnp.sin(ang)
    return jnp.concatenate([cos, cos], axis=-1), jnp.concatenate([-sin, sin], axis=-1)


def _attn_pre_fwd(aqkv, wq, wk, cosf, sinf, heads):
    s = aqkv.shape[0]
    tr = _tile(s, 1024, 8)

    def body(x_ref, wq_ref, wk_ref, c_ref, s_ref, o_ref):
        xv = x_ref[...]
        wv = jnp.where(pl.program_id(1) < heads, wq_ref[...], wk_ref[...])
        y = xv * lax.rsqrt(jnp.mean(xv * xv, axis=-1, keepdims=True) + EPS) * wv
        o_ref[...] = y * c_ref[...] + pltpu.roll(y, HEAD_DIM // 2, 1) * s_ref[...]

    blk = pl.BlockSpec((tr, HEAD_DIM), lambda i, j: (i, j))
    vec = pl.BlockSpec((1, HEAD_DIM), lambda i, j: (0, 0))
    tab = pl.BlockSpec((tr, HEAD_DIM), lambda i, j: (i, 0))
    return pl.pallas_call(
        body, name="attn_pre_fwd", grid=(s // tr, 2 * heads),
        in_specs=[blk, vec, vec, tab, tab], out_specs=blk,
        out_shape=jax.ShapeDtypeStruct((s, 2 * heads * HEAD_DIM), F32),
        compiler_params=_params(("parallel", "parallel")),
    )(aqkv, wq, wk, cosf, sinf)


ATT_ROWS = 2048


def _band():
    qi = lax.broadcasted_iota(jnp.int32, (SPAN, 2 * SPAN), 0)
    ki = lax.broadcasted_iota(jnp.int32, (SPAN, 2 * SPAN), 1)
    dist = qi + SPAN - ki
    return (dist >= 0) & (dist <= SPAN), ki >= SPAN


def _sub(g, r, d):
    if d == 1:
        return pl.ds(g * SPAN, SPAN)
    return pl.ds(g * SPAN * d + r, SPAN, stride=d)


def _attn_geometry(s, d):
    rb = min(ATT_ROWS, s)
    pr = SPAN * d
    return rb, pr, rb // pr, s // rb


def _attn_fwd(qk, aqkv, w, heads):
    s = qk.shape[0]
    aw = heads * HEAD_DIM
    rb, _, _, nstep = _attn_geometry(s, 1)

    def body(q_ref, k_ref, v_ref, *rest):
        w_ref, mix_ref, acc_ref, m_ref, l_ref = rest[6:]
        first = pl.program_id(1) == 0
        band, own = _band()
        edge = band & (own | jnp.logical_not(first))
        for p_i, d in enumerate(DILATIONS):
            kp_ref, vp_ref = rest[2 * p_i:2 * p_i + 2]
            g_cnt = _attn_geometry(s, d)[2]
            for r in range(d):
                for g in range(g_cnt):
                    rows = _sub(g, r, d)
                    if g == 0:
                        kp, vp, mask = kp_ref[_sub(0, r, d), :], vp_ref[_sub(0, r, d), :], edge
                    else:
                        kp, vp, mask = k_ref[_sub(g - 1, r, d), :], v_ref[_sub(g - 1, r, d), :], band
                    kcat = jnp.concatenate([kp, k_ref[rows, :]], axis=0)
                    vcat = jnp.concatenate([vp, v_ref[rows, :]], axis=0)
                    sc = _dot(q_ref[rows, :], kcat, "nt") * (HEAD_DIM ** -0.5)
                    sc = jnp.where(mask, sc, NEG)
                    m = jnp.max(sc, axis=-1, keepdims=True)
                    if p_i == 0:
                        p = jnp.exp(sc - m)
                        acc_ref[rows, :] = _dot(p, vcat)
                        l_new = jnp.sum(p, axis=-1, keepdims=True)
                    else:
                        m_old = m_ref[rows, 0:1]
                        m = jnp.maximum(m, m_old)
                        alpha = jnp.exp(m_old - m)
                        p = jnp.exp(sc - m)
                        acc_ref[rows, :] = alpha * acc_ref[rows, :] + _dot(p, vcat)
                        l_new = alpha * l_ref[rows, 0:1] + jnp.sum(p, axis=-1, keepdims=True)
                    m_ref[rows, :] = jnp.broadcast_to(m, (SPAN, HEAD_DIM))
                    l_ref[rows, :] = jnp.broadcast_to(l_new, (SPAN, HEAD_DIM))
        den = l_ref[...]
        ob = acc_ref[...] / den
        acc_ref[...] = ob
        m_ref[...] = m_ref[...] + jnp.log(den)
        rstd = lax.rsqrt(jnp.mean(ob * ob, axis=-1, keepdims=True) + EPS)
        mix_ref[...] = (ob * rstd * w_ref[...]).astype(BF16)

    cur = lambda off: pl.BlockSpec((rb, HEAD_DIM), lambda h, j: (j, off + h))
    in_specs = [cur(0), cur(heads), cur(2 * heads)]
    args = [qk, qk, aqkv]
    for d in DILATIONS:
        pr = SPAN * d
        prev = lambda off, pr=pr: pl.BlockSpec(
            (pr, HEAD_DIM), lambda h, j: (jnp.maximum(j * (rb // pr) - 1, 0), off + h))
        in_specs += [prev(heads), prev(2 * heads)]
        args += [qk, aqkv]
    out = pl.BlockSpec((rb, HEAD_DIM), lambda h, j: (j, h))
    return pl.pallas_call(
        body, name="attn_fwd", grid=(heads, nstep),
        in_specs=in_specs + [pl.BlockSpec((1, HEAD_DIM), lambda h, j: (0, 0))], out_specs=[out, out, out],
        out_shape=[jax.ShapeDtypeStruct((s, aw), BF16), jax.ShapeDtypeStruct((s, aw), F32),
                   jax.ShapeDtypeStruct((s, aw), F32)],
        scratch_shapes=[pltpu.VMEM((rb, HEAD_DIM), F32)],
        compiler_params=_params(("parallel", "parallel")),
    )(*args, w)


def _attn_merge_bwd(dmix, ob, w, heads):
    s = ob.shape[0]
    tr = _tile(s, 1024, 8)

    def body(d_ref, ob_ref, w_ref, do_ref, dsum_ref, dw_ref):
        d, ov = d_ref[...], ob_ref[...]
        rstd = lax.rsqrt(jnp.mean(ov * ov, axis=-1, keepdims=True) + EPS)
        ohat = ov * rstd
        gw = d * w_ref[...]
        dov = rstd * (gw - ohat * jnp.mean(gw * ohat, axis=-1, keepdims=True))
        do_ref[...] = dov
        dsum_ref[...] = jnp.broadcast_to(jnp.sum(dov * ov, axis=-1, keepdims=True), dov.shape)

        @pl.when((pl.program_id(0) == 0) & (pl.program_id(1) == 0))
        def _():
            dw_ref[...] = jnp.zeros_like(dw_ref)

        dw_ref[...] += jnp.sum(d * ohat, axis=0, keepdims=True)

    blk = pl.BlockSpec((tr, HEAD_DIM), lambda i, h: (i, h))
    vec = pl.BlockSpec((1, HEAD_DIM), lambda i, h: (0, 0))
    aw = heads * HEAD_DIM
    return pl.pallas_call(
        body, name="attn_merge_bwd", grid=(s // tr, heads),
        in_specs=[pl.BlockSpec((tr, HEAD_DIM), lambda i, h: (i, heads + h)), blk, vec],
        out_specs=[blk, blk, vec],
        out_shape=[jax.ShapeDtypeStruct((s, aw), F32), jax.ShapeDtypeStruct((s, aw), F32),
                   jax.ShapeDtypeStruct((1, HEAD_DIM), F32)],
        compiler_params=_params(("arbitrary", "arbitrary")),
    )(dmix, ob, w)


def _attn_bwd(qk, aqkv, do, lse, dsum, heads):
    s = qk.shape[0]
    aw = heads * HEAD_DIM
    rb, _, _, nstep = _attn_geometry(s, 1)
    scale = HEAD_DIM ** -0.5

    def body(q_ref, k_ref, v_ref, do_ref, l_ref, ds_ref, *rest):
        out_ref = rest[-1]
        first = pl.program_id(1) == 0
        last = pl.program_id(1) == nstep - 1
        band, own = _band()
        edge = band & (own | jnp.logical_not(first))
        dq_ref, dk_ref, dv_ref = out_ref.at[0], out_ref.at[1], out_ref.at[2]
        out_ref[...] = jnp.zeros((3, rb, HEAD_DIM), F32)
        for p_i, d in enumerate(DILATIONS):
            one_dilation(d, _attn_geometry(s, d)[2], rest[6 * p_i:6 * p_i + 6], q_ref, k_ref, v_ref, do_ref, l_ref,
                         ds_ref, dq_ref, dk_ref, dv_ref, band, edge, last)

    def one_dilation(d, g_cnt, extra, q_ref, k_ref, v_ref, do_ref, l_ref, ds_ref, dq_ref, dk_ref, dv_ref, band, edge,
                     last):
        kp_ref, vp_ref, qn_ref, don_ref, ln_ref, dsn_ref = extra
        for r in range(d):
            for g in range(g_cnt):
                rows = _sub(g, r, d)
                if g == 0:
                    kp, vp, mask = kp_ref[_sub(0, r, d), :], vp_ref[_sub(0, r, d), :], edge
                else:
                    prows = _sub(g - 1, r, d)
                    kp, vp, mask = k_ref[prows, :], v_ref[prows, :], band
                q, dov = q_ref[rows, :], do_ref[rows, :]
                kcat = jnp.concatenate([kp, k_ref[rows, :]], axis=0)
                vcat = jnp.concatenate([vp, v_ref[rows, :]], axis=0)
                sc = _dot(q, kcat, "nt") * scale
                p = jnp.where(mask, jnp.exp(sc - l_ref[rows, 0:1]), 0.0)
                dsc = p * (_dot(dov, vcat, "nt") - ds_ref[rows, 0:1]) * scale
                dq_ref[rows, :] += _dot(dsc, kcat)
                dk = _dot(dsc, q, "tn")
                dv = _dot(p, dov, "tn")
                dk_ref[rows, :] += dk[SPAN:]
                dv_ref[rows, :] += dv[SPAN:]
                if g > 0:
                    dk_ref[prows, :] += dk[:SPAN]
                    dv_ref[prows, :] += dv[:SPAN]
            rows, nrows = _sub(g_cnt - 1, r, d), _sub(0, r, d)
            q, dov = qn_ref[nrows, :], don_ref[nrows, :]
            sc = _dot(q, k_ref[rows, :], "nt") * scale
            p = jnp.where(band[:, :SPAN] & jnp.logical_not(last), jnp.exp(sc - ln_ref[nrows, 0:1]), 0.0)
            dsc = p * (_dot(dov, v_ref[rows, :], "nt") - dsn_ref[nrows, 0:1]) * scale
            dk_ref[rows, :] += _dot(dsc, q, "tn")
            dv_ref[rows, :] += _dot(p, dov, "tn")

    cur = lambda off: pl.BlockSpec((rb, HEAD_DIM), lambda h, j: (j, off + h))
    in_specs = [cur(0), cur(heads), cur(2 * heads), cur(0), cur(0), cur(0)]
    args = [qk, qk, aqkv, do, lse, dsum]
    for d in DILATIONS:
        pr = SPAN * d
        per = rb // pr
        prev = lambda off, pr=pr, per=per: pl.BlockSpec(
            (pr, HEAD_DIM), lambda h, j: (jnp.maximum(j * per - 1, 0), off + h))
        nxt = lambda off, pr=pr, per=per: pl.BlockSpec(
            (pr, HEAD_DIM), lambda h, j: (jnp.minimum((j + 1) * per, s // pr - 1), off + h))
        in_specs += [prev(heads), prev(2 * heads), nxt(0), nxt(0), nxt(0), nxt(0)]
        args += [qk, aqkv, qk, do, lse, dsum]
    return pl.pallas_call(
        body, name="attn_bwd", grid=(heads, nstep),
        in_specs=in_specs,
        out_specs=pl.BlockSpec((3, rb, HEAD_DIM), lambda h, j: (0, j, h)),
        out_shape=jax.ShapeDtypeStruct((3, s, aw), F32),
        compiler_params=_params(("parallel", "parallel")),
    )(*args)


def _attn_pre_bwd(grads, aqkv, wq, wk, cosf, sinf, heads):
    s = aqkv.shape[0]
    tr = _tile(s, 1024, 8)
    nrow = s // tr

    def body(g_ref, x_ref, wq_ref, wk_ref, c_ref, s_ref, dx_ref, dwq_ref, dwk_ref):
        i, j = pl.program_id(0), pl.program_id(1)
        kind = j // heads
        dout = g_ref[0]
        tot_v = dout
        dy = dout * c_ref[...] + pltpu.roll(dout * s_ref[...], HEAD_DIM // 2, 1)
        xv = x_ref[...]
        wv = jnp.where(kind == 0, wq_ref[...], wk_ref[...])
        rstd = lax.rsqrt(jnp.mean(xv * xv, axis=-1, keepdims=True) + EPS)
        xhat = xv * rstd
        gw = dy * wv
        dxn = rstd * (gw - xhat * jnp.mean(gw * xhat, axis=-1, keepdims=True))
        dx_ref[...] = jnp.where(kind == 2, tot_v, dxn).astype(BF16)
        dwc = jnp.sum(dy * xhat, axis=0, keepdims=True)

        @pl.when((i == 0) & (j == 0))
        def _():
            dwq_ref[...] = jnp.zeros_like(dwq_ref)
            dwk_ref[...] = jnp.zeros_like(dwk_ref)

        @pl.when(kind == 0)
        def _():
            dwq_ref[...] += dwc

        @pl.when(kind == 1)
        def _():
            dwk_ref[...] += dwc

    grad = pl.BlockSpec((1, tr, HEAD_DIM), lambda i, j: (j // heads, i, j % heads))
    blk = pl.BlockSpec((tr, HEAD_DIM), lambda i, j: (i, j))
    vec = pl.BlockSpec((1, HEAD_DIM), lambda i, j: (0, 0))
    tab = pl.BlockSpec((tr, HEAD_DIM), lambda i, j: (i, 0))
    return pl.pallas_call(
        body, name="attn_pre_bwd", grid=(nrow, 3 * heads),
        in_specs=[grad, blk, vec, vec, tab, tab], out_specs=[blk, vec, vec],
        out_shape=[jax.ShapeDtypeStruct((s, 3 * heads * HEAD_DIM), BF16),
                   jax.ShapeDtypeStruct((1, HEAD_DIM), F32), jax.ShapeDtypeStruct((1, HEAD_DIM), F32)],
        compiler_params=_params(("arbitrary", "arbitrary")),
    )(grads, aqkv, wq, wk, cosf, sinf)


def _swiglu_fwd(gu, ff):
    s = gu.shape[0]
    tr, tc = _tile(s, 512, 8), _tile(ff, 1024, 128)
    nc = ff // tc

    def body(g_ref, u_ref, o_ref):
        g = g_ref[...].astype(F32)
        o_ref[...] = (g * _sigmoid(g) * u_ref[...].astype(F32)).astype(BF16)

    return pl.pallas_call(
        body, name="swiglu_fwd", grid=(s // tr, nc),
        in_specs=[pl.BlockSpec((tr, tc), lambda i, j: (i, j)), pl.BlockSpec((tr, tc), lambda i, j: (i, j + nc))],
        out_specs=pl.BlockSpec((tr, tc), lambda i, j: (i, j)),
        out_shape=jax.ShapeDtypeStruct((s, ff), BF16),
        compiler_params=_params(("parallel", "parallel")),
    )(gu, gu)


def _swiglu_bwd(dact, gu, ff):
    s = gu.shape[0]
    tr, tc = _tile(s, 512, 8), _tile(ff, 1024, 128)
    nc = ff // tc

    def body(d_ref, g_ref, u_ref, dg_ref, du_ref):
        d, g = d_ref[...].astype(F32), g_ref[...].astype(F32)
        sg = _sigmoid(g)
        dg_ref[...] = (d * u_ref[...].astype(F32) * sg * (1.0 + g * (1.0 - sg))).astype(BF16)
        du_ref[...] = (d * g * sg).astype(BF16)

    lo = pl.BlockSpec((tr, tc), lambda i, j: (i, j))
    hi = pl.BlockSpec((tr, tc), lambda i, j: (i, j + nc))
    dg, du = pl.pallas_call(
        body, name="swiglu_bwd", grid=(s // tr, nc),
        in_specs=[lo, lo, hi], out_specs=[lo, lo],
        out_shape=[jax.ShapeDtypeStruct((s, ff), BF16)] * 2,
        compiler_params=_params(("parallel", "parallel")),
    )(dact, gu, gu)
    return dg, du


def _loss_head(y, target):
    s, d = y.shape
    tr = _tile(s, 512, 8)

    def body(y_ref, t_ref, dy_ref, dy16_ref, l_ref):
        err = y_ref[...] - t_ref[...]
        dy_ref[...] = err * (1.0 / d)
        dy16_ref[...] = (err * (1.0 / d)).astype(BF16)
        part = 0.5 * jnp.sum(jnp.sum(err * err, axis=-1, keepdims=True) * (1.0 / d), axis=0, keepdims=True)

        @pl.when(pl.program_id(0) == 0)
        def _():
            l_ref[...] = jnp.zeros_like(l_ref)

        lane = lax.broadcasted_iota(jnp.int32, (1, HEAD_DIM), 1)
        l_ref[...] += jnp.where(lane == 0, part, 0.0)

    row = pl.BlockSpec((tr, d), lambda i: (i, 0))
    return pl.pallas_call(
        body, name="loss_head", grid=(s // tr,),
        in_specs=[row, row], out_specs=[row, row, pl.BlockSpec((1, HEAD_DIM), lambda i: (0, 0))],
        out_shape=[jax.ShapeDtypeStruct((s, d), F32), jax.ShapeDtypeStruct((s, d), BF16),
                   jax.ShapeDtypeStruct((1, HEAD_DIM), F32)],
        compiler_params=_params(("arbitrary",)),
    )(y, target)


def _colsum(a, name):
    s, d = a.shape
    tr = _tile(s, 1024, 8)

    def body(a_ref, o_ref):
        @pl.when(pl.program_id(0) == 0)
        def _():
            o_ref[...] = jnp.zeros_like(o_ref)

        o_ref[...] += jnp.sum(a_ref[...], axis=0, keepdims=True)

    return pl.pallas_call(
        body, name=name, grid=(s // tr,),
        in_specs=[pl.BlockSpec((tr, d), lambda i: (i, 0))], out_specs=pl.BlockSpec((1, d), lambda i: (0, 0)),
        out_shape=jax.ShapeDtypeStruct((1, d), F32),
        compiler_params=_params(("arbitrary",)),
    )(a)


def _local_step(x, positions, target, small, w_qkvz, w_bd, w_attn, conv_w, later_weights, ffn_grads_ready,
                after=None):
    s, dmod = x.shape
    heads = dmod // (2 * HEAD_DIM)
    dw = heads * HEAD_DIM
    a_log, dt_bias = small["a_log"].reshape(heads, 1, 1), small["dt_bias"].reshape(heads, 1, 1)
    cosf, sinf = _rope_tables(positions, s)

    h1 = _rmsnorm_fwd(x, small["attn_norm_w"], "norm1_fwd", after=after)
    qkvz = _matmul(h1, w_qkvz, "nn", "proj_qkvz")
    bd = _matmul(h1, w_bd, "nn", "proj_bd")
    aqkv = _matmul(h1, w_attn, "nn", "proj_attn")
    dqkv = _delta_pre_fwd(qkvz, conv_w, heads)
    o_d, states = _delta_chunk_fwd(dqkv, bd, a_log, dt_bias, heads)
    mix_a = _delta_post_fwd(o_d, qkvz, small["delta_out_norm_w"], heads)
    qk_rot = _attn_pre_fwd(aqkv, small["q_norm_w"], small["k_norm_w"], cosf, sinf, heads)
    mix_b, ob, lse = _attn_fwd(qk_rot, aqkv, small["attn_out_norm_w"], heads)
    w_out, w_gu, w_down = later_weights((mix_a, mix_b))
    ff = w_down.shape[0]
    x1 = _matmul(mix_a, w_out, "nn", "out_proj_a", add=x, b_rows=(0, dw))
    x1 = _matmul(mix_b, w_out, "nn", "out_proj_b", add=x1, b_rows=(dw, dw))
    h2 = _rmsnorm_fwd(x1, small["ffn_norm_w"], "norm2_fwd")
    gu = _matmul(h2, w_gu, "nn", "ffn_gate_up", out_dtype=BF16)
    act = _swiglu_fwd(gu, ff)
    y = _matmul(act, w_down, "nn", "ffn_down", add=x1)
    dy, dy16, loss_row = _loss_head(y, target)

    dact = _matmul(dy16, w_down, "nt", "ffn_down_dx", out_dtype=BF16)
    g_w_down = _matmul(act, dy16, "tn", "ffn_down_dw")
    dgate, dup = _swiglu_bwd(dact, gu, ff)
    dh2 = _matmul(dgate, w_gu, "nt", "ffn_gate_dx", b_cols=(0, ff))
    dh2 = _matmul(dup, w_gu, "nt", "ffn_up_dx", b_cols=(ff, ff), add=dh2)
    g_w_gate = _matmul(h2, dgate, "tn", "ffn_gate_dw")
    g_w_up = _matmul(h2, dup, "tn", "ffn_up_dw")
    behind = ffn_grads_ready(g_w_gate, g_w_up, g_w_down)
    dx1, dx1_16, g_ffn_norm = _rmsnorm_bwd(dh2, x1, small["ffn_norm_w"], dy, "norm2_bwd", after=behind)
    dmix = _matmul(dx1_16, w_out, "nt", "out_proj_dx")
    g_w_out_a = _matmul(mix_a, dx1_16, "tn", "out_proj_dw_a")
    g_w_out_b = _matmul(mix_b, dx1_16, "tn", "out_proj_dw_b")
    dob, dsum, g_attn_out_norm = _attn_merge_bwd(dmix, ob, small["attn_out_norm_w"], heads)
    grads = _attn_bwd(qk_rot, aqkv, dob, lse, dsum, heads)
    d_aqkv, g_q_norm, g_k_norm = _attn_pre_bwd(grads, aqkv, small["q_norm_w"], small["k_norm_w"], cosf, sinf, heads)
    do_d, dz, g_delta_out_norm = _delta_post_bwd(dmix, o_d, qkvz, small["delta_out_norm_w"], heads)
    ddqkv, dbd = _delta_chunk_bwd(do_d, dqkv, bd, states, a_log, dt_bias, heads)
    d_qkv_raw, g_conv = _delta_pre_bwd(ddqkv, qkvz, conv_w, heads)
    bd_sums = _colsum(dbd, "bd_colsum")
    dh1 = _matmul(d_qkv_raw, w_qkvz, "nt", "proj_qkv_dx", b_cols=(0, 3 * dw))
    dh1 = _matmul(dz, w_qkvz, "nt", "proj_z_dx", b_cols=(3 * dw, dw), add=dh1)
    dh1 = _matmul(d_aqkv, w_attn, "nt", "proj_attn_dx", add=dh1)
    dh1 = _matmul(dbd, w_bd, "nt", "proj_bd_dx", add=dh1)
    g_w_qkv = _matmul(h1, d_qkv_raw, "tn", "proj_qkv_dw")
    g_w_z = _matmul(h1, dz, "tn", "proj_z_dw")
    g_w_bd = _matmul(h1, dbd, "tn", "proj_bd_dw")
    g_w_attn = _matmul(h1, d_aqkv, "tn", "proj_attn_dw")
    grad_x, _, g_attn_norm = _rmsnorm_bwd(dh1, x, small["attn_norm_w"], dx1, "norm1_bwd")
    small_grads = dict(
        attn_norm_w=g_attn_norm, a_log=bd_sums[:, 2 * heads:3 * heads], dt_bias=bd_sums[:, heads:2 * heads],
        delta_out_norm_w=g_delta_out_norm, q_norm_w=g_q_norm, k_norm_w=g_k_norm,
        attn_out_norm_w=g_attn_out_norm, ffn_norm_w=g_ffn_norm, conv_w=g_conv)
    big_grads = dict(w_qkv=g_w_qkv, w_z=g_w_z, w_bd=g_w_bd, w_attn=g_w_attn, w_out_a=g_w_out_a, w_out_b=g_w_out_b)
    return loss_row, grad_x, small_grads, big_grads


def _adamw(w, g, m, v, name):
    r, c = w.shape
    tr = _tile(r, 256, 8)

    def body(w_ref, g_ref, m_ref, v_ref, d_ref, nm_ref, nv_ref):
        gv = g_ref[...]
        nm = ADAM_B1 * m_ref[...] + (1.0 - ADAM_B1) * gv
        nv = ADAM_B2 * v_ref[...] + (1.0 - ADAM_B2) * (gv * gv)
        m_hat = nm / (1.0 - ADAM_B1 ** ADAM_STEP)
        v_hat = nv / (1.0 - ADAM_B2 ** ADAM_STEP)
        d_ref[...] = -ADAM_LR * (m_hat / (jnp.sqrt(v_hat) + ADAM_EPS) + ADAM_WD * w_ref[...])
        nm_ref[...] = nm
        nv_ref[...] = nv

    blk = pl.BlockSpec((tr, c), lambda i: (i, 0))
    return pl.pallas_call(
        body, name=name, grid=(r // tr,),
        in_specs=[blk] * 4, out_specs=[blk] * 3,
        out_shape=[jax.ShapeDtypeStruct((r, c), F32)] * 3,
        compiler_params=_params(("parallel",)),
    )(w, g, m, v)


def _add_half_bf16(g, b, place, name):
    n, half, c = b.shape
    tr = _tile(half, 512, 16)
    nb = half // tr

    def body(place_ref, g_ref, b_ref, o_ref):
        o_ref[...] = (g_ref[...].astype(F32) + b_ref[...].astype(F32)).astype(BF16)

    blk = pl.BlockSpec((1, tr, c), lambda i, j, p: (i, j, 0))
    return pl.pallas_call(
        body, name=name,
        grid_spec=pltpu.PrefetchScalarGridSpec(
            num_scalar_prefetch=1, grid=(n, nb),
            in_specs=[pl.BlockSpec((1, tr, c), lambda i, j, p: (i, p[0] * nb + j, 0)), blk], out_specs=blk),
        out_shape=jax.ShapeDtypeStruct((n, half, c), BF16),
        compiler_params=_params(("parallel", "parallel")),
    )(place, g, b)


def _sum4_f32(mine, others, place, name):
    _, half, c = mine.shape
    tr = _tile(half, 512, 16)
    nb = half // tr

    def body(place_ref, a_ref, b_ref, o_ref):
        acc = a_ref[0].astype(F32)
        for j in range(3):
            acc = acc + b_ref[j].astype(F32)
        o_ref[...] = acc

    return pl.pallas_call(
        body, name=name,
        grid_spec=pltpu.PrefetchScalarGridSpec(
            num_scalar_prefetch=1, grid=(nb,),
            in_specs=[pl.BlockSpec((1, tr, c), lambda i, p: (p[1], i, 0)),
                      pl.BlockSpec((3, tr, c), lambda i, p: (0, i, 0))],
            out_specs=pl.BlockSpec((tr, c), lambda i, p: (p[0] * nb + i, 0))),
        out_shape=jax.ShapeDtypeStruct((2 * half, c), F32),
        compiler_params=_params(("parallel",)),
    )(place, mine, others)


def _place():
    x, y, c = lax.axis_index("x"), lax.axis_index("y"), lax.axis_index("c")
    other_chips = [(1 - x, y), (x, 1 - y), (1 - x, 1 - y)]
    return x, y, c, (x, y, 1 - c), other_chips


ANY = pl.BlockSpec(memory_space=pl.ANY)


def _remote(k, src, dst, to, send_sems, recv_sems):
    return pltpu.make_async_remote_copy(src_ref=src, dst_ref=dst, send_sem=send_sems.at[k], recv_sem=recv_sems.at[k],
                                        device_id=to, device_id_type=MESH)


def _half(ref, lead, hc):
    if lead is None:
        half = ref.shape[0] // 2
        return ref.at[pl.ds(hc * half, half), :]
    half = ref.shape[1] // 2
    return ref.at[lead, pl.ds(hc * half, half), :]


def _all_gather_weights(slots, whole, name):
    nt, nw = len(slots), len(whole)
    n_ici = 3 * (nt + nw)

    def body(*refs):
        ins, outs = refs[:nt + nw], refs[nt + nw:2 * (nt + nw)]
        send_sems, recv_sems = refs[2 * (nt + nw):]
        x, y, c, sibling, chips = _place()
        me = 2 * x + y
        first = []
        for j, (px, py) in enumerate(chips):
            for t in range(nt):
                first.append(_remote(j * (nt + nw) + t, _half(ins[t], me, c), _half(outs[t], me, c), (px, py, c),
                                     send_sems, recv_sems))
            for t in range(nt, nt + nw):
                first.append(_remote(j * (nt + nw) + t, ins[t].at[me], outs[t].at[me], (px, py, c),
                                     send_sems, recv_sems))
        for cp in first:
            cp.start()
        passed = []
        for j, (px, py) in enumerate(chips):
            for t in range(nt):
                landed = _half(outs[t], 2 * px + py, c)
                _remote(j * (nt + nw) + t, landed, landed, (px, py, c), send_sems, recv_sems).wait_recv()
                fw = _remote(n_ici + j * nt + t, landed, landed, sibling, send_sems, recv_sems)
                fw.start()
                passed.append(fw)
            for t in range(nt, nt + nw):
                landed = outs[t].at[2 * px + py]
                _remote(j * (nt + nw) + t, landed, landed, (px, py, c), send_sems, recv_sems).wait_recv()
        for j, (px, py) in enumerate(chips):
            for t in range(nt):
                other = _half(outs[t], 2 * px + py, 1 - c)
                _remote(n_ici + j * nt + t, other, other, sibling, send_sems, recv_sems).wait_recv()
        for cp in first + passed:
            cp.wait_send()

    arrays = list(slots) + list(whole)
    n_sem = n_ici + 3 * nt
    return pl.pallas_call(
        body, name=name, in_specs=[ANY] * len(arrays), out_specs=[ANY] * len(arrays),
        input_output_aliases={i: i for i in range(len(arrays))},
        out_shape=[jax.ShapeDtypeStruct(a.shape, a.dtype) for a in arrays],
        scratch_shapes=[pltpu.SemaphoreType.DMA((n_sem,)), pltpu.SemaphoreType.DMA((n_sem,))],
    )(*arrays)


def _swap_halves_with_sibling(gs, name):
    nt = len(gs)

    def body(*refs):
        g_refs, o_refs, (send_sems, recv_sems) = refs[:nt], refs[nt:2 * nt], refs[2 * nt:]
        _, _, c, sibling, _ = _place()
        cps = []
        for t in range(nt):
            half = g_refs[t].shape[1] // 2
            cps.append(_remote(t, g_refs[t].at[:, pl.ds((1 - c) * half, half), :], o_refs[t], sibling,
                               send_sems, recv_sems))
        for cp in cps:
            cp.start()
        for cp in cps:
            cp.wait()

    return pl.pallas_call(
        body, name=name, in_specs=[ANY] * nt, out_specs=[ANY] * nt,
        out_shape=[jax.ShapeDtypeStruct((g.shape[0], g.shape[1] // 2, g.shape[2]), g.dtype) for g in gs],
        scratch_shapes=[pltpu.SemaphoreType.DMA((nt,)), pltpu.SemaphoreType.DMA((nt,))],
    )(*gs)


def _scatter_to_chips(ps, name):
    nt = len(ps)

    def body(*refs):
        p_refs, b_refs, (send_sems, recv_sems) = refs[:nt], refs[nt:2 * nt], refs[2 * nt:]
        _, _, c, _, chips = _place()
        cps = [_remote(j * nt + t, p_refs[t].at[2 * px + py], b_refs[t].at[j], (px, py, c), send_sems, recv_sems)
               for j, (px, py) in enumerate(chips) for t in range(nt)]
        for cp in cps:
            cp.start()
        for cp in cps:
            cp.wait()

    return pl.pallas_call(
        body, name=name, in_specs=[ANY] * nt, out_specs=[ANY] * nt,
        out_shape=[jax.ShapeDtypeStruct((3,) + p.shape[1:], p.dtype) for p in ps],
        scratch_shapes=[pltpu.SemaphoreType.DMA((3 * nt,)), pltpu.SemaphoreType.DMA((3 * nt,))],
    )(*ps)


def _join_halves(fs, name):
    nt = len(fs)

    def body(*refs):
        in_refs, out_refs, (send_sems, recv_sems) = refs[:nt], refs[nt:2 * nt], refs[2 * nt:]
        _, _, c, sibling, _ = _place()
        cps = [_remote(t, _half(in_refs[t], None, c), _half(out_refs[t], None, c), sibling, send_sems, recv_sems)
               for t in range(nt)]
        for cp in cps:
            cp.start()
        for t in range(nt):
            theirs = _half(out_refs[t], None, 1 - c)
            _remote(t, theirs, theirs, sibling, send_sems, recv_sems).wait_recv()
        for cp in cps:
            cp.wait_send()

    return pl.pallas_call(
        body, name=name, in_specs=[ANY] * nt, out_specs=[ANY] * nt,
        input_output_aliases={i: i for i in range(nt)},
        out_shape=[jax.ShapeDtypeStruct(f.shape, f.dtype) for f in fs],
        scratch_shapes=[pltpu.SemaphoreType.DMA((nt,)), pltpu.SemaphoreType.DMA((nt,))],
    )(*fs)


HBM = pl.BlockSpec(memory_space=pltpu.HBM)
SEM = pl.BlockSpec(memory_space=pltpu.SEMAPHORE)
EFFECT = pltpu.SideEffectType.DATAFLOW_SIDE_EFFECTING


def _split_start(arrays, after, plan, n_copies, name):
    na = len(arrays)

    def body(*refs):
        ins, send_sems, recv_sems = refs[:na], refs[na + 1], refs[na + 2]
        outs, token = refs[na + 3:2 * na + 3], refs[2 * na + 3]
        for k, (src, dst, to) in enumerate(plan(ins, outs)):
            _remote(k, src, dst, to, send_sems, recv_sems).start()
        token[...] = jnp.zeros_like(token)

    res = pl.pallas_call(
        body, name=name,
        out_shape=(pltpu.SemaphoreType.DMA((n_copies,)), pltpu.SemaphoreType.DMA((n_copies,)),
                   *[pltpu.HBM(a.shape, a.dtype) for a in arrays], jax.ShapeDtypeStruct((8, HEAD_DIM), F32)),
        in_specs=[HBM] * na + [ANY],
        out_specs=(SEM, SEM, *[HBM] * na, pl.BlockSpec(memory_space=pltpu.VMEM)),
        input_output_aliases={i: 2 + i for i in range(na)},
        compiler_params=pltpu.CompilerParams(has_side_effects=EFFECT),
    )(*[pltpu.with_memory_space_constraint(a, pltpu.HBM) for a in arrays], after)
    return res[0], res[1], list(res[2:2 + na]), res[2 + na]


def _split_wait(send_sems, recv_sems, arrays, after, plan, name):
    na = len(arrays)
    after = list(after) if isinstance(after, (list, tuple)) else [after]

    def body(*refs):
        ins, send, recv, outs = refs[:na], refs[na], refs[na + 1], refs[na + 2 + len(after):]
        for k, (src, dst, to) in enumerate(plan(ins, outs)):
            cp = _remote(k, src, dst, to, send, recv)
            cp.wait_send()
            cp.wait_recv()

    res = pl.pallas_call(
        body, name=name, out_shape=tuple(pltpu.HBM(a.shape, a.dtype) for a in arrays),
        in_specs=[HBM] * na + [SEM, SEM] + [ANY] * len(after), out_specs=tuple([HBM] * na),
        input_output_aliases={i: i for i in range(na)},
        compiler_params=pltpu.CompilerParams(has_side_effects=EFFECT),
    )(*arrays, send_sems, recv_sems, *after)
    return list(res)


def _gather_plan(nt):
    def plan(ins, outs):
        x, y, c, _, chips = _place()
        me = 2 * x + y
        return [(_half(ins[t], me, c), _half(outs[t], me, c), (px, py, c)) for px, py in chips for t in range(nt)]
    return plan


def _gather_landed_plan(nt):
    def plan(ins, outs):
        _, _, c, _, chips = _place()
        return [(_half(outs[t], 2 * px + py, c), _half(outs[t], 2 * px + py, c), (px, py, c))
                for px, py in chips for t in range(nt)]
    return plan


def _scatter_plan(nt):
    def plan(ins, outs):
        _, _, c, _, chips = _place()
        return [(ins[t].at[2 * px + py], outs[nt + t].at[j], (px, py, c))
                for j, (px, py) in enumerate(chips) for t in range(nt)]
    return plan


def _pass_halves_to_sibling(slots, name):
    nt = len(slots)

    def body(*refs):
        ins, outs, (send_sems, recv_sems) = refs[:nt], refs[nt:2 * nt], refs[2 * nt:]
        _, _, c, sibling, chips = _place()
        cps = [_remote(j * nt + t, _half(ins[t], 2 * px + py, c), _half(outs[t], 2 * px + py, c), sibling,
                       send_sems, recv_sems)
               for j, (px, py) in enumerate(chips) for t in range(nt)]
        for cp in cps:
            cp.start()
        for j, (px, py) in enumerate(chips):
            for t in range(nt):
                other = _half(outs[t], 2 * px + py, 1 - c)
                _remote(j * nt + t, other, other, sibling, send_sems, recv_sems).wait_recv()
        for cp in cps:
            cp.wait_send()

    return pl.pallas_call(
        body, name=name, in_specs=[ANY] * nt, out_specs=[ANY] * nt,
        input_output_aliases={i: i for i in range(nt)},
        out_shape=[jax.ShapeDtypeStruct(a.shape, a.dtype) for a in slots],
        scratch_shapes=[pltpu.SemaphoreType.DMA((3 * nt,)), pltpu.SemaphoreType.DMA((3 * nt,))],
    )(*slots)


def _all_reduce_small(v):
    r, lanes = v.shape

    def body(v_ref, out_ref, buf, send_sems, recv_sems):
        x, y, c, sibling, chips = _place()

        def slot(px, py, pc):
            return buf.at[4 * px + 2 * py + pc]

        def copy(k, block, to, src=None):
            return pltpu.make_async_remote_copy(src_ref=slot(*block) if src is None else src, dst_ref=slot(*block),
                                                send_sem=send_sems.at[k], recv_sem=recv_sems.at[k],
                                                device_id=to, device_id_type=MESH)

        me = (x, y, c)
        buf[4 * x + 2 * y + c] = v_ref[...]
        first = [copy(0, me, sibling, src=v_ref)]
        first += [copy(1 + j, me, (*chip, c), src=v_ref) for j, chip in enumerate(chips)]
        for cp in first:
            cp.start()
        passed = [copy(4 + j, (*chip, c), sibling) for j, chip in enumerate(chips)]
        for j, chip in enumerate(chips):
            copy(1 + j, (*chip, c), me).wait_recv()
            passed[j].start()
        copy(0, (x, y, 1 - c), me).wait_recv()
        for j, chip in enumerate(chips):
            copy(4 + j, (*chip, 1 - c), me).wait_recv()
        for cp in first + passed:
            cp.wait_send()
        acc = buf[0]
        for k in range(1, 8):
            acc = acc + buf[k]
        out_ref[...] = acc

    vmem = pl.BlockSpec(memory_space=pltpu.VMEM)
    return pl.pallas_call(
        body, name="all_reduce_small", in_specs=[vmem], out_specs=vmem,
        out_shape=jax.ShapeDtypeStruct((r, lanes), F32),
        scratch_shapes=[pltpu.VMEM((8, r, lanes), F32), pltpu.SemaphoreType.DMA((7,)), pltpu.SemaphoreType.DMA((7,))],
    )(v)


def _size(shape):
    n = 1
    for d in shape:
        n *= d
    return n


def _pack_small(parts):
    rows = []
    for p in parts:
        f = p.reshape(-1).astype(F32)
        n = -(-f.shape[0] // HEAD_DIM) * HEAD_DIM
        rows.append(jnp.pad(f, (0, n - f.shape[0])).reshape(-1, HEAD_DIM))
    a = jnp.concatenate(rows, axis=0)
    return jnp.pad(a, ((0, -a.shape[0] % 8), (0, 0)))


def _unpack_small(a, shapes):
    out, row = [], 0
    for shp in shapes:
        nrows = -(-_size(shp) // HEAD_DIM)
        out.append(a[row:row + nrows].reshape(-1)[:_size(shp)].reshape(shp))
        row += nrows
    return out


SMALL = ["attn_norm_w", "a_log", "dt_bias", "delta_out_norm_w", "q_norm_w", "k_norm_w", "attn_out_norm_w", "ffn_norm_w"]
BIG = ["w_in", "w_out", "w_gate_up", "w_down"]
ORDER = ["attn_norm_w", "w_in", "conv_w", "a_log", "dt_bias", "delta_out_norm_w", "q_norm_w", "k_norm_w",
         "attn_out_norm_w", "w_out", "ffn_norm_w", "w_gate_up", "w_down"]


def kernel(x, positions, attn_norm_w, w_in, conv_w, a_log, dt_bias, delta_out_norm_w, q_norm_w, k_norm_w, attn_out_norm_w, w_out, ffn_norm_w, w_gate_up, w_down, loss_target, m_attn_norm_w, m_w_in, m_conv_w, m_a_log, m_dt_bias, m_delta_out_norm_w, m_q_norm_w, m_k_norm_w, m_attn_out_norm_w, m_w_out, m_ffn_norm_w, m_w_gate_up, m_w_down, v_attn_norm_w, v_w_in, v_conv_w, v_a_log, v_dt_bias, v_delta_out_norm_w, v_q_norm_w, v_k_norm_w, v_attn_out_norm_w, v_w_out, v_ffn_norm_w, v_w_gate_up, v_w_down):
    wts = dict(attn_norm_w=attn_norm_w, w_in=w_in, conv_w=conv_w, a_log=a_log, dt_bias=dt_bias,
               delta_out_norm_w=delta_out_norm_w, q_norm_w=q_norm_w, k_norm_w=k_norm_w,
               attn_out_norm_w=attn_out_norm_w, w_out=w_out, ffn_norm_w=ffn_norm_w, w_gate_up=w_gate_up, w_down=w_down)
    mom = dict(attn_norm_w=m_attn_norm_w, w_in=m_w_in, conv_w=m_conv_w, a_log=m_a_log, dt_bias=m_dt_bias,
               delta_out_norm_w=m_delta_out_norm_w, q_norm_w=m_q_norm_w, k_norm_w=m_k_norm_w,
               attn_out_norm_w=m_attn_out_norm_w, w_out=m_w_out, ffn_norm_w=m_ffn_norm_w, w_gate_up=m_w_gate_up,
               w_down=m_w_down)
    var = dict(attn_norm_w=v_attn_norm_w, w_in=v_w_in, conv_w=v_conv_w, a_log=v_a_log, dt_bias=v_dt_bias,
               delta_out_norm_w=v_delta_out_norm_w, q_norm_w=v_q_norm_w, k_norm_w=v_k_norm_w,
               attn_out_norm_w=v_attn_out_norm_w, w_out=v_w_out, ffn_norm_w=v_ffn_norm_w, w_gate_up=v_w_gate_up,
               w_down=v_w_down)
    dmod = x.shape[2]
    heads = dmod // (2 * HEAD_DIM)
    dw = heads * HEAD_DIM
    chip = 2 * lax.axis_index("x") + lax.axis_index("y")
    core = lax.axis_index("c")
    n_in, n_out, n_gu, n_down, n_conv = (w_in.shape[2], w_out.shape[1], w_gate_up.shape[2], w_down.shape[1],
                                         conv_w.shape[2])

    def slots_of(w, dtype):
        shard = w[0].astype(dtype)
        return lax.dynamic_update_index_in_dim(lax.empty((4,) + shard.shape, dtype), shard, chip, axis=0)

    s_in, s_conv = _all_gather_weights([slots_of(w_in, BF16)], [slots_of(conv_w, F32)], "all_gather_w_in")
    later = [slots_of(w_out, BF16), slots_of(w_gate_up, BF16), slots_of(w_down, BF16)]
    w_send, w_recv, later, started = _split_start(later, s_conv, _gather_plan(3), 9, "gather_rest_start")
    by_cols = lambda a: a.transpose(1, 0, 2).reshape(a.shape[1], 4 * a.shape[2])
    w_in_f, conv_f = by_cols(s_in), by_cols(s_conv)
    w_bd = jnp.pad(w_in_f[:, 4 * dw:4 * dw + 2 * heads], ((0, 0), (0, HEAD_DIM - 2 * heads)))
    small = {n: wts[n] for n in SMALL}
    place = jnp.stack([core, chip]).astype(jnp.int32)
    to_slots = lambda a: a.reshape(a.shape[0], 4, a.shape[1] // 4).transpose(1, 0, 2)

    def later_weights(after):
        landed = _split_wait(w_send, w_recv, later, after, _gather_landed_plan(3), "gather_rest_wait")
        s_out, s_gu, s_down = _pass_halves_to_sibling(landed, "gather_rest_pass")
        return s_out.reshape(4 * n_out, dmod), by_cols(s_gu), s_down.reshape(4 * n_down, dmod)

    ffn = {}

    def ffn_grads_ready(g_gate, g_up, g_down):
        gs = [to_slots(jnp.concatenate([g_gate, g_up], axis=1)), g_down.reshape(4, n_down, dmod)]
        from_sibling = _swap_halves_with_sibling(gs, "swap_ffn_grad_halves")
        sums = [_add_half_bf16(g, b, place, "chip_partial_sum_" + n)
                for g, b, n in zip(gs, from_sibling, ["w_gate_up", "w_down"])]
        zones = [lax.empty((3,) + p.shape[1:], BF16) for p in sums]
        ffn["send"], ffn["recv"], ffn["bufs"], token = _split_start(sums + zones, from_sibling[0], _scatter_plan(2), 6,
                                                                    "scatter_ffn_start")
        return token

    loss_row, grad_x, sg, bg = _local_step(
        x[0], positions[0], loss_target[0], small, w_in_f[:, :4 * dw], w_bd, w_in_f[:, 4 * dw + 2 * heads:], conv_f,
        later_weights, ffn_grads_ready, after=started)

    gs = [to_slots(jnp.concatenate([bg["w_qkv"], bg["w_z"], bg["w_bd"][:, :2 * heads], bg["w_attn"]], axis=1)),
          jnp.concatenate([bg["w_out_a"], bg["w_out_b"]], axis=0).reshape(4, n_out, dmod)]
    from_sibling = _swap_halves_with_sibling(gs, "swap_grad_halves")
    sums = [_add_half_bf16(g, b, place, "chip_partial_sum_" + n) for g, b, n in zip(gs, from_sibling, ["w_in", "w_out"])]
    zones = [lax.empty((3,) + p.shape[1:], BF16) for p in sums]
    r_send, r_recv, r_bufs, r_started = _split_start(sums + zones, from_sibling[0], _scatter_plan(2), 6,
                                                     "scatter_rest_start")
    sum_gu, sum_down, got_gu, got_down = _split_wait(ffn["send"], ffn["recv"], ffn["bufs"], (grad_x, r_started),
                                                     _scatter_plan(2), "scatter_ffn_wait")
    g_big = dict(zip(["w_gate_up", "w_down"], _join_halves(
        [_sum4_f32(sum_gu, got_gu, place, "grad_total_w_gate_up"),
         _sum4_f32(sum_down, got_down, place, "grad_total_w_down")], "join_ffn_halves")))
    grads, deltas, new_m, new_v = {}, {}, {}, {}

    def adamw_big(n):
        shp = wts[n].shape
        d, nm, nv = _adamw(wts[n][0], g_big[n], mom[n][0], var[n][0], "adamw_" + n)
        grads[n], deltas[n], new_m[n], new_v[n] = g_big[n].reshape(shp), d.reshape(shp), nm.reshape(shp), nv.reshape(shp)
        return d

    done = [adamw_big("w_gate_up"), adamw_big("w_down")]
    sum_in, sum_out, got_in, got_out = _split_wait(r_send, r_recv, r_bufs, done, _scatter_plan(2), "scatter_rest_wait")
    g_big.update(zip(["w_in", "w_out"], _join_halves(
        [_sum4_f32(sum_in, got_in, place, "grad_total_w_in"),
         _sum4_f32(sum_out, got_out, place, "grad_total_w_out")], "join_rest_halves")))
    adamw_big("w_in")
    adamw_big("w_out")

    reduced = _all_reduce_small(_pack_small([sg[n] for n in SMALL] + [sg["conv_w"], loss_row]))
    red = _unpack_small(reduced, [wts[n].shape for n in SMALL] + [(4, 4 * n_conv), (1, HEAD_DIM)])
    g_small = dict(zip(SMALL, red[:len(SMALL)]))
    g_conv_full, loss_out = red[len(SMALL)], red[len(SMALL) + 1]
    g_small["conv_w"] = lax.dynamic_slice_in_dim(g_conv_full, chip * n_conv, n_conv, axis=1).reshape(conv_w.shape)

    names = SMALL + ["conv_w"]
    shapes = [wts[n].shape for n in names]
    d, nm, nv = _adamw(_pack_small([wts[n] for n in names]), _pack_small([g_small[n] for n in names]),
                       _pack_small([mom[n] for n in names]), _pack_small([var[n] for n in names]), "adamw_small")
    for n, dd, mm, vv in zip(names, _unpack_small(d, shapes), _unpack_small(nm, shapes), _unpack_small(nv, shapes)):
        grads[n], deltas[n], new_m[n], new_v[n] = g_small[n], dd, mm, vv
    return (loss_out[0, 0], grad_x[None], *[grads[n] for n in ORDER], *[deltas[n] for n in ORDER],
            *[new_m[n] for n in ORDER], *[new_v[n] for n in ORDER])
```

```python
import functools

import jax
import jax.numpy as jnp
from jax import lax
from jax.experimental import pallas as pl
from jax.experimental.pallas import tpu as pltpu

F32 = jnp.float32
BF16 = jnp.bfloat16
HEAD_DIM = 128
CHUNK = 128
INV_BLOCK = 64
SPAN = 128
DILATIONS = (1, 4, 16)
ROPE_THETA = 10000.0
EPS = 1e-6
NEG = -1e30
ADAM_LR, ADAM_B1, ADAM_B2, ADAM_EPS, ADAM_WD, ADAM_STEP = 0.001, 0.9, 0.999, 1e-08, 0.01, 10
VMEM_LIMIT = 48 * 1024 * 1024
MESH = pl.DeviceIdType.MESH

_DN = {"nn": (((1,), (0,)), ((), ())), "nt": (((1,), (1,)), ((), ())), "tn": (((0,), (0,)), ((), ()))}


def _dot(a, b, mode="nn"):
    (ca, cb), _ = _DN[mode]
    if a.ndim == 3:
        dn = (((ca[0] + 1,), (cb[0] + 1,)), ((0,), (0,)))
    else:
        dn = _DN[mode]
    return lax.dot_general(a.astype(BF16), b.astype(BF16), dn, preferred_element_type=F32)


def _rsum(x):
    return jnp.sum(x, axis=-1, keepdims=True)


def _csum(x):
    return jnp.sum(x, axis=-2, keepdims=True)


def _tile(dim, pref, unit=128):
    t = (min(pref, dim) // unit) * unit
    while t >= unit:
        if dim % t == 0:
            return t
        t -= unit
    return dim


def _params(sem):
    return pltpu.CompilerParams(dimension_semantics=sem, vmem_limit_bytes=VMEM_LIMIT)


def _sigmoid(x):
    return 1.0 / (1.0 + jnp.exp(-x))


def _matmul(a, b, mode, name, add=None, out_dtype=F32, a_cols=None, b_cols=None, b_rows=None,
            tiles=(), finish=None, out_dtypes=(), row_sum=False):
    if mode == "tn":
        out_dtype = BF16
    a_off, a_w = a_cols if a_cols else (0, a.shape[1])
    b_off, b_w = b_cols if b_cols else (0, b.shape[1])
    br_off, br_n = b_rows if b_rows else (0, b.shape[0])
    if mode == "nn":
        m, k, n = a.shape[0], a_w, b_w
        assert br_n == k
    elif mode == "nt":
        m, k, n = a.shape[0], a_w, b.shape[0]
        assert b_w == k
    else:
        k, m, n = a.shape[0], a_w, b_w
        assert b.shape[0] == k
    tm, tn = _tile(m, 1024, 128), _tile(n, 1024, 128)
    tk = _tile(k, 2048 if a.dtype == BF16 and b.dtype == BF16 else 1024, 128)
    if mode == "nn":
        assert a_off % tk == 0 and b_off % tn == 0 and br_off % tk == 0
        a_spec = pl.BlockSpec((tm, tk), lambda i, j, kk: (i, kk + a_off // tk))
        b_spec = pl.BlockSpec((tk, tn), lambda i, j, kk: (kk + br_off // tk, j + b_off // tn))
    elif mode == "nt":
        assert a_off % tk == 0 and b_off % tk == 0
        a_spec = pl.BlockSpec((tm, tk), lambda i, j, kk: (i, kk + a_off // tk))
        b_spec = pl.BlockSpec((tn, tk), lambda i, j, kk: (j, kk + b_off // tk))
    else:
        assert a_off % tm == 0 and b_off % tn == 0
        a_spec = pl.BlockSpec((tk, tm), lambda i, j, kk: (kk, i + a_off // tm))
        b_spec = pl.BlockSpec((tk, tn), lambda i, j, kk: (kk, j + b_off // tn))
    nk = k // tk
    if finish is None:
        out_dtypes = [out_dtype]
        if add is None:
            finish = lambda acc, vals: (acc,)
        else:
            tiles = [(add, 0)]
            finish = lambda acc, vals: (acc + vals[0].astype(F32),)
    n_tiles, n_out = len(tiles), len(out_dtypes)

    def body(*refs):
        a_ref, b_ref = refs[:2]
        tile_refs, out_refs = refs[2:2 + n_tiles], refs[2 + n_tiles:2 + n_tiles + n_out]
        acc_ref = refs[-1]
        kk = pl.program_id(2)
        first_tile = (pl.program_id(0) == 0) & (pl.program_id(1) == 0)

        @pl.when(kk == 0)
        def _():
            acc_ref[...] = jnp.zeros_like(acc_ref)

        acc_ref[...] += _dot(a_ref[...], b_ref[...], mode)

        @pl.when(kk == nk - 1)
        def _():
            res = finish(acc_ref[...], [t[...] for t in tile_refs])
            for o_ref, r in zip(out_refs, res):
                o_ref[...] = r.astype(o_ref.dtype)
            if row_sum:
                row_ref = refs[2 + n_tiles + n_out]

                @pl.when(first_tile)
                def _():
                    row_ref[...] = res[n_out]

                @pl.when(jnp.logical_not(first_tile))
                def _():
                    row_ref[...] += res[n_out]

    in_specs = [a_spec, b_spec]
    args = [a, b]
    for arr, off in tiles:
        assert off % tn == 0
        in_specs.append(pl.BlockSpec((tm, tn), lambda i, j, kk, off=off: (i, j + off // tn)))
        args.append(arr)
    out_specs = [pl.BlockSpec((tm, tn), lambda i, j, kk: (i, j))] * n_out
    out_shape = [jax.ShapeDtypeStruct((m, n), dt) for dt in out_dtypes]
    if row_sum:
        out_specs.append(pl.BlockSpec((1, HEAD_DIM), lambda i, j, kk: (0, 0)))
        out_shape.append(jax.ShapeDtypeStruct((1, HEAD_DIM), F32))
    res = pl.pallas_call(
        body, name=name, grid=(m // tm, n // tn, nk),
        in_specs=in_specs, out_specs=out_specs, out_shape=out_shape,
        scratch_shapes=[pltpu.VMEM((tm, tn), F32)],
        compiler_params=_params(("arbitrary",) * 3 if row_sum else ("parallel", "parallel", "arbitrary")),
    )(*args)
    return res[0] if len(res) == 1 else res


def _rmsnorm_fwd(x, w, name, after=None):
    s, d = x.shape
    tr = _tile(s, 512, 8)

    def body(x_ref, w_ref, *rest):
        h_ref = rest[-1]
        xv = x_ref[...]
        rstd = lax.rsqrt(jnp.mean(xv * xv, axis=-1, keepdims=True) + EPS)
        h_ref[...] = (xv * rstd * w_ref[...]).astype(BF16)

    extra = [] if after is None else [after]
    return pl.pallas_call(
        body, name=name, grid=(s // tr,),
        in_specs=[pl.BlockSpec((tr, d), lambda i: (i, 0)), pl.BlockSpec((1, d), lambda i: (0, 0))] + [ANY] * len(extra),
        out_specs=pl.BlockSpec((tr, d), lambda i: (i, 0)),
        out_shape=jax.ShapeDtypeStruct((s, d), BF16),
        compiler_params=_params(("parallel",)),
    )(x, w, *extra)


def _rmsnorm_bwd(dh, x, w, res, name, after=None):
    s, d = x.shape
    tr = _tile(s, 256, 8)

    def body(dh_ref, x_ref, w_ref, res_ref, *rest):
        dx_ref, dx16_ref, dw_ref = rest[-3:]
        xv = x_ref[...]
        rstd = lax.rsqrt(jnp.mean(xv * xv, axis=-1, keepdims=True) + EPS)
        xhat = xv * rstd
        dhv = dh_ref[...]
        gw = dhv * w_ref[...]
        dx = res_ref[...] + rstd * (gw - xhat * jnp.mean(gw * xhat, axis=-1, keepdims=True))
        dx_ref[...] = dx
        dx16_ref[...] = dx.astype(BF16)

        @pl.when(pl.program_id(0) == 0)
        def _():
            dw_ref[...] = jnp.zeros_like(dw_ref)

        dw_ref[...] += jnp.sum(dhv * xhat, axis=0, keepdims=True)

    row = pl.BlockSpec((tr, d), lambda i: (i, 0))
    vec = pl.BlockSpec((1, d), lambda i: (0, 0))
    extra = [] if after is None else [after]
    return pl.pallas_call(
        body, name=name, grid=(s // tr,),
        in_specs=[row, row, vec, row] + [ANY] * len(extra), out_specs=[row, row, vec],
        out_shape=[jax.ShapeDtypeStruct((s, d), F32), jax.ShapeDtypeStruct((s, d), BF16),
                   jax.ShapeDtypeStruct((1, d), F32)],
        compiler_params=_params(("arbitrary",)),
    )(dh, x, w, res, *extra)


def _conv_taps(x, w, rows):
    shifted = [x]
    for sft in (1, 2, 3):
        shifted.append(jnp.where(rows >= sft, pltpu.roll(x, sft, 0), 0.0))
    y = w[3:4, :] * shifted[0] + w[2:3, :] * shifted[1] + w[1:2, :] * shifted[2] + w[0:1, :] * shifted[3]
    return y, shifted


def _delta_pre_fwd(qkvz, conv_w, heads):
    s = qkvz.shape[0]
    nblk = 3 * heads

    def body(x_ref, w_ref, o_ref):
        part = pl.program_id(0) // heads
        rows = lax.broadcasted_iota(jnp.int32, (s, HEAD_DIM), 0)
        y, _ = _conv_taps(x_ref[...], w_ref[...], rows)
        a = y * _sigmoid(y)
        rs = lax.rsqrt(jnp.sum(a * a, axis=-1, keepdims=True) + EPS)
        fac = jnp.where(part == 0, rs * (HEAD_DIM ** -0.5), jnp.where(part == 1, rs, 1.0))
        o_ref[...] = a * fac

    return pl.pallas_call(
        body, name="delta_pre_fwd", grid=(nblk,),
        in_specs=[pl.BlockSpec((s, HEAD_DIM), lambda i: (0, i)), pl.BlockSpec((4, HEAD_DIM), lambda i: (0, i))],
        out_specs=pl.BlockSpec((s, HEAD_DIM), lambda i: (0, i)),
        out_shape=jax.ShapeDtypeStruct((s, 3 * heads * HEAD_DIM), F32),
        compiler_params=_params(("parallel",)),
    )(qkvz, conv_w)


def _delta_pre_bwd(dqkv, qkvz, conv_w, heads):
    s = qkvz.shape[0]
    nblk = 3 * heads

    def body(d_ref, x_ref, w_ref, dx_ref, dw_ref):
        part = pl.program_id(0) // heads
        rows = lax.broadcasted_iota(jnp.int32, (s, HEAD_DIM), 0)
        w = w_ref[...]
        y, shifted = _conv_taps(x_ref[...], w, rows)
        sg = _sigmoid(y)
        a = y * sg
        rs = lax.rsqrt(jnp.sum(a * a, axis=-1, keepdims=True) + EPS)
        unit = a * rs
        dn = d_ref[...]
        scale = jnp.where(part == 0, HEAD_DIM ** -0.5, 1.0)
        da_norm = scale * rs * (dn - unit * jnp.sum(dn * unit, axis=-1, keepdims=True))
        da = jnp.where(part < 2, da_norm, dn)
        dy = da * sg * (1.0 + y * (1.0 - sg))
        dx = w[3:4, :] * dy
        for sft in (1, 2, 3):
            dx = dx + w[3 - sft:4 - sft, :] * jnp.where(rows < s - sft, pltpu.roll(dy, s - sft, 0), 0.0)
        dx_ref[...] = dx.astype(BF16)
        for sft in range(4):
            dw_ref[3 - sft:4 - sft, :] = jnp.sum(dy * shifted[sft], axis=0, keepdims=True)

    col = pl.BlockSpec((s, HEAD_DIM), lambda i: (0, i))
    wsp = pl.BlockSpec((4, HEAD_DIM), lambda i: (0, i))
    return pl.pallas_call(
        body, name="delta_pre_bwd", grid=(nblk,),
        in_specs=[col, col, wsp], out_specs=[col, wsp],
        out_shape=[jax.ShapeDtypeStruct((s, 3 * heads * HEAD_DIM), BF16),
                   jax.ShapeDtypeStruct((4, 3 * heads * HEAD_DIM), F32)],
        compiler_params=_params(("parallel",)),
    )(dqkv, qkvz, conv_w)


def _heads_of(ref, heads):
    return jnp.stack([ref[:, h * HEAD_DIM:(h + 1) * HEAD_DIM] for h in range(heads)])


def _chunk_common(q, k, v, bd, a_log, dt_bias, heads):
    c = CHUNK
    braw = jnp.stack([bd[:, h:h + 1] for h in range(heads)])
    draw = jnp.stack([bd[:, heads + h:heads + h + 1] for h in range(heads)])
    beta = _sigmoid(braw)
    xd = draw + dt_bias
    sp = jnp.maximum(xd, 0.0) + jnp.log1p(jnp.exp(-jnp.abs(xd)))
    g = -jnp.exp(a_log) * sp
    row = lax.broadcasted_iota(jnp.int32, (c, c), 0)
    col = lax.broadcasted_iota(jnp.int32, (c, c), 1)
    sq = (heads, c, c)
    g_b = jnp.broadcast_to(g, sq)
    g_row = _csum(jnp.where(row == col, g_b, 0.0))
    gam_col = _rsum(jnp.where(col <= row, jnp.broadcast_to(g_row, sq), 0.0))
    gam_row = _csum(jnp.where(row <= col, g_b, 0.0))
    causal = row >= col
    dm = jnp.where(causal, jnp.exp(jnp.where(causal, gam_col - gam_row, 0.0)), 0.0)
    kk = _dot(k, k, "nt")
    low = jnp.where(row > col, beta * kk * dm, 0.0)
    assert c in (INV_BLOCK, 2 * INV_BLOCK)
    same = (row // INV_BLOCK) == (col // INV_BLOCK)
    diag = jnp.where(same, low, 0.0)
    t = jnp.where(row == col, 1.0, 0.0) - diag
    pw = diag
    for _ in range((INV_BLOCK - 1).bit_length() - 1):
        pw = _dot(pw, pw)
        t = t + _dot(t, pw)
    if c > INV_BLOCK:
        t = t - _dot(_dot(t, low - diag), t)
    e = jnp.exp(gam_col)
    u = _dot(t, beta * v)
    w = _dot(t, (beta * e) * k)
    qk_raw = _dot(q, k, "nt")
    gl = _csum(g)
    el = jnp.exp(gl - gam_col)
    return dict(beta=beta, xd=xd, g=g, row=row, col=col, dm=dm, kk=kk, t=t, e=e, u=u, w=w,
                qk_raw=qk_raw, qk=qk_raw * dm, gl=gl, el=el, qd=e * q, kd=el * k, cd=jnp.exp(gl))


def _delta_chunk_fwd(qkv, bd, a_log, dt_bias, heads):
    s = qkv.shape[0]
    n = s // CHUNK
    dw = heads * HEAD_DIM
    blk = lambda part: pl.BlockSpec((CHUNK, dw), lambda i: (i, part))

    def body(q_ref, k_ref, v_ref, bd_ref, al_ref, dt_ref, o_ref, st_ref, state):
        @pl.when(pl.program_id(0) == 0)
        def _():
            state[...] = jnp.zeros_like(state)

        cm = _chunk_common(_heads_of(q_ref, heads), _heads_of(k_ref, heads), _heads_of(v_ref, heads), bd_ref[...],
                           al_ref[...], dt_ref[...], heads)
        st = state[...]
        st_ref[0] = st
        vn = cm["u"] - _dot(cm["w"], st)
        o = _dot(cm["qd"], st) + _dot(cm["qk"], vn)
        for h in range(heads):
            o_ref[:, h * HEAD_DIM:(h + 1) * HEAD_DIM] = o[h]
        state[...] = cm["cd"] * st + _dot(cm["kd"], vn, "tn")

    smem = pl.BlockSpec((heads, 1, 1), lambda i: (0, 0, 0))
    return pl.pallas_call(
        body, name="delta_chunk_fwd", grid=(n,),
        in_specs=[blk(0), blk(1), blk(2), pl.BlockSpec((CHUNK, HEAD_DIM), lambda i: (i, 0)), smem, smem],
        out_specs=[pl.BlockSpec((CHUNK, dw), lambda i: (i, 0)),
                   pl.BlockSpec((1, heads, HEAD_DIM, HEAD_DIM), lambda i: (i, 0, 0, 0))],
        out_shape=[jax.ShapeDtypeStruct((s, dw), F32),
                   jax.ShapeDtypeStruct((n, heads, HEAD_DIM, HEAD_DIM), F32)],
        scratch_shapes=[pltpu.VMEM((heads, HEAD_DIM, HEAD_DIM), F32)],
        compiler_params=_params(("arbitrary",)),
    )(qkv, qkv, qkv, bd, a_log, dt_bias)


def _delta_chunk_bwd(do, qkv, bd, states, a_log, dt_bias, heads):
    s = qkv.shape[0]
    n = s // CHUNK
    dw = heads * HEAD_DIM
    c = CHUNK
    blk = lambda part: pl.BlockSpec((CHUNK, dw), lambda i: (n - 1 - i, part))

    def all_heads(q, k, v, dov, st, dsn, bd, a_log, dt_bias):
        cm = _chunk_common(q, k, v, bd, a_log, dt_bias, heads)
        beta, e, dm, row, col = cm["beta"], cm["e"], cm["dm"], cm["row"], cm["col"]
        sq = (heads, c, c)
        vn = cm["u"] - _dot(cm["w"], st)
        dvn = _dot(cm["kd"], dsn)
        dkd = _dot(vn, dsn, "nt")
        dcd = _csum(_rsum(st * dsn))
        ds = cm["cd"] * dsn
        dqd = _dot(dov, st, "nt")
        ds = ds + _dot(cm["qd"], dov, "tn")
        dqk = _dot(dov, vn, "nt")
        dvn = dvn + _dot(cm["qk"], dov, "tn")
        dw_ = -_dot(dvn, st, "nt")
        ds = ds - _dot(cm["w"], dvn, "tn")
        drhs_u = _dot(cm["t"], dvn, "tn")
        drhs_w = _dot(cm["t"], dw_, "tn")
        da = -(_dot(drhs_u, cm["u"], "nt") + _dot(drhs_w, cm["w"], "nt"))
        dl = jnp.where(row > col, da, 0.0)
        dbeta = _rsum(dl * cm["kk"] * dm)
        dkk = dl * beta * dm
        dd = dl * beta * cm["kk"]
        dv = beta * drhs_u
        ek = e * k
        dbeta = dbeta + _rsum(drhs_u * v) + _rsum(drhs_w * ek)
        dk = (beta * e) * drhs_w
        dgam = _rsum(drhs_w * (beta * ek))
        dqkm = dqk * dm
        dq = _dot(dqkm, k)
        dk = dk + _dot(dqkm, q, "tn")
        dd = dd + dqk * cm["qk_raw"]
        dk = dk + _dot(dkk, k) + _dot(dkk, k, "tn")
        dq = dq + e * dqd
        dgam = dgam + _rsum(dqd * cm["qd"])
        dk = dk + cm["el"] * dkd
        r = _rsum(dkd * cm["kd"])
        dgam = dgam - r
        dgl = _csum(r) + dcd * cm["cd"]
        mm = dd * dm
        colsum_c = _rsum(jnp.where(row == col, jnp.broadcast_to(_csum(mm), sq), 0.0))
        dgam = dgam + _rsum(mm) - colsum_c
        ridx = lax.broadcasted_iota(jnp.int32, (c, 1), 0)
        dgam = dgam + jnp.where(ridx == c - 1, dgl, 0.0)
        dgam_row = _csum(jnp.where(row == col, jnp.broadcast_to(dgam, sq), 0.0))
        dg = _rsum(jnp.where(col >= row, jnp.broadcast_to(dgam_row, sq), 0.0))
        d_xd = dg * (-jnp.exp(a_log)) * _sigmoid(cm["xd"])
        d_braw = dbeta * beta * (1.0 - beta)
        d_alog = dg * cm["g"]
        lane = lax.broadcasted_iota(jnp.int32, (c, HEAD_DIM), 1)
        dbd = jnp.zeros((c, HEAD_DIM), F32)
        for h in range(heads):
            dbd = (dbd + jnp.where(lane == h, d_braw[h], 0.0) + jnp.where(lane == h + heads, d_xd[h], 0.0)
                   + jnp.where(lane == h + 2 * heads, d_alog[h], 0.0))
        return dq, dk, dv, ds, dbd

    def body(do_ref, q_ref, k_ref, v_ref, bd_ref, st_ref, al_ref, dt_ref, dqkv_ref, dbd_ref, dstate):
        @pl.when(pl.program_id(0) == 0)
        def _():
            dstate[...] = jnp.zeros_like(dstate)

        dq, dk, dv, ds, dbd = all_heads(_heads_of(q_ref, heads), _heads_of(k_ref, heads), _heads_of(v_ref, heads),
                                        _heads_of(do_ref, heads), st_ref[0], dstate[...], bd_ref[...],
                                        al_ref[...], dt_ref[...])
        for part, val in enumerate((dq, dk, dv)):
            for h in range(heads):
                lo = part * dw + h * HEAD_DIM
                dqkv_ref[:, lo:lo + HEAD_DIM] = val[h]
        dstate[...] = ds
        dbd_ref[...] = dbd

    smem = pl.BlockSpec((heads, 1, 1), lambda i: (0, 0, 0))
    shared = pl.BlockSpec((CHUNK, HEAD_DIM), lambda i: (n - 1 - i, 0))
    wide = pl.BlockSpec((CHUNK, dw), lambda i: (n - 1 - i, 0))
    return pl.pallas_call(
        body, name="delta_chunk_bwd", grid=(n,),
        in_specs=[wide, blk(0), blk(1), blk(2), shared,
                  pl.BlockSpec((1, heads, HEAD_DIM, HEAD_DIM), lambda i: (n - 1 - i, 0, 0, 0)), smem, smem],
        out_specs=[pl.BlockSpec((CHUNK, 3 * dw), lambda i: (n - 1 - i, 0)), shared],
        out_shape=[jax.ShapeDtypeStruct((s, 3 * dw), F32), jax.ShapeDtypeStruct((s, HEAD_DIM), F32)],
        scratch_shapes=[pltpu.VMEM((heads, HEAD_DIM, HEAD_DIM), F32)],
        compiler_params=_params(("arbitrary",)),
    )(do, qkv, qkv, qkv, bd, states, a_log, dt_bias)


def _delta_post_fwd(o, qkvz, w, heads):
    s = o.shape[0]
    tr = _tile(s, 1024, 8)

    def body(o_ref, z_ref, w_ref, out_ref):
        ov, z = o_ref[...], z_ref[...]
        rstd = lax.rsqrt(jnp.mean(ov * ov, axis=-1, keepdims=True) + EPS)
        out_ref[...] = (ov * rstd * w_ref[...] * (z * _sigmoid(z))).astype(BF16)

    return pl.pallas_call(
        body, name="delta_post_fwd", grid=(s // tr, heads),
        in_specs=[pl.BlockSpec((tr, HEAD_DIM), lambda i, h: (i, h)),
                  pl.BlockSpec((tr, HEAD_DIM), lambda i, h: (i, 3 * heads + h)),
                  pl.BlockSpec((1, HEAD_DIM), lambda i, h: (0, 0))],
        out_specs=pl.BlockSpec((tr, HEAD_DIM), lambda i, h: (i, h)),
        out_shape=jax.ShapeDtypeStruct((s, heads * HEAD_DIM), BF16),
        compiler_params=_params(("parallel", "parallel")),
    )(o, qkvz, w)


def _delta_post_bwd(dmix, o, qkvz, w, heads):
    s = o.shape[0]
    tr = _tile(s, 1024, 8)

    def body(d_ref, o_ref, z_ref, w_ref, do_ref, dz_ref, dw_ref):
        d, ov, z, wv = d_ref[...], o_ref[...], z_ref[...], w_ref[...]
        sg = _sigmoid(z)
        rstd = lax.rsqrt(jnp.mean(ov * ov, axis=-1, keepdims=True) + EPS)
        ohat = ov * rstd
        dz_ref[...] = (d * (ohat * wv) * sg * (1.0 + z * (1.0 - sg))).astype(BF16)
        dn = d * (z * sg)
        gw = dn * wv
        do_ref[...] = rstd * (gw - ohat * jnp.mean(gw * ohat, axis=-1, keepdims=True))

        @pl.when((pl.program_id(0) == 0) & (pl.program_id(1) == 0))
        def _():
            dw_ref[...] = jnp.zeros_like(dw_ref)

        dw_ref[...] += jnp.sum(dn * ohat, axis=0, keepdims=True)

    head = pl.BlockSpec((tr, HEAD_DIM), lambda i, h: (i, h))
    vec = pl.BlockSpec((1, HEAD_DIM), lambda i, h: (0, 0))
    dw = heads * HEAD_DIM
    return pl.pallas_call(
        body, name="delta_post_bwd", grid=(s // tr, heads),
        in_specs=[head, head, pl.BlockSpec((tr, HEAD_DIM), lambda i, h: (i, 3 * heads + h)), vec],
        out_specs=[head, head, vec],
        out_shape=[jax.ShapeDtypeStruct((s, dw), F32), jax.ShapeDtypeStruct((s, dw), BF16),
                   jax.ShapeDtypeStruct((1, HEAD_DIM), F32)],
        compiler_params=_params(("arbitrary", "arbitrary")),
    )(dmix, o, qkvz, w)


def _rope_tables(positions, s):
    half = HEAD_DIM // 2
    inv_freq = ROPE_THETA ** (-jnp.arange(half, dtype=F32) / half)
    ang = positions.reshape(s, 1).astype(F32) * inv_freq
    cos, sin = jnp.cos(ang), jnp.sin(ang)
    return jnp.concatenate([cos, cos], axis=-1), jnp.concatenate([-sin, sin], axis=-1)


def _attn_pre_fwd(aqkv, wq, wk, cosf, sinf, heads):
    s = aqkv.shape[0]
    tr = _tile(s, 1024, 8)

    def body(x_ref, wq_ref, wk_ref, c_ref, s_ref, o_ref):
        xv = x_ref[...]
        wv = jnp.where(pl.program_id(1) < heads, wq_ref[...], wk_ref[...])
        y = xv * lax.rsqrt(jnp.mean(xv * xv, axis=-1, keepdims=True) + EPS) * wv
        o_ref[...] = y * c_ref[...] + pltpu.roll(y, HEAD_DIM // 2, 1) * s_ref[...]

    blk = pl.BlockSpec((tr, HEAD_DIM), lambda i, j: (i, j))
    vec = pl.BlockSpec((1, HEAD_DIM), lambda i, j: (0, 0))
    tab = pl.BlockSpec((tr, HEAD_DIM), lambda i, j: (i, 0))
    return pl.pallas_call(
        body, name="attn_pre_fwd", grid=(s // tr, 2 * heads),
        in_specs=[blk, vec, vec, tab, tab], out_specs=blk,
        out_shape=jax.ShapeDtypeStruct((s, 2 * heads * HEAD_DIM), F32),
        compiler_params=_params(("parallel", "parallel")),
    )(aqkv, wq, wk, cosf, sinf)


def _band():
    qi = lax.broadcasted_iota(jnp.int32, (SPAN, 2 * SPAN), 0)
    ki = lax.broadcasted_iota(jnp.int32, (SPAN, 2 * SPAN), 1)
    dist = qi + SPAN - ki
    return (dist >= 0) & (dist <= SPAN), ki >= SPAN


def _sub(g, r, d):
    if d == 1:
        return pl.ds(g * SPAN, SPAN)
    return pl.ds(g * SPAN * d + r, SPAN, stride=d)


def _attn_blocks(s):
    assert s % (SPAN * max(DILATIONS)) == 0
    return [(p_i, d, r, g) for p_i, d in enumerate(DILATIONS) for r in range(d) for g in range(s // (SPAN * d))]


def _attn_fwd(qk, aqkv, w, heads):
    s = qk.shape[0]
    aw = heads * HEAD_DIM

    def body(q_ref, k_ref, v_ref, w_ref, mix_ref, acc_ref, m_ref, l_ref):
        band, own = _band()
        for p_i, d, r, g in _attn_blocks(s):
            rows, prows = _sub(g, r, d), _sub(max(g - 1, 0), r, d)
            mask = band if g > 0 else band & own
            kcat = jnp.concatenate([k_ref[prows, :], k_ref[rows, :]], axis=0)
            vcat = jnp.concatenate([v_ref[prows, :], v_ref[rows, :]], axis=0)
            sc = _dot(q_ref[rows, :], kcat, "nt") * (HEAD_DIM ** -0.5)
            sc = jnp.where(mask, sc, NEG)
            m = jnp.max(sc, axis=-1, keepdims=True)
            if p_i == 0:
                p = jnp.exp(sc - m)
                acc_ref[rows, :] = _dot(p, vcat)
                l_new = jnp.sum(p, axis=-1, keepdims=True)
            else:
                m_old = m_ref[rows, 0:1]
                m = jnp.maximum(m, m_old)
                alpha = jnp.exp(m_old - m)
                p = jnp.exp(sc - m)
                acc_ref[rows, :] = alpha * acc_ref[rows, :] + _dot(p, vcat)
                l_new = alpha * l_ref[rows, 0:1] + jnp.sum(p, axis=-1, keepdims=True)
            m_ref[rows, :] = jnp.broadcast_to(m, (SPAN, HEAD_DIM))
            l_ref[rows, :] = jnp.broadcast_to(l_new, (SPAN, HEAD_DIM))
        den = l_ref[...]
        ob = acc_ref[...] / den
        acc_ref[...] = ob
        m_ref[...] = m_ref[...] + jnp.log(den)
        rstd = lax.rsqrt(jnp.mean(ob * ob, axis=-1, keepdims=True) + EPS)
        mix_ref[...] = (ob * rstd * w_ref[...]).astype(BF16)

    col = lambda off: pl.BlockSpec((s, HEAD_DIM), lambda h: (0, off + h))
    return pl.pallas_call(
        body, name="attn_fwd", grid=(heads,),
        in_specs=[col(0), col(heads), col(2 * heads), pl.BlockSpec((1, HEAD_DIM), lambda h: (0, 0))],
        out_specs=[col(0), col(0), col(0)],
        out_shape=[jax.ShapeDtypeStruct((s, aw), BF16), jax.ShapeDtypeStruct((s, aw), F32),
                   jax.ShapeDtypeStruct((s, aw), F32)],
        scratch_shapes=[pltpu.VMEM((s, HEAD_DIM), F32)],
        compiler_params=_params(("parallel",)),
    )(qk, qk, aqkv, w)


def _attn_merge_bwd(dmix, ob, w, heads):
    s = ob.shape[0]
    tr = _tile(s, 1024, 8)

    def body(d_ref, ob_ref, w_ref, do_ref, dsum_ref, dw_ref):
        d, ov = d_ref[...], ob_ref[...]
        rstd = lax.rsqrt(jnp.mean(ov * ov, axis=-1, keepdims=True) + EPS)
        ohat = ov * rstd
        gw = d * w_ref[...]
        dov = rstd * (gw - ohat * jnp.mean(gw * ohat, axis=-1, keepdims=True))
        do_ref[...] = dov
        dsum_ref[...] = jnp.broadcast_to(jnp.sum(dov * ov, axis=-1, keepdims=True), dov.shape)

        @pl.when((pl.program_id(0) == 0) & (pl.program_id(1) == 0))
        def _():
            dw_ref[...] = jnp.zeros_like(dw_ref)

        dw_ref[...] += jnp.sum(d * ohat, axis=0, keepdims=True)

    blk = pl.BlockSpec((tr, HEAD_DIM), lambda i, h: (i, h))
    vec = pl.BlockSpec((1, HEAD_DIM), lambda i, h: (0, 0))
    aw = heads * HEAD_DIM
    return pl.pallas_call(
        body, name="attn_merge_bwd", grid=(s // tr, heads),
        in_specs=[pl.BlockSpec((tr, HEAD_DIM), lambda i, h: (i, heads + h)), blk, vec],
        out_specs=[blk, blk, vec],
        out_shape=[jax.ShapeDtypeStruct((s, aw), F32), jax.ShapeDtypeStruct((s, aw), F32),
                   jax.ShapeDtypeStruct((1, HEAD_DIM), F32)],
        compiler_params=_params(("arbitrary", "arbitrary")),
    )(dmix, ob, w)


def _attn_bwd(qk, aqkv, do, lse, dsum, heads):
    s = qk.shape[0]
    aw = heads * HEAD_DIM
    scale = HEAD_DIM ** -0.5

    def body(q_ref, k_ref, v_ref, do_ref, l_ref, ds_ref, out_ref):
        band, own = _band()
        dq_ref, dk_ref, dv_ref = out_ref.at[0], out_ref.at[1], out_ref.at[2]
        out_ref[...] = jnp.zeros((3, s, HEAD_DIM), F32)
        for _, d, r, g in _attn_blocks(s):
            rows, prows = _sub(g, r, d), _sub(max(g - 1, 0), r, d)
            mask = band if g > 0 else band & own
            q, dov = q_ref[rows, :], do_ref[rows, :]
            kcat = jnp.concatenate([k_ref[prows, :], k_ref[rows, :]], axis=0)
            vcat = jnp.concatenate([v_ref[prows, :], v_ref[rows, :]], axis=0)
            sc = _dot(q, kcat, "nt") * scale
            p = jnp.where(mask, jnp.exp(sc - l_ref[rows, 0:1]), 0.0)
            dsc = p * (_dot(dov, vcat, "nt") - ds_ref[rows, 0:1]) * scale
            dq_ref[rows, :] += _dot(dsc, kcat)
            dk = _dot(dsc, q, "tn")
            dv = _dot(p, dov, "tn")
            dk_ref[rows, :] += dk[SPAN:]
            dv_ref[rows, :] += dv[SPAN:]
            if g > 0:
                dk_ref[prows, :] += dk[:SPAN]
                dv_ref[prows, :] += dv[:SPAN]

    col = lambda off: pl.BlockSpec((s, HEAD_DIM), lambda h: (0, off + h))
    return pl.pallas_call(
        body, name="attn_bwd", grid=(heads,),
        in_specs=[col(0), col(heads), col(2 * heads), col(0), col(0), col(0)],
        out_specs=pl.BlockSpec((3, s, HEAD_DIM), lambda h: (0, 0, h)),
        out_shape=jax.ShapeDtypeStruct((3, s, aw), F32),
        compiler_params=_params(("parallel",)),
    )(qk, qk, aqkv, do, lse, dsum)


def _attn_pre_bwd(grads, aqkv, wq, wk, cosf, sinf, heads):
    s = aqkv.shape[0]
    tr = _tile(s, 1024, 8)
    nrow = s // tr

    def body(g_ref, x_ref, wq_ref, wk_ref, c_ref, s_ref, dx_ref, dwq_ref, dwk_ref):
        i, j = pl.program_id(0), pl.program_id(1)
        kind = j // heads
        dout = g_ref[0]
        tot_v = dout
        dy = dout * c_ref[...] + pltpu.roll(dout * s_ref[...], HEAD_DIM // 2, 1)
        xv = x_ref[...]
        wv = jnp.where(kind == 0, wq_ref[...], wk_ref[...])
        rstd = lax.rsqrt(jnp.mean(xv * xv, axis=-1, keepdims=True) + EPS)
        xhat = xv * rstd
        gw = dy * wv
        dxn = rstd * (gw - xhat * jnp.mean(gw * xhat, axis=-1, keepdims=True))
        dx_ref[...] = jnp.where(kind == 2, tot_v, dxn).astype(BF16)
        dwc = jnp.sum(dy * xhat, axis=0, keepdims=True)

        @pl.when((i == 0) & (j == 0))
        def _():
            dwq_ref[...] = jnp.zeros_like(dwq_ref)
            dwk_ref[...] = jnp.zeros_like(dwk_ref)

        @pl.when(kind == 0)
        def _():
            dwq_ref[...] += dwc

        @pl.when(kind == 1)
        def _():
            dwk_ref[...] += dwc

    grad = pl.BlockSpec((1, tr, HEAD_DIM), lambda i, j: (j // heads, i, j % heads))
    blk = pl.BlockSpec((tr, HEAD_DIM), lambda i, j: (i, j))
    vec = pl.BlockSpec((1, HEAD_DIM), lambda i, j: (0, 0))
    tab = pl.BlockSpec((tr, HEAD_DIM), lambda i, j: (i, 0))
    return pl.pallas_call(
        body, name="attn_pre_bwd", grid=(nrow, 3 * heads),
        in_specs=[grad, blk, vec, vec, tab, tab], out_specs=[blk, vec, vec],
        out_shape=[jax.ShapeDtypeStruct((s, 3 * heads * HEAD_DIM), BF16),
                   jax.ShapeDtypeStruct((1, HEAD_DIM), F32), jax.ShapeDtypeStruct((1, HEAD_DIM), F32)],
        compiler_params=_params(("arbitrary", "arbitrary")),
    )(grads, aqkv, wq, wk, cosf, sinf)


def _swiglu_fwd(gu, ff):
    s = gu.shape[0]
    tr, tc = _tile(s, 512, 8), _tile(ff, 1024, 128)
    nc = ff // tc

    def body(g_ref, u_ref, o_ref):
        g = g_ref[...].astype(F32)
        o_ref[...] = (g * _sigmoid(g) * u_ref[...].astype(F32)).astype(BF16)

    return pl.pallas_call(
        body, name="swiglu_fwd", grid=(s // tr, nc),
        in_specs=[pl.BlockSpec((tr, tc), lambda i, j: (i, j)), pl.BlockSpec((tr, tc), lambda i, j: (i, j + nc))],
        out_specs=pl.BlockSpec((tr, tc), lambda i, j: (i, j)),
        out_shape=jax.ShapeDtypeStruct((s, ff), BF16),
        compiler_params=_params(("parallel", "parallel")),
    )(gu, gu)


def _colsum(a, name):
    s, d = a.shape
    tr = _tile(s, 1024, 8)

    def body(a_ref, o_ref):
        @pl.when(pl.program_id(0) == 0)
        def _():
            o_ref[...] = jnp.zeros_like(o_ref)

        o_ref[...] += jnp.sum(a_ref[...], axis=0, keepdims=True)

    return pl.pallas_call(
        body, name=name, grid=(s // tr,),
        in_specs=[pl.BlockSpec((tr, d), lambda i: (i, 0))], out_specs=pl.BlockSpec((1, d), lambda i: (0, 0)),
        out_shape=jax.ShapeDtypeStruct((1, d), F32),
        compiler_params=_params(("arbitrary",)),
    )(a)


def _local_step(x, positions, target, small, w_qkvz, w_bd, w_attn, conv_w, later_weights, ffn_grads_ready,
                after=None):
    s, dmod = x.shape
    heads = dmod // (2 * HEAD_DIM)
    dw = heads * HEAD_DIM
    a_log, dt_bias = small["a_log"].reshape(heads, 1, 1), small["dt_bias"].reshape(heads, 1, 1)
    cosf, sinf = _rope_tables(positions, s)

    h1 = _rmsnorm_fwd(x, small["attn_norm_w"], "norm1_fwd", after=after)
    qkvz = _matmul(h1, w_qkvz, "nn", "proj_qkvz")
    bd = _matmul(h1, w_bd, "nn", "proj_bd")
    aqkv = _matmul(h1, w_attn, "nn", "proj_attn")
    dqkv = _delta_pre_fwd(qkvz, conv_w, heads)
    o_d, states = _delta_chunk_fwd(dqkv, bd, a_log, dt_bias, heads)
    mix_a = _delta_post_fwd(o_d, qkvz, small["delta_out_norm_w"], heads)
    qk_rot = _attn_pre_fwd(aqkv, small["q_norm_w"], small["k_norm_w"], cosf, sinf, heads)
    mix_b, ob, lse = _attn_fwd(qk_rot, aqkv, small["attn_out_norm_w"], heads)
    w_out, w_gu, w_down = later_weights((mix_a, mix_b))
    ff = w_down.shape[0]
    x1 = _matmul(mix_a, w_out, "nn", "out_proj_a", add=x, b_rows=(0, dw))
    x1 = _matmul(mix_b, w_out, "nn", "out_proj_b", add=x1, b_rows=(dw, dw))
    h2 = _rmsnorm_fwd(x1, small["ffn_norm_w"], "norm2_fwd")
    gu = _matmul(h2, w_gu, "nn", "ffn_gate_up", out_dtype=BF16)
    act = _swiglu_fwd(gu, ff)

    def loss_head(acc, vals):
        err = acc + vals[0] - vals[1]
        part = 0.5 * jnp.sum(jnp.sum(err * err, axis=-1, keepdims=True) * (1.0 / dmod), axis=0, keepdims=True)
        lane = lax.broadcasted_iota(jnp.int32, (1, HEAD_DIM), 1)
        return err * (1.0 / dmod), err * (1.0 / dmod), jnp.where(lane == 0, part, 0.0)

    dy, dy16, loss_row = _matmul(act, w_down, "nn", "ffn_down", tiles=[(x1, 0), (target, 0)], finish=loss_head,
                                 out_dtypes=[F32, BF16], row_sum=True)

    def swiglu_bwd(acc, vals):
        g, u = vals[0].astype(F32), vals[1].astype(F32)
        sg = _sigmoid(g)
        return acc * u * sg * (1.0 + g * (1.0 - sg)), acc * g * sg

    dgate, dup = _matmul(dy16, w_down, "nt", "ffn_down_dx", tiles=[(gu, 0), (gu, ff)], finish=swiglu_bwd,
                         out_dtypes=[BF16, BF16])
    g_w_down = _matmul(act, dy16, "tn", "ffn_down_dw")
    dh2 = _matmul(dgate, w_gu, "nt", "ffn_gate_dx", b_cols=(0, ff))
    dh2 = _matmul(dup, w_gu, "nt", "ffn_up_dx", b_cols=(ff, ff), add=dh2)
    g_w_gate = _matmul(h2, dgate, "tn", "ffn_gate_dw")
    g_w_up = _matmul(h2, dup, "tn", "ffn_up_dw")
    behind = ffn_grads_ready(g_w_gate, g_w_up, g_w_down)
    dx1, dx1_16, g_ffn_norm = _rmsnorm_bwd(dh2, x1, small["ffn_norm_w"], dy, "norm2_bwd", after=behind)
    dmix = _matmul(dx1_16, w_out, "nt", "out_proj_dx")
    g_w_out_a = _matmul(mix_a, dx1_16, "tn", "out_proj_dw_a")
    g_w_out_b = _matmul(mix_b, dx1_16, "tn", "out_proj_dw_b")
    dob, dsum, g_attn_out_norm = _attn_merge_bwd(dmix, ob, small["attn_out_norm_w"], heads)
    grads = _attn_bwd(qk_rot, aqkv, dob, lse, dsum, heads)
    d_aqkv, g_q_norm, g_k_norm = _attn_pre_bwd(grads, aqkv, small["q_norm_w"], small["k_norm_w"], cosf, sinf, heads)
    do_d, dz, g_delta_out_norm = _delta_post_bwd(dmix, o_d, qkvz, small["delta_out_norm_w"], heads)
    ddqkv, dbd = _delta_chunk_bwd(do_d, dqkv, bd, states, a_log, dt_bias, heads)
    d_qkv_raw, g_conv = _delta_pre_bwd(ddqkv, qkvz, conv_w, heads)
    bd_sums = _colsum(dbd, "bd_colsum")
    dh1 = _matmul(d_qkv_raw, w_qkvz, "nt", "proj_qkv_dx", b_cols=(0, 3 * dw))
    dh1 = _matmul(dz, w_qkvz, "nt", "proj_z_dx", b_cols=(3 * dw, dw), add=dh1)
    dh1 = _matmul(d_aqkv, w_attn, "nt", "proj_attn_dx", add=dh1)
    dh1 = _matmul(dbd, w_bd, "nt", "proj_bd_dx", add=dh1)
    g_w_qkv = _matmul(h1, d_qkv_raw, "tn", "proj_qkv_dw")
    g_w_z = _matmul(h1, dz, "tn", "proj_z_dw")
    g_w_bd = _matmul(h1, dbd, "tn", "proj_bd_dw")
    g_w_attn = _matmul(h1, d_aqkv, "tn", "proj_attn_dw")
    grad_x, _, g_attn_norm = _rmsnorm_bwd(dh1, x, small["attn_norm_w"], dx1, "norm1_bwd")
    small_grads = dict(
        attn_norm_w=g_attn_norm, a_log=bd_sums[:, 2 * heads:3 * heads], dt_bias=bd_sums[:, heads:2 * heads],
        delta_out_norm_w=g_delta_out_norm, q_norm_w=g_q_norm, k_norm_w=g_k_norm,
        attn_out_norm_w=g_attn_out_norm, ffn_norm_w=g_ffn_norm, conv_w=g_conv)
    big_grads = dict(w_qkv=g_w_qkv, w_z=g_w_z, w_bd=g_w_bd, w_attn=g_w_attn, w_out_a=g_w_out_a, w_out_b=g_w_out_b)
    return loss_row, grad_x, small_grads, big_grads


def _adamw(w, g, m, v, name):
    r, c = w.shape
    tr = _tile(r, 256, 8)

    def body(w_ref, g_ref, m_ref, v_ref, d_ref, nm_ref, nv_ref):
        gv = g_ref[...]
        nm = ADAM_B1 * m_ref[...] + (1.0 - ADAM_B1) * gv
        nv = ADAM_B2 * v_ref[...] + (1.0 - ADAM_B2) * (gv * gv)
        m_hat = nm / (1.0 - ADAM_B1 ** ADAM_STEP)
        v_hat = nv / (1.0 - ADAM_B2 ** ADAM_STEP)
        d_ref[...] = -ADAM_LR * (m_hat / (jnp.sqrt(v_hat) + ADAM_EPS) + ADAM_WD * w_ref[...])
        nm_ref[...] = nm
        nv_ref[...] = nv

    blk = pl.BlockSpec((tr, c), lambda i: (i, 0))
    return pl.pallas_call(
        body, name=name, grid=(r // tr,),
        in_specs=[blk] * 4, out_specs=[blk] * 3,
        out_shape=[jax.ShapeDtypeStruct((r, c), F32)] * 3,
        compiler_params=_params(("parallel",)),
    )(w, g, m, v)


def _add_half_bf16(g, b, place, name):
    n, half, c = b.shape
    tr = _tile(half, 512, 16)
    nb = half // tr

    def body(place_ref, g_ref, b_ref, o_ref):
        o_ref[...] = (g_ref[...].astype(F32) + b_ref[...].astype(F32)).astype(BF16)

    blk = pl.BlockSpec((1, tr, c), lambda i, j, p: (i, j, 0))
    return pl.pallas_call(
        body, name=name,
        grid_spec=pltpu.PrefetchScalarGridSpec(
            num_scalar_prefetch=1, grid=(n, nb),
            in_specs=[pl.BlockSpec((1, tr, c), lambda i, j, p: (i, p[0] * nb + j, 0)), blk], out_specs=blk),
        out_shape=jax.ShapeDtypeStruct((n, half, c), BF16),
        compiler_params=_params(("parallel", "parallel")),
    )(place, g, b)


def _sum4_f32(mine, others, place, name):
    _, half, c = mine.shape
    tr = _tile(half, 512, 16)
    nb = half // tr

    def body(place_ref, a_ref, b_ref, o_ref):
        acc = a_ref[0].astype(F32)
        for j in range(3):
            acc = acc + b_ref[j].astype(F32)
        o_ref[...] = acc

    return pl.pallas_call(
        body, name=name,
        grid_spec=pltpu.PrefetchScalarGridSpec(
            num_scalar_prefetch=1, grid=(nb,),
            in_specs=[pl.BlockSpec((1, tr, c), lambda i, p: (p[1], i, 0)),
                      pl.BlockSpec((3, tr, c), lambda i, p: (0, i, 0))],
            out_specs=pl.BlockSpec((tr, c), lambda i, p: (p[0] * nb + i, 0))),
        out_shape=jax.ShapeDtypeStruct((2 * half, c), F32),
        compiler_params=_params(("parallel",)),
    )(place, mine, others)


def _place():
    x, y, c = lax.axis_index("x"), lax.axis_index("y"), lax.axis_index("c")
    other_chips = [(1 - x, y), (x, 1 - y), (1 - x, 1 - y)]
    return x, y, c, (x, y, 1 - c), other_chips


ANY = pl.BlockSpec(memory_space=pl.ANY)


def _remote(k, src, dst, to, send_sems, recv_sems):
    return pltpu.make_async_remote_copy(src_ref=src, dst_ref=dst, send_sem=send_sems.at[k], recv_sem=recv_sems.at[k],
                                        device_id=to, device_id_type=MESH)


def _half(ref, lead, hc):
    if lead is None:
        half = ref.shape[0] // 2
        return ref.at[pl.ds(hc * half, half), :]
    half = ref.shape[1] // 2
    return ref.at[lead, pl.ds(hc * half, half), :]


def _all_gather_weights(slots, whole, name):
    nt, nw = len(slots), len(whole)
    n_ici = 3 * (nt + nw)

    def body(*refs):
        ins, outs = refs[:nt + nw], refs[nt + nw:2 * (nt + nw)]
        send_sems, recv_sems = refs[2 * (nt + nw):]
        x, y, c, sibling, chips = _place()
        me = 2 * x + y
        first = []
        for j, (px, py) in enumerate(chips):
            for t in range(nt):
                first.append(_remote(j * (nt + nw) + t, _half(ins[t], me, c), _half(outs[t], me, c), (px, py, c),
                                     send_sems, recv_sems))
            for t in range(nt, nt + nw):
                first.append(_remote(j * (nt + nw) + t, ins[t].at[me], outs[t].at[me], (px, py, c),
                                     send_sems, recv_sems))
        for cp in first:
            cp.start()
        passed = []
        for j, (px, py) in enumerate(chips):
            for t in range(nt):
                landed = _half(outs[t], 2 * px + py, c)
                _remote(j * (nt + nw) + t, landed, landed, (px, py, c), send_sems, recv_sems).wait_recv()
                fw = _remote(n_ici + j * nt + t, landed, landed, sibling, send_sems, recv_sems)
                fw.start()
                passed.append(fw)
            for t in range(nt, nt + nw):
                landed = outs[t].at[2 * px + py]
                _remote(j * (nt + nw) + t, landed, landed, (px, py, c), send_sems, recv_sems).wait_recv()
        for j, (px, py) in enumerate(chips):
            for t in range(nt):
                other = _half(outs[t], 2 * px + py, 1 - c)
                _remote(n_ici + j * nt + t, other, other, sibling, send_sems, recv_sems).wait_recv()
        for cp in first + passed:
            cp.wait_send()

    arrays = list(slots) + list(whole)
    n_sem = n_ici + 3 * nt
    return pl.pallas_call(
        body, name=name, in_specs=[ANY] * len(arrays), out_specs=[ANY] * len(arrays),
        input_output_aliases={i: i for i in range(len(arrays))},
        out_shape=[jax.ShapeDtypeStruct(a.shape, a.dtype) for a in arrays],
        scratch_shapes=[pltpu.SemaphoreType.DMA((n_sem,)), pltpu.SemaphoreType.DMA((n_sem,))],
    )(*arrays)


def _swap_halves_with_sibling(gs, name):
    nt = len(gs)

    def body(*refs):
        g_refs, o_refs, (send_sems, recv_sems) = refs[:nt], refs[nt:2 * nt], refs[2 * nt:]
        _, _, c, sibling, _ = _place()
        cps = []
        for t in range(nt):
            half = g_refs[t].shape[1] // 2
            cps.append(_remote(t, g_refs[t].at[:, pl.ds((1 - c) * half, half), :], o_refs[t], sibling,
                               send_sems, recv_sems))
        for cp in cps:
            cp.start()
        for cp in cps:
            cp.wait()

    return pl.pallas_call(
        body, name=name, in_specs=[ANY] * nt, out_specs=[ANY] * nt,
        out_shape=[jax.ShapeDtypeStruct((g.shape[0], g.shape[1] // 2, g.shape[2]), g.dtype) for g in gs],
        scratch_shapes=[pltpu.SemaphoreType.DMA((nt,)), pltpu.SemaphoreType.DMA((nt,))],
    )(*gs)


def _scatter_to_chips(ps, name):
    nt = len(ps)

    def body(*refs):
        p_refs, b_refs, (send_sems, recv_sems) = refs[:nt], refs[nt:2 * nt], refs[2 * nt:]
        _, _, c, _, chips = _place()
        cps = [_remote(j * nt + t, p_refs[t].at[2 * px + py], b_refs[t].at[j], (px, py, c), send_sems, recv_sems)
               for j, (px, py) in enumerate(chips) for t in range(nt)]
        for cp in cps:
            cp.start()
        for cp in cps:
            cp.wait()

    return pl.pallas_call(
        body, name=name, in_specs=[ANY] * nt, out_specs=[ANY] * nt,
        out_shape=[jax.ShapeDtypeStruct((3,) + p.shape[1:], p.dtype) for p in ps],
        scratch_shapes=[pltpu.SemaphoreType.DMA((3 * nt,)), pltpu.SemaphoreType.DMA((3 * nt,))],
    )(*ps)


def _join_halves(fs, name):
    nt = len(fs)

    def body(*refs):
        in_refs, out_refs, (send_sems, recv_sems) = refs[:nt], refs[nt:2 * nt], refs[2 * nt:]
        _, _, c, sibling, _ = _place()
        cps = [_remote(t, _half(in_refs[t], None, c), _half(out_refs[t], None, c), sibling, send_sems, recv_sems)
               for t in range(nt)]
        for cp in cps:
            cp.start()
        for t in range(nt):
            theirs = _half(out_refs[t], None, 1 - c)
            _remote(t, theirs, theirs, sibling, send_sems, recv_sems).wait_recv()
        for cp in cps:
            cp.wait_send()

    return pl.pallas_call(
        body, name=name, in_specs=[ANY] * nt, out_specs=[ANY] * nt,
        input_output_aliases={i: i for i in range(nt)},
        out_shape=[jax.ShapeDtypeStruct(f.shape, f.dtype) for f in fs],
        scratch_shapes=[pltpu.SemaphoreType.DMA((nt,)), pltpu.SemaphoreType.DMA((nt,))],
    )(*fs)


HBM = pl.BlockSpec(memory_space=pltpu.HBM)
SEM = pl.BlockSpec(memory_space=pltpu.SEMAPHORE)
EFFECT = pltpu.SideEffectType.DATAFLOW_SIDE_EFFECTING


def _split_start(arrays, after, plan, n_copies, name):
    na = len(arrays)

    def body(*refs):
        ins, send_sems, recv_sems = refs[:na], refs[na + 1], refs[na + 2]
        outs, token = refs[na + 3:2 * na + 3], refs[2 * na + 3]
        for k, (src, dst, to) in enumerate(plan(ins, outs)):
            _remote(k, src, dst, to, send_sems, recv_sems).start()
        token[...] = jnp.zeros_like(token)

    res = pl.pallas_call(
        body, name=name,
        out_shape=(pltpu.SemaphoreType.DMA((n_copies,)), pltpu.SemaphoreType.DMA((n_copies,)),
                   *[pltpu.HBM(a.shape, a.dtype) for a in arrays], jax.ShapeDtypeStruct((8, HEAD_DIM), F32)),
        in_specs=[HBM] * na + [ANY],
        out_specs=(SEM, SEM, *[HBM] * na, pl.BlockSpec(memory_space=pltpu.VMEM)),
        input_output_aliases={i: 2 + i for i in range(na)},
        compiler_params=pltpu.CompilerParams(has_side_effects=EFFECT),
    )(*[pltpu.with_memory_space_constraint(a, pltpu.HBM) for a in arrays], after)
    return res[0], res[1], list(res[2:2 + na]), res[2 + na]


def _split_wait(send_sems, recv_sems, arrays, after, plan, name):
    na = len(arrays)
    after = list(after) if isinstance(after, (list, tuple)) else [after]

    def body(*refs):
        ins, send, recv, outs = refs[:na], refs[na], refs[na + 1], refs[na + 2 + len(after):]
        for k, (src, dst, to) in enumerate(plan(ins, outs)):
            cp = _remote(k, src, dst, to, send, recv)
            cp.wait_send()
            cp.wait_recv()

    res = pl.pallas_call(
        body, name=name, out_shape=tuple(pltpu.HBM(a.shape, a.dtype) for a in arrays),
        in_specs=[HBM] * na + [SEM, SEM] + [ANY] * len(after), out_specs=tuple([HBM] * na),
        input_output_aliases={i: i for i in range(na)},
        compiler_params=pltpu.CompilerParams(has_side_effects=EFFECT),
    )(*arrays, send_sems, recv_sems, *after)
    return list(res)


def _gather_plan(nt):
    def plan(ins, outs):
        x, y, c, _, chips = _place()
        me = 2 * x + y
        return [(_half(ins[t], me, c), _half(outs[t], me, c), (px, py, c)) for px, py in chips for t in range(nt)]
    return plan


def _gather_landed_plan(nt):
    def plan(ins, outs):
        _, _, c, _, chips = _place()
        return [(_half(outs[t], 2 * px + py, c), _half(outs[t], 2 * px + py, c), (px, py, c))
                for px, py in chips for t in range(nt)]
    return plan


def _scatter_plan(nt):
    def plan(ins, outs):
        _, _, c, _, chips = _place()
        return [(ins[t].at[2 * px + py], outs[nt + t].at[j], (px, py, c))
                for j, (px, py) in enumerate(chips) for t in range(nt)]
    return plan


def _pass_halves_to_sibling(slots, name):
    nt = len(slots)

    def body(*refs):
        ins, outs, (send_sems, recv_sems) = refs[:nt], refs[nt:2 * nt], refs[2 * nt:]
        _, _, c, sibling, chips = _place()
        cps = [_remote(j * nt + t, _half(ins[t], 2 * px + py, c), _half(outs[t], 2 * px + py, c), sibling,
                       send_sems, recv_sems)
               for j, (px, py) in enumerate(chips) for t in range(nt)]
        for cp in cps:
            cp.start()
        for j, (px, py) in enumerate(chips):
            for t in range(nt):
                other = _half(outs[t], 2 * px + py, 1 - c)
                _remote(j * nt + t, other, other, sibling, send_sems, recv_sems).wait_recv()
        for cp in cps:
            cp.wait_send()

    return pl.pallas_call(
        body, name=name, in_specs=[ANY] * nt, out_specs=[ANY] * nt,
        input_output_aliases={i: i for i in range(nt)},
        out_shape=[jax.ShapeDtypeStruct(a.shape, a.dtype) for a in slots],
        scratch_shapes=[pltpu.SemaphoreType.DMA((3 * nt,)), pltpu.SemaphoreType.DMA((3 * nt,))],
    )(*slots)


def _all_reduce_small(v):
    r, lanes = v.shape

    def body(v_ref, out_ref, buf, send_sems, recv_sems):
        x, y, c, sibling, chips = _place()

        def slot(px, py, pc):
            return buf.at[4 * px + 2 * py + pc]

        def copy(k, block, to, src=None):
            return pltpu.make_async_remote_copy(src_ref=slot(*block) if src is None else src, dst_ref=slot(*block),
                                                send_sem=send_sems.at[k], recv_sem=recv_sems.at[k],
                                                device_id=to, device_id_type=MESH)

        me = (x, y, c)
        buf[4 * x + 2 * y + c] = v_ref[...]
        first = [copy(0, me, sibling, src=v_ref)]
        first += [copy(1 + j, me, (*chip, c), src=v_ref) for j, chip in enumerate(chips)]
        for cp in first:
            cp.start()
        passed = [copy(4 + j, (*chip, c), sibling) for j, chip in enumerate(chips)]
        for j, chip in enumerate(chips):
            copy(1 + j, (*chip, c), me).wait_recv()
            passed[j].start()
        copy(0, (x, y, 1 - c), me).wait_recv()
        for j, chip in enumerate(chips):
            copy(4 + j, (*chip, 1 - c), me).wait_recv()
        for cp in first + passed:
            cp.wait_send()
        acc = buf[0]
        for k in range(1, 8):
            acc = acc + buf[k]
        out_ref[...] = acc

    vmem = pl.BlockSpec(memory_space=pltpu.VMEM)
    return pl.pallas_call(
        body, name="all_reduce_small", in_specs=[vmem], out_specs=vmem,
        out_shape=jax.ShapeDtypeStruct((r, lanes), F32),
        scratch_shapes=[pltpu.VMEM((8, r, lanes), F32), pltpu.SemaphoreType.DMA((7,)), pltpu.SemaphoreType.DMA((7,))],
    )(v)


def _size(shape):
    n = 1
    for d in shape:
        n *= d
    return n


def _pack_small(parts):
    rows = []
    for p in parts:
        f = p.reshape(-1).astype(F32)
        n = -(-f.shape[0] // HEAD_DIM) * HEAD_DIM
        rows.append(jnp.pad(f, (0, n - f.shape[0])).reshape(-1, HEAD_DIM))
    a = jnp.concatenate(rows, axis=0)
    return jnp.pad(a, ((0, -a.shape[0] % 8), (0, 0)))


def _unpack_small(a, shapes):
    out, row = [], 0
    for shp in shapes:
        nrows = -(-_size(shp) // HEAD_DIM)
        out.append(a[row:row + nrows].reshape(-1)[:_size(shp)].reshape(shp))
        row += nrows
    return out


SMALL = ["attn_norm_w", "a_log", "dt_bias", "delta_out_norm_w", "q_norm_w", "k_norm_w", "attn_out_norm_w", "ffn_norm_w"]
BIG = ["w_in", "w_out", "w_gate_up", "w_down"]
ORDER = ["attn_norm_w", "w_in", "conv_w", "a_log", "dt_bias", "delta_out_norm_w", "q_norm_w", "k_norm_w",
         "attn_out_norm_w", "w_out", "ffn_norm_w", "w_gate_up", "w_down"]


def kernel(x, positions, attn_norm_w, w_in, conv_w, a_log, dt_bias, delta_out_norm_w, q_norm_w, k_norm_w, attn_out_norm_w, w_out, ffn_norm_w, w_gate_up, w_down, loss_target, m_attn_norm_w, m_w_in, m_conv_w, m_a_log, m_dt_bias, m_delta_out_norm_w, m_q_norm_w, m_k_norm_w, m_attn_out_norm_w, m_w_out, m_ffn_norm_w, m_w_gate_up, m_w_down, v_attn_norm_w, v_w_in, v_conv_w, v_a_log, v_dt_bias, v_delta_out_norm_w, v_q_norm_w, v_k_norm_w, v_attn_out_norm_w, v_w_out, v_ffn_norm_w, v_w_gate_up, v_w_down):
    wts = dict(attn_norm_w=attn_norm_w, w_in=w_in, conv_w=conv_w, a_log=a_log, dt_bias=dt_bias,
               delta_out_norm_w=delta_out_norm_w, q_norm_w=q_norm_w, k_norm_w=k_norm_w,
               attn_out_norm_w=attn_out_norm_w, w_out=w_out, ffn_norm_w=ffn_norm_w, w_gate_up=w_gate_up, w_down=w_down)
    mom = dict(attn_norm_w=m_attn_norm_w, w_in=m_w_in, conv_w=m_conv_w, a_log=m_a_log, dt_bias=m_dt_bias,
               delta_out_norm_w=m_delta_out_norm_w, q_norm_w=m_q_norm_w, k_norm_w=m_k_norm_w,
               attn_out_norm_w=m_attn_out_norm_w, w_out=m_w_out, ffn_norm_w=m_ffn_norm_w, w_gate_up=m_w_gate_up,
               w_down=m_w_down)
    var = dict(attn_norm_w=v_attn_norm_w, w_in=v_w_in, conv_w=v_conv_w, a_log=v_a_log, dt_bias=v_dt_bias,
               delta_out_norm_w=v_delta_out_norm_w, q_norm_w=v_q_norm_w, k_norm_w=v_k_norm_w,
               attn_out_norm_w=v_attn_out_norm_w, w_out=v_w_out, ffn_norm_w=v_ffn_norm_w, w_gate_up=v_w_gate_up,
               w_down=v_w_down)
    dmod = x.shape[2]
    heads = dmod // (2 * HEAD_DIM)
    dw = heads * HEAD_DIM
    chip = 2 * lax.axis_index("x") + lax.axis_index("y")
    core = lax.axis_index("c")
    n_in, n_out, n_gu, n_down, n_conv = (w_in.shape[2], w_out.shape[1], w_gate_up.shape[2], w_down.shape[1],
                                         conv_w.shape[2])

    def slots_of(w, dtype):
        shard = w[0].astype(dtype)
        return lax.dynamic_update_index_in_dim(lax.empty((4,) + shard.shape, dtype), shard, chip, axis=0)

    s_in, s_conv = _all_gather_weights([slots_of(w_in, BF16)], [slots_of(conv_w, F32)], "all_gather_w_in")
    later = [slots_of(w_out, BF16), slots_of(w_gate_up, BF16), slots_of(w_down, BF16)]
    w_send, w_recv, later, started = _split_start(later, s_conv, _gather_plan(3), 9, "gather_rest_start")
    by_cols = lambda a: a.transpose(1, 0, 2).reshape(a.shape[1], 4 * a.shape[2])
    w_in_f, conv_f = by_cols(s_in), by_cols(s_conv)
    w_bd = jnp.pad(w_in_f[:, 4 * dw:4 * dw + 2 * heads], ((0, 0), (0, HEAD_DIM - 2 * heads)))
    small = {n: wts[n] for n in SMALL}
    place = jnp.stack([core, chip]).astype(jnp.int32)
    to_slots = lambda a: a.reshape(a.shape[0], 4, a.shape[1] // 4).transpose(1, 0, 2)

    def later_weights(after):
        landed = _split_wait(w_send, w_recv, later, after, _gather_landed_plan(3), "gather_rest_wait")
        s_out, s_gu, s_down = _pass_halves_to_sibling(landed, "gather_rest_pass")
        return s_out.reshape(4 * n_out, dmod), by_cols(s_gu), s_down.reshape(4 * n_down, dmod)

    ffn = {}

    def ffn_grads_ready(g_gate, g_up, g_down):
        gs = [to_slots(jnp.concatenate([g_gate, g_up], axis=1)), g_down.reshape(4, n_down, dmod)]
        from_sibling = _swap_halves_with_sibling(gs, "swap_ffn_grad_halves")
        sums = [_add_half_bf16(g, b, place, "chip_partial_sum_" + n)
                for g, b, n in zip(gs, from_sibling, ["w_gate_up", "w_down"])]
        zones = [lax.empty((3,) + p.shape[1:], BF16) for p in sums]
        ffn["send"], ffn["recv"], ffn["bufs"], token = _split_start(sums + zones, from_sibling[0], _scatter_plan(2), 6,
                                                                    "scatter_ffn_start")
        return token

    loss_row, grad_x, sg, bg = _local_step(
        x[0], positions[0], loss_target[0], small, w_in_f[:, :4 * dw], w_bd, w_in_f[:, 4 * dw + 2 * heads:], conv_f,
        later_weights, ffn_grads_ready, after=started)

    gs = [to_slots(jnp.concatenate([bg["w_qkv"], bg["w_z"], bg["w_bd"][:, :2 * heads], bg["w_attn"]], axis=1)),
          jnp.concatenate([bg["w_out_a"], bg["w_out_b"]], axis=0).reshape(4, n_out, dmod)]
    from_sibling = _swap_halves_with_sibling(gs, "swap_grad_halves")
    sums = [_add_half_bf16(g, b, place, "chip_partial_sum_" + n) for g, b, n in zip(gs, from_sibling, ["w_in", "w_out"])]
    zones = [lax.empty((3,) + p.shape[1:], BF16) for p in sums]
    r_send, r_recv, r_bufs, r_started = _split_start(sums + zones, from_sibling[0], _scatter_plan(2), 6,
                                                     "scatter_rest_start")
    sum_gu, sum_down, got_gu, got_down = _split_wait(ffn["send"], ffn["recv"], ffn["bufs"], (grad_x, r_started),
                                                     _scatter_plan(2), "scatter_ffn_wait")
    g_big = dict(zip(["w_gate_up", "w_down"], _join_halves(
        [_sum4_f32(sum_gu, got_gu, place, "grad_total_w_gate_up"),
         _sum4_f32(sum_down, got_down, place, "grad_total_w_down")], "join_ffn_halves")))
    grads, deltas, new_m, new_v = {}, {}, {}, {}

    def adamw_big(n):
        shp = wts[n].shape
        d, nm, nv = _adamw(wts[n][0], g_big[n], mom[n][0], var[n][0], "adamw_" + n)
        grads[n], deltas[n], new_m[n], new_v[n] = g_big[n].reshape(shp), d.reshape(shp), nm.reshape(shp), nv.reshape(shp)
        return d

    done = [adamw_big("w_gate_up"), adamw_big("w_down")]
    sum_in, sum_out, got_in, got_out = _split_wait(r_send, r_recv, r_bufs, done, _scatter_plan(2), "scatter_rest_wait")
    g_big.update(zip(["w_in", "w_out"], _join_halves(
        [_sum4_f32(sum_in, got_in, place, "grad_total_w_in"),
         _sum4_f32(sum_out, got_out, place, "grad_total_w_out")], "join_rest_halves")))
    adamw_big("w_in")
    adamw_big("w_out")

    reduced = _all_reduce_small(_pack_small([sg[n] for n in SMALL] + [sg["conv_w"], loss_row]))
    red = _unpack_small(reduced, [wts[n].shape for n in SMALL] + [(4, 4 * n_conv), (1, HEAD_DIM)])
    g_small = dict(zip(SMALL, red[:len(SMALL)]))
    g_conv_full, loss_out = red[len(SMALL)], red[len(SMALL) + 1]
    g_small["conv_w"] = lax.dynamic_slice_in_dim(g_conv_full, chip * n_conv, n_conv, axis=1).reshape(conv_w.shape)

    names = SMALL + ["conv_w"]
    shapes = [wts[n].shape for n in names]
    d, nm, nv = _adamw(_pack_small([wts[n] for n in names]), _pack_small([g_small[n] for n in names]),
                       _pack_small([mom[n] for n in names]), _pack_small([var[n] for n in names]), "adamw_small")
    for n, dd, mm, vv in zip(names, _unpack_small(d, shapes), _unpack_small(nm, shapes), _unpack_small(nv, shapes)):
        grads[n], deltas[n], new_m[n], new_v[n] = g_small[n], dd, mm, vv
    return (loss_out[0, 0], grad_x[None], *[grads[n] for n in ORDER], *[deltas[n] for n in ORDER],
            *[new_m[n] for n in ORDER], *[new_v[n] for n in ORDER])
```

```python
import functools

import jax
import jax.numpy as jnp
from jax import lax
from jax.experimental import pallas as pl
from jax.experimental.pallas import tpu as pltpu

F32 = jnp.float32
BF16 = jnp.bfloat16
HEAD_DIM = 128
CHUNK = 128
INV_BLOCK = 64
SPAN = 128
DILATIONS = (1, 4, 16)
ROPE_THETA = 10000.0
EPS = 1e-6
NEG = -1e30
ADAM_LR, ADAM_B1, ADAM_B2, ADAM_EPS, ADAM_WD, ADAM_STEP = 0.001, 0.9, 0.999, 1e-08, 0.01, 10
VMEM_LIMIT = 48 * 1024 * 1024
MESH = pl.DeviceIdType.MESH

_DN = {"nn": (((1,), (0,)), ((), ())), "nt": (((1,), (1,)), ((), ())), "tn": (((0,), (0,)), ((), ()))}


def _dot(a, b, mode="nn"):
    (ca, cb), _ = _DN[mode]
    if a.ndim == 3:
        dn = (((ca[0] + 1,), (cb[0] + 1,)), ((0,), (0,)))
    else:
        dn = _DN[mode]
    return lax.dot_general(a.astype(BF16), b.astype(BF16), dn, preferred_element_type=F32)


def _rsum(x):
    return jnp.sum(x, axis=-1, keepdims=True)


def _csum(x):
    return jnp.sum(x, axis=-2, keepdims=True)


def _tile(dim, pref, unit=128):
    t = (min(pref, dim) // unit) * unit
    while t >= unit:
        if dim % t == 0:
            return t
        t -= unit
    return dim


def _params(sem):
    return pltpu.CompilerParams(dimension_semantics=sem, vmem_limit_bytes=VMEM_LIMIT)


def _sigmoid(x):
    return 1.0 / (1.0 + jnp.exp(-x))


def _matmul(a, b, mode, name, add=None, out_dtype=F32, a_cols=None, b_cols=None, b_rows=None,
            tiles=(), finish=None, out_dtypes=(), row_sum=False, after=None):
    if mode == "tn":
        out_dtype = BF16
    a_off, a_w = a_cols if a_cols else (0, a.shape[1])
    b_off, b_w = b_cols if b_cols else (0, b.shape[1])
    br_off, br_n = b_rows if b_rows else (0, b.shape[0])
    if mode == "nn":
        m, k, n = a.shape[0], a_w, b_w
        assert br_n == k
    elif mode == "nt":
        m, k, n = a.shape[0], a_w, b.shape[0]
        assert b_w == k
    else:
        k, m, n = a.shape[0], a_w, b_w
        assert b.shape[0] == k
    tm, tn = _tile(m, 1024, 128), _tile(n, 1024, 128)
    tk = _tile(k, 2048 if a.dtype == BF16 and b.dtype == BF16 else 1024, 128)
    if mode == "nn":
        assert a_off % tk == 0 and b_off % tn == 0 and br_off % tk == 0
        a_spec = pl.BlockSpec((tm, tk), lambda i, j, kk: (i, kk + a_off // tk))
        b_spec = pl.BlockSpec((tk, tn), lambda i, j, kk: (kk + br_off // tk, j + b_off // tn))
    elif mode == "nt":
        assert a_off % tk == 0 and b_off % tk == 0
        a_spec = pl.BlockSpec((tm, tk), lambda i, j, kk: (i, kk + a_off // tk))
        b_spec = pl.BlockSpec((tn, tk), lambda i, j, kk: (j, kk + b_off // tk))
    else:
        assert a_off % tm == 0 and b_off % tn == 0
        a_spec = pl.BlockSpec((tk, tm), lambda i, j, kk: (kk, i + a_off // tm))
        b_spec = pl.BlockSpec((tk, tn), lambda i, j, kk: (kk, j + b_off // tn))
    nk = k // tk
    if finish is None:
        out_dtypes = [out_dtype]
        if add is None:
            finish = lambda acc, vals: (acc,)
        else:
            tiles = [(add, 0)]
            finish = lambda acc, vals: (acc + vals[0].astype(F32),)
    n_tiles, n_out = len(tiles), len(out_dtypes)
    extra = [] if after is None else [after]
    first_out = 2 + n_tiles + len(extra)

    def body(*refs):
        a_ref, b_ref = refs[:2]
        tile_refs, out_refs = refs[2:2 + n_tiles], refs[first_out:first_out + n_out]
        acc_ref = refs[-1]
        kk = pl.program_id(2)
        first_tile = (pl.program_id(0) == 0) & (pl.program_id(1) == 0)

        @pl.when(kk == 0)
        def _():
            acc_ref[...] = jnp.zeros_like(acc_ref)

        acc_ref[...] += _dot(a_ref[...], b_ref[...], mode)

        @pl.when(kk == nk - 1)
        def _():
            res = finish(acc_ref[...], [t[...] for t in tile_refs])
            for o_ref, r in zip(out_refs, res):
                o_ref[...] = r.astype(o_ref.dtype)
            if row_sum:
                row_ref = refs[first_out + n_out]

                @pl.when(first_tile)
                def _():
                    row_ref[...] = res[n_out]

                @pl.when(jnp.logical_not(first_tile))
                def _():
                    row_ref[...] += res[n_out]

    in_specs = [a_spec, b_spec]
    args = [a, b]
    for arr, off in tiles:
        assert off % tn == 0
        in_specs.append(pl.BlockSpec((tm, tn), lambda i, j, kk, off=off: (i, j + off // tn)))
        args.append(arr)
    in_specs += [ANY] * len(extra)
    args += extra
    out_specs = [pl.BlockSpec((tm, tn), lambda i, j, kk: (i, j))] * n_out
    out_shape = [jax.ShapeDtypeStruct((m, n), dt) for dt in out_dtypes]
    if row_sum:
        out_specs.append(pl.BlockSpec((1, HEAD_DIM), lambda i, j, kk: (0, 0)))
        out_shape.append(jax.ShapeDtypeStruct((1, HEAD_DIM), F32))
    res = pl.pallas_call(
        body, name=name, grid=(m // tm, n // tn, nk),
        in_specs=in_specs, out_specs=out_specs, out_shape=out_shape,
        scratch_shapes=[pltpu.VMEM((tm, tn), F32)],
        compiler_params=_params(("arbitrary",) * 3 if row_sum else ("parallel", "parallel", "arbitrary")),
    )(*args)
    return res[0] if len(res) == 1 else res


def _rmsnorm_fwd(x, w, name, after=None):
    s, d = x.shape
    tr = _tile(s, 512, 8)

    def body(x_ref, w_ref, *rest):
        h_ref = rest[-1]
        xv = x_ref[...]
        rstd = lax.rsqrt(jnp.mean(xv * xv, axis=-1, keepdims=True) + EPS)
        h_ref[...] = (xv * rstd * w_ref[...]).astype(BF16)

    extra = [] if after is None else [after]
    return pl.pallas_call(
        body, name=name, grid=(s // tr,),
        in_specs=[pl.BlockSpec((tr, d), lambda i: (i, 0)), pl.BlockSpec((1, d), lambda i: (0, 0))] + [ANY] * len(extra),
        out_specs=pl.BlockSpec((tr, d), lambda i: (i, 0)),
        out_shape=jax.ShapeDtypeStruct((s, d), BF16),
        compiler_params=_params(("parallel",)),
    )(x, w, *extra)


def _rmsnorm_bwd(dh, x, w, res, name, after=None):
    s, d = x.shape
    tr = _tile(s, 256, 8)

    def body(dh_ref, x_ref, w_ref, res_ref, *rest):
        dx_ref, dx16_ref, dw_ref = rest[-3:]
        xv = x_ref[...]
        rstd = lax.rsqrt(jnp.mean(xv * xv, axis=-1, keepdims=True) + EPS)
        xhat = xv * rstd
        dhv = dh_ref[...]
        gw = dhv * w_ref[...]
        dx = res_ref[...] + rstd * (gw - xhat * jnp.mean(gw * xhat, axis=-1, keepdims=True))
        dx_ref[...] = dx
        dx16_ref[...] = dx.astype(BF16)

        @pl.when(pl.program_id(0) == 0)
        def _():
            dw_ref[...] = jnp.zeros_like(dw_ref)

        dw_ref[...] += jnp.sum(dhv * xhat, axis=0, keepdims=True)

    row = pl.BlockSpec((tr, d), lambda i: (i, 0))
    vec = pl.BlockSpec((1, d), lambda i: (0, 0))
    extra = [] if after is None else [after]
    return pl.pallas_call(
        body, name=name, grid=(s // tr,),
        in_specs=[row, row, vec, row] + [ANY] * len(extra), out_specs=[row, row, vec],
        out_shape=[jax.ShapeDtypeStruct((s, d), F32), jax.ShapeDtypeStruct((s, d), BF16),
                   jax.ShapeDtypeStruct((1, d), F32)],
        compiler_params=_params(("arbitrary",)),
    )(dh, x, w, res, *extra)


def _conv_taps(x, w, rows):
    shifted = [x]
    for sft in (1, 2, 3):
        shifted.append(jnp.where(rows >= sft, pltpu.roll(x, sft, 0), 0.0))
    y = w[3:4, :] * shifted[0] + w[2:3, :] * shifted[1] + w[1:2, :] * shifted[2] + w[0:1, :] * shifted[3]
    return y, shifted


def _delta_pre_fwd(qkvz, conv_w, heads):
    s = qkvz.shape[0]
    nblk = 3 * heads

    def body(x_ref, w_ref, o_ref):
        part = pl.program_id(0) // heads
        rows = lax.broadcasted_iota(jnp.int32, (s, HEAD_DIM), 0)
        y, _ = _conv_taps(x_ref[...], w_ref[...], rows)
        a = y * _sigmoid(y)
        rs = lax.rsqrt(jnp.sum(a * a, axis=-1, keepdims=True) + EPS)
        fac = jnp.where(part == 0, rs * (HEAD_DIM ** -0.5), jnp.where(part == 1, rs, 1.0))
        o_ref[...] = a * fac

    return pl.pallas_call(
        body, name="delta_pre_fwd", grid=(nblk,),
        in_specs=[pl.BlockSpec((s, HEAD_DIM), lambda i: (0, i)), pl.BlockSpec((4, HEAD_DIM), lambda i: (0, i))],
        out_specs=pl.BlockSpec((s, HEAD_DIM), lambda i: (0, i)),
        out_shape=jax.ShapeDtypeStruct((s, 3 * heads * HEAD_DIM), F32),
        compiler_params=_params(("parallel",)),
    )(qkvz, conv_w)


def _delta_pre_bwd(dqkv, qkvz, conv_w, heads):
    s = qkvz.shape[0]
    nblk = 3 * heads

    def body(d_ref, x_ref, w_ref, dx_ref, dw_ref):
        part = pl.program_id(0) // heads
        rows = lax.broadcasted_iota(jnp.int32, (s, HEAD_DIM), 0)
        w = w_ref[...]
        y, shifted = _conv_taps(x_ref[...], w, rows)
        sg = _sigmoid(y)
        a = y * sg
        rs = lax.rsqrt(jnp.sum(a * a, axis=-1, keepdims=True) + EPS)
        unit = a * rs
        dn = d_ref[...]
        scale = jnp.where(part == 0, HEAD_DIM ** -0.5, 1.0)
        da_norm = scale * rs * (dn - unit * jnp.sum(dn * unit, axis=-1, keepdims=True))
        da = jnp.where(part < 2, da_norm, dn)
        dy = da * sg * (1.0 + y * (1.0 - sg))
        dx = w[3:4, :] * dy
        for sft in (1, 2, 3):
            dx = dx + w[3 - sft:4 - sft, :] * jnp.where(rows < s - sft, pltpu.roll(dy, s - sft, 0), 0.0)
        dx_ref[...] = dx.astype(BF16)
        for sft in range(4):
            dw_ref[3 - sft:4 - sft, :] = jnp.sum(dy * shifted[sft], axis=0, keepdims=True)

    col = pl.BlockSpec((s, HEAD_DIM), lambda i: (0, i))
    wsp = pl.BlockSpec((4, HEAD_DIM), lambda i: (0, i))
    return pl.pallas_call(
        body, name="delta_pre_bwd", grid=(nblk,),
        in_specs=[col, col, wsp], out_specs=[col, wsp],
        out_shape=[jax.ShapeDtypeStruct((s, 3 * heads * HEAD_DIM), BF16),
                   jax.ShapeDtypeStruct((4, 3 * heads * HEAD_DIM), F32)],
        compiler_params=_params(("parallel",)),
    )(dqkv, qkvz, conv_w)


def _heads_of(ref, heads):
    return jnp.stack([ref[:, h * HEAD_DIM:(h + 1) * HEAD_DIM] for h in range(heads)])


def _chunk_common(q, k, v, bd, a_log, dt_bias, heads):
    c = CHUNK
    braw = jnp.stack([bd[:, h:h + 1] for h in range(heads)])
    draw = jnp.stack([bd[:, heads + h:heads + h + 1] for h in range(heads)])
    beta = _sigmoid(braw)
    xd = draw + dt_bias
    sp = jnp.maximum(xd, 0.0) + jnp.log1p(jnp.exp(-jnp.abs(xd)))
    g = -jnp.exp(a_log) * sp
    row = lax.broadcasted_iota(jnp.int32, (c, c), 0)
    col = lax.broadcasted_iota(jnp.int32, (c, c), 1)
    sq = (heads, c, c)
    g_b = jnp.broadcast_to(g, sq)
    g_row = _csum(jnp.where(row == col, g_b, 0.0))
    gam_col = _rsum(jnp.where(col <= row, jnp.broadcast_to(g_row, sq), 0.0))
    gam_row = _csum(jnp.where(row <= col, g_b, 0.0))
    causal = row >= col
    dm = jnp.where(causal, jnp.exp(jnp.where(causal, gam_col - gam_row, 0.0)), 0.0)
    kk = _dot(k, k, "nt")
    low = jnp.where(row > col, beta * kk * dm, 0.0)
    assert c in (INV_BLOCK, 2 * INV_BLOCK)
    same = (row // INV_BLOCK) == (col // INV_BLOCK)
    diag = jnp.where(same, low, 0.0)
    t = jnp.where(row == col, 1.0, 0.0) - diag
    pw = diag
    for _ in range((INV_BLOCK - 1).bit_length() - 1):
        pw = _dot(pw, pw)
        t = t + _dot(t, pw)
    if c > INV_BLOCK:
        t = t - _dot(_dot(t, low - diag), t)
    e = jnp.exp(gam_col)
    u = _dot(t, beta * v)
    w = _dot(t, (beta * e) * k)
    qk_raw = _dot(q, k, "nt")
    gl = _csum(g)
    el = jnp.exp(gl - gam_col)
    return dict(beta=beta, xd=xd, g=g, row=row, col=col, dm=dm, kk=kk, t=t, e=e, u=u, w=w,
                qk_raw=qk_raw, qk=qk_raw * dm, gl=gl, el=el, qd=e * q, kd=el * k, cd=jnp.exp(gl))


def _delta_chunk_fwd(qkv, bd, a_log, dt_bias, heads):
    s = qkv.shape[0]
    n = s // CHUNK
    dw = heads * HEAD_DIM
    blk = lambda part: pl.BlockSpec((CHUNK, dw), lambda i: (i, part))

    def body(q_ref, k_ref, v_ref, bd_ref, al_ref, dt_ref, o_ref, st_ref, state):
        @pl.when(pl.program_id(0) == 0)
        def _():
            state[...] = jnp.zeros_like(state)

        cm = _chunk_common(_heads_of(q_ref, heads), _heads_of(k_ref, heads), _heads_of(v_ref, heads), bd_ref[...],
                           al_ref[...], dt_ref[...], heads)
        st = state[...]
        st_ref[0] = st
        vn = cm["u"] - _dot(cm["w"], st)
        o = _dot(cm["qd"], st) + _dot(cm["qk"], vn)
        for h in range(heads):
            o_ref[:, h * HEAD_DIM:(h + 1) * HEAD_DIM] = o[h]
        state[...] = cm["cd"] * st + _dot(cm["kd"], vn, "tn")

    smem = pl.BlockSpec((heads, 1, 1), lambda i: (0, 0, 0))
    return pl.pallas_call(
        body, name="delta_chunk_fwd", grid=(n,),
        in_specs=[blk(0), blk(1), blk(2), pl.BlockSpec((CHUNK, HEAD_DIM), lambda i: (i, 0)), smem, smem],
        out_specs=[pl.BlockSpec((CHUNK, dw), lambda i: (i, 0)),
                   pl.BlockSpec((1, heads, HEAD_DIM, HEAD_DIM), lambda i: (i, 0, 0, 0))],
        out_shape=[jax.ShapeDtypeStruct((s, dw), F32),
                   jax.ShapeDtypeStruct((n, heads, HEAD_DIM, HEAD_DIM), F32)],
        scratch_shapes=[pltpu.VMEM((heads, HEAD_DIM, HEAD_DIM), F32)],
        compiler_params=_params(("arbitrary",)),
    )(qkv, qkv, qkv, bd, a_log, dt_bias)


def _delta_chunk_bwd(do, qkv, bd, states, a_log, dt_bias, heads):
    s = qkv.shape[0]
    n = s // CHUNK
    dw = heads * HEAD_DIM
    c = CHUNK
    blk = lambda part: pl.BlockSpec((CHUNK, dw), lambda i: (n - 1 - i, part))

    def all_heads(q, k, v, dov, st, dsn, bd, a_log, dt_bias):
        cm = _chunk_common(q, k, v, bd, a_log, dt_bias, heads)
        beta, e, dm, row, col = cm["beta"], cm["e"], cm["dm"], cm["row"], cm["col"]
        sq = (heads, c, c)
        vn = cm["u"] - _dot(cm["w"], st)
        dvn = _dot(cm["kd"], dsn)
        dkd = _dot(vn, dsn, "nt")
        dcd = _csum(_rsum(st * dsn))
        ds = cm["cd"] * dsn
        dqd = _dot(dov, st, "nt")
        ds = ds + _dot(cm["qd"], dov, "tn")
        dqk = _dot(dov, vn, "nt")
        dvn = dvn + _dot(cm["qk"], dov, "tn")
        dw_ = -_dot(dvn, st, "nt")
        ds = ds - _dot(cm["w"], dvn, "tn")
        drhs_u = _dot(cm["t"], dvn, "tn")
        drhs_w = _dot(cm["t"], dw_, "tn")
        da = -(_dot(drhs_u, cm["u"], "nt") + _dot(drhs_w, cm["w"], "nt"))
        dl = jnp.where(row > col, da, 0.0)
        dbeta = _rsum(dl * cm["kk"] * dm)
        dkk = dl * beta * dm
        dd = dl * beta * cm["kk"]
        dv = beta * drhs_u
        ek = e * k
        dbeta = dbeta + _rsum(drhs_u * v) + _rsum(drhs_w * ek)
        dk = (beta * e) * drhs_w
        dgam = _rsum(drhs_w * (beta * ek))
        dqkm = dqk * dm
        dq = _dot(dqkm, k)
        dk = dk + _dot(dqkm, q, "tn")
        dd = dd + dqk * cm["qk_raw"]
        dk = dk + _dot(dkk, k) + _dot(dkk, k, "tn")
        dq = dq + e * dqd
        dgam = dgam + _rsum(dqd * cm["qd"])
        dk = dk + cm["el"] * dkd
        r = _rsum(dkd * cm["kd"])
        dgam = dgam - r
        dgl = _csum(r) + dcd * cm["cd"]
        mm = dd * dm
        colsum_c = _rsum(jnp.where(row == col, jnp.broadcast_to(_csum(mm), sq), 0.0))
        dgam = dgam + _rsum(mm) - colsum_c
        ridx = lax.broadcasted_iota(jnp.int32, (c, 1), 0)
        dgam = dgam + jnp.where(ridx == c - 1, dgl, 0.0)
        dgam_row = _csum(jnp.where(row == col, jnp.broadcast_to(dgam, sq), 0.0))
        dg = _rsum(jnp.where(col >= row, jnp.broadcast_to(dgam_row, sq), 0.0))
        d_xd = dg * (-jnp.exp(a_log)) * _sigmoid(cm["xd"])
        d_braw = dbeta * beta * (1.0 - beta)
        d_alog = dg * cm["g"]
        lane = lax.broadcasted_iota(jnp.int32, (c, HEAD_DIM), 1)
        dbd = jnp.zeros((c, HEAD_DIM), F32)
        for h in range(heads):
            dbd = (dbd + jnp.where(lane == h, d_braw[h], 0.0) + jnp.where(lane == h + heads, d_xd[h], 0.0)
                   + jnp.where(lane == h + 2 * heads, d_alog[h], 0.0))
        return dq, dk, dv, ds, dbd

    def body(do_ref, q_ref, k_ref, v_ref, bd_ref, st_ref, al_ref, dt_ref, dqkv_ref, dbd_ref, dstate):
        @pl.when(pl.program_id(0) == 0)
        def _():
            dstate[...] = jnp.zeros_like(dstate)

        dq, dk, dv, ds, dbd = all_heads(_heads_of(q_ref, heads), _heads_of(k_ref, heads), _heads_of(v_ref, heads),
                                        _heads_of(do_ref, heads), st_ref[0], dstate[...], bd_ref[...],
                                        al_ref[...], dt_ref[...])
        for part, val in enumerate((dq, dk, dv)):
            for h in range(heads):
                lo = part * dw + h * HEAD_DIM
                dqkv_ref[:, lo:lo + HEAD_DIM] = val[h]
        dstate[...] = ds
        dbd_ref[...] = dbd

    smem = pl.BlockSpec((heads, 1, 1), lambda i: (0, 0, 0))
    shared = pl.BlockSpec((CHUNK, HEAD_DIM), lambda i: (n - 1 - i, 0))
    wide = pl.BlockSpec((CHUNK, dw), lambda i: (n - 1 - i, 0))
    return pl.pallas_call(
        body, name="delta_chunk_bwd", grid=(n,),
        in_specs=[wide, blk(0), blk(1), blk(2), shared,
                  pl.BlockSpec((1, heads, HEAD_DIM, HEAD_DIM), lambda i: (n - 1 - i, 0, 0, 0)), smem, smem],
        out_specs=[pl.BlockSpec((CHUNK, 3 * dw), lambda i: (n - 1 - i, 0)), shared],
        out_shape=[jax.ShapeDtypeStruct((s, 3 * dw), F32), jax.ShapeDtypeStruct((s, HEAD_DIM), F32)],
        scratch_shapes=[pltpu.VMEM((heads, HEAD_DIM, HEAD_DIM), F32)],
        compiler_params=_params(("arbitrary",)),
    )(do, qkv, qkv, qkv, bd, states, a_log, dt_bias)


def _delta_post_fwd(o, qkvz, w, heads):
    s = o.shape[0]
    tr = _tile(s, 1024, 8)

    def body(o_ref, z_ref, w_ref, out_ref):
        ov, z = o_ref[...], z_ref[...]
        rstd = lax.rsqrt(jnp.mean(ov * ov, axis=-1, keepdims=True) + EPS)
        out_ref[...] = (ov * rstd * w_ref[...] * (z * _sigmoid(z))).astype(BF16)

    return pl.pallas_call(
        body, name="delta_post_fwd", grid=(s // tr, heads),
        in_specs=[pl.BlockSpec((tr, HEAD_DIM), lambda i, h: (i, h)),
                  pl.BlockSpec((tr, HEAD_DIM), lambda i, h: (i, 3 * heads + h)),
                  pl.BlockSpec((1, HEAD_DIM), lambda i, h: (0, 0))],
        out_specs=pl.BlockSpec((tr, HEAD_DIM), lambda i, h: (i, h)),
        out_shape=jax.ShapeDtypeStruct((s, heads * HEAD_DIM), BF16),
        compiler_params=_params(("parallel", "parallel")),
    )(o, qkvz, w)


def _delta_post_bwd(dmix, o, qkvz, w, heads):
    s = o.shape[0]
    tr = _tile(s, 1024, 8)

    def body(d_ref, o_ref, z_ref, w_ref, do_ref, dz_ref, dw_ref):
        d, ov, z, wv = d_ref[...], o_ref[...], z_ref[...], w_ref[...]
        sg = _sigmoid(z)
        rstd = lax.rsqrt(jnp.mean(ov * ov, axis=-1, keepdims=True) + EPS)
        ohat = ov * rstd
        dz_ref[...] = (d * (ohat * wv) * sg * (1.0 + z * (1.0 - sg))).astype(BF16)
        dn = d * (z * sg)
        gw = dn * wv
        do_ref[...] = rstd * (gw - ohat * jnp.mean(gw * ohat, axis=-1, keepdims=True))

        @pl.when((pl.program_id(0) == 0) & (pl.program_id(1) == 0))
        def _():
            dw_ref[...] = jnp.zeros_like(dw_ref)

        dw_ref[...] += jnp.sum(dn * ohat, axis=0, keepdims=True)

    head = pl.BlockSpec((tr, HEAD_DIM), lambda i, h: (i, h))
    vec = pl.BlockSpec((1, HEAD_DIM), lambda i, h: (0, 0))
    dw = heads * HEAD_DIM
    return pl.pallas_call(
        body, name="delta_post_bwd", grid=(s // tr, heads),
        in_specs=[head, head, pl.BlockSpec((tr, HEAD_DIM), lambda i, h: (i, 3 * heads + h)), vec],
        out_specs=[head, head, vec],
        out_shape=[jax.ShapeDtypeStruct((s, dw), F32), jax.ShapeDtypeStruct((s, dw), BF16),
                   jax.ShapeDtypeStruct((1, HEAD_DIM), F32)],
        compiler_params=_params(("arbitrary", "arbitrary")),
    )(dmix, o, qkvz, w)


def _rope_tables(positions, s):
    half = HEAD_DIM // 2
    inv_freq = ROPE_THETA ** (-jnp.arange(half, dtype=F32) / half)
    ang = positions.reshape(s, 1).astype(F32) * inv_freq
    cos, sin = jnp.cos(ang), jnp.sin(ang)
    return jnp.concatenate([cos, cos], axis=-1), jnp.concatenate([-sin, sin], axis=-1)


def _attn_pre_fwd(aqkv, wq, wk, cosf, sinf, heads):
    s = aqkv.shape[0]
    tr = _tile(s, 1024, 8)

    def body(x_ref, wq_ref, wk_ref, c_ref, s_ref, o_ref):
        xv = x_ref[...]
        wv = jnp.where(pl.program_id(1) < heads, wq_ref[...], wk_ref[...])
        y = xv * lax.rsqrt(jnp.mean(xv * xv, axis=-1, keepdims=True) + EPS) * wv
        o_ref[...] = y * c_ref[...] + pltpu.roll(y, HEAD_DIM // 2, 1) * s_ref[...]

    blk = pl.BlockSpec((tr, HEAD_DIM), lambda i, j: (i, j))
    vec = pl.BlockSpec((1, HEAD_DIM), lambda i, j: (0, 0))
    tab = pl.BlockSpec((tr, HEAD_DIM), lambda i, j: (i, 0))
    return pl.pallas_call(
        body, name="attn_pre_fwd", grid=(s // tr, 2 * heads),
        in_specs=[blk, vec, vec, tab, tab], out_specs=blk,
        out_shape=jax.ShapeDtypeStruct((s, 2 * heads * HEAD_DIM), F32),
        compiler_params=_params(("parallel", "parallel")),
    )(aqkv, wq, wk, cosf, sinf)


def _band():
    qi = lax.broadcasted_iota(jnp.int32, (SPAN, 2 * SPAN), 0)
    ki = lax.broadcasted_iota(jnp.int32, (SPAN, 2 * SPAN), 1)
    dist = qi + SPAN - ki
    return (dist >= 0) & (dist <= SPAN), ki >= SPAN


def _sub(g, r, d):
    if d == 1:
        return pl.ds(g * SPAN, SPAN)
    return pl.ds(g * SPAN * d + r, SPAN, stride=d)


def _attn_blocks(s):
    assert s % (SPAN * max(DILATIONS)) == 0
    return [(p_i, d, r, g) for p_i, d in enumerate(DILATIONS) for r in range(d) for g in range(s // (SPAN * d))]


def _attn_fwd(qk, aqkv, w, heads):
    s = qk.shape[0]
    aw = heads * HEAD_DIM

    def body(q_ref, k_ref, v_ref, w_ref, mix_ref, acc_ref, m_ref, l_ref):
        band, own = _band()
        for p_i, d, r, g in _attn_blocks(s):
            rows, prows = _sub(g, r, d), _sub(max(g - 1, 0), r, d)
            mask = band if g > 0 else band & own
            kcat = jnp.concatenate([k_ref[prows, :], k_ref[rows, :]], axis=0)
            vcat = jnp.concatenate([v_ref[prows, :], v_ref[rows, :]], axis=0)
            sc = _dot(q_ref[rows, :], kcat, "nt") * (HEAD_DIM ** -0.5)
            sc = jnp.where(mask, sc, NEG)
            m = jnp.max(sc, axis=-1, keepdims=True)
            if p_i == 0:
                p = jnp.exp(sc - m)
                acc_ref[rows, :] = _dot(p, vcat)
                l_new = jnp.sum(p, axis=-1, keepdims=True)
            else:
                m_old = m_ref[rows, 0:1]
                m = jnp.maximum(m, m_old)
                alpha = jnp.exp(m_old - m)
                p = jnp.exp(sc - m)
                acc_ref[rows, :] = alpha * acc_ref[rows, :] + _dot(p, vcat)
                l_new = alpha * l_ref[rows, 0:1] + jnp.sum(p, axis=-1, keepdims=True)
            m_ref[rows, :] = jnp.broadcast_to(m, (SPAN, HEAD_DIM))
            l_ref[rows, :] = jnp.broadcast_to(l_new, (SPAN, HEAD_DIM))
        den = l_ref[...]
        ob = acc_ref[...] / den
        acc_ref[...] = ob
        m_ref[...] = m_ref[...] + jnp.log(den)
        rstd = lax.rsqrt(jnp.mean(ob * ob, axis=-1, keepdims=True) + EPS)
        mix_ref[...] = (ob * rstd * w_ref[...]).astype(BF16)

    col = lambda off: pl.BlockSpec((s, HEAD_DIM), lambda h: (0, off + h))
    return pl.pallas_call(
        body, name="attn_fwd", grid=(heads,),
        in_specs=[col(0), col(heads), col(2 * heads), pl.BlockSpec((1, HEAD_DIM), lambda h: (0, 0))],
        out_specs=[col(0), col(0), col(0)],
        out_shape=[jax.ShapeDtypeStruct((s, aw), BF16), jax.ShapeDtypeStruct((s, aw), F32),
                   jax.ShapeDtypeStruct((s, aw), F32)],
        scratch_shapes=[pltpu.VMEM((s, HEAD_DIM), F32)],
        compiler_params=_params(("parallel",)),
    )(qk, qk, aqkv, w)


def _attn_merge_bwd(dmix, ob, w, heads):
    s = ob.shape[0]
    tr = _tile(s, 1024, 8)

    def body(d_ref, ob_ref, w_ref, do_ref, dsum_ref, dw_ref):
        d, ov = d_ref[...], ob_ref[...]
        rstd = lax.rsqrt(jnp.mean(ov * ov, axis=-1, keepdims=True) + EPS)
        ohat = ov * rstd
        gw = d * w_ref[...]
        dov = rstd * (gw - ohat * jnp.mean(gw * ohat, axis=-1, keepdims=True))
        do_ref[...] = dov
        dsum_ref[...] = jnp.broadcast_to(jnp.sum(dov * ov, axis=-1, keepdims=True), dov.shape)

        @pl.when((pl.program_id(0) == 0) & (pl.program_id(1) == 0))
        def _():
            dw_ref[...] = jnp.zeros_like(dw_ref)

        dw_ref[...] += jnp.sum(d * ohat, axis=0, keepdims=True)

    blk = pl.BlockSpec((tr, HEAD_DIM), lambda i, h: (i, h))
    vec = pl.BlockSpec((1, HEAD_DIM), lambda i, h: (0, 0))
    aw = heads * HEAD_DIM
    return pl.pallas_call(
        body, name="attn_merge_bwd", grid=(s // tr, heads),
        in_specs=[pl.BlockSpec((tr, HEAD_DIM), lambda i, h: (i, heads + h)), blk, vec],
        out_specs=[blk, blk, vec],
        out_shape=[jax.ShapeDtypeStruct((s, aw), F32), jax.ShapeDtypeStruct((s, aw), F32),
                   jax.ShapeDtypeStruct((1, HEAD_DIM), F32)],
        compiler_params=_params(("arbitrary", "arbitrary")),
    )(dmix, ob, w)


def _attn_bwd(qk, aqkv, do, lse, dsum, heads):
    s = qk.shape[0]
    aw = heads * HEAD_DIM
    scale = HEAD_DIM ** -0.5

    def body(q_ref, k_ref, v_ref, do_ref, l_ref, ds_ref, out_ref):
        band, own = _band()
        dq_ref, dk_ref, dv_ref = out_ref.at[0], out_ref.at[1], out_ref.at[2]
        out_ref[...] = jnp.zeros((3, s, HEAD_DIM), F32)
        for _, d, r, g in _attn_blocks(s):
            rows, prows = _sub(g, r, d), _sub(max(g - 1, 0), r, d)
            mask = band if g > 0 else band & own
            q, dov = q_ref[rows, :], do_ref[rows, :]
            kcat = jnp.concatenate([k_ref[prows, :], k_ref[rows, :]], axis=0)
            vcat = jnp.concatenate([v_ref[prows, :], v_ref[rows, :]], axis=0)
            sc = _dot(q, kcat, "nt") * scale
            p = jnp.where(mask, jnp.exp(sc - l_ref[rows, 0:1]), 0.0)
            dsc = p * (_dot(dov, vcat, "nt") - ds_ref[rows, 0:1]) * scale
            dq_ref[rows, :] += _dot(dsc, kcat)
            dk = _dot(dsc, q, "tn")
            dv = _dot(p, dov, "tn")
            dk_ref[rows, :] += dk[SPAN:]
            dv_ref[rows, :] += dv[SPAN:]
            if g > 0:
                dk_ref[prows, :] += dk[:SPAN]
                dv_ref[prows, :] += dv[:SPAN]

    col = lambda off: pl.BlockSpec((s, HEAD_DIM), lambda h: (0, off + h))
    return pl.pallas_call(
        body, name="attn_bwd", grid=(heads,),
        in_specs=[col(0), col(heads), col(2 * heads), col(0), col(0), col(0)],
        out_specs=pl.BlockSpec((3, s, HEAD_DIM), lambda h: (0, 0, h)),
        out_shape=jax.ShapeDtypeStruct((3, s, aw), F32),
        compiler_params=_params(("parallel",)),
    )(qk, qk, aqkv, do, lse, dsum)


def _attn_pre_bwd(grads, aqkv, wq, wk, cosf, sinf, heads):
    s = aqkv.shape[0]
    tr = _tile(s, 1024, 8)
    nrow = s // tr

    def body(g_ref, x_ref, wq_ref, wk_ref, c_ref, s_ref, dx_ref, dwq_ref, dwk_ref):
        i, j = pl.program_id(0), pl.program_id(1)
        kind = j // heads
        dout = g_ref[0]
        tot_v = dout
        dy = dout * c_ref[...] + pltpu.roll(dout * s_ref[...], HEAD_DIM // 2, 1)
        xv = x_ref[...]
        wv = jnp.where(kind == 0, wq_ref[...], wk_ref[...])
        rstd = lax.rsqrt(jnp.mean(xv * xv, axis=-1, keepdims=True) + EPS)
        xhat = xv * rstd
        gw = dy * wv
        dxn = rstd * (gw - xhat * jnp.mean(gw * xhat, axis=-1, keepdims=True))
        dx_ref[...] = jnp.where(kind == 2, tot_v, dxn).astype(BF16)
        dwc = jnp.sum(dy * xhat, axis=0, keepdims=True)

        @pl.when((i == 0) & (j == 0))
        def _():
            dwq_ref[...] = jnp.zeros_like(dwq_ref)
            dwk_ref[...] = jnp.zeros_like(dwk_ref)

        @pl.when(kind == 0)
        def _():
            dwq_ref[...] += dwc

        @pl.when(kind == 1)
        def _():
            dwk_ref[...] += dwc

    grad = pl.BlockSpec((1, tr, HEAD_DIM), lambda i, j: (j // heads, i, j % heads))
    blk = pl.BlockSpec((tr, HEAD_DIM), lambda i, j: (i, j))
    vec = pl.BlockSpec((1, HEAD_DIM), lambda i, j: (0, 0))
    tab = pl.BlockSpec((tr, HEAD_DIM), lambda i, j: (i, 0))
    return pl.pallas_call(
        body, name="attn_pre_bwd", grid=(nrow, 3 * heads),
        in_specs=[grad, blk, vec, vec, tab, tab], out_specs=[blk, vec, vec],
        out_shape=[jax.ShapeDtypeStruct((s, 3 * heads * HEAD_DIM), BF16),
                   jax.ShapeDtypeStruct((1, HEAD_DIM), F32), jax.ShapeDtypeStruct((1, HEAD_DIM), F32)],
        compiler_params=_params(("arbitrary", "arbitrary")),
    )(grads, aqkv, wq, wk, cosf, sinf)


def _swiglu_fwd(gu, ff):
    s = gu.shape[0]
    tr, tc = _tile(s, 512, 8), _tile(ff, 1024, 128)
    nc = ff // tc

    def body(g_ref, u_ref, o_ref):
        g = g_ref[...].astype(F32)
        o_ref[...] = (g * _sigmoid(g) * u_ref[...].astype(F32)).astype(BF16)

    return pl.pallas_call(
        body, name="swiglu_fwd", grid=(s // tr, nc),
        in_specs=[pl.BlockSpec((tr, tc), lambda i, j: (i, j)), pl.BlockSpec((tr, tc), lambda i, j: (i, j + nc))],
        out_specs=pl.BlockSpec((tr, tc), lambda i, j: (i, j)),
        out_shape=jax.ShapeDtypeStruct((s, ff), BF16),
        compiler_params=_params(("parallel", "parallel")),
    )(gu, gu)


def _colsum(a, name):
    s, d = a.shape
    tr = _tile(s, 1024, 8)

    def body(a_ref, o_ref):
        @pl.when(pl.program_id(0) == 0)
        def _():
            o_ref[...] = jnp.zeros_like(o_ref)

        o_ref[...] += jnp.sum(a_ref[...], axis=0, keepdims=True)

    return pl.pallas_call(
        body, name=name, grid=(s // tr,),
        in_specs=[pl.BlockSpec((tr, d), lambda i: (i, 0))], out_specs=pl.BlockSpec((1, d), lambda i: (0, 0)),
        out_shape=jax.ShapeDtypeStruct((1, d), F32),
        compiler_params=_params(("arbitrary",)),
    )(a)


def _local_step(x, positions, target, small, w_qkvz, w_bd, w_attn, conv_w, later_weights, ffn_grads_ready,
                rest_grads_ready, after=None):
    s, dmod = x.shape
    heads = dmod // (2 * HEAD_DIM)
    dw = heads * HEAD_DIM
    a_log, dt_bias = small["a_log"].reshape(heads, 1, 1), small["dt_bias"].reshape(heads, 1, 1)
    cosf, sinf = _rope_tables(positions, s)

    h1 = _rmsnorm_fwd(x, small["attn_norm_w"], "norm1_fwd", after=after)
    qkvz = _matmul(h1, w_qkvz, "nn", "proj_qkvz")
    bd = _matmul(h1, w_bd, "nn", "proj_bd")
    aqkv = _matmul(h1, w_attn, "nn", "proj_attn")
    dqkv = _delta_pre_fwd(qkvz, conv_w, heads)
    o_d, states = _delta_chunk_fwd(dqkv, bd, a_log, dt_bias, heads)
    mix_a = _delta_post_fwd(o_d, qkvz, small["delta_out_norm_w"], heads)
    qk_rot = _attn_pre_fwd(aqkv, small["q_norm_w"], small["k_norm_w"], cosf, sinf, heads)
    mix_b, ob, lse = _attn_fwd(qk_rot, aqkv, small["attn_out_norm_w"], heads)
    w_out, w_gu, w_down = later_weights((mix_a, mix_b))
    ff = w_down.shape[0]
    x1 = _matmul(mix_a, w_out, "nn", "out_proj_a", add=x, b_rows=(0, dw))
    x1 = _matmul(mix_b, w_out, "nn", "out_proj_b", add=x1, b_rows=(dw, dw))
    h2 = _rmsnorm_fwd(x1, small["ffn_norm_w"], "norm2_fwd")
    gu = _matmul(h2, w_gu, "nn", "ffn_gate_up", out_dtype=BF16)
    act = _swiglu_fwd(gu, ff)

    def loss_head(acc, vals):
        err = acc + vals[0] - vals[1]
        part = 0.5 * jnp.sum(jnp.sum(err * err, axis=-1, keepdims=True) * (1.0 / dmod), axis=0, keepdims=True)
        lane = lax.broadcasted_iota(jnp.int32, (1, HEAD_DIM), 1)
        return err * (1.0 / dmod), err * (1.0 / dmod), jnp.where(lane == 0, part, 0.0)

    dy, dy16, loss_row = _matmul(act, w_down, "nn", "ffn_down", tiles=[(x1, 0), (target, 0)], finish=loss_head,
                                 out_dtypes=[F32, BF16], row_sum=True)

    def swiglu_bwd(acc, vals):
        g, u = vals[0].astype(F32), vals[1].astype(F32)
        sg = _sigmoid(g)
        return acc * u * sg * (1.0 + g * (1.0 - sg)), acc * g * sg

    dgate, dup = _matmul(dy16, w_down, "nt", "ffn_down_dx", tiles=[(gu, 0), (gu, ff)], finish=swiglu_bwd,
                         out_dtypes=[BF16, BF16])
    g_w_down = _matmul(act, dy16, "tn", "ffn_down_dw")
    g_w_gate = _matmul(h2, dgate, "tn", "ffn_gate_dw")
    g_w_up = _matmul(h2, dup, "tn", "ffn_up_dw")
    behind, and_then = ffn_grads_ready(g_w_gate, g_w_up, g_w_down)
    dh2 = _matmul(dgate, w_gu, "nt", "ffn_gate_dx", b_cols=(0, ff), after=behind)
    dh2 = _matmul(dup, w_gu, "nt", "ffn_up_dx", b_cols=(ff, ff), add=dh2)
    behind = and_then(dh2)
    dx1, dx1_16, g_ffn_norm = _rmsnorm_bwd(dh2, x1, small["ffn_norm_w"], dy, "norm2_bwd", after=behind)
    dmix = _matmul(dx1_16, w_out, "nt", "out_proj_dx")
    g_w_out_a = _matmul(mix_a, dx1_16, "tn", "out_proj_dw_a")
    g_w_out_b = _matmul(mix_b, dx1_16, "tn", "out_proj_dw_b")
    dob, dsum, g_attn_out_norm = _attn_merge_bwd(dmix, ob, small["attn_out_norm_w"], heads)
    grads = _attn_bwd(qk_rot, aqkv, dob, lse, dsum, heads)
    d_aqkv, g_q_norm, g_k_norm = _attn_pre_bwd(grads, aqkv, small["q_norm_w"], small["k_norm_w"], cosf, sinf, heads)
    do_d, dz, g_delta_out_norm = _delta_post_bwd(dmix, o_d, qkvz, small["delta_out_norm_w"], heads)
    ddqkv, dbd = _delta_chunk_bwd(do_d, dqkv, bd, states, a_log, dt_bias, heads)
    d_qkv_raw, g_conv = _delta_pre_bwd(ddqkv, qkvz, conv_w, heads)
    bd_sums = _colsum(dbd, "bd_colsum")
    g_w_qkv = _matmul(h1, d_qkv_raw, "tn", "proj_qkv_dw")
    g_w_z = _matmul(h1, dz, "tn", "proj_z_dw")
    g_w_bd = _matmul(h1, dbd, "tn", "proj_bd_dw")
    g_w_attn = _matmul(h1, d_aqkv, "tn", "proj_attn_dw")
    behind = rest_grads_ready(dict(w_qkv=g_w_qkv, w_z=g_w_z, w_bd=g_w_bd, w_attn=g_w_attn, w_out_a=g_w_out_a,
                                   w_out_b=g_w_out_b))
    dh1 = _matmul(d_qkv_raw, w_qkvz, "nt", "proj_qkv_dx", b_cols=(0, 3 * dw), after=behind)
    dh1 = _matmul(dz, w_qkvz, "nt", "proj_z_dx", b_cols=(3 * dw, dw), add=dh1)
    dh1 = _matmul(d_aqkv, w_attn, "nt", "proj_attn_dx", add=dh1)
    dh1 = _matmul(dbd, w_bd, "nt", "proj_bd_dx", add=dh1)
    grad_x, _, g_attn_norm = _rmsnorm_bwd(dh1, x, small["attn_norm_w"], dx1, "norm1_bwd")
    small_grads = dict(
        attn_norm_w=g_attn_norm, a_log=bd_sums[:, 2 * heads:3 * heads], dt_bias=bd_sums[:, heads:2 * heads],
        delta_out_norm_w=g_delta_out_norm, q_norm_w=g_q_norm, k_norm_w=g_k_norm,
        attn_out_norm_w=g_attn_out_norm, ffn_norm_w=g_ffn_norm, conv_w=g_conv)
    return loss_row, grad_x, small_grads


def _adamw(w, g, m, v, name):
    r, c = w.shape
    tr = _tile(r, 256, 8)

    def body(w_ref, g_ref, m_ref, v_ref, d_ref, nm_ref, nv_ref):
        gv = g_ref[...]
        nm = ADAM_B1 * m_ref[...] + (1.0 - ADAM_B1) * gv
        nv = ADAM_B2 * v_ref[...] + (1.0 - ADAM_B2) * (gv * gv)
        m_hat = nm / (1.0 - ADAM_B1 ** ADAM_STEP)
        v_hat = nv / (1.0 - ADAM_B2 ** ADAM_STEP)
        d_ref[...] = -ADAM_LR * (m_hat / (jnp.sqrt(v_hat) + ADAM_EPS) + ADAM_WD * w_ref[...])
        nm_ref[...] = nm
        nv_ref[...] = nv

    blk = pl.BlockSpec((tr, c), lambda i: (i, 0))
    return pl.pallas_call(
        body, name=name, grid=(r // tr,),
        in_specs=[blk] * 4, out_specs=[blk] * 3,
        out_shape=[jax.ShapeDtypeStruct((r, c), F32)] * 3,
        compiler_params=_params(("parallel",)),
    )(w, g, m, v)


def _add_half_bf16(g, b, place, name):
    n, half, c = b.shape
    tr = _tile(half, 512, 16)
    nb = half // tr

    def body(place_ref, g_ref, b_ref, o_ref):
        o_ref[...] = (g_ref[...].astype(F32) + b_ref[...].astype(F32)).astype(BF16)

    blk = pl.BlockSpec((1, tr, c), lambda i, j, p: (i, j, 0))
    return pl.pallas_call(
        body, name=name,
        grid_spec=pltpu.PrefetchScalarGridSpec(
            num_scalar_prefetch=1, grid=(n, nb),
            in_specs=[pl.BlockSpec((1, tr, c), lambda i, j, p: (i, p[0] * nb + j, 0)), blk], out_specs=blk),
        out_shape=jax.ShapeDtypeStruct((n, half, c), BF16),
        compiler_params=_params(("parallel", "parallel")),
    )(place, g, b)


def _sum4_f32(mine, others, place, name):
    _, half, c = mine.shape
    tr = _tile(half, 512, 16)
    nb = half // tr

    def body(place_ref, a_ref, b_ref, o_ref):
        acc = a_ref[0].astype(F32)
        for j in range(3):
            acc = acc + b_ref[j].astype(F32)
        o_ref[...] = acc

    return pl.pallas_call(
        body, name=name,
        grid_spec=pltpu.PrefetchScalarGridSpec(
            num_scalar_prefetch=1, grid=(nb,),
            in_specs=[pl.BlockSpec((1, tr, c), lambda i, p: (p[1], i, 0)),
                      pl.BlockSpec((3, tr, c), lambda i, p: (0, i, 0))],
            out_specs=pl.BlockSpec((tr, c), lambda i, p: (p[0] * nb + i, 0))),
        out_shape=jax.ShapeDtypeStruct((2 * half, c), F32),
        compiler_params=_params(("parallel",)),
    )(place, mine, others)


def _place():
    x, y, c = lax.axis_index("x"), lax.axis_index("y"), lax.axis_index("c")
    other_chips = [(1 - x, y), (x, 1 - y), (1 - x, 1 - y)]
    return x, y, c, (x, y, 1 - c), other_chips


ANY = pl.BlockSpec(memory_space=pl.ANY)


def _remote(k, src, dst, to, send_sems, recv_sems):
    return pltpu.make_async_remote_copy(src_ref=src, dst_ref=dst, send_sem=send_sems.at[k], recv_sem=recv_sems.at[k],
                                        device_id=to, device_id_type=MESH)


def _half(ref, lead, hc):
    if lead is None:
        half = ref.shape[0] // 2
        return ref.at[pl.ds(hc * half, half), :]
    half = ref.shape[1] // 2
    return ref.at[lead, pl.ds(hc * half, half), :]


def _all_gather_weights(slots, whole, name):
    nt, nw = len(slots), len(whole)
    n_ici = 3 * (nt + nw)

    def body(*refs):
        ins, outs = refs[:nt + nw], refs[nt + nw:2 * (nt + nw)]
        send_sems, recv_sems = refs[2 * (nt + nw):]
        x, y, c, sibling, chips = _place()
        me = 2 * x + y
        first = []
        for j, (px, py) in enumerate(chips):
            for t in range(nt):
                first.append(_remote(j * (nt + nw) + t, _half(ins[t], me, c), _half(outs[t], me, c), (px, py, c),
                                     send_sems, recv_sems))
            for t in range(nt, nt + nw):
                first.append(_remote(j * (nt + nw) + t, ins[t].at[me], outs[t].at[me], (px, py, c),
                                     send_sems, recv_sems))
        for cp in first:
            cp.start()
        passed = []
        for j, (px, py) in enumerate(chips):
            for t in range(nt):
                landed = _half(outs[t], 2 * px + py, c)
                _remote(j * (nt + nw) + t, landed, landed, (px, py, c), send_sems, recv_sems).wait_recv()
                fw = _remote(n_ici + j * nt + t, landed, landed, sibling, send_sems, recv_sems)
                fw.start()
                passed.append(fw)
            for t in range(nt, nt + nw):
                landed = outs[t].at[2 * px + py]
                _remote(j * (nt + nw) + t, landed, landed, (px, py, c), send_sems, recv_sems).wait_recv()
        for j, (px, py) in enumerate(chips):
            for t in range(nt):
                other = _half(outs[t], 2 * px + py, 1 - c)
                _remote(n_ici + j * nt + t, other, other, sibling, send_sems, recv_sems).wait_recv()
        for cp in first + passed:
            cp.wait_send()

    arrays = list(slots) + list(whole)
    n_sem = n_ici + 3 * nt
    return pl.pallas_call(
        body, name=name, in_specs=[ANY] * len(arrays), out_specs=[ANY] * len(arrays),
        input_output_aliases={i: i for i in range(len(arrays))},
        out_shape=[jax.ShapeDtypeStruct(a.shape, a.dtype) for a in arrays],
        scratch_shapes=[pltpu.SemaphoreType.DMA((n_sem,)), pltpu.SemaphoreType.DMA((n_sem,))],
    )(*arrays)


def _swap_halves_with_sibling(gs, name):
    nt = len(gs)

    def body(*refs):
        g_refs, o_refs, (send_sems, recv_sems) = refs[:nt], refs[nt:2 * nt], refs[2 * nt:]
        _, _, c, sibling, _ = _place()
        cps = []
        for t in range(nt):
            half = g_refs[t].shape[1] // 2
            cps.append(_remote(t, g_refs[t].at[:, pl.ds((1 - c) * half, half), :], o_refs[t], sibling,
                               send_sems, recv_sems))
        for cp in cps:
            cp.start()
        for cp in cps:
            cp.wait()

    return pl.pallas_call(
        body, name=name, in_specs=[ANY] * nt, out_specs=[ANY] * nt,
        out_shape=[jax.ShapeDtypeStruct((g.shape[0], g.shape[1] // 2, g.shape[2]), g.dtype) for g in gs],
        scratch_shapes=[pltpu.SemaphoreType.DMA((nt,)), pltpu.SemaphoreType.DMA((nt,))],
    )(*gs)


def _scatter_to_chips(ps, name):
    nt = len(ps)

    def body(*refs):
        p_refs, b_refs, (send_sems, recv_sems) = refs[:nt], refs[nt:2 * nt], refs[2 * nt:]
        _, _, c, _, chips = _place()
        cps = [_remote(j * nt + t, p_refs[t].at[2 * px + py], b_refs[t].at[j], (px, py, c), send_sems, recv_sems)
               for j, (px, py) in enumerate(chips) for t in range(nt)]
        for cp in cps:
            cp.start()
        for cp in cps:
            cp.wait()

    return pl.pallas_call(
        body, name=name, in_specs=[ANY] * nt, out_specs=[ANY] * nt,
        out_shape=[jax.ShapeDtypeStruct((3,) + p.shape[1:], p.dtype) for p in ps],
        scratch_shapes=[pltpu.SemaphoreType.DMA((3 * nt,)), pltpu.SemaphoreType.DMA((3 * nt,))],
    )(*ps)


def _join_halves(fs, name):
    nt = len(fs)

    def body(*refs):
        in_refs, out_refs, (send_sems, recv_sems) = refs[:nt], refs[nt:2 * nt], refs[2 * nt:]
        _, _, c, sibling, _ = _place()
        cps = [_remote(t, _half(in_refs[t], None, c), _half(out_refs[t], None, c), sibling, send_sems, recv_sems)
               for t in range(nt)]
        for cp in cps:
            cp.start()
        for t in range(nt):
            theirs = _half(out_refs[t], None, 1 - c)
            _remote(t, theirs, theirs, sibling, send_sems, recv_sems).wait_recv()
        for cp in cps:
            cp.wait_send()

    return pl.pallas_call(
        body, name=name, in_specs=[ANY] * nt, out_specs=[ANY] * nt,
        input_output_aliases={i: i for i in range(nt)},
        out_shape=[jax.ShapeDtypeStruct(f.shape, f.dtype) for f in fs],
        scratch_shapes=[pltpu.SemaphoreType.DMA((nt,)), pltpu.SemaphoreType.DMA((nt,))],
    )(*fs)


HBM = pl.BlockSpec(memory_space=pltpu.HBM)
SEM = pl.BlockSpec(memory_space=pltpu.SEMAPHORE)
EFFECT = pltpu.SideEffectType.DATAFLOW_SIDE_EFFECTING


def _split_start(arrays, after, plan, n_copies, name):
    na = len(arrays)

    def body(*refs):
        ins, send_sems, recv_sems = refs[:na], refs[na + 1], refs[na + 2]
        outs, token = refs[na + 3:2 * na + 3], refs[2 * na + 3]
        for k, (src, dst, to) in enumerate(plan(ins, outs)):
            _remote(k, src, dst, to, send_sems, recv_sems).start()
        token[...] = jnp.zeros_like(token)

    res = pl.pallas_call(
        body, name=name,
        out_shape=(pltpu.SemaphoreType.DMA((n_copies,)), pltpu.SemaphoreType.DMA((n_copies,)),
                   *[pltpu.HBM(a.shape, a.dtype) for a in arrays], jax.ShapeDtypeStruct((8, HEAD_DIM), F32)),
        in_specs=[HBM] * na + [ANY],
        out_specs=(SEM, SEM, *[HBM] * na, pl.BlockSpec(memory_space=pltpu.VMEM)),
        input_output_aliases={i: 2 + i for i in range(na)},
        compiler_params=pltpu.CompilerParams(has_side_effects=EFFECT),
    )(*[pltpu.with_memory_space_constraint(a, pltpu.HBM) for a in arrays], after)
    return res[0], res[1], list(res[2:2 + na]), res[2 + na]


def _split_wait(send_sems, recv_sems, arrays, after, plan, name):
    na = len(arrays)
    after = list(after) if isinstance(after, (list, tuple)) else [after]

    def body(*refs):
        ins, send, recv, outs = refs[:na], refs[na], refs[na + 1], refs[na + 2 + len(after):]
        for k, (src, dst, to) in enumerate(plan(ins, outs)):
            cp = _remote(k, src, dst, to, send, recv)
            cp.wait_send()
            cp.wait_recv()

    res = pl.pallas_call(
        body, name=name, out_shape=tuple(pltpu.HBM(a.shape, a.dtype) for a in arrays),
        in_specs=[HBM] * na + [SEM, SEM] + [ANY] * len(after), out_specs=tuple([HBM] * na),
        input_output_aliases={i: i for i in range(na)},
        compiler_params=pltpu.CompilerParams(has_side_effects=EFFECT),
    )(*arrays, send_sems, recv_sems, *after)
    return list(res)


def _gather_plan(nt):
    def plan(ins, outs):
        x, y, c, _, chips = _place()
        me = 2 * x + y
        return [(_half(ins[t], me, c), _half(outs[t], me, c), (px, py, c)) for px, py in chips for t in range(nt)]
    return plan


def _gather_landed_plan(nt):
    def plan(ins, outs):
        _, _, c, _, chips = _place()
        return [(_half(outs[t], 2 * px + py, c), _half(outs[t], 2 * px + py, c), (px, py, c))
                for px, py in chips for t in range(nt)]
    return plan


def _scatter_plan(nt):
    def plan(ins, outs):
        _, _, c, _, chips = _place()
        return [(ins[t].at[2 * px + py], outs[nt + t].at[j], (px, py, c))
                for j, (px, py) in enumerate(chips) for t in range(nt)]
    return plan


def _swap_plan(nt):
    def plan(ins, outs):
        _, _, c, sibling, _ = _place()
        res = []
        for t in range(nt):
            half = ins[t].shape[1] // 2
            res.append((ins[t].at[:, pl.ds((1 - c) * half, half), :], outs[nt + t], sibling))
        return res
    return plan


def _pass_halves_to_sibling(slots, name):
    nt = len(slots)

    def body(*refs):
        ins, outs, (send_sems, recv_sems) = refs[:nt], refs[nt:2 * nt], refs[2 * nt:]
        _, _, c, sibling, chips = _place()
        cps = [_remote(j * nt + t, _half(ins[t], 2 * px + py, c), _half(outs[t], 2 * px + py, c), sibling,
                       send_sems, recv_sems)
               for j, (px, py) in enumerate(chips) for t in range(nt)]
        for cp in cps:
            cp.start()
        for j, (px, py) in enumerate(chips):
            for t in range(nt):
                other = _half(outs[t], 2 * px + py, 1 - c)
                _remote(j * nt + t, other, other, sibling, send_sems, recv_sems).wait_recv()
        for cp in cps:
            cp.wait_send()

    return pl.pallas_call(
        body, name=name, in_specs=[ANY] * nt, out_specs=[ANY] * nt,
        input_output_aliases={i: i for i in range(nt)},
        out_shape=[jax.ShapeDtypeStruct(a.shape, a.dtype) for a in slots],
        scratch_shapes=[pltpu.SemaphoreType.DMA((3 * nt,)), pltpu.SemaphoreType.DMA((3 * nt,))],
    )(*slots)


def _all_reduce_small(v):
    r, lanes = v.shape

    def body(v_ref, out_ref, buf, send_sems, recv_sems):
        x, y, c, sibling, chips = _place()

        def slot(px, py, pc):
            return buf.at[4 * px + 2 * py + pc]

        def copy(k, block, to, src=None):
            return pltpu.make_async_remote_copy(src_ref=slot(*block) if src is None else src, dst_ref=slot(*block),
                                                send_sem=send_sems.at[k], recv_sem=recv_sems.at[k],
                                                device_id=to, device_id_type=MESH)

        me = (x, y, c)
        buf[4 * x + 2 * y + c] = v_ref[...]
        first = [copy(0, me, sibling, src=v_ref)]
        first += [copy(1 + j, me, (*chip, c), src=v_ref) for j, chip in enumerate(chips)]
        for cp in first:
            cp.start()
        passed = [copy(4 + j, (*chip, c), sibling) for j, chip in enumerate(chips)]
        for j, chip in enumerate(chips):
            copy(1 + j, (*chip, c), me).wait_recv()
            passed[j].start()
        copy(0, (x, y, 1 - c), me).wait_recv()
        for j, chip in enumerate(chips):
            copy(4 + j, (*chip, 1 - c), me).wait_recv()
        for cp in first + passed:
            cp.wait_send()
        acc = buf[0]
        for k in range(1, 8):
            acc = acc + buf[k]
        out_ref[...] = acc

    vmem = pl.BlockSpec(memory_space=pltpu.VMEM)
    return pl.pallas_call(
        body, name="all_reduce_small", in_specs=[vmem], out_specs=vmem,
        out_shape=jax.ShapeDtypeStruct((r, lanes), F32),
        scratch_shapes=[pltpu.VMEM((8, r, lanes), F32), pltpu.SemaphoreType.DMA((7,)), pltpu.SemaphoreType.DMA((7,))],
    )(v)


def _size(shape):
    n = 1
    for d in shape:
        n *= d
    return n


def _pack_small(parts):
    rows = []
    for p in parts:
        f = p.reshape(-1).astype(F32)
        n = -(-f.shape[0] // HEAD_DIM) * HEAD_DIM
        rows.append(jnp.pad(f, (0, n - f.shape[0])).reshape(-1, HEAD_DIM))
    a = jnp.concatenate(rows, axis=0)
    return jnp.pad(a, ((0, -a.shape[0] % 8), (0, 0)))


def _unpack_small(a, shapes):
    out, row = [], 0
    for shp in shapes:
        nrows = -(-_size(shp) // HEAD_DIM)
        out.append(a[row:row + nrows].reshape(-1)[:_size(shp)].reshape(shp))
        row += nrows
    return out


SMALL = ["attn_norm_w", "a_log", "dt_bias", "delta_out_norm_w", "q_norm_w", "k_norm_w", "attn_out_norm_w", "ffn_norm_w"]
BIG = ["w_in", "w_out", "w_gate_up", "w_down"]
ORDER = ["attn_norm_w", "w_in", "conv_w", "a_log", "dt_bias", "delta_out_norm_w", "q_norm_w", "k_norm_w",
         "attn_out_norm_w", "w_out", "ffn_norm_w", "w_gate_up", "w_down"]


def kernel(x, positions, attn_norm_w, w_in, conv_w, a_log, dt_bias, delta_out_norm_w, q_norm_w, k_norm_w, attn_out_norm_w, w_out, ffn_norm_w, w_gate_up, w_down, loss_target, m_attn_norm_w, m_w_in, m_conv_w, m_a_log, m_dt_bias, m_delta_out_norm_w, m_q_norm_w, m_k_norm_w, m_attn_out_norm_w, m_w_out, m_ffn_norm_w, m_w_gate_up, m_w_down, v_attn_norm_w, v_w_in, v_conv_w, v_a_log, v_dt_bias, v_delta_out_norm_w, v_q_norm_w, v_k_norm_w, v_attn_out_norm_w, v_w_out, v_ffn_norm_w, v_w_gate_up, v_w_down):
    wts = dict(attn_norm_w=attn_norm_w, w_in=w_in, conv_w=conv_w, a_log=a_log, dt_bias=dt_bias,
               delta_out_norm_w=delta_out_norm_w, q_norm_w=q_norm_w, k_norm_w=k_norm_w,
               attn_out_norm_w=attn_out_norm_w, w_out=w_out, ffn_norm_w=ffn_norm_w, w_gate_up=w_gate_up, w_down=w_down)
    mom = dict(attn_norm_w=m_attn_norm_w, w_in=m_w_in, conv_w=m_conv_w, a_log=m_a_log, dt_bias=m_dt_bias,
               delta_out_norm_w=m_delta_out_norm_w, q_norm_w=m_q_norm_w, k_norm_w=m_k_norm_w,
               attn_out_norm_w=m_attn_out_norm_w, w_out=m_w_out, ffn_norm_w=m_ffn_norm_w, w_gate_up=m_w_gate_up,
               w_down=m_w_down)
    var = dict(attn_norm_w=v_attn_norm_w, w_in=v_w_in, conv_w=v_conv_w, a_log=v_a_log, dt_bias=v_dt_bias,
               delta_out_norm_w=v_delta_out_norm_w, q_norm_w=v_q_norm_w, k_norm_w=v_k_norm_w,
               attn_out_norm_w=v_attn_out_norm_w, w_out=v_w_out, ffn_norm_w=v_ffn_norm_w, w_gate_up=v_w_gate_up,
               w_down=v_w_down)
    dmod = x.shape[2]
    heads = dmod // (2 * HEAD_DIM)
    dw = heads * HEAD_DIM
    chip = 2 * lax.axis_index("x") + lax.axis_index("y")
    core = lax.axis_index("c")
    n_in, n_out, n_gu, n_down, n_conv = (w_in.shape[2], w_out.shape[1], w_gate_up.shape[2], w_down.shape[1],
                                         conv_w.shape[2])

    def slots_of(w, dtype):
        shard = w[0].astype(dtype)
        return lax.dynamic_update_index_in_dim(lax.empty((4,) + shard.shape, dtype), shard, chip, axis=0)

    s_in, s_conv = _all_gather_weights([slots_of(w_in, BF16)], [slots_of(conv_w, F32)], "all_gather_w_in")
    later = [slots_of(w_out, BF16), slots_of(w_gate_up, BF16), slots_of(w_down, BF16)]
    w_send, w_recv, later, started = _split_start(later, s_conv, _gather_plan(3), 9, "gather_rest_start")
    by_cols = lambda a: a.transpose(1, 0, 2).reshape(a.shape[1], 4 * a.shape[2])
    w_in_f, conv_f = by_cols(s_in), by_cols(s_conv)
    w_bd = jnp.pad(w_in_f[:, 4 * dw:4 * dw + 2 * heads], ((0, 0), (0, HEAD_DIM - 2 * heads)))
    small = {n: wts[n] for n in SMALL}
    place = jnp.stack([core, chip]).astype(jnp.int32)
    to_slots = lambda a: a.reshape(a.shape[0], 4, a.shape[1] // 4).transpose(1, 0, 2)

    def later_weights(after):
        landed = _split_wait(w_send, w_recv, later, after, _gather_landed_plan(3), "gather_rest_wait")
        s_out, s_gu, s_down = _pass_halves_to_sibling(landed, "gather_rest_pass")
        return s_out.reshape(4 * n_out, dmod), by_cols(s_gu), s_down.reshape(4 * n_down, dmod)

    ffn = {}

    def ffn_grads_ready(g_gate, g_up, g_down):
        gs = [to_slots(jnp.concatenate([g_gate, g_up], axis=1)), g_down.reshape(4, n_down, dmod)]
        zones = [lax.empty((4, g.shape[1] // 2, g.shape[2]), BF16) for g in gs]
        s_send, s_recv, s_bufs, swapping = _split_start(gs + zones, g_gate, _swap_plan(2), 2, "swap_ffn_start")

        def and_then(after):
            g_gu, g_dn, b_gu, b_dn = _split_wait(s_send, s_recv, s_bufs, after, _swap_plan(2), "swap_ffn_wait")
            sums = [_add_half_bf16(g_gu, b_gu, place, "chip_partial_sum_w_gate_up"),
                    _add_half_bf16(g_dn, b_dn, place, "chip_partial_sum_w_down")]
            zones3 = [lax.empty((3,) + p.shape[1:], BF16) for p in sums]
            ffn["send"], ffn["recv"], ffn["bufs"], token = _split_start(sums + zones3, b_gu, _scatter_plan(2), 6,
                                                                        "scatter_ffn_start")
            return token

        return swapping, and_then

    rest = {}

    def rest_grads_ready(bg):
        gs = [to_slots(jnp.concatenate([bg["w_qkv"], bg["w_z"], bg["w_bd"][:, :2 * heads], bg["w_attn"]], axis=1)),
              jnp.concatenate([bg["w_out_a"], bg["w_out_b"]], axis=0).reshape(4, n_out, dmod)]
        from_sibling = _swap_halves_with_sibling(gs, "swap_grad_halves")
        sums = [_add_half_bf16(g, b, place, "chip_partial_sum_" + n)
                for g, b, n in zip(gs, from_sibling, ["w_in", "w_out"])]
        zones = [lax.empty((3,) + p.shape[1:], BF16) for p in sums]
        rest["send"], rest["recv"], rest["bufs"], token = _split_start(sums + zones, from_sibling[0], _scatter_plan(2),
                                                                       6, "scatter_rest_start")
        return token

    loss_row, grad_x, sg = _local_step(
        x[0], positions[0], loss_target[0], small, w_in_f[:, :4 * dw], w_bd, w_in_f[:, 4 * dw + 2 * heads:], conv_f,
        later_weights, ffn_grads_ready, rest_grads_ready, after=started)

    r_send, r_recv, r_bufs = rest["send"], rest["recv"], rest["bufs"]
    sum_gu, sum_down, got_gu, got_down = _split_wait(ffn["send"], ffn["recv"], ffn["bufs"], grad_x,
                                                     _scatter_plan(2), "scatter_ffn_wait")
    g_big = dict(zip(["w_gate_up", "w_down"], _join_halves(
        [_sum4_f32(sum_gu, got_gu, place, "grad_total_w_gate_up"),
         _sum4_f32(sum_down, got_down, place, "grad_total_w_down")], "join_ffn_halves")))
    grads, deltas, new_m, new_v = {}, {}, {}, {}

    def adamw_big(n):
        shp = wts[n].shape
        d, nm, nv = _adamw(wts[n][0], g_big[n], mom[n][0], var[n][0], "adamw_" + n)
        grads[n], deltas[n], new_m[n], new_v[n] = g_big[n].reshape(shp), d.reshape(shp), nm.reshape(shp), nv.reshape(shp)
        return d

    done = [adamw_big("w_gate_up"), adamw_big("w_down")]
    sum_in, sum_out, got_in, got_out = _split_wait(r_send, r_recv, r_bufs, done, _scatter_plan(2), "scatter_rest_wait")
    g_big.update(zip(["w_in", "w_out"], _join_halves(
        [_sum4_f32(sum_in, got_in, place, "grad_total_w_in"),
         _sum4_f32(sum_out, got_out, place, "grad_total_w_out")], "join_rest_halves")))
    adamw_big("w_in")
    adamw_big("w_out")

    reduced = _all_reduce_small(_pack_small([sg[n] for n in SMALL] + [sg["conv_w"], loss_row]))
    red = _unpack_small(reduced, [wts[n].shape for n in SMALL] + [(4, 4 * n_conv), (1, HEAD_DIM)])
    g_small = dict(zip(SMALL, red[:len(SMALL)]))
    g_conv_full, loss_out = red[len(SMALL)], red[len(SMALL) + 1]
    g_small["conv_w"] = lax.dynamic_slice_in_dim(g_conv_full, chip * n_conv, n_conv, axis=1).reshape(conv_w.shape)

    names = SMALL + ["conv_w"]
    shapes = [wts[n].shape for n in names]
    d, nm, nv = _adamw(_pack_small([wts[n] for n in names]), _pack_small([g_small[n] for n in names]),
                       _pack_small([mom[n] for n in names]), _pack_small([var[n] for n in names]), "adamw_small")
    for n, dd, mm, vv in zip(names, _unpack_small(d, shapes), _unpack_small(nm, shapes), _unpack_small(nv, shapes)):
        grads[n], deltas[n], new_m[n], new_v[n] = g_small[n], dd, mm, vv
    return (loss_out[0, 0], grad_x[None], *[grads[n] for n in ORDER], *[deltas[n] for n in ORDER],
            *[new_m[n] for n in ORDER], *[new_v[n] for n in ORDER])
```

```python
import functools

import jax
import jax.numpy as jnp
from jax import lax
from jax.experimental import pallas as pl
from jax.experimental.pallas import tpu as pltpu

F32 = jnp.float32
BF16 = jnp.bfloat16
HEAD_DIM = 128
CHUNK = 128
INV_BLOCK = 64
SPAN = 128
DILATIONS = (1, 4, 16)
ROPE_THETA = 10000.0
EPS = 1e-6
NEG = -1e30
ADAM_LR, ADAM_B1, ADAM_B2, ADAM_EPS, ADAM_WD, ADAM_STEP = 0.001, 0.9, 0.999, 1e-08, 0.01, 10
VMEM_LIMIT = 48 * 1024 * 1024
MESH = pl.DeviceIdType.MESH

_DN = {"nn": (((1,), (0,)), ((), ())), "nt": (((1,), (1,)), ((), ())), "tn": (((0,), (0,)), ((), ()))}


def _dot(a, b, mode="nn"):
    (ca, cb), _ = _DN[mode]
    if a.ndim == 3:
        dn = (((ca[0] + 1,), (cb[0] + 1,)), ((0,), (0,)))
    else:
        dn = _DN[mode]
    return lax.dot_general(a.astype(BF16), b.astype(BF16), dn, preferred_element_type=F32)


def _rsum(x):
    return jnp.sum(x, axis=-1, keepdims=True)


def _csum(x):
    return jnp.sum(x, axis=-2, keepdims=True)


def _tile(dim, pref, unit=128):
    t = (min(pref, dim) // unit) * unit
    while t >= unit:
        if dim % t == 0:
            return t
        t -= unit
    return dim


def _params(sem):
    return pltpu.CompilerParams(dimension_semantics=sem, vmem_limit_bytes=VMEM_LIMIT)


def _sigmoid(x):
    return 1.0 / (1.0 + jnp.exp(-x))


def _matmul(a, b, mode, name, add=None, out_dtype=F32, a_cols=None, b_cols=None, b_rows=None,
            tiles=(), finish=None, out_dtypes=(), row_sum=False, after=None, b2_cols=None):
    if mode == "tn":
        out_dtype = BF16
    a_off, a_w = a_cols if a_cols else (0, a.shape[1])
    b_off, b_w = b_cols if b_cols else (0, b.shape[1])
    br_off, br_n = b_rows if b_rows else (0, b.shape[0])
    if mode == "nn":
        m, k, n = a.shape[0], a_w, b_w
        assert br_n == k
    elif mode == "nt":
        m, k, n = a.shape[0], a_w, b.shape[0]
        assert b_w == k
    else:
        k, m, n = a.shape[0], a_w, b_w
        assert b.shape[0] == k
    tm, tn = _tile(m, 1024, 128), _tile(n, 1024, 128)
    tk = _tile(k, 2048 if a.dtype == BF16 and b.dtype == BF16 else 1024, 128)
    if mode == "nn":
        assert a_off % tk == 0 and b_off % tn == 0 and br_off % tk == 0
        a_spec = pl.BlockSpec((tm, tk), lambda i, j, kk: (i, kk + a_off // tk))
        b_spec = pl.BlockSpec((tk, tn), lambda i, j, kk: (kk + br_off // tk, j + b_off // tn))
    elif mode == "nt":
        assert a_off % tk == 0 and b_off % tk == 0
        a_spec = pl.BlockSpec((tm, tk), lambda i, j, kk: (i, kk + a_off // tk))
        b_spec = pl.BlockSpec((tn, tk), lambda i, j, kk: (j, kk + b_off // tk))
    else:
        assert a_off % tm == 0 and b_off % tn == 0
        a_spec = pl.BlockSpec((tk, tm), lambda i, j, kk: (kk, i + a_off // tm))
        b_spec = pl.BlockSpec((tk, tn), lambda i, j, kk: (kk, j + b_off // tn))
    nk = k // tk
    if finish is None:
        out_dtypes = [out_dtype]
        if add is None:
            finish = lambda acc, vals: (acc,)
        else:
            tiles = [(add, 0)]
            finish = lambda acc, vals: (acc + vals[0].astype(F32),)
    n_tiles, n_out = len(tiles), len(out_dtypes)
    extra = [] if after is None else [after]
    n_b = 1 if b2_cols is None else 2
    first_out = 1 + n_b + n_tiles + len(extra)

    def body(*refs):
        a_ref, b_refs = refs[0], refs[1:1 + n_b]
        tile_refs, out_refs = refs[1 + n_b:1 + n_b + n_tiles], refs[first_out:first_out + n_out]
        acc_refs = refs[-n_b:]
        kk = pl.program_id(2)
        first_tile = (pl.program_id(0) == 0) & (pl.program_id(1) == 0)

        @pl.when(kk == 0)
        def _():
            for acc_ref in acc_refs:
                acc_ref[...] = jnp.zeros_like(acc_ref)

        av = a_ref[...]
        for acc_ref, b_ref in zip(acc_refs, b_refs):
            acc_ref[...] += _dot(av, b_ref[...], mode)

        @pl.when(kk == nk - 1)
        def _():
            accs = [acc_ref[...] for acc_ref in acc_refs]
            res = finish(accs[0] if n_b == 1 else accs, [t[...] for t in tile_refs])
            for o_ref, r in zip(out_refs, res):
                o_ref[...] = r.astype(o_ref.dtype)
            if row_sum:
                row_ref = refs[first_out + n_out]

                @pl.when(first_tile)
                def _():
                    row_ref[...] = res[n_out]

                @pl.when(jnp.logical_not(first_tile))
                def _():
                    row_ref[...] += res[n_out]

    in_specs = [a_spec, b_spec]
    args = [a, b]
    if b2_cols is not None:
        assert mode == "nn" and b2_cols[1] == n and b2_cols[0] % tn == 0
        in_specs.append(pl.BlockSpec((tk, tn), lambda i, j, kk: (kk + br_off // tk, j + b2_cols[0] // tn)))
        args.append(b)
    for arr, off in tiles:
        assert off % tn == 0
        in_specs.append(pl.BlockSpec((tm, tn), lambda i, j, kk, off=off: (i, j + off // tn)))
        args.append(arr)
    in_specs += [ANY] * len(extra)
    args += extra
    out_specs = [pl.BlockSpec((tm, tn), lambda i, j, kk: (i, j))] * n_out
    out_shape = [jax.ShapeDtypeStruct((m, n), dt) for dt in out_dtypes]
    if row_sum:
        out_specs.append(pl.BlockSpec((1, HEAD_DIM), lambda i, j, kk: (0, 0)))
        out_shape.append(jax.ShapeDtypeStruct((1, HEAD_DIM), F32))
    res = pl.pallas_call(
        body, name=name, grid=(m // tm, n // tn, nk),
        in_specs=in_specs, out_specs=out_specs, out_shape=out_shape,
        scratch_shapes=[pltpu.VMEM((tm, tn), F32)] * n_b,
        compiler_params=_params(("arbitrary",) * 3 if row_sum else ("parallel", "parallel", "arbitrary")),
    )(*args)
    return res[0] if len(res) == 1 else res


def _rmsnorm_fwd(x, w, name, after=None):
    s, d = x.shape
    tr = _tile(s, 512, 8)

    def body(x_ref, w_ref, *rest):
        h_ref = rest[-1]
        xv = x_ref[...]
        rstd = lax.rsqrt(jnp.mean(xv * xv, axis=-1, keepdims=True) + EPS)
        h_ref[...] = (xv * rstd * w_ref[...]).astype(BF16)

    extra = [] if after is None else [after]
    return pl.pallas_call(
        body, name=name, grid=(s // tr,),
        in_specs=[pl.BlockSpec((tr, d), lambda i: (i, 0)), pl.BlockSpec((1, d), lambda i: (0, 0))] + [ANY] * len(extra),
        out_specs=pl.BlockSpec((tr, d), lambda i: (i, 0)),
        out_shape=jax.ShapeDtypeStruct((s, d), BF16),
        compiler_params=_params(("parallel",)),
    )(x, w, *extra)


def _rmsnorm_bwd(dh, x, w, res, name, after=None):
    s, d = x.shape
    tr = _tile(s, 256, 8)

    def body(dh_ref, x_ref, w_ref, res_ref, *rest):
        dx_ref, dx16_ref, dw_ref = rest[-3:]
        xv = x_ref[...]
        rstd = lax.rsqrt(jnp.mean(xv * xv, axis=-1, keepdims=True) + EPS)
        xhat = xv * rstd
        dhv = dh_ref[...]
        gw = dhv * w_ref[...]
        dx = res_ref[...] + rstd * (gw - xhat * jnp.mean(gw * xhat, axis=-1, keepdims=True))
        dx_ref[...] = dx
        dx16_ref[...] = dx.astype(BF16)

        @pl.when(pl.program_id(0) == 0)
        def _():
            dw_ref[...] = jnp.zeros_like(dw_ref)

        dw_ref[...] += jnp.sum(dhv * xhat, axis=0, keepdims=True)

    row = pl.BlockSpec((tr, d), lambda i: (i, 0))
    vec = pl.BlockSpec((1, d), lambda i: (0, 0))
    extra = [] if after is None else [after]
    return pl.pallas_call(
        body, name=name, grid=(s // tr,),
        in_specs=[row, row, vec, row] + [ANY] * len(extra), out_specs=[row, row, vec],
        out_shape=[jax.ShapeDtypeStruct((s, d), F32), jax.ShapeDtypeStruct((s, d), BF16),
                   jax.ShapeDtypeStruct((1, d), F32)],
        compiler_params=_params(("arbitrary",)),
    )(dh, x, w, res, *extra)


def _conv_taps(x, w, rows):
    shifted = [x]
    for sft in (1, 2, 3):
        shifted.append(jnp.where(rows >= sft, pltpu.roll(x, sft, 0), 0.0))
    y = w[3:4, :] * shifted[0] + w[2:3, :] * shifted[1] + w[1:2, :] * shifted[2] + w[0:1, :] * shifted[3]
    return y, shifted


def _delta_pre_fwd(qkvz, conv_w, heads):
    s = qkvz.shape[0]
    nblk = 3 * heads

    def body(x_ref, w_ref, o_ref):
        part = pl.program_id(0) // heads
        rows = lax.broadcasted_iota(jnp.int32, (s, HEAD_DIM), 0)
        y, _ = _conv_taps(x_ref[...], w_ref[...], rows)
        a = y * _sigmoid(y)
        rs = lax.rsqrt(jnp.sum(a * a, axis=-1, keepdims=True) + EPS)
        fac = jnp.where(part == 0, rs * (HEAD_DIM ** -0.5), jnp.where(part == 1, rs, 1.0))
        o_ref[...] = a * fac

    return pl.pallas_call(
        body, name="delta_pre_fwd", grid=(nblk,),
        in_specs=[pl.BlockSpec((s, HEAD_DIM), lambda i: (0, i)), pl.BlockSpec((4, HEAD_DIM), lambda i: (0, i))],
        out_specs=pl.BlockSpec((s, HEAD_DIM), lambda i: (0, i)),
        out_shape=jax.ShapeDtypeStruct((s, 3 * heads * HEAD_DIM), F32),
        compiler_params=_params(("parallel",)),
    )(qkvz, conv_w)


def _delta_pre_bwd(dqkv, qkvz, conv_w, heads):
    s = qkvz.shape[0]
    nblk = 3 * heads

    def body(d_ref, x_ref, w_ref, dx_ref, dw_ref):
        part = pl.program_id(0) // heads
        rows = lax.broadcasted_iota(jnp.int32, (s, HEAD_DIM), 0)
        w = w_ref[...]
        y, shifted = _conv_taps(x_ref[...], w, rows)
        sg = _sigmoid(y)
        a = y * sg
        rs = lax.rsqrt(jnp.sum(a * a, axis=-1, keepdims=True) + EPS)
        unit = a * rs
        dn = d_ref[...]
        scale = jnp.where(part == 0, HEAD_DIM ** -0.5, 1.0)
        da_norm = scale * rs * (dn - unit * jnp.sum(dn * unit, axis=-1, keepdims=True))
        da = jnp.where(part < 2, da_norm, dn)
        dy = da * sg * (1.0 + y * (1.0 - sg))
        dx = w[3:4, :] * dy
        for sft in (1, 2, 3):
            dx = dx + w[3 - sft:4 - sft, :] * jnp.where(rows < s - sft, pltpu.roll(dy, s - sft, 0), 0.0)
        dx_ref[...] = dx.astype(BF16)
        for sft in range(4):
            dw_ref[3 - sft:4 - sft, :] = jnp.sum(dy * shifted[sft], axis=0, keepdims=True)

    col = pl.BlockSpec((s, HEAD_DIM), lambda i: (0, i))
    wsp = pl.BlockSpec((4, HEAD_DIM), lambda i: (0, i))
    return pl.pallas_call(
        body, name="delta_pre_bwd", grid=(nblk,),
        in_specs=[col, col, wsp], out_specs=[col, wsp],
        out_shape=[jax.ShapeDtypeStruct((s, 3 * heads * HEAD_DIM), BF16),
                   jax.ShapeDtypeStruct((4, 3 * heads * HEAD_DIM), F32)],
        compiler_params=_params(("parallel",)),
    )(dqkv, qkvz, conv_w)


def _heads_of(ref, heads):
    return jnp.stack([ref[:, h * HEAD_DIM:(h + 1) * HEAD_DIM] for h in range(heads)])


def _chunk_common(q, k, v, bd, a_log, dt_bias, heads):
    c = CHUNK
    braw = jnp.stack([bd[:, h:h + 1] for h in range(heads)])
    draw = jnp.stack([bd[:, heads + h:heads + h + 1] for h in range(heads)])
    beta = _sigmoid(braw)
    xd = draw + dt_bias
    sp = jnp.maximum(xd, 0.0) + jnp.log1p(jnp.exp(-jnp.abs(xd)))
    g = -jnp.exp(a_log) * sp
    row = lax.broadcasted_iota(jnp.int32, (c, c), 0)
    col = lax.broadcasted_iota(jnp.int32, (c, c), 1)
    sq = (heads, c, c)
    g_b = jnp.broadcast_to(g, sq)
    g_row = _csum(jnp.where(row == col, g_b, 0.0))
    gam_col = _rsum(jnp.where(col <= row, jnp.broadcast_to(g_row, sq), 0.0))
    gam_row = _csum(jnp.where(row <= col, g_b, 0.0))
    causal = row >= col
    dm = jnp.where(causal, jnp.exp(jnp.where(causal, gam_col - gam_row, 0.0)), 0.0)
    kk = _dot(k, k, "nt")
    low = jnp.where(row > col, beta * kk * dm, 0.0)
    assert c in (INV_BLOCK, 2 * INV_BLOCK)
    same = (row // INV_BLOCK) == (col // INV_BLOCK)
    diag = jnp.where(same, low, 0.0)
    t = jnp.where(row == col, 1.0, 0.0) - diag
    pw = diag
    for _ in range((INV_BLOCK - 1).bit_length() - 1):
        pw = _dot(pw, pw)
        t = t + _dot(t, pw)
    if c > INV_BLOCK:
        t = t - _dot(_dot(t, low - diag), t)
    e = jnp.exp(gam_col)
    u = _dot(t, beta * v)
    w = _dot(t, (beta * e) * k)
    qk_raw = _dot(q, k, "nt")
    gl = _csum(g)
    el = jnp.exp(gl - gam_col)
    return dict(beta=beta, xd=xd, g=g, row=row, col=col, dm=dm, kk=kk, t=t, e=e, u=u, w=w,
                qk_raw=qk_raw, qk=qk_raw * dm, gl=gl, el=el, qd=e * q, kd=el * k, cd=jnp.exp(gl))


def _delta_chunk_fwd(qkv, bd, a_log, dt_bias, heads):
    s = qkv.shape[0]
    n = s // CHUNK
    dw = heads * HEAD_DIM
    blk = lambda part: pl.BlockSpec((CHUNK, dw), lambda i: (i, part))

    def body(q_ref, k_ref, v_ref, bd_ref, al_ref, dt_ref, o_ref, st_ref, state):
        @pl.when(pl.program_id(0) == 0)
        def _():
            state[...] = jnp.zeros_like(state)

        cm = _chunk_common(_heads_of(q_ref, heads), _heads_of(k_ref, heads), _heads_of(v_ref, heads), bd_ref[...],
                           al_ref[...], dt_ref[...], heads)
        st = state[...]
        st_ref[0] = st
        vn = cm["u"] - _dot(cm["w"], st)
        o = _dot(cm["qd"], st) + _dot(cm["qk"], vn)
        for h in range(heads):
            o_ref[:, h * HEAD_DIM:(h + 1) * HEAD_DIM] = o[h]
        state[...] = cm["cd"] * st + _dot(cm["kd"], vn, "tn")

    smem = pl.BlockSpec((heads, 1, 1), lambda i: (0, 0, 0))
    return pl.pallas_call(
        body, name="delta_chunk_fwd", grid=(n,),
        in_specs=[blk(0), blk(1), blk(2), pl.BlockSpec((CHUNK, HEAD_DIM), lambda i: (i, 0)), smem, smem],
        out_specs=[pl.BlockSpec((CHUNK, dw), lambda i: (i, 0)),
                   pl.BlockSpec((1, heads, HEAD_DIM, HEAD_DIM), lambda i: (i, 0, 0, 0))],
        out_shape=[jax.ShapeDtypeStruct((s, dw), F32),
                   jax.ShapeDtypeStruct((n, heads, HEAD_DIM, HEAD_DIM), F32)],
        scratch_shapes=[pltpu.VMEM((heads, HEAD_DIM, HEAD_DIM), F32)],
        compiler_params=_params(("arbitrary",)),
    )(qkv, qkv, qkv, bd, a_log, dt_bias)


def _delta_chunk_bwd(do, qkv, bd, states, a_log, dt_bias, heads):
    s = qkv.shape[0]
    n = s // CHUNK
    dw = heads * HEAD_DIM
    c = CHUNK
    blk = lambda part: pl.BlockSpec((CHUNK, dw), lambda i: (n - 1 - i, part))

    def all_heads(q, k, v, dov, st, dsn, bd, a_log, dt_bias):
        cm = _chunk_common(q, k, v, bd, a_log, dt_bias, heads)
        beta, e, dm, row, col = cm["beta"], cm["e"], cm["dm"], cm["row"], cm["col"]
        sq = (heads, c, c)
        vn = cm["u"] - _dot(cm["w"], st)
        dvn = _dot(cm["kd"], dsn)
        dkd = _dot(vn, dsn, "nt")
        dcd = _csum(_rsum(st * dsn))
        ds = cm["cd"] * dsn
        dqd = _dot(dov, st, "nt")
        ds = ds + _dot(cm["qd"], dov, "tn")
        dqk = _dot(dov, vn, "nt")
        dvn = dvn + _dot(cm["qk"], dov, "tn")
        dw_ = -_dot(dvn, st, "nt")
        ds = ds - _dot(cm["w"], dvn, "tn")
        drhs_u = _dot(cm["t"], dvn, "tn")
        drhs_w = _dot(cm["t"], dw_, "tn")
        da = -(_dot(drhs_u, cm["u"], "nt") + _dot(drhs_w, cm["w"], "nt"))
        dl = jnp.where(row > col, da, 0.0)
        dbeta = _rsum(dl * cm["kk"] * dm)
        dkk = dl * beta * dm
        dd = dl * beta * cm["kk"]
        dv = beta * drhs_u
        ek = e * k
        dbeta = dbeta + _rsum(drhs_u * v) + _rsum(drhs_w * ek)
        dk = (beta * e) * drhs_w
        dgam = _rsum(drhs_w * (beta * ek))
        dqkm = dqk * dm
        dq = _dot(dqkm, k)
        dk = dk + _dot(dqkm, q, "tn")
        dd = dd + dqk * cm["qk_raw"]
        dk = dk + _dot(dkk, k) + _dot(dkk, k, "tn")
        dq = dq + e * dqd
        dgam = dgam + _rsum(dqd * cm["qd"])
        dk = dk + cm["el"] * dkd
        r = _rsum(dkd * cm["kd"])
        dgam = dgam - r
        dgl = _csum(r) + dcd * cm["cd"]
        mm = dd * dm
        colsum_c = _rsum(jnp.where(row == col, jnp.broadcast_to(_csum(mm), sq), 0.0))
        dgam = dgam + _rsum(mm) - colsum_c
        ridx = lax.broadcasted_iota(jnp.int32, (c, 1), 0)
        dgam = dgam + jnp.where(ridx == c - 1, dgl, 0.0)
        dgam_row = _csum(jnp.where(row == col, jnp.broadcast_to(dgam, sq), 0.0))
        dg = _rsum(jnp.where(col >= row, jnp.broadcast_to(dgam_row, sq), 0.0))
        d_xd = dg * (-jnp.exp(a_log)) * _sigmoid(cm["xd"])
        d_braw = dbeta * beta * (1.0 - beta)
        d_alog = dg * cm["g"]
        lane = lax.broadcasted_iota(jnp.int32, (c, HEAD_DIM), 1)
        dbd = jnp.zeros((c, HEAD_DIM), F32)
        for h in range(heads):
            dbd = (dbd + jnp.where(lane == h, d_braw[h], 0.0) + jnp.where(lane == h + heads, d_xd[h], 0.0)
                   + jnp.where(lane == h + 2 * heads, d_alog[h], 0.0))
        return dq, dk, dv, ds, dbd

    def body(do_ref, q_ref, k_ref, v_ref, bd_ref, st_ref, al_ref, dt_ref, dqkv_ref, dbd_ref, dstate):
        @pl.when(pl.program_id(0) == 0)
        def _():
            dstate[...] = jnp.zeros_like(dstate)

        dq, dk, dv, ds, dbd = all_heads(_heads_of(q_ref, heads), _heads_of(k_ref, heads), _heads_of(v_ref, heads),
                                        _heads_of(do_ref, heads), st_ref[0], dstate[...], bd_ref[...],
                                        al_ref[...], dt_ref[...])
        for part, val in enumerate((dq, dk, dv)):
            for h in range(heads):
                lo = part * dw + h * HEAD_DIM
                dqkv_ref[:, lo:lo + HEAD_DIM] = val[h]
        dstate[...] = ds
        dbd_ref[...] = dbd

    smem = pl.BlockSpec((heads, 1, 1), lambda i: (0, 0, 0))
    shared = pl.BlockSpec((CHUNK, HEAD_DIM), lambda i: (n - 1 - i, 0))
    wide = pl.BlockSpec((CHUNK, dw), lambda i: (n - 1 - i, 0))
    return pl.pallas_call(
        body, name="delta_chunk_bwd", grid=(n,),
        in_specs=[wide, blk(0), blk(1), blk(2), shared,
                  pl.BlockSpec((1, heads, HEAD_DIM, HEAD_DIM), lambda i: (n - 1 - i, 0, 0, 0)), smem, smem],
        out_specs=[pl.BlockSpec((CHUNK, 3 * dw), lambda i: (n - 1 - i, 0)), shared],
        out_shape=[jax.ShapeDtypeStruct((s, 3 * dw), F32), jax.ShapeDtypeStruct((s, HEAD_DIM), F32)],
        scratch_shapes=[pltpu.VMEM((heads, HEAD_DIM, HEAD_DIM), F32)],
        compiler_params=_params(("arbitrary",)),
    )(do, qkv, qkv, qkv, bd, states, a_log, dt_bias)


def _delta_post_fwd(o, qkvz, w, heads):
    s = o.shape[0]
    tr = _tile(s, 1024, 8)

    def body(o_ref, z_ref, w_ref, out_ref):
        ov, z = o_ref[...], z_ref[...]
        rstd = lax.rsqrt(jnp.mean(ov * ov, axis=-1, keepdims=True) + EPS)
        out_ref[...] = (ov * rstd * w_ref[...] * (z * _sigmoid(z))).astype(BF16)

    return pl.pallas_call(
        body, name="delta_post_fwd", grid=(s // tr, heads),
        in_specs=[pl.BlockSpec((tr, HEAD_DIM), lambda i, h: (i, h)),
                  pl.BlockSpec((tr, HEAD_DIM), lambda i, h: (i, 3 * heads + h)),
                  pl.BlockSpec((1, HEAD_DIM), lambda i, h: (0, 0))],
        out_specs=pl.BlockSpec((tr, HEAD_DIM), lambda i, h: (i, h)),
        out_shape=jax.ShapeDtypeStruct((s, heads * HEAD_DIM), BF16),
        compiler_params=_params(("parallel", "parallel")),
    )(o, qkvz, w)


def _delta_post_bwd(dmix, o, qkvz, w, heads):
    s = o.shape[0]
    tr = _tile(s, 1024, 8)

    def body(d_ref, o_ref, z_ref, w_ref, do_ref, dz_ref, dw_ref):
        d, ov, z, wv = d_ref[...], o_ref[...], z_ref[...], w_ref[...]
        sg = _sigmoid(z)
        rstd = lax.rsqrt(jnp.mean(ov * ov, axis=-1, keepdims=True) + EPS)
        ohat = ov * rstd
        dz_ref[...] = (d * (ohat * wv) * sg * (1.0 + z * (1.0 - sg))).astype(BF16)
        dn = d * (z * sg)
        gw = dn * wv
        do_ref[...] = rstd * (gw - ohat * jnp.mean(gw * ohat, axis=-1, keepdims=True))

        @pl.when((pl.program_id(0) == 0) & (pl.program_id(1) == 0))
        def _():
            dw_ref[...] = jnp.zeros_like(dw_ref)

        dw_ref[...] += jnp.sum(dn * ohat, axis=0, keepdims=True)

    head = pl.BlockSpec((tr, HEAD_DIM), lambda i, h: (i, h))
    vec = pl.BlockSpec((1, HEAD_DIM), lambda i, h: (0, 0))
    dw = heads * HEAD_DIM
    return pl.pallas_call(
        body, name="delta_post_bwd", grid=(s // tr, heads),
        in_specs=[head, head, pl.BlockSpec((tr, HEAD_DIM), lambda i, h: (i, 3 * heads + h)), vec],
        out_specs=[head, head, vec],
        out_shape=[jax.ShapeDtypeStruct((s, dw), F32), jax.ShapeDtypeStruct((s, dw), BF16),
                   jax.ShapeDtypeStruct((1, HEAD_DIM), F32)],
        compiler_params=_params(("arbitrary", "arbitrary")),
    )(dmix, o, qkvz, w)


def _rope_tables(positions, s):
    half = HEAD_DIM // 2
    inv_freq = ROPE_THETA ** (-jnp.arange(half, dtype=F32) / half)
    ang = positions.reshape(s, 1).astype(F32) * inv_freq
    cos, sin = jnp.cos(ang), jnp.sin(ang)
    return jnp.concatenate([cos, cos], axis=-1), jnp.concatenate([-sin, sin], axis=-1)


def _attn_pre_fwd(aqkv, wq, wk, cosf, sinf, heads):
    s = aqkv.shape[0]
    tr = _tile(s, 1024, 8)

    def body(x_ref, wq_ref, wk_ref, c_ref, s_ref, o_ref):
        xv = x_ref[...]
        wv = jnp.where(pl.program_id(1) < heads, wq_ref[...], wk_ref[...])
        y = xv * lax.rsqrt(jnp.mean(xv * xv, axis=-1, keepdims=True) + EPS) * wv
        o_ref[...] = y * c_ref[...] + pltpu.roll(y, HEAD_DIM // 2, 1) * s_ref[...]

    blk = pl.BlockSpec((tr, HEAD_DIM), lambda i, j: (i, j))
    vec = pl.BlockSpec((1, HEAD_DIM), lambda i, j: (0, 0))
    tab = pl.BlockSpec((tr, HEAD_DIM), lambda i, j: (i, 0))
    return pl.pallas_call(
        body, name="attn_pre_fwd", grid=(s // tr, 2 * heads),
        in_specs=[blk, vec, vec, tab, tab], out_specs=blk,
        out_shape=jax.ShapeDtypeStruct((s, 2 * heads * HEAD_DIM), F32),
        compiler_params=_params(("parallel", "parallel")),
    )(aqkv, wq, wk, cosf, sinf)


def _band():
    qi = lax.broadcasted_iota(jnp.int32, (SPAN, 2 * SPAN), 0)
    ki = lax.broadcasted_iota(jnp.int32, (SPAN, 2 * SPAN), 1)
    dist = qi + SPAN - ki
    return (dist >= 0) & (dist <= SPAN), ki >= SPAN


def _sub(g, r, d):
    if d == 1:
        return pl.ds(g * SPAN, SPAN)
    return pl.ds(g * SPAN * d + r, SPAN, stride=d)


def _attn_blocks(s):
    assert s % (SPAN * max(DILATIONS)) == 0
    return [(p_i, d, r, g) for p_i, d in enumerate(DILATIONS) for r in range(d) for g in range(s // (SPAN * d))]


def _attn_fwd(qk, aqkv, w, heads):
    s = qk.shape[0]
    aw = heads * HEAD_DIM

    def body(q_ref, k_ref, v_ref, w_ref, mix_ref, acc_ref, m_ref, l_ref):
        band, own = _band()
        for p_i, d, r, g in _attn_blocks(s):
            rows, prows = _sub(g, r, d), _sub(max(g - 1, 0), r, d)
            mask = band if g > 0 else band & own
            kcat = jnp.concatenate([k_ref[prows, :], k_ref[rows, :]], axis=0)
            vcat = jnp.concatenate([v_ref[prows, :], v_ref[rows, :]], axis=0)
            sc = _dot(q_ref[rows, :], kcat, "nt") * (HEAD_DIM ** -0.5)
            sc = jnp.where(mask, sc, NEG)
            m = jnp.max(sc, axis=-1, keepdims=True)
            if p_i == 0:
                p = jnp.exp(sc - m)
                acc_ref[rows, :] = _dot(p, vcat)
                l_new = jnp.sum(p, axis=-1, keepdims=True)
            else:
                m_old = m_ref[rows, 0:1]
                m = jnp.maximum(m, m_old)
                alpha = jnp.exp(m_old - m)
                p = jnp.exp(sc - m)
                acc_ref[rows, :] = alpha * acc_ref[rows, :] + _dot(p, vcat)
                l_new = alpha * l_ref[rows, 0:1] + jnp.sum(p, axis=-1, keepdims=True)
            m_ref[rows, :] = jnp.broadcast_to(m, (SPAN, HEAD_DIM))
            l_ref[rows, :] = jnp.broadcast_to(l_new, (SPAN, HEAD_DIM))
        den = l_ref[...]
        ob = acc_ref[...] / den
        acc_ref[...] = ob
        m_ref[...] = m_ref[...] + jnp.log(den)
        rstd = lax.rsqrt(jnp.mean(ob * ob, axis=-1, keepdims=True) + EPS)
        mix_ref[...] = (ob * rstd * w_ref[...]).astype(BF16)

    col = lambda off: pl.BlockSpec((s, HEAD_DIM), lambda h: (0, off + h))
    return pl.pallas_call(
        body, name="attn_fwd", grid=(heads,),
        in_specs=[col(0), col(heads), col(2 * heads), pl.BlockSpec((1, HEAD_DIM), lambda h: (0, 0))],
        out_specs=[col(0), col(0), col(0)],
        out_shape=[jax.ShapeDtypeStruct((s, aw), BF16), jax.ShapeDtypeStruct((s, aw), F32),
                   jax.ShapeDtypeStruct((s, aw), F32)],
        scratch_shapes=[pltpu.VMEM((s, HEAD_DIM), F32)],
        compiler_params=_params(("parallel",)),
    )(qk, qk, aqkv, w)


def _attn_merge_bwd(dmix, ob, w, heads):
    s = ob.shape[0]
    tr = _tile(s, 1024, 8)

    def body(d_ref, ob_ref, w_ref, do_ref, dsum_ref, dw_ref):
        d, ov = d_ref[...], ob_ref[...]
        rstd = lax.rsqrt(jnp.mean(ov * ov, axis=-1, keepdims=True) + EPS)
        ohat = ov * rstd
        gw = d * w_ref[...]
        dov = rstd * (gw - ohat * jnp.mean(gw * ohat, axis=-1, keepdims=True))
        do_ref[...] = dov
        dsum_ref[...] = jnp.broadcast_to(jnp.sum(dov * ov, axis=-1, keepdims=True), dov.shape)

        @pl.when((pl.program_id(0) == 0) & (pl.program_id(1) == 0))
        def _():
            dw_ref[...] = jnp.zeros_like(dw_ref)

        dw_ref[...] += jnp.sum(d * ohat, axis=0, keepdims=True)

    blk = pl.BlockSpec((tr, HEAD_DIM), lambda i, h: (i, h))
    vec = pl.BlockSpec((1, HEAD_DIM), lambda i, h: (0, 0))
    aw = heads * HEAD_DIM
    return pl.pallas_call(
        body, name="attn_merge_bwd", grid=(s // tr, heads),
        in_specs=[pl.BlockSpec((tr, HEAD_DIM), lambda i, h: (i, heads + h)), blk, vec],
        out_specs=[blk, blk, vec],
        out_shape=[jax.ShapeDtypeStruct((s, aw), F32), jax.ShapeDtypeStruct((s, aw), F32),
                   jax.ShapeDtypeStruct((1, HEAD_DIM), F32)],
        compiler_params=_params(("arbitrary", "arbitrary")),
    )(dmix, ob, w)


def _attn_bwd(qk, aqkv, do, lse, dsum, heads):
    s = qk.shape[0]
    aw = heads * HEAD_DIM
    scale = HEAD_DIM ** -0.5

    def body(q_ref, k_ref, v_ref, do_ref, l_ref, ds_ref, out_ref):
        band, own = _band()
        dq_ref, dk_ref, dv_ref = out_ref.at[0], out_ref.at[1], out_ref.at[2]
        out_ref[...] = jnp.zeros((3, s, HEAD_DIM), F32)
        for _, d, r, g in _attn_blocks(s):
            rows, prows = _sub(g, r, d), _sub(max(g - 1, 0), r, d)
            mask = band if g > 0 else band & own
            q, dov = q_ref[rows, :], do_ref[rows, :]
            kcat = jnp.concatenate([k_ref[prows, :], k_ref[rows, :]], axis=0)
            vcat = jnp.concatenate([v_ref[prows, :], v_ref[rows, :]], axis=0)
            sc = _dot(q, kcat, "nt") * scale
            p = jnp.where(mask, jnp.exp(sc - l_ref[rows, 0:1]), 0.0)
            dsc = p * (_dot(dov, vcat, "nt") - ds_ref[rows, 0:1]) * scale
            dq_ref[rows, :] += _dot(dsc, kcat)
            dk = _dot(dsc, q, "tn")
            dv = _dot(p, dov, "tn")
            dk_ref[rows, :] += dk[SPAN:]
            dv_ref[rows, :] += dv[SPAN:]
            if g > 0:
                dk_ref[prows, :] += dk[:SPAN]
                dv_ref[prows, :] += dv[:SPAN]

    col = lambda off: pl.BlockSpec((s, HEAD_DIM), lambda h: (0, off + h))
    return pl.pallas_call(
        body, name="attn_bwd", grid=(heads,),
        in_specs=[col(0), col(heads), col(2 * heads), col(0), col(0), col(0)],
        out_specs=pl.BlockSpec((3, s, HEAD_DIM), lambda h: (0, 0, h)),
        out_shape=jax.ShapeDtypeStruct((3, s, aw), F32),
        compiler_params=_params(("parallel",)),
    )(qk, qk, aqkv, do, lse, dsum)


def _attn_pre_bwd(grads, aqkv, wq, wk, cosf, sinf, heads):
    s = aqkv.shape[0]
    tr = _tile(s, 1024, 8)
    nrow = s // tr

    def body(g_ref, x_ref, wq_ref, wk_ref, c_ref, s_ref, dx_ref, dwq_ref, dwk_ref):
        i, j = pl.program_id(0), pl.program_id(1)
        kind = j // heads
        dout = g_ref[0]
        tot_v = dout
        dy = dout * c_ref[...] + pltpu.roll(dout * s_ref[...], HEAD_DIM // 2, 1)
        xv = x_ref[...]
        wv = jnp.where(kind == 0, wq_ref[...], wk_ref[...])
        rstd = lax.rsqrt(jnp.mean(xv * xv, axis=-1, keepdims=True) + EPS)
        xhat = xv * rstd
        gw = dy * wv
        dxn = rstd * (gw - xhat * jnp.mean(gw * xhat, axis=-1, keepdims=True))
        dx_ref[...] = jnp.where(kind == 2, tot_v, dxn).astype(BF16)
        dwc = jnp.sum(dy * xhat, axis=0, keepdims=True)

        @pl.when((i == 0) & (j == 0))
        def _():
            dwq_ref[...] = jnp.zeros_like(dwq_ref)
            dwk_ref[...] = jnp.zeros_like(dwk_ref)

        @pl.when(kind == 0)
        def _():
            dwq_ref[...] += dwc

        @pl.when(kind == 1)
        def _():
            dwk_ref[...] += dwc

    grad = pl.BlockSpec((1, tr, HEAD_DIM), lambda i, j: (j // heads, i, j % heads))
    blk = pl.BlockSpec((tr, HEAD_DIM), lambda i, j: (i, j))
    vec = pl.BlockSpec((1, HEAD_DIM), lambda i, j: (0, 0))
    tab = pl.BlockSpec((tr, HEAD_DIM), lambda i, j: (i, 0))
    return pl.pallas_call(
        body, name="attn_pre_bwd", grid=(nrow, 3 * heads),
        in_specs=[grad, blk, vec, vec, tab, tab], out_specs=[blk, vec, vec],
        out_shape=[jax.ShapeDtypeStruct((s, 3 * heads * HEAD_DIM), BF16),
                   jax.ShapeDtypeStruct((1, HEAD_DIM), F32), jax.ShapeDtypeStruct((1, HEAD_DIM), F32)],
        compiler_params=_params(("arbitrary", "arbitrary")),
    )(grads, aqkv, wq, wk, cosf, sinf)


def _colsum(a, name):
    s, d = a.shape
    tr = _tile(s, 1024, 8)

    def body(a_ref, o_ref):
        @pl.when(pl.program_id(0) == 0)
        def _():
            o_ref[...] = jnp.zeros_like(o_ref)

        o_ref[...] += jnp.sum(a_ref[...], axis=0, keepdims=True)

    return pl.pallas_call(
        body, name=name, grid=(s // tr,),
        in_specs=[pl.BlockSpec((tr, d), lambda i: (i, 0))], out_specs=pl.BlockSpec((1, d), lambda i: (0, 0)),
        out_shape=jax.ShapeDtypeStruct((1, d), F32),
        compiler_params=_params(("arbitrary",)),
    )(a)


def _local_step(x, positions, target, small, w_qkvz, w_bd, w_attn, conv_w, later_weights, ffn_grads_ready,
                rest_grads_ready, after=None):
    s, dmod = x.shape
    heads = dmod // (2 * HEAD_DIM)
    dw = heads * HEAD_DIM
    a_log, dt_bias = small["a_log"].reshape(heads, 1, 1), small["dt_bias"].reshape(heads, 1, 1)
    cosf, sinf = _rope_tables(positions, s)

    h1 = _rmsnorm_fwd(x, small["attn_norm_w"], "norm1_fwd", after=after)
    qkvz = _matmul(h1, w_qkvz, "nn", "proj_qkvz")
    bd = _matmul(h1, w_bd, "nn", "proj_bd")
    aqkv = _matmul(h1, w_attn, "nn", "proj_attn")
    dqkv = _delta_pre_fwd(qkvz, conv_w, heads)
    o_d, states = _delta_chunk_fwd(dqkv, bd, a_log, dt_bias, heads)
    mix_a = _delta_post_fwd(o_d, qkvz, small["delta_out_norm_w"], heads)
    qk_rot = _attn_pre_fwd(aqkv, small["q_norm_w"], small["k_norm_w"], cosf, sinf, heads)
    mix_b, ob, lse = _attn_fwd(qk_rot, aqkv, small["attn_out_norm_w"], heads)
    w_out, behind, ffn_weights = later_weights((mix_a, mix_b))
    x1 = _matmul(mix_a, w_out, "nn", "out_proj_a", add=x, b_rows=(0, dw), after=behind)
    x1 = _matmul(mix_b, w_out, "nn", "out_proj_b", add=x1, b_rows=(dw, dw))
    h2 = _rmsnorm_fwd(x1, small["ffn_norm_w"], "norm2_fwd")
    w_gu, w_down = ffn_weights(h2)
    ff = w_down.shape[0]

    def swiglu(accs, vals):
        g, u = accs
        return g, u, g * _sigmoid(g) * u

    gate, up, act = _matmul(h2, w_gu, "nn", "ffn_gate_up", b_cols=(0, ff), b2_cols=(ff, ff), finish=swiglu,
                            out_dtypes=[BF16, BF16, BF16])

    def loss_head(acc, vals):
        err = acc + vals[0] - vals[1]
        part = 0.5 * jnp.sum(jnp.sum(err * err, axis=-1, keepdims=True) * (1.0 / dmod), axis=0, keepdims=True)
        lane = lax.broadcasted_iota(jnp.int32, (1, HEAD_DIM), 1)
        return err * (1.0 / dmod), err * (1.0 / dmod), jnp.where(lane == 0, part, 0.0)

    dy, dy16, loss_row = _matmul(act, w_down, "nn", "ffn_down", tiles=[(x1, 0), (target, 0)], finish=loss_head,
                                 out_dtypes=[F32, BF16], row_sum=True)

    def swiglu_bwd(acc, vals):
        g, u = vals[0].astype(F32), vals[1].astype(F32)
        sg = _sigmoid(g)
        return acc * u * sg * (1.0 + g * (1.0 - sg)), acc * g * sg

    dgate, dup = _matmul(dy16, w_down, "nt", "ffn_down_dx", tiles=[(gate, 0), (up, 0)], finish=swiglu_bwd,
                         out_dtypes=[BF16, BF16])
    g_w_down = _matmul(act, dy16, "tn", "ffn_down_dw")
    g_w_gate = _matmul(h2, dgate, "tn", "ffn_gate_dw")
    g_w_up = _matmul(h2, dup, "tn", "ffn_up_dw")
    behind, and_then = ffn_grads_ready(g_w_gate, g_w_up, g_w_down)
    dh2 = _matmul(dgate, w_gu, "nt", "ffn_gate_dx", b_cols=(0, ff), after=behind)
    dh2 = _matmul(dup, w_gu, "nt", "ffn_up_dx", b_cols=(ff, ff), add=dh2)
    behind = and_then(dh2)
    dx1, dx1_16, g_ffn_norm = _rmsnorm_bwd(dh2, x1, small["ffn_norm_w"], dy, "norm2_bwd", after=behind)
    dmix = _matmul(dx1_16, w_out, "nt", "out_proj_dx")
    g_w_out_a = _matmul(mix_a, dx1_16, "tn", "out_proj_dw_a")
    g_w_out_b = _matmul(mix_b, dx1_16, "tn", "out_proj_dw_b")
    dob, dsum, g_attn_out_norm = _attn_merge_bwd(dmix, ob, small["attn_out_norm_w"], heads)
    grads = _attn_bwd(qk_rot, aqkv, dob, lse, dsum, heads)
    d_aqkv, g_q_norm, g_k_norm = _attn_pre_bwd(grads, aqkv, small["q_norm_w"], small["k_norm_w"], cosf, sinf, heads)
    do_d, dz, g_delta_out_norm = _delta_post_bwd(dmix, o_d, qkvz, small["delta_out_norm_w"], heads)
    ddqkv, dbd = _delta_chunk_bwd(do_d, dqkv, bd, states, a_log, dt_bias, heads)
    d_qkv_raw, g_conv = _delta_pre_bwd(ddqkv, qkvz, conv_w, heads)
    bd_sums = _colsum(dbd, "bd_colsum")
    g_w_qkv = _matmul(h1, d_qkv_raw, "tn", "proj_qkv_dw")
    g_w_z = _matmul(h1, dz, "tn", "proj_z_dw")
    g_w_bd = _matmul(h1, dbd, "tn", "proj_bd_dw")
    g_w_attn = _matmul(h1, d_aqkv, "tn", "proj_attn_dw")
    behind = rest_grads_ready(dict(w_qkv=g_w_qkv, w_z=g_w_z, w_bd=g_w_bd, w_attn=g_w_attn, w_out_a=g_w_out_a,
                                   w_out_b=g_w_out_b))
    dh1 = _matmul(d_qkv_raw, w_qkvz, "nt", "proj_qkv_dx", b_cols=(0, 3 * dw), after=behind)
    dh1 = _matmul(dz, w_qkvz, "nt", "proj_z_dx", b_cols=(3 * dw, dw), add=dh1)
    dh1 = _matmul(d_aqkv, w_attn, "nt", "proj_attn_dx", add=dh1)
    dh1 = _matmul(dbd, w_bd, "nt", "proj_bd_dx", add=dh1)
    grad_x, _, g_attn_norm = _rmsnorm_bwd(dh1, x, small["attn_norm_w"], dx1, "norm1_bwd")
    small_grads = dict(
        attn_norm_w=g_attn_norm, a_log=bd_sums[:, 2 * heads:3 * heads], dt_bias=bd_sums[:, heads:2 * heads],
        delta_out_norm_w=g_delta_out_norm, q_norm_w=g_q_norm, k_norm_w=g_k_norm,
        attn_out_norm_w=g_attn_out_norm, ffn_norm_w=g_ffn_norm, conv_w=g_conv)
    return loss_row, grad_x, small_grads


def _adamw(w, g, m, v, name):
    r, c = w.shape
    tr = _tile(r, 256, 8)

    def body(w_ref, g_ref, m_ref, v_ref, d_ref, nm_ref, nv_ref):
        gv = g_ref[...]
        nm = ADAM_B1 * m_ref[...] + (1.0 - ADAM_B1) * gv
        nv = ADAM_B2 * v_ref[...] + (1.0 - ADAM_B2) * (gv * gv)
        m_hat = nm / (1.0 - ADAM_B1 ** ADAM_STEP)
        v_hat = nv / (1.0 - ADAM_B2 ** ADAM_STEP)
        d_ref[...] = -ADAM_LR * (m_hat / (jnp.sqrt(v_hat) + ADAM_EPS) + ADAM_WD * w_ref[...])
        nm_ref[...] = nm
        nv_ref[...] = nv

    blk = pl.BlockSpec((tr, c), lambda i: (i, 0))
    return pl.pallas_call(
        body, name=name, grid=(r // tr,),
        in_specs=[blk] * 4, out_specs=[blk] * 3,
        out_shape=[jax.ShapeDtypeStruct((r, c), F32)] * 3,
        compiler_params=_params(("parallel",)),
    )(w, g, m, v)


def _add_half_bf16(g, b, place, name):
    n, half, c = b.shape
    tr = _tile(half, 512, 16)
    nb = half // tr

    def body(place_ref, g_ref, b_ref, o_ref):
        o_ref[...] = (g_ref[...].astype(F32) + b_ref[...].astype(F32)).astype(BF16)

    blk = pl.BlockSpec((1, tr, c), lambda i, j, p: (i, j, 0))
    return pl.pallas_call(
        body, name=name,
        grid_spec=pltpu.PrefetchScalarGridSpec(
            num_scalar_prefetch=1, grid=(n, nb),
            in_specs=[pl.BlockSpec((1, tr, c), lambda i, j, p: (i, p[0] * nb + j, 0)), blk], out_specs=blk),
        out_shape=jax.ShapeDtypeStruct((n, half, c), BF16),
        compiler_params=_params(("parallel", "parallel")),
    )(place, g, b)


def _sum4_f32(mine, others, place, name):
    _, half, c = mine.shape
    tr = _tile(half, 512, 16)
    nb = half // tr

    def body(place_ref, a_ref, b_ref, o_ref):
        acc = a_ref[0].astype(F32)
        for j in range(3):
            acc = acc + b_ref[j].astype(F32)
        o_ref[...] = acc

    return pl.pallas_call(
        body, name=name,
        grid_spec=pltpu.PrefetchScalarGridSpec(
            num_scalar_prefetch=1, grid=(nb,),
            in_specs=[pl.BlockSpec((1, tr, c), lambda i, p: (p[1], i, 0)),
                      pl.BlockSpec((3, tr, c), lambda i, p: (0, i, 0))],
            out_specs=pl.BlockSpec((tr, c), lambda i, p: (p[0] * nb + i, 0))),
        out_shape=jax.ShapeDtypeStruct((2 * half, c), F32),
        compiler_params=_params(("parallel",)),
    )(place, mine, others)


def _place():
    x, y, c = lax.axis_index("x"), lax.axis_index("y"), lax.axis_index("c")
    other_chips = [(1 - x, y), (x, 1 - y), (1 - x, 1 - y)]
    return x, y, c, (x, y, 1 - c), other_chips


ANY = pl.BlockSpec(memory_space=pl.ANY)


def _remote(k, src, dst, to, send_sems, recv_sems):
    return pltpu.make_async_remote_copy(src_ref=src, dst_ref=dst, send_sem=send_sems.at[k], recv_sem=recv_sems.at[k],
                                        device_id=to, device_id_type=MESH)


def _half(ref, lead, hc):
    if lead is None:
        half = ref.shape[0] // 2
        return ref.at[pl.ds(hc * half, half), :]
    half = ref.shape[1] // 2
    return ref.at[lead, pl.ds(hc * half, half), :]


def _all_gather_weights(slots, whole, name):
    nt, nw = len(slots), len(whole)
    base_w, base_f, base_d = 2 * nt, 2 * nt + 3 * nw, 4 * nt + 3 * nw

    def quarter(ref, lead, hc, q):
        quart = ref.shape[1] // 4
        return ref.at[lead, pl.ds((2 * hc + q) * quart, quart), :]

    def body(*refs):
        ins, outs = refs[:nt + nw], refs[nt + nw:2 * (nt + nw)]
        sems = refs[2 * (nt + nw):]
        x, y, c, sibling, chips = _place()
        me, xn, yn, dg = 2 * x + y, 2 * (1 - x) + y, 2 * x + 1 - y, 2 * (1 - x) + 1 - y
        to_x, to_y = (1 - x, y, c), (x, 1 - y, c)
        cps = []
        for t in range(nt):
            cps.append(_remote(2 * t, _half(ins[t], me, c), _half(outs[t], me, c), to_x, *sems))
            cps.append(_remote(2 * t + 1, _half(ins[t], me, c), _half(outs[t], me, c), to_y, *sems))
        for j, (px, py) in enumerate(chips):
            for t in range(nw):
                cps.append(_remote(base_w + j * nw + t, ins[nt + t].at[me], outs[nt + t].at[me], (px, py, c), *sems))
        for cp in cps:
            cp.start()

        def start(k, ref, to):
            cp = _remote(k, ref, ref, to, *sems)
            cp.start()
            cps.append(cp)

        for t in range(nt):
            landed = _half(outs[t], xn, c)
            _remote(2 * t, landed, landed, to_x, *sems).wait_recv()
            start(base_f + 2 * t, quarter(outs[t], xn, c, 0), to_y)
            start(base_d + 3 * t, landed, sibling)
            landed = _half(outs[t], yn, c)
            _remote(2 * t + 1, landed, landed, to_y, *sems).wait_recv()
            start(base_f + 2 * t + 1, quarter(outs[t], yn, c, 1), to_x)
            start(base_d + 3 * t + 1, landed, sibling)
        for t in range(nt):
            q0, q1 = quarter(outs[t], dg, c, 0), quarter(outs[t], dg, c, 1)
            _remote(base_f + 2 * t, q0, q0, to_y, *sems).wait_recv()
            _remote(base_f + 2 * t + 1, q1, q1, to_x, *sems).wait_recv()
            start(base_d + 3 * t + 2, _half(outs[t], dg, c), sibling)
        for j, (px, py) in enumerate(chips):
            for t in range(nw):
                landed = outs[nt + t].at[2 * px + py]
                _remote(base_w + j * nw + t, landed, landed, (px, py, c), *sems).wait_recv()
        for t in range(nt):
            for j, chip in enumerate((xn, yn, dg)):
                other = _half(outs[t], chip, 1 - c)
                _remote(base_d + 3 * t + j, other, other, sibling, *sems).wait_recv()
        for cp in cps:
            cp.wait_send()

    arrays = list(slots) + list(whole)
    n_sem = 7 * nt + 3 * nw
    return pl.pallas_call(
        body, name=name, in_specs=[ANY] * len(arrays), out_specs=[ANY] * len(arrays),
        input_output_aliases={i: i for i in range(len(arrays))},
        out_shape=[jax.ShapeDtypeStruct(a.shape, a.dtype) for a in arrays],
        scratch_shapes=[pltpu.SemaphoreType.DMA((n_sem,)), pltpu.SemaphoreType.DMA((n_sem,))],
    )(*arrays)


def _swap_halves_with_sibling(gs, name):
    nt = len(gs)

    def body(*refs):
        g_refs, o_refs, (send_sems, recv_sems) = refs[:nt], refs[nt:2 * nt], refs[2 * nt:]
        _, _, c, sibling, _ = _place()
        cps = []
        for t in range(nt):
            half = g_refs[t].shape[1] // 2
            cps.append(_remote(t, g_refs[t].at[:, pl.ds((1 - c) * half, half), :], o_refs[t], sibling,
                               send_sems, recv_sems))
        for cp in cps:
            cp.start()
        for cp in cps:
            cp.wait()

    return pl.pallas_call(
        body, name=name, in_specs=[ANY] * nt, out_specs=[ANY] * nt,
        out_shape=[jax.ShapeDtypeStruct((g.shape[0], g.shape[1] // 2, g.shape[2]), g.dtype) for g in gs],
        scratch_shapes=[pltpu.SemaphoreType.DMA((nt,)), pltpu.SemaphoreType.DMA((nt,))],
    )(*gs)


def _scatter_to_chips(ps, name):
    nt = len(ps)

    def body(*refs):
        p_refs, b_refs, (send_sems, recv_sems) = refs[:nt], refs[nt:2 * nt], refs[2 * nt:]
        _, _, c, _, chips = _place()
        cps = [_remote(j * nt + t, p_refs[t].at[2 * px + py], b_refs[t].at[j], (px, py, c), send_sems, recv_sems)
               for j, (px, py) in enumerate(chips) for t in range(nt)]
        for cp in cps:
            cp.start()
        for cp in cps:
            cp.wait()

    return pl.pallas_call(
        body, name=name, in_specs=[ANY] * nt, out_specs=[ANY] * nt,
        out_shape=[jax.ShapeDtypeStruct((3,) + p.shape[1:], p.dtype) for p in ps],
        scratch_shapes=[pltpu.SemaphoreType.DMA((3 * nt,)), pltpu.SemaphoreType.DMA((3 * nt,))],
    )(*ps)


def _join_halves(fs, name):
    nt = len(fs)

    def body(*refs):
        in_refs, out_refs, (send_sems, recv_sems) = refs[:nt], refs[nt:2 * nt], refs[2 * nt:]
        _, _, c, sibling, _ = _place()
        cps = [_remote(t, _half(in_refs[t], None, c), _half(out_refs[t], None, c), sibling, send_sems, recv_sems)
               for t in range(nt)]
        for cp in cps:
            cp.start()
        for t in range(nt):
            theirs = _half(out_refs[t], None, 1 - c)
            _remote(t, theirs, theirs, sibling, send_sems, recv_sems).wait_recv()
        for cp in cps:
            cp.wait_send()

    return pl.pallas_call(
        body, name=name, in_specs=[ANY] * nt, out_specs=[ANY] * nt,
        input_output_aliases={i: i for i in range(nt)},
        out_shape=[jax.ShapeDtypeStruct(f.shape, f.dtype) for f in fs],
        scratch_shapes=[pltpu.SemaphoreType.DMA((nt,)), pltpu.SemaphoreType.DMA((nt,))],
    )(*fs)


HBM = pl.BlockSpec(memory_space=pltpu.HBM)
SEM = pl.BlockSpec(memory_space=pltpu.SEMAPHORE)
EFFECT = pltpu.SideEffectType.DATAFLOW_SIDE_EFFECTING


def _split_start(arrays, after, plan, n_copies, name):
    na = len(arrays)

    def body(*refs):
        ins, send_sems, recv_sems = refs[:na], refs[na + 1], refs[na + 2]
        outs, token = refs[na + 3:2 * na + 3], refs[2 * na + 3]
        for k, (src, dst, to) in enumerate(plan(ins, outs)):
            _remote(k, src, dst, to, send_sems, recv_sems).start()
        token[...] = jnp.zeros_like(token)

    res = pl.pallas_call(
        body, name=name,
        out_shape=(pltpu.SemaphoreType.DMA((n_copies,)), pltpu.SemaphoreType.DMA((n_copies,)),
                   *[pltpu.HBM(a.shape, a.dtype) for a in arrays], jax.ShapeDtypeStruct((8, HEAD_DIM), F32)),
        in_specs=[HBM] * na + [ANY],
        out_specs=(SEM, SEM, *[HBM] * na, pl.BlockSpec(memory_space=pltpu.VMEM)),
        input_output_aliases={i: 2 + i for i in range(na)},
        compiler_params=pltpu.CompilerParams(has_side_effects=EFFECT),
    )(*[pltpu.with_memory_space_constraint(a, pltpu.HBM) for a in arrays], after)
    return res[0], res[1], list(res[2:2 + na]), res[2 + na]


def _split_wait(send_sems, recv_sems, arrays, after, plan, name):
    na = len(arrays)
    after = list(after) if isinstance(after, (list, tuple)) else [after]

    def body(*refs):
        ins, send, recv, outs = refs[:na], refs[na], refs[na + 1], refs[na + 2 + len(after):]
        for k, (src, dst, to) in enumerate(plan(ins, outs)):
            cp = _remote(k, src, dst, to, send, recv)
            cp.wait_send()
            cp.wait_recv()

    res = pl.pallas_call(
        body, name=name, out_shape=tuple(pltpu.HBM(a.shape, a.dtype) for a in arrays),
        in_specs=[HBM] * na + [SEM, SEM] + [ANY] * len(after), out_specs=tuple([HBM] * na),
        input_output_aliases={i: i for i in range(na)},
        compiler_params=pltpu.CompilerParams(has_side_effects=EFFECT),
    )(*arrays, send_sems, recv_sems, *after)
    return list(res)


def _gather_plan(nt):
    def plan(ins, outs):
        x, y, c, _, chips = _place()
        me = 2 * x + y
        return [(_half(ins[t], me, c), _half(outs[t], me, c), (px, py, c)) for px, py in chips for t in range(nt)]
    return plan


def _gather_landed_plan(nt):
    def plan(ins, outs):
        _, _, c, _, chips = _place()
        return [(_half(outs[t], 2 * px + py, c), _half(outs[t], 2 * px + py, c), (px, py, c))
                for px, py in chips for t in range(nt)]
    return plan


def _scatter_plan(nt):
    def plan(ins, outs):
        _, _, c, _, chips = _place()
        return [(ins[t].at[2 * px + py], outs[nt + t].at[j], (px, py, c))
                for j, (px, py) in enumerate(chips) for t in range(nt)]
    return plan


def _pass_plan(nt):
    def plan(ins, outs):
        _, _, c, sibling, chips = _place()
        return [(_half(ins[t], 2 * px + py, c), _half(outs[t], 2 * px + py, c), sibling)
                for px, py in chips for t in range(nt)]
    return plan


def _pass_landed_plan(nt):
    def plan(ins, outs):
        _, _, c, sibling, chips = _place()
        return [(_half(outs[t], 2 * px + py, c), _half(outs[t], 2 * px + py, 1 - c), sibling)
                for px, py in chips for t in range(nt)]
    return plan


def _swap_plan(nt):
    def plan(ins, outs):
        _, _, c, sibling, _ = _place()
        res = []
        for t in range(nt):
            half = ins[t].shape[1] // 2
            res.append((ins[t].at[:, pl.ds((1 - c) * half, half), :], outs[nt + t], sibling))
        return res
    return plan


def _pass_halves_to_sibling(slots, name):
    nt = len(slots)

    def body(*refs):
        ins, outs, (send_sems, recv_sems) = refs[:nt], refs[nt:2 * nt], refs[2 * nt:]
        _, _, c, sibling, chips = _place()
        cps = [_remote(j * nt + t, _half(ins[t], 2 * px + py, c), _half(outs[t], 2 * px + py, c), sibling,
                       send_sems, recv_sems)
               for j, (px, py) in enumerate(chips) for t in range(nt)]
        for cp in cps:
            cp.start()
        for j, (px, py) in enumerate(chips):
            for t in range(nt):
                other = _half(outs[t], 2 * px + py, 1 - c)
                _remote(j * nt + t, other, other, sibling, send_sems, recv_sems).wait_recv()
        for cp in cps:
            cp.wait_send()

    return pl.pallas_call(
        body, name=name, in_specs=[ANY] * nt, out_specs=[ANY] * nt,
        input_output_aliases={i: i for i in range(nt)},
        out_shape=[jax.ShapeDtypeStruct(a.shape, a.dtype) for a in slots],
        scratch_shapes=[pltpu.SemaphoreType.DMA((3 * nt,)), pltpu.SemaphoreType.DMA((3 * nt,))],
    )(*slots)


def _all_reduce_small(v):
    r, lanes = v.shape

    def body(v_ref, out_ref, buf, send_sems, recv_sems):
        x, y, c, sibling, chips = _place()

        def slot(px, py, pc):
            return buf.at[4 * px + 2 * py + pc]

        def copy(k, block, to, src=None):
            return pltpu.make_async_remote_copy(src_ref=slot(*block) if src is None else src, dst_ref=slot(*block),
                                                send_sem=send_sems.at[k], recv_sem=recv_sems.at[k],
                                                device_id=to, device_id_type=MESH)

        me = (x, y, c)
        buf[4 * x + 2 * y + c] = v_ref[...]
        first = [copy(0, me, sibling, src=v_ref)]
        first += [copy(1 + j, me, (*chip, c), src=v_ref) for j, chip in enumerate(chips)]
        for cp in first:
            cp.start()
        passed = [copy(4 + j, (*chip, c), sibling) for j, chip in enumerate(chips)]
        for j, chip in enumerate(chips):
            copy(1 + j, (*chip, c), me).wait_recv()
            passed[j].start()
        copy(0, (x, y, 1 - c), me).wait_recv()
        for j, chip in enumerate(chips):
            copy(4 + j, (*chip, 1 - c), me).wait_recv()
        for cp in first + passed:
            cp.wait_send()
        acc = buf[0]
        for k in range(1, 8):
            acc = acc + buf[k]
        out_ref[...] = acc

    vmem = pl.BlockSpec(memory_space=pltpu.VMEM)
    return pl.pallas_call(
        body, name="all_reduce_small", in_specs=[vmem], out_specs=vmem,
        out_shape=jax.ShapeDtypeStruct((r, lanes), F32),
        scratch_shapes=[pltpu.VMEM((8, r, lanes), F32), pltpu.SemaphoreType.DMA((7,)), pltpu.SemaphoreType.DMA((7,))],
    )(v)


def _size(shape):
    n = 1
    for d in shape:
        n *= d
    return n


def _pack_small(parts):
    rows = []
    for p in parts:
        f = p.reshape(-1).astype(F32)
        n = -(-f.shape[0] // HEAD_DIM) * HEAD_DIM
        rows.append(jnp.pad(f, (0, n - f.shape[0])).reshape(-1, HEAD_DIM))
    a = jnp.concatenate(rows, axis=0)
    return jnp.pad(a, ((0, -a.shape[0] % 8), (0, 0)))


def _unpack_small(a, shapes):
    out, row = [], 0
    for shp in shapes:
        nrows = -(-_size(shp) // HEAD_DIM)
        out.append(a[row:row + nrows].reshape(-1)[:_size(shp)].reshape(shp))
        row += nrows
    return out


SMALL = ["attn_norm_w", "a_log", "dt_bias", "delta_out_norm_w", "q_norm_w", "k_norm_w", "attn_out_norm_w", "ffn_norm_w"]
BIG = ["w_in", "w_out", "w_gate_up", "w_down"]
ORDER = ["attn_norm_w", "w_in", "conv_w", "a_log", "dt_bias", "delta_out_norm_w", "q_norm_w", "k_norm_w",
         "attn_out_norm_w", "w_out", "ffn_norm_w", "w_gate_up", "w_down"]


def kernel(x, positions, attn_norm_w, w_in, conv_w, a_log, dt_bias, delta_out_norm_w, q_norm_w, k_norm_w, attn_out_norm_w, w_out, ffn_norm_w, w_gate_up, w_down, loss_target, m_attn_norm_w, m_w_in, m_conv_w, m_a_log, m_dt_bias, m_delta_out_norm_w, m_q_norm_w, m_k_norm_w, m_attn_out_norm_w, m_w_out, m_ffn_norm_w, m_w_gate_up, m_w_down, v_attn_norm_w, v_w_in, v_conv_w, v_a_log, v_dt_bias, v_delta_out_norm_w, v_q_norm_w, v_k_norm_w, v_attn_out_norm_w, v_w_out, v_ffn_norm_w, v_w_gate_up, v_w_down):
    wts = dict(attn_norm_w=attn_norm_w, w_in=w_in, conv_w=conv_w, a_log=a_log, dt_bias=dt_bias,
               delta_out_norm_w=delta_out_norm_w, q_norm_w=q_norm_w, k_norm_w=k_norm_w,
               attn_out_norm_w=attn_out_norm_w, w_out=w_out, ffn_norm_w=ffn_norm_w, w_gate_up=w_gate_up, w_down=w_down)
    mom = dict(attn_norm_w=m_attn_norm_w, w_in=m_w_in, conv_w=m_conv_w, a_log=m_a_log, dt_bias=m_dt_bias,
               delta_out_norm_w=m_delta_out_norm_w, q_norm_w=m_q_norm_w, k_norm_w=m_k_norm_w,
               attn_out_norm_w=m_attn_out_norm_w, w_out=m_w_out, ffn_norm_w=m_ffn_norm_w, w_gate_up=m_w_gate_up,
               w_down=m_w_down)
    var = dict(attn_norm_w=v_attn_norm_w, w_in=v_w_in, conv_w=v_conv_w, a_log=v_a_log, dt_bias=v_dt_bias,
               delta_out_norm_w=v_delta_out_norm_w, q_norm_w=v_q_norm_w, k_norm_w=v_k_norm_w,
               attn_out_norm_w=v_attn_out_norm_w, w_out=v_w_out, ffn_norm_w=v_ffn_norm_w, w_gate_up=v_w_gate_up,
               w_down=v_w_down)
    dmod = x.shape[2]
    heads = dmod // (2 * HEAD_DIM)
    dw = heads * HEAD_DIM
    chip = 2 * lax.axis_index("x") + lax.axis_index("y")
    core = lax.axis_index("c")
    n_in, n_out, n_gu, n_down, n_conv = (w_in.shape[2], w_out.shape[1], w_gate_up.shape[2], w_down.shape[1],
                                         conv_w.shape[2])

    def slots_of(w, dtype):
        shard = w[0].astype(dtype)
        return lax.dynamic_update_index_in_dim(lax.empty((4,) + shard.shape, dtype), shard, chip, axis=0)

    s_in, s_conv = _all_gather_weights([slots_of(w_in, BF16)], [slots_of(conv_w, F32)], "all_gather_w_in")
    later = [slots_of(w_out, BF16), slots_of(w_gate_up, BF16), slots_of(w_down, BF16)]
    w_send, w_recv, later, started = _split_start(later, s_conv, _gather_plan(3), 9, "gather_rest_start")
    by_cols = lambda a: a.transpose(1, 0, 2).reshape(a.shape[1], 4 * a.shape[2])
    w_in_f, conv_f = by_cols(s_in), by_cols(s_conv)
    w_bd = jnp.pad(w_in_f[:, 4 * dw:4 * dw + 2 * heads], ((0, 0), (0, HEAD_DIM - 2 * heads)))
    small = {n: wts[n] for n in SMALL}
    place = jnp.stack([core, chip]).astype(jnp.int32)
    to_slots = lambda a: a.reshape(a.shape[0], 4, a.shape[1] // 4).transpose(1, 0, 2)

    def later_weights(after):
        landed = _split_wait(w_send, w_recv, later, after, _gather_landed_plan(3), "gather_rest_wait")
        s_out, = _pass_halves_to_sibling(landed[:1], "gather_out_pass")
        p_send, p_recv, passing, token = _split_start(landed[1:], s_out, _pass_plan(2), 6, "gather_ffn_pass_start")

        def ffn_weights(after):
            s_gu, s_down = _split_wait(p_send, p_recv, passing, after, _pass_landed_plan(2), "gather_ffn_pass_wait")
            return by_cols(s_gu), s_down.reshape(4 * n_down, dmod)

        return s_out.reshape(4 * n_out, dmod), token, ffn_weights

    ffn = {}

    def ffn_grads_ready(g_gate, g_up, g_down):
        gs = [to_slots(jnp.concatenate([g_gate, g_up], axis=1)), g_down.reshape(4, n_down, dmod)]
        zones = [lax.empty((4, g.shape[1] // 2, g.shape[2]), BF16) for g in gs]
        s_send, s_recv, s_bufs, swapping = _split_start(gs + zones, g_gate, _swap_plan(2), 2, "swap_ffn_start")

        def and_then(after):
            g_gu, g_dn, b_gu, b_dn = _split_wait(s_send, s_recv, s_bufs, after, _swap_plan(2), "swap_ffn_wait")
            sums = [_add_half_bf16(g_gu, b_gu, place, "chip_partial_sum_w_gate_up"),
                    _add_half_bf16(g_dn, b_dn, place, "chip_partial_sum_w_down")]
            zones3 = [lax.empty((3,) + p.shape[1:], BF16) for p in sums]
            ffn["send"], ffn["recv"], ffn["bufs"], token = _split_start(sums + zones3, b_gu, _scatter_plan(2), 6,
                                                                        "scatter_ffn_start")
            return token

        return swapping, and_then

    rest = {}

    def rest_grads_ready(bg):
        gs = [to_slots(jnp.concatenate([bg["w_qkv"], bg["w_z"], bg["w_bd"][:, :2 * heads], bg["w_attn"]], axis=1)),
              jnp.concatenate([bg["w_out_a"], bg["w_out_b"]], axis=0).reshape(4, n_out, dmod)]
        from_sibling = _swap_halves_with_sibling(gs, "swap_grad_halves")
        sums = [_add_half_bf16(g, b, place, "chip_partial_sum_" + n)
                for g, b, n in zip(gs, from_sibling, ["w_in", "w_out"])]
        zones = [lax.empty((3,) + p.shape[1:], BF16) for p in sums]
        rest["send"], rest["recv"], rest["bufs"], token = _split_start(sums + zones, from_sibling[0], _scatter_plan(2),
                                                                       6, "scatter_rest_start")
        return token

    loss_row, grad_x, sg = _local_step(
        x[0], positions[0], loss_target[0], small, w_in_f[:, :4 * dw], w_bd, w_in_f[:, 4 * dw + 2 * heads:], conv_f,
        later_weights, ffn_grads_ready, rest_grads_ready, after=started)

    r_send, r_recv, r_bufs = rest["send"], rest["recv"], rest["bufs"]
    sum_gu, sum_down, got_gu, got_down = _split_wait(ffn["send"], ffn["recv"], ffn["bufs"], grad_x,
                                                     _scatter_plan(2), "scatter_ffn_wait")
    g_big = dict(zip(["w_gate_up", "w_down"], _join_halves(
        [_sum4_f32(sum_gu, got_gu, place, "grad_total_w_gate_up"),
         _sum4_f32(sum_down, got_down, place, "grad_total_w_down")], "join_ffn_halves")))
    grads, deltas, new_m, new_v = {}, {}, {}, {}

    def adamw_big(n):
        shp = wts[n].shape
        d, nm, nv = _adamw(wts[n][0], g_big[n], mom[n][0], var[n][0], "adamw_" + n)
        grads[n], deltas[n], new_m[n], new_v[n] = g_big[n].reshape(shp), d.reshape(shp), nm.reshape(shp), nv.reshape(shp)
        return d

    done = [adamw_big("w_gate_up"), adamw_big("w_down")]
    sum_in, sum_out, got_in, got_out = _split_wait(r_send, r_recv, r_bufs, done, _scatter_plan(2), "scatter_rest_wait")
    g_big.update(zip(["w_in", "w_out"], _join_halves(
        [_sum4_f32(sum_in, got_in, place, "grad_total_w_in"),
         _sum4_f32(sum_out, got_out, place, "grad_total_w_out")], "join_rest_halves")))
    adamw_big("w_in")
    adamw_big("w_out")

    reduced = _all_reduce_small(_pack_small([sg[n] for n in SMALL] + [sg["conv_w"], loss_row]))
    red = _unpack_small(reduced, [wts[n].shape for n in SMALL] + [(4, 4 * n_conv), (1, HEAD_DIM)])
    g_small = dict(zip(SMALL, red[:len(SMALL)]))
    g_conv_full, loss_out = red[len(SMALL)], red[len(SMALL) + 1]
    g_small["conv_w"] = lax.dynamic_slice_in_dim(g_conv_full, chip * n_conv, n_conv, axis=1).reshape(conv_w.shape)

    names = SMALL + ["conv_w"]
    shapes = [wts[n].shape for n in names]
    d, nm, nv = _adamw(_pack_small([wts[n] for n in names]), _pack_small([g_small[n] for n in names]),
                       _pack_small([mom[n] for n in names]), _pack_small([var[n] for n in names]), "adamw_small")
    for n, dd, mm, vv in zip(names, _unpack_small(d, shapes), _unpack_small(nm, shapes), _unpack_small(nv, shapes)):
        grads[n], deltas[n], new_m[n], new_v[n] = g_small[n], dd, mm, vv
    return (loss_out[0, 0], grad_x[None], *[grads[n] for n in ORDER], *[deltas[n] for n in ORDER],
            *[new_m[n] for n in ORDER], *[new_v[n] for n in ORDER])
```

```python
import functools

import jax
import jax.numpy as jnp
from jax import lax
from jax.experimental import pallas as pl
from jax.experimental.pallas import tpu as pltpu

F32 = jnp.float32
BF16 = jnp.bfloat16
HEAD_DIM = 128
CHUNK = 128
INV_BLOCK = 64
SPAN = 128
DILATIONS = (1, 4, 16)
ROPE_THETA = 10000.0
EPS = 1e-6
NEG = -1e30
ADAM_LR, ADAM_B1, ADAM_B2, ADAM_EPS, ADAM_WD, ADAM_STEP = 0.001, 0.9, 0.999, 1e-08, 0.01, 10
VMEM_LIMIT = 48 * 1024 * 1024
MESH = pl.DeviceIdType.MESH

_DN = {"nn": (((1,), (0,)), ((), ())), "nt": (((1,), (1,)), ((), ())), "tn": (((0,), (0,)), ((), ()))}


def _dot(a, b, mode="nn"):
    (ca, cb), _ = _DN[mode]
    if a.ndim == 3:
        dn = (((ca[0] + 1,), (cb[0] + 1,)), ((0,), (0,)))
    else:
        dn = _DN[mode]
    return lax.dot_general(a.astype(BF16), b.astype(BF16), dn, preferred_element_type=F32)


def _rsum(x):
    return jnp.sum(x, axis=-1, keepdims=True)


def _csum(x):
    return jnp.sum(x, axis=-2, keepdims=True)


def _tile(dim, pref, unit=128):
    t = (min(pref, dim) // unit) * unit
    while t >= unit:
        if dim % t == 0:
            return t
        t -= unit
    return dim


def _params(sem):
    return pltpu.CompilerParams(dimension_semantics=sem, vmem_limit_bytes=VMEM_LIMIT)


def _sigmoid(x):
    return 1.0 / (1.0 + jnp.exp(-x))


def _matmul(a, b, mode, name, add=None, out_dtype=F32, a_cols=None, b_cols=None, b_rows=None,
            tiles=(), finish=None, out_dtypes=(), row_sum=False, after=None, b2_cols=None):
    if mode == "tn":
        out_dtype = BF16
    a_off, a_w = a_cols if a_cols else (0, a.shape[1])
    b_off, b_w = b_cols if b_cols else (0, b.shape[1])
    br_off, br_n = b_rows if b_rows else (0, b.shape[0])
    if mode == "nn":
        m, k, n = a.shape[0], a_w, b_w
        assert br_n == k
    elif mode == "nt":
        m, k, n = a.shape[0], a_w, b.shape[0]
        assert b_w == k
    else:
        k, m, n = a.shape[0], a_w, b_w
        assert b.shape[0] == k
    tm, tn = _tile(m, 1024, 128), _tile(n, 1024, 128)
    tk = _tile(k, 2048 if a.dtype == BF16 and b.dtype == BF16 else 1024, 128)
    if mode == "nn":
        assert a_off % tk == 0 and b_off % tn == 0 and br_off % tk == 0
        a_spec = pl.BlockSpec((tm, tk), lambda i, j, kk: (i, kk + a_off // tk))
        b_spec = pl.BlockSpec((tk, tn), lambda i, j, kk: (kk + br_off // tk, j + b_off // tn))
    elif mode == "nt":
        assert a_off % tk == 0 and b_off % tk == 0
        a_spec = pl.BlockSpec((tm, tk), lambda i, j, kk: (i, kk + a_off // tk))
        b_spec = pl.BlockSpec((tn, tk), lambda i, j, kk: (j, kk + b_off // tk))
    else:
        assert a_off % tm == 0 and b_off % tn == 0
        a_spec = pl.BlockSpec((tk, tm), lambda i, j, kk: (kk, i + a_off // tm))
        b_spec = pl.BlockSpec((tk, tn), lambda i, j, kk: (kk, j + b_off // tn))
    nk = k // tk
    if finish is None:
        out_dtypes = [out_dtype]
        if add is None:
            finish = lambda acc, vals: (acc,)
        else:
            tiles = [(add, 0)]
            finish = lambda acc, vals: (acc + vals[0].astype(F32),)
    n_tiles, n_out = len(tiles), len(out_dtypes)
    extra = [] if after is None else [after]
    n_b = 1 if b2_cols is None else 2
    first_out = 1 + n_b + n_tiles + len(extra)

    def body(*refs):
        a_ref, b_refs = refs[0], refs[1:1 + n_b]
        tile_refs, out_refs = refs[1 + n_b:1 + n_b + n_tiles], refs[first_out:first_out + n_out]
        acc_refs = refs[-n_b:]
        kk = pl.program_id(2)
        first_tile = (pl.program_id(0) == 0) & (pl.program_id(1) == 0)

        @pl.when(kk == 0)
        def _():
            for acc_ref in acc_refs:
                acc_ref[...] = jnp.zeros_like(acc_ref)

        av = a_ref[...]
        for acc_ref, b_ref in zip(acc_refs, b_refs):
            acc_ref[...] += _dot(av, b_ref[...], mode)

        @pl.when(kk == nk - 1)
        def _():
            accs = [acc_ref[...] for acc_ref in acc_refs]
            res = finish(accs[0] if n_b == 1 else accs, [t[...] for t in tile_refs])
            for o_ref, r in zip(out_refs, res):
                o_ref[...] = r.astype(o_ref.dtype)
            if row_sum:
                row_ref = refs[first_out + n_out]

                @pl.when(first_tile)
                def _():
                    row_ref[...] = res[n_out]

                @pl.when(jnp.logical_not(first_tile))
                def _():
                    row_ref[...] += res[n_out]

    in_specs = [a_spec, b_spec]
    args = [a, b]
    if b2_cols is not None:
        assert mode == "nn" and b2_cols[1] == n and b2_cols[0] % tn == 0
        in_specs.append(pl.BlockSpec((tk, tn), lambda i, j, kk: (kk + br_off // tk, j + b2_cols[0] // tn)))
        args.append(b)
    for arr, off in tiles:
        assert off % tn == 0
        in_specs.append(pl.BlockSpec((tm, tn), lambda i, j, kk, off=off: (i, j + off // tn)))
        args.append(arr)
    in_specs += [ANY] * len(extra)
    args += extra
    out_specs = [pl.BlockSpec((tm, tn), lambda i, j, kk: (i, j))] * n_out
    out_shape = [jax.ShapeDtypeStruct((m, n), dt) for dt in out_dtypes]
    if row_sum:
        out_specs.append(pl.BlockSpec((1, HEAD_DIM), lambda i, j, kk: (0, 0)))
        out_shape.append(jax.ShapeDtypeStruct((1, HEAD_DIM), F32))
    res = pl.pallas_call(
        body, name=name, grid=(m // tm, n // tn, nk),
        in_specs=in_specs, out_specs=out_specs, out_shape=out_shape,
        scratch_shapes=[pltpu.VMEM((tm, tn), F32)] * n_b,
        compiler_params=_params(("arbitrary",) * 3 if row_sum else ("parallel", "parallel", "arbitrary")),
    )(*args)
    return res[0] if len(res) == 1 else res


def _rmsnorm_fwd(x, w, name, after=None):
    s, d = x.shape
    tr = _tile(s, 512, 8)

    def body(x_ref, w_ref, *rest):
        h_ref = rest[-1]
        xv = x_ref[...]
        rstd = lax.rsqrt(jnp.mean(xv * xv, axis=-1, keepdims=True) + EPS)
        h_ref[...] = (xv * rstd * w_ref[...]).astype(BF16)

    extra = [] if after is None else [after]
    return pl.pallas_call(
        body, name=name, grid=(s // tr,),
        in_specs=[pl.BlockSpec((tr, d), lambda i: (i, 0)), pl.BlockSpec((1, d), lambda i: (0, 0))] + [ANY] * len(extra),
        out_specs=pl.BlockSpec((tr, d), lambda i: (i, 0)),
        out_shape=jax.ShapeDtypeStruct((s, d), BF16),
        compiler_params=_params(("parallel",)),
    )(x, w, *extra)


def _rmsnorm_bwd(dh, x, w, res, name, after=None):
    s, d = x.shape
    tr = _tile(s, 256, 8)

    def body(dh_ref, x_ref, w_ref, res_ref, *rest):
        dx_ref, dx16_ref, dw_ref = rest[-3:]
        xv = x_ref[...]
        rstd = lax.rsqrt(jnp.mean(xv * xv, axis=-1, keepdims=True) + EPS)
        xhat = xv * rstd
        dhv = dh_ref[...]
        gw = dhv * w_ref[...]
        dx = res_ref[...] + rstd * (gw - xhat * jnp.mean(gw * xhat, axis=-1, keepdims=True))
        dx_ref[...] = dx
        dx16_ref[...] = dx.astype(BF16)

        @pl.when(pl.program_id(0) == 0)
        def _():
            dw_ref[...] = jnp.zeros_like(dw_ref)

        dw_ref[...] += jnp.sum(dhv * xhat, axis=0, keepdims=True)

    row = pl.BlockSpec((tr, d), lambda i: (i, 0))
    vec = pl.BlockSpec((1, d), lambda i: (0, 0))
    extra = [] if after is None else [after]
    return pl.pallas_call(
        body, name=name, grid=(s // tr,),
        in_specs=[row, row, vec, row] + [ANY] * len(extra), out_specs=[row, row, vec],
        out_shape=[jax.ShapeDtypeStruct((s, d), F32), jax.ShapeDtypeStruct((s, d), BF16),
                   jax.ShapeDtypeStruct((1, d), F32)],
        compiler_params=_params(("arbitrary",)),
    )(dh, x, w, res, *extra)


def _conv_taps(x, w, rows):
    shifted = [x]
    for sft in (1, 2, 3):
        shifted.append(jnp.where(rows >= sft, pltpu.roll(x, sft, 0), 0.0))
    y = w[3:4, :] * shifted[0] + w[2:3, :] * shifted[1] + w[1:2, :] * shifted[2] + w[0:1, :] * shifted[3]
    return y, shifted


def _delta_pre_fwd(qkvz, conv_w, heads):
    s = qkvz.shape[0]
    nblk = 3 * heads

    def body(x_ref, w_ref, o_ref):
        part = pl.program_id(0) // heads
        rows = lax.broadcasted_iota(jnp.int32, (s, HEAD_DIM), 0)
        y, _ = _conv_taps(x_ref[...], w_ref[...], rows)
        a = y * _sigmoid(y)
        rs = lax.rsqrt(jnp.sum(a * a, axis=-1, keepdims=True) + EPS)
        fac = jnp.where(part == 0, rs * (HEAD_DIM ** -0.5), jnp.where(part == 1, rs, 1.0))
        o_ref[...] = a * fac

    return pl.pallas_call(
        body, name="delta_pre_fwd", grid=(nblk,),
        in_specs=[pl.BlockSpec((s, HEAD_DIM), lambda i: (0, i)), pl.BlockSpec((4, HEAD_DIM), lambda i: (0, i))],
        out_specs=pl.BlockSpec((s, HEAD_DIM), lambda i: (0, i)),
        out_shape=jax.ShapeDtypeStruct((s, 3 * heads * HEAD_DIM), F32),
        compiler_params=_params(("parallel",)),
    )(qkvz, conv_w)


def _delta_pre_bwd(dqkv, qkvz, conv_w, heads):
    s = qkvz.shape[0]
    nblk = 3 * heads

    def body(d_ref, x_ref, w_ref, dx_ref, dw_ref):
        part = pl.program_id(0) // heads
        rows = lax.broadcasted_iota(jnp.int32, (s, HEAD_DIM), 0)
        w = w_ref[...]
        y, shifted = _conv_taps(x_ref[...], w, rows)
        sg = _sigmoid(y)
        a = y * sg
        rs = lax.rsqrt(jnp.sum(a * a, axis=-1, keepdims=True) + EPS)
        unit = a * rs
        dn = d_ref[...]
        scale = jnp.where(part == 0, HEAD_DIM ** -0.5, 1.0)
        da_norm = scale * rs * (dn - unit * jnp.sum(dn * unit, axis=-1, keepdims=True))
        da = jnp.where(part < 2, da_norm, dn)
        dy = da * sg * (1.0 + y * (1.0 - sg))
        dx = w[3:4, :] * dy
        for sft in (1, 2, 3):
            dx = dx + w[3 - sft:4 - sft, :] * jnp.where(rows < s - sft, pltpu.roll(dy, s - sft, 0), 0.0)
        dx_ref[...] = dx.astype(BF16)
        for sft in range(4):
            dw_ref[3 - sft:4 - sft, :] = jnp.sum(dy * shifted[sft], axis=0, keepdims=True)

    col = pl.BlockSpec((s, HEAD_DIM), lambda i: (0, i))
    wsp = pl.BlockSpec((4, HEAD_DIM), lambda i: (0, i))
    return pl.pallas_call(
        body, name="delta_pre_bwd", grid=(nblk,),
        in_specs=[col, col, wsp], out_specs=[col, wsp],
        out_shape=[jax.ShapeDtypeStruct((s, 3 * heads * HEAD_DIM), BF16),
                   jax.ShapeDtypeStruct((4, 3 * heads * HEAD_DIM), F32)],
        compiler_params=_params(("parallel",)),
    )(dqkv, qkvz, conv_w)


def _heads_of(ref, heads):
    return jnp.stack([ref[:, h * HEAD_DIM:(h + 1) * HEAD_DIM] for h in range(heads)])


def _chunk_common(q, k, v, bd, a_log, dt_bias, heads, solved=None):
    c = CHUNK
    braw = jnp.stack([bd[:, h:h + 1] for h in range(heads)])
    draw = jnp.stack([bd[:, heads + h:heads + h + 1] for h in range(heads)])
    beta = _sigmoid(braw)
    xd = draw + dt_bias
    sp = jnp.maximum(xd, 0.0) + jnp.log1p(jnp.exp(-jnp.abs(xd)))
    g = -jnp.exp(a_log) * sp
    row = lax.broadcasted_iota(jnp.int32, (c, c), 0)
    col = lax.broadcasted_iota(jnp.int32, (c, c), 1)
    sq = (heads, c, c)
    g_b = jnp.broadcast_to(g, sq)
    g_row = _csum(jnp.where(row == col, g_b, 0.0))
    gam_col = _rsum(jnp.where(col <= row, jnp.broadcast_to(g_row, sq), 0.0))
    gam_row = _csum(jnp.where(row <= col, g_b, 0.0))
    causal = row >= col
    dm = jnp.where(causal, jnp.exp(jnp.where(causal, gam_col - gam_row, 0.0)), 0.0)
    kk = _dot(k, k, "nt")
    e = jnp.exp(gam_col)
    if solved is None:
        low = jnp.where(row > col, beta * kk * dm, 0.0)
        assert c in (INV_BLOCK, 2 * INV_BLOCK)
        same = (row // INV_BLOCK) == (col // INV_BLOCK)
        diag = jnp.where(same, low, 0.0)
        t = jnp.where(row == col, 1.0, 0.0) - diag
        pw = diag
        for _ in range((INV_BLOCK - 1).bit_length() - 1):
            pw = _dot(pw, pw)
            t = t + _dot(t, pw)
        if c > INV_BLOCK:
            t = t - _dot(_dot(t, low - diag), t)
        u = _dot(t, beta * v)
        w = _dot(t, (beta * e) * k)
    else:
        t, u, w = solved
    qk_raw = _dot(q, k, "nt")
    gl = _csum(g)
    el = jnp.exp(gl - gam_col)
    return dict(beta=beta, xd=xd, g=g, row=row, col=col, dm=dm, kk=kk, t=t, e=e, u=u, w=w,
                qk_raw=qk_raw, qk=qk_raw * dm, gl=gl, el=el, qd=e * q, kd=el * k, cd=jnp.exp(gl))


def _delta_chunk_fwd(qkv, bd, a_log, dt_bias, heads):
    s = qkv.shape[0]
    n = s // CHUNK
    dw = heads * HEAD_DIM
    blk = lambda part: pl.BlockSpec((CHUNK, dw), lambda i: (i, part))

    def body(q_ref, k_ref, v_ref, bd_ref, al_ref, dt_ref, o_ref, st_ref, t_ref, uw_ref, state):
        @pl.when(pl.program_id(0) == 0)
        def _():
            state[...] = jnp.zeros_like(state)

        cm = _chunk_common(_heads_of(q_ref, heads), _heads_of(k_ref, heads), _heads_of(v_ref, heads), bd_ref[...],
                           al_ref[...], dt_ref[...], heads)
        st = state[...]
        st_ref[0] = st
        t_ref[0] = cm["t"]
        uw_ref[0, 0] = cm["u"]
        uw_ref[0, 1] = cm["w"]
        vn = cm["u"] - _dot(cm["w"], st)
        o = _dot(cm["qd"], st) + _dot(cm["qk"], vn)
        for h in range(heads):
            o_ref[:, h * HEAD_DIM:(h + 1) * HEAD_DIM] = o[h]
        state[...] = cm["cd"] * st + _dot(cm["kd"], vn, "tn")

    smem = pl.BlockSpec((heads, 1, 1), lambda i: (0, 0, 0))
    return pl.pallas_call(
        body, name="delta_chunk_fwd", grid=(n,),
        in_specs=[blk(0), blk(1), blk(2), pl.BlockSpec((CHUNK, HEAD_DIM), lambda i: (i, 0)), smem, smem],
        out_specs=[pl.BlockSpec((CHUNK, dw), lambda i: (i, 0)),
                   pl.BlockSpec((1, heads, HEAD_DIM, HEAD_DIM), lambda i: (i, 0, 0, 0)),
                   pl.BlockSpec((1, heads, CHUNK, CHUNK), lambda i: (i, 0, 0, 0)),
                   pl.BlockSpec((1, 2, heads, CHUNK, HEAD_DIM), lambda i: (i, 0, 0, 0, 0))],
        out_shape=[jax.ShapeDtypeStruct((s, dw), F32),
                   jax.ShapeDtypeStruct((n, heads, HEAD_DIM, HEAD_DIM), F32),
                   jax.ShapeDtypeStruct((n, heads, CHUNK, CHUNK), F32),
                   jax.ShapeDtypeStruct((n, 2, heads, CHUNK, HEAD_DIM), F32)],
        scratch_shapes=[pltpu.VMEM((heads, HEAD_DIM, HEAD_DIM), F32)],
        compiler_params=_params(("arbitrary",)),
    )(qkv, qkv, qkv, bd, a_log, dt_bias)


def _delta_chunk_bwd(do, qkv, bd, saved, a_log, dt_bias, heads):
    s = qkv.shape[0]
    n = s // CHUNK
    dw = heads * HEAD_DIM
    c = CHUNK
    blk = lambda part: pl.BlockSpec((CHUNK, dw), lambda i: (n - 1 - i, part))

    def all_heads(q, k, v, dov, st, solved, dsn, bd, a_log, dt_bias):
        cm = _chunk_common(q, k, v, bd, a_log, dt_bias, heads, solved)
        beta, e, dm, row, col = cm["beta"], cm["e"], cm["dm"], cm["row"], cm["col"]
        sq = (heads, c, c)
        vn = cm["u"] - _dot(cm["w"], st)
        dvn = _dot(cm["kd"], dsn)
        dkd = _dot(vn, dsn, "nt")
        dcd = _csum(_rsum(st * dsn))
        ds = cm["cd"] * dsn
        dqd = _dot(dov, st, "nt")
        ds = ds + _dot(cm["qd"], dov, "tn")
        dqk = _dot(dov, vn, "nt")
        dvn = dvn + _dot(cm["qk"], dov, "tn")
        dw_ = -_dot(dvn, st, "nt")
        ds = ds - _dot(cm["w"], dvn, "tn")
        drhs_u = _dot(cm["t"], dvn, "tn")
        drhs_w = _dot(cm["t"], dw_, "tn")
        da = -(_dot(drhs_u, cm["u"], "nt") + _dot(drhs_w, cm["w"], "nt"))
        dl = jnp.where(row > col, da, 0.0)
        dbeta = _rsum(dl * cm["kk"] * dm)
        dkk = dl * beta * dm
        dd = dl * beta * cm["kk"]
        dv = beta * drhs_u
        ek = e * k
        dbeta = dbeta + _rsum(drhs_u * v) + _rsum(drhs_w * ek)
        dk = (beta * e) * drhs_w
        dgam = _rsum(drhs_w * (beta * ek))
        dqkm = dqk * dm
        dq = _dot(dqkm, k)
        dk = dk + _dot(dqkm, q, "tn")
        dd = dd + dqk * cm["qk_raw"]
        dk = dk + _dot(dkk, k) + _dot(dkk, k, "tn")
        dq = dq + e * dqd
        dgam = dgam + _rsum(dqd * cm["qd"])
        dk = dk + cm["el"] * dkd
        r = _rsum(dkd * cm["kd"])
        dgam = dgam - r
        dgl = _csum(r) + dcd * cm["cd"]
        mm = dd * dm
        colsum_c = _rsum(jnp.where(row == col, jnp.broadcast_to(_csum(mm), sq), 0.0))
        dgam = dgam + _rsum(mm) - colsum_c
        ridx = lax.broadcasted_iota(jnp.int32, (c, 1), 0)
        dgam = dgam + jnp.where(ridx == c - 1, dgl, 0.0)
        dgam_row = _csum(jnp.where(row == col, jnp.broadcast_to(dgam, sq), 0.0))
        dg = _rsum(jnp.where(col >= row, jnp.broadcast_to(dgam_row, sq), 0.0))
        d_xd = dg * (-jnp.exp(a_log)) * _sigmoid(cm["xd"])
        d_braw = dbeta * beta * (1.0 - beta)
        d_alog = dg * cm["g"]
        lane = lax.broadcasted_iota(jnp.int32, (c, HEAD_DIM), 1)
        dbd = jnp.zeros((c, HEAD_DIM), F32)
        for h in range(heads):
            dbd = (dbd + jnp.where(lane == h, d_braw[h], 0.0) + jnp.where(lane == h + heads, d_xd[h], 0.0)
                   + jnp.where(lane == h + 2 * heads, d_alog[h], 0.0))
        return dq, dk, dv, ds, dbd

    def body(do_ref, q_ref, k_ref, v_ref, bd_ref, st_ref, t_ref, uw_ref, al_ref, dt_ref, dqkv_ref, dbd_ref, dstate):
        @pl.when(pl.program_id(0) == 0)
        def _():
            dstate[...] = jnp.zeros_like(dstate)

        dq, dk, dv, ds, dbd = all_heads(_heads_of(q_ref, heads), _heads_of(k_ref, heads), _heads_of(v_ref, heads),
                                        _heads_of(do_ref, heads), st_ref[0], (t_ref[0], uw_ref[0, 0], uw_ref[0, 1]),
                                        dstate[...], bd_ref[...],
                                        al_ref[...], dt_ref[...])
        for part, val in enumerate((dq, dk, dv)):
            for h in range(heads):
                lo = part * dw + h * HEAD_DIM
                dqkv_ref[:, lo:lo + HEAD_DIM] = val[h]
        dstate[...] = ds
        dbd_ref[...] = dbd

    smem = pl.BlockSpec((heads, 1, 1), lambda i: (0, 0, 0))
    shared = pl.BlockSpec((CHUNK, HEAD_DIM), lambda i: (n - 1 - i, 0))
    wide = pl.BlockSpec((CHUNK, dw), lambda i: (n - 1 - i, 0))
    return pl.pallas_call(
        body, name="delta_chunk_bwd", grid=(n,),
        in_specs=[wide, blk(0), blk(1), blk(2), shared,
                  pl.BlockSpec((1, heads, HEAD_DIM, HEAD_DIM), lambda i: (n - 1 - i, 0, 0, 0)),
                  pl.BlockSpec((1, heads, CHUNK, CHUNK), lambda i: (n - 1 - i, 0, 0, 0)),
                  pl.BlockSpec((1, 2, heads, CHUNK, HEAD_DIM), lambda i: (n - 1 - i, 0, 0, 0, 0)), smem, smem],
        out_specs=[pl.BlockSpec((CHUNK, 3 * dw), lambda i: (n - 1 - i, 0)), shared],
        out_shape=[jax.ShapeDtypeStruct((s, 3 * dw), F32), jax.ShapeDtypeStruct((s, HEAD_DIM), F32)],
        scratch_shapes=[pltpu.VMEM((heads, HEAD_DIM, HEAD_DIM), F32)],
        compiler_params=_params(("arbitrary",)),
    )(do, qkv, qkv, qkv, bd, *saved, a_log, dt_bias)


def _delta_post_fwd(o, qkvz, w, heads):
    s = o.shape[0]
    tr = _tile(s, 1024, 8)

    def body(o_ref, z_ref, w_ref, out_ref):
        ov, z = o_ref[...], z_ref[...]
        rstd = lax.rsqrt(jnp.mean(ov * ov, axis=-1, keepdims=True) + EPS)
        out_ref[...] = (ov * rstd * w_ref[...] * (z * _sigmoid(z))).astype(BF16)

    return pl.pallas_call(
        body, name="delta_post_fwd", grid=(s // tr, heads),
        in_specs=[pl.BlockSpec((tr, HEAD_DIM), lambda i, h: (i, h)),
                  pl.BlockSpec((tr, HEAD_DIM), lambda i, h: (i, 3 * heads + h)),
                  pl.BlockSpec((1, HEAD_DIM), lambda i, h: (0, 0))],
        out_specs=pl.BlockSpec((tr, HEAD_DIM), lambda i, h: (i, h)),
        out_shape=jax.ShapeDtypeStruct((s, heads * HEAD_DIM), BF16),
        compiler_params=_params(("parallel", "parallel")),
    )(o, qkvz, w)


def _delta_post_bwd(dmix, o, qkvz, w, heads):
    s = o.shape[0]
    tr = _tile(s, 1024, 8)

    def body(d_ref, o_ref, z_ref, w_ref, do_ref, dz_ref, dw_ref):
        d, ov, z, wv = d_ref[...], o_ref[...], z_ref[...], w_ref[...]
        sg = _sigmoid(z)
        rstd = lax.rsqrt(jnp.mean(ov * ov, axis=-1, keepdims=True) + EPS)
        ohat = ov * rstd
        dz_ref[...] = (d * (ohat * wv) * sg * (1.0 + z * (1.0 - sg))).astype(BF16)
        dn = d * (z * sg)
        gw = dn * wv
        do_ref[...] = rstd * (gw - ohat * jnp.mean(gw * ohat, axis=-1, keepdims=True))

        @pl.when((pl.program_id(0) == 0) & (pl.program_id(1) == 0))
        def _():
            dw_ref[...] = jnp.zeros_like(dw_ref)

        dw_ref[...] += jnp.sum(dn * ohat, axis=0, keepdims=True)

    head = pl.BlockSpec((tr, HEAD_DIM), lambda i, h: (i, h))
    vec = pl.BlockSpec((1, HEAD_DIM), lambda i, h: (0, 0))
    dw = heads * HEAD_DIM
    return pl.pallas_call(
        body, name="delta_post_bwd", grid=(s // tr, heads),
        in_specs=[head, head, pl.BlockSpec((tr, HEAD_DIM), lambda i, h: (i, 3 * heads + h)), vec],
        out_specs=[head, head, vec],
        out_shape=[jax.ShapeDtypeStruct((s, dw), F32), jax.ShapeDtypeStruct((s, dw), BF16),
                   jax.ShapeDtypeStruct((1, HEAD_DIM), F32)],
        compiler_params=_params(("arbitrary", "arbitrary")),
    )(dmix, o, qkvz, w)


def _rope_tables(positions, s):
    half = HEAD_DIM // 2
    inv_freq = ROPE_THETA ** (-jnp.arange(half, dtype=F32) / half)
    ang = positions.reshape(s, 1).astype(F32) * inv_freq
    cos, sin = jnp.cos(ang), jnp.sin(ang)
    return jnp.concatenate([cos, cos], axis=-1), jnp.concatenate([-sin, sin], axis=-1)


def _attn_pre_fwd(aqkv, wq, wk, cosf, sinf, heads):
    s = aqkv.shape[0]
    tr = _tile(s, 1024, 8)

    def body(x_ref, wq_ref, wk_ref, c_ref, s_ref, o_ref):
        xv = x_ref[...]
        wv = jnp.where(pl.program_id(1) < heads, wq_ref[...], wk_ref[...])
        y = xv * lax.rsqrt(jnp.mean(xv * xv, axis=-1, keepdims=True) + EPS) * wv
        o_ref[...] = y * c_ref[...] + pltpu.roll(y, HEAD_DIM // 2, 1) * s_ref[...]

    blk = pl.BlockSpec((tr, HEAD_DIM), lambda i, j: (i, j))
    vec = pl.BlockSpec((1, HEAD_DIM), lambda i, j: (0, 0))
    tab = pl.BlockSpec((tr, HEAD_DIM), lambda i, j: (i, 0))
    return pl.pallas_call(
        body, name="attn_pre_fwd", grid=(s // tr, 2 * heads),
        in_specs=[blk, vec, vec, tab, tab], out_specs=blk,
        out_shape=jax.ShapeDtypeStruct((s, 2 * heads * HEAD_DIM), F32),
        compiler_params=_params(("parallel", "parallel")),
    )(aqkv, wq, wk, cosf, sinf)


def _band():
    qi = lax.broadcasted_iota(jnp.int32, (SPAN, 2 * SPAN), 0)
    ki = lax.broadcasted_iota(jnp.int32, (SPAN, 2 * SPAN), 1)
    dist = qi + SPAN - ki
    return (dist >= 0) & (dist <= SPAN), ki >= SPAN


def _sub(g, r, d):
    if d == 1:
        return pl.ds(g * SPAN, SPAN)
    return pl.ds(g * SPAN * d + r, SPAN, stride=d)


def _attn_blocks(s):
    assert s % (SPAN * max(DILATIONS)) == 0
    return [(p_i, d, r, g) for p_i, d in enumerate(DILATIONS) for r in range(d) for g in range(s // (SPAN * d))]


def _attn_fwd(qk, aqkv, w, heads):
    s = qk.shape[0]
    aw = heads * HEAD_DIM

    def body(q_ref, k_ref, v_ref, w_ref, mix_ref, acc_ref, m_ref, l_ref):
        band, own = _band()
        for p_i, d, r, g in _attn_blocks(s):
            rows, prows = _sub(g, r, d), _sub(max(g - 1, 0), r, d)
            mask = band if g > 0 else band & own
            kcat = jnp.concatenate([k_ref[prows, :], k_ref[rows, :]], axis=0)
            vcat = jnp.concatenate([v_ref[prows, :], v_ref[rows, :]], axis=0)
            sc = _dot(q_ref[rows, :], kcat, "nt") * (HEAD_DIM ** -0.5)
            sc = jnp.where(mask, sc, NEG)
            m = jnp.max(sc, axis=-1, keepdims=True)
            if p_i == 0:
                p = jnp.exp(sc - m)
                acc_ref[rows, :] = _dot(p, vcat)
                l_new = jnp.sum(p, axis=-1, keepdims=True)
            else:
                m_old = m_ref[rows, 0:1]
                m = jnp.maximum(m, m_old)
                alpha = jnp.exp(m_old - m)
                p = jnp.exp(sc - m)
                acc_ref[rows, :] = alpha * acc_ref[rows, :] + _dot(p, vcat)
                l_new = alpha * l_ref[rows, 0:1] + jnp.sum(p, axis=-1, keepdims=True)
            m_ref[rows, :] = jnp.broadcast_to(m, (SPAN, HEAD_DIM))
            l_ref[rows, :] = jnp.broadcast_to(l_new, (SPAN, HEAD_DIM))
        den = l_ref[...]
        ob = acc_ref[...] / den
        acc_ref[...] = ob
        m_ref[...] = m_ref[...] + jnp.log(den)
        rstd = lax.rsqrt(jnp.mean(ob * ob, axis=-1, keepdims=True) + EPS)
        mix_ref[...] = (ob * rstd * w_ref[...]).astype(BF16)

    col = lambda off: pl.BlockSpec((s, HEAD_DIM), lambda h: (0, off + h))
    return pl.pallas_call(
        body, name="attn_fwd", grid=(heads,),
        in_specs=[col(0), col(heads), col(2 * heads), pl.BlockSpec((1, HEAD_DIM), lambda h: (0, 0))],
        out_specs=[col(0), col(0), col(0)],
        out_shape=[jax.ShapeDtypeStruct((s, aw), BF16), jax.ShapeDtypeStruct((s, aw), F32),
                   jax.ShapeDtypeStruct((s, aw), F32)],
        scratch_shapes=[pltpu.VMEM((s, HEAD_DIM), F32)],
        compiler_params=_params(("parallel",)),
    )(qk, qk, aqkv, w)


def _attn_merge_bwd(dmix, ob, w, heads):
    s = ob.shape[0]
    tr = _tile(s, 1024, 8)

    def body(d_ref, ob_ref, w_ref, do_ref, dsum_ref, dw_ref):
        d, ov = d_ref[...], ob_ref[...]
        rstd = lax.rsqrt(jnp.mean(ov * ov, axis=-1, keepdims=True) + EPS)
        ohat = ov * rstd
        gw = d * w_ref[...]
        dov = rstd * (gw - ohat * jnp.mean(gw * ohat, axis=-1, keepdims=True))
        do_ref[...] = dov
        dsum_ref[...] = jnp.broadcast_to(jnp.sum(dov * ov, axis=-1, keepdims=True), dov.shape)

        @pl.when((pl.program_id(0) == 0) & (pl.program_id(1) == 0))
        def _():
            dw_ref[...] = jnp.zeros_like(dw_ref)

        dw_ref[...] += jnp.sum(d * ohat, axis=0, keepdims=True)

    blk = pl.BlockSpec((tr, HEAD_DIM), lambda i, h: (i, h))
    vec = pl.BlockSpec((1, HEAD_DIM), lambda i, h: (0, 0))
    aw = heads * HEAD_DIM
    return pl.pallas_call(
        body, name="attn_merge_bwd", grid=(s // tr, heads),
        in_specs=[pl.BlockSpec((tr, HEAD_DIM), lambda i, h: (i, heads + h)), blk, vec],
        out_specs=[blk, blk, vec],
        out_shape=[jax.ShapeDtypeStruct((s, aw), F32), jax.ShapeDtypeStruct((s, aw), F32),
                   jax.ShapeDtypeStruct((1, HEAD_DIM), F32)],
        compiler_params=_params(("arbitrary", "arbitrary")),
    )(dmix, ob, w)


def _attn_bwd(qk, aqkv, do, lse, dsum, heads):
    s = qk.shape[0]
    aw = heads * HEAD_DIM
    scale = HEAD_DIM ** -0.5

    def body(q_ref, k_ref, v_ref, do_ref, l_ref, ds_ref, out_ref):
        band, own = _band()
        dq_ref, dk_ref, dv_ref = out_ref.at[0], out_ref.at[1], out_ref.at[2]
        out_ref[...] = jnp.zeros((3, s, HEAD_DIM), F32)
        for _, d, r, g in _attn_blocks(s):
            rows, prows = _sub(g, r, d), _sub(max(g - 1, 0), r, d)
            mask = band if g > 0 else band & own
            q, dov = q_ref[rows, :], do_ref[rows, :]
            kcat = jnp.concatenate([k_ref[prows, :], k_ref[rows, :]], axis=0)
            vcat = jnp.concatenate([v_ref[prows, :], v_ref[rows, :]], axis=0)
            sc = _dot(q, kcat, "nt") * scale
            p = jnp.where(mask, jnp.exp(sc - l_ref[rows, 0:1]), 0.0)
            dsc = p * (_dot(dov, vcat, "nt") - ds_ref[rows, 0:1]) * scale
            dq_ref[rows, :] += _dot(dsc, kcat)
            dk = _dot(dsc, q, "tn")
            dv = _dot(p, dov, "tn")
            dk_ref[rows, :] += dk[SPAN:]
            dv_ref[rows, :] += dv[SPAN:]
            if g > 0:
                dk_ref[prows, :] += dk[:SPAN]
                dv_ref[prows, :] += dv[:SPAN]

    col = lambda off: pl.BlockSpec((s, HEAD_DIM), lambda h: (0, off + h))
    return pl.pallas_call(
        body, name="attn_bwd", grid=(heads,),
        in_specs=[col(0), col(heads), col(2 * heads), col(0), col(0), col(0)],
        out_specs=pl.BlockSpec((3, s, HEAD_DIM), lambda h: (0, 0, h)),
        out_shape=jax.ShapeDtypeStruct((3, s, aw), F32),
        compiler_params=_params(("parallel",)),
    )(qk, qk, aqkv, do, lse, dsum)


def _attn_pre_bwd(grads, aqkv, wq, wk, cosf, sinf, heads):
    s = aqkv.shape[0]
    tr = _tile(s, 1024, 8)
    nrow = s // tr

    def body(g_ref, x_ref, wq_ref, wk_ref, c_ref, s_ref, dx_ref, dwq_ref, dwk_ref):
        i, j = pl.program_id(0), pl.program_id(1)
        kind = j // heads
        dout = g_ref[0]
        tot_v = dout
        dy = dout * c_ref[...] + pltpu.roll(dout * s_ref[...], HEAD_DIM // 2, 1)
        xv = x_ref[...]
        wv = jnp.where(kind == 0, wq_ref[...], wk_ref[...])
        rstd = lax.rsqrt(jnp.mean(xv * xv, axis=-1, keepdims=True) + EPS)
        xhat = xv * rstd
        gw = dy * wv
        dxn = rstd * (gw - xhat * jnp.mean(gw * xhat, axis=-1, keepdims=True))
        dx_ref[...] = jnp.where(kind == 2, tot_v, dxn).astype(BF16)
        dwc = jnp.sum(dy * xhat, axis=0, keepdims=True)

        @pl.when((i == 0) & (j == 0))
        def _():
            dwq_ref[...] = jnp.zeros_like(dwq_ref)
            dwk_ref[...] = jnp.zeros_like(dwk_ref)

        @pl.when(kind == 0)
        def _():
            dwq_ref[...] += dwc

        @pl.when(kind == 1)
        def _():
            dwk_ref[...] += dwc

    grad = pl.BlockSpec((1, tr, HEAD_DIM), lambda i, j: (j // heads, i, j % heads))
    blk = pl.BlockSpec((tr, HEAD_DIM), lambda i, j: (i, j))
    vec = pl.BlockSpec((1, HEAD_DIM), lambda i, j: (0, 0))
    tab = pl.BlockSpec((tr, HEAD_DIM), lambda i, j: (i, 0))
    return pl.pallas_call(
        body, name="attn_pre_bwd", grid=(nrow, 3 * heads),
        in_specs=[grad, blk, vec, vec, tab, tab], out_specs=[blk, vec, vec],
        out_shape=[jax.ShapeDtypeStruct((s, 3 * heads * HEAD_DIM), BF16),
                   jax.ShapeDtypeStruct((1, HEAD_DIM), F32), jax.ShapeDtypeStruct((1, HEAD_DIM), F32)],
        compiler_params=_params(("arbitrary", "arbitrary")),
    )(grads, aqkv, wq, wk, cosf, sinf)


def _colsum(a, name):
    s, d = a.shape
    tr = _tile(s, 1024, 8)

    def body(a_ref, o_ref):
        @pl.when(pl.program_id(0) == 0)
        def _():
            o_ref[...] = jnp.zeros_like(o_ref)

        o_ref[...] += jnp.sum(a_ref[...], axis=0, keepdims=True)

    return pl.pallas_call(
        body, name=name, grid=(s // tr,),
        in_specs=[pl.BlockSpec((tr, d), lambda i: (i, 0))], out_specs=pl.BlockSpec((1, d), lambda i: (0, 0)),
        out_shape=jax.ShapeDtypeStruct((1, d), F32),
        compiler_params=_params(("arbitrary",)),
    )(a)


def _local_step(x, positions, target, small, w_qkvz, w_bd, w_attn, conv_w, later_weights, ffn_grads_ready,
                rest_grads_ready, after=None):
    s, dmod = x.shape
    heads = dmod // (2 * HEAD_DIM)
    dw = heads * HEAD_DIM
    a_log, dt_bias = small["a_log"].reshape(heads, 1, 1), small["dt_bias"].reshape(heads, 1, 1)
    cosf, sinf = _rope_tables(positions, s)

    h1 = _rmsnorm_fwd(x, small["attn_norm_w"], "norm1_fwd", after=after)
    qkvz = _matmul(h1, w_qkvz, "nn", "proj_qkvz")
    bd = _matmul(h1, w_bd, "nn", "proj_bd")
    aqkv = _matmul(h1, w_attn, "nn", "proj_attn")
    dqkv = _delta_pre_fwd(qkvz, conv_w, heads)
    o_d, *saved = _delta_chunk_fwd(dqkv, bd, a_log, dt_bias, heads)
    mix_a = _delta_post_fwd(o_d, qkvz, small["delta_out_norm_w"], heads)
    qk_rot = _attn_pre_fwd(aqkv, small["q_norm_w"], small["k_norm_w"], cosf, sinf, heads)
    mix_b, ob, lse = _attn_fwd(qk_rot, aqkv, small["attn_out_norm_w"], heads)
    w_out, behind, ffn_weights = later_weights((mix_a, mix_b))
    x1 = _matmul(mix_a, w_out, "nn", "out_proj_a", add=x, b_rows=(0, dw), after=behind)
    x1 = _matmul(mix_b, w_out, "nn", "out_proj_b", add=x1, b_rows=(dw, dw))
    h2 = _rmsnorm_fwd(x1, small["ffn_norm_w"], "norm2_fwd")
    w_gu, w_down = ffn_weights(h2)
    ff = w_down.shape[0]

    def swiglu(accs, vals):
        g, u = accs
        return g, u, g * _sigmoid(g) * u

    gate, up, act = _matmul(h2, w_gu, "nn", "ffn_gate_up", b_cols=(0, ff), b2_cols=(ff, ff), finish=swiglu,
                            out_dtypes=[BF16, BF16, BF16])

    def loss_head(acc, vals):
        err = acc + vals[0] - vals[1]
        part = 0.5 * jnp.sum(jnp.sum(err * err, axis=-1, keepdims=True) * (1.0 / dmod), axis=0, keepdims=True)
        lane = lax.broadcasted_iota(jnp.int32, (1, HEAD_DIM), 1)
        return err * (1.0 / dmod), err * (1.0 / dmod), jnp.where(lane == 0, part, 0.0)

    dy, dy16, loss_row = _matmul(act, w_down, "nn", "ffn_down", tiles=[(x1, 0), (target, 0)], finish=loss_head,
                                 out_dtypes=[F32, BF16], row_sum=True)

    def swiglu_bwd(acc, vals):
        g, u = vals[0].astype(F32), vals[1].astype(F32)
        sg = _sigmoid(g)
        return acc * u * sg * (1.0 + g * (1.0 - sg)), acc * g * sg

    dgate, dup = _matmul(dy16, w_down, "nt", "ffn_down_dx", tiles=[(gate, 0), (up, 0)], finish=swiglu_bwd,
                         out_dtypes=[BF16, BF16])
    g_w_down = _matmul(act, dy16, "tn", "ffn_down_dw")
    g_w_gate = _matmul(h2, dgate, "tn", "ffn_gate_dw")
    g_w_up = _matmul(h2, dup, "tn", "ffn_up_dw")
    behind, and_then = ffn_grads_ready(g_w_gate, g_w_up, g_w_down)
    dh2 = _matmul(dgate, w_gu, "nt", "ffn_gate_dx", b_cols=(0, ff), after=behind)
    dh2 = _matmul(dup, w_gu, "nt", "ffn_up_dx", b_cols=(ff, ff), add=dh2)
    behind = and_then(dh2)
    dx1, dx1_16, g_ffn_norm = _rmsnorm_bwd(dh2, x1, small["ffn_norm_w"], dy, "norm2_bwd", after=behind)
    dmix = _matmul(dx1_16, w_out, "nt", "out_proj_dx")
    g_w_out_a = _matmul(mix_a, dx1_16, "tn", "out_proj_dw_a")
    g_w_out_b = _matmul(mix_b, dx1_16, "tn", "out_proj_dw_b")
    dob, dsum, g_attn_out_norm = _attn_merge_bwd(dmix, ob, small["attn_out_norm_w"], heads)
    grads = _attn_bwd(qk_rot, aqkv, dob, lse, dsum, heads)
    d_aqkv, g_q_norm, g_k_norm = _attn_pre_bwd(grads, aqkv, small["q_norm_w"], small["k_norm_w"], cosf, sinf, heads)
    do_d, dz, g_delta_out_norm = _delta_post_bwd(dmix, o_d, qkvz, small["delta_out_norm_w"], heads)
    ddqkv, dbd = _delta_chunk_bwd(do_d, dqkv, bd, saved, a_log, dt_bias, heads)
    d_qkv_raw, g_conv = _delta_pre_bwd(ddqkv, qkvz, conv_w, heads)
    bd_sums = _colsum(dbd, "bd_colsum")
    g_w_qkv = _matmul(h1, d_qkv_raw, "tn", "proj_qkv_dw")
    g_w_z = _matmul(h1, dz, "tn", "proj_z_dw")
    g_w_bd = _matmul(h1, dbd, "tn", "proj_bd_dw")
    g_w_attn = _matmul(h1, d_aqkv, "tn", "proj_attn_dw")
    behind, and_then = rest_grads_ready(dict(w_qkv=g_w_qkv, w_z=g_w_z, w_bd=g_w_bd, w_attn=g_w_attn,
                                             w_out_a=g_w_out_a, w_out_b=g_w_out_b))
    dh1 = _matmul(d_qkv_raw, w_qkvz, "nt", "proj_qkv_dx", b_cols=(0, 3 * dw), after=behind)
    dh1 = _matmul(dz, w_qkvz, "nt", "proj_z_dx", b_cols=(3 * dw, dw), add=dh1, after=and_then(dh1))
    dh1 = _matmul(d_aqkv, w_attn, "nt", "proj_attn_dx", add=dh1)
    dh1 = _matmul(dbd, w_bd, "nt", "proj_bd_dx", add=dh1)
    grad_x, _, g_attn_norm = _rmsnorm_bwd(dh1, x, small["attn_norm_w"], dx1, "norm1_bwd")
    small_grads = dict(
        attn_norm_w=g_attn_norm, a_log=bd_sums[:, 2 * heads:3 * heads], dt_bias=bd_sums[:, heads:2 * heads],
        delta_out_norm_w=g_delta_out_norm, q_norm_w=g_q_norm, k_norm_w=g_k_norm,
        attn_out_norm_w=g_attn_out_norm, ffn_norm_w=g_ffn_norm, conv_w=g_conv)
    return loss_row, grad_x, small_grads


def _adamw(w, g, m, v, name):
    r, c = w.shape
    tr = _tile(r, 256, 8)

    def body(w_ref, g_ref, m_ref, v_ref, d_ref, nm_ref, nv_ref):
        gv = g_ref[...]
        nm = ADAM_B1 * m_ref[...] + (1.0 - ADAM_B1) * gv
        nv = ADAM_B2 * v_ref[...] + (1.0 - ADAM_B2) * (gv * gv)
        m_hat = nm / (1.0 - ADAM_B1 ** ADAM_STEP)
        v_hat = nv / (1.0 - ADAM_B2 ** ADAM_STEP)
        d_ref[...] = -ADAM_LR * (m_hat / (jnp.sqrt(v_hat) + ADAM_EPS) + ADAM_WD * w_ref[...])
        nm_ref[...] = nm
        nv_ref[...] = nv

    blk = pl.BlockSpec((tr, c), lambda i: (i, 0))
    return pl.pallas_call(
        body, name=name, grid=(r // tr,),
        in_specs=[blk] * 4, out_specs=[blk] * 3,
        out_shape=[jax.ShapeDtypeStruct((r, c), F32)] * 3,
        compiler_params=_params(("parallel",)),
    )(w, g, m, v)


def _add_half_bf16(g, b, place, name):
    n, half, c = b.shape
    tr = _tile(half, 512, 16)
    nb = half // tr

    def body(place_ref, g_ref, b_ref, o_ref):
        o_ref[...] = (g_ref[...].astype(F32) + b_ref[...].astype(F32)).astype(BF16)

    blk = pl.BlockSpec((1, tr, c), lambda i, j, p: (i, j, 0))
    return pl.pallas_call(
        body, name=name,
        grid_spec=pltpu.PrefetchScalarGridSpec(
            num_scalar_prefetch=1, grid=(n, nb),
            in_specs=[pl.BlockSpec((1, tr, c), lambda i, j, p: (i, p[0] * nb + j, 0)), blk], out_specs=blk),
        out_shape=jax.ShapeDtypeStruct((n, half, c), BF16),
        compiler_params=_params(("parallel", "parallel")),
    )(place, g, b)


def _sum4_f32(mine, others, place, name):
    _, half, c = mine.shape
    tr = _tile(half, 512, 16)
    nb = half // tr

    def body(place_ref, a_ref, b_ref, o_ref):
        acc = a_ref[0].astype(F32)
        for j in range(3):
            acc = acc + b_ref[j].astype(F32)
        o_ref[...] = acc

    return pl.pallas_call(
        body, name=name,
        grid_spec=pltpu.PrefetchScalarGridSpec(
            num_scalar_prefetch=1, grid=(nb,),
            in_specs=[pl.BlockSpec((1, tr, c), lambda i, p: (p[1], i, 0)),
                      pl.BlockSpec((3, tr, c), lambda i, p: (0, i, 0))],
            out_specs=pl.BlockSpec((tr, c), lambda i, p: (p[0] * nb + i, 0))),
        out_shape=jax.ShapeDtypeStruct((2 * half, c), F32),
        compiler_params=_params(("parallel",)),
    )(place, mine, others)


def _place():
    x, y, c = lax.axis_index("x"), lax.axis_index("y"), lax.axis_index("c")
    other_chips = [(1 - x, y), (x, 1 - y), (1 - x, 1 - y)]
    return x, y, c, (x, y, 1 - c), other_chips


ANY = pl.BlockSpec(memory_space=pl.ANY)


def _remote(k, src, dst, to, send_sems, recv_sems):
    return pltpu.make_async_remote_copy(src_ref=src, dst_ref=dst, send_sem=send_sems.at[k], recv_sem=recv_sems.at[k],
                                        device_id=to, device_id_type=MESH)


def _half(ref, lead, hc):
    if lead is None:
        half = ref.shape[0] // 2
        return ref.at[pl.ds(hc * half, half), :]
    half = ref.shape[1] // 2
    return ref.at[lead, pl.ds(hc * half, half), :]


def _all_gather_weights(slots, whole, name):
    nt, nw = len(slots), len(whole)
    base_w, base_f, base_d = 2 * nt, 2 * nt + 3 * nw, 4 * nt + 3 * nw

    def quarter(ref, lead, hc, q):
        quart = ref.shape[1] // 4
        return ref.at[lead, pl.ds((2 * hc + q) * quart, quart), :]

    def body(*refs):
        ins, outs = refs[:nt + nw], refs[nt + nw:2 * (nt + nw)]
        sems = refs[2 * (nt + nw):]
        x, y, c, sibling, chips = _place()
        me, xn, yn, dg = 2 * x + y, 2 * (1 - x) + y, 2 * x + 1 - y, 2 * (1 - x) + 1 - y
        to_x, to_y = (1 - x, y, c), (x, 1 - y, c)
        cps = []
        for t in range(nt):
            cps.append(_remote(2 * t, _half(ins[t], me, c), _half(outs[t], me, c), to_x, *sems))
            cps.append(_remote(2 * t + 1, _half(ins[t], me, c), _half(outs[t], me, c), to_y, *sems))
        for j, (px, py) in enumerate(chips):
            for t in range(nw):
                cps.append(_remote(base_w + j * nw + t, ins[nt + t].at[me], outs[nt + t].at[me], (px, py, c), *sems))
        for cp in cps:
            cp.start()

        def start(k, ref, to):
            cp = _remote(k, ref, ref, to, *sems)
            cp.start()
            cps.append(cp)

        for t in range(nt):
            landed = _half(outs[t], xn, c)
            _remote(2 * t, landed, landed, to_x, *sems).wait_recv()
            start(base_f + 2 * t, quarter(outs[t], xn, c, 0), to_y)
            start(base_d + 3 * t, landed, sibling)
            landed = _half(outs[t], yn, c)
            _remote(2 * t + 1, landed, landed, to_y, *sems).wait_recv()
            start(base_f + 2 * t + 1, quarter(outs[t], yn, c, 1), to_x)
            start(base_d + 3 * t + 1, landed, sibling)
        for t in range(nt):
            q0, q1 = quarter(outs[t], dg, c, 0), quarter(outs[t], dg, c, 1)
            _remote(base_f + 2 * t, q0, q0, to_y, *sems).wait_recv()
            _remote(base_f + 2 * t + 1, q1, q1, to_x, *sems).wait_recv()
            start(base_d + 3 * t + 2, _half(outs[t], dg, c), sibling)
        for j, (px, py) in enumerate(chips):
            for t in range(nw):
                landed = outs[nt + t].at[2 * px + py]
                _remote(base_w + j * nw + t, landed, landed, (px, py, c), *sems).wait_recv()
        for t in range(nt):
            for j, chip in enumerate((xn, yn, dg)):
                other = _half(outs[t], chip, 1 - c)
                _remote(base_d + 3 * t + j, other, other, sibling, *sems).wait_recv()
        for cp in cps:
            cp.wait_send()

    arrays = list(slots) + list(whole)
    n_sem = 7 * nt + 3 * nw
    return pl.pallas_call(
        body, name=name, in_specs=[ANY] * len(arrays), out_specs=[ANY] * len(arrays),
        input_output_aliases={i: i for i in range(len(arrays))},
        out_shape=[jax.ShapeDtypeStruct(a.shape, a.dtype) for a in arrays],
        scratch_shapes=[pltpu.SemaphoreType.DMA((n_sem,)), pltpu.SemaphoreType.DMA((n_sem,))],
    )(*arrays)


def _swap_halves_with_sibling(gs, name):
    nt = len(gs)

    def body(*refs):
        g_refs, o_refs, (send_sems, recv_sems) = refs[:nt], refs[nt:2 * nt], refs[2 * nt:]
        _, _, c, sibling, _ = _place()
        cps = []
        for t in range(nt):
            half = g_refs[t].shape[1] // 2
            cps.append(_remote(t, g_refs[t].at[:, pl.ds((1 - c) * half, half), :], o_refs[t], sibling,
                               send_sems, recv_sems))
        for cp in cps:
            cp.start()
        for cp in cps:
            cp.wait()

    return pl.pallas_call(
        body, name=name, in_specs=[ANY] * nt, out_specs=[ANY] * nt,
        out_shape=[jax.ShapeDtypeStruct((g.shape[0], g.shape[1] // 2, g.shape[2]), g.dtype) for g in gs],
        scratch_shapes=[pltpu.SemaphoreType.DMA((nt,)), pltpu.SemaphoreType.DMA((nt,))],
    )(*gs)


def _scatter_to_chips(ps, name):
    nt = len(ps)

    def body(*refs):
        p_refs, b_refs, (send_sems, recv_sems) = refs[:nt], refs[nt:2 * nt], refs[2 * nt:]
        _, _, c, _, chips = _place()
        cps = [_remote(j * nt + t, p_refs[t].at[2 * px + py], b_refs[t].at[j], (px, py, c), send_sems, recv_sems)
               for j, (px, py) in enumerate(chips) for t in range(nt)]
        for cp in cps:
            cp.start()
        for cp in cps:
            cp.wait()

    return pl.pallas_call(
        body, name=name, in_specs=[ANY] * nt, out_specs=[ANY] * nt,
        out_shape=[jax.ShapeDtypeStruct((3,) + p.shape[1:], p.dtype) for p in ps],
        scratch_shapes=[pltpu.SemaphoreType.DMA((3 * nt,)), pltpu.SemaphoreType.DMA((3 * nt,))],
    )(*ps)


def _join_halves(fs, name):
    nt = len(fs)

    def body(*refs):
        in_refs, out_refs, (send_sems, recv_sems) = refs[:nt], refs[nt:2 * nt], refs[2 * nt:]
        _, _, c, sibling, _ = _place()
        cps = [_remote(t, _half(in_refs[t], None, c), _half(out_refs[t], None, c), sibling, send_sems, recv_sems)
               for t in range(nt)]
        for cp in cps:
            cp.start()
        for t in range(nt):
            theirs = _half(out_refs[t], None, 1 - c)
            _remote(t, theirs, theirs, sibling, send_sems, recv_sems).wait_recv()
        for cp in cps:
            cp.wait_send()

    return pl.pallas_call(
        body, name=name, in_specs=[ANY] * nt, out_specs=[ANY] * nt,
        input_output_aliases={i: i for i in range(nt)},
        out_shape=[jax.ShapeDtypeStruct(f.shape, f.dtype) for f in fs],
        scratch_shapes=[pltpu.SemaphoreType.DMA((nt,)), pltpu.SemaphoreType.DMA((nt,))],
    )(*fs)


HBM = pl.BlockSpec(memory_space=pltpu.HBM)
SEM = pl.BlockSpec(memory_space=pltpu.SEMAPHORE)
EFFECT = pltpu.SideEffectType.DATAFLOW_SIDE_EFFECTING


def _split_start(arrays, after, plan, n_copies, name):
    na = len(arrays)

    def body(*refs):
        ins, send_sems, recv_sems = refs[:na], refs[na + 1], refs[na + 2]
        outs, token = refs[na + 3:2 * na + 3], refs[2 * na + 3]
        for k, (src, dst, to) in enumerate(plan(ins, outs)):
            _remote(k, src, dst, to, send_sems, recv_sems).start()
        token[...] = jnp.zeros_like(token)

    res = pl.pallas_call(
        body, name=name,
        out_shape=(pltpu.SemaphoreType.DMA((n_copies,)), pltpu.SemaphoreType.DMA((n_copies,)),
                   *[pltpu.HBM(a.shape, a.dtype) for a in arrays], jax.ShapeDtypeStruct((8, HEAD_DIM), F32)),
        in_specs=[HBM] * na + [ANY],
        out_specs=(SEM, SEM, *[HBM] * na, pl.BlockSpec(memory_space=pltpu.VMEM)),
        input_output_aliases={i: 2 + i for i in range(na)},
        compiler_params=pltpu.CompilerParams(has_side_effects=EFFECT),
    )(*[pltpu.with_memory_space_constraint(a, pltpu.HBM) for a in arrays], after)
    return res[0], res[1], list(res[2:2 + na]), res[2 + na]


def _split_wait(send_sems, recv_sems, arrays, after, plan, name):
    na = len(arrays)
    after = list(after) if isinstance(after, (list, tuple)) else [after]

    def body(*refs):
        ins, send, recv, outs = refs[:na], refs[na], refs[na + 1], refs[na + 2 + len(after):]
        for k, (src, dst, to) in enumerate(plan(ins, outs)):
            cp = _remote(k, src, dst, to, send, recv)
            cp.wait_send()
            cp.wait_recv()

    res = pl.pallas_call(
        body, name=name, out_shape=tuple(pltpu.HBM(a.shape, a.dtype) for a in arrays),
        in_specs=[HBM] * na + [SEM, SEM] + [ANY] * len(after), out_specs=tuple([HBM] * na),
        input_output_aliases={i: i for i in range(na)},
        compiler_params=pltpu.CompilerParams(has_side_effects=EFFECT),
    )(*arrays, send_sems, recv_sems, *after)
    return list(res)


def _gather_plan(nt):
    def plan(ins, outs):
        x, y, c, _, chips = _place()
        me = 2 * x + y
        return [(_half(ins[t], me, c), _half(outs[t], me, c), (px, py, c)) for px, py in chips for t in range(nt)]
    return plan


def _gather_landed_plan(nt):
    def plan(ins, outs):
        _, _, c, _, chips = _place()
        return [(_half(outs[t], 2 * px + py, c), _half(outs[t], 2 * px + py, c), (px, py, c))
                for px, py in chips for t in range(nt)]
    return plan


def _scatter_plan(nt):
    def plan(ins, outs):
        _, _, c, _, chips = _place()
        return [(ins[t].at[2 * px + py], outs[nt + t].at[j], (px, py, c))
                for j, (px, py) in enumerate(chips) for t in range(nt)]
    return plan


def _pass_plan(nt):
    def plan(ins, outs):
        _, _, c, sibling, chips = _place()
        return [(_half(ins[t], 2 * px + py, c), _half(outs[t], 2 * px + py, c), sibling)
                for px, py in chips for t in range(nt)]
    return plan


def _pass_landed_plan(nt):
    def plan(ins, outs):
        _, _, c, sibling, chips = _place()
        return [(_half(outs[t], 2 * px + py, c), _half(outs[t], 2 * px + py, 1 - c), sibling)
                for px, py in chips for t in range(nt)]
    return plan


def _swap_plan(nt):
    def plan(ins, outs):
        _, _, c, sibling, _ = _place()
        res = []
        for t in range(nt):
            half = ins[t].shape[1] // 2
            res.append((ins[t].at[:, pl.ds((1 - c) * half, half), :], outs[nt + t], sibling))
        return res
    return plan


def _pass_halves_to_sibling(slots, name):
    nt = len(slots)

    def body(*refs):
        ins, outs, (send_sems, recv_sems) = refs[:nt], refs[nt:2 * nt], refs[2 * nt:]
        _, _, c, sibling, chips = _place()
        cps = [_remote(j * nt + t, _half(ins[t], 2 * px + py, c), _half(outs[t], 2 * px + py, c), sibling,
                       send_sems, recv_sems)
               for j, (px, py) in enumerate(chips) for t in range(nt)]
        for cp in cps:
            cp.start()
        for j, (px, py) in enumerate(chips):
            for t in range(nt):
                other = _half(outs[t], 2 * px + py, 1 - c)
                _remote(j * nt + t, other, other, sibling, send_sems, recv_sems).wait_recv()
        for cp in cps:
            cp.wait_send()

    return pl.pallas_call(
        body, name=name, in_specs=[ANY] * nt, out_specs=[ANY] * nt,
        input_output_aliases={i: i for i in range(nt)},
        out_shape=[jax.ShapeDtypeStruct(a.shape, a.dtype) for a in slots],
        scratch_shapes=[pltpu.SemaphoreType.DMA((3 * nt,)), pltpu.SemaphoreType.DMA((3 * nt,))],
    )(*slots)


def _all_reduce_small(v):
    r, lanes = v.shape

    def body(v_ref, out_ref, buf, send_sems, recv_sems):
        x, y, c, sibling, chips = _place()

        def slot(px, py, pc):
            return buf.at[4 * px + 2 * py + pc]

        def copy(k, block, to, src=None):
            return pltpu.make_async_remote_copy(src_ref=slot(*block) if src is None else src, dst_ref=slot(*block),
                                                send_sem=send_sems.at[k], recv_sem=recv_sems.at[k],
                                                device_id=to, device_id_type=MESH)

        me = (x, y, c)
        buf[4 * x + 2 * y + c] = v_ref[...]
        first = [copy(0, me, sibling, src=v_ref)]
        first += [copy(1 + j, me, (*chip, c), src=v_ref) for j, chip in enumerate(chips)]
        for cp in first:
            cp.start()
        passed = [copy(4 + j, (*chip, c), sibling) for j, chip in enumerate(chips)]
        for j, chip in enumerate(chips):
            copy(1 + j, (*chip, c), me).wait_recv()
            passed[j].start()
        copy(0, (x, y, 1 - c), me).wait_recv()
        for j, chip in enumerate(chips):
            copy(4 + j, (*chip, 1 - c), me).wait_recv()
        for cp in first + passed:
            cp.wait_send()
        acc = buf[0]
        for k in range(1, 8):
            acc = acc + buf[k]
        out_ref[...] = acc

    vmem = pl.BlockSpec(memory_space=pltpu.VMEM)
    return pl.pallas_call(
        body, name="all_reduce_small", in_specs=[vmem], out_specs=vmem,
        out_shape=jax.ShapeDtypeStruct((r, lanes), F32),
        scratch_shapes=[pltpu.VMEM((8, r, lanes), F32), pltpu.SemaphoreType.DMA((7,)), pltpu.SemaphoreType.DMA((7,))],
    )(v)


def _size(shape):
    n = 1
    for d in shape:
        n *= d
    return n


def _pack_small(parts):
    rows = []
    for p in parts:
        f = p.reshape(-1).astype(F32)
        n = -(-f.shape[0] // HEAD_DIM) * HEAD_DIM
        rows.append(jnp.pad(f, (0, n - f.shape[0])).reshape(-1, HEAD_DIM))
    a = jnp.concatenate(rows, axis=0)
    return jnp.pad(a, ((0, -a.shape[0] % 8), (0, 0)))


def _unpack_small(a, shapes):
    out, row = [], 0
    for shp in shapes:
        nrows = -(-_size(shp) // HEAD_DIM)
        out.append(a[row:row + nrows].reshape(-1)[:_size(shp)].reshape(shp))
        row += nrows
    return out


SMALL = ["attn_norm_w", "a_log", "dt_bias", "delta_out_norm_w", "q_norm_w", "k_norm_w", "attn_out_norm_w", "ffn_norm_w"]
BIG = ["w_in", "w_out", "w_gate_up", "w_down"]
ORDER = ["attn_norm_w", "w_in", "conv_w", "a_log", "dt_bias", "delta_out_norm_w", "q_norm_w", "k_norm_w",
         "attn_out_norm_w", "w_out", "ffn_norm_w", "w_gate_up", "w_down"]


def kernel(x, positions, attn_norm_w, w_in, conv_w, a_log, dt_bias, delta_out_norm_w, q_norm_w, k_norm_w, attn_out_norm_w, w_out, ffn_norm_w, w_gate_up, w_down, loss_target, m_attn_norm_w, m_w_in, m_conv_w, m_a_log, m_dt_bias, m_delta_out_norm_w, m_q_norm_w, m_k_norm_w, m_attn_out_norm_w, m_w_out, m_ffn_norm_w, m_w_gate_up, m_w_down, v_attn_norm_w, v_w_in, v_conv_w, v_a_log, v_dt_bias, v_delta_out_norm_w, v_q_norm_w, v_k_norm_w, v_attn_out_norm_w, v_w_out, v_ffn_norm_w, v_w_gate_up, v_w_down):
    wts = dict(attn_norm_w=attn_norm_w, w_in=w_in, conv_w=conv_w, a_log=a_log, dt_bias=dt_bias,
               delta_out_norm_w=delta_out_norm_w, q_norm_w=q_norm_w, k_norm_w=k_norm_w,
               attn_out_norm_w=attn_out_norm_w, w_out=w_out, ffn_norm_w=ffn_norm_w, w_gate_up=w_gate_up, w_down=w_down)
    mom = dict(attn_norm_w=m_attn_norm_w, w_in=m_w_in, conv_w=m_conv_w, a_log=m_a_log, dt_bias=m_dt_bias,
               delta_out_norm_w=m_delta_out_norm_w, q_norm_w=m_q_norm_w, k_norm_w=m_k_norm_w,
               attn_out_norm_w=m_attn_out_norm_w, w_out=m_w_out, ffn_norm_w=m_ffn_norm_w, w_gate_up=m_w_gate_up,
               w_down=m_w_down)
    var = dict(attn_norm_w=v_attn_norm_w, w_in=v_w_in, conv_w=v_conv_w, a_log=v_a_log, dt_bias=v_dt_bias,
               delta_out_norm_w=v_delta_out_norm_w, q_norm_w=v_q_norm_w, k_norm_w=v_k_norm_w,
               attn_out_norm_w=v_attn_out_norm_w, w_out=v_w_out, ffn_norm_w=v_ffn_norm_w, w_gate_up=v_w_gate_up,
               w_down=v_w_down)
    dmod = x.shape[2]
    heads = dmod // (2 * HEAD_DIM)
    dw = heads * HEAD_DIM
    chip = 2 * lax.axis_index("x") + lax.axis_index("y")
    core = lax.axis_index("c")
    n_in, n_out, n_gu, n_down, n_conv = (w_in.shape[2], w_out.shape[1], w_gate_up.shape[2], w_down.shape[1],
                                         conv_w.shape[2])

    def slots_of(w, dtype):
        shard = w[0].astype(dtype)
        return lax.dynamic_update_index_in_dim(lax.empty((4,) + shard.shape, dtype), shard, chip, axis=0)

    s_in, s_conv = _all_gather_weights([slots_of(w_in, BF16)], [slots_of(conv_w, F32)], "all_gather_w_in")
    later = [slots_of(w_out, BF16), slots_of(w_gate_up, BF16), slots_of(w_down, BF16)]
    w_send, w_recv, later, started = _split_start(later, s_conv, _gather_plan(3), 9, "gather_rest_start")
    by_cols = lambda a: a.transpose(1, 0, 2).reshape(a.shape[1], 4 * a.shape[2])
    w_in_f, conv_f = by_cols(s_in), by_cols(s_conv)
    w_bd = jnp.pad(w_in_f[:, 4 * dw:4 * dw + 2 * heads], ((0, 0), (0, HEAD_DIM - 2 * heads)))
    small = {n: wts[n] for n in SMALL}
    place = jnp.stack([core, chip]).astype(jnp.int32)
    to_slots = lambda a: a.reshape(a.shape[0], 4, a.shape[1] // 4).transpose(1, 0, 2)

    def later_weights(after):
        landed = _split_wait(w_send, w_recv, later, after, _gather_landed_plan(3), "gather_rest_wait")
        s_out, = _pass_halves_to_sibling(landed[:1], "gather_out_pass")
        p_send, p_recv, passing, token = _split_start(landed[1:], s_out, _pass_plan(2), 6, "gather_ffn_pass_start")

        def ffn_weights(after):
            s_gu, s_down = _split_wait(p_send, p_recv, passing, after, _pass_landed_plan(2), "gather_ffn_pass_wait")
            return by_cols(s_gu), s_down.reshape(4 * n_down, dmod)

        return s_out.reshape(4 * n_out, dmod), token, ffn_weights

    ffn = {}

    def ffn_grads_ready(g_gate, g_up, g_down):
        gs = [to_slots(jnp.concatenate([g_gate, g_up], axis=1)), g_down.reshape(4, n_down, dmod)]
        zones = [lax.empty((4, g.shape[1] // 2, g.shape[2]), BF16) for g in gs]
        s_send, s_recv, s_bufs, swapping = _split_start(gs + zones, g_gate, _swap_plan(2), 2, "swap_ffn_start")

        def and_then(after):
            g_gu, g_dn, b_gu, b_dn = _split_wait(s_send, s_recv, s_bufs, after, _swap_plan(2), "swap_ffn_wait")
            sums = [_add_half_bf16(g_gu, b_gu, place, "chip_partial_sum_w_gate_up"),
                    _add_half_bf16(g_dn, b_dn, place, "chip_partial_sum_w_down")]
            zones3 = [lax.empty((3,) + p.shape[1:], BF16) for p in sums]
            ffn["send"], ffn["recv"], ffn["bufs"], token = _split_start(sums + zones3, b_gu, _scatter_plan(2), 6,
                                                                        "scatter_ffn_start")
            return token

        return swapping, and_then

    rest = {}

    def rest_grads_ready(bg):
        gs = [to_slots(jnp.concatenate([bg["w_qkv"], bg["w_z"], bg["w_bd"][:, :2 * heads], bg["w_attn"]], axis=1)),
              jnp.concatenate([bg["w_out_a"], bg["w_out_b"]], axis=0).reshape(4, n_out, dmod)]
        zones = [lax.empty((4, g.shape[1] // 2, g.shape[2]), BF16) for g in gs]
        s_send, s_recv, s_bufs, swapping = _split_start(gs + zones, bg["w_attn"], _swap_plan(2), 2, "swap_rest_start")

        def and_then(after):
            g_in, g_out, b_in, b_out = _split_wait(s_send, s_recv, s_bufs, after, _swap_plan(2), "swap_rest_wait")
            sums = [_add_half_bf16(g_in, b_in, place, "chip_partial_sum_w_in"),
                    _add_half_bf16(g_out, b_out, place, "chip_partial_sum_w_out")]
            zones3 = [lax.empty((3,) + p.shape[1:], BF16) for p in sums]
            rest["send"], rest["recv"], rest["bufs"], token = _split_start(sums + zones3, b_in, _scatter_plan(2), 6,
                                                                           "scatter_rest_start")
            return token

        return swapping, and_then

    loss_row, grad_x, sg = _local_step(
        x[0], positions[0], loss_target[0], small, w_in_f[:, :4 * dw], w_bd, w_in_f[:, 4 * dw + 2 * heads:], conv_f,
        later_weights, ffn_grads_ready, rest_grads_ready, after=started)

    r_send, r_recv, r_bufs = rest["send"], rest["recv"], rest["bufs"]
    sum_gu, sum_down, got_gu, got_down = _split_wait(ffn["send"], ffn["recv"], ffn["bufs"], grad_x,
                                                     _scatter_plan(2), "scatter_ffn_wait")
    g_big = dict(zip(["w_gate_up", "w_down"], _join_halves(
        [_sum4_f32(sum_gu, got_gu, place, "grad_total_w_gate_up"),
         _sum4_f32(sum_down, got_down, place, "grad_total_w_down")], "join_ffn_halves")))
    grads, deltas, new_m, new_v = {}, {}, {}, {}

    def adamw_big(n):
        shp = wts[n].shape
        d, nm, nv = _adamw(wts[n][0], g_big[n], mom[n][0], var[n][0], "adamw_" + n)
        grads[n], deltas[n], new_m[n], new_v[n] = g_big[n].reshape(shp), d.reshape(shp), nm.reshape(shp), nv.reshape(shp)
        return d

    done = [adamw_big("w_gate_up"), adamw_big("w_down")]
    sum_in, sum_out, got_in, got_out = _split_wait(r_send, r_recv, r_bufs, done, _scatter_plan(2), "scatter_rest_wait")
    g_big.update(zip(["w_in", "w_out"], _join_halves(
        [_sum4_f32(sum_in, got_in, place, "grad_total_w_in"),
         _sum4_f32(sum_out, got_out, place, "grad_total_w_out")], "join_rest_halves")))
    adamw_big("w_in")
    adamw_big("w_out")

    reduced = _all_reduce_small(_pack_small([sg[n] for n in SMALL] + [sg["conv_w"], loss_row]))
    red = _unpack_small(reduced, [wts[n].shape for n in SMALL] + [(4, 4 * n_conv), (1, HEAD_DIM)])
    g_small = dict(zip(SMALL, red[:len(SMALL)]))
    g_conv_full, loss_out = red[len(SMALL)], red[len(SMALL) + 1]
    g_small["conv_w"] = lax.dynamic_slice_in_dim(g_conv_full, chip * n_conv, n_conv, axis=1).reshape(conv_w.shape)

    names = SMALL + ["conv_w"]
    shapes = [wts[n].shape for n in names]
    d, nm, nv = _adamw(_pack_small([wts[n] for n in names]), _pack_small([g_small[n] for n in names]),
                       _pack_small([mom[n] for n in names]), _pack_small([var[n] for n in names]), "adamw_small")
    for n, dd, mm, vv in zip(names, _unpack_small(d, shapes), _unpack_small(nm, shapes), _unpack_small(nv, shapes)):
        grads[n], deltas[n], new_m[n], new_v[n] = g_small[n], dd, mm, vv
    return (loss_out[0, 0], grad_x[None], *[grads[n] for n in ORDER], *[deltas[n] for n in ORDER],
            *[new_m[n] for n in ORDER], *[new_v[n] for n in ORDER])
```

```python
import functools

import jax
import jax.numpy as jnp
from jax import lax
from jax.experimental import pallas as pl
from jax.experimental.pallas import tpu as pltpu

F32 = jnp.float32
BF16 = jnp.bfloat16
HEAD_DIM = 128
CHUNK = 128
INV_BLOCK = 64
SPAN = 128
DILATIONS = (1, 4, 16)
ROPE_THETA = 10000.0
EPS = 1e-6
NEG = -1e30
ADAM_LR, ADAM_B1, ADAM_B2, ADAM_EPS, ADAM_WD, ADAM_STEP = 0.001, 0.9, 0.999, 1e-08, 0.01, 10
VMEM_LIMIT = 48 * 1024 * 1024
MESH = pl.DeviceIdType.MESH

_DN = {"nn": (((1,), (0,)), ((), ())), "nt": (((1,), (1,)), ((), ())), "tn": (((0,), (0,)), ((), ()))}


def _dot(a, b, mode="nn"):
    (ca, cb), _ = _DN[mode]
    if a.ndim == 3:
        dn = (((ca[0] + 1,), (cb[0] + 1,)), ((0,), (0,)))
    else:
        dn = _DN[mode]
    return lax.dot_general(a.astype(BF16), b.astype(BF16), dn, preferred_element_type=F32)


def _rsum(x):
    return jnp.sum(x, axis=-1, keepdims=True)


def _csum(x):
    return jnp.sum(x, axis=-2, keepdims=True)


def _tile(dim, pref, unit=128):
    t = (min(pref, dim) // unit) * unit
    while t >= unit:
        if dim % t == 0:
            return t
        t -= unit
    return dim


def _params(sem):
    return pltpu.CompilerParams(dimension_semantics=sem, vmem_limit_bytes=VMEM_LIMIT)


def _sigmoid(x):
    return 1.0 / (1.0 + jnp.exp(-x))


def _matmul(a, b, mode, name, add=None, out_dtype=F32, a_cols=None, b_cols=None, b_rows=None,
            tiles=(), finish=None, out_dtypes=(), row_sum=False, after=None, b2_cols=None):
    if mode == "tn":
        out_dtype = BF16
    a_off, a_w = a_cols if a_cols else (0, a.shape[1])
    b_off, b_w = b_cols if b_cols else (0, b.shape[1])
    br_off, br_n = b_rows if b_rows else (0, b.shape[0])
    if mode == "nn":
        m, k, n = a.shape[0], a_w, b_w
        assert br_n == k
    elif mode == "nt":
        m, k, n = a.shape[0], a_w, b.shape[0]
        assert b_w == k
    else:
        k, m, n = a.shape[0], a_w, b_w
        assert b.shape[0] == k
    tm, tn = _tile(m, 1024, 128), _tile(n, 1024, 128)
    tk = _tile(k, (4096 if k <= 4096 else 2048) if a.dtype == BF16 and b.dtype == BF16 else 1024, 128)
    if mode == "nn":
        assert a_off % tk == 0 and b_off % tn == 0 and br_off % tk == 0
        a_spec = pl.BlockSpec((tm, tk), lambda i, j, kk: (i, kk + a_off // tk))
        b_spec = pl.BlockSpec((tk, tn), lambda i, j, kk: (kk + br_off // tk, j + b_off // tn))
    elif mode == "nt":
        assert a_off % tk == 0 and b_off % tk == 0
        a_spec = pl.BlockSpec((tm, tk), lambda i, j, kk: (i, kk + a_off // tk))
        b_spec = pl.BlockSpec((tn, tk), lambda i, j, kk: (j, kk + b_off // tk))
    else:
        assert a_off % tm == 0 and b_off % tn == 0
        a_spec = pl.BlockSpec((tk, tm), lambda i, j, kk: (kk, i + a_off // tm))
        b_spec = pl.BlockSpec((tk, tn), lambda i, j, kk: (kk, j + b_off // tn))
    nk = k // tk
    if finish is None:
        out_dtypes = [out_dtype]
        if add is None:
            finish = lambda acc, vals: (acc,)
        else:
            tiles = [(add, 0)]
            finish = lambda acc, vals: (acc + vals[0].astype(F32),)
    n_tiles, n_out = len(tiles), len(out_dtypes)
    extra = [] if after is None else [after]
    n_b = 1 if b2_cols is None else 2
    first_out = 1 + n_b + n_tiles + len(extra)

    def body(*refs):
        a_ref, b_refs = refs[0], refs[1:1 + n_b]
        tile_refs, out_refs = refs[1 + n_b:1 + n_b + n_tiles], refs[first_out:first_out + n_out]
        acc_refs = refs[-n_b:] if nk > 1 else ()
        kk = pl.program_id(2)
        first_tile = (pl.program_id(0) == 0) & (pl.program_id(1) == 0)

        av = a_ref[...]
        prods = [_dot(av, b_ref[...], mode) for b_ref in b_refs]
        if nk > 1:
            @pl.when(kk == 0)
            def _():
                for acc_ref, p in zip(acc_refs, prods):
                    acc_ref[...] = p

            @pl.when((kk > 0) & (kk < nk - 1))
            def _():
                for acc_ref, p in zip(acc_refs, prods):
                    acc_ref[...] += p

        @pl.when(kk == nk - 1)
        def _():
            accs = prods if nk == 1 else [acc_ref[...] + p for acc_ref, p in zip(acc_refs, prods)]
            res = finish(accs[0] if n_b == 1 else accs, [t[...] for t in tile_refs])
            for o_ref, r in zip(out_refs, res):
                o_ref[...] = r.astype(o_ref.dtype)
            if row_sum:
                row_ref = refs[first_out + n_out]

                @pl.when(first_tile)
                def _():
                    row_ref[...] = res[n_out]

                @pl.when(jnp.logical_not(first_tile))
                def _():
                    row_ref[...] += res[n_out]

    in_specs = [a_spec, b_spec]
    args = [a, b]
    if b2_cols is not None:
        assert mode == "nn" and b2_cols[1] == n and b2_cols[0] % tn == 0
        in_specs.append(pl.BlockSpec((tk, tn), lambda i, j, kk: (kk + br_off // tk, j + b2_cols[0] // tn)))
        args.append(b)
    for arr, off in tiles:
        assert off % tn == 0
        in_specs.append(pl.BlockSpec((tm, tn), lambda i, j, kk, off=off: (i, j + off // tn)))
        args.append(arr)
    in_specs += [ANY] * len(extra)
    args += extra
    out_specs = [pl.BlockSpec((tm, tn), lambda i, j, kk: (i, j))] * n_out
    out_shape = [jax.ShapeDtypeStruct((m, n), dt) for dt in out_dtypes]
    if row_sum:
        out_specs.append(pl.BlockSpec((1, HEAD_DIM), lambda i, j, kk: (0, 0)))
        out_shape.append(jax.ShapeDtypeStruct((1, HEAD_DIM), F32))
    res = pl.pallas_call(
        body, name=name, grid=(m // tm, n // tn, nk),
        in_specs=in_specs, out_specs=out_specs, out_shape=out_shape,
        scratch_shapes=[pltpu.VMEM((tm, tn), F32)] * (n_b if nk > 1 else 0),
        compiler_params=_params(("arbitrary",) * 3 if row_sum else ("parallel", "parallel", "arbitrary")),
    )(*args)
    return res[0] if len(res) == 1 else res


def _rmsnorm_fwd(x, w, name, after=None):
    s, d = x.shape
    tr = _tile(s, 512, 8)

    def body(x_ref, w_ref, *rest):
        h_ref = rest[-1]
        xv = x_ref[...]
        rstd = lax.rsqrt(jnp.mean(xv * xv, axis=-1, keepdims=True) + EPS)
        h_ref[...] = (xv * rstd * w_ref[...]).astype(BF16)

    extra = [] if after is None else [after]
    return pl.pallas_call(
        body, name=name, grid=(s // tr,),
        in_specs=[pl.BlockSpec((tr, d), lambda i: (i, 0)), pl.BlockSpec((1, d), lambda i: (0, 0))] + [ANY] * len(extra),
        out_specs=pl.BlockSpec((tr, d), lambda i: (i, 0)),
        out_shape=jax.ShapeDtypeStruct((s, d), BF16),
        compiler_params=_params(("parallel",)),
    )(x, w, *extra)


def _rmsnorm_bwd(dh, x, w, res, name, after=None):
    s, d = x.shape
    tr = _tile(s, 256, 8)

    def body(dh_ref, x_ref, w_ref, res_ref, *rest):
        dx_ref, dx16_ref, dw_ref = rest[-3:]
        xv = x_ref[...]
        rstd = lax.rsqrt(jnp.mean(xv * xv, axis=-1, keepdims=True) + EPS)
        xhat = xv * rstd
        dhv = dh_ref[...]
        gw = dhv * w_ref[...]
        dx = res_ref[...] + rstd * (gw - xhat * jnp.mean(gw * xhat, axis=-1, keepdims=True))
        dx_ref[...] = dx
        dx16_ref[...] = dx.astype(BF16)

        @pl.when(pl.program_id(0) == 0)
        def _():
            dw_ref[...] = jnp.zeros_like(dw_ref)

        dw_ref[...] += jnp.sum(dhv * xhat, axis=0, keepdims=True)

    row = pl.BlockSpec((tr, d), lambda i: (i, 0))
    vec = pl.BlockSpec((1, d), lambda i: (0, 0))
    extra = [] if after is None else [after]
    return pl.pallas_call(
        body, name=name, grid=(s // tr,),
        in_specs=[row, row, vec, row] + [ANY] * len(extra), out_specs=[row, row, vec],
        out_shape=[jax.ShapeDtypeStruct((s, d), F32), jax.ShapeDtypeStruct((s, d), BF16),
                   jax.ShapeDtypeStruct((1, d), F32)],
        compiler_params=_params(("arbitrary",)),
    )(dh, x, w, res, *extra)


def _conv_taps(x, w, rows):
    shifted = [x]
    for sft in (1, 2, 3):
        shifted.append(jnp.where(rows >= sft, pltpu.roll(x, sft, 0), 0.0))
    y = w[3:4, :] * shifted[0] + w[2:3, :] * shifted[1] + w[1:2, :] * shifted[2] + w[0:1, :] * shifted[3]
    return y, shifted


def _delta_pre_fwd(qkvz, conv_w, heads):
    s = qkvz.shape[0]
    nblk = 3 * heads

    def body(x_ref, w_ref, o_ref):
        part = pl.program_id(0) // heads
        rows = lax.broadcasted_iota(jnp.int32, (s, HEAD_DIM), 0)
        y, _ = _conv_taps(x_ref[...], w_ref[...], rows)
        a = y * _sigmoid(y)
        rs = lax.rsqrt(jnp.sum(a * a, axis=-1, keepdims=True) + EPS)
        fac = jnp.where(part == 0, rs * (HEAD_DIM ** -0.5), jnp.where(part == 1, rs, 1.0))
        o_ref[...] = a * fac

    return pl.pallas_call(
        body, name="delta_pre_fwd", grid=(nblk,),
        in_specs=[pl.BlockSpec((s, HEAD_DIM), lambda i: (0, i)), pl.BlockSpec((4, HEAD_DIM), lambda i: (0, i))],
        out_specs=pl.BlockSpec((s, HEAD_DIM), lambda i: (0, i)),
        out_shape=jax.ShapeDtypeStruct((s, 3 * heads * HEAD_DIM), F32),
        compiler_params=_params(("parallel",)),
    )(qkvz, conv_w)


def _delta_pre_bwd(dqkv, qkvz, conv_w, heads):
    s = qkvz.shape[0]
    nblk = 3 * heads

    def body(d_ref, x_ref, w_ref, dx_ref, dw_ref):
        part = pl.program_id(0) // heads
        rows = lax.broadcasted_iota(jnp.int32, (s, HEAD_DIM), 0)
        w = w_ref[...]
        y, shifted = _conv_taps(x_ref[...], w, rows)
        sg = _sigmoid(y)
        a = y * sg
        rs = lax.rsqrt(jnp.sum(a * a, axis=-1, keepdims=True) + EPS)
        unit = a * rs
        dn = d_ref[...]
        scale = jnp.where(part == 0, HEAD_DIM ** -0.5, 1.0)
        da_norm = scale * rs * (dn - unit * jnp.sum(dn * unit, axis=-1, keepdims=True))
        da = jnp.where(part < 2, da_norm, dn)
        dy = da * sg * (1.0 + y * (1.0 - sg))
        dx = w[3:4, :] * dy
        for sft in (1, 2, 3):
            dx = dx + w[3 - sft:4 - sft, :] * jnp.where(rows < s - sft, pltpu.roll(dy, s - sft, 0), 0.0)
        dx_ref[...] = dx.astype(BF16)
        for sft in range(4):
            dw_ref[3 - sft:4 - sft, :] = jnp.sum(dy * shifted[sft], axis=0, keepdims=True)

    col = pl.BlockSpec((s, HEAD_DIM), lambda i: (0, i))
    wsp = pl.BlockSpec((4, HEAD_DIM), lambda i: (0, i))
    return pl.pallas_call(
        body, name="delta_pre_bwd", grid=(nblk,),
        in_specs=[col, col, wsp], out_specs=[col, wsp],
        out_shape=[jax.ShapeDtypeStruct((s, 3 * heads * HEAD_DIM), BF16),
                   jax.ShapeDtypeStruct((4, 3 * heads * HEAD_DIM), F32)],
        compiler_params=_params(("parallel",)),
    )(dqkv, qkvz, conv_w)


def _heads_of(ref, heads):
    return jnp.stack([ref[:, h * HEAD_DIM:(h + 1) * HEAD_DIM] for h in range(heads)])


def _chunk_common(q, k, v, bd, a_log, dt_bias, heads, solved=None):
    c = CHUNK
    braw = jnp.stack([bd[:, h:h + 1] for h in range(heads)])
    draw = jnp.stack([bd[:, heads + h:heads + h + 1] for h in range(heads)])
    beta = _sigmoid(braw)
    xd = draw + dt_bias
    sp = jnp.maximum(xd, 0.0) + jnp.log1p(jnp.exp(-jnp.abs(xd)))
    g = -jnp.exp(a_log) * sp
    row = lax.broadcasted_iota(jnp.int32, (c, c), 0)
    col = lax.broadcasted_iota(jnp.int32, (c, c), 1)
    sq = (heads, c, c)
    g_b = jnp.broadcast_to(g, sq)
    g_row = _csum(jnp.where(row == col, g_b, 0.0))
    gam_col = _rsum(jnp.where(col <= row, jnp.broadcast_to(g_row, sq), 0.0))
    gam_row = _csum(jnp.where(row <= col, g_b, 0.0))
    causal = row >= col
    dm = jnp.where(causal, jnp.exp(jnp.where(causal, gam_col - gam_row, 0.0)), 0.0)
    kk = _dot(k, k, "nt")
    e = jnp.exp(gam_col)
    if solved is None:
        low = jnp.where(row > col, beta * kk * dm, 0.0)
        assert c in (INV_BLOCK, 2 * INV_BLOCK)
        same = (row // INV_BLOCK) == (col // INV_BLOCK)
        diag = jnp.where(same, low, 0.0)
        t = jnp.where(row == col, 1.0, 0.0) - diag
        pw = diag
        for _ in range((INV_BLOCK - 1).bit_length() - 1):
            pw = _dot(pw, pw)
            t = t + _dot(t, pw)
        if c > INV_BLOCK:
            t = t - _dot(_dot(t, low - diag), t)
        u = _dot(t, beta * v)
        w = _dot(t, (beta * e) * k)
    else:
        t, u, w = solved
    qk_raw = _dot(q, k, "nt")
    gl = _csum(g)
    el = jnp.exp(gl - gam_col)
    return dict(beta=beta, xd=xd, g=g, row=row, col=col, dm=dm, kk=kk, t=t, e=e, u=u, w=w,
                qk_raw=qk_raw, qk=qk_raw * dm, gl=gl, el=el, qd=e * q, kd=el * k, cd=jnp.exp(gl))


def _delta_chunk_fwd(qkv, bd, a_log, dt_bias, heads):
    s = qkv.shape[0]
    n = s // CHUNK
    dw = heads * HEAD_DIM
    blk = lambda part: pl.BlockSpec((CHUNK, dw), lambda i: (i, part))

    def body(q_ref, k_ref, v_ref, bd_ref, al_ref, dt_ref, o_ref, st_ref, t_ref, uw_ref, state):
        @pl.when(pl.program_id(0) == 0)
        def _():
            state[...] = jnp.zeros_like(state)

        cm = _chunk_common(_heads_of(q_ref, heads), _heads_of(k_ref, heads), _heads_of(v_ref, heads), bd_ref[...],
                           al_ref[...], dt_ref[...], heads)
        st = state[...]
        st_ref[0] = st
        t_ref[0] = cm["t"]
        uw_ref[0, 0] = cm["u"]
        uw_ref[0, 1] = cm["w"]
        vn = cm["u"] - _dot(cm["w"], st)
        o = _dot(cm["qd"], st) + _dot(cm["qk"], vn)
        for h in range(heads):
            o_ref[:, h * HEAD_DIM:(h + 1) * HEAD_DIM] = o[h]
        state[...] = cm["cd"] * st + _dot(cm["kd"], vn, "tn")

    smem = pl.BlockSpec((heads, 1, 1), lambda i: (0, 0, 0))
    return pl.pallas_call(
        body, name="delta_chunk_fwd", grid=(n,),
        in_specs=[blk(0), blk(1), blk(2), pl.BlockSpec((CHUNK, HEAD_DIM), lambda i: (i, 0)), smem, smem],
        out_specs=[pl.BlockSpec((CHUNK, dw), lambda i: (i, 0)),
                   pl.BlockSpec((1, heads, HEAD_DIM, HEAD_DIM), lambda i: (i, 0, 0, 0)),
                   pl.BlockSpec((1, heads, CHUNK, CHUNK), lambda i: (i, 0, 0, 0)),
                   pl.BlockSpec((1, 2, heads, CHUNK, HEAD_DIM), lambda i: (i, 0, 0, 0, 0))],
        out_shape=[jax.ShapeDtypeStruct((s, dw), F32),
                   jax.ShapeDtypeStruct((n, heads, HEAD_DIM, HEAD_DIM), F32),
                   jax.ShapeDtypeStruct((n, heads, CHUNK, CHUNK), F32),
                   jax.ShapeDtypeStruct((n, 2, heads, CHUNK, HEAD_DIM), F32)],
        scratch_shapes=[pltpu.VMEM((heads, HEAD_DIM, HEAD_DIM), F32)],
        compiler_params=_params(("arbitrary",)),
    )(qkv, qkv, qkv, bd, a_log, dt_bias)


def _delta_chunk_bwd(do, qkv, bd, saved, a_log, dt_bias, heads):
    s = qkv.shape[0]
    n = s // CHUNK
    dw = heads * HEAD_DIM
    c = CHUNK
    blk = lambda part: pl.BlockSpec((CHUNK, dw), lambda i: (n - 1 - i, part))

    def all_heads(q, k, v, dov, st, solved, dsn, bd, a_log, dt_bias):
        cm = _chunk_common(q, k, v, bd, a_log, dt_bias, heads, solved)
        beta, e, dm, row, col = cm["beta"], cm["e"], cm["dm"], cm["row"], cm["col"]
        sq = (heads, c, c)
        vn = cm["u"] - _dot(cm["w"], st)
        dvn = _dot(cm["kd"], dsn)
        dkd = _dot(vn, dsn, "nt")
        dcd = _csum(_rsum(st * dsn))
        ds = cm["cd"] * dsn
        dqd = _dot(dov, st, "nt")
        ds = ds + _dot(cm["qd"], dov, "tn")
        dqk = _dot(dov, vn, "nt")
        dvn = dvn + _dot(cm["qk"], dov, "tn")
        dw_ = -_dot(dvn, st, "nt")
        ds = ds - _dot(cm["w"], dvn, "tn")
        drhs_u = _dot(cm["t"], dvn, "tn")
        drhs_w = _dot(cm["t"], dw_, "tn")
        da = -(_dot(drhs_u, cm["u"], "nt") + _dot(drhs_w, cm["w"], "nt"))
        dl = jnp.where(row > col, da, 0.0)
        dbeta = _rsum(dl * cm["kk"] * dm)
        dkk = dl * beta * dm
        dd = dl * beta * cm["kk"]
        dv = beta * drhs_u
        ek = e * k
        dbeta = dbeta + _rsum(drhs_u * v) + _rsum(drhs_w * ek)
        dk = (beta * e) * drhs_w
        dgam = _rsum(drhs_w * (beta * ek))
        dqkm = dqk * dm
        dq = _dot(dqkm, k)
        dk = dk + _dot(dqkm, q, "tn")
        dd = dd + dqk * cm["qk_raw"]
        dk = dk + _dot(dkk, k) + _dot(dkk, k, "tn")
        dq = dq + e * dqd
        dgam = dgam + _rsum(dqd * cm["qd"])
        dk = dk + cm["el"] * dkd
        r = _rsum(dkd * cm["kd"])
        dgam = dgam - r
        dgl = _csum(r) + dcd * cm["cd"]
        mm = dd * dm
        colsum_c = _rsum(jnp.where(row == col, jnp.broadcast_to(_csum(mm), sq), 0.0))
        dgam = dgam + _rsum(mm) - colsum_c
        ridx = lax.broadcasted_iota(jnp.int32, (c, 1), 0)
        dgam = dgam + jnp.where(ridx == c - 1, dgl, 0.0)
        dgam_row = _csum(jnp.where(row == col, jnp.broadcast_to(dgam, sq), 0.0))
        dg = _rsum(jnp.where(col >= row, jnp.broadcast_to(dgam_row, sq), 0.0))
        d_xd = dg * (-jnp.exp(a_log)) * _sigmoid(cm["xd"])
        d_braw = dbeta * beta * (1.0 - beta)
        d_alog = dg * cm["g"]
        lane = lax.broadcasted_iota(jnp.int32, (c, HEAD_DIM), 1)
        dbd = jnp.zeros((c, HEAD_DIM), F32)
        for h in range(heads):
            dbd = (dbd + jnp.where(lane == h, d_braw[h], 0.0) + jnp.where(lane == h + heads, d_xd[h], 0.0)
                   + jnp.where(lane == h + 2 * heads, d_alog[h], 0.0))
        return dq, dk, dv, ds, dbd

    def body(do_ref, q_ref, k_ref, v_ref, bd_ref, st_ref, t_ref, uw_ref, al_ref, dt_ref, dqkv_ref, dbd_ref, dstate):
        @pl.when(pl.program_id(0) == 0)
        def _():
            dstate[...] = jnp.zeros_like(dstate)

        dq, dk, dv, ds, dbd = all_heads(_heads_of(q_ref, heads), _heads_of(k_ref, heads), _heads_of(v_ref, heads),
                                        _heads_of(do_ref, heads), st_ref[0], (t_ref[0], uw_ref[0, 0], uw_ref[0, 1]),
                                        dstate[...], bd_ref[...],
                                        al_ref[...], dt_ref[...])
        for part, val in enumerate((dq, dk, dv)):
            for h in range(heads):
                lo = part * dw + h * HEAD_DIM
                dqkv_ref[:, lo:lo + HEAD_DIM] = val[h]
        dstate[...] = ds
        dbd_ref[...] = dbd

    smem = pl.BlockSpec((heads, 1, 1), lambda i: (0, 0, 0))
    shared = pl.BlockSpec((CHUNK, HEAD_DIM), lambda i: (n - 1 - i, 0))
    wide = pl.BlockSpec((CHUNK, dw), lambda i: (n - 1 - i, 0))
    return pl.pallas_call(
        body, name="delta_chunk_bwd", grid=(n,),
        in_specs=[wide, blk(0), blk(1), blk(2), shared,
                  pl.BlockSpec((1, heads, HEAD_DIM, HEAD_DIM), lambda i: (n - 1 - i, 0, 0, 0)),
                  pl.BlockSpec((1, heads, CHUNK, CHUNK), lambda i: (n - 1 - i, 0, 0, 0)),
                  pl.BlockSpec((1, 2, heads, CHUNK, HEAD_DIM), lambda i: (n - 1 - i, 0, 0, 0, 0)), smem, smem],
        out_specs=[pl.BlockSpec((CHUNK, 3 * dw), lambda i: (n - 1 - i, 0)), shared],
        out_shape=[jax.ShapeDtypeStruct((s, 3 * dw), F32), jax.ShapeDtypeStruct((s, HEAD_DIM), F32)],
        scratch_shapes=[pltpu.VMEM((heads, HEAD_DIM, HEAD_DIM), F32)],
        compiler_params=_params(("arbitrary",)),
    )(do, qkv, qkv, qkv, bd, *saved, a_log, dt_bias)


def _delta_post_fwd(o, qkvz, w, heads):
    s = o.shape[0]
    tr = _tile(s, 1024, 8)

    def body(o_ref, z_ref, w_ref, out_ref):
        ov, z = o_ref[...], z_ref[...]
        rstd = lax.rsqrt(jnp.mean(ov * ov, axis=-1, keepdims=True) + EPS)
        out_ref[...] = (ov * rstd * w_ref[...] * (z * _sigmoid(z))).astype(BF16)

    return pl.pallas_call(
        body, name="delta_post_fwd", grid=(s // tr, heads),
        in_specs=[pl.BlockSpec((tr, HEAD_DIM), lambda i, h: (i, h)),
                  pl.BlockSpec((tr, HEAD_DIM), lambda i, h: (i, 3 * heads + h)),
                  pl.BlockSpec((1, HEAD_DIM), lambda i, h: (0, 0))],
        out_specs=pl.BlockSpec((tr, HEAD_DIM), lambda i, h: (i, h)),
        out_shape=jax.ShapeDtypeStruct((s, heads * HEAD_DIM), BF16),
        compiler_params=_params(("parallel", "parallel")),
    )(o, qkvz, w)


def _delta_post_bwd(dmix, o, qkvz, w, heads):
    s = o.shape[0]
    tr = _tile(s, 1024, 8)

    def body(d_ref, o_ref, z_ref, w_ref, do_ref, dz_ref, dw_ref):
        d, ov, z, wv = d_ref[...], o_ref[...], z_ref[...], w_ref[...]
        sg = _sigmoid(z)
        rstd = lax.rsqrt(jnp.mean(ov * ov, axis=-1, keepdims=True) + EPS)
        ohat = ov * rstd
        dz_ref[...] = (d * (ohat * wv) * sg * (1.0 + z * (1.0 - sg))).astype(BF16)
        dn = d * (z * sg)
        gw = dn * wv
        do_ref[...] = rstd * (gw - ohat * jnp.mean(gw * ohat, axis=-1, keepdims=True))

        @pl.when((pl.program_id(0) == 0) & (pl.program_id(1) == 0))
        def _():
            dw_ref[...] = jnp.zeros_like(dw_ref)

        dw_ref[...] += jnp.sum(dn * ohat, axis=0, keepdims=True)

    head = pl.BlockSpec((tr, HEAD_DIM), lambda i, h: (i, h))
    vec = pl.BlockSpec((1, HEAD_DIM), lambda i, h: (0, 0))
    dw = heads * HEAD_DIM
    return pl.pallas_call(
        body, name="delta_post_bwd", grid=(s // tr, heads),
        in_specs=[head, head, pl.BlockSpec((tr, HEAD_DIM), lambda i, h: (i, 3 * heads + h)), vec],
        out_specs=[head, head, vec],
        out_shape=[jax.ShapeDtypeStruct((s, dw), F32), jax.ShapeDtypeStruct((s, dw), BF16),
                   jax.ShapeDtypeStruct((1, HEAD_DIM), F32)],
        compiler_params=_params(("arbitrary", "arbitrary")),
    )(dmix, o, qkvz, w)


def _rope_tables(positions, s):
    half = HEAD_DIM // 2
    inv_freq = ROPE_THETA ** (-jnp.arange(half, dtype=F32) / half)
    ang = positions.reshape(s, 1).astype(F32) * inv_freq
    cos, sin = jnp.cos(ang), jnp.sin(ang)
    return jnp.concatenate([cos, cos], axis=-1), jnp.concatenate([-sin, sin], axis=-1)


def _attn_pre_fwd(aqkv, wq, wk, cosf, sinf, heads):
    s = aqkv.shape[0]
    tr = _tile(s, 1024, 8)

    def body(x_ref, wq_ref, wk_ref, c_ref, s_ref, o_ref):
        xv = x_ref[...]
        wv = jnp.where(pl.program_id(1) < heads, wq_ref[...], wk_ref[...])
        y = xv * lax.rsqrt(jnp.mean(xv * xv, axis=-1, keepdims=True) + EPS) * wv
        o_ref[...] = y * c_ref[...] + pltpu.roll(y, HEAD_DIM // 2, 1) * s_ref[...]

    blk = pl.BlockSpec((tr, HEAD_DIM), lambda i, j: (i, j))
    vec = pl.BlockSpec((1, HEAD_DIM), lambda i, j: (0, 0))
    tab = pl.BlockSpec((tr, HEAD_DIM), lambda i, j: (i, 0))
    return pl.pallas_call(
        body, name="attn_pre_fwd", grid=(s // tr, 2 * heads),
        in_specs=[blk, vec, vec, tab, tab], out_specs=blk,
        out_shape=jax.ShapeDtypeStruct((s, 2 * heads * HEAD_DIM), F32),
        compiler_params=_params(("parallel", "parallel")),
    )(aqkv, wq, wk, cosf, sinf)


def _band():
    qi = lax.broadcasted_iota(jnp.int32, (SPAN, 2 * SPAN), 0)
    ki = lax.broadcasted_iota(jnp.int32, (SPAN, 2 * SPAN), 1)
    dist = qi + SPAN - ki
    return (dist >= 0) & (dist <= SPAN), ki >= SPAN


def _sub(g, r, d):
    if d == 1:
        return pl.ds(g * SPAN, SPAN)
    return pl.ds(g * SPAN * d + r, SPAN, stride=d)


def _attn_blocks(s):
    assert s % (SPAN * max(DILATIONS)) == 0
    return [(p_i, d, r, g) for p_i, d in enumerate(DILATIONS) for r in range(d) for g in range(s // (SPAN * d))]


def _attn_fwd(qk, aqkv, w, heads):
    s = qk.shape[0]
    aw = heads * HEAD_DIM

    def body(q_ref, k_ref, v_ref, w_ref, mix_ref, acc_ref, m_ref, l_ref):
        band, own = _band()
        for p_i, d, r, g in _attn_blocks(s):
            rows, prows = _sub(g, r, d), _sub(max(g - 1, 0), r, d)
            mask = band if g > 0 else band & own
            kcat = jnp.concatenate([k_ref[prows, :], k_ref[rows, :]], axis=0)
            vcat = jnp.concatenate([v_ref[prows, :], v_ref[rows, :]], axis=0)
            sc = _dot(q_ref[rows, :], kcat, "nt") * (HEAD_DIM ** -0.5)
            sc = jnp.where(mask, sc, NEG)
            m = jnp.max(sc, axis=-1, keepdims=True)
            if p_i == 0:
                p = jnp.exp(sc - m)
                acc_ref[rows, :] = _dot(p, vcat)
                l_new = jnp.sum(p, axis=-1, keepdims=True)
            else:
                m_old = m_ref[rows, 0:1]
                m = jnp.maximum(m, m_old)
                alpha = jnp.exp(m_old - m)
                p = jnp.exp(sc - m)
                acc_ref[rows, :] = alpha * acc_ref[rows, :] + _dot(p, vcat)
                l_new = alpha * l_ref[rows, 0:1] + jnp.sum(p, axis=-1, keepdims=True)
            m_ref[rows, :] = jnp.broadcast_to(m, (SPAN, HEAD_DIM))
            l_ref[rows, :] = jnp.broadcast_to(l_new, (SPAN, HEAD_DIM))
        den = l_ref[...]
        ob = acc_ref[...] / den
        acc_ref[...] = ob
        m_ref[...] = m_ref[...] + jnp.log(den)
        rstd = lax.rsqrt(jnp.mean(ob * ob, axis=-1, keepdims=True) + EPS)
        mix_ref[...] = (ob * rstd * w_ref[...]).astype(BF16)

    col = lambda off: pl.BlockSpec((s, HEAD_DIM), lambda h: (0, off + h))
    return pl.pallas_call(
        body, name="attn_fwd", grid=(heads,),
        in_specs=[col(0), col(heads), col(2 * heads), pl.BlockSpec((1, HEAD_DIM), lambda h: (0, 0))],
        out_specs=[col(0), col(0), col(0)],
        out_shape=[jax.ShapeDtypeStruct((s, aw), BF16), jax.ShapeDtypeStruct((s, aw), F32),
                   jax.ShapeDtypeStruct((s, aw), F32)],
        scratch_shapes=[pltpu.VMEM((s, HEAD_DIM), F32)],
        compiler_params=_params(("parallel",)),
    )(qk, qk, aqkv, w)


def _attn_merge_bwd(dmix, ob, w, heads):
    s = ob.shape[0]
    tr = _tile(s, 1024, 8)

    def body(d_ref, ob_ref, w_ref, do_ref, dsum_ref, dw_ref):
        d, ov = d_ref[...], ob_ref[...]
        rstd = lax.rsqrt(jnp.mean(ov * ov, axis=-1, keepdims=True) + EPS)
        ohat = ov * rstd
        gw = d * w_ref[...]
        dov = rstd * (gw - ohat * jnp.mean(gw * ohat, axis=-1, keepdims=True))
        do_ref[...] = dov
        dsum_ref[...] = jnp.broadcast_to(jnp.sum(dov * ov, axis=-1, keepdims=True), dov.shape)

        @pl.when((pl.program_id(0) == 0) & (pl.program_id(1) == 0))
        def _():
            dw_ref[...] = jnp.zeros_like(dw_ref)

        dw_ref[...] += jnp.sum(d * ohat, axis=0, keepdims=True)

    blk = pl.BlockSpec((tr, HEAD_DIM), lambda i, h: (i, h))
    vec = pl.BlockSpec((1, HEAD_DIM), lambda i, h: (0, 0))
    aw = heads * HEAD_DIM
    return pl.pallas_call(
        body, name="attn_merge_bwd", grid=(s // tr, heads),
        in_specs=[pl.BlockSpec((tr, HEAD_DIM), lambda i, h: (i, heads + h)), blk, vec],
        out_specs=[blk, blk, vec],
        out_shape=[jax.ShapeDtypeStruct((s, aw), F32), jax.ShapeDtypeStruct((s, aw), F32),
                   jax.ShapeDtypeStruct((1, HEAD_DIM), F32)],
        compiler_params=_params(("arbitrary", "arbitrary")),
    )(dmix, ob, w)


def _attn_bwd(qk, aqkv, do, lse, dsum, heads):
    s = qk.shape[0]
    aw = heads * HEAD_DIM
    scale = HEAD_DIM ** -0.5

    def body(q_ref, k_ref, v_ref, do_ref, l_ref, ds_ref, out_ref):
        band, own = _band()
        dq_ref, dk_ref, dv_ref = out_ref.at[0], out_ref.at[1], out_ref.at[2]
        out_ref[...] = jnp.zeros((3, s, HEAD_DIM), F32)
        for _, d, r, g in _attn_blocks(s):
            rows, prows = _sub(g, r, d), _sub(max(g - 1, 0), r, d)
            mask = band if g > 0 else band & own
            q, dov = q_ref[rows, :], do_ref[rows, :]
            kcat = jnp.concatenate([k_ref[prows, :], k_ref[rows, :]], axis=0)
            vcat = jnp.concatenate([v_ref[prows, :], v_ref[rows, :]], axis=0)
            sc = _dot(q, kcat, "nt") * scale
            p = jnp.where(mask, jnp.exp(sc - l_ref[rows, 0:1]), 0.0)
            dsc = p * (_dot(dov, vcat, "nt") - ds_ref[rows, 0:1]) * scale
            dq_ref[rows, :] += _dot(dsc, kcat)
            dk = _dot(dsc, q, "tn")
            dv = _dot(p, dov, "tn")
            dk_ref[rows, :] += dk[SPAN:]
            dv_ref[rows, :] += dv[SPAN:]
            if g > 0:
                dk_ref[prows, :] += dk[:SPAN]
                dv_ref[prows, :] += dv[:SPAN]

    col = lambda off: pl.BlockSpec((s, HEAD_DIM), lambda h: (0, off + h))
    return pl.pallas_call(
        body, name="attn_bwd", grid=(heads,),
        in_specs=[col(0), col(heads), col(2 * heads), col(0), col(0), col(0)],
        out_specs=pl.BlockSpec((3, s, HEAD_DIM), lambda h: (0, 0, h)),
        out_shape=jax.ShapeDtypeStruct((3, s, aw), F32),
        compiler_params=_params(("parallel",)),
    )(qk, qk, aqkv, do, lse, dsum)


def _attn_pre_bwd(grads, aqkv, wq, wk, cosf, sinf, heads):
    s = aqkv.shape[0]
    tr = _tile(s, 1024, 8)
    nrow = s // tr

    def body(g_ref, x_ref, wq_ref, wk_ref, c_ref, s_ref, dx_ref, dwq_ref, dwk_ref):
        i, j = pl.program_id(0), pl.program_id(1)
        kind = j // heads
        dout = g_ref[0]
        tot_v = dout
        dy = dout * c_ref[...] + pltpu.roll(dout * s_ref[...], HEAD_DIM // 2, 1)
        xv = x_ref[...]
        wv = jnp.where(kind == 0, wq_ref[...], wk_ref[...])
        rstd = lax.rsqrt(jnp.mean(xv * xv, axis=-1, keepdims=True) + EPS)
        xhat = xv * rstd
        gw = dy * wv
        dxn = rstd * (gw - xhat * jnp.mean(gw * xhat, axis=-1, keepdims=True))
        dx_ref[...] = jnp.where(kind == 2, tot_v, dxn).astype(BF16)
        dwc = jnp.sum(dy * xhat, axis=0, keepdims=True)

        @pl.when((i == 0) & (j == 0))
        def _():
            dwq_ref[...] = jnp.zeros_like(dwq_ref)
            dwk_ref[...] = jnp.zeros_like(dwk_ref)

        @pl.when(kind == 0)
        def _():
            dwq_ref[...] += dwc

        @pl.when(kind == 1)
        def _():
            dwk_ref[...] += dwc

    grad = pl.BlockSpec((1, tr, HEAD_DIM), lambda i, j: (j // heads, i, j % heads))
    blk = pl.BlockSpec((tr, HEAD_DIM), lambda i, j: (i, j))
    vec = pl.BlockSpec((1, HEAD_DIM), lambda i, j: (0, 0))
    tab = pl.BlockSpec((tr, HEAD_DIM), lambda i, j: (i, 0))
    return pl.pallas_call(
        body, name="attn_pre_bwd", grid=(nrow, 3 * heads),
        in_specs=[grad, blk, vec, vec, tab, tab], out_specs=[blk, vec, vec],
        out_shape=[jax.ShapeDtypeStruct((s, 3 * heads * HEAD_DIM), BF16),
                   jax.ShapeDtypeStruct((1, HEAD_DIM), F32), jax.ShapeDtypeStruct((1, HEAD_DIM), F32)],
        compiler_params=_params(("arbitrary", "arbitrary")),
    )(grads, aqkv, wq, wk, cosf, sinf)


def _colsum(a, name):
    s, d = a.shape
    tr = _tile(s, 1024, 8)

    def body(a_ref, o_ref):
        @pl.when(pl.program_id(0) == 0)
        def _():
            o_ref[...] = jnp.zeros_like(o_ref)

        o_ref[...] += jnp.sum(a_ref[...], axis=0, keepdims=True)

    return pl.pallas_call(
        body, name=name, grid=(s // tr,),
        in_specs=[pl.BlockSpec((tr, d), lambda i: (i, 0))], out_specs=pl.BlockSpec((1, d), lambda i: (0, 0)),
        out_shape=jax.ShapeDtypeStruct((1, d), F32),
        compiler_params=_params(("arbitrary",)),
    )(a)


def _local_step(x, positions, target, small, w_qkvz, w_bd, w_attn, conv_w, later_weights, ffn_grads_ready,
                rest_grads_ready, after=None):
    s, dmod = x.shape
    heads = dmod // (2 * HEAD_DIM)
    dw = heads * HEAD_DIM
    a_log, dt_bias = small["a_log"].reshape(heads, 1, 1), small["dt_bias"].reshape(heads, 1, 1)
    cosf, sinf = _rope_tables(positions, s)

    h1 = _rmsnorm_fwd(x, small["attn_norm_w"], "norm1_fwd", after=after)
    qkvz = _matmul(h1, w_qkvz, "nn", "proj_qkvz")
    bd = _matmul(h1, w_bd, "nn", "proj_bd")
    aqkv = _matmul(h1, w_attn, "nn", "proj_attn")
    dqkv = _delta_pre_fwd(qkvz, conv_w, heads)
    o_d, *saved = _delta_chunk_fwd(dqkv, bd, a_log, dt_bias, heads)
    mix_a = _delta_post_fwd(o_d, qkvz, small["delta_out_norm_w"], heads)
    qk_rot = _attn_pre_fwd(aqkv, small["q_norm_w"], small["k_norm_w"], cosf, sinf, heads)
    mix_b, ob, lse = _attn_fwd(qk_rot, aqkv, small["attn_out_norm_w"], heads)
    w_out, behind, ffn_weights = later_weights((mix_a, mix_b))
    x1 = _matmul(mix_a, w_out, "nn", "out_proj_a", add=x, b_rows=(0, dw), after=behind)
    x1 = _matmul(mix_b, w_out, "nn", "out_proj_b", add=x1, b_rows=(dw, dw))
    h2 = _rmsnorm_fwd(x1, small["ffn_norm_w"], "norm2_fwd")
    w_gu, w_down = ffn_weights(h2)
    ff = w_down.shape[0]

    def swiglu(accs, vals):
        g, u = accs
        return g, u, g * _sigmoid(g) * u

    gate, up, act = _matmul(h2, w_gu, "nn", "ffn_gate_up", b_cols=(0, ff), b2_cols=(ff, ff), finish=swiglu,
                            out_dtypes=[BF16, BF16, BF16])

    def loss_head(acc, vals):
        err = acc + vals[0] - vals[1]
        part = 0.5 * jnp.sum(jnp.sum(err * err, axis=-1, keepdims=True) * (1.0 / dmod), axis=0, keepdims=True)
        lane = lax.broadcasted_iota(jnp.int32, (1, HEAD_DIM), 1)
        return err * (1.0 / dmod), err * (1.0 / dmod), jnp.where(lane == 0, part, 0.0)

    dy, dy16, loss_row = _matmul(act, w_down, "nn", "ffn_down", tiles=[(x1, 0), (target, 0)], finish=loss_head,
                                 out_dtypes=[F32, BF16], row_sum=True)

    def swiglu_bwd(acc, vals):
        g, u = vals[0].astype(F32), vals[1].astype(F32)
        sg = _sigmoid(g)
        return acc * u * sg * (1.0 + g * (1.0 - sg)), acc * g * sg

    dgate, dup = _matmul(dy16, w_down, "nt", "ffn_down_dx", tiles=[(gate, 0), (up, 0)], finish=swiglu_bwd,
                         out_dtypes=[BF16, BF16])
    g_w_down = _matmul(act, dy16, "tn", "ffn_down_dw")
    g_w_gate = _matmul(h2, dgate, "tn", "ffn_gate_dw")
    g_w_up = _matmul(h2, dup, "tn", "ffn_up_dw")
    behind, and_then = ffn_grads_ready(g_w_gate, g_w_up, g_w_down)
    dh2 = _matmul(dgate, w_gu, "nt", "ffn_gate_dx", b_cols=(0, ff), after=behind)
    dh2 = _matmul(dup, w_gu, "nt", "ffn_up_dx", b_cols=(ff, ff), add=dh2)
    behind = and_then(dh2)
    dx1, dx1_16, g_ffn_norm = _rmsnorm_bwd(dh2, x1, small["ffn_norm_w"], dy, "norm2_bwd", after=behind)
    dmix = _matmul(dx1_16, w_out, "nt", "out_proj_dx")
    g_w_out_a = _matmul(mix_a, dx1_16, "tn", "out_proj_dw_a")
    g_w_out_b = _matmul(mix_b, dx1_16, "tn", "out_proj_dw_b")
    dob, dsum, g_attn_out_norm = _attn_merge_bwd(dmix, ob, small["attn_out_norm_w"], heads)
    grads = _attn_bwd(qk_rot, aqkv, dob, lse, dsum, heads)
    d_aqkv, g_q_norm, g_k_norm = _attn_pre_bwd(grads, aqkv, small["q_norm_w"], small["k_norm_w"], cosf, sinf, heads)
    do_d, dz, g_delta_out_norm = _delta_post_bwd(dmix, o_d, qkvz, small["delta_out_norm_w"], heads)
    ddqkv, dbd = _delta_chunk_bwd(do_d, dqkv, bd, saved, a_log, dt_bias, heads)
    d_qkv_raw, g_conv = _delta_pre_bwd(ddqkv, qkvz, conv_w, heads)
    bd_sums = _colsum(dbd, "bd_colsum")
    g_w_qkv = _matmul(h1, d_qkv_raw, "tn", "proj_qkv_dw")
    g_w_z = _matmul(h1, dz, "tn", "proj_z_dw")
    g_w_bd = _matmul(h1, dbd, "tn", "proj_bd_dw")
    g_w_attn = _matmul(h1, d_aqkv, "tn", "proj_attn_dw")
    behind, and_then = rest_grads_ready(dict(w_qkv=g_w_qkv, w_z=g_w_z, w_bd=g_w_bd, w_attn=g_w_attn,
                                             w_out_a=g_w_out_a, w_out_b=g_w_out_b))
    dh1 = _matmul(d_qkv_raw, w_qkvz, "nt", "proj_qkv_dx", b_cols=(0, 3 * dw), after=behind)
    dh1 = _matmul(dz, w_qkvz, "nt", "proj_z_dx", b_cols=(3 * dw, dw), add=dh1, after=and_then(dh1))
    dh1 = _matmul(d_aqkv, w_attn, "nt", "proj_attn_dx", add=dh1)
    dh1 = _matmul(dbd, w_bd, "nt", "proj_bd_dx", add=dh1)
    grad_x, _, g_attn_norm = _rmsnorm_bwd(dh1, x, small["attn_norm_w"], dx1, "norm1_bwd")
    small_grads = dict(
        attn_norm_w=g_attn_norm, a_log=bd_sums[:, 2 * heads:3 * heads], dt_bias=bd_sums[:, heads:2 * heads],
        delta_out_norm_w=g_delta_out_norm, q_norm_w=g_q_norm, k_norm_w=g_k_norm,
        attn_out_norm_w=g_attn_out_norm, ffn_norm_w=g_ffn_norm, conv_w=g_conv)
    return loss_row, grad_x, small_grads


def _adamw(w, g, m, v, name):
    r, c = w.shape
    tr = _tile(r, 256, 8)

    def body(w_ref, g_ref, m_ref, v_ref, d_ref, nm_ref, nv_ref):
        gv = g_ref[...]
        nm = ADAM_B1 * m_ref[...] + (1.0 - ADAM_B1) * gv
        nv = ADAM_B2 * v_ref[...] + (1.0 - ADAM_B2) * (gv * gv)
        m_hat = nm / (1.0 - ADAM_B1 ** ADAM_STEP)
        v_hat = nv / (1.0 - ADAM_B2 ** ADAM_STEP)
        d_ref[...] = -ADAM_LR * (m_hat / (jnp.sqrt(v_hat) + ADAM_EPS) + ADAM_WD * w_ref[...])
        nm_ref[...] = nm
        nv_ref[...] = nv

    blk = pl.BlockSpec((tr, c), lambda i: (i, 0))
    return pl.pallas_call(
        body, name=name, grid=(r // tr,),
        in_specs=[blk] * 4, out_specs=[blk] * 3,
        out_shape=[jax.ShapeDtypeStruct((r, c), F32)] * 3,
        compiler_params=_params(("parallel",)),
    )(w, g, m, v)


def _add_half_bf16(g, b, place, name):
    n, half, c = b.shape
    tr = _tile(half, 512, 16)
    nb = half // tr

    def body(place_ref, g_ref, b_ref, o_ref):
        o_ref[...] = (g_ref[...].astype(F32) + b_ref[...].astype(F32)).astype(BF16)

    blk = pl.BlockSpec((1, tr, c), lambda i, j, p: (i, j, 0))
    return pl.pallas_call(
        body, name=name,
        grid_spec=pltpu.PrefetchScalarGridSpec(
            num_scalar_prefetch=1, grid=(n, nb),
            in_specs=[pl.BlockSpec((1, tr, c), lambda i, j, p: (i, p[0] * nb + j, 0)), blk], out_specs=blk),
        out_shape=jax.ShapeDtypeStruct((n, half, c), BF16),
        compiler_params=_params(("parallel", "parallel")),
    )(place, g, b)


def _sum4_f32(mine, others, place, name):
    _, half, c = mine.shape
    tr = _tile(half, 512, 16)
    nb = half // tr

    def body(place_ref, a_ref, b_ref, o_ref):
        acc = a_ref[0].astype(F32)
        for j in range(3):
            acc = acc + b_ref[j].astype(F32)
        o_ref[...] = acc

    return pl.pallas_call(
        body, name=name,
        grid_spec=pltpu.PrefetchScalarGridSpec(
            num_scalar_prefetch=1, grid=(nb,),
            in_specs=[pl.BlockSpec((1, tr, c), lambda i, p: (p[1], i, 0)),
                      pl.BlockSpec((3, tr, c), lambda i, p: (0, i, 0))],
            out_specs=pl.BlockSpec((tr, c), lambda i, p: (p[0] * nb + i, 0))),
        out_shape=jax.ShapeDtypeStruct((2 * half, c), F32),
        compiler_params=_params(("parallel",)),
    )(place, mine, others)


def _place():
    x, y, c = lax.axis_index("x"), lax.axis_index("y"), lax.axis_index("c")
    other_chips = [(1 - x, y), (x, 1 - y), (1 - x, 1 - y)]
    return x, y, c, (x, y, 1 - c), other_chips


ANY = pl.BlockSpec(memory_space=pl.ANY)


def _remote(k, src, dst, to, send_sems, recv_sems):
    return pltpu.make_async_remote_copy(src_ref=src, dst_ref=dst, send_sem=send_sems.at[k], recv_sem=recv_sems.at[k],
                                        device_id=to, device_id_type=MESH)


def _half(ref, lead, hc):
    if lead is None:
        half = ref.shape[0] // 2
        return ref.at[pl.ds(hc * half, half), :]
    half = ref.shape[1] // 2
    return ref.at[lead, pl.ds(hc * half, half), :]


def _all_gather_weights(slots, whole, name):
    nt, nw = len(slots), len(whole)
    base_w, base_f, base_d = 2 * nt, 2 * nt + 3 * nw, 4 * nt + 3 * nw

    def quarter(ref, lead, hc, q):
        quart = ref.shape[1] // 4
        return ref.at[lead, pl.ds((2 * hc + q) * quart, quart), :]

    def body(*refs):
        ins, outs = refs[:nt + nw], refs[nt + nw:2 * (nt + nw)]
        sems = refs[2 * (nt + nw):]
        x, y, c, sibling, chips = _place()
        me, xn, yn, dg = 2 * x + y, 2 * (1 - x) + y, 2 * x + 1 - y, 2 * (1 - x) + 1 - y
        to_x, to_y = (1 - x, y, c), (x, 1 - y, c)
        cps = []
        for t in range(nt):
            cps.append(_remote(2 * t, _half(ins[t], me, c), _half(outs[t], me, c), to_x, *sems))
            cps.append(_remote(2 * t + 1, _half(ins[t], me, c), _half(outs[t], me, c), to_y, *sems))
        for j, (px, py) in enumerate(chips):
            for t in range(nw):
                cps.append(_remote(base_w + j * nw + t, ins[nt + t].at[me], outs[nt + t].at[me], (px, py, c), *sems))
        for cp in cps:
            cp.start()

        def start(k, ref, to):
            cp = _remote(k, ref, ref, to, *sems)
            cp.start()
            cps.append(cp)

        for t in range(nt):
            landed = _half(outs[t], xn, c)
            _remote(2 * t, landed, landed, to_x, *sems).wait_recv()
            start(base_f + 2 * t, quarter(outs[t], xn, c, 0), to_y)
            start(base_d + 3 * t, landed, sibling)
            landed = _half(outs[t], yn, c)
            _remote(2 * t + 1, landed, landed, to_y, *sems).wait_recv()
            start(base_f + 2 * t + 1, quarter(outs[t], yn, c, 1), to_x)
            start(base_d + 3 * t + 1, landed, sibling)
        for t in range(nt):
            q0, q1 = quarter(outs[t], dg, c, 0), quarter(outs[t], dg, c, 1)
            _remote(base_f + 2 * t, q0, q0, to_y, *sems).wait_recv()
            _remote(base_f + 2 * t + 1, q1, q1, to_x, *sems).wait_recv()
            start(base_d + 3 * t + 2, _half(outs[t], dg, c), sibling)
        for j, (px, py) in enumerate(chips):
            for t in range(nw):
                landed = outs[nt + t].at[2 * px + py]
                _remote(base_w + j * nw + t, landed, landed, (px, py, c), *sems).wait_recv()
        for t in range(nt):
            for j, chip in enumerate((xn, yn, dg)):
                other = _half(outs[t], chip, 1 - c)
                _remote(base_d + 3 * t + j, other, other, sibling, *sems).wait_recv()
        for cp in cps:
            cp.wait_send()

    arrays = list(slots) + list(whole)
    n_sem = 7 * nt + 3 * nw
    return pl.pallas_call(
        body, name=name, in_specs=[ANY] * len(arrays), out_specs=[ANY] * len(arrays),
        input_output_aliases={i: i for i in range(len(arrays))},
        out_shape=[jax.ShapeDtypeStruct(a.shape, a.dtype) for a in arrays],
        scratch_shapes=[pltpu.SemaphoreType.DMA((n_sem,)), pltpu.SemaphoreType.DMA((n_sem,))],
    )(*arrays)


def _swap_halves_with_sibling(gs, name):
    nt = len(gs)

    def body(*refs):
        g_refs, o_refs, (send_sems, recv_sems) = refs[:nt], refs[nt:2 * nt], refs[2 * nt:]
        _, _, c, sibling, _ = _place()
        cps = []
        for t in range(nt):
            half = g_refs[t].shape[1] // 2
            cps.append(_remote(t, g_refs[t].at[:, pl.ds((1 - c) * half, half), :], o_refs[t], sibling,
                               send_sems, recv_sems))
        for cp in cps:
            cp.start()
        for cp in cps:
            cp.wait()

    return pl.pallas_call(
        body, name=name, in_specs=[ANY] * nt, out_specs=[ANY] * nt,
        out_shape=[jax.ShapeDtypeStruct((g.shape[0], g.shape[1] // 2, g.shape[2]), g.dtype) for g in gs],
        scratch_shapes=[pltpu.SemaphoreType.DMA((nt,)), pltpu.SemaphoreType.DMA((nt,))],
    )(*gs)


def _scatter_to_chips(ps, name):
    nt = len(ps)

    def body(*refs):
        p_refs, b_refs, (send_sems, recv_sems) = refs[:nt], refs[nt:2 * nt], refs[2 * nt:]
        _, _, c, _, chips = _place()
        cps = [_remote(j * nt + t, p_refs[t].at[2 * px + py], b_refs[t].at[j], (px, py, c), send_sems, recv_sems)
               for j, (px, py) in enumerate(chips) for t in range(nt)]
        for cp in cps:
            cp.start()
        for cp in cps:
            cp.wait()

    return pl.pallas_call(
        body, name=name, in_specs=[ANY] * nt, out_specs=[ANY] * nt,
        out_shape=[jax.ShapeDtypeStruct((3,) + p.shape[1:], p.dtype) for p in ps],
        scratch_shapes=[pltpu.SemaphoreType.DMA((3 * nt,)), pltpu.SemaphoreType.DMA((3 * nt,))],
    )(*ps)


def _join_halves(fs, name):
    nt = len(fs)

    def body(*refs):
        in_refs, out_refs, (send_sems, recv_sems) = refs[:nt], refs[nt:2 * nt], refs[2 * nt:]
        _, _, c, sibling, _ = _place()
        cps = [_remote(t, _half(in_refs[t], None, c), _half(out_refs[t], None, c), sibling, send_sems, recv_sems)
               for t in range(nt)]
        for cp in cps:
            cp.start()
        for t in range(nt):
            theirs = _half(out_refs[t], None, 1 - c)
            _remote(t, theirs, theirs, sibling, send_sems, recv_sems).wait_recv()
        for cp in cps:
            cp.wait_send()

    return pl.pallas_call(
        body, name=name, in_specs=[ANY] * nt, out_specs=[ANY] * nt,
        input_output_aliases={i: i for i in range(nt)},
        out_shape=[jax.ShapeDtypeStruct(f.shape, f.dtype) for f in fs],
        scratch_shapes=[pltpu.SemaphoreType.DMA((nt,)), pltpu.SemaphoreType.DMA((nt,))],
    )(*fs)


HBM = pl.BlockSpec(memory_space=pltpu.HBM)
SEM = pl.BlockSpec(memory_space=pltpu.SEMAPHORE)
EFFECT = pltpu.SideEffectType.DATAFLOW_SIDE_EFFECTING


def _split_start(arrays, after, plan, n_copies, name):
    na = len(arrays)

    def body(*refs):
        ins, send_sems, recv_sems = refs[:na], refs[na + 1], refs[na + 2]
        outs, token = refs[na + 3:2 * na + 3], refs[2 * na + 3]
        for k, (src, dst, to) in enumerate(plan(ins, outs)):
            _remote(k, src, dst, to, send_sems, recv_sems).start()
        token[...] = jnp.zeros_like(token)

    res = pl.pallas_call(
        body, name=name,
        out_shape=(pltpu.SemaphoreType.DMA((n_copies,)), pltpu.SemaphoreType.DMA((n_copies,)),
                   *[pltpu.HBM(a.shape, a.dtype) for a in arrays], jax.ShapeDtypeStruct((8, HEAD_DIM), F32)),
        in_specs=[HBM] * na + [ANY],
        out_specs=(SEM, SEM, *[HBM] * na, pl.BlockSpec(memory_space=pltpu.VMEM)),
        input_output_aliases={i: 2 + i for i in range(na)},
        compiler_params=pltpu.CompilerParams(has_side_effects=EFFECT),
    )(*[pltpu.with_memory_space_constraint(a, pltpu.HBM) for a in arrays], after)
    return res[0], res[1], list(res[2:2 + na]), res[2 + na]


def _split_wait(send_sems, recv_sems, arrays, after, plan, name):
    na = len(arrays)
    after = list(after) if isinstance(after, (list, tuple)) else [after]

    def body(*refs):
        ins, send, recv, outs = refs[:na], refs[na], refs[na + 1], refs[na + 2 + len(after):]
        for k, (src, dst, to) in enumerate(plan(ins, outs)):
            cp = _remote(k, src, dst, to, send, recv)
            cp.wait_send()
            cp.wait_recv()

    res = pl.pallas_call(
        body, name=name, out_shape=tuple(pltpu.HBM(a.shape, a.dtype) for a in arrays),
        in_specs=[HBM] * na + [SEM, SEM] + [ANY] * len(after), out_specs=tuple([HBM] * na),
        input_output_aliases={i: i for i in range(na)},
        compiler_params=pltpu.CompilerParams(has_side_effects=EFFECT),
    )(*arrays, send_sems, recv_sems, *after)
    return list(res)


def _gather_plan(nt):
    def plan(ins, outs):
        x, y, c, _, chips = _place()
        me = 2 * x + y
        return [(_half(ins[t], me, c), _half(outs[t], me, c), (px, py, c)) for px, py in chips for t in range(nt)]
    return plan


def _gather_landed_plan(nt):
    def plan(ins, outs):
        _, _, c, _, chips = _place()
        return [(_half(outs[t], 2 * px + py, c), _half(outs[t], 2 * px + py, c), (px, py, c))
                for px, py in chips for t in range(nt)]
    return plan


def _scatter_plan(nt):
    def plan(ins, outs):
        _, _, c, _, chips = _place()
        return [(ins[t].at[2 * px + py], outs[nt + t].at[j], (px, py, c))
                for j, (px, py) in enumerate(chips) for t in range(nt)]
    return plan


def _pass_plan(nt):
    def plan(ins, outs):
        _, _, c, sibling, chips = _place()
        return [(_half(ins[t], 2 * px + py, c), _half(outs[t], 2 * px + py, c), sibling)
                for px, py in chips for t in range(nt)]
    return plan


def _pass_landed_plan(nt):
    def plan(ins, outs):
        _, _, c, sibling, chips = _place()
        return [(_half(outs[t], 2 * px + py, c), _half(outs[t], 2 * px + py, 1 - c), sibling)
                for px, py in chips for t in range(nt)]
    return plan


def _swap_plan(nt):
    def plan(ins, outs):
        _, _, c, sibling, _ = _place()
        res = []
        for t in range(nt):
            half = ins[t].shape[1] // 2
            res.append((ins[t].at[:, pl.ds((1 - c) * half, half), :], outs[nt + t], sibling))
        return res
    return plan


def _pass_halves_to_sibling(slots, name):
    nt = len(slots)

    def body(*refs):
        ins, outs, (send_sems, recv_sems) = refs[:nt], refs[nt:2 * nt], refs[2 * nt:]
        _, _, c, sibling, chips = _place()
        cps = [_remote(j * nt + t, _half(ins[t], 2 * px + py, c), _half(outs[t], 2 * px + py, c), sibling,
                       send_sems, recv_sems)
               for j, (px, py) in enumerate(chips) for t in range(nt)]
        for cp in cps:
            cp.start()
        for j, (px, py) in enumerate(chips):
            for t in range(nt):
                other = _half(outs[t], 2 * px + py, 1 - c)
                _remote(j * nt + t, other, other, sibling, send_sems, recv_sems).wait_recv()
        for cp in cps:
            cp.wait_send()

    return pl.pallas_call(
        body, name=name, in_specs=[ANY] * nt, out_specs=[ANY] * nt,
        input_output_aliases={i: i for i in range(nt)},
        out_shape=[jax.ShapeDtypeStruct(a.shape, a.dtype) for a in slots],
        scratch_shapes=[pltpu.SemaphoreType.DMA((3 * nt,)), pltpu.SemaphoreType.DMA((3 * nt,))],
    )(*slots)


def _all_reduce_small(v):
    r, lanes = v.shape

    def body(v_ref, out_ref, buf, send_sems, recv_sems):
        x, y, c, sibling, chips = _place()

        def slot(px, py, pc):
            return buf.at[4 * px + 2 * py + pc]

        def copy(k, block, to, src=None):
            return pltpu.make_async_remote_copy(src_ref=slot(*block) if src is None else src, dst_ref=slot(*block),
                                                send_sem=send_sems.at[k], recv_sem=recv_sems.at[k],
                                                device_id=to, device_id_type=MESH)

        me = (x, y, c)
        buf[4 * x + 2 * y + c] = v_ref[...]
        first = [copy(0, me, sibling, src=v_ref)]
        first += [copy(1 + j, me, (*chip, c), src=v_ref) for j, chip in enumerate(chips)]
        for cp in first:
            cp.start()
        passed = [copy(4 + j, (*chip, c), sibling) for j, chip in enumerate(chips)]
        for j, chip in enumerate(chips):
            copy(1 + j, (*chip, c), me).wait_recv()
            passed[j].start()
        copy(0, (x, y, 1 - c), me).wait_recv()
        for j, chip in enumerate(chips):
            copy(4 + j, (*chip, 1 - c), me).wait_recv()
        for cp in first + passed:
            cp.wait_send()
        acc = buf[0]
        for k in range(1, 8):
            acc = acc + buf[k]
        out_ref[...] = acc

    vmem = pl.BlockSpec(memory_space=pltpu.VMEM)
    return pl.pallas_call(
        body, name="all_reduce_small", in_specs=[vmem], out_specs=vmem,
        out_shape=jax.ShapeDtypeStruct((r, lanes), F32),
        scratch_shapes=[pltpu.VMEM((8, r, lanes), F32), pltpu.SemaphoreType.DMA((7,)), pltpu.SemaphoreType.DMA((7,))],
    )(v)


def _size(shape):
    n = 1
    for d in shape:
        n *= d
    return n


def _pack_small(parts):
    rows = []
    for p in parts:
        f = p.reshape(-1).astype(F32)
        n = -(-f.shape[0] // HEAD_DIM) * HEAD_DIM
        rows.append(jnp.pad(f, (0, n - f.shape[0])).reshape(-1, HEAD_DIM))
    a = jnp.concatenate(rows, axis=0)
    return jnp.pad(a, ((0, -a.shape[0] % 8), (0, 0)))


def _unpack_small(a, shapes):
    out, row = [], 0
    for shp in shapes:
        nrows = -(-_size(shp) // HEAD_DIM)
        out.append(a[row:row + nrows].reshape(-1)[:_size(shp)].reshape(shp))
        row += nrows
    return out


SMALL = ["attn_norm_w", "a_log", "dt_bias", "delta_out_norm_w", "q_norm_w", "k_norm_w", "attn_out_norm_w", "ffn_norm_w"]
BIG = ["w_in", "w_out", "w_gate_up", "w_down"]
ORDER = ["attn_norm_w", "w_in", "conv_w", "a_log", "dt_bias", "delta_out_norm_w", "q_norm_w", "k_norm_w",
         "attn_out_norm_w", "w_out", "ffn_norm_w", "w_gate_up", "w_down"]


def kernel(x, positions, attn_norm_w, w_in, conv_w, a_log, dt_bias, delta_out_norm_w, q_norm_w, k_norm_w, attn_out_norm_w, w_out, ffn_norm_w, w_gate_up, w_down, loss_target, m_attn_norm_w, m_w_in, m_conv_w, m_a_log, m_dt_bias, m_delta_out_norm_w, m_q_norm_w, m_k_norm_w, m_attn_out_norm_w, m_w_out, m_ffn_norm_w, m_w_gate_up, m_w_down, v_attn_norm_w, v_w_in, v_conv_w, v_a_log, v_dt_bias, v_delta_out_norm_w, v_q_norm_w, v_k_norm_w, v_attn_out_norm_w, v_w_out, v_ffn_norm_w, v_w_gate_up, v_w_down):
    wts = dict(attn_norm_w=attn_norm_w, w_in=w_in, conv_w=conv_w, a_log=a_log, dt_bias=dt_bias,
               delta_out_norm_w=delta_out_norm_w, q_norm_w=q_norm_w, k_norm_w=k_norm_w,
               attn_out_norm_w=attn_out_norm_w, w_out=w_out, ffn_norm_w=ffn_norm_w, w_gate_up=w_gate_up, w_down=w_down)
    mom = dict(attn_norm_w=m_attn_norm_w, w_in=m_w_in, conv_w=m_conv_w, a_log=m_a_log, dt_bias=m_dt_bias,
               delta_out_norm_w=m_delta_out_norm_w, q_norm_w=m_q_norm_w, k_norm_w=m_k_norm_w,
               attn_out_norm_w=m_attn_out_norm_w, w_out=m_w_out, ffn_norm_w=m_ffn_norm_w, w_gate_up=m_w_gate_up,
               w_down=m_w_down)
    var = dict(attn_norm_w=v_attn_norm_w, w_in=v_w_in, conv_w=v_conv_w, a_log=v_a_log, dt_bias=v_dt_bias,
               delta_out_norm_w=v_delta_out_norm_w, q_norm_w=v_q_norm_w, k_norm_w=v_k_norm_w,
               attn_out_norm_w=v_attn_out_norm_w, w_out=v_w_out, ffn_norm_w=v_ffn_norm_w, w_gate_up=v_w_gate_up,
               w_down=v_w_down)
    dmod = x.shape[2]
    heads = dmod // (2 * HEAD_DIM)
    dw = heads * HEAD_DIM
    chip = 2 * lax.axis_index("x") + lax.axis_index("y")
    core = lax.axis_index("c")
    n_in, n_out, n_gu, n_down, n_conv = (w_in.shape[2], w_out.shape[1], w_gate_up.shape[2], w_down.shape[1],
                                         conv_w.shape[2])

    def slots_of(w, dtype):
        shard = w[0].astype(dtype)
        return lax.dynamic_update_index_in_dim(lax.empty((4,) + shard.shape, dtype), shard, chip, axis=0)

    s_in, s_conv = _all_gather_weights([slots_of(w_in, BF16)], [slots_of(conv_w, F32)], "all_gather_w_in")
    later = [slots_of(w_out, BF16), slots_of(w_gate_up, BF16), slots_of(w_down, BF16)]
    w_send, w_recv, later, started = _split_start(later, s_conv, _gather_plan(3), 9, "gather_rest_start")
    by_cols = lambda a: a.transpose(1, 0, 2).reshape(a.shape[1], 4 * a.shape[2])
    w_in_f, conv_f = by_cols(s_in), by_cols(s_conv)
    w_bd = jnp.pad(w_in_f[:, 4 * dw:4 * dw + 2 * heads], ((0, 0), (0, HEAD_DIM - 2 * heads)))
    small = {n: wts[n] for n in SMALL}
    place = jnp.stack([core, chip]).astype(jnp.int32)
    to_slots = lambda a: a.reshape(a.shape[0], 4, a.shape[1] // 4).transpose(1, 0, 2)

    def later_weights(after):
        landed = _split_wait(w_send, w_recv, later, after, _gather_landed_plan(3), "gather_rest_wait")
        s_out, = _pass_halves_to_sibling(landed[:1], "gather_out_pass")
        p_send, p_recv, passing, token = _split_start(landed[1:], s_out, _pass_plan(2), 6, "gather_ffn_pass_start")

        def ffn_weights(after):
            s_gu, s_down = _split_wait(p_send, p_recv, passing, after, _pass_landed_plan(2), "gather_ffn_pass_wait")
            return by_cols(s_gu), s_down.reshape(4 * n_down, dmod)

        return s_out.reshape(4 * n_out, dmod), token, ffn_weights

    ffn = {}

    def ffn_grads_ready(g_gate, g_up, g_down):
        gs = [to_slots(jnp.concatenate([g_gate, g_up], axis=1)), g_down.reshape(4, n_down, dmod)]
        zones = [lax.empty((4, g.shape[1] // 2, g.shape[2]), BF16) for g in gs]
        s_send, s_recv, s_bufs, swapping = _split_start(gs + zones, g_gate, _swap_plan(2), 2, "swap_ffn_start")

        def and_then(after):
            g_gu, g_dn, b_gu, b_dn = _split_wait(s_send, s_recv, s_bufs, after, _swap_plan(2), "swap_ffn_wait")
            sums = [_add_half_bf16(g_gu, b_gu, place, "chip_partial_sum_w_gate_up"),
                    _add_half_bf16(g_dn, b_dn, place, "chip_partial_sum_w_down")]
            zones3 = [lax.empty((3,) + p.shape[1:], BF16) for p in sums]
            ffn["send"], ffn["recv"], ffn["bufs"], token = _split_start(sums + zones3, b_gu, _scatter_plan(2), 6,
                                                                        "scatter_ffn_start")
            return token

        return swapping, and_then

    rest = {}

    def rest_grads_ready(bg):
        gs = [to_slots(jnp.concatenate([bg["w_qkv"], bg["w_z"], bg["w_bd"][:, :2 * heads], bg["w_attn"]], axis=1)),
              jnp.concatenate([bg["w_out_a"], bg["w_out_b"]], axis=0).reshape(4, n_out, dmod)]
        zones = [lax.empty((4, g.shape[1] // 2, g.shape[2]), BF16) for g in gs]
        s_send, s_recv, s_bufs, swapping = _split_start(gs + zones, bg["w_attn"], _swap_plan(2), 2, "swap_rest_start")

        def and_then(after):
            g_in, g_out, b_in, b_out = _split_wait(s_send, s_recv, s_bufs, after, _swap_plan(2), "swap_rest_wait")
            sums = [_add_half_bf16(g_in, b_in, place, "chip_partial_sum_w_in"),
                    _add_half_bf16(g_out, b_out, place, "chip_partial_sum_w_out")]
            zones3 = [lax.empty((3,) + p.shape[1:], BF16) for p in sums]
            rest["send"], rest["recv"], rest["bufs"], token = _split_start(sums + zones3, b_in, _scatter_plan(2), 6,
                                                                           "scatter_rest_start")
            return token

        return swapping, and_then

    loss_row, grad_x, sg = _local_step(
        x[0], positions[0], loss_target[0], small, w_in_f[:, :4 * dw], w_bd, w_in_f[:, 4 * dw + 2 * heads:], conv_f,
        later_weights, ffn_grads_ready, rest_grads_ready, after=started)

    r_send, r_recv, r_bufs = rest["send"], rest["recv"], rest["bufs"]
    sum_gu, sum_down, got_gu, got_down = _split_wait(ffn["send"], ffn["recv"], ffn["bufs"], grad_x,
                                                     _scatter_plan(2), "scatter_ffn_wait")
    g_big = dict(zip(["w_gate_up", "w_down"], _join_halves(
        [_sum4_f32(sum_gu, got_gu, place, "grad_total_w_gate_up"),
         _sum4_f32(sum_down, got_down, place, "grad_total_w_down")], "join_ffn_halves")))
    grads, deltas, new_m, new_v = {}, {}, {}, {}

    def adamw_big(n):
        shp = wts[n].shape
        d, nm, nv = _adamw(wts[n][0], g_big[n], mom[n][0], var[n][0], "adamw_" + n)
        grads[n], deltas[n], new_m[n], new_v[n] = g_big[n].reshape(shp), d.reshape(shp), nm.reshape(shp), nv.reshape(shp)
        return d

    done = [adamw_big("w_gate_up"), adamw_big("w_down")]
    sum_in, sum_out, got_in, got_out = _split_wait(r_send, r_recv, r_bufs, done, _scatter_plan(2), "scatter_rest_wait")
    g_big.update(zip(["w_in", "w_out"], _join_halves(
        [_sum4_f32(sum_in, got_in, place, "grad_total_w_in"),
         _sum4_f32(sum_out, got_out, place, "grad_total_w_out")], "join_rest_halves")))
    adamw_big("w_in")
    adamw_big("w_out")

    reduced = _all_reduce_small(_pack_small([sg[n] for n in SMALL] + [sg["conv_w"], loss_row]))
    red = _unpack_small(reduced, [wts[n].shape for n in SMALL] + [(4, 4 * n_conv), (1, HEAD_DIM)])
    g_small = dict(zip(SMALL, red[:len(SMALL)]))
    g_conv_full, loss_out = red[len(SMALL)], red[len(SMALL) + 1]
    g_small["conv_w"] = lax.dynamic_slice_in_dim(g_conv_full, chip * n_conv, n_conv, axis=1).reshape(conv_w.shape)

    names = SMALL + ["conv_w"]
    shapes = [wts[n].shape for n in names]
    d, nm, nv = _adamw(_pack_small([wts[n] for n in names]), _pack_small([g_small[n] for n in names]),
                       _pack_small([mom[n] for n in names]), _pack_small([var[n] for n in names]), "adamw_small")
    for n, dd, mm, vv in zip(names, _unpack_small(d, shapes), _unpack_small(nm, shapes), _unpack_small(nv, shapes)):
        grads[n], deltas[n], new_m[n], new_v[n] = g_small[n], dd, mm, vv
    return (loss_out[0, 0], grad_x[None], *[grads[n] for n in ORDER], *[deltas[n] for n in ORDER],
            *[new_m[n] for n in ORDER], *[new_v[n] for n in ORDER])
```

```python
import functools

import jax
import jax.numpy as jnp
from jax import lax
from jax.experimental import pallas as pl
from jax.experimental.pallas import tpu as pltpu

F32 = jnp.float32
BF16 = jnp.bfloat16
HEAD_DIM = 128
CHUNK = 128
INV_BLOCK = 64
SPAN = 128
DILATIONS = (1, 4, 16)
ROPE_THETA = 10000.0
EPS = 1e-6
NEG = -1e30
ADAM_LR, ADAM_B1, ADAM_B2, ADAM_EPS, ADAM_WD, ADAM_STEP = 0.001, 0.9, 0.999, 1e-08, 0.01, 10
VMEM_LIMIT = 48 * 1024 * 1024
MESH = pl.DeviceIdType.MESH

_DN = {"nn": (((1,), (0,)), ((), ())), "nt": (((1,), (1,)), ((), ())), "tn": (((0,), (0,)), ((), ()))}


def _dot(a, b, mode="nn"):
    (ca, cb), _ = _DN[mode]
    if a.ndim == 3:
        dn = (((ca[0] + 1,), (cb[0] + 1,)), ((0,), (0,)))
    else:
        dn = _DN[mode]
    return lax.dot_general(a.astype(BF16), b.astype(BF16), dn, preferred_element_type=F32)


def _rsum(x):
    return jnp.sum(x, axis=-1, keepdims=True)


def _csum(x):
    return jnp.sum(x, axis=-2, keepdims=True)


def _tile(dim, pref, unit=128):
    t = (min(pref, dim) // unit) * unit
    while t >= unit:
        if dim % t == 0:
            return t
        t -= unit
    return dim


def _params(sem):
    return pltpu.CompilerParams(dimension_semantics=sem, vmem_limit_bytes=VMEM_LIMIT)


def _sigmoid(x):
    return 1.0 / (1.0 + jnp.exp(-x))


def _matmul(a, b, mode, name, add=None, out_dtype=F32, a_cols=None, b_cols=None, b_rows=None,
            tiles=(), finish=None, out_dtypes=(), row_sum=False, after=None, b2_cols=None):
    if mode == "tn":
        out_dtype = BF16
    a_off, a_w = a_cols if a_cols else (0, a.shape[1])
    b_off, b_w = b_cols if b_cols else (0, b.shape[1])
    br_off, br_n = b_rows if b_rows else (0, b.shape[0])
    if mode == "nn":
        m, k, n = a.shape[0], a_w, b_w
        assert br_n == k
    elif mode == "nt":
        m, k, n = a.shape[0], a_w, b.shape[0]
        assert b_w == k
    else:
        k, m, n = a.shape[0], a_w, b_w
        assert b.shape[0] == k
    tm, tn = _tile(m, 1024, 128), _tile(n, 1024, 128)
    tk = _tile(k, (4096 if k <= 4096 else 2048) if a.dtype == BF16 and b.dtype == BF16 else 1024, 128)
    if mode == "nn":
        assert a_off % tk == 0 and b_off % tn == 0 and br_off % tk == 0
        a_spec = pl.BlockSpec((tm, tk), lambda i, j, kk: (i, kk + a_off // tk))
        b_spec = pl.BlockSpec((tk, tn), lambda i, j, kk: (kk + br_off // tk, j + b_off // tn))
    elif mode == "nt":
        assert a_off % tk == 0 and b_off % tk == 0
        a_spec = pl.BlockSpec((tm, tk), lambda i, j, kk: (i, kk + a_off // tk))
        b_spec = pl.BlockSpec((tn, tk), lambda i, j, kk: (j, kk + b_off // tk))
    else:
        assert a_off % tm == 0 and b_off % tn == 0
        a_spec = pl.BlockSpec((tk, tm), lambda i, j, kk: (kk, i + a_off // tm))
        b_spec = pl.BlockSpec((tk, tn), lambda i, j, kk: (kk, j + b_off // tn))
    nk = k // tk
    if finish is None:
        out_dtypes = [out_dtype]
        if add is None:
            finish = lambda acc, vals: (acc,)
        else:
            tiles = [(add, 0)]
            finish = lambda acc, vals: (acc + vals[0].astype(F32),)
    n_tiles, n_out = len(tiles), len(out_dtypes)
    extra = [] if after is None else [after]
    n_b = 1 if b2_cols is None else 2
    first_out = 1 + n_b + n_tiles + len(extra)

    def body(*refs):
        a_ref, b_refs = refs[0], refs[1:1 + n_b]
        tile_refs, out_refs = refs[1 + n_b:1 + n_b + n_tiles], refs[first_out:first_out + n_out]
        acc_refs = refs[-n_b:] if nk > 1 else ()
        kk = pl.program_id(2)
        first_tile = (pl.program_id(0) == 0) & (pl.program_id(1) == 0)

        av = a_ref[...]
        if nk > 1:
            @pl.when(kk == 0)
            def _():
                for acc_ref in acc_refs:
                    acc_ref[...] = jnp.zeros_like(acc_ref)

            for acc_ref, b_ref in zip(acc_refs, b_refs):
                acc_ref[...] += _dot(av, b_ref[...], mode)

        @pl.when(kk == nk - 1)
        def _():
            if nk > 1:
                accs = [acc_ref[...] for acc_ref in acc_refs]
            else:
                accs = [_dot(av, b_ref[...], mode) for b_ref in b_refs]
            res = finish(accs[0] if n_b == 1 else accs, [t[...] for t in tile_refs])
            for o_ref, r in zip(out_refs, res):
                o_ref[...] = r.astype(o_ref.dtype)
            if row_sum:
                row_ref = refs[first_out + n_out]

                @pl.when(first_tile)
                def _():
                    row_ref[...] = res[n_out]

                @pl.when(jnp.logical_not(first_tile))
                def _():
                    row_ref[...] += res[n_out]

    in_specs = [a_spec, b_spec]
    args = [a, b]
    if b2_cols is not None:
        assert mode == "nn" and b2_cols[1] == n and b2_cols[0] % tn == 0
        in_specs.append(pl.BlockSpec((tk, tn), lambda i, j, kk: (kk + br_off // tk, j + b2_cols[0] // tn)))
        args.append(b)
    for arr, off in tiles:
        assert off % tn == 0
        in_specs.append(pl.BlockSpec((tm, tn), lambda i, j, kk, off=off: (i, j + off // tn)))
        args.append(arr)
    in_specs += [ANY] * len(extra)
    args += extra
    out_specs = [pl.BlockSpec((tm, tn), lambda i, j, kk: (i, j))] * n_out
    out_shape = [jax.ShapeDtypeStruct((m, n), dt) for dt in out_dtypes]
    if row_sum:
        out_specs.append(pl.BlockSpec((1, HEAD_DIM), lambda i, j, kk: (0, 0)))
        out_shape.append(jax.ShapeDtypeStruct((1, HEAD_DIM), F32))
    res = pl.pallas_call(
        body, name=name, grid=(m // tm, n // tn, nk),
        in_specs=in_specs, out_specs=out_specs, out_shape=out_shape,
        scratch_shapes=[pltpu.VMEM((tm, tn), F32)] * (n_b if nk > 1 else 0),
        compiler_params=_params(("arbitrary",) * 3 if row_sum else ("parallel", "parallel", "arbitrary")),
    )(*args)
    return res[0] if len(res) == 1 else res


def _rmsnorm_fwd(x, w, name, after=None):
    s, d = x.shape
    tr = _tile(s, 512, 8)

    def body(x_ref, w_ref, *rest):
        h_ref = rest[-1]
        xv = x_ref[...]
        rstd = lax.rsqrt(jnp.mean(xv * xv, axis=-1, keepdims=True) + EPS)
        h_ref[...] = (xv * rstd * w_ref[...]).astype(BF16)

    extra = [] if after is None else [after]
    return pl.pallas_call(
        body, name=name, grid=(s // tr,),
        in_specs=[pl.BlockSpec((tr, d), lambda i: (i, 0)), pl.BlockSpec((1, d), lambda i: (0, 0))] + [ANY] * len(extra),
        out_specs=pl.BlockSpec((tr, d), lambda i: (i, 0)),
        out_shape=jax.ShapeDtypeStruct((s, d), BF16),
        compiler_params=_params(("parallel",)),
    )(x, w, *extra)


def _rmsnorm_bwd(dh, x, w, res, name, after=None):
    s, d = x.shape
    tr = _tile(s, 256, 8)

    def body(dh_ref, x_ref, w_ref, res_ref, *rest):
        dx_ref, dx16_ref, dw_ref = rest[-3:]
        xv = x_ref[...]
        rstd = lax.rsqrt(jnp.mean(xv * xv, axis=-1, keepdims=True) + EPS)
        xhat = xv * rstd
        dhv = dh_ref[...]
        gw = dhv * w_ref[...]
        dx = res_ref[...] + rstd * (gw - xhat * jnp.mean(gw * xhat, axis=-1, keepdims=True))
        dx_ref[...] = dx
        dx16_ref[...] = dx.astype(BF16)

        @pl.when(pl.program_id(0) == 0)
        def _():
            dw_ref[...] = jnp.zeros_like(dw_ref)

        dw_ref[...] += jnp.sum(dhv * xhat, axis=0, keepdims=True)

    row = pl.BlockSpec((tr, d), lambda i: (i, 0))
    vec = pl.BlockSpec((1, d), lambda i: (0, 0))
    extra = [] if after is None else [after]
    return pl.pallas_call(
        body, name=name, grid=(s // tr,),
        in_specs=[row, row, vec, row] + [ANY] * len(extra), out_specs=[row, row, vec],
        out_shape=[jax.ShapeDtypeStruct((s, d), F32), jax.ShapeDtypeStruct((s, d), BF16),
                   jax.ShapeDtypeStruct((1, d), F32)],
        compiler_params=_params(("arbitrary",)),
    )(dh, x, w, res, *extra)


def _conv_taps(x, w, rows):
    shifted = [x]
    for sft in (1, 2, 3):
        shifted.append(jnp.where(rows >= sft, pltpu.roll(x, sft, 0), 0.0))
    y = w[3:4, :] * shifted[0] + w[2:3, :] * shifted[1] + w[1:2, :] * shifted[2] + w[0:1, :] * shifted[3]
    return y, shifted


def _delta_pre_fwd(qkvz, conv_w, heads):
    s = qkvz.shape[0]
    nblk = 3 * heads

    def body(x_ref, w_ref, o_ref):
        part = pl.program_id(0) // heads
        rows = lax.broadcasted_iota(jnp.int32, (s, HEAD_DIM), 0)
        y, _ = _conv_taps(x_ref[...], w_ref[...], rows)
        a = y * _sigmoid(y)
        rs = lax.rsqrt(jnp.sum(a * a, axis=-1, keepdims=True) + EPS)
        fac = jnp.where(part == 0, rs * (HEAD_DIM ** -0.5), jnp.where(part == 1, rs, 1.0))
        o_ref[...] = a * fac

    return pl.pallas_call(
        body, name="delta_pre_fwd", grid=(nblk,),
        in_specs=[pl.BlockSpec((s, HEAD_DIM), lambda i: (0, i)), pl.BlockSpec((4, HEAD_DIM), lambda i: (0, i))],
        out_specs=pl.BlockSpec((s, HEAD_DIM), lambda i: (0, i)),
        out_shape=jax.ShapeDtypeStruct((s, 3 * heads * HEAD_DIM), F32),
        compiler_params=_params(("parallel",)),
    )(qkvz, conv_w)


def _delta_pre_bwd(dqkv, qkvz, conv_w, heads):
    s = qkvz.shape[0]
    nblk = 3 * heads

    def body(d_ref, x_ref, w_ref, dx_ref, dw_ref):
        part = pl.program_id(0) // heads
        rows = lax.broadcasted_iota(jnp.int32, (s, HEAD_DIM), 0)
        w = w_ref[...]
        y, shifted = _conv_taps(x_ref[...], w, rows)
        sg = _sigmoid(y)
        a = y * sg
        rs = lax.rsqrt(jnp.sum(a * a, axis=-1, keepdims=True) + EPS)
        unit = a * rs
        dn = d_ref[...]
        scale = jnp.where(part == 0, HEAD_DIM ** -0.5, 1.0)
        da_norm = scale * rs * (dn - unit * jnp.sum(dn * unit, axis=-1, keepdims=True))
        da = jnp.where(part < 2, da_norm, dn)
        dy = da * sg * (1.0 + y * (1.0 - sg))
        dx = w[3:4, :] * dy
        for sft in (1, 2, 3):
            dx = dx + w[3 - sft:4 - sft, :] * jnp.where(rows < s - sft, pltpu.roll(dy, s - sft, 0), 0.0)
        dx_ref[...] = dx.astype(BF16)
        for sft in range(4):
            dw_ref[3 - sft:4 - sft, :] = jnp.sum(dy * shifted[sft], axis=0, keepdims=True)

    col = pl.BlockSpec((s, HEAD_DIM), lambda i: (0, i))
    wsp = pl.BlockSpec((4, HEAD_DIM), lambda i: (0, i))
    return pl.pallas_call(
        body, name="delta_pre_bwd", grid=(nblk,),
        in_specs=[col, col, wsp], out_specs=[col, wsp],
        out_shape=[jax.ShapeDtypeStruct((s, 3 * heads * HEAD_DIM), BF16),
                   jax.ShapeDtypeStruct((4, 3 * heads * HEAD_DIM), F32)],
        compiler_params=_params(("parallel",)),
    )(dqkv, qkvz, conv_w)


def _heads_of(ref, heads):
    return jnp.stack([ref[:, h * HEAD_DIM:(h + 1) * HEAD_DIM] for h in range(heads)])


def _chunk_common(q, k, v, bd, a_log, dt_bias, heads, solved=None):
    c = CHUNK
    braw = jnp.stack([bd[:, h:h + 1] for h in range(heads)])
    draw = jnp.stack([bd[:, heads + h:heads + h + 1] for h in range(heads)])
    beta = _sigmoid(braw)
    xd = draw + dt_bias
    sp = jnp.maximum(xd, 0.0) + jnp.log1p(jnp.exp(-jnp.abs(xd)))
    g = -jnp.exp(a_log) * sp
    row = lax.broadcasted_iota(jnp.int32, (c, c), 0)
    col = lax.broadcasted_iota(jnp.int32, (c, c), 1)
    sq = (heads, c, c)
    g_b = jnp.broadcast_to(g, sq)
    g_row = _csum(jnp.where(row == col, g_b, 0.0))
    gam_col = _rsum(jnp.where(col <= row, jnp.broadcast_to(g_row, sq), 0.0))
    gam_row = _csum(jnp.where(row <= col, g_b, 0.0))
    causal = row >= col
    dm = jnp.where(causal, jnp.exp(jnp.where(causal, gam_col - gam_row, 0.0)), 0.0)
    kk = _dot(k, k, "nt")
    e = jnp.exp(gam_col)
    if solved is None:
        low = jnp.where(row > col, beta * kk * dm, 0.0)
        assert c in (INV_BLOCK, 2 * INV_BLOCK)
        same = (row // INV_BLOCK) == (col // INV_BLOCK)
        diag = jnp.where(same, low, 0.0)
        t = jnp.where(row == col, 1.0, 0.0) - diag
        pw = diag
        for _ in range((INV_BLOCK - 1).bit_length() - 1):
            pw = _dot(pw, pw)
            t = t + _dot(t, pw)
        if c > INV_BLOCK:
            t = t - _dot(_dot(t, low - diag), t)
        u = _dot(t, beta * v)
        w = _dot(t, (beta * e) * k)
    else:
        t, u, w = solved
    qk_raw = _dot(q, k, "nt")
    gl = _csum(g)
    el = jnp.exp(gl - gam_col)
    return dict(beta=beta, xd=xd, g=g, row=row, col=col, dm=dm, kk=kk, t=t, e=e, u=u, w=w,
                qk_raw=qk_raw, qk=qk_raw * dm, gl=gl, el=el, qd=e * q, kd=el * k, cd=jnp.exp(gl))


def _delta_chunk_fwd(qkv, bd, a_log, dt_bias, heads):
    s = qkv.shape[0]
    n = s // CHUNK
    dw = heads * HEAD_DIM
    blk = lambda part: pl.BlockSpec((CHUNK, dw), lambda i: (i, part))

    def body(q_ref, k_ref, v_ref, bd_ref, al_ref, dt_ref, o_ref, st_ref, t_ref, uw_ref, state):
        @pl.when(pl.program_id(0) == 0)
        def _():
            state[...] = jnp.zeros_like(state)

        cm = _chunk_common(_heads_of(q_ref, heads), _heads_of(k_ref, heads), _heads_of(v_ref, heads), bd_ref[...],
                           al_ref[...], dt_ref[...], heads)
        st = state[...]
        st_ref[0] = st
        t_ref[0] = cm["t"]
        uw_ref[0, 0] = cm["u"]
        uw_ref[0, 1] = cm["w"]
        vn = cm["u"] - _dot(cm["w"], st)
        o = _dot(cm["qd"], st) + _dot(cm["qk"], vn)
        for h in range(heads):
            o_ref[:, h * HEAD_DIM:(h + 1) * HEAD_DIM] = o[h]
        state[...] = cm["cd"] * st + _dot(cm["kd"], vn, "tn")

    smem = pl.BlockSpec((heads, 1, 1), lambda i: (0, 0, 0))
    return pl.pallas_call(
        body, name="delta_chunk_fwd", grid=(n,),
        in_specs=[blk(0), blk(1), blk(2), pl.BlockSpec((CHUNK, HEAD_DIM), lambda i: (i, 0)), smem, smem],
        out_specs=[pl.BlockSpec((CHUNK, dw), lambda i: (i, 0)),
                   pl.BlockSpec((1, heads, HEAD_DIM, HEAD_DIM), lambda i: (i, 0, 0, 0)),
                   pl.BlockSpec((1, heads, CHUNK, CHUNK), lambda i: (i, 0, 0, 0)),
                   pl.BlockSpec((1, 2, heads, CHUNK, HEAD_DIM), lambda i: (i, 0, 0, 0, 0))],
        out_shape=[jax.ShapeDtypeStruct((s, dw), F32),
                   jax.ShapeDtypeStruct((n, heads, HEAD_DIM, HEAD_DIM), F32),
                   jax.ShapeDtypeStruct((n, heads, CHUNK, CHUNK), F32),
                   jax.ShapeDtypeStruct((n, 2, heads, CHUNK, HEAD_DIM), F32)],
        scratch_shapes=[pltpu.VMEM((heads, HEAD_DIM, HEAD_DIM), F32)],
        compiler_params=_params(("arbitrary",)),
    )(qkv, qkv, qkv, bd, a_log, dt_bias)


def _delta_chunk_bwd(do, qkv, bd, saved, a_log, dt_bias, heads):
    s = qkv.shape[0]
    n = s // CHUNK
    dw = heads * HEAD_DIM
    c = CHUNK
    blk = lambda part: pl.BlockSpec((CHUNK, dw), lambda i: (n - 1 - i, part))

    def all_heads(q, k, v, dov, st, solved, dsn, bd, a_log, dt_bias):
        cm = _chunk_common(q, k, v, bd, a_log, dt_bias, heads, solved)
        beta, e, dm, row, col = cm["beta"], cm["e"], cm["dm"], cm["row"], cm["col"]
        sq = (heads, c, c)
        vn = cm["u"] - _dot(cm["w"], st)
        dvn = _dot(cm["kd"], dsn)
        dkd = _dot(vn, dsn, "nt")
        dcd = _csum(_rsum(st * dsn))
        ds = cm["cd"] * dsn
        dqd = _dot(dov, st, "nt")
        ds = ds + _dot(cm["qd"], dov, "tn")
        dqk = _dot(dov, vn, "nt")
        dvn = dvn + _dot(cm["qk"], dov, "tn")
        dw_ = -_dot(dvn, st, "nt")
        ds = ds - _dot(cm["w"], dvn, "tn")
        drhs_u = _dot(cm["t"], dvn, "tn")
        drhs_w = _dot(cm["t"], dw_, "tn")
        da = -(_dot(drhs_u, cm["u"], "nt") + _dot(drhs_w, cm["w"], "nt"))
        dl = jnp.where(row > col, da, 0.0)
        dbeta = _rsum(dl * cm["kk"] * dm)
        dkk = dl * beta * dm
        dd = dl * beta * cm["kk"]
        dv = beta * drhs_u
        ek = e * k
        dbeta = dbeta + _rsum(drhs_u * v) + _rsum(drhs_w * ek)
        dk = (beta * e) * drhs_w
        dgam = _rsum(drhs_w * (beta * ek))
        dqkm = dqk * dm
        dq = _dot(dqkm, k)
        dk = dk + _dot(dqkm, q, "tn")
        dd = dd + dqk * cm["qk_raw"]
        dk = dk + _dot(dkk, k) + _dot(dkk, k, "tn")
        dq = dq + e * dqd
        dgam = dgam + _rsum(dqd * cm["qd"])
        dk = dk + cm["el"] * dkd
        r = _rsum(dkd * cm["kd"])
        dgam = dgam - r
        dgl = _csum(r) + dcd * cm["cd"]
        mm = dd * dm
        colsum_c = _rsum(jnp.where(row == col, jnp.broadcast_to(_csum(mm), sq), 0.0))
        dgam = dgam + _rsum(mm) - colsum_c
        ridx = lax.broadcasted_iota(jnp.int32, (c, 1), 0)
        dgam = dgam + jnp.where(ridx == c - 1, dgl, 0.0)
        dgam_row = _csum(jnp.where(row == col, jnp.broadcast_to(dgam, sq), 0.0))
        dg = _rsum(jnp.where(col >= row, jnp.broadcast_to(dgam_row, sq), 0.0))
        d_xd = dg * (-jnp.exp(a_log)) * _sigmoid(cm["xd"])
        d_braw = dbeta * beta * (1.0 - beta)
        d_alog = dg * cm["g"]
        lane = lax.broadcasted_iota(jnp.int32, (c, HEAD_DIM), 1)
        dbd = jnp.zeros((c, HEAD_DIM), F32)
        for h in range(heads):
            dbd = (dbd + jnp.where(lane == h, d_braw[h], 0.0) + jnp.where(lane == h + heads, d_xd[h], 0.0)
                   + jnp.where(lane == h + 2 * heads, d_alog[h], 0.0))
        return dq, dk, dv, ds, dbd

    def body(do_ref, q_ref, k_ref, v_ref, bd_ref, st_ref, t_ref, uw_ref, al_ref, dt_ref, dqkv_ref, dbd_ref, dstate):
        @pl.when(pl.program_id(0) == 0)
        def _():
            dstate[...] = jnp.zeros_like(dstate)

        dq, dk, dv, ds, dbd = all_heads(_heads_of(q_ref, heads), _heads_of(k_ref, heads), _heads_of(v_ref, heads),
                                        _heads_of(do_ref, heads), st_ref[0], (t_ref[0], uw_ref[0, 0], uw_ref[0, 1]),
                                        dstate[...], bd_ref[...],
                                        al_ref[...], dt_ref[...])
        for part, val in enumerate((dq, dk, dv)):
            for h in range(heads):
                lo = part * dw + h * HEAD_DIM
                dqkv_ref[:, lo:lo + HEAD_DIM] = val[h]
        dstate[...] = ds
        dbd_ref[...] = dbd

    smem = pl.BlockSpec((heads, 1, 1), lambda i: (0, 0, 0))
    shared = pl.BlockSpec((CHUNK, HEAD_DIM), lambda i: (n - 1 - i, 0))
    wide = pl.BlockSpec((CHUNK, dw), lambda i: (n - 1 - i, 0))
    return pl.pallas_call(
        body, name="delta_chunk_bwd", grid=(n,),
        in_specs=[wide, blk(0), blk(1), blk(2), shared,
                  pl.BlockSpec((1, heads, HEAD_DIM, HEAD_DIM), lambda i: (n - 1 - i, 0, 0, 0)),
                  pl.BlockSpec((1, heads, CHUNK, CHUNK), lambda i: (n - 1 - i, 0, 0, 0)),
                  pl.BlockSpec((1, 2, heads, CHUNK, HEAD_DIM), lambda i: (n - 1 - i, 0, 0, 0, 0)), smem, smem],
        out_specs=[pl.BlockSpec((CHUNK, 3 * dw), lambda i: (n - 1 - i, 0)), shared],
        out_shape=[jax.ShapeDtypeStruct((s, 3 * dw), F32), jax.ShapeDtypeStruct((s, HEAD_DIM), F32)],
        scratch_shapes=[pltpu.VMEM((heads, HEAD_DIM, HEAD_DIM), F32)],
        compiler_params=_params(("arbitrary",)),
    )(do, qkv, qkv, qkv, bd, *saved, a_log, dt_bias)


def _delta_post_fwd(o, qkvz, w, heads):
    s = o.shape[0]
    tr = _tile(s, 1024, 8)

    def body(o_ref, z_ref, w_ref, out_ref):
        ov, z = o_ref[...], z_ref[...]
        rstd = lax.rsqrt(jnp.mean(ov * ov, axis=-1, keepdims=True) + EPS)
        out_ref[...] = (ov * rstd * w_ref[...] * (z * _sigmoid(z))).astype(BF16)

    return pl.pallas_call(
        body, name="delta_post_fwd", grid=(s // tr, heads),
        in_specs=[pl.BlockSpec((tr, HEAD_DIM), lambda i, h: (i, h)),
                  pl.BlockSpec((tr, HEAD_DIM), lambda i, h: (i, 3 * heads + h)),
                  pl.BlockSpec((1, HEAD_DIM), lambda i, h: (0, 0))],
        out_specs=pl.BlockSpec((tr, HEAD_DIM), lambda i, h: (i, h)),
        out_shape=jax.ShapeDtypeStruct((s, heads * HEAD_DIM), BF16),
        compiler_params=_params(("parallel", "parallel")),
    )(o, qkvz, w)


def _delta_post_bwd(dmix, o, qkvz, w, heads):
    s = o.shape[0]
    tr = _tile(s, 1024, 8)

    def body(d_ref, o_ref, z_ref, w_ref, do_ref, dz_ref, dw_ref):
        d, ov, z, wv = d_ref[...], o_ref[...], z_ref[...], w_ref[...]
        sg = _sigmoid(z)
        rstd = lax.rsqrt(jnp.mean(ov * ov, axis=-1, keepdims=True) + EPS)
        ohat = ov * rstd
        dz_ref[...] = (d * (ohat * wv) * sg * (1.0 + z * (1.0 - sg))).astype(BF16)
        dn = d * (z * sg)
        gw = dn * wv
        do_ref[...] = rstd * (gw - ohat * jnp.mean(gw * ohat, axis=-1, keepdims=True))

        @pl.when((pl.program_id(0) == 0) & (pl.program_id(1) == 0))
        def _():
            dw_ref[...] = jnp.zeros_like(dw_ref)

        dw_ref[...] += jnp.sum(dn * ohat, axis=0, keepdims=True)

    head = pl.BlockSpec((tr, HEAD_DIM), lambda i, h: (i, h))
    vec = pl.BlockSpec((1, HEAD_DIM), lambda i, h: (0, 0))
    dw = heads * HEAD_DIM
    return pl.pallas_call(
        body, name="delta_post_bwd", grid=(s // tr, heads),
        in_specs=[head, head, pl.BlockSpec((tr, HEAD_DIM), lambda i, h: (i, 3 * heads + h)), vec],
        out_specs=[head, head, vec],
        out_shape=[jax.ShapeDtypeStruct((s, dw), F32), jax.ShapeDtypeStruct((s, dw), BF16),
                   jax.ShapeDtypeStruct((1, HEAD_DIM), F32)],
        compiler_params=_params(("arbitrary", "arbitrary")),
    )(dmix, o, qkvz, w)


def _rope_tables(positions, s):
    half = HEAD_DIM // 2
    inv_freq = ROPE_THETA ** (-jnp.arange(half, dtype=F32) / half)
    ang = positions.reshape(s, 1).astype(F32) * inv_freq
    cos, sin = jnp.cos(ang), jnp.sin(ang)
    return jnp.concatenate([cos, cos], axis=-1), jnp.concatenate([-sin, sin], axis=-1)


def _attn_pre_fwd(aqkv, wq, wk, cosf, sinf, heads):
    s = aqkv.shape[0]
    tr = _tile(s, 1024, 8)

    def body(x_ref, wq_ref, wk_ref, c_ref, s_ref, o_ref):
        xv = x_ref[...]
        wv = jnp.where(pl.program_id(1) < heads, wq_ref[...], wk_ref[...])
        y = xv * lax.rsqrt(jnp.mean(xv * xv, axis=-1, keepdims=True) + EPS) * wv
        o_ref[...] = y * c_ref[...] + pltpu.roll(y, HEAD_DIM // 2, 1) * s_ref[...]

    blk = pl.BlockSpec((tr, HEAD_DIM), lambda i, j: (i, j))
    vec = pl.BlockSpec((1, HEAD_DIM), lambda i, j: (0, 0))
    tab = pl.BlockSpec((tr, HEAD_DIM), lambda i, j: (i, 0))
    return pl.pallas_call(
        body, name="attn_pre_fwd", grid=(s // tr, 2 * heads),
        in_specs=[blk, vec, vec, tab, tab], out_specs=blk,
        out_shape=jax.ShapeDtypeStruct((s, 2 * heads * HEAD_DIM), F32),
        compiler_params=_params(("parallel", "parallel")),
    )(aqkv, wq, wk, cosf, sinf)


def _band():
    qi = lax.broadcasted_iota(jnp.int32, (SPAN, 2 * SPAN), 0)
    ki = lax.broadcasted_iota(jnp.int32, (SPAN, 2 * SPAN), 1)
    dist = qi + SPAN - ki
    return (dist >= 0) & (dist <= SPAN), ki >= SPAN


def _sub(g, r, d):
    if d == 1:
        return pl.ds(g * SPAN, SPAN)
    return pl.ds(g * SPAN * d + r, SPAN, stride=d)


def _attn_blocks(s):
    assert s % (SPAN * max(DILATIONS)) == 0
    return [(p_i, d, r, g) for p_i, d in enumerate(DILATIONS) for r in range(d) for g in range(s // (SPAN * d))]


def _attn_fwd(qk, aqkv, w, heads):
    s = qk.shape[0]
    aw = heads * HEAD_DIM

    def body(q_ref, k_ref, v_ref, w_ref, mix_ref, acc_ref, m_ref, l_ref):
        band, own = _band()
        for p_i, d, r, g in _attn_blocks(s):
            rows = _sub(g, r, d)
            kc, vc = k_ref[rows, :], v_ref[rows, :]
            if g == 0:
                kp, vp, mask = kc, vc, band & own
            else:
                mask = band
            kcat = jnp.concatenate([kp, kc], axis=0)
            vcat = jnp.concatenate([vp, vc], axis=0)
            kp, vp = kc, vc
            sc = _dot(q_ref[rows, :] * (HEAD_DIM ** -0.5), kcat, "nt")
            sc = jnp.where(mask, sc, NEG)
            m = jnp.max(sc, axis=-1, keepdims=True)
            if p_i == 0:
                p = jnp.exp(sc - m)
                acc_ref[rows, :] = _dot(p, vcat)
                l_new = jnp.sum(p, axis=-1, keepdims=True)
            else:
                m_old = m_ref[rows, 0:1]
                m = jnp.maximum(m, m_old)
                alpha = jnp.exp(m_old - m)
                p = jnp.exp(sc - m)
                acc_ref[rows, :] = alpha * acc_ref[rows, :] + _dot(p, vcat)
                l_new = alpha * l_ref[rows, 0:1] + jnp.sum(p, axis=-1, keepdims=True)
            m_ref[rows, :] = jnp.broadcast_to(m, (SPAN, HEAD_DIM))
            l_ref[rows, :] = jnp.broadcast_to(l_new, (SPAN, HEAD_DIM))
        den = l_ref[...]
        ob = acc_ref[...] / den
        acc_ref[...] = ob
        m_ref[...] = m_ref[...] + jnp.log(den)
        rstd = lax.rsqrt(jnp.mean(ob * ob, axis=-1, keepdims=True) + EPS)
        mix_ref[...] = (ob * rstd * w_ref[...]).astype(BF16)

    col = lambda off: pl.BlockSpec((s, HEAD_DIM), lambda h: (0, off + h))
    return pl.pallas_call(
        body, name="attn_fwd", grid=(heads,),
        in_specs=[col(0), col(heads), col(2 * heads), pl.BlockSpec((1, HEAD_DIM), lambda h: (0, 0))],
        out_specs=[col(0), col(0), col(0)],
        out_shape=[jax.ShapeDtypeStruct((s, aw), BF16), jax.ShapeDtypeStruct((s, aw), F32),
                   jax.ShapeDtypeStruct((s, aw), F32)],
        scratch_shapes=[pltpu.VMEM((s, HEAD_DIM), F32)],
        compiler_params=_params(("parallel",)),
    )(qk, qk, aqkv, w)


def _attn_merge_bwd(dmix, ob, w, heads):
    s = ob.shape[0]
    tr = _tile(s, 1024, 8)

    def body(d_ref, ob_ref, w_ref, do_ref, dsum_ref, dw_ref):
        d, ov = d_ref[...], ob_ref[...]
        rstd = lax.rsqrt(jnp.mean(ov * ov, axis=-1, keepdims=True) + EPS)
        ohat = ov * rstd
        gw = d * w_ref[...]
        dov = rstd * (gw - ohat * jnp.mean(gw * ohat, axis=-1, keepdims=True))
        do_ref[...] = dov
        dsum_ref[...] = jnp.broadcast_to(jnp.sum(dov * ov, axis=-1, keepdims=True), dov.shape)

        @pl.when((pl.program_id(0) == 0) & (pl.program_id(1) == 0))
        def _():
            dw_ref[...] = jnp.zeros_like(dw_ref)

        dw_ref[...] += jnp.sum(d * ohat, axis=0, keepdims=True)

    blk = pl.BlockSpec((tr, HEAD_DIM), lambda i, h: (i, h))
    vec = pl.BlockSpec((1, HEAD_DIM), lambda i, h: (0, 0))
    aw = heads * HEAD_DIM
    return pl.pallas_call(
        body, name="attn_merge_bwd", grid=(s // tr, heads),
        in_specs=[pl.BlockSpec((tr, HEAD_DIM), lambda i, h: (i, heads + h)), blk, vec],
        out_specs=[blk, blk, vec],
        out_shape=[jax.ShapeDtypeStruct((s, aw), F32), jax.ShapeDtypeStruct((s, aw), F32),
                   jax.ShapeDtypeStruct((1, HEAD_DIM), F32)],
        compiler_params=_params(("arbitrary", "arbitrary")),
    )(dmix, ob, w)


def _attn_bwd(qk, aqkv, do, lse, dsum, heads):
    s = qk.shape[0]
    aw = heads * HEAD_DIM
    scale = HEAD_DIM ** -0.5

    def body(q_ref, k_ref, v_ref, do_ref, l_ref, ds_ref, out_ref):
        band, own = _band()
        dq_ref, dk_ref, dv_ref = out_ref.at[0], out_ref.at[1], out_ref.at[2]
        out_ref[...] = jnp.zeros((3, s, HEAD_DIM), F32)
        for _, d, r, g in _attn_blocks(s):
            rows = _sub(g, r, d)
            qs, dov = q_ref[rows, :] * scale, do_ref[rows, :]
            kc, vc = k_ref[rows, :], v_ref[rows, :]
            if g == 0:
                kp, vp, mask = kc, vc, band & own
            else:
                mask = band
            kcat = jnp.concatenate([kp, kc], axis=0)
            vcat = jnp.concatenate([vp, vc], axis=0)
            p = jnp.where(mask, jnp.exp(_dot(qs, kcat, "nt") - l_ref[rows, 0:1]), 0.0)
            dsc = p * (_dot(dov, vcat, "nt") - ds_ref[rows, 0:1])
            dq_ref[rows, :] += scale * _dot(dsc, kcat)
            dk = _dot(dsc, qs, "tn")
            dv = _dot(p, dov, "tn")
            if g > 0:
                dk_ref[prows, :] += dk_own + dk[:SPAN]
                dv_ref[prows, :] += dv_own + dv[:SPAN]
            dk_own, dv_own = dk[SPAN:], dv[SPAN:]
            if g == s // (SPAN * d) - 1:
                dk_ref[rows, :] += dk_own
                dv_ref[rows, :] += dv_own
            kp, vp, prows = kc, vc, rows

    col = lambda off: pl.BlockSpec((s, HEAD_DIM), lambda h: (0, off + h))
    return pl.pallas_call(
        body, name="attn_bwd", grid=(heads,),
        in_specs=[col(0), col(heads), col(2 * heads), col(0), col(0), col(0)],
        out_specs=pl.BlockSpec((3, s, HEAD_DIM), lambda h: (0, 0, h)),
        out_shape=jax.ShapeDtypeStruct((3, s, aw), F32),
        compiler_params=_params(("parallel",)),
    )(qk, qk, aqkv, do, lse, dsum)


def _attn_pre_bwd(grads, aqkv, wq, wk, cosf, sinf, heads):
    s = aqkv.shape[0]
    tr = _tile(s, 1024, 8)
    nrow = s // tr

    def body(g_ref, x_ref, wq_ref, wk_ref, c_ref, s_ref, dx_ref, dwq_ref, dwk_ref):
        i, j = pl.program_id(0), pl.program_id(1)
        kind = j // heads
        dout = g_ref[0]
        tot_v = dout
        dy = dout * c_ref[...] + pltpu.roll(dout * s_ref[...], HEAD_DIM // 2, 1)
        xv = x_ref[...]
        wv = jnp.where(kind == 0, wq_ref[...], wk_ref[...])
        rstd = lax.rsqrt(jnp.mean(xv * xv, axis=-1, keepdims=True) + EPS)
        xhat = xv * rstd
        gw = dy * wv
        dxn = rstd * (gw - xhat * jnp.mean(gw * xhat, axis=-1, keepdims=True))
        dx_ref[...] = jnp.where(kind == 2, tot_v, dxn).astype(BF16)
        dwc = jnp.sum(dy * xhat, axis=0, keepdims=True)

        @pl.when((i == 0) & (j == 0))
        def _():
            dwq_ref[...] = jnp.zeros_like(dwq_ref)
            dwk_ref[...] = jnp.zeros_like(dwk_ref)

        @pl.when(kind == 0)
        def _():
            dwq_ref[...] += dwc

        @pl.when(kind == 1)
        def _():
            dwk_ref[...] += dwc

    grad = pl.BlockSpec((1, tr, HEAD_DIM), lambda i, j: (j // heads, i, j % heads))
    blk = pl.BlockSpec((tr, HEAD_DIM), lambda i, j: (i, j))
    vec = pl.BlockSpec((1, HEAD_DIM), lambda i, j: (0, 0))
    tab = pl.BlockSpec((tr, HEAD_DIM), lambda i, j: (i, 0))
    return pl.pallas_call(
        body, name="attn_pre_bwd", grid=(nrow, 3 * heads),
        in_specs=[grad, blk, vec, vec, tab, tab], out_specs=[blk, vec, vec],
        out_shape=[jax.ShapeDtypeStruct((s, 3 * heads * HEAD_DIM), BF16),
                   jax.ShapeDtypeStruct((1, HEAD_DIM), F32), jax.ShapeDtypeStruct((1, HEAD_DIM), F32)],
        compiler_params=_params(("arbitrary", "arbitrary")),
    )(grads, aqkv, wq, wk, cosf, sinf)


def _colsum(a, name):
    s, d = a.shape
    tr = _tile(s, 1024, 8)

    def body(a_ref, o_ref):
        @pl.when(pl.program_id(0) == 0)
        def _():
            o_ref[...] = jnp.zeros_like(o_ref)

        o_ref[...] += jnp.sum(a_ref[...], axis=0, keepdims=True)

    return pl.pallas_call(
        body, name=name, grid=(s // tr,),
        in_specs=[pl.BlockSpec((tr, d), lambda i: (i, 0))], out_specs=pl.BlockSpec((1, d), lambda i: (0, 0)),
        out_shape=jax.ShapeDtypeStruct((1, d), F32),
        compiler_params=_params(("arbitrary",)),
    )(a)


def _local_step(x, positions, target, small, w_qkvz, w_bd, w_attn, conv_w, later_weights, ffn_grads_ready,
                rest_grads_ready, after=None):
    s, dmod = x.shape
    heads = dmod // (2 * HEAD_DIM)
    dw = heads * HEAD_DIM
    a_log, dt_bias = small["a_log"].reshape(heads, 1, 1), small["dt_bias"].reshape(heads, 1, 1)
    cosf, sinf = _rope_tables(positions, s)

    h1 = _rmsnorm_fwd(x, small["attn_norm_w"], "norm1_fwd", after=after)
    qkvz = _matmul(h1, w_qkvz, "nn", "proj_qkvz")
    bd = _matmul(h1, w_bd, "nn", "proj_bd")
    aqkv = _matmul(h1, w_attn, "nn", "proj_attn")
    dqkv = _delta_pre_fwd(qkvz, conv_w, heads)
    o_d, *saved = _delta_chunk_fwd(dqkv, bd, a_log, dt_bias, heads)
    mix_a = _delta_post_fwd(o_d, qkvz, small["delta_out_norm_w"], heads)
    qk_rot = _attn_pre_fwd(aqkv, small["q_norm_w"], small["k_norm_w"], cosf, sinf, heads)
    mix_b, ob, lse = _attn_fwd(qk_rot, aqkv, small["attn_out_norm_w"], heads)
    w_out, behind, ffn_weights = later_weights((mix_a, mix_b))
    x1 = _matmul(mix_a, w_out, "nn", "out_proj_a", add=x, b_rows=(0, dw), after=behind)
    x1 = _matmul(mix_b, w_out, "nn", "out_proj_b", add=x1, b_rows=(dw, dw))
    h2 = _rmsnorm_fwd(x1, small["ffn_norm_w"], "norm2_fwd")
    w_gu, w_down = ffn_weights(h2)
    ff = w_down.shape[0]

    def swiglu(accs, vals):
        g, u = accs
        return g, u, g * _sigmoid(g) * u

    gate, up, act = _matmul(h2, w_gu, "nn", "ffn_gate_up", b_cols=(0, ff), b2_cols=(ff, ff), finish=swiglu,
                            out_dtypes=[BF16, BF16, BF16])

    def loss_head(acc, vals):
        err = acc + vals[0] - vals[1]
        part = 0.5 * jnp.sum(jnp.sum(err * err, axis=-1, keepdims=True) * (1.0 / dmod), axis=0, keepdims=True)
        lane = lax.broadcasted_iota(jnp.int32, (1, HEAD_DIM), 1)
        return err * (1.0 / dmod), err * (1.0 / dmod), jnp.where(lane == 0, part, 0.0)

    dy, dy16, loss_row = _matmul(act, w_down, "nn", "ffn_down", tiles=[(x1, 0), (target, 0)], finish=loss_head,
                                 out_dtypes=[F32, BF16], row_sum=True)

    def swiglu_bwd(acc, vals):
        g, u = vals[0].astype(F32), vals[1].astype(F32)
        sg = _sigmoid(g)
        return acc * u * sg * (1.0 + g * (1.0 - sg)), acc * g * sg

    dgate, dup = _matmul(dy16, w_down, "nt", "ffn_down_dx", tiles=[(gate, 0), (up, 0)], finish=swiglu_bwd,
                         out_dtypes=[BF16, BF16])
    g_w_down = _matmul(act, dy16, "tn", "ffn_down_dw")
    g_w_gate = _matmul(h2, dgate, "tn", "ffn_gate_dw")
    g_w_up = _matmul(h2, dup, "tn", "ffn_up_dw")
    behind, and_then = ffn_grads_ready(g_w_gate, g_w_up, g_w_down)
    dh2 = _matmul(dgate, w_gu, "nt", "ffn_gate_dx", b_cols=(0, ff), after=behind)
    dh2 = _matmul(dup, w_gu, "nt", "ffn_up_dx", b_cols=(ff, ff), add=dh2)
    behind = and_then(dh2)
    dx1, dx1_16, g_ffn_norm = _rmsnorm_bwd(dh2, x1, small["ffn_norm_w"], dy, "norm2_bwd", after=behind)
    dmix = _matmul(dx1_16, w_out, "nt", "out_proj_dx")
    g_w_out_a = _matmul(mix_a, dx1_16, "tn", "out_proj_dw_a")
    g_w_out_b = _matmul(mix_b, dx1_16, "tn", "out_proj_dw_b")
    dob, dsum, g_attn_out_norm = _attn_merge_bwd(dmix, ob, small["attn_out_norm_w"], heads)
    grads = _attn_bwd(qk_rot, aqkv, dob, lse, dsum, heads)
    d_aqkv, g_q_norm, g_k_norm = _attn_pre_bwd(grads, aqkv, small["q_norm_w"], small["k_norm_w"], cosf, sinf, heads)
    do_d, dz, g_delta_out_norm = _delta_post_bwd(dmix, o_d, qkvz, small["delta_out_norm_w"], heads)
    ddqkv, dbd = _delta_chunk_bwd(do_d, dqkv, bd, saved, a_log, dt_bias, heads)
    d_qkv_raw, g_conv = _delta_pre_bwd(ddqkv, qkvz, conv_w, heads)
    bd_sums = _colsum(dbd, "bd_colsum")
    g_w_qkv = _matmul(h1, d_qkv_raw, "tn", "proj_qkv_dw")
    g_w_z = _matmul(h1, dz, "tn", "proj_z_dw")
    g_w_bd = _matmul(h1, dbd, "tn", "proj_bd_dw")
    g_w_attn = _matmul(h1, d_aqkv, "tn", "proj_attn_dw")
    behind, and_then = rest_grads_ready(dict(w_qkv=g_w_qkv, w_z=g_w_z, w_bd=g_w_bd, w_attn=g_w_attn,
                                             w_out_a=g_w_out_a, w_out_b=g_w_out_b))
    dh1 = _matmul(d_qkv_raw, w_qkvz, "nt", "proj_qkv_dx", b_cols=(0, 3 * dw), after=behind)
    dh1 = _matmul(dz, w_qkvz, "nt", "proj_z_dx", b_cols=(3 * dw, dw), add=dh1, after=and_then(dh1))
    dh1 = _matmul(d_aqkv, w_attn, "nt", "proj_attn_dx", add=dh1)
    dh1 = _matmul(dbd, w_bd, "nt", "proj_bd_dx", add=dh1)
    grad_x, _, g_attn_norm = _rmsnorm_bwd(dh1, x, small["attn_norm_w"], dx1, "norm1_bwd")
    small_grads = dict(
        attn_norm_w=g_attn_norm, a_log=bd_sums[:, 2 * heads:3 * heads], dt_bias=bd_sums[:, heads:2 * heads],
        delta_out_norm_w=g_delta_out_norm, q_norm_w=g_q_norm, k_norm_w=g_k_norm,
        attn_out_norm_w=g_attn_out_norm, ffn_norm_w=g_ffn_norm, conv_w=g_conv)
    return loss_row, grad_x, small_grads


def _adamw(w, g, m, v, name):
    r, c = w.shape
    tr = _tile(r, 256, 8)

    def body(w_ref, g_ref, m_ref, v_ref, d_ref, nm_ref, nv_ref):
        gv = g_ref[...]
        nm = ADAM_B1 * m_ref[...] + (1.0 - ADAM_B1) * gv
        nv = ADAM_B2 * v_ref[...] + (1.0 - ADAM_B2) * (gv * gv)
        m_hat = nm / (1.0 - ADAM_B1 ** ADAM_STEP)
        v_hat = nv / (1.0 - ADAM_B2 ** ADAM_STEP)
        d_ref[...] = -ADAM_LR * (m_hat / (jnp.sqrt(v_hat) + ADAM_EPS) + ADAM_WD * w_ref[...])
        nm_ref[...] = nm
        nv_ref[...] = nv

    blk = pl.BlockSpec((tr, c), lambda i: (i, 0))
    return pl.pallas_call(
        body, name=name, grid=(r // tr,),
        in_specs=[blk] * 4, out_specs=[blk] * 3,
        out_shape=[jax.ShapeDtypeStruct((r, c), F32)] * 3,
        compiler_params=_params(("parallel",)),
    )(w, g, m, v)


def _add_half_bf16(g, b, place, name):
    n, half, c = b.shape
    tr = _tile(half, 512, 16)
    nb = half // tr

    def body(place_ref, g_ref, b_ref, o_ref):
        o_ref[...] = (g_ref[...].astype(F32) + b_ref[...].astype(F32)).astype(BF16)

    blk = pl.BlockSpec((1, tr, c), lambda i, j, p: (i, j, 0))
    return pl.pallas_call(
        body, name=name,
        grid_spec=pltpu.PrefetchScalarGridSpec(
            num_scalar_prefetch=1, grid=(n, nb),
            in_specs=[pl.BlockSpec((1, tr, c), lambda i, j, p: (i, p[0] * nb + j, 0)), blk], out_specs=blk),
        out_shape=jax.ShapeDtypeStruct((n, half, c), BF16),
        compiler_params=_params(("parallel", "parallel")),
    )(place, g, b)


def _sum4_f32(mine, others, place, name):
    _, half, c = mine.shape
    tr = _tile(half, 512, 16)
    nb = half // tr

    def body(place_ref, a_ref, b_ref, o_ref):
        acc = a_ref[0].astype(F32)
        for j in range(3):
            acc = acc + b_ref[j].astype(F32)
        o_ref[...] = acc

    return pl.pallas_call(
        body, name=name,
        grid_spec=pltpu.PrefetchScalarGridSpec(
            num_scalar_prefetch=1, grid=(nb,),
            in_specs=[pl.BlockSpec((1, tr, c), lambda i, p: (p[1], i, 0)),
                      pl.BlockSpec((3, tr, c), lambda i, p: (0, i, 0))],
            out_specs=pl.BlockSpec((tr, c), lambda i, p: (p[0] * nb + i, 0))),
        out_shape=jax.ShapeDtypeStruct((2 * half, c), F32),
        compiler_params=_params(("parallel",)),
    )(place, mine, others)


def _place():
    x, y, c = lax.axis_index("x"), lax.axis_index("y"), lax.axis_index("c")
    other_chips = [(1 - x, y), (x, 1 - y), (1 - x, 1 - y)]
    return x, y, c, (x, y, 1 - c), other_chips


ANY = pl.BlockSpec(memory_space=pl.ANY)


def _remote(k, src, dst, to, send_sems, recv_sems):
    return pltpu.make_async_remote_copy(src_ref=src, dst_ref=dst, send_sem=send_sems.at[k], recv_sem=recv_sems.at[k],
                                        device_id=to, device_id_type=MESH)


def _half(ref, lead, hc):
    if lead is None:
        half = ref.shape[0] // 2
        return ref.at[pl.ds(hc * half, half), :]
    half = ref.shape[1] // 2
    return ref.at[lead, pl.ds(hc * half, half), :]


def _all_gather_weights(slots, whole, name):
    nt, nw = len(slots), len(whole)
    base_w, base_f, base_d = 2 * nt, 2 * nt + 3 * nw, 4 * nt + 3 * nw

    def quarter(ref, lead, hc, q):
        quart = ref.shape[1] // 4
        return ref.at[lead, pl.ds((2 * hc + q) * quart, quart), :]

    def body(*refs):
        ins, outs = refs[:nt + nw], refs[nt + nw:2 * (nt + nw)]
        sems = refs[2 * (nt + nw):]
        x, y, c, sibling, chips = _place()
        me, xn, yn, dg = 2 * x + y, 2 * (1 - x) + y, 2 * x + 1 - y, 2 * (1 - x) + 1 - y
        to_x, to_y = (1 - x, y, c), (x, 1 - y, c)
        cps = []
        for t in range(nt):
            cps.append(_remote(2 * t, _half(ins[t], me, c), _half(outs[t], me, c), to_x, *sems))
            cps.append(_remote(2 * t + 1, _half(ins[t], me, c), _half(outs[t], me, c), to_y, *sems))
        for j, (px, py) in enumerate(chips):
            for t in range(nw):
                cps.append(_remote(base_w + j * nw + t, ins[nt + t].at[me], outs[nt + t].at[me], (px, py, c), *sems))
        for cp in cps:
            cp.start()

        def start(k, ref, to):
            cp = _remote(k, ref, ref, to, *sems)
            cp.start()
            cps.append(cp)

        for t in range(nt):
            landed = _half(outs[t], xn, c)
            _remote(2 * t, landed, landed, to_x, *sems).wait_recv()
            start(base_f + 2 * t, quarter(outs[t], xn, c, 0), to_y)
            start(base_d + 3 * t, landed, sibling)
            landed = _half(outs[t], yn, c)
            _remote(2 * t + 1, landed, landed, to_y, *sems).wait_recv()
            start(base_f + 2 * t + 1, quarter(outs[t], yn, c, 1), to_x)
            start(base_d + 3 * t + 1, landed, sibling)
        for t in range(nt):
            q0, q1 = quarter(outs[t], dg, c, 0), quarter(outs[t], dg, c, 1)
            _remote(base_f + 2 * t, q0, q0, to_y, *sems).wait_recv()
            _remote(base_f + 2 * t + 1, q1, q1, to_x, *sems).wait_recv()
            start(base_d + 3 * t + 2, _half(outs[t], dg, c), sibling)
        for j, (px, py) in enumerate(chips):
            for t in range(nw):
                landed = outs[nt + t].at[2 * px + py]
                _remote(base_w + j * nw + t, landed, landed, (px, py, c), *sems).wait_recv()
        for t in range(nt):
            for j, chip in enumerate((xn, yn, dg)):
                other = _half(outs[t], chip, 1 - c)
                _remote(base_d + 3 * t + j, other, other, sibling, *sems).wait_recv()
        for cp in cps:
            cp.wait_send()

    arrays = list(slots) + list(whole)
    n_sem = 7 * nt + 3 * nw
    return pl.pallas_call(
        body, name=name, in_specs=[ANY] * len(arrays), out_specs=[ANY] * len(arrays),
        input_output_aliases={i: i for i in range(len(arrays))},
        out_shape=[jax.ShapeDtypeStruct(a.shape, a.dtype) for a in arrays],
        scratch_shapes=[pltpu.SemaphoreType.DMA((n_sem,)), pltpu.SemaphoreType.DMA((n_sem,))],
    )(*arrays)


def _swap_halves_with_sibling(gs, name):
    nt = len(gs)

    def body(*refs):
        g_refs, o_refs, (send_sems, recv_sems) = refs[:nt], refs[nt:2 * nt], refs[2 * nt:]
        _, _, c, sibling, _ = _place()
        cps = []
        for t in range(nt):
            half = g_refs[t].shape[1] // 2
            cps.append(_remote(t, g_refs[t].at[:, pl.ds((1 - c) * half, half), :], o_refs[t], sibling,
                               send_sems, recv_sems))
        for cp in cps:
            cp.start()
        for cp in cps:
            cp.wait()

    return pl.pallas_call(
        body, name=name, in_specs=[ANY] * nt, out_specs=[ANY] * nt,
        out_shape=[jax.ShapeDtypeStruct((g.shape[0], g.shape[1] // 2, g.shape[2]), g.dtype) for g in gs],
        scratch_shapes=[pltpu.SemaphoreType.DMA((nt,)), pltpu.SemaphoreType.DMA((nt,))],
    )(*gs)


def _scatter_to_chips(ps, name):
    nt = len(ps)

    def body(*refs):
        p_refs, b_refs, (send_sems, recv_sems) = refs[:nt], refs[nt:2 * nt], refs[2 * nt:]
        _, _, c, _, chips = _place()
        cps = [_remote(j * nt + t, p_refs[t].at[2 * px + py], b_refs[t].at[j], (px, py, c), send_sems, recv_sems)
               for j, (px, py) in enumerate(chips) for t in range(nt)]
        for cp in cps:
            cp.start()
        for cp in cps:
            cp.wait()

    return pl.pallas_call(
        body, name=name, in_specs=[ANY] * nt, out_specs=[ANY] * nt,
        out_shape=[jax.ShapeDtypeStruct((3,) + p.shape[1:], p.dtype) for p in ps],
        scratch_shapes=[pltpu.SemaphoreType.DMA((3 * nt,)), pltpu.SemaphoreType.DMA((3 * nt,))],
    )(*ps)


def _join_halves(fs, name):
    nt = len(fs)

    def body(*refs):
        in_refs, out_refs, (send_sems, recv_sems) = refs[:nt], refs[nt:2 * nt], refs[2 * nt:]
        _, _, c, sibling, _ = _place()
        cps = [_remote(t, _half(in_refs[t], None, c), _half(out_refs[t], None, c), sibling, send_sems, recv_sems)
               for t in range(nt)]
        for cp in cps:
            cp.start()
        for t in range(nt):
            theirs = _half(out_refs[t], None, 1 - c)
            _remote(t, theirs, theirs, sibling, send_sems, recv_sems).wait_recv()
        for cp in cps:
            cp.wait_send()

    return pl.pallas_call(
        body, name=name, in_specs=[ANY] * nt, out_specs=[ANY] * nt,
        input_output_aliases={i: i for i in range(nt)},
        out_shape=[jax.ShapeDtypeStruct(f.shape, f.dtype) for f in fs],
        scratch_shapes=[pltpu.SemaphoreType.DMA((nt,)), pltpu.SemaphoreType.DMA((nt,))],
    )(*fs)


HBM = pl.BlockSpec(memory_space=pltpu.HBM)
SEM = pl.BlockSpec(memory_space=pltpu.SEMAPHORE)
EFFECT = pltpu.SideEffectType.DATAFLOW_SIDE_EFFECTING


def _split_start(arrays, after, plan, n_copies, name):
    na = len(arrays)

    def body(*refs):
        ins, send_sems, recv_sems = refs[:na], refs[na + 1], refs[na + 2]
        outs, token = refs[na + 3:2 * na + 3], refs[2 * na + 3]
        for k, (src, dst, to) in enumerate(plan(ins, outs)):
            _remote(k, src, dst, to, send_sems, recv_sems).start()
        token[...] = jnp.zeros_like(token)

    res = pl.pallas_call(
        body, name=name,
        out_shape=(pltpu.SemaphoreType.DMA((n_copies,)), pltpu.SemaphoreType.DMA((n_copies,)),
                   *[pltpu.HBM(a.shape, a.dtype) for a in arrays], jax.ShapeDtypeStruct((8, HEAD_DIM), F32)),
        in_specs=[HBM] * na + [ANY],
        out_specs=(SEM, SEM, *[HBM] * na, pl.BlockSpec(memory_space=pltpu.VMEM)),
        input_output_aliases={i: 2 + i for i in range(na)},
        compiler_params=pltpu.CompilerParams(has_side_effects=EFFECT),
    )(*[pltpu.with_memory_space_constraint(a, pltpu.HBM) for a in arrays], after)
    return res[0], res[1], list(res[2:2 + na]), res[2 + na]


def _split_wait(send_sems, recv_sems, arrays, after, plan, name):
    na = len(arrays)
    after = list(after) if isinstance(after, (list, tuple)) else [after]

    def body(*refs):
        ins, send, recv, outs = refs[:na], refs[na], refs[na + 1], refs[na + 2 + len(after):]
        for k, (src, dst, to) in enumerate(plan(ins, outs)):
            cp = _remote(k, src, dst, to, send, recv)
            cp.wait_send()
            cp.wait_recv()

    res = pl.pallas_call(
        body, name=name, out_shape=tuple(pltpu.HBM(a.shape, a.dtype) for a in arrays),
        in_specs=[HBM] * na + [SEM, SEM] + [ANY] * len(after), out_specs=tuple([HBM] * na),
        input_output_aliases={i: i for i in range(na)},
        compiler_params=pltpu.CompilerParams(has_side_effects=EFFECT),
    )(*arrays, send_sems, recv_sems, *after)
    return list(res)


def _gather_plan(nt):
    def plan(ins, outs):
        x, y, c, _, chips = _place()
        me = 2 * x + y
        return [(_half(ins[t], me, c), _half(outs[t], me, c), (px, py, c)) for px, py in chips for t in range(nt)]
    return plan


def _gather_landed_plan(nt):
    def plan(ins, outs):
        _, _, c, _, chips = _place()
        return [(_half(outs[t], 2 * px + py, c), _half(outs[t], 2 * px + py, c), (px, py, c))
                for px, py in chips for t in range(nt)]
    return plan


def _scatter_plan(nt):
    def plan(ins, outs):
        _, _, c, _, chips = _place()
        return [(ins[t].at[2 * px + py], outs[nt + t].at[j], (px, py, c))
                for j, (px, py) in enumerate(chips) for t in range(nt)]
    return plan


def _pass_plan(nt):
    def plan(ins, outs):
        _, _, c, sibling, chips = _place()
        return [(_half(ins[t], 2 * px + py, c), _half(outs[t], 2 * px + py, c), sibling)
                for px, py in chips for t in range(nt)]
    return plan


def _pass_landed_plan(nt):
    def plan(ins, outs):
        _, _, c, sibling, chips = _place()
        return [(_half(outs[t], 2 * px + py, c), _half(outs[t], 2 * px + py, 1 - c), sibling)
                for px, py in chips for t in range(nt)]
    return plan


def _swap_plan(nt):
    def plan(ins, outs):
        _, _, c, sibling, _ = _place()
        res = []
        for t in range(nt):
            half = ins[t].shape[1] // 2
            res.append((ins[t].at[:, pl.ds((1 - c) * half, half), :], outs[nt + t], sibling))
        return res
    return plan


def _pass_halves_to_sibling(slots, name):
    nt = len(slots)

    def body(*refs):
        ins, outs, (send_sems, recv_sems) = refs[:nt], refs[nt:2 * nt], refs[2 * nt:]
        _, _, c, sibling, chips = _place()
        cps = [_remote(j * nt + t, _half(ins[t], 2 * px + py, c), _half(outs[t], 2 * px + py, c), sibling,
                       send_sems, recv_sems)
               for j, (px, py) in enumerate(chips) for t in range(nt)]
        for cp in cps:
            cp.start()
        for j, (px, py) in enumerate(chips):
            for t in range(nt):
                other = _half(outs[t], 2 * px + py, 1 - c)
                _remote(j * nt + t, other, other, sibling, send_sems, recv_sems).wait_recv()
        for cp in cps:
            cp.wait_send()

    return pl.pallas_call(
        body, name=name, in_specs=[ANY] * nt, out_specs=[ANY] * nt,
        input_output_aliases={i: i for i in range(nt)},
        out_shape=[jax.ShapeDtypeStruct(a.shape, a.dtype) for a in slots],
        scratch_shapes=[pltpu.SemaphoreType.DMA((3 * nt,)), pltpu.SemaphoreType.DMA((3 * nt,))],
    )(*slots)


def _all_reduce_small(v):
    r, lanes = v.shape

    def body(v_ref, out_ref, buf, send_sems, recv_sems):
        x, y, c, sibling, chips = _place()

        def slot(px, py, pc):
            return buf.at[4 * px + 2 * py + pc]

        def copy(k, block, to, src=None):
            return pltpu.make_async_remote_copy(src_ref=slot(*block) if src is None else src, dst_ref=slot(*block),
                                                send_sem=send_sems.at[k], recv_sem=recv_sems.at[k],
                                                device_id=to, device_id_type=MESH)

        me = (x, y, c)
        buf[4 * x + 2 * y + c] = v_ref[...]
        first = [copy(0, me, sibling, src=v_ref)]
        first += [copy(1 + j, me, (*chip, c), src=v_ref) for j, chip in enumerate(chips)]
        for cp in first:
            cp.start()
        passed = [copy(4 + j, (*chip, c), sibling) for j, chip in enumerate(chips)]
        for j, chip in enumerate(chips):
            copy(1 + j, (*chip, c), me).wait_recv()
            passed[j].start()
        copy(0, (x, y, 1 - c), me).wait_recv()
        for j, chip in enumerate(chips):
            copy(4 + j, (*chip, 1 - c), me).wait_recv()
        for cp in first + passed:
            cp.wait_send()
        acc = buf[0]
        for k in range(1, 8):
            acc = acc + buf[k]
        out_ref[...] = acc

    vmem = pl.BlockSpec(memory_space=pltpu.VMEM)
    return pl.pallas_call(
        body, name="all_reduce_small", in_specs=[vmem], out_specs=vmem,
        out_shape=jax.ShapeDtypeStruct((r, lanes), F32),
        scratch_shapes=[pltpu.VMEM((8, r, lanes), F32), pltpu.SemaphoreType.DMA((7,)), pltpu.SemaphoreType.DMA((7,))],
    )(v)


def _size(shape):
    n = 1
    for d in shape:
        n *= d
    return n


def _pack_small(parts):
    rows = []
    for p in parts:
        f = p.reshape(-1).astype(F32)
        n = -(-f.shape[0] // HEAD_DIM) * HEAD_DIM
        rows.append(jnp.pad(f, (0, n - f.shape[0])).reshape(-1, HEAD_DIM))
    a = jnp.concatenate(rows, axis=0)
    return jnp.pad(a, ((0, -a.shape[0] % 8), (0, 0)))


def _unpack_small(a, shapes):
    out, row = [], 0
    for shp in shapes:
        nrows = -(-_size(shp) // HEAD_DIM)
        out.append(a[row:row + nrows].reshape(-1)[:_size(shp)].reshape(shp))
        row += nrows
    return out


SMALL = ["attn_norm_w", "a_log", "dt_bias", "delta_out_norm_w", "q_norm_w", "k_norm_w", "attn_out_norm_w", "ffn_norm_w"]
BIG = ["w_in", "w_out", "w_gate_up", "w_down"]
ORDER = ["attn_norm_w", "w_in", "conv_w", "a_log", "dt_bias", "delta_out_norm_w", "q_norm_w", "k_norm_w",
         "attn_out_norm_w", "w_out", "ffn_norm_w", "w_gate_up", "w_down"]


def kernel(x, positions, attn_norm_w, w_in, conv_w, a_log, dt_bias, delta_out_norm_w, q_norm_w, k_norm_w, attn_out_norm_w, w_out, ffn_norm_w, w_gate_up, w_down, loss_target, m_attn_norm_w, m_w_in, m_conv_w, m_a_log, m_dt_bias, m_delta_out_norm_w, m_q_norm_w, m_k_norm_w, m_attn_out_norm_w, m_w_out, m_ffn_norm_w, m_w_gate_up, m_w_down, v_attn_norm_w, v_w_in, v_conv_w, v_a_log, v_dt_bias, v_delta_out_norm_w, v_q_norm_w, v_k_norm_w, v_attn_out_norm_w, v_w_out, v_ffn_norm_w, v_w_gate_up, v_w_down):
    wts = dict(attn_norm_w=attn_norm_w, w_in=w_in, conv_w=conv_w, a_log=a_log, dt_bias=dt_bias,
               delta_out_norm_w=delta_out_norm_w, q_norm_w=q_norm_w, k_norm_w=k_norm_w,
               attn_out_norm_w=attn_out_norm_w, w_out=w_out, ffn_norm_w=ffn_norm_w, w_gate_up=w_gate_up, w_down=w_down)
    mom = dict(attn_norm_w=m_attn_norm_w, w_in=m_w_in, conv_w=m_conv_w, a_log=m_a_log, dt_bias=m_dt_bias,
               delta_out_norm_w=m_delta_out_norm_w, q_norm_w=m_q_norm_w, k_norm_w=m_k_norm_w,
               attn_out_norm_w=m_attn_out_norm_w, w_out=m_w_out, ffn_norm_w=m_ffn_norm_w, w_gate_up=m_w_gate_up,
               w_down=m_w_down)
    var = dict(attn_norm_w=v_attn_norm_w, w_in=v_w_in, conv_w=v_conv_w, a_log=v_a_log, dt_bias=v_dt_bias,
               delta_out_norm_w=v_delta_out_norm_w, q_norm_w=v_q_norm_w, k_norm_w=v_k_norm_w,
               attn_out_norm_w=v_attn_out_norm_w, w_out=v_w_out, ffn_norm_w=v_ffn_norm_w, w_gate_up=v_w_gate_up,
               w_down=v_w_down)
    dmod = x.shape[2]
    heads = dmod // (2 * HEAD_DIM)
    dw = heads * HEAD_DIM
    chip = 2 * lax.axis_index("x") + lax.axis_index("y")
    core = lax.axis_index("c")
    n_in, n_out, n_gu, n_down, n_conv = (w_in.shape[2], w_out.shape[1], w_gate_up.shape[2], w_down.shape[1],
                                         conv_w.shape[2])

    def slots_of(w, dtype):
        shard = w[0].astype(dtype)
        return lax.dynamic_update_index_in_dim(lax.empty((4,) + shard.shape, dtype), shard, chip, axis=0)

    s_in, s_conv = _all_gather_weights([slots_of(w_in, BF16)], [slots_of(conv_w, F32)], "all_gather_w_in")
    later = [slots_of(w_out, BF16), slots_of(w_gate_up, BF16), slots_of(w_down, BF16)]
    w_send, w_recv, later, started = _split_start(later, s_conv, _gather_plan(3), 9, "gather_rest_start")
    by_cols = lambda a: a.transpose(1, 0, 2).reshape(a.shape[1], 4 * a.shape[2])
    w_in_f, conv_f = by_cols(s_in), by_cols(s_conv)
    w_bd = jnp.pad(w_in_f[:, 4 * dw:4 * dw + 2 * heads], ((0, 0), (0, HEAD_DIM - 2 * heads)))
    small = {n: wts[n] for n in SMALL}
    place = jnp.stack([core, chip]).astype(jnp.int32)
    to_slots = lambda a: a.reshape(a.shape[0], 4, a.shape[1] // 4).transpose(1, 0, 2)

    def later_weights(after):
        landed = _split_wait(w_send, w_recv, later, after, _gather_landed_plan(3), "gather_rest_wait")
        s_out, = _pass_halves_to_sibling(landed[:1], "gather_out_pass")
        p_send, p_recv, passing, token = _split_start(landed[1:], s_out, _pass_plan(2), 6, "gather_ffn_pass_start")

        def ffn_weights(after):
            s_gu, s_down = _split_wait(p_send, p_recv, passing, after, _pass_landed_plan(2), "gather_ffn_pass_wait")
            return by_cols(s_gu), s_down.reshape(4 * n_down, dmod)

        return s_out.reshape(4 * n_out, dmod), token, ffn_weights

    ffn = {}

    def ffn_grads_ready(g_gate, g_up, g_down):
        gs = [to_slots(jnp.concatenate([g_gate, g_up], axis=1)), g_down.reshape(4, n_down, dmod)]
        zones = [lax.empty((4, g.shape[1] // 2, g.shape[2]), BF16) for g in gs]
        s_send, s_recv, s_bufs, swapping = _split_start(gs + zones, g_gate, _swap_plan(2), 2, "swap_ffn_start")

        def and_then(after):
            g_gu, g_dn, b_gu, b_dn = _split_wait(s_send, s_recv, s_bufs, after, _swap_plan(2), "swap_ffn_wait")
            sums = [_add_half_bf16(g_gu, b_gu, place, "chip_partial_sum_w_gate_up"),
                    _add_half_bf16(g_dn, b_dn, place, "chip_partial_sum_w_down")]
            zones3 = [lax.empty((3,) + p.shape[1:], BF16) for p in sums]
            ffn["send"], ffn["recv"], ffn["bufs"], token = _split_start(sums + zones3, b_gu, _scatter_plan(2), 6,
                                                                        "scatter_ffn_start")
            return token

        return swapping, and_then

    rest = {}

    def rest_grads_ready(bg):
        gs = [to_slots(jnp.concatenate([bg["w_qkv"], bg["w_z"], bg["w_bd"][:, :2 * heads], bg["w_attn"]], axis=1)),
              jnp.concatenate([bg["w_out_a"], bg["w_out_b"]], axis=0).reshape(4, n_out, dmod)]
        zones = [lax.empty((4, g.shape[1] // 2, g.shape[2]), BF16) for g in gs]
        s_send, s_recv, s_bufs, swapping = _split_start(gs + zones, bg["w_attn"], _swap_plan(2), 2, "swap_rest_start")

        def and_then(after):
            g_in, g_out, b_in, b_out = _split_wait(s_send, s_recv, s_bufs, after, _swap_plan(2), "swap_rest_wait")
            sums = [_add_half_bf16(g_in, b_in, place, "chip_partial_sum_w_in"),
                    _add_half_bf16(g_out, b_out, place, "chip_partial_sum_w_out")]
            zones3 = [lax.empty((3,) + p.shape[1:], BF16) for p in sums]
            rest["send"], rest["recv"], rest["bufs"], token = _split_start(sums + zones3, b_in, _scatter_plan(2), 6,
                                                                           "scatter_rest_start")
            return token

        return swapping, and_then

    loss_row, grad_x, sg = _local_step(
        x[0], positions[0], loss_target[0], small, w_in_f[:, :4 * dw], w_bd, w_in_f[:, 4 * dw + 2 * heads:], conv_f,
        later_weights, ffn_grads_ready, rest_grads_ready, after=started)

    r_send, r_recv, r_bufs = rest["send"], rest["recv"], rest["bufs"]
    sum_gu, sum_down, got_gu, got_down = _split_wait(ffn["send"], ffn["recv"], ffn["bufs"], grad_x,
                                                     _scatter_plan(2), "scatter_ffn_wait")
    g_big = dict(zip(["w_gate_up", "w_down"], _join_halves(
        [_sum4_f32(sum_gu, got_gu, place, "grad_total_w_gate_up"),
         _sum4_f32(sum_down, got_down, place, "grad_total_w_down")], "join_ffn_halves")))
    grads, deltas, new_m, new_v = {}, {}, {}, {}

    def adamw_big(n):
        shp = wts[n].shape
        d, nm, nv = _adamw(wts[n][0], g_big[n], mom[n][0], var[n][0], "adamw_" + n)
        grads[n], deltas[n], new_m[n], new_v[n] = g_big[n].reshape(shp), d.reshape(shp), nm.reshape(shp), nv.reshape(shp)
        return d

    done = [adamw_big("w_gate_up"), adamw_big("w_down")]
    sum_in, sum_out, got_in, got_out = _split_wait(r_send, r_recv, r_bufs, done, _scatter_plan(2), "scatter_rest_wait")
    g_big.update(zip(["w_in", "w_out"], _join_halves(
        [_sum4_f32(sum_in, got_in, place, "grad_total_w_in"),
         _sum4_f32(sum_out, got_out, place, "grad_total_w_out")], "join_rest_halves")))
    adamw_big("w_in")
    adamw_big("w_out")

    reduced = _all_reduce_small(_pack_small([sg[n] for n in SMALL] + [sg["conv_w"], loss_row]))
    red = _unpack_small(reduced, [wts[n].shape for n in SMALL] + [(4, 4 * n_conv), (1, HEAD_DIM)])
    g_small = dict(zip(SMALL, red[:len(SMALL)]))
    g_conv_full, loss_out = red[len(SMALL)], red[len(SMALL) + 1]
    g_small["conv_w"] = lax.dynamic_slice_in_dim(g_conv_full, chip * n_conv, n_conv, axis=1).reshape(conv_w.shape)

    names = SMALL + ["conv_w"]
    shapes = [wts[n].shape for n in names]
    d, nm, nv = _adamw(_pack_small([wts[n] for n in names]), _pack_small([g_small[n] for n in names]),
                       _pack_small([mom[n] for n in names]), _pack_small([var[n] for n in names]), "adamw_small")
    for n, dd, mm, vv in zip(names, _unpack_small(d, shapes), _unpack_small(nm, shapes), _unpack_small(nv, shapes)):
        grads[n], deltas[n], new_m[n], new_v[n] = g_small[n], dd, mm, vv
    return (loss_out[0, 0], grad_x[None], *[grads[n] for n in ORDER], *[deltas[n] for n in ORDER],
            *[new_m[n] for n in ORDER], *[new_v[n] for n in ORDER])
```

```python
import jax
import jax.numpy as jnp
from jax import lax
from jax.experimental import pallas as pl
from jax.experimental.pallas import tpu as pltpu

F32 = jnp.float32
BF16 = jnp.bfloat16
HEAD_DIM = 128
CHUNK = 128
INV_BLOCK = 64
SPAN = 128
DILATIONS = (1, 4, 16)
ROPE_THETA = 10000.0
EPS = 1e-6
NEG = -1e30
ADAM_LR, ADAM_B1, ADAM_B2, ADAM_EPS, ADAM_WD, ADAM_STEP = 0.001, 0.9, 0.999, 1e-08, 0.01, 10
VMEM_LIMIT = 48 * 1024 * 1024
MESH = pl.DeviceIdType.MESH

_DN = {"nn": (((1,), (0,)), ((), ())), "nt": (((1,), (1,)), ((), ())), "tn": (((0,), (0,)), ((), ()))}


def _dot(a, b, mode="nn"):
    (ca, cb), _ = _DN[mode]
    if a.ndim == 3:
        dn = (((ca[0] + 1,), (cb[0] + 1,)), ((0,), (0,)))
    else:
        dn = _DN[mode]
    return lax.dot_general(a.astype(BF16), b.astype(BF16), dn, preferred_element_type=F32)


def _rsum(x):
    return jnp.sum(x, axis=-1, keepdims=True)


def _csum(x):
    return jnp.sum(x, axis=-2, keepdims=True)


def _tile(dim, pref, unit=128):
    t = (min(pref, dim) // unit) * unit
    while t >= unit:
        if dim % t == 0:
            return t
        t -= unit
    return dim


def _params(sem):
    return pltpu.CompilerParams(dimension_semantics=sem, vmem_limit_bytes=VMEM_LIMIT)


def _sigmoid(x):
    return 1.0 / (1.0 + jnp.exp(-x))


def _matmul(a, b, mode, name, add=None, out_dtype=F32, a_cols=None, b_cols=None, b_rows=None,
            tiles=(), finish=None, out_dtypes=(), row_sum=False, after=None, b2_cols=None):
    if mode == "tn":
        out_dtype = BF16
    a_off, a_w = a_cols if a_cols else (0, a.shape[1])
    b_off, b_w = b_cols if b_cols else (0, b.shape[1])
    br_off, br_n = b_rows if b_rows else (0, b.shape[0])
    if mode == "nn":
        m, k, n = a.shape[0], a_w, b_w
        assert br_n == k
    elif mode == "nt":
        m, k, n = a.shape[0], a_w, b.shape[0]
        assert b_w == k
    else:
        k, m, n = a.shape[0], a_w, b_w
        assert b.shape[0] == k
    tm, tn = _tile(m, 1024, 128), _tile(n, 1024, 128)
    tk = _tile(k, (4096 if k <= 4096 else 2048) if a.dtype == BF16 and b.dtype == BF16 else 1024, 128)
    if mode == "nn":
        assert a_off % tk == 0 and b_off % tn == 0 and br_off % tk == 0
        a_spec = pl.BlockSpec((tm, tk), lambda i, j, kk: (i, kk + a_off // tk))
        b_spec = pl.BlockSpec((tk, tn), lambda i, j, kk: (kk + br_off // tk, j + b_off // tn))
    elif mode == "nt":
        assert a_off % tk == 0 and b_off % tk == 0
        a_spec = pl.BlockSpec((tm, tk), lambda i, j, kk: (i, kk + a_off // tk))
        b_spec = pl.BlockSpec((tn, tk), lambda i, j, kk: (j, kk + b_off // tk))
    else:
        assert a_off % tm == 0 and b_off % tn == 0
        a_spec = pl.BlockSpec((tk, tm), lambda i, j, kk: (kk, i + a_off // tm))
        b_spec = pl.BlockSpec((tk, tn), lambda i, j, kk: (kk, j + b_off // tn))
    nk = k // tk
    if finish is None:
        out_dtypes = [out_dtype]
        if add is None:
            finish = lambda acc, vals: (acc,)
        else:
            tiles = [(add, 0)]
            finish = lambda acc, vals: (acc + vals[0].astype(F32),)
    n_tiles, n_out = len(tiles), len(out_dtypes)
    extra = [] if after is None else [after]
    n_b = 1 if b2_cols is None else 2
    first_out = 1 + n_b + n_tiles + len(extra)

    def body(*refs):
        a_ref, b_refs = refs[0], refs[1:1 + n_b]
        tile_refs, out_refs = refs[1 + n_b:1 + n_b + n_tiles], refs[first_out:first_out + n_out]
        acc_refs = refs[-n_b:] if nk > 1 else ()
        kk = pl.program_id(2)
        first_tile = (pl.program_id(0) == 0) & (pl.program_id(1) == 0)

        av = a_ref[...]
        if nk > 1:
            @pl.when(kk == 0)
            def _():
                for acc_ref in acc_refs:
                    acc_ref[...] = jnp.zeros_like(acc_ref)

            for acc_ref, b_ref in zip(acc_refs, b_refs):
                acc_ref[...] += _dot(av, b_ref[...], mode)

        @pl.when(kk == nk - 1)
        def _():
            if nk > 1:
                accs = [acc_ref[...] for acc_ref in acc_refs]
            else:
                accs = [_dot(av, b_ref[...], mode) for b_ref in b_refs]
            res = finish(accs[0] if n_b == 1 else accs, [t[...] for t in tile_refs])
            for o_ref, r in zip(out_refs, res):
                o_ref[...] = r.astype(o_ref.dtype)
            if row_sum:
                row_ref = refs[first_out + n_out]

                @pl.when(first_tile)
                def _():
                    row_ref[...] = res[n_out]

                @pl.when(jnp.logical_not(first_tile))
                def _():
                    row_ref[...] += res[n_out]

    in_specs = [a_spec, b_spec]
    args = [a, b]
    if b2_cols is not None:
        assert mode == "nn" and b2_cols[1] == n and b2_cols[0] % tn == 0
        in_specs.append(pl.BlockSpec((tk, tn), lambda i, j, kk: (kk + br_off // tk, j + b2_cols[0] // tn)))
        args.append(b)
    for arr, off in tiles:
        assert off % tn == 0
        in_specs.append(pl.BlockSpec((tm, tn), lambda i, j, kk, off=off: (i, j + off // tn)))
        args.append(arr)
    in_specs += [ANY] * len(extra)
    args += extra
    out_specs = [pl.BlockSpec((tm, tn), lambda i, j, kk: (i, j))] * n_out
    out_shape = [jax.ShapeDtypeStruct((m, n), dt) for dt in out_dtypes]
    if row_sum:
        out_specs.append(pl.BlockSpec((1, HEAD_DIM), lambda i, j, kk: (0, 0)))
        out_shape.append(jax.ShapeDtypeStruct((1, HEAD_DIM), F32))
    res = pl.pallas_call(
        body, name=name, grid=(m // tm, n // tn, nk),
        in_specs=in_specs, out_specs=out_specs, out_shape=out_shape,
        scratch_shapes=[pltpu.VMEM((tm, tn), F32)] * (n_b if nk > 1 else 0),
        compiler_params=_params(("arbitrary",) * 3 if row_sum else ("parallel", "parallel", "arbitrary")),
    )(*args)
    return res[0] if len(res) == 1 else res


def _rmsnorm_fwd(x, w, name, after=None):
    s, d = x.shape
    tr = _tile(s, 1024, 8)

    def body(x_ref, w_ref, *rest):
        h_ref = rest[-1]
        xv = x_ref[...]
        rstd = lax.rsqrt(jnp.mean(xv * xv, axis=-1, keepdims=True) + EPS)
        h_ref[...] = (xv * rstd * w_ref[...]).astype(BF16)

    extra = [] if after is None else [after]
    return pl.pallas_call(
        body, name=name, grid=(s // tr,),
        in_specs=[pl.BlockSpec((tr, d), lambda i: (i, 0)), pl.BlockSpec((1, d), lambda i: (0, 0))] + [ANY] * len(extra),
        out_specs=pl.BlockSpec((tr, d), lambda i: (i, 0)),
        out_shape=jax.ShapeDtypeStruct((s, d), BF16),
        compiler_params=_params(("parallel",)),
    )(x, w, *extra)


def _rmsnorm_bwd(dh, x, w, res, name, after=None):
    s, d = x.shape
    tr = _tile(s, 256, 8)

    def body(dh_ref, x_ref, w_ref, res_ref, *rest):
        dx_ref, dx16_ref, dw_ref = rest[-3:]
        xv = x_ref[...]
        rstd = lax.rsqrt(jnp.mean(xv * xv, axis=-1, keepdims=True) + EPS)
        xhat = xv * rstd
        dhv = dh_ref[...]
        gw = dhv * w_ref[...]
        dx = res_ref[...] + rstd * (gw - xhat * jnp.mean(gw * xhat, axis=-1, keepdims=True))
        dx_ref[...] = dx
        dx16_ref[...] = dx.astype(BF16)

        @pl.when(pl.program_id(0) == 0)
        def _():
            dw_ref[...] = jnp.zeros_like(dw_ref)

        dw_ref[...] += jnp.sum(dhv * xhat, axis=0, keepdims=True)

    row = pl.BlockSpec((tr, d), lambda i: (i, 0))
    vec = pl.BlockSpec((1, d), lambda i: (0, 0))
    extra = [] if after is None else [after]
    return pl.pallas_call(
        body, name=name, grid=(s // tr,),
        in_specs=[row, row, vec, row] + [ANY] * len(extra), out_specs=[row, row, vec],
        out_shape=[jax.ShapeDtypeStruct((s, d), F32), jax.ShapeDtypeStruct((s, d), BF16),
                   jax.ShapeDtypeStruct((1, d), F32)],
        compiler_params=_params(("arbitrary",)),
    )(dh, x, w, res, *extra)


def _conv_taps(x, w, rows):
    shifted = [x]
    for sft in (1, 2, 3):
        shifted.append(jnp.where(rows >= sft, pltpu.roll(x, sft, 0), 0.0))
    y = w[3:4, :] * shifted[0] + w[2:3, :] * shifted[1] + w[1:2, :] * shifted[2] + w[0:1, :] * shifted[3]
    return y, shifted


def _delta_pre_fwd(qkvz, conv_w, heads):
    s = qkvz.shape[0]
    nblk = 3 * heads

    def body(x_ref, w_ref, o_ref):
        part = pl.program_id(0) // heads
        rows = lax.broadcasted_iota(jnp.int32, (s, HEAD_DIM), 0)
        y, _ = _conv_taps(x_ref[...], w_ref[...], rows)
        a = y * _sigmoid(y)
        rs = lax.rsqrt(jnp.sum(a * a, axis=-1, keepdims=True) + EPS)
        fac = jnp.where(part == 0, rs * (HEAD_DIM ** -0.5), jnp.where(part == 1, rs, 1.0))
        o_ref[...] = a * fac

    return pl.pallas_call(
        body, name="delta_pre_fwd", grid=(nblk,),
        in_specs=[pl.BlockSpec((s, HEAD_DIM), lambda i: (0, i)), pl.BlockSpec((4, HEAD_DIM), lambda i: (0, i))],
        out_specs=pl.BlockSpec((s, HEAD_DIM), lambda i: (0, i)),
        out_shape=jax.ShapeDtypeStruct((s, 3 * heads * HEAD_DIM), F32),
        compiler_params=_params(("parallel",)),
    )(qkvz, conv_w)


def _delta_pre_bwd(dqkv, qkvz, conv_w, heads):
    s = qkvz.shape[0]
    nblk = 3 * heads

    def body(d_ref, x_ref, w_ref, dx_ref, dw_ref):
        part = pl.program_id(0) // heads
        rows = lax.broadcasted_iota(jnp.int32, (s, HEAD_DIM), 0)
        w = w_ref[...]
        y, shifted = _conv_taps(x_ref[...], w, rows)
        sg = _sigmoid(y)
        a = y * sg
        rs = lax.rsqrt(jnp.sum(a * a, axis=-1, keepdims=True) + EPS)
        unit = a * rs
        dn = d_ref[...]
        scale = jnp.where(part == 0, HEAD_DIM ** -0.5, 1.0)
        da_norm = scale * rs * (dn - unit * jnp.sum(dn * unit, axis=-1, keepdims=True))
        da = jnp.where(part < 2, da_norm, dn)
        dy = da * sg * (1.0 + y * (1.0 - sg))
        dx = w[3:4, :] * dy
        for sft in (1, 2, 3):
            dx = dx + w[3 - sft:4 - sft, :] * jnp.where(rows < s - sft, pltpu.roll(dy, s - sft, 0), 0.0)
        dx_ref[...] = dx.astype(BF16)
        for sft in range(4):
            dw_ref[3 - sft:4 - sft, :] = jnp.sum(dy * shifted[sft], axis=0, keepdims=True)

    col = pl.BlockSpec((s, HEAD_DIM), lambda i: (0, i))
    wsp = pl.BlockSpec((4, HEAD_DIM), lambda i: (0, i))
    return pl.pallas_call(
        body, name="delta_pre_bwd", grid=(nblk,),
        in_specs=[col, col, wsp], out_specs=[col, wsp],
        out_shape=[jax.ShapeDtypeStruct((s, 3 * heads * HEAD_DIM), BF16),
                   jax.ShapeDtypeStruct((4, 3 * heads * HEAD_DIM), F32)],
        compiler_params=_params(("parallel",)),
    )(dqkv, qkvz, conv_w)


def _heads_of(ref, heads):
    return jnp.stack([ref[:, h * HEAD_DIM:(h + 1) * HEAD_DIM] for h in range(heads)])


def _chunk_common(q, k, v, bd, a_log, dt_bias, heads, solved=None):
    c = CHUNK
    braw = jnp.stack([bd[:, h:h + 1] for h in range(heads)])
    draw = jnp.stack([bd[:, heads + h:heads + h + 1] for h in range(heads)])
    beta = _sigmoid(braw)
    xd = draw + dt_bias
    sp = jnp.maximum(xd, 0.0) + jnp.log1p(jnp.exp(-jnp.abs(xd)))
    g = -jnp.exp(a_log) * sp
    row = lax.broadcasted_iota(jnp.int32, (c, c), 0)
    col = lax.broadcasted_iota(jnp.int32, (c, c), 1)
    sq = (heads, c, c)
    g_b = jnp.broadcast_to(g, sq)
    g_row = _csum(jnp.where(row == col, g_b, 0.0))
    gam_col = _rsum(jnp.where(col <= row, jnp.broadcast_to(g_row, sq), 0.0))
    gam_row = _csum(jnp.where(row <= col, g_b, 0.0))
    causal = row >= col
    dm = jnp.where(causal, jnp.exp(jnp.where(causal, gam_col - gam_row, 0.0)), 0.0)
    kk = _dot(k, k, "nt")
    e = jnp.exp(gam_col)
    if solved is None:
        low = jnp.where(row > col, beta * kk * dm, 0.0)
        assert c in (INV_BLOCK, 2 * INV_BLOCK)
        same = (row // INV_BLOCK) == (col // INV_BLOCK)
        diag = jnp.where(same, low, 0.0)
        t = jnp.where(row == col, 1.0, 0.0) - diag
        pw = diag
        for _ in range((INV_BLOCK - 1).bit_length() - 1):
            pw = _dot(pw, pw)
            t = t + _dot(t, pw)
        if c > INV_BLOCK:
            t = t - _dot(_dot(t, low - diag), t)
        u = _dot(t, beta * v)
        w = _dot(t, (beta * e) * k)
    else:
        t, u, w = solved
    qk_raw = _dot(q, k, "nt")
    gl = _csum(g)
    el = jnp.exp(gl - gam_col)
    return dict(beta=beta, xd=xd, g=g, row=row, col=col, dm=dm, kk=kk, t=t, e=e, u=u, w=w,
                qk_raw=qk_raw, qk=qk_raw * dm, gl=gl, el=el, qd=e * q, kd=el * k, cd=jnp.exp(gl))


def _delta_chunk_fwd(qkv, bd, a_log, dt_bias, heads):
    s = qkv.shape[0]
    n = s // CHUNK
    dw = heads * HEAD_DIM
    blk = lambda part: pl.BlockSpec((CHUNK, dw), lambda i: (i, part))

    def body(q_ref, k_ref, v_ref, bd_ref, al_ref, dt_ref, o_ref, st_ref, t_ref, uw_ref, state):
        @pl.when(pl.program_id(0) == 0)
        def _():
            state[...] = jnp.zeros_like(state)

        cm = _chunk_common(_heads_of(q_ref, heads), _heads_of(k_ref, heads), _heads_of(v_ref, heads), bd_ref[...],
                           al_ref[...], dt_ref[...], heads)
        st = state[...]
        st_ref[0] = st
        t_ref[0] = cm["t"]
        uw_ref[0, 0] = cm["u"]
        uw_ref[0, 1] = cm["w"]
        vn = cm["u"] - _dot(cm["w"], st)
        o = _dot(cm["qd"], st) + _dot(cm["qk"], vn)
        for h in range(heads):
            o_ref[:, h * HEAD_DIM:(h + 1) * HEAD_DIM] = o[h]
        state[...] = cm["cd"] * st + _dot(cm["kd"], vn, "tn")

    smem = pl.BlockSpec((heads, 1, 1), lambda i: (0, 0, 0))
    return pl.pallas_call(
        body, name="delta_chunk_fwd", grid=(n,),
        in_specs=[blk(0), blk(1), blk(2), pl.BlockSpec((CHUNK, HEAD_DIM), lambda i: (i, 0)), smem, smem],
        out_specs=[pl.BlockSpec((CHUNK, dw), lambda i: (i, 0)),
                   pl.BlockSpec((1, heads, HEAD_DIM, HEAD_DIM), lambda i: (i, 0, 0, 0)),
                   pl.BlockSpec((1, heads, CHUNK, CHUNK), lambda i: (i, 0, 0, 0)),
                   pl.BlockSpec((1, 2, heads, CHUNK, HEAD_DIM), lambda i: (i, 0, 0, 0, 0))],
        out_shape=[jax.ShapeDtypeStruct((s, dw), F32),
                   jax.ShapeDtypeStruct((n, heads, HEAD_DIM, HEAD_DIM), F32),
                   jax.ShapeDtypeStruct((n, heads, CHUNK, CHUNK), F32),
                   jax.ShapeDtypeStruct((n, 2, heads, CHUNK, HEAD_DIM), F32)],
        scratch_shapes=[pltpu.VMEM((heads, HEAD_DIM, HEAD_DIM), F32)],
        compiler_params=_params(("arbitrary",)),
    )(qkv, qkv, qkv, bd, a_log, dt_bias)


def _delta_chunk_bwd(do, qkv, bd, saved, a_log, dt_bias, heads):
    s = qkv.shape[0]
    n = s // CHUNK
    dw = heads * HEAD_DIM
    c = CHUNK
    blk = lambda part: pl.BlockSpec((CHUNK, dw), lambda i: (n - 1 - i, part))

    def all_heads(q, k, v, dov, st, solved, dsn, bd, a_log, dt_bias):
        cm = _chunk_common(q, k, v, bd, a_log, dt_bias, heads, solved)
        beta, e, dm, row, col = cm["beta"], cm["e"], cm["dm"], cm["row"], cm["col"]
        sq = (heads, c, c)
        vn = cm["u"] - _dot(cm["w"], st)
        dvn = _dot(cm["kd"], dsn)
        dkd = _dot(vn, dsn, "nt")
        dcd = _csum(_rsum(st * dsn))
        ds = cm["cd"] * dsn
        dqd = _dot(dov, st, "nt")
        ds = ds + _dot(cm["qd"], dov, "tn")
        dqk = _dot(dov, vn, "nt")
        dvn = dvn + _dot(cm["qk"], dov, "tn")
        dw_ = -_dot(dvn, st, "nt")
        ds = ds - _dot(cm["w"], dvn, "tn")
        drhs_u = _dot(cm["t"], dvn, "tn")
        drhs_w = _dot(cm["t"], dw_, "tn")
        da = -(_dot(drhs_u, cm["u"], "nt") + _dot(drhs_w, cm["w"], "nt"))
        dl = jnp.where(row > col, da, 0.0)
        dbeta = _rsum(dl * cm["kk"] * dm)
        dkk = dl * beta * dm
        dd = dl * beta * cm["kk"]
        dv = beta * drhs_u
        ek = e * k
        dbeta = dbeta + _rsum(drhs_u * v) + _rsum(drhs_w * ek)
        dk = (beta * e) * drhs_w
        dgam = _rsum(drhs_w * (beta * ek))
        dqkm = dqk * dm
        dq = _dot(dqkm, k)
        dk = dk + _dot(dqkm, q, "tn")
        dd = dd + dqk * cm["qk_raw"]
        dk = dk + _dot(dkk, k) + _dot(dkk, k, "tn")
        dq = dq + e * dqd
        dgam = dgam + _rsum(dqd * cm["qd"])
        dk = dk + cm["el"] * dkd
        r = _rsum(dkd * cm["kd"])
        dgam = dgam - r
        dgl = _csum(r) + dcd * cm["cd"]
        mm = dd * dm
        colsum_c = _rsum(jnp.where(row == col, jnp.broadcast_to(_csum(mm), sq), 0.0))
        dgam = dgam + _rsum(mm) - colsum_c
        ridx = lax.broadcasted_iota(jnp.int32, (c, 1), 0)
        dgam = dgam + jnp.where(ridx == c - 1, dgl, 0.0)
        dgam_row = _csum(jnp.where(row == col, jnp.broadcast_to(dgam, sq), 0.0))
        dg = _rsum(jnp.where(col >= row, jnp.broadcast_to(dgam_row, sq), 0.0))
        d_xd = dg * (-jnp.exp(a_log)) * _sigmoid(cm["xd"])
        d_braw = dbeta * beta * (1.0 - beta)
        d_alog = dg * cm["g"]
        lane = lax.broadcasted_iota(jnp.int32, (c, HEAD_DIM), 1)
        dbd = jnp.zeros((c, HEAD_DIM), F32)
        for h in range(heads):
            dbd = (dbd + jnp.where(lane == h, d_braw[h], 0.0) + jnp.where(lane == h + heads, d_xd[h], 0.0)
                   + jnp.where(lane == h + 2 * heads, d_alog[h], 0.0))
        return dq, dk, dv, ds, dbd

    def body(do_ref, q_ref, k_ref, v_ref, bd_ref, st_ref, t_ref, uw_ref, al_ref, dt_ref, dqkv_ref, dbd_ref, dstate):
        @pl.when(pl.program_id(0) == 0)
        def _():
            dstate[...] = jnp.zeros_like(dstate)

        dq, dk, dv, ds, dbd = all_heads(_heads_of(q_ref, heads), _heads_of(k_ref, heads), _heads_of(v_ref, heads),
                                        _heads_of(do_ref, heads), st_ref[0], (t_ref[0], uw_ref[0, 0], uw_ref[0, 1]),
                                        dstate[...], bd_ref[...],
                                        al_ref[...], dt_ref[...])
        for part, val in enumerate((dq, dk, dv)):
            for h in range(heads):
                lo = part * dw + h * HEAD_DIM
                dqkv_ref[:, lo:lo + HEAD_DIM] = val[h]
        dstate[...] = ds
        dbd_ref[...] = dbd

    smem = pl.BlockSpec((heads, 1, 1), lambda i: (0, 0, 0))
    shared = pl.BlockSpec((CHUNK, HEAD_DIM), lambda i: (n - 1 - i, 0))
    wide = pl.BlockSpec((CHUNK, dw), lambda i: (n - 1 - i, 0))
    return pl.pallas_call(
        body, name="delta_chunk_bwd", grid=(n,),
        in_specs=[wide, blk(0), blk(1), blk(2), shared,
                  pl.BlockSpec((1, heads, HEAD_DIM, HEAD_DIM), lambda i: (n - 1 - i, 0, 0, 0)),
                  pl.BlockSpec((1, heads, CHUNK, CHUNK), lambda i: (n - 1 - i, 0, 0, 0)),
                  pl.BlockSpec((1, 2, heads, CHUNK, HEAD_DIM), lambda i: (n - 1 - i, 0, 0, 0, 0)), smem, smem],
        out_specs=[pl.BlockSpec((CHUNK, 3 * dw), lambda i: (n - 1 - i, 0)), shared],
        out_shape=[jax.ShapeDtypeStruct((s, 3 * dw), F32), jax.ShapeDtypeStruct((s, HEAD_DIM), F32)],
        scratch_shapes=[pltpu.VMEM((heads, HEAD_DIM, HEAD_DIM), F32)],
        compiler_params=_params(("arbitrary",)),
    )(do, qkv, qkv, qkv, bd, *saved, a_log, dt_bias)


def _delta_post_fwd(o, qkvz, w, heads):
    s = o.shape[0]
    tr = _tile(s, 1024, 8)

    def body(o_ref, z_ref, w_ref, out_ref):
        ov, z = o_ref[...], z_ref[...]
        rstd = lax.rsqrt(jnp.mean(ov * ov, axis=-1, keepdims=True) + EPS)
        out_ref[...] = (ov * rstd * w_ref[...] * (z * _sigmoid(z))).astype(BF16)

    return pl.pallas_call(
        body, name="delta_post_fwd", grid=(s // tr, heads),
        in_specs=[pl.BlockSpec((tr, HEAD_DIM), lambda i, h: (i, h)),
                  pl.BlockSpec((tr, HEAD_DIM), lambda i, h: (i, 3 * heads + h)),
                  pl.BlockSpec((1, HEAD_DIM), lambda i, h: (0, 0))],
        out_specs=pl.BlockSpec((tr, HEAD_DIM), lambda i, h: (i, h)),
        out_shape=jax.ShapeDtypeStruct((s, heads * HEAD_DIM), BF16),
        compiler_params=_params(("parallel", "parallel")),
    )(o, qkvz, w)


def _delta_post_bwd(dmix, o, qkvz, w, heads):
    s = o.shape[0]
    tr = _tile(s, 1024, 8)

    def body(d_ref, o_ref, z_ref, w_ref, do_ref, dz_ref, dw_ref):
        d, ov, z, wv = d_ref[...], o_ref[...], z_ref[...], w_ref[...]
        sg = _sigmoid(z)
        rstd = lax.rsqrt(jnp.mean(ov * ov, axis=-1, keepdims=True) + EPS)
        ohat = ov * rstd
        dz_ref[...] = (d * (ohat * wv) * sg * (1.0 + z * (1.0 - sg))).astype(BF16)
        dn = d * (z * sg)
        gw = dn * wv
        do_ref[...] = rstd * (gw - ohat * jnp.mean(gw * ohat, axis=-1, keepdims=True))

        @pl.when((pl.program_id(0) == 0) & (pl.program_id(1) == 0))
        def _():
            dw_ref[...] = jnp.zeros_like(dw_ref)

        dw_ref[...] += jnp.sum(dn * ohat, axis=0, keepdims=True)

    head = pl.BlockSpec((tr, HEAD_DIM), lambda i, h: (i, h))
    vec = pl.BlockSpec((1, HEAD_DIM), lambda i, h: (0, 0))
    dw = heads * HEAD_DIM
    return pl.pallas_call(
        body, name="delta_post_bwd", grid=(s // tr, heads),
        in_specs=[head, head, pl.BlockSpec((tr, HEAD_DIM), lambda i, h: (i, 3 * heads + h)), vec],
        out_specs=[head, head, vec],
        out_shape=[jax.ShapeDtypeStruct((s, dw), F32), jax.ShapeDtypeStruct((s, dw), BF16),
                   jax.ShapeDtypeStruct((1, HEAD_DIM), F32)],
        compiler_params=_params(("arbitrary", "arbitrary")),
    )(dmix, o, qkvz, w)


def _rope_tables(positions, s):
    half = HEAD_DIM // 2
    inv_freq = ROPE_THETA ** (-jnp.arange(half, dtype=F32) / half)
    ang = positions.reshape(s, 1).astype(F32) * inv_freq
    cos, sin = jnp.cos(ang), jnp.sin(ang)
    return jnp.concatenate([cos, cos], axis=-1), jnp.concatenate([-sin, sin], axis=-1)


def _attn_pre_fwd(aqkv, wq, wk, cosf, sinf, heads):
    s = aqkv.shape[0]
    tr = _tile(s, 1024, 8)

    def body(x_ref, wq_ref, wk_ref, c_ref, s_ref, o_ref):
        xv = x_ref[...]
        wv = jnp.where(pl.program_id(1) < heads, wq_ref[...], wk_ref[...])
        y = xv * lax.rsqrt(jnp.mean(xv * xv, axis=-1, keepdims=True) + EPS) * wv
        o_ref[...] = y * c_ref[...] + pltpu.roll(y, HEAD_DIM // 2, 1) * s_ref[...]

    blk = pl.BlockSpec((tr, HEAD_DIM), lambda i, j: (i, j))
    vec = pl.BlockSpec((1, HEAD_DIM), lambda i, j: (0, 0))
    tab = pl.BlockSpec((tr, HEAD_DIM), lambda i, j: (i, 0))
    return pl.pallas_call(
        body, name="attn_pre_fwd", grid=(s // tr, 2 * heads),
        in_specs=[blk, vec, vec, tab, tab], out_specs=blk,
        out_shape=jax.ShapeDtypeStruct((s, 2 * heads * HEAD_DIM), F32),
        compiler_params=_params(("parallel", "parallel")),
    )(aqkv, wq, wk, cosf, sinf)


def _band():
    qi = lax.broadcasted_iota(jnp.int32, (SPAN, 2 * SPAN), 0)
    ki = lax.broadcasted_iota(jnp.int32, (SPAN, 2 * SPAN), 1)
    dist = qi + SPAN - ki
    return (dist >= 0) & (dist <= SPAN), ki >= SPAN


def _sub(g, r, d):
    if d == 1:
        return pl.ds(g * SPAN, SPAN)
    return pl.ds(g * SPAN * d + r, SPAN, stride=d)


def _attn_blocks(s):
    assert s % (SPAN * max(DILATIONS)) == 0
    return [(p_i, d, r, g) for p_i, d in enumerate(DILATIONS) for r in range(d) for g in range(s // (SPAN * d))]


def _attn_fwd(qk, aqkv, w, heads):
    s = qk.shape[0]
    aw = heads * HEAD_DIM

    def body(q_ref, k_ref, v_ref, w_ref, mix_ref, acc_ref, m_ref, l_ref):
        band, own = _band()
        for p_i, d, r, g in _attn_blocks(s):
            rows = _sub(g, r, d)
            kc, vc = k_ref[rows, :], v_ref[rows, :]
            if g == 0:
                kp, vp, mask = kc, vc, band & own
            else:
                mask = band
            kcat = jnp.concatenate([kp, kc], axis=0)
            vcat = jnp.concatenate([vp, vc], axis=0)
            kp, vp = kc, vc
            sc = _dot(q_ref[rows, :] * (HEAD_DIM ** -0.5), kcat, "nt")
            sc = jnp.where(mask, sc, NEG)
            m = jnp.max(sc, axis=-1, keepdims=True)
            if p_i == 0:
                p = jnp.exp(sc - m)
                acc_ref[rows, :] = _dot(p, vcat)
                l_new = jnp.sum(p, axis=-1, keepdims=True)
            else:
                m_old = m_ref[rows, 0:1]
                m = jnp.maximum(m, m_old)
                alpha = jnp.exp(m_old - m)
                p = jnp.exp(sc - m)
                acc_ref[rows, :] = alpha * acc_ref[rows, :] + _dot(p, vcat)
                l_new = alpha * l_ref[rows, 0:1] + jnp.sum(p, axis=-1, keepdims=True)
            m_ref[rows, :] = jnp.broadcast_to(m, (SPAN, HEAD_DIM))
            l_ref[rows, :] = jnp.broadcast_to(l_new, (SPAN, HEAD_DIM))
        den = l_ref[...]
        ob = acc_ref[...] / den
        acc_ref[...] = ob
        m_ref[...] = m_ref[...] + jnp.log(den)
        rstd = lax.rsqrt(jnp.mean(ob * ob, axis=-1, keepdims=True) + EPS)
        mix_ref[...] = (ob * rstd * w_ref[...]).astype(BF16)

    col = lambda off: pl.BlockSpec((s, HEAD_DIM), lambda h: (0, off + h))
    return pl.pallas_call(
        body, name="attn_fwd", grid=(heads,),
        in_specs=[col(0), col(heads), col(2 * heads), pl.BlockSpec((1, HEAD_DIM), lambda h: (0, 0))],
        out_specs=[col(0), col(0), col(0)],
        out_shape=[jax.ShapeDtypeStruct((s, aw), BF16), jax.ShapeDtypeStruct((s, aw), F32),
                   jax.ShapeDtypeStruct((s, aw), F32)],
        scratch_shapes=[pltpu.VMEM((s, HEAD_DIM), F32)],
        compiler_params=_params(("parallel",)),
    )(qk, qk, aqkv, w)


def _attn_merge_bwd(dmix, ob, w, heads):
    s = ob.shape[0]
    tr = _tile(s, 1024, 8)

    def body(d_ref, ob_ref, w_ref, do_ref, dsum_ref, dw_ref):
        d, ov = d_ref[...], ob_ref[...]
        rstd = lax.rsqrt(jnp.mean(ov * ov, axis=-1, keepdims=True) + EPS)
        ohat = ov * rstd
        gw = d * w_ref[...]
        dov = rstd * (gw - ohat * jnp.mean(gw * ohat, axis=-1, keepdims=True))
        do_ref[...] = dov
        dsum_ref[...] = jnp.broadcast_to(jnp.sum(dov * ov, axis=-1, keepdims=True), dov.shape)

        @pl.when((pl.program_id(0) == 0) & (pl.program_id(1) == 0))
        def _():
            dw_ref[...] = jnp.zeros_like(dw_ref)

        dw_ref[...] += jnp.sum(d * ohat, axis=0, keepdims=True)

    blk = pl.BlockSpec((tr, HEAD_DIM), lambda i, h: (i, h))
    vec = pl.BlockSpec((1, HEAD_DIM), lambda i, h: (0, 0))
    aw = heads * HEAD_DIM
    return pl.pallas_call(
        body, name="attn_merge_bwd", grid=(s // tr, heads),
        in_specs=[pl.BlockSpec((tr, HEAD_DIM), lambda i, h: (i, heads + h)), blk, vec],
        out_specs=[blk, blk, vec],
        out_shape=[jax.ShapeDtypeStruct((s, aw), F32), jax.ShapeDtypeStruct((s, aw), F32),
                   jax.ShapeDtypeStruct((1, HEAD_DIM), F32)],
        compiler_params=_params(("arbitrary", "arbitrary")),
    )(dmix, ob, w)


def _attn_bwd(qk, aqkv, do, lse, dsum, heads):
    s = qk.shape[0]
    aw = heads * HEAD_DIM
    scale = HEAD_DIM ** -0.5

    def body(q_ref, k_ref, v_ref, do_ref, l_ref, ds_ref, out_ref):
        band, own = _band()
        dq_ref, dk_ref, dv_ref = out_ref.at[0], out_ref.at[1], out_ref.at[2]
        out_ref[...] = jnp.zeros((3, s, HEAD_DIM), F32)
        for _, d, r, g in _attn_blocks(s):
            rows = _sub(g, r, d)
            qs, dov = q_ref[rows, :] * scale, do_ref[rows, :]
            kc, vc = k_ref[rows, :], v_ref[rows, :]
            if g == 0:
                kp, vp, mask = kc, vc, band & own
            else:
                mask = band
            kcat = jnp.concatenate([kp, kc], axis=0)
            vcat = jnp.concatenate([vp, vc], axis=0)
            p = jnp.where(mask, jnp.exp(_dot(qs, kcat, "nt") - l_ref[rows, 0:1]), 0.0)
            dsc = p * (_dot(dov, vcat, "nt") - ds_ref[rows, 0:1])
            dq_ref[rows, :] += scale * _dot(dsc, kcat)
            dk = _dot(dsc, qs, "tn")
            dv = _dot(p, dov, "tn")
            if g > 0:
                dk_ref[prows, :] += dk_own + dk[:SPAN]
                dv_ref[prows, :] += dv_own + dv[:SPAN]
            dk_own, dv_own = dk[SPAN:], dv[SPAN:]
            if g == s // (SPAN * d) - 1:
                dk_ref[rows, :] += dk_own
                dv_ref[rows, :] += dv_own
            kp, vp, prows = kc, vc, rows

    col = lambda off: pl.BlockSpec((s, HEAD_DIM), lambda h: (0, off + h))
    return pl.pallas_call(
        body, name="attn_bwd", grid=(heads,),
        in_specs=[col(0), col(heads), col(2 * heads), col(0), col(0), col(0)],
        out_specs=pl.BlockSpec((3, s, HEAD_DIM), lambda h: (0, 0, h)),
        out_shape=jax.ShapeDtypeStruct((3, s, aw), F32),
        compiler_params=_params(("parallel",)),
    )(qk, qk, aqkv, do, lse, dsum)


def _attn_pre_bwd(grads, aqkv, wq, wk, cosf, sinf, heads):
    s = aqkv.shape[0]
    tr = _tile(s, 1024, 8)
    nrow = s // tr

    def body(g_ref, x_ref, wq_ref, wk_ref, c_ref, s_ref, dx_ref, dwq_ref, dwk_ref):
        i, j = pl.program_id(0), pl.program_id(1)
        kind = j // heads
        dout = g_ref[0]
        tot_v = dout
        dy = dout * c_ref[...] + pltpu.roll(dout * s_ref[...], HEAD_DIM // 2, 1)
        xv = x_ref[...]
        wv = jnp.where(kind == 0, wq_ref[...], wk_ref[...])
        rstd = lax.rsqrt(jnp.mean(xv * xv, axis=-1, keepdims=True) + EPS)
        xhat = xv * rstd
        gw = dy * wv
        dxn = rstd * (gw - xhat * jnp.mean(gw * xhat, axis=-1, keepdims=True))
        dx_ref[...] = jnp.where(kind == 2, tot_v, dxn).astype(BF16)
        dwc = jnp.sum(dy * xhat, axis=0, keepdims=True)

        @pl.when((i == 0) & (j == 0))
        def _():
            dwq_ref[...] = jnp.zeros_like(dwq_ref)
            dwk_ref[...] = jnp.zeros_like(dwk_ref)

        @pl.when(kind == 0)
        def _():
            dwq_ref[...] += dwc

        @pl.when(kind == 1)
        def _():
            dwk_ref[...] += dwc

    grad = pl.BlockSpec((1, tr, HEAD_DIM), lambda i, j: (j // heads, i, j % heads))
    blk = pl.BlockSpec((tr, HEAD_DIM), lambda i, j: (i, j))
    vec = pl.BlockSpec((1, HEAD_DIM), lambda i, j: (0, 0))
    tab = pl.BlockSpec((tr, HEAD_DIM), lambda i, j: (i, 0))
    return pl.pallas_call(
        body, name="attn_pre_bwd", grid=(nrow, 3 * heads),
        in_specs=[grad, blk, vec, vec, tab, tab], out_specs=[blk, vec, vec],
        out_shape=[jax.ShapeDtypeStruct((s, 3 * heads * HEAD_DIM), BF16),
                   jax.ShapeDtypeStruct((1, HEAD_DIM), F32), jax.ShapeDtypeStruct((1, HEAD_DIM), F32)],
        compiler_params=_params(("arbitrary", "arbitrary")),
    )(grads, aqkv, wq, wk, cosf, sinf)


def _colsum(a, name):
    s, d = a.shape
    tr = _tile(s, 1024, 8)

    def body(a_ref, o_ref):
        @pl.when(pl.program_id(0) == 0)
        def _():
            o_ref[...] = jnp.zeros_like(o_ref)

        o_ref[...] += jnp.sum(a_ref[...], axis=0, keepdims=True)

    return pl.pallas_call(
        body, name=name, grid=(s // tr,),
        in_specs=[pl.BlockSpec((tr, d), lambda i: (i, 0))], out_specs=pl.BlockSpec((1, d), lambda i: (0, 0)),
        out_shape=jax.ShapeDtypeStruct((1, d), F32),
        compiler_params=_params(("arbitrary",)),
    )(a)


def _local_step(x, positions, target, small, w_qkvz, w_bd, w_attn, conv_w, later_weights, ffn_grads_ready,
                rest_grads_ready, after=None):
    s, dmod = x.shape
    heads = dmod // (2 * HEAD_DIM)
    dw = heads * HEAD_DIM
    a_log, dt_bias = small["a_log"].reshape(heads, 1, 1), small["dt_bias"].reshape(heads, 1, 1)
    cosf, sinf = _rope_tables(positions, s)

    h1 = _rmsnorm_fwd(x, small["attn_norm_w"], "norm1_fwd", after=after)
    qkvz = _matmul(h1, w_qkvz, "nn", "proj_qkvz")
    bd = _matmul(h1, w_bd, "nn", "proj_bd")
    aqkv = _matmul(h1, w_attn, "nn", "proj_attn")
    dqkv = _delta_pre_fwd(qkvz, conv_w, heads)
    o_d, *saved = _delta_chunk_fwd(dqkv, bd, a_log, dt_bias, heads)
    mix_a = _delta_post_fwd(o_d, qkvz, small["delta_out_norm_w"], heads)
    qk_rot = _attn_pre_fwd(aqkv, small["q_norm_w"], small["k_norm_w"], cosf, sinf, heads)
    mix_b, ob, lse = _attn_fwd(qk_rot, aqkv, small["attn_out_norm_w"], heads)
    w_out, behind, ffn_weights = later_weights((mix_a, mix_b))
    x1 = _matmul(mix_a, w_out, "nn", "out_proj_a", add=x, b_rows=(0, dw), after=behind)
    x1 = _matmul(mix_b, w_out, "nn", "out_proj_b", add=x1, b_rows=(dw, dw))
    h2 = _rmsnorm_fwd(x1, small["ffn_norm_w"], "norm2_fwd")
    w_gu, w_down = ffn_weights(h2)
    ff = w_down.shape[0]

    def swiglu(accs, vals):
        g, u = accs
        return g, u, g * _sigmoid(g) * u

    gate, up, act = _matmul(h2, w_gu, "nn", "ffn_gate_up", b_cols=(0, ff), b2_cols=(ff, ff), finish=swiglu,
                            out_dtypes=[BF16, BF16, BF16])

    def loss_head(acc, vals):
        err = acc + vals[0] - vals[1]
        part = 0.5 * jnp.sum(jnp.sum(err * err, axis=-1, keepdims=True) * (1.0 / dmod), axis=0, keepdims=True)
        lane = lax.broadcasted_iota(jnp.int32, (1, HEAD_DIM), 1)
        return err * (1.0 / dmod), err * (1.0 / dmod), jnp.where(lane == 0, part, 0.0)

    dy, dy16, loss_row = _matmul(act, w_down, "nn", "ffn_down", tiles=[(x1, 0), (target, 0)], finish=loss_head,
                                 out_dtypes=[F32, BF16], row_sum=True)

    def swiglu_bwd(acc, vals):
        g, u = vals[0].astype(F32), vals[1].astype(F32)
        sg = _sigmoid(g)
        return acc * u * sg * (1.0 + g * (1.0 - sg)), acc * g * sg

    dgate, dup = _matmul(dy16, w_down, "nt", "ffn_down_dx", tiles=[(gate, 0), (up, 0)], finish=swiglu_bwd,
                         out_dtypes=[BF16, BF16])
    g_w_down = _matmul(act, dy16, "tn", "ffn_down_dw")
    g_w_gate = _matmul(h2, dgate, "tn", "ffn_gate_dw")
    g_w_up = _matmul(h2, dup, "tn", "ffn_up_dw")
    behind, and_then = ffn_grads_ready(g_w_gate, g_w_up, g_w_down)
    dh2 = _matmul(dgate, w_gu, "nt", "ffn_gate_dx", b_cols=(0, ff), after=behind)
    dh2 = _matmul(dup, w_gu, "nt", "ffn_up_dx", b_cols=(ff, ff), add=dh2)
    behind = and_then(dh2)
    dx1, dx1_16, g_ffn_norm = _rmsnorm_bwd(dh2, x1, small["ffn_norm_w"], dy, "norm2_bwd", after=behind)
    dmix = _matmul(dx1_16, w_out, "nt", "out_proj_dx")
    g_w_out_a = _matmul(mix_a, dx1_16, "tn", "out_proj_dw_a")
    g_w_out_b = _matmul(mix_b, dx1_16, "tn", "out_proj_dw_b")
    dob, dsum, g_attn_out_norm = _attn_merge_bwd(dmix, ob, small["attn_out_norm_w"], heads)
    grads = _attn_bwd(qk_rot, aqkv, dob, lse, dsum, heads)
    d_aqkv, g_q_norm, g_k_norm = _attn_pre_bwd(grads, aqkv, small["q_norm_w"], small["k_norm_w"], cosf, sinf, heads)
    do_d, dz, g_delta_out_norm = _delta_post_bwd(dmix, o_d, qkvz, small["delta_out_norm_w"], heads)
    ddqkv, dbd = _delta_chunk_bwd(do_d, dqkv, bd, saved, a_log, dt_bias, heads)
    d_qkv_raw, g_conv = _delta_pre_bwd(ddqkv, qkvz, conv_w, heads)
    bd_sums = _colsum(dbd, "bd_colsum")
    g_w_qkv = _matmul(h1, d_qkv_raw, "tn", "proj_qkv_dw")
    g_w_z = _matmul(h1, dz, "tn", "proj_z_dw")
    g_w_bd = _matmul(h1, dbd, "tn", "proj_bd_dw")
    g_w_attn = _matmul(h1, d_aqkv, "tn", "proj_attn_dw")
    behind, and_then = rest_grads_ready(dict(w_qkv=g_w_qkv, w_z=g_w_z, w_bd=g_w_bd, w_attn=g_w_attn,
                                             w_out_a=g_w_out_a, w_out_b=g_w_out_b))
    dh1 = _matmul(d_qkv_raw, w_qkvz, "nt", "proj_qkv_dx", b_cols=(0, 3 * dw), after=behind)
    dh1 = _matmul(dz, w_qkvz, "nt", "proj_z_dx", b_cols=(3 * dw, dw), add=dh1, after=and_then(dh1))
    dh1 = _matmul(d_aqkv, w_attn, "nt", "proj_attn_dx", add=dh1)
    dh1 = _matmul(dbd, w_bd, "nt", "proj_bd_dx", add=dh1)
    grad_x, _, g_attn_norm = _rmsnorm_bwd(dh1, x, small["attn_norm_w"], dx1, "norm1_bwd")
    small_grads = dict(
        attn_norm_w=g_attn_norm, a_log=bd_sums[:, 2 * heads:3 * heads], dt_bias=bd_sums[:, heads:2 * heads],
        delta_out_norm_w=g_delta_out_norm, q_norm_w=g_q_norm, k_norm_w=g_k_norm,
        attn_out_norm_w=g_attn_out_norm, ffn_norm_w=g_ffn_norm, conv_w=g_conv)
    return loss_row, grad_x, small_grads


def _adamw(w, g, m, v, name):
    r, c = w.shape
    tr = _tile(r, 256, 8)

    def body(w_ref, g_ref, m_ref, v_ref, d_ref, nm_ref, nv_ref):
        gv = g_ref[...]
        nm = ADAM_B1 * m_ref[...] + (1.0 - ADAM_B1) * gv
        nv = ADAM_B2 * v_ref[...] + (1.0 - ADAM_B2) * (gv * gv)
        m_hat = nm / (1.0 - ADAM_B1 ** ADAM_STEP)
        v_hat = nv / (1.0 - ADAM_B2 ** ADAM_STEP)
        d_ref[...] = -ADAM_LR * (m_hat / (jnp.sqrt(v_hat) + ADAM_EPS) + ADAM_WD * w_ref[...])
        nm_ref[...] = nm
        nv_ref[...] = nv

    blk = pl.BlockSpec((tr, c), lambda i: (i, 0))
    return pl.pallas_call(
        body, name=name, grid=(r // tr,),
        in_specs=[blk] * 4, out_specs=[blk] * 3,
        out_shape=[jax.ShapeDtypeStruct((r, c), F32)] * 3,
        compiler_params=_params(("parallel",)),
    )(w, g, m, v)


def _add_half_bf16(g, b, place, name):
    n, half, c = b.shape
    tr = _tile(half, 512, 16)
    nb = half // tr

    def body(place_ref, g_ref, b_ref, o_ref):
        o_ref[...] = (g_ref[...].astype(F32) + b_ref[...].astype(F32)).astype(BF16)

    blk = pl.BlockSpec((1, tr, c), lambda i, j, p: (i, j, 0))
    return pl.pallas_call(
        body, name=name,
        grid_spec=pltpu.PrefetchScalarGridSpec(
            num_scalar_prefetch=1, grid=(n, nb),
            in_specs=[pl.BlockSpec((1, tr, c), lambda i, j, p: (i, p[0] * nb + j, 0)), blk], out_specs=blk),
        out_shape=jax.ShapeDtypeStruct((n, half, c), BF16),
        compiler_params=_params(("parallel", "parallel")),
    )(place, g, b)


def _sum4_f32(mine, others, place, name):
    _, half, c = mine.shape
    tr = _tile(half, 512, 16)
    nb = half // tr

    def body(place_ref, a_ref, b_ref, o_ref):
        acc = a_ref[0].astype(F32)
        for j in range(3):
            acc = acc + b_ref[j].astype(F32)
        o_ref[...] = acc

    return pl.pallas_call(
        body, name=name,
        grid_spec=pltpu.PrefetchScalarGridSpec(
            num_scalar_prefetch=1, grid=(nb,),
            in_specs=[pl.BlockSpec((1, tr, c), lambda i, p: (p[1], i, 0)),
                      pl.BlockSpec((3, tr, c), lambda i, p: (0, i, 0))],
            out_specs=pl.BlockSpec((tr, c), lambda i, p: (p[0] * nb + i, 0))),
        out_shape=jax.ShapeDtypeStruct((2 * half, c), F32),
        compiler_params=_params(("parallel",)),
    )(place, mine, others)


def _place():
    x, y, c = lax.axis_index("x"), lax.axis_index("y"), lax.axis_index("c")
    other_chips = [(1 - x, y), (x, 1 - y), (1 - x, 1 - y)]
    return x, y, c, (x, y, 1 - c), other_chips


ANY = pl.BlockSpec(memory_space=pl.ANY)


def _remote(k, src, dst, to, send_sems, recv_sems):
    return pltpu.make_async_remote_copy(src_ref=src, dst_ref=dst, send_sem=send_sems.at[k], recv_sem=recv_sems.at[k],
                                        device_id=to, device_id_type=MESH)


def _half(ref, lead, hc):
    if lead is None:
        half = ref.shape[0] // 2
        return ref.at[pl.ds(hc * half, half), :]
    half = ref.shape[1] // 2
    return ref.at[lead, pl.ds(hc * half, half), :]


def _all_gather_weights(slots, whole, name):
    nt, nw = len(slots), len(whole)
    base_w, base_f, base_d = 2 * nt, 2 * nt + 3 * nw, 4 * nt + 3 * nw

    def quarter(ref, lead, hc, q):
        quart = ref.shape[1] // 4
        return ref.at[lead, pl.ds((2 * hc + q) * quart, quart), :]

    def body(*refs):
        ins, outs = refs[:nt + nw], refs[nt + nw:2 * (nt + nw)]
        sems = refs[2 * (nt + nw):]
        x, y, c, sibling, chips = _place()
        me, xn, yn, dg = 2 * x + y, 2 * (1 - x) + y, 2 * x + 1 - y, 2 * (1 - x) + 1 - y
        to_x, to_y = (1 - x, y, c), (x, 1 - y, c)
        cps = []
        for t in range(nt):
            cps.append(_remote(2 * t, _half(ins[t], me, c), _half(outs[t], me, c), to_x, *sems))
            cps.append(_remote(2 * t + 1, _half(ins[t], me, c), _half(outs[t], me, c), to_y, *sems))
        for j, (px, py) in enumerate(chips):
            for t in range(nw):
                cps.append(_remote(base_w + j * nw + t, ins[nt + t].at[me], outs[nt + t].at[me], (px, py, c), *sems))
        for cp in cps:
            cp.start()

        def start(k, ref, to):
            cp = _remote(k, ref, ref, to, *sems)
            cp.start()
            cps.append(cp)

        for t in range(nt):
            landed = _half(outs[t], xn, c)
            _remote(2 * t, landed, landed, to_x, *sems).wait_recv()
            start(base_f + 2 * t, quarter(outs[t], xn, c, 0), to_y)
            start(base_d + 3 * t, landed, sibling)
            landed = _half(outs[t], yn, c)
            _remote(2 * t + 1, landed, landed, to_y, *sems).wait_recv()
            start(base_f + 2 * t + 1, quarter(outs[t], yn, c, 1), to_x)
            start(base_d + 3 * t + 1, landed, sibling)
        for t in range(nt):
            q0, q1 = quarter(outs[t], dg, c, 0), quarter(outs[t], dg, c, 1)
            _remote(base_f + 2 * t, q0, q0, to_y, *sems).wait_recv()
            _remote(base_f + 2 * t + 1, q1, q1, to_x, *sems).wait_recv()
            start(base_d + 3 * t + 2, _half(outs[t], dg, c), sibling)
        for j, (px, py) in enumerate(chips):
            for t in range(nw):
                landed = outs[nt + t].at[2 * px + py]
                _remote(base_w + j * nw + t, landed, landed, (px, py, c), *sems).wait_recv()
        for t in range(nt):
            for j, chip in enumerate((xn, yn, dg)):
                other = _half(outs[t], chip, 1 - c)
                _remote(base_d + 3 * t + j, other, other, sibling, *sems).wait_recv()
        for cp in cps:
            cp.wait_send()

    arrays = list(slots) + list(whole)
    n_sem = 7 * nt + 3 * nw
    return pl.pallas_call(
        body, name=name, in_specs=[ANY] * len(arrays), out_specs=[ANY] * len(arrays),
        input_output_aliases={i: i for i in range(len(arrays))},
        out_shape=[jax.ShapeDtypeStruct(a.shape, a.dtype) for a in arrays],
        scratch_shapes=[pltpu.SemaphoreType.DMA((n_sem,)), pltpu.SemaphoreType.DMA((n_sem,))],
    )(*arrays)


def _join_halves(fs, name):
    nt = len(fs)

    def body(*refs):
        in_refs, out_refs, (send_sems, recv_sems) = refs[:nt], refs[nt:2 * nt], refs[2 * nt:]
        _, _, c, sibling, _ = _place()
        cps = [_remote(t, _half(in_refs[t], None, c), _half(out_refs[t], None, c), sibling, send_sems, recv_sems)
               for t in range(nt)]
        for cp in cps:
            cp.start()
        for t in range(nt):
            theirs = _half(out_refs[t], None, 1 - c)
            _remote(t, theirs, theirs, sibling, send_sems, recv_sems).wait_recv()
        for cp in cps:
            cp.wait_send()

    return pl.pallas_call(
        body, name=name, in_specs=[ANY] * nt, out_specs=[ANY] * nt,
        input_output_aliases={i: i for i in range(nt)},
        out_shape=[jax.ShapeDtypeStruct(f.shape, f.dtype) for f in fs],
        scratch_shapes=[pltpu.SemaphoreType.DMA((nt,)), pltpu.SemaphoreType.DMA((nt,))],
    )(*fs)


HBM = pl.BlockSpec(memory_space=pltpu.HBM)
SEM = pl.BlockSpec(memory_space=pltpu.SEMAPHORE)
EFFECT = pltpu.SideEffectType.DATAFLOW_SIDE_EFFECTING


def _split_start(arrays, after, plan, n_copies, name):
    na = len(arrays)

    def body(*refs):
        ins, send_sems, recv_sems = refs[:na], refs[na + 1], refs[na + 2]
        outs, token = refs[na + 3:2 * na + 3], refs[2 * na + 3]
        for k, (src, dst, to) in enumerate(plan(ins, outs)):
            _remote(k, src, dst, to, send_sems, recv_sems).start()
        token[...] = jnp.zeros_like(token)

    res = pl.pallas_call(
        body, name=name,
        out_shape=(pltpu.SemaphoreType.DMA((n_copies,)), pltpu.SemaphoreType.DMA((n_copies,)),
                   *[pltpu.HBM(a.shape, a.dtype) for a in arrays], jax.ShapeDtypeStruct((8, HEAD_DIM), F32)),
        in_specs=[HBM] * na + [ANY],
        out_specs=(SEM, SEM, *[HBM] * na, pl.BlockSpec(memory_space=pltpu.VMEM)),
        input_output_aliases={i: 2 + i for i in range(na)},
        compiler_params=pltpu.CompilerParams(has_side_effects=EFFECT),
    )(*[pltpu.with_memory_space_constraint(a, pltpu.HBM) for a in arrays], after)
    return res[0], res[1], list(res[2:2 + na]), res[2 + na]


def _split_wait(send_sems, recv_sems, arrays, after, plan, name):
    na = len(arrays)
    after = list(after) if isinstance(after, (list, tuple)) else [after]

    def body(*refs):
        ins, send, recv, outs = refs[:na], refs[na], refs[na + 1], refs[na + 2 + len(after):]
        for k, (src, dst, to) in enumerate(plan(ins, outs)):
            cp = _remote(k, src, dst, to, send, recv)
            cp.wait_send()
            cp.wait_recv()

    res = pl.pallas_call(
        body, name=name, out_shape=tuple(pltpu.HBM(a.shape, a.dtype) for a in arrays),
        in_specs=[HBM] * na + [SEM, SEM] + [ANY] * len(after), out_specs=tuple([HBM] * na),
        input_output_aliases={i: i for i in range(na)},
        compiler_params=pltpu.CompilerParams(has_side_effects=EFFECT),
    )(*arrays, send_sems, recv_sems, *after)
    return list(res)


def _gather_plan(nt):
    def plan(ins, outs):
        x, y, c, _, chips = _place()
        me = 2 * x + y
        return [(_half(ins[t], me, c), _half(outs[t], me, c), (px, py, c)) for px, py in chips for t in range(nt)]
    return plan


def _gather_landed_plan(nt):
    def plan(ins, outs):
        _, _, c, _, chips = _place()
        return [(_half(outs[t], 2 * px + py, c), _half(outs[t], 2 * px + py, c), (px, py, c))
                for px, py in chips for t in range(nt)]
    return plan


def _scatter_plan(nt):
    def plan(ins, outs):
        _, _, c, _, chips = _place()
        return [(ins[t].at[2 * px + py], outs[nt + t].at[j], (px, py, c))
                for j, (px, py) in enumerate(chips) for t in range(nt)]
    return plan


def _pass_plan(nt):
    def plan(ins, outs):
        _, _, c, sibling, chips = _place()
        return [(_half(ins[t], 2 * px + py, c), _half(outs[t], 2 * px + py, c), sibling)
                for px, py in chips for t in range(nt)]
    return plan


def _pass_landed_plan(nt):
    def plan(ins, outs):
        _, _, c, sibling, chips = _place()
        return [(_half(outs[t], 2 * px + py, c), _half(outs[t], 2 * px + py, 1 - c), sibling)
                for px, py in chips for t in range(nt)]
    return plan


def _swap_plan(nt):
    def plan(ins, outs):
        _, _, c, sibling, _ = _place()
        res = []
        for t in range(nt):
            half = ins[t].shape[1] // 2
            res.append((ins[t].at[:, pl.ds((1 - c) * half, half), :], outs[nt + t], sibling))
        return res
    return plan


def _pass_halves_to_sibling(slots, name):
    nt = len(slots)

    def body(*refs):
        ins, outs, (send_sems, recv_sems) = refs[:nt], refs[nt:2 * nt], refs[2 * nt:]
        _, _, c, sibling, chips = _place()
        cps = [_remote(j * nt + t, _half(ins[t], 2 * px + py, c), _half(outs[t], 2 * px + py, c), sibling,
                       send_sems, recv_sems)
               for j, (px, py) in enumerate(chips) for t in range(nt)]
        for cp in cps:
            cp.start()
        for j, (px, py) in enumerate(chips):
            for t in range(nt):
                other = _half(outs[t], 2 * px + py, 1 - c)
                _remote(j * nt + t, other, other, sibling, send_sems, recv_sems).wait_recv()
        for cp in cps:
            cp.wait_send()

    return pl.pallas_call(
        body, name=name, in_specs=[ANY] * nt, out_specs=[ANY] * nt,
        input_output_aliases={i: i for i in range(nt)},
        out_shape=[jax.ShapeDtypeStruct(a.shape, a.dtype) for a in slots],
        scratch_shapes=[pltpu.SemaphoreType.DMA((3 * nt,)), pltpu.SemaphoreType.DMA((3 * nt,))],
    )(*slots)


def _all_reduce_small(v):
    r, lanes = v.shape

    def body(v_ref, out_ref, buf, send_sems, recv_sems):
        x, y, c, sibling, chips = _place()

        def slot(px, py, pc):
            return buf.at[4 * px + 2 * py + pc]

        def copy(k, block, to, src=None):
            return pltpu.make_async_remote_copy(src_ref=slot(*block) if src is None else src, dst_ref=slot(*block),
                                                send_sem=send_sems.at[k], recv_sem=recv_sems.at[k],
                                                device_id=to, device_id_type=MESH)

        me = (x, y, c)
        buf[4 * x + 2 * y + c] = v_ref[...]
        first = [copy(0, me, sibling, src=v_ref)]
        first += [copy(1 + j, me, (*chip, c), src=v_ref) for j, chip in enumerate(chips)]
        for cp in first:
            cp.start()
        passed = [copy(4 + j, (*chip, c), sibling) for j, chip in enumerate(chips)]
        for j, chip in enumerate(chips):
            copy(1 + j, (*chip, c), me).wait_recv()
            passed[j].start()
        copy(0, (x, y, 1 - c), me).wait_recv()
        for j, chip in enumerate(chips):
            copy(4 + j, (*chip, 1 - c), me).wait_recv()
        for cp in first + passed:
            cp.wait_send()
        acc = buf[0]
        for k in range(1, 8):
            acc = acc + buf[k]
        out_ref[...] = acc

    vmem = pl.BlockSpec(memory_space=pltpu.VMEM)
    return pl.pallas_call(
        body, name="all_reduce_small", in_specs=[vmem], out_specs=vmem,
        out_shape=jax.ShapeDtypeStruct((r, lanes), F32),
        scratch_shapes=[pltpu.VMEM((8, r, lanes), F32), pltpu.SemaphoreType.DMA((7,)), pltpu.SemaphoreType.DMA((7,))],
    )(v)


def _size(shape):
    n = 1
    for d in shape:
        n *= d
    return n


def _pack_small(parts):
    rows = []
    for p in parts:
        f = p.reshape(-1).astype(F32)
        n = -(-f.shape[0] // HEAD_DIM) * HEAD_DIM
        rows.append(jnp.pad(f, (0, n - f.shape[0])).reshape(-1, HEAD_DIM))
    a = jnp.concatenate(rows, axis=0)
    return jnp.pad(a, ((0, -a.shape[0] % 8), (0, 0)))


def _unpack_small(a, shapes):
    out, row = [], 0
    for shp in shapes:
        nrows = -(-_size(shp) // HEAD_DIM)
        out.append(a[row:row + nrows].reshape(-1)[:_size(shp)].reshape(shp))
        row += nrows
    return out


SMALL = ["attn_norm_w", "a_log", "dt_bias", "delta_out_norm_w", "q_norm_w", "k_norm_w", "attn_out_norm_w", "ffn_norm_w"]
BIG = ["w_in", "w_out", "w_gate_up", "w_down"]
ORDER = ["attn_norm_w", "w_in", "conv_w", "a_log", "dt_bias", "delta_out_norm_w", "q_norm_w", "k_norm_w",
         "attn_out_norm_w", "w_out", "ffn_norm_w", "w_gate_up", "w_down"]


def kernel(x, positions, attn_norm_w, w_in, conv_w, a_log, dt_bias, delta_out_norm_w, q_norm_w, k_norm_w, attn_out_norm_w, w_out, ffn_norm_w, w_gate_up, w_down, loss_target, m_attn_norm_w, m_w_in, m_conv_w, m_a_log, m_dt_bias, m_delta_out_norm_w, m_q_norm_w, m_k_norm_w, m_attn_out_norm_w, m_w_out, m_ffn_norm_w, m_w_gate_up, m_w_down, v_attn_norm_w, v_w_in, v_conv_w, v_a_log, v_dt_bias, v_delta_out_norm_w, v_q_norm_w, v_k_norm_w, v_attn_out_norm_w, v_w_out, v_ffn_norm_w, v_w_gate_up, v_w_down):
    wts = dict(attn_norm_w=attn_norm_w, w_in=w_in, conv_w=conv_w, a_log=a_log, dt_bias=dt_bias,
               delta_out_norm_w=delta_out_norm_w, q_norm_w=q_norm_w, k_norm_w=k_norm_w,
               attn_out_norm_w=attn_out_norm_w, w_out=w_out, ffn_norm_w=ffn_norm_w, w_gate_up=w_gate_up, w_down=w_down)
    mom = dict(attn_norm_w=m_attn_norm_w, w_in=m_w_in, conv_w=m_conv_w, a_log=m_a_log, dt_bias=m_dt_bias,
               delta_out_norm_w=m_delta_out_norm_w, q_norm_w=m_q_norm_w, k_norm_w=m_k_norm_w,
               attn_out_norm_w=m_attn_out_norm_w, w_out=m_w_out, ffn_norm_w=m_ffn_norm_w, w_gate_up=m_w_gate_up,
               w_down=m_w_down)
    var = dict(attn_norm_w=v_attn_norm_w, w_in=v_w_in, conv_w=v_conv_w, a_log=v_a_log, dt_bias=v_dt_bias,
               delta_out_norm_w=v_delta_out_norm_w, q_norm_w=v_q_norm_w, k_norm_w=v_k_norm_w,
               attn_out_norm_w=v_attn_out_norm_w, w_out=v_w_out, ffn_norm_w=v_ffn_norm_w, w_gate_up=v_w_gate_up,
               w_down=v_w_down)
    dmod = x.shape[2]
    heads = dmod // (2 * HEAD_DIM)
    dw = heads * HEAD_DIM
    chip = 2 * lax.axis_index("x") + lax.axis_index("y")
    core = lax.axis_index("c")
    n_in, n_out, n_gu, n_down, n_conv = (w_in.shape[2], w_out.shape[1], w_gate_up.shape[2], w_down.shape[1],
                                         conv_w.shape[2])

    def slots_of(w, dtype):
        shard = w[0].astype(dtype)
        return lax.dynamic_update_index_in_dim(lax.empty((4,) + shard.shape, dtype), shard, chip, axis=0)

    s_in, s_conv = _all_gather_weights([slots_of(w_in, BF16)], [slots_of(conv_w, F32)], "all_gather_w_in")
    later = [slots_of(w_out, BF16), slots_of(w_gate_up, BF16), slots_of(w_down, BF16)]
    w_send, w_recv, later, started = _split_start(later, s_conv, _gather_plan(3), 9, "gather_rest_start")
    by_cols = lambda a: a.transpose(1, 0, 2).reshape(a.shape[1], 4 * a.shape[2])
    w_in_f, conv_f = by_cols(s_in), by_cols(s_conv)
    w_bd = jnp.pad(w_in_f[:, 4 * dw:4 * dw + 2 * heads], ((0, 0), (0, HEAD_DIM - 2 * heads)))
    small = {n: wts[n] for n in SMALL}
    place = jnp.stack([core, chip]).astype(jnp.int32)
    to_slots = lambda a: a.reshape(a.shape[0], 4, a.shape[1] // 4).transpose(1, 0, 2)

    def later_weights(after):
        landed = _split_wait(w_send, w_recv, later, after, _gather_landed_plan(3), "gather_rest_wait")
        s_out, = _pass_halves_to_sibling(landed[:1], "gather_out_pass")
        p_send, p_recv, passing, token = _split_start(landed[1:], s_out, _pass_plan(2), 6, "gather_ffn_pass_start")

        def ffn_weights(after):
            s_gu, s_down = _split_wait(p_send, p_recv, passing, after, _pass_landed_plan(2), "gather_ffn_pass_wait")
            return by_cols(s_gu), s_down.reshape(4 * n_down, dmod)

        return s_out.reshape(4 * n_out, dmod), token, ffn_weights

    ffn = {}

    def ffn_grads_ready(g_gate, g_up, g_down):
        gs = [to_slots(jnp.concatenate([g_gate, g_up], axis=1)), g_down.reshape(4, n_down, dmod)]
        zones = [lax.empty((4, g.shape[1] // 2, g.shape[2]), BF16) for g in gs]
        s_send, s_recv, s_bufs, swapping = _split_start(gs + zones, g_gate, _swap_plan(2), 2, "swap_ffn_start")

        def and_then(after):
            g_gu, g_dn, b_gu, b_dn = _split_wait(s_send, s_recv, s_bufs, after, _swap_plan(2), "swap_ffn_wait")
            sums = [_add_half_bf16(g_gu, b_gu, place, "chip_partial_sum_w_gate_up"),
                    _add_half_bf16(g_dn, b_dn, place, "chip_partial_sum_w_down")]
            zones3 = [lax.empty((3,) + p.shape[1:], BF16) for p in sums]
            ffn["send"], ffn["recv"], ffn["bufs"], token = _split_start(sums + zones3, b_gu, _scatter_plan(2), 6,
                                                                        "scatter_ffn_start")
            return token

        return swapping, and_then

    rest = {}

    def rest_grads_ready(bg):
        gs = [to_slots(jnp.concatenate([bg["w_qkv"], bg["w_z"], bg["w_bd"][:, :2 * heads], bg["w_attn"]], axis=1)),
              jnp.concatenate([bg["w_out_a"], bg["w_out_b"]], axis=0).reshape(4, n_out, dmod)]
        zones = [lax.empty((4, g.shape[1] // 2, g.shape[2]), BF16) for g in gs]
        s_send, s_recv, s_bufs, swapping = _split_start(gs + zones, bg["w_attn"], _swap_plan(2), 2, "swap_rest_start")

        def and_then(after):
            g_in, g_out, b_in, b_out = _split_wait(s_send, s_recv, s_bufs, after, _swap_plan(2), "swap_rest_wait")
            sums = [_add_half_bf16(g_in, b_in, place, "chip_partial_sum_w_in"),
                    _add_half_bf16(g_out, b_out, place, "chip_partial_sum_w_out")]
            zones3 = [lax.empty((3,) + p.shape[1:], BF16) for p in sums]
            rest["send"], rest["recv"], rest["bufs"], token = _split_start(sums + zones3, b_in, _scatter_plan(2), 6,
                                                                           "scatter_rest_start")
            return token

        return swapping, and_then

    loss_row, grad_x, sg = _local_step(
        x[0], positions[0], loss_target[0], small, w_in_f[:, :4 * dw], w_bd, w_in_f[:, 4 * dw + 2 * heads:], conv_f,
        later_weights, ffn_grads_ready, rest_grads_ready, after=started)

    r_send, r_recv, r_bufs = rest["send"], rest["recv"], rest["bufs"]
    sum_gu, sum_down, got_gu, got_down = _split_wait(ffn["send"], ffn["recv"], ffn["bufs"], grad_x,
                                                     _scatter_plan(2), "scatter_ffn_wait")
    g_big = dict(zip(["w_gate_up", "w_down"], _join_halves(
        [_sum4_f32(sum_gu, got_gu, place, "grad_total_w_gate_up"),
         _sum4_f32(sum_down, got_down, place, "grad_total_w_down")], "join_ffn_halves")))
    grads, deltas, new_m, new_v = {}, {}, {}, {}

    def adamw_big(n):
        shp = wts[n].shape
        d, nm, nv = _adamw(wts[n][0], g_big[n], mom[n][0], var[n][0], "adamw_" + n)
        grads[n], deltas[n], new_m[n], new_v[n] = g_big[n].reshape(shp), d.reshape(shp), nm.reshape(shp), nv.reshape(shp)
        return d

    done = [adamw_big("w_gate_up"), adamw_big("w_down")]
    sum_in, sum_out, got_in, got_out = _split_wait(r_send, r_recv, r_bufs, done, _scatter_plan(2), "scatter_rest_wait")
    g_big.update(zip(["w_in", "w_out"], _join_halves(
        [_sum4_f32(sum_in, got_in, place, "grad_total_w_in"),
         _sum4_f32(sum_out, got_out, place, "grad_total_w_out")], "join_rest_halves")))
    adamw_big("w_in")
    adamw_big("w_out")

    reduced = _all_reduce_small(_pack_small([sg[n] for n in SMALL] + [sg["conv_w"], loss_row]))
    red = _unpack_small(reduced, [wts[n].shape for n in SMALL] + [(4, 4 * n_conv), (1, HEAD_DIM)])
    g_small = dict(zip(SMALL, red[:len(SMALL)]))
    g_conv_full, loss_out = red[len(SMALL)], red[len(SMALL) + 1]
    g_small["conv_w"] = lax.dynamic_slice_in_dim(g_conv_full, chip * n_conv, n_conv, axis=1).reshape(conv_w.shape)

    names = SMALL + ["conv_w"]
    shapes = [wts[n].shape for n in names]
    d, nm, nv = _adamw(_pack_small([wts[n] for n in names]), _pack_small([g_small[n] for n in names]),
                       _pack_small([mom[n] for n in names]), _pack_small([var[n] for n in names]), "adamw_small")
    for n, dd, mm, vv in zip(names, _unpack_small(d, shapes), _unpack_small(nm, shapes), _unpack_small(nv, shapes)):
        grads[n], deltas[n], new_m[n], new_v[n] = g_small[n], dd, mm, vv
    return (loss_out[0, 0], grad_x[None], *[grads[n] for n in ORDER], *[deltas[n] for n in ORDER],
            *[new_m[n] for n in ORDER], *[new_v[n] for n in ORDER])
```

```python
import jax
import jax.numpy as jnp
from jax import lax
from jax.experimental import pallas as pl
from jax.experimental.pallas import tpu as pltpu

F32 = jnp.float32
BF16 = jnp.bfloat16
HEAD_DIM = 128
CHUNK = 128
INV_BLOCK = 64
SPAN = 128
DILATIONS = (1, 4, 16)
ROPE_THETA = 10000.0
EPS = 1e-6
NEG = -1e30
ADAM_LR, ADAM_B1, ADAM_B2, ADAM_EPS, ADAM_WD, ADAM_STEP = 0.001, 0.9, 0.999, 1e-08, 0.01, 10
VMEM_LIMIT = 48 * 1024 * 1024
MATMUL_VMEM = 44 * 1024 * 1024
MESH = pl.DeviceIdType.MESH

_DN = {"nn": (((1,), (0,)), ((), ())), "nt": (((1,), (1,)), ((), ())), "tn": (((0,), (0,)), ((), ()))}


def _dot(a, b, mode="nn"):
    (ca, cb), _ = _DN[mode]
    if a.ndim == 3:
        dn = (((ca[0] + 1,), (cb[0] + 1,)), ((0,), (0,)))
    else:
        dn = _DN[mode]
    return lax.dot_general(a.astype(BF16), b.astype(BF16), dn, preferred_element_type=F32)


def _rsum(x):
    return jnp.sum(x, axis=-1, keepdims=True)


def _csum(x):
    return jnp.sum(x, axis=-2, keepdims=True)


def _tile(dim, pref, unit=128):
    t = (min(pref, dim) // unit) * unit
    while t >= unit:
        if dim % t == 0:
            return t
        t -= unit
    return dim


def _params(sem):
    return pltpu.CompilerParams(dimension_semantics=sem, vmem_limit_bytes=VMEM_LIMIT)


def _sigmoid(x):
    return 1.0 / (1.0 + jnp.exp(-x))


def _matmul(a, b, mode, name, add=None, out_dtype=F32, a_cols=None, b_cols=None, b_rows=None,
            tiles=(), finish=None, out_dtypes=(), row_sum=False, after=None, b2_cols=None):
    if mode == "tn":
        out_dtype = BF16
    a_off, a_w = a_cols if a_cols else (0, a.shape[1])
    b_off, b_w = b_cols if b_cols else (0, b.shape[1])
    br_off, br_n = b_rows if b_rows else (0, b.shape[0])
    if mode == "nn":
        m, k, n = a.shape[0], a_w, b_w
        assert br_n == k
    elif mode == "nt":
        m, k, n = a.shape[0], a_w, b.shape[0]
        assert b_w == k
    else:
        k, m, n = a.shape[0], a_w, b_w
        assert b.shape[0] == k
    if finish is None:
        out_dtypes = [out_dtype]
        if add is None:
            finish = lambda acc, vals: (acc,)
        else:
            tiles = [(add, 0)]
            finish = lambda acc, vals: (acc + vals[0].astype(F32),)
    n_b = 1 if b2_cols is None else 2
    tm, tn = _tile(m, 1024, 128), _tile(n, 1024, 128)
    sides = sum(jnp.dtype(t.dtype).itemsize for t, _ in tiles) + sum(jnp.dtype(d).itemsize for d in out_dtypes)

    def vmem(tk_):
        ops = tm * tk_ * jnp.dtype(a.dtype).itemsize + n_b * tk_ * tn * jnp.dtype(b.dtype).itemsize
        return 2 * (ops + tm * tn * sides) + (n_b * tm * tn * 4 if tk_ < k else 0)

    tk = next(t for t in [_tile(k, p, 128) for p in (4096, 2048, 1024, 512, 128)] if vmem(t) <= MATMUL_VMEM)
    if mode == "nn":
        assert a_off % tk == 0 and b_off % tn == 0 and br_off % tk == 0
        a_spec = pl.BlockSpec((tm, tk), lambda i, j, kk: (i, kk + a_off // tk))
        b_spec = pl.BlockSpec((tk, tn), lambda i, j, kk: (kk + br_off // tk, j + b_off // tn))
    elif mode == "nt":
        assert a_off % tk == 0 and b_off % tk == 0
        a_spec = pl.BlockSpec((tm, tk), lambda i, j, kk: (i, kk + a_off // tk))
        b_spec = pl.BlockSpec((tn, tk), lambda i, j, kk: (j, kk + b_off // tk))
    else:
        assert a_off % tm == 0 and b_off % tn == 0
        a_spec = pl.BlockSpec((tk, tm), lambda i, j, kk: (kk, i + a_off // tm))
        b_spec = pl.BlockSpec((tk, tn), lambda i, j, kk: (kk, j + b_off // tn))
    nk = k // tk
    n_tiles, n_out = len(tiles), len(out_dtypes)
    extra = [] if after is None else [after]
    first_out = 1 + n_b + n_tiles + len(extra)

    def body(*refs):
        a_ref, b_refs = refs[0], refs[1:1 + n_b]
        tile_refs, out_refs = refs[1 + n_b:1 + n_b + n_tiles], refs[first_out:first_out + n_out]
        acc_refs = refs[-n_b:] if nk > 1 else ()
        kk = pl.program_id(2)
        first_tile = (pl.program_id(0) == 0) & (pl.program_id(1) == 0)

        av = a_ref[...]
        if nk > 1:
            @pl.when(kk == 0)
            def _():
                for acc_ref in acc_refs:
                    acc_ref[...] = jnp.zeros_like(acc_ref)

            for acc_ref, b_ref in zip(acc_refs, b_refs):
                acc_ref[...] += _dot(av, b_ref[...], mode)

        @pl.when(kk == nk - 1)
        def _():
            if nk > 1:
                accs = [acc_ref[...] for acc_ref in acc_refs]
            else:
                accs = [_dot(av, b_ref[...], mode) for b_ref in b_refs]
            res = finish(accs[0] if n_b == 1 else accs, [t[...] for t in tile_refs])
            for o_ref, r in zip(out_refs, res):
                o_ref[...] = r.astype(o_ref.dtype)
            if row_sum:
                row_ref = refs[first_out + n_out]

                @pl.when(first_tile)
                def _():
                    row_ref[...] = res[n_out]

                @pl.when(jnp.logical_not(first_tile))
                def _():
                    row_ref[...] += res[n_out]

    in_specs = [a_spec, b_spec]
    args = [a, b]
    if b2_cols is not None:
        assert mode == "nn" and b2_cols[1] == n and b2_cols[0] % tn == 0
        in_specs.append(pl.BlockSpec((tk, tn), lambda i, j, kk: (kk + br_off // tk, j + b2_cols[0] // tn)))
        args.append(b)
    for arr, off in tiles:
        assert off % tn == 0
        in_specs.append(pl.BlockSpec((tm, tn), lambda i, j, kk, off=off: (i, j + off // tn)))
        args.append(arr)
    in_specs += [ANY] * len(extra)
    args += extra
    out_specs = [pl.BlockSpec((tm, tn), lambda i, j, kk: (i, j))] * n_out
    out_shape = [jax.ShapeDtypeStruct((m, n), dt) for dt in out_dtypes]
    if row_sum:
        out_specs.append(pl.BlockSpec((1, HEAD_DIM), lambda i, j, kk: (0, 0)))
        out_shape.append(jax.ShapeDtypeStruct((1, HEAD_DIM), F32))
    res = pl.pallas_call(
        body, name=name, grid=(m // tm, n // tn, nk),
        in_specs=in_specs, out_specs=out_specs, out_shape=out_shape,
        scratch_shapes=[pltpu.VMEM((tm, tn), F32)] * (n_b if nk > 1 else 0),
        compiler_params=_params(("arbitrary",) * 3 if row_sum else ("parallel", "parallel", "arbitrary")),
    )(*args)
    return res[0] if len(res) == 1 else res


def _rmsnorm_fwd(x, w, name, after=None):
    s, d = x.shape
    tr = _tile(s, 1024, 8)

    def body(x_ref, w_ref, *rest):
        h_ref = rest[-1]
        xv = x_ref[...]
        rstd = lax.rsqrt(jnp.mean(xv * xv, axis=-1, keepdims=True) + EPS)
        h_ref[...] = (xv * rstd * w_ref[...]).astype(BF16)

    extra = [] if after is None else [after]
    return pl.pallas_call(
        body, name=name, grid=(s // tr,),
        in_specs=[pl.BlockSpec((tr, d), lambda i: (i, 0)), pl.BlockSpec((1, d), lambda i: (0, 0))] + [ANY] * len(extra),
        out_specs=pl.BlockSpec((tr, d), lambda i: (i, 0)),
        out_shape=jax.ShapeDtypeStruct((s, d), BF16),
        compiler_params=_params(("parallel",)),
    )(x, w, *extra)


def _rmsnorm_bwd(dh, x, w, res, name, after=None):
    s, d = x.shape
    tr = _tile(s, 256, 8)

    def body(dh_ref, x_ref, w_ref, res_ref, *rest):
        dx_ref, dx16_ref, dw_ref = rest[-3:]
        xv = x_ref[...]
        rstd = lax.rsqrt(jnp.mean(xv * xv, axis=-1, keepdims=True) + EPS)
        xhat = xv * rstd
        dhv = dh_ref[...]
        gw = dhv * w_ref[...]
        dx = res_ref[...] + rstd * (gw - xhat * jnp.mean(gw * xhat, axis=-1, keepdims=True))
        dx_ref[...] = dx
        dx16_ref[...] = dx.astype(BF16)

        @pl.when(pl.program_id(0) == 0)
        def _():
            dw_ref[...] = jnp.zeros_like(dw_ref)

        dw_ref[...] += jnp.sum(dhv * xhat, axis=0, keepdims=True)

    row = pl.BlockSpec((tr, d), lambda i: (i, 0))
    vec = pl.BlockSpec((1, d), lambda i: (0, 0))
    extra = [] if after is None else [after]
    return pl.pallas_call(
        body, name=name, grid=(s // tr,),
        in_specs=[row, row, vec, row] + [ANY] * len(extra), out_specs=[row, row, vec],
        out_shape=[jax.ShapeDtypeStruct((s, d), F32), jax.ShapeDtypeStruct((s, d), BF16),
                   jax.ShapeDtypeStruct((1, d), F32)],
        compiler_params=_params(("arbitrary",)),
    )(dh, x, w, res, *extra)


def _conv_taps(x, w, rows):
    shifted = [x]
    for sft in (1, 2, 3):
        shifted.append(jnp.where(rows >= sft, pltpu.roll(x, sft, 0), 0.0))
    y = w[3:4, :] * shifted[0] + w[2:3, :] * shifted[1] + w[1:2, :] * shifted[2] + w[0:1, :] * shifted[3]
    return y, shifted


def _delta_pre_fwd(qkvz, conv_w, heads):
    s = qkvz.shape[0]
    nblk = 3 * heads

    def body(x_ref, w_ref, o_ref):
        part = pl.program_id(0) // heads
        rows = lax.broadcasted_iota(jnp.int32, (s, HEAD_DIM), 0)
        y, _ = _conv_taps(x_ref[...], w_ref[...], rows)
        a = y * _sigmoid(y)
        rs = lax.rsqrt(jnp.sum(a * a, axis=-1, keepdims=True) + EPS)
        fac = jnp.where(part == 0, rs * (HEAD_DIM ** -0.5), jnp.where(part == 1, rs, 1.0))
        o_ref[...] = a * fac

    return pl.pallas_call(
        body, name="delta_pre_fwd", grid=(nblk,),
        in_specs=[pl.BlockSpec((s, HEAD_DIM), lambda i: (0, i)), pl.BlockSpec((4, HEAD_DIM), lambda i: (0, i))],
        out_specs=pl.BlockSpec((s, HEAD_DIM), lambda i: (0, i)),
        out_shape=jax.ShapeDtypeStruct((s, 3 * heads * HEAD_DIM), F32),
        compiler_params=_params(("parallel",)),
    )(qkvz, conv_w)


def _delta_pre_bwd(dqkv, qkvz, conv_w, heads):
    s = qkvz.shape[0]
    nblk = 3 * heads

    def body(d_ref, x_ref, w_ref, dx_ref, dw_ref):
        part = pl.program_id(0) // heads
        rows = lax.broadcasted_iota(jnp.int32, (s, HEAD_DIM), 0)
        w = w_ref[...]
        y, shifted = _conv_taps(x_ref[...], w, rows)
        sg = _sigmoid(y)
        a = y * sg
        rs = lax.rsqrt(jnp.sum(a * a, axis=-1, keepdims=True) + EPS)
        unit = a * rs
        dn = d_ref[...]
        scale = jnp.where(part == 0, HEAD_DIM ** -0.5, 1.0)
        da_norm = scale * rs * (dn - unit * jnp.sum(dn * unit, axis=-1, keepdims=True))
        da = jnp.where(part < 2, da_norm, dn)
        dy = da * sg * (1.0 + y * (1.0 - sg))
        dx = w[3:4, :] * dy
        for sft in (1, 2, 3):
            dx = dx + w[3 - sft:4 - sft, :] * jnp.where(rows < s - sft, pltpu.roll(dy, s - sft, 0), 0.0)
        dx_ref[...] = dx.astype(BF16)
        for sft in range(4):
            dw_ref[3 - sft:4 - sft, :] = jnp.sum(dy * shifted[sft], axis=0, keepdims=True)

    col = pl.BlockSpec((s, HEAD_DIM), lambda i: (0, i))
    wsp = pl.BlockSpec((4, HEAD_DIM), lambda i: (0, i))
    return pl.pallas_call(
        body, name="delta_pre_bwd", grid=(nblk,),
        in_specs=[col, col, wsp], out_specs=[col, wsp],
        out_shape=[jax.ShapeDtypeStruct((s, 3 * heads * HEAD_DIM), BF16),
                   jax.ShapeDtypeStruct((4, 3 * heads * HEAD_DIM), F32)],
        compiler_params=_params(("parallel",)),
    )(dqkv, qkvz, conv_w)


def _heads_of(ref, heads):
    return jnp.stack([ref[:, h * HEAD_DIM:(h + 1) * HEAD_DIM] for h in range(heads)])


def _chunk_common(q, k, v, bd, a_log, dt_bias, heads, solved=None):
    c = CHUNK
    braw = jnp.stack([bd[:, h:h + 1] for h in range(heads)])
    draw = jnp.stack([bd[:, heads + h:heads + h + 1] for h in range(heads)])
    beta = _sigmoid(braw)
    xd = draw + dt_bias
    sp = jnp.maximum(xd, 0.0) + jnp.log1p(jnp.exp(-jnp.abs(xd)))
    g = -jnp.exp(a_log) * sp
    row = lax.broadcasted_iota(jnp.int32, (c, c), 0)
    col = lax.broadcasted_iota(jnp.int32, (c, c), 1)
    sq = (heads, c, c)
    g_b = jnp.broadcast_to(g, sq)
    g_row = _csum(jnp.where(row == col, g_b, 0.0))
    gam_col = _rsum(jnp.where(col <= row, jnp.broadcast_to(g_row, sq), 0.0))
    gam_row = _csum(jnp.where(row <= col, g_b, 0.0))
    causal = row >= col
    dm = jnp.where(causal, jnp.exp(jnp.where(causal, gam_col - gam_row, 0.0)), 0.0)
    kk = _dot(k, k, "nt")
    e = jnp.exp(gam_col)
    if solved is None:
        low = jnp.where(row > col, beta * kk * dm, 0.0)
        assert c in (INV_BLOCK, 2 * INV_BLOCK)
        same = (row // INV_BLOCK) == (col // INV_BLOCK)
        diag = jnp.where(same, low, 0.0)
        t = jnp.where(row == col, 1.0, 0.0) - diag
        pw = diag
        for _ in range((INV_BLOCK - 1).bit_length() - 1):
            pw = _dot(pw, pw)
            t = t + _dot(t, pw)
        if c > INV_BLOCK:
            t = t - _dot(_dot(t, low - diag), t)
        u = _dot(t, beta * v)
        w = _dot(t, (beta * e) * k)
    else:
        t, u, w = solved
    qk_raw = _dot(q, k, "nt")
    gl = _csum(g)
    el = jnp.exp(gl - gam_col)
    return dict(beta=beta, xd=xd, g=g, row=row, col=col, dm=dm, kk=kk, t=t, e=e, u=u, w=w,
                qk_raw=qk_raw, qk=qk_raw * dm, gl=gl, el=el, qd=e * q, kd=el * k, cd=jnp.exp(gl))


def _delta_chunk_fwd(qkv, bd, a_log, dt_bias, heads):
    s = qkv.shape[0]
    n = s // CHUNK
    dw = heads * HEAD_DIM
    blk = lambda part: pl.BlockSpec((CHUNK, dw), lambda i: (i, part))

    def body(q_ref, k_ref, v_ref, bd_ref, al_ref, dt_ref, o_ref, st_ref, t_ref, uw_ref, state):
        @pl.when(pl.program_id(0) == 0)
        def _():
            state[...] = jnp.zeros_like(state)

        cm = _chunk_common(_heads_of(q_ref, heads), _heads_of(k_ref, heads), _heads_of(v_ref, heads), bd_ref[...],
                           al_ref[...], dt_ref[...], heads)
        st = state[...]
        st_ref[0] = st
        t_ref[0] = cm["t"]
        uw_ref[0, 0] = cm["u"]
        uw_ref[0, 1] = cm["w"]
        vn = cm["u"] - _dot(cm["w"], st)
        o = _dot(cm["qd"], st) + _dot(cm["qk"], vn)
        for h in range(heads):
            o_ref[:, h * HEAD_DIM:(h + 1) * HEAD_DIM] = o[h]
        state[...] = cm["cd"] * st + _dot(cm["kd"], vn, "tn")

    smem = pl.BlockSpec((heads, 1, 1), lambda i: (0, 0, 0))
    return pl.pallas_call(
        body, name="delta_chunk_fwd", grid=(n,),
        in_specs=[blk(0), blk(1), blk(2), pl.BlockSpec((CHUNK, HEAD_DIM), lambda i: (i, 0)), smem, smem],
        out_specs=[pl.BlockSpec((CHUNK, dw), lambda i: (i, 0)),
                   pl.BlockSpec((1, heads, HEAD_DIM, HEAD_DIM), lambda i: (i, 0, 0, 0)),
                   pl.BlockSpec((1, heads, CHUNK, CHUNK), lambda i: (i, 0, 0, 0)),
                   pl.BlockSpec((1, 2, heads, CHUNK, HEAD_DIM), lambda i: (i, 0, 0, 0, 0))],
        out_shape=[jax.ShapeDtypeStruct((s, dw), F32),
                   jax.ShapeDtypeStruct((n, heads, HEAD_DIM, HEAD_DIM), F32),
                   jax.ShapeDtypeStruct((n, heads, CHUNK, CHUNK), F32),
                   jax.ShapeDtypeStruct((n, 2, heads, CHUNK, HEAD_DIM), F32)],
        scratch_shapes=[pltpu.VMEM((heads, HEAD_DIM, HEAD_DIM), F32)],
        compiler_params=_params(("arbitrary",)),
    )(qkv, qkv, qkv, bd, a_log, dt_bias)


def _delta_chunk_bwd(do, qkv, bd, saved, a_log, dt_bias, heads):
    s = qkv.shape[0]
    n = s // CHUNK
    dw = heads * HEAD_DIM
    c = CHUNK
    blk = lambda part: pl.BlockSpec((CHUNK, dw), lambda i: (n - 1 - i, part))

    def all_heads(q, k, v, dov, st, solved, dsn, bd, a_log, dt_bias):
        cm = _chunk_common(q, k, v, bd, a_log, dt_bias, heads, solved)
        beta, e, dm, row, col = cm["beta"], cm["e"], cm["dm"], cm["row"], cm["col"]
        sq = (heads, c, c)
        vn = cm["u"] - _dot(cm["w"], st)
        dvn = _dot(cm["kd"], dsn)
        dkd = _dot(vn, dsn, "nt")
        dcd = _csum(_rsum(st * dsn))
        ds = cm["cd"] * dsn
        dqd = _dot(dov, st, "nt")
        ds = ds + _dot(cm["qd"], dov, "tn")
        dqk = _dot(dov, vn, "nt")
        dvn = dvn + _dot(cm["qk"], dov, "tn")
        dw_ = -_dot(dvn, st, "nt")
        ds = ds - _dot(cm["w"], dvn, "tn")
        drhs_u = _dot(cm["t"], dvn, "tn")
        drhs_w = _dot(cm["t"], dw_, "tn")
        da = -(_dot(drhs_u, cm["u"], "nt") + _dot(drhs_w, cm["w"], "nt"))
        dl = jnp.where(row > col, da, 0.0)
        dbeta = _rsum(dl * cm["kk"] * dm)
        dkk = dl * beta * dm
        dd = dl * beta * cm["kk"]
        dv = beta * drhs_u
        ek = e * k
        dbeta = dbeta + _rsum(drhs_u * v) + _rsum(drhs_w * ek)
        dk = (beta * e) * drhs_w
        dgam = _rsum(drhs_w * (beta * ek))
        dqkm = dqk * dm
        dq = _dot(dqkm, k)
        dk = dk + _dot(dqkm, q, "tn")
        dd = dd + dqk * cm["qk_raw"]
        dk = dk + _dot(dkk, k) + _dot(dkk, k, "tn")
        dq = dq + e * dqd
        dgam = dgam + _rsum(dqd * cm["qd"])
        dk = dk + cm["el"] * dkd
        r = _rsum(dkd * cm["kd"])
        dgam = dgam - r
        dgl = _csum(r) + dcd * cm["cd"]
        mm = dd * dm
        colsum_c = _rsum(jnp.where(row == col, jnp.broadcast_to(_csum(mm), sq), 0.0))
        dgam = dgam + _rsum(mm) - colsum_c
        ridx = lax.broadcasted_iota(jnp.int32, (c, 1), 0)
        dgam = dgam + jnp.where(ridx == c - 1, dgl, 0.0)
        dgam_row = _csum(jnp.where(row == col, jnp.broadcast_to(dgam, sq), 0.0))
        dg = _rsum(jnp.where(col >= row, jnp.broadcast_to(dgam_row, sq), 0.0))
        d_xd = dg * (-jnp.exp(a_log)) * _sigmoid(cm["xd"])
        d_braw = dbeta * beta * (1.0 - beta)
        d_alog = dg * cm["g"]
        lane = lax.broadcasted_iota(jnp.int32, (c, HEAD_DIM), 1)
        dbd = jnp.zeros((c, HEAD_DIM), F32)
        for h in range(heads):
            dbd = (dbd + jnp.where(lane == h, d_braw[h], 0.0) + jnp.where(lane == h + heads, d_xd[h], 0.0)
                   + jnp.where(lane == h + 2 * heads, d_alog[h], 0.0))
        return dq, dk, dv, ds, dbd

    def body(do_ref, q_ref, k_ref, v_ref, bd_ref, st_ref, t_ref, uw_ref, al_ref, dt_ref, dqkv_ref, dbd_ref, dstate):
        @pl.when(pl.program_id(0) == 0)
        def _():
            dstate[...] = jnp.zeros_like(dstate)

        dq, dk, dv, ds, dbd = all_heads(_heads_of(q_ref, heads), _heads_of(k_ref, heads), _heads_of(v_ref, heads),
                                        _heads_of(do_ref, heads), st_ref[0], (t_ref[0], uw_ref[0, 0], uw_ref[0, 1]),
                                        dstate[...], bd_ref[...],
                                        al_ref[...], dt_ref[...])
        for part, val in enumerate((dq, dk, dv)):
            for h in range(heads):
                lo = part * dw + h * HEAD_DIM
                dqkv_ref[:, lo:lo + HEAD_DIM] = val[h]
        dstate[...] = ds
        dbd_ref[...] = dbd

    smem = pl.BlockSpec((heads, 1, 1), lambda i: (0, 0, 0))
    shared = pl.BlockSpec((CHUNK, HEAD_DIM), lambda i: (n - 1 - i, 0))
    wide = pl.BlockSpec((CHUNK, dw), lambda i: (n - 1 - i, 0))
    return pl.pallas_call(
        body, name="delta_chunk_bwd", grid=(n,),
        in_specs=[wide, blk(0), blk(1), blk(2), shared,
                  pl.BlockSpec((1, heads, HEAD_DIM, HEAD_DIM), lambda i: (n - 1 - i, 0, 0, 0)),
                  pl.BlockSpec((1, heads, CHUNK, CHUNK), lambda i: (n - 1 - i, 0, 0, 0)),
                  pl.BlockSpec((1, 2, heads, CHUNK, HEAD_DIM), lambda i: (n - 1 - i, 0, 0, 0, 0)), smem, smem],
        out_specs=[pl.BlockSpec((CHUNK, 3 * dw), lambda i: (n - 1 - i, 0)), shared],
        out_shape=[jax.ShapeDtypeStruct((s, 3 * dw), F32), jax.ShapeDtypeStruct((s, HEAD_DIM), F32)],
        scratch_shapes=[pltpu.VMEM((heads, HEAD_DIM, HEAD_DIM), F32)],
        compiler_params=_params(("arbitrary",)),
    )(do, qkv, qkv, qkv, bd, *saved, a_log, dt_bias)


def _delta_post_fwd(o, qkvz, w, heads):
    s = o.shape[0]
    tr = _tile(s, 1024, 8)

    def body(o_ref, z_ref, w_ref, out_ref):
        ov, z = o_ref[...], z_ref[...]
        rstd = lax.rsqrt(jnp.mean(ov * ov, axis=-1, keepdims=True) + EPS)
        out_ref[...] = (ov * rstd * w_ref[...] * (z * _sigmoid(z))).astype(BF16)

    return pl.pallas_call(
        body, name="delta_post_fwd", grid=(s // tr, heads),
        in_specs=[pl.BlockSpec((tr, HEAD_DIM), lambda i, h: (i, h)),
                  pl.BlockSpec((tr, HEAD_DIM), lambda i, h: (i, 3 * heads + h)),
                  pl.BlockSpec((1, HEAD_DIM), lambda i, h: (0, 0))],
        out_specs=pl.BlockSpec((tr, HEAD_DIM), lambda i, h: (i, h)),
        out_shape=jax.ShapeDtypeStruct((s, heads * HEAD_DIM), BF16),
        compiler_params=_params(("parallel", "parallel")),
    )(o, qkvz, w)


def _delta_post_bwd(dmix, o, qkvz, w, heads):
    s = o.shape[0]
    tr = _tile(s, 1024, 8)

    def body(d_ref, o_ref, z_ref, w_ref, do_ref, dz_ref, dw_ref):
        d, ov, z, wv = d_ref[...], o_ref[...], z_ref[...], w_ref[...]
        sg = _sigmoid(z)
        rstd = lax.rsqrt(jnp.mean(ov * ov, axis=-1, keepdims=True) + EPS)
        ohat = ov * rstd
        dz_ref[...] = (d * (ohat * wv) * sg * (1.0 + z * (1.0 - sg))).astype(BF16)
        dn = d * (z * sg)
        gw = dn * wv
        do_ref[...] = rstd * (gw - ohat * jnp.mean(gw * ohat, axis=-1, keepdims=True))

        @pl.when((pl.program_id(0) == 0) & (pl.program_id(1) == 0))
        def _():
            dw_ref[...] = jnp.zeros_like(dw_ref)

        dw_ref[...] += jnp.sum(dn * ohat, axis=0, keepdims=True)

    head = pl.BlockSpec((tr, HEAD_DIM), lambda i, h: (i, h))
    vec = pl.BlockSpec((1, HEAD_DIM), lambda i, h: (0, 0))
    dw = heads * HEAD_DIM
    return pl.pallas_call(
        body, name="delta_post_bwd", grid=(s // tr, heads),
        in_specs=[head, head, pl.BlockSpec((tr, HEAD_DIM), lambda i, h: (i, 3 * heads + h)), vec],
        out_specs=[head, head, vec],
        out_shape=[jax.ShapeDtypeStruct((s, dw), F32), jax.ShapeDtypeStruct((s, dw), BF16),
                   jax.ShapeDtypeStruct((1, HEAD_DIM), F32)],
        compiler_params=_params(("arbitrary", "arbitrary")),
    )(dmix, o, qkvz, w)


def _rope_tables(positions, s):
    half = HEAD_DIM // 2
    inv_freq = ROPE_THETA ** (-jnp.arange(half, dtype=F32) / half)
    ang = positions.reshape(s, 1).astype(F32) * inv_freq
    cos, sin = jnp.cos(ang), jnp.sin(ang)
    return jnp.concatenate([cos, cos], axis=-1), jnp.concatenate([-sin, sin], axis=-1)


def _attn_pre_fwd(aqkv, wq, wk, cosf, sinf, heads):
    s = aqkv.shape[0]
    tr = _tile(s, 1024, 8)

    def body(x_ref, wq_ref, wk_ref, c_ref, s_ref, o_ref):
        xv = x_ref[...]
        wv = jnp.where(pl.program_id(1) < heads, wq_ref[...], wk_ref[...])
        y = xv * lax.rsqrt(jnp.mean(xv * xv, axis=-1, keepdims=True) + EPS) * wv
        o_ref[...] = y * c_ref[...] + pltpu.roll(y, HEAD_DIM // 2, 1) * s_ref[...]

    blk = pl.BlockSpec((tr, HEAD_DIM), lambda i, j: (i, j))
    vec = pl.BlockSpec((1, HEAD_DIM), lambda i, j: (0, 0))
    tab = pl.BlockSpec((tr, HEAD_DIM), lambda i, j: (i, 0))
    return pl.pallas_call(
        body, name="attn_pre_fwd", grid=(s // tr, 2 * heads),
        in_specs=[blk, vec, vec, tab, tab], out_specs=blk,
        out_shape=jax.ShapeDtypeStruct((s, 2 * heads * HEAD_DIM), F32),
        compiler_params=_params(("parallel", "parallel")),
    )(aqkv, wq, wk, cosf, sinf)


def _band():
    qi = lax.broadcasted_iota(jnp.int32, (SPAN, 2 * SPAN), 0)
    ki = lax.broadcasted_iota(jnp.int32, (SPAN, 2 * SPAN), 1)
    dist = qi + SPAN - ki
    return (dist >= 0) & (dist <= SPAN), ki >= SPAN


def _sub(g, r, d):
    if d == 1:
        return pl.ds(g * SPAN, SPAN)
    return pl.ds(g * SPAN * d + r, SPAN, stride=d)


def _attn_blocks(s):
    assert s % (SPAN * max(DILATIONS)) == 0
    return [(p_i, d, r, g) for p_i, d in enumerate(DILATIONS) for r in range(d) for g in range(s // (SPAN * d))]


def _attn_fwd(qk, aqkv, w, heads):
    s = qk.shape[0]
    aw = heads * HEAD_DIM

    def body(q_ref, k_ref, v_ref, w_ref, mix_ref, acc_ref, m_ref, l_ref):
        band, own = _band()
        for p_i, d, r, g in _attn_blocks(s):
            rows = _sub(g, r, d)
            kc, vc = k_ref[rows, :], v_ref[rows, :]
            if g == 0:
                kp, vp, mask = kc, vc, band & own
            else:
                mask = band
            kcat = jnp.concatenate([kp, kc], axis=0)
            vcat = jnp.concatenate([vp, vc], axis=0)
            kp, vp = kc, vc
            sc = _dot(q_ref[rows, :] * (HEAD_DIM ** -0.5), kcat, "nt")
            sc = jnp.where(mask, sc, NEG)
            m = jnp.max(sc, axis=-1, keepdims=True)
            if p_i == 0:
                p = jnp.exp(sc - m)
                acc_ref[rows, :] = _dot(p, vcat)
                l_new = jnp.sum(p, axis=-1, keepdims=True)
            else:
                m_old = m_ref[rows, 0:1]
                m = jnp.maximum(m, m_old)
                alpha = jnp.exp(m_old - m)
                p = jnp.exp(sc - m)
                acc_ref[rows, :] = alpha * acc_ref[rows, :] + _dot(p, vcat)
                l_new = alpha * l_ref[rows, 0:1] + jnp.sum(p, axis=-1, keepdims=True)
            m_ref[rows, :] = jnp.broadcast_to(m, (SPAN, HEAD_DIM))
            l_ref[rows, :] = jnp.broadcast_to(l_new, (SPAN, HEAD_DIM))
        den = l_ref[...]
        ob = acc_ref[...] / den
        acc_ref[...] = ob
        m_ref[...] = m_ref[...] + jnp.log(den)
        rstd = lax.rsqrt(jnp.mean(ob * ob, axis=-1, keepdims=True) + EPS)
        mix_ref[...] = (ob * rstd * w_ref[...]).astype(BF16)

    col = lambda off: pl.BlockSpec((s, HEAD_DIM), lambda h: (0, off + h))
    return pl.pallas_call(
        body, name="attn_fwd", grid=(heads,),
        in_specs=[col(0), col(heads), col(2 * heads), pl.BlockSpec((1, HEAD_DIM), lambda h: (0, 0))],
        out_specs=[col(0), col(0), col(0)],
        out_shape=[jax.ShapeDtypeStruct((s, aw), BF16), jax.ShapeDtypeStruct((s, aw), F32),
                   jax.ShapeDtypeStruct((s, aw), F32)],
        scratch_shapes=[pltpu.VMEM((s, HEAD_DIM), F32)],
        compiler_params=_params(("parallel",)),
    )(qk, qk, aqkv, w)


def _attn_merge_bwd(dmix, ob, w, heads):
    s = ob.shape[0]
    tr = _tile(s, 1024, 8)

    def body(d_ref, ob_ref, w_ref, do_ref, dsum_ref, dw_ref):
        d, ov = d_ref[...], ob_ref[...]
        rstd = lax.rsqrt(jnp.mean(ov * ov, axis=-1, keepdims=True) + EPS)
        ohat = ov * rstd
        gw = d * w_ref[...]
        dov = rstd * (gw - ohat * jnp.mean(gw * ohat, axis=-1, keepdims=True))
        do_ref[...] = dov
        dsum_ref[...] = jnp.broadcast_to(jnp.sum(dov * ov, axis=-1, keepdims=True), dov.shape)

        @pl.when((pl.program_id(0) == 0) & (pl.program_id(1) == 0))
        def _():
            dw_ref[...] = jnp.zeros_like(dw_ref)

        dw_ref[...] += jnp.sum(d * ohat, axis=0, keepdims=True)

    blk = pl.BlockSpec((tr, HEAD_DIM), lambda i, h: (i, h))
    vec = pl.BlockSpec((1, HEAD_DIM), lambda i, h: (0, 0))
    aw = heads * HEAD_DIM
    return pl.pallas_call(
        body, name="attn_merge_bwd", grid=(s // tr, heads),
        in_specs=[pl.BlockSpec((tr, HEAD_DIM), lambda i, h: (i, heads + h)), blk, vec],
        out_specs=[blk, blk, vec],
        out_shape=[jax.ShapeDtypeStruct((s, aw), F32), jax.ShapeDtypeStruct((s, aw), F32),
                   jax.ShapeDtypeStruct((1, HEAD_DIM), F32)],
        compiler_params=_params(("arbitrary", "arbitrary")),
    )(dmix, ob, w)


def _attn_bwd(qk, aqkv, do, lse, dsum, heads):
    s = qk.shape[0]
    aw = heads * HEAD_DIM
    scale = HEAD_DIM ** -0.5

    def body(q_ref, k_ref, v_ref, do_ref, l_ref, ds_ref, out_ref):
        band, own = _band()
        dq_ref, dk_ref, dv_ref = out_ref.at[0], out_ref.at[1], out_ref.at[2]
        out_ref[...] = jnp.zeros((3, s, HEAD_DIM), F32)
        for _, d, r, g in _attn_blocks(s):
            rows = _sub(g, r, d)
            qs, dov = q_ref[rows, :] * scale, do_ref[rows, :]
            kc, vc = k_ref[rows, :], v_ref[rows, :]
            if g == 0:
                kp, vp, mask = kc, vc, band & own
            else:
                mask = band
            kcat = jnp.concatenate([kp, kc], axis=0)
            vcat = jnp.concatenate([vp, vc], axis=0)
            p = jnp.where(mask, jnp.exp(_dot(qs, kcat, "nt") - l_ref[rows, 0:1]), 0.0)
            dsc = p * (_dot(dov, vcat, "nt") - ds_ref[rows, 0:1])
            dq_ref[rows, :] += scale * _dot(dsc, kcat)
            dk = _dot(dsc, qs, "tn")
            dv = _dot(p, dov, "tn")
            if g > 0:
                dk_ref[prows, :] += dk_own + dk[:SPAN]
                dv_ref[prows, :] += dv_own + dv[:SPAN]
            dk_own, dv_own = dk[SPAN:], dv[SPAN:]
            if g == s // (SPAN * d) - 1:
                dk_ref[rows, :] += dk_own
                dv_ref[rows, :] += dv_own
            kp, vp, prows = kc, vc, rows

    col = lambda off: pl.BlockSpec((s, HEAD_DIM), lambda h: (0, off + h))
    return pl.pallas_call(
        body, name="attn_bwd", grid=(heads,),
        in_specs=[col(0), col(heads), col(2 * heads), col(0), col(0), col(0)],
        out_specs=pl.BlockSpec((3, s, HEAD_DIM), lambda h: (0, 0, h)),
        out_shape=jax.ShapeDtypeStruct((3, s, aw), F32),
        compiler_params=_params(("parallel",)),
    )(qk, qk, aqkv, do, lse, dsum)


def _attn_pre_bwd(grads, aqkv, wq, wk, cosf, sinf, heads):
    s = aqkv.shape[0]
    tr = _tile(s, 1024, 8)
    nrow = s // tr

    def body(g_ref, x_ref, wq_ref, wk_ref, c_ref, s_ref, dx_ref, dwq_ref, dwk_ref):
        i, j = pl.program_id(0), pl.program_id(1)
        kind = j // heads
        dout = g_ref[0]
        tot_v = dout
        dy = dout * c_ref[...] + pltpu.roll(dout * s_ref[...], HEAD_DIM // 2, 1)
        xv = x_ref[...]
        wv = jnp.where(kind == 0, wq_ref[...], wk_ref[...])
        rstd = lax.rsqrt(jnp.mean(xv * xv, axis=-1, keepdims=True) + EPS)
        xhat = xv * rstd
        gw = dy * wv
        dxn = rstd * (gw - xhat * jnp.mean(gw * xhat, axis=-1, keepdims=True))
        dx_ref[...] = jnp.where(kind == 2, tot_v, dxn).astype(BF16)
        dwc = jnp.sum(dy * xhat, axis=0, keepdims=True)

        @pl.when((i == 0) & (j == 0))
        def _():
            dwq_ref[...] = jnp.zeros_like(dwq_ref)
            dwk_ref[...] = jnp.zeros_like(dwk_ref)

        @pl.when(kind == 0)
        def _():
            dwq_ref[...] += dwc

        @pl.when(kind == 1)
        def _():
            dwk_ref[...] += dwc

    grad = pl.BlockSpec((1, tr, HEAD_DIM), lambda i, j: (j // heads, i, j % heads))
    blk = pl.BlockSpec((tr, HEAD_DIM), lambda i, j: (i, j))
    vec = pl.BlockSpec((1, HEAD_DIM), lambda i, j: (0, 0))
    tab = pl.BlockSpec((tr, HEAD_DIM), lambda i, j: (i, 0))
    return pl.pallas_call(
        body, name="attn_pre_bwd", grid=(nrow, 3 * heads),
        in_specs=[grad, blk, vec, vec, tab, tab], out_specs=[blk, vec, vec],
        out_shape=[jax.ShapeDtypeStruct((s, 3 * heads * HEAD_DIM), BF16),
                   jax.ShapeDtypeStruct((1, HEAD_DIM), F32), jax.ShapeDtypeStruct((1, HEAD_DIM), F32)],
        compiler_params=_params(("arbitrary", "arbitrary")),
    )(grads, aqkv, wq, wk, cosf, sinf)


def _colsum(a, name):
    s, d = a.shape
    tr = _tile(s, 1024, 8)

    def body(a_ref, o_ref):
        @pl.when(pl.program_id(0) == 0)
        def _():
            o_ref[...] = jnp.zeros_like(o_ref)

        o_ref[...] += jnp.sum(a_ref[...], axis=0, keepdims=True)

    return pl.pallas_call(
        body, name=name, grid=(s // tr,),
        in_specs=[pl.BlockSpec((tr, d), lambda i: (i, 0))], out_specs=pl.BlockSpec((1, d), lambda i: (0, 0)),
        out_shape=jax.ShapeDtypeStruct((1, d), F32),
        compiler_params=_params(("arbitrary",)),
    )(a)


def _local_step(x, positions, target, small, w_qkvz, w_bd, w_attn, conv_w, later_weights, ffn_grads_ready,
                rest_grads_ready, after=None):
    s, dmod = x.shape
    heads = dmod // (2 * HEAD_DIM)
    dw = heads * HEAD_DIM
    a_log, dt_bias = small["a_log"].reshape(heads, 1, 1), small["dt_bias"].reshape(heads, 1, 1)
    cosf, sinf = _rope_tables(positions, s)

    h1 = _rmsnorm_fwd(x, small["attn_norm_w"], "norm1_fwd", after=after)
    qkvz = _matmul(h1, w_qkvz, "nn", "proj_qkvz")
    bd = _matmul(h1, w_bd, "nn", "proj_bd")
    aqkv = _matmul(h1, w_attn, "nn", "proj_attn")
    dqkv = _delta_pre_fwd(qkvz, conv_w, heads)
    o_d, *saved = _delta_chunk_fwd(dqkv, bd, a_log, dt_bias, heads)
    mix_a = _delta_post_fwd(o_d, qkvz, small["delta_out_norm_w"], heads)
    qk_rot = _attn_pre_fwd(aqkv, small["q_norm_w"], small["k_norm_w"], cosf, sinf, heads)
    mix_b, ob, lse = _attn_fwd(qk_rot, aqkv, small["attn_out_norm_w"], heads)
    w_out, behind, ffn_weights = later_weights((mix_a, mix_b))
    x1 = _matmul(mix_a, w_out, "nn", "out_proj_a", add=x, b_rows=(0, dw), after=behind)
    x1 = _matmul(mix_b, w_out, "nn", "out_proj_b", add=x1, b_rows=(dw, dw))
    h2 = _rmsnorm_fwd(x1, small["ffn_norm_w"], "norm2_fwd")
    w_gu, w_down = ffn_weights(h2)
    ff = w_down.shape[0]

    def swiglu(accs, vals):
        g, u = accs
        return g, u, g * _sigmoid(g) * u

    gate, up, act = _matmul(h2, w_gu, "nn", "ffn_gate_up", b_cols=(0, ff), b2_cols=(ff, ff), finish=swiglu,
                            out_dtypes=[BF16, BF16, BF16])

    def loss_head(acc, vals):
        err = acc + vals[0] - vals[1]
        part = 0.5 * jnp.sum(jnp.sum(err * err, axis=-1, keepdims=True) * (1.0 / dmod), axis=0, keepdims=True)
        lane = lax.broadcasted_iota(jnp.int32, (1, HEAD_DIM), 1)
        return err * (1.0 / dmod), err * (1.0 / dmod), jnp.where(lane == 0, part, 0.0)

    dy, dy16, loss_row = _matmul(act, w_down, "nn", "ffn_down", tiles=[(x1, 0), (target, 0)], finish=loss_head,
                                 out_dtypes=[F32, BF16], row_sum=True)

    def swiglu_bwd(acc, vals):
        g, u = vals[0].astype(F32), vals[1].astype(F32)
        sg = _sigmoid(g)
        return acc * u * sg * (1.0 + g * (1.0 - sg)), acc * g * sg

    dgate, dup = _matmul(dy16, w_down, "nt", "ffn_down_dx", tiles=[(gate, 0), (up, 0)], finish=swiglu_bwd,
                         out_dtypes=[BF16, BF16])
    g_w_down = _matmul(act, dy16, "tn", "ffn_down_dw")
    g_w_gate = _matmul(h2, dgate, "tn", "ffn_gate_dw")
    g_w_up = _matmul(h2, dup, "tn", "ffn_up_dw")
    behind, and_then = ffn_grads_ready(g_w_gate, g_w_up, g_w_down)
    dh2 = _matmul(dgate, w_gu, "nt", "ffn_gate_dx", b_cols=(0, ff), after=behind)
    dh2 = _matmul(dup, w_gu, "nt", "ffn_up_dx", b_cols=(ff, ff), add=dh2)
    behind = and_then(dh2)
    dx1, dx1_16, g_ffn_norm = _rmsnorm_bwd(dh2, x1, small["ffn_norm_w"], dy, "norm2_bwd", after=behind)
    dmix = _matmul(dx1_16, w_out, "nt", "out_proj_dx")
    g_w_out_a = _matmul(mix_a, dx1_16, "tn", "out_proj_dw_a")
    g_w_out_b = _matmul(mix_b, dx1_16, "tn", "out_proj_dw_b")
    dob, dsum, g_attn_out_norm = _attn_merge_bwd(dmix, ob, small["attn_out_norm_w"], heads)
    grads = _attn_bwd(qk_rot, aqkv, dob, lse, dsum, heads)
    d_aqkv, g_q_norm, g_k_norm = _attn_pre_bwd(grads, aqkv, small["q_norm_w"], small["k_norm_w"], cosf, sinf, heads)
    do_d, dz, g_delta_out_norm = _delta_post_bwd(dmix, o_d, qkvz, small["delta_out_norm_w"], heads)
    ddqkv, dbd = _delta_chunk_bwd(do_d, dqkv, bd, saved, a_log, dt_bias, heads)
    d_qkv_raw, g_conv = _delta_pre_bwd(ddqkv, qkvz, conv_w, heads)
    bd_sums = _colsum(dbd, "bd_colsum")
    g_w_qkv = _matmul(h1, d_qkv_raw, "tn", "proj_qkv_dw")
    g_w_z = _matmul(h1, dz, "tn", "proj_z_dw")
    g_w_bd = _matmul(h1, dbd, "tn", "proj_bd_dw")
    g_w_attn = _matmul(h1, d_aqkv, "tn", "proj_attn_dw")
    behind, and_then = rest_grads_ready(dict(w_qkv=g_w_qkv, w_z=g_w_z, w_bd=g_w_bd, w_attn=g_w_attn,
                                             w_out_a=g_w_out_a, w_out_b=g_w_out_b))
    dh1 = _matmul(d_qkv_raw, w_qkvz, "nt", "proj_qkv_dx", b_cols=(0, 3 * dw), after=behind)
    dh1 = _matmul(dz, w_qkvz, "nt", "proj_z_dx", b_cols=(3 * dw, dw), add=dh1, after=and_then(dh1))
    dh1 = _matmul(d_aqkv, w_attn, "nt", "proj_attn_dx", add=dh1)
    dh1 = _matmul(dbd, w_bd, "nt", "proj_bd_dx", add=dh1)
    grad_x, _, g_attn_norm = _rmsnorm_bwd(dh1, x, small["attn_norm_w"], dx1, "norm1_bwd")
    small_grads = dict(
        attn_norm_w=g_attn_norm, a_log=bd_sums[:, 2 * heads:3 * heads], dt_bias=bd_sums[:, heads:2 * heads],
        delta_out_norm_w=g_delta_out_norm, q_norm_w=g_q_norm, k_norm_w=g_k_norm,
        attn_out_norm_w=g_attn_out_norm, ffn_norm_w=g_ffn_norm, conv_w=g_conv)
    return loss_row, grad_x, small_grads


def _adamw(w, g, m, v, name):
    r, c = w.shape
    tr = _tile(r, 256, 8)

    def body(w_ref, g_ref, m_ref, v_ref, d_ref, nm_ref, nv_ref):
        gv = g_ref[...]
        nm = ADAM_B1 * m_ref[...] + (1.0 - ADAM_B1) * gv
        nv = ADAM_B2 * v_ref[...] + (1.0 - ADAM_B2) * (gv * gv)
        m_hat = nm / (1.0 - ADAM_B1 ** ADAM_STEP)
        v_hat = nv / (1.0 - ADAM_B2 ** ADAM_STEP)
        d_ref[...] = -ADAM_LR * (m_hat / (jnp.sqrt(v_hat) + ADAM_EPS) + ADAM_WD * w_ref[...])
        nm_ref[...] = nm
        nv_ref[...] = nv

    blk = pl.BlockSpec((tr, c), lambda i: (i, 0))
    return pl.pallas_call(
        body, name=name, grid=(r // tr,),
        in_specs=[blk] * 4, out_specs=[blk] * 3,
        out_shape=[jax.ShapeDtypeStruct((r, c), F32)] * 3,
        compiler_params=_params(("parallel",)),
    )(w, g, m, v)


def _add_half_bf16(g, b, place, name):
    n, half, c = b.shape
    tr = _tile(half, 512, 16)
    nb = half // tr

    def body(place_ref, g_ref, b_ref, o_ref):
        o_ref[...] = (g_ref[...].astype(F32) + b_ref[...].astype(F32)).astype(BF16)

    blk = pl.BlockSpec((1, tr, c), lambda i, j, p: (i, j, 0))
    return pl.pallas_call(
        body, name=name,
        grid_spec=pltpu.PrefetchScalarGridSpec(
            num_scalar_prefetch=1, grid=(n, nb),
            in_specs=[pl.BlockSpec((1, tr, c), lambda i, j, p: (i, p[0] * nb + j, 0)), blk], out_specs=blk),
        out_shape=jax.ShapeDtypeStruct((n, half, c), BF16),
        compiler_params=_params(("parallel", "parallel")),
    )(place, g, b)


def _sum4_f32(mine, others, place, name):
    _, half, c = mine.shape
    tr = _tile(half, 512, 16)
    nb = half // tr

    def body(place_ref, a_ref, b_ref, o_ref):
        acc = a_ref[0].astype(F32)
        for j in range(3):
            acc = acc + b_ref[j].astype(F32)
        o_ref[...] = acc

    return pl.pallas_call(
        body, name=name,
        grid_spec=pltpu.PrefetchScalarGridSpec(
            num_scalar_prefetch=1, grid=(nb,),
            in_specs=[pl.BlockSpec((1, tr, c), lambda i, p: (p[1], i, 0)),
                      pl.BlockSpec((3, tr, c), lambda i, p: (0, i, 0))],
            out_specs=pl.BlockSpec((tr, c), lambda i, p: (p[0] * nb + i, 0))),
        out_shape=jax.ShapeDtypeStruct((2 * half, c), F32),
        compiler_params=_params(("parallel",)),
    )(place, mine, others)


def _place():
    x, y, c = lax.axis_index("x"), lax.axis_index("y"), lax.axis_index("c")
    other_chips = [(1 - x, y), (x, 1 - y), (1 - x, 1 - y)]
    return x, y, c, (x, y, 1 - c), other_chips


ANY = pl.BlockSpec(memory_space=pl.ANY)


def _remote(k, src, dst, to, send_sems, recv_sems):
    return pltpu.make_async_remote_copy(src_ref=src, dst_ref=dst, send_sem=send_sems.at[k], recv_sem=recv_sems.at[k],
                                        device_id=to, device_id_type=MESH)


def _half(ref, lead, hc):
    if lead is None:
        half = ref.shape[0] // 2
        return ref.at[pl.ds(hc * half, half), :]
    half = ref.shape[1] // 2
    return ref.at[lead, pl.ds(hc * half, half), :]


def _all_gather_weights(slots, whole, name):
    nt, nw = len(slots), len(whole)
    base_w, base_f, base_d = 2 * nt, 2 * nt + 3 * nw, 4 * nt + 3 * nw

    def quarter(ref, lead, hc, q):
        quart = ref.shape[1] // 4
        return ref.at[lead, pl.ds((2 * hc + q) * quart, quart), :]

    def body(*refs):
        ins, outs = refs[:nt + nw], refs[nt + nw:2 * (nt + nw)]
        sems = refs[2 * (nt + nw):]
        x, y, c, sibling, chips = _place()
        me, xn, yn, dg = 2 * x + y, 2 * (1 - x) + y, 2 * x + 1 - y, 2 * (1 - x) + 1 - y
        to_x, to_y = (1 - x, y, c), (x, 1 - y, c)
        cps = []
        for t in range(nt):
            cps.append(_remote(2 * t, _half(ins[t], me, c), _half(outs[t], me, c), to_x, *sems))
            cps.append(_remote(2 * t + 1, _half(ins[t], me, c), _half(outs[t], me, c), to_y, *sems))
        for j, (px, py) in enumerate(chips):
            for t in range(nw):
                cps.append(_remote(base_w + j * nw + t, ins[nt + t].at[me], outs[nt + t].at[me], (px, py, c), *sems))
        for cp in cps:
            cp.start()

        def start(k, ref, to):
            cp = _remote(k, ref, ref, to, *sems)
            cp.start()
            cps.append(cp)

        for t in range(nt):
            landed = _half(outs[t], xn, c)
            _remote(2 * t, landed, landed, to_x, *sems).wait_recv()
            start(base_f + 2 * t, quarter(outs[t], xn, c, 0), to_y)
            start(base_d + 3 * t, landed, sibling)
            landed = _half(outs[t], yn, c)
            _remote(2 * t + 1, landed, landed, to_y, *sems).wait_recv()
            start(base_f + 2 * t + 1, quarter(outs[t], yn, c, 1), to_x)
            start(base_d + 3 * t + 1, landed, sibling)
        for t in range(nt):
            q0, q1 = quarter(outs[t], dg, c, 0), quarter(outs[t], dg, c, 1)
            _remote(base_f + 2 * t, q0, q0, to_y, *sems).wait_recv()
            _remote(base_f + 2 * t + 1, q1, q1, to_x, *sems).wait_recv()
            start(base_d + 3 * t + 2, _half(outs[t], dg, c), sibling)
        for j, (px, py) in enumerate(chips):
            for t in range(nw):
                landed = outs[nt + t].at[2 * px + py]
                _remote(base_w + j * nw + t, landed, landed, (px, py, c), *sems).wait_recv()
        for t in range(nt):
            for j, chip in enumerate((xn, yn, dg)):
                other = _half(outs[t], chip, 1 - c)
                _remote(base_d + 3 * t + j, other, other, sibling, *sems).wait_recv()
        for cp in cps:
            cp.wait_send()

    arrays = list(slots) + list(whole)
    n_sem = 7 * nt + 3 * nw
    return pl.pallas_call(
        body, name=name, in_specs=[ANY] * len(arrays), out_specs=[ANY] * len(arrays),
        input_output_aliases={i: i for i in range(len(arrays))},
        out_shape=[jax.ShapeDtypeStruct(a.shape, a.dtype) for a in arrays],
        scratch_shapes=[pltpu.SemaphoreType.DMA((n_sem,)), pltpu.SemaphoreType.DMA((n_sem,))],
    )(*arrays)


def _join_halves(fs, name):
    nt = len(fs)

    def body(*refs):
        in_refs, out_refs, (send_sems, recv_sems) = refs[:nt], refs[nt:2 * nt], refs[2 * nt:]
        _, _, c, sibling, _ = _place()
        cps = [_remote(t, _half(in_refs[t], None, c), _half(out_refs[t], None, c), sibling, send_sems, recv_sems)
               for t in range(nt)]
        for cp in cps:
            cp.start()
        for t in range(nt):
            theirs = _half(out_refs[t], None, 1 - c)
            _remote(t, theirs, theirs, sibling, send_sems, recv_sems).wait_recv()
        for cp in cps:
            cp.wait_send()

    return pl.pallas_call(
        body, name=name, in_specs=[ANY] * nt, out_specs=[ANY] * nt,
        input_output_aliases={i: i for i in range(nt)},
        out_shape=[jax.ShapeDtypeStruct(f.shape, f.dtype) for f in fs],
        scratch_shapes=[pltpu.SemaphoreType.DMA((nt,)), pltpu.SemaphoreType.DMA((nt,))],
    )(*fs)


HBM = pl.BlockSpec(memory_space=pltpu.HBM)
SEM = pl.BlockSpec(memory_space=pltpu.SEMAPHORE)
EFFECT = pltpu.SideEffectType.DATAFLOW_SIDE_EFFECTING


def _split_start(arrays, after, plan, n_copies, name):
    na = len(arrays)

    def body(*refs):
        ins, send_sems, recv_sems = refs[:na], refs[na + 1], refs[na + 2]
        outs, token = refs[na + 3:2 * na + 3], refs[2 * na + 3]
        for k, (src, dst, to) in enumerate(plan(ins, outs)):
            _remote(k, src, dst, to, send_sems, recv_sems).start()
        token[...] = jnp.zeros_like(token)

    res = pl.pallas_call(
        body, name=name,
        out_shape=(pltpu.SemaphoreType.DMA((n_copies,)), pltpu.SemaphoreType.DMA((n_copies,)),
                   *[pltpu.HBM(a.shape, a.dtype) for a in arrays], jax.ShapeDtypeStruct((8, HEAD_DIM), F32)),
        in_specs=[HBM] * na + [ANY],
        out_specs=(SEM, SEM, *[HBM] * na, pl.BlockSpec(memory_space=pltpu.VMEM)),
        input_output_aliases={i: 2 + i for i in range(na)},
        compiler_params=pltpu.CompilerParams(has_side_effects=EFFECT),
    )(*[pltpu.with_memory_space_constraint(a, pltpu.HBM) for a in arrays], after)
    return res[0], res[1], list(res[2:2 + na]), res[2 + na]


def _split_wait(send_sems, recv_sems, arrays, after, plan, name):
    na = len(arrays)
    after = list(after) if isinstance(after, (list, tuple)) else [after]

    def body(*refs):
        ins, send, recv, outs = refs[:na], refs[na], refs[na + 1], refs[na + 2 + len(after):]
        for k, (src, dst, to) in enumerate(plan(ins, outs)):
            cp = _remote(k, src, dst, to, send, recv)
            cp.wait_send()
            cp.wait_recv()

    res = pl.pallas_call(
        body, name=name, out_shape=tuple(pltpu.HBM(a.shape, a.dtype) for a in arrays),
        in_specs=[HBM] * na + [SEM, SEM] + [ANY] * len(after), out_specs=tuple([HBM] * na),
        input_output_aliases={i: i for i in range(na)},
        compiler_params=pltpu.CompilerParams(has_side_effects=EFFECT),
    )(*arrays, send_sems, recv_sems, *after)
    return list(res)


def _gather_plan(nt):
    def plan(ins, outs):
        x, y, c, _, chips = _place()
        me = 2 * x + y
        return [(_half(ins[t], me, c), _half(outs[t], me, c), (px, py, c)) for px, py in chips for t in range(nt)]
    return plan


def _gather_landed_plan(nt):
    def plan(ins, outs):
        _, _, c, _, chips = _place()
        return [(_half(outs[t], 2 * px + py, c), _half(outs[t], 2 * px + py, c), (px, py, c))
                for px, py in chips for t in range(nt)]
    return plan


def _scatter_plan(nt):
    def plan(ins, outs):
        _, _, c, _, chips = _place()
        return [(ins[t].at[2 * px + py], outs[nt + t].at[j], (px, py, c))
                for j, (px, py) in enumerate(chips) for t in range(nt)]
    return plan


def _pass_plan(nt):
    def plan(ins, outs):
        _, _, c, sibling, chips = _place()
        return [(_half(ins[t], 2 * px + py, c), _half(outs[t], 2 * px + py, c), sibling)
                for px, py in chips for t in range(nt)]
    return plan


def _pass_landed_plan(nt):
    def plan(ins, outs):
        _, _, c, sibling, chips = _place()
        return [(_half(outs[t], 2 * px + py, c), _half(outs[t], 2 * px + py, 1 - c), sibling)
                for px, py in chips for t in range(nt)]
    return plan


def _swap_plan(nt):
    def plan(ins, outs):
        _, _, c, sibling, _ = _place()
        res = []
        for t in range(nt):
            half = ins[t].shape[1] // 2
            res.append((ins[t].at[:, pl.ds((1 - c) * half, half), :], outs[nt + t], sibling))
        return res
    return plan


def _pass_halves_to_sibling(slots, name):
    nt = len(slots)

    def body(*refs):
        ins, outs, (send_sems, recv_sems) = refs[:nt], refs[nt:2 * nt], refs[2 * nt:]
        _, _, c, sibling, chips = _place()
        cps = [_remote(j * nt + t, _half(ins[t], 2 * px + py, c), _half(outs[t], 2 * px + py, c), sibling,
                       send_sems, recv_sems)
               for j, (px, py) in enumerate(chips) for t in range(nt)]
        for cp in cps:
            cp.start()
        for j, (px, py) in enumerate(chips):
            for t in range(nt):
                other = _half(outs[t], 2 * px + py, 1 - c)
                _remote(j * nt + t, other, other, sibling, send_sems, recv_sems).wait_recv()
        for cp in cps:
            cp.wait_send()

    return pl.pallas_call(
        body, name=name, in_specs=[ANY] * nt, out_specs=[ANY] * nt,
        input_output_aliases={i: i for i in range(nt)},
        out_shape=[jax.ShapeDtypeStruct(a.shape, a.dtype) for a in slots],
        scratch_shapes=[pltpu.SemaphoreType.DMA((3 * nt,)), pltpu.SemaphoreType.DMA((3 * nt,))],
    )(*slots)


def _all_reduce_small(v):
    r, lanes = v.shape

    def body(v_ref, out_ref, buf, send_sems, recv_sems):
        x, y, c, sibling, chips = _place()

        def slot(px, py, pc):
            return buf.at[4 * px + 2 * py + pc]

        def copy(k, block, to, src=None):
            return pltpu.make_async_remote_copy(src_ref=slot(*block) if src is None else src, dst_ref=slot(*block),
                                                send_sem=send_sems.at[k], recv_sem=recv_sems.at[k],
                                                device_id=to, device_id_type=MESH)

        me = (x, y, c)
        buf[4 * x + 2 * y + c] = v_ref[...]
        first = [copy(0, me, sibling, src=v_ref)]
        first += [copy(1 + j, me, (*chip, c), src=v_ref) for j, chip in enumerate(chips)]
        for cp in first:
            cp.start()
        passed = [copy(4 + j, (*chip, c), sibling) for j, chip in enumerate(chips)]
        for j, chip in enumerate(chips):
            copy(1 + j, (*chip, c), me).wait_recv()
            passed[j].start()
        copy(0, (x, y, 1 - c), me).wait_recv()
        for j, chip in enumerate(chips):
            copy(4 + j, (*chip, 1 - c), me).wait_recv()
        for cp in first + passed:
            cp.wait_send()
        acc = buf[0]
        for k in range(1, 8):
            acc = acc + buf[k]
        out_ref[...] = acc

    vmem = pl.BlockSpec(memory_space=pltpu.VMEM)
    return pl.pallas_call(
        body, name="all_reduce_small", in_specs=[vmem], out_specs=vmem,
        out_shape=jax.ShapeDtypeStruct((r, lanes), F32),
        scratch_shapes=[pltpu.VMEM((8, r, lanes), F32), pltpu.SemaphoreType.DMA((7,)), pltpu.SemaphoreType.DMA((7,))],
    )(v)


def _size(shape):
    n = 1
    for d in shape:
        n *= d
    return n


def _pack_small(parts):
    rows = []
    for p in parts:
        f = p.reshape(-1).astype(F32)
        n = -(-f.shape[0] // HEAD_DIM) * HEAD_DIM
        rows.append(jnp.pad(f, (0, n - f.shape[0])).reshape(-1, HEAD_DIM))
    a = jnp.concatenate(rows, axis=0)
    return jnp.pad(a, ((0, -a.shape[0] % 8), (0, 0)))


def _unpack_small(a, shapes):
    out, row = [], 0
    for shp in shapes:
        nrows = -(-_size(shp) // HEAD_DIM)
        out.append(a[row:row + nrows].reshape(-1)[:_size(shp)].reshape(shp))
        row += nrows
    return out


SMALL = ["attn_norm_w", "a_log", "dt_bias", "delta_out_norm_w", "q_norm_w", "k_norm_w", "attn_out_norm_w", "ffn_norm_w"]
BIG = ["w_in", "w_out", "w_gate_up", "w_down"]
ORDER = ["attn_norm_w", "w_in", "conv_w", "a_log", "dt_bias", "delta_out_norm_w", "q_norm_w", "k_norm_w",
         "attn_out_norm_w", "w_out", "ffn_norm_w", "w_gate_up", "w_down"]


def kernel(x, positions, attn_norm_w, w_in, conv_w, a_log, dt_bias, delta_out_norm_w, q_norm_w, k_norm_w, attn_out_norm_w, w_out, ffn_norm_w, w_gate_up, w_down, loss_target, m_attn_norm_w, m_w_in, m_conv_w, m_a_log, m_dt_bias, m_delta_out_norm_w, m_q_norm_w, m_k_norm_w, m_attn_out_norm_w, m_w_out, m_ffn_norm_w, m_w_gate_up, m_w_down, v_attn_norm_w, v_w_in, v_conv_w, v_a_log, v_dt_bias, v_delta_out_norm_w, v_q_norm_w, v_k_norm_w, v_attn_out_norm_w, v_w_out, v_ffn_norm_w, v_w_gate_up, v_w_down):
    wts = dict(attn_norm_w=attn_norm_w, w_in=w_in, conv_w=conv_w, a_log=a_log, dt_bias=dt_bias,
               delta_out_norm_w=delta_out_norm_w, q_norm_w=q_norm_w, k_norm_w=k_norm_w,
               attn_out_norm_w=attn_out_norm_w, w_out=w_out, ffn_norm_w=ffn_norm_w, w_gate_up=w_gate_up, w_down=w_down)
    mom = dict(attn_norm_w=m_attn_norm_w, w_in=m_w_in, conv_w=m_conv_w, a_log=m_a_log, dt_bias=m_dt_bias,
               delta_out_norm_w=m_delta_out_norm_w, q_norm_w=m_q_norm_w, k_norm_w=m_k_norm_w,
               attn_out_norm_w=m_attn_out_norm_w, w_out=m_w_out, ffn_norm_w=m_ffn_norm_w, w_gate_up=m_w_gate_up,
               w_down=m_w_down)
    var = dict(attn_norm_w=v_attn_norm_w, w_in=v_w_in, conv_w=v_conv_w, a_log=v_a_log, dt_bias=v_dt_bias,
               delta_out_norm_w=v_delta_out_norm_w, q_norm_w=v_q_norm_w, k_norm_w=v_k_norm_w,
               attn_out_norm_w=v_attn_out_norm_w, w_out=v_w_out, ffn_norm_w=v_ffn_norm_w, w_gate_up=v_w_gate_up,
               w_down=v_w_down)
    dmod = x.shape[2]
    heads = dmod // (2 * HEAD_DIM)
    dw = heads * HEAD_DIM
    chip = 2 * lax.axis_index("x") + lax.axis_index("y")
    core = lax.axis_index("c")
    n_in, n_out, n_gu, n_down, n_conv = (w_in.shape[2], w_out.shape[1], w_gate_up.shape[2], w_down.shape[1],
                                         conv_w.shape[2])

    def slots_of(w, dtype):
        shard = w[0].astype(dtype)
        return lax.dynamic_update_index_in_dim(lax.empty((4,) + shard.shape, dtype), shard, chip, axis=0)

    s_in, s_conv = _all_gather_weights([slots_of(w_in, BF16)], [slots_of(conv_w, F32)], "all_gather_w_in")
    later = [slots_of(w_out, BF16), slots_of(w_gate_up, BF16), slots_of(w_down, BF16)]
    w_send, w_recv, later, started = _split_start(later, s_conv, _gather_plan(3), 9, "gather_rest_start")
    by_cols = lambda a: a.transpose(1, 0, 2).reshape(a.shape[1], 4 * a.shape[2])
    w_in_f, conv_f = by_cols(s_in), by_cols(s_conv)
    w_bd = jnp.pad(w_in_f[:, 4 * dw:4 * dw + 2 * heads], ((0, 0), (0, HEAD_DIM - 2 * heads)))
    small = {n: wts[n] for n in SMALL}
    place = jnp.stack([core, chip]).astype(jnp.int32)
    to_slots = lambda a: a.reshape(a.shape[0], 4, a.shape[1] // 4).transpose(1, 0, 2)

    def later_weights(after):
        landed = _split_wait(w_send, w_recv, later, after, _gather_landed_plan(3), "gather_rest_wait")
        s_out, = _pass_halves_to_sibling(landed[:1], "gather_out_pass")
        p_send, p_recv, passing, token = _split_start(landed[1:], s_out, _pass_plan(2), 6, "gather_ffn_pass_start")

        def ffn_weights(after):
            s_gu, s_down = _split_wait(p_send, p_recv, passing, after, _pass_landed_plan(2), "gather_ffn_pass_wait")
            return by_cols(s_gu), s_down.reshape(4 * n_down, dmod)

        return s_out.reshape(4 * n_out, dmod), token, ffn_weights

    ffn = {}

    def ffn_grads_ready(g_gate, g_up, g_down):
        gs = [to_slots(jnp.concatenate([g_gate, g_up], axis=1)), g_down.reshape(4, n_down, dmod)]
        zones = [lax.empty((4, g.shape[1] // 2, g.shape[2]), BF16) for g in gs]
        s_send, s_recv, s_bufs, swapping = _split_start(gs + zones, g_gate, _swap_plan(2), 2, "swap_ffn_start")

        def and_then(after):
            g_gu, g_dn, b_gu, b_dn = _split_wait(s_send, s_recv, s_bufs, after, _swap_plan(2), "swap_ffn_wait")
            sums = [_add_half_bf16(g_gu, b_gu, place, "chip_partial_sum_w_gate_up"),
                    _add_half_bf16(g_dn, b_dn, place, "chip_partial_sum_w_down")]
            zones3 = [lax.empty((3,) + p.shape[1:], BF16) for p in sums]
            ffn["send"], ffn["recv"], ffn["bufs"], token = _split_start(sums + zones3, b_gu, _scatter_plan(2), 6,
                                                                        "scatter_ffn_start")
            return token

        return swapping, and_then

    rest = {}

    def rest_grads_ready(bg):
        gs = [to_slots(jnp.concatenate([bg["w_qkv"], bg["w_z"], bg["w_bd"][:, :2 * heads], bg["w_attn"]], axis=1)),
              jnp.concatenate([bg["w_out_a"], bg["w_out_b"]], axis=0).reshape(4, n_out, dmod)]
        zones = [lax.empty((4, g.shape[1] // 2, g.shape[2]), BF16) for g in gs]
        s_send, s_recv, s_bufs, swapping = _split_start(gs + zones, bg["w_attn"], _swap_plan(2), 2, "swap_rest_start")

        def and_then(after):
            g_in, g_out, b_in, b_out = _split_wait(s_send, s_recv, s_bufs, after, _swap_plan(2), "swap_rest_wait")
            sums = [_add_half_bf16(g_in, b_in, place, "chip_partial_sum_w_in"),
                    _add_half_bf16(g_out, b_out, place, "chip_partial_sum_w_out")]
            zones3 = [lax.empty((3,) + p.shape[1:], BF16) for p in sums]
            rest["send"], rest["recv"], rest["bufs"], token = _split_start(sums + zones3, b_in, _scatter_plan(2), 6,
                                                                           "scatter_rest_start")
            return token

        return swapping, and_then

    loss_row, grad_x, sg = _local_step(
        x[0], positions[0], loss_target[0], small, w_in_f[:, :4 * dw], w_bd, w_in_f[:, 4 * dw + 2 * heads:], conv_f,
        later_weights, ffn_grads_ready, rest_grads_ready, after=started)

    r_send, r_recv, r_bufs = rest["send"], rest["recv"], rest["bufs"]
    sum_gu, sum_down, got_gu, got_down = _split_wait(ffn["send"], ffn["recv"], ffn["bufs"], grad_x,
                                                     _scatter_plan(2), "scatter_ffn_wait")
    g_big = dict(zip(["w_gate_up", "w_down"], _join_halves(
        [_sum4_f32(sum_gu, got_gu, place, "grad_total_w_gate_up"),
         _sum4_f32(sum_down, got_down, place, "grad_total_w_down")], "join_ffn_halves")))
    grads, deltas, new_m, new_v = {}, {}, {}, {}

    def adamw_big(n):
        shp = wts[n].shape
        d, nm, nv = _adamw(wts[n][0], g_big[n], mom[n][0], var[n][0], "adamw_" + n)
        grads[n], deltas[n], new_m[n], new_v[n] = g_big[n].reshape(shp), d.reshape(shp), nm.reshape(shp), nv.reshape(shp)
        return d

    done = [adamw_big("w_gate_up"), adamw_big("w_down")]
    sum_in, sum_out, got_in, got_out = _split_wait(r_send, r_recv, r_bufs, done, _scatter_plan(2), "scatter_rest_wait")
    g_big.update(zip(["w_in", "w_out"], _join_halves(
        [_sum4_f32(sum_in, got_in, place, "grad_total_w_in"),
         _sum4_f32(sum_out, got_out, place, "grad_total_w_out")], "join_rest_halves")))
    adamw_big("w_in")
    adamw_big("w_out")

    reduced = _all_reduce_small(_pack_small([sg[n] for n in SMALL] + [sg["conv_w"], loss_row]))
    red = _unpack_small(reduced, [wts[n].shape for n in SMALL] + [(4, 4 * n_conv), (1, HEAD_DIM)])
    g_small = dict(zip(SMALL, red[:len(SMALL)]))
    g_conv_full, loss_out = red[len(SMALL)], red[len(SMALL) + 1]
    g_small["conv_w"] = lax.dynamic_slice_in_dim(g_conv_full, chip * n_conv, n_conv, axis=1).reshape(conv_w.shape)

    names = SMALL + ["conv_w"]
    shapes = [wts[n].shape for n in names]
    d, nm, nv = _adamw(_pack_small([wts[n] for n in names]), _pack_small([g_small[n] for n in names]),
                       _pack_small([mom[n] for n in names]), _pack_small([var[n] for n in names]), "adamw_small")
    for n, dd, mm, vv in zip(names, _unpack_small(d, shapes), _unpack_small(nm, shapes), _unpack_small(nv, shapes)):
        grads[n], deltas[n], new_m[n], new_v[n] = g_small[n], dd, mm, vv
    return (loss_out[0, 0], grad_x[None], *[grads[n] for n in ORDER], *[deltas[n] for n in ORDER],
            *[new_m[n] for n in ORDER], *[new_v[n] for n in ORDER])
```

```python
import jax
import jax.numpy as jnp
from jax import lax
from jax.experimental import pallas as pl
from jax.experimental.pallas import tpu as pltpu

F32 = jnp.float32
BF16 = jnp.bfloat16
HEAD_DIM = 128
CHUNK = 128
INV_BLOCK = 64
SPAN = 128
DILATIONS = (1, 4, 16)
ROPE_THETA = 10000.0
EPS = 1e-6
NEG = -1e30
ADAM_LR, ADAM_B1, ADAM_B2, ADAM_EPS, ADAM_WD, ADAM_STEP = 0.001, 0.9, 0.999, 1e-08, 0.01, 10
VMEM_LIMIT = 48 * 1024 * 1024
MATMUL_VMEM = 44 * 1024 * 1024
MESH = pl.DeviceIdType.MESH

_DN = {"nn": (((1,), (0,)), ((), ())), "nt": (((1,), (1,)), ((), ())), "tn": (((0,), (0,)), ((), ()))}


def _dot(a, b, mode="nn"):
    (ca, cb), _ = _DN[mode]
    if a.ndim == 3:
        dn = (((ca[0] + 1,), (cb[0] + 1,)), ((0,), (0,)))
    else:
        dn = _DN[mode]
    return lax.dot_general(a.astype(BF16), b.astype(BF16), dn, preferred_element_type=F32)


def _rsum(x):
    return jnp.sum(x, axis=-1, keepdims=True)


def _csum(x):
    return jnp.sum(x, axis=-2, keepdims=True)


def _tile(dim, pref, unit=128):
    t = (min(pref, dim) // unit) * unit
    while t >= unit:
        if dim % t == 0:
            return t
        t -= unit
    return dim


def _params(sem):
    return pltpu.CompilerParams(dimension_semantics=sem, vmem_limit_bytes=VMEM_LIMIT)


def _sigmoid(x):
    return 1.0 / (1.0 + jnp.exp(-x))


def _matmul(a, b, mode, name, add=None, out_dtype=F32, a_cols=None, b_cols=None, b_rows=None,
            tiles=(), finish=None, out_dtypes=(), row_sum=False, after=None, b2_cols=None):
    if mode == "tn":
        out_dtype = BF16
    a_off, a_w = a_cols if a_cols else (0, a.shape[1])
    b_off, b_w = b_cols if b_cols else (0, b.shape[1])
    br_off, br_n = b_rows if b_rows else (0, b.shape[0])
    if mode == "nn":
        m, k, n = a.shape[0], a_w, b_w
        assert br_n == k
    elif mode == "nt":
        m, k, n = a.shape[0], a_w, b.shape[0]
        assert b_w == k
    else:
        k, m, n = a.shape[0], a_w, b_w
        assert b.shape[0] == k
    if finish is None:
        out_dtypes = [out_dtype]
        if add is None:
            finish = lambda acc, vals: (acc,)
        else:
            tiles = [(add, 0)]
            finish = lambda acc, vals: (acc + vals[0].astype(F32),)
    n_b = 1 if b2_cols is None else 2
    sides = sum(jnp.dtype(t.dtype).itemsize for t, _ in tiles) + sum(jnp.dtype(d).itemsize for d in out_dtypes)

    def vmem(tm_, tn_, tk_):
        ops = tm_ * tk_ * jnp.dtype(a.dtype).itemsize + n_b * tk_ * tn_ * jnp.dtype(b.dtype).itemsize
        return 2 * (ops + tm_ * tn_ * sides) + (n_b * tm_ * tn_ * 4 if tk_ < k else 0)

    tm, tn = _tile(m, 1024, 128), _tile(n, 1024, 128)
    tall = _tile(m, 1536, 128), _tile(n, 512, 128)
    if tall[0] * tall[1] > tm * tn and vmem(*tall, k) <= MATMUL_VMEM:
        tm, tn = tall
    tk = next(t for t in [_tile(k, p, 128) for p in (4096, 2048, 1024, 512, 128)] if vmem(tm, tn, t) <= MATMUL_VMEM)
    if mode == "nn":
        assert a_off % tk == 0 and b_off % tn == 0 and br_off % tk == 0
        a_spec = pl.BlockSpec((tm, tk), lambda i, j, kk: (i, kk + a_off // tk))
        b_spec = pl.BlockSpec((tk, tn), lambda i, j, kk: (kk + br_off // tk, j + b_off // tn))
    elif mode == "nt":
        assert a_off % tk == 0 and b_off % tk == 0
        a_spec = pl.BlockSpec((tm, tk), lambda i, j, kk: (i, kk + a_off // tk))
        b_spec = pl.BlockSpec((tn, tk), lambda i, j, kk: (j, kk + b_off // tk))
    else:
        assert a_off % tm == 0 and b_off % tn == 0
        a_spec = pl.BlockSpec((tk, tm), lambda i, j, kk: (kk, i + a_off // tm))
        b_spec = pl.BlockSpec((tk, tn), lambda i, j, kk: (kk, j + b_off // tn))
    nk = k // tk
    n_tiles, n_out = len(tiles), len(out_dtypes)
    extra = [] if after is None else [after]
    first_out = 1 + n_b + n_tiles + len(extra)

    def body(*refs):
        a_ref, b_refs = refs[0], refs[1:1 + n_b]
        tile_refs, out_refs = refs[1 + n_b:1 + n_b + n_tiles], refs[first_out:first_out + n_out]
        acc_refs = refs[-n_b:] if nk > 1 else ()
        kk = pl.program_id(2)
        first_tile = (pl.program_id(0) == 0) & (pl.program_id(1) == 0)

        av = a_ref[...]
        if nk > 1:
            @pl.when(kk == 0)
            def _():
                for acc_ref in acc_refs:
                    acc_ref[...] = jnp.zeros_like(acc_ref)

            for acc_ref, b_ref in zip(acc_refs, b_refs):
                acc_ref[...] += _dot(av, b_ref[...], mode)

        @pl.when(kk == nk - 1)
        def _():
            if nk > 1:
                accs = [acc_ref[...] for acc_ref in acc_refs]
            else:
                accs = [_dot(av, b_ref[...], mode) for b_ref in b_refs]
            res = finish(accs[0] if n_b == 1 else accs, [t[...] for t in tile_refs])
            for o_ref, r in zip(out_refs, res):
                o_ref[...] = r.astype(o_ref.dtype)
            if row_sum:
                row_ref = refs[first_out + n_out]

                @pl.when(first_tile)
                def _():
                    row_ref[...] = res[n_out]

                @pl.when(jnp.logical_not(first_tile))
                def _():
                    row_ref[...] += res[n_out]

    in_specs = [a_spec, b_spec]
    args = [a, b]
    if b2_cols is not None:
        assert mode == "nn" and b2_cols[1] == n and b2_cols[0] % tn == 0
        in_specs.append(pl.BlockSpec((tk, tn), lambda i, j, kk: (kk + br_off // tk, j + b2_cols[0] // tn)))
        args.append(b)
    for arr, off in tiles:
        assert off % tn == 0
        in_specs.append(pl.BlockSpec((tm, tn), lambda i, j, kk, off=off: (i, j + off // tn)))
        args.append(arr)
    in_specs += [ANY] * len(extra)
    args += extra
    out_specs = [pl.BlockSpec((tm, tn), lambda i, j, kk: (i, j))] * n_out
    out_shape = [jax.ShapeDtypeStruct((m, n), dt) for dt in out_dtypes]
    if row_sum:
        out_specs.append(pl.BlockSpec((1, HEAD_DIM), lambda i, j, kk: (0, 0)))
        out_shape.append(jax.ShapeDtypeStruct((1, HEAD_DIM), F32))
    res = pl.pallas_call(
        body, name=name, grid=(m // tm, n // tn, nk),
        in_specs=in_specs, out_specs=out_specs, out_shape=out_shape,
        scratch_shapes=[pltpu.VMEM((tm, tn), F32)] * (n_b if nk > 1 else 0),
        compiler_params=_params(("arbitrary",) * 3 if row_sum else ("parallel", "parallel", "arbitrary")),
    )(*args)
    return res[0] if len(res) == 1 else res


def _rmsnorm_fwd(x, w, name, after=None):
    s, d = x.shape
    tr = _tile(s, 1024, 8)

    def body(x_ref, w_ref, *rest):
        h_ref = rest[-1]
        xv = x_ref[...]
        rstd = lax.rsqrt(jnp.mean(xv * xv, axis=-1, keepdims=True) + EPS)
        h_ref[...] = (xv * rstd * w_ref[...]).astype(BF16)

    extra = [] if after is None else [after]
    return pl.pallas_call(
        body, name=name, grid=(s // tr,),
        in_specs=[pl.BlockSpec((tr, d), lambda i: (i, 0)), pl.BlockSpec((1, d), lambda i: (0, 0))] + [ANY] * len(extra),
        out_specs=pl.BlockSpec((tr, d), lambda i: (i, 0)),
        out_shape=jax.ShapeDtypeStruct((s, d), BF16),
        compiler_params=_params(("parallel",)),
    )(x, w, *extra)


def _rmsnorm_bwd(dh, x, w, res, name, after=None):
    s, d = x.shape
    tr = _tile(s, 256, 8)

    def body(dh_ref, x_ref, w_ref, res_ref, *rest):
        dx_ref, dx16_ref, dw_ref = rest[-3:]
        xv = x_ref[...]
        rstd = lax.rsqrt(jnp.mean(xv * xv, axis=-1, keepdims=True) + EPS)
        xhat = xv * rstd
        dhv = dh_ref[...]
        gw = dhv * w_ref[...]
        dx = res_ref[...] + rstd * (gw - xhat * jnp.mean(gw * xhat, axis=-1, keepdims=True))
        dx_ref[...] = dx
        dx16_ref[...] = dx.astype(BF16)

        @pl.when(pl.program_id(0) == 0)
        def _():
            dw_ref[...] = jnp.zeros_like(dw_ref)

        dw_ref[...] += jnp.sum(dhv * xhat, axis=0, keepdims=True)

    row = pl.BlockSpec((tr, d), lambda i: (i, 0))
    vec = pl.BlockSpec((1, d), lambda i: (0, 0))
    extra = [] if after is None else [after]
    return pl.pallas_call(
        body, name=name, grid=(s // tr,),
        in_specs=[row, row, vec, row] + [ANY] * len(extra), out_specs=[row, row, vec],
        out_shape=[jax.ShapeDtypeStruct((s, d), F32), jax.ShapeDtypeStruct((s, d), BF16),
                   jax.ShapeDtypeStruct((1, d), F32)],
        compiler_params=_params(("arbitrary",)),
    )(dh, x, w, res, *extra)


def _conv_taps(x, w, rows):
    shifted = [x]
    for sft in (1, 2, 3):
        shifted.append(jnp.where(rows >= sft, pltpu.roll(x, sft, 0), 0.0))
    y = w[3:4, :] * shifted[0] + w[2:3, :] * shifted[1] + w[1:2, :] * shifted[2] + w[0:1, :] * shifted[3]
    return y, shifted


def _delta_pre_fwd(qkvz, conv_w, heads):
    s = qkvz.shape[0]
    nblk = 3 * heads

    def body(x_ref, w_ref, o_ref):
        part = pl.program_id(0) // heads
        rows = lax.broadcasted_iota(jnp.int32, (s, HEAD_DIM), 0)
        y, _ = _conv_taps(x_ref[...], w_ref[...], rows)
        a = y * _sigmoid(y)
        rs = lax.rsqrt(jnp.sum(a * a, axis=-1, keepdims=True) + EPS)
        fac = jnp.where(part == 0, rs * (HEAD_DIM ** -0.5), jnp.where(part == 1, rs, 1.0))
        o_ref[...] = a * fac

    return pl.pallas_call(
        body, name="delta_pre_fwd", grid=(nblk,),
        in_specs=[pl.BlockSpec((s, HEAD_DIM), lambda i: (0, i)), pl.BlockSpec((4, HEAD_DIM), lambda i: (0, i))],
        out_specs=pl.BlockSpec((s, HEAD_DIM), lambda i: (0, i)),
        out_shape=jax.ShapeDtypeStruct((s, 3 * heads * HEAD_DIM), F32),
        compiler_params=_params(("parallel",)),
    )(qkvz, conv_w)


def _delta_pre_bwd(dqkv, qkvz, conv_w, heads):
    s = qkvz.shape[0]
    nblk = 3 * heads

    def body(d_ref, x_ref, w_ref, dx_ref, dw_ref):
        part = pl.program_id(0) // heads
        rows = lax.broadcasted_iota(jnp.int32, (s, HEAD_DIM), 0)
        w = w_ref[...]
        y, shifted = _conv_taps(x_ref[...], w, rows)
        sg = _sigmoid(y)
        a = y * sg
        rs = lax.rsqrt(jnp.sum(a * a, axis=-1, keepdims=True) + EPS)
        unit = a * rs
        dn = d_ref[...]
        scale = jnp.where(part == 0, HEAD_DIM ** -0.5, 1.0)
        da_norm = scale * rs * (dn - unit * jnp.sum(dn * unit, axis=-1, keepdims=True))
        da = jnp.where(part < 2, da_norm, dn)
        dy = da * sg * (1.0 + y * (1.0 - sg))
        dx = w[3:4, :] * dy
        for sft in (1, 2, 3):
            dx = dx + w[3 - sft:4 - sft, :] * jnp.where(rows < s - sft, pltpu.roll(dy, s - sft, 0), 0.0)
        dx_ref[...] = dx.astype(BF16)
        for sft in range(4):
            dw_ref[3 - sft:4 - sft, :] = jnp.sum(dy * shifted[sft], axis=0, keepdims=True)

    col = pl.BlockSpec((s, HEAD_DIM), lambda i: (0, i))
    wsp = pl.BlockSpec((4, HEAD_DIM), lambda i: (0, i))
    return pl.pallas_call(
        body, name="delta_pre_bwd", grid=(nblk,),
        in_specs=[col, col, wsp], out_specs=[col, wsp],
        out_shape=[jax.ShapeDtypeStruct((s, 3 * heads * HEAD_DIM), BF16),
                   jax.ShapeDtypeStruct((4, 3 * heads * HEAD_DIM), F32)],
        compiler_params=_params(("parallel",)),
    )(dqkv, qkvz, conv_w)


def _heads_of(ref, heads):
    return jnp.stack([ref[:, h * HEAD_DIM:(h + 1) * HEAD_DIM] for h in range(heads)])


def _chunk_common(q, k, v, bd, a_log, dt_bias, heads, solved=None):
    c = CHUNK
    braw = jnp.stack([bd[:, h:h + 1] for h in range(heads)])
    draw = jnp.stack([bd[:, heads + h:heads + h + 1] for h in range(heads)])
    beta = _sigmoid(braw)
    xd = draw + dt_bias
    sp = jnp.maximum(xd, 0.0) + jnp.log1p(jnp.exp(-jnp.abs(xd)))
    g = -jnp.exp(a_log) * sp
    row = lax.broadcasted_iota(jnp.int32, (c, c), 0)
    col = lax.broadcasted_iota(jnp.int32, (c, c), 1)
    sq = (heads, c, c)
    g_b = jnp.broadcast_to(g, sq)
    g_row = _csum(jnp.where(row == col, g_b, 0.0))
    gam_col = _rsum(jnp.where(col <= row, jnp.broadcast_to(g_row, sq), 0.0))
    gam_row = _csum(jnp.where(row <= col, g_b, 0.0))
    causal = row >= col
    dm = jnp.where(causal, jnp.exp(jnp.where(causal, gam_col - gam_row, 0.0)), 0.0)
    kk = _dot(k, k, "nt")
    e = jnp.exp(gam_col)
    if solved is None:
        low = jnp.where(row > col, beta * kk * dm, 0.0)
        assert c in (INV_BLOCK, 2 * INV_BLOCK)
        same = (row // INV_BLOCK) == (col // INV_BLOCK)
        diag = jnp.where(same, low, 0.0)
        t = jnp.where(row == col, 1.0, 0.0) - diag
        pw = diag
        for _ in range((INV_BLOCK - 1).bit_length() - 1):
            pw = _dot(pw, pw)
            t = t + _dot(t, pw)
        if c > INV_BLOCK:
            t = t - _dot(_dot(t, low - diag), t)
        u = _dot(t, beta * v)
        w = _dot(t, (beta * e) * k)
    else:
        t, u, w = solved
    qk_raw = _dot(q, k, "nt")
    gl = _csum(g)
    el = jnp.exp(gl - gam_col)
    return dict(beta=beta, xd=xd, g=g, row=row, col=col, dm=dm, kk=kk, t=t, e=e, u=u, w=w,
                qk_raw=qk_raw, qk=qk_raw * dm, gl=gl, el=el, qd=e * q, kd=el * k, cd=jnp.exp(gl))


def _delta_chunk_fwd(qkv, bd, a_log, dt_bias, heads):
    s = qkv.shape[0]
    n = s // CHUNK
    dw = heads * HEAD_DIM
    blk = lambda part: pl.BlockSpec((CHUNK, dw), lambda i: (i, part))

    def body(q_ref, k_ref, v_ref, bd_ref, al_ref, dt_ref, o_ref, st_ref, t_ref, uw_ref, state):
        @pl.when(pl.program_id(0) == 0)
        def _():
            state[...] = jnp.zeros_like(state)

        cm = _chunk_common(_heads_of(q_ref, heads), _heads_of(k_ref, heads), _heads_of(v_ref, heads), bd_ref[...],
                           al_ref[...], dt_ref[...], heads)
        st = state[...]
        st_ref[0] = st
        t_ref[0] = cm["t"]
        uw_ref[0, 0] = cm["u"]
        uw_ref[0, 1] = cm["w"]
        vn = cm["u"] - _dot(cm["w"], st)
        o = _dot(cm["qd"], st) + _dot(cm["qk"], vn)
        for h in range(heads):
            o_ref[:, h * HEAD_DIM:(h + 1) * HEAD_DIM] = o[h]
        state[...] = cm["cd"] * st + _dot(cm["kd"], vn, "tn")

    smem = pl.BlockSpec((heads, 1, 1), lambda i: (0, 0, 0))
    return pl.pallas_call(
        body, name="delta_chunk_fwd", grid=(n,),
        in_specs=[blk(0), blk(1), blk(2), pl.BlockSpec((CHUNK, HEAD_DIM), lambda i: (i, 0)), smem, smem],
        out_specs=[pl.BlockSpec((CHUNK, dw), lambda i: (i, 0)),
                   pl.BlockSpec((1, heads, HEAD_DIM, HEAD_DIM), lambda i: (i, 0, 0, 0)),
                   pl.BlockSpec((1, heads, CHUNK, CHUNK), lambda i: (i, 0, 0, 0)),
                   pl.BlockSpec((1, 2, heads, CHUNK, HEAD_DIM), lambda i: (i, 0, 0, 0, 0))],
        out_shape=[jax.ShapeDtypeStruct((s, dw), F32),
                   jax.ShapeDtypeStruct((n, heads, HEAD_DIM, HEAD_DIM), F32),
                   jax.ShapeDtypeStruct((n, heads, CHUNK, CHUNK), F32),
                   jax.ShapeDtypeStruct((n, 2, heads, CHUNK, HEAD_DIM), F32)],
        scratch_shapes=[pltpu.VMEM((heads, HEAD_DIM, HEAD_DIM), F32)],
        compiler_params=_params(("arbitrary",)),
    )(qkv, qkv, qkv, bd, a_log, dt_bias)


def _delta_chunk_bwd(do, qkv, bd, saved, a_log, dt_bias, heads):
    s = qkv.shape[0]
    n = s // CHUNK
    dw = heads * HEAD_DIM
    c = CHUNK
    blk = lambda part: pl.BlockSpec((CHUNK, dw), lambda i: (n - 1 - i, part))

    def all_heads(q, k, v, dov, st, solved, dsn, bd, a_log, dt_bias):
        cm = _chunk_common(q, k, v, bd, a_log, dt_bias, heads, solved)
        beta, e, dm, row, col = cm["beta"], cm["e"], cm["dm"], cm["row"], cm["col"]
        sq = (heads, c, c)
        vn = cm["u"] - _dot(cm["w"], st)
        dvn = _dot(cm["kd"], dsn)
        dkd = _dot(vn, dsn, "nt")
        dcd = _csum(_rsum(st * dsn))
        ds = cm["cd"] * dsn
        dqd = _dot(dov, st, "nt")
        ds = ds + _dot(cm["qd"], dov, "tn")
        dqk = _dot(dov, vn, "nt")
        dvn = dvn + _dot(cm["qk"], dov, "tn")
        dw_ = -_dot(dvn, st, "nt")
        ds = ds - _dot(cm["w"], dvn, "tn")
        drhs_u = _dot(cm["t"], dvn, "tn")
        drhs_w = _dot(cm["t"], dw_, "tn")
        da = -(_dot(drhs_u, cm["u"], "nt") + _dot(drhs_w, cm["w"], "nt"))
        dl = jnp.where(row > col, da, 0.0)
        dbeta = _rsum(dl * cm["kk"] * dm)
        dkk = dl * beta * dm
        dd = dl * beta * cm["kk"]
        dv = beta * drhs_u
        ek = e * k
        dbeta = dbeta + _rsum(drhs_u * v) + _rsum(drhs_w * ek)
        dk = (beta * e) * drhs_w
        dgam = _rsum(drhs_w * (beta * ek))
        dqkm = dqk * dm
        dq = _dot(dqkm, k)
        dk = dk + _dot(dqkm, q, "tn")
        dd = dd + dqk * cm["qk_raw"]
        dk = dk + _dot(dkk, k) + _dot(dkk, k, "tn")
        dq = dq + e * dqd
        dgam = dgam + _rsum(dqd * cm["qd"])
        dk = dk + cm["el"] * dkd
        r = _rsum(dkd * cm["kd"])
        dgam = dgam - r
        dgl = _csum(r) + dcd * cm["cd"]
        mm = dd * dm
        colsum_c = _rsum(jnp.where(row == col, jnp.broadcast_to(_csum(mm), sq), 0.0))
        dgam = dgam + _rsum(mm) - colsum_c
        ridx = lax.broadcasted_iota(jnp.int32, (c, 1), 0)
        dgam = dgam + jnp.where(ridx == c - 1, dgl, 0.0)
        dgam_row = _csum(jnp.where(row == col, jnp.broadcast_to(dgam, sq), 0.0))
        dg = _rsum(jnp.where(col >= row, jnp.broadcast_to(dgam_row, sq), 0.0))
        d_xd = dg * (-jnp.exp(a_log)) * _sigmoid(cm["xd"])
        d_braw = dbeta * beta * (1.0 - beta)
        d_alog = dg * cm["g"]
        lane = lax.broadcasted_iota(jnp.int32, (c, HEAD_DIM), 1)
        dbd = jnp.zeros((c, HEAD_DIM), F32)
        for h in range(heads):
            dbd = (dbd + jnp.where(lane == h, d_braw[h], 0.0) + jnp.where(lane == h + heads, d_xd[h], 0.0)
                   + jnp.where(lane == h + 2 * heads, d_alog[h], 0.0))
        return dq, dk, dv, ds, dbd

    def body(do_ref, q_ref, k_ref, v_ref, bd_ref, st_ref, t_ref, uw_ref, al_ref, dt_ref, dqkv_ref, dbd_ref, dstate):
        @pl.when(pl.program_id(0) == 0)
        def _():
            dstate[...] = jnp.zeros_like(dstate)

        dq, dk, dv, ds, dbd = all_heads(_heads_of(q_ref, heads), _heads_of(k_ref, heads), _heads_of(v_ref, heads),
                                        _heads_of(do_ref, heads), st_ref[0], (t_ref[0], uw_ref[0, 0], uw_ref[0, 1]),
                                        dstate[...], bd_ref[...],
                                        al_ref[...], dt_ref[...])
        for part, val in enumerate((dq, dk, dv)):
            for h in range(heads):
                lo = part * dw + h * HEAD_DIM
                dqkv_ref[:, lo:lo + HEAD_DIM] = val[h]
        dstate[...] = ds
        dbd_ref[...] = dbd

    smem = pl.BlockSpec((heads, 1, 1), lambda i: (0, 0, 0))
    shared = pl.BlockSpec((CHUNK, HEAD_DIM), lambda i: (n - 1 - i, 0))
    wide = pl.BlockSpec((CHUNK, dw), lambda i: (n - 1 - i, 0))
    return pl.pallas_call(
        body, name="delta_chunk_bwd", grid=(n,),
        in_specs=[wide, blk(0), blk(1), blk(2), shared,
                  pl.BlockSpec((1, heads, HEAD_DIM, HEAD_DIM), lambda i: (n - 1 - i, 0, 0, 0)),
                  pl.BlockSpec((1, heads, CHUNK, CHUNK), lambda i: (n - 1 - i, 0, 0, 0)),
                  pl.BlockSpec((1, 2, heads, CHUNK, HEAD_DIM), lambda i: (n - 1 - i, 0, 0, 0, 0)), smem, smem],
        out_specs=[pl.BlockSpec((CHUNK, 3 * dw), lambda i: (n - 1 - i, 0)), shared],
        out_shape=[jax.ShapeDtypeStruct((s, 3 * dw), F32), jax.ShapeDtypeStruct((s, HEAD_DIM), F32)],
        scratch_shapes=[pltpu.VMEM((heads, HEAD_DIM, HEAD_DIM), F32)],
        compiler_params=_params(("arbitrary",)),
    )(do, qkv, qkv, qkv, bd, *saved, a_log, dt_bias)


def _delta_post_fwd(o, qkvz, w, heads):
    s = o.shape[0]
    tr = _tile(s, 1024, 8)

    def body(o_ref, z_ref, w_ref, out_ref):
        ov, z = o_ref[...], z_ref[...]
        rstd = lax.rsqrt(jnp.mean(ov * ov, axis=-1, keepdims=True) + EPS)
        out_ref[...] = (ov * rstd * w_ref[...] * (z * _sigmoid(z))).astype(BF16)

    return pl.pallas_call(
        body, name="delta_post_fwd", grid=(s // tr, heads),
        in_specs=[pl.BlockSpec((tr, HEAD_DIM), lambda i, h: (i, h)),
                  pl.BlockSpec((tr, HEAD_DIM), lambda i, h: (i, 3 * heads + h)),
                  pl.BlockSpec((1, HEAD_DIM), lambda i, h: (0, 0))],
        out_specs=pl.BlockSpec((tr, HEAD_DIM), lambda i, h: (i, h)),
        out_shape=jax.ShapeDtypeStruct((s, heads * HEAD_DIM), BF16),
        compiler_params=_params(("parallel", "parallel")),
    )(o, qkvz, w)


def _delta_post_bwd(dmix, o, qkvz, w, heads):
    s = o.shape[0]
    tr = _tile(s, 1024, 8)

    def body(d_ref, o_ref, z_ref, w_ref, do_ref, dz_ref, dw_ref):
        d, ov, z, wv = d_ref[...], o_ref[...], z_ref[...], w_ref[...]
        sg = _sigmoid(z)
        rstd = lax.rsqrt(jnp.mean(ov * ov, axis=-1, keepdims=True) + EPS)
        ohat = ov * rstd
        dz_ref[...] = (d * (ohat * wv) * sg * (1.0 + z * (1.0 - sg))).astype(BF16)
        dn = d * (z * sg)
        gw = dn * wv
        do_ref[...] = rstd * (gw - ohat * jnp.mean(gw * ohat, axis=-1, keepdims=True))

        @pl.when((pl.program_id(0) == 0) & (pl.program_id(1) == 0))
        def _():
            dw_ref[...] = jnp.zeros_like(dw_ref)

        dw_ref[...] += jnp.sum(dn * ohat, axis=0, keepdims=True)

    head = pl.BlockSpec((tr, HEAD_DIM), lambda i, h: (i, h))
    vec = pl.BlockSpec((1, HEAD_DIM), lambda i, h: (0, 0))
    dw = heads * HEAD_DIM
    return pl.pallas_call(
        body, name="delta_post_bwd", grid=(s // tr, heads),
        in_specs=[head, head, pl.BlockSpec((tr, HEAD_DIM), lambda i, h: (i, 3 * heads + h)), vec],
        out_specs=[head, head, vec],
        out_shape=[jax.ShapeDtypeStruct((s, dw), F32), jax.ShapeDtypeStruct((s, dw), BF16),
                   jax.ShapeDtypeStruct((1, HEAD_DIM), F32)],
        compiler_params=_params(("arbitrary", "arbitrary")),
    )(dmix, o, qkvz, w)


def _rope_tables(positions, s):
    half = HEAD_DIM // 2
    inv_freq = ROPE_THETA ** (-jnp.arange(half, dtype=F32) / half)
    ang = positions.reshape(s, 1).astype(F32) * inv_freq
    cos, sin = jnp.cos(ang), jnp.sin(ang)
    return jnp.concatenate([cos, cos], axis=-1), jnp.concatenate([-sin, sin], axis=-1)


def _attn_pre_fwd(aqkv, wq, wk, cosf, sinf, heads):
    s = aqkv.shape[0]
    tr = _tile(s, 1024, 8)

    def body(x_ref, wq_ref, wk_ref, c_ref, s_ref, o_ref):
        xv = x_ref[...]
        wv = jnp.where(pl.program_id(1) < heads, wq_ref[...], wk_ref[...])
        y = xv * lax.rsqrt(jnp.mean(xv * xv, axis=-1, keepdims=True) + EPS) * wv
        o_ref[...] = y * c_ref[...] + pltpu.roll(y, HEAD_DIM // 2, 1) * s_ref[...]

    blk = pl.BlockSpec((tr, HEAD_DIM), lambda i, j: (i, j))
    vec = pl.BlockSpec((1, HEAD_DIM), lambda i, j: (0, 0))
    tab = pl.BlockSpec((tr, HEAD_DIM), lambda i, j: (i, 0))
    return pl.pallas_call(
        body, name="attn_pre_fwd", grid=(s // tr, 2 * heads),
        in_specs=[blk, vec, vec, tab, tab], out_specs=blk,
        out_shape=jax.ShapeDtypeStruct((s, 2 * heads * HEAD_DIM), F32),
        compiler_params=_params(("parallel", "parallel")),
    )(aqkv, wq, wk, cosf, sinf)


def _band():
    qi = lax.broadcasted_iota(jnp.int32, (SPAN, 2 * SPAN), 0)
    ki = lax.broadcasted_iota(jnp.int32, (SPAN, 2 * SPAN), 1)
    dist = qi + SPAN - ki
    return (dist >= 0) & (dist <= SPAN), ki >= SPAN


def _sub(g, r, d):
    if d == 1:
        return pl.ds(g * SPAN, SPAN)
    return pl.ds(g * SPAN * d + r, SPAN, stride=d)


def _attn_blocks(s):
    assert s % (SPAN * max(DILATIONS)) == 0
    return [(p_i, d, r, g) for p_i, d in enumerate(DILATIONS) for r in range(d) for g in range(s // (SPAN * d))]


def _attn_fwd(qk, aqkv, w, heads):
    s = qk.shape[0]
    aw = heads * HEAD_DIM

    def body(q_ref, k_ref, v_ref, w_ref, mix_ref, acc_ref, m_ref, l_ref):
        band, own = _band()
        for p_i, d, r, g in _attn_blocks(s):
            rows = _sub(g, r, d)
            kc, vc = k_ref[rows, :], v_ref[rows, :]
            if g == 0:
                kp, vp, mask = kc, vc, band & own
            else:
                mask = band
            kcat = jnp.concatenate([kp, kc], axis=0)
            vcat = jnp.concatenate([vp, vc], axis=0)
            kp, vp = kc, vc
            sc = _dot(q_ref[rows, :] * (HEAD_DIM ** -0.5), kcat, "nt")
            sc = jnp.where(mask, sc, NEG)
            m = jnp.max(sc, axis=-1, keepdims=True)
            if p_i == 0:
                p = jnp.exp(sc - m)
                acc_ref[rows, :] = _dot(p, vcat)
                l_new = jnp.sum(p, axis=-1, keepdims=True)
            else:
                m_old = m_ref[rows, 0:1]
                m = jnp.maximum(m, m_old)
                alpha = jnp.exp(m_old - m)
                p = jnp.exp(sc - m)
                acc_ref[rows, :] = alpha * acc_ref[rows, :] + _dot(p, vcat)
                l_new = alpha * l_ref[rows, 0:1] + jnp.sum(p, axis=-1, keepdims=True)
            m_ref[rows, :] = jnp.broadcast_to(m, (SPAN, HEAD_DIM))
            l_ref[rows, :] = jnp.broadcast_to(l_new, (SPAN, HEAD_DIM))
        den = l_ref[...]
        ob = acc_ref[...] / den
        acc_ref[...] = ob
        m_ref[...] = m_ref[...] + jnp.log(den)
        rstd = lax.rsqrt(jnp.mean(ob * ob, axis=-1, keepdims=True) + EPS)
        mix_ref[...] = (ob * rstd * w_ref[...]).astype(BF16)

    col = lambda off: pl.BlockSpec((s, HEAD_DIM), lambda h: (0, off + h))
    return pl.pallas_call(
        body, name="attn_fwd", grid=(heads,),
        in_specs=[col(0), col(heads), col(2 * heads), pl.BlockSpec((1, HEAD_DIM), lambda h: (0, 0))],
        out_specs=[col(0), col(0), col(0)],
        out_shape=[jax.ShapeDtypeStruct((s, aw), BF16), jax.ShapeDtypeStruct((s, aw), F32),
                   jax.ShapeDtypeStruct((s, aw), F32)],
        scratch_shapes=[pltpu.VMEM((s, HEAD_DIM), F32)],
        compiler_params=_params(("parallel",)),
    )(qk, qk, aqkv, w)


def _attn_merge_bwd(dmix, ob, w, heads):
    s = ob.shape[0]
    tr = _tile(s, 1024, 8)

    def body(d_ref, ob_ref, w_ref, do_ref, dsum_ref, dw_ref):
        d, ov = d_ref[...], ob_ref[...]
        rstd = lax.rsqrt(jnp.mean(ov * ov, axis=-1, keepdims=True) + EPS)
        ohat = ov * rstd
        gw = d * w_ref[...]
        dov = rstd * (gw - ohat * jnp.mean(gw * ohat, axis=-1, keepdims=True))
        do_ref[...] = dov
        dsum_ref[...] = jnp.broadcast_to(jnp.sum(dov * ov, axis=-1, keepdims=True), dov.shape)

        @pl.when((pl.program_id(0) == 0) & (pl.program_id(1) == 0))
        def _():
            dw_ref[...] = jnp.zeros_like(dw_ref)

        dw_ref[...] += jnp.sum(d * ohat, axis=0, keepdims=True)

    blk = pl.BlockSpec((tr, HEAD_DIM), lambda i, h: (i, h))
    vec = pl.BlockSpec((1, HEAD_DIM), lambda i, h: (0, 0))
    aw = heads * HEAD_DIM
    return pl.pallas_call(
        body, name="attn_merge_bwd", grid=(s // tr, heads),
        in_specs=[pl.BlockSpec((tr, HEAD_DIM), lambda i, h: (i, heads + h)), blk, vec],
        out_specs=[blk, blk, vec],
        out_shape=[jax.ShapeDtypeStruct((s, aw), F32), jax.ShapeDtypeStruct((s, aw), F32),
                   jax.ShapeDtypeStruct((1, HEAD_DIM), F32)],
        compiler_params=_params(("arbitrary", "arbitrary")),
    )(dmix, ob, w)


def _attn_bwd(qk, aqkv, do, lse, dsum, heads):
    s = qk.shape[0]
    aw = heads * HEAD_DIM
    scale = HEAD_DIM ** -0.5

    def body(q_ref, k_ref, v_ref, do_ref, l_ref, ds_ref, out_ref):
        band, own = _band()
        dq_ref, dk_ref, dv_ref = out_ref.at[0], out_ref.at[1], out_ref.at[2]
        out_ref[...] = jnp.zeros((3, s, HEAD_DIM), F32)
        for _, d, r, g in _attn_blocks(s):
            rows = _sub(g, r, d)
            qs, dov = q_ref[rows, :] * scale, do_ref[rows, :]
            kc, vc = k_ref[rows, :], v_ref[rows, :]
            if g == 0:
                kp, vp, mask = kc, vc, band & own
            else:
                mask = band
            kcat = jnp.concatenate([kp, kc], axis=0)
            vcat = jnp.concatenate([vp, vc], axis=0)
            p = jnp.where(mask, jnp.exp(_dot(qs, kcat, "nt") - l_ref[rows, 0:1]), 0.0)
            dsc = p * (_dot(dov, vcat, "nt") - ds_ref[rows, 0:1])
            dq_ref[rows, :] += scale * _dot(dsc, kcat)
            dk = _dot(dsc, qs, "tn")
            dv = _dot(p, dov, "tn")
            if g > 0:
                dk_ref[prows, :] += dk_own + dk[:SPAN]
                dv_ref[prows, :] += dv_own + dv[:SPAN]
            dk_own, dv_own = dk[SPAN:], dv[SPAN:]
            if g == s // (SPAN * d) - 1:
                dk_ref[rows, :] += dk_own
                dv_ref[rows, :] += dv_own
            kp, vp, prows = kc, vc, rows

    col = lambda off: pl.BlockSpec((s, HEAD_DIM), lambda h: (0, off + h))
    return pl.pallas_call(
        body, name="attn_bwd", grid=(heads,),
        in_specs=[col(0), col(heads), col(2 * heads), col(0), col(0), col(0)],
        out_specs=pl.BlockSpec((3, s, HEAD_DIM), lambda h: (0, 0, h)),
        out_shape=jax.ShapeDtypeStruct((3, s, aw), F32),
        compiler_params=_params(("parallel",)),
    )(qk, qk, aqkv, do, lse, dsum)


def _attn_pre_bwd(grads, aqkv, wq, wk, cosf, sinf, heads):
    s = aqkv.shape[0]
    tr = _tile(s, 1024, 8)
    nrow = s // tr

    def body(g_ref, x_ref, wq_ref, wk_ref, c_ref, s_ref, dx_ref, dwq_ref, dwk_ref):
        i, j = pl.program_id(0), pl.program_id(1)
        kind = j // heads
        dout = g_ref[0]
        tot_v = dout
        dy = dout * c_ref[...] + pltpu.roll(dout * s_ref[...], HEAD_DIM // 2, 1)
        xv = x_ref[...]
        wv = jnp.where(kind == 0, wq_ref[...], wk_ref[...])
        rstd = lax.rsqrt(jnp.mean(xv * xv, axis=-1, keepdims=True) + EPS)
        xhat = xv * rstd
        gw = dy * wv
        dxn = rstd * (gw - xhat * jnp.mean(gw * xhat, axis=-1, keepdims=True))
        dx_ref[...] = jnp.where(kind == 2, tot_v, dxn).astype(BF16)
        dwc = jnp.sum(dy * xhat, axis=0, keepdims=True)

        @pl.when((i == 0) & (j == 0))
        def _():
            dwq_ref[...] = jnp.zeros_like(dwq_ref)
            dwk_ref[...] = jnp.zeros_like(dwk_ref)

        @pl.when(kind == 0)
        def _():
            dwq_ref[...] += dwc

        @pl.when(kind == 1)
        def _():
            dwk_ref[...] += dwc

    grad = pl.BlockSpec((1, tr, HEAD_DIM), lambda i, j: (j // heads, i, j % heads))
    blk = pl.BlockSpec((tr, HEAD_DIM), lambda i, j: (i, j))
    vec = pl.BlockSpec((1, HEAD_DIM), lambda i, j: (0, 0))
    tab = pl.BlockSpec((tr, HEAD_DIM), lambda i, j: (i, 0))
    return pl.pallas_call(
        body, name="attn_pre_bwd", grid=(nrow, 3 * heads),
        in_specs=[grad, blk, vec, vec, tab, tab], out_specs=[blk, vec, vec],
        out_shape=[jax.ShapeDtypeStruct((s, 3 * heads * HEAD_DIM), BF16),
                   jax.ShapeDtypeStruct((1, HEAD_DIM), F32), jax.ShapeDtypeStruct((1, HEAD_DIM), F32)],
        compiler_params=_params(("arbitrary", "arbitrary")),
    )(grads, aqkv, wq, wk, cosf, sinf)


def _colsum(a, name):
    s, d = a.shape
    tr = _tile(s, 1024, 8)

    def body(a_ref, o_ref):
        @pl.when(pl.program_id(0) == 0)
        def _():
            o_ref[...] = jnp.zeros_like(o_ref)

        o_ref[...] += jnp.sum(a_ref[...], axis=0, keepdims=True)

    return pl.pallas_call(
        body, name=name, grid=(s // tr,),
        in_specs=[pl.BlockSpec((tr, d), lambda i: (i, 0))], out_specs=pl.BlockSpec((1, d), lambda i: (0, 0)),
        out_shape=jax.ShapeDtypeStruct((1, d), F32),
        compiler_params=_params(("arbitrary",)),
    )(a)


def _local_step(x, positions, target, small, w_qkvz, w_bd, w_attn, conv_w, later_weights, ffn_grads_ready,
                rest_grads_ready, after=None):
    s, dmod = x.shape
    heads = dmod // (2 * HEAD_DIM)
    dw = heads * HEAD_DIM
    a_log, dt_bias = small["a_log"].reshape(heads, 1, 1), small["dt_bias"].reshape(heads, 1, 1)
    cosf, sinf = _rope_tables(positions, s)

    h1 = _rmsnorm_fwd(x, small["attn_norm_w"], "norm1_fwd", after=after)
    qkvz = _matmul(h1, w_qkvz, "nn", "proj_qkvz")
    bd = _matmul(h1, w_bd, "nn", "proj_bd")
    aqkv = _matmul(h1, w_attn, "nn", "proj_attn")
    dqkv = _delta_pre_fwd(qkvz, conv_w, heads)
    o_d, *saved = _delta_chunk_fwd(dqkv, bd, a_log, dt_bias, heads)
    mix_a = _delta_post_fwd(o_d, qkvz, small["delta_out_norm_w"], heads)
    qk_rot = _attn_pre_fwd(aqkv, small["q_norm_w"], small["k_norm_w"], cosf, sinf, heads)
    mix_b, ob, lse = _attn_fwd(qk_rot, aqkv, small["attn_out_norm_w"], heads)
    w_out, behind, ffn_weights = later_weights((mix_a, mix_b))
    x1 = _matmul(mix_a, w_out, "nn", "out_proj_a", add=x, b_rows=(0, dw), after=behind)
    x1 = _matmul(mix_b, w_out, "nn", "out_proj_b", add=x1, b_rows=(dw, dw))
    h2 = _rmsnorm_fwd(x1, small["ffn_norm_w"], "norm2_fwd")
    w_gu, w_down = ffn_weights(h2)
    ff = w_down.shape[0]

    def swiglu(accs, vals):
        g, u = accs
        return g, u, g * _sigmoid(g) * u

    gate, up, act = _matmul(h2, w_gu, "nn", "ffn_gate_up", b_cols=(0, ff), b2_cols=(ff, ff), finish=swiglu,
                            out_dtypes=[BF16, BF16, BF16])

    def loss_head(acc, vals):
        err = acc + vals[0] - vals[1]
        part = 0.5 * jnp.sum(jnp.sum(err * err, axis=-1, keepdims=True) * (1.0 / dmod), axis=0, keepdims=True)
        lane = lax.broadcasted_iota(jnp.int32, (1, HEAD_DIM), 1)
        return err * (1.0 / dmod), err * (1.0 / dmod), jnp.where(lane == 0, part, 0.0)

    dy, dy16, loss_row = _matmul(act, w_down, "nn", "ffn_down", tiles=[(x1, 0), (target, 0)], finish=loss_head,
                                 out_dtypes=[F32, BF16], row_sum=True)

    def swiglu_bwd(acc, vals):
        g, u = vals[0].astype(F32), vals[1].astype(F32)
        sg = _sigmoid(g)
        return acc * u * sg * (1.0 + g * (1.0 - sg)), acc * g * sg

    dgate, dup = _matmul(dy16, w_down, "nt", "ffn_down_dx", tiles=[(gate, 0), (up, 0)], finish=swiglu_bwd,
                         out_dtypes=[BF16, BF16])
    g_w_down = _matmul(act, dy16, "tn", "ffn_down_dw")
    g_w_gate = _matmul(h2, dgate, "tn", "ffn_gate_dw")
    g_w_up = _matmul(h2, dup, "tn", "ffn_up_dw")
    behind, and_then = ffn_grads_ready(g_w_gate, g_w_up, g_w_down)
    dh2 = _matmul(dgate, w_gu, "nt", "ffn_gate_dx", b_cols=(0, ff), after=behind)
    dh2 = _matmul(dup, w_gu, "nt", "ffn_up_dx", b_cols=(ff, ff), add=dh2)
    behind = and_then(dh2)
    dx1, dx1_16, g_ffn_norm = _rmsnorm_bwd(dh2, x1, small["ffn_norm_w"], dy, "norm2_bwd", after=behind)
    dmix = _matmul(dx1_16, w_out, "nt", "out_proj_dx")
    g_w_out_a = _matmul(mix_a, dx1_16, "tn", "out_proj_dw_a")
    g_w_out_b = _matmul(mix_b, dx1_16, "tn", "out_proj_dw_b")
    dob, dsum, g_attn_out_norm = _attn_merge_bwd(dmix, ob, small["attn_out_norm_w"], heads)
    grads = _attn_bwd(qk_rot, aqkv, dob, lse, dsum, heads)
    d_aqkv, g_q_norm, g_k_norm = _attn_pre_bwd(grads, aqkv, small["q_norm_w"], small["k_norm_w"], cosf, sinf, heads)
    do_d, dz, g_delta_out_norm = _delta_post_bwd(dmix, o_d, qkvz, small["delta_out_norm_w"], heads)
    ddqkv, dbd = _delta_chunk_bwd(do_d, dqkv, bd, saved, a_log, dt_bias, heads)
    d_qkv_raw, g_conv = _delta_pre_bwd(ddqkv, qkvz, conv_w, heads)
    bd_sums = _colsum(dbd, "bd_colsum")
    g_w_qkv = _matmul(h1, d_qkv_raw, "tn", "proj_qkv_dw")
    g_w_z = _matmul(h1, dz, "tn", "proj_z_dw")
    g_w_bd = _matmul(h1, dbd, "tn", "proj_bd_dw")
    g_w_attn = _matmul(h1, d_aqkv, "tn", "proj_attn_dw")
    behind, and_then = rest_grads_ready(dict(w_qkv=g_w_qkv, w_z=g_w_z, w_bd=g_w_bd, w_attn=g_w_attn,
                                             w_out_a=g_w_out_a, w_out_b=g_w_out_b))
    dh1 = _matmul(d_qkv_raw, w_qkvz, "nt", "proj_qkv_dx", b_cols=(0, 3 * dw), after=behind)
    dh1 = _matmul(dz, w_qkvz, "nt", "proj_z_dx", b_cols=(3 * dw, dw), add=dh1, after=and_then(dh1))
    dh1 = _matmul(d_aqkv, w_attn, "nt", "proj_attn_dx", add=dh1)
    dh1 = _matmul(dbd, w_bd, "nt", "proj_bd_dx", add=dh1)
    grad_x, _, g_attn_norm = _rmsnorm_bwd(dh1, x, small["attn_norm_w"], dx1, "norm1_bwd")
    small_grads = dict(
        attn_norm_w=g_attn_norm, a_log=bd_sums[:, 2 * heads:3 * heads], dt_bias=bd_sums[:, heads:2 * heads],
        delta_out_norm_w=g_delta_out_norm, q_norm_w=g_q_norm, k_norm_w=g_k_norm,
        attn_out_norm_w=g_attn_out_norm, ffn_norm_w=g_ffn_norm, conv_w=g_conv)
    return loss_row, grad_x, small_grads


def _adamw(w, g, m, v, name):
    r, c = w.shape
    tr = _tile(r, 256, 8)

    def body(w_ref, g_ref, m_ref, v_ref, d_ref, nm_ref, nv_ref):
        gv = g_ref[...]
        nm = ADAM_B1 * m_ref[...] + (1.0 - ADAM_B1) * gv
        nv = ADAM_B2 * v_ref[...] + (1.0 - ADAM_B2) * (gv * gv)
        m_hat = nm / (1.0 - ADAM_B1 ** ADAM_STEP)
        v_hat = nv / (1.0 - ADAM_B2 ** ADAM_STEP)
        d_ref[...] = -ADAM_LR * (m_hat / (jnp.sqrt(v_hat) + ADAM_EPS) + ADAM_WD * w_ref[...])
        nm_ref[...] = nm
        nv_ref[...] = nv

    blk = pl.BlockSpec((tr, c), lambda i: (i, 0))
    return pl.pallas_call(
        body, name=name, grid=(r // tr,),
        in_specs=[blk] * 4, out_specs=[blk] * 3,
        out_shape=[jax.ShapeDtypeStruct((r, c), F32)] * 3,
        compiler_params=_params(("parallel",)),
    )(w, g, m, v)


def _add_half_bf16(g, b, place, name):
    n, half, c = b.shape
    tr = _tile(half, 512, 16)
    nb = half // tr

    def body(place_ref, g_ref, b_ref, o_ref):
        o_ref[...] = (g_ref[...].astype(F32) + b_ref[...].astype(F32)).astype(BF16)

    blk = pl.BlockSpec((1, tr, c), lambda i, j, p: (i, j, 0))
    return pl.pallas_call(
        body, name=name,
        grid_spec=pltpu.PrefetchScalarGridSpec(
            num_scalar_prefetch=1, grid=(n, nb),
            in_specs=[pl.BlockSpec((1, tr, c), lambda i, j, p: (i, p[0] * nb + j, 0)), blk], out_specs=blk),
        out_shape=jax.ShapeDtypeStruct((n, half, c), BF16),
        compiler_params=_params(("parallel", "parallel")),
    )(place, g, b)


def _sum4_f32(mine, others, place, name):
    _, half, c = mine.shape
    tr = _tile(half, 512, 16)
    nb = half // tr

    def body(place_ref, a_ref, b_ref, o_ref):
        acc = a_ref[0].astype(F32)
        for j in range(3):
            acc = acc + b_ref[j].astype(F32)
        o_ref[...] = acc

    return pl.pallas_call(
        body, name=name,
        grid_spec=pltpu.PrefetchScalarGridSpec(
            num_scalar_prefetch=1, grid=(nb,),
            in_specs=[pl.BlockSpec((1, tr, c), lambda i, p: (p[1], i, 0)),
                      pl.BlockSpec((3, tr, c), lambda i, p: (0, i, 0))],
            out_specs=pl.BlockSpec((tr, c), lambda i, p: (p[0] * nb + i, 0))),
        out_shape=jax.ShapeDtypeStruct((2 * half, c), F32),
        compiler_params=_params(("parallel",)),
    )(place, mine, others)


def _place():
    x, y, c = lax.axis_index("x"), lax.axis_index("y"), lax.axis_index("c")
    other_chips = [(1 - x, y), (x, 1 - y), (1 - x, 1 - y)]
    return x, y, c, (x, y, 1 - c), other_chips


ANY = pl.BlockSpec(memory_space=pl.ANY)


def _remote(k, src, dst, to, send_sems, recv_sems):
    return pltpu.make_async_remote_copy(src_ref=src, dst_ref=dst, send_sem=send_sems.at[k], recv_sem=recv_sems.at[k],
                                        device_id=to, device_id_type=MESH)


def _half(ref, lead, hc):
    if lead is None:
        half = ref.shape[0] // 2
        return ref.at[pl.ds(hc * half, half), :]
    half = ref.shape[1] // 2
    return ref.at[lead, pl.ds(hc * half, half), :]


def _all_gather_weights(slots, whole, name):
    nt, nw = len(slots), len(whole)
    base_w, base_f, base_d = 2 * nt, 2 * nt + 3 * nw, 4 * nt + 3 * nw

    def quarter(ref, lead, hc, q):
        quart = ref.shape[1] // 4
        return ref.at[lead, pl.ds((2 * hc + q) * quart, quart), :]

    def body(*refs):
        ins, outs = refs[:nt + nw], refs[nt + nw:2 * (nt + nw)]
        sems = refs[2 * (nt + nw):]
        x, y, c, sibling, chips = _place()
        me, xn, yn, dg = 2 * x + y, 2 * (1 - x) + y, 2 * x + 1 - y, 2 * (1 - x) + 1 - y
        to_x, to_y = (1 - x, y, c), (x, 1 - y, c)
        cps = []
        for t in range(nt):
            cps.append(_remote(2 * t, _half(ins[t], me, c), _half(outs[t], me, c), to_x, *sems))
            cps.append(_remote(2 * t + 1, _half(ins[t], me, c), _half(outs[t], me, c), to_y, *sems))
        for j, (px, py) in enumerate(chips):
            for t in range(nw):
                cps.append(_remote(base_w + j * nw + t, ins[nt + t].at[me], outs[nt + t].at[me], (px, py, c), *sems))
        for cp in cps:
            cp.start()

        def start(k, ref, to):
            cp = _remote(k, ref, ref, to, *sems)
            cp.start()
            cps.append(cp)

        for t in range(nt):
            landed = _half(outs[t], xn, c)
            _remote(2 * t, landed, landed, to_x, *sems).wait_recv()
            start(base_f + 2 * t, quarter(outs[t], xn, c, 0), to_y)
            start(base_d + 3 * t, landed, sibling)
            landed = _half(outs[t], yn, c)
            _remote(2 * t + 1, landed, landed, to_y, *sems).wait_recv()
            start(base_f + 2 * t + 1, quarter(outs[t], yn, c, 1), to_x)
            start(base_d + 3 * t + 1, landed, sibling)
        for t in range(nt):
            q0, q1 = quarter(outs[t], dg, c, 0), quarter(outs[t], dg, c, 1)
            _remote(base_f + 2 * t, q0, q0, to_y, *sems).wait_recv()
            _remote(base_f + 2 * t + 1, q1, q1, to_x, *sems).wait_recv()
            start(base_d + 3 * t + 2, _half(outs[t], dg, c), sibling)
        for j, (px, py) in enumerate(chips):
            for t in range(nw):
                landed = outs[nt + t].at[2 * px + py]
                _remote(base_w + j * nw + t, landed, landed, (px, py, c), *sems).wait_recv()
        for t in range(nt):
            for j, chip in enumerate((xn, yn, dg)):
                other = _half(outs[t], chip, 1 - c)
                _remote(base_d + 3 * t + j, other, other, sibling, *sems).wait_recv()
        for cp in cps:
            cp.wait_send()

    arrays = list(slots) + list(whole)
    n_sem = 7 * nt + 3 * nw
    return pl.pallas_call(
        body, name=name, in_specs=[ANY] * len(arrays), out_specs=[ANY] * len(arrays),
        input_output_aliases={i: i for i in range(len(arrays))},
        out_shape=[jax.ShapeDtypeStruct(a.shape, a.dtype) for a in arrays],
        scratch_shapes=[pltpu.SemaphoreType.DMA((n_sem,)), pltpu.SemaphoreType.DMA((n_sem,))],
    )(*arrays)


def _join_halves(fs, name):
    nt = len(fs)

    def body(*refs):
        in_refs, out_refs, (send_sems, recv_sems) = refs[:nt], refs[nt:2 * nt], refs[2 * nt:]
        _, _, c, sibling, _ = _place()
        cps = [_remote(t, _half(in_refs[t], None, c), _half(out_refs[t], None, c), sibling, send_sems, recv_sems)
               for t in range(nt)]
        for cp in cps:
            cp.start()
        for t in range(nt):
            theirs = _half(out_refs[t], None, 1 - c)
            _remote(t, theirs, theirs, sibling, send_sems, recv_sems).wait_recv()
        for cp in cps:
            cp.wait_send()

    return pl.pallas_call(
        body, name=name, in_specs=[ANY] * nt, out_specs=[ANY] * nt,
        input_output_aliases={i: i for i in range(nt)},
        out_shape=[jax.ShapeDtypeStruct(f.shape, f.dtype) for f in fs],
        scratch_shapes=[pltpu.SemaphoreType.DMA((nt,)), pltpu.SemaphoreType.DMA((nt,))],
    )(*fs)


HBM = pl.BlockSpec(memory_space=pltpu.HBM)
SEM = pl.BlockSpec(memory_space=pltpu.SEMAPHORE)
EFFECT = pltpu.SideEffectType.DATAFLOW_SIDE_EFFECTING


def _split_start(arrays, after, plan, n_copies, name):
    na = len(arrays)

    def body(*refs):
        ins, send_sems, recv_sems = refs[:na], refs[na + 1], refs[na + 2]
        outs, token = refs[na + 3:2 * na + 3], refs[2 * na + 3]
        for k, (src, dst, to) in enumerate(plan(ins, outs)):
            _remote(k, src, dst, to, send_sems, recv_sems).start()
        token[...] = jnp.zeros_like(token)

    res = pl.pallas_call(
        body, name=name,
        out_shape=(pltpu.SemaphoreType.DMA((n_copies,)), pltpu.SemaphoreType.DMA((n_copies,)),
                   *[pltpu.HBM(a.shape, a.dtype) for a in arrays], jax.ShapeDtypeStruct((8, HEAD_DIM), F32)),
        in_specs=[HBM] * na + [ANY],
        out_specs=(SEM, SEM, *[HBM] * na, pl.BlockSpec(memory_space=pltpu.VMEM)),
        input_output_aliases={i: 2 + i for i in range(na)},
        compiler_params=pltpu.CompilerParams(has_side_effects=EFFECT),
    )(*[pltpu.with_memory_space_constraint(a, pltpu.HBM) for a in arrays], after)
    return res[0], res[1], list(res[2:2 + na]), res[2 + na]


def _split_wait(send_sems, recv_sems, arrays, after, plan, name):
    na = len(arrays)
    after = list(after) if isinstance(after, (list, tuple)) else [after]

    def body(*refs):
        ins, send, recv, outs = refs[:na], refs[na], refs[na + 1], refs[na + 2 + len(after):]
        for k, (src, dst, to) in enumerate(plan(ins, outs)):
            cp = _remote(k, src, dst, to, send, recv)
            cp.wait_send()
            cp.wait_recv()

    res = pl.pallas_call(
        body, name=name, out_shape=tuple(pltpu.HBM(a.shape, a.dtype) for a in arrays),
        in_specs=[HBM] * na + [SEM, SEM] + [ANY] * len(after), out_specs=tuple([HBM] * na),
        input_output_aliases={i: i for i in range(na)},
        compiler_params=pltpu.CompilerParams(has_side_effects=EFFECT),
    )(*arrays, send_sems, recv_sems, *after)
    return list(res)


def _gather_plan(nt):
    def plan(ins, outs):
        x, y, c, _, chips = _place()
        me = 2 * x + y
        return [(_half(ins[t], me, c), _half(outs[t], me, c), (px, py, c)) for px, py in chips for t in range(nt)]
    return plan


def _gather_landed_plan(nt):
    def plan(ins, outs):
        _, _, c, _, chips = _place()
        return [(_half(outs[t], 2 * px + py, c), _half(outs[t], 2 * px + py, c), (px, py, c))
                for px, py in chips for t in range(nt)]
    return plan


def _scatter_plan(nt):
    def plan(ins, outs):
        _, _, c, _, chips = _place()
        return [(ins[t].at[2 * px + py], outs[nt + t].at[j], (px, py, c))
                for j, (px, py) in enumerate(chips) for t in range(nt)]
    return plan


def _pass_plan(nt):
    def plan(ins, outs):
        _, _, c, sibling, chips = _place()
        return [(_half(ins[t], 2 * px + py, c), _half(outs[t], 2 * px + py, c), sibling)
                for px, py in chips for t in range(nt)]
    return plan


def _pass_landed_plan(nt):
    def plan(ins, outs):
        _, _, c, sibling, chips = _place()
        return [(_half(outs[t], 2 * px + py, c), _half(outs[t], 2 * px + py, 1 - c), sibling)
                for px, py in chips for t in range(nt)]
    return plan


def _swap_plan(nt):
    def plan(ins, outs):
        _, _, c, sibling, _ = _place()
        res = []
        for t in range(nt):
            half = ins[t].shape[1] // 2
            res.append((ins[t].at[:, pl.ds((1 - c) * half, half), :], outs[nt + t], sibling))
        return res
    return plan


def _pass_halves_to_sibling(slots, name):
    nt = len(slots)

    def body(*refs):
        ins, outs, (send_sems, recv_sems) = refs[:nt], refs[nt:2 * nt], refs[2 * nt:]
        _, _, c, sibling, chips = _place()
        cps = [_remote(j * nt + t, _half(ins[t], 2 * px + py, c), _half(outs[t], 2 * px + py, c), sibling,
                       send_sems, recv_sems)
               for j, (px, py) in enumerate(chips) for t in range(nt)]
        for cp in cps:
            cp.start()
        for j, (px, py) in enumerate(chips):
            for t in range(nt):
                other = _half(outs[t], 2 * px + py, 1 - c)
                _remote(j * nt + t, other, other, sibling, send_sems, recv_sems).wait_recv()
        for cp in cps:
            cp.wait_send()

    return pl.pallas_call(
        body, name=name, in_specs=[ANY] * nt, out_specs=[ANY] * nt,
        input_output_aliases={i: i for i in range(nt)},
        out_shape=[jax.ShapeDtypeStruct(a.shape, a.dtype) for a in slots],
        scratch_shapes=[pltpu.SemaphoreType.DMA((3 * nt,)), pltpu.SemaphoreType.DMA((3 * nt,))],
    )(*slots)


def _all_reduce_small(v):
    r, lanes = v.shape

    def body(v_ref, out_ref, buf, send_sems, recv_sems):
        x, y, c, sibling, chips = _place()

        def slot(px, py, pc):
            return buf.at[4 * px + 2 * py + pc]

        def copy(k, block, to, src=None):
            return pltpu.make_async_remote_copy(src_ref=slot(*block) if src is None else src, dst_ref=slot(*block),
                                                send_sem=send_sems.at[k], recv_sem=recv_sems.at[k],
                                                device_id=to, device_id_type=MESH)

        me = (x, y, c)
        buf[4 * x + 2 * y + c] = v_ref[...]
        first = [copy(0, me, sibling, src=v_ref)]
        first += [copy(1 + j, me, (*chip, c), src=v_ref) for j, chip in enumerate(chips)]
        for cp in first:
            cp.start()
        passed = [copy(4 + j, (*chip, c), sibling) for j, chip in enumerate(chips)]
        for j, chip in enumerate(chips):
            copy(1 + j, (*chip, c), me).wait_recv()
            passed[j].start()
        copy(0, (x, y, 1 - c), me).wait_recv()
        for j, chip in enumerate(chips):
            copy(4 + j, (*chip, 1 - c), me).wait_recv()
        for cp in first + passed:
            cp.wait_send()
        acc = buf[0]
        for k in range(1, 8):
            acc = acc + buf[k]
        out_ref[...] = acc

    vmem = pl.BlockSpec(memory_space=pltpu.VMEM)
    return pl.pallas_call(
        body, name="all_reduce_small", in_specs=[vmem], out_specs=vmem,
        out_shape=jax.ShapeDtypeStruct((r, lanes), F32),
        scratch_shapes=[pltpu.VMEM((8, r, lanes), F32), pltpu.SemaphoreType.DMA((7,)), pltpu.SemaphoreType.DMA((7,))],
    )(v)


def _size(shape):
    n = 1
    for d in shape:
        n *= d
    return n


def _pack_small(parts):
    rows = []
    for p in parts:
        f = p.reshape(-1).astype(F32)
        n = -(-f.shape[0] // HEAD_DIM) * HEAD_DIM
        rows.append(jnp.pad(f, (0, n - f.shape[0])).reshape(-1, HEAD_DIM))
    a = jnp.concatenate(rows, axis=0)
    return jnp.pad(a, ((0, -a.shape[0] % 8), (0, 0)))


def _unpack_small(a, shapes):
    out, row = [], 0
    for shp in shapes:
        nrows = -(-_size(shp) // HEAD_DIM)
        out.append(a[row:row + nrows].reshape(-1)[:_size(shp)].reshape(shp))
        row += nrows
    return out


SMALL = ["attn_norm_w", "a_log", "dt_bias", "delta_out_norm_w", "q_norm_w", "k_norm_w", "attn_out_norm_w", "ffn_norm_w"]
BIG = ["w_in", "w_out", "w_gate_up", "w_down"]
ORDER = ["attn_norm_w", "w_in", "conv_w", "a_log", "dt_bias", "delta_out_norm_w", "q_norm_w", "k_norm_w",
         "attn_out_norm_w", "w_out", "ffn_norm_w", "w_gate_up", "w_down"]


def kernel(x, positions, attn_norm_w, w_in, conv_w, a_log, dt_bias, delta_out_norm_w, q_norm_w, k_norm_w, attn_out_norm_w, w_out, ffn_norm_w, w_gate_up, w_down, loss_target, m_attn_norm_w, m_w_in, m_conv_w, m_a_log, m_dt_bias, m_delta_out_norm_w, m_q_norm_w, m_k_norm_w, m_attn_out_norm_w, m_w_out, m_ffn_norm_w, m_w_gate_up, m_w_down, v_attn_norm_w, v_w_in, v_conv_w, v_a_log, v_dt_bias, v_delta_out_norm_w, v_q_norm_w, v_k_norm_w, v_attn_out_norm_w, v_w_out, v_ffn_norm_w, v_w_gate_up, v_w_down):
    wts = dict(attn_norm_w=attn_norm_w, w_in=w_in, conv_w=conv_w, a_log=a_log, dt_bias=dt_bias,
               delta_out_norm_w=delta_out_norm_w, q_norm_w=q_norm_w, k_norm_w=k_norm_w,
               attn_out_norm_w=attn_out_norm_w, w_out=w_out, ffn_norm_w=ffn_norm_w, w_gate_up=w_gate_up, w_down=w_down)
    mom = dict(attn_norm_w=m_attn_norm_w, w_in=m_w_in, conv_w=m_conv_w, a_log=m_a_log, dt_bias=m_dt_bias,
               delta_out_norm_w=m_delta_out_norm_w, q_norm_w=m_q_norm_w, k_norm_w=m_k_norm_w,
               attn_out_norm_w=m_attn_out_norm_w, w_out=m_w_out, ffn_norm_w=m_ffn_norm_w, w_gate_up=m_w_gate_up,
               w_down=m_w_down)
    var = dict(attn_norm_w=v_attn_norm_w, w_in=v_w_in, conv_w=v_conv_w, a_log=v_a_log, dt_bias=v_dt_bias,
               delta_out_norm_w=v_delta_out_norm_w, q_norm_w=v_q_norm_w, k_norm_w=v_k_norm_w,
               attn_out_norm_w=v_attn_out_norm_w, w_out=v_w_out, ffn_norm_w=v_ffn_norm_w, w_gate_up=v_w_gate_up,
               w_down=v_w_down)
    dmod = x.shape[2]
    heads = dmod // (2 * HEAD_DIM)
    dw = heads * HEAD_DIM
    chip = 2 * lax.axis_index("x") + lax.axis_index("y")
    core = lax.axis_index("c")
    n_in, n_out, n_gu, n_down, n_conv = (w_in.shape[2], w_out.shape[1], w_gate_up.shape[2], w_down.shape[1],
                                         conv_w.shape[2])

    def slots_of(w, dtype):
        shard = w[0].astype(dtype)
        return lax.dynamic_update_index_in_dim(lax.empty((4,) + shard.shape, dtype), shard, chip, axis=0)

    s_in, s_conv = _all_gather_weights([slots_of(w_in, BF16)], [slots_of(conv_w, F32)], "all_gather_w_in")
    later = [slots_of(w_out, BF16), slots_of(w_gate_up, BF16), slots_of(w_down, BF16)]
    w_send, w_recv, later, started = _split_start(later, s_conv, _gather_plan(3), 9, "gather_rest_start")
    by_cols = lambda a: a.transpose(1, 0, 2).reshape(a.shape[1], 4 * a.shape[2])
    w_in_f, conv_f = by_cols(s_in), by_cols(s_conv)
    w_bd = jnp.pad(w_in_f[:, 4 * dw:4 * dw + 2 * heads], ((0, 0), (0, HEAD_DIM - 2 * heads)))
    small = {n: wts[n] for n in SMALL}
    place = jnp.stack([core, chip]).astype(jnp.int32)
    to_slots = lambda a: a.reshape(a.shape[0], 4, a.shape[1] // 4).transpose(1, 0, 2)

    def later_weights(after):
        landed = _split_wait(w_send, w_recv, later, after, _gather_landed_plan(3), "gather_rest_wait")
        s_out, = _pass_halves_to_sibling(landed[:1], "gather_out_pass")
        p_send, p_recv, passing, token = _split_start(landed[1:], s_out, _pass_plan(2), 6, "gather_ffn_pass_start")

        def ffn_weights(after):
            s_gu, s_down = _split_wait(p_send, p_recv, passing, after, _pass_landed_plan(2), "gather_ffn_pass_wait")
            return by_cols(s_gu), s_down.reshape(4 * n_down, dmod)

        return s_out.reshape(4 * n_out, dmod), token, ffn_weights

    ffn = {}

    def ffn_grads_ready(g_gate, g_up, g_down):
        gs = [to_slots(jnp.concatenate([g_gate, g_up], axis=1)), g_down.reshape(4, n_down, dmod)]
        zones = [lax.empty((4, g.shape[1] // 2, g.shape[2]), BF16) for g in gs]
        s_send, s_recv, s_bufs, swapping = _split_start(gs + zones, g_gate, _swap_plan(2), 2, "swap_ffn_start")

        def and_then(after):
            g_gu, g_dn, b_gu, b_dn = _split_wait(s_send, s_recv, s_bufs, after, _swap_plan(2), "swap_ffn_wait")
            sums = [_add_half_bf16(g_gu, b_gu, place, "chip_partial_sum_w_gate_up"),
                    _add_half_bf16(g_dn, b_dn, place, "chip_partial_sum_w_down")]
            zones3 = [lax.empty((3,) + p.shape[1:], BF16) for p in sums]
            ffn["send"], ffn["recv"], ffn["bufs"], token = _split_start(sums + zones3, b_gu, _scatter_plan(2), 6,
                                                                        "scatter_ffn_start")
            return token

        return swapping, and_then

    rest = {}

    def rest_grads_ready(bg):
        gs = [to_slots(jnp.concatenate([bg["w_qkv"], bg["w_z"], bg["w_bd"][:, :2 * heads], bg["w_attn"]], axis=1)),
              jnp.concatenate([bg["w_out_a"], bg["w_out_b"]], axis=0).reshape(4, n_out, dmod)]
        zones = [lax.empty((4, g.shape[1] // 2, g.shape[2]), BF16) for g in gs]
        s_send, s_recv, s_bufs, swapping = _split_start(gs + zones, bg["w_attn"], _swap_plan(2), 2, "swap_rest_start")

        def and_then(after):
            g_in, g_out, b_in, b_out = _split_wait(s_send, s_recv, s_bufs, after, _swap_plan(2), "swap_rest_wait")
            sums = [_add_half_bf16(g_in, b_in, place, "chip_partial_sum_w_in"),
                    _add_half_bf16(g_out, b_out, place, "chip_partial_sum_w_out")]
            zones3 = [lax.empty((3,) + p.shape[1:], BF16) for p in sums]
            rest["send"], rest["recv"], rest["bufs"], token = _split_start(sums + zones3, b_in, _scatter_plan(2), 6,
                                                                           "scatter_rest_start")
            return token

        return swapping, and_then

    loss_row, grad_x, sg = _local_step(
        x[0], positions[0], loss_target[0], small, w_in_f[:, :4 * dw], w_bd, w_in_f[:, 4 * dw + 2 * heads:], conv_f,
        later_weights, ffn_grads_ready, rest_grads_ready, after=started)

    r_send, r_recv, r_bufs = rest["send"], rest["recv"], rest["bufs"]
    sum_gu, sum_down, got_gu, got_down = _split_wait(ffn["send"], ffn["recv"], ffn["bufs"], grad_x,
                                                     _scatter_plan(2), "scatter_ffn_wait")
    g_big = dict(zip(["w_gate_up", "w_down"], _join_halves(
        [_sum4_f32(sum_gu, got_gu, place, "grad_total_w_gate_up"),
         _sum4_f32(sum_down, got_down, place, "grad_total_w_down")], "join_ffn_halves")))
    grads, deltas, new_m, new_v = {}, {}, {}, {}

    def adamw_big(n):
        shp = wts[n].shape
        d, nm, nv = _adamw(wts[n][0], g_big[n], mom[n][0], var[n][0], "adamw_" + n)
        grads[n], deltas[n], new_m[n], new_v[n] = g_big[n].reshape(shp), d.reshape(shp), nm.reshape(shp), nv.reshape(shp)
        return d

    done = [adamw_big("w_gate_up"), adamw_big("w_down")]
    sum_in, sum_out, got_in, got_out = _split_wait(r_send, r_recv, r_bufs, done, _scatter_plan(2), "scatter_rest_wait")
    g_big.update(zip(["w_in", "w_out"], _join_halves(
        [_sum4_f32(sum_in, got_in, place, "grad_total_w_in"),
         _sum4_f32(sum_out, got_out, place, "grad_total_w_out")], "join_rest_halves")))
    adamw_big("w_in")
    adamw_big("w_out")

    reduced = _all_reduce_small(_pack_small([sg[n] for n in SMALL] + [sg["conv_w"], loss_row]))
    red = _unpack_small(reduced, [wts[n].shape for n in SMALL] + [(4, 4 * n_conv), (1, HEAD_DIM)])
    g_small = dict(zip(SMALL, red[:len(SMALL)]))
    g_conv_full, loss_out = red[len(SMALL)], red[len(SMALL) + 1]
    g_small["conv_w"] = lax.dynamic_slice_in_dim(g_conv_full, chip * n_conv, n_conv, axis=1).reshape(conv_w.shape)

    names = SMALL + ["conv_w"]
    shapes = [wts[n].shape for n in names]
    d, nm, nv = _adamw(_pack_small([wts[n] for n in names]), _pack_small([g_small[n] for n in names]),
                       _pack_small([mom[n] for n in names]), _pack_small([var[n] for n in names]), "adamw_small")
    for n, dd, mm, vv in zip(names, _unpack_small(d, shapes), _unpack_small(nm, shapes), _unpack_small(nv, shapes)):
        grads[n], deltas[n], new_m[n], new_v[n] = g_small[n], dd, mm, vv
    return (loss_out[0, 0], grad_x[None], *[grads[n] for n in ORDER], *[deltas[n] for n in ORDER],
            *[new_m[n] for n in ORDER], *[new_v[n] for n in ORDER])
```

```python
import jax
import jax.numpy as jnp
from jax import lax
from jax.experimental import pallas as pl
from jax.experimental.pallas import tpu as pltpu

F32 = jnp.float32
BF16 = jnp.bfloat16
HEAD_DIM = 128
CHUNK = 128
INV_BLOCK = 64
SPAN = 128
DILATIONS = (1, 4, 16)
ROPE_THETA = 10000.0
EPS = 1e-6
NEG = -1e30
ADAM_LR, ADAM_B1, ADAM_B2, ADAM_EPS, ADAM_WD, ADAM_STEP = 0.001, 0.9, 0.999, 1e-08, 0.01, 10
VMEM_LIMIT = 48 * 1024 * 1024
MATMUL_VMEM = 44 * 1024 * 1024
MESH = pl.DeviceIdType.MESH

_DN = {"nn": (((1,), (0,)), ((), ())), "nt": (((1,), (1,)), ((), ())), "tn": (((0,), (0,)), ((), ()))}


def _dot(a, b, mode="nn"):
    (ca, cb), _ = _DN[mode]
    if a.ndim == 3:
        dn = (((ca[0] + 1,), (cb[0] + 1,)), ((0,), (0,)))
    else:
        dn = _DN[mode]
    return lax.dot_general(a.astype(BF16), b.astype(BF16), dn, preferred_element_type=F32)


def _rsum(x):
    return jnp.sum(x, axis=-1, keepdims=True)


def _csum(x):
    return jnp.sum(x, axis=-2, keepdims=True)


def _tile(dim, pref, unit=128):
    t = (min(pref, dim) // unit) * unit
    while t >= unit:
        if dim % t == 0:
            return t
        t -= unit
    return dim


def _params(sem):
    return pltpu.CompilerParams(dimension_semantics=sem, vmem_limit_bytes=VMEM_LIMIT)


def _sigmoid(x):
    return 1.0 / (1.0 + jnp.exp(-x))


def _matmul(a, b, mode, name, add=None, out_dtype=F32, a_cols=None, b_cols=None, b_rows=None,
            tiles=(), finish=None, out_dtypes=(), row_sum=False, after=None, b2_cols=None):
    if mode == "tn":
        out_dtype = BF16
    a_off, a_w = a_cols if a_cols else (0, a.shape[1])
    b_off, b_w = b_cols if b_cols else (0, b.shape[1])
    br_off, br_n = b_rows if b_rows else (0, b.shape[0])
    if mode == "nn":
        m, k, n = a.shape[0], a_w, b_w
        assert br_n == k
    elif mode == "nt":
        m, k, n = a.shape[0], a_w, b.shape[0]
        assert b_w == k
    else:
        k, m, n = a.shape[0], a_w, b_w
        assert b.shape[0] == k
    if finish is None:
        out_dtypes = [out_dtype]
        if add is None:
            finish = lambda acc, vals: (acc,)
        else:
            tiles = [(add, 0)]
            finish = lambda acc, vals: (acc + vals[0].astype(F32),)
    n_b = 1 if b2_cols is None else 2
    sides = sum(jnp.dtype(t.dtype).itemsize for t, _ in tiles) + sum(jnp.dtype(d).itemsize for d in out_dtypes)

    def vmem(tm_, tn_, tk_):
        ops = tm_ * tk_ * jnp.dtype(a.dtype).itemsize + n_b * tk_ * tn_ * jnp.dtype(b.dtype).itemsize
        return 2 * (ops + tm_ * tn_ * sides) + (n_b * tm_ * tn_ * 4 if tk_ < k else 0)

    tm, tn = _tile(m, 1024, 128), _tile(n, 1024, 128)
    tall = _tile(m, 1536, 128), _tile(n, 512, 128)
    if tall[0] * tall[1] > tm * tn and vmem(*tall, k) <= MATMUL_VMEM:
        tm, tn = tall
    tk = next(t for t in [_tile(k, p, 128) for p in (4096, 2048, 1024, 512, 128)] if vmem(tm, tn, t) <= MATMUL_VMEM)
    if mode == "nn":
        assert a_off % tk == 0 and b_off % tn == 0 and br_off % tk == 0
        a_spec = pl.BlockSpec((tm, tk), lambda i, j, kk: (i, kk + a_off // tk))
        b_spec = pl.BlockSpec((tk, tn), lambda i, j, kk: (kk + br_off // tk, j + b_off // tn))
    elif mode == "nt":
        assert a_off % tk == 0 and b_off % tk == 0
        a_spec = pl.BlockSpec((tm, tk), lambda i, j, kk: (i, kk + a_off // tk))
        b_spec = pl.BlockSpec((tn, tk), lambda i, j, kk: (j, kk + b_off // tk))
    else:
        assert a_off % tm == 0 and b_off % tn == 0
        a_spec = pl.BlockSpec((tk, tm), lambda i, j, kk: (kk, i + a_off // tm))
        b_spec = pl.BlockSpec((tk, tn), lambda i, j, kk: (kk, j + b_off // tn))
    nk = k // tk
    n_tiles, n_out = len(tiles), len(out_dtypes)
    extra = [] if after is None else [after]
    first_out = 1 + n_b + n_tiles + len(extra)

    def body(*refs):
        a_ref, b_refs = refs[0], refs[1:1 + n_b]
        tile_refs, out_refs = refs[1 + n_b:1 + n_b + n_tiles], refs[first_out:first_out + n_out]
        acc_refs = refs[-n_b:] if nk > 1 else ()
        kk = pl.program_id(2)
        first_tile = (pl.program_id(0) == 0) & (pl.program_id(1) == 0)

        av = a_ref[...]
        if nk > 1:
            @pl.when(kk == 0)
            def _():
                for acc_ref in acc_refs:
                    acc_ref[...] = jnp.zeros_like(acc_ref)

            for acc_ref, b_ref in zip(acc_refs, b_refs):
                acc_ref[...] += _dot(av, b_ref[...], mode)

        @pl.when(kk == nk - 1)
        def _():
            if nk > 1:
                accs = [acc_ref[...] for acc_ref in acc_refs]
            else:
                accs = [_dot(av, b_ref[...], mode) for b_ref in b_refs]
            res = finish(accs[0] if n_b == 1 else accs, [t[...] for t in tile_refs])
            for o_ref, r in zip(out_refs, res):
                o_ref[...] = r.astype(o_ref.dtype)
            if row_sum:
                row_ref = refs[first_out + n_out]

                @pl.when(first_tile)
                def _():
                    row_ref[...] = res[n_out]

                @pl.when(jnp.logical_not(first_tile))
                def _():
                    row_ref[...] += res[n_out]

    in_specs = [a_spec, b_spec]
    args = [a, b]
    if b2_cols is not None:
        assert mode == "nn" and b2_cols[1] == n and b2_cols[0] % tn == 0
        in_specs.append(pl.BlockSpec((tk, tn), lambda i, j, kk: (kk + br_off // tk, j + b2_cols[0] // tn)))
        args.append(b)
    for arr, off in tiles:
        assert off % tn == 0
        in_specs.append(pl.BlockSpec((tm, tn), lambda i, j, kk, off=off: (i, j + off // tn)))
        args.append(arr)
    in_specs += [ANY] * len(extra)
    args += extra
    out_specs = [pl.BlockSpec((tm, tn), lambda i, j, kk: (i, j))] * n_out
    out_shape = [jax.ShapeDtypeStruct((m, n), dt) for dt in out_dtypes]
    if row_sum:
        out_specs.append(pl.BlockSpec((1, HEAD_DIM), lambda i, j, kk: (0, 0)))
        out_shape.append(jax.ShapeDtypeStruct((1, HEAD_DIM), F32))
    res = pl.pallas_call(
        body, name=name, grid=(m // tm, n // tn, nk),
        in_specs=in_specs, out_specs=out_specs, out_shape=out_shape,
        scratch_shapes=[pltpu.VMEM((tm, tn), F32)] * (n_b if nk > 1 else 0),
        compiler_params=_params(("arbitrary",) * 3 if row_sum else ("parallel", "parallel", "arbitrary")),
    )(*args)
    return res[0] if len(res) == 1 else res


def _rmsnorm_fwd(x, w, name, after=None):
    s, d = x.shape
    tr = _tile(s, 1024, 8)

    def body(x_ref, w_ref, *rest):
        h_ref = rest[-1]
        xv = x_ref[...]
        rstd = lax.rsqrt(jnp.mean(xv * xv, axis=-1, keepdims=True) + EPS)
        h_ref[...] = (xv * rstd * w_ref[...]).astype(BF16)

    extra = [] if after is None else [after]
    return pl.pallas_call(
        body, name=name, grid=(s // tr,),
        in_specs=[pl.BlockSpec((tr, d), lambda i: (i, 0)), pl.BlockSpec((1, d), lambda i: (0, 0))] + [ANY] * len(extra),
        out_specs=pl.BlockSpec((tr, d), lambda i: (i, 0)),
        out_shape=jax.ShapeDtypeStruct((s, d), BF16),
        compiler_params=_params(("parallel",)),
    )(x, w, *extra)


def _rmsnorm_bwd(dh, x, w, res, name, after=None):
    s, d = x.shape
    tr = _tile(s, 256, 8)

    def body(dh_ref, x_ref, w_ref, res_ref, *rest):
        dx_ref, dx16_ref, dw_ref = rest[-3:]
        xv = x_ref[...]
        rstd = lax.rsqrt(jnp.mean(xv * xv, axis=-1, keepdims=True) + EPS)
        xhat = xv * rstd
        dhv = dh_ref[...]
        gw = dhv * w_ref[...]
        dx = res_ref[...] + rstd * (gw - xhat * jnp.mean(gw * xhat, axis=-1, keepdims=True))
        dx_ref[...] = dx
        dx16_ref[...] = dx.astype(BF16)

        @pl.when(pl.program_id(0) == 0)
        def _():
            dw_ref[...] = jnp.zeros_like(dw_ref)

        dw_ref[...] += jnp.sum(dhv * xhat, axis=0, keepdims=True)

    row = pl.BlockSpec((tr, d), lambda i: (i, 0))
    vec = pl.BlockSpec((1, d), lambda i: (0, 0))
    extra = [] if after is None else [after]
    return pl.pallas_call(
        body, name=name, grid=(s // tr,),
        in_specs=[row, row, vec, row] + [ANY] * len(extra), out_specs=[row, row, vec],
        out_shape=[jax.ShapeDtypeStruct((s, d), F32), jax.ShapeDtypeStruct((s, d), BF16),
                   jax.ShapeDtypeStruct((1, d), F32)],
        compiler_params=_params(("arbitrary",)),
    )(dh, x, w, res, *extra)


def _conv_taps(x, w, rows):
    shifted = [x]
    for sft in (1, 2, 3):
        shifted.append(jnp.where(rows >= sft, pltpu.roll(x, sft, 0), 0.0))
    y = w[3:4, :] * shifted[0] + w[2:3, :] * shifted[1] + w[1:2, :] * shifted[2] + w[0:1, :] * shifted[3]
    return y, shifted


def _delta_pre_fwd(qkvz, conv_w, heads):
    s = qkvz.shape[0]
    nblk = 3 * heads

    def body(x_ref, w_ref, o_ref):
        part = pl.program_id(0) // heads
        rows = lax.broadcasted_iota(jnp.int32, (s, HEAD_DIM), 0)
        y, _ = _conv_taps(x_ref[...], w_ref[...], rows)
        a = y * _sigmoid(y)
        rs = lax.rsqrt(jnp.sum(a * a, axis=-1, keepdims=True) + EPS)
        fac = jnp.where(part == 0, rs * (HEAD_DIM ** -0.5), jnp.where(part == 1, rs, 1.0))
        o_ref[...] = a * fac

    return pl.pallas_call(
        body, name="delta_pre_fwd", grid=(nblk,),
        in_specs=[pl.BlockSpec((s, HEAD_DIM), lambda i: (0, i)), pl.BlockSpec((4, HEAD_DIM), lambda i: (0, i))],
        out_specs=pl.BlockSpec((s, HEAD_DIM), lambda i: (0, i)),
        out_shape=jax.ShapeDtypeStruct((s, 3 * heads * HEAD_DIM), F32),
        compiler_params=_params(("parallel",)),
    )(qkvz, conv_w)


def _delta_pre_bwd(dqkv, qkvz, conv_w, heads):
    s = qkvz.shape[0]
    nblk = 3 * heads

    def body(d_ref, x_ref, w_ref, dx_ref, dw_ref):
        part = pl.program_id(0) // heads
        rows = lax.broadcasted_iota(jnp.int32, (s, HEAD_DIM), 0)
        w = w_ref[...]
        y, shifted = _conv_taps(x_ref[...], w, rows)
        sg = _sigmoid(y)
        a = y * sg
        rs = lax.rsqrt(jnp.sum(a * a, axis=-1, keepdims=True) + EPS)
        unit = a * rs
        dn = d_ref[...]
        scale = jnp.where(part == 0, HEAD_DIM ** -0.5, 1.0)
        da_norm = scale * rs * (dn - unit * jnp.sum(dn * unit, axis=-1, keepdims=True))
        da = jnp.where(part < 2, da_norm, dn)
        dy = da * sg * (1.0 + y * (1.0 - sg))
        dx = w[3:4, :] * dy
        for sft in (1, 2, 3):
            dx = dx + w[3 - sft:4 - sft, :] * jnp.where(rows < s - sft, pltpu.roll(dy, s - sft, 0), 0.0)
        dx_ref[...] = dx.astype(BF16)
        for sft in range(4):
            dw_ref[3 - sft:4 - sft, :] = jnp.sum(dy * shifted[sft], axis=0, keepdims=True)

    col = pl.BlockSpec((s, HEAD_DIM), lambda i: (0, i))
    wsp = pl.BlockSpec((4, HEAD_DIM), lambda i: (0, i))
    return pl.pallas_call(
        body, name="delta_pre_bwd", grid=(nblk,),
        in_specs=[col, col, wsp], out_specs=[col, wsp],
        out_shape=[jax.ShapeDtypeStruct((s, 3 * heads * HEAD_DIM), BF16),
                   jax.ShapeDtypeStruct((4, 3 * heads * HEAD_DIM), F32)],
        compiler_params=_params(("parallel",)),
    )(dqkv, qkvz, conv_w)


def _heads_of(ref, heads):
    return jnp.stack([ref[:, h * HEAD_DIM:(h + 1) * HEAD_DIM] for h in range(heads)])


def _chunk_common(q, k, v, bd, a_log, dt_bias, heads, solved=None):
    c = CHUNK
    braw = jnp.stack([bd[:, h:h + 1] for h in range(heads)])
    draw = jnp.stack([bd[:, heads + h:heads + h + 1] for h in range(heads)])
    beta = _sigmoid(braw)
    xd = draw + dt_bias
    sp = jnp.maximum(xd, 0.0) + jnp.log1p(jnp.exp(-jnp.abs(xd)))
    g = -jnp.exp(a_log) * sp
    row = lax.broadcasted_iota(jnp.int32, (c, c), 0)
    col = lax.broadcasted_iota(jnp.int32, (c, c), 1)
    sq = (heads, c, c)
    g_b = jnp.broadcast_to(g, sq)
    g_row = _csum(jnp.where(row == col, g_b, 0.0))
    gam_col = _rsum(jnp.where(col <= row, jnp.broadcast_to(g_row, sq), 0.0))
    gam_row = _csum(jnp.where(row <= col, g_b, 0.0))
    causal = row >= col
    dm = jnp.where(causal, jnp.exp(jnp.where(causal, gam_col - gam_row, 0.0)), 0.0)
    kk = _dot(k, k, "nt")
    e = jnp.exp(gam_col)
    if solved is None:
        low = jnp.where(row > col, beta * kk * dm, 0.0)
        assert c in (INV_BLOCK, 2 * INV_BLOCK)
        same = (row // INV_BLOCK) == (col // INV_BLOCK)
        diag = jnp.where(same, low, 0.0)
        t = jnp.where(row == col, 1.0, 0.0) - diag
        pw = diag
        for _ in range((INV_BLOCK - 1).bit_length() - 1):
            pw = _dot(pw, pw)
            t = t + _dot(t, pw)
        if c > INV_BLOCK:
            t = t - _dot(_dot(t, low - diag), t)
        u = _dot(t, beta * v)
        w = _dot(t, (beta * e) * k)
    else:
        t, u, w = solved
    qk_raw = _dot(q, k, "nt")
    gl = _csum(g)
    el = jnp.exp(gl - gam_col)
    return dict(beta=beta, xd=xd, g=g, row=row, col=col, dm=dm, kk=kk, t=t, e=e, u=u, w=w,
                qk_raw=qk_raw, qk=qk_raw * dm, gl=gl, el=el, qd=e * q, kd=el * k, cd=jnp.exp(gl))


def _delta_chunk_fwd(qkv, bd, a_log, dt_bias, heads):
    s = qkv.shape[0]
    n = s // CHUNK
    dw = heads * HEAD_DIM
    blk = lambda part: pl.BlockSpec((CHUNK, dw), lambda i: (i, part))

    def body(q_ref, k_ref, v_ref, bd_ref, al_ref, dt_ref, o_ref, st_ref, t_ref, uw_ref, state):
        @pl.when(pl.program_id(0) == 0)
        def _():
            state[...] = jnp.zeros_like(state)

        cm = _chunk_common(_heads_of(q_ref, heads), _heads_of(k_ref, heads), _heads_of(v_ref, heads), bd_ref[...],
                           al_ref[...], dt_ref[...], heads)
        st = state[...]
        st_ref[0] = st
        t_ref[0] = cm["t"]
        uw_ref[0, 0] = cm["u"]
        uw_ref[0, 1] = cm["w"]
        vn = cm["u"] - _dot(cm["w"], st)
        o = _dot(cm["qd"], st) + _dot(cm["qk"], vn)
        for h in range(heads):
            o_ref[:, h * HEAD_DIM:(h + 1) * HEAD_DIM] = o[h]
        state[...] = cm["cd"] * st + _dot(cm["kd"], vn, "tn")

    smem = pl.BlockSpec((heads, 1, 1), lambda i: (0, 0, 0))
    return pl.pallas_call(
        body, name="delta_chunk_fwd", grid=(n,),
        in_specs=[blk(0), blk(1), blk(2), pl.BlockSpec((CHUNK, HEAD_DIM), lambda i: (i, 0)), smem, smem],
        out_specs=[pl.BlockSpec((CHUNK, dw), lambda i: (i, 0)),
                   pl.BlockSpec((1, heads, HEAD_DIM, HEAD_DIM), lambda i: (i, 0, 0, 0)),
                   pl.BlockSpec((1, heads, CHUNK, CHUNK), lambda i: (i, 0, 0, 0)),
                   pl.BlockSpec((1, 2, heads, CHUNK, HEAD_DIM), lambda i: (i, 0, 0, 0, 0))],
        out_shape=[jax.ShapeDtypeStruct((s, dw), F32),
                   jax.ShapeDtypeStruct((n, heads, HEAD_DIM, HEAD_DIM), F32),
                   jax.ShapeDtypeStruct((n, heads, CHUNK, CHUNK), F32),
                   jax.ShapeDtypeStruct((n, 2, heads, CHUNK, HEAD_DIM), F32)],
        scratch_shapes=[pltpu.VMEM((heads, HEAD_DIM, HEAD_DIM), F32)],
        compiler_params=_params(("arbitrary",)),
    )(qkv, qkv, qkv, bd, a_log, dt_bias)


def _delta_chunk_bwd(do, qkv, bd, saved, a_log, dt_bias, heads):
    s = qkv.shape[0]
    n = s // CHUNK
    dw = heads * HEAD_DIM
    c = CHUNK
    blk = lambda part: pl.BlockSpec((CHUNK, dw), lambda i: (n - 1 - i, part))

    def all_heads(q, k, v, dov, st, solved, dsn, bd, a_log, dt_bias):
        cm = _chunk_common(q, k, v, bd, a_log, dt_bias, heads, solved)
        beta, e, dm, row, col = cm["beta"], cm["e"], cm["dm"], cm["row"], cm["col"]
        sq = (heads, c, c)
        vn = cm["u"] - _dot(cm["w"], st)
        dvn = _dot(cm["kd"], dsn)
        dkd = _dot(vn, dsn, "nt")
        dcd = _csum(_rsum(st * dsn))
        ds = cm["cd"] * dsn
        dqd = _dot(dov, st, "nt")
        ds = ds + _dot(cm["qd"], dov, "tn")
        dqk = _dot(dov, vn, "nt")
        dvn = dvn + _dot(cm["qk"], dov, "tn")
        dw_ = -_dot(dvn, st, "nt")
        ds = ds - _dot(cm["w"], dvn, "tn")
        drhs_u = _dot(cm["t"], dvn, "tn")
        drhs_w = _dot(cm["t"], dw_, "tn")
        da = -(_dot(drhs_u, cm["u"], "nt") + _dot(drhs_w, cm["w"], "nt"))
        dl = jnp.where(row > col, da, 0.0)
        dbeta = _rsum(dl * cm["kk"] * dm)
        dkk = dl * beta * dm
        dd = dl * beta * cm["kk"]
        dv = beta * drhs_u
        ek = e * k
        dbeta = dbeta + _rsum(drhs_u * v) + _rsum(drhs_w * ek)
        dk = (beta * e) * drhs_w
        dgam = _rsum(drhs_w * (beta * ek))
        dqkm = dqk * dm
        dq = _dot(dqkm, k)
        dk = dk + _dot(dqkm, q, "tn")
        dd = dd + dqk * cm["qk_raw"]
        dk = dk + _dot(dkk, k) + _dot(dkk, k, "tn")
        dq = dq + e * dqd
        dgam = dgam + _rsum(dqd * cm["qd"])
        dk = dk + cm["el"] * dkd
        r = _rsum(dkd * cm["kd"])
        dgam = dgam - r
        dgl = _csum(r) + dcd * cm["cd"]
        mm = dd * dm
        colsum_c = _rsum(jnp.where(row == col, jnp.broadcast_to(_csum(mm), sq), 0.0))
        dgam = dgam + _rsum(mm) - colsum_c
        ridx = lax.broadcasted_iota(jnp.int32, (c, 1), 0)
        dgam = dgam + jnp.where(ridx == c - 1, dgl, 0.0)
        dgam_row = _csum(jnp.where(row == col, jnp.broadcast_to(dgam, sq), 0.0))
        dg = _rsum(jnp.where(col >= row, jnp.broadcast_to(dgam_row, sq), 0.0))
        d_xd = dg * (-jnp.exp(a_log)) * _sigmoid(cm["xd"])
        d_braw = dbeta * beta * (1.0 - beta)
        d_alog = dg * cm["g"]
        lane = lax.broadcasted_iota(jnp.int32, (c, HEAD_DIM), 1)
        dbd = jnp.zeros((c, HEAD_DIM), F32)
        for h in range(heads):
            dbd = (dbd + jnp.where(lane == h, d_braw[h], 0.0) + jnp.where(lane == h + heads, d_xd[h], 0.0)
                   + jnp.where(lane == h + 2 * heads, d_alog[h], 0.0))
        return dq, dk, dv, ds, dbd

    def body(do_ref, q_ref, k_ref, v_ref, bd_ref, st_ref, t_ref, uw_ref, al_ref, dt_ref, dqkv_ref, dbd_ref, dstate):
        @pl.when(pl.program_id(0) == 0)
        def _():
            dstate[...] = jnp.zeros_like(dstate)

        dq, dk, dv, ds, dbd = all_heads(_heads_of(q_ref, heads), _heads_of(k_ref, heads), _heads_of(v_ref, heads),
                                        _heads_of(do_ref, heads), st_ref[0], (t_ref[0], uw_ref[0, 0], uw_ref[0, 1]),
                                        dstate[...], bd_ref[...],
                                        al_ref[...], dt_ref[...])
        for part, val in enumerate((dq, dk, dv)):
            for h in range(heads):
                lo = part * dw + h * HEAD_DIM
                dqkv_ref[:, lo:lo + HEAD_DIM] = val[h]
        dstate[...] = ds
        dbd_ref[...] = dbd

    smem = pl.BlockSpec((heads, 1, 1), lambda i: (0, 0, 0))
    shared = pl.BlockSpec((CHUNK, HEAD_DIM), lambda i: (n - 1 - i, 0))
    wide = pl.BlockSpec((CHUNK, dw), lambda i: (n - 1 - i, 0))
    return pl.pallas_call(
        body, name="delta_chunk_bwd", grid=(n,),
        in_specs=[wide, blk(0), blk(1), blk(2), shared,
                  pl.BlockSpec((1, heads, HEAD_DIM, HEAD_DIM), lambda i: (n - 1 - i, 0, 0, 0)),
                  pl.BlockSpec((1, heads, CHUNK, CHUNK), lambda i: (n - 1 - i, 0, 0, 0)),
                  pl.BlockSpec((1, 2, heads, CHUNK, HEAD_DIM), lambda i: (n - 1 - i, 0, 0, 0, 0)), smem, smem],
        out_specs=[pl.BlockSpec((CHUNK, 3 * dw), lambda i: (n - 1 - i, 0)), shared],
        out_shape=[jax.ShapeDtypeStruct((s, 3 * dw), F32), jax.ShapeDtypeStruct((s, HEAD_DIM), F32)],
        scratch_shapes=[pltpu.VMEM((heads, HEAD_DIM, HEAD_DIM), F32)],
        compiler_params=_params(("arbitrary",)),
    )(do, qkv, qkv, qkv, bd, *saved, a_log, dt_bias)


def _delta_post_fwd(o, qkvz, w, heads):
    s = o.shape[0]
    tr = _tile(s, 4096, 8)

    def body(o_ref, z_ref, w_ref, out_ref):
        ov, z = o_ref[...], z_ref[...]
        rstd = lax.rsqrt(jnp.mean(ov * ov, axis=-1, keepdims=True) + EPS)
        out_ref[...] = (ov * rstd * w_ref[...] * (z * _sigmoid(z))).astype(BF16)

    return pl.pallas_call(
        body, name="delta_post_fwd", grid=(s // tr, heads),
        in_specs=[pl.BlockSpec((tr, HEAD_DIM), lambda i, h: (i, h)),
                  pl.BlockSpec((tr, HEAD_DIM), lambda i, h: (i, 3 * heads + h)),
                  pl.BlockSpec((1, HEAD_DIM), lambda i, h: (0, 0))],
        out_specs=pl.BlockSpec((tr, HEAD_DIM), lambda i, h: (i, h)),
        out_shape=jax.ShapeDtypeStruct((s, heads * HEAD_DIM), BF16),
        compiler_params=_params(("parallel", "parallel")),
    )(o, qkvz, w)


def _delta_post_bwd(dmix, o, qkvz, w, heads):
    s = o.shape[0]
    tr = _tile(s, 4096, 8)

    def body(d_ref, o_ref, z_ref, w_ref, do_ref, dz_ref, dw_ref):
        d, ov, z, wv = d_ref[...], o_ref[...], z_ref[...], w_ref[...]
        sg = _sigmoid(z)
        rstd = lax.rsqrt(jnp.mean(ov * ov, axis=-1, keepdims=True) + EPS)
        ohat = ov * rstd
        dz_ref[...] = (d * (ohat * wv) * sg * (1.0 + z * (1.0 - sg))).astype(BF16)
        dn = d * (z * sg)
        gw = dn * wv
        do_ref[...] = rstd * (gw - ohat * jnp.mean(gw * ohat, axis=-1, keepdims=True))

        @pl.when((pl.program_id(0) == 0) & (pl.program_id(1) == 0))
        def _():
            dw_ref[...] = jnp.zeros_like(dw_ref)

        dw_ref[...] += jnp.sum(dn * ohat, axis=0, keepdims=True)

    head = pl.BlockSpec((tr, HEAD_DIM), lambda i, h: (i, h))
    vec = pl.BlockSpec((1, HEAD_DIM), lambda i, h: (0, 0))
    dw = heads * HEAD_DIM
    return pl.pallas_call(
        body, name="delta_post_bwd", grid=(s // tr, heads),
        in_specs=[head, head, pl.BlockSpec((tr, HEAD_DIM), lambda i, h: (i, 3 * heads + h)), vec],
        out_specs=[head, head, vec],
        out_shape=[jax.ShapeDtypeStruct((s, dw), F32), jax.ShapeDtypeStruct((s, dw), BF16),
                   jax.ShapeDtypeStruct((1, HEAD_DIM), F32)],
        compiler_params=_params(("arbitrary", "arbitrary")),
    )(dmix, o, qkvz, w)


def _rope_tables(positions, s):
    half = HEAD_DIM // 2
    inv_freq = ROPE_THETA ** (-jnp.arange(half, dtype=F32) / half)
    ang = positions.reshape(s, 1).astype(F32) * inv_freq
    cos, sin = jnp.cos(ang), jnp.sin(ang)
    return jnp.concatenate([cos, cos], axis=-1), jnp.concatenate([-sin, sin], axis=-1)


def _attn_pre_fwd(aqkv, wq, wk, cosf, sinf, heads):
    s = aqkv.shape[0]
    tr = _tile(s, 4096, 8)

    def body(x_ref, wq_ref, wk_ref, c_ref, s_ref, o_ref):
        xv = x_ref[...]
        wv = jnp.where(pl.program_id(1) < heads, wq_ref[...], wk_ref[...])
        y = xv * lax.rsqrt(jnp.mean(xv * xv, axis=-1, keepdims=True) + EPS) * wv
        o_ref[...] = y * c_ref[...] + pltpu.roll(y, HEAD_DIM // 2, 1) * s_ref[...]

    blk = pl.BlockSpec((tr, HEAD_DIM), lambda i, j: (i, j))
    vec = pl.BlockSpec((1, HEAD_DIM), lambda i, j: (0, 0))
    tab = pl.BlockSpec((tr, HEAD_DIM), lambda i, j: (i, 0))
    return pl.pallas_call(
        body, name="attn_pre_fwd", grid=(s // tr, 2 * heads),
        in_specs=[blk, vec, vec, tab, tab], out_specs=blk,
        out_shape=jax.ShapeDtypeStruct((s, 2 * heads * HEAD_DIM), F32),
        compiler_params=_params(("parallel", "parallel")),
    )(aqkv, wq, wk, cosf, sinf)


def _band():
    qi = lax.broadcasted_iota(jnp.int32, (SPAN, 2 * SPAN), 0)
    ki = lax.broadcasted_iota(jnp.int32, (SPAN, 2 * SPAN), 1)
    dist = qi + SPAN - ki
    return (dist >= 0) & (dist <= SPAN), ki >= SPAN


def _sub(g, r, d):
    if d == 1:
        return pl.ds(g * SPAN, SPAN)
    return pl.ds(g * SPAN * d + r, SPAN, stride=d)


def _attn_blocks(s):
    assert s % (SPAN * max(DILATIONS)) == 0
    return [(p_i, d, r, g) for p_i, d in enumerate(DILATIONS) for r in range(d) for g in range(s // (SPAN * d))]


def _attn_fwd(qk, aqkv, w, heads):
    s = qk.shape[0]
    aw = heads * HEAD_DIM

    def body(q_ref, k_ref, v_ref, w_ref, mix_ref, acc_ref, m_ref, l_ref):
        band, own = _band()
        for p_i, d, r, g in _attn_blocks(s):
            rows = _sub(g, r, d)
            kc, vc = k_ref[rows, :], v_ref[rows, :]
            if g == 0:
                kp, vp, mask = kc, vc, band & own
            else:
                mask = band
            kcat = jnp.concatenate([kp, kc], axis=0)
            vcat = jnp.concatenate([vp, vc], axis=0)
            kp, vp = kc, vc
            sc = _dot(q_ref[rows, :] * (HEAD_DIM ** -0.5), kcat, "nt")
            sc = jnp.where(mask, sc, NEG)
            m = jnp.max(sc, axis=-1, keepdims=True)
            if p_i == 0:
                p = jnp.exp(sc - m)
                acc_ref[rows, :] = _dot(p, vcat)
                l_new = jnp.sum(p, axis=-1, keepdims=True)
            else:
                m_old = m_ref[rows, 0:1]
                m = jnp.maximum(m, m_old)
                alpha = jnp.exp(m_old - m)
                p = jnp.exp(sc - m)
                acc_ref[rows, :] = alpha * acc_ref[rows, :] + _dot(p, vcat)
                l_new = alpha * l_ref[rows, 0:1] + jnp.sum(p, axis=-1, keepdims=True)
            m_ref[rows, :] = jnp.broadcast_to(m, (SPAN, HEAD_DIM))
            l_ref[rows, :] = jnp.broadcast_to(l_new, (SPAN, HEAD_DIM))
        den = l_ref[...]
        ob = acc_ref[...] / den
        acc_ref[...] = ob
        m_ref[...] = m_ref[...] + jnp.log(den)
        rstd = lax.rsqrt(jnp.mean(ob * ob, axis=-1, keepdims=True) + EPS)
        mix_ref[...] = (ob * rstd * w_ref[...]).astype(BF16)

    col = lambda off: pl.BlockSpec((s, HEAD_DIM), lambda h: (0, off + h))
    return pl.pallas_call(
        body, name="attn_fwd", grid=(heads,),
        in_specs=[col(0), col(heads), col(2 * heads), pl.BlockSpec((1, HEAD_DIM), lambda h: (0, 0))],
        out_specs=[col(0), col(0), col(0)],
        out_shape=[jax.ShapeDtypeStruct((s, aw), BF16), jax.ShapeDtypeStruct((s, aw), F32),
                   jax.ShapeDtypeStruct((s, aw), F32)],
        scratch_shapes=[pltpu.VMEM((s, HEAD_DIM), F32)],
        compiler_params=_params(("parallel",)),
    )(qk, qk, aqkv, w)


def _attn_merge_bwd(dmix, ob, w, heads):
    s = ob.shape[0]
    tr = _tile(s, 4096, 8)

    def body(d_ref, ob_ref, w_ref, do_ref, dsum_ref, dw_ref):
        d, ov = d_ref[...], ob_ref[...]
        rstd = lax.rsqrt(jnp.mean(ov * ov, axis=-1, keepdims=True) + EPS)
        ohat = ov * rstd
        gw = d * w_ref[...]
        dov = rstd * (gw - ohat * jnp.mean(gw * ohat, axis=-1, keepdims=True))
        do_ref[...] = dov
        dsum_ref[...] = jnp.broadcast_to(jnp.sum(dov * ov, axis=-1, keepdims=True), dov.shape)

        @pl.when((pl.program_id(0) == 0) & (pl.program_id(1) == 0))
        def _():
            dw_ref[...] = jnp.zeros_like(dw_ref)

        dw_ref[...] += jnp.sum(d * ohat, axis=0, keepdims=True)

    blk = pl.BlockSpec((tr, HEAD_DIM), lambda i, h: (i, h))
    vec = pl.BlockSpec((1, HEAD_DIM), lambda i, h: (0, 0))
    aw = heads * HEAD_DIM
    return pl.pallas_call(
        body, name="attn_merge_bwd", grid=(s // tr, heads),
        in_specs=[pl.BlockSpec((tr, HEAD_DIM), lambda i, h: (i, heads + h)), blk, vec],
        out_specs=[blk, blk, vec],
        out_shape=[jax.ShapeDtypeStruct((s, aw), F32), jax.ShapeDtypeStruct((s, aw), F32),
                   jax.ShapeDtypeStruct((1, HEAD_DIM), F32)],
        compiler_params=_params(("arbitrary", "arbitrary")),
    )(dmix, ob, w)


def _attn_bwd(qk, aqkv, do, lse, dsum, heads):
    s = qk.shape[0]
    aw = heads * HEAD_DIM
    scale = HEAD_DIM ** -0.5

    def body(q_ref, k_ref, v_ref, do_ref, l_ref, ds_ref, out_ref):
        band, own = _band()
        dq_ref, dk_ref, dv_ref = out_ref.at[0], out_ref.at[1], out_ref.at[2]
        out_ref[...] = jnp.zeros((3, s, HEAD_DIM), F32)
        for _, d, r, g in _attn_blocks(s):
            rows = _sub(g, r, d)
            qs, dov = q_ref[rows, :] * scale, do_ref[rows, :]
            kc, vc = k_ref[rows, :], v_ref[rows, :]
            if g == 0:
                kp, vp, mask = kc, vc, band & own
            else:
                mask = band
            kcat = jnp.concatenate([kp, kc], axis=0)
            vcat = jnp.concatenate([vp, vc], axis=0)
            p = jnp.where(mask, jnp.exp(_dot(qs, kcat, "nt") - l_ref[rows, 0:1]), 0.0)
            dsc = p * (_dot(dov, vcat, "nt") - ds_ref[rows, 0:1])
            dq_ref[rows, :] += scale * _dot(dsc, kcat)
            dk = _dot(dsc, qs, "tn")
            dv = _dot(p, dov, "tn")
            if g > 0:
                dk_ref[prows, :] += dk_own + dk[:SPAN]
                dv_ref[prows, :] += dv_own + dv[:SPAN]
            dk_own, dv_own = dk[SPAN:], dv[SPAN:]
            if g == s // (SPAN * d) - 1:
                dk_ref[rows, :] += dk_own
                dv_ref[rows, :] += dv_own
            kp, vp, prows = kc, vc, rows

    col = lambda off: pl.BlockSpec((s, HEAD_DIM), lambda h: (0, off + h))
    return pl.pallas_call(
        body, name="attn_bwd", grid=(heads,),
        in_specs=[col(0), col(heads), col(2 * heads), col(0), col(0), col(0)],
        out_specs=pl.BlockSpec((3, s, HEAD_DIM), lambda h: (0, 0, h)),
        out_shape=jax.ShapeDtypeStruct((3, s, aw), F32),
        compiler_params=_params(("parallel",)),
    )(qk, qk, aqkv, do, lse, dsum)


def _attn_pre_bwd(grads, aqkv, wq, wk, cosf, sinf, heads):
    s = aqkv.shape[0]
    tr = _tile(s, 4096, 8)
    nrow = s // tr

    def body(g_ref, x_ref, wq_ref, wk_ref, c_ref, s_ref, dx_ref, dwq_ref, dwk_ref):
        i, j = pl.program_id(0), pl.program_id(1)
        kind = j // heads
        dout = g_ref[0]
        tot_v = dout
        dy = dout * c_ref[...] + pltpu.roll(dout * s_ref[...], HEAD_DIM // 2, 1)
        xv = x_ref[...]
        wv = jnp.where(kind == 0, wq_ref[...], wk_ref[...])
        rstd = lax.rsqrt(jnp.mean(xv * xv, axis=-1, keepdims=True) + EPS)
        xhat = xv * rstd
        gw = dy * wv
        dxn = rstd * (gw - xhat * jnp.mean(gw * xhat, axis=-1, keepdims=True))
        dx_ref[...] = jnp.where(kind == 2, tot_v, dxn).astype(BF16)
        dwc = jnp.sum(dy * xhat, axis=0, keepdims=True)

        @pl.when((i == 0) & (j == 0))
        def _():
            dwq_ref[...] = jnp.zeros_like(dwq_ref)
            dwk_ref[...] = jnp.zeros_like(dwk_ref)

        @pl.when(kind == 0)
        def _():
            dwq_ref[...] += dwc

        @pl.when(kind == 1)
        def _():
            dwk_ref[...] += dwc

    grad = pl.BlockSpec((1, tr, HEAD_DIM), lambda i, j: (j // heads, i, j % heads))
    blk = pl.BlockSpec((tr, HEAD_DIM), lambda i, j: (i, j))
    vec = pl.BlockSpec((1, HEAD_DIM), lambda i, j: (0, 0))
    tab = pl.BlockSpec((tr, HEAD_DIM), lambda i, j: (i, 0))
    return pl.pallas_call(
        body, name="attn_pre_bwd", grid=(nrow, 3 * heads),
        in_specs=[grad, blk, vec, vec, tab, tab], out_specs=[blk, vec, vec],
        out_shape=[jax.ShapeDtypeStruct((s, 3 * heads * HEAD_DIM), BF16),
                   jax.ShapeDtypeStruct((1, HEAD_DIM), F32), jax.ShapeDtypeStruct((1, HEAD_DIM), F32)],
        compiler_params=_params(("arbitrary", "arbitrary")),
    )(grads, aqkv, wq, wk, cosf, sinf)


def _colsum(a, name):
    s, d = a.shape
    tr = _tile(s, 4096, 8)

    def body(a_ref, o_ref):
        @pl.when(pl.program_id(0) == 0)
        def _():
            o_ref[...] = jnp.zeros_like(o_ref)

        o_ref[...] += jnp.sum(a_ref[...], axis=0, keepdims=True)

    return pl.pallas_call(
        body, name=name, grid=(s // tr,),
        in_specs=[pl.BlockSpec((tr, d), lambda i: (i, 0))], out_specs=pl.BlockSpec((1, d), lambda i: (0, 0)),
        out_shape=jax.ShapeDtypeStruct((1, d), F32),
        compiler_params=_params(("arbitrary",)),
    )(a)


def _local_step(x, positions, target, small, w_qkvz, w_bd, w_attn, conv_w, later_weights, ffn_grads_ready,
                rest_grads_ready, after=None):
    s, dmod = x.shape
    heads = dmod // (2 * HEAD_DIM)
    dw = heads * HEAD_DIM
    a_log, dt_bias = small["a_log"].reshape(heads, 1, 1), small["dt_bias"].reshape(heads, 1, 1)
    cosf, sinf = _rope_tables(positions, s)

    h1 = _rmsnorm_fwd(x, small["attn_norm_w"], "norm1_fwd", after=after)
    qkvz = _matmul(h1, w_qkvz, "nn", "proj_qkvz")
    bd = _matmul(h1, w_bd, "nn", "proj_bd")
    aqkv = _matmul(h1, w_attn, "nn", "proj_attn")
    dqkv = _delta_pre_fwd(qkvz, conv_w, heads)
    o_d, *saved = _delta_chunk_fwd(dqkv, bd, a_log, dt_bias, heads)
    mix_a = _delta_post_fwd(o_d, qkvz, small["delta_out_norm_w"], heads)
    qk_rot = _attn_pre_fwd(aqkv, small["q_norm_w"], small["k_norm_w"], cosf, sinf, heads)
    mix_b, ob, lse = _attn_fwd(qk_rot, aqkv, small["attn_out_norm_w"], heads)
    w_out, behind, ffn_weights = later_weights((mix_a, mix_b))
    x1 = _matmul(mix_a, w_out, "nn", "out_proj_a", add=x, b_rows=(0, dw), after=behind)
    x1 = _matmul(mix_b, w_out, "nn", "out_proj_b", add=x1, b_rows=(dw, dw))
    h2 = _rmsnorm_fwd(x1, small["ffn_norm_w"], "norm2_fwd")
    w_gu, w_down = ffn_weights(h2)
    ff = w_down.shape[0]

    def swiglu(accs, vals):
        g, u = accs
        return g, u, g * _sigmoid(g) * u

    gate, up, act = _matmul(h2, w_gu, "nn", "ffn_gate_up", b_cols=(0, ff), b2_cols=(ff, ff), finish=swiglu,
                            out_dtypes=[BF16, BF16, BF16])

    def loss_head(acc, vals):
        err = acc + vals[0] - vals[1]
        part = 0.5 * jnp.sum(jnp.sum(err * err, axis=-1, keepdims=True) * (1.0 / dmod), axis=0, keepdims=True)
        lane = lax.broadcasted_iota(jnp.int32, (1, HEAD_DIM), 1)
        return err * (1.0 / dmod), err * (1.0 / dmod), jnp.where(lane == 0, part, 0.0)

    dy, dy16, loss_row = _matmul(act, w_down, "nn", "ffn_down", tiles=[(x1, 0), (target, 0)], finish=loss_head,
                                 out_dtypes=[F32, BF16], row_sum=True)

    def swiglu_bwd(acc, vals):
        g, u = vals[0].astype(F32), vals[1].astype(F32)
        sg = _sigmoid(g)
        return acc * u * sg * (1.0 + g * (1.0 - sg)), acc * g * sg

    dgate, dup = _matmul(dy16, w_down, "nt", "ffn_down_dx", tiles=[(gate, 0), (up, 0)], finish=swiglu_bwd,
                         out_dtypes=[BF16, BF16])
    g_w_down = _matmul(act, dy16, "tn", "ffn_down_dw")
    g_w_gate = _matmul(h2, dgate, "tn", "ffn_gate_dw")
    g_w_up = _matmul(h2, dup, "tn", "ffn_up_dw")
    behind, and_then = ffn_grads_ready(g_w_gate, g_w_up, g_w_down)
    dh2 = _matmul(dgate, w_gu, "nt", "ffn_gate_dx", b_cols=(0, ff), after=behind)
    dh2 = _matmul(dup, w_gu, "nt", "ffn_up_dx", b_cols=(ff, ff), add=dh2)
    behind = and_then(dh2)
    dx1, dx1_16, g_ffn_norm = _rmsnorm_bwd(dh2, x1, small["ffn_norm_w"], dy, "norm2_bwd", after=behind)
    dmix = _matmul(dx1_16, w_out, "nt", "out_proj_dx")
    g_w_out_a = _matmul(mix_a, dx1_16, "tn", "out_proj_dw_a")
    g_w_out_b = _matmul(mix_b, dx1_16, "tn", "out_proj_dw_b")
    dob, dsum, g_attn_out_norm = _attn_merge_bwd(dmix, ob, small["attn_out_norm_w"], heads)
    grads = _attn_bwd(qk_rot, aqkv, dob, lse, dsum, heads)
    d_aqkv, g_q_norm, g_k_norm = _attn_pre_bwd(grads, aqkv, small["q_norm_w"], small["k_norm_w"], cosf, sinf, heads)
    do_d, dz, g_delta_out_norm = _delta_post_bwd(dmix, o_d, qkvz, small["delta_out_norm_w"], heads)
    ddqkv, dbd = _delta_chunk_bwd(do_d, dqkv, bd, saved, a_log, dt_bias, heads)
    d_qkv_raw, g_conv = _delta_pre_bwd(ddqkv, qkvz, conv_w, heads)
    bd_sums = _colsum(dbd, "bd_colsum")
    g_w_qkv = _matmul(h1, d_qkv_raw, "tn", "proj_qkv_dw")
    g_w_z = _matmul(h1, dz, "tn", "proj_z_dw")
    g_w_bd = _matmul(h1, dbd, "tn", "proj_bd_dw")
    g_w_attn = _matmul(h1, d_aqkv, "tn", "proj_attn_dw")
    behind, and_then = rest_grads_ready(dict(w_qkv=g_w_qkv, w_z=g_w_z, w_bd=g_w_bd, w_attn=g_w_attn,
                                             w_out_a=g_w_out_a, w_out_b=g_w_out_b))
    dh1 = _matmul(d_qkv_raw, w_qkvz, "nt", "proj_qkv_dx", b_cols=(0, 3 * dw), after=behind)
    dh1 = _matmul(dz, w_qkvz, "nt", "proj_z_dx", b_cols=(3 * dw, dw), add=dh1, after=and_then(dh1))
    dh1 = _matmul(d_aqkv, w_attn, "nt", "proj_attn_dx", add=dh1)
    dh1 = _matmul(dbd, w_bd, "nt", "proj_bd_dx", add=dh1)
    grad_x, _, g_attn_norm = _rmsnorm_bwd(dh1, x, small["attn_norm_w"], dx1, "norm1_bwd")
    small_grads = dict(
        attn_norm_w=g_attn_norm, a_log=bd_sums[:, 2 * heads:3 * heads], dt_bias=bd_sums[:, heads:2 * heads],
        delta_out_norm_w=g_delta_out_norm, q_norm_w=g_q_norm, k_norm_w=g_k_norm,
        attn_out_norm_w=g_attn_out_norm, ffn_norm_w=g_ffn_norm, conv_w=g_conv)
    return loss_row, grad_x, small_grads


def _adamw(w, g, m, v, name):
    r, c = w.shape
    tr = _tile(r, 256, 8)

    def body(w_ref, g_ref, m_ref, v_ref, d_ref, nm_ref, nv_ref):
        gv = g_ref[...]
        nm = ADAM_B1 * m_ref[...] + (1.0 - ADAM_B1) * gv
        nv = ADAM_B2 * v_ref[...] + (1.0 - ADAM_B2) * (gv * gv)
        m_hat = nm / (1.0 - ADAM_B1 ** ADAM_STEP)
        v_hat = nv / (1.0 - ADAM_B2 ** ADAM_STEP)
        d_ref[...] = -ADAM_LR * (m_hat / (jnp.sqrt(v_hat) + ADAM_EPS) + ADAM_WD * w_ref[...])
        nm_ref[...] = nm
        nv_ref[...] = nv

    blk = pl.BlockSpec((tr, c), lambda i: (i, 0))
    return pl.pallas_call(
        body, name=name, grid=(r // tr,),
        in_specs=[blk] * 4, out_specs=[blk] * 3,
        out_shape=[jax.ShapeDtypeStruct((r, c), F32)] * 3,
        compiler_params=_params(("parallel",)),
    )(w, g, m, v)


def _add_half_bf16(g, b, place, name):
    n, half, c = b.shape
    tr = _tile(half, 512, 16)
    nb = half // tr

    def body(place_ref, g_ref, b_ref, o_ref):
        o_ref[...] = (g_ref[...].astype(F32) + b_ref[...].astype(F32)).astype(BF16)

    blk = pl.BlockSpec((1, tr, c), lambda i, j, p: (i, j, 0))
    return pl.pallas_call(
        body, name=name,
        grid_spec=pltpu.PrefetchScalarGridSpec(
            num_scalar_prefetch=1, grid=(n, nb),
            in_specs=[pl.BlockSpec((1, tr, c), lambda i, j, p: (i, p[0] * nb + j, 0)), blk], out_specs=blk),
        out_shape=jax.ShapeDtypeStruct((n, half, c), BF16),
        compiler_params=_params(("parallel", "parallel")),
    )(place, g, b)


def _sum4_f32(mine, others, place, name):
    _, half, c = mine.shape
    tr = _tile(half, 512, 16)
    nb = half // tr

    def body(place_ref, a_ref, b_ref, o_ref):
        acc = a_ref[0].astype(F32)
        for j in range(3):
            acc = acc + b_ref[j].astype(F32)
        o_ref[...] = acc

    return pl.pallas_call(
        body, name=name,
        grid_spec=pltpu.PrefetchScalarGridSpec(
            num_scalar_prefetch=1, grid=(nb,),
            in_specs=[pl.BlockSpec((1, tr, c), lambda i, p: (p[1], i, 0)),
                      pl.BlockSpec((3, tr, c), lambda i, p: (0, i, 0))],
            out_specs=pl.BlockSpec((tr, c), lambda i, p: (p[0] * nb + i, 0))),
        out_shape=jax.ShapeDtypeStruct((2 * half, c), F32),
        compiler_params=_params(("parallel",)),
    )(place, mine, others)


def _place():
    x, y, c = lax.axis_index("x"), lax.axis_index("y"), lax.axis_index("c")
    other_chips = [(1 - x, y), (x, 1 - y), (1 - x, 1 - y)]
    return x, y, c, (x, y, 1 - c), other_chips


ANY = pl.BlockSpec(memory_space=pl.ANY)


def _remote(k, src, dst, to, send_sems, recv_sems):
    return pltpu.make_async_remote_copy(src_ref=src, dst_ref=dst, send_sem=send_sems.at[k], recv_sem=recv_sems.at[k],
                                        device_id=to, device_id_type=MESH)


def _half(ref, lead, hc):
    if lead is None:
        half = ref.shape[0] // 2
        return ref.at[pl.ds(hc * half, half), :]
    half = ref.shape[1] // 2
    return ref.at[lead, pl.ds(hc * half, half), :]


def _all_gather_weights(slots, whole, name):
    nt, nw = len(slots), len(whole)
    base_w, base_f, base_d = 2 * nt, 2 * nt + 3 * nw, 4 * nt + 3 * nw

    def quarter(ref, lead, hc, q):
        quart = ref.shape[1] // 4
        return ref.at[lead, pl.ds((2 * hc + q) * quart, quart), :]

    def body(*refs):
        ins, outs = refs[:nt + nw], refs[nt + nw:2 * (nt + nw)]
        sems = refs[2 * (nt + nw):]
        x, y, c, sibling, chips = _place()
        me, xn, yn, dg = 2 * x + y, 2 * (1 - x) + y, 2 * x + 1 - y, 2 * (1 - x) + 1 - y
        to_x, to_y = (1 - x, y, c), (x, 1 - y, c)
        cps = []
        for t in range(nt):
            cps.append(_remote(2 * t, _half(ins[t], me, c), _half(outs[t], me, c), to_x, *sems))
            cps.append(_remote(2 * t + 1, _half(ins[t], me, c), _half(outs[t], me, c), to_y, *sems))
        for j, (px, py) in enumerate(chips):
            for t in range(nw):
                cps.append(_remote(base_w + j * nw + t, ins[nt + t].at[me], outs[nt + t].at[me], (px, py, c), *sems))
        for cp in cps:
            cp.start()

        def start(k, ref, to):
            cp = _remote(k, ref, ref, to, *sems)
            cp.start()
            cps.append(cp)

        for t in range(nt):
            landed = _half(outs[t], xn, c)
            _remote(2 * t, landed, landed, to_x, *sems).wait_recv()
            start(base_f + 2 * t, quarter(outs[t], xn, c, 0), to_y)
            start(base_d + 3 * t, landed, sibling)
            landed = _half(outs[t], yn, c)
            _remote(2 * t + 1, landed, landed, to_y, *sems).wait_recv()
            start(base_f + 2 * t + 1, quarter(outs[t], yn, c, 1), to_x)
            start(base_d + 3 * t + 1, landed, sibling)
        for t in range(nt):
            q0, q1 = quarter(outs[t], dg, c, 0), quarter(outs[t], dg, c, 1)
            _remote(base_f + 2 * t, q0, q0, to_y, *sems).wait_recv()
            _remote(base_f + 2 * t + 1, q1, q1, to_x, *sems).wait_recv()
            start(base_d + 3 * t + 2, _half(outs[t], dg, c), sibling)
        for j, (px, py) in enumerate(chips):
            for t in range(nw):
                landed = outs[nt + t].at[2 * px + py]
                _remote(base_w + j * nw + t, landed, landed, (px, py, c), *sems).wait_recv()
        for t in range(nt):
            for j, chip in enumerate((xn, yn, dg)):
                other = _half(outs[t], chip, 1 - c)
                _remote(base_d + 3 * t + j, other, other, sibling, *sems).wait_recv()
        for cp in cps:
            cp.wait_send()

    arrays = list(slots) + list(whole)
    n_sem = 7 * nt + 3 * nw
    return pl.pallas_call(
        body, name=name, in_specs=[ANY] * len(arrays), out_specs=[ANY] * len(arrays),
        input_output_aliases={i: i for i in range(len(arrays))},
        out_shape=[jax.ShapeDtypeStruct(a.shape, a.dtype) for a in arrays],
        scratch_shapes=[pltpu.SemaphoreType.DMA((n_sem,)), pltpu.SemaphoreType.DMA((n_sem,))],
    )(*arrays)


def _join_halves(fs, name):
    nt = len(fs)

    def body(*refs):
        in_refs, out_refs, (send_sems, recv_sems) = refs[:nt], refs[nt:2 * nt], refs[2 * nt:]
        _, _, c, sibling, _ = _place()
        cps = [_remote(t, _half(in_refs[t], None, c), _half(out_refs[t], None, c), sibling, send_sems, recv_sems)
               for t in range(nt)]
        for cp in cps:
            cp.start()
        for t in range(nt):
            theirs = _half(out_refs[t], None, 1 - c)
            _remote(t, theirs, theirs, sibling, send_sems, recv_sems).wait_recv()
        for cp in cps:
            cp.wait_send()

    return pl.pallas_call(
        body, name=name, in_specs=[ANY] * nt, out_specs=[ANY] * nt,
        input_output_aliases={i: i for i in range(nt)},
        out_shape=[jax.ShapeDtypeStruct(f.shape, f.dtype) for f in fs],
        scratch_shapes=[pltpu.SemaphoreType.DMA((nt,)), pltpu.SemaphoreType.DMA((nt,))],
    )(*fs)


HBM = pl.BlockSpec(memory_space=pltpu.HBM)
SEM = pl.BlockSpec(memory_space=pltpu.SEMAPHORE)
EFFECT = pltpu.SideEffectType.DATAFLOW_SIDE_EFFECTING


def _split_start(arrays, after, plan, n_copies, name):
    na = len(arrays)

    def body(*refs):
        ins, send_sems, recv_sems = refs[:na], refs[na + 1], refs[na + 2]
        outs, token = refs[na + 3:2 * na + 3], refs[2 * na + 3]
        for k, (src, dst, to) in enumerate(plan(ins, outs)):
            _remote(k, src, dst, to, send_sems, recv_sems).start()
        token[...] = jnp.zeros_like(token)

    res = pl.pallas_call(
        body, name=name,
        out_shape=(pltpu.SemaphoreType.DMA((n_copies,)), pltpu.SemaphoreType.DMA((n_copies,)),
                   *[pltpu.HBM(a.shape, a.dtype) for a in arrays], jax.ShapeDtypeStruct((8, HEAD_DIM), F32)),
        in_specs=[HBM] * na + [ANY],
        out_specs=(SEM, SEM, *[HBM] * na, pl.BlockSpec(memory_space=pltpu.VMEM)),
        input_output_aliases={i: 2 + i for i in range(na)},
        compiler_params=pltpu.CompilerParams(has_side_effects=EFFECT),
    )(*[pltpu.with_memory_space_constraint(a, pltpu.HBM) for a in arrays], after)
    return res[0], res[1], list(res[2:2 + na]), res[2 + na]


def _split_wait(send_sems, recv_sems, arrays, after, plan, name):
    na = len(arrays)
    after = list(after) if isinstance(after, (list, tuple)) else [after]

    def body(*refs):
        ins, send, recv, outs = refs[:na], refs[na], refs[na + 1], refs[na + 2 + len(after):]
        for k, (src, dst, to) in enumerate(plan(ins, outs)):
            cp = _remote(k, src, dst, to, send, recv)
            cp.wait_send()
            cp.wait_recv()

    res = pl.pallas_call(
        body, name=name, out_shape=tuple(pltpu.HBM(a.shape, a.dtype) for a in arrays),
        in_specs=[HBM] * na + [SEM, SEM] + [ANY] * len(after), out_specs=tuple([HBM] * na),
        input_output_aliases={i: i for i in range(na)},
        compiler_params=pltpu.CompilerParams(has_side_effects=EFFECT),
    )(*arrays, send_sems, recv_sems, *after)
    return list(res)


def _gather_plan(nt):
    def plan(ins, outs):
        x, y, c, _, chips = _place()
        me = 2 * x + y
        return [(_half(ins[t], me, c), _half(outs[t], me, c), (px, py, c)) for px, py in chips for t in range(nt)]
    return plan


def _gather_landed_plan(nt):
    def plan(ins, outs):
        _, _, c, _, chips = _place()
        return [(_half(outs[t], 2 * px + py, c), _half(outs[t], 2 * px + py, c), (px, py, c))
                for px, py in chips for t in range(nt)]
    return plan


def _scatter_plan(nt):
    def plan(ins, outs):
        _, _, c, _, chips = _place()
        return [(ins[t].at[2 * px + py], outs[nt + t].at[j], (px, py, c))
                for j, (px, py) in enumerate(chips) for t in range(nt)]
    return plan


def _pass_plan(nt):
    def plan(ins, outs):
        _, _, c, sibling, chips = _place()
        return [(_half(ins[t], 2 * px + py, c), _half(outs[t], 2 * px + py, c), sibling)
                for px, py in chips for t in range(nt)]
    return plan


def _pass_landed_plan(nt):
    def plan(ins, outs):
        _, _, c, sibling, chips = _place()
        return [(_half(outs[t], 2 * px + py, c), _half(outs[t], 2 * px + py, 1 - c), sibling)
                for px, py in chips for t in range(nt)]
    return plan


def _swap_plan(nt):
    def plan(ins, outs):
        _, _, c, sibling, _ = _place()
        res = []
        for t in range(nt):
            half = ins[t].shape[1] // 2
            res.append((ins[t].at[:, pl.ds((1 - c) * half, half), :], outs[nt + t], sibling))
        return res
    return plan


def _pass_halves_to_sibling(slots, name):
    nt = len(slots)

    def body(*refs):
        ins, outs, (send_sems, recv_sems) = refs[:nt], refs[nt:2 * nt], refs[2 * nt:]
        _, _, c, sibling, chips = _place()
        cps = [_remote(j * nt + t, _half(ins[t], 2 * px + py, c), _half(outs[t], 2 * px + py, c), sibling,
                       send_sems, recv_sems)
               for j, (px, py) in enumerate(chips) for t in range(nt)]
        for cp in cps:
            cp.start()
        for j, (px, py) in enumerate(chips):
            for t in range(nt):
                other = _half(outs[t], 2 * px + py, 1 - c)
                _remote(j * nt + t, other, other, sibling, send_sems, recv_sems).wait_recv()
        for cp in cps:
            cp.wait_send()

    return pl.pallas_call(
        body, name=name, in_specs=[ANY] * nt, out_specs=[ANY] * nt,
        input_output_aliases={i: i for i in range(nt)},
        out_shape=[jax.ShapeDtypeStruct(a.shape, a.dtype) for a in slots],
        scratch_shapes=[pltpu.SemaphoreType.DMA((3 * nt,)), pltpu.SemaphoreType.DMA((3 * nt,))],
    )(*slots)


def _all_reduce_small(v):
    r, lanes = v.shape

    def body(v_ref, out_ref, buf, send_sems, recv_sems):
        x, y, c, sibling, chips = _place()

        def slot(px, py, pc):
            return buf.at[4 * px + 2 * py + pc]

        def copy(k, block, to, src=None):
            return pltpu.make_async_remote_copy(src_ref=slot(*block) if src is None else src, dst_ref=slot(*block),
                                                send_sem=send_sems.at[k], recv_sem=recv_sems.at[k],
                                                device_id=to, device_id_type=MESH)

        me = (x, y, c)
        buf[4 * x + 2 * y + c] = v_ref[...]
        first = [copy(0, me, sibling, src=v_ref)]
        first += [copy(1 + j, me, (*chip, c), src=v_ref) for j, chip in enumerate(chips)]
        for cp in first:
            cp.start()
        passed = [copy(4 + j, (*chip, c), sibling) for j, chip in enumerate(chips)]
        for j, chip in enumerate(chips):
            copy(1 + j, (*chip, c), me).wait_recv()
            passed[j].start()
        copy(0, (x, y, 1 - c), me).wait_recv()
        for j, chip in enumerate(chips):
            copy(4 + j, (*chip, 1 - c), me).wait_recv()
        for cp in first + passed:
            cp.wait_send()
        acc = buf[0]
        for k in range(1, 8):
            acc = acc + buf[k]
        out_ref[...] = acc

    vmem = pl.BlockSpec(memory_space=pltpu.VMEM)
    return pl.pallas_call(
        body, name="all_reduce_small", in_specs=[vmem], out_specs=vmem,
        out_shape=jax.ShapeDtypeStruct((r, lanes), F32),
        scratch_shapes=[pltpu.VMEM((8, r, lanes), F32), pltpu.SemaphoreType.DMA((7,)), pltpu.SemaphoreType.DMA((7,))],
    )(v)


def _size(shape):
    n = 1
    for d in shape:
        n *= d
    return n


def _pack_small(parts):
    rows = []
    for p in parts:
        f = p.reshape(-1).astype(F32)
        n = -(-f.shape[0] // HEAD_DIM) * HEAD_DIM
        rows.append(jnp.pad(f, (0, n - f.shape[0])).reshape(-1, HEAD_DIM))
    a = jnp.concatenate(rows, axis=0)
    return jnp.pad(a, ((0, -a.shape[0] % 8), (0, 0)))


def _unpack_small(a, shapes):
    out, row = [], 0
    for shp in shapes:
        nrows = -(-_size(shp) // HEAD_DIM)
        out.append(a[row:row + nrows].reshape(-1)[:_size(shp)].reshape(shp))
        row += nrows
    return out


SMALL = ["attn_norm_w", "a_log", "dt_bias", "delta_out_norm_w", "q_norm_w", "k_norm_w", "attn_out_norm_w", "ffn_norm_w"]
BIG = ["w_in", "w_out", "w_gate_up", "w_down"]
ORDER = ["attn_norm_w", "w_in", "conv_w", "a_log", "dt_bias", "delta_out_norm_w", "q_norm_w", "k_norm_w",
         "attn_out_norm_w", "w_out", "ffn_norm_w", "w_gate_up", "w_down"]


def kernel(x, positions, attn_norm_w, w_in, conv_w, a_log, dt_bias, delta_out_norm_w, q_norm_w, k_norm_w, attn_out_norm_w, w_out, ffn_norm_w, w_gate_up, w_down, loss_target, m_attn_norm_w, m_w_in, m_conv_w, m_a_log, m_dt_bias, m_delta_out_norm_w, m_q_norm_w, m_k_norm_w, m_attn_out_norm_w, m_w_out, m_ffn_norm_w, m_w_gate_up, m_w_down, v_attn_norm_w, v_w_in, v_conv_w, v_a_log, v_dt_bias, v_delta_out_norm_w, v_q_norm_w, v_k_norm_w, v_attn_out_norm_w, v_w_out, v_ffn_norm_w, v_w_gate_up, v_w_down):
    wts = dict(attn_norm_w=attn_norm_w, w_in=w_in, conv_w=conv_w, a_log=a_log, dt_bias=dt_bias,
               delta_out_norm_w=delta_out_norm_w, q_norm_w=q_norm_w, k_norm_w=k_norm_w,
               attn_out_norm_w=attn_out_norm_w, w_out=w_out, ffn_norm_w=ffn_norm_w, w_gate_up=w_gate_up, w_down=w_down)
    mom = dict(attn_norm_w=m_attn_norm_w, w_in=m_w_in, conv_w=m_conv_w, a_log=m_a_log, dt_bias=m_dt_bias,
               delta_out_norm_w=m_delta_out_norm_w, q_norm_w=m_q_norm_w, k_norm_w=m_k_norm_w,
               attn_out_norm_w=m_attn_out_norm_w, w_out=m_w_out, ffn_norm_w=m_ffn_norm_w, w_gate_up=m_w_gate_up,
               w_down=m_w_down)
    var = dict(attn_norm_w=v_attn_norm_w, w_in=v_w_in, conv_w=v_conv_w, a_log=v_a_log, dt_bias=v_dt_bias,
               delta_out_norm_w=v_delta_out_norm_w, q_norm_w=v_q_norm_w, k_norm_w=v_k_norm_w,
               attn_out_norm_w=v_attn_out_norm_w, w_out=v_w_out, ffn_norm_w=v_ffn_norm_w, w_gate_up=v_w_gate_up,
               w_down=v_w_down)
    dmod = x.shape[2]
    heads = dmod // (2 * HEAD_DIM)
    dw = heads * HEAD_DIM
    chip = 2 * lax.axis_index("x") + lax.axis_index("y")
    core = lax.axis_index("c")
    n_in, n_out, n_gu, n_down, n_conv = (w_in.shape[2], w_out.shape[1], w_gate_up.shape[2], w_down.shape[1],
                                         conv_w.shape[2])

    def slots_of(w, dtype):
        shard = w[0].astype(dtype)
        return lax.dynamic_update_index_in_dim(lax.empty((4,) + shard.shape, dtype), shard, chip, axis=0)

    s_in, s_conv = _all_gather_weights([slots_of(w_in, BF16)], [slots_of(conv_w, F32)], "all_gather_w_in")
    later = [slots_of(w_out, BF16), slots_of(w_gate_up, BF16), slots_of(w_down, BF16)]
    w_send, w_recv, later, started = _split_start(later, s_conv, _gather_plan(3), 9, "gather_rest_start")
    by_cols = lambda a: a.transpose(1, 0, 2).reshape(a.shape[1], 4 * a.shape[2])
    w_in_f, conv_f = by_cols(s_in), by_cols(s_conv)
    w_bd = jnp.pad(w_in_f[:, 4 * dw:4 * dw + 2 * heads], ((0, 0), (0, HEAD_DIM - 2 * heads)))
    small = {n: wts[n] for n in SMALL}
    place = jnp.stack([core, chip]).astype(jnp.int32)
    to_slots = lambda a: a.reshape(a.shape[0], 4, a.shape[1] // 4).transpose(1, 0, 2)

    def later_weights(after):
        landed = _split_wait(w_send, w_recv, later, after, _gather_landed_plan(3), "gather_rest_wait")
        s_out, = _pass_halves_to_sibling(landed[:1], "gather_out_pass")
        p_send, p_recv, passing, token = _split_start(landed[1:], s_out, _pass_plan(2), 6, "gather_ffn_pass_start")

        def ffn_weights(after):
            s_gu, s_down = _split_wait(p_send, p_recv, passing, after, _pass_landed_plan(2), "gather_ffn_pass_wait")
            return by_cols(s_gu), s_down.reshape(4 * n_down, dmod)

        return s_out.reshape(4 * n_out, dmod), token, ffn_weights

    ffn = {}

    def ffn_grads_ready(g_gate, g_up, g_down):
        gs = [to_slots(jnp.concatenate([g_gate, g_up], axis=1)), g_down.reshape(4, n_down, dmod)]
        zones = [lax.empty((4, g.shape[1] // 2, g.shape[2]), BF16) for g in gs]
        s_send, s_recv, s_bufs, swapping = _split_start(gs + zones, g_gate, _swap_plan(2), 2, "swap_ffn_start")

        def and_then(after):
            g_gu, g_dn, b_gu, b_dn = _split_wait(s_send, s_recv, s_bufs, after, _swap_plan(2), "swap_ffn_wait")
            sums = [_add_half_bf16(g_gu, b_gu, place, "chip_partial_sum_w_gate_up"),
                    _add_half_bf16(g_dn, b_dn, place, "chip_partial_sum_w_down")]
            zones3 = [lax.empty((3,) + p.shape[1:], BF16) for p in sums]
            ffn["send"], ffn["recv"], ffn["bufs"], token = _split_start(sums + zones3, b_gu, _scatter_plan(2), 6,
                                                                        "scatter_ffn_start")
            return token

        return swapping, and_then

    rest = {}

    def rest_grads_ready(bg):
        gs = [to_slots(jnp.concatenate([bg["w_qkv"], bg["w_z"], bg["w_bd"][:, :2 * heads], bg["w_attn"]], axis=1)),
              jnp.concatenate([bg["w_out_a"], bg["w_out_b"]], axis=0).reshape(4, n_out, dmod)]
        zones = [lax.empty((4, g.shape[1] // 2, g.shape[2]), BF16) for g in gs]
        s_send, s_recv, s_bufs, swapping = _split_start(gs + zones, bg["w_attn"], _swap_plan(2), 2, "swap_rest_start")

        def and_then(after):
            g_in, g_out, b_in, b_out = _split_wait(s_send, s_recv, s_bufs, after, _swap_plan(2), "swap_rest_wait")
            sums = [_add_half_bf16(g_in, b_in, place, "chip_partial_sum_w_in"),
                    _add_half_bf16(g_out, b_out, place, "chip_partial_sum_w_out")]
            zones3 = [lax.empty((3,) + p.shape[1:], BF16) for p in sums]
            rest["send"], rest["recv"], rest["bufs"], token = _split_start(sums + zones3, b_in, _scatter_plan(2), 6,
                                                                           "scatter_rest_start")
            return token

        return swapping, and_then

    loss_row, grad_x, sg = _local_step(
        x[0], positions[0], loss_target[0], small, w_in_f[:, :4 * dw], w_bd, w_in_f[:, 4 * dw + 2 * heads:], conv_f,
        later_weights, ffn_grads_ready, rest_grads_ready, after=started)

    r_send, r_recv, r_bufs = rest["send"], rest["recv"], rest["bufs"]
    sum_gu, sum_down, got_gu, got_down = _split_wait(ffn["send"], ffn["recv"], ffn["bufs"], grad_x,
                                                     _scatter_plan(2), "scatter_ffn_wait")
    g_big = dict(zip(["w_gate_up", "w_down"], _join_halves(
        [_sum4_f32(sum_gu, got_gu, place, "grad_total_w_gate_up"),
         _sum4_f32(sum_down, got_down, place, "grad_total_w_down")], "join_ffn_halves")))
    grads, deltas, new_m, new_v = {}, {}, {}, {}

    def adamw_big(n):
        shp = wts[n].shape
        d, nm, nv = _adamw(wts[n][0], g_big[n], mom[n][0], var[n][0], "adamw_" + n)
        grads[n], deltas[n], new_m[n], new_v[n] = g_big[n].reshape(shp), d.reshape(shp), nm.reshape(shp), nv.reshape(shp)
        return d

    done = [adamw_big("w_gate_up"), adamw_big("w_down")]
    sum_in, sum_out, got_in, got_out = _split_wait(r_send, r_recv, r_bufs, done, _scatter_plan(2), "scatter_rest_wait")
    g_big.update(zip(["w_in", "w_out"], _join_halves(
        [_sum4_f32(sum_in, got_in, place, "grad_total_w_in"),
         _sum4_f32(sum_out, got_out, place, "grad_total_w_out")], "join_rest_halves")))
    adamw_big("w_in")
    adamw_big("w_out")

    reduced = _all_reduce_small(_pack_small([sg[n] for n in SMALL] + [sg["conv_w"], loss_row]))
    red = _unpack_small(reduced, [wts[n].shape for n in SMALL] + [(4, 4 * n_conv), (1, HEAD_DIM)])
    g_small = dict(zip(SMALL, red[:len(SMALL)]))
    g_conv_full, loss_out = red[len(SMALL)], red[len(SMALL) + 1]
    g_small["conv_w"] = lax.dynamic_slice_in_dim(g_conv_full, chip * n_conv, n_conv, axis=1).reshape(conv_w.shape)

    names = SMALL + ["conv_w"]
    shapes = [wts[n].shape for n in names]
    d, nm, nv = _adamw(_pack_small([wts[n] for n in names]), _pack_small([g_small[n] for n in names]),
                       _pack_small([mom[n] for n in names]), _pack_small([var[n] for n in names]), "adamw_small")
    for n, dd, mm, vv in zip(names, _unpack_small(d, shapes), _unpack_small(nm, shapes), _unpack_small(nv, shapes)):
        grads[n], deltas[n], new_m[n], new_v[n] = g_small[n], dd, mm, vv
    return (loss_out[0, 0], grad_x[None], *[grads[n] for n in ORDER], *[deltas[n] for n in ORDER],
            *[new_m[n] for n in ORDER], *[new_v[n] for n in ORDER])
```

```python
import jax
import jax.numpy as jnp
from jax import lax
from jax.experimental import pallas as pl
from jax.experimental.pallas import tpu as pltpu

F32 = jnp.float32
BF16 = jnp.bfloat16
HEAD_DIM = 128
CHUNK = 128
INV_BLOCK = 64
SPAN = 128
DILATIONS = (1, 4, 16)
ROPE_THETA = 10000.0
EPS = 1e-6
NEG = -1e30
ADAM_LR, ADAM_B1, ADAM_B2, ADAM_EPS, ADAM_WD, ADAM_STEP = 0.001, 0.9, 0.999, 1e-08, 0.01, 10
VMEM_LIMIT = 48 * 1024 * 1024
MATMUL_VMEM = 44 * 1024 * 1024
MESH = pl.DeviceIdType.MESH

_DN = {"nn": (((1,), (0,)), ((), ())), "nt": (((1,), (1,)), ((), ())), "tn": (((0,), (0,)), ((), ()))}


def _dot(a, b, mode="nn"):
    (ca, cb), _ = _DN[mode]
    if a.ndim == 3:
        dn = (((ca[0] + 1,), (cb[0] + 1,)), ((0,), (0,)))
    else:
        dn = _DN[mode]
    return lax.dot_general(a.astype(BF16), b.astype(BF16), dn, preferred_element_type=F32)


def _rsum(x):
    return jnp.sum(x, axis=-1, keepdims=True)


def _csum(x):
    return jnp.sum(x, axis=-2, keepdims=True)


def _tile(dim, pref, unit=128):
    t = (min(pref, dim) // unit) * unit
    while t >= unit:
        if dim % t == 0:
            return t
        t -= unit
    return dim


def _params(sem):
    return pltpu.CompilerParams(dimension_semantics=sem, vmem_limit_bytes=VMEM_LIMIT)


def _sigmoid(x):
    return 1.0 / (1.0 + jnp.exp(-x))


def _matmul(a, b, mode, name, add=None, out_dtype=F32, a_cols=None, b_cols=None, b_rows=None,
            tiles=(), finish=None, out_dtypes=(), row_sum=False, after=None, b2_cols=None):
    if mode == "tn":
        out_dtype = BF16
    a_off, a_w = a_cols if a_cols else (0, a.shape[1])
    b_off, b_w = b_cols if b_cols else (0, b.shape[1])
    br_off, br_n = b_rows if b_rows else (0, b.shape[0])
    if mode == "nn":
        m, k, n = a.shape[0], a_w, b_w
        assert br_n == k
    elif mode == "nt":
        m, k, n = a.shape[0], a_w, b.shape[0]
        assert b_w == k
    else:
        k, m, n = a.shape[0], a_w, b_w
        assert b.shape[0] == k
    if finish is None:
        out_dtypes = [out_dtype]
        if add is None:
            finish = lambda acc, vals: (acc,)
        else:
            tiles = [(add, 0)]
            finish = lambda acc, vals: (acc + vals[0].astype(F32),)
    n_b = 1 if b2_cols is None else 2
    sides = sum(jnp.dtype(t.dtype).itemsize for t, _ in tiles) + sum(jnp.dtype(d).itemsize for d in out_dtypes)

    def vmem(tm_, tn_, tk_):
        ops = tm_ * tk_ * jnp.dtype(a.dtype).itemsize + n_b * tk_ * tn_ * jnp.dtype(b.dtype).itemsize
        return 2 * (ops + tm_ * tn_ * sides) + (n_b * tm_ * tn_ * 4 if tk_ < k else 0)

    tm, tn = _tile(m, 1024, 128), _tile(n, 1024, 128)
    tall = _tile(m, 1536, 128), _tile(n, 512, 128)
    if tall[0] * tall[1] > tm * tn and vmem(*tall, k) <= MATMUL_VMEM:
        tm, tn = tall
    tk = next(t for t in [_tile(k, p, 128) for p in (4096, 2048, 1024, 512, 128)] if vmem(tm, tn, t) <= MATMUL_VMEM)
    if mode == "nn":
        assert a_off % tk == 0 and b_off % tn == 0 and br_off % tk == 0
        a_spec = pl.BlockSpec((tm, tk), lambda i, j, kk: (i, kk + a_off // tk))
        b_spec = pl.BlockSpec((tk, tn), lambda i, j, kk: (kk + br_off // tk, j + b_off // tn))
    elif mode == "nt":
        assert a_off % tk == 0 and b_off % tk == 0
        a_spec = pl.BlockSpec((tm, tk), lambda i, j, kk: (i, kk + a_off // tk))
        b_spec = pl.BlockSpec((tn, tk), lambda i, j, kk: (j, kk + b_off // tk))
    else:
        assert a_off % tm == 0 and b_off % tn == 0
        a_spec = pl.BlockSpec((tk, tm), lambda i, j, kk: (kk, i + a_off // tm))
        b_spec = pl.BlockSpec((tk, tn), lambda i, j, kk: (kk, j + b_off // tn))
    nk = k // tk
    n_tiles, n_out = len(tiles), len(out_dtypes)
    extra = [] if after is None else [after]
    first_out = 1 + n_b + n_tiles + len(extra)

    def body(*refs):
        a_ref, b_refs = refs[0], refs[1:1 + n_b]
        tile_refs, out_refs = refs[1 + n_b:1 + n_b + n_tiles], refs[first_out:first_out + n_out]
        acc_refs = refs[-n_b:] if nk > 1 else ()
        kk = pl.program_id(2)
        first_tile = (pl.program_id(0) == 0) & (pl.program_id(1) == 0)

        av = a_ref[...]
        if nk > 1:
            @pl.when(kk == 0)
            def _():
                for acc_ref in acc_refs:
                    acc_ref[...] = jnp.zeros_like(acc_ref)

            for acc_ref, b_ref in zip(acc_refs, b_refs):
                acc_ref[...] += _dot(av, b_ref[...], mode)

        @pl.when(kk == nk - 1)
        def _():
            if nk > 1:
                accs = [acc_ref[...] for acc_ref in acc_refs]
            else:
                accs = [_dot(av, b_ref[...], mode) for b_ref in b_refs]
            res = finish(accs[0] if n_b == 1 else accs, [t[...] for t in tile_refs])
            for o_ref, r in zip(out_refs, res):
                o_ref[...] = r.astype(o_ref.dtype)
            if row_sum:
                row_ref = refs[first_out + n_out]

                @pl.when(first_tile)
                def _():
                    row_ref[...] = res[n_out]

                @pl.when(jnp.logical_not(first_tile))
                def _():
                    row_ref[...] += res[n_out]

    in_specs = [a_spec, b_spec]
    args = [a, b]
    if b2_cols is not None:
        assert mode == "nn" and b2_cols[1] == n and b2_cols[0] % tn == 0
        in_specs.append(pl.BlockSpec((tk, tn), lambda i, j, kk: (kk + br_off // tk, j + b2_cols[0] // tn)))
        args.append(b)
    for arr, off in tiles:
        assert off % tn == 0
        in_specs.append(pl.BlockSpec((tm, tn), lambda i, j, kk, off=off: (i, j + off // tn)))
        args.append(arr)
    in_specs += [ANY] * len(extra)
    args += extra
    out_specs = [pl.BlockSpec((tm, tn), lambda i, j, kk: (i, j))] * n_out
    out_shape = [jax.ShapeDtypeStruct((m, n), dt) for dt in out_dtypes]
    if row_sum:
        out_specs.append(pl.BlockSpec((1, HEAD_DIM), lambda i, j, kk: (0, 0)))
        out_shape.append(jax.ShapeDtypeStruct((1, HEAD_DIM), F32))
    res = pl.pallas_call(
        body, name=name, grid=(m // tm, n // tn, nk),
        in_specs=in_specs, out_specs=out_specs, out_shape=out_shape,
        scratch_shapes=[pltpu.VMEM((tm, tn), F32)] * (n_b if nk > 1 else 0),
        compiler_params=_params(("arbitrary",) * 3 if row_sum else ("parallel", "parallel", "arbitrary")),
    )(*args)
    return res[0] if len(res) == 1 else res


def _rmsnorm_fwd(x, w, name, after=None):
    s, d = x.shape
    tr = _tile(s, 1024, 8)

    def body(x_ref, w_ref, *rest):
        h_ref = rest[-1]
        xv = x_ref[...]
        rstd = lax.rsqrt(jnp.mean(xv * xv, axis=-1, keepdims=True) + EPS)
        h_ref[...] = (xv * rstd * w_ref[...]).astype(BF16)

    extra = [] if after is None else [after]
    return pl.pallas_call(
        body, name=name, grid=(s // tr,),
        in_specs=[pl.BlockSpec((tr, d), lambda i: (i, 0)), pl.BlockSpec((1, d), lambda i: (0, 0))] + [ANY] * len(extra),
        out_specs=pl.BlockSpec((tr, d), lambda i: (i, 0)),
        out_shape=jax.ShapeDtypeStruct((s, d), BF16),
        compiler_params=_params(("parallel",)),
    )(x, w, *extra)


def _rmsnorm_bwd(dh, x, w, res, name, after=None):
    s, d = x.shape
    tr = _tile(s, 256, 8)

    def body(dh_ref, x_ref, w_ref, res_ref, *rest):
        dx_ref, dx16_ref, dw_ref = rest[-3:]
        xv = x_ref[...]
        rstd = lax.rsqrt(jnp.mean(xv * xv, axis=-1, keepdims=True) + EPS)
        xhat = xv * rstd
        dhv = dh_ref[...]
        gw = dhv * w_ref[...]
        dx = res_ref[...] + rstd * (gw - xhat * jnp.mean(gw * xhat, axis=-1, keepdims=True))
        dx_ref[...] = dx
        dx16_ref[...] = dx.astype(BF16)

        @pl.when(pl.program_id(0) == 0)
        def _():
            dw_ref[...] = jnp.zeros_like(dw_ref)

        dw_ref[...] += jnp.sum(dhv * xhat, axis=0, keepdims=True)

    row = pl.BlockSpec((tr, d), lambda i: (i, 0))
    vec = pl.BlockSpec((1, d), lambda i: (0, 0))
    extra = [] if after is None else [after]
    return pl.pallas_call(
        body, name=name, grid=(s // tr,),
        in_specs=[row, row, vec, row] + [ANY] * len(extra), out_specs=[row, row, vec],
        out_shape=[jax.ShapeDtypeStruct((s, d), F32), jax.ShapeDtypeStruct((s, d), BF16),
                   jax.ShapeDtypeStruct((1, d), F32)],
        compiler_params=_params(("arbitrary",)),
    )(dh, x, w, res, *extra)


def _conv_taps(x, w, rows):
    shifted = [x]
    for sft in (1, 2, 3):
        shifted.append(jnp.where(rows >= sft, pltpu.roll(x, sft, 0), 0.0))
    y = w[3:4, :] * shifted[0] + w[2:3, :] * shifted[1] + w[1:2, :] * shifted[2] + w[0:1, :] * shifted[3]
    return y, shifted


def _delta_pre_fwd(qkvz, conv_w, heads):
    s = qkvz.shape[0]
    nblk = 3 * heads

    def body(x_ref, w_ref, o_ref):
        part = pl.program_id(0) // heads
        rows = lax.broadcasted_iota(jnp.int32, (s, HEAD_DIM), 0)
        y, _ = _conv_taps(x_ref[...], w_ref[...], rows)
        a = y * _sigmoid(y)
        rs = lax.rsqrt(jnp.sum(a * a, axis=-1, keepdims=True) + EPS)
        fac = jnp.where(part == 0, rs * (HEAD_DIM ** -0.5), jnp.where(part == 1, rs, 1.0))
        o_ref[...] = a * fac

    return pl.pallas_call(
        body, name="delta_pre_fwd", grid=(nblk,),
        in_specs=[pl.BlockSpec((s, HEAD_DIM), lambda i: (0, i)), pl.BlockSpec((4, HEAD_DIM), lambda i: (0, i))],
        out_specs=pl.BlockSpec((s, HEAD_DIM), lambda i: (0, i)),
        out_shape=jax.ShapeDtypeStruct((s, 3 * heads * HEAD_DIM), F32),
        compiler_params=_params(("parallel",)),
    )(qkvz, conv_w)


def _delta_pre_bwd(dqkv, qkvz, conv_w, heads):
    s = qkvz.shape[0]
    nblk = 3 * heads

    def body(d_ref, x_ref, w_ref, dx_ref, dw_ref):
        part = pl.program_id(0) // heads
        rows = lax.broadcasted_iota(jnp.int32, (s, HEAD_DIM), 0)
        w = w_ref[...]
        y, shifted = _conv_taps(x_ref[...], w, rows)
        sg = _sigmoid(y)
        a = y * sg
        rs = lax.rsqrt(jnp.sum(a * a, axis=-1, keepdims=True) + EPS)
        unit = a * rs
        dn = d_ref[...]
        scale = jnp.where(part == 0, HEAD_DIM ** -0.5, 1.0)
        da_norm = scale * rs * (dn - unit * jnp.sum(dn * unit, axis=-1, keepdims=True))
        da = jnp.where(part < 2, da_norm, dn)
        dy = da * sg * (1.0 + y * (1.0 - sg))
        dx = w[3:4, :] * dy
        for sft in (1, 2, 3):
            dx = dx + w[3 - sft:4 - sft, :] * jnp.where(rows < s - sft, pltpu.roll(dy, s - sft, 0), 0.0)
        dx_ref[...] = dx.astype(BF16)
        for sft in range(4):
            dw_ref[3 - sft:4 - sft, :] = jnp.sum(dy * shifted[sft], axis=0, keepdims=True)

    col = pl.BlockSpec((s, HEAD_DIM), lambda i: (0, i))
    wsp = pl.BlockSpec((4, HEAD_DIM), lambda i: (0, i))
    return pl.pallas_call(
        body, name="delta_pre_bwd", grid=(nblk,),
        in_specs=[col, col, wsp], out_specs=[col, wsp],
        out_shape=[jax.ShapeDtypeStruct((s, 3 * heads * HEAD_DIM), BF16),
                   jax.ShapeDtypeStruct((4, 3 * heads * HEAD_DIM), F32)],
        compiler_params=_params(("parallel",)),
    )(dqkv, qkvz, conv_w)


def _heads_of(ref, heads):
    return jnp.stack([ref[:, h * HEAD_DIM:(h + 1) * HEAD_DIM] for h in range(heads)])


def _chunk_common(q, k, v, bd, a_log, dt_bias, heads, solved=None):
    c = CHUNK
    braw = jnp.stack([bd[:, h:h + 1] for h in range(heads)])
    draw = jnp.stack([bd[:, heads + h:heads + h + 1] for h in range(heads)])
    beta = _sigmoid(braw)
    xd = draw + dt_bias
    sp = jnp.maximum(xd, 0.0) + jnp.log1p(jnp.exp(-jnp.abs(xd)))
    g = -jnp.exp(a_log) * sp
    row = lax.broadcasted_iota(jnp.int32, (c, c), 0)
    col = lax.broadcasted_iota(jnp.int32, (c, c), 1)
    sq = (heads, c, c)
    g_b = jnp.broadcast_to(g, sq)
    g_row = _csum(jnp.where(row == col, g_b, 0.0))
    gam_col = _rsum(jnp.where(col <= row, jnp.broadcast_to(g_row, sq), 0.0))
    gam_row = _csum(jnp.where(row <= col, g_b, 0.0))
    causal = row >= col
    dm = jnp.where(causal, jnp.exp(jnp.where(causal, gam_col - gam_row, 0.0)), 0.0)
    kk = _dot(k, k, "nt")
    e = jnp.exp(gam_col)
    if solved is None:
        low = jnp.where(row > col, beta * kk * dm, 0.0)
        assert c in (INV_BLOCK, 2 * INV_BLOCK)
        same = (row // INV_BLOCK) == (col // INV_BLOCK)
        diag = jnp.where(same, low, 0.0)
        t = jnp.where(row == col, 1.0, 0.0) - diag
        pw = diag
        for _ in range((INV_BLOCK - 1).bit_length() - 1):
            pw = _dot(pw, pw)
            t = t + _dot(t, pw)
        if c > INV_BLOCK:
            t = t - _dot(_dot(t, low - diag), t)
        u = _dot(t, beta * v)
        w = _dot(t, (beta * e) * k)
    else:
        t, u, w = solved
    qk_raw = _dot(q, k, "nt")
    gl = _csum(g)
    el = jnp.exp(gl - gam_col)
    return dict(beta=beta, xd=xd, g=g, row=row, col=col, dm=dm, kk=kk, t=t, e=e, u=u, w=w,
                qk_raw=qk_raw, qk=qk_raw * dm, gl=gl, el=el, qd=e * q, kd=el * k, cd=jnp.exp(gl))


def _delta_chunk_fwd(qkv, bd, a_log, dt_bias, heads):
    s = qkv.shape[0]
    n = s // CHUNK
    dw = heads * HEAD_DIM
    blk = lambda part: pl.BlockSpec((CHUNK, dw), lambda i: (i, part))

    def body(q_ref, k_ref, v_ref, bd_ref, al_ref, dt_ref, o_ref, st_ref, t_ref, uw_ref, state):
        @pl.when(pl.program_id(0) == 0)
        def _():
            state[...] = jnp.zeros_like(state)

        cm = _chunk_common(_heads_of(q_ref, heads), _heads_of(k_ref, heads), _heads_of(v_ref, heads), bd_ref[...],
                           al_ref[...], dt_ref[...], heads)
        st = state[...]
        st_ref[0] = st
        t_ref[0] = cm["t"]
        uw_ref[0, 0] = cm["u"]
        uw_ref[0, 1] = cm["w"]
        vn = cm["u"] - _dot(cm["w"], st)
        o = _dot(cm["qd"], st) + _dot(cm["qk"], vn)
        for h in range(heads):
            o_ref[:, h * HEAD_DIM:(h + 1) * HEAD_DIM] = o[h]
        state[...] = cm["cd"] * st + _dot(cm["kd"], vn, "tn")

    smem = pl.BlockSpec((heads, 1, 1), lambda i: (0, 0, 0))
    return pl.pallas_call(
        body, name="delta_chunk_fwd", grid=(n,),
        in_specs=[blk(0), blk(1), blk(2), pl.BlockSpec((CHUNK, HEAD_DIM), lambda i: (i, 0)), smem, smem],
        out_specs=[pl.BlockSpec((CHUNK, dw), lambda i: (i, 0)),
                   pl.BlockSpec((1, heads, HEAD_DIM, HEAD_DIM), lambda i: (i, 0, 0, 0)),
                   pl.BlockSpec((1, heads, CHUNK, CHUNK), lambda i: (i, 0, 0, 0)),
                   pl.BlockSpec((1, 2, heads, CHUNK, HEAD_DIM), lambda i: (i, 0, 0, 0, 0))],
        out_shape=[jax.ShapeDtypeStruct((s, dw), F32),
                   jax.ShapeDtypeStruct((n, heads, HEAD_DIM, HEAD_DIM), F32),
                   jax.ShapeDtypeStruct((n, heads, CHUNK, CHUNK), F32),
                   jax.ShapeDtypeStruct((n, 2, heads, CHUNK, HEAD_DIM), F32)],
        scratch_shapes=[pltpu.VMEM((heads, HEAD_DIM, HEAD_DIM), F32)],
        compiler_params=_params(("arbitrary",)),
    )(qkv, qkv, qkv, bd, a_log, dt_bias)


def _delta_chunk_bwd(do, qkv, bd, saved, a_log, dt_bias, heads):
    s = qkv.shape[0]
    n = s // CHUNK
    dw = heads * HEAD_DIM
    c = CHUNK
    blk = lambda part: pl.BlockSpec((CHUNK, dw), lambda i: (n - 1 - i, part))

    def all_heads(q, k, v, dov, st, solved, dsn, bd, a_log, dt_bias):
        cm = _chunk_common(q, k, v, bd, a_log, dt_bias, heads, solved)
        beta, e, dm, row, col = cm["beta"], cm["e"], cm["dm"], cm["row"], cm["col"]
        sq = (heads, c, c)
        vn = cm["u"] - _dot(cm["w"], st)
        dvn = _dot(cm["kd"], dsn)
        dkd = _dot(vn, dsn, "nt")
        dcd = _csum(_rsum(st * dsn))
        ds = cm["cd"] * dsn
        dqd = _dot(dov, st, "nt")
        ds = ds + _dot(cm["qd"], dov, "tn")
        dqk = _dot(dov, vn, "nt")
        dvn = dvn + _dot(cm["qk"], dov, "tn")
        dw_ = -_dot(dvn, st, "nt")
        ds = ds - _dot(cm["w"], dvn, "tn")
        drhs_u = _dot(cm["t"], dvn, "tn")
        drhs_w = _dot(cm["t"], dw_, "tn")
        da = -(_dot(drhs_u, cm["u"], "nt") + _dot(drhs_w, cm["w"], "nt"))
        dl = jnp.where(row > col, da, 0.0)
        dbeta = _rsum(dl * cm["kk"] * dm)
        dkk = dl * beta * dm
        dd = dl * beta * cm["kk"]
        dv = beta * drhs_u
        ek = e * k
        dbeta = dbeta + _rsum(drhs_u * v) + _rsum(drhs_w * ek)
        dk = (beta * e) * drhs_w
        dgam = _rsum(drhs_w * (beta * ek))
        dqkm = dqk * dm
        dq = _dot(dqkm, k)
        dk = dk + _dot(dqkm, q, "tn")
        dd = dd + dqk * cm["qk_raw"]
        dk = dk + _dot(dkk, k) + _dot(dkk, k, "tn")
        dq = dq + e * dqd
        dgam = dgam + _rsum(dqd * cm["qd"])
        dk = dk + cm["el"] * dkd
        r = _rsum(dkd * cm["kd"])
        dgam = dgam - r
        dgl = _csum(r) + dcd * cm["cd"]
        mm = dd * dm
        colsum_c = _rsum(jnp.where(row == col, jnp.broadcast_to(_csum(mm), sq), 0.0))
        dgam = dgam + _rsum(mm) - colsum_c
        ridx = lax.broadcasted_iota(jnp.int32, (c, 1), 0)
        dgam = dgam + jnp.where(ridx == c - 1, dgl, 0.0)
        dgam_row = _csum(jnp.where(row == col, jnp.broadcast_to(dgam, sq), 0.0))
        dg = _rsum(jnp.where(col >= row, jnp.broadcast_to(dgam_row, sq), 0.0))
        d_xd = dg * (-jnp.exp(a_log)) * _sigmoid(cm["xd"])
        d_braw = dbeta * beta * (1.0 - beta)
        d_alog = dg * cm["g"]
        lane = lax.broadcasted_iota(jnp.int32, (c, HEAD_DIM), 1)
        dbd = jnp.zeros((c, HEAD_DIM), F32)
        for h in range(heads):
            dbd = (dbd + jnp.where(lane == h, d_braw[h], 0.0) + jnp.where(lane == h + heads, d_xd[h], 0.0)
                   + jnp.where(lane == h + 2 * heads, d_alog[h], 0.0))
        return dq, dk, dv, ds, dbd

    def body(do_ref, q_ref, k_ref, v_ref, bd_ref, st_ref, t_ref, uw_ref, al_ref, dt_ref, dqkv_ref, dbd_ref, dstate):
        @pl.when(pl.program_id(0) == 0)
        def _():
            dstate[...] = jnp.zeros_like(dstate)

        dq, dk, dv, ds, dbd = all_heads(_heads_of(q_ref, heads), _heads_of(k_ref, heads), _heads_of(v_ref, heads),
                                        _heads_of(do_ref, heads), st_ref[0], (t_ref[0], uw_ref[0, 0], uw_ref[0, 1]),
                                        dstate[...], bd_ref[...],
                                        al_ref[...], dt_ref[...])
        for part, val in enumerate((dq, dk, dv)):
            for h in range(heads):
                lo = part * dw + h * HEAD_DIM
                dqkv_ref[:, lo:lo + HEAD_DIM] = val[h]
        dstate[...] = ds
        dbd_ref[...] = dbd

    smem = pl.BlockSpec((heads, 1, 1), lambda i: (0, 0, 0))
    shared = pl.BlockSpec((CHUNK, HEAD_DIM), lambda i: (n - 1 - i, 0))
    wide = pl.BlockSpec((CHUNK, dw), lambda i: (n - 1 - i, 0))
    return pl.pallas_call(
        body, name="delta_chunk_bwd", grid=(n,),
        in_specs=[wide, blk(0), blk(1), blk(2), shared,
                  pl.BlockSpec((1, heads, HEAD_DIM, HEAD_DIM), lambda i: (n - 1 - i, 0, 0, 0)),
                  pl.BlockSpec((1, heads, CHUNK, CHUNK), lambda i: (n - 1 - i, 0, 0, 0)),
                  pl.BlockSpec((1, 2, heads, CHUNK, HEAD_DIM), lambda i: (n - 1 - i, 0, 0, 0, 0)), smem, smem],
        out_specs=[pl.BlockSpec((CHUNK, 3 * dw), lambda i: (n - 1 - i, 0)), shared],
        out_shape=[jax.ShapeDtypeStruct((s, 3 * dw), F32), jax.ShapeDtypeStruct((s, HEAD_DIM), F32)],
        scratch_shapes=[pltpu.VMEM((heads, HEAD_DIM, HEAD_DIM), F32)],
        compiler_params=_params(("arbitrary",)),
    )(do, qkv, qkv, qkv, bd, *saved, a_log, dt_bias)


def _delta_post_fwd(o, qkvz, w, heads):
    s = o.shape[0]
    tr = _tile(s, 4096, 8)

    def body(o_ref, z_ref, w_ref, out_ref):
        ov, z = o_ref[...], z_ref[...]
        rstd = lax.rsqrt(jnp.mean(ov * ov, axis=-1, keepdims=True) + EPS)
        out_ref[...] = (ov * rstd * w_ref[...] * (z * _sigmoid(z))).astype(BF16)

    return pl.pallas_call(
        body, name="delta_post_fwd", grid=(s // tr, heads),
        in_specs=[pl.BlockSpec((tr, HEAD_DIM), lambda i, h: (i, h)),
                  pl.BlockSpec((tr, HEAD_DIM), lambda i, h: (i, 3 * heads + h)),
                  pl.BlockSpec((1, HEAD_DIM), lambda i, h: (0, 0))],
        out_specs=pl.BlockSpec((tr, HEAD_DIM), lambda i, h: (i, h)),
        out_shape=jax.ShapeDtypeStruct((s, heads * HEAD_DIM), BF16),
        compiler_params=_params(("parallel", "parallel")),
    )(o, qkvz, w)


def _delta_post_bwd(dmix, o, qkvz, w, heads):
    s = o.shape[0]
    tr = _tile(s, 4096, 8)

    def body(d_ref, o_ref, z_ref, w_ref, do_ref, dz_ref, dw_ref):
        d, ov, z, wv = d_ref[...], o_ref[...], z_ref[...], w_ref[...]
        sg = _sigmoid(z)
        rstd = lax.rsqrt(jnp.mean(ov * ov, axis=-1, keepdims=True) + EPS)
        ohat = ov * rstd
        dz_ref[...] = (d * (ohat * wv) * sg * (1.0 + z * (1.0 - sg))).astype(BF16)
        dn = d * (z * sg)
        gw = dn * wv
        do_ref[...] = rstd * (gw - ohat * jnp.mean(gw * ohat, axis=-1, keepdims=True))

        @pl.when((pl.program_id(0) == 0) & (pl.program_id(1) == 0))
        def _():
            dw_ref[...] = jnp.zeros_like(dw_ref)

        dw_ref[...] += jnp.sum(dn * ohat, axis=0, keepdims=True)

    head = pl.BlockSpec((tr, HEAD_DIM), lambda i, h: (i, h))
    vec = pl.BlockSpec((1, HEAD_DIM), lambda i, h: (0, 0))
    dw = heads * HEAD_DIM
    return pl.pallas_call(
        body, name="delta_post_bwd", grid=(s // tr, heads),
        in_specs=[head, head, pl.BlockSpec((tr, HEAD_DIM), lambda i, h: (i, 3 * heads + h)), vec],
        out_specs=[head, head, vec],
        out_shape=[jax.ShapeDtypeStruct((s, dw), F32), jax.ShapeDtypeStruct((s, dw), BF16),
                   jax.ShapeDtypeStruct((1, HEAD_DIM), F32)],
        compiler_params=_params(("arbitrary", "arbitrary")),
    )(dmix, o, qkvz, w)


def _rope_tables(positions, s):
    half = HEAD_DIM // 2
    inv_freq = ROPE_THETA ** (-jnp.arange(half, dtype=F32) / half)
    ang = positions.reshape(s, 1).astype(F32) * inv_freq
    cos, sin = jnp.cos(ang), jnp.sin(ang)
    return jnp.concatenate([cos, cos], axis=-1), jnp.concatenate([-sin, sin], axis=-1)


def _attn_pre_fwd(aqkv, wq, wk, cosf, sinf, heads):
    s = aqkv.shape[0]
    tr = _tile(s, 4096, 8)

    def body(x_ref, wq_ref, wk_ref, c_ref, s_ref, o_ref):
        xv = x_ref[...]
        wv = jnp.where(pl.program_id(1) < heads, wq_ref[...], wk_ref[...])
        y = xv * lax.rsqrt(jnp.mean(xv * xv, axis=-1, keepdims=True) + EPS) * wv
        o_ref[...] = y * c_ref[...] + pltpu.roll(y, HEAD_DIM // 2, 1) * s_ref[...]

    blk = pl.BlockSpec((tr, HEAD_DIM), lambda i, j: (i, j))
    vec = pl.BlockSpec((1, HEAD_DIM), lambda i, j: (0, 0))
    tab = pl.BlockSpec((tr, HEAD_DIM), lambda i, j: (i, 0))
    return pl.pallas_call(
        body, name="attn_pre_fwd", grid=(s // tr, 2 * heads),
        in_specs=[blk, vec, vec, tab, tab], out_specs=blk,
        out_shape=jax.ShapeDtypeStruct((s, 2 * heads * HEAD_DIM), F32),
        compiler_params=_params(("parallel", "parallel")),
    )(aqkv, wq, wk, cosf, sinf)


def _band():
    qi = lax.broadcasted_iota(jnp.int32, (SPAN, 2 * SPAN), 0)
    ki = lax.broadcasted_iota(jnp.int32, (SPAN, 2 * SPAN), 1)
    dist = qi + SPAN - ki
    return (dist >= 0) & (dist <= SPAN), ki >= SPAN


def _sub(g, r, d):
    if d == 1:
        return pl.ds(g * SPAN, SPAN)
    return pl.ds(g * SPAN * d + r, SPAN, stride=d)


def _attn_blocks(s):
    assert s % (SPAN * max(DILATIONS)) == 0
    return [(p_i, d, r, g) for p_i, d in enumerate(DILATIONS) for r in range(d) for g in range(s // (SPAN * d))]


def _attn_fwd(qk, aqkv, w, heads):
    s = qk.shape[0]
    aw = heads * HEAD_DIM

    def body(q_ref, k_ref, v_ref, w_ref, mix_ref, acc_ref, m_ref, l_ref):
        band, own = _band()
        for p_i, d, r, g in _attn_blocks(s):
            rows = _sub(g, r, d)
            kc, vc = k_ref[rows, :], v_ref[rows, :]
            if g == 0:
                kp, vp, mask = kc, vc, band & own
            else:
                mask = band
            kcat = jnp.concatenate([kp, kc], axis=0)
            vcat = jnp.concatenate([vp, vc], axis=0)
            kp, vp = kc, vc
            sc = _dot(q_ref[rows, :] * (HEAD_DIM ** -0.5), kcat, "nt")
            sc = jnp.where(mask, sc, NEG)
            m = jnp.max(sc, axis=-1, keepdims=True)
            if p_i == 0:
                p = jnp.exp(sc - m)
                acc_ref[rows, :] = _dot(p, vcat)
                l_new = jnp.sum(p, axis=-1, keepdims=True)
            else:
                m_old = m_ref[rows, 0:1]
                m = jnp.maximum(m, m_old)
                alpha = jnp.exp(m_old - m)
                p = jnp.exp(sc - m)
                acc_ref[rows, :] = alpha * acc_ref[rows, :] + _dot(p, vcat)
                l_new = alpha * l_ref[rows, 0:1] + jnp.sum(p, axis=-1, keepdims=True)
            m_ref[rows, :] = jnp.broadcast_to(m, (SPAN, HEAD_DIM))
            l_ref[rows, :] = jnp.broadcast_to(l_new, (SPAN, HEAD_DIM))
        den = l_ref[...]
        ob = acc_ref[...] / den
        acc_ref[...] = ob
        m_ref[...] = m_ref[...] + jnp.log(den)
        rstd = lax.rsqrt(jnp.mean(ob * ob, axis=-1, keepdims=True) + EPS)
        mix_ref[...] = (ob * rstd * w_ref[...]).astype(BF16)

    col = lambda off: pl.BlockSpec((s, HEAD_DIM), lambda h: (0, off + h))
    return pl.pallas_call(
        body, name="attn_fwd", grid=(heads,),
        in_specs=[col(0), col(heads), col(2 * heads), pl.BlockSpec((1, HEAD_DIM), lambda h: (0, 0))],
        out_specs=[col(0), col(0), col(0)],
        out_shape=[jax.ShapeDtypeStruct((s, aw), BF16), jax.ShapeDtypeStruct((s, aw), F32),
                   jax.ShapeDtypeStruct((s, aw), F32)],
        scratch_shapes=[pltpu.VMEM((s, HEAD_DIM), F32)],
        compiler_params=_params(("parallel",)),
    )(qk, qk, aqkv, w)


def _attn_merge_bwd(dmix, ob, w, heads):
    s = ob.shape[0]
    tr = _tile(s, 4096, 8)

    def body(d_ref, ob_ref, w_ref, do_ref, dsum_ref, dw_ref):
        d, ov = d_ref[...], ob_ref[...]
        rstd = lax.rsqrt(jnp.mean(ov * ov, axis=-1, keepdims=True) + EPS)
        ohat = ov * rstd
        gw = d * w_ref[...]
        dov = rstd * (gw - ohat * jnp.mean(gw * ohat, axis=-1, keepdims=True))
        do_ref[...] = dov
        dsum_ref[...] = jnp.broadcast_to(jnp.sum(dov * ov, axis=-1, keepdims=True), dov.shape)

        @pl.when((pl.program_id(0) == 0) & (pl.program_id(1) == 0))
        def _():
            dw_ref[...] = jnp.zeros_like(dw_ref)

        dw_ref[...] += jnp.sum(d * ohat, axis=0, keepdims=True)

    blk = pl.BlockSpec((tr, HEAD_DIM), lambda i, h: (i, h))
    vec = pl.BlockSpec((1, HEAD_DIM), lambda i, h: (0, 0))
    aw = heads * HEAD_DIM
    return pl.pallas_call(
        body, name="attn_merge_bwd", grid=(s // tr, heads),
        in_specs=[pl.BlockSpec((tr, HEAD_DIM), lambda i, h: (i, heads + h)), blk, vec],
        out_specs=[blk, blk, vec],
        out_shape=[jax.ShapeDtypeStruct((s, aw), F32), jax.ShapeDtypeStruct((s, aw), F32),
                   jax.ShapeDtypeStruct((1, HEAD_DIM), F32)],
        compiler_params=_params(("arbitrary", "arbitrary")),
    )(dmix, ob, w)


def _attn_bwd(qk, aqkv, do, lse, dsum, heads):
    s = qk.shape[0]
    aw = heads * HEAD_DIM
    scale = HEAD_DIM ** -0.5

    def body(q_ref, k_ref, v_ref, do_ref, l_ref, ds_ref, out_ref):
        band, own = _band()
        dq_ref, dk_ref, dv_ref = out_ref.at[0], out_ref.at[1], out_ref.at[2]
        out_ref[...] = jnp.zeros((3, s, HEAD_DIM), F32)
        for _, d, r, g in _attn_blocks(s):
            rows = _sub(g, r, d)
            qs, dov = q_ref[rows, :] * scale, do_ref[rows, :]
            kc, vc = k_ref[rows, :], v_ref[rows, :]
            if g == 0:
                kp, vp, mask = kc, vc, band & own
            else:
                mask = band
            kcat = jnp.concatenate([kp, kc], axis=0)
            vcat = jnp.concatenate([vp, vc], axis=0)
            p = jnp.where(mask, jnp.exp(_dot(qs, kcat, "nt") - l_ref[rows, 0:1]), 0.0)
            dsc = p * (_dot(dov, vcat, "nt") - ds_ref[rows, 0:1])
            dq_ref[rows, :] += scale * _dot(dsc, kcat)
            dk = _dot(dsc, qs, "tn")
            dv = _dot(p, dov, "tn")
            if g > 0:
                dk_ref[prows, :] += dk_own + dk[:SPAN]
                dv_ref[prows, :] += dv_own + dv[:SPAN]
            dk_own, dv_own = dk[SPAN:], dv[SPAN:]
            if g == s // (SPAN * d) - 1:
                dk_ref[rows, :] += dk_own
                dv_ref[rows, :] += dv_own
            kp, vp, prows = kc, vc, rows

    col = lambda off: pl.BlockSpec((s, HEAD_DIM), lambda h: (0, off + h))
    return pl.pallas_call(
        body, name="attn_bwd", grid=(heads,),
        in_specs=[col(0), col(heads), col(2 * heads), col(0), col(0), col(0)],
        out_specs=pl.BlockSpec((3, s, HEAD_DIM), lambda h: (0, 0, h)),
        out_shape=jax.ShapeDtypeStruct((3, s, aw), F32),
        compiler_params=_params(("parallel",)),
    )(qk, qk, aqkv, do, lse, dsum)


def _attn_pre_bwd(grads, aqkv, wq, wk, cosf, sinf, heads):
    s = aqkv.shape[0]
    tr = _tile(s, 4096, 8)
    nrow = s // tr

    def body(g_ref, x_ref, wq_ref, wk_ref, c_ref, s_ref, dx_ref, dwq_ref, dwk_ref):
        i, j = pl.program_id(0), pl.program_id(1)
        kind = j // heads
        dout = g_ref[0]
        tot_v = dout
        dy = dout * c_ref[...] + pltpu.roll(dout * s_ref[...], HEAD_DIM // 2, 1)
        xv = x_ref[...]
        wv = jnp.where(kind == 0, wq_ref[...], wk_ref[...])
        rstd = lax.rsqrt(jnp.mean(xv * xv, axis=-1, keepdims=True) + EPS)
        xhat = xv * rstd
        gw = dy * wv
        dxn = rstd * (gw - xhat * jnp.mean(gw * xhat, axis=-1, keepdims=True))
        dx_ref[...] = jnp.where(kind == 2, tot_v, dxn).astype(BF16)
        dwc = jnp.sum(dy * xhat, axis=0, keepdims=True)

        @pl.when((i == 0) & (j == 0))
        def _():
            dwq_ref[...] = jnp.zeros_like(dwq_ref)
            dwk_ref[...] = jnp.zeros_like(dwk_ref)

        @pl.when(kind == 0)
        def _():
            dwq_ref[...] += dwc

        @pl.when(kind == 1)
        def _():
            dwk_ref[...] += dwc

    grad = pl.BlockSpec((1, tr, HEAD_DIM), lambda i, j: (j // heads, i, j % heads))
    blk = pl.BlockSpec((tr, HEAD_DIM), lambda i, j: (i, j))
    vec = pl.BlockSpec((1, HEAD_DIM), lambda i, j: (0, 0))
    tab = pl.BlockSpec((tr, HEAD_DIM), lambda i, j: (i, 0))
    return pl.pallas_call(
        body, name="attn_pre_bwd", grid=(nrow, 3 * heads),
        in_specs=[grad, blk, vec, vec, tab, tab], out_specs=[blk, vec, vec],
        out_shape=[jax.ShapeDtypeStruct((s, 3 * heads * HEAD_DIM), BF16),
                   jax.ShapeDtypeStruct((1, HEAD_DIM), F32), jax.ShapeDtypeStruct((1, HEAD_DIM), F32)],
        compiler_params=_params(("arbitrary", "arbitrary")),
    )(grads, aqkv, wq, wk, cosf, sinf)


def _colsum(a, name):
    s, d = a.shape
    tr = _tile(s, 4096, 8)

    def body(a_ref, o_ref):
        @pl.when(pl.program_id(0) == 0)
        def _():
            o_ref[...] = jnp.zeros_like(o_ref)

        o_ref[...] += jnp.sum(a_ref[...], axis=0, keepdims=True)

    return pl.pallas_call(
        body, name=name, grid=(s // tr,),
        in_specs=[pl.BlockSpec((tr, d), lambda i: (i, 0))], out_specs=pl.BlockSpec((1, d), lambda i: (0, 0)),
        out_shape=jax.ShapeDtypeStruct((1, d), F32),
        compiler_params=_params(("arbitrary",)),
    )(a)


def _local_step(x, positions, target, small, w_qkvz, w_bd, w_attn, conv_w, later_weights, ffn_grads_ready,
                rest_grads_ready, after=None):
    s, dmod = x.shape
    heads = dmod // (2 * HEAD_DIM)
    dw = heads * HEAD_DIM
    a_log, dt_bias = small["a_log"].reshape(heads, 1, 1), small["dt_bias"].reshape(heads, 1, 1)
    cosf, sinf = _rope_tables(positions, s)

    h1 = _rmsnorm_fwd(x, small["attn_norm_w"], "norm1_fwd", after=after)
    qkvz = _matmul(h1, w_qkvz, "nn", "proj_qkvz")
    bd = _matmul(h1, w_bd, "nn", "proj_bd")
    aqkv = _matmul(h1, w_attn, "nn", "proj_attn")
    dqkv = _delta_pre_fwd(qkvz, conv_w, heads)
    o_d, *saved = _delta_chunk_fwd(dqkv, bd, a_log, dt_bias, heads)
    mix_a = _delta_post_fwd(o_d, qkvz, small["delta_out_norm_w"], heads)
    qk_rot = _attn_pre_fwd(aqkv, small["q_norm_w"], small["k_norm_w"], cosf, sinf, heads)
    mix_b, ob, lse = _attn_fwd(qk_rot, aqkv, small["attn_out_norm_w"], heads)
    w_out, behind, ffn_weights = later_weights((mix_a, mix_b))
    x1 = _matmul(mix_a, w_out, "nn", "out_proj_a", add=x, b_rows=(0, dw), after=behind)
    x1 = _matmul(mix_b, w_out, "nn", "out_proj_b", add=x1, b_rows=(dw, dw))
    h2 = _rmsnorm_fwd(x1, small["ffn_norm_w"], "norm2_fwd")
    w_gu, w_down = ffn_weights(h2)
    ff = w_down.shape[0]

    def swiglu(accs, vals):
        g, u = accs
        return g, u, g * _sigmoid(g) * u

    gate, up, act = _matmul(h2, w_gu, "nn", "ffn_gate_up", b_cols=(0, ff), b2_cols=(ff, ff), finish=swiglu,
                            out_dtypes=[BF16, BF16, BF16])

    def loss_head(acc, vals):
        err = acc + vals[0] - vals[1]
        part = 0.5 * jnp.sum(jnp.sum(err * err, axis=-1, keepdims=True) * (1.0 / dmod), axis=0, keepdims=True)
        lane = lax.broadcasted_iota(jnp.int32, (1, HEAD_DIM), 1)
        return err * (1.0 / dmod), err * (1.0 / dmod), jnp.where(lane == 0, part, 0.0)

    dy, dy16, loss_row = _matmul(act, w_down, "nn", "ffn_down", tiles=[(x1, 0), (target, 0)], finish=loss_head,
                                 out_dtypes=[F32, BF16], row_sum=True)

    def swiglu_bwd(acc, vals):
        g, u = vals[0].astype(F32), vals[1].astype(F32)
        sg = _sigmoid(g)
        return acc * u * sg * (1.0 + g * (1.0 - sg)), acc * g * sg

    dgate, dup = _matmul(dy16, w_down, "nt", "ffn_down_dx", tiles=[(gate, 0), (up, 0)], finish=swiglu_bwd,
                         out_dtypes=[BF16, BF16])
    g_w_down = _matmul(act, dy16, "tn", "ffn_down_dw")
    g_w_gate = _matmul(h2, dgate, "tn", "ffn_gate_dw")
    g_w_up = _matmul(h2, dup, "tn", "ffn_up_dw")
    behind, and_then = ffn_grads_ready(g_w_gate, g_w_up, g_w_down)
    dh2 = _matmul(dgate, w_gu, "nt", "ffn_gate_dx", b_cols=(0, ff), after=behind)
    dh2 = _matmul(dup, w_gu, "nt", "ffn_up_dx", b_cols=(ff, ff), add=dh2)
    behind = and_then(dh2)
    dx1, dx1_16, g_ffn_norm = _rmsnorm_bwd(dh2, x1, small["ffn_norm_w"], dy, "norm2_bwd", after=behind)
    dmix = _matmul(dx1_16, w_out, "nt", "out_proj_dx")
    g_w_out_a = _matmul(mix_a, dx1_16, "tn", "out_proj_dw_a")
    g_w_out_b = _matmul(mix_b, dx1_16, "tn", "out_proj_dw_b")
    dob, dsum, g_attn_out_norm = _attn_merge_bwd(dmix, ob, small["attn_out_norm_w"], heads)
    grads = _attn_bwd(qk_rot, aqkv, dob, lse, dsum, heads)
    d_aqkv, g_q_norm, g_k_norm = _attn_pre_bwd(grads, aqkv, small["q_norm_w"], small["k_norm_w"], cosf, sinf, heads)
    do_d, dz, g_delta_out_norm = _delta_post_bwd(dmix, o_d, qkvz, small["delta_out_norm_w"], heads)
    ddqkv, dbd = _delta_chunk_bwd(do_d, dqkv, bd, saved, a_log, dt_bias, heads)
    d_qkv_raw, g_conv = _delta_pre_bwd(ddqkv, qkvz, conv_w, heads)
    bd_sums = _colsum(dbd, "bd_colsum")
    g_w_qkv = _matmul(h1, d_qkv_raw, "tn", "proj_qkv_dw")
    g_w_z = _matmul(h1, dz, "tn", "proj_z_dw")
    g_w_bd = _matmul(h1, dbd, "tn", "proj_bd_dw")
    g_w_attn = _matmul(h1, d_aqkv, "tn", "proj_attn_dw")
    behind, and_then = rest_grads_ready(dict(w_qkv=g_w_qkv, w_z=g_w_z, w_bd=g_w_bd, w_attn=g_w_attn,
                                             w_out_a=g_w_out_a, w_out_b=g_w_out_b))
    dh1 = _matmul(d_qkv_raw, w_qkvz, "nt", "proj_qkv_dx", b_cols=(0, 3 * dw), after=behind)
    dh1 = _matmul(dz, w_qkvz, "nt", "proj_z_dx", b_cols=(3 * dw, dw), add=dh1, after=and_then(dh1))
    dh1 = _matmul(d_aqkv, w_attn, "nt", "proj_attn_dx", add=dh1)
    dh1 = _matmul(dbd, w_bd, "nt", "proj_bd_dx", add=dh1)
    grad_x, _, g_attn_norm = _rmsnorm_bwd(dh1, x, small["attn_norm_w"], dx1, "norm1_bwd")
    small_grads = dict(
        attn_norm_w=g_attn_norm, a_log=bd_sums[:, 2 * heads:3 * heads], dt_bias=bd_sums[:, heads:2 * heads],
        delta_out_norm_w=g_delta_out_norm, q_norm_w=g_q_norm, k_norm_w=g_k_norm,
        attn_out_norm_w=g_attn_out_norm, ffn_norm_w=g_ffn_norm, conv_w=g_conv)
    return loss_row, grad_x, small_grads


def _adamw(w, g, m, v, name):
    r, c = w.shape
    tr = _tile(r, 256, 8)

    def body(w_ref, g_ref, m_ref, v_ref, d_ref, nm_ref, nv_ref):
        gv = g_ref[...]
        nm = ADAM_B1 * m_ref[...] + (1.0 - ADAM_B1) * gv
        nv = ADAM_B2 * v_ref[...] + (1.0 - ADAM_B2) * (gv * gv)
        m_hat = nm / (1.0 - ADAM_B1 ** ADAM_STEP)
        v_hat = nv / (1.0 - ADAM_B2 ** ADAM_STEP)
        d_ref[...] = -ADAM_LR * (m_hat / (jnp.sqrt(v_hat) + ADAM_EPS) + ADAM_WD * w_ref[...])
        nm_ref[...] = nm
        nv_ref[...] = nv

    blk = pl.BlockSpec((tr, c), lambda i: (i, 0))
    return pl.pallas_call(
        body, name=name, grid=(r // tr,),
        in_specs=[blk] * 4, out_specs=[blk] * 3,
        out_shape=[jax.ShapeDtypeStruct((r, c), F32)] * 3,
        compiler_params=_params(("parallel",)),
    )(w, g, m, v)


def _add_half_bf16(g, b, place, name):
    n, half, c = b.shape
    tr = _tile(half, 512, 16)
    nb = half // tr

    def body(place_ref, g_ref, b_ref, o_ref):
        o_ref[...] = (g_ref[...].astype(F32) + b_ref[...].astype(F32)).astype(BF16)

    blk = pl.BlockSpec((1, tr, c), lambda i, j, p: (i, j, 0))
    return pl.pallas_call(
        body, name=name,
        grid_spec=pltpu.PrefetchScalarGridSpec(
            num_scalar_prefetch=1, grid=(n, nb),
            in_specs=[pl.BlockSpec((1, tr, c), lambda i, j, p: (i, p[0] * nb + j, 0)), blk], out_specs=blk),
        out_shape=jax.ShapeDtypeStruct((n, half, c), BF16),
        compiler_params=_params(("parallel", "parallel")),
    )(place, g, b)


def _sum4_f32(mine, others, place, name):
    _, half, c = mine.shape
    tr = _tile(half, 512, 16)
    nb = half // tr

    def body(place_ref, a_ref, b_ref, o_ref):
        acc = a_ref[0].astype(F32)
        for j in range(3):
            acc = acc + b_ref[j].astype(F32)
        o_ref[...] = acc

    return pl.pallas_call(
        body, name=name,
        grid_spec=pltpu.PrefetchScalarGridSpec(
            num_scalar_prefetch=1, grid=(nb,),
            in_specs=[pl.BlockSpec((1, tr, c), lambda i, p: (p[1], i, 0)),
                      pl.BlockSpec((3, tr, c), lambda i, p: (0, i, 0))],
            out_specs=pl.BlockSpec((tr, c), lambda i, p: (p[0] * nb + i, 0))),
        out_shape=jax.ShapeDtypeStruct((2 * half, c), F32),
        compiler_params=_params(("parallel",)),
    )(place, mine, others)


def _place():
    x, y, c = lax.axis_index("x"), lax.axis_index("y"), lax.axis_index("c")
    other_chips = [(1 - x, y), (x, 1 - y), (1 - x, 1 - y)]
    return x, y, c, (x, y, 1 - c), other_chips


ANY = pl.BlockSpec(memory_space=pl.ANY)


def _remote(k, src, dst, to, send_sems, recv_sems):
    return pltpu.make_async_remote_copy(src_ref=src, dst_ref=dst, send_sem=send_sems.at[k], recv_sem=recv_sems.at[k],
                                        device_id=to, device_id_type=MESH)


def _half(ref, lead, hc):
    if lead is None:
        half = ref.shape[0] // 2
        return ref.at[pl.ds(hc * half, half), :]
    half = ref.shape[1] // 2
    return ref.at[lead, pl.ds(hc * half, half), :]


def _all_gather_weights(slots, whole, name):
    nt, nw = len(slots), len(whole)
    base_w, base_f, base_d = 2 * nt, 2 * nt + 3 * nw, 4 * nt + 3 * nw

    def quarter(ref, lead, hc, q):
        quart = ref.shape[1] // 4
        return ref.at[lead, pl.ds((2 * hc + q) * quart, quart), :]

    def body(*refs):
        ins, outs = refs[:nt + nw], refs[nt + nw:2 * (nt + nw)]
        sems = refs[2 * (nt + nw):]
        x, y, c, sibling, chips = _place()
        me, xn, yn, dg = 2 * x + y, 2 * (1 - x) + y, 2 * x + 1 - y, 2 * (1 - x) + 1 - y
        to_x, to_y = (1 - x, y, c), (x, 1 - y, c)
        cps = []
        for t in range(nt):
            cps.append(_remote(2 * t, _half(ins[t], me, c), _half(outs[t], me, c), to_x, *sems))
            cps.append(_remote(2 * t + 1, _half(ins[t], me, c), _half(outs[t], me, c), to_y, *sems))
        for j, (px, py) in enumerate(chips):
            for t in range(nw):
                cps.append(_remote(base_w + j * nw + t, ins[nt + t].at[me], outs[nt + t].at[me], (px, py, c), *sems))
        for cp in cps:
            cp.start()

        def start(k, ref, to):
            cp = _remote(k, ref, ref, to, *sems)
            cp.start()
            cps.append(cp)

        for t in range(nt):
            landed = _half(outs[t], xn, c)
            _remote(2 * t, landed, landed, to_x, *sems).wait_recv()
            start(base_f + 2 * t, quarter(outs[t], xn, c, 0), to_y)
            start(base_d + 3 * t, landed, sibling)
            landed = _half(outs[t], yn, c)
            _remote(2 * t + 1, landed, landed, to_y, *sems).wait_recv()
            start(base_f + 2 * t + 1, quarter(outs[t], yn, c, 1), to_x)
            start(base_d + 3 * t + 1, landed, sibling)
        for t in range(nt):
            q0, q1 = quarter(outs[t], dg, c, 0), quarter(outs[t], dg, c, 1)
            _remote(base_f + 2 * t, q0, q0, to_y, *sems).wait_recv()
            _remote(base_f + 2 * t + 1, q1, q1, to_x, *sems).wait_recv()
            start(base_d + 3 * t + 2, _half(outs[t], dg, c), sibling)
        for j, (px, py) in enumerate(chips):
            for t in range(nw):
                landed = outs[nt + t].at[2 * px + py]
                _remote(base_w + j * nw + t, landed, landed, (px, py, c), *sems).wait_recv()
        for t in range(nt):
            for j, chip in enumerate((xn, yn, dg)):
                other = _half(outs[t], chip, 1 - c)
                _remote(base_d + 3 * t + j, other, other, sibling, *sems).wait_recv()
        for cp in cps:
            cp.wait_send()

    arrays = list(slots) + list(whole)
    n_sem = 7 * nt + 3 * nw
    return pl.pallas_call(
        body, name=name, in_specs=[ANY] * len(arrays), out_specs=[ANY] * len(arrays),
        input_output_aliases={i: i for i in range(len(arrays))},
        out_shape=[jax.ShapeDtypeStruct(a.shape, a.dtype) for a in arrays],
        scratch_shapes=[pltpu.SemaphoreType.DMA((n_sem,)), pltpu.SemaphoreType.DMA((n_sem,))],
    )(*arrays)


def _join_halves(fs, name):
    nt = len(fs)

    def body(*refs):
        in_refs, out_refs, (send_sems, recv_sems) = refs[:nt], refs[nt:2 * nt], refs[2 * nt:]
        _, _, c, sibling, _ = _place()
        cps = [_remote(t, _half(in_refs[t], None, c), _half(out_refs[t], None, c), sibling, send_sems, recv_sems)
               for t in range(nt)]
        for cp in cps:
            cp.start()
        for t in range(nt):
            theirs = _half(out_refs[t], None, 1 - c)
            _remote(t, theirs, theirs, sibling, send_sems, recv_sems).wait_recv()
        for cp in cps:
            cp.wait_send()

    return pl.pallas_call(
        body, name=name, in_specs=[ANY] * nt, out_specs=[ANY] * nt,
        input_output_aliases={i: i for i in range(nt)},
        out_shape=[jax.ShapeDtypeStruct(f.shape, f.dtype) for f in fs],
        scratch_shapes=[pltpu.SemaphoreType.DMA((nt,)), pltpu.SemaphoreType.DMA((nt,))],
    )(*fs)


HBM = pl.BlockSpec(memory_space=pltpu.HBM)
SEM = pl.BlockSpec(memory_space=pltpu.SEMAPHORE)
EFFECT = pltpu.SideEffectType.DATAFLOW_SIDE_EFFECTING


def _split_start(arrays, after, plan, n_copies, name):
    na = len(arrays)

    def body(*refs):
        ins, send_sems, recv_sems = refs[:na], refs[na + 1], refs[na + 2]
        outs, token = refs[na + 3:2 * na + 3], refs[2 * na + 3]
        for k, (src, dst, to) in enumerate(plan(ins, outs)):
            _remote(k, src, dst, to, send_sems, recv_sems).start()
        token[...] = jnp.zeros_like(token)

    res = pl.pallas_call(
        body, name=name,
        out_shape=(pltpu.SemaphoreType.DMA((n_copies,)), pltpu.SemaphoreType.DMA((n_copies,)),
                   *[pltpu.HBM(a.shape, a.dtype) for a in arrays], jax.ShapeDtypeStruct((8, HEAD_DIM), F32)),
        in_specs=[HBM] * na + [ANY],
        out_specs=(SEM, SEM, *[HBM] * na, pl.BlockSpec(memory_space=pltpu.VMEM)),
        input_output_aliases={i: 2 + i for i in range(na)},
        compiler_params=pltpu.CompilerParams(has_side_effects=EFFECT),
    )(*[pltpu.with_memory_space_constraint(a, pltpu.HBM) for a in arrays], after)
    return res[0], res[1], list(res[2:2 + na]), res[2 + na]


def _split_wait(send_sems, recv_sems, arrays, after, plan, name):
    na = len(arrays)
    after = list(after) if isinstance(after, (list, tuple)) else [after]

    def body(*refs):
        ins, send, recv, outs = refs[:na], refs[na], refs[na + 1], refs[na + 2 + len(after):]
        for k, (src, dst, to) in enumerate(plan(ins, outs)):
            cp = _remote(k, src, dst, to, send, recv)
            cp.wait_send()
            cp.wait_recv()

    res = pl.pallas_call(
        body, name=name, out_shape=tuple(pltpu.HBM(a.shape, a.dtype) for a in arrays),
        in_specs=[HBM] * na + [SEM, SEM] + [ANY] * len(after), out_specs=tuple([HBM] * na),
        input_output_aliases={i: i for i in range(na)},
        compiler_params=pltpu.CompilerParams(has_side_effects=EFFECT),
    )(*arrays, send_sems, recv_sems, *after)
    return list(res)


def _gather_plan(nt):
    def plan(ins, outs):
        x, y, c, _, chips = _place()
        me = 2 * x + y
        return [(_half(ins[t], me, c), _half(outs[t], me, c), (px, py, c)) for px, py in chips for t in range(nt)]
    return plan


def _gather_landed_plan(nt):
    def plan(ins, outs):
        _, _, c, _, chips = _place()
        return [(_half(outs[t], 2 * px + py, c), _half(outs[t], 2 * px + py, c), (px, py, c))
                for px, py in chips for t in range(nt)]
    return plan


def _scatter_plan(nt):
    def plan(ins, outs):
        _, _, c, _, chips = _place()
        return [(ins[t].at[2 * px + py], outs[nt + t].at[j], (px, py, c))
                for j, (px, py) in enumerate(chips) for t in range(nt)]
    return plan


def _pass_plan(nt):
    def plan(ins, outs):
        _, _, c, sibling, chips = _place()
        return [(_half(ins[t], 2 * px + py, c), _half(outs[t], 2 * px + py, c), sibling)
                for px, py in chips for t in range(nt)]
    return plan


def _pass_landed_plan(nt):
    def plan(ins, outs):
        _, _, c, sibling, chips = _place()
        return [(_half(outs[t], 2 * px + py, c), _half(outs[t], 2 * px + py, 1 - c), sibling)
                for px, py in chips for t in range(nt)]
    return plan


def _swap_plan(nt):
    def plan(ins, outs):
        _, _, c, sibling, _ = _place()
        res = []
        for t in range(nt):
            half = ins[t].shape[1] // 2
            res.append((ins[t].at[:, pl.ds((1 - c) * half, half), :], outs[nt + t], sibling))
        return res
    return plan


def _pass_halves_to_sibling(slots, name):
    nt = len(slots)

    def body(*refs):
        ins, outs, (send_sems, recv_sems) = refs[:nt], refs[nt:2 * nt], refs[2 * nt:]
        _, _, c, sibling, chips = _place()
        cps = [_remote(j * nt + t, _half(ins[t], 2 * px + py, c), _half(outs[t], 2 * px + py, c), sibling,
                       send_sems, recv_sems)
               for j, (px, py) in enumerate(chips) for t in range(nt)]
        for cp in cps:
            cp.start()
        for j, (px, py) in enumerate(chips):
            for t in range(nt):
                other = _half(outs[t], 2 * px + py, 1 - c)
                _remote(j * nt + t, other, other, sibling, send_sems, recv_sems).wait_recv()
        for cp in cps:
            cp.wait_send()

    return pl.pallas_call(
        body, name=name, in_specs=[ANY] * nt, out_specs=[ANY] * nt,
        input_output_aliases={i: i for i in range(nt)},
        out_shape=[jax.ShapeDtypeStruct(a.shape, a.dtype) for a in slots],
        scratch_shapes=[pltpu.SemaphoreType.DMA((3 * nt,)), pltpu.SemaphoreType.DMA((3 * nt,))],
    )(*slots)


def _all_reduce_small(v):
    r, lanes = v.shape

    def body(v_ref, out_ref, buf, send_sems, recv_sems):
        x, y, c, sibling, chips = _place()

        def slot(px, py, pc):
            return buf.at[4 * px + 2 * py + pc]

        def copy(k, block, to, src=None):
            return pltpu.make_async_remote_copy(src_ref=slot(*block) if src is None else src, dst_ref=slot(*block),
                                                send_sem=send_sems.at[k], recv_sem=recv_sems.at[k],
                                                device_id=to, device_id_type=MESH)

        me = (x, y, c)
        buf[4 * x + 2 * y + c] = v_ref[...]
        first = [copy(0, me, sibling, src=v_ref)]
        first += [copy(1 + j, me, (*chip, c), src=v_ref) for j, chip in enumerate(chips)]
        for cp in first:
            cp.start()
        passed = [copy(4 + j, (*chip, c), sibling) for j, chip in enumerate(chips)]
        for j, chip in enumerate(chips):
            copy(1 + j, (*chip, c), me).wait_recv()
            passed[j].start()
        copy(0, (x, y, 1 - c), me).wait_recv()
        for j, chip in enumerate(chips):
            copy(4 + j, (*chip, 1 - c), me).wait_recv()
        for cp in first + passed:
            cp.wait_send()
        acc = buf[0]
        for k in range(1, 8):
            acc = acc + buf[k]
        out_ref[...] = acc

    vmem = pl.BlockSpec(memory_space=pltpu.VMEM)
    return pl.pallas_call(
        body, name="all_reduce_small", in_specs=[vmem], out_specs=vmem,
        out_shape=jax.ShapeDtypeStruct((r, lanes), F32),
        scratch_shapes=[pltpu.VMEM((8, r, lanes), F32), pltpu.SemaphoreType.DMA((7,)), pltpu.SemaphoreType.DMA((7,))],
    )(v)


def _size(shape):
    n = 1
    for d in shape:
        n *= d
    return n


def _regroup_cols(parts, widths):
    out, start = [], 0
    for width in widths:
        pieces, lo = [], 0
        for p in parts:
            a, b = max(start, lo), min(start + width, lo + p.shape[1])
            if a < b:
                pieces.append(p[:, a - lo:b - lo])
            lo += p.shape[1]
        out.append(pieces[0] if len(pieces) == 1 else jnp.concatenate(pieces, axis=1))
        start += width
    return out


def _pack_small(parts):
    rows = []
    for p in parts:
        f = p.reshape(-1).astype(F32)
        n = -(-f.shape[0] // HEAD_DIM) * HEAD_DIM
        rows.append(jnp.pad(f, (0, n - f.shape[0])).reshape(-1, HEAD_DIM))
    a = jnp.concatenate(rows, axis=0)
    return jnp.pad(a, ((0, -a.shape[0] % 8), (0, 0)))


def _unpack_small(a, shapes):
    out, row = [], 0
    for shp in shapes:
        nrows = -(-_size(shp) // HEAD_DIM)
        out.append(a[row:row + nrows].reshape(-1)[:_size(shp)].reshape(shp))
        row += nrows
    return out


SMALL = ["attn_norm_w", "a_log", "dt_bias", "delta_out_norm_w", "q_norm_w", "k_norm_w", "attn_out_norm_w", "ffn_norm_w"]
BIG = ["w_in", "w_out", "w_gate_up", "w_down"]
ORDER = ["attn_norm_w", "w_in", "conv_w", "a_log", "dt_bias", "delta_out_norm_w", "q_norm_w", "k_norm_w",
         "attn_out_norm_w", "w_out", "ffn_norm_w", "w_gate_up", "w_down"]


def kernel(x, positions, attn_norm_w, w_in, conv_w, a_log, dt_bias, delta_out_norm_w, q_norm_w, k_norm_w, attn_out_norm_w, w_out, ffn_norm_w, w_gate_up, w_down, loss_target, m_attn_norm_w, m_w_in, m_conv_w, m_a_log, m_dt_bias, m_delta_out_norm_w, m_q_norm_w, m_k_norm_w, m_attn_out_norm_w, m_w_out, m_ffn_norm_w, m_w_gate_up, m_w_down, v_attn_norm_w, v_w_in, v_conv_w, v_a_log, v_dt_bias, v_delta_out_norm_w, v_q_norm_w, v_k_norm_w, v_attn_out_norm_w, v_w_out, v_ffn_norm_w, v_w_gate_up, v_w_down):
    wts = dict(attn_norm_w=attn_norm_w, w_in=w_in, conv_w=conv_w, a_log=a_log, dt_bias=dt_bias,
               delta_out_norm_w=delta_out_norm_w, q_norm_w=q_norm_w, k_norm_w=k_norm_w,
               attn_out_norm_w=attn_out_norm_w, w_out=w_out, ffn_norm_w=ffn_norm_w, w_gate_up=w_gate_up, w_down=w_down)
    mom = dict(attn_norm_w=m_attn_norm_w, w_in=m_w_in, conv_w=m_conv_w, a_log=m_a_log, dt_bias=m_dt_bias,
               delta_out_norm_w=m_delta_out_norm_w, q_norm_w=m_q_norm_w, k_norm_w=m_k_norm_w,
               attn_out_norm_w=m_attn_out_norm_w, w_out=m_w_out, ffn_norm_w=m_ffn_norm_w, w_gate_up=m_w_gate_up,
               w_down=m_w_down)
    var = dict(attn_norm_w=v_attn_norm_w, w_in=v_w_in, conv_w=v_conv_w, a_log=v_a_log, dt_bias=v_dt_bias,
               delta_out_norm_w=v_delta_out_norm_w, q_norm_w=v_q_norm_w, k_norm_w=v_k_norm_w,
               attn_out_norm_w=v_attn_out_norm_w, w_out=v_w_out, ffn_norm_w=v_ffn_norm_w, w_gate_up=v_w_gate_up,
               w_down=v_w_down)
    dmod = x.shape[2]
    heads = dmod // (2 * HEAD_DIM)
    dw = heads * HEAD_DIM
    chip = 2 * lax.axis_index("x") + lax.axis_index("y")
    core = lax.axis_index("c")
    n_in, n_out, n_gu, n_down, n_conv = (w_in.shape[2], w_out.shape[1], w_gate_up.shape[2], w_down.shape[1],
                                         conv_w.shape[2])

    def slots_of(w, dtype):
        shard = w[0].astype(dtype)
        return lax.dynamic_update_index_in_dim(lax.empty((4,) + shard.shape, dtype), shard, chip, axis=0)

    s_in, s_conv = _all_gather_weights([slots_of(w_in, BF16)], [slots_of(conv_w, F32)], "all_gather_w_in")
    later = [slots_of(w_out, BF16), slots_of(w_gate_up, BF16), slots_of(w_down, BF16)]
    w_send, w_recv, later, started = _split_start(later, s_conv, _gather_plan(3), 9, "gather_rest_start")
    by_cols = lambda a: a.transpose(1, 0, 2).reshape(a.shape[1], 4 * a.shape[2])
    conv_f = by_cols(s_conv)
    w_qkvz, w_bd, w_attn = _regroup_cols([s_in[i] for i in range(4)], [4 * dw, 2 * heads, 3 * dw])
    w_bd = jnp.pad(w_bd, ((0, 0), (0, HEAD_DIM - 2 * heads)))
    small = {n: wts[n] for n in SMALL}
    place = jnp.stack([core, chip]).astype(jnp.int32)
    to_slots = lambda a: a.reshape(a.shape[0], 4, a.shape[1] // 4).transpose(1, 0, 2)

    def later_weights(after):
        landed = _split_wait(w_send, w_recv, later, after, _gather_landed_plan(3), "gather_rest_wait")
        s_out, = _pass_halves_to_sibling(landed[:1], "gather_out_pass")
        p_send, p_recv, passing, token = _split_start(landed[1:], s_out, _pass_plan(2), 6, "gather_ffn_pass_start")

        def ffn_weights(after):
            s_gu, s_down = _split_wait(p_send, p_recv, passing, after, _pass_landed_plan(2), "gather_ffn_pass_wait")
            return by_cols(s_gu), s_down.reshape(4 * n_down, dmod)

        return s_out.reshape(4 * n_out, dmod), token, ffn_weights

    ffn = {}

    def ffn_grads_ready(g_gate, g_up, g_down):
        gs = [to_slots(jnp.concatenate([g_gate, g_up], axis=1)), g_down.reshape(4, n_down, dmod)]
        zones = [lax.empty((4, g.shape[1] // 2, g.shape[2]), BF16) for g in gs]
        s_send, s_recv, s_bufs, swapping = _split_start(gs + zones, g_gate, _swap_plan(2), 2, "swap_ffn_start")

        def and_then(after):
            g_gu, g_dn, b_gu, b_dn = _split_wait(s_send, s_recv, s_bufs, after, _swap_plan(2), "swap_ffn_wait")
            sums = [_add_half_bf16(g_gu, b_gu, place, "chip_partial_sum_w_gate_up"),
                    _add_half_bf16(g_dn, b_dn, place, "chip_partial_sum_w_down")]
            zones3 = [lax.empty((3,) + p.shape[1:], BF16) for p in sums]
            ffn["send"], ffn["recv"], ffn["bufs"], token = _split_start(sums + zones3, b_gu, _scatter_plan(2), 6,
                                                                        "scatter_ffn_start")
            return token

        return swapping, and_then

    rest = {}

    def rest_grads_ready(bg):
        gs = [jnp.stack(_regroup_cols([bg["w_qkv"], bg["w_z"], bg["w_bd"][:, :2 * heads], bg["w_attn"]], [n_in] * 4)),
              jnp.concatenate([bg["w_out_a"], bg["w_out_b"]], axis=0).reshape(4, n_out, dmod)]
        zones = [lax.empty((4, g.shape[1] // 2, g.shape[2]), BF16) for g in gs]
        s_send, s_recv, s_bufs, swapping = _split_start(gs + zones, bg["w_attn"], _swap_plan(2), 2, "swap_rest_start")

        def and_then(after):
            g_in, g_out, b_in, b_out = _split_wait(s_send, s_recv, s_bufs, after, _swap_plan(2), "swap_rest_wait")
            sums = [_add_half_bf16(g_in, b_in, place, "chip_partial_sum_w_in"),
                    _add_half_bf16(g_out, b_out, place, "chip_partial_sum_w_out")]
            zones3 = [lax.empty((3,) + p.shape[1:], BF16) for p in sums]
            rest["send"], rest["recv"], rest["bufs"], token = _split_start(sums + zones3, b_in, _scatter_plan(2), 6,
                                                                           "scatter_rest_start")
            return token

        return swapping, and_then

    loss_row, grad_x, sg = _local_step(
        x[0], positions[0], loss_target[0], small, w_qkvz, w_bd, w_attn, conv_f,
        later_weights, ffn_grads_ready, rest_grads_ready, after=started)

    r_send, r_recv, r_bufs = rest["send"], rest["recv"], rest["bufs"]
    sum_gu, sum_down, got_gu, got_down = _split_wait(ffn["send"], ffn["recv"], ffn["bufs"], grad_x,
                                                     _scatter_plan(2), "scatter_ffn_wait")
    g_big = dict(zip(["w_gate_up", "w_down"], _join_halves(
        [_sum4_f32(sum_gu, got_gu, place, "grad_total_w_gate_up"),
         _sum4_f32(sum_down, got_down, place, "grad_total_w_down")], "join_ffn_halves")))
    grads, deltas, new_m, new_v = {}, {}, {}, {}

    def adamw_big(n):
        shp = wts[n].shape
        d, nm, nv = _adamw(wts[n][0], g_big[n], mom[n][0], var[n][0], "adamw_" + n)
        grads[n], deltas[n], new_m[n], new_v[n] = g_big[n].reshape(shp), d.reshape(shp), nm.reshape(shp), nv.reshape(shp)
        return d

    done = [adamw_big("w_gate_up"), adamw_big("w_down")]
    sum_in, sum_out, got_in, got_out = _split_wait(r_send, r_recv, r_bufs, done, _scatter_plan(2), "scatter_rest_wait")
    g_big.update(zip(["w_in", "w_out"], _join_halves(
        [_sum4_f32(sum_in, got_in, place, "grad_total_w_in"),
         _sum4_f32(sum_out, got_out, place, "grad_total_w_out")], "join_rest_halves")))
    adamw_big("w_in")
    adamw_big("w_out")

    reduced = _all_reduce_small(_pack_small([sg[n] for n in SMALL] + [sg["conv_w"], loss_row]))
    red = _unpack_small(reduced, [wts[n].shape for n in SMALL] + [(4, 4 * n_conv), (1, HEAD_DIM)])
    g_small = dict(zip(SMALL, red[:len(SMALL)]))
    g_conv_full, loss_out = red[len(SMALL)], red[len(SMALL) + 1]
    g_small["conv_w"] = lax.dynamic_slice_in_dim(g_conv_full, chip * n_conv, n_conv, axis=1).reshape(conv_w.shape)

    names = SMALL + ["conv_w"]
    shapes = [wts[n].shape for n in names]
    d, nm, nv = _adamw(_pack_small([wts[n] for n in names]), _pack_small([g_small[n] for n in names]),
                       _pack_small([mom[n] for n in names]), _pack_small([var[n] for n in names]), "adamw_small")
    for n, dd, mm, vv in zip(names, _unpack_small(d, shapes), _unpack_small(nm, shapes), _unpack_small(nv, shapes)):
        grads[n], deltas[n], new_m[n], new_v[n] = g_small[n], dd, mm, vv
    return (loss_out[0, 0], grad_x[None], *[grads[n] for n in ORDER], *[deltas[n] for n in ORDER],
            *[new_m[n] for n in ORDER], *[new_v[n] for n in ORDER])
```

```python
import jax
import jax.numpy as jnp
from jax import lax
from jax.experimental import pallas as pl
from jax.experimental.pallas import tpu as pltpu

F32 = jnp.float32
BF16 = jnp.bfloat16
HEAD_DIM = 128
CHUNK = 128
INV_BLOCK = 64
SPAN = 128
DILATIONS = (1, 4, 16)
ROPE_THETA = 10000.0
EPS = 1e-6
NEG = -1e30
ADAM_LR, ADAM_B1, ADAM_B2, ADAM_EPS, ADAM_WD, ADAM_STEP = 0.001, 0.9, 0.999, 1e-08, 0.01, 10
VMEM_LIMIT = 48 * 1024 * 1024
MATMUL_VMEM = 44 * 1024 * 1024
MESH = pl.DeviceIdType.MESH

_DN = {"nn": (((1,), (0,)), ((), ())), "nt": (((1,), (1,)), ((), ())), "tn": (((0,), (0,)), ((), ()))}


def _dot(a, b, mode="nn"):
    (ca, cb), _ = _DN[mode]
    if a.ndim == 3:
        dn = (((ca[0] + 1,), (cb[0] + 1,)), ((0,), (0,)))
    else:
        dn = _DN[mode]
    return lax.dot_general(a.astype(BF16), b.astype(BF16), dn, preferred_element_type=F32)


def _rsum(x):
    return jnp.sum(x, axis=-1, keepdims=True)


def _csum(x):
    return jnp.sum(x, axis=-2, keepdims=True)


def _tile(dim, pref, unit=128):
    t = (min(pref, dim) // unit) * unit
    while t >= unit:
        if dim % t == 0:
            return t
        t -= unit
    return dim


def _params(sem):
    return pltpu.CompilerParams(dimension_semantics=sem, vmem_limit_bytes=VMEM_LIMIT)


def _sigmoid(x):
    return 1.0 / (1.0 + jnp.exp(-x))


def _matmul(a, b, mode, name, add=None, out_dtype=F32, a_cols=None, b_cols=None, b_rows=None,
            tiles=(), finish=None, out_dtypes=(), row_sum=False, after=None, b2_cols=None):
    if mode == "tn":
        out_dtype = BF16
    a_off, a_w = a_cols if a_cols else (0, a.shape[1])
    b_off, b_w = b_cols if b_cols else (0, b.shape[1])
    br_off, br_n = b_rows if b_rows else (0, b.shape[0])
    if mode == "nn":
        m, k, n = a.shape[0], a_w, b_w
        assert br_n == k
    elif mode == "nt":
        m, k, n = a.shape[0], a_w, br_n
        assert b_w == k
    else:
        k, m, n = a.shape[0], a_w, b_w
        assert b.shape[0] == k
    if finish is None:
        out_dtypes = [out_dtype]
        if add is None:
            finish = lambda acc, vals: (acc,)
        else:
            tiles = [(add, 0)]
            finish = lambda acc, vals: (acc + vals[0].astype(F32),)
    n_b = 1 if b2_cols is None else 2
    sides = sum(jnp.dtype(t.dtype).itemsize for t, _ in tiles) + sum(jnp.dtype(d).itemsize for d in out_dtypes)

    def vmem(tm_, tn_, tk_):
        ops = tm_ * tk_ * jnp.dtype(a.dtype).itemsize + n_b * tk_ * tn_ * jnp.dtype(b.dtype).itemsize
        return 2 * (ops + tm_ * tn_ * sides) + (n_b * tm_ * tn_ * 4 if tk_ < k else 0)

    tm, tn = _tile(m, 1024, 128), _tile(n, 1024, 128)
    tall = _tile(m, 1536, 128), _tile(n, 512, 128)
    if tall[0] * tall[1] > tm * tn and vmem(*tall, k) <= MATMUL_VMEM:
        tm, tn = tall
    tk = next(t for t in [_tile(k, p, 128) for p in (4096, 2048, 1024, 512, 128)] if vmem(tm, tn, t) <= MATMUL_VMEM)
    if mode == "nn":
        assert a_off % tk == 0 and b_off % tn == 0 and br_off % tk == 0
        a_spec = pl.BlockSpec((tm, tk), lambda i, j, kk: (i, kk + a_off // tk))
        b_spec = pl.BlockSpec((tk, tn), lambda i, j, kk: (kk + br_off // tk, j + b_off // tn))
    elif mode == "nt":
        assert a_off % tk == 0 and b_off % tk == 0 and br_off % tn == 0
        a_spec = pl.BlockSpec((tm, tk), lambda i, j, kk: (i, kk + a_off // tk))
        b_spec = pl.BlockSpec((tn, tk), lambda i, j, kk: (j + br_off // tn, kk + b_off // tk))
    else:
        assert a_off % tm == 0 and b_off % tn == 0
        a_spec = pl.BlockSpec((tk, tm), lambda i, j, kk: (kk, i + a_off // tm))
        b_spec = pl.BlockSpec((tk, tn), lambda i, j, kk: (kk, j + b_off // tn))
    nk = k // tk
    n_tiles, n_out = len(tiles), len(out_dtypes)
    extra = [] if after is None else [after]
    first_out = 1 + n_b + n_tiles + len(extra)

    def body(*refs):
        a_ref, b_refs = refs[0], refs[1:1 + n_b]
        tile_refs, out_refs = refs[1 + n_b:1 + n_b + n_tiles], refs[first_out:first_out + n_out]
        acc_refs = refs[-n_b:] if nk > 1 else ()
        kk = pl.program_id(2)
        first_tile = (pl.program_id(0) == 0) & (pl.program_id(1) == 0)

        av = a_ref[...]
        if nk > 1:
            @pl.when(kk == 0)
            def _():
                for acc_ref in acc_refs:
                    acc_ref[...] = jnp.zeros_like(acc_ref)

            for acc_ref, b_ref in zip(acc_refs, b_refs):
                acc_ref[...] += _dot(av, b_ref[...], mode)

        @pl.when(kk == nk - 1)
        def _():
            if nk > 1:
                accs = [acc_ref[...] for acc_ref in acc_refs]
            else:
                accs = [_dot(av, b_ref[...], mode) for b_ref in b_refs]
            res = finish(accs[0] if n_b == 1 else accs, [t[...] for t in tile_refs])
            for o_ref, r in zip(out_refs, res):
                o_ref[...] = r.astype(o_ref.dtype)
            if row_sum:
                row_ref = refs[first_out + n_out]

                @pl.when(first_tile)
                def _():
                    row_ref[...] = res[n_out]

                @pl.when(jnp.logical_not(first_tile))
                def _():
                    row_ref[...] += res[n_out]

    in_specs = [a_spec, b_spec]
    args = [a, b]
    if b2_cols is not None:
        assert mode == "nn" and b2_cols[1] == n and b2_cols[0] % tn == 0
        in_specs.append(pl.BlockSpec((tk, tn), lambda i, j, kk: (kk + br_off // tk, j + b2_cols[0] // tn)))
        args.append(b)
    for arr, off in tiles:
        assert off % tn == 0
        if arr.shape[0] == 1:
            in_specs.append(pl.BlockSpec((1, tn), lambda i, j, kk, off=off: (0, j + off // tn)))
        else:
            in_specs.append(pl.BlockSpec((tm, tn), lambda i, j, kk, off=off: (i, j + off // tn)))
        args.append(arr)
    in_specs += [ANY] * len(extra)
    args += extra
    out_specs = [pl.BlockSpec((tm, tn), lambda i, j, kk: (i, j))] * n_out
    out_shape = [jax.ShapeDtypeStruct((m, n), dt) for dt in out_dtypes]
    if row_sum:
        out_specs.append(pl.BlockSpec((1, HEAD_DIM), lambda i, j, kk: (0, 0)))
        out_shape.append(jax.ShapeDtypeStruct((1, HEAD_DIM), F32))
    res = pl.pallas_call(
        body, name=name, grid=(m // tm, n // tn, nk),
        in_specs=in_specs, out_specs=out_specs, out_shape=out_shape,
        scratch_shapes=[pltpu.VMEM((tm, tn), F32)] * (n_b if nk > 1 else 0),
        compiler_params=_params(("arbitrary",) * 3 if row_sum else ("parallel", "parallel", "arbitrary")),
    )(*args)
    return res[0] if len(res) == 1 else res


def _rmsnorm_fwd(x, w, name, after=None):
    s, d = x.shape
    tr = _tile(s, 1024, 8)

    def body(x_ref, w_ref, *rest):
        h_ref = rest[-1]
        xv = x_ref[...]
        rstd = lax.rsqrt(jnp.mean(xv * xv, axis=-1, keepdims=True) + EPS)
        h_ref[...] = (xv * rstd * w_ref[...]).astype(BF16)

    extra = [] if after is None else [after]
    return pl.pallas_call(
        body, name=name, grid=(s // tr,),
        in_specs=[pl.BlockSpec((tr, d), lambda i: (i, 0)), pl.BlockSpec((1, d), lambda i: (0, 0))] + [ANY] * len(extra),
        out_specs=pl.BlockSpec((tr, d), lambda i: (i, 0)),
        out_shape=jax.ShapeDtypeStruct((s, d), BF16),
        compiler_params=_params(("parallel",)),
    )(x, w, *extra)


def _rmsnorm_bwd(dh, x, w, res, name, after=None):
    s, d = x.shape
    tr = _tile(s, 256, 8)

    def body(dh_ref, x_ref, w_ref, res_ref, *rest):
        dx_ref, dx16_ref, dw_ref = rest[-3:]
        xv = x_ref[...]
        rstd = lax.rsqrt(jnp.mean(xv * xv, axis=-1, keepdims=True) + EPS)
        xhat = xv * rstd
        dhv = dh_ref[...]
        gw = dhv * w_ref[...]
        dx = res_ref[...] + rstd * (gw - xhat * jnp.mean(gw * xhat, axis=-1, keepdims=True))
        dx_ref[...] = dx
        dx16_ref[...] = dx.astype(BF16)

        @pl.when(pl.program_id(0) == 0)
        def _():
            dw_ref[...] = jnp.zeros_like(dw_ref)

        dw_ref[...] += jnp.sum(dhv * xhat, axis=0, keepdims=True)

    row = pl.BlockSpec((tr, d), lambda i: (i, 0))
    vec = pl.BlockSpec((1, d), lambda i: (0, 0))
    extra = [] if after is None else [after]
    return pl.pallas_call(
        body, name=name, grid=(s // tr,),
        in_specs=[row, row, vec, row] + [ANY] * len(extra), out_specs=[row, row, vec],
        out_shape=[jax.ShapeDtypeStruct((s, d), F32), jax.ShapeDtypeStruct((s, d), BF16),
                   jax.ShapeDtypeStruct((1, d), F32)],
        compiler_params=_params(("arbitrary",)),
    )(dh, x, w, res, *extra)


def _conv_taps(x, w, rows):
    shifted = [x]
    for sft in (1, 2, 3):
        shifted.append(jnp.where(rows >= sft, pltpu.roll(x, sft, 0), 0.0))
    y = w[3:4, :] * shifted[0] + w[2:3, :] * shifted[1] + w[1:2, :] * shifted[2] + w[0:1, :] * shifted[3]
    return y, shifted


def _delta_pre_fwd(qkvz, conv_w, heads):
    s = qkvz.shape[0]
    nblk = 3 * heads

    def body(x_ref, w_ref, o_ref):
        part = pl.program_id(0) // heads
        rows = lax.broadcasted_iota(jnp.int32, (s, HEAD_DIM), 0)
        y, _ = _conv_taps(x_ref[...], w_ref[...], rows)
        a = y * _sigmoid(y)
        rs = lax.rsqrt(jnp.sum(a * a, axis=-1, keepdims=True) + EPS)
        fac = jnp.where(part == 0, rs * (HEAD_DIM ** -0.5), jnp.where(part == 1, rs, 1.0))
        o_ref[...] = a * fac

    return pl.pallas_call(
        body, name="delta_pre_fwd", grid=(nblk,),
        in_specs=[pl.BlockSpec((s, HEAD_DIM), lambda i: (0, i)), pl.BlockSpec((4, HEAD_DIM), lambda i: (0, i))],
        out_specs=pl.BlockSpec((s, HEAD_DIM), lambda i: (0, i)),
        out_shape=jax.ShapeDtypeStruct((s, 3 * heads * HEAD_DIM), F32),
        compiler_params=_params(("parallel",)),
    )(qkvz, conv_w)


def _delta_pre_bwd(dqkv, qkvz, conv_w, heads):
    s = qkvz.shape[0]
    nblk = 3 * heads

    def body(d_ref, x_ref, w_ref, dx_ref, dw_ref):
        part = pl.program_id(0) // heads
        rows = lax.broadcasted_iota(jnp.int32, (s, HEAD_DIM), 0)
        w = w_ref[...]
        y, shifted = _conv_taps(x_ref[...], w, rows)
        sg = _sigmoid(y)
        a = y * sg
        rs = lax.rsqrt(jnp.sum(a * a, axis=-1, keepdims=True) + EPS)
        unit = a * rs
        dn = d_ref[...]
        scale = jnp.where(part == 0, HEAD_DIM ** -0.5, 1.0)
        da_norm = scale * rs * (dn - unit * jnp.sum(dn * unit, axis=-1, keepdims=True))
        da = jnp.where(part < 2, da_norm, dn)
        dy = da * sg * (1.0 + y * (1.0 - sg))
        dx = w[3:4, :] * dy
        for sft in (1, 2, 3):
            dx = dx + w[3 - sft:4 - sft, :] * jnp.where(rows < s - sft, pltpu.roll(dy, s - sft, 0), 0.0)
        dx_ref[...] = dx.astype(BF16)
        for sft in range(4):
            dw_ref[3 - sft:4 - sft, :] = jnp.sum(dy * shifted[sft], axis=0, keepdims=True)

    col = pl.BlockSpec((s, HEAD_DIM), lambda i: (0, i))
    wsp = pl.BlockSpec((4, HEAD_DIM), lambda i: (0, i))
    return pl.pallas_call(
        body, name="delta_pre_bwd", grid=(nblk,),
        in_specs=[col, col, wsp], out_specs=[col, wsp],
        out_shape=[jax.ShapeDtypeStruct((s, 3 * heads * HEAD_DIM), BF16),
                   jax.ShapeDtypeStruct((4, 3 * heads * HEAD_DIM), F32)],
        compiler_params=_params(("parallel",)),
    )(dqkv, qkvz, conv_w)


def _heads_of(ref, heads):
    return jnp.stack([ref[:, h * HEAD_DIM:(h + 1) * HEAD_DIM] for h in range(heads)])


def _chunk_common(q, k, v, bd, a_log, dt_bias, heads, solved=None):
    c = CHUNK
    braw = jnp.stack([bd[:, h:h + 1] for h in range(heads)])
    draw = jnp.stack([bd[:, heads + h:heads + h + 1] for h in range(heads)])
    beta = _sigmoid(braw)
    xd = draw + dt_bias
    sp = jnp.maximum(xd, 0.0) + jnp.log1p(jnp.exp(-jnp.abs(xd)))
    g = -jnp.exp(a_log) * sp
    row = lax.broadcasted_iota(jnp.int32, (c, c), 0)
    col = lax.broadcasted_iota(jnp.int32, (c, c), 1)
    sq = (heads, c, c)
    g_b = jnp.broadcast_to(g, sq)
    g_row = _csum(jnp.where(row == col, g_b, 0.0))
    gam_col = _rsum(jnp.where(col <= row, jnp.broadcast_to(g_row, sq), 0.0))
    gam_row = _csum(jnp.where(row <= col, g_b, 0.0))
    causal = row >= col
    dm = jnp.where(causal, jnp.exp(jnp.where(causal, gam_col - gam_row, 0.0)), 0.0)
    kk = _dot(k, k, "nt")
    e = jnp.exp(gam_col)
    if solved is None:
        low = jnp.where(row > col, beta * kk * dm, 0.0)
        assert c in (INV_BLOCK, 2 * INV_BLOCK)
        same = (row // INV_BLOCK) == (col // INV_BLOCK)
        diag = jnp.where(same, low, 0.0)
        t = jnp.where(row == col, 1.0, 0.0) - diag
        pw = diag
        for _ in range((INV_BLOCK - 1).bit_length() - 1):
            pw = _dot(pw, pw)
            t = t + _dot(t, pw)
        if c > INV_BLOCK:
            t = t - _dot(_dot(t, low - diag), t)
        u = _dot(t, beta * v)
        w = _dot(t, (beta * e) * k)
    else:
        t, u, w = solved
    qk_raw = _dot(q, k, "nt")
    gl = _csum(g)
    el = jnp.exp(gl - gam_col)
    return dict(beta=beta, xd=xd, g=g, row=row, col=col, dm=dm, kk=kk, t=t, e=e, u=u, w=w,
                qk_raw=qk_raw, qk=qk_raw * dm, gl=gl, el=el, qd=e * q, kd=el * k, cd=jnp.exp(gl))


def _delta_chunk_fwd(qkv, bd, a_log, dt_bias, heads):
    s = qkv.shape[0]
    n = s // CHUNK
    dw = heads * HEAD_DIM
    blk = lambda part: pl.BlockSpec((CHUNK, dw), lambda i: (i, part))

    def body(q_ref, k_ref, v_ref, bd_ref, al_ref, dt_ref, o_ref, st_ref, t_ref, uw_ref, state):
        @pl.when(pl.program_id(0) == 0)
        def _():
            state[...] = jnp.zeros_like(state)

        cm = _chunk_common(_heads_of(q_ref, heads), _heads_of(k_ref, heads), _heads_of(v_ref, heads), bd_ref[...],
                           al_ref[...], dt_ref[...], heads)
        st = state[...]
        st_ref[0] = st
        t_ref[0] = cm["t"]
        uw_ref[0, 0] = cm["u"]
        uw_ref[0, 1] = cm["w"]
        vn = cm["u"] - _dot(cm["w"], st)
        o = _dot(cm["qd"], st) + _dot(cm["qk"], vn)
        for h in range(heads):
            o_ref[:, h * HEAD_DIM:(h + 1) * HEAD_DIM] = o[h]
        state[...] = cm["cd"] * st + _dot(cm["kd"], vn, "tn")

    smem = pl.BlockSpec((heads, 1, 1), lambda i: (0, 0, 0))
    return pl.pallas_call(
        body, name="delta_chunk_fwd", grid=(n,),
        in_specs=[blk(0), blk(1), blk(2), pl.BlockSpec((CHUNK, HEAD_DIM), lambda i: (i, 0)), smem, smem],
        out_specs=[pl.BlockSpec((CHUNK, dw), lambda i: (i, 0)),
                   pl.BlockSpec((1, heads, HEAD_DIM, HEAD_DIM), lambda i: (i, 0, 0, 0)),
                   pl.BlockSpec((1, heads, CHUNK, CHUNK), lambda i: (i, 0, 0, 0)),
                   pl.BlockSpec((1, 2, heads, CHUNK, HEAD_DIM), lambda i: (i, 0, 0, 0, 0))],
        out_shape=[jax.ShapeDtypeStruct((s, dw), F32),
                   jax.ShapeDtypeStruct((n, heads, HEAD_DIM, HEAD_DIM), F32),
                   jax.ShapeDtypeStruct((n, heads, CHUNK, CHUNK), F32),
                   jax.ShapeDtypeStruct((n, 2, heads, CHUNK, HEAD_DIM), F32)],
        scratch_shapes=[pltpu.VMEM((heads, HEAD_DIM, HEAD_DIM), F32)],
        compiler_params=_params(("arbitrary",)),
    )(qkv, qkv, qkv, bd, a_log, dt_bias)


def _delta_chunk_bwd(do, qkv, bd, saved, a_log, dt_bias, heads):
    s = qkv.shape[0]
    n = s // CHUNK
    dw = heads * HEAD_DIM
    c = CHUNK
    blk = lambda part: pl.BlockSpec((CHUNK, dw), lambda i: (n - 1 - i, part))

    def all_heads(q, k, v, dov, st, solved, dsn, bd, a_log, dt_bias):
        cm = _chunk_common(q, k, v, bd, a_log, dt_bias, heads, solved)
        beta, e, dm, row, col = cm["beta"], cm["e"], cm["dm"], cm["row"], cm["col"]
        sq = (heads, c, c)
        vn = cm["u"] - _dot(cm["w"], st)
        dvn = _dot(cm["kd"], dsn)
        dkd = _dot(vn, dsn, "nt")
        dcd = _csum(_rsum(st * dsn))
        ds = cm["cd"] * dsn
        dqd = _dot(dov, st, "nt")
        ds = ds + _dot(cm["qd"], dov, "tn")
        dqk = _dot(dov, vn, "nt")
        dvn = dvn + _dot(cm["qk"], dov, "tn")
        dw_ = -_dot(dvn, st, "nt")
        ds = ds - _dot(cm["w"], dvn, "tn")
        drhs_u = _dot(cm["t"], dvn, "tn")
        drhs_w = _dot(cm["t"], dw_, "tn")
        da = -(_dot(drhs_u, cm["u"], "nt") + _dot(drhs_w, cm["w"], "nt"))
        dl = jnp.where(row > col, da, 0.0)
        dbeta = _rsum(dl * cm["kk"] * dm)
        dkk = dl * beta * dm
        dd = dl * beta * cm["kk"]
        dv = beta * drhs_u
        ek = e * k
        dbeta = dbeta + _rsum(drhs_u * v) + _rsum(drhs_w * ek)
        dk = (beta * e) * drhs_w
        dgam = _rsum(drhs_w * (beta * ek))
        dqkm = dqk * dm
        dq = _dot(dqkm, k)
        dk = dk + _dot(dqkm, q, "tn")
        dd = dd + dqk * cm["qk_raw"]
        dk = dk + _dot(dkk, k) + _dot(dkk, k, "tn")
        dq = dq + e * dqd
        dgam = dgam + _rsum(dqd * cm["qd"])
        dk = dk + cm["el"] * dkd
        r = _rsum(dkd * cm["kd"])
        dgam = dgam - r
        dgl = _csum(r) + dcd * cm["cd"]
        mm = dd * dm
        colsum_c = _rsum(jnp.where(row == col, jnp.broadcast_to(_csum(mm), sq), 0.0))
        dgam = dgam + _rsum(mm) - colsum_c
        ridx = lax.broadcasted_iota(jnp.int32, (c, 1), 0)
        dgam = dgam + jnp.where(ridx == c - 1, dgl, 0.0)
        dgam_row = _csum(jnp.where(row == col, jnp.broadcast_to(dgam, sq), 0.0))
        dg = _rsum(jnp.where(col >= row, jnp.broadcast_to(dgam_row, sq), 0.0))
        d_xd = dg * (-jnp.exp(a_log)) * _sigmoid(cm["xd"])
        d_braw = dbeta * beta * (1.0 - beta)
        d_alog = dg * cm["g"]
        lane = lax.broadcasted_iota(jnp.int32, (c, HEAD_DIM), 1)
        dbd = jnp.zeros((c, HEAD_DIM), F32)
        for h in range(heads):
            dbd = (dbd + jnp.where(lane == h, d_braw[h], 0.0) + jnp.where(lane == h + heads, d_xd[h], 0.0)
                   + jnp.where(lane == h + 2 * heads, d_alog[h], 0.0))
        return dq, dk, dv, ds, dbd

    def body(do_ref, q_ref, k_ref, v_ref, bd_ref, st_ref, t_ref, uw_ref, al_ref, dt_ref, dqkv_ref, dbd_ref, dstate):
        @pl.when(pl.program_id(0) == 0)
        def _():
            dstate[...] = jnp.zeros_like(dstate)

        dq, dk, dv, ds, dbd = all_heads(_heads_of(q_ref, heads), _heads_of(k_ref, heads), _heads_of(v_ref, heads),
                                        _heads_of(do_ref, heads), st_ref[0], (t_ref[0], uw_ref[0, 0], uw_ref[0, 1]),
                                        dstate[...], bd_ref[...],
                                        al_ref[...], dt_ref[...])
        for part, val in enumerate((dq, dk, dv)):
            for h in range(heads):
                lo = part * dw + h * HEAD_DIM
                dqkv_ref[:, lo:lo + HEAD_DIM] = val[h]
        dstate[...] = ds
        dbd_ref[...] = dbd

    smem = pl.BlockSpec((heads, 1, 1), lambda i: (0, 0, 0))
    shared = pl.BlockSpec((CHUNK, HEAD_DIM), lambda i: (n - 1 - i, 0))
    wide = pl.BlockSpec((CHUNK, dw), lambda i: (n - 1 - i, 0))
    return pl.pallas_call(
        body, name="delta_chunk_bwd", grid=(n,),
        in_specs=[wide, blk(0), blk(1), blk(2), shared,
                  pl.BlockSpec((1, heads, HEAD_DIM, HEAD_DIM), lambda i: (n - 1 - i, 0, 0, 0)),
                  pl.BlockSpec((1, heads, CHUNK, CHUNK), lambda i: (n - 1 - i, 0, 0, 0)),
                  pl.BlockSpec((1, 2, heads, CHUNK, HEAD_DIM), lambda i: (n - 1 - i, 0, 0, 0, 0)), smem, smem],
        out_specs=[pl.BlockSpec((CHUNK, 3 * dw), lambda i: (n - 1 - i, 0)), shared],
        out_shape=[jax.ShapeDtypeStruct((s, 3 * dw), F32), jax.ShapeDtypeStruct((s, HEAD_DIM), F32)],
        scratch_shapes=[pltpu.VMEM((heads, HEAD_DIM, HEAD_DIM), F32)],
        compiler_params=_params(("arbitrary",)),
    )(do, qkv, qkv, qkv, bd, *saved, a_log, dt_bias)


def _delta_post_fwd(o, qkvz, w, heads):
    s = o.shape[0]
    tr = _tile(s, 4096, 8)

    def body(o_ref, z_ref, w_ref, out_ref):
        ov, z = o_ref[...], z_ref[...]
        rstd = lax.rsqrt(jnp.mean(ov * ov, axis=-1, keepdims=True) + EPS)
        out_ref[...] = (ov * rstd * w_ref[...] * (z * _sigmoid(z))).astype(BF16)

    return pl.pallas_call(
        body, name="delta_post_fwd", grid=(s // tr, heads),
        in_specs=[pl.BlockSpec((tr, HEAD_DIM), lambda i, h: (i, h)),
                  pl.BlockSpec((tr, HEAD_DIM), lambda i, h: (i, 3 * heads + h)),
                  pl.BlockSpec((1, HEAD_DIM), lambda i, h: (0, 0))],
        out_specs=pl.BlockSpec((tr, HEAD_DIM), lambda i, h: (i, h)),
        out_shape=jax.ShapeDtypeStruct((s, heads * HEAD_DIM), BF16),
        compiler_params=_params(("parallel", "parallel")),
    )(o, qkvz, w)


def _rope_tables(positions, s):
    half = HEAD_DIM // 2
    inv_freq = ROPE_THETA ** (-jnp.arange(half, dtype=F32) / half)
    ang = positions.reshape(s, 1).astype(F32) * inv_freq
    cos, sin = jnp.cos(ang), jnp.sin(ang)
    return jnp.concatenate([cos, cos], axis=-1), jnp.concatenate([-sin, sin], axis=-1)


def _attn_pre_fwd(aqkv, wq, wk, cosf, sinf, heads):
    s = aqkv.shape[0]
    tr = _tile(s, 4096, 8)

    def body(x_ref, wq_ref, wk_ref, c_ref, s_ref, o_ref):
        xv = x_ref[...]
        wv = jnp.where(pl.program_id(1) < heads, wq_ref[...], wk_ref[...])
        y = xv * lax.rsqrt(jnp.mean(xv * xv, axis=-1, keepdims=True) + EPS) * wv
        o_ref[...] = y * c_ref[...] + pltpu.roll(y, HEAD_DIM // 2, 1) * s_ref[...]

    blk = pl.BlockSpec((tr, HEAD_DIM), lambda i, j: (i, j))
    vec = pl.BlockSpec((1, HEAD_DIM), lambda i, j: (0, 0))
    tab = pl.BlockSpec((tr, HEAD_DIM), lambda i, j: (i, 0))
    return pl.pallas_call(
        body, name="attn_pre_fwd", grid=(s // tr, 2 * heads),
        in_specs=[blk, vec, vec, tab, tab], out_specs=blk,
        out_shape=jax.ShapeDtypeStruct((s, 2 * heads * HEAD_DIM), F32),
        compiler_params=_params(("parallel", "parallel")),
    )(aqkv, wq, wk, cosf, sinf)


def _band():
    qi = lax.broadcasted_iota(jnp.int32, (SPAN, 2 * SPAN), 0)
    ki = lax.broadcasted_iota(jnp.int32, (SPAN, 2 * SPAN), 1)
    dist = qi + SPAN - ki
    return (dist >= 0) & (dist <= SPAN), ki >= SPAN


def _sub(g, r, d):
    if d == 1:
        return pl.ds(g * SPAN, SPAN)
    return pl.ds(g * SPAN * d + r, SPAN, stride=d)


def _attn_blocks(s):
    assert s % (SPAN * max(DILATIONS)) == 0
    return [(p_i, d, r, g) for p_i, d in enumerate(DILATIONS) for r in range(d) for g in range(s // (SPAN * d))]


def _attn_fwd(qk, aqkv, w, heads):
    s = qk.shape[0]
    aw = heads * HEAD_DIM

    def body(q_ref, k_ref, v_ref, w_ref, mix_ref, acc_ref, m_ref, l_ref):
        band, own = _band()
        for p_i, d, r, g in _attn_blocks(s):
            rows = _sub(g, r, d)
            kc, vc = k_ref[rows, :], v_ref[rows, :]
            if g == 0:
                kp, vp, mask = kc, vc, band & own
            else:
                mask = band
            kcat = jnp.concatenate([kp, kc], axis=0)
            vcat = jnp.concatenate([vp, vc], axis=0)
            kp, vp = kc, vc
            sc = _dot(q_ref[rows, :] * (HEAD_DIM ** -0.5), kcat, "nt")
            sc = jnp.where(mask, sc, NEG)
            m = jnp.max(sc, axis=-1, keepdims=True)
            if p_i == 0:
                p = jnp.exp(sc - m)
                acc_ref[rows, :] = _dot(p, vcat)
                l_new = jnp.sum(p, axis=-1, keepdims=True)
            else:
                m_old = m_ref[rows, 0:1]
                m = jnp.maximum(m, m_old)
                alpha = jnp.exp(m_old - m)
                p = jnp.exp(sc - m)
                acc_ref[rows, :] = alpha * acc_ref[rows, :] + _dot(p, vcat)
                l_new = alpha * l_ref[rows, 0:1] + jnp.sum(p, axis=-1, keepdims=True)
            m_ref[rows, :] = jnp.broadcast_to(m, (SPAN, HEAD_DIM))
            l_ref[rows, :] = jnp.broadcast_to(l_new, (SPAN, HEAD_DIM))
        den = l_ref[...]
        ob = acc_ref[...] / den
        acc_ref[...] = ob
        m_ref[...] = m_ref[...] + jnp.log(den)
        rstd = lax.rsqrt(jnp.mean(ob * ob, axis=-1, keepdims=True) + EPS)
        mix_ref[...] = (ob * rstd * w_ref[...]).astype(BF16)

    col = lambda off: pl.BlockSpec((s, HEAD_DIM), lambda h: (0, off + h))
    return pl.pallas_call(
        body, name="attn_fwd", grid=(heads,),
        in_specs=[col(0), col(heads), col(2 * heads), pl.BlockSpec((1, HEAD_DIM), lambda h: (0, 0))],
        out_specs=[col(0), col(0), col(0)],
        out_shape=[jax.ShapeDtypeStruct((s, aw), BF16), jax.ShapeDtypeStruct((s, aw), F32),
                   jax.ShapeDtypeStruct((s, aw), F32)],
        scratch_shapes=[pltpu.VMEM((s, HEAD_DIM), F32)],
        compiler_params=_params(("parallel",)),
    )(qk, qk, aqkv, w)


def _attn_bwd(qk, aqkv, do, lse, dsum, heads):
    s = qk.shape[0]
    aw = heads * HEAD_DIM
    scale = HEAD_DIM ** -0.5

    def body(q_ref, k_ref, v_ref, do_ref, l_ref, ds_ref, out_ref):
        band, own = _band()
        dq_ref, dk_ref, dv_ref = out_ref.at[0], out_ref.at[1], out_ref.at[2]
        out_ref[...] = jnp.zeros((3, s, HEAD_DIM), F32)
        for _, d, r, g in _attn_blocks(s):
            rows = _sub(g, r, d)
            qs, dov = q_ref[rows, :] * scale, do_ref[rows, :]
            kc, vc = k_ref[rows, :], v_ref[rows, :]
            if g == 0:
                kp, vp, mask = kc, vc, band & own
            else:
                mask = band
            kcat = jnp.concatenate([kp, kc], axis=0)
            vcat = jnp.concatenate([vp, vc], axis=0)
            p = jnp.where(mask, jnp.exp(_dot(qs, kcat, "nt") - l_ref[rows, 0:1]), 0.0)
            dsc = p * (_dot(dov, vcat, "nt") - ds_ref[rows, 0:1])
            dq_ref[rows, :] += scale * _dot(dsc, kcat)
            dk = _dot(dsc, qs, "tn")
            dv = _dot(p, dov, "tn")
            if g > 0:
                dk_ref[prows, :] += dk_own + dk[:SPAN]
                dv_ref[prows, :] += dv_own + dv[:SPAN]
            dk_own, dv_own = dk[SPAN:], dv[SPAN:]
            if g == s // (SPAN * d) - 1:
                dk_ref[rows, :] += dk_own
                dv_ref[rows, :] += dv_own
            kp, vp, prows = kc, vc, rows

    col = lambda off: pl.BlockSpec((s, HEAD_DIM), lambda h: (0, off + h))
    return pl.pallas_call(
        body, name="attn_bwd", grid=(heads,),
        in_specs=[col(0), col(heads), col(2 * heads), col(0), col(0), col(0)],
        out_specs=pl.BlockSpec((3, s, HEAD_DIM), lambda h: (0, 0, h)),
        out_shape=jax.ShapeDtypeStruct((3, s, aw), F32),
        compiler_params=_params(("parallel",)),
    )(qk, qk, aqkv, do, lse, dsum)


def _attn_pre_bwd(grads, aqkv, wq, wk, cosf, sinf, heads):
    s = aqkv.shape[0]
    tr = _tile(s, 4096, 8)
    nrow = s // tr

    def body(g_ref, x_ref, wq_ref, wk_ref, c_ref, s_ref, dx_ref, dwq_ref, dwk_ref):
        i, j = pl.program_id(0), pl.program_id(1)
        kind = j // heads
        dout = g_ref[0]
        tot_v = dout
        dy = dout * c_ref[...] + pltpu.roll(dout * s_ref[...], HEAD_DIM // 2, 1)
        xv = x_ref[...]
        wv = jnp.where(kind == 0, wq_ref[...], wk_ref[...])
        rstd = lax.rsqrt(jnp.mean(xv * xv, axis=-1, keepdims=True) + EPS)
        xhat = xv * rstd
        gw = dy * wv
        dxn = rstd * (gw - xhat * jnp.mean(gw * xhat, axis=-1, keepdims=True))
        dx_ref[...] = jnp.where(kind == 2, tot_v, dxn).astype(BF16)
        dwc = jnp.sum(dy * xhat, axis=0, keepdims=True)

        @pl.when((i == 0) & (j == 0))
        def _():
            dwq_ref[...] = jnp.zeros_like(dwq_ref)
            dwk_ref[...] = jnp.zeros_like(dwk_ref)

        @pl.when(kind == 0)
        def _():
            dwq_ref[...] += dwc

        @pl.when(kind == 1)
        def _():
            dwk_ref[...] += dwc

    grad = pl.BlockSpec((1, tr, HEAD_DIM), lambda i, j: (j // heads, i, j % heads))
    blk = pl.BlockSpec((tr, HEAD_DIM), lambda i, j: (i, j))
    vec = pl.BlockSpec((1, HEAD_DIM), lambda i, j: (0, 0))
    tab = pl.BlockSpec((tr, HEAD_DIM), lambda i, j: (i, 0))
    return pl.pallas_call(
        body, name="attn_pre_bwd", grid=(nrow, 3 * heads),
        in_specs=[grad, blk, vec, vec, tab, tab], out_specs=[blk, vec, vec],
        out_shape=[jax.ShapeDtypeStruct((s, 3 * heads * HEAD_DIM), BF16),
                   jax.ShapeDtypeStruct((1, HEAD_DIM), F32), jax.ShapeDtypeStruct((1, HEAD_DIM), F32)],
        compiler_params=_params(("arbitrary", "arbitrary")),
    )(grads, aqkv, wq, wk, cosf, sinf)


def _colsum(a, name):
    s, d = a.shape
    tr = _tile(s, 4096, 8)

    def body(a_ref, o_ref):
        @pl.when(pl.program_id(0) == 0)
        def _():
            o_ref[...] = jnp.zeros_like(o_ref)

        o_ref[...] += jnp.sum(a_ref[...], axis=0, keepdims=True)

    return pl.pallas_call(
        body, name=name, grid=(s // tr,),
        in_specs=[pl.BlockSpec((tr, d), lambda i: (i, 0))], out_specs=pl.BlockSpec((1, d), lambda i: (0, 0)),
        out_shape=jax.ShapeDtypeStruct((1, d), F32),
        compiler_params=_params(("arbitrary",)),
    )(a)


def _local_step(x, positions, target, small, w_qkvz, w_bd, w_attn, conv_w, later_weights, ffn_grads_ready,
                rest_grads_ready, after=None):
    s, dmod = x.shape
    heads = dmod // (2 * HEAD_DIM)
    dw = heads * HEAD_DIM
    a_log, dt_bias = small["a_log"].reshape(heads, 1, 1), small["dt_bias"].reshape(heads, 1, 1)
    cosf, sinf = _rope_tables(positions, s)

    h1 = _rmsnorm_fwd(x, small["attn_norm_w"], "norm1_fwd", after=after)
    qkvz = _matmul(h1, w_qkvz, "nn", "proj_qkvz")
    bd = _matmul(h1, w_bd, "nn", "proj_bd")
    aqkv = _matmul(h1, w_attn, "nn", "proj_attn")
    dqkv = _delta_pre_fwd(qkvz, conv_w, heads)
    o_d, *saved = _delta_chunk_fwd(dqkv, bd, a_log, dt_bias, heads)
    mix_a = _delta_post_fwd(o_d, qkvz, small["delta_out_norm_w"], heads)
    qk_rot = _attn_pre_fwd(aqkv, small["q_norm_w"], small["k_norm_w"], cosf, sinf, heads)
    mix_b, ob, lse = _attn_fwd(qk_rot, aqkv, small["attn_out_norm_w"], heads)
    w_out, behind, ffn_weights = later_weights((mix_a, mix_b))
    x1 = _matmul(mix_a, w_out, "nn", "out_proj_a", add=x, b_rows=(0, dw), after=behind)
    x1 = _matmul(mix_b, w_out, "nn", "out_proj_b", add=x1, b_rows=(dw, dw))
    h2 = _rmsnorm_fwd(x1, small["ffn_norm_w"], "norm2_fwd")
    w_gu, w_down = ffn_weights(h2)
    ff = w_down.shape[0]

    def swiglu(accs, vals):
        g, u = accs
        return g, u, g * _sigmoid(g) * u

    gate, up, act = _matmul(h2, w_gu, "nn", "ffn_gate_up", b_cols=(0, ff), b2_cols=(ff, ff), finish=swiglu,
                            out_dtypes=[BF16, BF16, BF16])

    def loss_head(acc, vals):
        err = acc + vals[0] - vals[1]
        part = 0.5 * jnp.sum(jnp.sum(err * err, axis=-1, keepdims=True) * (1.0 / dmod), axis=0, keepdims=True)
        lane = lax.broadcasted_iota(jnp.int32, (1, HEAD_DIM), 1)
        return err * (1.0 / dmod), err * (1.0 / dmod), jnp.where(lane == 0, part, 0.0)

    dy, dy16, loss_row = _matmul(act, w_down, "nn", "ffn_down", tiles=[(x1, 0), (target, 0)], finish=loss_head,
                                 out_dtypes=[F32, BF16], row_sum=True)

    def swiglu_bwd(acc, vals):
        g, u = vals[0].astype(F32), vals[1].astype(F32)
        sg = _sigmoid(g)
        return acc * u * sg * (1.0 + g * (1.0 - sg)), acc * g * sg

    dgate, dup = _matmul(dy16, w_down, "nt", "ffn_down_dx", tiles=[(gate, 0), (up, 0)], finish=swiglu_bwd,
                         out_dtypes=[BF16, BF16])
    g_w_down = _matmul(act, dy16, "tn", "ffn_down_dw")
    g_w_gate = _matmul(h2, dgate, "tn", "ffn_gate_dw")
    g_w_up = _matmul(h2, dup, "tn", "ffn_up_dw")
    behind, and_then = ffn_grads_ready(g_w_gate, g_w_up, g_w_down)
    dh2 = _matmul(dgate, w_gu, "nt", "ffn_gate_dx", b_cols=(0, ff), after=behind)
    dh2 = _matmul(dup, w_gu, "nt", "ffn_up_dx", b_cols=(ff, ff), add=dh2)
    behind = and_then(dh2)
    dx1, dx1_16, g_ffn_norm = _rmsnorm_bwd(dh2, x1, small["ffn_norm_w"], dy, "norm2_bwd", after=behind)
    def per_head(fn, tile_w):
        outs = [fn(slice(h * HEAD_DIM, (h + 1) * HEAD_DIM)) for h in range(tile_w // HEAD_DIM)]
        joined = [jnp.concatenate([o[i] for o in outs], axis=1) for i in range(len(outs[0]) - 1)]
        return (*joined, sum(o[-1] for o in outs))

    def delta_post_bwd(acc, vals):
        def one(hs):
            d, ov, z, wv = acc[:, hs], vals[0][:, hs], vals[1][:, hs], vals[2][:, hs]
            sg = _sigmoid(z)
            rstd = lax.rsqrt(jnp.mean(ov * ov, axis=-1, keepdims=True) + EPS)
            ohat = ov * rstd
            dn = d * (z * sg)
            gw = dn * wv
            return (rstd * (gw - ohat * jnp.mean(gw * ohat, axis=-1, keepdims=True)),
                    d * (ohat * wv) * sg * (1.0 + z * (1.0 - sg)), jnp.sum(dn * ohat, axis=0, keepdims=True))
        return per_head(one, acc.shape[1])

    def attn_merge_bwd(acc, vals):
        def one(hs):
            d, ov, wv = acc[:, hs], vals[0][:, hs], vals[1][:, hs]
            rstd = lax.rsqrt(jnp.mean(ov * ov, axis=-1, keepdims=True) + EPS)
            ohat = ov * rstd
            gw = d * wv
            dov = rstd * (gw - ohat * jnp.mean(gw * ohat, axis=-1, keepdims=True))
            return (dov, jnp.broadcast_to(jnp.sum(dov * ov, axis=-1, keepdims=True), dov.shape),
                    jnp.sum(d * ohat, axis=0, keepdims=True))
        return per_head(one, acc.shape[1])

    rep = lambda wv: jnp.tile(wv, (1, heads))
    do_d, dz, g_delta_out_norm = _matmul(
        dx1_16, w_out, "nt", "out_proj_dx_a", b_rows=(0, dw), finish=delta_post_bwd, out_dtypes=[F32, BF16],
        tiles=[(o_d, 0), (qkvz, 3 * dw), (rep(small["delta_out_norm_w"]), 0)], row_sum=True)
    dob, dsum, g_attn_out_norm = _matmul(
        dx1_16, w_out, "nt", "out_proj_dx_b", b_rows=(dw, dw), finish=attn_merge_bwd, out_dtypes=[F32, F32],
        tiles=[(ob, 0), (rep(small["attn_out_norm_w"]), 0)], row_sum=True)
    g_w_out_a = _matmul(mix_a, dx1_16, "tn", "out_proj_dw_a")
    g_w_out_b = _matmul(mix_b, dx1_16, "tn", "out_proj_dw_b")
    grads = _attn_bwd(qk_rot, aqkv, dob, lse, dsum, heads)
    d_aqkv, g_q_norm, g_k_norm = _attn_pre_bwd(grads, aqkv, small["q_norm_w"], small["k_norm_w"], cosf, sinf, heads)
    ddqkv, dbd = _delta_chunk_bwd(do_d, dqkv, bd, saved, a_log, dt_bias, heads)
    d_qkv_raw, g_conv = _delta_pre_bwd(ddqkv, qkvz, conv_w, heads)
    bd_sums = _colsum(dbd, "bd_colsum")
    g_w_qkv = _matmul(h1, d_qkv_raw, "tn", "proj_qkv_dw")
    g_w_z = _matmul(h1, dz, "tn", "proj_z_dw")
    g_w_bd = _matmul(h1, dbd, "tn", "proj_bd_dw")
    g_w_attn = _matmul(h1, d_aqkv, "tn", "proj_attn_dw")
    behind, and_then = rest_grads_ready(dict(w_qkv=g_w_qkv, w_z=g_w_z, w_bd=g_w_bd, w_attn=g_w_attn,
                                             w_out_a=g_w_out_a, w_out_b=g_w_out_b))
    dh1 = _matmul(d_qkv_raw, w_qkvz, "nt", "proj_qkv_dx", b_cols=(0, 3 * dw), after=behind)
    dh1 = _matmul(dz, w_qkvz, "nt", "proj_z_dx", b_cols=(3 * dw, dw), add=dh1, after=and_then(dh1))
    dh1 = _matmul(d_aqkv, w_attn, "nt", "proj_attn_dx", add=dh1)
    dh1 = _matmul(dbd, w_bd, "nt", "proj_bd_dx", add=dh1)
    grad_x, _, g_attn_norm = _rmsnorm_bwd(dh1, x, small["attn_norm_w"], dx1, "norm1_bwd")
    small_grads = dict(
        attn_norm_w=g_attn_norm, a_log=bd_sums[:, 2 * heads:3 * heads], dt_bias=bd_sums[:, heads:2 * heads],
        delta_out_norm_w=g_delta_out_norm, q_norm_w=g_q_norm, k_norm_w=g_k_norm,
        attn_out_norm_w=g_attn_out_norm, ffn_norm_w=g_ffn_norm, conv_w=g_conv)
    return loss_row, grad_x, small_grads


def _adamw(w, g, m, v, name):
    r, c = w.shape
    tr = _tile(r, 256, 8)

    def body(w_ref, g_ref, m_ref, v_ref, d_ref, nm_ref, nv_ref):
        gv = g_ref[...]
        nm = ADAM_B1 * m_ref[...] + (1.0 - ADAM_B1) * gv
        nv = ADAM_B2 * v_ref[...] + (1.0 - ADAM_B2) * (gv * gv)
        m_hat = nm / (1.0 - ADAM_B1 ** ADAM_STEP)
        v_hat = nv / (1.0 - ADAM_B2 ** ADAM_STEP)
        d_ref[...] = -ADAM_LR * (m_hat / (jnp.sqrt(v_hat) + ADAM_EPS) + ADAM_WD * w_ref[...])
        nm_ref[...] = nm
        nv_ref[...] = nv

    blk = pl.BlockSpec((tr, c), lambda i: (i, 0))
    return pl.pallas_call(
        body, name=name, grid=(r // tr,),
        in_specs=[blk] * 4, out_specs=[blk] * 3,
        out_shape=[jax.ShapeDtypeStruct((r, c), F32)] * 3,
        compiler_params=_params(("parallel",)),
    )(w, g, m, v)


def _add_half_bf16(g, b, place, name):
    n, half, c = b.shape
    tr = _tile(half, 512, 16)
    nb = half // tr

    def body(place_ref, g_ref, b_ref, o_ref):
        o_ref[...] = (g_ref[...].astype(F32) + b_ref[...].astype(F32)).astype(BF16)

    blk = pl.BlockSpec((1, tr, c), lambda i, j, p: (i, j, 0))
    return pl.pallas_call(
        body, name=name,
        grid_spec=pltpu.PrefetchScalarGridSpec(
            num_scalar_prefetch=1, grid=(n, nb),
            in_specs=[pl.BlockSpec((1, tr, c), lambda i, j, p: (i, p[0] * nb + j, 0)), blk], out_specs=blk),
        out_shape=jax.ShapeDtypeStruct((n, half, c), BF16),
        compiler_params=_params(("parallel", "parallel")),
    )(place, g, b)


def _sum4_f32(mine, others, place, name):
    _, half, c = mine.shape
    tr = _tile(half, 512, 16)
    nb = half // tr

    def body(place_ref, a_ref, b_ref, o_ref):
        acc = a_ref[0].astype(F32)
        for j in range(3):
            acc = acc + b_ref[j].astype(F32)
        o_ref[...] = acc

    return pl.pallas_call(
        body, name=name,
        grid_spec=pltpu.PrefetchScalarGridSpec(
            num_scalar_prefetch=1, grid=(nb,),
            in_specs=[pl.BlockSpec((1, tr, c), lambda i, p: (p[1], i, 0)),
                      pl.BlockSpec((3, tr, c), lambda i, p: (0, i, 0))],
            out_specs=pl.BlockSpec((tr, c), lambda i, p: (p[0] * nb + i, 0))),
        out_shape=jax.ShapeDtypeStruct((2 * half, c), F32),
        compiler_params=_params(("parallel",)),
    )(place, mine, others)


def _place():
    x, y, c = lax.axis_index("x"), lax.axis_index("y"), lax.axis_index("c")
    other_chips = [(1 - x, y), (x, 1 - y), (1 - x, 1 - y)]
    return x, y, c, (x, y, 1 - c), other_chips


ANY = pl.BlockSpec(memory_space=pl.ANY)


def _remote(k, src, dst, to, send_sems, recv_sems):
    return pltpu.make_async_remote_copy(src_ref=src, dst_ref=dst, send_sem=send_sems.at[k], recv_sem=recv_sems.at[k],
                                        device_id=to, device_id_type=MESH)


def _half(ref, lead, hc):
    if lead is None:
        half = ref.shape[0] // 2
        return ref.at[pl.ds(hc * half, half), :]
    half = ref.shape[1] // 2
    return ref.at[lead, pl.ds(hc * half, half), :]


def _all_gather_weights(slots, whole, name):
    nt, nw = len(slots), len(whole)
    base_w, base_f, base_d = 2 * nt, 2 * nt + 3 * nw, 4 * nt + 3 * nw

    def quarter(ref, lead, hc, q):
        quart = ref.shape[1] // 4
        return ref.at[lead, pl.ds((2 * hc + q) * quart, quart), :]

    def body(*refs):
        ins, outs = refs[:nt + nw], refs[nt + nw:2 * (nt + nw)]
        sems = refs[2 * (nt + nw):]
        x, y, c, sibling, chips = _place()
        me, xn, yn, dg = 2 * x + y, 2 * (1 - x) + y, 2 * x + 1 - y, 2 * (1 - x) + 1 - y
        to_x, to_y = (1 - x, y, c), (x, 1 - y, c)
        cps = []
        for t in range(nt):
            cps.append(_remote(2 * t, _half(ins[t], me, c), _half(outs[t], me, c), to_x, *sems))
            cps.append(_remote(2 * t + 1, _half(ins[t], me, c), _half(outs[t], me, c), to_y, *sems))
        for j, (px, py) in enumerate(chips):
            for t in range(nw):
                cps.append(_remote(base_w + j * nw + t, ins[nt + t].at[me], outs[nt + t].at[me], (px, py, c), *sems))
        for cp in cps:
            cp.start()

        def start(k, ref, to):
            cp = _remote(k, ref, ref, to, *sems)
            cp.start()
            cps.append(cp)

        for t in range(nt):
            landed = _half(outs[t], xn, c)
            _remote(2 * t, landed, landed, to_x, *sems).wait_recv()
            start(base_f + 2 * t, quarter(outs[t], xn, c, 0), to_y)
            start(base_d + 3 * t, landed, sibling)
            landed = _half(outs[t], yn, c)
            _remote(2 * t + 1, landed, landed, to_y, *sems).wait_recv()
            start(base_f + 2 * t + 1, quarter(outs[t], yn, c, 1), to_x)
            start(base_d + 3 * t + 1, landed, sibling)
        for t in range(nt):
            q0, q1 = quarter(outs[t], dg, c, 0), quarter(outs[t], dg, c, 1)
            _remote(base_f + 2 * t, q0, q0, to_y, *sems).wait_recv()
            _remote(base_f + 2 * t + 1, q1, q1, to_x, *sems).wait_recv()
            start(base_d + 3 * t + 2, _half(outs[t], dg, c), sibling)
        for j, (px, py) in enumerate(chips):
            for t in range(nw):
                landed = outs[nt + t].at[2 * px + py]
                _remote(base_w + j * nw + t, landed, landed, (px, py, c), *sems).wait_recv()
        for t in range(nt):
            for j, chip in enumerate((xn, yn, dg)):
                other = _half(outs[t], chip, 1 - c)
                _remote(base_d + 3 * t + j, other, other, sibling, *sems).wait_recv()
        for cp in cps:
            cp.wait_send()

    arrays = list(slots) + list(whole)
    n_sem = 7 * nt + 3 * nw
    return pl.pallas_call(
        body, name=name, in_specs=[ANY] * len(arrays), out_specs=[ANY] * len(arrays),
        input_output_aliases={i: i for i in range(len(arrays))},
        out_shape=[jax.ShapeDtypeStruct(a.shape, a.dtype) for a in arrays],
        scratch_shapes=[pltpu.SemaphoreType.DMA((n_sem,)), pltpu.SemaphoreType.DMA((n_sem,))],
    )(*arrays)


def _join_halves(fs, name):
    nt = len(fs)

    def body(*refs):
        in_refs, out_refs, (send_sems, recv_sems) = refs[:nt], refs[nt:2 * nt], refs[2 * nt:]
        _, _, c, sibling, _ = _place()
        cps = [_remote(t, _half(in_refs[t], None, c), _half(out_refs[t], None, c), sibling, send_sems, recv_sems)
               for t in range(nt)]
        for cp in cps:
            cp.start()
        for t in range(nt):
            theirs = _half(out_refs[t], None, 1 - c)
            _remote(t, theirs, theirs, sibling, send_sems, recv_sems).wait_recv()
        for cp in cps:
            cp.wait_send()

    return pl.pallas_call(
        body, name=name, in_specs=[ANY] * nt, out_specs=[ANY] * nt,
        input_output_aliases={i: i for i in range(nt)},
        out_shape=[jax.ShapeDtypeStruct(f.shape, f.dtype) for f in fs],
        scratch_shapes=[pltpu.SemaphoreType.DMA((nt,)), pltpu.SemaphoreType.DMA((nt,))],
    )(*fs)


HBM = pl.BlockSpec(memory_space=pltpu.HBM)
SEM = pl.BlockSpec(memory_space=pltpu.SEMAPHORE)
EFFECT = pltpu.SideEffectType.DATAFLOW_SIDE_EFFECTING


def _split_start(arrays, after, plan, n_copies, name):
    na = len(arrays)

    def body(*refs):
        ins, send_sems, recv_sems = refs[:na], refs[na + 1], refs[na + 2]
        outs, token = refs[na + 3:2 * na + 3], refs[2 * na + 3]
        for k, (src, dst, to) in enumerate(plan(ins, outs)):
            _remote(k, src, dst, to, send_sems, recv_sems).start()
        token[...] = jnp.zeros_like(token)

    res = pl.pallas_call(
        body, name=name,
        out_shape=(pltpu.SemaphoreType.DMA((n_copies,)), pltpu.SemaphoreType.DMA((n_copies,)),
                   *[pltpu.HBM(a.shape, a.dtype) for a in arrays], jax.ShapeDtypeStruct((8, HEAD_DIM), F32)),
        in_specs=[HBM] * na + [ANY],
        out_specs=(SEM, SEM, *[HBM] * na, pl.BlockSpec(memory_space=pltpu.VMEM)),
        input_output_aliases={i: 2 + i for i in range(na)},
        compiler_params=pltpu.CompilerParams(has_side_effects=EFFECT),
    )(*[pltpu.with_memory_space_constraint(a, pltpu.HBM) for a in arrays], after)
    return res[0], res[1], list(res[2:2 + na]), res[2 + na]


def _split_wait(send_sems, recv_sems, arrays, after, plan, name):
    na = len(arrays)
    after = list(after) if isinstance(after, (list, tuple)) else [after]

    def body(*refs):
        ins, send, recv, outs = refs[:na], refs[na], refs[na + 1], refs[na + 2 + len(after):]
        for k, (src, dst, to) in enumerate(plan(ins, outs)):
            cp = _remote(k, src, dst, to, send, recv)
            cp.wait_send()
            cp.wait_recv()

    res = pl.pallas_call(
        body, name=name, out_shape=tuple(pltpu.HBM(a.shape, a.dtype) for a in arrays),
        in_specs=[HBM] * na + [SEM, SEM] + [ANY] * len(after), out_specs=tuple([HBM] * na),
        input_output_aliases={i: i for i in range(na)},
        compiler_params=pltpu.CompilerParams(has_side_effects=EFFECT),
    )(*arrays, send_sems, recv_sems, *after)
    return list(res)


def _gather_plan(nt):
    def plan(ins, outs):
        x, y, c, _, chips = _place()
        me = 2 * x + y
        return [(_half(ins[t], me, c), _half(outs[t], me, c), (px, py, c)) for px, py in chips for t in range(nt)]
    return plan


def _gather_landed_plan(nt):
    def plan(ins, outs):
        _, _, c, _, chips = _place()
        return [(_half(outs[t], 2 * px + py, c), _half(outs[t], 2 * px + py, c), (px, py, c))
                for px, py in chips for t in range(nt)]
    return plan


def _scatter_plan(nt):
    def plan(ins, outs):
        _, _, c, _, chips = _place()
        return [(ins[t].at[2 * px + py], outs[nt + t].at[j], (px, py, c))
                for j, (px, py) in enumerate(chips) for t in range(nt)]
    return plan


def _pass_plan(nt):
    def plan(ins, outs):
        _, _, c, sibling, chips = _place()
        return [(_half(ins[t], 2 * px + py, c), _half(outs[t], 2 * px + py, c), sibling)
                for px, py in chips for t in range(nt)]
    return plan


def _pass_landed_plan(nt):
    def plan(ins, outs):
        _, _, c, sibling, chips = _place()
        return [(_half(outs[t], 2 * px + py, c), _half(outs[t], 2 * px + py, 1 - c), sibling)
                for px, py in chips for t in range(nt)]
    return plan


def _swap_plan(nt):
    def plan(ins, outs):
        _, _, c, sibling, _ = _place()
        res = []
        for t in range(nt):
            half = ins[t].shape[1] // 2
            res.append((ins[t].at[:, pl.ds((1 - c) * half, half), :], outs[nt + t], sibling))
        return res
    return plan


def _pass_halves_to_sibling(slots, name):
    nt = len(slots)

    def body(*refs):
        ins, outs, (send_sems, recv_sems) = refs[:nt], refs[nt:2 * nt], refs[2 * nt:]
        _, _, c, sibling, chips = _place()
        cps = [_remote(j * nt + t, _half(ins[t], 2 * px + py, c), _half(outs[t], 2 * px + py, c), sibling,
                       send_sems, recv_sems)
               for j, (px, py) in enumerate(chips) for t in range(nt)]
        for cp in cps:
            cp.start()
        for j, (px, py) in enumerate(chips):
            for t in range(nt):
                other = _half(outs[t], 2 * px + py, 1 - c)
                _remote(j * nt + t, other, other, sibling, send_sems, recv_sems).wait_recv()
        for cp in cps:
            cp.wait_send()

    return pl.pallas_call(
        body, name=name, in_specs=[ANY] * nt, out_specs=[ANY] * nt,
        input_output_aliases={i: i for i in range(nt)},
        out_shape=[jax.ShapeDtypeStruct(a.shape, a.dtype) for a in slots],
        scratch_shapes=[pltpu.SemaphoreType.DMA((3 * nt,)), pltpu.SemaphoreType.DMA((3 * nt,))],
    )(*slots)


def _all_reduce_small(v):
    r, lanes = v.shape

    def body(v_ref, out_ref, buf, send_sems, recv_sems):
        x, y, c, sibling, chips = _place()

        def slot(px, py, pc):
            return buf.at[4 * px + 2 * py + pc]

        def copy(k, block, to, src=None):
            return pltpu.make_async_remote_copy(src_ref=slot(*block) if src is None else src, dst_ref=slot(*block),
                                                send_sem=send_sems.at[k], recv_sem=recv_sems.at[k],
                                                device_id=to, device_id_type=MESH)

        me = (x, y, c)
        buf[4 * x + 2 * y + c] = v_ref[...]
        first = [copy(0, me, sibling, src=v_ref)]
        first += [copy(1 + j, me, (*chip, c), src=v_ref) for j, chip in enumerate(chips)]
        for cp in first:
            cp.start()
        passed = [copy(4 + j, (*chip, c), sibling) for j, chip in enumerate(chips)]
        for j, chip in enumerate(chips):
            copy(1 + j, (*chip, c), me).wait_recv()
            passed[j].start()
        copy(0, (x, y, 1 - c), me).wait_recv()
        for j, chip in enumerate(chips):
            copy(4 + j, (*chip, 1 - c), me).wait_recv()
        for cp in first + passed:
            cp.wait_send()
        acc = buf[0]
        for k in range(1, 8):
            acc = acc + buf[k]
        out_ref[...] = acc

    vmem = pl.BlockSpec(memory_space=pltpu.VMEM)
    return pl.pallas_call(
        body, name="all_reduce_small", in_specs=[vmem], out_specs=vmem,
        out_shape=jax.ShapeDtypeStruct((r, lanes), F32),
        scratch_shapes=[pltpu.VMEM((8, r, lanes), F32), pltpu.SemaphoreType.DMA((7,)), pltpu.SemaphoreType.DMA((7,))],
    )(v)


def _size(shape):
    n = 1
    for d in shape:
        n *= d
    return n


def _regroup_cols(parts, widths):
    out, start = [], 0
    for width in widths:
        pieces, lo = [], 0
        for p in parts:
            a, b = max(start, lo), min(start + width, lo + p.shape[1])
            if a < b:
                pieces.append(p[:, a - lo:b - lo])
            lo += p.shape[1]
        out.append(pieces[0] if len(pieces) == 1 else jnp.concatenate(pieces, axis=1))
        start += width
    return out


def _pack_small(parts):
    rows = []
    for p in parts:
        f = p.reshape(-1).astype(F32)
        n = -(-f.shape[0] // HEAD_DIM) * HEAD_DIM
        rows.append(jnp.pad(f, (0, n - f.shape[0])).reshape(-1, HEAD_DIM))
    a = jnp.concatenate(rows, axis=0)
    return jnp.pad(a, ((0, -a.shape[0] % 8), (0, 0)))


def _unpack_small(a, shapes):
    out, row = [], 0
    for shp in shapes:
        nrows = -(-_size(shp) // HEAD_DIM)
        out.append(a[row:row + nrows].reshape(-1)[:_size(shp)].reshape(shp))
        row += nrows
    return out


SMALL = ["attn_norm_w", "a_log", "dt_bias", "delta_out_norm_w", "q_norm_w", "k_norm_w", "attn_out_norm_w", "ffn_norm_w"]
BIG = ["w_in", "w_out", "w_gate_up", "w_down"]
ORDER = ["attn_norm_w", "w_in", "conv_w", "a_log", "dt_bias", "delta_out_norm_w", "q_norm_w", "k_norm_w",
         "attn_out_norm_w", "w_out", "ffn_norm_w", "w_gate_up", "w_down"]


def kernel(x, positions, attn_norm_w, w_in, conv_w, a_log, dt_bias, delta_out_norm_w, q_norm_w, k_norm_w, attn_out_norm_w, w_out, ffn_norm_w, w_gate_up, w_down, loss_target, m_attn_norm_w, m_w_in, m_conv_w, m_a_log, m_dt_bias, m_delta_out_norm_w, m_q_norm_w, m_k_norm_w, m_attn_out_norm_w, m_w_out, m_ffn_norm_w, m_w_gate_up, m_w_down, v_attn_norm_w, v_w_in, v_conv_w, v_a_log, v_dt_bias, v_delta_out_norm_w, v_q_norm_w, v_k_norm_w, v_attn_out_norm_w, v_w_out, v_ffn_norm_w, v_w_gate_up, v_w_down):
    wts = dict(attn_norm_w=attn_norm_w, w_in=w_in, conv_w=conv_w, a_log=a_log, dt_bias=dt_bias,
               delta_out_norm_w=delta_out_norm_w, q_norm_w=q_norm_w, k_norm_w=k_norm_w,
               attn_out_norm_w=attn_out_norm_w, w_out=w_out, ffn_norm_w=ffn_norm_w, w_gate_up=w_gate_up, w_down=w_down)
    mom = dict(attn_norm_w=m_attn_norm_w, w_in=m_w_in, conv_w=m_conv_w, a_log=m_a_log, dt_bias=m_dt_bias,
               delta_out_norm_w=m_delta_out_norm_w, q_norm_w=m_q_norm_w, k_norm_w=m_k_norm_w,
               attn_out_norm_w=m_attn_out_norm_w, w_out=m_w_out, ffn_norm_w=m_ffn_norm_w, w_gate_up=m_w_gate_up,
               w_down=m_w_down)
    var = dict(attn_norm_w=v_attn_norm_w, w_in=v_w_in, conv_w=v_conv_w, a_log=v_a_log, dt_bias=v_dt_bias,
               delta_out_norm_w=v_delta_out_norm_w, q_norm_w=v_q_norm_w, k_norm_w=v_k_norm_w,
               attn_out_norm_w=v_attn_out_norm_w, w_out=v_w_out, ffn_norm_w=v_ffn_norm_w, w_gate_up=v_w_gate_up,
               w_down=v_w_down)
    dmod = x.shape[2]
    heads = dmod // (2 * HEAD_DIM)
    dw = heads * HEAD_DIM
    chip = 2 * lax.axis_index("x") + lax.axis_index("y")
    core = lax.axis_index("c")
    n_in, n_out, n_gu, n_down, n_conv = (w_in.shape[2], w_out.shape[1], w_gate_up.shape[2], w_down.shape[1],
                                         conv_w.shape[2])

    def slots_of(w, dtype):
        shard = w[0].astype(dtype)
        return lax.dynamic_update_index_in_dim(lax.empty((4,) + shard.shape, dtype), shard, chip, axis=0)

    s_in, s_conv = _all_gather_weights([slots_of(w_in, BF16)], [slots_of(conv_w, F32)], "all_gather_w_in")
    later = [slots_of(w_out, BF16), slots_of(w_gate_up, BF16), slots_of(w_down, BF16)]
    w_send, w_recv, later, started = _split_start(later, s_conv, _gather_plan(3), 9, "gather_rest_start")
    by_cols = lambda a: a.transpose(1, 0, 2).reshape(a.shape[1], 4 * a.shape[2])
    conv_f = by_cols(s_conv)
    w_qkvz, w_bd, w_attn = _regroup_cols([s_in[i] for i in range(4)], [4 * dw, 2 * heads, 3 * dw])
    w_bd = jnp.pad(w_bd, ((0, 0), (0, HEAD_DIM - 2 * heads)))
    small = {n: wts[n] for n in SMALL}
    place = jnp.stack([core, chip]).astype(jnp.int32)
    to_slots = lambda a: a.reshape(a.shape[0], 4, a.shape[1] // 4).transpose(1, 0, 2)

    def later_weights(after):
        landed = _split_wait(w_send, w_recv, later, after, _gather_landed_plan(3), "gather_rest_wait")
        s_out, = _pass_halves_to_sibling(landed[:1], "gather_out_pass")
        p_send, p_recv, passing, token = _split_start(landed[1:], s_out, _pass_plan(2), 6, "gather_ffn_pass_start")

        def ffn_weights(after):
            s_gu, s_down = _split_wait(p_send, p_recv, passing, after, _pass_landed_plan(2), "gather_ffn_pass_wait")
            return by_cols(s_gu), s_down.reshape(4 * n_down, dmod)

        return s_out.reshape(4 * n_out, dmod), token, ffn_weights

    ffn = {}

    def ffn_grads_ready(g_gate, g_up, g_down):
        gs = [to_slots(jnp.concatenate([g_gate, g_up], axis=1)), g_down.reshape(4, n_down, dmod)]
        zones = [lax.empty((4, g.shape[1] // 2, g.shape[2]), BF16) for g in gs]
        s_send, s_recv, s_bufs, swapping = _split_start(gs + zones, g_gate, _swap_plan(2), 2, "swap_ffn_start")

        def and_then(after):
            g_gu, g_dn, b_gu, b_dn = _split_wait(s_send, s_recv, s_bufs, after, _swap_plan(2), "swap_ffn_wait")
            sums = [_add_half_bf16(g_gu, b_gu, place, "chip_partial_sum_w_gate_up"),
                    _add_half_bf16(g_dn, b_dn, place, "chip_partial_sum_w_down")]
            zones3 = [lax.empty((3,) + p.shape[1:], BF16) for p in sums]
            ffn["send"], ffn["recv"], ffn["bufs"], token = _split_start(sums + zones3, b_gu, _scatter_plan(2), 6,
                                                                        "scatter_ffn_start")
            return token

        return swapping, and_then

    rest = {}

    def rest_grads_ready(bg):
        gs = [jnp.stack(_regroup_cols([bg["w_qkv"], bg["w_z"], bg["w_bd"][:, :2 * heads], bg["w_attn"]], [n_in] * 4)),
              jnp.concatenate([bg["w_out_a"], bg["w_out_b"]], axis=0).reshape(4, n_out, dmod)]
        zones = [lax.empty((4, g.shape[1] // 2, g.shape[2]), BF16) for g in gs]
        s_send, s_recv, s_bufs, swapping = _split_start(gs + zones, bg["w_attn"], _swap_plan(2), 2, "swap_rest_start")

        def and_then(after):
            g_in, g_out, b_in, b_out = _split_wait(s_send, s_recv, s_bufs, after, _swap_plan(2), "swap_rest_wait")
            sums = [_add_half_bf16(g_in, b_in, place, "chip_partial_sum_w_in"),
                    _add_half_bf16(g_out, b_out, place, "chip_partial_sum_w_out")]
            zones3 = [lax.empty((3,) + p.shape[1:], BF16) for p in sums]
            rest["send"], rest["recv"], rest["bufs"], token = _split_start(sums + zones3, b_in, _scatter_plan(2), 6,
                                                                           "scatter_rest_start")
            return token

        return swapping, and_then

    loss_row, grad_x, sg = _local_step(
        x[0], positions[0], loss_target[0], small, w_qkvz, w_bd, w_attn, conv_f,
        later_weights, ffn_grads_ready, rest_grads_ready, after=started)

    r_send, r_recv, r_bufs = rest["send"], rest["recv"], rest["bufs"]
    sum_gu, sum_down, got_gu, got_down = _split_wait(ffn["send"], ffn["recv"], ffn["bufs"], grad_x,
                                                     _scatter_plan(2), "scatter_ffn_wait")
    g_big = dict(zip(["w_gate_up", "w_down"], _join_halves(
        [_sum4_f32(sum_gu, got_gu, place, "grad_total_w_gate_up"),
         _sum4_f32(sum_down, got_down, place, "grad_total_w_down")], "join_ffn_halves")))
    grads, deltas, new_m, new_v = {}, {}, {}, {}

    def adamw_big(n):
        shp = wts[n].shape
        d, nm, nv = _adamw(wts[n][0], g_big[n], mom[n][0], var[n][0], "adamw_" + n)
        grads[n], deltas[n], new_m[n], new_v[n] = g_big[n].reshape(shp), d.reshape(shp), nm.reshape(shp), nv.reshape(shp)
        return d

    done = [adamw_big("w_gate_up"), adamw_big("w_down")]
    sum_in, sum_out, got_in, got_out = _split_wait(r_send, r_recv, r_bufs, done, _scatter_plan(2), "scatter_rest_wait")
    g_big.update(zip(["w_in", "w_out"], _join_halves(
        [_sum4_f32(sum_in, got_in, place, "grad_total_w_in"),
         _sum4_f32(sum_out, got_out, place, "grad_total_w_out")], "join_rest_halves")))
    adamw_big("w_in")
    adamw_big("w_out")

    reduced = _all_reduce_small(_pack_small([sg[n] for n in SMALL] + [sg["conv_w"], loss_row]))
    red = _unpack_small(reduced, [wts[n].shape for n in SMALL] + [(4, 4 * n_conv), (1, HEAD_DIM)])
    g_small = dict(zip(SMALL, red[:len(SMALL)]))
    g_conv_full, loss_out = red[len(SMALL)], red[len(SMALL) + 1]
    g_small["conv_w"] = lax.dynamic_slice_in_dim(g_conv_full, chip * n_conv, n_conv, axis=1).reshape(conv_w.shape)

    names = SMALL + ["conv_w"]
    shapes = [wts[n].shape for n in names]
    d, nm, nv = _adamw(_pack_small([wts[n] for n in names]), _pack_small([g_small[n] for n in names]),
                       _pack_small([mom[n] for n in names]), _pack_small([var[n] for n in names]), "adamw_small")
    for n, dd, mm, vv in zip(names, _unpack_small(d, shapes), _unpack_small(nm, shapes), _unpack_small(nv, shapes)):
        grads[n], deltas[n], new_m[n], new_v[n] = g_small[n], dd, mm, vv
    return (loss_out[0, 0], grad_x[None], *[grads[n] for n in ORDER], *[deltas[n] for n in ORDER],
            *[new_m[n] for n in ORDER], *[new_v[n] for n in ORDER])
```

```python
import jax
import jax.numpy as jnp
from jax import lax
from jax.experimental import pallas as pl
from jax.experimental.pallas import tpu as pltpu

F32 = jnp.float32
BF16 = jnp.bfloat16
HEAD_DIM = 128
CHUNK = 128
INV_BLOCK = 64
SPAN = 128
DILATIONS = (1, 4, 16)
ROPE_THETA = 10000.0
EPS = 1e-6
NEG = -1e30
ADAM_LR, ADAM_B1, ADAM_B2, ADAM_EPS, ADAM_WD, ADAM_STEP = 0.001, 0.9, 0.999, 1e-08, 0.01, 10
VMEM_LIMIT = 48 * 1024 * 1024
MATMUL_VMEM = 44 * 1024 * 1024
MESH = pl.DeviceIdType.MESH

_DN = {"nn": (((1,), (0,)), ((), ())), "nt": (((1,), (1,)), ((), ())), "tn": (((0,), (0,)), ((), ()))}


def _dot(a, b, mode="nn"):
    (ca, cb), _ = _DN[mode]
    if a.ndim == 3:
        dn = (((ca[0] + 1,), (cb[0] + 1,)), ((0,), (0,)))
    else:
        dn = _DN[mode]
    return lax.dot_general(a.astype(BF16), b.astype(BF16), dn, preferred_element_type=F32)


def _rsum(x):
    return jnp.sum(x, axis=-1, keepdims=True)


def _csum(x):
    return jnp.sum(x, axis=-2, keepdims=True)


def _tile(dim, pref, unit=128):
    t = (min(pref, dim) // unit) * unit
    while t >= unit:
        if dim % t == 0:
            return t
        t -= unit
    return dim


def _params(sem):
    return pltpu.CompilerParams(dimension_semantics=sem, vmem_limit_bytes=VMEM_LIMIT)


def _sigmoid(x):
    return 1.0 / (1.0 + jnp.exp(-x))


def _matmul(a, b, mode, name, add=None, out_dtype=F32, a_cols=None, b_cols=None, b_rows=None,
            tiles=(), finish=None, out_dtypes=(), row_sum=False, after=None, b2_cols=None, more=()):
    if mode == "tn":
        out_dtype = BF16
    a_off, a_w = a_cols if a_cols else (0, a.shape[1])
    b_off, b_w = b_cols if b_cols else (0, b.shape[1])
    br_off, br_n = b_rows if b_rows else (0, b.shape[0])
    if mode == "nn":
        m, k, n = a.shape[0], a_w, b_w
        assert br_n == k
    elif mode == "nt":
        m, k, n = a.shape[0], a_w, br_n
        assert b_w == k
    else:
        k, m, n = a.shape[0], a_w, b_w
        assert b.shape[0] == k
    if finish is None:
        out_dtypes = [out_dtype]
        if add is None:
            finish = lambda acc, vals: (acc,)
        else:
            tiles = [(add, 0)]
            finish = lambda acc, vals: (acc + vals[0].astype(F32),)
    n_b = 1 if b2_cols is None else 2
    sides = sum(jnp.dtype(t.dtype).itemsize for t, _ in tiles) + sum(jnp.dtype(d).itemsize for d in out_dtypes)

    def vmem(tm_, tn_, tk_):
        ops = tm_ * tk_ * jnp.dtype(a.dtype).itemsize + n_b * tk_ * tn_ * jnp.dtype(b.dtype).itemsize
        for a_i, b_i, (_, k_i) in more:
            ops += tm_ * k_i * jnp.dtype(a_i.dtype).itemsize + k_i * tn_ * jnp.dtype(b_i.dtype).itemsize
        return 2 * (ops + tm_ * tn_ * sides) + (n_b * tm_ * tn_ * 4 if tk_ < k else 0)

    tm, tn = _tile(m, 1024, 128), _tile(n, 1024, 128)
    tall = _tile(m, 1536, 128), _tile(n, 512, 128)
    if tall[0] * tall[1] > tm * tn and vmem(*tall, k) <= MATMUL_VMEM:
        tm, tn = tall
    while more and tn > 128 and vmem(tm, tn, k) > MATMUL_VMEM * 3 // 4:
        tn //= 2
    tk = next(t for t in [_tile(k, p, 128) for p in (4096, 2048, 1024, 512, 128)] if vmem(tm, tn, t) <= MATMUL_VMEM)
    if mode == "nn":
        assert a_off % tk == 0 and b_off % tn == 0 and br_off % tk == 0
        a_spec = pl.BlockSpec((tm, tk), lambda i, j, kk: (i, kk + a_off // tk))
        b_spec = pl.BlockSpec((tk, tn), lambda i, j, kk: (kk + br_off // tk, j + b_off // tn))
    elif mode == "nt":
        assert a_off % tk == 0 and b_off % tk == 0 and br_off % tn == 0
        a_spec = pl.BlockSpec((tm, tk), lambda i, j, kk: (i, kk + a_off // tk))
        b_spec = pl.BlockSpec((tn, tk), lambda i, j, kk: (j + br_off // tn, kk + b_off // tk))
    else:
        assert a_off % tm == 0 and b_off % tn == 0
        a_spec = pl.BlockSpec((tk, tm), lambda i, j, kk: (kk, i + a_off // tm))
        b_spec = pl.BlockSpec((tk, tn), lambda i, j, kk: (kk, j + b_off // tn))
    nk = k // tk
    assert not more or (nk == 1 and n_b == 1 and mode in ("nn", "nt"))
    n_tiles, n_out = len(tiles), len(out_dtypes)
    extra = [] if after is None else [after]
    first_tile_ref = 1 + n_b + 2 * len(more)
    first_out = first_tile_ref + n_tiles + len(extra)

    def body(*refs):
        a_ref, b_refs = refs[0], refs[1:1 + n_b]
        tile_refs, out_refs = refs[first_tile_ref:first_tile_ref + n_tiles], refs[first_out:first_out + n_out]
        acc_refs = refs[-n_b:] if nk > 1 else ()
        kk = pl.program_id(2)
        first_tile = (pl.program_id(0) == 0) & (pl.program_id(1) == 0)

        av = a_ref[...]
        if nk > 1:
            @pl.when(kk == 0)
            def _():
                for acc_ref in acc_refs:
                    acc_ref[...] = jnp.zeros_like(acc_ref)

            for acc_ref, b_ref in zip(acc_refs, b_refs):
                acc_ref[...] += _dot(av, b_ref[...], mode)

        @pl.when(kk == nk - 1)
        def _():
            if nk > 1:
                accs = [acc_ref[...] for acc_ref in acc_refs]
            else:
                accs = [_dot(av, b_ref[...], mode) for b_ref in b_refs]
                for p_i in range(len(more)):
                    accs[0] = accs[0] + _dot(refs[1 + n_b + 2 * p_i][...], refs[2 + n_b + 2 * p_i][...], mode)
            res = finish(accs[0] if n_b == 1 else accs, [t[...] for t in tile_refs])
            for o_ref, r in zip(out_refs, res):
                o_ref[...] = r.astype(o_ref.dtype)
            if row_sum:
                row_ref = refs[first_out + n_out]

                @pl.when(first_tile)
                def _():
                    row_ref[...] = res[n_out]

                @pl.when(jnp.logical_not(first_tile))
                def _():
                    row_ref[...] += res[n_out]

    in_specs = [a_spec, b_spec]
    args = [a, b]
    if b2_cols is not None:
        assert mode == "nn" and b2_cols[1] == n and b2_cols[0] % tn == 0
        in_specs.append(pl.BlockSpec((tk, tn), lambda i, j, kk: (kk + br_off // tk, j + b2_cols[0] // tn)))
        args.append(b)
    for a_i, b_i, (start, k_i) in more:
        assert a_i.shape == (m, k_i) and start % k_i == 0
        in_specs.append(pl.BlockSpec((tm, k_i), lambda i, j, kk: (i, 0)))
        if mode == "nn":
            in_specs.append(pl.BlockSpec((k_i, tn), lambda i, j, kk, s_=start // k_i: (s_, j)))
        else:
            in_specs.append(pl.BlockSpec((tn, k_i), lambda i, j, kk, s_=start // k_i: (j, s_)))
        args += [a_i, b_i]
    for arr, off in tiles:
        assert off % tn == 0
        if arr.shape[0] == 1:
            in_specs.append(pl.BlockSpec((1, tn), lambda i, j, kk, off=off: (0, j + off // tn)))
        else:
            in_specs.append(pl.BlockSpec((tm, tn), lambda i, j, kk, off=off: (i, j + off // tn)))
        args.append(arr)
    in_specs += [ANY] * len(extra)
    args += extra
    out_specs = [pl.BlockSpec((tm, tn), lambda i, j, kk: (i, j))] * n_out
    out_shape = [jax.ShapeDtypeStruct((m, n), dt) for dt in out_dtypes]
    if row_sum:
        out_specs.append(pl.BlockSpec((1, HEAD_DIM), lambda i, j, kk: (0, 0)))
        out_shape.append(jax.ShapeDtypeStruct((1, HEAD_DIM), F32))
    res = pl.pallas_call(
        body, name=name, grid=(m // tm, n // tn, nk),
        in_specs=in_specs, out_specs=out_specs, out_shape=out_shape,
        scratch_shapes=[pltpu.VMEM((tm, tn), F32)] * (n_b if nk > 1 else 0),
        compiler_params=_params(("arbitrary",) * 3 if row_sum else ("parallel", "parallel", "arbitrary")),
    )(*args)
    return res[0] if len(res) == 1 else res


def _rmsnorm_fwd(x, w, name, after=None):
    s, d = x.shape
    tr = _tile(s, 1024, 8)

    def body(x_ref, w_ref, *rest):
        h_ref = rest[-1]
        xv = x_ref[...]
        rstd = lax.rsqrt(jnp.mean(xv * xv, axis=-1, keepdims=True) + EPS)
        h_ref[...] = (xv * rstd * w_ref[...]).astype(BF16)

    extra = [] if after is None else [after]
    return pl.pallas_call(
        body, name=name, grid=(s // tr,),
        in_specs=[pl.BlockSpec((tr, d), lambda i: (i, 0)), pl.BlockSpec((1, d), lambda i: (0, 0))] + [ANY] * len(extra),
        out_specs=pl.BlockSpec((tr, d), lambda i: (i, 0)),
        out_shape=jax.ShapeDtypeStruct((s, d), BF16),
        compiler_params=_params(("parallel",)),
    )(x, w, *extra)


def _rmsnorm_bwd(dh, x, w, res, name, after=None):
    s, d = x.shape
    tr = _tile(s, 256, 8)

    def body(dh_ref, x_ref, w_ref, res_ref, *rest):
        dx_ref, dx16_ref, dw_ref = rest[-3:]
        xv = x_ref[...]
        rstd = lax.rsqrt(jnp.mean(xv * xv, axis=-1, keepdims=True) + EPS)
        xhat = xv * rstd
        dhv = dh_ref[...]
        gw = dhv * w_ref[...]
        dx = res_ref[...] + rstd * (gw - xhat * jnp.mean(gw * xhat, axis=-1, keepdims=True))
        dx_ref[...] = dx
        dx16_ref[...] = dx.astype(BF16)

        @pl.when(pl.program_id(0) == 0)
        def _():
            dw_ref[...] = jnp.zeros_like(dw_ref)

        dw_ref[...] += jnp.sum(dhv * xhat, axis=0, keepdims=True)

    row = pl.BlockSpec((tr, d), lambda i: (i, 0))
    vec = pl.BlockSpec((1, d), lambda i: (0, 0))
    extra = [] if after is None else [after]
    return pl.pallas_call(
        body, name=name, grid=(s // tr,),
        in_specs=[row, row, vec, row] + [ANY] * len(extra), out_specs=[row, row, vec],
        out_shape=[jax.ShapeDtypeStruct((s, d), F32), jax.ShapeDtypeStruct((s, d), BF16),
                   jax.ShapeDtypeStruct((1, d), F32)],
        compiler_params=_params(("arbitrary",)),
    )(dh, x, w, res, *extra)


def _conv_taps(x, w, rows):
    shifted = [x]
    for sft in (1, 2, 3):
        shifted.append(jnp.where(rows >= sft, pltpu.roll(x, sft, 0), 0.0))
    y = w[3:4, :] * shifted[0] + w[2:3, :] * shifted[1] + w[1:2, :] * shifted[2] + w[0:1, :] * shifted[3]
    return y, shifted


def _delta_pre_fwd(qkvz, conv_w, heads):
    s = qkvz.shape[0]
    nblk = 3 * heads

    def body(x_ref, w_ref, o_ref):
        part = pl.program_id(0) // heads
        rows = lax.broadcasted_iota(jnp.int32, (s, HEAD_DIM), 0)
        y, _ = _conv_taps(x_ref[...], w_ref[...], rows)
        a = y * _sigmoid(y)
        rs = lax.rsqrt(jnp.sum(a * a, axis=-1, keepdims=True) + EPS)
        fac = jnp.where(part == 0, rs * (HEAD_DIM ** -0.5), jnp.where(part == 1, rs, 1.0))
        o_ref[...] = a * fac

    return pl.pallas_call(
        body, name="delta_pre_fwd", grid=(nblk,),
        in_specs=[pl.BlockSpec((s, HEAD_DIM), lambda i: (0, i)), pl.BlockSpec((4, HEAD_DIM), lambda i: (0, i))],
        out_specs=pl.BlockSpec((s, HEAD_DIM), lambda i: (0, i)),
        out_shape=jax.ShapeDtypeStruct((s, 3 * heads * HEAD_DIM), F32),
        compiler_params=_params(("parallel",)),
    )(qkvz, conv_w)


def _delta_pre_bwd(dqkv, qkvz, conv_w, heads):
    s = qkvz.shape[0]
    nblk = 3 * heads

    def body(d_ref, x_ref, w_ref, dx_ref, dw_ref):
        part = pl.program_id(0) // heads
        rows = lax.broadcasted_iota(jnp.int32, (s, HEAD_DIM), 0)
        w = w_ref[...]
        y, shifted = _conv_taps(x_ref[...], w, rows)
        sg = _sigmoid(y)
        a = y * sg
        rs = lax.rsqrt(jnp.sum(a * a, axis=-1, keepdims=True) + EPS)
        unit = a * rs
        dn = d_ref[...]
        scale = jnp.where(part == 0, HEAD_DIM ** -0.5, 1.0)
        da_norm = scale * rs * (dn - unit * jnp.sum(dn * unit, axis=-1, keepdims=True))
        da = jnp.where(part < 2, da_norm, dn)
        dy = da * sg * (1.0 + y * (1.0 - sg))
        dx = w[3:4, :] * dy
        for sft in (1, 2, 3):
            dx = dx + w[3 - sft:4 - sft, :] * jnp.where(rows < s - sft, pltpu.roll(dy, s - sft, 0), 0.0)
        dx_ref[...] = dx.astype(BF16)
        for sft in range(4):
            dw_ref[3 - sft:4 - sft, :] = jnp.sum(dy * shifted[sft], axis=0, keepdims=True)

    col = pl.BlockSpec((s, HEAD_DIM), lambda i: (0, i))
    wsp = pl.BlockSpec((4, HEAD_DIM), lambda i: (0, i))
    return pl.pallas_call(
        body, name="delta_pre_bwd", grid=(nblk,),
        in_specs=[col, col, wsp], out_specs=[col, wsp],
        out_shape=[jax.ShapeDtypeStruct((s, 3 * heads * HEAD_DIM), BF16),
                   jax.ShapeDtypeStruct((4, 3 * heads * HEAD_DIM), F32)],
        compiler_params=_params(("parallel",)),
    )(dqkv, qkvz, conv_w)


def _heads_of(ref, heads):
    return jnp.stack([ref[:, h * HEAD_DIM:(h + 1) * HEAD_DIM] for h in range(heads)])


def _chunk_common(q, k, v, bd, a_log, dt_bias, heads, solved=None):
    c = CHUNK
    braw = jnp.stack([bd[:, h:h + 1] for h in range(heads)])
    draw = jnp.stack([bd[:, heads + h:heads + h + 1] for h in range(heads)])
    beta = _sigmoid(braw)
    xd = draw + dt_bias
    sp = jnp.maximum(xd, 0.0) + jnp.log1p(jnp.exp(-jnp.abs(xd)))
    g = -jnp.exp(a_log) * sp
    row = lax.broadcasted_iota(jnp.int32, (c, c), 0)
    col = lax.broadcasted_iota(jnp.int32, (c, c), 1)
    sq = (heads, c, c)
    g_b = jnp.broadcast_to(g, sq)
    g_row = _csum(jnp.where(row == col, g_b, 0.0))
    gam_col = _rsum(jnp.where(col <= row, jnp.broadcast_to(g_row, sq), 0.0))
    gam_row = _csum(jnp.where(row <= col, g_b, 0.0))
    causal = row >= col
    dm = jnp.where(causal, jnp.exp(jnp.where(causal, gam_col - gam_row, 0.0)), 0.0)
    kk = _dot(k, k, "nt")
    e = jnp.exp(gam_col)
    if solved is None:
        low = jnp.where(row > col, beta * kk * dm, 0.0)
        assert c in (INV_BLOCK, 2 * INV_BLOCK)
        same = (row // INV_BLOCK) == (col // INV_BLOCK)
        diag = jnp.where(same, low, 0.0)
        t = jnp.where(row == col, 1.0, 0.0) - diag
        pw = diag
        for _ in range((INV_BLOCK - 1).bit_length() - 1):
            pw = _dot(pw, pw)
            t = t + _dot(t, pw)
        if c > INV_BLOCK:
            t = t - _dot(_dot(t, low - diag), t)
        u = _dot(t, beta * v)
        w = _dot(t, (beta * e) * k)
    else:
        t, u, w = solved
    qk_raw = _dot(q, k, "nt")
    gl = _csum(g)
    el = jnp.exp(gl - gam_col)
    return dict(beta=beta, xd=xd, g=g, row=row, col=col, dm=dm, kk=kk, t=t, e=e, u=u, w=w,
                qk_raw=qk_raw, qk=qk_raw * dm, gl=gl, el=el, qd=e * q, kd=el * k, cd=jnp.exp(gl))


def _delta_chunk_fwd(qkv, bd, a_log, dt_bias, heads):
    s = qkv.shape[0]
    n = s // CHUNK
    dw = heads * HEAD_DIM
    blk = lambda part: pl.BlockSpec((CHUNK, dw), lambda i: (i, part))

    def body(q_ref, k_ref, v_ref, bd_ref, al_ref, dt_ref, o_ref, st_ref, t_ref, uw_ref, state):
        @pl.when(pl.program_id(0) == 0)
        def _():
            state[...] = jnp.zeros_like(state)

        cm = _chunk_common(_heads_of(q_ref, heads), _heads_of(k_ref, heads), _heads_of(v_ref, heads), bd_ref[...],
                           al_ref[...], dt_ref[...], heads)
        st = state[...]
        st_ref[0] = st
        t_ref[0] = cm["t"]
        uw_ref[0, 0] = cm["u"]
        uw_ref[0, 1] = cm["w"]
        vn = cm["u"] - _dot(cm["w"], st)
        o = _dot(cm["qd"], st) + _dot(cm["qk"], vn)
        for h in range(heads):
            o_ref[:, h * HEAD_DIM:(h + 1) * HEAD_DIM] = o[h]
        state[...] = cm["cd"] * st + _dot(cm["kd"], vn, "tn")

    smem = pl.BlockSpec((heads, 1, 1), lambda i: (0, 0, 0))
    return pl.pallas_call(
        body, name="delta_chunk_fwd", grid=(n,),
        in_specs=[blk(0), blk(1), blk(2), pl.BlockSpec((CHUNK, HEAD_DIM), lambda i: (i, 0)), smem, smem],
        out_specs=[pl.BlockSpec((CHUNK, dw), lambda i: (i, 0)),
                   pl.BlockSpec((1, heads, HEAD_DIM, HEAD_DIM), lambda i: (i, 0, 0, 0)),
                   pl.BlockSpec((1, heads, CHUNK, CHUNK), lambda i: (i, 0, 0, 0)),
                   pl.BlockSpec((1, 2, heads, CHUNK, HEAD_DIM), lambda i: (i, 0, 0, 0, 0))],
        out_shape=[jax.ShapeDtypeStruct((s, dw), F32),
                   jax.ShapeDtypeStruct((n, heads, HEAD_DIM, HEAD_DIM), F32),
                   jax.ShapeDtypeStruct((n, heads, CHUNK, CHUNK), F32),
                   jax.ShapeDtypeStruct((n, 2, heads, CHUNK, HEAD_DIM), F32)],
        scratch_shapes=[pltpu.VMEM((heads, HEAD_DIM, HEAD_DIM), F32)],
        compiler_params=_params(("arbitrary",)),
    )(qkv, qkv, qkv, bd, a_log, dt_bias)


def _delta_chunk_bwd(do, qkv, bd, saved, a_log, dt_bias, heads):
    s = qkv.shape[0]
    n = s // CHUNK
    dw = heads * HEAD_DIM
    c = CHUNK
    blk = lambda part: pl.BlockSpec((CHUNK, dw), lambda i: (n - 1 - i, part))

    def all_heads(q, k, v, dov, st, solved, dsn, bd, a_log, dt_bias):
        cm = _chunk_common(q, k, v, bd, a_log, dt_bias, heads, solved)
        beta, e, dm, row, col = cm["beta"], cm["e"], cm["dm"], cm["row"], cm["col"]
        sq = (heads, c, c)
        vn = cm["u"] - _dot(cm["w"], st)
        dvn = _dot(cm["kd"], dsn)
        dkd = _dot(vn, dsn, "nt")
        dcd = _csum(_rsum(st * dsn))
        ds = cm["cd"] * dsn
        dqd = _dot(dov, st, "nt")
        ds = ds + _dot(cm["qd"], dov, "tn")
        dqk = _dot(dov, vn, "nt")
        dvn = dvn + _dot(cm["qk"], dov, "tn")
        dw_ = -_dot(dvn, st, "nt")
        ds = ds - _dot(cm["w"], dvn, "tn")
        drhs_u = _dot(cm["t"], dvn, "tn")
        drhs_w = _dot(cm["t"], dw_, "tn")
        da = -(_dot(drhs_u, cm["u"], "nt") + _dot(drhs_w, cm["w"], "nt"))
        dl = jnp.where(row > col, da, 0.0)
        dbeta = _rsum(dl * cm["kk"] * dm)
        dkk = dl * beta * dm
        dd = dl * beta * cm["kk"]
        dv = beta * drhs_u
        ek = e * k
        dbeta = dbeta + _rsum(drhs_u * v) + _rsum(drhs_w * ek)
        dk = (beta * e) * drhs_w
        dgam = _rsum(drhs_w * (beta * ek))
        dqkm = dqk * dm
        dq = _dot(dqkm, k)
        dk = dk + _dot(dqkm, q, "tn")
        dd = dd + dqk * cm["qk_raw"]
        dk = dk + _dot(dkk, k) + _dot(dkk, k, "tn")
        dq = dq + e * dqd
        dgam = dgam + _rsum(dqd * cm["qd"])
        dk = dk + cm["el"] * dkd
        r = _rsum(dkd * cm["kd"])
        dgam = dgam - r
        dgl = _csum(r) + dcd * cm["cd"]
        mm = dd * dm
        colsum_c = _rsum(jnp.where(row == col, jnp.broadcast_to(_csum(mm), sq), 0.0))
        dgam = dgam + _rsum(mm) - colsum_c
        ridx = lax.broadcasted_iota(jnp.int32, (c, 1), 0)
        dgam = dgam + jnp.where(ridx == c - 1, dgl, 0.0)
        dgam_row = _csum(jnp.where(row == col, jnp.broadcast_to(dgam, sq), 0.0))
        dg = _rsum(jnp.where(col >= row, jnp.broadcast_to(dgam_row, sq), 0.0))
        d_xd = dg * (-jnp.exp(a_log)) * _sigmoid(cm["xd"])
        d_braw = dbeta * beta * (1.0 - beta)
        d_alog = dg * cm["g"]
        lane = lax.broadcasted_iota(jnp.int32, (c, HEAD_DIM), 1)
        dbd = jnp.zeros((c, HEAD_DIM), F32)
        for h in range(heads):
            dbd = (dbd + jnp.where(lane == h, d_braw[h], 0.0) + jnp.where(lane == h + heads, d_xd[h], 0.0)
                   + jnp.where(lane == h + 2 * heads, d_alog[h], 0.0))
        return dq, dk, dv, ds, dbd

    def body(do_ref, q_ref, k_ref, v_ref, bd_ref, st_ref, t_ref, uw_ref, al_ref, dt_ref, dqkv_ref, dbd_ref, dstate):
        @pl.when(pl.program_id(0) == 0)
        def _():
            dstate[...] = jnp.zeros_like(dstate)

        dq, dk, dv, ds, dbd = all_heads(_heads_of(q_ref, heads), _heads_of(k_ref, heads), _heads_of(v_ref, heads),
                                        _heads_of(do_ref, heads), st_ref[0], (t_ref[0], uw_ref[0, 0], uw_ref[0, 1]),
                                        dstate[...], bd_ref[...],
                                        al_ref[...], dt_ref[...])
        for part, val in enumerate((dq, dk, dv)):
            for h in range(heads):
                lo = part * dw + h * HEAD_DIM
                dqkv_ref[:, lo:lo + HEAD_DIM] = val[h]
        dstate[...] = ds
        dbd_ref[...] = dbd

    smem = pl.BlockSpec((heads, 1, 1), lambda i: (0, 0, 0))
    shared = pl.BlockSpec((CHUNK, HEAD_DIM), lambda i: (n - 1 - i, 0))
    wide = pl.BlockSpec((CHUNK, dw), lambda i: (n - 1 - i, 0))
    return pl.pallas_call(
        body, name="delta_chunk_bwd", grid=(n,),
        in_specs=[wide, blk(0), blk(1), blk(2), shared,
                  pl.BlockSpec((1, heads, HEAD_DIM, HEAD_DIM), lambda i: (n - 1 - i, 0, 0, 0)),
                  pl.BlockSpec((1, heads, CHUNK, CHUNK), lambda i: (n - 1 - i, 0, 0, 0)),
                  pl.BlockSpec((1, 2, heads, CHUNK, HEAD_DIM), lambda i: (n - 1 - i, 0, 0, 0, 0)), smem, smem],
        out_specs=[pl.BlockSpec((CHUNK, 3 * dw), lambda i: (n - 1 - i, 0)), shared],
        out_shape=[jax.ShapeDtypeStruct((s, 3 * dw), F32), jax.ShapeDtypeStruct((s, HEAD_DIM), F32)],
        scratch_shapes=[pltpu.VMEM((heads, HEAD_DIM, HEAD_DIM), F32)],
        compiler_params=_params(("arbitrary",)),
    )(do, qkv, qkv, qkv, bd, *saved, a_log, dt_bias)


def _delta_post_fwd(o, qkvz, w, heads):
    s = o.shape[0]
    tr = _tile(s, 4096, 8)

    def body(o_ref, z_ref, w_ref, out_ref):
        ov, z = o_ref[...], z_ref[...]
        rstd = lax.rsqrt(jnp.mean(ov * ov, axis=-1, keepdims=True) + EPS)
        out_ref[...] = (ov * rstd * w_ref[...] * (z * _sigmoid(z))).astype(BF16)

    return pl.pallas_call(
        body, name="delta_post_fwd", grid=(s // tr, heads),
        in_specs=[pl.BlockSpec((tr, HEAD_DIM), lambda i, h: (i, h)),
                  pl.BlockSpec((tr, HEAD_DIM), lambda i, h: (i, 3 * heads + h)),
                  pl.BlockSpec((1, HEAD_DIM), lambda i, h: (0, 0))],
        out_specs=pl.BlockSpec((tr, HEAD_DIM), lambda i, h: (i, h)),
        out_shape=jax.ShapeDtypeStruct((s, heads * HEAD_DIM), BF16),
        compiler_params=_params(("parallel", "parallel")),
    )(o, qkvz, w)


def _rope_tables(positions, s):
    half = HEAD_DIM // 2
    inv_freq = ROPE_THETA ** (-jnp.arange(half, dtype=F32) / half)
    ang = positions.reshape(s, 1).astype(F32) * inv_freq
    cos, sin = jnp.cos(ang), jnp.sin(ang)
    return jnp.concatenate([cos, cos], axis=-1), jnp.concatenate([-sin, sin], axis=-1)


def _attn_pre_fwd(aqkv, wq, wk, cosf, sinf, heads):
    s = aqkv.shape[0]
    tr = _tile(s, 4096, 8)

    def body(x_ref, wq_ref, wk_ref, c_ref, s_ref, o_ref):
        xv = x_ref[...]
        wv = jnp.where(pl.program_id(1) < heads, wq_ref[...], wk_ref[...])
        y = xv * lax.rsqrt(jnp.mean(xv * xv, axis=-1, keepdims=True) + EPS) * wv
        o_ref[...] = y * c_ref[...] + pltpu.roll(y, HEAD_DIM // 2, 1) * s_ref[...]

    blk = pl.BlockSpec((tr, HEAD_DIM), lambda i, j: (i, j))
    vec = pl.BlockSpec((1, HEAD_DIM), lambda i, j: (0, 0))
    tab = pl.BlockSpec((tr, HEAD_DIM), lambda i, j: (i, 0))
    return pl.pallas_call(
        body, name="attn_pre_fwd", grid=(s // tr, 2 * heads),
        in_specs=[blk, vec, vec, tab, tab], out_specs=blk,
        out_shape=jax.ShapeDtypeStruct((s, 2 * heads * HEAD_DIM), F32),
        compiler_params=_params(("parallel", "parallel")),
    )(aqkv, wq, wk, cosf, sinf)


def _band():
    qi = lax.broadcasted_iota(jnp.int32, (SPAN, 2 * SPAN), 0)
    ki = lax.broadcasted_iota(jnp.int32, (SPAN, 2 * SPAN), 1)
    dist = qi + SPAN - ki
    return (dist >= 0) & (dist <= SPAN), ki >= SPAN


def _sub(g, r, d):
    if d == 1:
        return pl.ds(g * SPAN, SPAN)
    return pl.ds(g * SPAN * d + r, SPAN, stride=d)


def _attn_blocks(s):
    assert s % (SPAN * max(DILATIONS)) == 0
    return [(p_i, d, r, g) for p_i, d in enumerate(DILATIONS) for r in range(d) for g in range(s // (SPAN * d))]


def _attn_fwd(qk, aqkv, w, heads):
    s = qk.shape[0]
    aw = heads * HEAD_DIM

    def body(q_ref, k_ref, v_ref, w_ref, mix_ref, acc_ref, m_ref, l_ref):
        band, own = _band()
        for p_i, d, r, g in _attn_blocks(s):
            rows = _sub(g, r, d)
            kc, vc = k_ref[rows, :], v_ref[rows, :]
            if g == 0:
                kp, vp, mask = kc, vc, band & own
            else:
                mask = band
            kcat = jnp.concatenate([kp, kc], axis=0)
            vcat = jnp.concatenate([vp, vc], axis=0)
            kp, vp = kc, vc
            sc = _dot(q_ref[rows, :] * (HEAD_DIM ** -0.5), kcat, "nt")
            sc = jnp.where(mask, sc, NEG)
            m = jnp.max(sc, axis=-1, keepdims=True)
            if p_i == 0:
                p = jnp.exp(sc - m)
                acc_ref[rows, :] = _dot(p, vcat)
                l_new = jnp.sum(p, axis=-1, keepdims=True)
            else:
                m_old = m_ref[rows, 0:1]
                m = jnp.maximum(m, m_old)
                alpha = jnp.exp(m_old - m)
                p = jnp.exp(sc - m)
                acc_ref[rows, :] = alpha * acc_ref[rows, :] + _dot(p, vcat)
                l_new = alpha * l_ref[rows, 0:1] + jnp.sum(p, axis=-1, keepdims=True)
            m_ref[rows, :] = jnp.broadcast_to(m, (SPAN, HEAD_DIM))
            l_ref[rows, :] = jnp.broadcast_to(l_new, (SPAN, HEAD_DIM))
        den = l_ref[...]
        ob = acc_ref[...] / den
        acc_ref[...] = ob
        m_ref[...] = m_ref[...] + jnp.log(den)
        rstd = lax.rsqrt(jnp.mean(ob * ob, axis=-1, keepdims=True) + EPS)
        mix_ref[...] = (ob * rstd * w_ref[...]).astype(BF16)

    col = lambda off: pl.BlockSpec((s, HEAD_DIM), lambda h: (0, off + h))
    return pl.pallas_call(
        body, name="attn_fwd", grid=(heads,),
        in_specs=[col(0), col(heads), col(2 * heads), pl.BlockSpec((1, HEAD_DIM), lambda h: (0, 0))],
        out_specs=[col(0), col(0), col(0)],
        out_shape=[jax.ShapeDtypeStruct((s, aw), BF16), jax.ShapeDtypeStruct((s, aw), F32),
                   jax.ShapeDtypeStruct((s, aw), F32)],
        scratch_shapes=[pltpu.VMEM((s, HEAD_DIM), F32)],
        compiler_params=_params(("parallel",)),
    )(qk, qk, aqkv, w)


def _attn_bwd(qk, aqkv, do, lse, dsum, heads):
    s = qk.shape[0]
    aw = heads * HEAD_DIM
    scale = HEAD_DIM ** -0.5

    def body(q_ref, k_ref, v_ref, do_ref, l_ref, ds_ref, out_ref):
        band, own = _band()
        dq_ref, dk_ref, dv_ref = out_ref.at[0], out_ref.at[1], out_ref.at[2]
        out_ref[...] = jnp.zeros((3, s, HEAD_DIM), F32)
        for _, d, r, g in _attn_blocks(s):
            rows = _sub(g, r, d)
            qs, dov = q_ref[rows, :] * scale, do_ref[rows, :]
            kc, vc = k_ref[rows, :], v_ref[rows, :]
            if g == 0:
                kp, vp, mask = kc, vc, band & own
            else:
                mask = band
            kcat = jnp.concatenate([kp, kc], axis=0)
            vcat = jnp.concatenate([vp, vc], axis=0)
            p = jnp.where(mask, jnp.exp(_dot(qs, kcat, "nt") - l_ref[rows, 0:1]), 0.0)
            dsc = p * (_dot(dov, vcat, "nt") - ds_ref[rows, 0:1])
            dq_ref[rows, :] += scale * _dot(dsc, kcat)
            dk = _dot(dsc, qs, "tn")
            dv = _dot(p, dov, "tn")
            if g > 0:
                dk_ref[prows, :] += dk_own + dk[:SPAN]
                dv_ref[prows, :] += dv_own + dv[:SPAN]
            dk_own, dv_own = dk[SPAN:], dv[SPAN:]
            if g == s // (SPAN * d) - 1:
                dk_ref[rows, :] += dk_own
                dv_ref[rows, :] += dv_own
            kp, vp, prows = kc, vc, rows

    col = lambda off: pl.BlockSpec((s, HEAD_DIM), lambda h: (0, off + h))
    return pl.pallas_call(
        body, name="attn_bwd", grid=(heads,),
        in_specs=[col(0), col(heads), col(2 * heads), col(0), col(0), col(0)],
        out_specs=pl.BlockSpec((3, s, HEAD_DIM), lambda h: (0, 0, h)),
        out_shape=jax.ShapeDtypeStruct((3, s, aw), F32),
        compiler_params=_params(("parallel",)),
    )(qk, qk, aqkv, do, lse, dsum)


def _attn_pre_bwd(grads, aqkv, wq, wk, cosf, sinf, heads):
    s = aqkv.shape[0]
    tr = _tile(s, 4096, 8)
    nrow = s // tr

    def body(g_ref, x_ref, wq_ref, wk_ref, c_ref, s_ref, dx_ref, dwq_ref, dwk_ref):
        i, j = pl.program_id(0), pl.program_id(1)
        kind = j // heads
        dout = g_ref[0]
        tot_v = dout
        dy = dout * c_ref[...] + pltpu.roll(dout * s_ref[...], HEAD_DIM // 2, 1)
        xv = x_ref[...]
        wv = jnp.where(kind == 0, wq_ref[...], wk_ref[...])
        rstd = lax.rsqrt(jnp.mean(xv * xv, axis=-1, keepdims=True) + EPS)
        xhat = xv * rstd
        gw = dy * wv
        dxn = rstd * (gw - xhat * jnp.mean(gw * xhat, axis=-1, keepdims=True))
        dx_ref[...] = jnp.where(kind == 2, tot_v, dxn).astype(BF16)
        dwc = jnp.sum(dy * xhat, axis=0, keepdims=True)

        @pl.when((i == 0) & (j == 0))
        def _():
            dwq_ref[...] = jnp.zeros_like(dwq_ref)
            dwk_ref[...] = jnp.zeros_like(dwk_ref)

        @pl.when(kind == 0)
        def _():
            dwq_ref[...] += dwc

        @pl.when(kind == 1)
        def _():
            dwk_ref[...] += dwc

    grad = pl.BlockSpec((1, tr, HEAD_DIM), lambda i, j: (j // heads, i, j % heads))
    blk = pl.BlockSpec((tr, HEAD_DIM), lambda i, j: (i, j))
    vec = pl.BlockSpec((1, HEAD_DIM), lambda i, j: (0, 0))
    tab = pl.BlockSpec((tr, HEAD_DIM), lambda i, j: (i, 0))
    return pl.pallas_call(
        body, name="attn_pre_bwd", grid=(nrow, 3 * heads),
        in_specs=[grad, blk, vec, vec, tab, tab], out_specs=[blk, vec, vec],
        out_shape=[jax.ShapeDtypeStruct((s, 3 * heads * HEAD_DIM), BF16),
                   jax.ShapeDtypeStruct((1, HEAD_DIM), F32), jax.ShapeDtypeStruct((1, HEAD_DIM), F32)],
        compiler_params=_params(("arbitrary", "arbitrary")),
    )(grads, aqkv, wq, wk, cosf, sinf)


def _colsum(a, name):
    s, d = a.shape
    tr = _tile(s, 4096, 8)

    def body(a_ref, o_ref):
        @pl.when(pl.program_id(0) == 0)
        def _():
            o_ref[...] = jnp.zeros_like(o_ref)

        o_ref[...] += jnp.sum(a_ref[...], axis=0, keepdims=True)

    return pl.pallas_call(
        body, name=name, grid=(s // tr,),
        in_specs=[pl.BlockSpec((tr, d), lambda i: (i, 0))], out_specs=pl.BlockSpec((1, d), lambda i: (0, 0)),
        out_shape=jax.ShapeDtypeStruct((1, d), F32),
        compiler_params=_params(("arbitrary",)),
    )(a)


def _local_step(x, positions, target, small, w_qkvz, w_bd, w_attn, conv_w, later_weights, ffn_grads_ready,
                rest_grads_ready, after=None):
    s, dmod = x.shape
    heads = dmod // (2 * HEAD_DIM)
    dw = heads * HEAD_DIM
    a_log, dt_bias = small["a_log"].reshape(heads, 1, 1), small["dt_bias"].reshape(heads, 1, 1)
    cosf, sinf = _rope_tables(positions, s)

    h1 = _rmsnorm_fwd(x, small["attn_norm_w"], "norm1_fwd", after=after)
    qkvz = _matmul(h1, w_qkvz, "nn", "proj_qkvz")
    bd = _matmul(h1, w_bd, "nn", "proj_bd")
    aqkv = _matmul(h1, w_attn, "nn", "proj_attn")
    dqkv = _delta_pre_fwd(qkvz, conv_w, heads)
    o_d, *saved = _delta_chunk_fwd(dqkv, bd, a_log, dt_bias, heads)
    mix_a = _delta_post_fwd(o_d, qkvz, small["delta_out_norm_w"], heads)
    qk_rot = _attn_pre_fwd(aqkv, small["q_norm_w"], small["k_norm_w"], cosf, sinf, heads)
    mix_b, ob, lse = _attn_fwd(qk_rot, aqkv, small["attn_out_norm_w"], heads)
    w_out, behind, ffn_weights = later_weights((mix_a, mix_b))
    x1 = _matmul(mix_a, w_out, "nn", "out_proj", add=x, b_rows=(0, dw), more=[(mix_b, w_out, (dw, dw))], after=behind)
    h2 = _rmsnorm_fwd(x1, small["ffn_norm_w"], "norm2_fwd")
    w_gu, w_down = ffn_weights(h2)
    ff = w_down.shape[0]

    def swiglu(accs, vals):
        g, u = accs
        return g, u, g * _sigmoid(g) * u

    gate, up, act = _matmul(h2, w_gu, "nn", "ffn_gate_up", b_cols=(0, ff), b2_cols=(ff, ff), finish=swiglu,
                            out_dtypes=[BF16, BF16, BF16])

    def loss_head(acc, vals):
        err = acc + vals[0] - vals[1]
        part = 0.5 * jnp.sum(jnp.sum(err * err, axis=-1, keepdims=True) * (1.0 / dmod), axis=0, keepdims=True)
        lane = lax.broadcasted_iota(jnp.int32, (1, HEAD_DIM), 1)
        return err * (1.0 / dmod), err * (1.0 / dmod), jnp.where(lane == 0, part, 0.0)

    dy, dy16, loss_row = _matmul(act, w_down, "nn", "ffn_down", tiles=[(x1, 0), (target, 0)], finish=loss_head,
                                 out_dtypes=[F32, BF16], row_sum=True)

    def swiglu_bwd(acc, vals):
        g, u = vals[0].astype(F32), vals[1].astype(F32)
        sg = _sigmoid(g)
        return acc * u * sg * (1.0 + g * (1.0 - sg)), acc * g * sg

    dgate, dup = _matmul(dy16, w_down, "nt", "ffn_down_dx", tiles=[(gate, 0), (up, 0)], finish=swiglu_bwd,
                         out_dtypes=[BF16, BF16])
    g_w_down = _matmul(act, dy16, "tn", "ffn_down_dw")
    g_w_gate = _matmul(h2, dgate, "tn", "ffn_gate_dw")
    g_w_up = _matmul(h2, dup, "tn", "ffn_up_dw")
    behind, and_then = ffn_grads_ready(g_w_gate, g_w_up, g_w_down)
    dh2 = _matmul(dgate, w_gu, "nt", "ffn_gate_dx", b_cols=(0, ff), after=behind)
    dh2 = _matmul(dup, w_gu, "nt", "ffn_up_dx", b_cols=(ff, ff), add=dh2)
    behind = and_then(dh2)
    dx1, dx1_16, g_ffn_norm = _rmsnorm_bwd(dh2, x1, small["ffn_norm_w"], dy, "norm2_bwd", after=behind)
    def per_head(fn, tile_w):
        outs = [fn(slice(h * HEAD_DIM, (h + 1) * HEAD_DIM)) for h in range(tile_w // HEAD_DIM)]
        joined = [jnp.concatenate([o[i] for o in outs], axis=1) for i in range(len(outs[0]) - 1)]
        return (*joined, sum(o[-1] for o in outs))

    def delta_post_bwd(acc, vals):
        def one(hs):
            d, ov, z, wv = acc[:, hs], vals[0][:, hs], vals[1][:, hs], vals[2][:, hs]
            sg = _sigmoid(z)
            rstd = lax.rsqrt(jnp.mean(ov * ov, axis=-1, keepdims=True) + EPS)
            ohat = ov * rstd
            dn = d * (z * sg)
            gw = dn * wv
            return (rstd * (gw - ohat * jnp.mean(gw * ohat, axis=-1, keepdims=True)),
                    d * (ohat * wv) * sg * (1.0 + z * (1.0 - sg)), jnp.sum(dn * ohat, axis=0, keepdims=True))
        return per_head(one, acc.shape[1])

    def attn_merge_bwd(acc, vals):
        def one(hs):
            d, ov, wv = acc[:, hs], vals[0][:, hs], vals[1][:, hs]
            rstd = lax.rsqrt(jnp.mean(ov * ov, axis=-1, keepdims=True) + EPS)
            ohat = ov * rstd
            gw = d * wv
            dov = rstd * (gw - ohat * jnp.mean(gw * ohat, axis=-1, keepdims=True))
            return (dov, jnp.broadcast_to(jnp.sum(dov * ov, axis=-1, keepdims=True), dov.shape),
                    jnp.sum(d * ohat, axis=0, keepdims=True))
        return per_head(one, acc.shape[1])

    rep = lambda wv: jnp.tile(wv, (1, heads))
    do_d, dz, g_delta_out_norm = _matmul(
        dx1_16, w_out, "nt", "out_proj_dx_a", b_rows=(0, dw), finish=delta_post_bwd, out_dtypes=[F32, BF16],
        tiles=[(o_d, 0), (qkvz, 3 * dw), (rep(small["delta_out_norm_w"]), 0)], row_sum=True)
    dob, dsum, g_attn_out_norm = _matmul(
        dx1_16, w_out, "nt", "out_proj_dx_b", b_rows=(dw, dw), finish=attn_merge_bwd, out_dtypes=[F32, F32],
        tiles=[(ob, 0), (rep(small["attn_out_norm_w"]), 0)], row_sum=True)
    g_w_out_a = _matmul(mix_a, dx1_16, "tn", "out_proj_dw_a")
    g_w_out_b = _matmul(mix_b, dx1_16, "tn", "out_proj_dw_b")
    grads = _attn_bwd(qk_rot, aqkv, dob, lse, dsum, heads)
    d_aqkv, g_q_norm, g_k_norm = _attn_pre_bwd(grads, aqkv, small["q_norm_w"], small["k_norm_w"], cosf, sinf, heads)
    ddqkv, dbd = _delta_chunk_bwd(do_d, dqkv, bd, saved, a_log, dt_bias, heads)
    d_qkv_raw, g_conv = _delta_pre_bwd(ddqkv, qkvz, conv_w, heads)
    bd_sums = _colsum(dbd, "bd_colsum")
    g_w_qkv = _matmul(h1, d_qkv_raw, "tn", "proj_qkv_dw")
    g_w_z = _matmul(h1, dz, "tn", "proj_z_dw")
    g_w_bd = _matmul(h1, dbd, "tn", "proj_bd_dw")
    g_w_attn = _matmul(h1, d_aqkv, "tn", "proj_attn_dw")
    behind, and_then = rest_grads_ready(dict(w_qkv=g_w_qkv, w_z=g_w_z, w_bd=g_w_bd, w_attn=g_w_attn,
                                             w_out_a=g_w_out_a, w_out_b=g_w_out_b))
    dh1 = _matmul(d_qkv_raw, w_qkvz, "nt", "proj_qkvz_dx", b_cols=(0, 3 * dw), more=[(dz, w_qkvz, (3 * dw, dw))],
                  after=behind)
    dh1 = _matmul(d_aqkv, w_attn, "nt", "proj_attn_bd_dx", add=dh1, more=[(dbd, w_bd, (0, HEAD_DIM))],
                  after=and_then(dh1))
    grad_x, _, g_attn_norm = _rmsnorm_bwd(dh1, x, small["attn_norm_w"], dx1, "norm1_bwd")
    small_grads = dict(
        attn_norm_w=g_attn_norm, a_log=bd_sums[:, 2 * heads:3 * heads], dt_bias=bd_sums[:, heads:2 * heads],
        delta_out_norm_w=g_delta_out_norm, q_norm_w=g_q_norm, k_norm_w=g_k_norm,
        attn_out_norm_w=g_attn_out_norm, ffn_norm_w=g_ffn_norm, conv_w=g_conv)
    return loss_row, grad_x, small_grads


def _adamw(w, g, m, v, name):
    r, c = w.shape
    tr = _tile(r, 256, 8)

    def body(w_ref, g_ref, m_ref, v_ref, d_ref, nm_ref, nv_ref):
        gv = g_ref[...]
        nm = ADAM_B1 * m_ref[...] + (1.0 - ADAM_B1) * gv
        nv = ADAM_B2 * v_ref[...] + (1.0 - ADAM_B2) * (gv * gv)
        m_hat = nm / (1.0 - ADAM_B1 ** ADAM_STEP)
        v_hat = nv / (1.0 - ADAM_B2 ** ADAM_STEP)
        d_ref[...] = -ADAM_LR * (m_hat / (jnp.sqrt(v_hat) + ADAM_EPS) + ADAM_WD * w_ref[...])
        nm_ref[...] = nm
        nv_ref[...] = nv

    blk = pl.BlockSpec((tr, c), lambda i: (i, 0))
    return pl.pallas_call(
        body, name=name, grid=(r // tr,),
        in_specs=[blk] * 4, out_specs=[blk] * 3,
        out_shape=[jax.ShapeDtypeStruct((r, c), F32)] * 3,
        compiler_params=_params(("parallel",)),
    )(w, g, m, v)


def _add_half_bf16(g, b, place, name):
    n, half, c = b.shape
    tr = _tile(half, 512, 16)
    nb = half // tr

    def body(place_ref, g_ref, b_ref, o_ref):
        o_ref[...] = (g_ref[...].astype(F32) + b_ref[...].astype(F32)).astype(BF16)

    blk = pl.BlockSpec((1, tr, c), lambda i, j, p: (i, j, 0))
    return pl.pallas_call(
        body, name=name,
        grid_spec=pltpu.PrefetchScalarGridSpec(
            num_scalar_prefetch=1, grid=(n, nb),
            in_specs=[pl.BlockSpec((1, tr, c), lambda i, j, p: (i, p[0] * nb + j, 0)), blk], out_specs=blk),
        out_shape=jax.ShapeDtypeStruct((n, half, c), BF16),
        compiler_params=_params(("parallel", "parallel")),
    )(place, g, b)


def _sum4_f32(mine, others, place, name):
    _, half, c = mine.shape
    tr = _tile(half, 512, 16)
    nb = half // tr

    def body(place_ref, a_ref, b_ref, o_ref):
        acc = a_ref[0].astype(F32)
        for j in range(3):
            acc = acc + b_ref[j].astype(F32)
        o_ref[...] = acc

    return pl.pallas_call(
        body, name=name,
        grid_spec=pltpu.PrefetchScalarGridSpec(
            num_scalar_prefetch=1, grid=(nb,),
            in_specs=[pl.BlockSpec((1, tr, c), lambda i, p: (p[1], i, 0)),
                      pl.BlockSpec((3, tr, c), lambda i, p: (0, i, 0))],
            out_specs=pl.BlockSpec((tr, c), lambda i, p: (p[0] * nb + i, 0))),
        out_shape=jax.ShapeDtypeStruct((2 * half, c), F32),
        compiler_params=_params(("parallel",)),
    )(place, mine, others)


def _place():
    x, y, c = lax.axis_index("x"), lax.axis_index("y"), lax.axis_index("c")
    other_chips = [(1 - x, y), (x, 1 - y), (1 - x, 1 - y)]
    return x, y, c, (x, y, 1 - c), other_chips


ANY = pl.BlockSpec(memory_space=pl.ANY)


def _remote(k, src, dst, to, send_sems, recv_sems):
    return pltpu.make_async_remote_copy(src_ref=src, dst_ref=dst, send_sem=send_sems.at[k], recv_sem=recv_sems.at[k],
                                        device_id=to, device_id_type=MESH)


def _half(ref, lead, hc):
    if lead is None:
        half = ref.shape[0] // 2
        return ref.at[pl.ds(hc * half, half), :]
    half = ref.shape[1] // 2
    return ref.at[lead, pl.ds(hc * half, half), :]


def _all_gather_weights(slots, whole, name):
    nt, nw = len(slots), len(whole)
    base_w, base_f, base_d = 2 * nt, 2 * nt + 3 * nw, 4 * nt + 3 * nw

    def quarter(ref, lead, hc, q):
        quart = ref.shape[1] // 4
        return ref.at[lead, pl.ds((2 * hc + q) * quart, quart), :]

    def body(*refs):
        ins, outs = refs[:nt + nw], refs[nt + nw:2 * (nt + nw)]
        sems = refs[2 * (nt + nw):]
        x, y, c, sibling, chips = _place()
        me, xn, yn, dg = 2 * x + y, 2 * (1 - x) + y, 2 * x + 1 - y, 2 * (1 - x) + 1 - y
        to_x, to_y = (1 - x, y, c), (x, 1 - y, c)
        cps = []
        for t in range(nt):
            cps.append(_remote(2 * t, _half(ins[t], me, c), _half(outs[t], me, c), to_x, *sems))
            cps.append(_remote(2 * t + 1, _half(ins[t], me, c), _half(outs[t], me, c), to_y, *sems))
        for j, (px, py) in enumerate(chips):
            for t in range(nw):
                cps.append(_remote(base_w + j * nw + t, ins[nt + t].at[me], outs[nt + t].at[me], (px, py, c), *sems))
        for cp in cps:
            cp.start()

        def start(k, ref, to):
            cp = _remote(k, ref, ref, to, *sems)
            cp.start()
            cps.append(cp)

        for t in range(nt):
            landed = _half(outs[t], xn, c)
            _remote(2 * t, landed, landed, to_x, *sems).wait_recv()
            start(base_f + 2 * t, quarter(outs[t], xn, c, 0), to_y)
            start(base_d + 3 * t, landed, sibling)
            landed = _half(outs[t], yn, c)
            _remote(2 * t + 1, landed, landed, to_y, *sems).wait_recv()
            start(base_f + 2 * t + 1, quarter(outs[t], yn, c, 1), to_x)
            start(base_d + 3 * t + 1, landed, sibling)
        for t in range(nt):
            q0, q1 = quarter(outs[t], dg, c, 0), quarter(outs[t], dg, c, 1)
            _remote(base_f + 2 * t, q0, q0, to_y, *sems).wait_recv()
            _remote(base_f + 2 * t + 1, q1, q1, to_x, *sems).wait_recv()
            start(base_d + 3 * t + 2, _half(outs[t], dg, c), sibling)
        for j, (px, py) in enumerate(chips):
            for t in range(nw):
                landed = outs[nt + t].at[2 * px + py]
                _remote(base_w + j * nw + t, landed, landed, (px, py, c), *sems).wait_recv()
        for t in range(nt):
            for j, chip in enumerate((xn, yn, dg)):
                other = _half(outs[t], chip, 1 - c)
                _remote(base_d + 3 * t + j, other, other, sibling, *sems).wait_recv()
        for cp in cps:
            cp.wait_send()

    arrays = list(slots) + list(whole)
    n_sem = 7 * nt + 3 * nw
    return pl.pallas_call(
        body, name=name, in_specs=[ANY] * len(arrays), out_specs=[ANY] * len(arrays),
        input_output_aliases={i: i for i in range(len(arrays))},
        out_shape=[jax.ShapeDtypeStruct(a.shape, a.dtype) for a in arrays],
        scratch_shapes=[pltpu.SemaphoreType.DMA((n_sem,)), pltpu.SemaphoreType.DMA((n_sem,))],
    )(*arrays)


def _join_halves(fs, name):
    nt = len(fs)

    def body(*refs):
        in_refs, out_refs, (send_sems, recv_sems) = refs[:nt], refs[nt:2 * nt], refs[2 * nt:]
        _, _, c, sibling, _ = _place()
        cps = [_remote(t, _half(in_refs[t], None, c), _half(out_refs[t], None, c), sibling, send_sems, recv_sems)
               for t in range(nt)]
        for cp in cps:
            cp.start()
        for t in range(nt):
            theirs = _half(out_refs[t], None, 1 - c)
            _remote(t, theirs, theirs, sibling, send_sems, recv_sems).wait_recv()
        for cp in cps:
            cp.wait_send()

    return pl.pallas_call(
        body, name=name, in_specs=[ANY] * nt, out_specs=[ANY] * nt,
        input_output_aliases={i: i for i in range(nt)},
        out_shape=[jax.ShapeDtypeStruct(f.shape, f.dtype) for f in fs],
        scratch_shapes=[pltpu.SemaphoreType.DMA((nt,)), pltpu.SemaphoreType.DMA((nt,))],
    )(*fs)


HBM = pl.BlockSpec(memory_space=pltpu.HBM)
SEM = pl.BlockSpec(memory_space=pltpu.SEMAPHORE)
EFFECT = pltpu.SideEffectType.DATAFLOW_SIDE_EFFECTING


def _split_start(arrays, after, plan, n_copies, name):
    na = len(arrays)

    def body(*refs):
        ins, send_sems, recv_sems = refs[:na], refs[na + 1], refs[na + 2]
        outs, token = refs[na + 3:2 * na + 3], refs[2 * na + 3]
        for k, (src, dst, to) in enumerate(plan(ins, outs)):
            _remote(k, src, dst, to, send_sems, recv_sems).start()
        token[...] = jnp.zeros_like(token)

    res = pl.pallas_call(
        body, name=name,
        out_shape=(pltpu.SemaphoreType.DMA((n_copies,)), pltpu.SemaphoreType.DMA((n_copies,)),
                   *[pltpu.HBM(a.shape, a.dtype) for a in arrays], jax.ShapeDtypeStruct((8, HEAD_DIM), F32)),
        in_specs=[HBM] * na + [ANY],
        out_specs=(SEM, SEM, *[HBM] * na, pl.BlockSpec(memory_space=pltpu.VMEM)),
        input_output_aliases={i: 2 + i for i in range(na)},
        compiler_params=pltpu.CompilerParams(has_side_effects=EFFECT),
    )(*[pltpu.with_memory_space_constraint(a, pltpu.HBM) for a in arrays], after)
    return res[0], res[1], list(res[2:2 + na]), res[2 + na]


def _split_wait(send_sems, recv_sems, arrays, after, plan, name):
    na = len(arrays)
    after = list(after) if isinstance(after, (list, tuple)) else [after]

    def body(*refs):
        ins, send, recv, outs = refs[:na], refs[na], refs[na + 1], refs[na + 2 + len(after):]
        for k, (src, dst, to) in enumerate(plan(ins, outs)):
            cp = _remote(k, src, dst, to, send, recv)
            cp.wait_send()
            cp.wait_recv()

    res = pl.pallas_call(
        body, name=name, out_shape=tuple(pltpu.HBM(a.shape, a.dtype) for a in arrays),
        in_specs=[HBM] * na + [SEM, SEM] + [ANY] * len(after), out_specs=tuple([HBM] * na),
        input_output_aliases={i: i for i in range(na)},
        compiler_params=pltpu.CompilerParams(has_side_effects=EFFECT),
    )(*arrays, send_sems, recv_sems, *after)
    return list(res)


def _gather_plan(nt):
    def plan(ins, outs):
        x, y, c, _, chips = _place()
        me = 2 * x + y
        return [(_half(ins[t], me, c), _half(outs[t], me, c), (px, py, c)) for px, py in chips for t in range(nt)]
    return plan


def _gather_landed_plan(nt):
    def plan(ins, outs):
        _, _, c, _, chips = _place()
        return [(_half(outs[t], 2 * px + py, c), _half(outs[t], 2 * px + py, c), (px, py, c))
                for px, py in chips for t in range(nt)]
    return plan


def _scatter_plan(nt):
    def plan(ins, outs):
        _, _, c, _, chips = _place()
        return [(ins[t].at[2 * px + py], outs[nt + t].at[j], (px, py, c))
                for j, (px, py) in enumerate(chips) for t in range(nt)]
    return plan


def _pass_plan(nt):
    def plan(ins, outs):
        _, _, c, sibling, chips = _place()
        return [(_half(ins[t], 2 * px + py, c), _half(outs[t], 2 * px + py, c), sibling)
                for px, py in chips for t in range(nt)]
    return plan


def _pass_landed_plan(nt):
    def plan(ins, outs):
        _, _, c, sibling, chips = _place()
        return [(_half(outs[t], 2 * px + py, c), _half(outs[t], 2 * px + py, 1 - c), sibling)
                for px, py in chips for t in range(nt)]
    return plan


def _swap_plan(nt):
    def plan(ins, outs):
        _, _, c, sibling, _ = _place()
        res = []
        for t in range(nt):
            half = ins[t].shape[1] // 2
            res.append((ins[t].at[:, pl.ds((1 - c) * half, half), :], outs[nt + t], sibling))
        return res
    return plan


def _pass_halves_to_sibling(slots, name):
    nt = len(slots)

    def body(*refs):
        ins, outs, (send_sems, recv_sems) = refs[:nt], refs[nt:2 * nt], refs[2 * nt:]
        _, _, c, sibling, chips = _place()
        cps = [_remote(j * nt + t, _half(ins[t], 2 * px + py, c), _half(outs[t], 2 * px + py, c), sibling,
                       send_sems, recv_sems)
               for j, (px, py) in enumerate(chips) for t in range(nt)]
        for cp in cps:
            cp.start()
        for j, (px, py) in enumerate(chips):
            for t in range(nt):
                other = _half(outs[t], 2 * px + py, 1 - c)
                _remote(j * nt + t, other, other, sibling, send_sems, recv_sems).wait_recv()
        for cp in cps:
            cp.wait_send()

    return pl.pallas_call(
        body, name=name, in_specs=[ANY] * nt, out_specs=[ANY] * nt,
        input_output_aliases={i: i for i in range(nt)},
        out_shape=[jax.ShapeDtypeStruct(a.shape, a.dtype) for a in slots],
        scratch_shapes=[pltpu.SemaphoreType.DMA((3 * nt,)), pltpu.SemaphoreType.DMA((3 * nt,))],
    )(*slots)


def _all_reduce_small(v):
    r, lanes = v.shape

    def body(v_ref, out_ref, buf, send_sems, recv_sems):
        x, y, c, sibling, chips = _place()

        def slot(px, py, pc):
            return buf.at[4 * px + 2 * py + pc]

        def copy(k, block, to, src=None):
            return pltpu.make_async_remote_copy(src_ref=slot(*block) if src is None else src, dst_ref=slot(*block),
                                                send_sem=send_sems.at[k], recv_sem=recv_sems.at[k],
                                                device_id=to, device_id_type=MESH)

        me = (x, y, c)
        buf[4 * x + 2 * y + c] = v_ref[...]
        first = [copy(0, me, sibling, src=v_ref)]
        first += [copy(1 + j, me, (*chip, c), src=v_ref) for j, chip in enumerate(chips)]
        for cp in first:
            cp.start()
        passed = [copy(4 + j, (*chip, c), sibling) for j, chip in enumerate(chips)]
        for j, chip in enumerate(chips):
            copy(1 + j, (*chip, c), me).wait_recv()
            passed[j].start()
        copy(0, (x, y, 1 - c), me).wait_recv()
        for j, chip in enumerate(chips):
            copy(4 + j, (*chip, 1 - c), me).wait_recv()
        for cp in first + passed:
            cp.wait_send()
        acc = buf[0]
        for k in range(1, 8):
            acc = acc + buf[k]
        out_ref[...] = acc

    vmem = pl.BlockSpec(memory_space=pltpu.VMEM)
    return pl.pallas_call(
        body, name="all_reduce_small", in_specs=[vmem], out_specs=vmem,
        out_shape=jax.ShapeDtypeStruct((r, lanes), F32),
        scratch_shapes=[pltpu.VMEM((8, r, lanes), F32), pltpu.SemaphoreType.DMA((7,)), pltpu.SemaphoreType.DMA((7,))],
    )(v)


def _size(shape):
    n = 1
    for d in shape:
        n *= d
    return n


def _regroup_cols(parts, widths):
    out, start = [], 0
    for width in widths:
        pieces, lo = [], 0
        for p in parts:
            a, b = max(start, lo), min(start + width, lo + p.shape[1])
            if a < b:
                pieces.append(p[:, a - lo:b - lo])
            lo += p.shape[1]
        out.append(pieces[0] if len(pieces) == 1 else jnp.concatenate(pieces, axis=1))
        start += width
    return out


def _pack_small(parts):
    rows = []
    for p in parts:
        f = p.reshape(-1).astype(F32)
        n = -(-f.shape[0] // HEAD_DIM) * HEAD_DIM
        rows.append(jnp.pad(f, (0, n - f.shape[0])).reshape(-1, HEAD_DIM))
    a = jnp.concatenate(rows, axis=0)
    return jnp.pad(a, ((0, -a.shape[0] % 8), (0, 0)))


def _unpack_small(a, shapes):
    out, row = [], 0
    for shp in shapes:
        nrows = -(-_size(shp) // HEAD_DIM)
        out.append(a[row:row + nrows].reshape(-1)[:_size(shp)].reshape(shp))
        row += nrows
    return out


SMALL = ["attn_norm_w", "a_log", "dt_bias", "delta_out_norm_w", "q_norm_w", "k_norm_w", "attn_out_norm_w", "ffn_norm_w"]
BIG = ["w_in", "w_out", "w_gate_up", "w_down"]
ORDER = ["attn_norm_w", "w_in", "conv_w", "a_log", "dt_bias", "delta_out_norm_w", "q_norm_w", "k_norm_w",
         "attn_out_norm_w", "w_out", "ffn_norm_w", "w_gate_up", "w_down"]


def kernel(x, positions, attn_norm_w, w_in, conv_w, a_log, dt_bias, delta_out_norm_w, q_norm_w, k_norm_w, attn_out_norm_w, w_out, ffn_norm_w, w_gate_up, w_down, loss_target, m_attn_norm_w, m_w_in, m_conv_w, m_a_log, m_dt_bias, m_delta_out_norm_w, m_q_norm_w, m_k_norm_w, m_attn_out_norm_w, m_w_out, m_ffn_norm_w, m_w_gate_up, m_w_down, v_attn_norm_w, v_w_in, v_conv_w, v_a_log, v_dt_bias, v_delta_out_norm_w, v_q_norm_w, v_k_norm_w, v_attn_out_norm_w, v_w_out, v_ffn_norm_w, v_w_gate_up, v_w_down):
    wts = dict(attn_norm_w=attn_norm_w, w_in=w_in, conv_w=conv_w, a_log=a_log, dt_bias=dt_bias,
               delta_out_norm_w=delta_out_norm_w, q_norm_w=q_norm_w, k_norm_w=k_norm_w,
               attn_out_norm_w=attn_out_norm_w, w_out=w_out, ffn_norm_w=ffn_norm_w, w_gate_up=w_gate_up, w_down=w_down)
    mom = dict(attn_norm_w=m_attn_norm_w, w_in=m_w_in, conv_w=m_conv_w, a_log=m_a_log, dt_bias=m_dt_bias,
               delta_out_norm_w=m_delta_out_norm_w, q_norm_w=m_q_norm_w, k_norm_w=m_k_norm_w,
               attn_out_norm_w=m_attn_out_norm_w, w_out=m_w_out, ffn_norm_w=m_ffn_norm_w, w_gate_up=m_w_gate_up,
               w_down=m_w_down)
    var = dict(attn_norm_w=v_attn_norm_w, w_in=v_w_in, conv_w=v_conv_w, a_log=v_a_log, dt_bias=v_dt_bias,
               delta_out_norm_w=v_delta_out_norm_w, q_norm_w=v_q_norm_w, k_norm_w=v_k_norm_w,
               attn_out_norm_w=v_attn_out_norm_w, w_out=v_w_out, ffn_norm_w=v_ffn_norm_w, w_gate_up=v_w_gate_up,
               w_down=v_w_down)
    dmod = x.shape[2]
    heads = dmod // (2 * HEAD_DIM)
    dw = heads * HEAD_DIM
    chip = 2 * lax.axis_index("x") + lax.axis_index("y")
    core = lax.axis_index("c")
    n_in, n_out, n_gu, n_down, n_conv = (w_in.shape[2], w_out.shape[1], w_gate_up.shape[2], w_down.shape[1],
                                         conv_w.shape[2])

    def slots_of(w, dtype):
        shard = w[0].astype(dtype)
        return lax.dynamic_update_index_in_dim(lax.empty((4,) + shard.shape, dtype), shard, chip, axis=0)

    s_in, s_conv = _all_gather_weights([slots_of(w_in, BF16)], [slots_of(conv_w, F32)], "all_gather_w_in")
    later = [slots_of(w_out, BF16), slots_of(w_gate_up, BF16), slots_of(w_down, BF16)]
    w_send, w_recv, later, started = _split_start(later, s_conv, _gather_plan(3), 9, "gather_rest_start")
    by_cols = lambda a: a.transpose(1, 0, 2).reshape(a.shape[1], 4 * a.shape[2])
    conv_f = by_cols(s_conv)
    w_qkvz, w_bd, w_attn = _regroup_cols([s_in[i] for i in range(4)], [4 * dw, 2 * heads, 3 * dw])
    w_bd = jnp.pad(w_bd, ((0, 0), (0, HEAD_DIM - 2 * heads)))
    small = {n: wts[n] for n in SMALL}
    place = jnp.stack([core, chip]).astype(jnp.int32)
    to_slots = lambda a: a.reshape(a.shape[0], 4, a.shape[1] // 4).transpose(1, 0, 2)

    def later_weights(after):
        landed = _split_wait(w_send, w_recv, later, after, _gather_landed_plan(3), "gather_rest_wait")
        s_out, = _pass_halves_to_sibling(landed[:1], "gather_out_pass")
        p_send, p_recv, passing, token = _split_start(landed[1:], s_out, _pass_plan(2), 6, "gather_ffn_pass_start")

        def ffn_weights(after):
            s_gu, s_down = _split_wait(p_send, p_recv, passing, after, _pass_landed_plan(2), "gather_ffn_pass_wait")
            return by_cols(s_gu), s_down.reshape(4 * n_down, dmod)

        return s_out.reshape(4 * n_out, dmod), token, ffn_weights

    ffn = {}

    def ffn_grads_ready(g_gate, g_up, g_down):
        gs = [to_slots(jnp.concatenate([g_gate, g_up], axis=1)), g_down.reshape(4, n_down, dmod)]
        zones = [lax.empty((4, g.shape[1] // 2, g.shape[2]), BF16) for g in gs]
        s_send, s_recv, s_bufs, swapping = _split_start(gs + zones, g_gate, _swap_plan(2), 2, "swap_ffn_start")

        def and_then(after):
            g_gu, g_dn, b_gu, b_dn = _split_wait(s_send, s_recv, s_bufs, after, _swap_plan(2), "swap_ffn_wait")
            sums = [_add_half_bf16(g_gu, b_gu, place, "chip_partial_sum_w_gate_up"),
                    _add_half_bf16(g_dn, b_dn, place, "chip_partial_sum_w_down")]
            zones3 = [lax.empty((3,) + p.shape[1:], BF16) for p in sums]
            ffn["send"], ffn["recv"], ffn["bufs"], token = _split_start(sums + zones3, b_gu, _scatter_plan(2), 6,
                                                                        "scatter_ffn_start")
            return token

        return swapping, and_then

    rest = {}

    def rest_grads_ready(bg):
        gs = [jnp.stack(_regroup_cols([bg["w_qkv"], bg["w_z"], bg["w_bd"][:, :2 * heads], bg["w_attn"]], [n_in] * 4)),
              jnp.concatenate([bg["w_out_a"], bg["w_out_b"]], axis=0).reshape(4, n_out, dmod)]
        zones = [lax.empty((4, g.shape[1] // 2, g.shape[2]), BF16) for g in gs]
        s_send, s_recv, s_bufs, swapping = _split_start(gs + zones, bg["w_attn"], _swap_plan(2), 2, "swap_rest_start")

        def and_then(after):
            g_in, g_out, b_in, b_out = _split_wait(s_send, s_recv, s_bufs, after, _swap_plan(2), "swap_rest_wait")
            sums = [_add_half_bf16(g_in, b_in, place, "chip_partial_sum_w_in"),
                    _add_half_bf16(g_out, b_out, place, "chip_partial_sum_w_out")]
            zones3 = [lax.empty((3,) + p.shape[1:], BF16) for p in sums]
            rest["send"], rest["recv"], rest["bufs"], token = _split_start(sums + zones3, b_in, _scatter_plan(2), 6,
                                                                           "scatter_rest_start")
            return token

        return swapping, and_then

    loss_row, grad_x, sg = _local_step(
        x[0], positions[0], loss_target[0], small, w_qkvz, w_bd, w_attn, conv_f,
        later_weights, ffn_grads_ready, rest_grads_ready, after=started)

    r_send, r_recv, r_bufs = rest["send"], rest["recv"], rest["bufs"]
    sum_gu, sum_down, got_gu, got_down = _split_wait(ffn["send"], ffn["recv"], ffn["bufs"], grad_x,
                                                     _scatter_plan(2), "scatter_ffn_wait")
    g_big = dict(zip(["w_gate_up", "w_down"], _join_halves(
        [_sum4_f32(sum_gu, got_gu, place, "grad_total_w_gate_up"),
         _sum4_f32(sum_down, got_down, place, "grad_total_w_down")], "join_ffn_halves")))
    grads, deltas, new_m, new_v = {}, {}, {}, {}

    def adamw_big(n):
        shp = wts[n].shape
        d, nm, nv = _adamw(wts[n][0], g_big[n], mom[n][0], var[n][0], "adamw_" + n)
        grads[n], deltas[n], new_m[n], new_v[n] = g_big[n].reshape(shp), d.reshape(shp), nm.reshape(shp), nv.reshape(shp)
        return d

    done = [adamw_big("w_gate_up"), adamw_big("w_down")]
    sum_in, sum_out, got_in, got_out = _split_wait(r_send, r_recv, r_bufs, done, _scatter_plan(2), "scatter_rest_wait")
    g_big.update(zip(["w_in", "w_out"], _join_halves(
        [_sum4_f32(sum_in, got_in, place, "grad_total_w_in"),
         _sum4_f32(sum_out, got_out, place, "grad_total_w_out")], "join_rest_halves")))
    adamw_big("w_in")
    adamw_big("w_out")

    reduced = _all_reduce_small(_pack_small([sg[n] for n in SMALL] + [sg["conv_w"], loss_row]))
    red = _unpack_small(reduced, [wts[n].shape for n in SMALL] + [(4, 4 * n_conv), (1, HEAD_DIM)])
    g_small = dict(zip(SMALL, red[:len(SMALL)]))
    g_conv_full, loss_out = red[len(SMALL)], red[len(SMALL) + 1]
    g_small["conv_w"] = lax.dynamic_slice_in_dim(g_conv_full, chip * n_conv, n_conv, axis=1).reshape(conv_w.shape)

    names = SMALL + ["conv_w"]
    shapes = [wts[n].shape for n in names]
    d, nm, nv = _adamw(_pack_small([wts[n] for n in names]), _pack_small([g_small[n] for n in names]),
                       _pack_small([mom[n] for n in names]), _pack_small([var[n] for n in names]), "adamw_small")
    for n, dd, mm, vv in zip(names, _unpack_small(d, shapes), _unpack_small(nm, shapes), _unpack_small(nv, shapes)):
        grads[n], deltas[n], new_m[n], new_v[n] = g_small[n], dd, mm, vv
    return (loss_out[0, 0], grad_x[None], *[grads[n] for n in ORDER], *[deltas[n] for n in ORDER],
            *[new_m[n] for n in ORDER], *[new_v[n] for n in ORDER])
```

```python
import jax
import jax.numpy as jnp
from jax import lax
from jax.experimental import pallas as pl
from jax.experimental.pallas import tpu as pltpu

F32 = jnp.float32
BF16 = jnp.bfloat16
HEAD_DIM = 128
CHUNK = 128
INV_BLOCK = 64
SPAN = 128
DILATIONS = (1, 4, 16)
ROPE_THETA = 10000.0
EPS = 1e-6
NEG = -1e30
ADAM_LR, ADAM_B1, ADAM_B2, ADAM_EPS, ADAM_WD, ADAM_STEP = 0.001, 0.9, 0.999, 1e-08, 0.01, 10
VMEM_LIMIT = 48 * 1024 * 1024
MATMUL_VMEM = 44 * 1024 * 1024
MESH = pl.DeviceIdType.MESH

_DN = {"nn": (((1,), (0,)), ((), ())), "nt": (((1,), (1,)), ((), ())), "tn": (((0,), (0,)), ((), ()))}


def _dot(a, b, mode="nn"):
    (ca, cb), _ = _DN[mode]
    if a.ndim == 3:
        dn = (((ca[0] + 1,), (cb[0] + 1,)), ((0,), (0,)))
    else:
        dn = _DN[mode]
    return lax.dot_general(a.astype(BF16), b.astype(BF16), dn, preferred_element_type=F32)


def _rsum(x):
    return jnp.sum(x, axis=-1, keepdims=True)


def _csum(x):
    return jnp.sum(x, axis=-2, keepdims=True)


def _tile(dim, pref, unit=128):
    t = (min(pref, dim) // unit) * unit
    while t >= unit:
        if dim % t == 0:
            return t
        t -= unit
    return dim


def _params(sem):
    return pltpu.CompilerParams(dimension_semantics=sem, vmem_limit_bytes=VMEM_LIMIT)


def _sigmoid(x):
    return 1.0 / (1.0 + jnp.exp(-x))


def _matmul(a, b, mode, name, add=None, out_dtype=F32, a_cols=None, b_cols=None, b_rows=None,
            tiles=(), finish=None, out_dtypes=(), row_sum=False, after=None, b2_cols=None, more=()):
    if mode == "tn":
        out_dtype = BF16
    a_off, a_w = a_cols if a_cols else (0, a.shape[1])
    b_off, b_w = b_cols if b_cols else (0, b.shape[1])
    br_off, br_n = b_rows if b_rows else (0, b.shape[0])
    if mode == "nn":
        m, k, n = a.shape[0], a_w, b_w
        assert br_n == k
    elif mode == "nt":
        m, k, n = a.shape[0], a_w, br_n
        assert b_w == k
    else:
        k, m, n = a.shape[0], a_w, b_w
        assert b.shape[0] == k
    if finish is None:
        out_dtypes = [out_dtype]
        if add is None:
            finish = lambda acc, vals: (acc,)
        else:
            tiles = [(add, 0)]
            finish = lambda acc, vals: (acc + vals[0].astype(F32),)
    n_b = 1 if b2_cols is None else 2
    sides = sum(jnp.dtype(t.dtype).itemsize for t, _ in tiles) + sum(jnp.dtype(d).itemsize for d in out_dtypes)

    def vmem(tm_, tn_, tk_):
        ops = tm_ * tk_ * jnp.dtype(a.dtype).itemsize + n_b * tk_ * tn_ * jnp.dtype(b.dtype).itemsize
        for a_i, b_i, (_, k_i) in more:
            ops += tm_ * k_i * jnp.dtype(a_i.dtype).itemsize + k_i * tn_ * jnp.dtype(b_i.dtype).itemsize
        return 2 * (ops + tm_ * tn_ * sides) + (n_b * tm_ * tn_ * 4 if tk_ < k else 0)

    tm, tn = _tile(m, 1024, 128), _tile(n, 1024, 128)
    tall = _tile(m, 1536, 128), _tile(n, 512, 128)
    if tall[0] * tall[1] > tm * tn and vmem(*tall, k) <= MATMUL_VMEM:
        tm, tn = tall
    tk = next(t for t in [_tile(k, p, 128) for p in (4096, 2048, 1024, 512, 128)] if vmem(tm, tn, t) <= MATMUL_VMEM)
    if mode == "nn":
        assert a_off % tk == 0 and b_off % tn == 0 and br_off % tk == 0
        a_spec = pl.BlockSpec((tm, tk), lambda i, j, kk: (i, kk + a_off // tk))
        b_spec = pl.BlockSpec((tk, tn), lambda i, j, kk: (kk + br_off // tk, j + b_off // tn))
    elif mode == "nt":
        assert a_off % tk == 0 and b_off % tk == 0 and br_off % tn == 0
        a_spec = pl.BlockSpec((tm, tk), lambda i, j, kk: (i, kk + a_off // tk))
        b_spec = pl.BlockSpec((tn, tk), lambda i, j, kk: (j + br_off // tn, kk + b_off // tk))
    else:
        assert a_off % tm == 0 and b_off % tn == 0
        a_spec = pl.BlockSpec((tk, tm), lambda i, j, kk: (kk, i + a_off // tm))
        b_spec = pl.BlockSpec((tk, tn), lambda i, j, kk: (kk, j + b_off // tn))
    nk = k // tk
    assert not more or (nk == 1 and n_b == 1 and mode == "nn")
    n_tiles, n_out = len(tiles), len(out_dtypes)
    extra = [] if after is None else [after]
    first_tile_ref = 1 + n_b + 2 * len(more)
    first_out = first_tile_ref + n_tiles + len(extra)

    def body(*refs):
        a_ref, b_refs = refs[0], refs[1:1 + n_b]
        tile_refs, out_refs = refs[first_tile_ref:first_tile_ref + n_tiles], refs[first_out:first_out + n_out]
        acc_refs = refs[-n_b:] if nk > 1 else ()
        kk = pl.program_id(2)
        first_tile = (pl.program_id(0) == 0) & (pl.program_id(1) == 0)

        av = a_ref[...]
        if nk > 1:
            @pl.when(kk == 0)
            def _():
                for acc_ref in acc_refs:
                    acc_ref[...] = jnp.zeros_like(acc_ref)

            for acc_ref, b_ref in zip(acc_refs, b_refs):
                acc_ref[...] += _dot(av, b_ref[...], mode)

        @pl.when(kk == nk - 1)
        def _():
            if nk > 1:
                accs = [acc_ref[...] for acc_ref in acc_refs]
            else:
                accs = [_dot(av, b_ref[...], mode) for b_ref in b_refs]
                for p_i in range(len(more)):
                    accs[0] = accs[0] + _dot(refs[1 + n_b + 2 * p_i][...], refs[2 + n_b + 2 * p_i][...], mode)
            res = finish(accs[0] if n_b == 1 else accs, [t[...] for t in tile_refs])
            for o_ref, r in zip(out_refs, res):
                o_ref[...] = r.astype(o_ref.dtype)
            if row_sum:
                row_ref = refs[first_out + n_out]

                @pl.when(first_tile)
                def _():
                    row_ref[...] = res[n_out]

                @pl.when(jnp.logical_not(first_tile))
                def _():
                    row_ref[...] += res[n_out]

    in_specs = [a_spec, b_spec]
    args = [a, b]
    if b2_cols is not None:
        assert mode == "nn" and b2_cols[1] == n and b2_cols[0] % tn == 0
        in_specs.append(pl.BlockSpec((tk, tn), lambda i, j, kk: (kk + br_off // tk, j + b2_cols[0] // tn)))
        args.append(b)
    for a_i, b_i, (start, k_i) in more:
        assert a_i.shape == (m, k_i) and start % k_i == 0
        in_specs.append(pl.BlockSpec((tm, k_i), lambda i, j, kk: (i, 0)))
        in_specs.append(pl.BlockSpec((k_i, tn), lambda i, j, kk, s_=start // k_i: (s_, j)))
        args += [a_i, b_i]
    for arr, off in tiles:
        assert off % tn == 0
        if arr.shape[0] == 1:
            in_specs.append(pl.BlockSpec((1, tn), lambda i, j, kk, off=off: (0, j + off // tn)))
        else:
            in_specs.append(pl.BlockSpec((tm, tn), lambda i, j, kk, off=off: (i, j + off // tn)))
        args.append(arr)
    in_specs += [ANY] * len(extra)
    args += extra
    out_specs = [pl.BlockSpec((tm, tn), lambda i, j, kk: (i, j))] * n_out
    out_shape = [jax.ShapeDtypeStruct((m, n), dt) for dt in out_dtypes]
    if row_sum:
        out_specs.append(pl.BlockSpec((1, HEAD_DIM), lambda i, j, kk: (0, 0)))
        out_shape.append(jax.ShapeDtypeStruct((1, HEAD_DIM), F32))
    res = pl.pallas_call(
        body, name=name, grid=(m // tm, n // tn, nk),
        in_specs=in_specs, out_specs=out_specs, out_shape=out_shape,
        scratch_shapes=[pltpu.VMEM((tm, tn), F32)] * (n_b if nk > 1 else 0),
        compiler_params=_params(("arbitrary",) * 3 if row_sum else ("parallel", "parallel", "arbitrary")),
    )(*args)
    return res[0] if len(res) == 1 else res


def _rmsnorm_fwd(x, w, name, after=None):
    s, d = x.shape
    tr = _tile(s, 1024, 8)

    def body(x_ref, w_ref, *rest):
        h_ref = rest[-1]
        xv = x_ref[...]
        rstd = lax.rsqrt(jnp.mean(xv * xv, axis=-1, keepdims=True) + EPS)
        h_ref[...] = (xv * rstd * w_ref[...]).astype(BF16)

    extra = [] if after is None else [after]
    return pl.pallas_call(
        body, name=name, grid=(s // tr,),
        in_specs=[pl.BlockSpec((tr, d), lambda i: (i, 0)), pl.BlockSpec((1, d), lambda i: (0, 0))] + [ANY] * len(extra),
        out_specs=pl.BlockSpec((tr, d), lambda i: (i, 0)),
        out_shape=jax.ShapeDtypeStruct((s, d), BF16),
        compiler_params=_params(("parallel",)),
    )(x, w, *extra)


def _rmsnorm_bwd(dh, x, w, res, name, after=None):
    s, d = x.shape
    tr = _tile(s, 256, 8)

    def body(dh_ref, x_ref, w_ref, res_ref, *rest):
        dx_ref, dx16_ref, dw_ref = rest[-3:]
        xv = x_ref[...]
        rstd = lax.rsqrt(jnp.mean(xv * xv, axis=-1, keepdims=True) + EPS)
        xhat = xv * rstd
        dhv = dh_ref[...]
        gw = dhv * w_ref[...]
        dx = res_ref[...] + rstd * (gw - xhat * jnp.mean(gw * xhat, axis=-1, keepdims=True))
        dx_ref[...] = dx
        dx16_ref[...] = dx.astype(BF16)

        @pl.when(pl.program_id(0) == 0)
        def _():
            dw_ref[...] = jnp.zeros_like(dw_ref)

        dw_ref[...] += jnp.sum(dhv * xhat, axis=0, keepdims=True)

    row = pl.BlockSpec((tr, d), lambda i: (i, 0))
    vec = pl.BlockSpec((1, d), lambda i: (0, 0))
    extra = [] if after is None else [after]
    return pl.pallas_call(
        body, name=name, grid=(s // tr,),
        in_specs=[row, row, vec, row] + [ANY] * len(extra), out_specs=[row, row, vec],
        out_shape=[jax.ShapeDtypeStruct((s, d), F32), jax.ShapeDtypeStruct((s, d), BF16),
                   jax.ShapeDtypeStruct((1, d), F32)],
        compiler_params=_params(("arbitrary",)),
    )(dh, x, w, res, *extra)


def _conv_taps(x, w, rows):
    shifted = [x]
    for sft in (1, 2, 3):
        shifted.append(jnp.where(rows >= sft, pltpu.roll(x, sft, 0), 0.0))
    y = w[3:4, :] * shifted[0] + w[2:3, :] * shifted[1] + w[1:2, :] * shifted[2] + w[0:1, :] * shifted[3]
    return y, shifted


def _delta_pre_fwd(qkvz, conv_w, heads):
    s = qkvz.shape[0]
    nblk = 3 * heads

    def body(x_ref, w_ref, o_ref):
        part = pl.program_id(0) // heads
        rows = lax.broadcasted_iota(jnp.int32, (s, HEAD_DIM), 0)
        y, _ = _conv_taps(x_ref[...], w_ref[...], rows)
        a = y * _sigmoid(y)
        rs = lax.rsqrt(jnp.sum(a * a, axis=-1, keepdims=True) + EPS)
        fac = jnp.where(part == 0, rs * (HEAD_DIM ** -0.5), jnp.where(part == 1, rs, 1.0))
        o_ref[...] = a * fac

    return pl.pallas_call(
        body, name="delta_pre_fwd", grid=(nblk,),
        in_specs=[pl.BlockSpec((s, HEAD_DIM), lambda i: (0, i)), pl.BlockSpec((4, HEAD_DIM), lambda i: (0, i))],
        out_specs=pl.BlockSpec((s, HEAD_DIM), lambda i: (0, i)),
        out_shape=jax.ShapeDtypeStruct((s, 3 * heads * HEAD_DIM), F32),
        compiler_params=_params(("parallel",)),
    )(qkvz, conv_w)


def _delta_pre_bwd(dqkv, qkvz, conv_w, heads):
    s = qkvz.shape[0]
    nblk = 3 * heads

    def body(d_ref, x_ref, w_ref, dx_ref, dw_ref):
        part = pl.program_id(0) // heads
        rows = lax.broadcasted_iota(jnp.int32, (s, HEAD_DIM), 0)
        w = w_ref[...]
        y, shifted = _conv_taps(x_ref[...], w, rows)
        sg = _sigmoid(y)
        a = y * sg
        rs = lax.rsqrt(jnp.sum(a * a, axis=-1, keepdims=True) + EPS)
        unit = a * rs
        dn = d_ref[...]
        scale = jnp.where(part == 0, HEAD_DIM ** -0.5, 1.0)
        da_norm = scale * rs * (dn - unit * jnp.sum(dn * unit, axis=-1, keepdims=True))
        da = jnp.where(part < 2, da_norm, dn)
        dy = da * sg * (1.0 + y * (1.0 - sg))
        dx = w[3:4, :] * dy
        for sft in (1, 2, 3):
            dx = dx + w[3 - sft:4 - sft, :] * jnp.where(rows < s - sft, pltpu.roll(dy, s - sft, 0), 0.0)
        dx_ref[...] = dx.astype(BF16)
        for sft in range(4):
            dw_ref[3 - sft:4 - sft, :] = jnp.sum(dy * shifted[sft], axis=0, keepdims=True)

    col = pl.BlockSpec((s, HEAD_DIM), lambda i: (0, i))
    wsp = pl.BlockSpec((4, HEAD_DIM), lambda i: (0, i))
    return pl.pallas_call(
        body, name="delta_pre_bwd", grid=(nblk,),
        in_specs=[col, col, wsp], out_specs=[col, wsp],
        out_shape=[jax.ShapeDtypeStruct((s, 3 * heads * HEAD_DIM), BF16),
                   jax.ShapeDtypeStruct((4, 3 * heads * HEAD_DIM), F32)],
        compiler_params=_params(("parallel",)),
    )(dqkv, qkvz, conv_w)


def _heads_of(ref, heads):
    return jnp.stack([ref[:, h * HEAD_DIM:(h + 1) * HEAD_DIM] for h in range(heads)])


def _chunk_common(q, k, v, bd, a_log, dt_bias, heads, solved=None):
    c = CHUNK
    braw = jnp.stack([bd[:, h:h + 1] for h in range(heads)])
    draw = jnp.stack([bd[:, heads + h:heads + h + 1] for h in range(heads)])
    beta = _sigmoid(braw)
    xd = draw + dt_bias
    sp = jnp.maximum(xd, 0.0) + jnp.log1p(jnp.exp(-jnp.abs(xd)))
    g = -jnp.exp(a_log) * sp
    row = lax.broadcasted_iota(jnp.int32, (c, c), 0)
    col = lax.broadcasted_iota(jnp.int32, (c, c), 1)
    sq = (heads, c, c)
    g_b = jnp.broadcast_to(g, sq)
    g_row = _csum(jnp.where(row == col, g_b, 0.0))
    gam_col = _rsum(jnp.where(col <= row, jnp.broadcast_to(g_row, sq), 0.0))
    gam_row = _csum(jnp.where(row <= col, g_b, 0.0))
    causal = row >= col
    dm = jnp.where(causal, jnp.exp(jnp.where(causal, gam_col - gam_row, 0.0)), 0.0)
    kk = _dot(k, k, "nt")
    e = jnp.exp(gam_col)
    if solved is None:
        low = jnp.where(row > col, beta * kk * dm, 0.0)
        assert c in (INV_BLOCK, 2 * INV_BLOCK)
        same = (row // INV_BLOCK) == (col // INV_BLOCK)
        diag = jnp.where(same, low, 0.0)
        t = jnp.where(row == col, 1.0, 0.0) - diag
        pw = diag
        for _ in range((INV_BLOCK - 1).bit_length() - 1):
            pw = _dot(pw, pw)
            t = t + _dot(t, pw)
        if c > INV_BLOCK:
            t = t - _dot(_dot(t, low - diag), t)
        u = _dot(t, beta * v)
        w = _dot(t, (beta * e) * k)
    else:
        t, u, w = solved
    qk_raw = _dot(q, k, "nt")
    gl = _csum(g)
    el = jnp.exp(gl - gam_col)
    return dict(beta=beta, xd=xd, g=g, row=row, col=col, dm=dm, kk=kk, t=t, e=e, u=u, w=w,
                qk_raw=qk_raw, qk=qk_raw * dm, gl=gl, el=el, qd=e * q, kd=el * k, cd=jnp.exp(gl))


def _delta_chunk_fwd(qkv, bd, a_log, dt_bias, heads):
    s = qkv.shape[0]
    n = s // CHUNK
    dw = heads * HEAD_DIM
    blk = lambda part: pl.BlockSpec((CHUNK, dw), lambda i: (i, part))

    def body(q_ref, k_ref, v_ref, bd_ref, al_ref, dt_ref, o_ref, st_ref, t_ref, uw_ref, state):
        @pl.when(pl.program_id(0) == 0)
        def _():
            state[...] = jnp.zeros_like(state)

        cm = _chunk_common(_heads_of(q_ref, heads), _heads_of(k_ref, heads), _heads_of(v_ref, heads), bd_ref[...],
                           al_ref[...], dt_ref[...], heads)
        st = state[...]
        st_ref[0] = st
        t_ref[0] = cm["t"]
        uw_ref[0, 0] = cm["u"]
        uw_ref[0, 1] = cm["w"]
        vn = cm["u"] - _dot(cm["w"], st)
        o = _dot(cm["qd"], st) + _dot(cm["qk"], vn)
        for h in range(heads):
            o_ref[:, h * HEAD_DIM:(h + 1) * HEAD_DIM] = o[h]
        state[...] = cm["cd"] * st + _dot(cm["kd"], vn, "tn")

    smem = pl.BlockSpec((heads, 1, 1), lambda i: (0, 0, 0))
    return pl.pallas_call(
        body, name="delta_chunk_fwd", grid=(n,),
        in_specs=[blk(0), blk(1), blk(2), pl.BlockSpec((CHUNK, HEAD_DIM), lambda i: (i, 0)), smem, smem],
        out_specs=[pl.BlockSpec((CHUNK, dw), lambda i: (i, 0)),
                   pl.BlockSpec((1, heads, HEAD_DIM, HEAD_DIM), lambda i: (i, 0, 0, 0)),
                   pl.BlockSpec((1, heads, CHUNK, CHUNK), lambda i: (i, 0, 0, 0)),
                   pl.BlockSpec((1, 2, heads, CHUNK, HEAD_DIM), lambda i: (i, 0, 0, 0, 0))],
        out_shape=[jax.ShapeDtypeStruct((s, dw), F32),
                   jax.ShapeDtypeStruct((n, heads, HEAD_DIM, HEAD_DIM), F32),
                   jax.ShapeDtypeStruct((n, heads, CHUNK, CHUNK), F32),
                   jax.ShapeDtypeStruct((n, 2, heads, CHUNK, HEAD_DIM), F32)],
        scratch_shapes=[pltpu.VMEM((heads, HEAD_DIM, HEAD_DIM), F32)],
        compiler_params=_params(("arbitrary",)),
    )(qkv, qkv, qkv, bd, a_log, dt_bias)


def _delta_chunk_bwd(do, qkv, bd, saved, a_log, dt_bias, heads):
    s = qkv.shape[0]
    n = s // CHUNK
    dw = heads * HEAD_DIM
    c = CHUNK
    blk = lambda part: pl.BlockSpec((CHUNK, dw), lambda i: (n - 1 - i, part))

    def all_heads(q, k, v, dov, st, solved, dsn, bd, a_log, dt_bias):
        cm = _chunk_common(q, k, v, bd, a_log, dt_bias, heads, solved)
        beta, e, dm, row, col = cm["beta"], cm["e"], cm["dm"], cm["row"], cm["col"]
        sq = (heads, c, c)
        vn = cm["u"] - _dot(cm["w"], st)
        dvn = _dot(cm["kd"], dsn)
        dkd = _dot(vn, dsn, "nt")
        dcd = _csum(_rsum(st * dsn))
        ds = cm["cd"] * dsn
        dqd = _dot(dov, st, "nt")
        ds = ds + _dot(cm["qd"], dov, "tn")
        dqk = _dot(dov, vn, "nt")
        dvn = dvn + _dot(cm["qk"], dov, "tn")
        dw_ = -_dot(dvn, st, "nt")
        ds = ds - _dot(cm["w"], dvn, "tn")
        drhs_u = _dot(cm["t"], dvn, "tn")
        drhs_w = _dot(cm["t"], dw_, "tn")
        da = -(_dot(drhs_u, cm["u"], "nt") + _dot(drhs_w, cm["w"], "nt"))
        dl = jnp.where(row > col, da, 0.0)
        dbeta = _rsum(dl * cm["kk"] * dm)
        dkk = dl * beta * dm
        dd = dl * beta * cm["kk"]
        dv = beta * drhs_u
        ek = e * k
        dbeta = dbeta + _rsum(drhs_u * v) + _rsum(drhs_w * ek)
        dk = (beta * e) * drhs_w
        dgam = _rsum(drhs_w * (beta * ek))
        dqkm = dqk * dm
        dq = _dot(dqkm, k)
        dk = dk + _dot(dqkm, q, "tn")
        dd = dd + dqk * cm["qk_raw"]
        dk = dk + _dot(dkk, k) + _dot(dkk, k, "tn")
        dq = dq + e * dqd
        dgam = dgam + _rsum(dqd * cm["qd"])
        dk = dk + cm["el"] * dkd
        r = _rsum(dkd * cm["kd"])
        dgam = dgam - r
        dgl = _csum(r) + dcd * cm["cd"]
        mm = dd * dm
        colsum_c = _rsum(jnp.where(row == col, jnp.broadcast_to(_csum(mm), sq), 0.0))
        dgam = dgam + _rsum(mm) - colsum_c
        ridx = lax.broadcasted_iota(jnp.int32, (c, 1), 0)
        dgam = dgam + jnp.where(ridx == c - 1, dgl, 0.0)
        dgam_row = _csum(jnp.where(row == col, jnp.broadcast_to(dgam, sq), 0.0))
        dg = _rsum(jnp.where(col >= row, jnp.broadcast_to(dgam_row, sq), 0.0))
        d_xd = dg * (-jnp.exp(a_log)) * _sigmoid(cm["xd"])
        d_braw = dbeta * beta * (1.0 - beta)
        d_alog = dg * cm["g"]
        lane = lax.broadcasted_iota(jnp.int32, (c, HEAD_DIM), 1)
        dbd = jnp.zeros((c, HEAD_DIM), F32)
        for h in range(heads):
            dbd = (dbd + jnp.where(lane == h, d_braw[h], 0.0) + jnp.where(lane == h + heads, d_xd[h], 0.0)
                   + jnp.where(lane == h + 2 * heads, d_alog[h], 0.0))
        return dq, dk, dv, ds, dbd

    def body(do_ref, q_ref, k_ref, v_ref, bd_ref, st_ref, t_ref, uw_ref, al_ref, dt_ref, dqkv_ref, dbd_ref, dstate):
        @pl.when(pl.program_id(0) == 0)
        def _():
            dstate[...] = jnp.zeros_like(dstate)

        dq, dk, dv, ds, dbd = all_heads(_heads_of(q_ref, heads), _heads_of(k_ref, heads), _heads_of(v_ref, heads),
                                        _heads_of(do_ref, heads), st_ref[0], (t_ref[0], uw_ref[0, 0], uw_ref[0, 1]),
                                        dstate[...], bd_ref[...],
                                        al_ref[...], dt_ref[...])
        for part, val in enumerate((dq, dk, dv)):
            for h in range(heads):
                lo = part * dw + h * HEAD_DIM
                dqkv_ref[:, lo:lo + HEAD_DIM] = val[h]
        dstate[...] = ds
        dbd_ref[...] = dbd

    smem = pl.BlockSpec((heads, 1, 1), lambda i: (0, 0, 0))
    shared = pl.BlockSpec((CHUNK, HEAD_DIM), lambda i: (n - 1 - i, 0))
    wide = pl.BlockSpec((CHUNK, dw), lambda i: (n - 1 - i, 0))
    return pl.pallas_call(
        body, name="delta_chunk_bwd", grid=(n,),
        in_specs=[wide, blk(0), blk(1), blk(2), shared,
                  pl.BlockSpec((1, heads, HEAD_DIM, HEAD_DIM), lambda i: (n - 1 - i, 0, 0, 0)),
                  pl.BlockSpec((1, heads, CHUNK, CHUNK), lambda i: (n - 1 - i, 0, 0, 0)),
                  pl.BlockSpec((1, 2, heads, CHUNK, HEAD_DIM), lambda i: (n - 1 - i, 0, 0, 0, 0)), smem, smem],
        out_specs=[pl.BlockSpec((CHUNK, 3 * dw), lambda i: (n - 1 - i, 0)), shared],
        out_shape=[jax.ShapeDtypeStruct((s, 3 * dw), F32), jax.ShapeDtypeStruct((s, HEAD_DIM), F32)],
        scratch_shapes=[pltpu.VMEM((heads, HEAD_DIM, HEAD_DIM), F32)],
        compiler_params=_params(("arbitrary",)),
    )(do, qkv, qkv, qkv, bd, *saved, a_log, dt_bias)


def _delta_post_fwd(o, qkvz, w, heads):
    s = o.shape[0]
    tr = _tile(s, 4096, 8)

    def body(o_ref, z_ref, w_ref, out_ref):
        ov, z = o_ref[...], z_ref[...]
        rstd = lax.rsqrt(jnp.mean(ov * ov, axis=-1, keepdims=True) + EPS)
        out_ref[...] = (ov * rstd * w_ref[...] * (z * _sigmoid(z))).astype(BF16)

    return pl.pallas_call(
        body, name="delta_post_fwd", grid=(s // tr, heads),
        in_specs=[pl.BlockSpec((tr, HEAD_DIM), lambda i, h: (i, h)),
                  pl.BlockSpec((tr, HEAD_DIM), lambda i, h: (i, 3 * heads + h)),
                  pl.BlockSpec((1, HEAD_DIM), lambda i, h: (0, 0))],
        out_specs=pl.BlockSpec((tr, HEAD_DIM), lambda i, h: (i, h)),
        out_shape=jax.ShapeDtypeStruct((s, heads * HEAD_DIM), BF16),
        compiler_params=_params(("parallel", "parallel")),
    )(o, qkvz, w)


def _rope_tables(positions, s):
    half = HEAD_DIM // 2
    inv_freq = ROPE_THETA ** (-jnp.arange(half, dtype=F32) / half)
    ang = positions.reshape(s, 1).astype(F32) * inv_freq
    cos, sin = jnp.cos(ang), jnp.sin(ang)
    return jnp.concatenate([cos, cos], axis=-1), jnp.concatenate([-sin, sin], axis=-1)


def _attn_pre_fwd(aqkv, wq, wk, cosf, sinf, heads):
    s = aqkv.shape[0]
    tr = _tile(s, 4096, 8)

    def body(x_ref, wq_ref, wk_ref, c_ref, s_ref, o_ref):
        xv = x_ref[...]
        wv = jnp.where(pl.program_id(1) < heads, wq_ref[...], wk_ref[...])
        y = xv * lax.rsqrt(jnp.mean(xv * xv, axis=-1, keepdims=True) + EPS) * wv
        o_ref[...] = y * c_ref[...] + pltpu.roll(y, HEAD_DIM // 2, 1) * s_ref[...]

    blk = pl.BlockSpec((tr, HEAD_DIM), lambda i, j: (i, j))
    vec = pl.BlockSpec((1, HEAD_DIM), lambda i, j: (0, 0))
    tab = pl.BlockSpec((tr, HEAD_DIM), lambda i, j: (i, 0))
    return pl.pallas_call(
        body, name="attn_pre_fwd", grid=(s // tr, 2 * heads),
        in_specs=[blk, vec, vec, tab, tab], out_specs=blk,
        out_shape=jax.ShapeDtypeStruct((s, 2 * heads * HEAD_DIM), F32),
        compiler_params=_params(("parallel", "parallel")),
    )(aqkv, wq, wk, cosf, sinf)


def _band():
    qi = lax.broadcasted_iota(jnp.int32, (SPAN, 2 * SPAN), 0)
    ki = lax.broadcasted_iota(jnp.int32, (SPAN, 2 * SPAN), 1)
    dist = qi + SPAN - ki
    return (dist >= 0) & (dist <= SPAN), ki >= SPAN


def _sub(g, r, d):
    if d == 1:
        return pl.ds(g * SPAN, SPAN)
    return pl.ds(g * SPAN * d + r, SPAN, stride=d)


def _attn_blocks(s):
    assert s % (SPAN * max(DILATIONS)) == 0
    return [(p_i, d, r, g) for p_i, d in enumerate(DILATIONS) for r in range(d) for g in range(s // (SPAN * d))]


def _attn_fwd(qk, aqkv, w, heads):
    s = qk.shape[0]
    aw = heads * HEAD_DIM

    def body(q_ref, k_ref, v_ref, w_ref, mix_ref, acc_ref, m_ref, l_ref):
        band, own = _band()
        for p_i, d, r, g in _attn_blocks(s):
            rows = _sub(g, r, d)
            kc, vc = k_ref[rows, :], v_ref[rows, :]
            if g == 0:
                kp, vp, mask = kc, vc, band & own
            else:
                mask = band
            kcat = jnp.concatenate([kp, kc], axis=0)
            vcat = jnp.concatenate([vp, vc], axis=0)
            kp, vp = kc, vc
            sc = _dot(q_ref[rows, :] * (HEAD_DIM ** -0.5), kcat, "nt")
            sc = jnp.where(mask, sc, NEG)
            m = jnp.max(sc, axis=-1, keepdims=True)
            if p_i == 0:
                p = jnp.exp(sc - m)
                acc_ref[rows, :] = _dot(p, vcat)
                l_new = jnp.sum(p, axis=-1, keepdims=True)
            else:
                m_old = m_ref[rows, 0:1]
                m = jnp.maximum(m, m_old)
                alpha = jnp.exp(m_old - m)
                p = jnp.exp(sc - m)
                acc_ref[rows, :] = alpha * acc_ref[rows, :] + _dot(p, vcat)
                l_new = alpha * l_ref[rows, 0:1] + jnp.sum(p, axis=-1, keepdims=True)
            m_ref[rows, :] = jnp.broadcast_to(m, (SPAN, HEAD_DIM))
            l_ref[rows, :] = jnp.broadcast_to(l_new, (SPAN, HEAD_DIM))
        den = l_ref[...]
        ob = acc_ref[...] / den
        acc_ref[...] = ob
        m_ref[...] = m_ref[...] + jnp.log(den)
        rstd = lax.rsqrt(jnp.mean(ob * ob, axis=-1, keepdims=True) + EPS)
        mix_ref[...] = (ob * rstd * w_ref[...]).astype(BF16)

    col = lambda off: pl.BlockSpec((s, HEAD_DIM), lambda h: (0, off + h))
    return pl.pallas_call(
        body, name="attn_fwd", grid=(heads,),
        in_specs=[col(0), col(heads), col(2 * heads), pl.BlockSpec((1, HEAD_DIM), lambda h: (0, 0))],
        out_specs=[col(0), col(0), col(0)],
        out_shape=[jax.ShapeDtypeStruct((s, aw), BF16), jax.ShapeDtypeStruct((s, aw), F32),
                   jax.ShapeDtypeStruct((s, aw), F32)],
        scratch_shapes=[pltpu.VMEM((s, HEAD_DIM), F32)],
        compiler_params=_params(("parallel",)),
    )(qk, qk, aqkv, w)


def _attn_bwd(qk, aqkv, do, lse, dsum, heads):
    s = qk.shape[0]
    aw = heads * HEAD_DIM
    scale = HEAD_DIM ** -0.5

    def body(q_ref, k_ref, v_ref, do_ref, l_ref, ds_ref, out_ref):
        band, own = _band()
        dq_ref, dk_ref, dv_ref = out_ref.at[0], out_ref.at[1], out_ref.at[2]
        out_ref[...] = jnp.zeros((3, s, HEAD_DIM), F32)
        for _, d, r, g in _attn_blocks(s):
            rows = _sub(g, r, d)
            qs, dov = q_ref[rows, :] * scale, do_ref[rows, :]
            kc, vc = k_ref[rows, :], v_ref[rows, :]
            if g == 0:
                kp, vp, mask = kc, vc, band & own
            else:
                mask = band
            kcat = jnp.concatenate([kp, kc], axis=0)
            vcat = jnp.concatenate([vp, vc], axis=0)
            p = jnp.where(mask, jnp.exp(_dot(qs, kcat, "nt") - l_ref[rows, 0:1]), 0.0)
            dsc = p * (_dot(dov, vcat, "nt") - ds_ref[rows, 0:1])
            dq_ref[rows, :] += scale * _dot(dsc, kcat)
            dk = _dot(dsc, qs, "tn")
            dv = _dot(p, dov, "tn")
            if g > 0:
                dk_ref[prows, :] += dk_own + dk[:SPAN]
                dv_ref[prows, :] += dv_own + dv[:SPAN]
            dk_own, dv_own = dk[SPAN:], dv[SPAN:]
            if g == s // (SPAN * d) - 1:
                dk_ref[rows, :] += dk_own
                dv_ref[rows, :] += dv_own
            kp, vp, prows = kc, vc, rows

    col = lambda off: pl.BlockSpec((s, HEAD_DIM), lambda h: (0, off + h))
    return pl.pallas_call(
        body, name="attn_bwd", grid=(heads,),
        in_specs=[col(0), col(heads), col(2 * heads), col(0), col(0), col(0)],
        out_specs=pl.BlockSpec((3, s, HEAD_DIM), lambda h: (0, 0, h)),
        out_shape=jax.ShapeDtypeStruct((3, s, aw), F32),
        compiler_params=_params(("parallel",)),
    )(qk, qk, aqkv, do, lse, dsum)


def _attn_pre_bwd(grads, aqkv, wq, wk, cosf, sinf, heads):
    s = aqkv.shape[0]
    tr = _tile(s, 4096, 8)
    nrow = s // tr

    def body(g_ref, x_ref, wq_ref, wk_ref, c_ref, s_ref, dx_ref, dwq_ref, dwk_ref):
        i, j = pl.program_id(0), pl.program_id(1)
        kind = j // heads
        dout = g_ref[0]
        tot_v = dout
        dy = dout * c_ref[...] + pltpu.roll(dout * s_ref[...], HEAD_DIM // 2, 1)
        xv = x_ref[...]
        wv = jnp.where(kind == 0, wq_ref[...], wk_ref[...])
        rstd = lax.rsqrt(jnp.mean(xv * xv, axis=-1, keepdims=True) + EPS)
        xhat = xv * rstd
        gw = dy * wv
        dxn = rstd * (gw - xhat * jnp.mean(gw * xhat, axis=-1, keepdims=True))
        dx_ref[...] = jnp.where(kind == 2, tot_v, dxn).astype(BF16)
        dwc = jnp.sum(dy * xhat, axis=0, keepdims=True)

        @pl.when((i == 0) & (j == 0))
        def _():
            dwq_ref[...] = jnp.zeros_like(dwq_ref)
            dwk_ref[...] = jnp.zeros_like(dwk_ref)

        @pl.when(kind == 0)
        def _():
            dwq_ref[...] += dwc

        @pl.when(kind == 1)
        def _():
            dwk_ref[...] += dwc

    grad = pl.BlockSpec((1, tr, HEAD_DIM), lambda i, j: (j // heads, i, j % heads))
    blk = pl.BlockSpec((tr, HEAD_DIM), lambda i, j: (i, j))
    vec = pl.BlockSpec((1, HEAD_DIM), lambda i, j: (0, 0))
    tab = pl.BlockSpec((tr, HEAD_DIM), lambda i, j: (i, 0))
    return pl.pallas_call(
        body, name="attn_pre_bwd", grid=(nrow, 3 * heads),
        in_specs=[grad, blk, vec, vec, tab, tab], out_specs=[blk, vec, vec],
        out_shape=[jax.ShapeDtypeStruct((s, 3 * heads * HEAD_DIM), BF16),
                   jax.ShapeDtypeStruct((1, HEAD_DIM), F32), jax.ShapeDtypeStruct((1, HEAD_DIM), F32)],
        compiler_params=_params(("arbitrary", "arbitrary")),
    )(grads, aqkv, wq, wk, cosf, sinf)


def _colsum(a, name):
    s, d = a.shape
    tr = _tile(s, 4096, 8)

    def body(a_ref, o_ref):
        @pl.when(pl.program_id(0) == 0)
        def _():
            o_ref[...] = jnp.zeros_like(o_ref)

        o_ref[...] += jnp.sum(a_ref[...], axis=0, keepdims=True)

    return pl.pallas_call(
        body, name=name, grid=(s // tr,),
        in_specs=[pl.BlockSpec((tr, d), lambda i: (i, 0))], out_specs=pl.BlockSpec((1, d), lambda i: (0, 0)),
        out_shape=jax.ShapeDtypeStruct((1, d), F32),
        compiler_params=_params(("arbitrary",)),
    )(a)


def _local_step(x, positions, target, small, w_qkvz, w_bd, w_attn, conv_w, later_weights, ffn_grads_ready,
                rest_grads_ready, after=None):
    s, dmod = x.shape
    heads = dmod // (2 * HEAD_DIM)
    dw = heads * HEAD_DIM
    a_log, dt_bias = small["a_log"].reshape(heads, 1, 1), small["dt_bias"].reshape(heads, 1, 1)
    cosf, sinf = _rope_tables(positions, s)

    h1 = _rmsnorm_fwd(x, small["attn_norm_w"], "norm1_fwd", after=after)
    qkvz = _matmul(h1, w_qkvz, "nn", "proj_qkvz")
    bd = _matmul(h1, w_bd, "nn", "proj_bd")
    aqkv = _matmul(h1, w_attn, "nn", "proj_attn")
    dqkv = _delta_pre_fwd(qkvz, conv_w, heads)
    o_d, *saved = _delta_chunk_fwd(dqkv, bd, a_log, dt_bias, heads)
    mix_a = _delta_post_fwd(o_d, qkvz, small["delta_out_norm_w"], heads)
    qk_rot = _attn_pre_fwd(aqkv, small["q_norm_w"], small["k_norm_w"], cosf, sinf, heads)
    mix_b, ob, lse = _attn_fwd(qk_rot, aqkv, small["attn_out_norm_w"], heads)
    w_out, behind, ffn_weights = later_weights((mix_a, mix_b))
    x1 = _matmul(mix_a, w_out, "nn", "out_proj", add=x, b_rows=(0, dw), more=[(mix_b, w_out, (dw, dw))], after=behind)
    h2 = _rmsnorm_fwd(x1, small["ffn_norm_w"], "norm2_fwd")
    w_gu, w_down = ffn_weights(h2)
    ff = w_down.shape[0]

    def swiglu(accs, vals):
        g, u = accs
        return g, u, g * _sigmoid(g) * u

    gate, up, act = _matmul(h2, w_gu, "nn", "ffn_gate_up", b_cols=(0, ff), b2_cols=(ff, ff), finish=swiglu,
                            out_dtypes=[BF16, BF16, BF16])

    def loss_head(acc, vals):
        err = acc + vals[0] - vals[1]
        part = 0.5 * jnp.sum(jnp.sum(err * err, axis=-1, keepdims=True) * (1.0 / dmod), axis=0, keepdims=True)
        lane = lax.broadcasted_iota(jnp.int32, (1, HEAD_DIM), 1)
        return err * (1.0 / dmod), err * (1.0 / dmod), jnp.where(lane == 0, part, 0.0)

    dy, dy16, loss_row = _matmul(act, w_down, "nn", "ffn_down", tiles=[(x1, 0), (target, 0)], finish=loss_head,
                                 out_dtypes=[F32, BF16], row_sum=True)

    def swiglu_bwd(acc, vals):
        g, u = vals[0].astype(F32), vals[1].astype(F32)
        sg = _sigmoid(g)
        return acc * u * sg * (1.0 + g * (1.0 - sg)), acc * g * sg

    dgate, dup = _matmul(dy16, w_down, "nt", "ffn_down_dx", tiles=[(gate, 0), (up, 0)], finish=swiglu_bwd,
                         out_dtypes=[BF16, BF16])
    g_w_down = _matmul(act, dy16, "tn", "ffn_down_dw")
    g_w_gate = _matmul(h2, dgate, "tn", "ffn_gate_dw")
    g_w_up = _matmul(h2, dup, "tn", "ffn_up_dw")
    behind, and_then = ffn_grads_ready(g_w_gate, g_w_up, g_w_down)
    dh2 = _matmul(dgate, w_gu, "nt", "ffn_gate_dx", b_cols=(0, ff), after=behind)
    dh2 = _matmul(dup, w_gu, "nt", "ffn_up_dx", b_cols=(ff, ff), add=dh2)
    behind = and_then(dh2)
    dx1, dx1_16, g_ffn_norm = _rmsnorm_bwd(dh2, x1, small["ffn_norm_w"], dy, "norm2_bwd", after=behind)
    def per_head(fn, tile_w):
        outs = [fn(slice(h * HEAD_DIM, (h + 1) * HEAD_DIM)) for h in range(tile_w // HEAD_DIM)]
        joined = [jnp.concatenate([o[i] for o in outs], axis=1) for i in range(len(outs[0]) - 1)]
        return (*joined, sum(o[-1] for o in outs))

    def delta_post_bwd(acc, vals):
        def one(hs):
            d, ov, z, wv = acc[:, hs], vals[0][:, hs], vals[1][:, hs], vals[2][:, hs]
            sg = _sigmoid(z)
            rstd = lax.rsqrt(jnp.mean(ov * ov, axis=-1, keepdims=True) + EPS)
            ohat = ov * rstd
            dn = d * (z * sg)
            gw = dn * wv
            return (rstd * (gw - ohat * jnp.mean(gw * ohat, axis=-1, keepdims=True)),
                    d * (ohat * wv) * sg * (1.0 + z * (1.0 - sg)), jnp.sum(dn * ohat, axis=0, keepdims=True))
        return per_head(one, acc.shape[1])

    def attn_merge_bwd(acc, vals):
        def one(hs):
            d, ov, wv = acc[:, hs], vals[0][:, hs], vals[1][:, hs]
            rstd = lax.rsqrt(jnp.mean(ov * ov, axis=-1, keepdims=True) + EPS)
            ohat = ov * rstd
            gw = d * wv
            dov = rstd * (gw - ohat * jnp.mean(gw * ohat, axis=-1, keepdims=True))
            return (dov, jnp.broadcast_to(jnp.sum(dov * ov, axis=-1, keepdims=True), dov.shape),
                    jnp.sum(d * ohat, axis=0, keepdims=True))
        return per_head(one, acc.shape[1])

    rep = lambda wv: jnp.tile(wv, (1, heads))
    do_d, dz, g_delta_out_norm = _matmul(
        dx1_16, w_out, "nt", "out_proj_dx_a", b_rows=(0, dw), finish=delta_post_bwd, out_dtypes=[F32, BF16],
        tiles=[(o_d, 0), (qkvz, 3 * dw), (rep(small["delta_out_norm_w"]), 0)], row_sum=True)
    dob, dsum, g_attn_out_norm = _matmul(
        dx1_16, w_out, "nt", "out_proj_dx_b", b_rows=(dw, dw), finish=attn_merge_bwd, out_dtypes=[F32, F32],
        tiles=[(ob, 0), (rep(small["attn_out_norm_w"]), 0)], row_sum=True)
    g_w_out_a = _matmul(mix_a, dx1_16, "tn", "out_proj_dw_a")
    g_w_out_b = _matmul(mix_b, dx1_16, "tn", "out_proj_dw_b")
    grads = _attn_bwd(qk_rot, aqkv, dob, lse, dsum, heads)
    d_aqkv, g_q_norm, g_k_norm = _attn_pre_bwd(grads, aqkv, small["q_norm_w"], small["k_norm_w"], cosf, sinf, heads)
    ddqkv, dbd = _delta_chunk_bwd(do_d, dqkv, bd, saved, a_log, dt_bias, heads)
    d_qkv_raw, g_conv = _delta_pre_bwd(ddqkv, qkvz, conv_w, heads)
    bd_sums = _colsum(dbd, "bd_colsum")
    g_w_qkv = _matmul(h1, d_qkv_raw, "tn", "proj_qkv_dw")
    g_w_z = _matmul(h1, dz, "tn", "proj_z_dw")
    g_w_bd = _matmul(h1, dbd, "tn", "proj_bd_dw")
    g_w_attn = _matmul(h1, d_aqkv, "tn", "proj_attn_dw")
    behind, and_then = rest_grads_ready(dict(w_qkv=g_w_qkv, w_z=g_w_z, w_bd=g_w_bd, w_attn=g_w_attn,
                                             w_out_a=g_w_out_a, w_out_b=g_w_out_b))
    dh1 = _matmul(d_qkv_raw, w_qkvz, "nt", "proj_qkv_dx", b_cols=(0, 3 * dw), after=behind)
    dh1 = _matmul(dz, w_qkvz, "nt", "proj_z_dx", b_cols=(3 * dw, dw), add=dh1, after=and_then(dh1))
    dh1 = _matmul(d_aqkv, w_attn, "nt", "proj_attn_dx", add=dh1)
    dh1 = _matmul(dbd, w_bd, "nt", "proj_bd_dx", add=dh1)
    grad_x, _, g_attn_norm = _rmsnorm_bwd(dh1, x, small["attn_norm_w"], dx1, "norm1_bwd")
    small_grads = dict(
        attn_norm_w=g_attn_norm, a_log=bd_sums[:, 2 * heads:3 * heads], dt_bias=bd_sums[:, heads:2 * heads],
        delta_out_norm_w=g_delta_out_norm, q_norm_w=g_q_norm, k_norm_w=g_k_norm,
        attn_out_norm_w=g_attn_out_norm, ffn_norm_w=g_ffn_norm, conv_w=g_conv)
    return loss_row, grad_x, small_grads


def _adamw(w, g, m, v, name):
    r, c = w.shape
    tr = _tile(r, 256, 8)

    def body(w_ref, g_ref, m_ref, v_ref, d_ref, nm_ref, nv_ref):
        gv = g_ref[...]
        nm = ADAM_B1 * m_ref[...] + (1.0 - ADAM_B1) * gv
        nv = ADAM_B2 * v_ref[...] + (1.0 - ADAM_B2) * (gv * gv)
        m_hat = nm / (1.0 - ADAM_B1 ** ADAM_STEP)
        v_hat = nv / (1.0 - ADAM_B2 ** ADAM_STEP)
        d_ref[...] = -ADAM_LR * (m_hat / (jnp.sqrt(v_hat) + ADAM_EPS) + ADAM_WD * w_ref[...])
        nm_ref[...] = nm
        nv_ref[...] = nv

    blk = pl.BlockSpec((tr, c), lambda i: (i, 0))
    return pl.pallas_call(
        body, name=name, grid=(r // tr,),
        in_specs=[blk] * 4, out_specs=[blk] * 3,
        out_shape=[jax.ShapeDtypeStruct((r, c), F32)] * 3,
        compiler_params=_params(("parallel",)),
    )(w, g, m, v)


def _add_half_bf16(g, b, place, name):
    n, half, c = b.shape
    tr = _tile(half, 512, 16)
    nb = half // tr

    def body(place_ref, g_ref, b_ref, o_ref):
        o_ref[...] = (g_ref[...].astype(F32) + b_ref[...].astype(F32)).astype(BF16)

    blk = pl.BlockSpec((1, tr, c), lambda i, j, p: (i, j, 0))
    return pl.pallas_call(
        body, name=name,
        grid_spec=pltpu.PrefetchScalarGridSpec(
            num_scalar_prefetch=1, grid=(n, nb),
            in_specs=[pl.BlockSpec((1, tr, c), lambda i, j, p: (i, p[0] * nb + j, 0)), blk], out_specs=blk),
        out_shape=jax.ShapeDtypeStruct((n, half, c), BF16),
        compiler_params=_params(("parallel", "parallel")),
    )(place, g, b)


def _sum4_f32(mine, others, place, name):
    _, half, c = mine.shape
    tr = _tile(half, 512, 16)
    nb = half // tr

    def body(place_ref, a_ref, b_ref, o_ref):
        acc = a_ref[0].astype(F32)
        for j in range(3):
            acc = acc + b_ref[j].astype(F32)
        o_ref[...] = acc

    return pl.pallas_call(
        body, name=name,
        grid_spec=pltpu.PrefetchScalarGridSpec(
            num_scalar_prefetch=1, grid=(nb,),
            in_specs=[pl.BlockSpec((1, tr, c), lambda i, p: (p[1], i, 0)),
                      pl.BlockSpec((3, tr, c), lambda i, p: (0, i, 0))],
            out_specs=pl.BlockSpec((tr, c), lambda i, p: (p[0] * nb + i, 0))),
        out_shape=jax.ShapeDtypeStruct((2 * half, c), F32),
        compiler_params=_params(("parallel",)),
    )(place, mine, others)


def _place():
    x, y, c = lax.axis_index("x"), lax.axis_index("y"), lax.axis_index("c")
    other_chips = [(1 - x, y), (x, 1 - y), (1 - x, 1 - y)]
    return x, y, c, (x, y, 1 - c), other_chips


ANY = pl.BlockSpec(memory_space=pl.ANY)


def _remote(k, src, dst, to, send_sems, recv_sems):
    return pltpu.make_async_remote_copy(src_ref=src, dst_ref=dst, send_sem=send_sems.at[k], recv_sem=recv_sems.at[k],
                                        device_id=to, device_id_type=MESH)


def _half(ref, lead, hc):
    if lead is None:
        half = ref.shape[0] // 2
        return ref.at[pl.ds(hc * half, half), :]
    half = ref.shape[1] // 2
    return ref.at[lead, pl.ds(hc * half, half), :]


def _all_gather_weights(slots, whole, name):
    nt, nw = len(slots), len(whole)
    base_w, base_f, base_d = 2 * nt, 2 * nt + 3 * nw, 4 * nt + 3 * nw

    def quarter(ref, lead, hc, q):
        quart = ref.shape[1] // 4
        return ref.at[lead, pl.ds((2 * hc + q) * quart, quart), :]

    def body(*refs):
        ins, outs = refs[:nt + nw], refs[nt + nw:2 * (nt + nw)]
        sems = refs[2 * (nt + nw):]
        x, y, c, sibling, chips = _place()
        me, xn, yn, dg = 2 * x + y, 2 * (1 - x) + y, 2 * x + 1 - y, 2 * (1 - x) + 1 - y
        to_x, to_y = (1 - x, y, c), (x, 1 - y, c)
        cps = []
        for t in range(nt):
            cps.append(_remote(2 * t, _half(ins[t], me, c), _half(outs[t], me, c), to_x, *sems))
            cps.append(_remote(2 * t + 1, _half(ins[t], me, c), _half(outs[t], me, c), to_y, *sems))
        for j, (px, py) in enumerate(chips):
            for t in range(nw):
                cps.append(_remote(base_w + j * nw + t, ins[nt + t].at[me], outs[nt + t].at[me], (px, py, c), *sems))
        for cp in cps:
            cp.start()

        def start(k, ref, to):
            cp = _remote(k, ref, ref, to, *sems)
            cp.start()
            cps.append(cp)

        for t in range(nt):
            landed = _half(outs[t], xn, c)
            _remote(2 * t, landed, landed, to_x, *sems).wait_recv()
            start(base_f + 2 * t, quarter(outs[t], xn, c, 0), to_y)
            start(base_d + 3 * t, landed, sibling)
            landed = _half(outs[t], yn, c)
            _remote(2 * t + 1, landed, landed, to_y, *sems).wait_recv()
            start(base_f + 2 * t + 1, quarter(outs[t], yn, c, 1), to_x)
            start(base_d + 3 * t + 1, landed, sibling)
        for t in range(nt):
            q0, q1 = quarter(outs[t], dg, c, 0), quarter(outs[t], dg, c, 1)
            _remote(base_f + 2 * t, q0, q0, to_y, *sems).wait_recv()
            _remote(base_f + 2 * t + 1, q1, q1, to_x, *sems).wait_recv()
            start(base_d + 3 * t + 2, _half(outs[t], dg, c), sibling)
        for j, (px, py) in enumerate(chips):
            for t in range(nw):
                landed = outs[nt + t].at[2 * px + py]
                _remote(base_w + j * nw + t, landed, landed, (px, py, c), *sems).wait_recv()
        for t in range(nt):
            for j, chip in enumerate((xn, yn, dg)):
                other = _half(outs[t], chip, 1 - c)
                _remote(base_d + 3 * t + j, other, other, sibling, *sems).wait_recv()
        for cp in cps:
            cp.wait_send()

    arrays = list(slots) + list(whole)
    n_sem = 7 * nt + 3 * nw
    return pl.pallas_call(
        body, name=name, in_specs=[ANY] * len(arrays), out_specs=[ANY] * len(arrays),
        input_output_aliases={i: i for i in range(len(arrays))},
        out_shape=[jax.ShapeDtypeStruct(a.shape, a.dtype) for a in arrays],
        scratch_shapes=[pltpu.SemaphoreType.DMA((n_sem,)), pltpu.SemaphoreType.DMA((n_sem,))],
    )(*arrays)


def _join_halves(fs, name):
    nt = len(fs)

    def body(*refs):
        in_refs, out_refs, (send_sems, recv_sems) = refs[:nt], refs[nt:2 * nt], refs[2 * nt:]
        _, _, c, sibling, _ = _place()
        cps = [_remote(t, _half(in_refs[t], None, c), _half(out_refs[t], None, c), sibling, send_sems, recv_sems)
               for t in range(nt)]
        for cp in cps:
            cp.start()
        for t in range(nt):
            theirs = _half(out_refs[t], None, 1 - c)
            _remote(t, theirs, theirs, sibling, send_sems, recv_sems).wait_recv()
        for cp in cps:
            cp.wait_send()

    return pl.pallas_call(
        body, name=name, in_specs=[ANY] * nt, out_specs=[ANY] * nt,
        input_output_aliases={i: i for i in range(nt)},
        out_shape=[jax.ShapeDtypeStruct(f.shape, f.dtype) for f in fs],
        scratch_shapes=[pltpu.SemaphoreType.DMA((nt,)), pltpu.SemaphoreType.DMA((nt,))],
    )(*fs)


HBM = pl.BlockSpec(memory_space=pltpu.HBM)
SEM = pl.BlockSpec(memory_space=pltpu.SEMAPHORE)
EFFECT = pltpu.SideEffectType.DATAFLOW_SIDE_EFFECTING


def _split_start(arrays, after, plan, n_copies, name):
    na = len(arrays)

    def body(*refs):
        ins, send_sems, recv_sems = refs[:na], refs[na + 1], refs[na + 2]
        outs, token = refs[na + 3:2 * na + 3], refs[2 * na + 3]
        for k, (src, dst, to) in enumerate(plan(ins, outs)):
            _remote(k, src, dst, to, send_sems, recv_sems).start()
        token[...] = jnp.zeros_like(token)

    res = pl.pallas_call(
        body, name=name,
        out_shape=(pltpu.SemaphoreType.DMA((n_copies,)), pltpu.SemaphoreType.DMA((n_copies,)),
                   *[pltpu.HBM(a.shape, a.dtype) for a in arrays], jax.ShapeDtypeStruct((8, HEAD_DIM), F32)),
        in_specs=[HBM] * na + [ANY],
        out_specs=(SEM, SEM, *[HBM] * na, pl.BlockSpec(memory_space=pltpu.VMEM)),
        input_output_aliases={i: 2 + i for i in range(na)},
        compiler_params=pltpu.CompilerParams(has_side_effects=EFFECT),
    )(*[pltpu.with_memory_space_constraint(a, pltpu.HBM) for a in arrays], after)
    return res[0], res[1], list(res[2:2 + na]), res[2 + na]


def _split_wait(send_sems, recv_sems, arrays, after, plan, name):
    na = len(arrays)
    after = list(after) if isinstance(after, (list, tuple)) else [after]

    def body(*refs):
        ins, send, recv, outs = refs[:na], refs[na], refs[na + 1], refs[na + 2 + len(after):]
        for k, (src, dst, to) in enumerate(plan(ins, outs)):
            cp = _remote(k, src, dst, to, send, recv)
            cp.wait_send()
            cp.wait_recv()

    res = pl.pallas_call(
        body, name=name, out_shape=tuple(pltpu.HBM(a.shape, a.dtype) for a in arrays),
        in_specs=[HBM] * na + [SEM, SEM] + [ANY] * len(after), out_specs=tuple([HBM] * na),
        input_output_aliases={i: i for i in range(na)},
        compiler_params=pltpu.CompilerParams(has_side_effects=EFFECT),
    )(*arrays, send_sems, recv_sems, *after)
    return list(res)


def _gather_plan(nt):
    def plan(ins, outs):
        x, y, c, _, chips = _place()
        me = 2 * x + y
        return [(_half(ins[t], me, c), _half(outs[t], me, c), (px, py, c)) for px, py in chips for t in range(nt)]
    return plan


def _gather_landed_plan(nt):
    def plan(ins, outs):
        _, _, c, _, chips = _place()
        return [(_half(outs[t], 2 * px + py, c), _half(outs[t], 2 * px + py, c), (px, py, c))
                for px, py in chips for t in range(nt)]
    return plan


def _scatter_plan(nt):
    def plan(ins, outs):
        _, _, c, _, chips = _place()
        return [(ins[t].at[2 * px + py], outs[nt + t].at[j], (px, py, c))
                for j, (px, py) in enumerate(chips) for t in range(nt)]
    return plan


def _pass_plan(nt):
    def plan(ins, outs):
        _, _, c, sibling, chips = _place()
        return [(_half(ins[t], 2 * px + py, c), _half(outs[t], 2 * px + py, c), sibling)
                for px, py in chips for t in range(nt)]
    return plan


def _pass_landed_plan(nt):
    def plan(ins, outs):
        _, _, c, sibling, chips = _place()
        return [(_half(outs[t], 2 * px + py, c), _half(outs[t], 2 * px + py, 1 - c), sibling)
                for px, py in chips for t in range(nt)]
    return plan


def _swap_plan(nt):
    def plan(ins, outs):
        _, _, c, sibling, _ = _place()
        res = []
        for t in range(nt):
            half = ins[t].shape[1] // 2
            res.append((ins[t].at[:, pl.ds((1 - c) * half, half), :], outs[nt + t], sibling))
        return res
    return plan


def _pass_halves_to_sibling(slots, name):
    nt = len(slots)

    def body(*refs):
        ins, outs, (send_sems, recv_sems) = refs[:nt], refs[nt:2 * nt], refs[2 * nt:]
        _, _, c, sibling, chips = _place()
        cps = [_remote(j * nt + t, _half(ins[t], 2 * px + py, c), _half(outs[t], 2 * px + py, c), sibling,
                       send_sems, recv_sems)
               for j, (px, py) in enumerate(chips) for t in range(nt)]
        for cp in cps:
            cp.start()
        for j, (px, py) in enumerate(chips):
            for t in range(nt):
                other = _half(outs[t], 2 * px + py, 1 - c)
                _remote(j * nt + t, other, other, sibling, send_sems, recv_sems).wait_recv()
        for cp in cps:
            cp.wait_send()

    return pl.pallas_call(
        body, name=name, in_specs=[ANY] * nt, out_specs=[ANY] * nt,
        input_output_aliases={i: i for i in range(nt)},
        out_shape=[jax.ShapeDtypeStruct(a.shape, a.dtype) for a in slots],
        scratch_shapes=[pltpu.SemaphoreType.DMA((3 * nt,)), pltpu.SemaphoreType.DMA((3 * nt,))],
    )(*slots)


def _all_reduce_small(v):
    r, lanes = v.shape

    def body(v_ref, out_ref, buf, send_sems, recv_sems):
        x, y, c, sibling, chips = _place()

        def slot(px, py, pc):
            return buf.at[4 * px + 2 * py + pc]

        def copy(k, block, to, src=None):
            return pltpu.make_async_remote_copy(src_ref=slot(*block) if src is None else src, dst_ref=slot(*block),
                                                send_sem=send_sems.at[k], recv_sem=recv_sems.at[k],
                                                device_id=to, device_id_type=MESH)

        me = (x, y, c)
        buf[4 * x + 2 * y + c] = v_ref[...]
        first = [copy(0, me, sibling, src=v_ref)]
        first += [copy(1 + j, me, (*chip, c), src=v_ref) for j, chip in enumerate(chips)]
        for cp in first:
            cp.start()
        passed = [copy(4 + j, (*chip, c), sibling) for j, chip in enumerate(chips)]
        for j, chip in enumerate(chips):
            copy(1 + j, (*chip, c), me).wait_recv()
            passed[j].start()
        copy(0, (x, y, 1 - c), me).wait_recv()
        for j, chip in enumerate(chips):
            copy(4 + j, (*chip, 1 - c), me).wait_recv()
        for cp in first + passed:
            cp.wait_send()
        acc = buf[0]
        for k in range(1, 8):
            acc = acc + buf[k]
        out_ref[...] = acc

    vmem = pl.BlockSpec(memory_space=pltpu.VMEM)
    return pl.pallas_call(
        body, name="all_reduce_small", in_specs=[vmem], out_specs=vmem,
        out_shape=jax.ShapeDtypeStruct((r, lanes), F32),
        scratch_shapes=[pltpu.VMEM((8, r, lanes), F32), pltpu.SemaphoreType.DMA((7,)), pltpu.SemaphoreType.DMA((7,))],
    )(v)


def _size(shape):
    n = 1
    for d in shape:
        n *= d
    return n


def _regroup_cols(parts, widths):
    out, start = [], 0
    for width in widths:
        pieces, lo = [], 0
        for p in parts:
            a, b = max(start, lo), min(start + width, lo + p.shape[1])
            if a < b:
                pieces.append(p[:, a - lo:b - lo])
            lo += p.shape[1]
        out.append(pieces[0] if len(pieces) == 1 else jnp.concatenate(pieces, axis=1))
        start += width
    return out


def _pack_small(parts):
    rows = []
    for p in parts:
        f = p.reshape(-1).astype(F32)
        n = -(-f.shape[0] // HEAD_DIM) * HEAD_DIM
        rows.append(jnp.pad(f, (0, n - f.shape[0])).reshape(-1, HEAD_DIM))
    a = jnp.concatenate(rows, axis=0)
    return jnp.pad(a, ((0, -a.shape[0] % 8), (0, 0)))


def _unpack_small(a, shapes):
    out, row = [], 0
    for shp in shapes:
        nrows = -(-_size(shp) // HEAD_DIM)
        out.append(a[row:row + nrows].reshape(-1)[:_size(shp)].reshape(shp))
        row += nrows
    return out


SMALL = ["attn_norm_w", "a_log", "dt_bias", "delta_out_norm_w", "q_norm_w", "k_norm_w", "attn_out_norm_w", "ffn_norm_w"]
BIG = ["w_in", "w_out", "w_gate_up", "w_down"]
ORDER = ["attn_norm_w", "w_in", "conv_w", "a_log", "dt_bias", "delta_out_norm_w", "q_norm_w", "k_norm_w",
         "attn_out_norm_w", "w_out", "ffn_norm_w", "w_gate_up", "w_down"]


def kernel(x, positions, attn_norm_w, w_in, conv_w, a_log, dt_bias, delta_out_norm_w, q_norm_w, k_norm_w, attn_out_norm_w, w_out, ffn_norm_w, w_gate_up, w_down, loss_target, m_attn_norm_w, m_w_in, m_conv_w, m_a_log, m_dt_bias, m_delta_out_norm_w, m_q_norm_w, m_k_norm_w, m_attn_out_norm_w, m_w_out, m_ffn_norm_w, m_w_gate_up, m_w_down, v_attn_norm_w, v_w_in, v_conv_w, v_a_log, v_dt_bias, v_delta_out_norm_w, v_q_norm_w, v_k_norm_w, v_attn_out_norm_w, v_w_out, v_ffn_norm_w, v_w_gate_up, v_w_down):
    wts = dict(attn_norm_w=attn_norm_w, w_in=w_in, conv_w=conv_w, a_log=a_log, dt_bias=dt_bias,
               delta_out_norm_w=delta_out_norm_w, q_norm_w=q_norm_w, k_norm_w=k_norm_w,
               attn_out_norm_w=attn_out_norm_w, w_out=w_out, ffn_norm_w=ffn_norm_w, w_gate_up=w_gate_up, w_down=w_down)
    mom = dict(attn_norm_w=m_attn_norm_w, w_in=m_w_in, conv_w=m_conv_w, a_log=m_a_log, dt_bias=m_dt_bias,
               delta_out_norm_w=m_delta_out_norm_w, q_norm_w=m_q_norm_w, k_norm_w=m_k_norm_w,
               attn_out_norm_w=m_attn_out_norm_w, w_out=m_w_out, ffn_norm_w=m_ffn_norm_w, w_gate_up=m_w_gate_up,
               w_down=m_w_down)
    var = dict(attn_norm_w=v_attn_norm_w, w_in=v_w_in, conv_w=v_conv_w, a_log=v_a_log, dt_bias=v_dt_bias,
               delta_out_norm_w=v_delta_out_norm_w, q_norm_w=v_q_norm_w, k_norm_w=v_k_norm_w,
               attn_out_norm_w=v_attn_out_norm_w, w_out=v_w_out, ffn_norm_w=v_ffn_norm_w, w_gate_up=v_w_gate_up,
               w_down=v_w_down)
    dmod = x.shape[2]
    heads = dmod // (2 * HEAD_DIM)
    dw = heads * HEAD_DIM
    chip = 2 * lax.axis_index("x") + lax.axis_index("y")
    core = lax.axis_index("c")
    n_in, n_out, n_gu, n_down, n_conv = (w_in.shape[2], w_out.shape[1], w_gate_up.shape[2], w_down.shape[1],
                                         conv_w.shape[2])

    def slots_of(w, dtype):
        shard = w[0].astype(dtype)
        return lax.dynamic_update_index_in_dim(lax.empty((4,) + shard.shape, dtype), shard, chip, axis=0)

    s_in, s_conv = _all_gather_weights([slots_of(w_in, BF16)], [slots_of(conv_w, F32)], "all_gather_w_in")
    later = [slots_of(w_out, BF16), slots_of(w_gate_up, BF16), slots_of(w_down, BF16)]
    w_send, w_recv, later, started = _split_start(later, s_conv, _gather_plan(3), 9, "gather_rest_start")
    by_cols = lambda a: a.transpose(1, 0, 2).reshape(a.shape[1], 4 * a.shape[2])
    conv_f = by_cols(s_conv)
    w_qkvz, w_bd, w_attn = _regroup_cols([s_in[i] for i in range(4)], [4 * dw, 2 * heads, 3 * dw])
    w_bd = jnp.pad(w_bd, ((0, 0), (0, HEAD_DIM - 2 * heads)))
    small = {n: wts[n] for n in SMALL}
    place = jnp.stack([core, chip]).astype(jnp.int32)
    to_slots = lambda a: a.reshape(a.shape[0], 4, a.shape[1] // 4).transpose(1, 0, 2)

    def later_weights(after):
        landed = _split_wait(w_send, w_recv, later, after, _gather_landed_plan(3), "gather_rest_wait")
        s_out, = _pass_halves_to_sibling(landed[:1], "gather_out_pass")
        p_send, p_recv, passing, token = _split_start(landed[1:], s_out, _pass_plan(2), 6, "gather_ffn_pass_start")

        def ffn_weights(after):
            s_gu, s_down = _split_wait(p_send, p_recv, passing, after, _pass_landed_plan(2), "gather_ffn_pass_wait")
            return by_cols(s_gu), s_down.reshape(4 * n_down, dmod)

        return s_out.reshape(4 * n_out, dmod), token, ffn_weights

    ffn = {}

    def ffn_grads_ready(g_gate, g_up, g_down):
        gs = [to_slots(jnp.concatenate([g_gate, g_up], axis=1)), g_down.reshape(4, n_down, dmod)]
        zones = [lax.empty((4, g.shape[1] // 2, g.shape[2]), BF16) for g in gs]
        s_send, s_recv, s_bufs, swapping = _split_start(gs + zones, g_gate, _swap_plan(2), 2, "swap_ffn_start")

        def and_then(after):
            g_gu, g_dn, b_gu, b_dn = _split_wait(s_send, s_recv, s_bufs, after, _swap_plan(2), "swap_ffn_wait")
            sums = [_add_half_bf16(g_gu, b_gu, place, "chip_partial_sum_w_gate_up"),
                    _add_half_bf16(g_dn, b_dn, place, "chip_partial_sum_w_down")]
            zones3 = [lax.empty((3,) + p.shape[1:], BF16) for p in sums]
            ffn["send"], ffn["recv"], ffn["bufs"], token = _split_start(sums + zones3, b_gu, _scatter_plan(2), 6,
                                                                        "scatter_ffn_start")
            return token

        return swapping, and_then

    rest = {}

    def rest_grads_ready(bg):
        gs = [jnp.stack(_regroup_cols([bg["w_qkv"], bg["w_z"], bg["w_bd"][:, :2 * heads], bg["w_attn"]], [n_in] * 4)),
              jnp.concatenate([bg["w_out_a"], bg["w_out_b"]], axis=0).reshape(4, n_out, dmod)]
        zones = [lax.empty((4, g.shape[1] // 2, g.shape[2]), BF16) for g in gs]
        s_send, s_recv, s_bufs, swapping = _split_start(gs + zones, bg["w_attn"], _swap_plan(2), 2, "swap_rest_start")

        def and_then(after):
            g_in, g_out, b_in, b_out = _split_wait(s_send, s_recv, s_bufs, after, _swap_plan(2), "swap_rest_wait")
            sums = [_add_half_bf16(g_in, b_in, place, "chip_partial_sum_w_in"),
                    _add_half_bf16(g_out, b_out, place, "chip_partial_sum_w_out")]
            zones3 = [lax.empty((3,) + p.shape[1:], BF16) for p in sums]
            rest["send"], rest["recv"], rest["bufs"], token = _split_start(sums + zones3, b_in, _scatter_plan(2), 6,
                                                                           "scatter_rest_start")
            return token

        return swapping, and_then

    loss_row, grad_x, sg = _local_step(
        x[0], positions[0], loss_target[0], small, w_qkvz, w_bd, w_attn, conv_f,
        later_weights, ffn_grads_ready, rest_grads_ready, after=started)

    r_send, r_recv, r_bufs = rest["send"], rest["recv"], rest["bufs"]
    sum_gu, sum_down, got_gu, got_down = _split_wait(ffn["send"], ffn["recv"], ffn["bufs"], grad_x,
                                                     _scatter_plan(2), "scatter_ffn_wait")
    g_big = dict(zip(["w_gate_up", "w_down"], _join_halves(
        [_sum4_f32(sum_gu, got_gu, place, "grad_total_w_gate_up"),
         _sum4_f32(sum_down, got_down, place, "grad_total_w_down")], "join_ffn_halves")))
    grads, deltas, new_m, new_v = {}, {}, {}, {}

    def adamw_big(n):
        shp = wts[n].shape
        d, nm, nv = _adamw(wts[n][0], g_big[n], mom[n][0], var[n][0], "adamw_" + n)
        grads[n], deltas[n], new_m[n], new_v[n] = g_big[n].reshape(shp), d.reshape(shp), nm.reshape(shp), nv.reshape(shp)
        return d

    done = [adamw_big("w_gate_up"), adamw_big("w_down")]
    sum_in, sum_out, got_in, got_out = _split_wait(r_send, r_recv, r_bufs, done, _scatter_plan(2), "scatter_rest_wait")
    g_big.update(zip(["w_in", "w_out"], _join_halves(
        [_sum4_f32(sum_in, got_in, place, "grad_total_w_in"),
         _sum4_f32(sum_out, got_out, place, "grad_total_w_out")], "join_rest_halves")))
    adamw_big("w_in")
    adamw_big("w_out")

    reduced = _all_reduce_small(_pack_small([sg[n] for n in SMALL] + [sg["conv_w"], loss_row]))
    red = _unpack_small(reduced, [wts[n].shape for n in SMALL] + [(4, 4 * n_conv), (1, HEAD_DIM)])
    g_small = dict(zip(SMALL, red[:len(SMALL)]))
    g_conv_full, loss_out = red[len(SMALL)], red[len(SMALL) + 1]
    g_small["conv_w"] = lax.dynamic_slice_in_dim(g_conv_full, chip * n_conv, n_conv, axis=1).reshape(conv_w.shape)

    names = SMALL + ["conv_w"]
    shapes = [wts[n].shape for n in names]
    d, nm, nv = _adamw(_pack_small([wts[n] for n in names]), _pack_small([g_small[n] for n in names]),
                       _pack_small([mom[n] for n in names]), _pack_small([var[n] for n in names]), "adamw_small")
    for n, dd, mm, vv in zip(names, _unpack_small(d, shapes), _unpack_small(nm, shapes), _unpack_small(nv, shapes)):
        grads[n], deltas[n], new_m[n], new_v[n] = g_small[n], dd, mm, vv
    return (loss_out[0, 0], grad_x[None], *[grads[n] for n in ORDER], *[deltas[n] for n in ORDER],
            *[new_m[n] for n in ORDER], *[new_v[n] for n in ORDER])
```

```python
import jax
import jax.numpy as jnp
from jax import lax
from jax.experimental import pallas as pl
from jax.experimental.pallas import tpu as pltpu

F32 = jnp.float32
BF16 = jnp.bfloat16
HEAD_DIM = 128
CHUNK = 128
INV_BLOCK = 64
SPAN = 128
DILATIONS = (1, 4, 16)
ROPE_THETA = 10000.0
EPS = 1e-6
NEG = -1e30
ADAM_LR, ADAM_B1, ADAM_B2, ADAM_EPS, ADAM_WD, ADAM_STEP = 0.001, 0.9, 0.999, 1e-08, 0.01, 10
VMEM_LIMIT = 48 * 1024 * 1024
MATMUL_VMEM = 44 * 1024 * 1024
MESH = pl.DeviceIdType.MESH

_DN = {"nn": (((1,), (0,)), ((), ())), "nt": (((1,), (1,)), ((), ())), "tn": (((0,), (0,)), ((), ()))}


def _dot(a, b, mode="nn"):
    (ca, cb), _ = _DN[mode]
    if a.ndim == 3:
        dn = (((ca[0] + 1,), (cb[0] + 1,)), ((0,), (0,)))
    else:
        dn = _DN[mode]
    return lax.dot_general(a.astype(BF16), b.astype(BF16), dn, preferred_element_type=F32)


def _rsum(x):
    return jnp.sum(x, axis=-1, keepdims=True)


def _csum(x):
    return jnp.sum(x, axis=-2, keepdims=True)


def _tile(dim, pref, unit=128):
    t = (min(pref, dim) // unit) * unit
    while t >= unit:
        if dim % t == 0:
            return t
        t -= unit
    return dim


def _params(sem):
    return pltpu.CompilerParams(dimension_semantics=sem, vmem_limit_bytes=VMEM_LIMIT)


def _sigmoid(x):
    return 1.0 / (1.0 + jnp.exp(-x))


def _matmul(a, b, mode, name, add=None, out_dtype=F32, a_cols=None, b_cols=None, b_rows=None,
            tiles=(), finish=None, out_dtypes=(), row_sum=False, after=None, b2_cols=None, more=()):
    if mode == "tn":
        out_dtype = BF16
    a_off, a_w = a_cols if a_cols else (0, a.shape[1])
    b_off, b_w = b_cols if b_cols else (0, b.shape[1])
    br_off, br_n = b_rows if b_rows else (0, b.shape[0])
    if mode == "nn":
        m, k, n = a.shape[0], a_w, b_w
        assert br_n == k
    elif mode == "nt":
        m, k, n = a.shape[0], a_w, br_n
        assert b_w == k
    else:
        k, m, n = a.shape[0], a_w, b_w
        assert b.shape[0] == k
    if finish is None:
        out_dtypes = [out_dtype]
        if add is None:
            finish = lambda acc, vals: (acc,)
        else:
            tiles = [(add, 0)]
            finish = lambda acc, vals: (acc + vals[0].astype(F32),)
    n_b = 1 if b2_cols is None else 2
    sides = sum(jnp.dtype(t.dtype).itemsize for t, _ in tiles) + sum(jnp.dtype(d).itemsize for d in out_dtypes)

    def vmem(tm_, tn_, tk_):
        ops = tm_ * tk_ * jnp.dtype(a.dtype).itemsize + n_b * tk_ * tn_ * jnp.dtype(b.dtype).itemsize
        for a_i, b_i, (_, k_i) in more:
            ops += tm_ * k_i * jnp.dtype(a_i.dtype).itemsize + k_i * tn_ * jnp.dtype(b_i.dtype).itemsize
        return 2 * (ops + tm_ * tn_ * sides) + (n_b * tm_ * tn_ * 4 if tk_ < k else 0)

    tm, tn = _tile(m, 1024, 128), _tile(n, 1024, 128)
    tall = _tile(m, 1536, 128), _tile(n, 512, 128)
    if tall[0] * tall[1] > tm * tn and vmem(*tall, k) <= MATMUL_VMEM:
        tm, tn = tall
    while more and tn > 128 and vmem(tm, tn, k) > MATMUL_VMEM * 3 // 4:
        tn //= 2
    tk = next(t for t in [_tile(k, p, 128) for p in (4096, 2048, 1024, 512, 128)] if vmem(tm, tn, t) <= MATMUL_VMEM)
    if mode == "nn":
        assert a_off % tk == 0 and b_off % tn == 0 and br_off % tk == 0
        a_spec = pl.BlockSpec((tm, tk), lambda i, j, kk: (i, kk + a_off // tk))
        b_spec = pl.BlockSpec((tk, tn), lambda i, j, kk: (kk + br_off // tk, j + b_off // tn))
    elif mode == "nt":
        assert a_off % tk == 0 and b_off % tk == 0 and br_off % tn == 0
        a_spec = pl.BlockSpec((tm, tk), lambda i, j, kk: (i, kk + a_off // tk))
        b_spec = pl.BlockSpec((tn, tk), lambda i, j, kk: (j + br_off // tn, kk + b_off // tk))
    else:
        assert a_off % tm == 0 and b_off % tn == 0
        a_spec = pl.BlockSpec((tk, tm), lambda i, j, kk: (kk, i + a_off // tm))
        b_spec = pl.BlockSpec((tk, tn), lambda i, j, kk: (kk, j + b_off // tn))
    nk = k // tk
    assert not more or (nk == 1 and n_b == 1 and mode in ("nn", "nt"))
    n_tiles, n_out = len(tiles), len(out_dtypes)
    extra = [] if after is None else [after]
    first_tile_ref = 1 + n_b + 2 * len(more)
    first_out = first_tile_ref + n_tiles + len(extra)

    def body(*refs):
        a_ref, b_refs = refs[0], refs[1:1 + n_b]
        tile_refs, out_refs = refs[first_tile_ref:first_tile_ref + n_tiles], refs[first_out:first_out + n_out]
        acc_refs = refs[-n_b:] if nk > 1 else ()
        kk = pl.program_id(2)
        first_tile = (pl.program_id(0) == 0) & (pl.program_id(1) == 0)

        av = a_ref[...]
        if nk > 1:
            @pl.when(kk == 0)
            def _():
                for acc_ref in acc_refs:
                    acc_ref[...] = jnp.zeros_like(acc_ref)

            for acc_ref, b_ref in zip(acc_refs, b_refs):
                acc_ref[...] += _dot(av, b_ref[...], mode)

        @pl.when(kk == nk - 1)
        def _():
            if nk > 1:
                accs = [acc_ref[...] for acc_ref in acc_refs]
            else:
                accs = [_dot(av, b_ref[...], mode) for b_ref in b_refs]
                for p_i in range(len(more)):
                    accs[0] = accs[0] + _dot(refs[1 + n_b + 2 * p_i][...], refs[2 + n_b + 2 * p_i][...], mode)
            res = finish(accs[0] if n_b == 1 else accs, [t[...] for t in tile_refs])
            for o_ref, r in zip(out_refs, res):
                o_ref[...] = r.astype(o_ref.dtype)
            if row_sum:
                row_ref = refs[first_out + n_out]

                @pl.when(first_tile)
                def _():
                    row_ref[...] = res[n_out]

                @pl.when(jnp.logical_not(first_tile))
                def _():
                    row_ref[...] += res[n_out]

    in_specs = [a_spec, b_spec]
    args = [a, b]
    if b2_cols is not None:
        assert mode == "nn" and b2_cols[1] == n and b2_cols[0] % tn == 0
        in_specs.append(pl.BlockSpec((tk, tn), lambda i, j, kk: (kk + br_off // tk, j + b2_cols[0] // tn)))
        args.append(b)
    for a_i, b_i, (start, k_i) in more:
        assert a_i.shape == (m, k_i) and start % k_i == 0
        in_specs.append(pl.BlockSpec((tm, k_i), lambda i, j, kk: (i, 0)))
        if mode == "nn":
            in_specs.append(pl.BlockSpec((k_i, tn), lambda i, j, kk, s_=start // k_i: (s_, j)))
        else:
            in_specs.append(pl.BlockSpec((tn, k_i), lambda i, j, kk, s_=start // k_i: (j, s_)))
        args += [a_i, b_i]
    for arr, off in tiles:
        assert off % tn == 0
        if arr.shape[0] == 1:
            in_specs.append(pl.BlockSpec((1, tn), lambda i, j, kk, off=off: (0, j + off // tn)))
        else:
            in_specs.append(pl.BlockSpec((tm, tn), lambda i, j, kk, off=off: (i, j + off // tn)))
        args.append(arr)
    in_specs += [ANY] * len(extra)
    args += extra
    out_specs = [pl.BlockSpec((tm, tn), lambda i, j, kk: (i, j))] * n_out
    out_shape = [jax.ShapeDtypeStruct((m, n), dt) for dt in out_dtypes]
    if row_sum:
        out_specs.append(pl.BlockSpec((1, HEAD_DIM), lambda i, j, kk: (0, 0)))
        out_shape.append(jax.ShapeDtypeStruct((1, HEAD_DIM), F32))
    res = pl.pallas_call(
        body, name=name, grid=(m // tm, n // tn, nk),
        in_specs=in_specs, out_specs=out_specs, out_shape=out_shape,
        scratch_shapes=[pltpu.VMEM((tm, tn), F32)] * (n_b if nk > 1 else 0),
        compiler_params=_params(("arbitrary",) * 3 if row_sum else ("parallel", "parallel", "arbitrary")),
    )(*args)
    return res[0] if len(res) == 1 else res


def _rmsnorm_fwd(x, w, name, after=None):
    s, d = x.shape
    tr = _tile(s, 1024, 8)

    def body(x_ref, w_ref, *rest):
        h_ref = rest[-1]
        xv = x_ref[...]
        rstd = lax.rsqrt(jnp.mean(xv * xv, axis=-1, keepdims=True) + EPS)
        h_ref[...] = (xv * rstd * w_ref[...]).astype(BF16)

    extra = [] if after is None else [after]
    return pl.pallas_call(
        body, name=name, grid=(s // tr,),
        in_specs=[pl.BlockSpec((tr, d), lambda i: (i, 0)), pl.BlockSpec((1, d), lambda i: (0, 0))] + [ANY] * len(extra),
        out_specs=pl.BlockSpec((tr, d), lambda i: (i, 0)),
        out_shape=jax.ShapeDtypeStruct((s, d), BF16),
        compiler_params=_params(("parallel",)),
    )(x, w, *extra)


def _rmsnorm_bwd(dh, x, w, res, name, after=None):
    s, d = x.shape
    tr = _tile(s, 256, 8)

    def body(dh_ref, x_ref, w_ref, res_ref, *rest):
        dx_ref, dx16_ref, dw_ref = rest[-3:]
        xv = x_ref[...]
        rstd = lax.rsqrt(jnp.mean(xv * xv, axis=-1, keepdims=True) + EPS)
        xhat = xv * rstd
        dhv = dh_ref[...]
        gw = dhv * w_ref[...]
        dx = res_ref[...] + rstd * (gw - xhat * jnp.mean(gw * xhat, axis=-1, keepdims=True))
        dx_ref[...] = dx
        dx16_ref[...] = dx.astype(BF16)

        @pl.when(pl.program_id(0) == 0)
        def _():
            dw_ref[...] = jnp.zeros_like(dw_ref)

        dw_ref[...] += jnp.sum(dhv * xhat, axis=0, keepdims=True)

    row = pl.BlockSpec((tr, d), lambda i: (i, 0))
    vec = pl.BlockSpec((1, d), lambda i: (0, 0))
    extra = [] if after is None else [after]
    return pl.pallas_call(
        body, name=name, grid=(s // tr,),
        in_specs=[row, row, vec, row] + [ANY] * len(extra), out_specs=[row, row, vec],
        out_shape=[jax.ShapeDtypeStruct((s, d), F32), jax.ShapeDtypeStruct((s, d), BF16),
                   jax.ShapeDtypeStruct((1, d), F32)],
        compiler_params=_params(("arbitrary",)),
    )(dh, x, w, res, *extra)


def _conv_taps(x, w, rows):
    shifted = [x]
    for sft in (1, 2, 3):
        shifted.append(jnp.where(rows >= sft, pltpu.roll(x, sft, 0), 0.0))
    y = w[3:4, :] * shifted[0] + w[2:3, :] * shifted[1] + w[1:2, :] * shifted[2] + w[0:1, :] * shifted[3]
    return y, shifted


def _delta_pre_fwd(qkvz, conv_w, heads):
    s = qkvz.shape[0]
    nblk = 3 * heads

    def body(x_ref, w_ref, o_ref):
        part = pl.program_id(0) // heads
        rows = lax.broadcasted_iota(jnp.int32, (s, HEAD_DIM), 0)
        y, _ = _conv_taps(x_ref[...], w_ref[...], rows)
        a = y * _sigmoid(y)
        rs = lax.rsqrt(jnp.sum(a * a, axis=-1, keepdims=True) + EPS)
        fac = jnp.where(part == 0, rs * (HEAD_DIM ** -0.5), jnp.where(part == 1, rs, 1.0))
        o_ref[...] = a * fac

    return pl.pallas_call(
        body, name="delta_pre_fwd", grid=(nblk,),
        in_specs=[pl.BlockSpec((s, HEAD_DIM), lambda i: (0, i)), pl.BlockSpec((4, HEAD_DIM), lambda i: (0, i))],
        out_specs=pl.BlockSpec((s, HEAD_DIM), lambda i: (0, i)),
        out_shape=jax.ShapeDtypeStruct((s, 3 * heads * HEAD_DIM), F32),
        compiler_params=_params(("parallel",)),
    )(qkvz, conv_w)


def _delta_pre_bwd(dqkv, qkvz, conv_w, heads):
    s = qkvz.shape[0]
    nblk = 3 * heads

    def body(d_ref, x_ref, w_ref, dx_ref, dw_ref):
        part = pl.program_id(0) // heads
        rows = lax.broadcasted_iota(jnp.int32, (s, HEAD_DIM), 0)
        w = w_ref[...]
        y, shifted = _conv_taps(x_ref[...], w, rows)
        sg = _sigmoid(y)
        a = y * sg
        rs = lax.rsqrt(jnp.sum(a * a, axis=-1, keepdims=True) + EPS)
        unit = a * rs
        dn = d_ref[...]
        scale = jnp.where(part == 0, HEAD_DIM ** -0.5, 1.0)
        da_norm = scale * rs * (dn - unit * jnp.sum(dn * unit, axis=-1, keepdims=True))
        da = jnp.where(part < 2, da_norm, dn)
        dy = da * sg * (1.0 + y * (1.0 - sg))
        dx = w[3:4, :] * dy
        for sft in (1, 2, 3):
            dx = dx + w[3 - sft:4 - sft, :] * jnp.where(rows < s - sft, pltpu.roll(dy, s - sft, 0), 0.0)
        dx_ref[...] = dx.astype(BF16)
        for sft in range(4):
            dw_ref[3 - sft:4 - sft, :] = jnp.sum(dy * shifted[sft], axis=0, keepdims=True)

    col = pl.BlockSpec((s, HEAD_DIM), lambda i: (0, i))
    wsp = pl.BlockSpec((4, HEAD_DIM), lambda i: (0, i))
    return pl.pallas_call(
        body, name="delta_pre_bwd", grid=(nblk,),
        in_specs=[col, col, wsp], out_specs=[col, wsp],
        out_shape=[jax.ShapeDtypeStruct((s, 3 * heads * HEAD_DIM), BF16),
                   jax.ShapeDtypeStruct((4, 3 * heads * HEAD_DIM), F32)],
        compiler_params=_params(("parallel",)),
    )(dqkv, qkvz, conv_w)


def _heads_of(ref, heads):
    return jnp.stack([ref[:, h * HEAD_DIM:(h + 1) * HEAD_DIM] for h in range(heads)])


def _chunk_common(q, k, v, bd, a_log, dt_bias, heads, solved=None):
    c = CHUNK
    braw = jnp.stack([bd[:, h:h + 1] for h in range(heads)])
    draw = jnp.stack([bd[:, heads + h:heads + h + 1] for h in range(heads)])
    beta = _sigmoid(braw)
    xd = draw + dt_bias
    sp = jnp.maximum(xd, 0.0) + jnp.log1p(jnp.exp(-jnp.abs(xd)))
    g = -jnp.exp(a_log) * sp
    row = lax.broadcasted_iota(jnp.int32, (c, c), 0)
    col = lax.broadcasted_iota(jnp.int32, (c, c), 1)
    sq = (heads, c, c)
    g_b = jnp.broadcast_to(g, sq)
    g_row = _csum(jnp.where(row == col, g_b, 0.0))
    gam_col = _rsum(jnp.where(col <= row, jnp.broadcast_to(g_row, sq), 0.0))
    gam_row = _csum(jnp.where(row <= col, g_b, 0.0))
    causal = row >= col
    dm = jnp.where(causal, jnp.exp(jnp.where(causal, gam_col - gam_row, 0.0)), 0.0)
    kk = _dot(k, k, "nt")
    e = jnp.exp(gam_col)
    if solved is None:
        low = jnp.where(row > col, beta * kk * dm, 0.0)
        assert c in (INV_BLOCK, 2 * INV_BLOCK)
        same = (row // INV_BLOCK) == (col // INV_BLOCK)
        diag = jnp.where(same, low, 0.0)
        t = jnp.where(row == col, 1.0, 0.0) - diag
        pw = diag
        for _ in range((INV_BLOCK - 1).bit_length() - 1):
            pw = _dot(pw, pw)
            t = t + _dot(t, pw)
        if c > INV_BLOCK:
            t = t - _dot(_dot(t, low - diag), t)
        u = _dot(t, beta * v)
        w = _dot(t, (beta * e) * k)
    else:
        t, u, w = solved
    qk_raw = _dot(q, k, "nt")
    gl = _csum(g)
    el = jnp.exp(gl - gam_col)
    return dict(beta=beta, xd=xd, g=g, row=row, col=col, dm=dm, kk=kk, t=t, e=e, u=u, w=w,
                qk_raw=qk_raw, qk=qk_raw * dm, gl=gl, el=el, qd=e * q, kd=el * k, cd=jnp.exp(gl))


def _delta_chunk_fwd(qkv, bd, a_log, dt_bias, heads):
    s = qkv.shape[0]
    n = s // CHUNK
    dw = heads * HEAD_DIM
    blk = lambda part: pl.BlockSpec((CHUNK, dw), lambda i: (i, part))

    def body(q_ref, k_ref, v_ref, bd_ref, al_ref, dt_ref, o_ref, st_ref, t_ref, uw_ref, state):
        @pl.when(pl.program_id(0) == 0)
        def _():
            state[...] = jnp.zeros_like(state)

        cm = _chunk_common(_heads_of(q_ref, heads), _heads_of(k_ref, heads), _heads_of(v_ref, heads), bd_ref[...],
                           al_ref[...], dt_ref[...], heads)
        st = state[...]
        st_ref[0] = st
        t_ref[0] = cm["t"]
        uw_ref[0, 0] = cm["u"]
        uw_ref[0, 1] = cm["w"]
        vn = cm["u"] - _dot(cm["w"], st)
        o = _dot(cm["qd"], st) + _dot(cm["qk"], vn)
        for h in range(heads):
            o_ref[:, h * HEAD_DIM:(h + 1) * HEAD_DIM] = o[h]
        state[...] = cm["cd"] * st + _dot(cm["kd"], vn, "tn")

    smem = pl.BlockSpec((heads, 1, 1), lambda i: (0, 0, 0))
    return pl.pallas_call(
        body, name="delta_chunk_fwd", grid=(n,),
        in_specs=[blk(0), blk(1), blk(2), pl.BlockSpec((CHUNK, HEAD_DIM), lambda i: (i, 0)), smem, smem],
        out_specs=[pl.BlockSpec((CHUNK, dw), lambda i: (i, 0)),
                   pl.BlockSpec((1, heads, HEAD_DIM, HEAD_DIM), lambda i: (i, 0, 0, 0)),
                   pl.BlockSpec((1, heads, CHUNK, CHUNK), lambda i: (i, 0, 0, 0)),
                   pl.BlockSpec((1, 2, heads, CHUNK, HEAD_DIM), lambda i: (i, 0, 0, 0, 0))],
        out_shape=[jax.ShapeDtypeStruct((s, dw), F32),
                   jax.ShapeDtypeStruct((n, heads, HEAD_DIM, HEAD_DIM), F32),
                   jax.ShapeDtypeStruct((n, heads, CHUNK, CHUNK), F32),
                   jax.ShapeDtypeStruct((n, 2, heads, CHUNK, HEAD_DIM), F32)],
        scratch_shapes=[pltpu.VMEM((heads, HEAD_DIM, HEAD_DIM), F32)],
        compiler_params=_params(("arbitrary",)),
    )(qkv, qkv, qkv, bd, a_log, dt_bias)


def _delta_chunk_bwd(do, qkv, bd, saved, a_log, dt_bias, heads):
    s = qkv.shape[0]
    n = s // CHUNK
    dw = heads * HEAD_DIM
    c = CHUNK
    blk = lambda part: pl.BlockSpec((CHUNK, dw), lambda i: (n - 1 - i, part))

    def all_heads(q, k, v, dov, st, solved, dsn, bd, a_log, dt_bias):
        cm = _chunk_common(q, k, v, bd, a_log, dt_bias, heads, solved)
        beta, e, dm, row, col = cm["beta"], cm["e"], cm["dm"], cm["row"], cm["col"]
        sq = (heads, c, c)
        vn = cm["u"] - _dot(cm["w"], st)
        dvn = _dot(cm["kd"], dsn)
        dkd = _dot(vn, dsn, "nt")
        dcd = _csum(_rsum(st * dsn))
        ds = cm["cd"] * dsn
        dqd = _dot(dov, st, "nt")
        ds = ds + _dot(cm["qd"], dov, "tn")
        dqk = _dot(dov, vn, "nt")
        dvn = dvn + _dot(cm["qk"], dov, "tn")
        dw_ = -_dot(dvn, st, "nt")
        ds = ds - _dot(cm["w"], dvn, "tn")
        drhs_u = _dot(cm["t"], dvn, "tn")
        drhs_w = _dot(cm["t"], dw_, "tn")
        da = -(_dot(drhs_u, cm["u"], "nt") + _dot(drhs_w, cm["w"], "nt"))
        dl = jnp.where(row > col, da, 0.0)
        dbeta = _rsum(dl * cm["kk"] * dm)
        dkk = dl * beta * dm
        dd = dl * beta * cm["kk"]
        dv = beta * drhs_u
        ek = e * k
        dbeta = dbeta + _rsum(drhs_u * v) + _rsum(drhs_w * ek)
        dk = (beta * e) * drhs_w
        dgam = _rsum(drhs_w * (beta * ek))
        dqkm = dqk * dm
        dq = _dot(dqkm, k)
        dk = dk + _dot(dqkm, q, "tn")
        dd = dd + dqk * cm["qk_raw"]
        dk = dk + _dot(dkk, k) + _dot(dkk, k, "tn")
        dq = dq + e * dqd
        dgam = dgam + _rsum(dqd * cm["qd"])
        dk = dk + cm["el"] * dkd
        r = _rsum(dkd * cm["kd"])
        dgam = dgam - r
        dgl = _csum(r) + dcd * cm["cd"]
        mm = dd * dm
        colsum_c = _rsum(jnp.where(row == col, jnp.broadcast_to(_csum(mm), sq), 0.0))
        dgam = dgam + _rsum(mm) - colsum_c
        ridx = lax.broadcasted_iota(jnp.int32, (c, 1), 0)
        dgam = dgam + jnp.where(ridx == c - 1, dgl, 0.0)
        dgam_row = _csum(jnp.where(row == col, jnp.broadcast_to(dgam, sq), 0.0))
        dg = _rsum(jnp.where(col >= row, jnp.broadcast_to(dgam_row, sq), 0.0))
        d_xd = dg * (-jnp.exp(a_log)) * _sigmoid(cm["xd"])
        d_braw = dbeta * beta * (1.0 - beta)
        d_alog = dg * cm["g"]
        lane = lax.broadcasted_iota(jnp.int32, (c, HEAD_DIM), 1)
        dbd = jnp.zeros((c, HEAD_DIM), F32)
        for h in range(heads):
            dbd = (dbd + jnp.where(lane == h, d_braw[h], 0.0) + jnp.where(lane == h + heads, d_xd[h], 0.0)
                   + jnp.where(lane == h + 2 * heads, d_alog[h], 0.0))
        return dq, dk, dv, ds, dbd

    def body(do_ref, q_ref, k_ref, v_ref, bd_ref, st_ref, t_ref, uw_ref, al_ref, dt_ref, dqkv_ref, dbd_ref, dstate):
        @pl.when(pl.program_id(0) == 0)
        def _():
            dstate[...] = jnp.zeros_like(dstate)

        dq, dk, dv, ds, dbd = all_heads(_heads_of(q_ref, heads), _heads_of(k_ref, heads), _heads_of(v_ref, heads),
                                        _heads_of(do_ref, heads), st_ref[0], (t_ref[0], uw_ref[0, 0], uw_ref[0, 1]),
                                        dstate[...], bd_ref[...],
                                        al_ref[...], dt_ref[...])
        for part, val in enumerate((dq, dk, dv)):
            for h in range(heads):
                lo = part * dw + h * HEAD_DIM
                dqkv_ref[:, lo:lo + HEAD_DIM] = val[h]
        dstate[...] = ds
        dbd_ref[...] = dbd

    smem = pl.BlockSpec((heads, 1, 1), lambda i: (0, 0, 0))
    shared = pl.BlockSpec((CHUNK, HEAD_DIM), lambda i: (n - 1 - i, 0))
    wide = pl.BlockSpec((CHUNK, dw), lambda i: (n - 1 - i, 0))
    return pl.pallas_call(
        body, name="delta_chunk_bwd", grid=(n,),
        in_specs=[wide, blk(0), blk(1), blk(2), shared,
                  pl.BlockSpec((1, heads, HEAD_DIM, HEAD_DIM), lambda i: (n - 1 - i, 0, 0, 0)),
                  pl.BlockSpec((1, heads, CHUNK, CHUNK), lambda i: (n - 1 - i, 0, 0, 0)),
                  pl.BlockSpec((1, 2, heads, CHUNK, HEAD_DIM), lambda i: (n - 1 - i, 0, 0, 0, 0)), smem, smem],
        out_specs=[pl.BlockSpec((CHUNK, 3 * dw), lambda i: (n - 1 - i, 0)), shared],
        out_shape=[jax.ShapeDtypeStruct((s, 3 * dw), F32), jax.ShapeDtypeStruct((s, HEAD_DIM), F32)],
        scratch_shapes=[pltpu.VMEM((heads, HEAD_DIM, HEAD_DIM), F32)],
        compiler_params=_params(("arbitrary",)),
    )(do, qkv, qkv, qkv, bd, *saved, a_log, dt_bias)


def _delta_post_fwd(o, qkvz, w, heads):
    s = o.shape[0]
    tr = _tile(s, 4096, 8)

    def body(o_ref, z_ref, w_ref, out_ref):
        ov, z = o_ref[...], z_ref[...]
        rstd = lax.rsqrt(jnp.mean(ov * ov, axis=-1, keepdims=True) + EPS)
        out_ref[...] = (ov * rstd * w_ref[...] * (z * _sigmoid(z))).astype(BF16)

    return pl.pallas_call(
        body, name="delta_post_fwd", grid=(s // tr, heads),
        in_specs=[pl.BlockSpec((tr, HEAD_DIM), lambda i, h: (i, h)),
                  pl.BlockSpec((tr, HEAD_DIM), lambda i, h: (i, 3 * heads + h)),
                  pl.BlockSpec((1, HEAD_DIM), lambda i, h: (0, 0))],
        out_specs=pl.BlockSpec((tr, HEAD_DIM), lambda i, h: (i, h)),
        out_shape=jax.ShapeDtypeStruct((s, heads * HEAD_DIM), BF16),
        compiler_params=_params(("parallel", "parallel")),
    )(o, qkvz, w)


def _rope_tables(positions, s):
    half = HEAD_DIM // 2
    inv_freq = ROPE_THETA ** (-jnp.arange(half, dtype=F32) / half)
    ang = positions.reshape(s, 1).astype(F32) * inv_freq
    cos, sin = jnp.cos(ang), jnp.sin(ang)
    return jnp.concatenate([cos, cos], axis=-1), jnp.concatenate([-sin, sin], axis=-1)


def _attn_pre_fwd(aqkv, wq, wk, cosf, sinf, heads):
    s = aqkv.shape[0]
    tr = _tile(s, 4096, 8)

    def body(x_ref, wq_ref, wk_ref, c_ref, s_ref, o_ref):
        xv = x_ref[...]
        wv = jnp.where(pl.program_id(1) < heads, wq_ref[...], wk_ref[...])
        y = xv * lax.rsqrt(jnp.mean(xv * xv, axis=-1, keepdims=True) + EPS) * wv
        o_ref[...] = y * c_ref[...] + pltpu.roll(y, HEAD_DIM // 2, 1) * s_ref[...]

    blk = pl.BlockSpec((tr, HEAD_DIM), lambda i, j: (i, j))
    vec = pl.BlockSpec((1, HEAD_DIM), lambda i, j: (0, 0))
    tab = pl.BlockSpec((tr, HEAD_DIM), lambda i, j: (i, 0))
    return pl.pallas_call(
        body, name="attn_pre_fwd", grid=(s // tr, 2 * heads),
        in_specs=[blk, vec, vec, tab, tab], out_specs=blk,
        out_shape=jax.ShapeDtypeStruct((s, 2 * heads * HEAD_DIM), F32),
        compiler_params=_params(("parallel", "parallel")),
    )(aqkv, wq, wk, cosf, sinf)


def _band():
    qi = lax.broadcasted_iota(jnp.int32, (SPAN, 2 * SPAN), 0)
    ki = lax.broadcasted_iota(jnp.int32, (SPAN, 2 * SPAN), 1)
    dist = qi + SPAN - ki
    return (dist >= 0) & (dist <= SPAN), ki >= SPAN


def _sub(g, r, d):
    if d == 1:
        return pl.ds(g * SPAN, SPAN)
    return pl.ds(g * SPAN * d + r, SPAN, stride=d)


def _attn_blocks(s):
    assert s % (SPAN * max(DILATIONS)) == 0
    return [(p_i, d, r, g) for p_i, d in enumerate(DILATIONS) for r in range(d) for g in range(s // (SPAN * d))]


def _attn_fwd(qk, aqkv, w, heads):
    s = qk.shape[0]
    aw = heads * HEAD_DIM

    def body(q_ref, k_ref, v_ref, w_ref, mix_ref, acc_ref, m_ref, l_ref):
        band, own = _band()
        for p_i, d, r, g in _attn_blocks(s):
            rows = _sub(g, r, d)
            kc, vc = k_ref[rows, :], v_ref[rows, :]
            if g == 0:
                kp, vp, mask = kc, vc, band & own
            else:
                mask = band
            kcat = jnp.concatenate([kp, kc], axis=0)
            vcat = jnp.concatenate([vp, vc], axis=0)
            kp, vp = kc, vc
            sc = _dot(q_ref[rows, :] * (HEAD_DIM ** -0.5), kcat, "nt")
            sc = jnp.where(mask, sc, NEG)
            m = jnp.max(sc, axis=-1, keepdims=True)
            if p_i == 0:
                p = jnp.exp(sc - m)
                acc_ref[rows, :] = _dot(p, vcat)
                l_new = jnp.sum(p, axis=-1, keepdims=True)
            else:
                m_old = m_ref[rows, 0:1]
                m = jnp.maximum(m, m_old)
                alpha = jnp.exp(m_old - m)
                p = jnp.exp(sc - m)
                acc_ref[rows, :] = alpha * acc_ref[rows, :] + _dot(p, vcat)
                l_new = alpha * l_ref[rows, 0:1] + jnp.sum(p, axis=-1, keepdims=True)
            m_ref[rows, :] = jnp.broadcast_to(m, (SPAN, HEAD_DIM))
            l_ref[rows, :] = jnp.broadcast_to(l_new, (SPAN, HEAD_DIM))
        den = l_ref[...]
        ob = acc_ref[...] / den
        acc_ref[...] = ob
        m_ref[...] = m_ref[...] + jnp.log(den)
        rstd = lax.rsqrt(jnp.mean(ob * ob, axis=-1, keepdims=True) + EPS)
        mix_ref[...] = (ob * rstd * w_ref[...]).astype(BF16)

    col = lambda off: pl.BlockSpec((s, HEAD_DIM), lambda h: (0, off + h))
    return pl.pallas_call(
        body, name="attn_fwd", grid=(heads,),
        in_specs=[col(0), col(heads), col(2 * heads), pl.BlockSpec((1, HEAD_DIM), lambda h: (0, 0))],
        out_specs=[col(0), col(0), col(0)],
        out_shape=[jax.ShapeDtypeStruct((s, aw), BF16), jax.ShapeDtypeStruct((s, aw), F32),
                   jax.ShapeDtypeStruct((s, aw), F32)],
        scratch_shapes=[pltpu.VMEM((s, HEAD_DIM), F32)],
        compiler_params=_params(("parallel",)),
    )(qk, qk, aqkv, w)


def _attn_bwd(qk, aqkv, do, lse, dsum, heads):
    s = qk.shape[0]
    aw = heads * HEAD_DIM
    scale = HEAD_DIM ** -0.5

    def body(q_ref, k_ref, v_ref, do_ref, l_ref, ds_ref, out_ref):
        band, own = _band()
        dq_ref, dk_ref, dv_ref = out_ref.at[0], out_ref.at[1], out_ref.at[2]
        out_ref[...] = jnp.zeros((3, s, HEAD_DIM), F32)
        for _, d, r, g in _attn_blocks(s):
            rows = _sub(g, r, d)
            qs, dov = q_ref[rows, :] * scale, do_ref[rows, :]
            kc, vc = k_ref[rows, :], v_ref[rows, :]
            if g == 0:
                kp, vp, mask = kc, vc, band & own
            else:
                mask = band
            kcat = jnp.concatenate([kp, kc], axis=0)
            vcat = jnp.concatenate([vp, vc], axis=0)
            p = jnp.where(mask, jnp.exp(_dot(qs, kcat, "nt") - l_ref[rows, 0:1]), 0.0)
            dsc = p * (_dot(dov, vcat, "nt") - ds_ref[rows, 0:1])
            dq_ref[rows, :] += scale * _dot(dsc, kcat)
            dk = _dot(dsc, qs, "tn")
            dv = _dot(p, dov, "tn")
            if g > 0:
                dk_ref[prows, :] += dk_own + dk[:SPAN]
                dv_ref[prows, :] += dv_own + dv[:SPAN]
            dk_own, dv_own = dk[SPAN:], dv[SPAN:]
            if g == s // (SPAN * d) - 1:
                dk_ref[rows, :] += dk_own
                dv_ref[rows, :] += dv_own
            kp, vp, prows = kc, vc, rows

    col = lambda off: pl.BlockSpec((s, HEAD_DIM), lambda h: (0, off + h))
    return pl.pallas_call(
        body, name="attn_bwd", grid=(heads,),
        in_specs=[col(0), col(heads), col(2 * heads), col(0), col(0), col(0)],
        out_specs=pl.BlockSpec((3, s, HEAD_DIM), lambda h: (0, 0, h)),
        out_shape=jax.ShapeDtypeStruct((3, s, aw), F32),
        compiler_params=_params(("parallel",)),
    )(qk, qk, aqkv, do, lse, dsum)


def _attn_pre_bwd(grads, aqkv, wq, wk, cosf, sinf, heads):
    s = aqkv.shape[0]
    tr = _tile(s, 4096, 8)
    nrow = s // tr

    def body(g_ref, x_ref, wq_ref, wk_ref, c_ref, s_ref, dx_ref, dwq_ref, dwk_ref):
        i, j = pl.program_id(0), pl.program_id(1)
        kind = j // heads
        dout = g_ref[0]
        tot_v = dout
        dy = dout * c_ref[...] + pltpu.roll(dout * s_ref[...], HEAD_DIM // 2, 1)
        xv = x_ref[...]
        wv = jnp.where(kind == 0, wq_ref[...], wk_ref[...])
        rstd = lax.rsqrt(jnp.mean(xv * xv, axis=-1, keepdims=True) + EPS)
        xhat = xv * rstd
        gw = dy * wv
        dxn = rstd * (gw - xhat * jnp.mean(gw * xhat, axis=-1, keepdims=True))
        dx_ref[...] = jnp.where(kind == 2, tot_v, dxn).astype(BF16)
        dwc = jnp.sum(dy * xhat, axis=0, keepdims=True)

        @pl.when((i == 0) & (j == 0))
        def _():
            dwq_ref[...] = jnp.zeros_like(dwq_ref)
            dwk_ref[...] = jnp.zeros_like(dwk_ref)

        @pl.when(kind == 0)
        def _():
            dwq_ref[...] += dwc

        @pl.when(kind == 1)
        def _():
            dwk_ref[...] += dwc

    grad = pl.BlockSpec((1, tr, HEAD_DIM), lambda i, j: (j // heads, i, j % heads))
    blk = pl.BlockSpec((tr, HEAD_DIM), lambda i, j: (i, j))
    vec = pl.BlockSpec((1, HEAD_DIM), lambda i, j: (0, 0))
    tab = pl.BlockSpec((tr, HEAD_DIM), lambda i, j: (i, 0))
    return pl.pallas_call(
        body, name="attn_pre_bwd", grid=(nrow, 3 * heads),
        in_specs=[grad, blk, vec, vec, tab, tab], out_specs=[blk, vec, vec],
        out_shape=[jax.ShapeDtypeStruct((s, 3 * heads * HEAD_DIM), BF16),
                   jax.ShapeDtypeStruct((1, HEAD_DIM), F32), jax.ShapeDtypeStruct((1, HEAD_DIM), F32)],
        compiler_params=_params(("arbitrary", "arbitrary")),
    )(grads, aqkv, wq, wk, cosf, sinf)


def _colsum(a, name):
    s, d = a.shape
    tr = _tile(s, 4096, 8)

    def body(a_ref, o_ref):
        @pl.when(pl.program_id(0) == 0)
        def _():
            o_ref[...] = jnp.zeros_like(o_ref)

        o_ref[...] += jnp.sum(a_ref[...], axis=0, keepdims=True)

    return pl.pallas_call(
        body, name=name, grid=(s // tr,),
        in_specs=[pl.BlockSpec((tr, d), lambda i: (i, 0))], out_specs=pl.BlockSpec((1, d), lambda i: (0, 0)),
        out_shape=jax.ShapeDtypeStruct((1, d), F32),
        compiler_params=_params(("arbitrary",)),
    )(a)


def _local_step(x, positions, target, small, w_qkvz, w_bd, w_attn, conv_w, later_weights, ffn_grads_ready,
                rest_grads_ready, after=None):
    s, dmod = x.shape
    heads = dmod // (2 * HEAD_DIM)
    dw = heads * HEAD_DIM
    a_log, dt_bias = small["a_log"].reshape(heads, 1, 1), small["dt_bias"].reshape(heads, 1, 1)
    cosf, sinf = _rope_tables(positions, s)

    h1 = _rmsnorm_fwd(x, small["attn_norm_w"], "norm1_fwd", after=after)
    qkvz = _matmul(h1, w_qkvz, "nn", "proj_qkvz")
    bd = _matmul(h1, w_bd, "nn", "proj_bd")
    aqkv = _matmul(h1, w_attn, "nn", "proj_attn")
    dqkv = _delta_pre_fwd(qkvz, conv_w, heads)
    o_d, *saved = _delta_chunk_fwd(dqkv, bd, a_log, dt_bias, heads)
    mix_a = _delta_post_fwd(o_d, qkvz, small["delta_out_norm_w"], heads)
    qk_rot = _attn_pre_fwd(aqkv, small["q_norm_w"], small["k_norm_w"], cosf, sinf, heads)
    mix_b, ob, lse = _attn_fwd(qk_rot, aqkv, small["attn_out_norm_w"], heads)
    w_out, behind, ffn_weights = later_weights((mix_a, mix_b))
    x1 = _matmul(mix_a, w_out, "nn", "out_proj", add=x, b_rows=(0, dw), more=[(mix_b, w_out, (dw, dw))], after=behind)
    h2 = _rmsnorm_fwd(x1, small["ffn_norm_w"], "norm2_fwd")
    w_gu, w_down = ffn_weights(h2)
    ff = w_down.shape[0]

    def swiglu(accs, vals):
        g, u = accs
        return g, u, g * _sigmoid(g) * u

    gate, up, act = _matmul(h2, w_gu, "nn", "ffn_gate_up", b_cols=(0, ff), b2_cols=(ff, ff), finish=swiglu,
                            out_dtypes=[BF16, BF16, BF16])

    def loss_head(acc, vals):
        err = acc + vals[0] - vals[1]
        part = 0.5 * jnp.sum(jnp.sum(err * err, axis=-1, keepdims=True) * (1.0 / dmod), axis=0, keepdims=True)
        lane = lax.broadcasted_iota(jnp.int32, (1, HEAD_DIM), 1)
        return err * (1.0 / dmod), err * (1.0 / dmod), jnp.where(lane == 0, part, 0.0)

    dy, dy16, loss_row = _matmul(act, w_down, "nn", "ffn_down", tiles=[(x1, 0), (target, 0)], finish=loss_head,
                                 out_dtypes=[F32, BF16], row_sum=True)

    def swiglu_bwd(acc, vals):
        g, u = vals[0].astype(F32), vals[1].astype(F32)
        sg = _sigmoid(g)
        return acc * u * sg * (1.0 + g * (1.0 - sg)), acc * g * sg

    dgate, dup = _matmul(dy16, w_down, "nt", "ffn_down_dx", tiles=[(gate, 0), (up, 0)], finish=swiglu_bwd,
                         out_dtypes=[BF16, BF16])
    g_w_down = _matmul(act, dy16, "tn", "ffn_down_dw")
    g_w_gate = _matmul(h2, dgate, "tn", "ffn_gate_dw")
    g_w_up = _matmul(h2, dup, "tn", "ffn_up_dw")
    behind, and_then = ffn_grads_ready(g_w_gate, g_w_up, g_w_down)
    dh2 = _matmul(dgate, w_gu, "nt", "ffn_gate_dx", b_cols=(0, ff), after=behind)
    dh2 = _matmul(dup, w_gu, "nt", "ffn_up_dx", b_cols=(ff, ff), add=dh2)
    behind = and_then(dh2)
    dx1, dx1_16, g_ffn_norm = _rmsnorm_bwd(dh2, x1, small["ffn_norm_w"], dy, "norm2_bwd", after=behind)
    def per_head(fn, tile_w):
        outs = [fn(slice(h * HEAD_DIM, (h + 1) * HEAD_DIM)) for h in range(tile_w // HEAD_DIM)]
        joined = [jnp.concatenate([o[i] for o in outs], axis=1) for i in range(len(outs[0]) - 1)]
        return (*joined, sum(o[-1] for o in outs))

    def delta_post_bwd(acc, vals):
        def one(hs):
            d, ov, z, wv = acc[:, hs], vals[0][:, hs], vals[1][:, hs], vals[2][:, hs]
            sg = _sigmoid(z)
            rstd = lax.rsqrt(jnp.mean(ov * ov, axis=-1, keepdims=True) + EPS)
            ohat = ov * rstd
            dn = d * (z * sg)
            gw = dn * wv
            return (rstd * (gw - ohat * jnp.mean(gw * ohat, axis=-1, keepdims=True)),
                    d * (ohat * wv) * sg * (1.0 + z * (1.0 - sg)), jnp.sum(dn * ohat, axis=0, keepdims=True))
        return per_head(one, acc.shape[1])

    def attn_merge_bwd(acc, vals):
        def one(hs):
            d, ov, wv = acc[:, hs], vals[0][:, hs], vals[1][:, hs]
            rstd = lax.rsqrt(jnp.mean(ov * ov, axis=-1, keepdims=True) + EPS)
            ohat = ov * rstd
            gw = d * wv
            dov = rstd * (gw - ohat * jnp.mean(gw * ohat, axis=-1, keepdims=True))
            return (dov, jnp.broadcast_to(jnp.sum(dov * ov, axis=-1, keepdims=True), dov.shape),
                    jnp.sum(d * ohat, axis=0, keepdims=True))
        return per_head(one, acc.shape[1])

    rep = lambda wv: jnp.tile(wv, (1, heads))
    do_d, dz, g_delta_out_norm = _matmul(
        dx1_16, w_out, "nt", "out_proj_dx_a", b_rows=(0, dw), finish=delta_post_bwd, out_dtypes=[F32, BF16],
        tiles=[(o_d, 0), (qkvz, 3 * dw), (rep(small["delta_out_norm_w"]), 0)], row_sum=True)
    dob, dsum, g_attn_out_norm = _matmul(
        dx1_16, w_out, "nt", "out_proj_dx_b", b_rows=(dw, dw), finish=attn_merge_bwd, out_dtypes=[F32, F32],
        tiles=[(ob, 0), (rep(small["attn_out_norm_w"]), 0)], row_sum=True)
    g_w_out_a = _matmul(mix_a, dx1_16, "tn", "out_proj_dw_a")
    g_w_out_b = _matmul(mix_b, dx1_16, "tn", "out_proj_dw_b")
    grads = _attn_bwd(qk_rot, aqkv, dob, lse, dsum, heads)
    d_aqkv, g_q_norm, g_k_norm = _attn_pre_bwd(grads, aqkv, small["q_norm_w"], small["k_norm_w"], cosf, sinf, heads)
    ddqkv, dbd = _delta_chunk_bwd(do_d, dqkv, bd, saved, a_log, dt_bias, heads)
    d_qkv_raw, g_conv = _delta_pre_bwd(ddqkv, qkvz, conv_w, heads)
    bd_sums = _colsum(dbd, "bd_colsum")
    g_w_qkv = _matmul(h1, d_qkv_raw, "tn", "proj_qkv_dw")
    g_w_z = _matmul(h1, dz, "tn", "proj_z_dw")
    g_w_bd = _matmul(h1, dbd, "tn", "proj_bd_dw")
    g_w_attn = _matmul(h1, d_aqkv, "tn", "proj_attn_dw")
    behind, and_then = rest_grads_ready(dict(w_qkv=g_w_qkv, w_z=g_w_z, w_bd=g_w_bd, w_attn=g_w_attn,
                                             w_out_a=g_w_out_a, w_out_b=g_w_out_b))
    dh1 = _matmul(d_qkv_raw, w_qkvz, "nt", "proj_qkv_dx", b_cols=(0, 3 * dw), after=behind)
    dh1 = _matmul(dz, w_qkvz, "nt", "proj_z_dx", b_cols=(3 * dw, dw), add=dh1, after=and_then(dh1))
    dh1 = _matmul(d_aqkv, w_attn, "nt", "proj_attn_bd_dx", add=dh1, more=[(dbd, w_bd, (0, HEAD_DIM))])
    grad_x, _, g_attn_norm = _rmsnorm_bwd(dh1, x, small["attn_norm_w"], dx1, "norm1_bwd")
    small_grads = dict(
        attn_norm_w=g_attn_norm, a_log=bd_sums[:, 2 * heads:3 * heads], dt_bias=bd_sums[:, heads:2 * heads],
        delta_out_norm_w=g_delta_out_norm, q_norm_w=g_q_norm, k_norm_w=g_k_norm,
        attn_out_norm_w=g_attn_out_norm, ffn_norm_w=g_ffn_norm, conv_w=g_conv)
    return loss_row, grad_x, small_grads


def _adamw(w, g, m, v, name):
    r, c = w.shape
    tr = _tile(r, 256, 8)

    def body(w_ref, g_ref, m_ref, v_ref, d_ref, nm_ref, nv_ref):
        gv = g_ref[...]
        nm = ADAM_B1 * m_ref[...] + (1.0 - ADAM_B1) * gv
        nv = ADAM_B2 * v_ref[...] + (1.0 - ADAM_B2) * (gv * gv)
        m_hat = nm / (1.0 - ADAM_B1 ** ADAM_STEP)
        v_hat = nv / (1.0 - ADAM_B2 ** ADAM_STEP)
        d_ref[...] = -ADAM_LR * (m_hat / (jnp.sqrt(v_hat) + ADAM_EPS) + ADAM_WD * w_ref[...])
        nm_ref[...] = nm
        nv_ref[...] = nv

    blk = pl.BlockSpec((tr, c), lambda i: (i, 0))
    return pl.pallas_call(
        body, name=name, grid=(r // tr,),
        in_specs=[blk] * 4, out_specs=[blk] * 3,
        out_shape=[jax.ShapeDtypeStruct((r, c), F32)] * 3,
        compiler_params=_params(("parallel",)),
    )(w, g, m, v)


def _add_half_bf16(g, b, place, name):
    n, half, c = b.shape
    tr = _tile(half, 512, 16)
    nb = half // tr

    def body(place_ref, g_ref, b_ref, o_ref):
        o_ref[...] = (g_ref[...].astype(F32) + b_ref[...].astype(F32)).astype(BF16)

    blk = pl.BlockSpec((1, tr, c), lambda i, j, p: (i, j, 0))
    return pl.pallas_call(
        body, name=name,
        grid_spec=pltpu.PrefetchScalarGridSpec(
            num_scalar_prefetch=1, grid=(n, nb),
            in_specs=[pl.BlockSpec((1, tr, c), lambda i, j, p: (i, p[0] * nb + j, 0)), blk], out_specs=blk),
        out_shape=jax.ShapeDtypeStruct((n, half, c), BF16),
        compiler_params=_params(("parallel", "parallel")),
    )(place, g, b)


def _sum4_f32(mine, others, place, name):
    _, half, c = mine.shape
    tr = _tile(half, 512, 16)
    nb = half // tr

    def body(place_ref, a_ref, b_ref, o_ref):
        acc = a_ref[0].astype(F32)
        for j in range(3):
            acc = acc + b_ref[j].astype(F32)
        o_ref[...] = acc

    return pl.pallas_call(
        body, name=name,
        grid_spec=pltpu.PrefetchScalarGridSpec(
            num_scalar_prefetch=1, grid=(nb,),
            in_specs=[pl.BlockSpec((1, tr, c), lambda i, p: (p[1], i, 0)),
                      pl.BlockSpec((3, tr, c), lambda i, p: (0, i, 0))],
            out_specs=pl.BlockSpec((tr, c), lambda i, p: (p[0] * nb + i, 0))),
        out_shape=jax.ShapeDtypeStruct((2 * half, c), F32),
        compiler_params=_params(("parallel",)),
    )(place, mine, others)


def _place():
    x, y, c = lax.axis_index("x"), lax.axis_index("y"), lax.axis_index("c")
    other_chips = [(1 - x, y), (x, 1 - y), (1 - x, 1 - y)]
    return x, y, c, (x, y, 1 - c), other_chips


ANY = pl.BlockSpec(memory_space=pl.ANY)


def _remote(k, src, dst, to, send_sems, recv_sems):
    return pltpu.make_async_remote_copy(src_ref=src, dst_ref=dst, send_sem=send_sems.at[k], recv_sem=recv_sems.at[k],
                                        device_id=to, device_id_type=MESH)


def _half(ref, lead, hc):
    if lead is None:
        half = ref.shape[0] // 2
        return ref.at[pl.ds(hc * half, half), :]
    half = ref.shape[1] // 2
    return ref.at[lead, pl.ds(hc * half, half), :]


def _all_gather_weights(slots, whole, name):
    nt, nw = len(slots), len(whole)
    base_w, base_f, base_d = 2 * nt, 2 * nt + 3 * nw, 4 * nt + 3 * nw

    def quarter(ref, lead, hc, q):
        quart = ref.shape[1] // 4
        return ref.at[lead, pl.ds((2 * hc + q) * quart, quart), :]

    def body(*refs):
        ins, outs = refs[:nt + nw], refs[nt + nw:2 * (nt + nw)]
        sems = refs[2 * (nt + nw):]
        x, y, c, sibling, chips = _place()
        me, xn, yn, dg = 2 * x + y, 2 * (1 - x) + y, 2 * x + 1 - y, 2 * (1 - x) + 1 - y
        to_x, to_y = (1 - x, y, c), (x, 1 - y, c)
        cps = []
        for t in range(nt):
            cps.append(_remote(2 * t, _half(ins[t], me, c), _half(outs[t], me, c), to_x, *sems))
            cps.append(_remote(2 * t + 1, _half(ins[t], me, c), _half(outs[t], me, c), to_y, *sems))
        for j, (px, py) in enumerate(chips):
            for t in range(nw):
                cps.append(_remote(base_w + j * nw + t, ins[nt + t].at[me], outs[nt + t].at[me], (px, py, c), *sems))
        for cp in cps:
            cp.start()

        def start(k, ref, to):
            cp = _remote(k, ref, ref, to, *sems)
            cp.start()
            cps.append(cp)

        for t in range(nt):
            landed = _half(outs[t], xn, c)
            _remote(2 * t, landed, landed, to_x, *sems).wait_recv()
            start(base_f + 2 * t, quarter(outs[t], xn, c, 0), to_y)
            start(base_d + 3 * t, landed, sibling)
            landed = _half(outs[t], yn, c)
            _remote(2 * t + 1, landed, landed, to_y, *sems).wait_recv()
            start(base_f + 2 * t + 1, quarter(outs[t], yn, c, 1), to_x)
            start(base_d + 3 * t + 1, landed, sibling)
        for t in range(nt):
            q0, q1 = quarter(outs[t], dg, c, 0), quarter(outs[t], dg, c, 1)
            _remote(base_f + 2 * t, q0, q0, to_y, *sems).wait_recv()
            _remote(base_f + 2 * t + 1, q1, q1, to_x, *sems).wait_recv()
            start(base_d + 3 * t + 2, _half(outs[t], dg, c), sibling)
        for j, (px, py) in enumerate(chips):
            for t in range(nw):
                landed = outs[nt + t].at[2 * px + py]
                _remote(base_w + j * nw + t, landed, landed, (px, py, c), *sems).wait_recv()
        for t in range(nt):
            for j, chip in enumerate((xn, yn, dg)):
                other = _half(outs[t], chip, 1 - c)
                _remote(base_d + 3 * t + j, other, other, sibling, *sems).wait_recv()
        for cp in cps:
            cp.wait_send()

    arrays = list(slots) + list(whole)
    n_sem = 7 * nt + 3 * nw
    return pl.pallas_call(
        body, name=name, in_specs=[ANY] * len(arrays), out_specs=[ANY] * len(arrays),
        input_output_aliases={i: i for i in range(len(arrays))},
        out_shape=[jax.ShapeDtypeStruct(a.shape, a.dtype) for a in arrays],
        scratch_shapes=[pltpu.SemaphoreType.DMA((n_sem,)), pltpu.SemaphoreType.DMA((n_sem,))],
    )(*arrays)


def _join_halves(fs, name):
    nt = len(fs)

    def body(*refs):
        in_refs, out_refs, (send_sems, recv_sems) = refs[:nt], refs[nt:2 * nt], refs[2 * nt:]
        _, _, c, sibling, _ = _place()
        cps = [_remote(t, _half(in_refs[t], None, c), _half(out_refs[t], None, c), sibling, send_sems, recv_sems)
               for t in range(nt)]
        for cp in cps:
            cp.start()
        for t in range(nt):
            theirs = _half(out_refs[t], None, 1 - c)
            _remote(t, theirs, theirs, sibling, send_sems, recv_sems).wait_recv()
        for cp in cps:
            cp.wait_send()

    return pl.pallas_call(
        body, name=name, in_specs=[ANY] * nt, out_specs=[ANY] * nt,
        input_output_aliases={i: i for i in range(nt)},
        out_shape=[jax.ShapeDtypeStruct(f.shape, f.dtype) for f in fs],
        scratch_shapes=[pltpu.SemaphoreType.DMA((nt,)), pltpu.SemaphoreType.DMA((nt,))],
    )(*fs)


HBM = pl.BlockSpec(memory_space=pltpu.HBM)
SEM = pl.BlockSpec(memory_space=pltpu.SEMAPHORE)
EFFECT = pltpu.SideEffectType.DATAFLOW_SIDE_EFFECTING


def _split_start(arrays, after, plan, n_copies, name):
    na = len(arrays)

    def body(*refs):
        ins, send_sems, recv_sems = refs[:na], refs[na + 1], refs[na + 2]
        outs, token = refs[na + 3:2 * na + 3], refs[2 * na + 3]
        for k, (src, dst, to) in enumerate(plan(ins, outs)):
            _remote(k, src, dst, to, send_sems, recv_sems).start()
        token[...] = jnp.zeros_like(token)

    res = pl.pallas_call(
        body, name=name,
        out_shape=(pltpu.SemaphoreType.DMA((n_copies,)), pltpu.SemaphoreType.DMA((n_copies,)),
                   *[pltpu.HBM(a.shape, a.dtype) for a in arrays], jax.ShapeDtypeStruct((8, HEAD_DIM), F32)),
        in_specs=[HBM] * na + [ANY],
        out_specs=(SEM, SEM, *[HBM] * na, pl.BlockSpec(memory_space=pltpu.VMEM)),
        input_output_aliases={i: 2 + i for i in range(na)},
        compiler_params=pltpu.CompilerParams(has_side_effects=EFFECT),
    )(*[pltpu.with_memory_space_constraint(a, pltpu.HBM) for a in arrays], after)
    return res[0], res[1], list(res[2:2 + na]), res[2 + na]


def _split_wait(send_sems, recv_sems, arrays, after, plan, name):
    na = len(arrays)
    after = list(after) if isinstance(after, (list, tuple)) else [after]

    def body(*refs):
        ins, send, recv, outs = refs[:na], refs[na], refs[na + 1], refs[na + 2 + len(after):]
        for k, (src, dst, to) in enumerate(plan(ins, outs)):
            cp = _remote(k, src, dst, to, send, recv)
            cp.wait_send()
            cp.wait_recv()

    res = pl.pallas_call(
        body, name=name, out_shape=tuple(pltpu.HBM(a.shape, a.dtype) for a in arrays),
        in_specs=[HBM] * na + [SEM, SEM] + [ANY] * len(after), out_specs=tuple([HBM] * na),
        input_output_aliases={i: i for i in range(na)},
        compiler_params=pltpu.CompilerParams(has_side_effects=EFFECT),
    )(*arrays, send_sems, recv_sems, *after)
    return list(res)


def _gather_plan(nt):
    def plan(ins, outs):
        x, y, c, _, chips = _place()
        me = 2 * x + y
        return [(_half(ins[t], me, c), _half(outs[t], me, c), (px, py, c)) for px, py in chips for t in range(nt)]
    return plan


def _gather_landed_plan(nt):
    def plan(ins, outs):
        _, _, c, _, chips = _place()
        return [(_half(outs[t], 2 * px + py, c), _half(outs[t], 2 * px + py, c), (px, py, c))
                for px, py in chips for t in range(nt)]
    return plan


def _scatter_plan(nt):
    def plan(ins, outs):
        _, _, c, _, chips = _place()
        return [(ins[t].at[2 * px + py], outs[nt + t].at[j], (px, py, c))
                for j, (px, py) in enumerate(chips) for t in range(nt)]
    return plan


def _pass_plan(nt):
    def plan(ins, outs):
        _, _, c, sibling, chips = _place()
        return [(_half(ins[t], 2 * px + py, c), _half(outs[t], 2 * px + py, c), sibling)
                for px, py in chips for t in range(nt)]
    return plan


def _pass_landed_plan(nt):
    def plan(ins, outs):
        _, _, c, sibling, chips = _place()
        return [(_half(outs[t], 2 * px + py, c), _half(outs[t], 2 * px + py, 1 - c), sibling)
                for px, py in chips for t in range(nt)]
    return plan


def _swap_plan(nt):
    def plan(ins, outs):
        _, _, c, sibling, _ = _place()
        res = []
        for t in range(nt):
            half = ins[t].shape[1] // 2
            res.append((ins[t].at[:, pl.ds((1 - c) * half, half), :], outs[nt + t], sibling))
        return res
    return plan


def _pass_halves_to_sibling(slots, name):
    nt = len(slots)

    def body(*refs):
        ins, outs, (send_sems, recv_sems) = refs[:nt], refs[nt:2 * nt], refs[2 * nt:]
        _, _, c, sibling, chips = _place()
        cps = [_remote(j * nt + t, _half(ins[t], 2 * px + py, c), _half(outs[t], 2 * px + py, c), sibling,
                       send_sems, recv_sems)
               for j, (px, py) in enumerate(chips) for t in range(nt)]
        for cp in cps:
            cp.start()
        for j, (px, py) in enumerate(chips):
            for t in range(nt):
                other = _half(outs[t], 2 * px + py, 1 - c)
                _remote(j * nt + t, other, other, sibling, send_sems, recv_sems).wait_recv()
        for cp in cps:
            cp.wait_send()

    return pl.pallas_call(
        body, name=name, in_specs=[ANY] * nt, out_specs=[ANY] * nt,
        input_output_aliases={i: i for i in range(nt)},
        out_shape=[jax.ShapeDtypeStruct(a.shape, a.dtype) for a in slots],
        scratch_shapes=[pltpu.SemaphoreType.DMA((3 * nt,)), pltpu.SemaphoreType.DMA((3 * nt,))],
    )(*slots)


def _all_reduce_small(v):
    r, lanes = v.shape

    def body(v_ref, out_ref, buf, send_sems, recv_sems):
        x, y, c, sibling, chips = _place()

        def slot(px, py, pc):
            return buf.at[4 * px + 2 * py + pc]

        def copy(k, block, to, src=None):
            return pltpu.make_async_remote_copy(src_ref=slot(*block) if src is None else src, dst_ref=slot(*block),
                                                send_sem=send_sems.at[k], recv_sem=recv_sems.at[k],
                                                device_id=to, device_id_type=MESH)

        me = (x, y, c)
        buf[4 * x + 2 * y + c] = v_ref[...]
        first = [copy(0, me, sibling, src=v_ref)]
        first += [copy(1 + j, me, (*chip, c), src=v_ref) for j, chip in enumerate(chips)]
        for cp in first:
            cp.start()
        passed = [copy(4 + j, (*chip, c), sibling) for j, chip in enumerate(chips)]
        for j, chip in enumerate(chips):
            copy(1 + j, (*chip, c), me).wait_recv()
            passed[j].start()
        copy(0, (x, y, 1 - c), me).wait_recv()
        for j, chip in enumerate(chips):
            copy(4 + j, (*chip, 1 - c), me).wait_recv()
        for cp in first + passed:
            cp.wait_send()
        acc = buf[0]
        for k in range(1, 8):
            acc = acc + buf[k]
        out_ref[...] = acc

    vmem = pl.BlockSpec(memory_space=pltpu.VMEM)
    return pl.pallas_call(
        body, name="all_reduce_small", in_specs=[vmem], out_specs=vmem,
        out_shape=jax.ShapeDtypeStruct((r, lanes), F32),
        scratch_shapes=[pltpu.VMEM((8, r, lanes), F32), pltpu.SemaphoreType.DMA((7,)), pltpu.SemaphoreType.DMA((7,))],
    )(v)


def _size(shape):
    n = 1
    for d in shape:
        n *= d
    return n


def _regroup_cols(parts, widths):
    out, start = [], 0
    for width in widths:
        pieces, lo = [], 0
        for p in parts:
            a, b = max(start, lo), min(start + width, lo + p.shape[1])
            if a < b:
                pieces.append(p[:, a - lo:b - lo])
            lo += p.shape[1]
        out.append(pieces[0] if len(pieces) == 1 else jnp.concatenate(pieces, axis=1))
        start += width
    return out


def _pack_small(parts):
    rows = []
    for p in parts:
        f = p.reshape(-1).astype(F32)
        n = -(-f.shape[0] // HEAD_DIM) * HEAD_DIM
        rows.append(jnp.pad(f, (0, n - f.shape[0])).reshape(-1, HEAD_DIM))
    a = jnp.concatenate(rows, axis=0)
    return jnp.pad(a, ((0, -a.shape[0] % 8), (0, 0)))


def _unpack_small(a, shapes):
    out, row = [], 0
    for shp in shapes:
        nrows = -(-_size(shp) // HEAD_DIM)
        out.append(a[row:row + nrows].reshape(-1)[:_size(shp)].reshape(shp))
        row += nrows
    return out


SMALL = ["attn_norm_w", "a_log", "dt_bias", "delta_out_norm_w", "q_norm_w", "k_norm_w", "attn_out_norm_w", "ffn_norm_w"]
BIG = ["w_in", "w_out", "w_gate_up", "w_down"]
ORDER = ["attn_norm_w", "w_in", "conv_w", "a_log", "dt_bias", "delta_out_norm_w", "q_norm_w", "k_norm_w",
         "attn_out_norm_w", "w_out", "ffn_norm_w", "w_gate_up", "w_down"]


def kernel(x, positions, attn_norm_w, w_in, conv_w, a_log, dt_bias, delta_out_norm_w, q_norm_w, k_norm_w, attn_out_norm_w, w_out, ffn_norm_w, w_gate_up, w_down, loss_target, m_attn_norm_w, m_w_in, m_conv_w, m_a_log, m_dt_bias, m_delta_out_norm_w, m_q_norm_w, m_k_norm_w, m_attn_out_norm_w, m_w_out, m_ffn_norm_w, m_w_gate_up, m_w_down, v_attn_norm_w, v_w_in, v_conv_w, v_a_log, v_dt_bias, v_delta_out_norm_w, v_q_norm_w, v_k_norm_w, v_attn_out_norm_w, v_w_out, v_ffn_norm_w, v_w_gate_up, v_w_down):
    wts = dict(attn_norm_w=attn_norm_w, w_in=w_in, conv_w=conv_w, a_log=a_log, dt_bias=dt_bias,
               delta_out_norm_w=delta_out_norm_w, q_norm_w=q_norm_w, k_norm_w=k_norm_w,
               attn_out_norm_w=attn_out_norm_w, w_out=w_out, ffn_norm_w=ffn_norm_w, w_gate_up=w_gate_up, w_down=w_down)
    mom = dict(attn_norm_w=m_attn_norm_w, w_in=m_w_in, conv_w=m_conv_w, a_log=m_a_log, dt_bias=m_dt_bias,
               delta_out_norm_w=m_delta_out_norm_w, q_norm_w=m_q_norm_w, k_norm_w=m_k_norm_w,
               attn_out_norm_w=m_attn_out_norm_w, w_out=m_w_out, ffn_norm_w=m_ffn_norm_w, w_gate_up=m_w_gate_up,
               w_down=m_w_down)
    var = dict(attn_norm_w=v_attn_norm_w, w_in=v_w_in, conv_w=v_conv_w, a_log=v_a_log, dt_bias=v_dt_bias,
               delta_out_norm_w=v_delta_out_norm_w, q_norm_w=v_q_norm_w, k_norm_w=v_k_norm_w,
               attn_out_norm_w=v_attn_out_norm_w, w_out=v_w_out, ffn_norm_w=v_ffn_norm_w, w_gate_up=v_w_gate_up,
               w_down=v_w_down)
    dmod = x.shape[2]
    heads = dmod // (2 * HEAD_DIM)
    dw = heads * HEAD_DIM
    chip = 2 * lax.axis_index("x") + lax.axis_index("y")
    core = lax.axis_index("c")
    n_in, n_out, n_gu, n_down, n_conv = (w_in.shape[2], w_out.shape[1], w_gate_up.shape[2], w_down.shape[1],
                                         conv_w.shape[2])

    def slots_of(w, dtype):
        shard = w[0].astype(dtype)
        return lax.dynamic_update_index_in_dim(lax.empty((4,) + shard.shape, dtype), shard, chip, axis=0)

    s_in, s_conv = _all_gather_weights([slots_of(w_in, BF16)], [slots_of(conv_w, F32)], "all_gather_w_in")
    later = [slots_of(w_out, BF16), slots_of(w_gate_up, BF16), slots_of(w_down, BF16)]
    w_send, w_recv, later, started = _split_start(later, s_conv, _gather_plan(3), 9, "gather_rest_start")
    by_cols = lambda a: a.transpose(1, 0, 2).reshape(a.shape[1], 4 * a.shape[2])
    conv_f = by_cols(s_conv)
    w_qkvz, w_bd, w_attn = _regroup_cols([s_in[i] for i in range(4)], [4 * dw, 2 * heads, 3 * dw])
    w_bd = jnp.pad(w_bd, ((0, 0), (0, HEAD_DIM - 2 * heads)))
    small = {n: wts[n] for n in SMALL}
    place = jnp.stack([core, chip]).astype(jnp.int32)
    to_slots = lambda a: a.reshape(a.shape[0], 4, a.shape[1] // 4).transpose(1, 0, 2)

    def later_weights(after):
        landed = _split_wait(w_send, w_recv, later, after, _gather_landed_plan(3), "gather_rest_wait")
        s_out, = _pass_halves_to_sibling(landed[:1], "gather_out_pass")
        p_send, p_recv, passing, token = _split_start(landed[1:], s_out, _pass_plan(2), 6, "gather_ffn_pass_start")

        def ffn_weights(after):
            s_gu, s_down = _split_wait(p_send, p_recv, passing, after, _pass_landed_plan(2), "gather_ffn_pass_wait")
            return by_cols(s_gu), s_down.reshape(4 * n_down, dmod)

        return s_out.reshape(4 * n_out, dmod), token, ffn_weights

    ffn = {}

    def ffn_grads_ready(g_gate, g_up, g_down):
        gs = [to_slots(jnp.concatenate([g_gate, g_up], axis=1)), g_down.reshape(4, n_down, dmod)]
        zones = [lax.empty((4, g.shape[1] // 2, g.shape[2]), BF16) for g in gs]
        s_send, s_recv, s_bufs, swapping = _split_start(gs + zones, g_gate, _swap_plan(2), 2, "swap_ffn_start")

        def and_then(after):
            g_gu, g_dn, b_gu, b_dn = _split_wait(s_send, s_recv, s_bufs, after, _swap_plan(2), "swap_ffn_wait")
            sums = [_add_half_bf16(g_gu, b_gu, place, "chip_partial_sum_w_gate_up"),
                    _add_half_bf16(g_dn, b_dn, place, "chip_partial_sum_w_down")]
            zones3 = [lax.empty((3,) + p.shape[1:], BF16) for p in sums]
            ffn["send"], ffn["recv"], ffn["bufs"], token = _split_start(sums + zones3, b_gu, _scatter_plan(2), 6,
                                                                        "scatter_ffn_start")
            return token

        return swapping, and_then

    rest = {}

    def rest_grads_ready(bg):
        gs = [jnp.stack(_regroup_cols([bg["w_qkv"], bg["w_z"], bg["w_bd"][:, :2 * heads], bg["w_attn"]], [n_in] * 4)),
              jnp.concatenate([bg["w_out_a"], bg["w_out_b"]], axis=0).reshape(4, n_out, dmod)]
        zones = [lax.empty((4, g.shape[1] // 2, g.shape[2]), BF16) for g in gs]
        s_send, s_recv, s_bufs, swapping = _split_start(gs + zones, bg["w_attn"], _swap_plan(2), 2, "swap_rest_start")

        def and_then(after):
            g_in, g_out, b_in, b_out = _split_wait(s_send, s_recv, s_bufs, after, _swap_plan(2), "swap_rest_wait")
            sums = [_add_half_bf16(g_in, b_in, place, "chip_partial_sum_w_in"),
                    _add_half_bf16(g_out, b_out, place, "chip_partial_sum_w_out")]
            zones3 = [lax.empty((3,) + p.shape[1:], BF16) for p in sums]
            rest["send"], rest["recv"], rest["bufs"], token = _split_start(sums + zones3, b_in, _scatter_plan(2), 6,
                                                                           "scatter_rest_start")
            return token

        return swapping, and_then

    loss_row, grad_x, sg = _local_step(
        x[0], positions[0], loss_target[0], small, w_qkvz, w_bd, w_attn, conv_f,
        later_weights, ffn_grads_ready, rest_grads_ready, after=started)

    r_send, r_recv, r_bufs = rest["send"], rest["recv"], rest["bufs"]
    sum_gu, sum_down, got_gu, got_down = _split_wait(ffn["send"], ffn["recv"], ffn["bufs"], grad_x,
                                                     _scatter_plan(2), "scatter_ffn_wait")
    g_big = dict(zip(["w_gate_up", "w_down"], _join_halves(
        [_sum4_f32(sum_gu, got_gu, place, "grad_total_w_gate_up"),
         _sum4_f32(sum_down, got_down, place, "grad_total_w_down")], "join_ffn_halves")))
    grads, deltas, new_m, new_v = {}, {}, {}, {}

    def adamw_big(n):
        shp = wts[n].shape
        d, nm, nv = _adamw(wts[n][0], g_big[n], mom[n][0], var[n][0], "adamw_" + n)
        grads[n], deltas[n], new_m[n], new_v[n] = g_big[n].reshape(shp), d.reshape(shp), nm.reshape(shp), nv.reshape(shp)
        return d

    done = [adamw_big("w_gate_up"), adamw_big("w_down")]
    sum_in, sum_out, got_in, got_out = _split_wait(r_send, r_recv, r_bufs, done, _scatter_plan(2), "scatter_rest_wait")
    g_big.update(zip(["w_in", "w_out"], _join_halves(
        [_sum4_f32(sum_in, got_in, place, "grad_total_w_in"),
         _sum4_f32(sum_out, got_out, place, "grad_total_w_out")], "join_rest_halves")))
    adamw_big("w_in")
    adamw_big("w_out")

    reduced = _all_reduce_small(_pack_small([sg[n] for n in SMALL] + [sg["conv_w"], loss_row]))
    red = _unpack_small(reduced, [wts[n].shape for n in SMALL] + [(4, 4 * n_conv), (1, HEAD_DIM)])
    g_small = dict(zip(SMALL, red[:len(SMALL)]))
    g_conv_full, loss_out = red[len(SMALL)], red[len(SMALL) + 1]
    g_small["conv_w"] = lax.dynamic_slice_in_dim(g_conv_full, chip * n_conv, n_conv, axis=1).reshape(conv_w.shape)

    names = SMALL + ["conv_w"]
    shapes = [wts[n].shape for n in names]
    d, nm, nv = _adamw(_pack_small([wts[n] for n in names]), _pack_small([g_small[n] for n in names]),
                       _pack_small([mom[n] for n in names]), _pack_small([var[n] for n in names]), "adamw_small")
    for n, dd, mm, vv in zip(names, _unpack_small(d, shapes), _unpack_small(nm, shapes), _unpack_small(nv, shapes)):
        grads[n], deltas[n], new_m[n], new_v[n] = g_small[n], dd, mm, vv
    return (loss_out[0, 0], grad_x[None], *[grads[n] for n in ORDER], *[deltas[n] for n in ORDER],
            *[new_m[n] for n in ORDER], *[new_v[n] for n in ORDER])
```

```python
import jax
import jax.numpy as jnp
from jax import lax
from jax.experimental import pallas as pl
from jax.experimental.pallas import tpu as pltpu

F32 = jnp.float32
BF16 = jnp.bfloat16
HEAD_DIM = 128
CHUNK = 128
INV_BLOCK = 64
SPAN = 128
DILATIONS = (1, 4, 16)
ROPE_THETA = 10000.0
EPS = 1e-6
NEG = -1e30
ADAM_LR, ADAM_B1, ADAM_B2, ADAM_EPS, ADAM_WD, ADAM_STEP = 0.001, 0.9, 0.999, 1e-08, 0.01, 10
VMEM_LIMIT = 48 * 1024 * 1024
MATMUL_VMEM = 44 * 1024 * 1024
MESH = pl.DeviceIdType.MESH

_DN = {"nn": (((1,), (0,)), ((), ())), "nt": (((1,), (1,)), ((), ())), "tn": (((0,), (0,)), ((), ()))}


def _dot(a, b, mode="nn"):
    (ca, cb), _ = _DN[mode]
    if a.ndim == 3:
        dn = (((ca[0] + 1,), (cb[0] + 1,)), ((0,), (0,)))
    else:
        dn = _DN[mode]
    return lax.dot_general(a.astype(BF16), b.astype(BF16), dn, preferred_element_type=F32)


def _rsum(x):
    return jnp.sum(x, axis=-1, keepdims=True)


def _csum(x):
    return jnp.sum(x, axis=-2, keepdims=True)


def _tile(dim, pref, unit=128):
    t = (min(pref, dim) // unit) * unit
    while t >= unit:
        if dim % t == 0:
            return t
        t -= unit
    return dim


def _params(sem):
    return pltpu.CompilerParams(dimension_semantics=sem, vmem_limit_bytes=VMEM_LIMIT)


def _sigmoid(x):
    return 1.0 / (1.0 + jnp.exp(-x))


def _matmul(a, b, mode, name, add=None, out_dtype=F32, a_cols=None, b_cols=None, b_rows=None,
            tiles=(), finish=None, out_dtypes=(), row_sum=False, after=None, b2_cols=None, more=()):
    if mode == "tn":
        out_dtype = BF16
    a_off, a_w = a_cols if a_cols else (0, a.shape[1])
    b_off, b_w = b_cols if b_cols else (0, b.shape[1])
    br_off, br_n = b_rows if b_rows else (0, b.shape[0])
    if mode == "nn":
        m, k, n = a.shape[0], a_w, b_w
        assert br_n == k
    elif mode == "nt":
        m, k, n = a.shape[0], a_w, br_n
        assert b_w == k
    else:
        k, m, n = a.shape[0], a_w, b_w
        assert b.shape[0] == k
    if finish is None:
        out_dtypes = [out_dtype]
        if add is None:
            finish = lambda acc, vals: (acc,)
        else:
            tiles = [(add, 0)]
            finish = lambda acc, vals: (acc + vals[0].astype(F32),)
    n_b = 1 if b2_cols is None else 2
    sides = sum(jnp.dtype(t.dtype).itemsize for t, _ in tiles) + sum(jnp.dtype(d).itemsize for d in out_dtypes)

    def vmem(tm_, tn_, tk_):
        ops = tm_ * tk_ * jnp.dtype(a.dtype).itemsize + n_b * tk_ * tn_ * jnp.dtype(b.dtype).itemsize
        for a_i, b_i, (_, k_i) in more:
            ops += tm_ * k_i * jnp.dtype(a_i.dtype).itemsize + k_i * tn_ * jnp.dtype(b_i.dtype).itemsize
        return 2 * (ops + tm_ * tn_ * sides) + (n_b * tm_ * tn_ * 4 if tk_ < k else 0)

    tm, tn = _tile(m, 1024, 128), _tile(n, 1024, 128)
    tall = _tile(m, 1536, 128), _tile(n, 512, 128)
    if tall[0] * tall[1] > tm * tn and vmem(*tall, k) <= MATMUL_VMEM:
        tm, tn = tall
    while more and tn > 128 and vmem(tm, tn, k) > MATMUL_VMEM * 3 // 4:
        tn //= 2
    tk = next(t for t in [_tile(k, p, 128) for p in (4096, 2048, 1024, 512, 128)] if vmem(tm, tn, t) <= MATMUL_VMEM)
    if mode == "nn":
        assert a_off % tk == 0 and b_off % tn == 0 and br_off % tk == 0
        a_spec = pl.BlockSpec((tm, tk), lambda i, j, kk: (i, kk + a_off // tk))
        b_spec = pl.BlockSpec((tk, tn), lambda i, j, kk: (kk + br_off // tk, j + b_off // tn))
    elif mode == "nt":
        assert a_off % tk == 0 and b_off % tk == 0 and br_off % tn == 0
        a_spec = pl.BlockSpec((tm, tk), lambda i, j, kk: (i, kk + a_off // tk))
        b_spec = pl.BlockSpec((tn, tk), lambda i, j, kk: (j + br_off // tn, kk + b_off // tk))
    else:
        assert a_off % tm == 0 and b_off % tn == 0
        a_spec = pl.BlockSpec((tk, tm), lambda i, j, kk: (kk, i + a_off // tm))
        b_spec = pl.BlockSpec((tk, tn), lambda i, j, kk: (kk, j + b_off // tn))
    nk = k // tk
    assert not more or (nk == 1 and n_b == 1 and mode in ("nn", "nt"))
    n_tiles, n_out = len(tiles), len(out_dtypes)
    extra = [] if after is None else [after]
    first_tile_ref = 1 + n_b + 2 * len(more)
    first_out = first_tile_ref + n_tiles + len(extra)

    def body(*refs):
        a_ref, b_refs = refs[0], refs[1:1 + n_b]
        tile_refs, out_refs = refs[first_tile_ref:first_tile_ref + n_tiles], refs[first_out:first_out + n_out]
        acc_refs = refs[-n_b:] if nk > 1 else ()
        kk = pl.program_id(2)
        first_tile = (pl.program_id(0) == 0) & (pl.program_id(1) == 0)

        av = a_ref[...]
        if nk > 1:
            @pl.when(kk == 0)
            def _():
                for acc_ref in acc_refs:
                    acc_ref[...] = jnp.zeros_like(acc_ref)

            for acc_ref, b_ref in zip(acc_refs, b_refs):
                acc_ref[...] += _dot(av, b_ref[...], mode)

        @pl.when(kk == nk - 1)
        def _():
            if nk > 1:
                accs = [acc_ref[...] for acc_ref in acc_refs]
            else:
                accs = [_dot(av, b_ref[...], mode) for b_ref in b_refs]
                for p_i in range(len(more)):
                    accs[0] = accs[0] + _dot(refs[1 + n_b + 2 * p_i][...], refs[2 + n_b + 2 * p_i][...], mode)
            res = finish(accs[0] if n_b == 1 else accs, [t[...] for t in tile_refs])
            for o_ref, r in zip(out_refs, res):
                o_ref[...] = r.astype(o_ref.dtype)
            if row_sum:
                row_ref = refs[first_out + n_out]

                @pl.when(first_tile)
                def _():
                    row_ref[...] = res[n_out]

                @pl.when(jnp.logical_not(first_tile))
                def _():
                    row_ref[...] += res[n_out]

    in_specs = [a_spec, b_spec]
    args = [a, b]
    if b2_cols is not None:
        assert mode == "nn" and b2_cols[1] == n and b2_cols[0] % tn == 0
        in_specs.append(pl.BlockSpec((tk, tn), lambda i, j, kk: (kk + br_off // tk, j + b2_cols[0] // tn)))
        args.append(b)
    for a_i, b_i, (start, k_i) in more:
        assert a_i.shape == (m, k_i) and start % k_i == 0
        in_specs.append(pl.BlockSpec((tm, k_i), lambda i, j, kk: (i, 0)))
        if mode == "nn":
            in_specs.append(pl.BlockSpec((k_i, tn), lambda i, j, kk, s_=start // k_i: (s_, j)))
        else:
            in_specs.append(pl.BlockSpec((tn, k_i), lambda i, j, kk, s_=start // k_i: (j, s_)))
        args += [a_i, b_i]
    for arr, off in tiles:
        assert off % tn == 0
        if arr.shape[0] == 1:
            in_specs.append(pl.BlockSpec((1, tn), lambda i, j, kk, off=off: (0, j + off // tn)))
        else:
            in_specs.append(pl.BlockSpec((tm, tn), lambda i, j, kk, off=off: (i, j + off // tn)))
        args.append(arr)
    in_specs += [ANY] * len(extra)
    args += extra
    out_specs = [pl.BlockSpec((tm, tn), lambda i, j, kk: (i, j))] * n_out
    out_shape = [jax.ShapeDtypeStruct((m, n), dt) for dt in out_dtypes]
    if row_sum:
        out_specs.append(pl.BlockSpec((1, HEAD_DIM), lambda i, j, kk: (0, 0)))
        out_shape.append(jax.ShapeDtypeStruct((1, HEAD_DIM), F32))
    res = pl.pallas_call(
        body, name=name, grid=(m // tm, n // tn, nk),
        in_specs=in_specs, out_specs=out_specs, out_shape=out_shape,
        scratch_shapes=[pltpu.VMEM((tm, tn), F32)] * (n_b if nk > 1 else 0),
        compiler_params=_params(("arbitrary",) * 3 if row_sum else ("parallel", "parallel", "arbitrary")),
    )(*args)
    return res[0] if len(res) == 1 else res


def _rmsnorm_fwd(x, w, name, after=None):
    s, d = x.shape
    tr = _tile(s, 1024, 8)

    def body(x_ref, w_ref, *rest):
        h_ref = rest[-1]
        xv = x_ref[...]
        rstd = lax.rsqrt(jnp.mean(xv * xv, axis=-1, keepdims=True) + EPS)
        h_ref[...] = (xv * rstd * w_ref[...]).astype(BF16)

    extra = [] if after is None else [after]
    return pl.pallas_call(
        body, name=name, grid=(s // tr,),
        in_specs=[pl.BlockSpec((tr, d), lambda i: (i, 0)), pl.BlockSpec((1, d), lambda i: (0, 0))] + [ANY] * len(extra),
        out_specs=pl.BlockSpec((tr, d), lambda i: (i, 0)),
        out_shape=jax.ShapeDtypeStruct((s, d), BF16),
        compiler_params=_params(("parallel",)),
    )(x, w, *extra)


def _rmsnorm_bwd(dh, x, w, res, name, after=None):
    s, d = x.shape
    tr = _tile(s, 256, 8)

    def body(dh_ref, x_ref, w_ref, res_ref, *rest):
        dx_ref, dx16_ref, dw_ref = rest[-3:]
        xv = x_ref[...]
        rstd = lax.rsqrt(jnp.mean(xv * xv, axis=-1, keepdims=True) + EPS)
        xhat = xv * rstd
        dhv = dh_ref[...]
        gw = dhv * w_ref[...]
        dx = res_ref[...] + rstd * (gw - xhat * jnp.mean(gw * xhat, axis=-1, keepdims=True))
        dx_ref[...] = dx
        dx16_ref[...] = dx.astype(BF16)

        @pl.when(pl.program_id(0) == 0)
        def _():
            dw_ref[...] = jnp.zeros_like(dw_ref)

        dw_ref[...] += jnp.sum(dhv * xhat, axis=0, keepdims=True)

    row = pl.BlockSpec((tr, d), lambda i: (i, 0))
    vec = pl.BlockSpec((1, d), lambda i: (0, 0))
    extra = [] if after is None else [after]
    return pl.pallas_call(
        body, name=name, grid=(s // tr,),
        in_specs=[row, row, vec, row] + [ANY] * len(extra), out_specs=[row, row, vec],
        out_shape=[jax.ShapeDtypeStruct((s, d), F32), jax.ShapeDtypeStruct((s, d), BF16),
                   jax.ShapeDtypeStruct((1, d), F32)],
        compiler_params=_params(("arbitrary",)),
    )(dh, x, w, res, *extra)


def _conv_taps(x, w, rows):
    shifted = [x]
    for sft in (1, 2, 3):
        shifted.append(jnp.where(rows >= sft, pltpu.roll(x, sft, 0), 0.0))
    y = w[3:4, :] * shifted[0] + w[2:3, :] * shifted[1] + w[1:2, :] * shifted[2] + w[0:1, :] * shifted[3]
    return y, shifted


def _delta_pre_fwd(qkvz, conv_w, heads):
    s = qkvz.shape[0]
    nblk = 3 * heads

    def body(x_ref, w_ref, o_ref):
        part = pl.program_id(0) // heads
        rows = lax.broadcasted_iota(jnp.int32, (s, HEAD_DIM), 0)
        y, _ = _conv_taps(x_ref[...], w_ref[...], rows)
        a = y * _sigmoid(y)
        rs = lax.rsqrt(jnp.sum(a * a, axis=-1, keepdims=True) + EPS)
        fac = jnp.where(part == 0, rs * (HEAD_DIM ** -0.5), jnp.where(part == 1, rs, 1.0))
        o_ref[...] = a * fac

    return pl.pallas_call(
        body, name="delta_pre_fwd", grid=(nblk,),
        in_specs=[pl.BlockSpec((s, HEAD_DIM), lambda i: (0, i)), pl.BlockSpec((4, HEAD_DIM), lambda i: (0, i))],
        out_specs=pl.BlockSpec((s, HEAD_DIM), lambda i: (0, i)),
        out_shape=jax.ShapeDtypeStruct((s, 3 * heads * HEAD_DIM), F32),
        compiler_params=_params(("parallel",)),
    )(qkvz, conv_w)


def _delta_pre_bwd(dqkv, qkvz, conv_w, heads):
    s = qkvz.shape[0]
    nblk = 3 * heads

    def body(d_ref, x_ref, w_ref, dx_ref, dw_ref):
        part = pl.program_id(0) // heads
        rows = lax.broadcasted_iota(jnp.int32, (s, HEAD_DIM), 0)
        w = w_ref[...]
        y, shifted = _conv_taps(x_ref[...], w, rows)
        sg = _sigmoid(y)
        a = y * sg
        rs = lax.rsqrt(jnp.sum(a * a, axis=-1, keepdims=True) + EPS)
        unit = a * rs
        dn = d_ref[...]
        scale = jnp.where(part == 0, HEAD_DIM ** -0.5, 1.0)
        da_norm = scale * rs * (dn - unit * jnp.sum(dn * unit, axis=-1, keepdims=True))
        da = jnp.where(part < 2, da_norm, dn)
        dy = da * sg * (1.0 + y * (1.0 - sg))
        dx = w[3:4, :] * dy
        for sft in (1, 2, 3):
            dx = dx + w[3 - sft:4 - sft, :] * jnp.where(rows < s - sft, pltpu.roll(dy, s - sft, 0), 0.0)
        dx_ref[...] = dx.astype(BF16)
        for sft in range(4):
            dw_ref[3 - sft:4 - sft, :] = jnp.sum(dy * shifted[sft], axis=0, keepdims=True)

    col = pl.BlockSpec((s, HEAD_DIM), lambda i: (0, i))
    wsp = pl.BlockSpec((4, HEAD_DIM), lambda i: (0, i))
    return pl.pallas_call(
        body, name="delta_pre_bwd", grid=(nblk,),
        in_specs=[col, col, wsp], out_specs=[col, wsp],
        out_shape=[jax.ShapeDtypeStruct((s, 3 * heads * HEAD_DIM), BF16),
                   jax.ShapeDtypeStruct((4, 3 * heads * HEAD_DIM), F32)],
        compiler_params=_params(("parallel",)),
    )(dqkv, qkvz, conv_w)


def _heads_of(ref, heads):
    return jnp.stack([ref[:, h * HEAD_DIM:(h + 1) * HEAD_DIM] for h in range(heads)])


def _chunk_common(q, k, v, bd, a_log, dt_bias, heads, solved=None):
    c = CHUNK
    braw = jnp.stack([bd[:, h:h + 1] for h in range(heads)])
    draw = jnp.stack([bd[:, heads + h:heads + h + 1] for h in range(heads)])
    beta = _sigmoid(braw)
    xd = draw + dt_bias
    sp = jnp.maximum(xd, 0.0) + jnp.log1p(jnp.exp(-jnp.abs(xd)))
    g = -jnp.exp(a_log) * sp
    row = lax.broadcasted_iota(jnp.int32, (c, c), 0)
    col = lax.broadcasted_iota(jnp.int32, (c, c), 1)
    sq = (heads, c, c)
    g_b = jnp.broadcast_to(g, sq)
    g_row = _csum(jnp.where(row == col, g_b, 0.0))
    gam_col = _rsum(jnp.where(col <= row, jnp.broadcast_to(g_row, sq), 0.0))
    gam_row = _csum(jnp.where(row <= col, g_b, 0.0))
    causal = row >= col
    dm = jnp.where(causal, jnp.exp(jnp.where(causal, gam_col - gam_row, 0.0)), 0.0)
    kk = _dot(k, k, "nt")
    e = jnp.exp(gam_col)
    if solved is None:
        low = jnp.where(row > col, beta * kk * dm, 0.0)
        assert c in (INV_BLOCK, 2 * INV_BLOCK)
        same = (row // INV_BLOCK) == (col // INV_BLOCK)
        diag = jnp.where(same, low, 0.0)
        t = jnp.where(row == col, 1.0, 0.0) - diag
        pw = diag
        for _ in range((INV_BLOCK - 1).bit_length() - 1):
            pw = _dot(pw, pw)
            t = t + _dot(t, pw)
        if c > INV_BLOCK:
            t = t - _dot(_dot(t, low - diag), t)
        u = _dot(t, beta * v)
        w = _dot(t, (beta * e) * k)
    else:
        t, u, w = solved
    qk_raw = _dot(q, k, "nt")
    gl = _csum(g)
    el = jnp.exp(gl - gam_col)
    return dict(beta=beta, xd=xd, g=g, row=row, col=col, dm=dm, kk=kk, t=t, e=e, u=u, w=w,
                qk_raw=qk_raw, qk=qk_raw * dm, gl=gl, el=el, qd=e * q, kd=el * k, cd=jnp.exp(gl))


def _delta_chunk_fwd(qkv, bd, a_log, dt_bias, heads):
    s = qkv.shape[0]
    n = s // CHUNK
    dw = heads * HEAD_DIM
    blk = lambda part: pl.BlockSpec((CHUNK, dw), lambda i: (i, part))

    def body(q_ref, k_ref, v_ref, bd_ref, al_ref, dt_ref, o_ref, st_ref, t_ref, uw_ref, state):
        @pl.when(pl.program_id(0) == 0)
        def _():
            state[...] = jnp.zeros_like(state)

        cm = _chunk_common(_heads_of(q_ref, heads), _heads_of(k_ref, heads), _heads_of(v_ref, heads), bd_ref[...],
                           al_ref[...], dt_ref[...], heads)
        st = state[...]
        st_ref[0] = st
        t_ref[0] = cm["t"]
        uw_ref[0, 0] = cm["u"]
        uw_ref[0, 1] = cm["w"]
        vn = cm["u"] - _dot(cm["w"], st)
        o = _dot(cm["qd"], st) + _dot(cm["qk"], vn)
        for h in range(heads):
            o_ref[:, h * HEAD_DIM:(h + 1) * HEAD_DIM] = o[h]
        state[...] = cm["cd"] * st + _dot(cm["kd"], vn, "tn")

    smem = pl.BlockSpec((heads, 1, 1), lambda i: (0, 0, 0))
    return pl.pallas_call(
        body, name="delta_chunk_fwd", grid=(n,),
        in_specs=[blk(0), blk(1), blk(2), pl.BlockSpec((CHUNK, HEAD_DIM), lambda i: (i, 0)), smem, smem],
        out_specs=[pl.BlockSpec((CHUNK, dw), lambda i: (i, 0)),
                   pl.BlockSpec((1, heads, HEAD_DIM, HEAD_DIM), lambda i: (i, 0, 0, 0)),
                   pl.BlockSpec((1, heads, CHUNK, CHUNK), lambda i: (i, 0, 0, 0)),
                   pl.BlockSpec((1, 2, heads, CHUNK, HEAD_DIM), lambda i: (i, 0, 0, 0, 0))],
        out_shape=[jax.ShapeDtypeStruct((s, dw), F32),
                   jax.ShapeDtypeStruct((n, heads, HEAD_DIM, HEAD_DIM), F32),
                   jax.ShapeDtypeStruct((n, heads, CHUNK, CHUNK), F32),
                   jax.ShapeDtypeStruct((n, 2, heads, CHUNK, HEAD_DIM), F32)],
        scratch_shapes=[pltpu.VMEM((heads, HEAD_DIM, HEAD_DIM), F32)],
        compiler_params=_params(("arbitrary",)),
    )(qkv, qkv, qkv, bd, a_log, dt_bias)


def _delta_chunk_bwd(do, qkv, bd, saved, a_log, dt_bias, heads):
    s = qkv.shape[0]
    n = s // CHUNK
    dw = heads * HEAD_DIM
    c = CHUNK
    blk = lambda part: pl.BlockSpec((CHUNK, dw), lambda i: (n - 1 - i, part))

    def all_heads(q, k, v, dov, st, solved, dsn, bd, a_log, dt_bias):
        cm = _chunk_common(q, k, v, bd, a_log, dt_bias, heads, solved)
        beta, e, dm, row, col = cm["beta"], cm["e"], cm["dm"], cm["row"], cm["col"]
        sq = (heads, c, c)
        vn = cm["u"] - _dot(cm["w"], st)
        dvn = _dot(cm["kd"], dsn)
        dkd = _dot(vn, dsn, "nt")
        dcd = _csum(_rsum(st * dsn))
        ds = cm["cd"] * dsn
        dqd = _dot(dov, st, "nt")
        ds = ds + _dot(cm["qd"], dov, "tn")
        dqk = _dot(dov, vn, "nt")
        dvn = dvn + _dot(cm["qk"], dov, "tn")
        dw_ = -_dot(dvn, st, "nt")
        ds = ds - _dot(cm["w"], dvn, "tn")
        drhs_u = _dot(cm["t"], dvn, "tn")
        drhs_w = _dot(cm["t"], dw_, "tn")
        da = -(_dot(drhs_u, cm["u"], "nt") + _dot(drhs_w, cm["w"], "nt"))
        dl = jnp.where(row > col, da, 0.0)
        dbeta = _rsum(dl * cm["kk"] * dm)
        dkk = dl * beta * dm
        dd = dl * beta * cm["kk"]
        dv = beta * drhs_u
        ek = e * k
        dbeta = dbeta + _rsum(drhs_u * v) + _rsum(drhs_w * ek)
        dk = (beta * e) * drhs_w
        dgam = _rsum(drhs_w * (beta * ek))
        dqkm = dqk * dm
        dq = _dot(dqkm, k)
        dk = dk + _dot(dqkm, q, "tn")
        dd = dd + dqk * cm["qk_raw"]
        dk = dk + _dot(dkk, k) + _dot(dkk, k, "tn")
        dq = dq + e * dqd
        dgam = dgam + _rsum(dqd * cm["qd"])
        dk = dk + cm["el"] * dkd
        r = _rsum(dkd * cm["kd"])
        dgam = dgam - r
        dgl = _csum(r) + dcd * cm["cd"]
        mm = dd * dm
        colsum_c = _rsum(jnp.where(row == col, jnp.broadcast_to(_csum(mm), sq), 0.0))
        dgam = dgam + _rsum(mm) - colsum_c
        ridx = lax.broadcasted_iota(jnp.int32, (c, 1), 0)
        dgam = dgam + jnp.where(ridx == c - 1, dgl, 0.0)
        dgam_row = _csum(jnp.where(row == col, jnp.broadcast_to(dgam, sq), 0.0))
        dg = _rsum(jnp.where(col >= row, jnp.broadcast_to(dgam_row, sq), 0.0))
        d_xd = dg * (-jnp.exp(a_log)) * _sigmoid(cm["xd"])
        d_braw = dbeta * beta * (1.0 - beta)
        d_alog = dg * cm["g"]
        lane = lax.broadcasted_iota(jnp.int32, (c, HEAD_DIM), 1)
        dbd = jnp.zeros((c, HEAD_DIM), F32)
        for h in range(heads):
            dbd = (dbd + jnp.where(lane == h, d_braw[h], 0.0) + jnp.where(lane == h + heads, d_xd[h], 0.0)
                   + jnp.where(lane == h + 2 * heads, d_alog[h], 0.0))
        return dq, dk, dv, ds, dbd

    def body(do_ref, q_ref, k_ref, v_ref, bd_ref, st_ref, t_ref, uw_ref, al_ref, dt_ref, dqkv_ref, dbd_ref, dstate):
        @pl.when(pl.program_id(0) == 0)
        def _():
            dstate[...] = jnp.zeros_like(dstate)

        dq, dk, dv, ds, dbd = all_heads(_heads_of(q_ref, heads), _heads_of(k_ref, heads), _heads_of(v_ref, heads),
                                        _heads_of(do_ref, heads), st_ref[0], (t_ref[0], uw_ref[0, 0], uw_ref[0, 1]),
                                        dstate[...], bd_ref[...],
                                        al_ref[...], dt_ref[...])
        for part, val in enumerate((dq, dk, dv)):
            for h in range(heads):
                lo = part * dw + h * HEAD_DIM
                dqkv_ref[:, lo:lo + HEAD_DIM] = val[h]
        dstate[...] = ds
        dbd_ref[...] = dbd

    smem = pl.BlockSpec((heads, 1, 1), lambda i: (0, 0, 0))
    shared = pl.BlockSpec((CHUNK, HEAD_DIM), lambda i: (n - 1 - i, 0))
    wide = pl.BlockSpec((CHUNK, dw), lambda i: (n - 1 - i, 0))
    return pl.pallas_call(
        body, name="delta_chunk_bwd", grid=(n,),
        in_specs=[wide, blk(0), blk(1), blk(2), shared,
                  pl.BlockSpec((1, heads, HEAD_DIM, HEAD_DIM), lambda i: (n - 1 - i, 0, 0, 0)),
                  pl.BlockSpec((1, heads, CHUNK, CHUNK), lambda i: (n - 1 - i, 0, 0, 0)),
                  pl.BlockSpec((1, 2, heads, CHUNK, HEAD_DIM), lambda i: (n - 1 - i, 0, 0, 0, 0)), smem, smem],
        out_specs=[pl.BlockSpec((CHUNK, 3 * dw), lambda i: (n - 1 - i, 0)), shared],
        out_shape=[jax.ShapeDtypeStruct((s, 3 * dw), F32), jax.ShapeDtypeStruct((s, HEAD_DIM), F32)],
        scratch_shapes=[pltpu.VMEM((heads, HEAD_DIM, HEAD_DIM), F32)],
        compiler_params=_params(("arbitrary",)),
    )(do, qkv, qkv, qkv, bd, *saved, a_log, dt_bias)


def _delta_post_fwd(o, qkvz, w, heads):
    s = o.shape[0]
    tr = _tile(s, 4096, 8)

    def body(o_ref, z_ref, w_ref, out_ref):
        ov, z = o_ref[...], z_ref[...]
        rstd = lax.rsqrt(jnp.mean(ov * ov, axis=-1, keepdims=True) + EPS)
        out_ref[...] = (ov * rstd * w_ref[...] * (z * _sigmoid(z))).astype(BF16)

    return pl.pallas_call(
        body, name="delta_post_fwd", grid=(s // tr, heads),
        in_specs=[pl.BlockSpec((tr, HEAD_DIM), lambda i, h: (i, h)),
                  pl.BlockSpec((tr, HEAD_DIM), lambda i, h: (i, 3 * heads + h)),
                  pl.BlockSpec((1, HEAD_DIM), lambda i, h: (0, 0))],
        out_specs=pl.BlockSpec((tr, HEAD_DIM), lambda i, h: (i, h)),
        out_shape=jax.ShapeDtypeStruct((s, heads * HEAD_DIM), BF16),
        compiler_params=_params(("parallel", "parallel")),
    )(o, qkvz, w)


def _rope_tables(positions, s):
    half = HEAD_DIM // 2
    inv_freq = ROPE_THETA ** (-jnp.arange(half, dtype=F32) / half)
    ang = positions.reshape(s, 1).astype(F32) * inv_freq
    cos, sin = jnp.cos(ang), jnp.sin(ang)
    return jnp.concatenate([cos, cos], axis=-1), jnp.concatenate([-sin, sin], axis=-1)


def _attn_pre_fwd(aqkv, wq, wk, cosf, sinf, heads):
    s = aqkv.shape[0]
    tr = _tile(s, 4096, 8)

    def body(x_ref, wq_ref, wk_ref, c_ref, s_ref, o_ref):
        xv = x_ref[...]
        wv = jnp.where(pl.program_id(1) < heads, wq_ref[...], wk_ref[...])
        y = xv * lax.rsqrt(jnp.mean(xv * xv, axis=-1, keepdims=True) + EPS) * wv
        o_ref[...] = y * c_ref[...] + pltpu.roll(y, HEAD_DIM // 2, 1) * s_ref[...]

    blk = pl.BlockSpec((tr, HEAD_DIM), lambda i, j: (i, j))
    vec = pl.BlockSpec((1, HEAD_DIM), lambda i, j: (0, 0))
    tab = pl.BlockSpec((tr, HEAD_DIM), lambda i, j: (i, 0))
    return pl.pallas_call(
        body, name="attn_pre_fwd", grid=(s // tr, 2 * heads),
        in_specs=[blk, vec, vec, tab, tab], out_specs=blk,
        out_shape=jax.ShapeDtypeStruct((s, 2 * heads * HEAD_DIM), F32),
        compiler_params=_params(("parallel", "parallel")),
    )(aqkv, wq, wk, cosf, sinf)


def _band():
    qi = lax.broadcasted_iota(jnp.int32, (SPAN, 2 * SPAN), 0)
    ki = lax.broadcasted_iota(jnp.int32, (SPAN, 2 * SPAN), 1)
    dist = qi + SPAN - ki
    return (dist >= 0) & (dist <= SPAN), ki >= SPAN


def _sub(g, r, d):
    if d == 1:
        return pl.ds(g * SPAN, SPAN)
    return pl.ds(g * SPAN * d + r, SPAN, stride=d)


def _attn_blocks(s):
    assert s % (SPAN * max(DILATIONS)) == 0
    return [(p_i, d, r, g) for p_i, d in enumerate(DILATIONS) for r in range(d) for g in range(s // (SPAN * d))]


def _attn_fwd(qk, aqkv, w, heads):
    s = qk.shape[0]
    aw = heads * HEAD_DIM

    def body(q_ref, k_ref, v_ref, w_ref, mix_ref, acc_ref, m_ref, l_ref):
        band, own = _band()
        for p_i, d, r, g in _attn_blocks(s):
            rows = _sub(g, r, d)
            kc, vc = k_ref[rows, :], v_ref[rows, :]
            if g == 0:
                kp, vp, mask = kc, vc, band & own
            else:
                mask = band
            kcat = jnp.concatenate([kp, kc], axis=0)
            vcat = jnp.concatenate([vp, vc], axis=0)
            kp, vp = kc, vc
            sc = _dot(q_ref[rows, :] * (HEAD_DIM ** -0.5), kcat, "nt")
            sc = jnp.where(mask, sc, NEG)
            m = jnp.max(sc, axis=-1, keepdims=True)
            if p_i == 0:
                p = jnp.exp(sc - m)
                acc_ref[rows, :] = _dot(p, vcat)
                l_new = jnp.sum(p, axis=-1, keepdims=True)
            else:
                m_old = m_ref[rows, 0:1]
                m = jnp.maximum(m, m_old)
                alpha = jnp.exp(m_old - m)
                p = jnp.exp(sc - m)
                acc_ref[rows, :] = alpha * acc_ref[rows, :] + _dot(p, vcat)
                l_new = alpha * l_ref[rows, 0:1] + jnp.sum(p, axis=-1, keepdims=True)
            m_ref[rows, :] = jnp.broadcast_to(m, (SPAN, HEAD_DIM))
            l_ref[rows, :] = jnp.broadcast_to(l_new, (SPAN, HEAD_DIM))
        den = l_ref[...]
        ob = acc_ref[...] / den
        acc_ref[...] = ob
        m_ref[...] = m_ref[...] + jnp.log(den)
        rstd = lax.rsqrt(jnp.mean(ob * ob, axis=-1, keepdims=True) + EPS)
        mix_ref[...] = (ob * rstd * w_ref[...]).astype(BF16)

    col = lambda off: pl.BlockSpec((s, HEAD_DIM), lambda h: (0, off + h))
    return pl.pallas_call(
        body, name="attn_fwd", grid=(heads,),
        in_specs=[col(0), col(heads), col(2 * heads), pl.BlockSpec((1, HEAD_DIM), lambda h: (0, 0))],
        out_specs=[col(0), col(0), col(0)],
        out_shape=[jax.ShapeDtypeStruct((s, aw), BF16), jax.ShapeDtypeStruct((s, aw), F32),
                   jax.ShapeDtypeStruct((s, aw), F32)],
        scratch_shapes=[pltpu.VMEM((s, HEAD_DIM), F32)],
        compiler_params=_params(("parallel",)),
    )(qk, qk, aqkv, w)


def _attn_bwd(qk, aqkv, do, lse, dsum, heads):
    s = qk.shape[0]
    aw = heads * HEAD_DIM
    scale = HEAD_DIM ** -0.5

    def body(q_ref, k_ref, v_ref, do_ref, l_ref, ds_ref, out_ref):
        band, own = _band()
        dq_ref, dk_ref, dv_ref = out_ref.at[0], out_ref.at[1], out_ref.at[2]
        out_ref[...] = jnp.zeros((3, s, HEAD_DIM), F32)
        for _, d, r, g in _attn_blocks(s):
            rows = _sub(g, r, d)
            qs, dov = q_ref[rows, :] * scale, do_ref[rows, :]
            kc, vc = k_ref[rows, :], v_ref[rows, :]
            if g == 0:
                kp, vp, mask = kc, vc, band & own
            else:
                mask = band
            kcat = jnp.concatenate([kp, kc], axis=0)
            vcat = jnp.concatenate([vp, vc], axis=0)
            p = jnp.where(mask, jnp.exp(_dot(qs, kcat, "nt") - l_ref[rows, 0:1]), 0.0)
            dsc = p * (_dot(dov, vcat, "nt") - ds_ref[rows, 0:1])
            dq_ref[rows, :] += scale * _dot(dsc, kcat)
            dk = _dot(dsc, qs, "tn")
            dv = _dot(p, dov, "tn")
            if g > 0:
                dk_ref[prows, :] += dk_own + dk[:SPAN]
                dv_ref[prows, :] += dv_own + dv[:SPAN]
            dk_own, dv_own = dk[SPAN:], dv[SPAN:]
            if g == s // (SPAN * d) - 1:
                dk_ref[rows, :] += dk_own
                dv_ref[rows, :] += dv_own
            kp, vp, prows = kc, vc, rows

    col = lambda off: pl.BlockSpec((s, HEAD_DIM), lambda h: (0, off + h))
    return pl.pallas_call(
        body, name="attn_bwd", grid=(heads,),
        in_specs=[col(0), col(heads), col(2 * heads), col(0), col(0), col(0)],
        out_specs=pl.BlockSpec((3, s, HEAD_DIM), lambda h: (0, 0, h)),
        out_shape=jax.ShapeDtypeStruct((3, s, aw), F32),
        compiler_params=_params(("parallel",)),
    )(qk, qk, aqkv, do, lse, dsum)


def _attn_pre_bwd(grads, aqkv, wq, wk, cosf, sinf, heads):
    s = aqkv.shape[0]
    tr = _tile(s, 4096, 8)
    nrow = s // tr

    def body(g_ref, x_ref, wq_ref, wk_ref, c_ref, s_ref, dx_ref, dwq_ref, dwk_ref):
        i, j = pl.program_id(0), pl.program_id(1)
        kind = j // heads
        dout = g_ref[0]
        tot_v = dout
        dy = dout * c_ref[...] + pltpu.roll(dout * s_ref[...], HEAD_DIM // 2, 1)
        xv = x_ref[...]
        wv = jnp.where(kind == 0, wq_ref[...], wk_ref[...])
        rstd = lax.rsqrt(jnp.mean(xv * xv, axis=-1, keepdims=True) + EPS)
        xhat = xv * rstd
        gw = dy * wv
        dxn = rstd * (gw - xhat * jnp.mean(gw * xhat, axis=-1, keepdims=True))
        dx_ref[...] = jnp.where(kind == 2, tot_v, dxn).astype(BF16)
        dwc = jnp.sum(dy * xhat, axis=0, keepdims=True)

        @pl.when((i == 0) & (j == 0))
        def _():
            dwq_ref[...] = jnp.zeros_like(dwq_ref)
            dwk_ref[...] = jnp.zeros_like(dwk_ref)

        @pl.when(kind == 0)
        def _():
            dwq_ref[...] += dwc

        @pl.when(kind == 1)
        def _():
            dwk_ref[...] += dwc

    grad = pl.BlockSpec((1, tr, HEAD_DIM), lambda i, j: (j // heads, i, j % heads))
    blk = pl.BlockSpec((tr, HEAD_DIM), lambda i, j: (i, j))
    vec = pl.BlockSpec((1, HEAD_DIM), lambda i, j: (0, 0))
    tab = pl.BlockSpec((tr, HEAD_DIM), lambda i, j: (i, 0))
    return pl.pallas_call(
        body, name="attn_pre_bwd", grid=(nrow, 3 * heads),
        in_specs=[grad, blk, vec, vec, tab, tab], out_specs=[blk, vec, vec],
        out_shape=[jax.ShapeDtypeStruct((s, 3 * heads * HEAD_DIM), BF16),
                   jax.ShapeDtypeStruct((1, HEAD_DIM), F32), jax.ShapeDtypeStruct((1, HEAD_DIM), F32)],
        compiler_params=_params(("arbitrary", "arbitrary")),
    )(grads, aqkv, wq, wk, cosf, sinf)


def _colsum(a, name):
    s, d = a.shape
    tr = _tile(s, 4096, 8)

    def body(a_ref, o_ref):
        @pl.when(pl.program_id(0) == 0)
        def _():
            o_ref[...] = jnp.zeros_like(o_ref)

        o_ref[...] += jnp.sum(a_ref[...], axis=0, keepdims=True)

    return pl.pallas_call(
        body, name=name, grid=(s // tr,),
        in_specs=[pl.BlockSpec((tr, d), lambda i: (i, 0))], out_specs=pl.BlockSpec((1, d), lambda i: (0, 0)),
        out_shape=jax.ShapeDtypeStruct((1, d), F32),
        compiler_params=_params(("arbitrary",)),
    )(a)


def _local_step(x, positions, target, small, w_qkvz, w_bd, w_attn, conv_w, later_weights, ffn_grads_ready,
                rest_grads_ready, after=None):
    s, dmod = x.shape
    heads = dmod // (2 * HEAD_DIM)
    dw = heads * HEAD_DIM
    a_log, dt_bias = small["a_log"].reshape(heads, 1, 1), small["dt_bias"].reshape(heads, 1, 1)
    cosf, sinf = _rope_tables(positions, s)

    h1 = _rmsnorm_fwd(x, small["attn_norm_w"], "norm1_fwd", after=after)
    qkvz = _matmul(h1, w_qkvz, "nn", "proj_qkvz")
    bd = _matmul(h1, w_bd, "nn", "proj_bd")
    aqkv = _matmul(h1, w_attn, "nn", "proj_attn")
    dqkv = _delta_pre_fwd(qkvz, conv_w, heads)
    o_d, *saved = _delta_chunk_fwd(dqkv, bd, a_log, dt_bias, heads)
    mix_a = _delta_post_fwd(o_d, qkvz, small["delta_out_norm_w"], heads)
    qk_rot = _attn_pre_fwd(aqkv, small["q_norm_w"], small["k_norm_w"], cosf, sinf, heads)
    mix_b, ob, lse = _attn_fwd(qk_rot, aqkv, small["attn_out_norm_w"], heads)
    w_out, behind, ffn_weights = later_weights((mix_a, mix_b))
    x1 = _matmul(mix_a, w_out, "nn", "out_proj", add=x, b_rows=(0, dw), more=[(mix_b, w_out, (dw, dw))], after=behind)
    h2 = _rmsnorm_fwd(x1, small["ffn_norm_w"], "norm2_fwd")
    w_gu, w_down = ffn_weights(h2)
    ff = w_down.shape[0]

    def swiglu(accs, vals):
        g, u = accs
        return g, u, g * _sigmoid(g) * u

    gate, up, act = _matmul(h2, w_gu, "nn", "ffn_gate_up", b_cols=(0, ff), b2_cols=(ff, ff), finish=swiglu,
                            out_dtypes=[BF16, BF16, BF16])

    def loss_head(acc, vals):
        err = acc + vals[0] - vals[1]
        part = 0.5 * jnp.sum(jnp.sum(err * err, axis=-1, keepdims=True) * (1.0 / dmod), axis=0, keepdims=True)
        lane = lax.broadcasted_iota(jnp.int32, (1, HEAD_DIM), 1)
        return err * (1.0 / dmod), err * (1.0 / dmod), jnp.where(lane == 0, part, 0.0)

    dy, dy16, loss_row = _matmul(act, w_down, "nn", "ffn_down", tiles=[(x1, 0), (target, 0)], finish=loss_head,
                                 out_dtypes=[F32, BF16], row_sum=True)

    def swiglu_bwd(acc, vals):
        g, u = vals[0].astype(F32), vals[1].astype(F32)
        sg = _sigmoid(g)
        return acc * u * sg * (1.0 + g * (1.0 - sg)), acc * g * sg

    dgate, dup = _matmul(dy16, w_down, "nt", "ffn_down_dx", tiles=[(gate, 0), (up, 0)], finish=swiglu_bwd,
                         out_dtypes=[BF16, BF16])
    g_w_down = _matmul(act, dy16, "tn", "ffn_down_dw")
    g_w_gate = _matmul(h2, dgate, "tn", "ffn_gate_dw")
    g_w_up = _matmul(h2, dup, "tn", "ffn_up_dw")
    behind, and_then = ffn_grads_ready(g_w_gate, g_w_up, g_w_down)
    dh2 = _matmul(dgate, w_gu, "nt", "ffn_gate_dx", b_cols=(0, ff), after=behind)
    dh2 = _matmul(dup, w_gu, "nt", "ffn_up_dx", b_cols=(ff, ff), add=dh2)
    behind = and_then(dh2)
    dx1, dx1_16, g_ffn_norm = _rmsnorm_bwd(dh2, x1, small["ffn_norm_w"], dy, "norm2_bwd", after=behind)
    def per_head(fn, tile_w):
        outs = [fn(slice(h * HEAD_DIM, (h + 1) * HEAD_DIM)) for h in range(tile_w // HEAD_DIM)]
        joined = [jnp.concatenate([o[i] for o in outs], axis=1) for i in range(len(outs[0]) - 1)]
        return (*joined, sum(o[-1] for o in outs))

    def delta_post_bwd(acc, vals):
        def one(hs):
            d, ov, z, wv = acc[:, hs], vals[0][:, hs], vals[1][:, hs], vals[2][:, hs]
            sg = _sigmoid(z)
            rstd = lax.rsqrt(jnp.mean(ov * ov, axis=-1, keepdims=True) + EPS)
            ohat = ov * rstd
            dn = d * (z * sg)
            gw = dn * wv
            return (rstd * (gw - ohat * jnp.mean(gw * ohat, axis=-1, keepdims=True)),
                    d * (ohat * wv) * sg * (1.0 + z * (1.0 - sg)), jnp.sum(dn * ohat, axis=0, keepdims=True))
        return per_head(one, acc.shape[1])

    def attn_merge_bwd(acc, vals):
        def one(hs):
            d, ov, wv = acc[:, hs], vals[0][:, hs], vals[1][:, hs]
            rstd = lax.rsqrt(jnp.mean(ov * ov, axis=-1, keepdims=True) + EPS)
            ohat = ov * rstd
            gw = d * wv
            dov = rstd * (gw - ohat * jnp.mean(gw * ohat, axis=-1, keepdims=True))
            return (dov, jnp.broadcast_to(jnp.sum(dov * ov, axis=-1, keepdims=True), dov.shape),
                    jnp.sum(d * ohat, axis=0, keepdims=True))
        return per_head(one, acc.shape[1])

    rep = lambda wv: jnp.tile(wv, (1, heads))
    do_d, dz, g_delta_out_norm = _matmul(
        dx1_16, w_out, "nt", "out_proj_dx_a", b_rows=(0, dw), finish=delta_post_bwd, out_dtypes=[F32, BF16],
        tiles=[(o_d, 0), (qkvz, 3 * dw), (rep(small["delta_out_norm_w"]), 0)], row_sum=True)
    dob, dsum, g_attn_out_norm = _matmul(
        dx1_16, w_out, "nt", "out_proj_dx_b", b_rows=(dw, dw), finish=attn_merge_bwd, out_dtypes=[F32, F32],
        tiles=[(ob, 0), (rep(small["attn_out_norm_w"]), 0)], row_sum=True)
    g_w_out_a = _matmul(mix_a, dx1_16, "tn", "out_proj_dw_a")
    g_w_out_b = _matmul(mix_b, dx1_16, "tn", "out_proj_dw_b")
    grads = _attn_bwd(qk_rot, aqkv, dob, lse, dsum, heads)
    d_aqkv, g_q_norm, g_k_norm = _attn_pre_bwd(grads, aqkv, small["q_norm_w"], small["k_norm_w"], cosf, sinf, heads)
    ddqkv, dbd = _delta_chunk_bwd(do_d, dqkv, bd, saved, a_log, dt_bias, heads)
    d_qkv_raw, g_conv = _delta_pre_bwd(ddqkv, qkvz, conv_w, heads)
    bd_sums = _colsum(dbd, "bd_colsum")
    g_w_qkv = _matmul(h1, d_qkv_raw, "tn", "proj_qkv_dw")
    g_w_z = _matmul(h1, dz, "tn", "proj_z_dw")
    g_w_bd = _matmul(h1, dbd, "tn", "proj_bd_dw")
    g_w_attn = _matmul(h1, d_aqkv, "tn", "proj_attn_dw")
    behind, and_then = rest_grads_ready(dict(w_qkv=g_w_qkv, w_z=g_w_z, w_bd=g_w_bd, w_attn=g_w_attn,
                                             w_out_a=g_w_out_a, w_out_b=g_w_out_b))
    dh1 = _matmul(d_qkv_raw, w_qkvz, "nt", "proj_qkvz_dx", b_cols=(0, 3 * dw), more=[(dz, w_qkvz, (3 * dw, dw))],
                  after=behind)
    dh1 = _matmul(d_aqkv, w_attn, "nt", "proj_attn_dx", add=dh1, after=and_then(dh1))
    dh1 = _matmul(dbd, w_bd, "nt", "proj_bd_dx", add=dh1)
    grad_x, _, g_attn_norm = _rmsnorm_bwd(dh1, x, small["attn_norm_w"], dx1, "norm1_bwd")
    small_grads = dict(
        attn_norm_w=g_attn_norm, a_log=bd_sums[:, 2 * heads:3 * heads], dt_bias=bd_sums[:, heads:2 * heads],
        delta_out_norm_w=g_delta_out_norm, q_norm_w=g_q_norm, k_norm_w=g_k_norm,
        attn_out_norm_w=g_attn_out_norm, ffn_norm_w=g_ffn_norm, conv_w=g_conv)
    return loss_row, grad_x, small_grads


def _adamw(w, g, m, v, name):
    r, c = w.shape
    tr = _tile(r, 256, 8)

    def body(w_ref, g_ref, m_ref, v_ref, d_ref, nm_ref, nv_ref):
        gv = g_ref[...]
        nm = ADAM_B1 * m_ref[...] + (1.0 - ADAM_B1) * gv
        nv = ADAM_B2 * v_ref[...] + (1.0 - ADAM_B2) * (gv * gv)
        m_hat = nm / (1.0 - ADAM_B1 ** ADAM_STEP)
        v_hat = nv / (1.0 - ADAM_B2 ** ADAM_STEP)
        d_ref[...] = -ADAM_LR * (m_hat / (jnp.sqrt(v_hat) + ADAM_EPS) + ADAM_WD * w_ref[...])
        nm_ref[...] = nm
        nv_ref[...] = nv

    blk = pl.BlockSpec((tr, c), lambda i: (i, 0))
    return pl.pallas_call(
        body, name=name, grid=(r // tr,),
        in_specs=[blk] * 4, out_specs=[blk] * 3,
        out_shape=[jax.ShapeDtypeStruct((r, c), F32)] * 3,
        compiler_params=_params(("parallel",)),
    )(w, g, m, v)


def _add_half_bf16(g, b, place, name):
    n, half, c = b.shape
    tr = _tile(half, 512, 16)
    nb = half // tr

    def body(place_ref, g_ref, b_ref, o_ref):
        o_ref[...] = (g_ref[...].astype(F32) + b_ref[...].astype(F32)).astype(BF16)

    blk = pl.BlockSpec((1, tr, c), lambda i, j, p: (i, j, 0))
    return pl.pallas_call(
        body, name=name,
        grid_spec=pltpu.PrefetchScalarGridSpec(
            num_scalar_prefetch=1, grid=(n, nb),
            in_specs=[pl.BlockSpec((1, tr, c), lambda i, j, p: (i, p[0] * nb + j, 0)), blk], out_specs=blk),
        out_shape=jax.ShapeDtypeStruct((n, half, c), BF16),
        compiler_params=_params(("parallel", "parallel")),
    )(place, g, b)


def _sum4_f32(mine, others, place, name):
    _, half, c = mine.shape
    tr = _tile(half, 512, 16)
    nb = half // tr

    def body(place_ref, a_ref, b_ref, o_ref):
        acc = a_ref[0].astype(F32)
        for j in range(3):
            acc = acc + b_ref[j].astype(F32)
        o_ref[...] = acc

    return pl.pallas_call(
        body, name=name,
        grid_spec=pltpu.PrefetchScalarGridSpec(
            num_scalar_prefetch=1, grid=(nb,),
            in_specs=[pl.BlockSpec((1, tr, c), lambda i, p: (p[1], i, 0)),
                      pl.BlockSpec((3, tr, c), lambda i, p: (0, i, 0))],
            out_specs=pl.BlockSpec((tr, c), lambda i, p: (p[0] * nb + i, 0))),
        out_shape=jax.ShapeDtypeStruct((2 * half, c), F32),
        compiler_params=_params(("parallel",)),
    )(place, mine, others)


def _place():
    x, y, c = lax.axis_index("x"), lax.axis_index("y"), lax.axis_index("c")
    other_chips = [(1 - x, y), (x, 1 - y), (1 - x, 1 - y)]
    return x, y, c, (x, y, 1 - c), other_chips


ANY = pl.BlockSpec(memory_space=pl.ANY)


def _remote(k, src, dst, to, send_sems, recv_sems):
    return pltpu.make_async_remote_copy(src_ref=src, dst_ref=dst, send_sem=send_sems.at[k], recv_sem=recv_sems.at[k],
                                        device_id=to, device_id_type=MESH)


def _half(ref, lead, hc):
    if lead is None:
        half = ref.shape[0] // 2
        return ref.at[pl.ds(hc * half, half), :]
    half = ref.shape[1] // 2
    return ref.at[lead, pl.ds(hc * half, half), :]


def _all_gather_weights(slots, whole, name):
    nt, nw = len(slots), len(whole)
    base_w, base_f, base_d = 2 * nt, 2 * nt + 3 * nw, 4 * nt + 3 * nw

    def quarter(ref, lead, hc, q):
        quart = ref.shape[1] // 4
        return ref.at[lead, pl.ds((2 * hc + q) * quart, quart), :]

    def body(*refs):
        ins, outs = refs[:nt + nw], refs[nt + nw:2 * (nt + nw)]
        sems = refs[2 * (nt + nw):]
        x, y, c, sibling, chips = _place()
        me, xn, yn, dg = 2 * x + y, 2 * (1 - x) + y, 2 * x + 1 - y, 2 * (1 - x) + 1 - y
        to_x, to_y = (1 - x, y, c), (x, 1 - y, c)
        cps = []
        for t in range(nt):
            cps.append(_remote(2 * t, _half(ins[t], me, c), _half(outs[t], me, c), to_x, *sems))
            cps.append(_remote(2 * t + 1, _half(ins[t], me, c), _half(outs[t], me, c), to_y, *sems))
        for j, (px, py) in enumerate(chips):
            for t in range(nw):
                cps.append(_remote(base_w + j * nw + t, ins[nt + t].at[me], outs[nt + t].at[me], (px, py, c), *sems))
        for cp in cps:
            cp.start()

        def start(k, ref, to):
            cp = _remote(k, ref, ref, to, *sems)
            cp.start()
            cps.append(cp)

        for t in range(nt):
            landed = _half(outs[t], xn, c)
            _remote(2 * t, landed, landed, to_x, *sems).wait_recv()
            start(base_f + 2 * t, quarter(outs[t], xn, c, 0), to_y)
            start(base_d + 3 * t, landed, sibling)
            landed = _half(outs[t], yn, c)
            _remote(2 * t + 1, landed, landed, to_y, *sems).wait_recv()
            start(base_f + 2 * t + 1, quarter(outs[t], yn, c, 1), to_x)
            start(base_d + 3 * t + 1, landed, sibling)
        for t in range(nt):
            q0, q1 = quarter(outs[t], dg, c, 0), quarter(outs[t], dg, c, 1)
            _remote(base_f + 2 * t, q0, q0, to_y, *sems).wait_recv()
            _remote(base_f + 2 * t + 1, q1, q1, to_x, *sems).wait_recv()
            start(base_d + 3 * t + 2, _half(outs[t], dg, c), sibling)
        for j, (px, py) in enumerate(chips):
            for t in range(nw):
                landed = outs[nt + t].at[2 * px + py]
                _remote(base_w + j * nw + t, landed, landed, (px, py, c), *sems).wait_recv()
        for t in range(nt):
            for j, chip in enumerate((xn, yn, dg)):
                other = _half(outs[t], chip, 1 - c)
                _remote(base_d + 3 * t + j, other, other, sibling, *sems).wait_recv()
        for cp in cps:
            cp.wait_send()

    arrays = list(slots) + list(whole)
    n_sem = 7 * nt + 3 * nw
    return pl.pallas_call(
        body, name=name, in_specs=[ANY] * len(arrays), out_specs=[ANY] * len(arrays),
        input_output_aliases={i: i for i in range(len(arrays))},
        out_shape=[jax.ShapeDtypeStruct(a.shape, a.dtype) for a in arrays],
        scratch_shapes=[pltpu.SemaphoreType.DMA((n_sem,)), pltpu.SemaphoreType.DMA((n_sem,))],
    )(*arrays)


def _join_halves(fs, name):
    nt = len(fs)

    def body(*refs):
        in_refs, out_refs, (send_sems, recv_sems) = refs[:nt], refs[nt:2 * nt], refs[2 * nt:]
        _, _, c, sibling, _ = _place()
        cps = [_remote(t, _half(in_refs[t], None, c), _half(out_refs[t], None, c), sibling, send_sems, recv_sems)
               for t in range(nt)]
        for cp in cps:
            cp.start()
        for t in range(nt):
            theirs = _half(out_refs[t], None, 1 - c)
            _remote(t, theirs, theirs, sibling, send_sems, recv_sems).wait_recv()
        for cp in cps:
            cp.wait_send()

    return pl.pallas_call(
        body, name=name, in_specs=[ANY] * nt, out_specs=[ANY] * nt,
        input_output_aliases={i: i for i in range(nt)},
        out_shape=[jax.ShapeDtypeStruct(f.shape, f.dtype) for f in fs],
        scratch_shapes=[pltpu.SemaphoreType.DMA((nt,)), pltpu.SemaphoreType.DMA((nt,))],
    )(*fs)


HBM = pl.BlockSpec(memory_space=pltpu.HBM)
SEM = pl.BlockSpec(memory_space=pltpu.SEMAPHORE)
EFFECT = pltpu.SideEffectType.DATAFLOW_SIDE_EFFECTING


def _split_start(arrays, after, plan, n_copies, name):
    na = len(arrays)

    def body(*refs):
        ins, send_sems, recv_sems = refs[:na], refs[na + 1], refs[na + 2]
        outs, token = refs[na + 3:2 * na + 3], refs[2 * na + 3]
        for k, (src, dst, to) in enumerate(plan(ins, outs)):
            _remote(k, src, dst, to, send_sems, recv_sems).start()
        token[...] = jnp.zeros_like(token)

    res = pl.pallas_call(
        body, name=name,
        out_shape=(pltpu.SemaphoreType.DMA((n_copies,)), pltpu.SemaphoreType.DMA((n_copies,)),
                   *[pltpu.HBM(a.shape, a.dtype) for a in arrays], jax.ShapeDtypeStruct((8, HEAD_DIM), F32)),
        in_specs=[HBM] * na + [ANY],
        out_specs=(SEM, SEM, *[HBM] * na, pl.BlockSpec(memory_space=pltpu.VMEM)),
        input_output_aliases={i: 2 + i for i in range(na)},
        compiler_params=pltpu.CompilerParams(has_side_effects=EFFECT),
    )(*[pltpu.with_memory_space_constraint(a, pltpu.HBM) for a in arrays], after)
    return res[0], res[1], list(res[2:2 + na]), res[2 + na]


def _split_wait(send_sems, recv_sems, arrays, after, plan, name):
    na = len(arrays)
    after = list(after) if isinstance(after, (list, tuple)) else [after]

    def body(*refs):
        ins, send, recv, outs = refs[:na], refs[na], refs[na + 1], refs[na + 2 + len(after):]
        for k, (src, dst, to) in enumerate(plan(ins, outs)):
            cp = _remote(k, src, dst, to, send, recv)
            cp.wait_send()
            cp.wait_recv()

    res = pl.pallas_call(
        body, name=name, out_shape=tuple(pltpu.HBM(a.shape, a.dtype) for a in arrays),
        in_specs=[HBM] * na + [SEM, SEM] + [ANY] * len(after), out_specs=tuple([HBM] * na),
        input_output_aliases={i: i for i in range(na)},
        compiler_params=pltpu.CompilerParams(has_side_effects=EFFECT),
    )(*arrays, send_sems, recv_sems, *after)
    return list(res)


def _gather_plan(nt):
    def plan(ins, outs):
        x, y, c, _, chips = _place()
        me = 2 * x + y
        return [(_half(ins[t], me, c), _half(outs[t], me, c), (px, py, c)) for px, py in chips for t in range(nt)]
    return plan


def _gather_landed_plan(nt):
    def plan(ins, outs):
        _, _, c, _, chips = _place()
        return [(_half(outs[t], 2 * px + py, c), _half(outs[t], 2 * px + py, c), (px, py, c))
                for px, py in chips for t in range(nt)]
    return plan


def _scatter_plan(nt):
    def plan(ins, outs):
        _, _, c, _, chips = _place()
        return [(ins[t].at[2 * px + py], outs[nt + t].at[j], (px, py, c))
                for j, (px, py) in enumerate(chips) for t in range(nt)]
    return plan


def _pass_plan(nt):
    def plan(ins, outs):
        _, _, c, sibling, chips = _place()
        return [(_half(ins[t], 2 * px + py, c), _half(outs[t], 2 * px + py, c), sibling)
                for px, py in chips for t in range(nt)]
    return plan


def _pass_landed_plan(nt):
    def plan(ins, outs):
        _, _, c, sibling, chips = _place()
        return [(_half(outs[t], 2 * px + py, c), _half(outs[t], 2 * px + py, 1 - c), sibling)
                for px, py in chips for t in range(nt)]
    return plan


def _swap_plan(nt):
    def plan(ins, outs):
        _, _, c, sibling, _ = _place()
        res = []
        for t in range(nt):
            half = ins[t].shape[1] // 2
            res.append((ins[t].at[:, pl.ds((1 - c) * half, half), :], outs[nt + t], sibling))
        return res
    return plan


def _pass_halves_to_sibling(slots, name):
    nt = len(slots)

    def body(*refs):
        ins, outs, (send_sems, recv_sems) = refs[:nt], refs[nt:2 * nt], refs[2 * nt:]
        _, _, c, sibling, chips = _place()
        cps = [_remote(j * nt + t, _half(ins[t], 2 * px + py, c), _half(outs[t], 2 * px + py, c), sibling,
                       send_sems, recv_sems)
               for j, (px, py) in enumerate(chips) for t in range(nt)]
        for cp in cps:
            cp.start()
        for j, (px, py) in enumerate(chips):
            for t in range(nt):
                other = _half(outs[t], 2 * px + py, 1 - c)
                _remote(j * nt + t, other, other, sibling, send_sems, recv_sems).wait_recv()
        for cp in cps:
            cp.wait_send()

    return pl.pallas_call(
        body, name=name, in_specs=[ANY] * nt, out_specs=[ANY] * nt,
        input_output_aliases={i: i for i in range(nt)},
        out_shape=[jax.ShapeDtypeStruct(a.shape, a.dtype) for a in slots],
        scratch_shapes=[pltpu.SemaphoreType.DMA((3 * nt,)), pltpu.SemaphoreType.DMA((3 * nt,))],
    )(*slots)


def _all_reduce_small(v):
    r, lanes = v.shape

    def body(v_ref, out_ref, buf, send_sems, recv_sems):
        x, y, c, sibling, chips = _place()

        def slot(px, py, pc):
            return buf.at[4 * px + 2 * py + pc]

        def copy(k, block, to, src=None):
            return pltpu.make_async_remote_copy(src_ref=slot(*block) if src is None else src, dst_ref=slot(*block),
                                                send_sem=send_sems.at[k], recv_sem=recv_sems.at[k],
                                                device_id=to, device_id_type=MESH)

        me = (x, y, c)
        buf[4 * x + 2 * y + c] = v_ref[...]
        first = [copy(0, me, sibling, src=v_ref)]
        first += [copy(1 + j, me, (*chip, c), src=v_ref) for j, chip in enumerate(chips)]
        for cp in first:
            cp.start()
        passed = [copy(4 + j, (*chip, c), sibling) for j, chip in enumerate(chips)]
        for j, chip in enumerate(chips):
            copy(1 + j, (*chip, c), me).wait_recv()
            passed[j].start()
        copy(0, (x, y, 1 - c), me).wait_recv()
        for j, chip in enumerate(chips):
            copy(4 + j, (*chip, 1 - c), me).wait_recv()
        for cp in first + passed:
            cp.wait_send()
        acc = buf[0]
        for k in range(1, 8):
            acc = acc + buf[k]
        out_ref[...] = acc

    vmem = pl.BlockSpec(memory_space=pltpu.VMEM)
    return pl.pallas_call(
        body, name="all_reduce_small", in_specs=[vmem], out_specs=vmem,
        out_shape=jax.ShapeDtypeStruct((r, lanes), F32),
        scratch_shapes=[pltpu.VMEM((8, r, lanes), F32), pltpu.SemaphoreType.DMA((7,)), pltpu.SemaphoreType.DMA((7,))],
    )(v)


def _size(shape):
    n = 1
    for d in shape:
        n *= d
    return n


def _regroup_cols(parts, widths):
    out, start = [], 0
    for width in widths:
        pieces, lo = [], 0
        for p in parts:
            a, b = max(start, lo), min(start + width, lo + p.shape[1])
            if a < b:
                pieces.append(p[:, a - lo:b - lo])
            lo += p.shape[1]
        out.append(pieces[0] if len(pieces) == 1 else jnp.concatenate(pieces, axis=1))
        start += width
    return out


def _pack_small(parts):
    rows = []
    for p in parts:
        f = p.reshape(-1).astype(F32)
        n = -(-f.shape[0] // HEAD_DIM) * HEAD_DIM
        rows.append(jnp.pad(f, (0, n - f.shape[0])).reshape(-1, HEAD_DIM))
    a = jnp.concatenate(rows, axis=0)
    return jnp.pad(a, ((0, -a.shape[0] % 8), (0, 0)))


def _unpack_small(a, shapes):
    out, row = [], 0
    for shp in shapes:
        nrows = -(-_size(shp) // HEAD_DIM)
        out.append(a[row:row + nrows].reshape(-1)[:_size(shp)].reshape(shp))
        row += nrows
    return out


SMALL = ["attn_norm_w", "a_log", "dt_bias", "delta_out_norm_w", "q_norm_w", "k_norm_w", "attn_out_norm_w", "ffn_norm_w"]
BIG = ["w_in", "w_out", "w_gate_up", "w_down"]
ORDER = ["attn_norm_w", "w_in", "conv_w", "a_log", "dt_bias", "delta_out_norm_w", "q_norm_w", "k_norm_w",
         "attn_out_norm_w", "w_out", "ffn_norm_w", "w_gate_up", "w_down"]


def kernel(x, positions, attn_norm_w, w_in, conv_w, a_log, dt_bias, delta_out_norm_w, q_norm_w, k_norm_w, attn_out_norm_w, w_out, ffn_norm_w, w_gate_up, w_down, loss_target, m_attn_norm_w, m_w_in, m_conv_w, m_a_log, m_dt_bias, m_delta_out_norm_w, m_q_norm_w, m_k_norm_w, m_attn_out_norm_w, m_w_out, m_ffn_norm_w, m_w_gate_up, m_w_down, v_attn_norm_w, v_w_in, v_conv_w, v_a_log, v_dt_bias, v_delta_out_norm_w, v_q_norm_w, v_k_norm_w, v_attn_out_norm_w, v_w_out, v_ffn_norm_w, v_w_gate_up, v_w_down):
    wts = dict(attn_norm_w=attn_norm_w, w_in=w_in, conv_w=conv_w, a_log=a_log, dt_bias=dt_bias,
               delta_out_norm_w=delta_out_norm_w, q_norm_w=q_norm_w, k_norm_w=k_norm_w,
               attn_out_norm_w=attn_out_norm_w, w_out=w_out, ffn_norm_w=ffn_norm_w, w_gate_up=w_gate_up, w_down=w_down)
    mom = dict(attn_norm_w=m_attn_norm_w, w_in=m_w_in, conv_w=m_conv_w, a_log=m_a_log, dt_bias=m_dt_bias,
               delta_out_norm_w=m_delta_out_norm_w, q_norm_w=m_q_norm_w, k_norm_w=m_k_norm_w,
               attn_out_norm_w=m_attn_out_norm_w, w_out=m_w_out, ffn_norm_w=m_ffn_norm_w, w_gate_up=m_w_gate_up,
               w_down=m_w_down)
    var = dict(attn_norm_w=v_attn_norm_w, w_in=v_w_in, conv_w=v_conv_w, a_log=v_a_log, dt_bias=v_dt_bias,
               delta_out_norm_w=v_delta_out_norm_w, q_norm_w=v_q_norm_w, k_norm_w=v_k_norm_w,
               attn_out_norm_w=v_attn_out_norm_w, w_out=v_w_out, ffn_norm_w=v_ffn_norm_w, w_gate_up=v_w_gate_up,
               w_down=v_w_down)
    dmod = x.shape[2]
    heads = dmod // (2 * HEAD_DIM)
    dw = heads * HEAD_DIM
    chip = 2 * lax.axis_index("x") + lax.axis_index("y")
    core = lax.axis_index("c")
    n_in, n_out, n_gu, n_down, n_conv = (w_in.shape[2], w_out.shape[1], w_gate_up.shape[2], w_down.shape[1],
                                         conv_w.shape[2])

    def slots_of(w, dtype):
        shard = w[0].astype(dtype)
        return lax.dynamic_update_index_in_dim(lax.empty((4,) + shard.shape, dtype), shard, chip, axis=0)

    s_in, s_conv = _all_gather_weights([slots_of(w_in, BF16)], [slots_of(conv_w, F32)], "all_gather_w_in")
    later = [slots_of(w_out, BF16), slots_of(w_gate_up, BF16), slots_of(w_down, BF16)]
    w_send, w_recv, later, started = _split_start(later, s_conv, _gather_plan(3), 9, "gather_rest_start")
    by_cols = lambda a: a.transpose(1, 0, 2).reshape(a.shape[1], 4 * a.shape[2])
    conv_f = by_cols(s_conv)
    w_qkvz, w_bd, w_attn = _regroup_cols([s_in[i] for i in range(4)], [4 * dw, 2 * heads, 3 * dw])
    w_bd = jnp.pad(w_bd, ((0, 0), (0, HEAD_DIM - 2 * heads)))
    small = {n: wts[n] for n in SMALL}
    place = jnp.stack([core, chip]).astype(jnp.int32)
    to_slots = lambda a: a.reshape(a.shape[0], 4, a.shape[1] // 4).transpose(1, 0, 2)

    def later_weights(after):
        landed = _split_wait(w_send, w_recv, later, after, _gather_landed_plan(3), "gather_rest_wait")
        s_out, = _pass_halves_to_sibling(landed[:1], "gather_out_pass")
        p_send, p_recv, passing, token = _split_start(landed[1:], s_out, _pass_plan(2), 6, "gather_ffn_pass_start")

        def ffn_weights(after):
            s_gu, s_down = _split_wait(p_send, p_recv, passing, after, _pass_landed_plan(2), "gather_ffn_pass_wait")
            return by_cols(s_gu), s_down.reshape(4 * n_down, dmod)

        return s_out.reshape(4 * n_out, dmod), token, ffn_weights

    ffn = {}

    def ffn_grads_ready(g_gate, g_up, g_down):
        gs = [to_slots(jnp.concatenate([g_gate, g_up], axis=1)), g_down.reshape(4, n_down, dmod)]
        zones = [lax.empty((4, g.shape[1] // 2, g.shape[2]), BF16) for g in gs]
        s_send, s_recv, s_bufs, swapping = _split_start(gs + zones, g_gate, _swap_plan(2), 2, "swap_ffn_start")

        def and_then(after):
            g_gu, g_dn, b_gu, b_dn = _split_wait(s_send, s_recv, s_bufs, after, _swap_plan(2), "swap_ffn_wait")
            sums = [_add_half_bf16(g_gu, b_gu, place, "chip_partial_sum_w_gate_up"),
                    _add_half_bf16(g_dn, b_dn, place, "chip_partial_sum_w_down")]
            zones3 = [lax.empty((3,) + p.shape[1:], BF16) for p in sums]
            ffn["send"], ffn["recv"], ffn["bufs"], token = _split_start(sums + zones3, b_gu, _scatter_plan(2), 6,
                                                                        "scatter_ffn_start")
            return token

        return swapping, and_then

    rest = {}

    def rest_grads_ready(bg):
        gs = [jnp.stack(_regroup_cols([bg["w_qkv"], bg["w_z"], bg["w_bd"][:, :2 * heads], bg["w_attn"]], [n_in] * 4)),
              jnp.concatenate([bg["w_out_a"], bg["w_out_b"]], axis=0).reshape(4, n_out, dmod)]
        zones = [lax.empty((4, g.shape[1] // 2, g.shape[2]), BF16) for g in gs]
        s_send, s_recv, s_bufs, swapping = _split_start(gs + zones, bg["w_attn"], _swap_plan(2), 2, "swap_rest_start")

        def and_then(after):
            g_in, g_out, b_in, b_out = _split_wait(s_send, s_recv, s_bufs, after, _swap_plan(2), "swap_rest_wait")
            sums = [_add_half_bf16(g_in, b_in, place, "chip_partial_sum_w_in"),
                    _add_half_bf16(g_out, b_out, place, "chip_partial_sum_w_out")]
            zones3 = [lax.empty((3,) + p.shape[1:], BF16) for p in sums]
            rest["send"], rest["recv"], rest["bufs"], token = _split_start(sums + zones3, b_in, _scatter_plan(2), 6,
                                                                           "scatter_rest_start")
            return token

        return swapping, and_then

    loss_row, grad_x, sg = _local_step(
        x[0], positions[0], loss_target[0], small, w_qkvz, w_bd, w_attn, conv_f,
        later_weights, ffn_grads_ready, rest_grads_ready, after=started)

    r_send, r_recv, r_bufs = rest["send"], rest["recv"], rest["bufs"]
    sum_gu, sum_down, got_gu, got_down = _split_wait(ffn["send"], ffn["recv"], ffn["bufs"], grad_x,
                                                     _scatter_plan(2), "scatter_ffn_wait")
    g_big = dict(zip(["w_gate_up", "w_down"], _join_halves(
        [_sum4_f32(sum_gu, got_gu, place, "grad_total_w_gate_up"),
         _sum4_f32(sum_down, got_down, place, "grad_total_w_down")], "join_ffn_halves")))
    grads, deltas, new_m, new_v = {}, {}, {}, {}

    def adamw_big(n):
        shp = wts[n].shape
        d, nm, nv = _adamw(wts[n][0], g_big[n], mom[n][0], var[n][0], "adamw_" + n)
        grads[n], deltas[n], new_m[n], new_v[n] = g_big[n].reshape(shp), d.reshape(shp), nm.reshape(shp), nv.reshape(shp)
        return d

    done = [adamw_big("w_gate_up"), adamw_big("w_down")]
    sum_in, sum_out, got_in, got_out = _split_wait(r_send, r_recv, r_bufs, done, _scatter_plan(2), "scatter_rest_wait")
    g_big.update(zip(["w_in", "w_out"], _join_halves(
        [_sum4_f32(sum_in, got_in, place, "grad_total_w_in"),
         _sum4_f32(sum_out, got_out, place, "grad_total_w_out")], "join_rest_halves")))
    adamw_big("w_in")
    adamw_big("w_out")

    reduced = _all_reduce_small(_pack_small([sg[n] for n in SMALL] + [sg["conv_w"], loss_row]))
    red = _unpack_small(reduced, [wts[n].shape for n in SMALL] + [(4, 4 * n_conv), (1, HEAD_DIM)])
    g_small = dict(zip(SMALL, red[:len(SMALL)]))
    g_conv_full, loss_out = red[len(SMALL)], red[len(SMALL) + 1]
    g_small["conv_w"] = lax.dynamic_slice_in_dim(g_conv_full, chip * n_conv, n_conv, axis=1).reshape(conv_w.shape)

    names = SMALL + ["conv_w"]
    shapes = [wts[n].shape for n in names]
    d, nm, nv = _adamw(_pack_small([wts[n] for n in names]), _pack_small([g_small[n] for n in names]),
                       _pack_small([mom[n] for n in names]), _pack_small([var[n] for n in names]), "adamw_small")
    for n, dd, mm, vv in zip(names, _unpack_small(d, shapes), _unpack_small(nm, shapes), _unpack_small(nv, shapes)):
        grads[n], deltas[n], new_m[n], new_v[n] = g_small[n], dd, mm, vv
    return (loss_out[0, 0], grad_x[None], *[grads[n] for n in ORDER], *[deltas[n] for n in ORDER],
            *[new_m[n] for n in ORDER], *[new_v[n] for n in ORDER])
```
